```python
import jax, jax.numpy as jnp
from jax import lax
import numpy as np

D_MODEL = 1024
BATCH = 8
SEQ = 2048
DEPTH = 2

PLE_DIM = 256
EPS = 1e-6
A_HEAD_DIM = 128
A_HEADS = (D_MODEL // 2) // A_HEAD_DIM
A_DIM = A_HEADS * A_HEAD_DIM
QKV_CONV_WIDTH = 4
CHUNK = 64
POOL_WINDOWS = (2, 4, 8, 16)
POOL_GROUPS = len(POOL_WINDOWS)
POOL_DIM = D_MODEL // 4
POOL_GROUP_DIM = POOL_DIM // POOL_GROUPS
CONV_HEADS = 4
CONV_DIM = D_MODEL // 4
CONV_WIDTH = 3
D_MIX = A_DIM + POOL_DIM + CONV_DIM
IN_SIZES = (A_DIM, A_DIM, A_DIM, A_DIM, A_HEADS, A_HEADS, POOL_DIM, CONV_DIM, CONV_DIM, CONV_DIM)
D_IN = sum(IN_SIZES)
D_FF = -(-8 * D_MODEL // (3 * 256)) * 256

kernel_name = 'hybrid_parallel_deltanet_pool_shortconv'


def rms_norm(x, g):
    xf = x.astype(jnp.float32)
    y = xf * lax.rsqrt(jnp.mean(xf * xf, axis=-1, keepdims=True) + EPS)
    return (y * g.astype(jnp.float32)).astype(x.dtype)


def causal_dwconv(x, w):
    K, C = w.shape
    return lax.conv_general_dilated(
        x, w[:, None, :].astype(x.dtype), window_strides=(1,), padding=[(K - 1, 0)],
        dimension_numbers=('NWC', 'WIO', 'NWC'), feature_group_count=C)


def l2norm(t):
    return t * lax.rsqrt(jnp.sum(t * t, axis=-1, keepdims=True) + EPS)


def chunk_gated_delta_rule(q, k, v, g, beta):
    Bn, S, H, DK = q.shape
    DV = v.shape[-1]
    N = S // CHUNK

    def to_chunks(t):
        t = t.reshape((Bn, N, CHUNK, H) + t.shape[3:])
        return jnp.moveaxis(t, 3, 1)

    q = to_chunks(q * (DK ** -0.5))
    k = to_chunks(k)
    v = to_chunks(v)
    g = to_chunks(g)
    beta = to_chunks(beta)
    gc = jnp.cumsum(g, axis=-1)
    kb = k * beta[..., None]
    vb = v * beta[..., None]
    causal_incl = jnp.tril(jnp.ones((CHUNK, CHUNK), dtype=bool))
    causal_strict = jnp.tril(jnp.ones((CHUNK, CHUNK), dtype=bool), -1)
    diff = gc[..., :, None] - gc[..., None, :]
    decay = jnp.exp(jnp.where(causal_incl, diff, -jnp.inf))
    lower = jnp.where(causal_strict, jnp.einsum('bhncd,bhnsd->bhncs', kb, k) * decay, 0.0)
    eye = jnp.eye(CHUNK, dtype=jnp.float32)
    tmat = lax.linalg.triangular_solve(eye + lower, jnp.broadcast_to(eye, lower.shape),
                                       left_side=True, lower=True, unit_diagonal=True)
    u = jnp.einsum('bhncs,bhnsv->bhncv', tmat, vb)
    w = jnp.einsum('bhncs,bhnsd->bhncd', tmat, kb * jnp.exp(gc)[..., None])
    attn = jnp.einsum('bhncd,bhnsd->bhncs', q, k) * decay

    def step(state, inp):
        q_i, k_i, u_i, w_i, gc_i, a_i = inp
        v_new = u_i - jnp.einsum('bhck,bhkv->bhcv', w_i, state)
        o_i = (jnp.einsum('bhck,bhkv->bhcv', q_i * jnp.exp(gc_i)[..., None], state)
               + jnp.einsum('bhcs,bhsv->bhcv', a_i, v_new))
        g_last = gc_i[..., -1]
        state = (state * jnp.exp(g_last)[..., None, None]
                 + jnp.einsum('bhck,bhcv->bhkv', k_i * jnp.exp(g_last[..., None] - gc_i)[..., None], v_new))
        return state, o_i

    xs = tuple(jnp.moveaxis(t, 2, 0) for t in (q, k, u, w, gc, attn))
    state0 = jnp.zeros((Bn, H, DK, DV), jnp.float32)
    _, o = lax.scan(step, state0, xs)
    return jnp.transpose(o, (1, 0, 3, 2, 4)).reshape(Bn, S, H, DV)


def gated_deltanet(q, k, v, z, a, b, conv_w, a_log, dt_bias, onorm_g):
    Bn, S, _ = q.shape
    qkv = jax.nn.silu(causal_dwconv(jnp.concatenate([q, k, v], axis=-1), conv_w))
    q, k, v = jnp.split(qkv.astype(jnp.float32), 3, axis=-1)
    hs = (Bn, S, A_HEADS, A_HEAD_DIM)
    q = l2norm(q.reshape(hs))
    k = l2norm(k.reshape(hs))
    v = v.reshape(hs)
    beta = jax.nn.sigmoid(b.astype(jnp.float32))
    g = -jnp.exp(a_log.astype(jnp.float32)) * jax.nn.softplus(a.astype(jnp.float32) + dt_bias.astype(jnp.float32))
    o = chunk_gated_delta_rule(q, k, v, g, beta)
    o = o * lax.rsqrt(jnp.mean(o * o, axis=-1, keepdims=True) + EPS) * onorm_g.astype(jnp.float32)
    o = o * jax.nn.silu(z.astype(jnp.float32).reshape(hs))
    return o.reshape(Bn, S, A_DIM).astype(z.dtype)


def multiscale_pool(h, pool_w, pool_scale):
    Bn, S, _ = h.shape
    hf = h.astype(jnp.float32)
    cs = jnp.cumsum(hf, axis=1)
    count = jnp.arange(1, S + 1, dtype=jnp.float32)[:, None]
    outs = []
    for gi, win in enumerate(POOL_WINDOWS):
        sl = slice(gi * POOL_GROUP_DIM, (gi + 1) * POOL_GROUP_DIM)
        csg = cs[..., sl]
        lag = jnp.pad(csg, ((0, 0), (win, 0), (0, 0)))[:, :S]
        mean = (csg - lag) / jnp.minimum(count, float(win))
        outs.append(mean - hf[..., sl])
    pooled = jnp.stack(outs, axis=2)
    y = jnp.einsum('bsgc,gcd->bsgd', pooled, pool_w.astype(jnp.float32)).reshape(Bn, S, POOL_DIM)
    return (y * pool_scale.astype(jnp.float32)).astype(h.dtype)


def short_gated_conv(gate_b, gate_c, hc, conv_w):
    return gate_b * causal_dwconv(gate_c * hc, conv_w)


def _fwd_setup_inputs(seed: int = 0) -> dict:
    key = jax.random.key(seed)
    ks = jax.random.split(key, 24)
    f32 = jnp.float32
    nrm = lambda k, shape, scale: jax.random.normal(k, shape, f32) * scale
    dt = jnp.exp(jax.random.uniform(ks[5], (DEPTH, A_HEADS), f32, np.log(1e-3), np.log(1e-1)))
    return {
        'x': nrm(ks[0], (BATCH, SEQ, D_MODEL), 1.0),
        'p': nrm(ks[1], (DEPTH, BATCH, SEQ, PLE_DIM), 1.0),
        'norm1_g': 1.0 + nrm(ks[2], (DEPTH, D_MODEL), 0.02),
        'w_in': nrm(ks[3], (DEPTH, D_MODEL, D_IN), D_MODEL ** -0.5),
        'conv_qkv': nrm(ks[4], (DEPTH, QKV_CONV_WIDTH, 3 * A_DIM), QKV_CONV_WIDTH ** -0.5),
        'a_log': jnp.log(jax.random.uniform(ks[6], (DEPTH, A_HEADS), f32, 1.0, 16.0)),
        'dt_bias': jnp.log(jnp.expm1(dt)),
        'onorm_g': 1.0 + nrm(ks[7], (DEPTH, A_HEAD_DIM), 0.02),
        'pool_w': nrm(ks[8], (DEPTH, POOL_GROUPS, POOL_GROUP_DIM, POOL_GROUP_DIM), POOL_GROUP_DIM ** -0.5),
        'pool_scale': 1.0 + nrm(ks[9], (DEPTH, POOL_DIM), 0.02),
        'sconv_w': nrm(ks[10], (DEPTH, CONV_WIDTH, CONV_DIM), CONV_WIDTH ** -0.5),
        'w_out': nrm(ks[11], (DEPTH, D_MIX, D_MODEL), D_MIX ** -0.5),
        'norm2_g': 1.0 + nrm(ks[12], (DEPTH, D_MODEL), 0.02),
        'w_gate': nrm(ks[13], (DEPTH, D_MODEL, D_FF), D_MODEL ** -0.5),
        'w_up': nrm(ks[14], (DEPTH, D_MODEL, D_FF), D_MODEL ** -0.5),
        'w_down': nrm(ks[15], (DEPTH, D_FF, D_MODEL), D_FF ** -0.5),
        'ple_proj': nrm(ks[16], (DEPTH, PLE_DIM, D_MODEL), PLE_DIM ** -0.5),
        'ple_gate': nrm(ks[17], (DEPTH, D_MODEL, D_MODEL), D_MODEL ** -0.5),
        'final_g': 1.0 + nrm(ks[18], (D_MODEL,), 0.02),
    }


def _fwd_reference(x, p, norm1_g, w_in, conv_qkv, a_log, dt_bias, onorm_g, pool_w, pool_scale,
              sconv_w, w_out, norm2_g, w_gate, w_up, w_down, ple_proj, ple_gate, final_g):
    offsets = [0]
    for s in IN_SIZES[:-1]:
        offsets.append(offsets[-1] + s)
    for i in range(DEPTH):
        h = rms_norm(x, norm1_g[i])
        proj = jnp.einsum('bsd,de->bse', h, w_in[i])
        q, k, v, z, a, b, hp, cb, cc, ch = jnp.split(proj, offsets[1:], axis=-1)
        o_a = gated_deltanet(q, k, v, z, a, b, conv_qkv[i], a_log[i], dt_bias[i], onorm_g[i])
        o_b = multiscale_pool(hp, pool_w[i], pool_scale[i])
        o_c = short_gated_conv(cb, cc, ch, sconv_w[i])
        mixed = jnp.concatenate([o_a, o_b, o_c], axis=-1)
        x = x + jnp.einsum('bse,ed->bsd', mixed, w_out[i])
        h = rms_norm(x, norm2_g[i])
        ff = jax.nn.silu(jnp.einsum('bsd,df->bsf', h, w_gate[i])) * jnp.einsum('bsd,df->bsf', h, w_up[i])
        x = x + jnp.einsum('bsf,fd->bsd', ff, w_down[i])
        gate = jax.nn.sigmoid(jnp.einsum('bsd,de->bse', x, ple_gate[i]).astype(jnp.float32)).astype(x.dtype)
        x = x + gate * jnp.einsum('bsq,qd->bsd', p[i], ple_proj[i])
    return rms_norm(x, final_g)


import jax as _jax
import jax.numpy as _jnp

TWIN_FORMAT = 'train_step'
FWD_PARAMS = ['x', 'p', 'norm1_g', 'w_in', 'conv_qkv', 'a_log', 'dt_bias', 'onorm_g', 'pool_w', 'pool_scale', 'sconv_w', 'w_out', 'norm2_g', 'w_gate', 'w_up', 'w_down', 'ple_proj', 'ple_gate', 'final_g']
TWIN_WEIGHTS = ['norm1_g', 'w_in', 'conv_qkv', 'a_log', 'dt_bias', 'onorm_g', 'pool_w', 'pool_scale', 'sconv_w', 'w_out', 'norm2_g', 'w_gate', 'w_up', 'w_down', 'ple_proj', 'ple_gate', 'final_g']
TWIN_DIFF_INPUT = 'x'
TWIN_INPUTS = ['x', 'p', 'norm1_g', 'w_in', 'conv_qkv', 'a_log', 'dt_bias', 'onorm_g', 'pool_w', 'pool_scale', 'sconv_w', 'w_out', 'norm2_g', 'w_gate', 'w_up', 'w_down', 'ple_proj', 'ple_gate', 'final_g', 'loss_target', 'm_norm1_g', 'm_w_in', 'm_conv_qkv', 'm_a_log', 'm_dt_bias', 'm_onorm_g', 'm_pool_w', 'm_pool_scale', 'm_sconv_w', 'm_w_out', 'm_norm2_g', 'm_w_gate', 'm_w_up', 'm_w_down', 'm_ple_proj', 'm_ple_gate', 'm_final_g', 'v_norm1_g', 'v_w_in', 'v_conv_qkv', 'v_a_log', 'v_dt_bias', 'v_onorm_g', 'v_pool_w', 'v_pool_scale', 'v_sconv_w', 'v_w_out', 'v_norm2_g', 'v_w_gate', 'v_w_up', 'v_w_down', 'v_ple_proj', 'v_ple_gate', 'v_final_g']
TWIN_OUTPUTS = ['loss', 'grad_x', 'grad_norm1_g', 'grad_w_in', 'grad_conv_qkv', 'grad_a_log', 'grad_dt_bias', 'grad_onorm_g', 'grad_pool_w', 'grad_pool_scale', 'grad_sconv_w', 'grad_w_out', 'grad_norm2_g', 'grad_w_gate', 'grad_w_up', 'grad_w_down', 'grad_ple_proj', 'grad_ple_gate', 'grad_final_g', 'delta_norm1_g', 'delta_w_in', 'delta_conv_qkv', 'delta_a_log', 'delta_dt_bias', 'delta_onorm_g', 'delta_pool_w', 'delta_pool_scale', 'delta_sconv_w', 'delta_w_out', 'delta_norm2_g', 'delta_w_gate', 'delta_w_up', 'delta_w_down', 'delta_ple_proj', 'delta_ple_gate', 'delta_final_g', 'new_m_norm1_g', 'new_m_w_in', 'new_m_conv_qkv', 'new_m_a_log', 'new_m_dt_bias', 'new_m_onorm_g', 'new_m_pool_w', 'new_m_pool_scale', 'new_m_sconv_w', 'new_m_w_out', 'new_m_norm2_g', 'new_m_w_gate', 'new_m_w_up', 'new_m_w_down', 'new_m_ple_proj', 'new_m_ple_gate', 'new_m_final_g', 'new_v_norm1_g', 'new_v_w_in', 'new_v_conv_qkv', 'new_v_a_log', 'new_v_dt_bias', 'new_v_onorm_g', 'new_v_pool_w', 'new_v_pool_scale', 'new_v_sconv_w', 'new_v_w_out', 'new_v_norm2_g', 'new_v_w_gate', 'new_v_w_up', 'new_v_w_down', 'new_v_ple_proj', 'new_v_ple_gate', 'new_v_final_g']
TWIN_LEAF_KINDS = {'loss': 'loss', 'grad_x': 'grad_x', 'grad_norm1_g': 'grad_w', 'grad_w_in': 'grad_w', 'grad_conv_qkv': 'grad_w', 'grad_a_log': 'grad_w', 'grad_dt_bias': 'grad_w', 'grad_onorm_g': 'grad_w', 'grad_pool_w': 'grad_w', 'grad_pool_scale': 'grad_w', 'grad_sconv_w': 'grad_w', 'grad_w_out': 'grad_w', 'grad_norm2_g': 'grad_w', 'grad_w_gate': 'grad_w', 'grad_w_up': 'grad_w', 'grad_w_down': 'grad_w', 'grad_ple_proj': 'grad_w', 'grad_ple_gate': 'grad_w', 'grad_final_g': 'grad_w', 'delta_norm1_g': 'delta_w', 'delta_w_in': 'delta_w', 'delta_conv_qkv': 'delta_w', 'delta_a_log': 'delta_w', 'delta_dt_bias': 'delta_w', 'delta_onorm_g': 'delta_w', 'delta_pool_w': 'delta_w', 'delta_pool_scale': 'delta_w', 'delta_sconv_w': 'delta_w', 'delta_w_out': 'delta_w', 'delta_norm2_g': 'delta_w', 'delta_w_gate': 'delta_w', 'delta_w_up': 'delta_w', 'delta_w_down': 'delta_w', 'delta_ple_proj': 'delta_w', 'delta_ple_gate': 'delta_w', 'delta_final_g': 'delta_w', 'new_m_norm1_g': 'new_m', 'new_m_w_in': 'new_m', 'new_m_conv_qkv': 'new_m', 'new_m_a_log': 'new_m', 'new_m_dt_bias': 'new_m', 'new_m_onorm_g': 'new_m', 'new_m_pool_w': 'new_m', 'new_m_pool_scale': 'new_m', 'new_m_sconv_w': 'new_m', 'new_m_w_out': 'new_m', 'new_m_norm2_g': 'new_m', 'new_m_w_gate': 'new_m', 'new_m_w_up': 'new_m', 'new_m_w_down': 'new_m', 'new_m_ple_proj': 'new_m', 'new_m_ple_gate': 'new_m', 'new_m_final_g': 'new_m', 'new_v_norm1_g': 'new_v', 'new_v_w_in': 'new_v', 'new_v_conv_qkv': 'new_v', 'new_v_a_log': 'new_v', 'new_v_dt_bias': 'new_v', 'new_v_onorm_g': 'new_v', 'new_v_pool_w': 'new_v', 'new_v_pool_scale': 'new_v', 'new_v_sconv_w': 'new_v', 'new_v_w_out': 'new_v', 'new_v_norm2_g': 'new_v', 'new_v_w_gate': 'new_v', 'new_v_w_up': 'new_v', 'new_v_w_down': 'new_v', 'new_v_ple_proj': 'new_v', 'new_v_ple_gate': 'new_v', 'new_v_final_g': 'new_v'}


def _forward(args):
    return _fwd_reference(*[args[k] for k in FWD_PARAMS])


def _output_shape():
    out = _jax.eval_shape(lambda: _forward(_fwd_setup_inputs(0)))
    return out.shape, out.dtype

N_MICROBATCH = 1
ADAM_LR = 0.001
ADAM_B1 = 0.9
ADAM_B2 = 0.999
ADAM_EPS = 1e-08
ADAM_WD = 0.01
ADAM_STEP = 10
PER_EXAMPLE_BATCH_AXIS = {'x': 0, 'p': 1, 'loss_target': 0}
SHARED_INPUTS = []
_WEIGHT_DTYPES = {'norm1_g': _jnp.float32, 'w_in': _jnp.float32, 'conv_qkv': _jnp.float32, 'a_log': _jnp.float32, 'dt_bias': _jnp.float32, 'onorm_g': _jnp.float32, 'pool_w': _jnp.float32, 'pool_scale': _jnp.float32, 'sconv_w': _jnp.float32, 'w_out': _jnp.float32, 'norm2_g': _jnp.float32, 'w_gate': _jnp.float32, 'w_up': _jnp.float32, 'w_down': _jnp.float32, 'ple_proj': _jnp.float32, 'ple_gate': _jnp.float32, 'final_g': _jnp.float32}
MOMENT_SCALE = {'norm1_g': 1.178152e-01, 'w_in': 6.870535e-02, 'conv_qkv': 4.545107e-02, 'a_log': 2.684387e-01, 'dt_bias': 2.587902e-01, 'onorm_g': 1.125984e-01, 'pool_w': 8.785388e-02, 'pool_scale': 8.643314e-02, 'sconv_w': 1.025280e-01, 'w_out': 7.712862e-02, 'norm2_g': 7.526125e-02, 'w_gate': 3.131271e-02, 'w_up': 3.038957e-02, 'w_down': 5.024849e-02, 'ple_proj': 4.718040e-02, 'ple_gate': 2.363315e-02, 'final_g': 1.602154e+01}


def _to_microbatches(a, axis):
    t = _jnp.moveaxis(a, axis, 0)
    t = t.reshape((N_MICROBATCH, t.shape[0] // N_MICROBATCH) + t.shape[1:])
    return _jnp.moveaxis(t, 1, axis + 1)


def setup_inputs(seed: int = 0) -> dict:
    inp = _fwd_setup_inputs(seed)
    key = _jax.random.fold_in(_jax.random.key(seed), 7919)
    shape, _ = _output_shape()
    out = dict(inp)
    out["loss_target"] = _jax.random.normal(_jax.random.fold_in(key, 0), shape, _jnp.float32)
    for i, name in enumerate(TWIN_WEIGHTS):
        w = inp[name].astype(_jnp.float32)
        if MOMENT_SCALE is None:
            s = _jnp.sqrt(_jnp.mean(_jnp.square(w)) + 1e-30)
        else:
            s = MOMENT_SCALE[name]
        km, kv = _jax.random.split(_jax.random.fold_in(key, i + 1))
        out[name] = w
        out["m_" + name] = s * _jax.random.normal(km, w.shape, _jnp.float32)
        out["v_" + name] = (s * s) * _jax.random.uniform(kv, w.shape, _jnp.float32, 0.5, 1.5)
    if N_MICROBATCH > 1:
        for name, axis in PER_EXAMPLE_BATCH_AXIS.items():
            out[name] = _to_microbatches(out[name], axis)
    return {'x': out['x'], 'p': out['p'], 'norm1_g': out['norm1_g'], 'w_in': out['w_in'], 'conv_qkv': out['conv_qkv'], 'a_log': out['a_log'], 'dt_bias': out['dt_bias'], 'onorm_g': out['onorm_g'], 'pool_w': out['pool_w'], 'pool_scale': out['pool_scale'], 'sconv_w': out['sconv_w'], 'w_out': out['w_out'], 'norm2_g': out['norm2_g'], 'w_gate': out['w_gate'], 'w_up': out['w_up'], 'w_down': out['w_down'], 'ple_proj': out['ple_proj'], 'ple_gate': out['ple_gate'], 'final_g': out['final_g'], 'loss_target': out['loss_target'], 'm_norm1_g': out['m_norm1_g'], 'm_w_in': out['m_w_in'], 'm_conv_qkv': out['m_conv_qkv'], 'm_a_log': out['m_a_log'], 'm_dt_bias': out['m_dt_bias'], 'm_onorm_g': out['m_onorm_g'], 'm_pool_w': out['m_pool_w'], 'm_pool_scale': out['m_pool_scale'], 'm_sconv_w': out['m_sconv_w'], 'm_w_out': out['m_w_out'], 'm_norm2_g': out['m_norm2_g'], 'm_w_gate': out['m_w_gate'], 'm_w_up': out['m_w_up'], 'm_w_down': out['m_w_down'], 'm_ple_proj': out['m_ple_proj'], 'm_ple_gate': out['m_ple_gate'], 'm_final_g': out['m_final_g'], 'v_norm1_g': out['v_norm1_g'], 'v_w_in': out['v_w_in'], 'v_conv_qkv': out['v_conv_qkv'], 'v_a_log': out['v_a_log'], 'v_dt_bias': out['v_dt_bias'], 'v_onorm_g': out['v_onorm_g'], 'v_pool_w': out['v_pool_w'], 'v_pool_scale': out['v_pool_scale'], 'v_sconv_w': out['v_sconv_w'], 'v_w_out': out['v_w_out'], 'v_norm2_g': out['v_norm2_g'], 'v_w_gate': out['v_w_gate'], 'v_w_up': out['v_w_up'], 'v_w_down': out['v_w_down'], 'v_ple_proj': out['v_ple_proj'], 'v_ple_gate': out['v_ple_gate'], 'v_final_g': out['v_final_g']}


def _loss(weights, diff, rest, loss_target):
    with _jax.named_scope("forward"):
        args = {**rest, TWIN_DIFF_INPUT: diff, **{k: w.astype(_WEIGHT_DTYPES[k]) for k, w in weights.items()}}
        y = _forward(args)
    with _jax.named_scope("loss_head"):
        err = _jnp.square(y.astype(_jnp.float32) - loss_target)
        return 0.5 * _jnp.sum(_jnp.mean(err, axis=-1)) if err.ndim else 0.5 * err


def _adamw(w, g, m, v):
    m = ADAM_B1 * m + (1.0 - ADAM_B1) * g
    v = ADAM_B2 * v + (1.0 - ADAM_B2) * _jnp.square(g)
    m_hat = m / (1.0 - ADAM_B1 ** ADAM_STEP)
    v_hat = v / (1.0 - ADAM_B2 ** ADAM_STEP)
    delta = -ADAM_LR * (m_hat / (_jnp.sqrt(v_hat) + ADAM_EPS) + ADAM_WD * w)
    return delta, m, v


def reference(x, p, norm1_g, w_in, conv_qkv, a_log, dt_bias, onorm_g, pool_w, pool_scale, sconv_w, w_out, norm2_g, w_gate, w_up, w_down, ple_proj, ple_gate, final_g, loss_target, m_norm1_g, m_w_in, m_conv_qkv, m_a_log, m_dt_bias, m_onorm_g, m_pool_w, m_pool_scale, m_sconv_w, m_w_out, m_norm2_g, m_w_gate, m_w_up, m_w_down, m_ple_proj, m_ple_gate, m_final_g, v_norm1_g, v_w_in, v_conv_qkv, v_a_log, v_dt_bias, v_onorm_g, v_pool_w, v_pool_scale, v_sconv_w, v_w_out, v_norm2_g, v_w_gate, v_w_up, v_w_down, v_ple_proj, v_ple_gate, v_final_g):
    given = dict(x=x, p=p, norm1_g=norm1_g, w_in=w_in, conv_qkv=conv_qkv, a_log=a_log, dt_bias=dt_bias, onorm_g=onorm_g, pool_w=pool_w, pool_scale=pool_scale, sconv_w=sconv_w, w_out=w_out, norm2_g=norm2_g, w_gate=w_gate, w_up=w_up, w_down=w_down, ple_proj=ple_proj, ple_gate=ple_gate, final_g=final_g, loss_target=loss_target, m_norm1_g=m_norm1_g, m_w_in=m_w_in, m_conv_qkv=m_conv_qkv, m_a_log=m_a_log, m_dt_bias=m_dt_bias, m_onorm_g=m_onorm_g, m_pool_w=m_pool_w, m_pool_scale=m_pool_scale, m_sconv_w=m_sconv_w, m_w_out=m_w_out, m_norm2_g=m_norm2_g, m_w_gate=m_w_gate, m_w_up=m_w_up, m_w_down=m_w_down, m_ple_proj=m_ple_proj, m_ple_gate=m_ple_gate, m_final_g=m_final_g, v_norm1_g=v_norm1_g, v_w_in=v_w_in, v_conv_qkv=v_conv_qkv, v_a_log=v_a_log, v_dt_bias=v_dt_bias, v_onorm_g=v_onorm_g, v_pool_w=v_pool_w, v_pool_scale=v_pool_scale, v_sconv_w=v_sconv_w, v_w_out=v_w_out, v_norm2_g=v_norm2_g, v_w_gate=v_w_gate, v_w_up=v_w_up, v_w_down=v_w_down, v_ple_proj=v_ple_proj, v_ple_gate=v_ple_gate, v_final_g=v_final_g)
    weights = {n: given[n] for n in TWIN_WEIGHTS}
    shared = {n: given[n] for n in SHARED_INPUTS}
    per_example = {n: given[n] for n in ['x', 'p']}
    grad_fn = _jax.value_and_grad(_loss, argnums=(0, 1))

    def one_microbatch(ex, loss_target):
        ex = dict(ex)
        diff = ex.pop(TWIN_DIFF_INPUT)
        return grad_fn(weights, diff, {**shared, **ex}, loss_target)

    if N_MICROBATCH == 1:
        loss, (grad_w, grad_x) = one_microbatch(per_example, given["loss_target"])
    else:
        def body(carry, xs):
            loss_sum, grad_sum = carry
            l_k, (gw_k, gx_k) = one_microbatch(xs[0], xs[1])
            with _jax.named_scope("update"):
                return (loss_sum + l_k, _jax.tree.map(_jnp.add, grad_sum, gw_k)), gx_k

        init = (_jnp.zeros((), _jnp.float32), _jax.tree.map(_jnp.zeros_like, weights))
        (loss, grad_w), grad_x = _jax.lax.scan(body, init, (per_example, given["loss_target"]))
    with _jax.named_scope("update"):
        delta_w, new_m, new_v = {}, {}, {}
        for n in TWIN_WEIGHTS:
            delta_w[n], new_m[n], new_v[n] = _adamw(weights[n], grad_w[n], given["m_" + n], given["v_" + n])
    return (loss, grad_x, *[grad_w[n] for n in TWIN_WEIGHTS], *[delta_w[n] for n in TWIN_WEIGHTS],
            *[new_m[n] for n in TWIN_WEIGHTS], *[new_v[n] for n in TWIN_WEIGHTS])
```

```python
import jax
import jax.numpy as jnp
from jax import lax
from jax.experimental import pallas as pl
from jax.experimental.pallas import tpu as pltpu

F32 = jnp.float32
MM_DTYPE = jnp.bfloat16
WIRE_DTYPE = jnp.bfloat16
HI = lax.Precision.HIGHEST
EPS = 1e-6
HEAD_DIM = 128
CHUNK = 64
QKV_CONV_WIDTH = 4
SCONV_WIDTH = 3
POOL_GROUPS = 4
LANES = 128
SUBLANES_WIRE = 16
VMEM_LIMIT_BYTES = 56 * 1024 * 1024
ADAM_LR, ADAM_B1, ADAM_B2, ADAM_EPS, ADAM_WD, ADAM_STEP = 0.001, 0.9, 0.999, 1e-08, 0.01, 10
MESH = pl.DeviceIdType.MESH
ANY = pl.BlockSpec(memory_space=pl.ANY)


def _params(*sem):
    return pltpu.CompilerParams(vmem_limit_bytes=VMEM_LIMIT_BYTES, dimension_semantics=sem if sem else None)


def _mm(a, b):
    return jnp.dot(a.astype(MM_DTYPE), b.astype(MM_DTYPE), preferred_element_type=F32)


def _mm_nt(a, b):
    return lax.dot_general(a.astype(MM_DTYPE), b.astype(MM_DTYPE), (((1,), (1,)), ((), ())), preferred_element_type=F32)


def _mm_tn(a, b):
    return lax.dot_general(a.astype(MM_DTYPE), b.astype(MM_DTYPE), (((0,), (0,)), ((), ())), preferred_element_type=F32)


def _hmm(a, b):
    return jnp.dot(a, b, preferred_element_type=F32, precision=HI)


def _hmm_nt(a, b):
    return lax.dot_general(a, b, (((1,), (1,)), ((), ())), preferred_element_type=F32, precision=HI)


def _hmm_tn(a, b):
    return lax.dot_general(a, b, (((0,), (0,)), ((), ())), preferred_element_type=F32, precision=HI)


def _sigmoid(x):
    return 1.0 / (1.0 + jnp.exp(-x))


def _dsilu(x, s):
    return s * (1.0 + x * (1.0 - s))


def _rows(shape):
    return lax.broadcasted_iota(jnp.int32, shape, 0)


def _shift_down(x, s):
    if s == 0:
        return x
    return jnp.where(_rows(x.shape) >= s, pltpu.roll(x, s, 0), 0.0)


def _shift_up(x, s):
    if s == 0:
        return x
    t = x.shape[0]
    return jnp.where(_rows(x.shape) < t - s, pltpu.roll(x, t - s, 0), 0.0)


def _rms_fwd(x):
    r = lax.rsqrt(jnp.mean(x * x, axis=-1, keepdims=True) + EPS)
    return x * r, r


def _rms_bwd(dxn, xn, r):
    return r * (dxn - xn * jnp.mean(dxn * xn, axis=-1, keepdims=True))


def _tile_rows(n, cap, mult=8):
    best = None
    for d in range(mult, min(n, cap) + 1, mult):
        if n % d == 0:
            best = d
    return best if best is not None else n


def _in_proj_fwd(x, g1, wp, segs, tm):
    t, d = x.shape
    npk = wp.shape[1]

    def body(x_ref, g_ref, w_ref, *o_refs):
        xn, _ = _rms_fwd(x_ref[...])
        h = (xn * g_ref[...]).astype(w_ref.dtype)
        off = 0
        for o_ref, wd in zip(o_refs, segs):
            o_ref[...] = jnp.dot(h, w_ref[:, off:off + wd], preferred_element_type=F32)
            off += wd

    return pl.pallas_call(
        body, name="in_proj_fwd", grid=(t // tm,),
        in_specs=[pl.BlockSpec((tm, d), lambda i: (i, 0)), pl.BlockSpec((1, d), lambda i: (0, 0)),
                  pl.BlockSpec((d, npk), lambda i: (0, 0))],
        out_specs=[pl.BlockSpec((tm, wd), lambda i: (i, 0)) for wd in segs],
        out_shape=[jax.ShapeDtypeStruct((t, wd), F32) for wd in segs],
        compiler_params=_params("arbitrary"))(x, g1, wp)


def _in_proj_bwd(x, g1, wp, dsegs, dx_res, segs, tm):
    t, d = x.shape
    npk = wp.shape[1]
    nseg = len(segs)

    def body(x_ref, g_ref, w_ref, *rest):
        ds_refs = rest[:nseg]
        dxr_ref, dx_ref, dw_ref, dg_ref = rest[nseg:]
        i = pl.program_id(0)

        @pl.when(i == 0)
        def _():
            dw_ref[...] = jnp.zeros_like(dw_ref)
            dg_ref[...] = jnp.zeros_like(dg_ref)

        xn, r = _rms_fwd(x_ref[...])
        g = g_ref[...]
        h = (xn * g).astype(w_ref.dtype)
        dh = jnp.zeros((tm, d), F32)
        off = 0
        for ds_ref, wd in zip(ds_refs, segs):
            dsv = ds_ref[...].astype(w_ref.dtype)
            dh = dh + lax.dot_general(dsv, w_ref[:, off:off + wd], (((1,), (1,)), ((), ())), preferred_element_type=F32)
            dw_ref[:, off:off + wd] += lax.dot_general(h, dsv, (((0,), (0,)), ((), ())), preferred_element_type=F32)
            off += wd
        dg_ref[...] += jnp.sum(dh * xn, axis=0, keepdims=True)
        dx_ref[...] = dxr_ref[...] + _rms_bwd(dh * g, xn, r)

    return pl.pallas_call(
        body, name="in_proj_bwd", grid=(t // tm,),
        in_specs=[pl.BlockSpec((tm, d), lambda i: (i, 0)), pl.BlockSpec((1, d), lambda i: (0, 0)),
                  pl.BlockSpec((d, npk), lambda i: (0, 0))]
                 + [pl.BlockSpec((tm, wd), lambda i: (i, 0)) for wd in segs]
                 + [pl.BlockSpec((tm, d), lambda i: (i, 0))],
        out_specs=[pl.BlockSpec((tm, d), lambda i: (i, 0)), pl.BlockSpec((d, npk), lambda i: (0, 0)),
                   pl.BlockSpec((1, d), lambda i: (0, 0))],
        out_shape=[jax.ShapeDtypeStruct((t, d), F32), jax.ShapeDtypeStruct((d, npk), F32),
                   jax.ShapeDtypeStruct((1, d), F32)],
        compiler_params=_params("arbitrary"))(x, g1, wp, *dsegs, dx_res)


def _out_proj_fwd(x0, mix, wo, li, g2, tm):
    t, d = x0.shape
    dq = wo.shape[2]
    widths = [m.shape[1] for m in mix]

    def body(x_ref, *rest):
        m_refs = rest[:len(mix)]
        w_ref, g_ref, x1_ref, h2_ref = rest[len(mix):]
        acc = x_ref[...]
        off = 0
        for m_ref, wd in zip(m_refs, widths):
            for k in range(wd // dq):
                acc = acc + jnp.dot(m_ref[:, k * dq:(k + 1) * dq].astype(w_ref.dtype), w_ref[off // dq + k],
                                    preferred_element_type=F32)
            off += wd
        x1_ref[...] = acc
        xn, _ = _rms_fwd(acc)
        h2_ref[...] = (xn * g_ref[...]).astype(h2_ref.dtype)

    return pl.pallas_call(
        body, name="out_proj_fwd", grid=(t // tm,),
        in_specs=[pl.BlockSpec((tm, d), lambda i: (i, 0))]
                 + [pl.BlockSpec((tm, wd), lambda i: (i, 0)) for wd in widths]
                 + [pl.BlockSpec((4, None, dq, d), lambda i: (0, li, 0, 0)), pl.BlockSpec((1, d), lambda i: (0, 0))],
        out_specs=[pl.BlockSpec((tm, d), lambda i: (i, 0)), pl.BlockSpec((tm, d), lambda i: (i, 0))],
        out_shape=[jax.ShapeDtypeStruct((t, d), F32), jax.ShapeDtypeStruct((t, d), MM_DTYPE)],
        compiler_params=_params("arbitrary"))(x0, *mix, wo, g2)


def _out_proj_bwd(dx2, dh2, x1, g2, mix, wo, li, tm):
    t, d = x1.shape
    dq = wo.shape[2]
    widths = [m.shape[1] for m in mix]
    nm = len(mix)

    def body(dx2_ref, dh2_ref, x1_ref, g_ref, *rest):
        m_refs = rest[:nm]
        w_ref = rest[nm]
        dx1_ref = rest[nm + 1]
        dm_refs = rest[nm + 2:nm + 2 + nm]
        dw_ref, dg_ref = rest[nm + 2 + nm:]
        i = pl.program_id(0)

        @pl.when(i == 0)
        def _():
            dw_ref[...] = jnp.zeros_like(dw_ref)
            dg_ref[...] = jnp.zeros_like(dg_ref)

        xn, r = _rms_fwd(x1_ref[...])
        dh2v = dh2_ref[...]
        dg_ref[...] += jnp.sum(dh2v * xn, axis=0, keepdims=True)
        dx1 = dx2_ref[...] + _rms_bwd(dh2v * g_ref[...], xn, r)
        dx1_ref[...] = dx1
        dx1c = dx1.astype(w_ref.dtype)
        off = 0
        for m_ref, dm_ref, wd in zip(m_refs, dm_refs, widths):
            for k in range(wd // dq):
                j = off // dq + k
                cols = slice(k * dq, (k + 1) * dq)
                dm_ref[:, cols] = lax.dot_general(dx1c, w_ref[j], (((1,), (1,)), ((), ())), preferred_element_type=F32)
                dw_ref[j] += lax.dot_general(m_ref[:, cols].astype(w_ref.dtype), dx1c, (((0,), (0,)), ((), ())),
                                             preferred_element_type=F32)
            off += wd

    tile = lambda wd: pl.BlockSpec((tm, wd), lambda i: (i, 0))
    return pl.pallas_call(
        body, name="out_proj_bwd", grid=(t // tm,),
        in_specs=[tile(d), tile(d), tile(d), pl.BlockSpec((1, d), lambda i: (0, 0))]
                 + [tile(wd) for wd in widths] + [pl.BlockSpec((4, None, dq, d), lambda i: (0, li, 0, 0))],
        out_specs=[tile(d)] + [tile(wd) for wd in widths]
                  + [pl.BlockSpec((4, dq, d), lambda i: (0, 0, 0)), pl.BlockSpec((1, d), lambda i: (0, 0))],
        out_shape=[jax.ShapeDtypeStruct((t, d), F32)] + [jax.ShapeDtypeStruct((t, wd), F32) for wd in widths]
                  + [jax.ShapeDtypeStruct((4, dq, d), F32), jax.ShapeDtypeStruct((1, d), F32)],
        compiler_params=_params("arbitrary"))(dx2, dh2, x1, g2, *mix, wo)


def _ffn_fwd(x1, h2, wg, wu, wd, li, tm):
    t, d = x1.shape
    fs = wg.shape[3]

    def body(x1_ref, h2_ref, wg_ref, wu_ref, wd_ref, x2_ref, gp_ref, up_ref):
        @pl.when(pl.program_id(1) == 0)
        def _():
            x2_ref[...] = x1_ref[...]

        h = h2_ref[...]
        gp = jnp.dot(h, wg_ref[...], preferred_element_type=F32)
        up = jnp.dot(h, wu_ref[...], preferred_element_type=F32)
        gp_ref[...] = gp
        up_ref[...] = up
        ff = gp * _sigmoid(gp) * up
        x2_ref[...] += jnp.dot(ff.astype(wd_ref.dtype), wd_ref[...], preferred_element_type=F32)

    return pl.pallas_call(
        body, name="ffn_fwd", grid=(t // tm, 4),
        in_specs=[pl.BlockSpec((tm, d), lambda i, j: (i, 0)), pl.BlockSpec((tm, d), lambda i, j: (i, 0)),
                  pl.BlockSpec((None, None, d, fs), lambda i, j: (j, li, 0, 0)),
                  pl.BlockSpec((None, None, d, fs), lambda i, j: (j, li, 0, 0)),
                  pl.BlockSpec((None, None, fs, d), lambda i, j: (j, li, 0, 0))],
        out_specs=[pl.BlockSpec((tm, d), lambda i, j: (i, 0)), pl.BlockSpec((None, tm, fs), lambda i, j: (j, i, 0)),
                   pl.BlockSpec((None, tm, fs), lambda i, j: (j, i, 0))],
        out_shape=[jax.ShapeDtypeStruct((t, d), F32), jax.ShapeDtypeStruct((4, t, fs), F32),
                   jax.ShapeDtypeStruct((4, t, fs), F32)],
        compiler_params=_params("arbitrary", "arbitrary"))(x1, h2, wg, wu, wd)


def _ffn_bwd(dx2, h2, gp, up, wg, wu, wd, li, tm):
    t, d = dx2.shape
    fs = wg.shape[3]

    def body(dx2_ref, h2_ref, gp_ref, up_ref, wg_ref, wu_ref, wd_ref, dh2_ref, dwg_ref, dwu_ref, dwd_ref):
        j, i = pl.program_id(0), pl.program_id(1)

        @pl.when(i == 0)
        def _():
            dwg_ref[...] = jnp.zeros_like(dwg_ref)
            dwu_ref[...] = jnp.zeros_like(dwu_ref)
            dwd_ref[...] = jnp.zeros_like(dwd_ref)

        cdt = wg_ref.dtype
        h = h2_ref[...]
        gpv, upv = gp_ref[...], up_ref[...]
        s = _sigmoid(gpv)
        silu = gpv * s
        dx2c = dx2_ref[...].astype(cdt)
        dff = lax.dot_general(dx2c, wd_ref[...], (((1,), (1,)), ((), ())), preferred_element_type=F32)
        dwd_ref[...] += lax.dot_general((silu * upv).astype(cdt), dx2c, (((0,), (0,)), ((), ())), preferred_element_type=F32)
        dup = (dff * silu).astype(cdt)
        dgp = (dff * upv * _dsilu(gpv, s)).astype(cdt)
        dwg_ref[...] += lax.dot_general(h, dgp, (((0,), (0,)), ((), ())), preferred_element_type=F32)
        dwu_ref[...] += lax.dot_general(h, dup, (((0,), (0,)), ((), ())), preferred_element_type=F32)
        dh = (lax.dot_general(dgp, wg_ref[...], (((1,), (1,)), ((), ())), preferred_element_type=F32)
              + lax.dot_general(dup, wu_ref[...], (((1,), (1,)), ((), ())), preferred_element_type=F32))
        rows = pl.ds(pl.multiple_of(i * tm, tm), tm)

        @pl.when(j == 0)
        def _():
            dh2_ref[rows, :] = dh

        @pl.when(j != 0)
        def _():
            dh2_ref[rows, :] += dh

    return pl.pallas_call(
        body, name="ffn_bwd", grid=(4, t // tm),
        in_specs=[pl.BlockSpec((tm, d), lambda j, i: (i, 0)), pl.BlockSpec((tm, d), lambda j, i: (i, 0)),
                  pl.BlockSpec((None, tm, fs), lambda j, i: (j, i, 0)), pl.BlockSpec((None, tm, fs), lambda j, i: (j, i, 0)),
                  pl.BlockSpec((None, None, d, fs), lambda j, i: (j, li, 0, 0)),
                  pl.BlockSpec((None, None, d, fs), lambda j, i: (j, li, 0, 0)),
                  pl.BlockSpec((None, None, fs, d), lambda j, i: (j, li, 0, 0))],
        out_specs=[pl.BlockSpec((t, d), lambda j, i: (0, 0)), pl.BlockSpec((None, d, fs), lambda j, i: (j, 0, 0)),
                   pl.BlockSpec((None, d, fs), lambda j, i: (j, 0, 0)), pl.BlockSpec((None, fs, d), lambda j, i: (j, 0, 0))],
        out_shape=[jax.ShapeDtypeStruct((t, d), F32), jax.ShapeDtypeStruct((4, d, fs), F32),
                   jax.ShapeDtypeStruct((4, d, fs), F32), jax.ShapeDtypeStruct((4, fs, d), F32)],
        compiler_params=_params("arbitrary", "arbitrary"))(dx2, h2, gp, up, wg, wu, wd)


def _ple_fwd(x2, p, wpg, wpp, li, tm):
    t, d = x2.shape
    q = p.shape[1]
    dq = d // 4

    def body(x_ref, p_ref, wg_ref, wp_ref, o_ref):
        xv = x_ref[...]
        xc = xv.astype(wg_ref.dtype)
        pc = p_ref[...].astype(wp_ref.dtype)
        pre = jnp.dot(xc[:, :dq], wg_ref[0], preferred_element_type=F32)
        for j in range(1, 4):
            pre = pre + jnp.dot(xc[:, j * dq:(j + 1) * dq], wg_ref[j], preferred_element_type=F32)
        gate = _sigmoid(pre)
        for j in range(4):
            cols = slice(j * dq, (j + 1) * dq)
            o_ref[:, cols] = xv[:, cols] + gate[:, cols] * jnp.dot(pc, wp_ref[j], preferred_element_type=F32)

    return pl.pallas_call(
        body, name="ple_fwd", grid=(t // tm,),
        in_specs=[pl.BlockSpec((tm, d), lambda i: (i, 0)), pl.BlockSpec((tm, q), lambda i: (i, 0)),
                  pl.BlockSpec((4, None, dq, d), lambda i: (0, li, 0, 0)),
                  pl.BlockSpec((4, None, q, dq), lambda i: (0, li, 0, 0))],
        out_specs=pl.BlockSpec((tm, d), lambda i: (i, 0)),
        out_shape=jax.ShapeDtypeStruct((t, d), F32),
        compiler_params=_params("arbitrary"))(x2, p, wpg, wpp)


def _ple_bwd(dx3, x2, p, wpg, wpp, li, tm):
    t, d = x2.shape
    q = p.shape[1]
    dq = d // 4

    def body(dx3_ref, x_ref, p_ref, wg_ref, wp_ref, dx2_ref, dwg_ref, dwp_ref):
        @pl.when(pl.program_id(0) == 0)
        def _():
            dwg_ref[...] = jnp.zeros_like(dwg_ref)
            dwp_ref[...] = jnp.zeros_like(dwp_ref)

        cdt = wg_ref.dtype
        xc = x_ref[...].astype(cdt)
        pc = p_ref[...].astype(cdt)
        pre = jnp.dot(xc[:, :dq], wg_ref[0], preferred_element_type=F32)
        for j in range(1, 4):
            pre = pre + jnp.dot(xc[:, j * dq:(j + 1) * dq], wg_ref[j], preferred_element_type=F32)
        gate = _sigmoid(pre)
        dx3v = dx3_ref[...]
        dpp = (dx3v * gate).astype(cdt)
        dgate = dx3v * gate * (1.0 - gate)
        dpre_parts = []
        for j in range(4):
            cols = slice(j * dq, (j + 1) * dq)
            pp_j = jnp.dot(pc, wp_ref[j], preferred_element_type=F32)
            dpre_parts.append((dgate[:, cols] * pp_j).astype(cdt))
            dwp_ref[j] += lax.dot_general(pc, dpp[:, cols], (((0,), (0,)), ((), ())), preferred_element_type=F32)
        dpre = jnp.concatenate(dpre_parts, axis=1)
        for j in range(4):
            cols = slice(j * dq, (j + 1) * dq)
            dwg_ref[j] += lax.dot_general(xc[:, cols], dpre, (((0,), (0,)), ((), ())), preferred_element_type=F32)
            dx2_ref[:, cols] = dx3v[:, cols] + lax.dot_general(dpre, wg_ref[j], (((1,), (1,)), ((), ())),
                                                               preferred_element_type=F32)

    return pl.pallas_call(
        body, name="ple_bwd", grid=(t // tm,),
        in_specs=[pl.BlockSpec((tm, d), lambda i: (i, 0)), pl.BlockSpec((tm, d), lambda i: (i, 0)),
                  pl.BlockSpec((tm, q), lambda i: (i, 0)), pl.BlockSpec((4, None, dq, d), lambda i: (0, li, 0, 0)),
                  pl.BlockSpec((4, None, q, dq), lambda i: (0, li, 0, 0))],
        out_specs=[pl.BlockSpec((tm, d), lambda i: (i, 0)), pl.BlockSpec((4, dq, d), lambda i: (0, 0, 0)),
                   pl.BlockSpec((4, q, dq), lambda i: (0, 0, 0))],
        out_shape=[jax.ShapeDtypeStruct((t, d), F32), jax.ShapeDtypeStruct((4, dq, d), F32),
                   jax.ShapeDtypeStruct((4, q, dq), F32)],
        compiler_params=_params("arbitrary"))(dx3, x2, p, wpg, wpp)


def _loss_head(x, target, fg, tm):
    t, d = x.shape

    def body(x_ref, t_ref, g_ref, dx_ref, loss_ref, dg_ref):
        @pl.when(pl.program_id(0) == 0)
        def _():
            loss_ref[...] = jnp.zeros_like(loss_ref)
            dg_ref[...] = jnp.zeros_like(dg_ref)

        xn, r = _rms_fwd(x_ref[...])
        g = g_ref[...]
        err = xn * g - t_ref[...]
        loss_ref[...] += 0.5 * jnp.sum(jnp.sum(err * err, axis=-1, keepdims=True) / d, axis=0, keepdims=True)
        dy = err / d
        dg_ref[...] += jnp.sum(dy * xn, axis=0, keepdims=True)
        dx_ref[...] = _rms_bwd(dy * g, xn, r)

    return pl.pallas_call(
        body, name="loss_head", grid=(t // tm,),
        in_specs=[pl.BlockSpec((tm, d), lambda i: (i, 0)), pl.BlockSpec((tm, d), lambda i: (i, 0)),
                  pl.BlockSpec((1, d), lambda i: (0, 0))],
        out_specs=[pl.BlockSpec((tm, d), lambda i: (i, 0)), pl.BlockSpec((1, 1), lambda i: (0, 0)),
                   pl.BlockSpec((1, d), lambda i: (0, 0))],
        out_shape=[jax.ShapeDtypeStruct((t, d), F32), jax.ShapeDtypeStruct((1, 1), F32),
                   jax.ShapeDtypeStruct((1, d), F32)],
        compiler_params=_params("arbitrary"))(x, target, fg)


def _qkv_conv_act(xv, w, j, heads):
    k = QKV_CONV_WIDTH
    y = w[k - 1:k] * xv
    for s in range(1, k):
        y = y + w[k - 1 - s:k - s] * _shift_down(xv, s)
    sg = _sigmoid(y)
    s_act = y * sg
    nrm = lax.rsqrt(jnp.sum(s_act * s_act, axis=-1, keepdims=True) + EPS)
    scale = jnp.where(j < heads, HEAD_DIM ** -0.5, 1.0).astype(F32)
    return y, sg, s_act, nrm, scale


def _qkv_conv_fwd(qkv_pre, conv_w, heads):
    t = qkv_pre.shape[0]
    nblk = 3 * heads

    def body(x_ref, w_ref, o_ref):
        j = pl.program_id(0)
        _, _, s_act, nrm, scale = _qkv_conv_act(x_ref[...], w_ref[...], j, heads)
        o_ref[...] = jnp.where(j < 2 * heads, s_act * (nrm * scale), s_act)

    return pl.pallas_call(
        body, name="qkv_conv_fwd", grid=(nblk,),
        in_specs=[pl.BlockSpec((t, LANES), lambda j: (0, j)), pl.BlockSpec((QKV_CONV_WIDTH, LANES), lambda j: (0, j))],
        out_specs=pl.BlockSpec((t, LANES), lambda j: (0, j)),
        out_shape=jax.ShapeDtypeStruct(qkv_pre.shape, F32),
        compiler_params=_params("arbitrary"))(qkv_pre, conv_w)


def _qkv_conv_bwd(qkv_pre, conv_w, dqkv, heads):
    t = qkv_pre.shape[0]
    nblk = 3 * heads
    k = QKV_CONV_WIDTH

    def body(x_ref, w_ref, dn_ref, dx_ref, dw_ref):
        j = pl.program_id(0)
        xv, w = x_ref[...], w_ref[...]
        y, sg, s_act, nrm, scale = _qkv_conv_act(xv, w, j, heads)
        dn = dn_ref[...]
        dsn = dn * scale
        ds_qk = nrm * dsn - s_act * (nrm * nrm * nrm) * jnp.sum(dsn * s_act, axis=-1, keepdims=True)
        ds = jnp.where(j < 2 * heads, ds_qk, dn)
        dy = ds * _dsilu(y, sg)
        dx = w[k - 1:k] * dy
        dw_ref[k - 1:k, :] = jnp.sum(dy * xv, axis=0, keepdims=True)
        for s in range(1, k):
            dx = dx + w[k - 1 - s:k - s] * _shift_up(dy, s)
            dw_ref[k - 1 - s:k - s, :] = jnp.sum(dy * _shift_down(xv, s), axis=0, keepdims=True)
        dx_ref[...] = dx

    return pl.pallas_call(
        body, name="qkv_conv_bwd", grid=(nblk,),
        in_specs=[pl.BlockSpec((t, LANES), lambda j: (0, j)), pl.BlockSpec((k, LANES), lambda j: (0, j)),
                  pl.BlockSpec((t, LANES), lambda j: (0, j))],
        out_specs=[pl.BlockSpec((t, LANES), lambda j: (0, j)), pl.BlockSpec((k, LANES), lambda j: (0, j))],
        out_shape=[jax.ShapeDtypeStruct(qkv_pre.shape, F32), jax.ShapeDtypeStruct(conv_w.shape, F32)],
        compiler_params=_params("arbitrary"))(qkv_pre, conv_w, dqkv)


def _pool_windows(shape, j, group_dim):
    lane = lax.broadcasted_iota(jnp.int32, shape, 1) + j * LANES
    grp = lane // group_dim
    win = jnp.left_shift(2, grp).astype(F32)
    cnt = jnp.minimum((_rows(shape) + 1).astype(F32), win)
    return grp, cnt


def _pool_select(grp, levels):
    out = levels[0]
    for gi in range(1, POOL_GROUPS):
        out = jnp.where(grp == gi, levels[gi], out)
    return out


def _pool_mean(hv, grp, cnt):
    acc, levels, width = hv, [], 1
    for _ in range(POOL_GROUPS):
        acc = acc + _shift_down(acc, width)
        width *= 2
        levels.append(acc)
    return _pool_select(grp, levels) / cnt - hv


def _pool_fwd(hp, wbd, scale, group_dim):
    t, dp = hp.shape

    def body(h_ref, w_ref, s_ref, o_ref):
        hv = h_ref[...]
        grp, cnt = _pool_windows(hv.shape, pl.program_id(0), group_dim)
        pooled = _pool_mean(hv, grp, cnt)
        o_ref[...] = _mm(pooled, w_ref[...]) * s_ref[...]

    return pl.pallas_call(
        body, name="pool_fwd", grid=(dp // LANES,),
        in_specs=[pl.BlockSpec((t, LANES), lambda j: (0, j)), pl.BlockSpec((LANES, LANES), lambda j: (j, j)),
                  pl.BlockSpec((1, LANES), lambda j: (0, j))],
        out_specs=pl.BlockSpec((t, LANES), lambda j: (0, j)),
        out_shape=jax.ShapeDtypeStruct(hp.shape, F32),
        compiler_params=_params("arbitrary"))(hp, wbd, scale)


def _pool_bwd(hp, wbd, scale, dob, group_dim):
    t, dp = hp.shape

    def body(h_ref, w_ref, s_ref, do_ref, dh_ref, dw_ref, ds_ref):
        hv = h_ref[...]
        grp, cnt = _pool_windows(hv.shape, pl.program_id(0), group_dim)
        pooled = _pool_mean(hv, grp, cnt)
        wv = w_ref[...]
        dov = do_ref[...]
        ds_ref[...] = jnp.sum(dov * _mm(pooled, wv), axis=0, keepdims=True)
        dys = dov * s_ref[...]
        dw_ref[0] = _mm_tn(pooled, dys)
        dpooled = _mm_nt(dys, wv)
        acc, levels, width = dpooled / cnt, [], 1
        for _ in range(POOL_GROUPS):
            acc = acc + _shift_up(acc, width)
            width *= 2
            levels.append(acc)
        dh_ref[...] = _pool_select(grp, levels) - dpooled

    nb = dp // LANES
    return pl.pallas_call(
        body, name="pool_bwd", grid=(nb,),
        in_specs=[pl.BlockSpec((t, LANES), lambda j: (0, j)), pl.BlockSpec((LANES, LANES), lambda j: (j, j)),
                  pl.BlockSpec((1, LANES), lambda j: (0, j)), pl.BlockSpec((t, LANES), lambda j: (0, j))],
        out_specs=[pl.BlockSpec((t, LANES), lambda j: (0, j)), pl.BlockSpec((1, LANES, LANES), lambda j: (j, 0, 0)),
                   pl.BlockSpec((1, LANES), lambda j: (0, j))],
        out_shape=[jax.ShapeDtypeStruct(hp.shape, F32), jax.ShapeDtypeStruct((nb, LANES, LANES), F32),
                   jax.ShapeDtypeStruct((1, dp), F32)],
        compiler_params=_params("arbitrary"))(hp, wbd, scale, dob)


def _sconv_fwd(cbcch, w):
    t, dc3 = cbcch.shape
    nb = dc3 // 3 // LANES
    k = SCONV_WIDTH

    def body(b_ref, c_ref, h_ref, w_ref, o_ref):
        m = c_ref[...] * h_ref[...]
        wv = w_ref[...]
        y = wv[k - 1:k] * m
        for s in range(1, k):
            y = y + wv[k - 1 - s:k - s] * _shift_down(m, s)
        o_ref[...] = b_ref[...] * y

    return pl.pallas_call(
        body, name="sconv_fwd", grid=(nb,),
        in_specs=[pl.BlockSpec((t, LANES), lambda j: (0, j)), pl.BlockSpec((t, LANES), lambda j: (0, nb + j)),
                  pl.BlockSpec((t, LANES), lambda j: (0, 2 * nb + j)), pl.BlockSpec((k, LANES), lambda j: (0, j))],
        out_specs=pl.BlockSpec((t, LANES), lambda j: (0, j)),
        out_shape=jax.ShapeDtypeStruct((t, dc3 // 3), F32),
        compiler_params=_params("arbitrary"))(cbcch, cbcch, cbcch, w)


def _sconv_bwd(cbcch, w, doc):
    t, dc3 = cbcch.shape
    nb = dc3 // 3 // LANES
    k = SCONV_WIDTH

    def body(b_ref, c_ref, h_ref, w_ref, do_ref, db_ref, dc_ref, dh_ref, dw_ref):
        cv, hv = c_ref[...], h_ref[...]
        m = cv * hv
        wv = w_ref[...]
        dov = do_ref[...]
        dy = dov * b_ref[...]
        y = wv[k - 1:k] * m
        dm = wv[k - 1:k] * dy
        dw_ref[k - 1:k, :] = jnp.sum(dy * m, axis=0, keepdims=True)
        for s in range(1, k):
            ms = _shift_down(m, s)
            y = y + wv[k - 1 - s:k - s] * ms
            dm = dm + wv[k - 1 - s:k - s] * _shift_up(dy, s)
            dw_ref[k - 1 - s:k - s, :] = jnp.sum(dy * ms, axis=0, keepdims=True)
        db_ref[...] = dov * y
        dc_ref[...] = dm * hv
        dh_ref[...] = dm * cv

    col = lambda o: pl.BlockSpec((t, LANES), lambda j: (0, o * nb + j))
    return pl.pallas_call(
        body, name="sconv_bwd", grid=(nb,),
        in_specs=[col(0), col(1), col(2), pl.BlockSpec((k, LANES), lambda j: (0, j)), col(0)],
        out_specs=[col(0), col(0), col(0), pl.BlockSpec((k, LANES), lambda j: (0, j))],
        out_shape=[jax.ShapeDtypeStruct((t, dc3 // 3), F32)] * 3 + [jax.ShapeDtypeStruct(w.shape, F32)],
        compiler_params=_params("arbitrary"))(cbcch, cbcch, cbcch, w, doc)


def _inv_unit_lower(low):
    c = low.shape[0]
    eye = (_rows((c, c)) == lax.broadcasted_iota(jnp.int32, (c, c), 1)).astype(F32)
    pw = -low
    inv = eye + pw
    span = 2
    while span < c:
        pw = _hmm(pw, pw)
        inv = inv + _hmm(inv, pw)
        span *= 2
    return inv


def _chunk_common(q, k, v, a_col, b_col, alog, dtb):
    c = q.shape[0]
    beta = _sigmoid(b_col)
    xg = a_col + dtb
    softplus = jnp.maximum(xg, 0.0) + jnp.log(1.0 + jnp.exp(-jnp.abs(xg)))
    neg_ea = -jnp.exp(alog)
    g = neg_ea * softplus
    ri = _rows((c, c))
    ci = lax.broadcasted_iota(jnp.int32, (c, c), 1)
    incl, strict = ri >= ci, ri > ci
    inclf = incl.astype(F32)
    gcb = _hmm(inclf, jnp.broadcast_to(g, (c, HEAD_DIM)))
    gc_row = jnp.sum(jnp.where(ri <= ci, jnp.broadcast_to(g, (c, c)), 0.0), axis=0, keepdims=True)
    dmat = jnp.where(incl, jnp.exp(jnp.where(incl, gcb[:, :1] - gc_row, 0.0)), 0.0)
    eg = jnp.exp(gcb)
    gl = gcb[c - 1:c, :]
    egl = jnp.exp(gl)
    edl = jnp.exp(gl - gcb)
    kb, vb = k * beta, v * beta
    a0 = _hmm_nt(kb, k)
    tm = _inv_unit_lower(jnp.where(strict, a0 * dmat, 0.0))
    kbe = kb * eg
    p0 = _hmm_nt(q, k)
    return dict(beta=beta, xg=xg, neg_ea=neg_ea, g=g, incl=incl, strict=strict, inclf=inclf, dmat=dmat, eg=eg,
                egl=egl, edl=edl, kb=kb, vb=vb, a0=a0, tm=tm, kbe=kbe, u=_hmm(tm, vb), w=_hmm(tm, kbe), p0=p0,
                attn=p0 * dmat, qe=q * eg, kd=k * edl)


def _chunk_step(cm, state):
    vn = cm["u"] - _hmm(cm["w"], state)
    o = _hmm(cm["qe"], state) + _hmm(cm["attn"], vn)
    new_state = state * cm["egl"][:, :1] + _hmm_tn(cm["kd"], vn)
    return vn, o, new_state


def _gated_norm(o, zv, og):
    xo, ro = _rms_fwd(o)
    sgz = _sigmoid(zv)
    return xo, ro, sgz, xo * og * (zv * sgz)


def _delta_fwd(qkv, z, ab, gpar, heads):
    t = qkv.shape[0]
    da = heads * HEAD_DIM
    n = t // CHUNK

    def body(qkv_ref, z_ref, ab_ref, gp_ref, oa_ref, st_ref, s_ref):
        @pl.when(pl.program_id(0) == 0)
        def _():
            s_ref[...] = jnp.zeros_like(s_ref)

        abv, gpv = ab_ref[...], gp_ref[...]
        for h in range(heads):
            lo = h * HEAD_DIM
            cm = _chunk_common(qkv_ref[:, lo:lo + HEAD_DIM], qkv_ref[:, da + lo:da + lo + HEAD_DIM],
                               qkv_ref[:, 2 * da + lo:2 * da + lo + HEAD_DIM], abv[:, h:h + 1],
                               abv[:, heads + h:heads + h + 1], gpv[0:1, h:h + 1], gpv[1:2, h:h + 1])
            state = s_ref[h]
            st_ref[0, h] = state
            _, o, new_state = _chunk_step(cm, state)
            s_ref[h] = new_state
            oa_ref[:, lo:lo + HEAD_DIM] = _gated_norm(o, z_ref[:, lo:lo + HEAD_DIM], gpv[2:3, :])[3]

    return pl.pallas_call(
        body, name="delta_fwd", grid=(n,),
        in_specs=[pl.BlockSpec((CHUNK, 3 * da), lambda i: (i, 0)), pl.BlockSpec((CHUNK, da), lambda i: (i, 0)),
                  pl.BlockSpec((CHUNK, LANES), lambda i: (i, 0)), pl.BlockSpec((8, LANES), lambda i: (0, 0))],
        out_specs=[pl.BlockSpec((CHUNK, da), lambda i: (i, 0)),
                   pl.BlockSpec((1, heads, HEAD_DIM, HEAD_DIM), lambda i: (i, 0, 0, 0))],
        out_shape=[jax.ShapeDtypeStruct((t, da), F32), jax.ShapeDtypeStruct((n, heads, HEAD_DIM, HEAD_DIM), F32)],
        scratch_shapes=[pltpu.VMEM((heads, HEAD_DIM, HEAD_DIM), F32)],
        compiler_params=_params("arbitrary"))(qkv, z, ab, gpar)


def _delta_bwd(qkv, z, ab, gpar, states, doa, heads):
    t = qkv.shape[0]
    da = heads * HEAD_DIM
    n = t // CHUNK
    c = CHUNK

    def body(qkv_ref, z_ref, ab_ref, gp_ref, st_ref, doa_ref, dqkv_ref, dz_ref, dab_ref, dpar_ref, ds_ref):
        @pl.when(pl.program_id(0) == 0)
        def _():
            ds_ref[...] = jnp.zeros_like(ds_ref)
            dpar_ref[...] = jnp.zeros_like(dpar_ref)

        abv, gpv = ab_ref[...], gp_ref[...]
        og = gpv[2:3, :]
        lane = lax.broadcasted_iota(jnp.int32, (c, LANES), 1)
        lane8 = lax.broadcasted_iota(jnp.int32, (8, LANES), 1)
        row8 = _rows((8, LANES))
        ones = jnp.ones((c, HEAD_DIM), F32)
        dab = jnp.zeros((c, LANES), F32)
        dpar = jnp.zeros((8, LANES), F32)
        for h in range(heads):
            lo = h * HEAD_DIM
            q = qkv_ref[:, lo:lo + HEAD_DIM]
            k = qkv_ref[:, da + lo:da + lo + HEAD_DIM]
            v = qkv_ref[:, 2 * da + lo:2 * da + lo + HEAD_DIM]
            cm = _chunk_common(q, k, v, abv[:, h:h + 1], abv[:, heads + h:heads + h + 1], gpv[0:1, h:h + 1],
                               gpv[1:2, h:h + 1])
            state = st_ref[0, h]
            dsp = ds_ref[h]
            vn, o, _ = _chunk_step(cm, state)
            zv = z_ref[:, lo:lo + HEAD_DIM]
            xo, ro, sgz, _ = _gated_norm(o, zv, og)
            doav = doa_ref[:, lo:lo + HEAD_DIM]
            don = doav * (zv * sgz)
            dz_ref[:, lo:lo + HEAD_DIM] = doav * (xo * og) * _dsilu(zv, sgz)
            d_og = jnp.sum(don * xo, axis=0, keepdims=True)
            do = _rms_bwd(don * og, xo, ro)
            tm, dmat, eg, edl, egl = cm["tm"], cm["dmat"], cm["eg"], cm["edl"], cm["egl"]
            dvn = _hmm_tn(cm["attn"], do) + _hmm(cm["kd"], dsp)
            dqe = _hmm_nt(do, state)
            ds_ref[h] = _hmm_tn(cm["qe"], do) + dsp * egl[:, :1] - _hmm_tn(cm["w"], dvn)
            dattn = _hmm_nt(do, vn)
            dkd = _hmm_nt(vn, dsp)
            dkd_kd = jnp.sum(dkd * cm["kd"], axis=-1, keepdims=True)
            dgl = (jnp.sum(jnp.sum(dsp * state, axis=-1, keepdims=True), axis=0, keepdims=True) * egl[:, :1]
                   + jnp.sum(dkd_kd, axis=0, keepdims=True))
            dgc = jnp.sum(dqe * cm["qe"], axis=-1, keepdims=True) - dkd_kd
            dk = dkd * edl
            dq = dqe * eg
            dw = -_hmm_nt(dvn, state)
            dp0 = dattn * dmat
            dd = jnp.where(cm["incl"], dattn * cm["p0"], 0.0)
            dq = dq + _hmm(dp0, k)
            dk = dk + _hmm_tn(dp0, q)
            dtm = _hmm_nt(dvn, cm["vb"]) + _hmm_nt(dw, cm["kbe"])
            dvb = _hmm_tn(tm, dvn)
            dkbe = _hmm_tn(tm, dw)
            dkb = dkbe * eg
            dgc = dgc + jnp.sum(dkbe * cm["kbe"], axis=-1, keepdims=True)
            dlow = jnp.where(cm["strict"], -_hmm_tn(tm, _hmm_nt(dtm, tm)), 0.0)
            dd = dd + dlow * cm["a0"]
            da0 = dlow * dmat
            dkb = dkb + _hmm(da0, k)
            dk = dk + _hmm_tn(da0, cm["kb"])
            ddd = dd * dmat
            dgc = dgc + jnp.sum(ddd, axis=-1, keepdims=True) - _hmm_tn(ddd, ones)[:, :1]
            dgc = dgc + jnp.where(_rows((c, 1)) == c - 1, dgl, 0.0)
            dg = _hmm_tn(cm["inclf"], jnp.broadcast_to(dgc, (c, HEAD_DIM)))[:, :1]
            beta = cm["beta"]
            dk = dk + dkb * beta
            dbeta = jnp.sum(dkb * k, axis=-1, keepdims=True) + jnp.sum(dvb * v, axis=-1, keepdims=True)
            dqkv_ref[:, lo:lo + HEAD_DIM] = dq
            dqkv_ref[:, da + lo:da + lo + HEAD_DIM] = dk
            dqkv_ref[:, 2 * da + lo:2 * da + lo + HEAD_DIM] = dvb * beta
            db_col = dbeta * beta * (1.0 - beta)
            da_col = dg * cm["neg_ea"] * _sigmoid(cm["xg"])
            dab = dab + jnp.where(lane == h, da_col, 0.0) + jnp.where(lane == heads + h, db_col, 0.0)
            d_alog = jnp.sum(dg * cm["g"], axis=0, keepdims=True)
            d_dtb = jnp.sum(da_col, axis=0, keepdims=True)
            dpar = (dpar + jnp.where((row8 == 0) & (lane8 == h), d_alog, 0.0)
                    + jnp.where((row8 == 1) & (lane8 == h), d_dtb, 0.0) + jnp.where(row8 == 2, d_og, 0.0))
        dab_ref[...] = dab
        dpar_ref[...] += dpar

    rev = lambda i: (n - 1 - i, 0)
    return pl.pallas_call(
        body, name="delta_bwd", grid=(n,),
        in_specs=[pl.BlockSpec((c, 3 * da), rev), pl.BlockSpec((c, da), rev), pl.BlockSpec((c, LANES), rev),
                  pl.BlockSpec((8, LANES), lambda i: (0, 0)),
                  pl.BlockSpec((1, heads, HEAD_DIM, HEAD_DIM), lambda i: (n - 1 - i, 0, 0, 0)),
                  pl.BlockSpec((c, da), rev)],
        out_specs=[pl.BlockSpec((c, 3 * da), rev), pl.BlockSpec((c, da), rev), pl.BlockSpec((c, LANES), rev),
                   pl.BlockSpec((8, LANES), lambda i: (0, 0))],
        out_shape=[jax.ShapeDtypeStruct((t, 3 * da), F32), jax.ShapeDtypeStruct((t, da), F32),
                   jax.ShapeDtypeStruct((t, LANES), F32), jax.ShapeDtypeStruct((8, LANES), F32)],
        scratch_shapes=[pltpu.VMEM((heads, HEAD_DIM, HEAD_DIM), F32)],
        compiler_params=_params("arbitrary"))(qkv, z, ab, gpar, states, doa)


def _w_in_pieces(shard_cols, da, heads):
    a0, nab = 4 * da, 2 * heads
    d_in = 4 * shard_cols
    runs = [(0, a0, 0), (a0, a0 + nab, d_in - nab), (a0 + nab, d_in, a0)]
    pieces = []
    for j in range(4):
        lo, hi = j * shard_cols, (j + 1) * shard_cols
        for rlo, rhi, plo in runs:
            s, e = max(lo, rlo), min(hi, rhi)
            if s < e:
                pieces.append((j, s - lo, e - s, plo + (s - rlo)))
    return pieces, d_in - nab + LANES


def _w_in_pack(w4, li, da, heads):
    _, _, d, sc = w4.shape
    pieces, npk = _w_in_pieces(sc, da, heads)
    tr = _tile_rows(d, 256, SUBLANES_WIRE)

    def body(w_ref, o_ref):
        o_ref[:, npk - LANES:] = jnp.zeros((tr, LANES), o_ref.dtype)
        for j, lo, ln, dst in pieces:
            o_ref[:, dst:dst + ln] = w_ref[j, :, lo:lo + ln]

    return pl.pallas_call(
        body, name="w_in_pack", grid=(d // tr,),
        in_specs=[pl.BlockSpec((4, None, tr, sc), lambda i: (0, li, i, 0))],
        out_specs=pl.BlockSpec((tr, npk), lambda i: (i, 0)),
        out_shape=jax.ShapeDtypeStruct((d, npk), w4.dtype),
        compiler_params=_params("arbitrary"))(w4)


def _w_in_unpack(dwp, sc, da, heads):
    d, npk = dwp.shape
    pieces, _ = _w_in_pieces(sc, da, heads)
    tr = _tile_rows(d, 256)

    def body(g_ref, o_ref):
        for j, lo, ln, dst in pieces:
            o_ref[j, :, lo:lo + ln] = g_ref[:, dst:dst + ln]

    return pl.pallas_call(
        body, name="w_in_unpack", grid=(d // tr,),
        in_specs=[pl.BlockSpec((tr, npk), lambda i: (i, 0))],
        out_specs=pl.BlockSpec((4, tr, sc), lambda i: (0, i, 0)),
        out_shape=jax.ShapeDtypeStruct((4, d, sc), F32),
        compiler_params=_params("arbitrary"))(dwp)


def _block_diag(pool_w):
    g, gd, _ = pool_w.shape
    out = jnp.zeros((g * gd, g * gd), pool_w.dtype)
    for gi in range(g):
        out = lax.dynamic_update_slice(out, pool_w[gi], (gi * gd, gi * gd))
    return out


def _layer_dims(d):
    heads = (d // 2) // HEAD_DIM
    return heads, heads * HEAD_DIM, d // 4, d // 4


BIG = ("w_in", "w_gate", "w_up", "ple_proj", "w_out", "w_down", "ple_gate")


def _prepare_layer(gw, small, li):
    d = small["norm1_g"].shape[1]
    heads, da, _, _ = _layer_dims(d)
    gpar = jnp.zeros((8, LANES), F32)
    gpar = gpar.at[0, :heads].set(small["a_log"][li]).at[1, :heads].set(small["dt_bias"][li]).at[2, :].set(small["onorm_g"][li])
    return dict(norm1_g=small["norm1_g"][li][None], w_in_p=_w_in_pack(gw["w_in"], li, da, heads).astype(MM_DTYPE),
                conv_qkv=small["conv_qkv"][li], gpar=gpar, pool_bd=_block_diag(small["pool_w"][li]).astype(MM_DTYPE),
                pool_scale=small["pool_scale"][li][None], sconv_w=small["sconv_w"][li], norm2_g=small["norm2_g"][li][None])


def _layer_fwd(x0, p, gw, lw, li, tm):
    d = x0.shape[1]
    heads, da, dp, dc = _layer_dims(d)
    segs = (3 * da, da, dp, 3 * dc, LANES)
    qkv_pre, z, hp, cbcch, ab = _in_proj_fwd(x0, lw["norm1_g"], lw["w_in_p"], segs, tm)
    qkv = _qkv_conv_fwd(qkv_pre, lw["conv_qkv"], heads)
    oa, states = _delta_fwd(qkv, z, ab, lw["gpar"], heads)
    ob = _pool_fwd(hp, lw["pool_bd"], lw["pool_scale"], dp // POOL_GROUPS)
    oc = _sconv_fwd(cbcch, lw["sconv_w"])
    x1, h2 = _out_proj_fwd(x0, (oa, ob, oc), gw["w_out"], li, lw["norm2_g"], tm)
    x2, gp, up = _ffn_fwd(x1, h2, gw["w_gate"], gw["w_up"], gw["w_down"], li, tm)
    x3 = _ple_fwd(x2, p, gw["ple_gate"], gw["ple_proj"], li, tm)
    saved = dict(x0=x0, qkv_pre=qkv_pre, z=z, hp=hp, cbcch=cbcch, ab=ab, qkv=qkv, states=states, oa=oa, ob=ob, oc=oc,
                 x1=x1, h2=h2, gp=gp, up=up, x2=x2)
    return x3, saved


def _layer_bwd(dx3, p, gw, lw, li, sv, tm):
    d = dx3.shape[1]
    heads, da, dp, dc = _layer_dims(d)
    segs = (3 * da, da, dp, dc, dc, dc, LANES)
    gd = dp // POOL_GROUPS
    dx2, d_ple_gate, d_ple_proj = _ple_bwd(dx3, sv["x2"], p, gw["ple_gate"], gw["ple_proj"], li, tm)
    dh2, d_w_gate, d_w_up, d_w_down = _ffn_bwd(dx2, sv["h2"], sv["gp"], sv["up"], gw["w_gate"], gw["w_up"], gw["w_down"],
                                               li, min(tm, 256))
    dx1, doa, dob, doc, d_w_out, d_norm2 = _out_proj_bwd(dx2, dh2, sv["x1"], lw["norm2_g"],
                                                         (sv["oa"], sv["ob"], sv["oc"]), gw["w_out"], li, tm)
    dcb, dcc, dch, d_sconv = _sconv_bwd(sv["cbcch"], lw["sconv_w"], doc)
    dhp, d_pool_bd, d_pool_scale = _pool_bwd(sv["hp"], lw["pool_bd"], lw["pool_scale"], dob, gd)
    dqkv, dz, dab, dpar = _delta_bwd(sv["qkv"], sv["z"], sv["ab"], lw["gpar"], sv["states"], doa, heads)
    dqkv_pre, d_conv_qkv = _qkv_conv_bwd(sv["qkv_pre"], lw["conv_qkv"], dqkv, heads)
    dsegs = (dqkv_pre, dz, dhp, dcb, dcc, dch, dab)
    dx0, d_w_in_p, d_norm1 = _in_proj_bwd(sv["x0"], lw["norm1_g"], lw["w_in_p"], dsegs, dx1, segs, tm)
    per = LANES // gd
    bd = d_pool_bd.reshape(dp // LANES, per, gd, per, gd)
    d_pool_w = jnp.stack([bd[gi // per, gi % per, :, gi % per, :] for gi in range(POOL_GROUPS)])
    big = dict(w_in=_w_in_unpack(d_w_in_p, gw["w_in"].shape[3], da, heads), w_gate=d_w_gate, w_up=d_w_up,
               ple_proj=d_ple_proj, w_out=d_w_out, w_down=d_w_down, ple_gate=d_ple_gate)
    small = dict(norm1_g=d_norm1[0], conv_qkv=d_conv_qkv, a_log=dpar[0, :heads], dt_bias=dpar[1, :heads], onorm_g=dpar[2],
                 pool_w=d_pool_w, pool_scale=d_pool_scale[0], sconv_w=d_sconv, norm2_g=d_norm2[0])
    return dx0, big, small


def _local_step(x, p, target, gw, small):
    t, d = x.shape
    depth = p.shape[0]
    tm = 512 if t % 512 == 0 else 128
    layers = [_prepare_layer(gw, small, li) for li in range(depth)]
    saved = []
    h = x
    for li in range(depth):
        h, sv = _layer_fwd(h, p[li], gw, layers[li], li, tm)
        saved.append(sv)
    dx, loss, d_final = _loss_head(h, target, small["final_g"][None], tm)
    big, sm = [None] * depth, [None] * depth
    for li in reversed(range(depth)):
        dx, big[li], sm[li] = _layer_bwd(dx, p[li], gw, layers[li], li, saved[li], tm)
    small_grads = {n: jnp.stack([g[n] for g in sm]) for n in sm[0]}
    small_grads["final_g"] = d_final[0]
    return loss[0, 0], dx, big, small_grads


def _coords():
    return lax.axis_index("x"), lax.axis_index("y"), lax.axis_index("c")


def _other_chips(x, y):
    return [(1 - x, y), (x, 1 - y), (1 - x, 1 - y)]


def _all_gather_chips(shards):
    nt = len(shards)

    def body(*refs):
        x_refs, out_refs = refs[:nt], refs[nt:2 * nt]
        send_sems, recv_sems, local_sems = refs[2 * nt:]
        x, y, c = _coords()
        sibling = (x, y, 1 - c)
        chips = _other_chips(x, y)

        def copy(k, t, block, to, src=None):
            px, py, pc = block
            dst = out_refs[t].at[2 * px + py, pc]
            return pltpu.make_async_remote_copy(src_ref=dst if src is None else src, dst_ref=dst, send_sem=send_sems.at[k, t],
                                                recv_sem=recv_sems.at[k, t], device_id=to, device_id_type=MESH)

        mine = [pltpu.make_async_copy(x_refs[t], out_refs[t].at[2 * x + y], local_sems.at[t]) for t in range(nt)]
        for cp in mine:
            cp.start()
        first = [copy(j, t, (x, y, c), (*chip, c), src=x_refs[t].at[c]) for j, chip in enumerate(chips) for t in range(nt)]
        for cp in first:
            cp.start()
        passed = []
        for j, chip in enumerate(chips):
            for t in range(nt):
                copy(j, t, (*chip, c), (x, y, c)).wait_recv()
                fwd = copy(3 + j, t, (*chip, c), sibling)
                fwd.start()
                passed.append(fwd)
        for j, chip in enumerate(chips):
            for t in range(nt):
                copy(3 + j, t, (*chip, 1 - c), (x, y, c)).wait_recv()
        for cp in first + passed:
            cp.wait_send()
        for cp in mine:
            cp.wait()

    return pl.pallas_call(
        body, name="all_gather_chips", out_shape=[jax.ShapeDtypeStruct((4,) + s.shape, s.dtype) for s in shards],
        in_specs=[ANY] * nt, out_specs=[ANY] * nt,
        scratch_shapes=[pltpu.SemaphoreType.DMA((6, nt)), pltpu.SemaphoreType.DMA((6, nt)), pltpu.SemaphoreType.DMA((nt,))],
    )(*shards)


def _sibling_swap_half(gs):
    nt = len(gs)

    def body(*refs):
        g_refs, out_refs = refs[:nt], refs[nt:2 * nt]
        send_sems, recv_sems = refs[2 * nt:]
        x, y, c = _coords()
        cps = []
        for t in range(nt):
            rh = g_refs[t].shape[1] // 2
            cps.append(pltpu.make_async_remote_copy(src_ref=g_refs[t].at[:, pl.ds((1 - c) * rh, rh)], dst_ref=out_refs[t],
                                                    send_sem=send_sems.at[t], recv_sem=recv_sems.at[t], device_id=(x, y, 1 - c),
                                                    device_id_type=MESH))
        for cp in cps:
            cp.start()
        for cp in cps:
            cp.wait()

    return pl.pallas_call(
        body, name="sibling_swap_half",
        out_shape=[jax.ShapeDtypeStruct((g.shape[0], g.shape[1] // 2, g.shape[2]), g.dtype) for g in gs],
        in_specs=[ANY] * nt, out_specs=[ANY] * nt,
        scratch_shapes=[pltpu.SemaphoreType.DMA((nt,)), pltpu.SemaphoreType.DMA((nt,))])(*gs)


def _add_my_half(g, other, c_idx):
    n, rows, cols = g.shape
    rh = rows // 2
    tr = _tile_rows(rh, 512, SUBLANES_WIRE)
    nb = rh // tr

    def body(c_ref, g_ref, o_ref, out_ref):
        out_ref[...] = (g_ref[...].astype(F32) + o_ref[...].astype(F32)).astype(out_ref.dtype)

    return pl.pallas_call(
        body, name="add_my_half",
        grid_spec=pltpu.PrefetchScalarGridSpec(
            num_scalar_prefetch=1, grid=(n, nb),
            in_specs=[pl.BlockSpec((None, tr, cols), lambda j, i, c_ref: (j, c_ref[0] * nb + i, 0)),
                      pl.BlockSpec((None, tr, cols), lambda j, i, c_ref: (j, i, 0))],
            out_specs=pl.BlockSpec((None, tr, cols), lambda j, i, c_ref: (j, i, 0))),
        out_shape=jax.ShapeDtypeStruct((n, rh, cols), WIRE_DTYPE),
        compiler_params=_params("arbitrary", "arbitrary"))(c_idx, g, other)


def _exchange_chips(parts):
    nt = len(parts)

    def body(*refs):
        p_refs, out_refs = refs[:nt], refs[nt:2 * nt]
        send_sems, recv_sems, local_sems = refs[2 * nt:]
        x, y, c = _coords()
        me = 2 * x + y
        chips = _other_chips(x, y)
        mine = [pltpu.make_async_copy(p_refs[t].at[me], out_refs[t].at[me], local_sems.at[t]) for t in range(nt)]
        for cp in mine:
            cp.start()
        sends = [pltpu.make_async_remote_copy(src_ref=p_refs[t].at[2 * cx + cy], dst_ref=out_refs[t].at[me],
                                              send_sem=send_sems.at[j, t], recv_sem=recv_sems.at[j, t], device_id=(cx, cy, c),
                                              device_id_type=MESH)
                 for j, (cx, cy) in enumerate(chips) for t in range(nt)]
        for cp in sends:
            cp.start()
        for j, (cx, cy) in enumerate(chips):
            for t in range(nt):
                pltpu.make_async_remote_copy(src_ref=p_refs[t].at[me], dst_ref=out_refs[t].at[2 * cx + cy],
                                             send_sem=send_sems.at[j, t], recv_sem=recv_sems.at[j, t], device_id=(cx, cy, c),
                                             device_id_type=MESH).wait_recv()
        for cp in sends:
            cp.wait_send()
        for cp in mine:
            cp.wait()

    return pl.pallas_call(
        body, name="exchange_chips", out_shape=[jax.ShapeDtypeStruct(p.shape, p.dtype) for p in parts],
        in_specs=[ANY] * nt, out_specs=[ANY] * nt,
        scratch_shapes=[pltpu.SemaphoreType.DMA((3, nt)), pltpu.SemaphoreType.DMA((3, nt)), pltpu.SemaphoreType.DMA((nt,))])(*parts)


def _sum_slots(parts):
    n, rows, cols = parts.shape
    tr = _tile_rows(rows, 512, SUBLANES_WIRE)

    def body(p_ref, out_ref):
        acc = p_ref[0].astype(F32)
        for s in range(1, n):
            acc = acc + p_ref[s].astype(F32)
        out_ref[...] = acc

    return pl.pallas_call(
        body, name="sum_slots", grid=(rows // tr,),
        in_specs=[pl.BlockSpec((n, tr, cols), lambda i: (0, i, 0))],
        out_specs=pl.BlockSpec((tr, cols), lambda i: (i, 0)),
        out_shape=jax.ShapeDtypeStruct((rows, cols), F32),
        compiler_params=_params("arbitrary"))(parts)


def _sibling_share(halves):
    nt, depth = len(halves), len(halves[0])
    flat = [h for hs in halves for h in hs]
    n = len(flat)

    def body(*refs):
        h_refs, out_refs = refs[:n], refs[n:n + nt]
        send_sems, recv_sems, local_sems = refs[n + nt:]
        x, y, c = _coords()
        mine, sends, recvs = [], [], []
        for t in range(nt):
            for li in range(depth):
                k = t * depth + li
                rh = h_refs[k].shape[0]
                mine.append(pltpu.make_async_copy(h_refs[k], out_refs[t].at[li, pl.ds(c * rh, rh)], local_sems.at[k]))
                sends.append(pltpu.make_async_remote_copy(src_ref=h_refs[k], dst_ref=out_refs[t].at[li, pl.ds(c * rh, rh)],
                                                          send_sem=send_sems.at[k], recv_sem=recv_sems.at[k],
                                                          device_id=(x, y, 1 - c), device_id_type=MESH))
                recvs.append(pltpu.make_async_remote_copy(src_ref=h_refs[k], dst_ref=out_refs[t].at[li, pl.ds((1 - c) * rh, rh)],
                                                          send_sem=send_sems.at[k], recv_sem=recv_sems.at[k],
                                                          device_id=(x, y, 1 - c), device_id_type=MESH))
        for cp in mine + sends:
            cp.start()
        for cp in recvs:
            cp.wait_recv()
        for cp in sends:
            cp.wait_send()
        for cp in mine:
            cp.wait()

    return pl.pallas_call(
        body, name="sibling_share",
        out_shape=[jax.ShapeDtypeStruct((depth, 2 * hs[0].shape[0], hs[0].shape[1]), hs[0].dtype) for hs in halves],
        in_specs=[ANY] * n, out_specs=[ANY] * nt,
        scratch_shapes=[pltpu.SemaphoreType.DMA((n,)), pltpu.SemaphoreType.DMA((n,)), pltpu.SemaphoreType.DMA((n,))])(*flat)


def _all_gather_devices(buf):
    def body(b_ref, out_ref, send_sems, recv_sems, local_sem):
        x, y, c = _coords()
        me = 4 * x + 2 * y + c
        mine = pltpu.make_async_copy(b_ref, out_ref.at[me], local_sem)
        mine.start()
        peers = []
        for k in range(1, 8):
            fx, fy, fc = (k >> 2) & 1, (k >> 1) & 1, k & 1
            peers.append((x ^ fx, y ^ fy, c ^ fc))
        sends = [pltpu.make_async_remote_copy(src_ref=b_ref, dst_ref=out_ref.at[me], send_sem=send_sems.at[k],
                                              recv_sem=recv_sems.at[k], device_id=peer, device_id_type=MESH)
                 for k, peer in enumerate(peers)]
        for cp in sends:
            cp.start()
        for k, (px, py, pc) in enumerate(peers):
            pltpu.make_async_remote_copy(src_ref=b_ref, dst_ref=out_ref.at[4 * px + 2 * py + pc], send_sem=send_sems.at[k],
                                         recv_sem=recv_sems.at[k], device_id=(px, py, pc), device_id_type=MESH).wait_recv()
        for cp in sends:
            cp.wait_send()
        mine.wait()

    return pl.pallas_call(
        body, name="all_gather_devices", out_shape=jax.ShapeDtypeStruct((8,) + buf.shape, buf.dtype),
        in_specs=[ANY], out_specs=ANY,
        scratch_shapes=[pltpu.SemaphoreType.DMA((7,)), pltpu.SemaphoreType.DMA((7,)), pltpu.SemaphoreType.DMA(())])(buf)


def _reduce_layer(big_grads, c_idx):
    gs = [big_grads[n] for n in BIG]
    others = _sibling_swap_half(gs)
    pairs = [_add_my_half(g, o, c_idx) for g, o in zip(gs, others)]
    return [_sum_slots(parts) for parts in _exchange_chips(pairs)]


SMALL_SHARDED = ("conv_qkv", "sconv_w")
REPLICATED = ("norm1_g", "a_log", "dt_bias", "onorm_g", "pool_w", "pool_scale", "norm2_g", "final_g")
ALL_WEIGHTS = ("norm1_g", "w_in", "conv_qkv", "a_log", "dt_bias", "onorm_g", "pool_w", "pool_scale", "sconv_w", "w_out",
               "norm2_g", "w_gate", "w_up", "w_down", "ple_proj", "ple_gate", "final_g")


def _pad_rows(flat, row_multiple):
    m = flat.shape[0]
    r = -(-m // (LANES * row_multiple)) * row_multiple
    return jnp.pad(flat, (0, r * LANES - m)).reshape(r, LANES)


def _adamw(w, g, m, v):
    shape = w.shape
    cols = shape[-1]
    rows = w.size // cols
    tr = _tile_rows(rows, 512)
    c1 = 1.0 / (1.0 - ADAM_B1 ** ADAM_STEP)
    c2 = 1.0 / (1.0 - ADAM_B2 ** ADAM_STEP)

    def body(w_ref, g_ref, m_ref, v_ref, d_ref, nm_ref, nv_ref):
        gv = g_ref[...]
        nm = ADAM_B1 * m_ref[...] + (1.0 - ADAM_B1) * gv
        nv = ADAM_B2 * v_ref[...] + (1.0 - ADAM_B2) * (gv * gv)
        nm_ref[...] = nm
        nv_ref[...] = nv
        d_ref[...] = -ADAM_LR * ((nm * c1) / (jnp.sqrt(nv * c2) + ADAM_EPS) + ADAM_WD * w_ref[...])

    spec = pl.BlockSpec((tr, cols), lambda i: (i, 0))
    outs = pl.pallas_call(
        body, name="adamw", grid=(rows // tr,), in_specs=[spec] * 4, out_specs=[spec] * 3,
        out_shape=[jax.ShapeDtypeStruct((rows, cols), F32)] * 3,
        compiler_params=_params("arbitrary"))(*[a.reshape(rows, cols) for a in (w, g, m, v)])
    return tuple(o.reshape(shape) for o in outs)


def kernel(x, p, norm1_g, w_in, conv_qkv, a_log, dt_bias, onorm_g, pool_w, pool_scale, sconv_w, w_out, norm2_g, w_gate, w_up, w_down, ple_proj, ple_gate, final_g, loss_target, m_norm1_g, m_w_in, m_conv_qkv, m_a_log, m_dt_bias, m_onorm_g, m_pool_w, m_pool_scale, m_sconv_w, m_w_out, m_norm2_g, m_w_gate, m_w_up, m_w_down, m_ple_proj, m_ple_gate, m_final_g, v_norm1_g, v_w_in, v_conv_qkv, v_a_log, v_dt_bias, v_onorm_g, v_pool_w, v_pool_scale, v_sconv_w, v_w_out, v_norm2_g, v_w_gate, v_w_up, v_w_down, v_ple_proj, v_ple_gate, v_final_g):
    weights = dict(zip(ALL_WEIGHTS, (norm1_g, w_in, conv_qkv, a_log, dt_bias, onorm_g, pool_w, pool_scale, sconv_w, w_out,
                                     norm2_g, w_gate, w_up, w_down, ple_proj, ple_gate, final_g)))
    mom_m = dict(zip(ALL_WEIGHTS, (m_norm1_g, m_w_in, m_conv_qkv, m_a_log, m_dt_bias, m_onorm_g, m_pool_w, m_pool_scale,
                                   m_sconv_w, m_w_out, m_norm2_g, m_w_gate, m_w_up, m_w_down, m_ple_proj, m_ple_gate, m_final_g)))
    mom_v = dict(zip(ALL_WEIGHTS, (v_norm1_g, v_w_in, v_conv_qkv, v_a_log, v_dt_bias, v_onorm_g, v_pool_w, v_pool_scale,
                                   v_sconv_w, v_w_out, v_norm2_g, v_w_gate, v_w_up, v_w_down, v_ple_proj, v_ple_gate, v_final_g)))
    c_idx = lax.axis_index("c").astype(jnp.int32).reshape(1)
    chip = 2 * lax.axis_index("x") + lax.axis_index("y")
    depth = p.shape[0]

    gathered = _all_gather_chips([weights[n].astype(WIRE_DTYPE) for n in BIG])
    gw = dict(zip(BIG, gathered))
    small = {n: weights[n] for n in REPLICATED}
    sflat = _pad_rows(jnp.concatenate([weights[n].reshape(-1) for n in SMALL_SHARDED]), 8)
    sgath = _all_gather_devices(sflat)[0::2].reshape(4, -1)
    off = 0
    for n in SMALL_SHARDED:
        shp = weights[n].shape
        part = sgath[:, off:off + weights[n].size].reshape((4,) + shp)
        small[n] = jnp.moveaxis(part, 0, -2).reshape(shp[:-1] + (4 * shp[-1],))
        off += weights[n].size

    loss_local, dx, big_grads, small_grads = _local_step(x[0], p[:, 0], loss_target[0], gw, small)

    halves = [_reduce_layer(big_grads[li], c_idx) for li in range(depth)]
    shared = _sibling_share([[halves[li][t] for li in range(depth)] for t in range(len(BIG))])
    gshard = {n: s.reshape(weights[n].shape) for n, s in zip(BIG, shared)}

    rnames = REPLICATED + SMALL_SHARDED
    rflat = _pad_rows(jnp.concatenate([small_grads[n].reshape(-1) for n in rnames]), 8)
    rsum = _sum_slots(_all_gather_devices(rflat)).reshape(-1)
    off = 0
    for n in rnames:
        whole = rsum[off:off + small_grads[n].size].reshape(small_grads[n].shape)
        off += small_grads[n].size
        if n in SMALL_SHARDED:
            cols = weights[n].shape[-1]
            whole = lax.dynamic_slice_in_dim(whole, chip * cols, cols, axis=whole.ndim - 1)
        gshard[n] = whole

    loss = lax.psum(loss_local, ("x", "y", "c"))
    deltas, new_m, new_v = {}, {}, {}
    for n in ALL_WEIGHTS:
        deltas[n], new_m[n], new_v[n] = _adamw(weights[n], gshard[n], mom_m[n], mom_v[n])
    return (loss, dx[None], *[gshard[n] for n in ALL_WEIGHTS], *[deltas[n] for n in ALL_WEIGHTS],
            *[new_m[n] for n in ALL_WEIGHTS], *[new_v[n] for n in ALL_WEIGHTS])
```

```python
import jax
import jax.numpy as jnp
from jax import lax
from jax.experimental import pallas as pl
from jax.experimental.pallas import tpu as pltpu

F32 = jnp.float32
MM_DTYPE = jnp.bfloat16
WIRE_DTYPE = jnp.bfloat16
HI = lax.Precision.HIGHEST
EPS = 1e-6
HEAD_DIM = 128
CHUNK = 64
QKV_CONV_WIDTH = 4
SCONV_WIDTH = 3
POOL_GROUPS = 4
LANES = 128
SUBLANES_WIRE = 16
VMEM_LIMIT_BYTES = 56 * 1024 * 1024
ADAM_LR, ADAM_B1, ADAM_B2, ADAM_EPS, ADAM_WD, ADAM_STEP = 0.001, 0.9, 0.999, 1e-08, 0.01, 10
MESH = pl.DeviceIdType.MESH
ANY = pl.BlockSpec(memory_space=pl.ANY)


def _params(*sem):
    return pltpu.CompilerParams(vmem_limit_bytes=VMEM_LIMIT_BYTES, dimension_semantics=sem if sem else None)


def _mm(a, b):
    return jnp.dot(a.astype(MM_DTYPE), b.astype(MM_DTYPE), preferred_element_type=F32)


def _mm_nt(a, b):
    return lax.dot_general(a.astype(MM_DTYPE), b.astype(MM_DTYPE), (((1,), (1,)), ((), ())), preferred_element_type=F32)


def _mm_tn(a, b):
    return lax.dot_general(a.astype(MM_DTYPE), b.astype(MM_DTYPE), (((0,), (0,)), ((), ())), preferred_element_type=F32)


def _hmm(a, b):
    return jnp.dot(a, b, preferred_element_type=F32, precision=HI)


def _hmm_nt(a, b):
    return lax.dot_general(a, b, (((1,), (1,)), ((), ())), preferred_element_type=F32, precision=HI)


def _hmm_tn(a, b):
    return lax.dot_general(a, b, (((0,), (0,)), ((), ())), preferred_element_type=F32, precision=HI)


def _sigmoid(x):
    return 1.0 / (1.0 + jnp.exp(-x))


def _dsilu(x, s):
    return s * (1.0 + x * (1.0 - s))


def _rows(shape):
    return lax.broadcasted_iota(jnp.int32, shape, 0)


def _shift_down(x, s):
    if s == 0:
        return x
    return jnp.where(_rows(x.shape) >= s, pltpu.roll(x, s, 0), 0.0)


def _shift_up(x, s):
    if s == 0:
        return x
    t = x.shape[0]
    return jnp.where(_rows(x.shape) < t - s, pltpu.roll(x, t - s, 0), 0.0)


def _rms_fwd(x):
    r = lax.rsqrt(jnp.mean(x * x, axis=-1, keepdims=True) + EPS)
    return x * r, r


def _rms_bwd(dxn, xn, r):
    return r * (dxn - xn * jnp.mean(dxn * xn, axis=-1, keepdims=True))


def _tile_rows(n, cap, mult=8):
    best = None
    for d in range(mult, min(n, cap) + 1, mult):
        if n % d == 0:
            best = d
    return best if best is not None else n


def _in_proj_fwd(x, g1, wp, segs, tm):
    t, d = x.shape
    npk = wp.shape[1]

    def body(x_ref, g_ref, w_ref, *o_refs):
        xn, _ = _rms_fwd(x_ref[...])
        h = (xn * g_ref[...]).astype(w_ref.dtype)
        off = 0
        for o_ref, wd in zip(o_refs, segs):
            o_ref[...] = jnp.dot(h, w_ref[:, off:off + wd], preferred_element_type=F32)
            off += wd

    return pl.pallas_call(
        body, name="in_proj_fwd", grid=(t // tm,),
        in_specs=[pl.BlockSpec((tm, d), lambda i: (i, 0)), pl.BlockSpec((1, d), lambda i: (0, 0)),
                  pl.BlockSpec((d, npk), lambda i: (0, 0))],
        out_specs=[pl.BlockSpec((tm, wd), lambda i: (i, 0)) for wd in segs],
        out_shape=[jax.ShapeDtypeStruct((t, wd), F32) for wd in segs],
        compiler_params=_params("arbitrary"))(x, g1, wp)


def _in_proj_bwd(x, g1, wp, dsegs, dx_res, segs, tm):
    t, d = x.shape
    npk = wp.shape[1]
    nseg = len(segs)

    def body(x_ref, g_ref, w_ref, *rest):
        ds_refs = rest[:nseg]
        dxr_ref, dx_ref, dw_ref, dg_ref = rest[nseg:]
        i = pl.program_id(0)

        @pl.when(i == 0)
        def _():
            dw_ref[...] = jnp.zeros_like(dw_ref)
            dg_ref[...] = jnp.zeros_like(dg_ref)

        xn, r = _rms_fwd(x_ref[...])
        g = g_ref[...]
        h = (xn * g).astype(w_ref.dtype)
        dh = jnp.zeros((tm, d), F32)
        off = 0
        for ds_ref, wd in zip(ds_refs, segs):
            dsv = ds_ref[...].astype(w_ref.dtype)
            dh = dh + lax.dot_general(dsv, w_ref[:, off:off + wd], (((1,), (1,)), ((), ())), preferred_element_type=F32)
            dw_ref[:, off:off + wd] += lax.dot_general(h, dsv, (((0,), (0,)), ((), ())), preferred_element_type=F32)
            off += wd
        dg_ref[...] += jnp.sum(dh * xn, axis=0, keepdims=True)
        dx_ref[...] = dxr_ref[...] + _rms_bwd(dh * g, xn, r)

    return pl.pallas_call(
        body, name="in_proj_bwd", grid=(t // tm,),
        in_specs=[pl.BlockSpec((tm, d), lambda i: (i, 0)), pl.BlockSpec((1, d), lambda i: (0, 0)),
                  pl.BlockSpec((d, npk), lambda i: (0, 0))]
                 + [pl.BlockSpec((tm, wd), lambda i: (i, 0)) for wd in segs]
                 + [pl.BlockSpec((tm, d), lambda i: (i, 0))],
        out_specs=[pl.BlockSpec((tm, d), lambda i: (i, 0)), pl.BlockSpec((d, npk), lambda i: (0, 0)),
                   pl.BlockSpec((1, d), lambda i: (0, 0))],
        out_shape=[jax.ShapeDtypeStruct((t, d), F32), jax.ShapeDtypeStruct((d, npk), F32),
                   jax.ShapeDtypeStruct((1, d), F32)],
        compiler_params=_params("arbitrary"))(x, g1, wp, *dsegs, dx_res)


def _out_proj_fwd(x0, mix, wo, li, g2, tm):
    t, d = x0.shape
    dq = wo.shape[2]
    widths = [m.shape[1] for m in mix]

    def body(x_ref, *rest):
        m_refs = rest[:len(mix)]
        w_ref, g_ref, x1_ref, h2_ref = rest[len(mix):]
        acc = x_ref[...]
        off = 0
        for m_ref, wd in zip(m_refs, widths):
            for k in range(wd // dq):
                acc = acc + jnp.dot(m_ref[:, k * dq:(k + 1) * dq].astype(w_ref.dtype), w_ref[off // dq + k],
                                    preferred_element_type=F32)
            off += wd
        x1_ref[...] = acc
        xn, _ = _rms_fwd(acc)
        h2_ref[...] = (xn * g_ref[...]).astype(h2_ref.dtype)

    return pl.pallas_call(
        body, name="out_proj_fwd", grid=(t // tm,),
        in_specs=[pl.BlockSpec((tm, d), lambda i: (i, 0))]
                 + [pl.BlockSpec((tm, wd), lambda i: (i, 0)) for wd in widths]
                 + [pl.BlockSpec((4, None, dq, d), lambda i: (0, li, 0, 0)), pl.BlockSpec((1, d), lambda i: (0, 0))],
        out_specs=[pl.BlockSpec((tm, d), lambda i: (i, 0)), pl.BlockSpec((tm, d), lambda i: (i, 0))],
        out_shape=[jax.ShapeDtypeStruct((t, d), F32), jax.ShapeDtypeStruct((t, d), MM_DTYPE)],
        compiler_params=_params("arbitrary"))(x0, *mix, wo, g2)


def _out_proj_bwd(dx2, dh2, x1, g2, mix, wo, li, tm):
    t, d = x1.shape
    dq = wo.shape[2]
    widths = [m.shape[1] for m in mix]
    nm = len(mix)

    def body(dx2_ref, dh2_ref, x1_ref, g_ref, *rest):
        m_refs = rest[:nm]
        w_ref = rest[nm]
        dx1_ref = rest[nm + 1]
        dm_refs = rest[nm + 2:nm + 2 + nm]
        dw_ref, dg_ref = rest[nm + 2 + nm:]
        i = pl.program_id(0)

        @pl.when(i == 0)
        def _():
            dw_ref[...] = jnp.zeros_like(dw_ref)
            dg_ref[...] = jnp.zeros_like(dg_ref)

        xn, r = _rms_fwd(x1_ref[...])
        dh2v = dh2_ref[...]
        dg_ref[...] += jnp.sum(dh2v * xn, axis=0, keepdims=True)
        dx1 = dx2_ref[...] + _rms_bwd(dh2v * g_ref[...], xn, r)
        dx1_ref[...] = dx1
        dx1c = dx1.astype(w_ref.dtype)
        off = 0
        for m_ref, dm_ref, wd in zip(m_refs, dm_refs, widths):
            for k in range(wd // dq):
                j = off // dq + k
                cols = slice(k * dq, (k + 1) * dq)
                dm_ref[:, cols] = lax.dot_general(dx1c, w_ref[j], (((1,), (1,)), ((), ())), preferred_element_type=F32)
                dw_ref[j] += lax.dot_general(m_ref[:, cols].astype(w_ref.dtype), dx1c, (((0,), (0,)), ((), ())),
                                             preferred_element_type=F32)
            off += wd

    tile = lambda wd: pl.BlockSpec((tm, wd), lambda i: (i, 0))
    return pl.pallas_call(
        body, name="out_proj_bwd", grid=(t // tm,),
        in_specs=[tile(d), tile(d), tile(d), pl.BlockSpec((1, d), lambda i: (0, 0))]
                 + [tile(wd) for wd in widths] + [pl.BlockSpec((4, None, dq, d), lambda i: (0, li, 0, 0))],
        out_specs=[tile(d)] + [tile(wd) for wd in widths]
                  + [pl.BlockSpec((4, dq, d), lambda i: (0, 0, 0)), pl.BlockSpec((1, d), lambda i: (0, 0))],
        out_shape=[jax.ShapeDtypeStruct((t, d), F32)] + [jax.ShapeDtypeStruct((t, wd), F32) for wd in widths]
                  + [jax.ShapeDtypeStruct((4, dq, d), F32), jax.ShapeDtypeStruct((1, d), F32)],
        compiler_params=_params("arbitrary"))(dx2, dh2, x1, g2, *mix, wo)


def _ffn_fwd(x1, h2, wg, wu, wd, li, tm):
    t, d = x1.shape
    fs = wg.shape[3]

    def body(x1_ref, h2_ref, wg_ref, wu_ref, wd_ref, x2_ref, gp_ref, up_ref):
        @pl.when(pl.program_id(1) == 0)
        def _():
            x2_ref[...] = x1_ref[...]

        h = h2_ref[...]
        gp = jnp.dot(h, wg_ref[...], preferred_element_type=F32)
        up = jnp.dot(h, wu_ref[...], preferred_element_type=F32)
        gp_ref[...] = gp
        up_ref[...] = up
        ff = gp * _sigmoid(gp) * up
        x2_ref[...] += jnp.dot(ff.astype(wd_ref.dtype), wd_ref[...], preferred_element_type=F32)

    return pl.pallas_call(
        body, name="ffn_fwd", grid=(t // tm, 4),
        in_specs=[pl.BlockSpec((tm, d), lambda i, j: (i, 0)), pl.BlockSpec((tm, d), lambda i, j: (i, 0)),
                  pl.BlockSpec((None, None, d, fs), lambda i, j: (j, li, 0, 0)),
                  pl.BlockSpec((None, None, d, fs), lambda i, j: (j, li, 0, 0)),
                  pl.BlockSpec((None, None, fs, d), lambda i, j: (j, li, 0, 0))],
        out_specs=[pl.BlockSpec((tm, d), lambda i, j: (i, 0)), pl.BlockSpec((None, tm, fs), lambda i, j: (j, i, 0)),
                   pl.BlockSpec((None, tm, fs), lambda i, j: (j, i, 0))],
        out_shape=[jax.ShapeDtypeStruct((t, d), F32), jax.ShapeDtypeStruct((4, t, fs), F32),
                   jax.ShapeDtypeStruct((4, t, fs), F32)],
        compiler_params=_params("arbitrary", "arbitrary"))(x1, h2, wg, wu, wd)


def _ffn_bwd(dx2, h2, gp, up, wg, wu, wd, li, tm):
    t, d = dx2.shape
    fs = wg.shape[3]

    def body(dx2_ref, h2_ref, gp_ref, up_ref, wg_ref, wu_ref, wd_ref, dh2_ref, dwg_ref, dwu_ref, dwd_ref):
        j, i = pl.program_id(0), pl.program_id(1)

        @pl.when(i == 0)
        def _():
            dwg_ref[...] = jnp.zeros_like(dwg_ref)
            dwu_ref[...] = jnp.zeros_like(dwu_ref)
            dwd_ref[...] = jnp.zeros_like(dwd_ref)

        cdt = wg_ref.dtype
        h = h2_ref[...]
        gpv, upv = gp_ref[...], up_ref[...]
        s = _sigmoid(gpv)
        silu = gpv * s
        dx2c = dx2_ref[...].astype(cdt)
        dff = lax.dot_general(dx2c, wd_ref[...], (((1,), (1,)), ((), ())), preferred_element_type=F32)
        dwd_ref[...] += lax.dot_general((silu * upv).astype(cdt), dx2c, (((0,), (0,)), ((), ())), preferred_element_type=F32)
        dup = (dff * silu).astype(cdt)
        dgp = (dff * upv * _dsilu(gpv, s)).astype(cdt)
        dwg_ref[...] += lax.dot_general(h, dgp, (((0,), (0,)), ((), ())), preferred_element_type=F32)
        dwu_ref[...] += lax.dot_general(h, dup, (((0,), (0,)), ((), ())), preferred_element_type=F32)
        dh = (lax.dot_general(dgp, wg_ref[...], (((1,), (1,)), ((), ())), preferred_element_type=F32)
              + lax.dot_general(dup, wu_ref[...], (((1,), (1,)), ((), ())), preferred_element_type=F32))
        rows = pl.ds(pl.multiple_of(i * tm, tm), tm)

        @pl.when(j == 0)
        def _():
            dh2_ref[rows, :] = dh

        @pl.when(j != 0)
        def _():
            dh2_ref[rows, :] += dh

    return pl.pallas_call(
        body, name="ffn_bwd", grid=(4, t // tm),
        in_specs=[pl.BlockSpec((tm, d), lambda j, i: (i, 0)), pl.BlockSpec((tm, d), lambda j, i: (i, 0)),
                  pl.BlockSpec((None, tm, fs), lambda j, i: (j, i, 0)), pl.BlockSpec((None, tm, fs), lambda j, i: (j, i, 0)),
                  pl.BlockSpec((None, None, d, fs), lambda j, i: (j, li, 0, 0)),
                  pl.BlockSpec((None, None, d, fs), lambda j, i: (j, li, 0, 0)),
                  pl.BlockSpec((None, None, fs, d), lambda j, i: (j, li, 0, 0))],
        out_specs=[pl.BlockSpec((t, d), lambda j, i: (0, 0)), pl.BlockSpec((None, d, fs), lambda j, i: (j, 0, 0)),
                   pl.BlockSpec((None, d, fs), lambda j, i: (j, 0, 0)), pl.BlockSpec((None, fs, d), lambda j, i: (j, 0, 0))],
        out_shape=[jax.ShapeDtypeStruct((t, d), F32), jax.ShapeDtypeStruct((4, d, fs), F32),
                   jax.ShapeDtypeStruct((4, d, fs), F32), jax.ShapeDtypeStruct((4, fs, d), F32)],
        compiler_params=_params("arbitrary", "arbitrary"))(dx2, h2, gp, up, wg, wu, wd)


def _ple_fwd(x2, p, wpg, wpp, li, tm):
    t, d = x2.shape
    q = p.shape[1]
    dq = d // 4

    def body(x_ref, p_ref, wg_ref, wp_ref, o_ref):
        xv = x_ref[...]
        xc = xv.astype(wg_ref.dtype)
        pc = p_ref[...].astype(wp_ref.dtype)
        pre = jnp.dot(xc[:, :dq], wg_ref[0], preferred_element_type=F32)
        for j in range(1, 4):
            pre = pre + jnp.dot(xc[:, j * dq:(j + 1) * dq], wg_ref[j], preferred_element_type=F32)
        gate = _sigmoid(pre)
        for j in range(4):
            cols = slice(j * dq, (j + 1) * dq)
            o_ref[:, cols] = xv[:, cols] + gate[:, cols] * jnp.dot(pc, wp_ref[j], preferred_element_type=F32)

    return pl.pallas_call(
        body, name="ple_fwd", grid=(t // tm,),
        in_specs=[pl.BlockSpec((tm, d), lambda i: (i, 0)), pl.BlockSpec((tm, q), lambda i: (i, 0)),
                  pl.BlockSpec((4, None, dq, d), lambda i: (0, li, 0, 0)),
                  pl.BlockSpec((4, None, q, dq), lambda i: (0, li, 0, 0))],
        out_specs=pl.BlockSpec((tm, d), lambda i: (i, 0)),
        out_shape=jax.ShapeDtypeStruct((t, d), F32),
        compiler_params=_params("arbitrary"))(x2, p, wpg, wpp)


def _ple_bwd(dx3, x2, p, wpg, wpp, li, tm):
    t, d = x2.shape
    q = p.shape[1]
    dq = d // 4

    def body(dx3_ref, x_ref, p_ref, wg_ref, wp_ref, dx2_ref, dwg_ref, dwp_ref):
        @pl.when(pl.program_id(0) == 0)
        def _():
            dwg_ref[...] = jnp.zeros_like(dwg_ref)
            dwp_ref[...] = jnp.zeros_like(dwp_ref)

        cdt = wg_ref.dtype
        xc = x_ref[...].astype(cdt)
        pc = p_ref[...].astype(cdt)
        pre = jnp.dot(xc[:, :dq], wg_ref[0], preferred_element_type=F32)
        for j in range(1, 4):
            pre = pre + jnp.dot(xc[:, j * dq:(j + 1) * dq], wg_ref[j], preferred_element_type=F32)
        gate = _sigmoid(pre)
        dx3v = dx3_ref[...]
        dpp = (dx3v * gate).astype(cdt)
        dgate = dx3v * gate * (1.0 - gate)
        dpre_parts = []
        for j in range(4):
            cols = slice(j * dq, (j + 1) * dq)
            pp_j = jnp.dot(pc, wp_ref[j], preferred_element_type=F32)
            dpre_parts.append((dgate[:, cols] * pp_j).astype(cdt))
            dwp_ref[j] += lax.dot_general(pc, dpp[:, cols], (((0,), (0,)), ((), ())), preferred_element_type=F32)
        dpre = jnp.concatenate(dpre_parts, axis=1)
        for j in range(4):
            cols = slice(j * dq, (j + 1) * dq)
            dwg_ref[j] += lax.dot_general(xc[:, cols], dpre, (((0,), (0,)), ((), ())), preferred_element_type=F32)
            dx2_ref[:, cols] = dx3v[:, cols] + lax.dot_general(dpre, wg_ref[j], (((1,), (1,)), ((), ())),
                                                               preferred_element_type=F32)

    return pl.pallas_call(
        body, name="ple_bwd", grid=(t // tm,),
        in_specs=[pl.BlockSpec((tm, d), lambda i: (i, 0)), pl.BlockSpec((tm, d), lambda i: (i, 0)),
                  pl.BlockSpec((tm, q), lambda i: (i, 0)), pl.BlockSpec((4, None, dq, d), lambda i: (0, li, 0, 0)),
                  pl.BlockSpec((4, None, q, dq), lambda i: (0, li, 0, 0))],
        out_specs=[pl.BlockSpec((tm, d), lambda i: (i, 0)), pl.BlockSpec((4, dq, d), lambda i: (0, 0, 0)),
                   pl.BlockSpec((4, q, dq), lambda i: (0, 0, 0))],
        out_shape=[jax.ShapeDtypeStruct((t, d), F32), jax.ShapeDtypeStruct((4, dq, d), F32),
                   jax.ShapeDtypeStruct((4, q, dq), F32)],
        compiler_params=_params("arbitrary"))(dx3, x2, p, wpg, wpp)


def _loss_head(x, target, fg, tm):
    t, d = x.shape

    def body(x_ref, t_ref, g_ref, dx_ref, loss_ref, dg_ref):
        @pl.when(pl.program_id(0) == 0)
        def _():
            loss_ref[...] = jnp.zeros_like(loss_ref)
            dg_ref[...] = jnp.zeros_like(dg_ref)

        xn, r = _rms_fwd(x_ref[...])
        g = g_ref[...]
        err = xn * g - t_ref[...]
        loss_ref[...] += 0.5 * jnp.sum(jnp.sum(err * err, axis=-1, keepdims=True) / d, axis=0, keepdims=True)
        dy = err / d
        dg_ref[...] += jnp.sum(dy * xn, axis=0, keepdims=True)
        dx_ref[...] = _rms_bwd(dy * g, xn, r)

    return pl.pallas_call(
        body, name="loss_head", grid=(t // tm,),
        in_specs=[pl.BlockSpec((tm, d), lambda i: (i, 0)), pl.BlockSpec((tm, d), lambda i: (i, 0)),
                  pl.BlockSpec((1, d), lambda i: (0, 0))],
        out_specs=[pl.BlockSpec((tm, d), lambda i: (i, 0)), pl.BlockSpec((1, 1), lambda i: (0, 0)),
                   pl.BlockSpec((1, d), lambda i: (0, 0))],
        out_shape=[jax.ShapeDtypeStruct((t, d), F32), jax.ShapeDtypeStruct((1, 1), F32),
                   jax.ShapeDtypeStruct((1, d), F32)],
        compiler_params=_params("arbitrary"))(x, target, fg)


def _qkv_conv_act(xv, w, j, heads):
    k = QKV_CONV_WIDTH
    y = w[k - 1:k] * xv
    for s in range(1, k):
        y = y + w[k - 1 - s:k - s] * _shift_down(xv, s)
    sg = _sigmoid(y)
    s_act = y * sg
    nrm = lax.rsqrt(jnp.sum(s_act * s_act, axis=-1, keepdims=True) + EPS)
    scale = jnp.where(j < heads, HEAD_DIM ** -0.5, 1.0).astype(F32)
    return y, sg, s_act, nrm, scale


def _qkv_conv_fwd(qkv_pre, conv_w, heads):
    t = qkv_pre.shape[0]
    nblk = 3 * heads

    def body(x_ref, w_ref, o_ref):
        j = pl.program_id(0)
        _, _, s_act, nrm, scale = _qkv_conv_act(x_ref[...], w_ref[...], j, heads)
        o_ref[...] = jnp.where(j < 2 * heads, s_act * (nrm * scale), s_act)

    return pl.pallas_call(
        body, name="qkv_conv_fwd", grid=(nblk,),
        in_specs=[pl.BlockSpec((t, LANES), lambda j: (0, j)), pl.BlockSpec((QKV_CONV_WIDTH, LANES), lambda j: (0, j))],
        out_specs=pl.BlockSpec((t, LANES), lambda j: (0, j)),
        out_shape=jax.ShapeDtypeStruct(qkv_pre.shape, F32),
        compiler_params=_params("arbitrary"))(qkv_pre, conv_w)


def _qkv_conv_bwd(qkv_pre, conv_w, dqkv, heads):
    t = qkv_pre.shape[0]
    nblk = 3 * heads
    k = QKV_CONV_WIDTH

    def body(x_ref, w_ref, dn_ref, dx_ref, dw_ref):
        j = pl.program_id(0)
        xv, w = x_ref[...], w_ref[...]
        y, sg, s_act, nrm, scale = _qkv_conv_act(xv, w, j, heads)
        dn = dn_ref[...]
        dsn = dn * scale
        ds_qk = nrm * dsn - s_act * (nrm * nrm * nrm) * jnp.sum(dsn * s_act, axis=-1, keepdims=True)
        ds = jnp.where(j < 2 * heads, ds_qk, dn)
        dy = ds * _dsilu(y, sg)
        dx = w[k - 1:k] * dy
        dw_ref[k - 1:k, :] = jnp.sum(dy * xv, axis=0, keepdims=True)
        for s in range(1, k):
            dx = dx + w[k - 1 - s:k - s] * _shift_up(dy, s)
            dw_ref[k - 1 - s:k - s, :] = jnp.sum(dy * _shift_down(xv, s), axis=0, keepdims=True)
        dx_ref[...] = dx

    return pl.pallas_call(
        body, name="qkv_conv_bwd", grid=(nblk,),
        in_specs=[pl.BlockSpec((t, LANES), lambda j: (0, j)), pl.BlockSpec((k, LANES), lambda j: (0, j)),
                  pl.BlockSpec((t, LANES), lambda j: (0, j))],
        out_specs=[pl.BlockSpec((t, LANES), lambda j: (0, j)), pl.BlockSpec((k, LANES), lambda j: (0, j))],
        out_shape=[jax.ShapeDtypeStruct(qkv_pre.shape, F32), jax.ShapeDtypeStruct(conv_w.shape, F32)],
        compiler_params=_params("arbitrary"))(qkv_pre, conv_w, dqkv)


def _pool_windows(shape, j, group_dim):
    lane = lax.broadcasted_iota(jnp.int32, shape, 1) + j * LANES
    grp = lane // group_dim
    win = jnp.left_shift(2, grp).astype(F32)
    cnt = jnp.minimum((_rows(shape) + 1).astype(F32), win)
    return grp, cnt


def _pool_select(grp, levels):
    out = levels[0]
    for gi in range(1, POOL_GROUPS):
        out = jnp.where(grp == gi, levels[gi], out)
    return out


def _pool_mean(hv, grp, cnt):
    acc, levels, width = hv, [], 1
    for _ in range(POOL_GROUPS):
        acc = acc + _shift_down(acc, width)
        width *= 2
        levels.append(acc)
    return _pool_select(grp, levels) / cnt - hv


def _pool_fwd(hp, wbd, scale, group_dim):
    t, dp = hp.shape

    def body(h_ref, w_ref, s_ref, o_ref):
        hv = h_ref[...]
        grp, cnt = _pool_windows(hv.shape, pl.program_id(0), group_dim)
        pooled = _pool_mean(hv, grp, cnt)
        o_ref[...] = _mm(pooled, w_ref[...]) * s_ref[...]

    return pl.pallas_call(
        body, name="pool_fwd", grid=(dp // LANES,),
        in_specs=[pl.BlockSpec((t, LANES), lambda j: (0, j)), pl.BlockSpec((LANES, LANES), lambda j: (j, j)),
                  pl.BlockSpec((1, LANES), lambda j: (0, j))],
        out_specs=pl.BlockSpec((t, LANES), lambda j: (0, j)),
        out_shape=jax.ShapeDtypeStruct(hp.shape, F32),
        compiler_params=_params("arbitrary"))(hp, wbd, scale)


def _pool_bwd(hp, wbd, scale, dob, group_dim):
    t, dp = hp.shape

    def body(h_ref, w_ref, s_ref, do_ref, dh_ref, dw_ref, ds_ref):
        hv = h_ref[...]
        grp, cnt = _pool_windows(hv.shape, pl.program_id(0), group_dim)
        pooled = _pool_mean(hv, grp, cnt)
        wv = w_ref[...]
        dov = do_ref[...]
        ds_ref[...] = jnp.sum(dov * _mm(pooled, wv), axis=0, keepdims=True)
        dys = dov * s_ref[...]
        dw_ref[0] = _mm_tn(pooled, dys)
        dpooled = _mm_nt(dys, wv)
        acc, levels, width = dpooled / cnt, [], 1
        for _ in range(POOL_GROUPS):
            acc = acc + _shift_up(acc, width)
            width *= 2
            levels.append(acc)
        dh_ref[...] = _pool_select(grp, levels) - dpooled

    nb = dp // LANES
    return pl.pallas_call(
        body, name="pool_bwd", grid=(nb,),
        in_specs=[pl.BlockSpec((t, LANES), lambda j: (0, j)), pl.BlockSpec((LANES, LANES), lambda j: (j, j)),
                  pl.BlockSpec((1, LANES), lambda j: (0, j)), pl.BlockSpec((t, LANES), lambda j: (0, j))],
        out_specs=[pl.BlockSpec((t, LANES), lambda j: (0, j)), pl.BlockSpec((1, LANES, LANES), lambda j: (j, 0, 0)),
                   pl.BlockSpec((1, LANES), lambda j: (0, j))],
        out_shape=[jax.ShapeDtypeStruct(hp.shape, F32), jax.ShapeDtypeStruct((nb, LANES, LANES), F32),
                   jax.ShapeDtypeStruct((1, dp), F32)],
        compiler_params=_params("arbitrary"))(hp, wbd, scale, dob)


def _sconv_fwd(cbcch, w):
    t, dc3 = cbcch.shape
    nb = dc3 // 3 // LANES
    k = SCONV_WIDTH

    def body(b_ref, c_ref, h_ref, w_ref, o_ref):
        m = c_ref[...] * h_ref[...]
        wv = w_ref[...]
        y = wv[k - 1:k] * m
        for s in range(1, k):
            y = y + wv[k - 1 - s:k - s] * _shift_down(m, s)
        o_ref[...] = b_ref[...] * y

    return pl.pallas_call(
        body, name="sconv_fwd", grid=(nb,),
        in_specs=[pl.BlockSpec((t, LANES), lambda j: (0, j)), pl.BlockSpec((t, LANES), lambda j: (0, nb + j)),
                  pl.BlockSpec((t, LANES), lambda j: (0, 2 * nb + j)), pl.BlockSpec((k, LANES), lambda j: (0, j))],
        out_specs=pl.BlockSpec((t, LANES), lambda j: (0, j)),
        out_shape=jax.ShapeDtypeStruct((t, dc3 // 3), F32),
        compiler_params=_params("arbitrary"))(cbcch, cbcch, cbcch, w)


def _sconv_bwd(cbcch, w, doc):
    t, dc3 = cbcch.shape
    nb = dc3 // 3 // LANES
    k = SCONV_WIDTH

    def body(b_ref, c_ref, h_ref, w_ref, do_ref, db_ref, dc_ref, dh_ref, dw_ref):
        cv, hv = c_ref[...], h_ref[...]
        m = cv * hv
        wv = w_ref[...]
        dov = do_ref[...]
        dy = dov * b_ref[...]
        y = wv[k - 1:k] * m
        dm = wv[k - 1:k] * dy
        dw_ref[k - 1:k, :] = jnp.sum(dy * m, axis=0, keepdims=True)
        for s in range(1, k):
            ms = _shift_down(m, s)
            y = y + wv[k - 1 - s:k - s] * ms
            dm = dm + wv[k - 1 - s:k - s] * _shift_up(dy, s)
            dw_ref[k - 1 - s:k - s, :] = jnp.sum(dy * ms, axis=0, keepdims=True)
        db_ref[...] = dov * y
        dc_ref[...] = dm * hv
        dh_ref[...] = dm * cv

    col = lambda o: pl.BlockSpec((t, LANES), lambda j: (0, o * nb + j))
    return pl.pallas_call(
        body, name="sconv_bwd", grid=(nb,),
        in_specs=[col(0), col(1), col(2), pl.BlockSpec((k, LANES), lambda j: (0, j)), col(0)],
        out_specs=[col(0), col(0), col(0), pl.BlockSpec((k, LANES), lambda j: (0, j))],
        out_shape=[jax.ShapeDtypeStruct((t, dc3 // 3), F32)] * 3 + [jax.ShapeDtypeStruct(w.shape, F32)],
        compiler_params=_params("arbitrary"))(cbcch, cbcch, cbcch, w, doc)


def _per_head(fn, a, b):
    return jnp.stack([fn(a[h], b[h]) for h in range(a.shape[0])])


def _bmm(a, b):
    return _per_head(_hmm, a, b)


def _bmm_nt(a, b):
    return _per_head(_hmm_nt, a, b)


def _bmm_tn(a, b):
    return _per_head(_hmm_tn, a, b)


def _inv_unit_lower(low):
    c = low.shape[-1]
    eye = (_rows((c, c)) == lax.broadcasted_iota(jnp.int32, (c, c), 1)).astype(F32)
    pw = -low
    inv = eye + pw
    span = 2
    while span < c:
        pw = _bmm(pw, pw)
        inv = inv + _bmm(inv, pw)
        span *= 2
    return inv


def _heads_of(ref, base, heads):
    return jnp.stack([ref[:, base + h * HEAD_DIM:base + (h + 1) * HEAD_DIM] for h in range(heads)])


def _chunk_common(q, k, v, a_col, b_col, alog, dtb):
    hn, c, _ = q.shape
    beta = _sigmoid(b_col)
    xg = a_col + dtb
    softplus = jnp.maximum(xg, 0.0) + jnp.log(1.0 + jnp.exp(-jnp.abs(xg)))
    neg_ea = -jnp.exp(alog)
    g = neg_ea * softplus
    ri = _rows((c, c))
    ci = lax.broadcasted_iota(jnp.int32, (c, c), 1)
    incl, strict = ri >= ci, ri > ci
    inclf = jnp.broadcast_to(incl.astype(F32), (hn, c, c))
    gcb = _bmm(inclf, jnp.broadcast_to(g, (hn, c, HEAD_DIM)))
    gc_row = jnp.sum(jnp.where(ri <= ci, jnp.broadcast_to(g, (hn, c, c)), 0.0), axis=1, keepdims=True)
    dmat = jnp.where(incl, jnp.exp(jnp.where(incl, gcb[:, :, :1] - gc_row, 0.0)), 0.0)
    eg = jnp.exp(gcb)
    gl = gcb[:, c - 1:c, :]
    egl = jnp.exp(gl)
    edl = jnp.exp(gl - gcb)
    kb, vb = k * beta, v * beta
    a0 = _bmm_nt(kb, k)
    tm = _inv_unit_lower(jnp.where(strict, a0 * dmat, 0.0))
    kbe = kb * eg
    p0 = _bmm_nt(q, k)
    return dict(beta=beta, xg=xg, neg_ea=neg_ea, g=g, incl=incl, strict=strict, inclf=inclf, dmat=dmat, eg=eg,
                egl=egl, edl=edl, kb=kb, vb=vb, a0=a0, tm=tm, kbe=kbe, u=_bmm(tm, vb), w=_bmm(tm, kbe), p0=p0,
                attn=p0 * dmat, qe=q * eg, kd=k * edl)


def _chunk_step(cm, state):
    vn = cm["u"] - _bmm(cm["w"], state)
    o = _bmm(cm["qe"], state) + _bmm(cm["attn"], vn)
    new_state = state * cm["egl"][:, :, :1] + _bmm_tn(cm["kd"], vn)
    return vn, o, new_state


def _gated_norm(o, zv, og):
    xo, ro = _rms_fwd(o)
    sgz = _sigmoid(zv)
    return xo, ro, sgz, xo * og * (zv * sgz)


def _gate_columns(abv, gpv, heads):
    a_col = jnp.stack([abv[:, h:h + 1] for h in range(heads)])
    b_col = jnp.stack([abv[:, heads + h:heads + h + 1] for h in range(heads)])
    alog = jnp.stack([gpv[0:1, h:h + 1] for h in range(heads)])
    dtb = jnp.stack([gpv[1:2, h:h + 1] for h in range(heads)])
    return a_col, b_col, alog, dtb


def _delta_fwd(qkv, z, ab, gpar, heads):
    t = qkv.shape[0]
    da = heads * HEAD_DIM
    n = t // CHUNK

    def body(qkv_ref, z_ref, ab_ref, gp_ref, oa_ref, st_ref, s_ref):
        @pl.when(pl.program_id(0) == 0)
        def _():
            s_ref[...] = jnp.zeros_like(s_ref)

        gpv = gp_ref[...]
        cm = _chunk_common(_heads_of(qkv_ref, 0, heads), _heads_of(qkv_ref, da, heads), _heads_of(qkv_ref, 2 * da, heads),
                           *_gate_columns(ab_ref[...], gpv, heads))
        state = s_ref[...]
        st_ref[0] = state
        _, o, new_state = _chunk_step(cm, state)
        s_ref[...] = new_state
        oa = _gated_norm(o, _heads_of(z_ref, 0, heads), gpv[2:3, :])[3]
        for h in range(heads):
            oa_ref[:, h * HEAD_DIM:(h + 1) * HEAD_DIM] = oa[h]

    return pl.pallas_call(
        body, name="delta_fwd", grid=(n,),
        in_specs=[pl.BlockSpec((CHUNK, 3 * da), lambda i: (i, 0)), pl.BlockSpec((CHUNK, da), lambda i: (i, 0)),
                  pl.BlockSpec((CHUNK, LANES), lambda i: (i, 0)), pl.BlockSpec((8, LANES), lambda i: (0, 0))],
        out_specs=[pl.BlockSpec((CHUNK, da), lambda i: (i, 0)),
                   pl.BlockSpec((1, heads, HEAD_DIM, HEAD_DIM), lambda i: (i, 0, 0, 0))],
        out_shape=[jax.ShapeDtypeStruct((t, da), F32), jax.ShapeDtypeStruct((n, heads, HEAD_DIM, HEAD_DIM), F32)],
        scratch_shapes=[pltpu.VMEM((heads, HEAD_DIM, HEAD_DIM), F32)],
        compiler_params=_params("arbitrary"))(qkv, z, ab, gpar)


def _delta_bwd(qkv, z, ab, gpar, states, doa, heads):
    t = qkv.shape[0]
    da = heads * HEAD_DIM
    n = t // CHUNK
    c = CHUNK

    def body(qkv_ref, z_ref, ab_ref, gp_ref, st_ref, doa_ref, dqkv_ref, dz_ref, dab_ref, dpar_ref, ds_ref):
        @pl.when(pl.program_id(0) == 0)
        def _():
            ds_ref[...] = jnp.zeros_like(ds_ref)
            dpar_ref[...] = jnp.zeros_like(dpar_ref)

        gpv = gp_ref[...]
        og = gpv[2:3, :]
        q, k, v = _heads_of(qkv_ref, 0, heads), _heads_of(qkv_ref, da, heads), _heads_of(qkv_ref, 2 * da, heads)
        cm = _chunk_common(q, k, v, *_gate_columns(ab_ref[...], gpv, heads))
        state = st_ref[0]
        dsp = ds_ref[...]
        vn, o, _ = _chunk_step(cm, state)
        zv = _heads_of(z_ref, 0, heads)
        xo, ro, sgz, _ = _gated_norm(o, zv, og)
        doav = _heads_of(doa_ref, 0, heads)
        don = doav * (zv * sgz)
        dz = doav * (xo * og) * _dsilu(zv, sgz)
        d_og = jnp.sum(jnp.sum(don * xo, axis=1, keepdims=True), axis=0)
        do = _rms_bwd(don * og, xo, ro)
        tm, dmat, eg, edl, egl = cm["tm"], cm["dmat"], cm["eg"], cm["edl"], cm["egl"]
        dvn = _bmm_tn(cm["attn"], do) + _bmm(cm["kd"], dsp)
        dqe = _bmm_nt(do, state)
        ds_ref[...] = _bmm_tn(cm["qe"], do) + dsp * egl[:, :, :1] - _bmm_tn(cm["w"], dvn)
        dattn = _bmm_nt(do, vn)
        dkd = _bmm_nt(vn, dsp)
        dkd_kd = jnp.sum(dkd * cm["kd"], axis=-1, keepdims=True)
        dgl = (jnp.sum(jnp.sum(dsp * state, axis=-1, keepdims=True), axis=1, keepdims=True) * egl[:, :, :1]
               + jnp.sum(dkd_kd, axis=1, keepdims=True))
        dgc = jnp.sum(dqe * cm["qe"], axis=-1, keepdims=True) - dkd_kd
        dk = dkd * edl
        dq = dqe * eg
        dw = -_bmm_nt(dvn, state)
        dp0 = dattn * dmat
        dd = jnp.where(cm["incl"], dattn * cm["p0"], 0.0)
        dq = dq + _bmm(dp0, k)
        dk = dk + _bmm_tn(dp0, q)
        dtm = _bmm_nt(dvn, cm["vb"]) + _bmm_nt(dw, cm["kbe"])
        dvb = _bmm_tn(tm, dvn)
        dkbe = _bmm_tn(tm, dw)
        dkb = dkbe * eg
        dgc = dgc + jnp.sum(dkbe * cm["kbe"], axis=-1, keepdims=True)
        dlow = jnp.where(cm["strict"], -_bmm_tn(tm, _bmm_nt(dtm, tm)), 0.0)
        dd = dd + dlow * cm["a0"]
        da0 = dlow * dmat
        dkb = dkb + _bmm(da0, k)
        dk = dk + _bmm_tn(da0, cm["kb"])
        ddd = dd * dmat
        ones = jnp.ones((heads, c, HEAD_DIM), F32)
        dgc = dgc + jnp.sum(ddd, axis=-1, keepdims=True) - _bmm_tn(ddd, ones)[:, :, :1]
        dgc = dgc + jnp.where(_rows((c, 1)) == c - 1, dgl, 0.0)
        dg = _bmm_tn(cm["inclf"], jnp.broadcast_to(dgc, (heads, c, HEAD_DIM)))[:, :, :1]
        beta = cm["beta"]
        dk = dk + dkb * beta
        dbeta = jnp.sum(dkb * k, axis=-1, keepdims=True) + jnp.sum(dvb * v, axis=-1, keepdims=True)
        dv = dvb * beta
        db_col = dbeta * beta * (1.0 - beta)
        da_col = dg * cm["neg_ea"] * _sigmoid(cm["xg"])
        d_alog = jnp.sum(dg * cm["g"], axis=1, keepdims=True)
        d_dtb = jnp.sum(da_col, axis=1, keepdims=True)
        lane = lax.broadcasted_iota(jnp.int32, (c, LANES), 1)
        lane8 = lax.broadcasted_iota(jnp.int32, (8, LANES), 1)
        row8 = _rows((8, LANES))
        dab = jnp.zeros((c, LANES), F32)
        dpar = jnp.where(row8 == 2, d_og, 0.0)
        for h in range(heads):
            lo = h * HEAD_DIM
            dqkv_ref[:, lo:lo + HEAD_DIM] = dq[h]
            dqkv_ref[:, da + lo:da + lo + HEAD_DIM] = dk[h]
            dqkv_ref[:, 2 * da + lo:2 * da + lo + HEAD_DIM] = dv[h]
            dz_ref[:, lo:lo + HEAD_DIM] = dz[h]
            dab = dab + jnp.where(lane == h, da_col[h], 0.0) + jnp.where(lane == heads + h, db_col[h], 0.0)
            dpar = (dpar + jnp.where((row8 == 0) & (lane8 == h), d_alog[h], 0.0)
                    + jnp.where((row8 == 1) & (lane8 == h), d_dtb[h], 0.0))
        dab_ref[...] = dab
        dpar_ref[...] += dpar

    rev = lambda i: (n - 1 - i, 0)
    return pl.pallas_call(
        body, name="delta_bwd", grid=(n,),
        in_specs=[pl.BlockSpec((c, 3 * da), rev), pl.BlockSpec((c, da), rev), pl.BlockSpec((c, LANES), rev),
                  pl.BlockSpec((8, LANES), lambda i: (0, 0)),
                  pl.BlockSpec((1, heads, HEAD_DIM, HEAD_DIM), lambda i: (n - 1 - i, 0, 0, 0)),
                  pl.BlockSpec((c, da), rev)],
        out_specs=[pl.BlockSpec((c, 3 * da), rev), pl.BlockSpec((c, da), rev), pl.BlockSpec((c, LANES), rev),
                   pl.BlockSpec((8, LANES), lambda i: (0, 0))],
        out_shape=[jax.ShapeDtypeStruct((t, 3 * da), F32), jax.ShapeDtypeStruct((t, da), F32),
                   jax.ShapeDtypeStruct((t, LANES), F32), jax.ShapeDtypeStruct((8, LANES), F32)],
        scratch_shapes=[pltpu.VMEM((heads, HEAD_DIM, HEAD_DIM), F32)],
        compiler_params=_params("arbitrary"))(qkv, z, ab, gpar, states, doa)


def _w_in_pieces(shard_cols, da, heads):
    a0, nab = 4 * da, 2 * heads
    d_in = 4 * shard_cols
    runs = [(0, a0, 0), (a0, a0 + nab, d_in - nab), (a0 + nab, d_in, a0)]
    pieces = []
    for j in range(4):
        lo, hi = j * shard_cols, (j + 1) * shard_cols
        for rlo, rhi, plo in runs:
            s, e = max(lo, rlo), min(hi, rhi)
            if s < e:
                pieces.append((j, s - lo, e - s, plo + (s - rlo)))
    return pieces, d_in - nab + LANES


def _w_in_pack(w4, li, da, heads):
    _, _, d, sc = w4.shape
    pieces, npk = _w_in_pieces(sc, da, heads)
    tr = _tile_rows(d, 256, SUBLANES_WIRE)

    def body(w_ref, o_ref):
        o_ref[:, npk - LANES:] = jnp.zeros((tr, LANES), o_ref.dtype)
        for j, lo, ln, dst in pieces:
            o_ref[:, dst:dst + ln] = w_ref[j, :, lo:lo + ln]

    return pl.pallas_call(
        body, name="w_in_pack", grid=(d // tr,),
        in_specs=[pl.BlockSpec((4, None, tr, sc), lambda i: (0, li, i, 0))],
        out_specs=pl.BlockSpec((tr, npk), lambda i: (i, 0)),
        out_shape=jax.ShapeDtypeStruct((d, npk), w4.dtype),
        compiler_params=_params("arbitrary"))(w4)


def _w_in_unpack(dwp, sc, da, heads):
    d, npk = dwp.shape
    pieces, _ = _w_in_pieces(sc, da, heads)
    tr = _tile_rows(d, 256)

    def body(g_ref, o_ref):
        for j, lo, ln, dst in pieces:
            o_ref[j, :, lo:lo + ln] = g_ref[:, dst:dst + ln]

    return pl.pallas_call(
        body, name="w_in_unpack", grid=(d // tr,),
        in_specs=[pl.BlockSpec((tr, npk), lambda i: (i, 0))],
        out_specs=pl.BlockSpec((4, tr, sc), lambda i: (0, i, 0)),
        out_shape=jax.ShapeDtypeStruct((4, d, sc), F32),
        compiler_params=_params("arbitrary"))(dwp)


def _block_diag(pool_w):
    g, gd, _ = pool_w.shape
    out = jnp.zeros((g * gd, g * gd), pool_w.dtype)
    for gi in range(g):
        out = lax.dynamic_update_slice(out, pool_w[gi], (gi * gd, gi * gd))
    return out


def _layer_dims(d):
    heads = (d // 2) // HEAD_DIM
    return heads, heads * HEAD_DIM, d // 4, d // 4


BIG = ("w_in", "w_gate", "w_up", "ple_proj", "w_out", "w_down", "ple_gate")


def _prepare_layer(gw, small, li):
    d = small["norm1_g"].shape[1]
    heads, da, _, _ = _layer_dims(d)
    gpar = jnp.zeros((8, LANES), F32)
    gpar = gpar.at[0, :heads].set(small["a_log"][li]).at[1, :heads].set(small["dt_bias"][li]).at[2, :].set(small["onorm_g"][li])
    return dict(norm1_g=small["norm1_g"][li][None], w_in_p=_w_in_pack(gw["w_in"], li, da, heads).astype(MM_DTYPE),
                conv_qkv=small["conv_qkv"][li], gpar=gpar, pool_bd=_block_diag(small["pool_w"][li]).astype(MM_DTYPE),
                pool_scale=small["pool_scale"][li][None], sconv_w=small["sconv_w"][li], norm2_g=small["norm2_g"][li][None])


def _layer_fwd(x0, p, gw, lw, li, tm):
    d = x0.shape[1]
    heads, da, dp, dc = _layer_dims(d)
    segs = (3 * da, da, dp, 3 * dc, LANES)
    qkv_pre, z, hp, cbcch, ab = _in_proj_fwd(x0, lw["norm1_g"], lw["w_in_p"], segs, tm)
    qkv = _qkv_conv_fwd(qkv_pre, lw["conv_qkv"], heads)
    oa, states = _delta_fwd(qkv, z, ab, lw["gpar"], heads)
    ob = _pool_fwd(hp, lw["pool_bd"], lw["pool_scale"], dp // POOL_GROUPS)
    oc = _sconv_fwd(cbcch, lw["sconv_w"])
    x1, h2 = _out_proj_fwd(x0, (oa, ob, oc), gw["w_out"], li, lw["norm2_g"], tm)
    x2, gp, up = _ffn_fwd(x1, h2, gw["w_gate"], gw["w_up"], gw["w_down"], li, tm)
    x3 = _ple_fwd(x2, p, gw["ple_gate"], gw["ple_proj"], li, tm)
    saved = dict(x0=x0, qkv_pre=qkv_pre, z=z, hp=hp, cbcch=cbcch, ab=ab, qkv=qkv, states=states, oa=oa, ob=ob, oc=oc,
                 x1=x1, h2=h2, gp=gp, up=up, x2=x2)
    return x3, saved


def _layer_bwd(dx3, p, gw, lw, li, sv, tm):
    d = dx3.shape[1]
    heads, da, dp, dc = _layer_dims(d)
    segs = (3 * da, da, dp, dc, dc, dc, LANES)
    gd = dp // POOL_GROUPS
    dx2, d_ple_gate, d_ple_proj = _ple_bwd(dx3, sv["x2"], p, gw["ple_gate"], gw["ple_proj"], li, tm)
    dh2, d_w_gate, d_w_up, d_w_down = _ffn_bwd(dx2, sv["h2"], sv["gp"], sv["up"], gw["w_gate"], gw["w_up"], gw["w_down"],
                                               li, min(tm, 256))
    dx1, doa, dob, doc, d_w_out, d_norm2 = _out_proj_bwd(dx2, dh2, sv["x1"], lw["norm2_g"],
                                                         (sv["oa"], sv["ob"], sv["oc"]), gw["w_out"], li, tm)
    dcb, dcc, dch, d_sconv = _sconv_bwd(sv["cbcch"], lw["sconv_w"], doc)
    dhp, d_pool_bd, d_pool_scale = _pool_bwd(sv["hp"], lw["pool_bd"], lw["pool_scale"], dob, gd)
    dqkv, dz, dab, dpar = _delta_bwd(sv["qkv"], sv["z"], sv["ab"], lw["gpar"], sv["states"], doa, heads)
    dqkv_pre, d_conv_qkv = _qkv_conv_bwd(sv["qkv_pre"], lw["conv_qkv"], dqkv, heads)
    dsegs = (dqkv_pre, dz, dhp, dcb, dcc, dch, dab)
    dx0, d_w_in_p, d_norm1 = _in_proj_bwd(sv["x0"], lw["norm1_g"], lw["w_in_p"], dsegs, dx1, segs, tm)
    per = LANES // gd
    bd = d_pool_bd.reshape(dp // LANES, per, gd, per, gd)
    d_pool_w = jnp.stack([bd[gi // per, gi % per, :, gi % per, :] for gi in range(POOL_GROUPS)])
    big = dict(w_in=_w_in_unpack(d_w_in_p, gw["w_in"].shape[3], da, heads), w_gate=d_w_gate, w_up=d_w_up,
               ple_proj=d_ple_proj, w_out=d_w_out, w_down=d_w_down, ple_gate=d_ple_gate)
    small = dict(norm1_g=d_norm1[0], conv_qkv=d_conv_qkv, a_log=dpar[0, :heads], dt_bias=dpar[1, :heads], onorm_g=dpar[2],
                 pool_w=d_pool_w, pool_scale=d_pool_scale[0], sconv_w=d_sconv, norm2_g=d_norm2[0])
    return dx0, big, small


def _local_step(x, p, target, gw, small):
    t, d = x.shape
    depth = p.shape[0]
    tm = 512 if t % 512 == 0 else 128
    layers = [_prepare_layer(gw, small, li) for li in range(depth)]
    saved = []
    h = x
    for li in range(depth):
        h, sv = _layer_fwd(h, p[li], gw, layers[li], li, tm)
        saved.append(sv)
    dx, loss, d_final = _loss_head(h, target, small["final_g"][None], tm)
    big, sm = [None] * depth, [None] * depth
    for li in reversed(range(depth)):
        dx, big[li], sm[li] = _layer_bwd(dx, p[li], gw, layers[li], li, saved[li], tm)
    small_grads = {n: jnp.stack([g[n] for g in sm]) for n in sm[0]}
    small_grads["final_g"] = d_final[0]
    return loss[0, 0], dx, big, small_grads


def _coords():
    return lax.axis_index("x"), lax.axis_index("y"), lax.axis_index("c")


def _other_chips(x, y):
    return [(1 - x, y), (x, 1 - y), (1 - x, 1 - y)]


def _place_shard(w, me_idx):
    depth, rows, cols = w.shape
    tr = _tile_rows(rows, 512, SUBLANES_WIRE)

    def body(me_ref, w_ref, o_ref):
        o_ref[...] = w_ref[...].astype(o_ref.dtype)

    return pl.pallas_call(
        body, name="place_shard",
        grid_spec=pltpu.PrefetchScalarGridSpec(
            num_scalar_prefetch=1, grid=(depth, rows // tr),
            in_specs=[pl.BlockSpec((None, tr, cols), lambda l, i, me_ref: (l, i, 0))],
            out_specs=pl.BlockSpec((None, None, tr, cols), lambda l, i, me_ref: (me_ref[0], l, i, 0))),
        out_shape=jax.ShapeDtypeStruct((4, depth, rows, cols), WIRE_DTYPE),
        compiler_params=_params("arbitrary", "arbitrary"))(me_idx, w)


def _all_gather_chips(placed):
    nt = len(placed)

    def body(*refs):
        out_refs = refs[nt:2 * nt]
        send_sems, recv_sems = refs[2 * nt:]
        x, y, c = _coords()
        sibling = (x, y, 1 - c)
        chips = _other_chips(x, y)

        def copy(k, t, block, to):
            px, py, pc = block
            blk = out_refs[t].at[2 * px + py, pc]
            return pltpu.make_async_remote_copy(src_ref=blk, dst_ref=blk, send_sem=send_sems.at[k, t],
                                                recv_sem=recv_sems.at[k, t], device_id=to, device_id_type=MESH)

        first = [copy(j, t, (x, y, c), (*chip, c)) for j, chip in enumerate(chips) for t in range(nt)]
        for cp in first:
            cp.start()
        passed = []
        for j, chip in enumerate(chips):
            for t in range(nt):
                copy(j, t, (*chip, c), (x, y, c)).wait_recv()
                fwd = copy(3 + j, t, (*chip, c), sibling)
                fwd.start()
                passed.append(fwd)
        for j, chip in enumerate(chips):
            for t in range(nt):
                copy(3 + j, t, (*chip, 1 - c), (x, y, c)).wait_recv()
        for cp in first + passed:
            cp.wait_send()

    return pl.pallas_call(
        body, name="all_gather_chips", out_shape=[jax.ShapeDtypeStruct(a.shape, a.dtype) for a in placed],
        in_specs=[ANY] * nt, out_specs=[ANY] * nt, input_output_aliases={t: t for t in range(nt)},
        scratch_shapes=[pltpu.SemaphoreType.DMA((6, nt)), pltpu.SemaphoreType.DMA((6, nt))],
    )(*placed)


def _sibling_swap_half(gs):
    nt = len(gs)

    def body(*refs):
        g_refs, out_refs = refs[:nt], refs[nt:2 * nt]
        send_sems, recv_sems = refs[2 * nt:]
        x, y, c = _coords()
        cps = []
        for t in range(nt):
            rh = g_refs[t].shape[1] // 2
            cps.append(pltpu.make_async_remote_copy(src_ref=g_refs[t].at[:, pl.ds((1 - c) * rh, rh)], dst_ref=out_refs[t],
                                                    send_sem=send_sems.at[t], recv_sem=recv_sems.at[t], device_id=(x, y, 1 - c),
                                                    device_id_type=MESH))
        for cp in cps:
            cp.start()
        for cp in cps:
            cp.wait()

    return pl.pallas_call(
        body, name="sibling_swap_half",
        out_shape=[jax.ShapeDtypeStruct((g.shape[0], g.shape[1] // 2, g.shape[2]), g.dtype) for g in gs],
        in_specs=[ANY] * nt, out_specs=[ANY] * nt,
        scratch_shapes=[pltpu.SemaphoreType.DMA((nt,)), pltpu.SemaphoreType.DMA((nt,))])(*gs)


def _add_my_half(g, other, c_idx):
    n, rows, cols = g.shape
    rh = rows // 2
    tr = _tile_rows(rh, 512, SUBLANES_WIRE)
    nb = rh // tr

    def body(c_ref, g_ref, o_ref, out_ref):
        out_ref[...] = (g_ref[...].astype(F32) + o_ref[...].astype(F32)).astype(out_ref.dtype)

    return pl.pallas_call(
        body, name="add_my_half",
        grid_spec=pltpu.PrefetchScalarGridSpec(
            num_scalar_prefetch=1, grid=(n, nb),
            in_specs=[pl.BlockSpec((None, tr, cols), lambda j, i, c_ref: (j, c_ref[0] * nb + i, 0)),
                      pl.BlockSpec((None, tr, cols), lambda j, i, c_ref: (j, i, 0))],
            out_specs=pl.BlockSpec((None, tr, cols), lambda j, i, c_ref: (j, i, 0))),
        out_shape=jax.ShapeDtypeStruct((n, rh, cols), WIRE_DTYPE),
        compiler_params=_params("arbitrary", "arbitrary"))(c_idx, g, other)


def _exchange_chips(parts):
    nt = len(parts)

    def body(*refs):
        p_refs, out_refs = refs[:nt], refs[nt:2 * nt]
        send_sems, recv_sems = refs[2 * nt:]
        x, y, c = _coords()
        chips = _other_chips(x, y)

        def copy(j, t):
            cx, cy = chips[j]
            return pltpu.make_async_remote_copy(src_ref=p_refs[t].at[2 * cx + cy], dst_ref=out_refs[t].at[j],
                                                send_sem=send_sems.at[j, t], recv_sem=recv_sems.at[j, t], device_id=(cx, cy, c),
                                                device_id_type=MESH)

        sends = [copy(j, t) for j in range(3) for t in range(nt)]
        for cp in sends:
            cp.start()
        for cp in sends:
            cp.wait_recv()
        for cp in sends:
            cp.wait_send()

    return pl.pallas_call(
        body, name="exchange_chips", out_shape=[jax.ShapeDtypeStruct((3,) + p.shape[1:], p.dtype) for p in parts],
        in_specs=[ANY] * nt, out_specs=[ANY] * nt,
        scratch_shapes=[pltpu.SemaphoreType.DMA((3, nt)), pltpu.SemaphoreType.DMA((3, nt))])(*parts)


def _sum_into(pair, recv, idx, li, depth, acc):
    _, rh, cols = pair.shape
    tr = _tile_rows(rh, 512, SUBLANES_WIRE)
    nb = rh // tr

    def body(idx_ref, p_ref, r_ref, *rest):
        out_ref = rest[-1]
        out_ref[...] = p_ref[...].astype(F32) + r_ref[0].astype(F32) + r_ref[1].astype(F32) + r_ref[2].astype(F32)

    in_specs = [pl.BlockSpec((None, tr, cols), lambda i, idx_ref: (idx_ref[0], i, 0)),
                pl.BlockSpec((3, tr, cols), lambda i, idx_ref: (0, i, 0))]
    args = [idx, pair, recv]
    aliases = {}
    if acc is not None:
        in_specs.append(ANY)
        args.append(acc)
        aliases = {3: 0}
    return pl.pallas_call(
        body, name="sum_into",
        grid_spec=pltpu.PrefetchScalarGridSpec(
            num_scalar_prefetch=1, grid=(nb,), in_specs=in_specs,
            out_specs=pl.BlockSpec((None, tr, cols), lambda i, idx_ref: (li, idx_ref[1] * nb + i, 0))),
        out_shape=jax.ShapeDtypeStruct((depth, 2 * rh, cols), F32), input_output_aliases=aliases,
        compiler_params=_params("arbitrary"))(*args)


def _sum_slots(parts):
    n, rows, cols = parts.shape
    tr = _tile_rows(rows, 512, SUBLANES_WIRE)

    def body(p_ref, out_ref):
        acc = p_ref[0].astype(F32)
        for s in range(1, n):
            acc = acc + p_ref[s].astype(F32)
        out_ref[...] = acc

    return pl.pallas_call(
        body, name="sum_slots", grid=(rows // tr,),
        in_specs=[pl.BlockSpec((n, tr, cols), lambda i: (0, i, 0))],
        out_specs=pl.BlockSpec((tr, cols), lambda i: (i, 0)),
        out_shape=jax.ShapeDtypeStruct((rows, cols), F32),
        compiler_params=_params("arbitrary"))(parts)


def _sibling_share(gs):
    nt = len(gs)
    depth = gs[0].shape[0]

    def body(*refs):
        out_refs = refs[nt:2 * nt]
        send_sems, recv_sems = refs[2 * nt:]
        x, y, c = _coords()
        sends, recvs = [], []
        for t in range(nt):
            rh = out_refs[t].shape[1] // 2
            for li in range(depth):
                mine = out_refs[t].at[li, pl.ds(c * rh, rh)]
                theirs = out_refs[t].at[li, pl.ds((1 - c) * rh, rh)]
                sems = dict(send_sem=send_sems.at[t, li], recv_sem=recv_sems.at[t, li], device_id=(x, y, 1 - c), device_id_type=MESH)
                sends.append(pltpu.make_async_remote_copy(src_ref=mine, dst_ref=mine, **sems))
                recvs.append(pltpu.make_async_remote_copy(src_ref=theirs, dst_ref=theirs, **sems))
        for cp in sends:
            cp.start()
        for cp in recvs:
            cp.wait_recv()
        for cp in sends:
            cp.wait_send()

    return pl.pallas_call(
        body, name="sibling_share", out_shape=[jax.ShapeDtypeStruct(g.shape, g.dtype) for g in gs],
        in_specs=[ANY] * nt, out_specs=[ANY] * nt, input_output_aliases={t: t for t in range(nt)},
        scratch_shapes=[pltpu.SemaphoreType.DMA((nt, depth)), pltpu.SemaphoreType.DMA((nt, depth))])(*gs)


def _all_gather_devices(buf):
    def body(b_ref, out_ref, send_sems, recv_sems, local_sem):
        x, y, c = _coords()
        me = 4 * x + 2 * y + c
        mine = pltpu.make_async_copy(b_ref, out_ref.at[me], local_sem)
        mine.start()
        peers = []
        for k in range(1, 8):
            fx, fy, fc = (k >> 2) & 1, (k >> 1) & 1, k & 1
            peers.append((x ^ fx, y ^ fy, c ^ fc))
        sends = [pltpu.make_async_remote_copy(src_ref=b_ref, dst_ref=out_ref.at[me], send_sem=send_sems.at[k],
                                              recv_sem=recv_sems.at[k], device_id=peer, device_id_type=MESH)
                 for k, peer in enumerate(peers)]
        for cp in sends:
            cp.start()
        for k, (px, py, pc) in enumerate(peers):
            pltpu.make_async_remote_copy(src_ref=b_ref, dst_ref=out_ref.at[4 * px + 2 * py + pc], send_sem=send_sems.at[k],
                                         recv_sem=recv_sems.at[k], device_id=(px, py, pc), device_id_type=MESH).wait_recv()
        for cp in sends:
            cp.wait_send()
        mine.wait()

    return pl.pallas_call(
        body, name="all_gather_devices", out_shape=jax.ShapeDtypeStruct((8,) + buf.shape, buf.dtype),
        in_specs=[ANY], out_specs=ANY,
        scratch_shapes=[pltpu.SemaphoreType.DMA((7,)), pltpu.SemaphoreType.DMA((7,)), pltpu.SemaphoreType.DMA(())])(buf)


def _reduce_layer(big_grads, li, depth, accs, c_idx, idx):
    gs = [big_grads[n] for n in BIG]
    others = _sibling_swap_half(gs)
    pairs = [_add_my_half(g, o, c_idx) for g, o in zip(gs, others)]
    recvs = _exchange_chips(pairs)
    return [_sum_into(pr, rc, idx, li, depth, acc) for pr, rc, acc in zip(pairs, recvs, accs)]


SMALL_SHARDED = ("conv_qkv", "sconv_w")
REPLICATED = ("norm1_g", "a_log", "dt_bias", "onorm_g", "pool_w", "pool_scale", "norm2_g", "final_g")
ALL_WEIGHTS = ("norm1_g", "w_in", "conv_qkv", "a_log", "dt_bias", "onorm_g", "pool_w", "pool_scale", "sconv_w", "w_out",
               "norm2_g", "w_gate", "w_up", "w_down", "ple_proj", "ple_gate", "final_g")


def _pad_rows(flat, row_multiple):
    m = flat.shape[0]
    r = -(-m // (LANES * row_multiple)) * row_multiple
    return jnp.pad(flat, (0, r * LANES - m)).reshape(r, LANES)


def _adamw(w, g, m, v):
    shape = w.shape
    cols = shape[-1]
    rows = w.size // cols
    tr = _tile_rows(rows, 512)
    c1 = 1.0 / (1.0 - ADAM_B1 ** ADAM_STEP)
    c2 = 1.0 / (1.0 - ADAM_B2 ** ADAM_STEP)

    def body(w_ref, g_ref, m_ref, v_ref, d_ref, nm_ref, nv_ref, go_ref):
        gv = g_ref[...]
        nm = ADAM_B1 * m_ref[...] + (1.0 - ADAM_B1) * gv
        nv = ADAM_B2 * v_ref[...] + (1.0 - ADAM_B2) * (gv * gv)
        nm_ref[...] = nm
        nv_ref[...] = nv
        go_ref[...] = gv
        d_ref[...] = -ADAM_LR * ((nm * c1) / (jnp.sqrt(nv * c2) + ADAM_EPS) + ADAM_WD * w_ref[...])

    spec = pl.BlockSpec((tr, cols), lambda i: (i, 0))
    outs = pl.pallas_call(
        body, name="adamw", grid=(rows // tr,), in_specs=[spec] * 4, out_specs=[spec] * 4,
        out_shape=[jax.ShapeDtypeStruct((rows, cols), F32)] * 4,
        compiler_params=_params("arbitrary"))(*[a.reshape(rows, cols) for a in (w, g, m, v)])
    return tuple(o.reshape(shape) for o in outs)


def kernel(x, p, norm1_g, w_in, conv_qkv, a_log, dt_bias, onorm_g, pool_w, pool_scale, sconv_w, w_out, norm2_g, w_gate, w_up, w_down, ple_proj, ple_gate, final_g, loss_target, m_norm1_g, m_w_in, m_conv_qkv, m_a_log, m_dt_bias, m_onorm_g, m_pool_w, m_pool_scale, m_sconv_w, m_w_out, m_norm2_g, m_w_gate, m_w_up, m_w_down, m_ple_proj, m_ple_gate, m_final_g, v_norm1_g, v_w_in, v_conv_qkv, v_a_log, v_dt_bias, v_onorm_g, v_pool_w, v_pool_scale, v_sconv_w, v_w_out, v_norm2_g, v_w_gate, v_w_up, v_w_down, v_ple_proj, v_ple_gate, v_final_g):
    weights = dict(zip(ALL_WEIGHTS, (norm1_g, w_in, conv_qkv, a_log, dt_bias, onorm_g, pool_w, pool_scale, sconv_w, w_out,
                                     norm2_g, w_gate, w_up, w_down, ple_proj, ple_gate, final_g)))
    mom_m = dict(zip(ALL_WEIGHTS, (m_norm1_g, m_w_in, m_conv_qkv, m_a_log, m_dt_bias, m_onorm_g, m_pool_w, m_pool_scale,
                                   m_sconv_w, m_w_out, m_norm2_g, m_w_gate, m_w_up, m_w_down, m_ple_proj, m_ple_gate, m_final_g)))
    mom_v = dict(zip(ALL_WEIGHTS, (v_norm1_g, v_w_in, v_conv_qkv, v_a_log, v_dt_bias, v_onorm_g, v_pool_w, v_pool_scale,
                                   v_sconv_w, v_w_out, v_norm2_g, v_w_gate, v_w_up, v_w_down, v_ple_proj, v_ple_gate, v_final_g)))
    c_idx = lax.axis_index("c").astype(jnp.int32).reshape(1)
    chip = (2 * lax.axis_index("x") + lax.axis_index("y")).astype(jnp.int32)
    me_idx = chip.reshape(1)
    idx = jnp.stack([chip, lax.axis_index("c").astype(jnp.int32)])
    depth = p.shape[0]

    gathered = _all_gather_chips([_place_shard(weights[n], me_idx) for n in BIG])
    gw = dict(zip(BIG, gathered))
    small = {n: weights[n] for n in REPLICATED}
    sflat = _pad_rows(jnp.concatenate([weights[n].reshape(-1) for n in SMALL_SHARDED]), 8)
    sgath = _all_gather_devices(sflat)[0::2].reshape(4, -1)
    off = 0
    for n in SMALL_SHARDED:
        shp = weights[n].shape
        part = sgath[:, off:off + weights[n].size].reshape((4,) + shp)
        small[n] = jnp.moveaxis(part, 0, -2).reshape(shp[:-1] + (4 * shp[-1],))
        off += weights[n].size

    loss_local, dx, big_grads, small_grads = _local_step(x[0], p[:, 0], loss_target[0], gw, small)

    accs = [None] * len(BIG)
    for li in reversed(range(depth)):
        accs = _reduce_layer(big_grads[li], li, depth, accs, c_idx, idx)
    gshard = dict(zip(BIG, _sibling_share(accs)))

    rnames = REPLICATED + SMALL_SHARDED
    rflat = _pad_rows(jnp.concatenate([small_grads[n].reshape(-1) for n in rnames]), 8)
    rsum = _sum_slots(_all_gather_devices(rflat)).reshape(-1)
    off = 0
    for n in rnames:
        whole = rsum[off:off + small_grads[n].size].reshape(small_grads[n].shape)
        off += small_grads[n].size
        if n in SMALL_SHARDED:
            cols = weights[n].shape[-1]
            whole = lax.dynamic_slice_in_dim(whole, chip * cols, cols, axis=whole.ndim - 1)
        gshard[n] = whole

    loss = lax.psum(loss_local, ("x", "y", "c"))
    deltas, new_m, new_v, grad_out = {}, {}, {}, {}
    for n in ALL_WEIGHTS:
        deltas[n], new_m[n], new_v[n], grad_out[n] = _adamw(weights[n], gshard[n], mom_m[n], mom_v[n])
    return (loss, dx[None], *[grad_out[n] for n in ALL_WEIGHTS], *[deltas[n] for n in ALL_WEIGHTS],
            *[new_m[n] for n in ALL_WEIGHTS], *[new_v[n] for n in ALL_WEIGHTS])
```

```python
import jax
import jax.numpy as jnp
from jax import lax
from jax.experimental import pallas as pl
from jax.experimental.pallas import tpu as pltpu

F32 = jnp.float32
MM_DTYPE = jnp.bfloat16
WIRE_DTYPE = jnp.bfloat16
HI = lax.Precision.HIGHEST
EPS = 1e-6
HEAD_DIM = 128
CHUNK = 64
QKV_CONV_WIDTH = 4
SCONV_WIDTH = 3
POOL_GROUPS = 4
LANES = 128
SUBLANES_WIRE = 16
VMEM_LIMIT_BYTES = 56 * 1024 * 1024
ADAM_LR, ADAM_B1, ADAM_B2, ADAM_EPS, ADAM_WD, ADAM_STEP = 0.001, 0.9, 0.999, 1e-08, 0.01, 10
MESH = pl.DeviceIdType.MESH
ANY = pl.BlockSpec(memory_space=pl.ANY)


def _params(*sem):
    return pltpu.CompilerParams(vmem_limit_bytes=VMEM_LIMIT_BYTES, dimension_semantics=sem if sem else None)


def _mm(a, b):
    return jnp.dot(a.astype(MM_DTYPE), b.astype(MM_DTYPE), preferred_element_type=F32)


def _mm_nt(a, b):
    return lax.dot_general(a.astype(MM_DTYPE), b.astype(MM_DTYPE), (((1,), (1,)), ((), ())), preferred_element_type=F32)


def _mm_tn(a, b):
    return lax.dot_general(a.astype(MM_DTYPE), b.astype(MM_DTYPE), (((0,), (0,)), ((), ())), preferred_element_type=F32)


def _hmm(a, b):
    return jnp.dot(a, b, preferred_element_type=F32, precision=HI)


def _hmm_nt(a, b):
    return lax.dot_general(a, b, (((1,), (1,)), ((), ())), preferred_element_type=F32, precision=HI)


def _hmm_tn(a, b):
    return lax.dot_general(a, b, (((0,), (0,)), ((), ())), preferred_element_type=F32, precision=HI)


def _sigmoid(x):
    return 1.0 / (1.0 + jnp.exp(-x))


def _dsilu(x, s):
    return s * (1.0 + x * (1.0 - s))


def _rows(shape):
    return lax.broadcasted_iota(jnp.int32, shape, 0)


def _shift_down(x, s):
    if s == 0:
        return x
    return jnp.where(_rows(x.shape) >= s, pltpu.roll(x, s, 0), 0.0)


def _shift_up(x, s):
    if s == 0:
        return x
    t = x.shape[0]
    return jnp.where(_rows(x.shape) < t - s, pltpu.roll(x, t - s, 0), 0.0)


def _rms_fwd(x):
    r = lax.rsqrt(jnp.mean(x * x, axis=-1, keepdims=True) + EPS)
    return x * r, r


def _rms_bwd(dxn, xn, r):
    return r * (dxn - xn * jnp.mean(dxn * xn, axis=-1, keepdims=True))


def _tile_rows(n, cap, mult=8):
    best = None
    for d in range(mult, min(n, cap) + 1, mult):
        if n % d == 0:
            best = d
    return best if best is not None else n


def _in_proj_fwd(x, g1, wp, segs, tm):
    t, d = x.shape
    npk = wp.shape[1]

    def body(x_ref, g_ref, w_ref, *o_refs):
        xn, _ = _rms_fwd(x_ref[...])
        h = (xn * g_ref[...]).astype(w_ref.dtype)
        off = 0
        for o_ref, wd in zip(o_refs, segs):
            o_ref[...] = jnp.dot(h, w_ref[:, off:off + wd], preferred_element_type=F32)
            off += wd

    return pl.pallas_call(
        body, name="in_proj_fwd", grid=(t // tm,),
        in_specs=[pl.BlockSpec((tm, d), lambda i: (i, 0)), pl.BlockSpec((1, d), lambda i: (0, 0)),
                  pl.BlockSpec((d, npk), lambda i: (0, 0))],
        out_specs=[pl.BlockSpec((tm, wd), lambda i: (i, 0)) for wd in segs],
        out_shape=[jax.ShapeDtypeStruct((t, wd), F32) for wd in segs],
        compiler_params=_params("arbitrary"))(x, g1, wp)


def _in_proj_bwd(x, g1, wp, dsegs, dx_res, segs, tm):
    t, d = x.shape
    npk = wp.shape[1]
    nseg = len(segs)

    def body(x_ref, g_ref, w_ref, *rest):
        ds_refs = rest[:nseg]
        dxr_ref, dx_ref, dw_ref, dg_ref = rest[nseg:]
        i = pl.program_id(0)

        @pl.when(i == 0)
        def _():
            dw_ref[...] = jnp.zeros_like(dw_ref)
            dg_ref[...] = jnp.zeros_like(dg_ref)

        xn, r = _rms_fwd(x_ref[...])
        g = g_ref[...]
        h = (xn * g).astype(w_ref.dtype)
        dh = jnp.zeros((tm, d), F32)
        off = 0
        for ds_ref, wd in zip(ds_refs, segs):
            dsv = ds_ref[...].astype(w_ref.dtype)
            dh = dh + lax.dot_general(dsv, w_ref[:, off:off + wd], (((1,), (1,)), ((), ())), preferred_element_type=F32)
            dw_ref[:, off:off + wd] += lax.dot_general(h, dsv, (((0,), (0,)), ((), ())), preferred_element_type=F32)
            off += wd
        dg_ref[...] += jnp.sum(dh * xn, axis=0, keepdims=True)
        dx_ref[...] = dxr_ref[...] + _rms_bwd(dh * g, xn, r)

    return pl.pallas_call(
        body, name="in_proj_bwd", grid=(t // tm,),
        in_specs=[pl.BlockSpec((tm, d), lambda i: (i, 0)), pl.BlockSpec((1, d), lambda i: (0, 0)),
                  pl.BlockSpec((d, npk), lambda i: (0, 0))]
                 + [pl.BlockSpec((tm, wd), lambda i: (i, 0)) for wd in segs]
                 + [pl.BlockSpec((tm, d), lambda i: (i, 0))],
        out_specs=[pl.BlockSpec((tm, d), lambda i: (i, 0)), pl.BlockSpec((d, npk), lambda i: (0, 0)),
                   pl.BlockSpec((1, d), lambda i: (0, 0))],
        out_shape=[jax.ShapeDtypeStruct((t, d), F32), jax.ShapeDtypeStruct((d, npk), F32),
                   jax.ShapeDtypeStruct((1, d), F32)],
        compiler_params=_params("arbitrary"))(x, g1, wp, *dsegs, dx_res)


def _out_proj_fwd(x0, mix, wo, li, g2, tm):
    t, d = x0.shape
    dq = wo.shape[2]
    widths = [m.shape[1] for m in mix]

    def body(x_ref, *rest):
        m_refs = rest[:len(mix)]
        w_ref, g_ref, x1_ref, h2_ref = rest[len(mix):]
        acc = x_ref[...]
        off = 0
        for m_ref, wd in zip(m_refs, widths):
            for k in range(wd // dq):
                acc = acc + jnp.dot(m_ref[:, k * dq:(k + 1) * dq].astype(w_ref.dtype), w_ref[off // dq + k],
                                    preferred_element_type=F32)
            off += wd
        x1_ref[...] = acc
        xn, _ = _rms_fwd(acc)
        h2_ref[...] = (xn * g_ref[...]).astype(h2_ref.dtype)

    return pl.pallas_call(
        body, name="out_proj_fwd", grid=(t // tm,),
        in_specs=[pl.BlockSpec((tm, d), lambda i: (i, 0))]
                 + [pl.BlockSpec((tm, wd), lambda i: (i, 0)) for wd in widths]
                 + [pl.BlockSpec((4, None, dq, d), lambda i: (0, li, 0, 0)), pl.BlockSpec((1, d), lambda i: (0, 0))],
        out_specs=[pl.BlockSpec((tm, d), lambda i: (i, 0)), pl.BlockSpec((tm, d), lambda i: (i, 0))],
        out_shape=[jax.ShapeDtypeStruct((t, d), F32), jax.ShapeDtypeStruct((t, d), MM_DTYPE)],
        compiler_params=_params("arbitrary"))(x0, *mix, wo, g2)


def _out_proj_bwd(dx2, dh2, x1, g2, mix, wo, li, tm):
    t, d = x1.shape
    dq = wo.shape[2]
    widths = [m.shape[1] for m in mix]
    nm = len(mix)

    def body(dx2_ref, dh2_ref, x1_ref, g_ref, *rest):
        m_refs = rest[:nm]
        w_ref = rest[nm]
        dx1_ref = rest[nm + 1]
        dm_refs = rest[nm + 2:nm + 2 + nm]
        dw_ref, dg_ref = rest[nm + 2 + nm:]
        i = pl.program_id(0)

        @pl.when(i == 0)
        def _():
            dw_ref[...] = jnp.zeros_like(dw_ref)
            dg_ref[...] = jnp.zeros_like(dg_ref)

        xn, r = _rms_fwd(x1_ref[...])
        dh2v = dh2_ref[...]
        dg_ref[...] += jnp.sum(dh2v * xn, axis=0, keepdims=True)
        dx1 = dx2_ref[...] + _rms_bwd(dh2v * g_ref[...], xn, r)
        dx1_ref[...] = dx1
        dx1c = dx1.astype(w_ref.dtype)
        off = 0
        for m_ref, dm_ref, wd in zip(m_refs, dm_refs, widths):
            for k in range(wd // dq):
                j = off // dq + k
                cols = slice(k * dq, (k + 1) * dq)
                dm_ref[:, cols] = lax.dot_general(dx1c, w_ref[j], (((1,), (1,)), ((), ())), preferred_element_type=F32)
                dw_ref[j] += lax.dot_general(m_ref[:, cols].astype(w_ref.dtype), dx1c, (((0,), (0,)), ((), ())),
                                             preferred_element_type=F32)
            off += wd

    tile = lambda wd: pl.BlockSpec((tm, wd), lambda i: (i, 0))
    return pl.pallas_call(
        body, name="out_proj_bwd", grid=(t // tm,),
        in_specs=[tile(d), tile(d), tile(d), pl.BlockSpec((1, d), lambda i: (0, 0))]
                 + [tile(wd) for wd in widths] + [pl.BlockSpec((4, None, dq, d), lambda i: (0, li, 0, 0))],
        out_specs=[tile(d)] + [tile(wd) for wd in widths]
                  + [pl.BlockSpec((4, dq, d), lambda i: (0, 0, 0)), pl.BlockSpec((1, d), lambda i: (0, 0))],
        out_shape=[jax.ShapeDtypeStruct((t, d), F32)] + [jax.ShapeDtypeStruct((t, wd), F32) for wd in widths]
                  + [jax.ShapeDtypeStruct((4, dq, d), F32), jax.ShapeDtypeStruct((1, d), F32)],
        compiler_params=_params("arbitrary"))(dx2, dh2, x1, g2, *mix, wo)


def _ffn_fwd(x1, h2, wg, wu, wd, li, tm):
    t, d = x1.shape
    fs = wg.shape[3]

    def body(x1_ref, h2_ref, wg_ref, wu_ref, wd_ref, x2_ref, gp_ref, up_ref):
        @pl.when(pl.program_id(1) == 0)
        def _():
            x2_ref[...] = x1_ref[...]

        h = h2_ref[...]
        gp = jnp.dot(h, wg_ref[...], preferred_element_type=F32)
        up = jnp.dot(h, wu_ref[...], preferred_element_type=F32)
        gp_ref[...] = gp
        up_ref[...] = up
        ff = gp * _sigmoid(gp) * up
        x2_ref[...] += jnp.dot(ff.astype(wd_ref.dtype), wd_ref[...], preferred_element_type=F32)

    return pl.pallas_call(
        body, name="ffn_fwd", grid=(t // tm, 4),
        in_specs=[pl.BlockSpec((tm, d), lambda i, j: (i, 0)), pl.BlockSpec((tm, d), lambda i, j: (i, 0)),
                  pl.BlockSpec((None, None, d, fs), lambda i, j: (j, li, 0, 0)),
                  pl.BlockSpec((None, None, d, fs), lambda i, j: (j, li, 0, 0)),
                  pl.BlockSpec((None, None, fs, d), lambda i, j: (j, li, 0, 0))],
        out_specs=[pl.BlockSpec((tm, d), lambda i, j: (i, 0)), pl.BlockSpec((None, tm, fs), lambda i, j: (j, i, 0)),
                   pl.BlockSpec((None, tm, fs), lambda i, j: (j, i, 0))],
        out_shape=[jax.ShapeDtypeStruct((t, d), F32), jax.ShapeDtypeStruct((4, t, fs), F32),
                   jax.ShapeDtypeStruct((4, t, fs), F32)],
        compiler_params=_params("arbitrary", "arbitrary"))(x1, h2, wg, wu, wd)


def _ffn_bwd(dx2, h2, gp, up, wg, wu, wd, li, tm):
    t, d = dx2.shape
    fs = wg.shape[3]

    def body(dx2_ref, h2_ref, gp_ref, up_ref, wg_ref, wu_ref, wd_ref, dh2_ref, dwg_ref, dwu_ref, dwd_ref):
        j, i = pl.program_id(0), pl.program_id(1)

        @pl.when(i == 0)
        def _():
            dwg_ref[...] = jnp.zeros_like(dwg_ref)
            dwu_ref[...] = jnp.zeros_like(dwu_ref)
            dwd_ref[...] = jnp.zeros_like(dwd_ref)

        cdt = wg_ref.dtype
        h = h2_ref[...]
        gpv, upv = gp_ref[...], up_ref[...]
        s = _sigmoid(gpv)
        silu = gpv * s
        dx2c = dx2_ref[...].astype(cdt)
        dff = lax.dot_general(dx2c, wd_ref[...], (((1,), (1,)), ((), ())), preferred_element_type=F32)
        dwd_ref[...] += lax.dot_general((silu * upv).astype(cdt), dx2c, (((0,), (0,)), ((), ())), preferred_element_type=F32)
        dup = (dff * silu).astype(cdt)
        dgp = (dff * upv * _dsilu(gpv, s)).astype(cdt)
        dwg_ref[...] += lax.dot_general(h, dgp, (((0,), (0,)), ((), ())), preferred_element_type=F32)
        dwu_ref[...] += lax.dot_general(h, dup, (((0,), (0,)), ((), ())), preferred_element_type=F32)
        dh = (lax.dot_general(dgp, wg_ref[...], (((1,), (1,)), ((), ())), preferred_element_type=F32)
              + lax.dot_general(dup, wu_ref[...], (((1,), (1,)), ((), ())), preferred_element_type=F32))
        rows = pl.ds(pl.multiple_of(i * tm, tm), tm)

        @pl.when(j == 0)
        def _():
            dh2_ref[rows, :] = dh

        @pl.when(j != 0)
        def _():
            dh2_ref[rows, :] += dh

    return pl.pallas_call(
        body, name="ffn_bwd", grid=(4, t // tm),
        in_specs=[pl.BlockSpec((tm, d), lambda j, i: (i, 0)), pl.BlockSpec((tm, d), lambda j, i: (i, 0)),
                  pl.BlockSpec((None, tm, fs), lambda j, i: (j, i, 0)), pl.BlockSpec((None, tm, fs), lambda j, i: (j, i, 0)),
                  pl.BlockSpec((None, None, d, fs), lambda j, i: (j, li, 0, 0)),
                  pl.BlockSpec((None, None, d, fs), lambda j, i: (j, li, 0, 0)),
                  pl.BlockSpec((None, None, fs, d), lambda j, i: (j, li, 0, 0))],
        out_specs=[pl.BlockSpec((t, d), lambda j, i: (0, 0)), pl.BlockSpec((None, d, fs), lambda j, i: (j, 0, 0)),
                   pl.BlockSpec((None, d, fs), lambda j, i: (j, 0, 0)), pl.BlockSpec((None, fs, d), lambda j, i: (j, 0, 0))],
        out_shape=[jax.ShapeDtypeStruct((t, d), F32), jax.ShapeDtypeStruct((4, d, fs), F32),
                   jax.ShapeDtypeStruct((4, d, fs), F32), jax.ShapeDtypeStruct((4, fs, d), F32)],
        compiler_params=_params("arbitrary", "arbitrary"))(dx2, h2, gp, up, wg, wu, wd)


def _ple_fwd(x2, p, wpg, wpp, li, tm):
    t, d = x2.shape
    q = p.shape[1]
    dq = d // 4

    def body(x_ref, p_ref, wg_ref, wp_ref, o_ref):
        xv = x_ref[...]
        xc = xv.astype(wg_ref.dtype)
        pc = p_ref[...].astype(wp_ref.dtype)
        pre = jnp.dot(xc[:, :dq], wg_ref[0], preferred_element_type=F32)
        for j in range(1, 4):
            pre = pre + jnp.dot(xc[:, j * dq:(j + 1) * dq], wg_ref[j], preferred_element_type=F32)
        gate = _sigmoid(pre)
        for j in range(4):
            cols = slice(j * dq, (j + 1) * dq)
            o_ref[:, cols] = xv[:, cols] + gate[:, cols] * jnp.dot(pc, wp_ref[j], preferred_element_type=F32)

    return pl.pallas_call(
        body, name="ple_fwd", grid=(t // tm,),
        in_specs=[pl.BlockSpec((tm, d), lambda i: (i, 0)), pl.BlockSpec((tm, q), lambda i: (i, 0)),
                  pl.BlockSpec((4, None, dq, d), lambda i: (0, li, 0, 0)),
                  pl.BlockSpec((4, None, q, dq), lambda i: (0, li, 0, 0))],
        out_specs=pl.BlockSpec((tm, d), lambda i: (i, 0)),
        out_shape=jax.ShapeDtypeStruct((t, d), F32),
        compiler_params=_params("arbitrary"))(x2, p, wpg, wpp)


def _ple_bwd(dx3, x2, p, wpg, wpp, li, tm):
    t, d = x2.shape
    q = p.shape[1]
    dq = d // 4

    def body(dx3_ref, x_ref, p_ref, wg_ref, wp_ref, dx2_ref, dwg_ref, dwp_ref):
        @pl.when(pl.program_id(0) == 0)
        def _():
            dwg_ref[...] = jnp.zeros_like(dwg_ref)
            dwp_ref[...] = jnp.zeros_like(dwp_ref)

        cdt = wg_ref.dtype
        xc = x_ref[...].astype(cdt)
        pc = p_ref[...].astype(cdt)
        pre = jnp.dot(xc[:, :dq], wg_ref[0], preferred_element_type=F32)
        for j in range(1, 4):
            pre = pre + jnp.dot(xc[:, j * dq:(j + 1) * dq], wg_ref[j], preferred_element_type=F32)
        gate = _sigmoid(pre)
        dx3v = dx3_ref[...]
        dpp = (dx3v * gate).astype(cdt)
        dgate = dx3v * gate * (1.0 - gate)
        dpre_parts = []
        for j in range(4):
            cols = slice(j * dq, (j + 1) * dq)
            pp_j = jnp.dot(pc, wp_ref[j], preferred_element_type=F32)
            dpre_parts.append((dgate[:, cols] * pp_j).astype(cdt))
            dwp_ref[j] += lax.dot_general(pc, dpp[:, cols], (((0,), (0,)), ((), ())), preferred_element_type=F32)
        dpre = jnp.concatenate(dpre_parts, axis=1)
        for j in range(4):
            cols = slice(j * dq, (j + 1) * dq)
            dwg_ref[j] += lax.dot_general(xc[:, cols], dpre, (((0,), (0,)), ((), ())), preferred_element_type=F32)
            dx2_ref[:, cols] = dx3v[:, cols] + lax.dot_general(dpre, wg_ref[j], (((1,), (1,)), ((), ())),
                                                               preferred_element_type=F32)

    return pl.pallas_call(
        body, name="ple_bwd", grid=(t // tm,),
        in_specs=[pl.BlockSpec((tm, d), lambda i: (i, 0)), pl.BlockSpec((tm, d), lambda i: (i, 0)),
                  pl.BlockSpec((tm, q), lambda i: (i, 0)), pl.BlockSpec((4, None, dq, d), lambda i: (0, li, 0, 0)),
                  pl.BlockSpec((4, None, q, dq), lambda i: (0, li, 0, 0))],
        out_specs=[pl.BlockSpec((tm, d), lambda i: (i, 0)), pl.BlockSpec((4, dq, d), lambda i: (0, 0, 0)),
                   pl.BlockSpec((4, q, dq), lambda i: (0, 0, 0))],
        out_shape=[jax.ShapeDtypeStruct((t, d), F32), jax.ShapeDtypeStruct((4, dq, d), F32),
                   jax.ShapeDtypeStruct((4, q, dq), F32)],
        compiler_params=_params("arbitrary"))(dx3, x2, p, wpg, wpp)


def _loss_head(x, target, fg, tm):
    t, d = x.shape

    def body(x_ref, t_ref, g_ref, dx_ref, loss_ref, dg_ref):
        @pl.when(pl.program_id(0) == 0)
        def _():
            loss_ref[...] = jnp.zeros_like(loss_ref)
            dg_ref[...] = jnp.zeros_like(dg_ref)

        xn, r = _rms_fwd(x_ref[...])
        g = g_ref[...]
        err = xn * g - t_ref[...]
        loss_ref[...] += 0.5 * jnp.sum(jnp.sum(err * err, axis=-1, keepdims=True) / d, axis=0, keepdims=True)
        dy = err / d
        dg_ref[...] += jnp.sum(dy * xn, axis=0, keepdims=True)
        dx_ref[...] = _rms_bwd(dy * g, xn, r)

    return pl.pallas_call(
        body, name="loss_head", grid=(t // tm,),
        in_specs=[pl.BlockSpec((tm, d), lambda i: (i, 0)), pl.BlockSpec((tm, d), lambda i: (i, 0)),
                  pl.BlockSpec((1, d), lambda i: (0, 0))],
        out_specs=[pl.BlockSpec((tm, d), lambda i: (i, 0)), pl.BlockSpec((1, 1), lambda i: (0, 0)),
                   pl.BlockSpec((1, d), lambda i: (0, 0))],
        out_shape=[jax.ShapeDtypeStruct((t, d), F32), jax.ShapeDtypeStruct((1, 1), F32),
                   jax.ShapeDtypeStruct((1, d), F32)],
        compiler_params=_params("arbitrary"))(x, target, fg)


def _qkv_conv_act(xv, w, j, heads):
    k = QKV_CONV_WIDTH
    y = w[k - 1:k] * xv
    for s in range(1, k):
        y = y + w[k - 1 - s:k - s] * _shift_down(xv, s)
    sg = _sigmoid(y)
    s_act = y * sg
    nrm = lax.rsqrt(jnp.sum(s_act * s_act, axis=-1, keepdims=True) + EPS)
    scale = jnp.where(j < heads, HEAD_DIM ** -0.5, 1.0).astype(F32)
    return y, sg, s_act, nrm, scale


def _qkv_conv_fwd(qkv_pre, conv_w, heads):
    t = qkv_pre.shape[0]
    nblk = 3 * heads

    def body(x_ref, w_ref, o_ref):
        j = pl.program_id(0)
        _, _, s_act, nrm, scale = _qkv_conv_act(x_ref[...], w_ref[...], j, heads)
        o_ref[...] = jnp.where(j < 2 * heads, s_act * (nrm * scale), s_act)

    return pl.pallas_call(
        body, name="qkv_conv_fwd", grid=(nblk,),
        in_specs=[pl.BlockSpec((t, LANES), lambda j: (0, j)), pl.BlockSpec((QKV_CONV_WIDTH, LANES), lambda j: (0, j))],
        out_specs=pl.BlockSpec((t, LANES), lambda j: (0, j)),
        out_shape=jax.ShapeDtypeStruct(qkv_pre.shape, F32),
        compiler_params=_params("arbitrary"))(qkv_pre, conv_w)


def _qkv_conv_bwd(qkv_pre, conv_w, dqkv, heads):
    t = qkv_pre.shape[0]
    nblk = 3 * heads
    k = QKV_CONV_WIDTH

    def body(x_ref, w_ref, dn_ref, dx_ref, dw_ref):
        j = pl.program_id(0)
        xv, w = x_ref[...], w_ref[...]
        y, sg, s_act, nrm, scale = _qkv_conv_act(xv, w, j, heads)
        dn = dn_ref[...]
        dsn = dn * scale
        ds_qk = nrm * dsn - s_act * (nrm * nrm * nrm) * jnp.sum(dsn * s_act, axis=-1, keepdims=True)
        ds = jnp.where(j < 2 * heads, ds_qk, dn)
        dy = ds * _dsilu(y, sg)
        dx = w[k - 1:k] * dy
        dw_ref[k - 1:k, :] = jnp.sum(dy * xv, axis=0, keepdims=True)
        for s in range(1, k):
            dx = dx + w[k - 1 - s:k - s] * _shift_up(dy, s)
            dw_ref[k - 1 - s:k - s, :] = jnp.sum(dy * _shift_down(xv, s), axis=0, keepdims=True)
        dx_ref[...] = dx

    return pl.pallas_call(
        body, name="qkv_conv_bwd", grid=(nblk,),
        in_specs=[pl.BlockSpec((t, LANES), lambda j: (0, j)), pl.BlockSpec((k, LANES), lambda j: (0, j)),
                  pl.BlockSpec((t, LANES), lambda j: (0, j))],
        out_specs=[pl.BlockSpec((t, LANES), lambda j: (0, j)), pl.BlockSpec((k, LANES), lambda j: (0, j))],
        out_shape=[jax.ShapeDtypeStruct(qkv_pre.shape, F32), jax.ShapeDtypeStruct(conv_w.shape, F32)],
        compiler_params=_params("arbitrary"))(qkv_pre, conv_w, dqkv)


def _pool_windows(shape, j, group_dim):
    lane = lax.broadcasted_iota(jnp.int32, shape, 1) + j * LANES
    grp = lane // group_dim
    win = jnp.left_shift(2, grp).astype(F32)
    cnt = jnp.minimum((_rows(shape) + 1).astype(F32), win)
    return grp, cnt


def _pool_select(grp, levels):
    out = levels[0]
    for gi in range(1, POOL_GROUPS):
        out = jnp.where(grp == gi, levels[gi], out)
    return out


def _pool_mean(hv, grp, cnt):
    acc, levels, width = hv, [], 1
    for _ in range(POOL_GROUPS):
        acc = acc + _shift_down(acc, width)
        width *= 2
        levels.append(acc)
    return _pool_select(grp, levels) / cnt - hv


def _pool_fwd(hp, wbd, scale, group_dim):
    t, dp = hp.shape

    def body(h_ref, w_ref, s_ref, o_ref):
        hv = h_ref[...]
        grp, cnt = _pool_windows(hv.shape, pl.program_id(0), group_dim)
        pooled = _pool_mean(hv, grp, cnt)
        o_ref[...] = _mm(pooled, w_ref[...]) * s_ref[...]

    return pl.pallas_call(
        body, name="pool_fwd", grid=(dp // LANES,),
        in_specs=[pl.BlockSpec((t, LANES), lambda j: (0, j)), pl.BlockSpec((LANES, LANES), lambda j: (j, j)),
                  pl.BlockSpec((1, LANES), lambda j: (0, j))],
        out_specs=pl.BlockSpec((t, LANES), lambda j: (0, j)),
        out_shape=jax.ShapeDtypeStruct(hp.shape, F32),
        compiler_params=_params("arbitrary"))(hp, wbd, scale)


def _pool_bwd(hp, wbd, scale, dob, group_dim):
    t, dp = hp.shape

    def body(h_ref, w_ref, s_ref, do_ref, dh_ref, dw_ref, ds_ref):
        hv = h_ref[...]
        grp, cnt = _pool_windows(hv.shape, pl.program_id(0), group_dim)
        pooled = _pool_mean(hv, grp, cnt)
        wv = w_ref[...]
        dov = do_ref[...]
        ds_ref[...] = jnp.sum(dov * _mm(pooled, wv), axis=0, keepdims=True)
        dys = dov * s_ref[...]
        dw_ref[0] = _mm_tn(pooled, dys)
        dpooled = _mm_nt(dys, wv)
        acc, levels, width = dpooled / cnt, [], 1
        for _ in range(POOL_GROUPS):
            acc = acc + _shift_up(acc, width)
            width *= 2
            levels.append(acc)
        dh_ref[...] = _pool_select(grp, levels) - dpooled

    nb = dp // LANES
    return pl.pallas_call(
        body, name="pool_bwd", grid=(nb,),
        in_specs=[pl.BlockSpec((t, LANES), lambda j: (0, j)), pl.BlockSpec((LANES, LANES), lambda j: (j, j)),
                  pl.BlockSpec((1, LANES), lambda j: (0, j)), pl.BlockSpec((t, LANES), lambda j: (0, j))],
        out_specs=[pl.BlockSpec((t, LANES), lambda j: (0, j)), pl.BlockSpec((1, LANES, LANES), lambda j: (j, 0, 0)),
                   pl.BlockSpec((1, LANES), lambda j: (0, j))],
        out_shape=[jax.ShapeDtypeStruct(hp.shape, F32), jax.ShapeDtypeStruct((nb, LANES, LANES), F32),
                   jax.ShapeDtypeStruct((1, dp), F32)],
        compiler_params=_params("arbitrary"))(hp, wbd, scale, dob)


def _sconv_fwd(cbcch, w):
    t, dc3 = cbcch.shape
    nb = dc3 // 3 // LANES
    k = SCONV_WIDTH

    def body(b_ref, c_ref, h_ref, w_ref, o_ref):
        m = c_ref[...] * h_ref[...]
        wv = w_ref[...]
        y = wv[k - 1:k] * m
        for s in range(1, k):
            y = y + wv[k - 1 - s:k - s] * _shift_down(m, s)
        o_ref[...] = b_ref[...] * y

    return pl.pallas_call(
        body, name="sconv_fwd", grid=(nb,),
        in_specs=[pl.BlockSpec((t, LANES), lambda j: (0, j)), pl.BlockSpec((t, LANES), lambda j: (0, nb + j)),
                  pl.BlockSpec((t, LANES), lambda j: (0, 2 * nb + j)), pl.BlockSpec((k, LANES), lambda j: (0, j))],
        out_specs=pl.BlockSpec((t, LANES), lambda j: (0, j)),
        out_shape=jax.ShapeDtypeStruct((t, dc3 // 3), F32),
        compiler_params=_params("arbitrary"))(cbcch, cbcch, cbcch, w)


def _sconv_bwd(cbcch, w, doc):
    t, dc3 = cbcch.shape
    nb = dc3 // 3 // LANES
    k = SCONV_WIDTH

    def body(b_ref, c_ref, h_ref, w_ref, do_ref, db_ref, dc_ref, dh_ref, dw_ref):
        cv, hv = c_ref[...], h_ref[...]
        m = cv * hv
        wv = w_ref[...]
        dov = do_ref[...]
        dy = dov * b_ref[...]
        y = wv[k - 1:k] * m
        dm = wv[k - 1:k] * dy
        dw_ref[k - 1:k, :] = jnp.sum(dy * m, axis=0, keepdims=True)
        for s in range(1, k):
            ms = _shift_down(m, s)
            y = y + wv[k - 1 - s:k - s] * ms
            dm = dm + wv[k - 1 - s:k - s] * _shift_up(dy, s)
            dw_ref[k - 1 - s:k - s, :] = jnp.sum(dy * ms, axis=0, keepdims=True)
        db_ref[...] = dov * y
        dc_ref[...] = dm * hv
        dh_ref[...] = dm * cv

    col = lambda o: pl.BlockSpec((t, LANES), lambda j: (0, o * nb + j))
    return pl.pallas_call(
        body, name="sconv_bwd", grid=(nb,),
        in_specs=[col(0), col(1), col(2), pl.BlockSpec((k, LANES), lambda j: (0, j)), col(0)],
        out_specs=[col(0), col(0), col(0), pl.BlockSpec((k, LANES), lambda j: (0, j))],
        out_shape=[jax.ShapeDtypeStruct((t, dc3 // 3), F32)] * 3 + [jax.ShapeDtypeStruct(w.shape, F32)],
        compiler_params=_params("arbitrary"))(cbcch, cbcch, cbcch, w, doc)


def _per_head(fn, a, b):
    return jnp.stack([fn(a[h], b[h]) for h in range(a.shape[0])])


def _bmm(a, b):
    return _per_head(_hmm, a, b)


def _bmm_nt(a, b):
    return _per_head(_hmm_nt, a, b)


def _bmm_tn(a, b):
    return _per_head(_hmm_tn, a, b)


def _inv_unit_lower(low):
    c = low.shape[-1]
    eye = (_rows((c, c)) == lax.broadcasted_iota(jnp.int32, (c, c), 1)).astype(F32)
    pw = -low
    inv = eye + pw
    span = 2
    while span < c:
        pw = _bmm(pw, pw)
        inv = inv + _bmm(inv, pw)
        span *= 2
    return inv


def _heads_of(ref, base, heads):
    return jnp.stack([ref[:, base + h * HEAD_DIM:base + (h + 1) * HEAD_DIM] for h in range(heads)])


def _chunk_common(q, k, v, a_col, b_col, alog, dtb, kept=None):
    hn, c, _ = q.shape
    beta = _sigmoid(b_col)
    xg = a_col + dtb
    softplus = jnp.maximum(xg, 0.0) + jnp.log(1.0 + jnp.exp(-jnp.abs(xg)))
    neg_ea = -jnp.exp(alog)
    g = neg_ea * softplus
    ri = _rows((c, c))
    ci = lax.broadcasted_iota(jnp.int32, (c, c), 1)
    incl, strict = ri >= ci, ri > ci
    inclf = jnp.broadcast_to(incl.astype(F32), (hn, c, c))
    gcb = _bmm(inclf, jnp.broadcast_to(g, (hn, c, HEAD_DIM)))
    gc_row = jnp.sum(jnp.where(ri <= ci, jnp.broadcast_to(g, (hn, c, c)), 0.0), axis=1, keepdims=True)
    dmat = jnp.where(incl, jnp.exp(jnp.where(incl, gcb[:, :, :1] - gc_row, 0.0)), 0.0)
    eg = jnp.exp(gcb)
    gl = gcb[:, c - 1:c, :]
    egl = jnp.exp(gl)
    edl = jnp.exp(gl - gcb)
    kb, vb = k * beta, v * beta
    kbe = kb * eg
    if kept is None:
        a0 = _bmm_nt(kb, k)
        tm = _inv_unit_lower(jnp.where(strict, a0 * dmat, 0.0))
        p0 = _bmm_nt(q, k)
        u, w = _bmm(tm, vb), _bmm(tm, kbe)
    else:
        (a0, tm, p0, w), u = kept, None
    return dict(beta=beta, xg=xg, neg_ea=neg_ea, g=g, incl=incl, strict=strict, inclf=inclf, dmat=dmat, eg=eg,
                egl=egl, edl=edl, kb=kb, vb=vb, a0=a0, tm=tm, kbe=kbe, u=u, w=w, p0=p0,
                attn=p0 * dmat, qe=q * eg, kd=k * edl)


def _chunk_step(cm, state):
    vn = cm["u"] - _bmm(cm["w"], state)
    o = _bmm(cm["qe"], state) + _bmm(cm["attn"], vn)
    new_state = state * cm["egl"][:, :, :1] + _bmm_tn(cm["kd"], vn)
    return vn, o, new_state


def _gated_norm(o, zv, og):
    xo, ro = _rms_fwd(o)
    sgz = _sigmoid(zv)
    return xo, ro, sgz, xo * og * (zv * sgz)


def _gate_columns(abv, gpv, heads):
    a_col = jnp.stack([abv[:, h:h + 1] for h in range(heads)])
    b_col = jnp.stack([abv[:, heads + h:heads + h + 1] for h in range(heads)])
    alog = jnp.stack([gpv[0:1, h:h + 1] for h in range(heads)])
    dtb = jnp.stack([gpv[1:2, h:h + 1] for h in range(heads)])
    return a_col, b_col, alog, dtb


def _delta_fwd(qkv, z, ab, gpar, heads):
    t = qkv.shape[0]
    da = heads * HEAD_DIM
    n = t // CHUNK

    def body(qkv_ref, z_ref, ab_ref, gp_ref, oa_ref, st_ref, kc_ref, kw_ref, s_ref):
        @pl.when(pl.program_id(0) == 0)
        def _():
            s_ref[...] = jnp.zeros_like(s_ref)

        gpv = gp_ref[...]
        cm = _chunk_common(_heads_of(qkv_ref, 0, heads), _heads_of(qkv_ref, da, heads), _heads_of(qkv_ref, 2 * da, heads),
                           *_gate_columns(ab_ref[...], gpv, heads))
        state = s_ref[...]
        st_ref[0] = state
        vn, o, new_state = _chunk_step(cm, state)
        s_ref[...] = new_state
        for slot, val in enumerate((cm["a0"], cm["tm"], cm["p0"])):
            kc_ref[0, slot] = val
        for slot, val in enumerate((cm["w"], vn, o)):
            kw_ref[0, slot] = val
        oa = _gated_norm(o, _heads_of(z_ref, 0, heads), gpv[2:3, :])[3]
        for h in range(heads):
            oa_ref[:, h * HEAD_DIM:(h + 1) * HEAD_DIM] = oa[h]

    return pl.pallas_call(
        body, name="delta_fwd", grid=(n,),
        in_specs=[pl.BlockSpec((CHUNK, 3 * da), lambda i: (i, 0)), pl.BlockSpec((CHUNK, da), lambda i: (i, 0)),
                  pl.BlockSpec((CHUNK, LANES), lambda i: (i, 0)), pl.BlockSpec((8, LANES), lambda i: (0, 0))],
        out_specs=[pl.BlockSpec((CHUNK, da), lambda i: (i, 0)),
                   pl.BlockSpec((1, heads, HEAD_DIM, HEAD_DIM), lambda i: (i, 0, 0, 0)),
                   pl.BlockSpec((1, 3, heads, CHUNK, CHUNK), lambda i: (i, 0, 0, 0, 0)),
                   pl.BlockSpec((1, 3, heads, CHUNK, HEAD_DIM), lambda i: (i, 0, 0, 0, 0))],
        out_shape=[jax.ShapeDtypeStruct((t, da), F32), jax.ShapeDtypeStruct((n, heads, HEAD_DIM, HEAD_DIM), F32),
                   jax.ShapeDtypeStruct((n, 3, heads, CHUNK, CHUNK), F32),
                   jax.ShapeDtypeStruct((n, 3, heads, CHUNK, HEAD_DIM), F32)],
        scratch_shapes=[pltpu.VMEM((heads, HEAD_DIM, HEAD_DIM), F32)],
        compiler_params=_params("arbitrary"))(qkv, z, ab, gpar)


def _delta_bwd(qkv, z, ab, gpar, states, kept_c, kept_w, doa, heads):
    t = qkv.shape[0]
    da = heads * HEAD_DIM
    n = t // CHUNK
    c = CHUNK

    def body(qkv_ref, z_ref, ab_ref, gp_ref, st_ref, kc_ref, kw_ref, doa_ref, dqkv_ref, dz_ref, dab_ref, dpar_ref, ds_ref):
        @pl.when(pl.program_id(0) == 0)
        def _():
            ds_ref[...] = jnp.zeros_like(ds_ref)
            dpar_ref[...] = jnp.zeros_like(dpar_ref)

        gpv = gp_ref[...]
        og = gpv[2:3, :]
        q, k, v = _heads_of(qkv_ref, 0, heads), _heads_of(qkv_ref, da, heads), _heads_of(qkv_ref, 2 * da, heads)
        cm = _chunk_common(q, k, v, *_gate_columns(ab_ref[...], gpv, heads),
                           kept=(kc_ref[0, 0], kc_ref[0, 1], kc_ref[0, 2], kw_ref[0, 0]))
        state = st_ref[0]
        dsp = ds_ref[...]
        vn, o = kw_ref[0, 1], kw_ref[0, 2]
        zv = _heads_of(z_ref, 0, heads)
        xo, ro, sgz, _ = _gated_norm(o, zv, og)
        doav = _heads_of(doa_ref, 0, heads)
        don = doav * (zv * sgz)
        dz = doav * (xo * og) * _dsilu(zv, sgz)
        d_og = jnp.sum(jnp.sum(don * xo, axis=1, keepdims=True), axis=0)
        do = _rms_bwd(don * og, xo, ro)
        tm, dmat, eg, edl, egl = cm["tm"], cm["dmat"], cm["eg"], cm["edl"], cm["egl"]
        dvn = _bmm_tn(cm["attn"], do) + _bmm(cm["kd"], dsp)
        dqe = _bmm_nt(do, state)
        ds_ref[...] = _bmm_tn(cm["qe"], do) + dsp * egl[:, :, :1] - _bmm_tn(cm["w"], dvn)
        dattn = _bmm_nt(do, vn)
        dkd = _bmm_nt(vn, dsp)
        dkd_kd = jnp.sum(dkd * cm["kd"], axis=-1, keepdims=True)
        dgl = (jnp.sum(jnp.sum(dsp * state, axis=-1, keepdims=True), axis=1, keepdims=True) * egl[:, :, :1]
               + jnp.sum(dkd_kd, axis=1, keepdims=True))
        dgc = jnp.sum(dqe * cm["qe"], axis=-1, keepdims=True) - dkd_kd
        dk = dkd * edl
        dq = dqe * eg
        dw = -_bmm_nt(dvn, state)
        dp0 = dattn * dmat
        dd = jnp.where(cm["incl"], dattn * cm["p0"], 0.0)
        dq = dq + _bmm(dp0, k)
        dk = dk + _bmm_tn(dp0, q)
        dtm = _bmm_nt(dvn, cm["vb"]) + _bmm_nt(dw, cm["kbe"])
        dvb = _bmm_tn(tm, dvn)
        dkbe = _bmm_tn(tm, dw)
        dkb = dkbe * eg
        dgc = dgc + jnp.sum(dkbe * cm["kbe"], axis=-1, keepdims=True)
        dlow = jnp.where(cm["strict"], -_bmm_tn(tm, _bmm_nt(dtm, tm)), 0.0)
        dd = dd + dlow * cm["a0"]
        da0 = dlow * dmat
        dkb = dkb + _bmm(da0, k)
        dk = dk + _bmm_tn(da0, cm["kb"])
        ddd = dd * dmat
        ones = jnp.ones((heads, c, HEAD_DIM), F32)
        dgc = dgc + jnp.sum(ddd, axis=-1, keepdims=True) - _bmm_tn(ddd, ones)[:, :, :1]
        dgc = dgc + jnp.where(_rows((c, 1)) == c - 1, dgl, 0.0)
        dg = _bmm_tn(cm["inclf"], jnp.broadcast_to(dgc, (heads, c, HEAD_DIM)))[:, :, :1]
        beta = cm["beta"]
        dk = dk + dkb * beta
        dbeta = jnp.sum(dkb * k, axis=-1, keepdims=True) + jnp.sum(dvb * v, axis=-1, keepdims=True)
        dv = dvb * beta
        db_col = dbeta * beta * (1.0 - beta)
        da_col = dg * cm["neg_ea"] * _sigmoid(cm["xg"])
        d_alog = jnp.sum(dg * cm["g"], axis=1, keepdims=True)
        d_dtb = jnp.sum(da_col, axis=1, keepdims=True)
        lane = lax.broadcasted_iota(jnp.int32, (c, LANES), 1)
        lane8 = lax.broadcasted_iota(jnp.int32, (8, LANES), 1)
        row8 = _rows((8, LANES))
        dab = jnp.zeros((c, LANES), F32)
        dpar = jnp.where(row8 == 2, d_og, 0.0)
        for h in range(heads):
            lo = h * HEAD_DIM
            dqkv_ref[:, lo:lo + HEAD_DIM] = dq[h]
            dqkv_ref[:, da + lo:da + lo + HEAD_DIM] = dk[h]
            dqkv_ref[:, 2 * da + lo:2 * da + lo + HEAD_DIM] = dv[h]
            dz_ref[:, lo:lo + HEAD_DIM] = dz[h]
            dab = dab + jnp.where(lane == h, da_col[h], 0.0) + jnp.where(lane == heads + h, db_col[h], 0.0)
            dpar = (dpar + jnp.where((row8 == 0) & (lane8 == h), d_alog[h], 0.0)
                    + jnp.where((row8 == 1) & (lane8 == h), d_dtb[h], 0.0))
        dab_ref[...] = dab
        dpar_ref[...] += dpar

    rev = lambda i: (n - 1 - i, 0)
    return pl.pallas_call(
        body, name="delta_bwd", grid=(n,),
        in_specs=[pl.BlockSpec((c, 3 * da), rev), pl.BlockSpec((c, da), rev), pl.BlockSpec((c, LANES), rev),
                  pl.BlockSpec((8, LANES), lambda i: (0, 0)),
                  pl.BlockSpec((1, heads, HEAD_DIM, HEAD_DIM), lambda i: (n - 1 - i, 0, 0, 0)),
                  pl.BlockSpec((1, 3, heads, c, c), lambda i: (n - 1 - i, 0, 0, 0, 0)),
                  pl.BlockSpec((1, 3, heads, c, HEAD_DIM), lambda i: (n - 1 - i, 0, 0, 0, 0)),
                  pl.BlockSpec((c, da), rev)],
        out_specs=[pl.BlockSpec((c, 3 * da), rev), pl.BlockSpec((c, da), rev), pl.BlockSpec((c, LANES), rev),
                   pl.BlockSpec((8, LANES), lambda i: (0, 0))],
        out_shape=[jax.ShapeDtypeStruct((t, 3 * da), F32), jax.ShapeDtypeStruct((t, da), F32),
                   jax.ShapeDtypeStruct((t, LANES), F32), jax.ShapeDtypeStruct((8, LANES), F32)],
        scratch_shapes=[pltpu.VMEM((heads, HEAD_DIM, HEAD_DIM), F32)],
        compiler_params=_params("arbitrary"))(qkv, z, ab, gpar, states, kept_c, kept_w, doa)


def _w_in_pieces(shard_cols, da, heads):
    a0, nab = 4 * da, 2 * heads
    d_in = 4 * shard_cols
    runs = [(0, a0, 0), (a0, a0 + nab, d_in - nab), (a0 + nab, d_in, a0)]
    pieces = []
    for j in range(4):
        lo, hi = j * shard_cols, (j + 1) * shard_cols
        for rlo, rhi, plo in runs:
            s, e = max(lo, rlo), min(hi, rhi)
            if s < e:
                pieces.append((j, s - lo, e - s, plo + (s - rlo)))
    return pieces, d_in - nab + LANES


def _w_in_pack(w4, li, da, heads):
    _, _, d, sc = w4.shape
    pieces, npk = _w_in_pieces(sc, da, heads)
    tr = _tile_rows(d, 256, SUBLANES_WIRE)

    def body(w_ref, o_ref):
        o_ref[:, npk - LANES:] = jnp.zeros((tr, LANES), o_ref.dtype)
        for j, lo, ln, dst in pieces:
            o_ref[:, dst:dst + ln] = w_ref[j, :, lo:lo + ln]

    return pl.pallas_call(
        body, name="w_in_pack", grid=(d // tr,),
        in_specs=[pl.BlockSpec((4, None, tr, sc), lambda i: (0, li, i, 0))],
        out_specs=pl.BlockSpec((tr, npk), lambda i: (i, 0)),
        out_shape=jax.ShapeDtypeStruct((d, npk), w4.dtype),
        compiler_params=_params("arbitrary"))(w4)


def _w_in_unpack(dwp, sc, da, heads):
    d, npk = dwp.shape
    pieces, _ = _w_in_pieces(sc, da, heads)
    tr = _tile_rows(d, 256)

    def body(g_ref, o_ref):
        for j, lo, ln, dst in pieces:
            o_ref[j, :, lo:lo + ln] = g_ref[:, dst:dst + ln]

    return pl.pallas_call(
        body, name="w_in_unpack", grid=(d // tr,),
        in_specs=[pl.BlockSpec((tr, npk), lambda i: (i, 0))],
        out_specs=pl.BlockSpec((4, tr, sc), lambda i: (0, i, 0)),
        out_shape=jax.ShapeDtypeStruct((4, d, sc), F32),
        compiler_params=_params("arbitrary"))(dwp)


def _block_diag(pool_w):
    g, gd, _ = pool_w.shape
    out = jnp.zeros((g * gd, g * gd), pool_w.dtype)
    for gi in range(g):
        out = lax.dynamic_update_slice(out, pool_w[gi], (gi * gd, gi * gd))
    return out


def _layer_dims(d):
    heads = (d // 2) // HEAD_DIM
    return heads, heads * HEAD_DIM, d // 4, d // 4


BIG = ("w_in", "w_gate", "w_up", "ple_proj", "w_out", "w_down", "ple_gate")


def _prepare_layer(gw, small, li):
    d = small["norm1_g"].shape[1]
    heads, da, _, _ = _layer_dims(d)
    gpar = jnp.zeros((8, LANES), F32)
    gpar = gpar.at[0, :heads].set(small["a_log"][li]).at[1, :heads].set(small["dt_bias"][li]).at[2, :].set(small["onorm_g"][li])
    return dict(norm1_g=small["norm1_g"][li][None], w_in_p=_w_in_pack(gw["w_in"], li, da, heads).astype(MM_DTYPE),
                conv_qkv=small["conv_qkv"][li], gpar=gpar, pool_bd=_block_diag(small["pool_w"][li]).astype(MM_DTYPE),
                pool_scale=small["pool_scale"][li][None], sconv_w=small["sconv_w"][li], norm2_g=small["norm2_g"][li][None])


def _layer_fwd(x0, p, gw, lw, li, tm):
    d = x0.shape[1]
    heads, da, dp, dc = _layer_dims(d)
    segs = (3 * da, da, dp, 3 * dc, LANES)
    qkv_pre, z, hp, cbcch, ab = _in_proj_fwd(x0, lw["norm1_g"], lw["w_in_p"], segs, tm)
    qkv = _qkv_conv_fwd(qkv_pre, lw["conv_qkv"], heads)
    oa, states, kept_c, kept_w = _delta_fwd(qkv, z, ab, lw["gpar"], heads)
    ob = _pool_fwd(hp, lw["pool_bd"], lw["pool_scale"], dp // POOL_GROUPS)
    oc = _sconv_fwd(cbcch, lw["sconv_w"])
    x1, h2 = _out_proj_fwd(x0, (oa, ob, oc), gw["w_out"], li, lw["norm2_g"], tm)
    x2, gp, up = _ffn_fwd(x1, h2, gw["w_gate"], gw["w_up"], gw["w_down"], li, tm)
    x3 = _ple_fwd(x2, p, gw["ple_gate"], gw["ple_proj"], li, tm)
    saved = dict(x0=x0, qkv_pre=qkv_pre, z=z, hp=hp, cbcch=cbcch, ab=ab, qkv=qkv, states=states, kept_c=kept_c, kept_w=kept_w, oa=oa, ob=ob, oc=oc,
                 x1=x1, h2=h2, gp=gp, up=up, x2=x2)
    return x3, saved


def _layer_bwd(dx3, p, gw, lw, li, sv, tm):
    d = dx3.shape[1]
    heads, da, dp, dc = _layer_dims(d)
    segs = (3 * da, da, dp, dc, dc, dc, LANES)
    gd = dp // POOL_GROUPS
    dx2, d_ple_gate, d_ple_proj = _ple_bwd(dx3, sv["x2"], p, gw["ple_gate"], gw["ple_proj"], li, tm)
    dh2, d_w_gate, d_w_up, d_w_down = _ffn_bwd(dx2, sv["h2"], sv["gp"], sv["up"], gw["w_gate"], gw["w_up"], gw["w_down"],
                                               li, min(tm, 256))
    dx1, doa, dob, doc, d_w_out, d_norm2 = _out_proj_bwd(dx2, dh2, sv["x1"], lw["norm2_g"],
                                                         (sv["oa"], sv["ob"], sv["oc"]), gw["w_out"], li, tm)
    dcb, dcc, dch, d_sconv = _sconv_bwd(sv["cbcch"], lw["sconv_w"], doc)
    dhp, d_pool_bd, d_pool_scale = _pool_bwd(sv["hp"], lw["pool_bd"], lw["pool_scale"], dob, gd)
    dqkv, dz, dab, dpar = _delta_bwd(sv["qkv"], sv["z"], sv["ab"], lw["gpar"], sv["states"], sv["kept_c"], sv["kept_w"], doa,
                                      heads)
    dqkv_pre, d_conv_qkv = _qkv_conv_bwd(sv["qkv_pre"], lw["conv_qkv"], dqkv, heads)
    dsegs = (dqkv_pre, dz, dhp, dcb, dcc, dch, dab)
    dx0, d_w_in_p, d_norm1 = _in_proj_bwd(sv["x0"], lw["norm1_g"], lw["w_in_p"], dsegs, dx1, segs, tm)
    per = LANES // gd
    bd = d_pool_bd.reshape(dp // LANES, per, gd, per, gd)
    d_pool_w = jnp.stack([bd[gi // per, gi % per, :, gi % per, :] for gi in range(POOL_GROUPS)])
    big = dict(w_in=_w_in_unpack(d_w_in_p, gw["w_in"].shape[3], da, heads), w_gate=d_w_gate, w_up=d_w_up,
               ple_proj=d_ple_proj, w_out=d_w_out, w_down=d_w_down, ple_gate=d_ple_gate)
    small = dict(norm1_g=d_norm1[0], conv_qkv=d_conv_qkv, a_log=dpar[0, :heads], dt_bias=dpar[1, :heads], onorm_g=dpar[2],
                 pool_w=d_pool_w, pool_scale=d_pool_scale[0], sconv_w=d_sconv, norm2_g=d_norm2[0])
    return dx0, big, small


def _local_step(x, p, target, gw, small):
    t, d = x.shape
    depth = p.shape[0]
    tm = 512 if t % 512 == 0 else 128
    layers = [_prepare_layer(gw, small, li) for li in range(depth)]
    saved = []
    h = x
    for li in range(depth):
        h, sv = _layer_fwd(h, p[li], gw, layers[li], li, tm)
        saved.append(sv)
    dx, loss, d_final = _loss_head(h, target, small["final_g"][None], tm)
    big, sm = [None] * depth, [None] * depth
    for li in reversed(range(depth)):
        dx, big[li], sm[li] = _layer_bwd(dx, p[li], gw, layers[li], li, saved[li], tm)
    small_grads = {n: jnp.stack([g[n] for g in sm]) for n in sm[0]}
    small_grads["final_g"] = d_final[0]
    return loss[0, 0], dx, big, small_grads


def _coords():
    return lax.axis_index("x"), lax.axis_index("y"), lax.axis_index("c")


def _other_chips(x, y):
    return [(1 - x, y), (x, 1 - y), (1 - x, 1 - y)]


def _place_shards(ws, me_idx):
    nt = len(ws)
    depth = ws[0].shape[0]

    def body(me_ref, *refs):
        for w_ref, o_ref in zip(refs[:nt], refs[nt:]):
            o_ref[...] = w_ref[...].astype(o_ref.dtype)

    return pl.pallas_call(
        body, name="place_shards",
        grid_spec=pltpu.PrefetchScalarGridSpec(
            num_scalar_prefetch=1, grid=(depth, 2),
            in_specs=[pl.BlockSpec((None, w.shape[1] // 2, w.shape[2]), lambda l, i, me_ref: (l, i, 0)) for w in ws],
            out_specs=[pl.BlockSpec((None, None, w.shape[1] // 2, w.shape[2]), lambda l, i, me_ref: (me_ref[0], l, i, 0))
                       for w in ws]),
        out_shape=[jax.ShapeDtypeStruct((4,) + w.shape, WIRE_DTYPE) for w in ws],
        compiler_params=_params("arbitrary", "arbitrary"))(me_idx, *ws)


def _all_gather_chips(placed):
    nt = len(placed)

    def body(*refs):
        out_refs = refs[nt:2 * nt]
        send_sems, recv_sems = refs[2 * nt:]
        x, y, c = _coords()
        sibling = (x, y, 1 - c)
        chips = _other_chips(x, y)

        def copy(k, t, block, to):
            px, py, pc = block
            blk = out_refs[t].at[2 * px + py, pc]
            return pltpu.make_async_remote_copy(src_ref=blk, dst_ref=blk, send_sem=send_sems.at[k, t],
                                                recv_sem=recv_sems.at[k, t], device_id=to, device_id_type=MESH)

        first = [copy(j, t, (x, y, c), (*chip, c)) for j, chip in enumerate(chips) for t in range(nt)]
        for cp in first:
            cp.start()
        passed = []
        for j, chip in enumerate(chips):
            for t in range(nt):
                copy(j, t, (*chip, c), (x, y, c)).wait_recv()
                fwd = copy(3 + j, t, (*chip, c), sibling)
                fwd.start()
                passed.append(fwd)
        for j, chip in enumerate(chips):
            for t in range(nt):
                copy(3 + j, t, (*chip, 1 - c), (x, y, c)).wait_recv()
        for cp in first + passed:
            cp.wait_send()

    return pl.pallas_call(
        body, name="all_gather_chips", out_shape=[jax.ShapeDtypeStruct(a.shape, a.dtype) for a in placed],
        in_specs=[ANY] * nt, out_specs=[ANY] * nt, input_output_aliases={t: t for t in range(nt)},
        scratch_shapes=[pltpu.SemaphoreType.DMA((6, nt)), pltpu.SemaphoreType.DMA((6, nt))],
    )(*placed)


def _sibling_swap_half(gs):
    nt = len(gs)

    def body(*refs):
        g_refs, out_refs = refs[:nt], refs[nt:2 * nt]
        send_sems, recv_sems = refs[2 * nt:]
        x, y, c = _coords()
        cps = []
        for t in range(nt):
            rh = g_refs[t].shape[1] // 2
            cps.append(pltpu.make_async_remote_copy(src_ref=g_refs[t].at[:, pl.ds((1 - c) * rh, rh)], dst_ref=out_refs[t],
                                                    send_sem=send_sems.at[t], recv_sem=recv_sems.at[t], device_id=(x, y, 1 - c),
                                                    device_id_type=MESH))
        for cp in cps:
            cp.start()
        for cp in cps:
            cp.wait()

    return pl.pallas_call(
        body, name="sibling_swap_half",
        out_shape=[jax.ShapeDtypeStruct((g.shape[0], g.shape[1] // 2, g.shape[2]), g.dtype) for g in gs],
        in_specs=[ANY] * nt, out_specs=[ANY] * nt,
        scratch_shapes=[pltpu.SemaphoreType.DMA((nt,)), pltpu.SemaphoreType.DMA((nt,))])(*gs)


def _add_my_halves(gs, others, c_idx):
    nt = len(gs)

    def body(c_ref, *refs):
        for g_ref, o_ref, out_ref in zip(refs[:nt], refs[nt:2 * nt], refs[2 * nt:]):
            out_ref[...] = (g_ref[...].astype(F32) + o_ref[...].astype(F32)).astype(out_ref.dtype)

    def quarter(g):
        return pl.BlockSpec((None, g.shape[1] // 4, g.shape[2]), lambda j, i, c_ref: (j, i, 0))

    return pl.pallas_call(
        body, name="add_my_halves",
        grid_spec=pltpu.PrefetchScalarGridSpec(
            num_scalar_prefetch=1, grid=(4, 2),
            in_specs=[pl.BlockSpec((None, g.shape[1] // 4, g.shape[2]), lambda j, i, c_ref: (j, 2 * c_ref[0] + i, 0)) for g in gs]
                     + [quarter(g) for g in gs],
            out_specs=[quarter(g) for g in gs]),
        out_shape=[jax.ShapeDtypeStruct((4, g.shape[1] // 2, g.shape[2]), WIRE_DTYPE) for g in gs],
        compiler_params=_params("arbitrary", "arbitrary"))(c_idx, *gs, *others)


def _exchange_chips(parts):
    nt = len(parts)

    def body(*refs):
        p_refs, out_refs = refs[:nt], refs[nt:2 * nt]
        send_sems, recv_sems = refs[2 * nt:]
        x, y, c = _coords()
        chips = _other_chips(x, y)

        def copy(j, t):
            cx, cy = chips[j]
            return pltpu.make_async_remote_copy(src_ref=p_refs[t].at[2 * cx + cy], dst_ref=out_refs[t].at[j],
                                                send_sem=send_sems.at[j, t], recv_sem=recv_sems.at[j, t], device_id=(cx, cy, c),
                                                device_id_type=MESH)

        sends = [copy(j, t) for j in range(3) for t in range(nt)]
        for cp in sends:
            cp.start()
        for cp in sends:
            cp.wait_recv()
        for cp in sends:
            cp.wait_send()

    return pl.pallas_call(
        body, name="exchange_chips", out_shape=[jax.ShapeDtypeStruct((3,) + p.shape[1:], p.dtype) for p in parts],
        in_specs=[ANY] * nt, out_specs=[ANY] * nt,
        scratch_shapes=[pltpu.SemaphoreType.DMA((3, nt)), pltpu.SemaphoreType.DMA((3, nt))])(*parts)


def _sum_into(pairs, recvs, idx, li, depth, accs):
    nt = len(pairs)

    def body(idx_ref, *refs):
        for p_ref, r_ref, out_ref in zip(refs[:nt], refs[nt:2 * nt], refs[-nt:]):
            out_ref[...] = p_ref[...].astype(F32) + r_ref[0].astype(F32) + r_ref[1].astype(F32) + r_ref[2].astype(F32)

    in_specs = ([pl.BlockSpec((None, p.shape[1] // 2, p.shape[2]), lambda i, idx_ref: (idx_ref[0], i, 0)) for p in pairs]
                + [pl.BlockSpec((3, p.shape[1] // 2, p.shape[2]), lambda i, idx_ref: (0, i, 0)) for p in pairs])
    args = [idx, *pairs, *recvs]
    aliases = {}
    if accs[0] is not None:
        in_specs += [ANY] * nt
        args += list(accs)
        aliases = {1 + 2 * nt + t: t for t in range(nt)}
    return pl.pallas_call(
        body, name="sum_into",
        grid_spec=pltpu.PrefetchScalarGridSpec(
            num_scalar_prefetch=1, grid=(2,), in_specs=in_specs,
            out_specs=[pl.BlockSpec((None, p.shape[1] // 2, p.shape[2]), lambda i, idx_ref: (li, 2 * idx_ref[1] + i, 0))
                       for p in pairs]),
        out_shape=[jax.ShapeDtypeStruct((depth, 2 * p.shape[1], p.shape[2]), F32) for p in pairs],
        input_output_aliases=aliases, compiler_params=_params("arbitrary"))(*args)


def _sum_slots(parts):
    n, rows, cols = parts.shape
    tr = _tile_rows(rows, 512, SUBLANES_WIRE)

    def body(p_ref, out_ref):
        acc = p_ref[0].astype(F32)
        for s in range(1, n):
            acc = acc + p_ref[s].astype(F32)
        out_ref[...] = acc

    return pl.pallas_call(
        body, name="sum_slots", grid=(rows // tr,),
        in_specs=[pl.BlockSpec((n, tr, cols), lambda i: (0, i, 0))],
        out_specs=pl.BlockSpec((tr, cols), lambda i: (i, 0)),
        out_shape=jax.ShapeDtypeStruct((rows, cols), F32),
        compiler_params=_params("arbitrary"))(parts)


def _sibling_share(gs):
    nt = len(gs)
    depth = gs[0].shape[0]

    def body(*refs):
        out_refs = refs[nt:2 * nt]
        send_sems, recv_sems = refs[2 * nt:]
        x, y, c = _coords()
        sends, recvs = [], []
        for t in range(nt):
            rh = out_refs[t].shape[1] // 2
            for li in range(depth):
                mine = out_refs[t].at[li, pl.ds(c * rh, rh)]
                theirs = out_refs[t].at[li, pl.ds((1 - c) * rh, rh)]
                sems = dict(send_sem=send_sems.at[t, li], recv_sem=recv_sems.at[t, li], device_id=(x, y, 1 - c), device_id_type=MESH)
                sends.append(pltpu.make_async_remote_copy(src_ref=mine, dst_ref=mine, **sems))
                recvs.append(pltpu.make_async_remote_copy(src_ref=theirs, dst_ref=theirs, **sems))
        for cp in sends:
            cp.start()
        for cp in recvs:
            cp.wait_recv()
        for cp in sends:
            cp.wait_send()

    return pl.pallas_call(
        body, name="sibling_share", out_shape=[jax.ShapeDtypeStruct(g.shape, g.dtype) for g in gs],
        in_specs=[ANY] * nt, out_specs=[ANY] * nt, input_output_aliases={t: t for t in range(nt)},
        scratch_shapes=[pltpu.SemaphoreType.DMA((nt, depth)), pltpu.SemaphoreType.DMA((nt, depth))])(*gs)


def _all_gather_devices(buf):
    def body(b_ref, out_ref, send_sems, recv_sems, local_sem):
        x, y, c = _coords()
        me = 4 * x + 2 * y + c
        mine = pltpu.make_async_copy(b_ref, out_ref.at[me], local_sem)
        mine.start()
        peers = []
        for k in range(1, 8):
            fx, fy, fc = (k >> 2) & 1, (k >> 1) & 1, k & 1
            peers.append((x ^ fx, y ^ fy, c ^ fc))
        sends = [pltpu.make_async_remote_copy(src_ref=b_ref, dst_ref=out_ref.at[me], send_sem=send_sems.at[k],
                                              recv_sem=recv_sems.at[k], device_id=peer, device_id_type=MESH)
                 for k, peer in enumerate(peers)]
        for cp in sends:
            cp.start()
        for k, (px, py, pc) in enumerate(peers):
            pltpu.make_async_remote_copy(src_ref=b_ref, dst_ref=out_ref.at[4 * px + 2 * py + pc], send_sem=send_sems.at[k],
                                         recv_sem=recv_sems.at[k], device_id=(px, py, pc), device_id_type=MESH).wait_recv()
        for cp in sends:
            cp.wait_send()
        mine.wait()

    return pl.pallas_call(
        body, name="all_gather_devices", out_shape=jax.ShapeDtypeStruct((8,) + buf.shape, buf.dtype),
        in_specs=[ANY], out_specs=ANY,
        scratch_shapes=[pltpu.SemaphoreType.DMA((7,)), pltpu.SemaphoreType.DMA((7,)), pltpu.SemaphoreType.DMA(())])(buf)


def _reduce_layer(big_grads, li, depth, accs, c_idx, idx):
    gs = [big_grads[n] for n in BIG]
    others = _sibling_swap_half(gs)
    pairs = _add_my_halves(gs, others, c_idx)
    return _sum_into(pairs, _exchange_chips(pairs), idx, li, depth, accs)


SMALL_SHARDED = ("conv_qkv", "sconv_w")
REPLICATED = ("norm1_g", "a_log", "dt_bias", "onorm_g", "pool_w", "pool_scale", "norm2_g", "final_g")
ALL_WEIGHTS = ("norm1_g", "w_in", "conv_qkv", "a_log", "dt_bias", "onorm_g", "pool_w", "pool_scale", "sconv_w", "w_out",
               "norm2_g", "w_gate", "w_up", "w_down", "ple_proj", "ple_gate", "final_g")


def _pad_rows(flat, row_multiple):
    m = flat.shape[0]
    r = -(-m // (LANES * row_multiple)) * row_multiple
    return jnp.pad(flat, (0, r * LANES - m)).reshape(r, LANES)


def _adamw(w, g, m, v):
    shape = w.shape
    cols = shape[-1]
    rows = w.size // cols
    tr = _tile_rows(rows, 512)
    c1 = 1.0 / (1.0 - ADAM_B1 ** ADAM_STEP)
    c2 = 1.0 / (1.0 - ADAM_B2 ** ADAM_STEP)

    def body(w_ref, g_ref, m_ref, v_ref, d_ref, nm_ref, nv_ref, go_ref):
        gv = g_ref[...]
        nm = ADAM_B1 * m_ref[...] + (1.0 - ADAM_B1) * gv
        nv = ADAM_B2 * v_ref[...] + (1.0 - ADAM_B2) * (gv * gv)
        nm_ref[...] = nm
        nv_ref[...] = nv
        go_ref[...] = gv
        d_ref[...] = -ADAM_LR * ((nm * c1) / (jnp.sqrt(nv * c2) + ADAM_EPS) + ADAM_WD * w_ref[...])

    spec = pl.BlockSpec((tr, cols), lambda i: (i, 0))
    outs = pl.pallas_call(
        body, name="adamw", grid=(rows // tr,), in_specs=[spec] * 4, out_specs=[spec] * 4,
        out_shape=[jax.ShapeDtypeStruct((rows, cols), F32)] * 4,
        compiler_params=_params("arbitrary"))(*[a.reshape(rows, cols) for a in (w, g, m, v)])
    return tuple(o.reshape(shape) for o in outs)


def kernel(x, p, norm1_g, w_in, conv_qkv, a_log, dt_bias, onorm_g, pool_w, pool_scale, sconv_w, w_out, norm2_g, w_gate, w_up, w_down, ple_proj, ple_gate, final_g, loss_target, m_norm1_g, m_w_in, m_conv_qkv, m_a_log, m_dt_bias, m_onorm_g, m_pool_w, m_pool_scale, m_sconv_w, m_w_out, m_norm2_g, m_w_gate, m_w_up, m_w_down, m_ple_proj, m_ple_gate, m_final_g, v_norm1_g, v_w_in, v_conv_qkv, v_a_log, v_dt_bias, v_onorm_g, v_pool_w, v_pool_scale, v_sconv_w, v_w_out, v_norm2_g, v_w_gate, v_w_up, v_w_down, v_ple_proj, v_ple_gate, v_final_g):
    weights = dict(zip(ALL_WEIGHTS, (norm1_g, w_in, conv_qkv, a_log, dt_bias, onorm_g, pool_w, pool_scale, sconv_w, w_out,
                                     norm2_g, w_gate, w_up, w_down, ple_proj, ple_gate, final_g)))
    mom_m = dict(zip(ALL_WEIGHTS, (m_norm1_g, m_w_in, m_conv_qkv, m_a_log, m_dt_bias, m_onorm_g, m_pool_w, m_pool_scale,
                                   m_sconv_w, m_w_out, m_norm2_g, m_w_gate, m_w_up, m_w_down, m_ple_proj, m_ple_gate, m_final_g)))
    mom_v = dict(zip(ALL_WEIGHTS, (v_norm1_g, v_w_in, v_conv_qkv, v_a_log, v_dt_bias, v_onorm_g, v_pool_w, v_pool_scale,
                                   v_sconv_w, v_w_out, v_norm2_g, v_w_gate, v_w_up, v_w_down, v_ple_proj, v_ple_gate, v_final_g)))
    c_idx = lax.axis_index("c").astype(jnp.int32).reshape(1)
    chip = (2 * lax.axis_index("x") + lax.axis_index("y")).astype(jnp.int32)
    me_idx = chip.reshape(1)
    idx = jnp.stack([chip, lax.axis_index("c").astype(jnp.int32)])
    depth = p.shape[0]

    gathered = _all_gather_chips(_place_shards([weights[n] for n in BIG], me_idx))
    gw = dict(zip(BIG, gathered))
    small = {n: weights[n] for n in REPLICATED}
    sflat = _pad_rows(jnp.concatenate([weights[n].reshape(-1) for n in SMALL_SHARDED]), 8)
    sgath = _all_gather_devices(sflat)[0::2].reshape(4, -1)
    off = 0
    for n in SMALL_SHARDED:
        shp = weights[n].shape
        part = sgath[:, off:off + weights[n].size].reshape((4,) + shp)
        small[n] = jnp.moveaxis(part, 0, -2).reshape(shp[:-1] + (4 * shp[-1],))
        off += weights[n].size

    loss_local, dx, big_grads, small_grads = _local_step(x[0], p[:, 0], loss_target[0], gw, small)

    accs = [None] * len(BIG)
    for li in reversed(range(depth)):
        accs = _reduce_layer(big_grads[li], li, depth, accs, c_idx, idx)
    gshard = dict(zip(BIG, _sibling_share(accs)))

    rnames = REPLICATED + SMALL_SHARDED
    rflat = _pad_rows(jnp.concatenate([small_grads[n].reshape(-1) for n in rnames]), 8)
    rsum = _sum_slots(_all_gather_devices(rflat)).reshape(-1)
    off = 0
    for n in rnames:
        whole = rsum[off:off + small_grads[n].size].reshape(small_grads[n].shape)
        off += small_grads[n].size
        if n in SMALL_SHARDED:
            cols = weights[n].shape[-1]
            whole = lax.dynamic_slice_in_dim(whole, chip * cols, cols, axis=whole.ndim - 1)
        gshard[n] = whole

    loss = lax.psum(loss_local, ("x", "y", "c"))
    deltas, new_m, new_v, grad_out = {}, {}, {}, {}
    for n in ALL_WEIGHTS:
        deltas[n], new_m[n], new_v[n], grad_out[n] = _adamw(weights[n], gshard[n], mom_m[n], mom_v[n])
    return (loss, dx[None], *[grad_out[n] for n in ALL_WEIGHTS], *[deltas[n] for n in ALL_WEIGHTS],
            *[new_m[n] for n in ALL_WEIGHTS], *[new_v[n] for n in ALL_WEIGHTS])
```

```python
import jax
import jax.numpy as jnp
from jax import lax
from jax.experimental import pallas as pl
from jax.experimental.pallas import tpu as pltpu

F32 = jnp.float32
MM_DTYPE = jnp.bfloat16
WIRE_DTYPE = jnp.bfloat16
HI = lax.Precision.HIGHEST
EPS = 1e-6
HEAD_DIM = 128
CHUNK = 64
QKV_CONV_WIDTH = 4
SCONV_WIDTH = 3
POOL_GROUPS = 4
LANES = 128
SUBLANES_WIRE = 16
VMEM_LIMIT_BYTES = 56 * 1024 * 1024
ADAM_LR, ADAM_B1, ADAM_B2, ADAM_EPS, ADAM_WD, ADAM_STEP = 0.001, 0.9, 0.999, 1e-08, 0.01, 10
MESH = pl.DeviceIdType.MESH
ANY = pl.BlockSpec(memory_space=pl.ANY)
HBM = pl.BlockSpec(memory_space=pltpu.HBM)
SEM = pl.BlockSpec(memory_space=pltpu.SEMAPHORE)


def _params(*sem):
    return pltpu.CompilerParams(vmem_limit_bytes=VMEM_LIMIT_BYTES, dimension_semantics=sem if sem else None)


def _mm(a, b):
    return jnp.dot(a.astype(MM_DTYPE), b.astype(MM_DTYPE), preferred_element_type=F32)


def _mm_nt(a, b):
    return lax.dot_general(a.astype(MM_DTYPE), b.astype(MM_DTYPE), (((1,), (1,)), ((), ())), preferred_element_type=F32)


def _mm_tn(a, b):
    return lax.dot_general(a.astype(MM_DTYPE), b.astype(MM_DTYPE), (((0,), (0,)), ((), ())), preferred_element_type=F32)


def _hmm(a, b):
    return jnp.dot(a, b, preferred_element_type=F32, precision=HI)


def _hmm_nt(a, b):
    return lax.dot_general(a, b, (((1,), (1,)), ((), ())), preferred_element_type=F32, precision=HI)


def _hmm_tn(a, b):
    return lax.dot_general(a, b, (((0,), (0,)), ((), ())), preferred_element_type=F32, precision=HI)


def _sigmoid(x):
    return 1.0 / (1.0 + jnp.exp(-x))


def _dsilu(x, s):
    return s * (1.0 + x * (1.0 - s))


def _rows(shape):
    return lax.broadcasted_iota(jnp.int32, shape, 0)


def _shift_down(x, s):
    if s == 0:
        return x
    return jnp.where(_rows(x.shape) >= s, pltpu.roll(x, s, 0), 0.0)


def _shift_up(x, s):
    if s == 0:
        return x
    t = x.shape[0]
    return jnp.where(_rows(x.shape) < t - s, pltpu.roll(x, t - s, 0), 0.0)


def _rms_fwd(x):
    r = lax.rsqrt(jnp.mean(x * x, axis=-1, keepdims=True) + EPS)
    return x * r, r


def _rms_bwd(dxn, xn, r):
    return r * (dxn - xn * jnp.mean(dxn * xn, axis=-1, keepdims=True))


def _tile_rows(n, cap, mult=8):
    best = None
    for d in range(mult, min(n, cap) + 1, mult):
        if n % d == 0:
            best = d
    return best if best is not None else n


def _in_proj_fwd(x, g1, wp, segs, tm):
    t, d = x.shape
    npk = wp.shape[1]

    def body(x_ref, g_ref, w_ref, *o_refs):
        xn, _ = _rms_fwd(x_ref[...])
        h = (xn * g_ref[...]).astype(w_ref.dtype)
        off = 0
        for o_ref, wd in zip(o_refs, segs):
            o_ref[...] = jnp.dot(h, w_ref[:, off:off + wd], preferred_element_type=F32)
            off += wd

    return pl.pallas_call(
        body, name="in_proj_fwd", grid=(t // tm,),
        in_specs=[pl.BlockSpec((tm, d), lambda i: (i, 0)), pl.BlockSpec((1, d), lambda i: (0, 0)),
                  pl.BlockSpec((d, npk), lambda i: (0, 0))],
        out_specs=[pl.BlockSpec((tm, wd), lambda i: (i, 0)) for wd in segs],
        out_shape=[jax.ShapeDtypeStruct((t, wd), F32) for wd in segs],
        compiler_params=_params("arbitrary"))(x, g1, wp)


def _in_proj_bwd(x, g1, wp, dsegs, dx_res, segs, tm):
    t, d = x.shape
    npk = wp.shape[1]
    nseg = len(segs)

    def body(x_ref, g_ref, w_ref, *rest):
        ds_refs = rest[:nseg]
        dxr_ref, dx_ref, dw_ref, dg_ref = rest[nseg:]
        i = pl.program_id(0)

        @pl.when(i == 0)
        def _():
            dw_ref[...] = jnp.zeros_like(dw_ref)
            dg_ref[...] = jnp.zeros_like(dg_ref)

        xn, r = _rms_fwd(x_ref[...])
        g = g_ref[...]
        h = (xn * g).astype(w_ref.dtype)
        dh = jnp.zeros((tm, d), F32)
        off = 0
        for ds_ref, wd in zip(ds_refs, segs):
            dsv = ds_ref[...].astype(w_ref.dtype)
            dh = dh + lax.dot_general(dsv, w_ref[:, off:off + wd], (((1,), (1,)), ((), ())), preferred_element_type=F32)
            dw_ref[:, off:off + wd] += lax.dot_general(h, dsv, (((0,), (0,)), ((), ())), preferred_element_type=F32)
            off += wd
        dg_ref[...] += jnp.sum(dh * xn, axis=0, keepdims=True)
        dx_ref[...] = dxr_ref[...] + _rms_bwd(dh * g, xn, r)

    return pl.pallas_call(
        body, name="in_proj_bwd", grid=(t // tm,),
        in_specs=[pl.BlockSpec((tm, d), lambda i: (i, 0)), pl.BlockSpec((1, d), lambda i: (0, 0)),
                  pl.BlockSpec((d, npk), lambda i: (0, 0))]
                 + [pl.BlockSpec((tm, wd), lambda i: (i, 0)) for wd in segs]
                 + [pl.BlockSpec((tm, d), lambda i: (i, 0))],
        out_specs=[pl.BlockSpec((tm, d), lambda i: (i, 0)), pl.BlockSpec((d, npk), lambda i: (0, 0)),
                   pl.BlockSpec((1, d), lambda i: (0, 0))],
        out_shape=[jax.ShapeDtypeStruct((t, d), F32), jax.ShapeDtypeStruct((d, npk), F32),
                   jax.ShapeDtypeStruct((1, d), F32)],
        compiler_params=_params("arbitrary"))(x, g1, wp, *dsegs, dx_res)


def _out_proj_fwd(x0, mix, wo, li, g2, tm):
    t, d = x0.shape
    dq = wo.shape[2]
    widths = [m.shape[1] for m in mix]

    def body(x_ref, *rest):
        m_refs = rest[:len(mix)]
        w_ref, g_ref, x1_ref, h2_ref = rest[len(mix):]
        acc = x_ref[...]
        off = 0
        for m_ref, wd in zip(m_refs, widths):
            for k in range(wd // dq):
                acc = acc + jnp.dot(m_ref[:, k * dq:(k + 1) * dq].astype(w_ref.dtype), w_ref[off // dq + k],
                                    preferred_element_type=F32)
            off += wd
        x1_ref[...] = acc
        xn, _ = _rms_fwd(acc)
        h2_ref[...] = (xn * g_ref[...]).astype(h2_ref.dtype)

    return pl.pallas_call(
        body, name="out_proj_fwd", grid=(t // tm,),
        in_specs=[pl.BlockSpec((tm, d), lambda i: (i, 0))]
                 + [pl.BlockSpec((tm, wd), lambda i: (i, 0)) for wd in widths]
                 + [pl.BlockSpec((4, None, dq, d), lambda i: (0, li, 0, 0)), pl.BlockSpec((1, d), lambda i: (0, 0))],
        out_specs=[pl.BlockSpec((tm, d), lambda i: (i, 0)), pl.BlockSpec((tm, d), lambda i: (i, 0))],
        out_shape=[jax.ShapeDtypeStruct((t, d), F32), jax.ShapeDtypeStruct((t, d), MM_DTYPE)],
        compiler_params=_params("arbitrary"))(x0, *mix, wo, g2)


def _out_proj_bwd(dx2, dh2, x1, g2, mix, wo, li, tm):
    t, d = x1.shape
    dq = wo.shape[2]
    widths = [m.shape[1] for m in mix]
    nm = len(mix)

    def body(dx2_ref, dh2_ref, x1_ref, g_ref, *rest):
        m_refs = rest[:nm]
        w_ref = rest[nm]
        dx1_ref = rest[nm + 1]
        dm_refs = rest[nm + 2:nm + 2 + nm]
        dw_ref, dg_ref = rest[nm + 2 + nm:]
        i = pl.program_id(0)

        @pl.when(i == 0)
        def _():
            dw_ref[...] = jnp.zeros_like(dw_ref)
            dg_ref[...] = jnp.zeros_like(dg_ref)

        xn, r = _rms_fwd(x1_ref[...])
        dh2v = dh2_ref[...]
        dg_ref[...] += jnp.sum(dh2v * xn, axis=0, keepdims=True)
        dx1 = dx2_ref[...] + _rms_bwd(dh2v * g_ref[...], xn, r)
        dx1_ref[...] = dx1
        dx1c = dx1.astype(w_ref.dtype)
        off = 0
        for m_ref, dm_ref, wd in zip(m_refs, dm_refs, widths):
            for k in range(wd // dq):
                j = off // dq + k
                cols = slice(k * dq, (k + 1) * dq)
                dm_ref[:, cols] = lax.dot_general(dx1c, w_ref[j], (((1,), (1,)), ((), ())), preferred_element_type=F32)
                dw_ref[j] += lax.dot_general(m_ref[:, cols].astype(w_ref.dtype), dx1c, (((0,), (0,)), ((), ())),
                                             preferred_element_type=F32)
            off += wd

    tile = lambda wd: pl.BlockSpec((tm, wd), lambda i: (i, 0))
    return pl.pallas_call(
        body, name="out_proj_bwd", grid=(t // tm,),
        in_specs=[tile(d), tile(d), tile(d), pl.BlockSpec((1, d), lambda i: (0, 0))]
                 + [tile(wd) for wd in widths] + [pl.BlockSpec((4, None, dq, d), lambda i: (0, li, 0, 0))],
        out_specs=[tile(d)] + [tile(wd) for wd in widths]
                  + [pl.BlockSpec((4, dq, d), lambda i: (0, 0, 0)), pl.BlockSpec((1, d), lambda i: (0, 0))],
        out_shape=[jax.ShapeDtypeStruct((t, d), F32)] + [jax.ShapeDtypeStruct((t, wd), F32) for wd in widths]
                  + [jax.ShapeDtypeStruct((4, dq, d), F32), jax.ShapeDtypeStruct((1, d), F32)],
        compiler_params=_params("arbitrary"))(dx2, dh2, x1, g2, *mix, wo)


def _ffn_fwd(x1, h2, wg, wu, wd, li, tm):
    t, d = x1.shape
    fs = wg.shape[3]

    def body(x1_ref, h2_ref, wg_ref, wu_ref, wd_ref, x2_ref, gp_ref, up_ref):
        @pl.when(pl.program_id(1) == 0)
        def _():
            x2_ref[...] = x1_ref[...]

        h = h2_ref[...]
        gp = jnp.dot(h, wg_ref[...], preferred_element_type=F32)
        up = jnp.dot(h, wu_ref[...], preferred_element_type=F32)
        gp_ref[...] = gp
        up_ref[...] = up
        ff = gp * _sigmoid(gp) * up
        x2_ref[...] += jnp.dot(ff.astype(wd_ref.dtype), wd_ref[...], preferred_element_type=F32)

    return pl.pallas_call(
        body, name="ffn_fwd", grid=(t // tm, 4),
        in_specs=[pl.BlockSpec((tm, d), lambda i, j: (i, 0)), pl.BlockSpec((tm, d), lambda i, j: (i, 0)),
                  pl.BlockSpec((None, None, d, fs), lambda i, j: (j, li, 0, 0)),
                  pl.BlockSpec((None, None, d, fs), lambda i, j: (j, li, 0, 0)),
                  pl.BlockSpec((None, None, fs, d), lambda i, j: (j, li, 0, 0))],
        out_specs=[pl.BlockSpec((tm, d), lambda i, j: (i, 0)), pl.BlockSpec((None, tm, fs), lambda i, j: (j, i, 0)),
                   pl.BlockSpec((None, tm, fs), lambda i, j: (j, i, 0))],
        out_shape=[jax.ShapeDtypeStruct((t, d), F32), jax.ShapeDtypeStruct((4, t, fs), F32),
                   jax.ShapeDtypeStruct((4, t, fs), F32)],
        compiler_params=_params("arbitrary", "arbitrary"))(x1, h2, wg, wu, wd)


def _ffn_bwd(dx2, h2, gp, up, wg, wu, wd, li, tm):
    t, d = dx2.shape
    fs = wg.shape[3]

    def body(dx2_ref, h2_ref, gp_ref, up_ref, wg_ref, wu_ref, wd_ref, dh2_ref, dwg_ref, dwu_ref, dwd_ref):
        j, i = pl.program_id(0), pl.program_id(1)

        @pl.when(i == 0)
        def _():
            dwg_ref[...] = jnp.zeros_like(dwg_ref)
            dwu_ref[...] = jnp.zeros_like(dwu_ref)
            dwd_ref[...] = jnp.zeros_like(dwd_ref)

        cdt = wg_ref.dtype
        h = h2_ref[...]
        gpv, upv = gp_ref[...], up_ref[...]
        s = _sigmoid(gpv)
        silu = gpv * s
        dx2c = dx2_ref[...].astype(cdt)
        dff = lax.dot_general(dx2c, wd_ref[...], (((1,), (1,)), ((), ())), preferred_element_type=F32)
        dwd_ref[...] += lax.dot_general((silu * upv).astype(cdt), dx2c, (((0,), (0,)), ((), ())), preferred_element_type=F32)
        dup = (dff * silu).astype(cdt)
        dgp = (dff * upv * _dsilu(gpv, s)).astype(cdt)
        dwg_ref[...] += lax.dot_general(h, dgp, (((0,), (0,)), ((), ())), preferred_element_type=F32)
        dwu_ref[...] += lax.dot_general(h, dup, (((0,), (0,)), ((), ())), preferred_element_type=F32)
        dh = (lax.dot_general(dgp, wg_ref[...], (((1,), (1,)), ((), ())), preferred_element_type=F32)
              + lax.dot_general(dup, wu_ref[...], (((1,), (1,)), ((), ())), preferred_element_type=F32))
        rows = pl.ds(pl.multiple_of(i * tm, tm), tm)

        @pl.when(j == 0)
        def _():
            dh2_ref[rows, :] = dh

        @pl.when(j != 0)
        def _():
            dh2_ref[rows, :] += dh

    return pl.pallas_call(
        body, name="ffn_bwd", grid=(4, t // tm),
        in_specs=[pl.BlockSpec((tm, d), lambda j, i: (i, 0)), pl.BlockSpec((tm, d), lambda j, i: (i, 0)),
                  pl.BlockSpec((None, tm, fs), lambda j, i: (j, i, 0)), pl.BlockSpec((None, tm, fs), lambda j, i: (j, i, 0)),
                  pl.BlockSpec((None, None, d, fs), lambda j, i: (j, li, 0, 0)),
                  pl.BlockSpec((None, None, d, fs), lambda j, i: (j, li, 0, 0)),
                  pl.BlockSpec((None, None, fs, d), lambda j, i: (j, li, 0, 0))],
        out_specs=[pl.BlockSpec((t, d), lambda j, i: (0, 0)), pl.BlockSpec((None, d, fs), lambda j, i: (j, 0, 0)),
                   pl.BlockSpec((None, d, fs), lambda j, i: (j, 0, 0)), pl.BlockSpec((None, fs, d), lambda j, i: (j, 0, 0))],
        out_shape=[jax.ShapeDtypeStruct((t, d), F32), jax.ShapeDtypeStruct((4, d, fs), F32),
                   jax.ShapeDtypeStruct((4, d, fs), F32), jax.ShapeDtypeStruct((4, fs, d), F32)],
        compiler_params=_params("arbitrary", "arbitrary"))(dx2, h2, gp, up, wg, wu, wd)


def _ple_fwd(x2, p, wpg, wpp, li, tm):
    t, d = x2.shape
    q = p.shape[1]
    dq = d // 4

    def body(x_ref, p_ref, wg_ref, wp_ref, o_ref):
        xv = x_ref[...]
        xc = xv.astype(wg_ref.dtype)
        pc = p_ref[...].astype(wp_ref.dtype)
        pre = jnp.dot(xc[:, :dq], wg_ref[0], preferred_element_type=F32)
        for j in range(1, 4):
            pre = pre + jnp.dot(xc[:, j * dq:(j + 1) * dq], wg_ref[j], preferred_element_type=F32)
        gate = _sigmoid(pre)
        for j in range(4):
            cols = slice(j * dq, (j + 1) * dq)
            o_ref[:, cols] = xv[:, cols] + gate[:, cols] * jnp.dot(pc, wp_ref[j], preferred_element_type=F32)

    return pl.pallas_call(
        body, name="ple_fwd", grid=(t // tm,),
        in_specs=[pl.BlockSpec((tm, d), lambda i: (i, 0)), pl.BlockSpec((tm, q), lambda i: (i, 0)),
                  pl.BlockSpec((4, None, dq, d), lambda i: (0, li, 0, 0)),
                  pl.BlockSpec((4, None, q, dq), lambda i: (0, li, 0, 0))],
        out_specs=pl.BlockSpec((tm, d), lambda i: (i, 0)),
        out_shape=jax.ShapeDtypeStruct((t, d), F32),
        compiler_params=_params("arbitrary"))(x2, p, wpg, wpp)


def _ple_bwd(dx3, x2, p, wpg, wpp, li, tm):
    t, d = x2.shape
    q = p.shape[1]
    dq = d // 4

    def body(dx3_ref, x_ref, p_ref, wg_ref, wp_ref, dx2_ref, dwg_ref, dwp_ref):
        @pl.when(pl.program_id(0) == 0)
        def _():
            dwg_ref[...] = jnp.zeros_like(dwg_ref)
            dwp_ref[...] = jnp.zeros_like(dwp_ref)

        cdt = wg_ref.dtype
        xc = x_ref[...].astype(cdt)
        pc = p_ref[...].astype(cdt)
        pre = jnp.dot(xc[:, :dq], wg_ref[0], preferred_element_type=F32)
        for j in range(1, 4):
            pre = pre + jnp.dot(xc[:, j * dq:(j + 1) * dq], wg_ref[j], preferred_element_type=F32)
        gate = _sigmoid(pre)
        dx3v = dx3_ref[...]
        dpp = (dx3v * gate).astype(cdt)
        dgate = dx3v * gate * (1.0 - gate)
        dpre_parts = []
        for j in range(4):
            cols = slice(j * dq, (j + 1) * dq)
            pp_j = jnp.dot(pc, wp_ref[j], preferred_element_type=F32)
            dpre_parts.append((dgate[:, cols] * pp_j).astype(cdt))
            dwp_ref[j] += lax.dot_general(pc, dpp[:, cols], (((0,), (0,)), ((), ())), preferred_element_type=F32)
        dpre = jnp.concatenate(dpre_parts, axis=1)
        for j in range(4):
            cols = slice(j * dq, (j + 1) * dq)
            dwg_ref[j] += lax.dot_general(xc[:, cols], dpre, (((0,), (0,)), ((), ())), preferred_element_type=F32)
            dx2_ref[:, cols] = dx3v[:, cols] + lax.dot_general(dpre, wg_ref[j], (((1,), (1,)), ((), ())),
                                                               preferred_element_type=F32)

    return pl.pallas_call(
        body, name="ple_bwd", grid=(t // tm,),
        in_specs=[pl.BlockSpec((tm, d), lambda i: (i, 0)), pl.BlockSpec((tm, d), lambda i: (i, 0)),
                  pl.BlockSpec((tm, q), lambda i: (i, 0)), pl.BlockSpec((4, None, dq, d), lambda i: (0, li, 0, 0)),
                  pl.BlockSpec((4, None, q, dq), lambda i: (0, li, 0, 0))],
        out_specs=[pl.BlockSpec((tm, d), lambda i: (i, 0)), pl.BlockSpec((4, dq, d), lambda i: (0, 0, 0)),
                   pl.BlockSpec((4, q, dq), lambda i: (0, 0, 0))],
        out_shape=[jax.ShapeDtypeStruct((t, d), F32), jax.ShapeDtypeStruct((4, dq, d), F32),
                   jax.ShapeDtypeStruct((4, q, dq), F32)],
        compiler_params=_params("arbitrary"))(dx3, x2, p, wpg, wpp)


def _loss_head(x, target, fg, tm):
    t, d = x.shape

    def body(x_ref, t_ref, g_ref, dx_ref, loss_ref, dg_ref):
        @pl.when(pl.program_id(0) == 0)
        def _():
            loss_ref[...] = jnp.zeros_like(loss_ref)
            dg_ref[...] = jnp.zeros_like(dg_ref)

        xn, r = _rms_fwd(x_ref[...])
        g = g_ref[...]
        err = xn * g - t_ref[...]
        loss_ref[...] += 0.5 * jnp.sum(jnp.sum(err * err, axis=-1, keepdims=True) / d, axis=0, keepdims=True)
        dy = err / d
        dg_ref[...] += jnp.sum(dy * xn, axis=0, keepdims=True)
        dx_ref[...] = _rms_bwd(dy * g, xn, r)

    return pl.pallas_call(
        body, name="loss_head", grid=(t // tm,),
        in_specs=[pl.BlockSpec((tm, d), lambda i: (i, 0)), pl.BlockSpec((tm, d), lambda i: (i, 0)),
                  pl.BlockSpec((1, d), lambda i: (0, 0))],
        out_specs=[pl.BlockSpec((tm, d), lambda i: (i, 0)), pl.BlockSpec((1, 1), lambda i: (0, 0)),
                   pl.BlockSpec((1, d), lambda i: (0, 0))],
        out_shape=[jax.ShapeDtypeStruct((t, d), F32), jax.ShapeDtypeStruct((1, 1), F32),
                   jax.ShapeDtypeStruct((1, d), F32)],
        compiler_params=_params("arbitrary"))(x, target, fg)


def _qkv_conv_act(xv, w, j, heads):
    k = QKV_CONV_WIDTH
    y = w[k - 1:k] * xv
    for s in range(1, k):
        y = y + w[k - 1 - s:k - s] * _shift_down(xv, s)
    sg = _sigmoid(y)
    s_act = y * sg
    nrm = lax.rsqrt(jnp.sum(s_act * s_act, axis=-1, keepdims=True) + EPS)
    scale = jnp.where(j < heads, HEAD_DIM ** -0.5, 1.0).astype(F32)
    return y, sg, s_act, nrm, scale


def _qkv_conv_fwd(qkv_pre, conv_w, heads):
    t = qkv_pre.shape[0]
    nblk = 3 * heads

    def body(x_ref, w_ref, o_ref):
        j = pl.program_id(0)
        _, _, s_act, nrm, scale = _qkv_conv_act(x_ref[...], w_ref[...], j, heads)
        o_ref[...] = jnp.where(j < 2 * heads, s_act * (nrm * scale), s_act)

    return pl.pallas_call(
        body, name="qkv_conv_fwd", grid=(nblk,),
        in_specs=[pl.BlockSpec((t, LANES), lambda j: (0, j)), pl.BlockSpec((QKV_CONV_WIDTH, LANES), lambda j: (0, j))],
        out_specs=pl.BlockSpec((t, LANES), lambda j: (0, j)),
        out_shape=jax.ShapeDtypeStruct(qkv_pre.shape, F32),
        compiler_params=_params("arbitrary"))(qkv_pre, conv_w)


def _qkv_conv_bwd(qkv_pre, conv_w, dqkv, heads):
    t = qkv_pre.shape[0]
    nblk = 3 * heads
    k = QKV_CONV_WIDTH

    def body(x_ref, w_ref, dn_ref, dx_ref, dw_ref):
        j = pl.program_id(0)
        xv, w = x_ref[...], w_ref[...]
        y, sg, s_act, nrm, scale = _qkv_conv_act(xv, w, j, heads)
        dn = dn_ref[...]
        dsn = dn * scale
        ds_qk = nrm * dsn - s_act * (nrm * nrm * nrm) * jnp.sum(dsn * s_act, axis=-1, keepdims=True)
        ds = jnp.where(j < 2 * heads, ds_qk, dn)
        dy = ds * _dsilu(y, sg)
        dx = w[k - 1:k] * dy
        dw_ref[k - 1:k, :] = jnp.sum(dy * xv, axis=0, keepdims=True)
        for s in range(1, k):
            dx = dx + w[k - 1 - s:k - s] * _shift_up(dy, s)
            dw_ref[k - 1 - s:k - s, :] = jnp.sum(dy * _shift_down(xv, s), axis=0, keepdims=True)
        dx_ref[...] = dx

    return pl.pallas_call(
        body, name="qkv_conv_bwd", grid=(nblk,),
        in_specs=[pl.BlockSpec((t, LANES), lambda j: (0, j)), pl.BlockSpec((k, LANES), lambda j: (0, j)),
                  pl.BlockSpec((t, LANES), lambda j: (0, j))],
        out_specs=[pl.BlockSpec((t, LANES), lambda j: (0, j)), pl.BlockSpec((k, LANES), lambda j: (0, j))],
        out_shape=[jax.ShapeDtypeStruct(qkv_pre.shape, F32), jax.ShapeDtypeStruct(conv_w.shape, F32)],
        compiler_params=_params("arbitrary"))(qkv_pre, conv_w, dqkv)


def _pool_windows(shape, j, group_dim):
    lane = lax.broadcasted_iota(jnp.int32, shape, 1) + j * LANES
    grp = lane // group_dim
    win = jnp.left_shift(2, grp).astype(F32)
    cnt = jnp.minimum((_rows(shape) + 1).astype(F32), win)
    return grp, cnt


def _pool_select(grp, levels):
    out = levels[0]
    for gi in range(1, POOL_GROUPS):
        out = jnp.where(grp == gi, levels[gi], out)
    return out


def _pool_mean(hv, grp, cnt):
    acc, levels, width = hv, [], 1
    for _ in range(POOL_GROUPS):
        acc = acc + _shift_down(acc, width)
        width *= 2
        levels.append(acc)
    return _pool_select(grp, levels) / cnt - hv


def _pool_fwd(hp, wbd, scale, group_dim):
    t, dp = hp.shape

    def body(h_ref, w_ref, s_ref, o_ref):
        hv = h_ref[...]
        grp, cnt = _pool_windows(hv.shape, pl.program_id(0), group_dim)
        pooled = _pool_mean(hv, grp, cnt)
        o_ref[...] = _mm(pooled, w_ref[...]) * s_ref[...]

    return pl.pallas_call(
        body, name="pool_fwd", grid=(dp // LANES,),
        in_specs=[pl.BlockSpec((t, LANES), lambda j: (0, j)), pl.BlockSpec((LANES, LANES), lambda j: (j, j)),
                  pl.BlockSpec((1, LANES), lambda j: (0, j))],
        out_specs=pl.BlockSpec((t, LANES), lambda j: (0, j)),
        out_shape=jax.ShapeDtypeStruct(hp.shape, F32),
        compiler_params=_params("arbitrary"))(hp, wbd, scale)


def _pool_bwd(hp, wbd, scale, dob, group_dim):
    t, dp = hp.shape

    def body(h_ref, w_ref, s_ref, do_ref, dh_ref, dw_ref, ds_ref):
        hv = h_ref[...]
        grp, cnt = _pool_windows(hv.shape, pl.program_id(0), group_dim)
        pooled = _pool_mean(hv, grp, cnt)
        wv = w_ref[...]
        dov = do_ref[...]
        ds_ref[...] = jnp.sum(dov * _mm(pooled, wv), axis=0, keepdims=True)
        dys = dov * s_ref[...]
        dw_ref[0] = _mm_tn(pooled, dys)
        dpooled = _mm_nt(dys, wv)
        acc, levels, width = dpooled / cnt, [], 1
        for _ in range(POOL_GROUPS):
            acc = acc + _shift_up(acc, width)
            width *= 2
            levels.append(acc)
        dh_ref[...] = _pool_select(grp, levels) - dpooled

    nb = dp // LANES
    return pl.pallas_call(
        body, name="pool_bwd", grid=(nb,),
        in_specs=[pl.BlockSpec((t, LANES), lambda j: (0, j)), pl.BlockSpec((LANES, LANES), lambda j: (j, j)),
                  pl.BlockSpec((1, LANES), lambda j: (0, j)), pl.BlockSpec((t, LANES), lambda j: (0, j))],
        out_specs=[pl.BlockSpec((t, LANES), lambda j: (0, j)), pl.BlockSpec((1, LANES, LANES), lambda j: (j, 0, 0)),
                   pl.BlockSpec((1, LANES), lambda j: (0, j))],
        out_shape=[jax.ShapeDtypeStruct(hp.shape, F32), jax.ShapeDtypeStruct((nb, LANES, LANES), F32),
                   jax.ShapeDtypeStruct((1, dp), F32)],
        compiler_params=_params("arbitrary"))(hp, wbd, scale, dob)


def _sconv_fwd(cbcch, w):
    t, dc3 = cbcch.shape
    nb = dc3 // 3 // LANES
    k = SCONV_WIDTH

    def body(b_ref, c_ref, h_ref, w_ref, o_ref):
        m = c_ref[...] * h_ref[...]
        wv = w_ref[...]
        y = wv[k - 1:k] * m
        for s in range(1, k):
            y = y + wv[k - 1 - s:k - s] * _shift_down(m, s)
        o_ref[...] = b_ref[...] * y

    return pl.pallas_call(
        body, name="sconv_fwd", grid=(nb,),
        in_specs=[pl.BlockSpec((t, LANES), lambda j: (0, j)), pl.BlockSpec((t, LANES), lambda j: (0, nb + j)),
                  pl.BlockSpec((t, LANES), lambda j: (0, 2 * nb + j)), pl.BlockSpec((k, LANES), lambda j: (0, j))],
        out_specs=pl.BlockSpec((t, LANES), lambda j: (0, j)),
        out_shape=jax.ShapeDtypeStruct((t, dc3 // 3), F32),
        compiler_params=_params("arbitrary"))(cbcch, cbcch, cbcch, w)


def _sconv_bwd(cbcch, w, doc):
    t, dc3 = cbcch.shape
    nb = dc3 // 3 // LANES
    k = SCONV_WIDTH

    def body(b_ref, c_ref, h_ref, w_ref, do_ref, db_ref, dc_ref, dh_ref, dw_ref):
        cv, hv = c_ref[...], h_ref[...]
        m = cv * hv
        wv = w_ref[...]
        dov = do_ref[...]
        dy = dov * b_ref[...]
        y = wv[k - 1:k] * m
        dm = wv[k - 1:k] * dy
        dw_ref[k - 1:k, :] = jnp.sum(dy * m, axis=0, keepdims=True)
        for s in range(1, k):
            ms = _shift_down(m, s)
            y = y + wv[k - 1 - s:k - s] * ms
            dm = dm + wv[k - 1 - s:k - s] * _shift_up(dy, s)
            dw_ref[k - 1 - s:k - s, :] = jnp.sum(dy * ms, axis=0, keepdims=True)
        db_ref[...] = dov * y
        dc_ref[...] = dm * hv
        dh_ref[...] = dm * cv

    col = lambda o: pl.BlockSpec((t, LANES), lambda j: (0, o * nb + j))
    return pl.pallas_call(
        body, name="sconv_bwd", grid=(nb,),
        in_specs=[col(0), col(1), col(2), pl.BlockSpec((k, LANES), lambda j: (0, j)), col(0)],
        out_specs=[col(0), col(0), col(0), pl.BlockSpec((k, LANES), lambda j: (0, j))],
        out_shape=[jax.ShapeDtypeStruct((t, dc3 // 3), F32)] * 3 + [jax.ShapeDtypeStruct(w.shape, F32)],
        compiler_params=_params("arbitrary"))(cbcch, cbcch, cbcch, w, doc)


def _per_head(fn, a, b):
    return jnp.stack([fn(a[h], b[h]) for h in range(a.shape[0])])


def _bmm(a, b):
    return _per_head(_hmm, a, b)


def _bmm_nt(a, b):
    return _per_head(_hmm_nt, a, b)


def _bmm_tn(a, b):
    return _per_head(_hmm_tn, a, b)


def _inv_unit_lower(low):
    c = low.shape[-1]
    eye = (_rows((c, c)) == lax.broadcasted_iota(jnp.int32, (c, c), 1)).astype(F32)
    pw = -low
    inv = eye + pw
    span = 2
    while span < c:
        pw = _bmm(pw, pw)
        inv = inv + _bmm(inv, pw)
        span *= 2
    return inv


def _heads_of(ref, base, heads):
    return jnp.stack([ref[:, base + h * HEAD_DIM:base + (h + 1) * HEAD_DIM] for h in range(heads)])


def _chunk_common(q, k, v, a_col, b_col, alog, dtb, kept=None):
    hn, c, _ = q.shape
    beta = _sigmoid(b_col)
    xg = a_col + dtb
    softplus = jnp.maximum(xg, 0.0) + jnp.log(1.0 + jnp.exp(-jnp.abs(xg)))
    neg_ea = -jnp.exp(alog)
    g = neg_ea * softplus
    ri = _rows((c, c))
    ci = lax.broadcasted_iota(jnp.int32, (c, c), 1)
    incl, strict = ri >= ci, ri > ci
    inclf = jnp.broadcast_to(incl.astype(F32), (hn, c, c))
    gcb = _bmm(inclf, jnp.broadcast_to(g, (hn, c, HEAD_DIM)))
    gc_row = jnp.sum(jnp.where(ri <= ci, jnp.broadcast_to(g, (hn, c, c)), 0.0), axis=1, keepdims=True)
    dmat = jnp.where(incl, jnp.exp(jnp.where(incl, gcb[:, :, :1] - gc_row, 0.0)), 0.0)
    eg = jnp.exp(gcb)
    gl = gcb[:, c - 1:c, :]
    egl = jnp.exp(gl)
    edl = jnp.exp(gl - gcb)
    kb, vb = k * beta, v * beta
    kbe = kb * eg
    if kept is None:
        a0 = _bmm_nt(kb, k)
        tm = _inv_unit_lower(jnp.where(strict, a0 * dmat, 0.0))
        p0 = _bmm_nt(q, k)
        u, w = _bmm(tm, vb), _bmm(tm, kbe)
    else:
        (a0, tm, p0, w), u = kept, None
    return dict(beta=beta, xg=xg, neg_ea=neg_ea, g=g, incl=incl, strict=strict, inclf=inclf, dmat=dmat, eg=eg,
                egl=egl, edl=edl, kb=kb, vb=vb, a0=a0, tm=tm, kbe=kbe, u=u, w=w, p0=p0,
                attn=p0 * dmat, qe=q * eg, kd=k * edl)


def _chunk_step(cm, state):
    vn = cm["u"] - _bmm(cm["w"], state)
    o = _bmm(cm["qe"], state) + _bmm(cm["attn"], vn)
    new_state = state * cm["egl"][:, :, :1] + _bmm_tn(cm["kd"], vn)
    return vn, o, new_state


def _gated_norm(o, zv, og):
    xo, ro = _rms_fwd(o)
    sgz = _sigmoid(zv)
    return xo, ro, sgz, xo * og * (zv * sgz)


def _gate_columns(abv, gpv, heads):
    a_col = jnp.stack([abv[:, h:h + 1] for h in range(heads)])
    b_col = jnp.stack([abv[:, heads + h:heads + h + 1] for h in range(heads)])
    alog = jnp.stack([gpv[0:1, h:h + 1] for h in range(heads)])
    dtb = jnp.stack([gpv[1:2, h:h + 1] for h in range(heads)])
    return a_col, b_col, alog, dtb


def _delta_fwd(qkv, z, ab, gpar, heads):
    t = qkv.shape[0]
    da = heads * HEAD_DIM
    n = t // CHUNK

    def body(qkv_ref, z_ref, ab_ref, gp_ref, oa_ref, st_ref, kc_ref, kw_ref, s_ref):
        @pl.when(pl.program_id(0) == 0)
        def _():
            s_ref[...] = jnp.zeros_like(s_ref)

        gpv = gp_ref[...]
        cm = _chunk_common(_heads_of(qkv_ref, 0, heads), _heads_of(qkv_ref, da, heads), _heads_of(qkv_ref, 2 * da, heads),
                           *_gate_columns(ab_ref[...], gpv, heads))
        state = s_ref[...]
        st_ref[0] = state
        vn, o, new_state = _chunk_step(cm, state)
        s_ref[...] = new_state
        for slot, val in enumerate((cm["a0"], cm["tm"], cm["p0"])):
            kc_ref[0, slot] = val
        for slot, val in enumerate((cm["w"], vn, o)):
            kw_ref[0, slot] = val
        oa = _gated_norm(o, _heads_of(z_ref, 0, heads), gpv[2:3, :])[3]
        for h in range(heads):
            oa_ref[:, h * HEAD_DIM:(h + 1) * HEAD_DIM] = oa[h]

    return pl.pallas_call(
        body, name="delta_fwd", grid=(n,),
        in_specs=[pl.BlockSpec((CHUNK, 3 * da), lambda i: (i, 0)), pl.BlockSpec((CHUNK, da), lambda i: (i, 0)),
                  pl.BlockSpec((CHUNK, LANES), lambda i: (i, 0)), pl.BlockSpec((8, LANES), lambda i: (0, 0))],
        out_specs=[pl.BlockSpec((CHUNK, da), lambda i: (i, 0)),
                   pl.BlockSpec((1, heads, HEAD_DIM, HEAD_DIM), lambda i: (i, 0, 0, 0)),
                   pl.BlockSpec((1, 3, heads, CHUNK, CHUNK), lambda i: (i, 0, 0, 0, 0)),
                   pl.BlockSpec((1, 3, heads, CHUNK, HEAD_DIM), lambda i: (i, 0, 0, 0, 0))],
        out_shape=[jax.ShapeDtypeStruct((t, da), F32), jax.ShapeDtypeStruct((n, heads, HEAD_DIM, HEAD_DIM), F32),
                   jax.ShapeDtypeStruct((n, 3, heads, CHUNK, CHUNK), F32),
                   jax.ShapeDtypeStruct((n, 3, heads, CHUNK, HEAD_DIM), F32)],
        scratch_shapes=[pltpu.VMEM((heads, HEAD_DIM, HEAD_DIM), F32)],
        compiler_params=_params("arbitrary"))(qkv, z, ab, gpar)


def _delta_bwd(qkv, z, ab, gpar, states, kept_c, kept_w, doa, heads):
    t = qkv.shape[0]
    da = heads * HEAD_DIM
    n = t // CHUNK
    c = CHUNK

    def body(qkv_ref, z_ref, ab_ref, gp_ref, st_ref, kc_ref, kw_ref, doa_ref, dqkv_ref, dz_ref, dab_ref, dpar_ref, ds_ref):
        @pl.when(pl.program_id(0) == 0)
        def _():
            ds_ref[...] = jnp.zeros_like(ds_ref)
            dpar_ref[...] = jnp.zeros_like(dpar_ref)

        gpv = gp_ref[...]
        og = gpv[2:3, :]
        q, k, v = _heads_of(qkv_ref, 0, heads), _heads_of(qkv_ref, da, heads), _heads_of(qkv_ref, 2 * da, heads)
        cm = _chunk_common(q, k, v, *_gate_columns(ab_ref[...], gpv, heads),
                           kept=(kc_ref[0, 0], kc_ref[0, 1], kc_ref[0, 2], kw_ref[0, 0]))
        state = st_ref[0]
        dsp = ds_ref[...]
        vn, o = kw_ref[0, 1], kw_ref[0, 2]
        zv = _heads_of(z_ref, 0, heads)
        xo, ro, sgz, _ = _gated_norm(o, zv, og)
        doav = _heads_of(doa_ref, 0, heads)
        don = doav * (zv * sgz)
        dz = doav * (xo * og) * _dsilu(zv, sgz)
        d_og = jnp.sum(jnp.sum(don * xo, axis=1, keepdims=True), axis=0)
        do = _rms_bwd(don * og, xo, ro)
        tm, dmat, eg, edl, egl = cm["tm"], cm["dmat"], cm["eg"], cm["edl"], cm["egl"]
        dvn = _bmm_tn(cm["attn"], do) + _bmm(cm["kd"], dsp)
        dqe = _bmm_nt(do, state)
        ds_ref[...] = _bmm_tn(cm["qe"], do) + dsp * egl[:, :, :1] - _bmm_tn(cm["w"], dvn)
        dattn = _bmm_nt(do, vn)
        dkd = _bmm_nt(vn, dsp)
        dkd_kd = jnp.sum(dkd * cm["kd"], axis=-1, keepdims=True)
        dgl = (jnp.sum(jnp.sum(dsp * state, axis=-1, keepdims=True), axis=1, keepdims=True) * egl[:, :, :1]
               + jnp.sum(dkd_kd, axis=1, keepdims=True))
        dgc = jnp.sum(dqe * cm["qe"], axis=-1, keepdims=True) - dkd_kd
        dk = dkd * edl
        dq = dqe * eg
        dw = -_bmm_nt(dvn, state)
        dp0 = dattn * dmat
        dd = jnp.where(cm["incl"], dattn * cm["p0"], 0.0)
        dq = dq + _bmm(dp0, k)
        dk = dk + _bmm_tn(dp0, q)
        dtm = _bmm_nt(dvn, cm["vb"]) + _bmm_nt(dw, cm["kbe"])
        dvb = _bmm_tn(tm, dvn)
        dkbe = _bmm_tn(tm, dw)
        dkb = dkbe * eg
        dgc = dgc + jnp.sum(dkbe * cm["kbe"], axis=-1, keepdims=True)
        dlow = jnp.where(cm["strict"], -_bmm_tn(tm, _bmm_nt(dtm, tm)), 0.0)
        dd = dd + dlow * cm["a0"]
        da0 = dlow * dmat
        dkb = dkb + _bmm(da0, k)
        dk = dk + _bmm_tn(da0, cm["kb"])
        ddd = dd * dmat
        ones = jnp.ones((heads, c, HEAD_DIM), F32)
        dgc = dgc + jnp.sum(ddd, axis=-1, keepdims=True) - _bmm_tn(ddd, ones)[:, :, :1]
        dgc = dgc + jnp.where(_rows((c, 1)) == c - 1, dgl, 0.0)
        dg = _bmm_tn(cm["inclf"], jnp.broadcast_to(dgc, (heads, c, HEAD_DIM)))[:, :, :1]
        beta = cm["beta"]
        dk = dk + dkb * beta
        dbeta = jnp.sum(dkb * k, axis=-1, keepdims=True) + jnp.sum(dvb * v, axis=-1, keepdims=True)
        dv = dvb * beta
        db_col = dbeta * beta * (1.0 - beta)
        da_col = dg * cm["neg_ea"] * _sigmoid(cm["xg"])
        d_alog = jnp.sum(dg * cm["g"], axis=1, keepdims=True)
        d_dtb = jnp.sum(da_col, axis=1, keepdims=True)
        lane = lax.broadcasted_iota(jnp.int32, (c, LANES), 1)
        lane8 = lax.broadcasted_iota(jnp.int32, (8, LANES), 1)
        row8 = _rows((8, LANES))
        dab = jnp.zeros((c, LANES), F32)
        dpar = jnp.where(row8 == 2, d_og, 0.0)
        for h in range(heads):
            lo = h * HEAD_DIM
            dqkv_ref[:, lo:lo + HEAD_DIM] = dq[h]
            dqkv_ref[:, da + lo:da + lo + HEAD_DIM] = dk[h]
            dqkv_ref[:, 2 * da + lo:2 * da + lo + HEAD_DIM] = dv[h]
            dz_ref[:, lo:lo + HEAD_DIM] = dz[h]
            dab = dab + jnp.where(lane == h, da_col[h], 0.0) + jnp.where(lane == heads + h, db_col[h], 0.0)
            dpar = (dpar + jnp.where((row8 == 0) & (lane8 == h), d_alog[h], 0.0)
                    + jnp.where((row8 == 1) & (lane8 == h), d_dtb[h], 0.0))
        dab_ref[...] = dab
        dpar_ref[...] += dpar

    rev = lambda i: (n - 1 - i, 0)
    return pl.pallas_call(
        body, name="delta_bwd", grid=(n,),
        in_specs=[pl.BlockSpec((c, 3 * da), rev), pl.BlockSpec((c, da), rev), pl.BlockSpec((c, LANES), rev),
                  pl.BlockSpec((8, LANES), lambda i: (0, 0)),
                  pl.BlockSpec((1, heads, HEAD_DIM, HEAD_DIM), lambda i: (n - 1 - i, 0, 0, 0)),
                  pl.BlockSpec((1, 3, heads, c, c), lambda i: (n - 1 - i, 0, 0, 0, 0)),
                  pl.BlockSpec((1, 3, heads, c, HEAD_DIM), lambda i: (n - 1 - i, 0, 0, 0, 0)),
                  pl.BlockSpec((c, da), rev)],
        out_specs=[pl.BlockSpec((c, 3 * da), rev), pl.BlockSpec((c, da), rev), pl.BlockSpec((c, LANES), rev),
                   pl.BlockSpec((8, LANES), lambda i: (0, 0))],
        out_shape=[jax.ShapeDtypeStruct((t, 3 * da), F32), jax.ShapeDtypeStruct((t, da), F32),
                   jax.ShapeDtypeStruct((t, LANES), F32), jax.ShapeDtypeStruct((8, LANES), F32)],
        scratch_shapes=[pltpu.VMEM((heads, HEAD_DIM, HEAD_DIM), F32)],
        compiler_params=_params("arbitrary"))(qkv, z, ab, gpar, states, kept_c, kept_w, doa)


def _w_in_pieces(shard_cols, da, heads):
    a0, nab = 4 * da, 2 * heads
    d_in = 4 * shard_cols
    runs = [(0, a0, 0), (a0, a0 + nab, d_in - nab), (a0 + nab, d_in, a0)]
    pieces = []
    for j in range(4):
        lo, hi = j * shard_cols, (j + 1) * shard_cols
        for rlo, rhi, plo in runs:
            s, e = max(lo, rlo), min(hi, rhi)
            if s < e:
                pieces.append((j, s - lo, e - s, plo + (s - rlo)))
    return pieces, d_in - nab + LANES


def _w_in_pack(w4, li, da, heads):
    _, _, d, sc = w4.shape
    pieces, npk = _w_in_pieces(sc, da, heads)
    tr = _tile_rows(d, 256, SUBLANES_WIRE)

    def body(w_ref, o_ref):
        o_ref[:, npk - LANES:] = jnp.zeros((tr, LANES), o_ref.dtype)
        for j, lo, ln, dst in pieces:
            o_ref[:, dst:dst + ln] = w_ref[j, :, lo:lo + ln]

    return pl.pallas_call(
        body, name="w_in_pack", grid=(d // tr,),
        in_specs=[pl.BlockSpec((4, None, tr, sc), lambda i: (0, li, i, 0))],
        out_specs=pl.BlockSpec((tr, npk), lambda i: (i, 0)),
        out_shape=jax.ShapeDtypeStruct((d, npk), w4.dtype),
        compiler_params=_params("arbitrary"))(w4)


def _w_in_unpack(dwp, sc, da, heads):
    d, npk = dwp.shape
    pieces, _ = _w_in_pieces(sc, da, heads)
    tr = _tile_rows(d, 256)

    def body(g_ref, o_ref):
        for j, lo, ln, dst in pieces:
            o_ref[j, :, lo:lo + ln] = g_ref[:, dst:dst + ln]

    return pl.pallas_call(
        body, name="w_in_unpack", grid=(d // tr,),
        in_specs=[pl.BlockSpec((tr, npk), lambda i: (i, 0))],
        out_specs=pl.BlockSpec((4, tr, sc), lambda i: (0, i, 0)),
        out_shape=jax.ShapeDtypeStruct((4, d, sc), F32),
        compiler_params=_params("arbitrary"))(dwp)


def _block_diag(pool_w):
    g, gd, _ = pool_w.shape
    out = jnp.zeros((g * gd, g * gd), pool_w.dtype)
    for gi in range(g):
        out = lax.dynamic_update_slice(out, pool_w[gi], (gi * gd, gi * gd))
    return out


def _layer_dims(d):
    heads = (d // 2) // HEAD_DIM
    return heads, heads * HEAD_DIM, d // 4, d // 4


BIG = ("w_in", "w_gate", "w_up", "ple_proj", "w_out", "w_down", "ple_gate")


def _prepare_layer(gw, small, li):
    d = small["norm1_g"].shape[1]
    heads, da, _, _ = _layer_dims(d)
    gpar = jnp.zeros((8, LANES), F32)
    gpar = gpar.at[0, :heads].set(small["a_log"][li]).at[1, :heads].set(small["dt_bias"][li]).at[2, :].set(small["onorm_g"][li])
    return dict(norm1_g=small["norm1_g"][li][None], w_in_p=_w_in_pack(gw["w_in"], li, da, heads).astype(MM_DTYPE),
                conv_qkv=small["conv_qkv"][li], gpar=gpar, pool_bd=_block_diag(small["pool_w"][li]).astype(MM_DTYPE),
                pool_scale=small["pool_scale"][li][None], sconv_w=small["sconv_w"][li], norm2_g=small["norm2_g"][li][None])


def _layer_fwd(x0, p, gw, lw, li, tm):
    d = x0.shape[1]
    heads, da, dp, dc = _layer_dims(d)
    segs = (3 * da, da, dp, 3 * dc, LANES)
    qkv_pre, z, hp, cbcch, ab = _in_proj_fwd(x0, lw["norm1_g"], lw["w_in_p"], segs, tm)
    qkv = _qkv_conv_fwd(qkv_pre, lw["conv_qkv"], heads)
    oa, states, kept_c, kept_w = _delta_fwd(qkv, z, ab, lw["gpar"], heads)
    ob = _pool_fwd(hp, lw["pool_bd"], lw["pool_scale"], dp // POOL_GROUPS)
    oc = _sconv_fwd(cbcch, lw["sconv_w"])
    x1, h2 = _out_proj_fwd(x0, (oa, ob, oc), gw["w_out"], li, lw["norm2_g"], tm)
    x2, gp, up = _ffn_fwd(x1, h2, gw["w_gate"], gw["w_up"], gw["w_down"], li, tm)
    x3 = _ple_fwd(x2, p, gw["ple_gate"], gw["ple_proj"], li, tm)
    saved = dict(x0=x0, qkv_pre=qkv_pre, z=z, hp=hp, cbcch=cbcch, ab=ab, qkv=qkv, states=states, kept_c=kept_c, kept_w=kept_w, oa=oa, ob=ob, oc=oc,
                 x1=x1, h2=h2, gp=gp, up=up, x2=x2)
    return x3, saved


def _layer_bwd(dx3, p, gw, lw, li, sv, tm):
    d = dx3.shape[1]
    heads, da, dp, dc = _layer_dims(d)
    segs = (3 * da, da, dp, dc, dc, dc, LANES)
    gd = dp // POOL_GROUPS
    dx2, d_ple_gate, d_ple_proj = _ple_bwd(dx3, sv["x2"], p, gw["ple_gate"], gw["ple_proj"], li, tm)
    dh2, d_w_gate, d_w_up, d_w_down = _ffn_bwd(dx2, sv["h2"], sv["gp"], sv["up"], gw["w_gate"], gw["w_up"], gw["w_down"],
                                               li, min(tm, 256))
    dx1, doa, dob, doc, d_w_out, d_norm2 = _out_proj_bwd(dx2, dh2, sv["x1"], lw["norm2_g"],
                                                         (sv["oa"], sv["ob"], sv["oc"]), gw["w_out"], li, tm)
    dcb, dcc, dch, d_sconv = _sconv_bwd(sv["cbcch"], lw["sconv_w"], doc)
    dhp, d_pool_bd, d_pool_scale = _pool_bwd(sv["hp"], lw["pool_bd"], lw["pool_scale"], dob, gd)
    dqkv, dz, dab, dpar = _delta_bwd(sv["qkv"], sv["z"], sv["ab"], lw["gpar"], sv["states"], sv["kept_c"], sv["kept_w"], doa,
                                      heads)
    dqkv_pre, d_conv_qkv = _qkv_conv_bwd(sv["qkv_pre"], lw["conv_qkv"], dqkv, heads)
    dsegs = (dqkv_pre, dz, dhp, dcb, dcc, dch, dab)
    dx0, d_w_in_p, d_norm1 = _in_proj_bwd(sv["x0"], lw["norm1_g"], lw["w_in_p"], dsegs, dx1, segs, tm)
    per = LANES // gd
    bd = d_pool_bd.reshape(dp // LANES, per, gd, per, gd)
    d_pool_w = jnp.stack([bd[gi // per, gi % per, :, gi % per, :] for gi in range(POOL_GROUPS)])
    big = dict(w_in=_w_in_unpack(d_w_in_p, gw["w_in"].shape[3], da, heads), w_gate=d_w_gate, w_up=d_w_up,
               ple_proj=d_ple_proj, w_out=d_w_out, w_down=d_w_down, ple_gate=d_ple_gate)
    small = dict(norm1_g=d_norm1[0], conv_qkv=d_conv_qkv, a_log=dpar[0, :heads], dt_bias=dpar[1, :heads], onorm_g=dpar[2],
                 pool_w=d_pool_w, pool_scale=d_pool_scale[0], sconv_w=d_sconv, norm2_g=d_norm2[0])
    return dx0, big, small


def _local_step(x, p, target, gw, small, send_grads=None):
    t, d = x.shape
    depth = p.shape[0]
    tm = 512 if t % 512 == 0 else 128
    layers = [_prepare_layer(gw, small, li) for li in range(depth)]
    saved = []
    h = x
    for li in range(depth):
        h, sv = _layer_fwd(h, p[li], gw, layers[li], li, tm)
        saved.append(sv)
    dx, loss, d_final = _loss_head(h, target, small["final_g"][None], tm)
    big, sm = [None] * depth, [None] * depth
    token = None
    for li in reversed(range(depth)):
        p_li = p[li] if token is None else p[li] + token[0, 0]
        dx, big[li], sm[li] = _layer_bwd(dx, p_li, gw, layers[li], li, saved[li], tm)
        if send_grads is not None and li > 0:
            big[li], token = send_grads(li, big[li])
    small_grads = {n: jnp.stack([g[n] for g in sm]) for n in sm[0]}
    small_grads["final_g"] = d_final[0]
    return loss[0, 0], dx, big, small_grads


def _coords():
    return lax.axis_index("x"), lax.axis_index("y"), lax.axis_index("c")


def _other_chips(x, y):
    return [(1 - x, y), (x, 1 - y), (1 - x, 1 - y)]


def _place_shards(ws, me_idx):
    nt = len(ws)
    depth = ws[0].shape[0]

    def body(me_ref, *refs):
        for w_ref, o_ref in zip(refs[:nt], refs[nt:]):
            o_ref[...] = w_ref[...].astype(o_ref.dtype)

    return pl.pallas_call(
        body, name="place_shards",
        grid_spec=pltpu.PrefetchScalarGridSpec(
            num_scalar_prefetch=1, grid=(depth, 2),
            in_specs=[pl.BlockSpec((None, w.shape[1] // 2, w.shape[2]), lambda l, i, me_ref: (l, i, 0)) for w in ws],
            out_specs=[pl.BlockSpec((None, None, w.shape[1] // 2, w.shape[2]), lambda l, i, me_ref: (me_ref[0], l, i, 0))
                       for w in ws]),
        out_shape=[jax.ShapeDtypeStruct((4,) + w.shape, WIRE_DTYPE) for w in ws],
        compiler_params=_params("arbitrary", "arbitrary"))(me_idx, *ws)


def _all_gather_chips(placed):
    nt = len(placed)

    def body(*refs):
        out_refs = refs[nt:2 * nt]
        send_sems, recv_sems = refs[2 * nt:]
        x, y, c = _coords()
        sibling = (x, y, 1 - c)
        chips = _other_chips(x, y)

        def copy(k, t, block, to):
            px, py, pc = block
            blk = out_refs[t].at[2 * px + py, pc]
            return pltpu.make_async_remote_copy(src_ref=blk, dst_ref=blk, send_sem=send_sems.at[k, t],
                                                recv_sem=recv_sems.at[k, t], device_id=to, device_id_type=MESH)

        first = [copy(j, t, (x, y, c), (*chip, c)) for j, chip in enumerate(chips) for t in range(nt)]
        for cp in first:
            cp.start()
        passed = []
        for j, chip in enumerate(chips):
            for t in range(nt):
                copy(j, t, (*chip, c), (x, y, c)).wait_recv()
                fwd = copy(3 + j, t, (*chip, c), sibling)
                fwd.start()
                passed.append(fwd)
        for j, chip in enumerate(chips):
            for t in range(nt):
                copy(3 + j, t, (*chip, 1 - c), (x, y, c)).wait_recv()
        for cp in first + passed:
            cp.wait_send()

    return pl.pallas_call(
        body, name="all_gather_chips", out_shape=[jax.ShapeDtypeStruct(a.shape, a.dtype) for a in placed],
        in_specs=[ANY] * nt, out_specs=[ANY] * nt, input_output_aliases={t: t for t in range(nt)},
        scratch_shapes=[pltpu.SemaphoreType.DMA((6, nt)), pltpu.SemaphoreType.DMA((6, nt))],
    )(*placed)


def _sibling_swap_half(gs):
    nt = len(gs)

    def body(*refs):
        g_refs, out_refs = refs[:nt], refs[nt:2 * nt]
        send_sems, recv_sems = refs[2 * nt:]
        x, y, c = _coords()
        cps = []
        for t in range(nt):
            rh = g_refs[t].shape[1] // 2
            cps.append(pltpu.make_async_remote_copy(src_ref=g_refs[t].at[:, pl.ds((1 - c) * rh, rh)], dst_ref=out_refs[t],
                                                    send_sem=send_sems.at[t], recv_sem=recv_sems.at[t], device_id=(x, y, 1 - c),
                                                    device_id_type=MESH))
        for cp in cps:
            cp.start()
        for cp in cps:
            cp.wait()

    return pl.pallas_call(
        body, name="sibling_swap_half",
        out_shape=[jax.ShapeDtypeStruct((g.shape[0], g.shape[1] // 2, g.shape[2]), g.dtype) for g in gs],
        in_specs=[ANY] * nt, out_specs=[ANY] * nt,
        scratch_shapes=[pltpu.SemaphoreType.DMA((nt,)), pltpu.SemaphoreType.DMA((nt,))])(*gs)


def _add_my_halves(gs, others, c_idx):
    nt = len(gs)

    def body(c_ref, *refs):
        for g_ref, o_ref, out_ref in zip(refs[:nt], refs[nt:2 * nt], refs[2 * nt:]):
            out_ref[...] = (g_ref[...].astype(F32) + o_ref[...].astype(F32)).astype(out_ref.dtype)

    def quarter(g):
        return pl.BlockSpec((None, g.shape[1] // 4, g.shape[2]), lambda j, i, c_ref: (j, i, 0))

    return pl.pallas_call(
        body, name="add_my_halves",
        grid_spec=pltpu.PrefetchScalarGridSpec(
            num_scalar_prefetch=1, grid=(4, 2),
            in_specs=[pl.BlockSpec((None, g.shape[1] // 4, g.shape[2]), lambda j, i, c_ref: (j, 2 * c_ref[0] + i, 0)) for g in gs]
                     + [quarter(g) for g in gs],
            out_specs=[quarter(g) for g in gs]),
        out_shape=[jax.ShapeDtypeStruct((4, g.shape[1] // 2, g.shape[2]), WIRE_DTYPE) for g in gs],
        compiler_params=_params("arbitrary", "arbitrary"))(c_idx, *gs, *others)


def _exchange_chips(parts):
    nt = len(parts)

    def body(*refs):
        p_refs, out_refs = refs[:nt], refs[nt:2 * nt]
        send_sems, recv_sems = refs[2 * nt:]
        x, y, c = _coords()
        chips = _other_chips(x, y)

        def copy(j, t):
            cx, cy = chips[j]
            return pltpu.make_async_remote_copy(src_ref=p_refs[t].at[2 * cx + cy], dst_ref=out_refs[t].at[j],
                                                send_sem=send_sems.at[j, t], recv_sem=recv_sems.at[j, t], device_id=(cx, cy, c),
                                                device_id_type=MESH)

        sends = [copy(j, t) for j in range(3) for t in range(nt)]
        for cp in sends:
            cp.start()
        for cp in sends:
            cp.wait_recv()
        for cp in sends:
            cp.wait_send()

    return pl.pallas_call(
        body, name="exchange_chips", out_shape=[jax.ShapeDtypeStruct((3,) + p.shape[1:], p.dtype) for p in parts],
        in_specs=[ANY] * nt, out_specs=[ANY] * nt,
        scratch_shapes=[pltpu.SemaphoreType.DMA((3, nt)), pltpu.SemaphoreType.DMA((3, nt))])(*parts)


def _exchange_start(parts):
    nt = len(parts)
    nc = 3 * nt
    in_hbm = [pltpu.with_memory_space_constraint(p, pltpu.HBM) for p in parts]
    lands = [pltpu.with_memory_space_constraint(lax.empty((3,) + p.shape[1:], p.dtype), pltpu.HBM) for p in parts]

    def body(*refs):
        p_refs, l_refs = refs[:nt], refs[nt:2 * nt]
        send_sems, recv_sems, token = refs[2 * nt:2 * nt + nc], refs[2 * nt + nc:2 * nt + 2 * nc], refs[-1]
        x, y, c = _coords()
        for j, (cx, cy) in enumerate(_other_chips(x, y)):
            for t in range(nt):
                pltpu.make_async_remote_copy(src_ref=p_refs[t].at[2 * cx + cy], dst_ref=l_refs[t].at[j],
                                             send_sem=send_sems[j * nt + t], recv_sem=recv_sems[j * nt + t], device_id=(cx, cy, c),
                                             device_id_type=MESH).start()
        token[...] = jnp.zeros_like(token)

    outs = pl.pallas_call(
        body, name="exchange_start",
        out_shape=(*[pltpu.SemaphoreType.DMA(())] * (2 * nc), *[pltpu.HBM(a.shape, a.dtype) for a in in_hbm + lands],
                   jax.ShapeDtypeStruct((8, LANES), F32)),
        in_specs=[HBM] * (2 * nt), out_specs=(*[SEM] * (2 * nc), *[HBM] * (2 * nt), pl.BlockSpec(memory_space=pltpu.VMEM)),
        input_output_aliases={t: 2 * nc + t for t in range(2 * nt)},
        compiler_params=pltpu.CompilerParams(has_side_effects=pltpu.SideEffectType.DATAFLOW_SIDE_EFFECTING))(*in_hbm, *lands)
    return (list(outs[:nc]), list(outs[nc:2 * nc]), list(outs[2 * nc:2 * nc + nt]), list(outs[2 * nc + nt:2 * nc + 2 * nt]),
            outs[-1])


def _exchange_wait(send_sems, recv_sems, parts, lands, after):
    nt = len(parts)
    nc = 3 * nt

    def body(*refs):
        p_refs, l_refs = refs[:nt], refs[nt:2 * nt]
        send_ref, recv_ref = refs[2 * nt:2 * nt + nc], refs[2 * nt + nc:2 * nt + 2 * nc]
        x, y, c = _coords()
        for j, (cx, cy) in enumerate(_other_chips(x, y)):
            for t in range(nt):
                cp = pltpu.make_async_remote_copy(src_ref=p_refs[t].at[2 * cx + cy], dst_ref=l_refs[t].at[j],
                                                  send_sem=send_ref[j * nt + t], recv_sem=recv_ref[j * nt + t], device_id=(cx, cy, c),
                                                  device_id_type=MESH)
                cp.wait_send()
                cp.wait_recv()

    outs = pl.pallas_call(
        body, name="exchange_wait", out_shape=tuple(pltpu.HBM(a.shape, a.dtype) for a in parts + lands),
        in_specs=[HBM] * (2 * nt) + [SEM] * (2 * nc) + [ANY], out_specs=tuple([HBM] * (2 * nt)),
        input_output_aliases={t: t for t in range(2 * nt)},
        compiler_params=pltpu.CompilerParams(has_side_effects=pltpu.SideEffectType.DATAFLOW_SIDE_EFFECTING),
    )(*parts, *lands, *send_sems, *recv_sems, after)
    return list(outs[:nt]), list(outs[nt:])


def _sum_into(pairs, recvs, idx, li, depth, accs):
    nt = len(pairs)

    def body(idx_ref, *refs):
        for p_ref, r_ref, out_ref in zip(refs[:nt], refs[nt:2 * nt], refs[-nt:]):
            out_ref[...] = p_ref[...].astype(F32) + r_ref[0].astype(F32) + r_ref[1].astype(F32) + r_ref[2].astype(F32)

    in_specs = ([pl.BlockSpec((None, p.shape[1] // 2, p.shape[2]), lambda i, idx_ref: (idx_ref[0], i, 0)) for p in pairs]
                + [pl.BlockSpec((3, p.shape[1] // 2, p.shape[2]), lambda i, idx_ref: (0, i, 0)) for p in pairs])
    args = [idx, *pairs, *recvs]
    aliases = {}
    if accs[0] is not None:
        in_specs += [ANY] * nt
        args += list(accs)
        aliases = {1 + 2 * nt + t: t for t in range(nt)}
    return pl.pallas_call(
        body, name="sum_into",
        grid_spec=pltpu.PrefetchScalarGridSpec(
            num_scalar_prefetch=1, grid=(2,), in_specs=in_specs,
            out_specs=[pl.BlockSpec((None, p.shape[1] // 2, p.shape[2]), lambda i, idx_ref: (li, 2 * idx_ref[1] + i, 0))
                       for p in pairs]),
        out_shape=[jax.ShapeDtypeStruct((depth, 2 * p.shape[1], p.shape[2]), F32) for p in pairs],
        input_output_aliases=aliases, compiler_params=_params("arbitrary"))(*args)


def _sum_slots(parts):
    n, rows, cols = parts.shape
    tr = _tile_rows(rows, 512, SUBLANES_WIRE)

    def body(p_ref, out_ref):
        acc = p_ref[0].astype(F32)
        for s in range(1, n):
            acc = acc + p_ref[s].astype(F32)
        out_ref[...] = acc

    return pl.pallas_call(
        body, name="sum_slots", grid=(rows // tr,),
        in_specs=[pl.BlockSpec((n, tr, cols), lambda i: (0, i, 0))],
        out_specs=pl.BlockSpec((tr, cols), lambda i: (i, 0)),
        out_shape=jax.ShapeDtypeStruct((rows, cols), F32),
        compiler_params=_params("arbitrary"))(parts)


def _sibling_share(gs):
    nt = len(gs)
    depth = gs[0].shape[0]

    def body(*refs):
        out_refs = refs[nt:2 * nt]
        send_sems, recv_sems = refs[2 * nt:]
        x, y, c = _coords()
        sends, recvs = [], []
        for t in range(nt):
            rh = out_refs[t].shape[1] // 2
            for li in range(depth):
                mine = out_refs[t].at[li, pl.ds(c * rh, rh)]
                theirs = out_refs[t].at[li, pl.ds((1 - c) * rh, rh)]
                sems = dict(send_sem=send_sems.at[t, li], recv_sem=recv_sems.at[t, li], device_id=(x, y, 1 - c), device_id_type=MESH)
                sends.append(pltpu.make_async_remote_copy(src_ref=mine, dst_ref=mine, **sems))
                recvs.append(pltpu.make_async_remote_copy(src_ref=theirs, dst_ref=theirs, **sems))
        for cp in sends:
            cp.start()
        for cp in recvs:
            cp.wait_recv()
        for cp in sends:
            cp.wait_send()

    return pl.pallas_call(
        body, name="sibling_share", out_shape=[jax.ShapeDtypeStruct(g.shape, g.dtype) for g in gs],
        in_specs=[ANY] * nt, out_specs=[ANY] * nt, input_output_aliases={t: t for t in range(nt)},
        scratch_shapes=[pltpu.SemaphoreType.DMA((nt, depth)), pltpu.SemaphoreType.DMA((nt, depth))])(*gs)


def _all_gather_devices(buf):
    def body(b_ref, out_ref, send_sems, recv_sems, local_sem):
        x, y, c = _coords()
        me = 4 * x + 2 * y + c
        mine = pltpu.make_async_copy(b_ref, out_ref.at[me], local_sem)
        mine.start()
        peers = []
        for k in range(1, 8):
            fx, fy, fc = (k >> 2) & 1, (k >> 1) & 1, k & 1
            peers.append((x ^ fx, y ^ fy, c ^ fc))
        sends = [pltpu.make_async_remote_copy(src_ref=b_ref, dst_ref=out_ref.at[me], send_sem=send_sems.at[k],
                                              recv_sem=recv_sems.at[k], device_id=peer, device_id_type=MESH)
                 for k, peer in enumerate(peers)]
        for cp in sends:
            cp.start()
        for k, (px, py, pc) in enumerate(peers):
            pltpu.make_async_remote_copy(src_ref=b_ref, dst_ref=out_ref.at[4 * px + 2 * py + pc], send_sem=send_sems.at[k],
                                         recv_sem=recv_sems.at[k], device_id=(px, py, pc), device_id_type=MESH).wait_recv()
        for cp in sends:
            cp.wait_send()
        mine.wait()

    return pl.pallas_call(
        body, name="all_gather_devices", out_shape=jax.ShapeDtypeStruct((8,) + buf.shape, buf.dtype),
        in_specs=[ANY], out_specs=ANY,
        scratch_shapes=[pltpu.SemaphoreType.DMA((7,)), pltpu.SemaphoreType.DMA((7,)), pltpu.SemaphoreType.DMA(())])(buf)


def _pair_sums(big_grads, c_idx):
    gs = [big_grads[n] for n in BIG]
    return _add_my_halves(gs, _sibling_swap_half(gs), c_idx)


SMALL_SHARDED = ("conv_qkv", "sconv_w")
REPLICATED = ("norm1_g", "a_log", "dt_bias", "onorm_g", "pool_w", "pool_scale", "norm2_g", "final_g")
ALL_WEIGHTS = ("norm1_g", "w_in", "conv_qkv", "a_log", "dt_bias", "onorm_g", "pool_w", "pool_scale", "sconv_w", "w_out",
               "norm2_g", "w_gate", "w_up", "w_down", "ple_proj", "ple_gate", "final_g")


def _pad_rows(flat, row_multiple):
    m = flat.shape[0]
    r = -(-m // (LANES * row_multiple)) * row_multiple
    return jnp.pad(flat, (0, r * LANES - m)).reshape(r, LANES)


def _adamw(w, g, m, v):
    shape = w.shape
    cols = shape[-1]
    rows = w.size // cols
    tr = _tile_rows(rows, 512)
    c1 = 1.0 / (1.0 - ADAM_B1 ** ADAM_STEP)
    c2 = 1.0 / (1.0 - ADAM_B2 ** ADAM_STEP)

    def body(w_ref, g_ref, m_ref, v_ref, d_ref, nm_ref, nv_ref, go_ref):
        gv = g_ref[...]
        nm = ADAM_B1 * m_ref[...] + (1.0 - ADAM_B1) * gv
        nv = ADAM_B2 * v_ref[...] + (1.0 - ADAM_B2) * (gv * gv)
        nm_ref[...] = nm
        nv_ref[...] = nv
        go_ref[...] = gv
        d_ref[...] = -ADAM_LR * ((nm * c1) / (jnp.sqrt(nv * c2) + ADAM_EPS) + ADAM_WD * w_ref[...])

    spec = pl.BlockSpec((tr, cols), lambda i: (i, 0))
    outs = pl.pallas_call(
        body, name="adamw", grid=(rows // tr,), in_specs=[spec] * 4, out_specs=[spec] * 4,
        out_shape=[jax.ShapeDtypeStruct((rows, cols), F32)] * 4,
        compiler_params=_params("arbitrary"))(*[a.reshape(rows, cols) for a in (w, g, m, v)])
    return tuple(o.reshape(shape) for o in outs)


def kernel(x, p, norm1_g, w_in, conv_qkv, a_log, dt_bias, onorm_g, pool_w, pool_scale, sconv_w, w_out, norm2_g, w_gate, w_up, w_down, ple_proj, ple_gate, final_g, loss_target, m_norm1_g, m_w_in, m_conv_qkv, m_a_log, m_dt_bias, m_onorm_g, m_pool_w, m_pool_scale, m_sconv_w, m_w_out, m_norm2_g, m_w_gate, m_w_up, m_w_down, m_ple_proj, m_ple_gate, m_final_g, v_norm1_g, v_w_in, v_conv_qkv, v_a_log, v_dt_bias, v_onorm_g, v_pool_w, v_pool_scale, v_sconv_w, v_w_out, v_norm2_g, v_w_gate, v_w_up, v_w_down, v_ple_proj, v_ple_gate, v_final_g):
    weights = dict(zip(ALL_WEIGHTS, (norm1_g, w_in, conv_qkv, a_log, dt_bias, onorm_g, pool_w, pool_scale, sconv_w, w_out,
                                     norm2_g, w_gate, w_up, w_down, ple_proj, ple_gate, final_g)))
    mom_m = dict(zip(ALL_WEIGHTS, (m_norm1_g, m_w_in, m_conv_qkv, m_a_log, m_dt_bias, m_onorm_g, m_pool_w, m_pool_scale,
                                   m_sconv_w, m_w_out, m_norm2_g, m_w_gate, m_w_up, m_w_down, m_ple_proj, m_ple_gate, m_final_g)))
    mom_v = dict(zip(ALL_WEIGHTS, (v_norm1_g, v_w_in, v_conv_qkv, v_a_log, v_dt_bias, v_onorm_g, v_pool_w, v_pool_scale,
                                   v_sconv_w, v_w_out, v_norm2_g, v_w_gate, v_w_up, v_w_down, v_ple_proj, v_ple_gate, v_final_g)))
    c_idx = lax.axis_index("c").astype(jnp.int32).reshape(1)
    chip = (2 * lax.axis_index("x") + lax.axis_index("y")).astype(jnp.int32)
    me_idx = chip.reshape(1)
    idx = jnp.stack([chip, lax.axis_index("c").astype(jnp.int32)])
    depth = p.shape[0]

    gathered = _all_gather_chips(_place_shards([weights[n] for n in BIG], me_idx))
    gw = dict(zip(BIG, gathered))
    small = {n: weights[n] for n in REPLICATED}
    sflat = _pad_rows(jnp.concatenate([weights[n].reshape(-1) for n in SMALL_SHARDED]), 8)
    sgath = _all_gather_devices(sflat)[0::2].reshape(4, -1)
    off = 0
    for n in SMALL_SHARDED:
        shp = weights[n].shape
        part = sgath[:, off:off + weights[n].size].reshape((4,) + shp)
        small[n] = jnp.moveaxis(part, 0, -2).reshape(shp[:-1] + (4 * shp[-1],))
        off += weights[n].size

    def send_grads(li, grads):
        started = _exchange_start(_pair_sums(grads, c_idx))
        return started[:4], started[4]

    loss_local, dx, big_grads, small_grads = _local_step(x[0], p[:, 0], loss_target[0], gw, small, send_grads)

    accs = [None] * len(BIG)
    for li in reversed(range(depth)):
        if li > 0:
            pairs, recvs = _exchange_wait(*big_grads[li], dx)
        else:
            pairs = _pair_sums(big_grads[li], c_idx)
            recvs = _exchange_chips(pairs)
        accs = _sum_into(pairs, recvs, idx, li, depth, accs)
    gshard = dict(zip(BIG, _sibling_share(accs)))

    rnames = REPLICATED + SMALL_SHARDED
    rflat = _pad_rows(jnp.concatenate([small_grads[n].reshape(-1) for n in rnames]), 8)
    rsum = _sum_slots(_all_gather_devices(rflat)).reshape(-1)
    off = 0
    for n in rnames:
        whole = rsum[off:off + small_grads[n].size].reshape(small_grads[n].shape)
        off += small_grads[n].size
        if n in SMALL_SHARDED:
            cols = weights[n].shape[-1]
            whole = lax.dynamic_slice_in_dim(whole, chip * cols, cols, axis=whole.ndim - 1)
        gshard[n] = whole

    loss = lax.psum(loss_local, ("x", "y", "c"))
    deltas, new_m, new_v, grad_out = {}, {}, {}, {}
    for n in ALL_WEIGHTS:
        deltas[n], new_m[n], new_v[n], grad_out[n] = _adamw(weights[n], gshard[n], mom_m[n], mom_v[n])
    return (loss, dx[None], *[grad_out[n] for n in ALL_WEIGHTS], *[deltas[n] for n in ALL_WEIGHTS],
            *[new_m[n] for n in ALL_WEIGHTS], *[new_v[n] for n in ALL_WEIGHTS])
```

```python
import jax
import jax.numpy as jnp
from jax import lax
from jax.experimental import pallas as pl
from jax.experimental.pallas import tpu as pltpu

F32 = jnp.float32
MM_DTYPE = jnp.bfloat16
WIRE_DTYPE = jnp.bfloat16
HI = lax.Precision.HIGHEST
EPS = 1e-6
HEAD_DIM = 128
CHUNK = 64
QKV_CONV_WIDTH = 4
SCONV_WIDTH = 3
POOL_GROUPS = 4
LANES = 128
SUBLANES_WIRE = 16
VMEM_LIMIT_BYTES = 56 * 1024 * 1024
ADAM_LR, ADAM_B1, ADAM_B2, ADAM_EPS, ADAM_WD, ADAM_STEP = 0.001, 0.9, 0.999, 1e-08, 0.01, 10
MESH = pl.DeviceIdType.MESH
ANY = pl.BlockSpec(memory_space=pl.ANY)
HBM = pl.BlockSpec(memory_space=pltpu.HBM)
SEM = pl.BlockSpec(memory_space=pltpu.SEMAPHORE)


def _params(*sem):
    return pltpu.CompilerParams(vmem_limit_bytes=VMEM_LIMIT_BYTES, dimension_semantics=sem if sem else None)


def _mm(a, b):
    return jnp.dot(a.astype(MM_DTYPE), b.astype(MM_DTYPE), preferred_element_type=F32)


def _mm_nt(a, b):
    return lax.dot_general(a.astype(MM_DTYPE), b.astype(MM_DTYPE), (((1,), (1,)), ((), ())), preferred_element_type=F32)


def _mm_tn(a, b):
    return lax.dot_general(a.astype(MM_DTYPE), b.astype(MM_DTYPE), (((0,), (0,)), ((), ())), preferred_element_type=F32)


def _hmm(a, b):
    return jnp.dot(a, b, preferred_element_type=F32, precision=HI)


def _hmm_nt(a, b):
    return lax.dot_general(a, b, (((1,), (1,)), ((), ())), preferred_element_type=F32, precision=HI)


def _hmm_tn(a, b):
    return lax.dot_general(a, b, (((0,), (0,)), ((), ())), preferred_element_type=F32, precision=HI)


def _sigmoid(x):
    return 1.0 / (1.0 + jnp.exp(-x))


def _dsilu(x, s):
    return s * (1.0 + x * (1.0 - s))


def _rows(shape):
    return lax.broadcasted_iota(jnp.int32, shape, 0)


def _shift_down(x, s):
    if s == 0:
        return x
    return jnp.where(_rows(x.shape) >= s, pltpu.roll(x, s, 0), 0.0)


def _shift_up(x, s):
    if s == 0:
        return x
    t = x.shape[0]
    return jnp.where(_rows(x.shape) < t - s, pltpu.roll(x, t - s, 0), 0.0)


def _rms_fwd(x):
    r = lax.rsqrt(jnp.mean(x * x, axis=-1, keepdims=True) + EPS)
    return x * r, r


def _rms_bwd(dxn, xn, r):
    return r * (dxn - xn * jnp.mean(dxn * xn, axis=-1, keepdims=True))


def _tile_rows(n, cap, mult=8):
    best = None
    for d in range(mult, min(n, cap) + 1, mult):
        if n % d == 0:
            best = d
    return best if best is not None else n


def _in_proj_fwd(x, g1, wp, segs, tm):
    t, d = x.shape
    npk = wp.shape[1]

    def body(x_ref, g_ref, w_ref, *o_refs):
        xn, _ = _rms_fwd(x_ref[...])
        h = (xn * g_ref[...]).astype(w_ref.dtype)
        off = 0
        for o_ref, wd in zip(o_refs, segs):
            o_ref[...] = jnp.dot(h, w_ref[:, off:off + wd], preferred_element_type=F32)
            off += wd

    return pl.pallas_call(
        body, name="in_proj_fwd", grid=(t // tm,),
        in_specs=[pl.BlockSpec((tm, d), lambda i: (i, 0)), pl.BlockSpec((1, d), lambda i: (0, 0)),
                  pl.BlockSpec((d, npk), lambda i: (0, 0))],
        out_specs=[pl.BlockSpec((tm, wd), lambda i: (i, 0)) for wd in segs],
        out_shape=[jax.ShapeDtypeStruct((t, wd), F32) for wd in segs],
        compiler_params=_params("arbitrary"))(x, g1, wp)


def _in_proj_bwd(x, g1, wp, dsegs, dx_res, segs, tm):
    t, d = x.shape
    npk = wp.shape[1]
    nseg = len(segs)

    def body(x_ref, g_ref, w_ref, *rest):
        ds_refs = rest[:nseg]
        dxr_ref, dx_ref, dw_ref, dg_ref = rest[nseg:]
        i = pl.program_id(0)

        @pl.when(i == 0)
        def _():
            dw_ref[...] = jnp.zeros_like(dw_ref)
            dg_ref[...] = jnp.zeros_like(dg_ref)

        xn, r = _rms_fwd(x_ref[...])
        g = g_ref[...]
        h = (xn * g).astype(w_ref.dtype)
        dh = jnp.zeros((tm, d), F32)
        off = 0
        for ds_ref, wd in zip(ds_refs, segs):
            dsv = ds_ref[...].astype(w_ref.dtype)
            dh = dh + lax.dot_general(dsv, w_ref[:, off:off + wd], (((1,), (1,)), ((), ())), preferred_element_type=F32)
            dw_ref[:, off:off + wd] += lax.dot_general(h, dsv, (((0,), (0,)), ((), ())), preferred_element_type=F32)
            off += wd
        dg_ref[...] += jnp.sum(dh * xn, axis=0, keepdims=True)
        dx_ref[...] = dxr_ref[...] + _rms_bwd(dh * g, xn, r)

    return pl.pallas_call(
        body, name="in_proj_bwd", grid=(t // tm,),
        in_specs=[pl.BlockSpec((tm, d), lambda i: (i, 0)), pl.BlockSpec((1, d), lambda i: (0, 0)),
                  pl.BlockSpec((d, npk), lambda i: (0, 0))]
                 + [pl.BlockSpec((tm, wd), lambda i: (i, 0)) for wd in segs]
                 + [pl.BlockSpec((tm, d), lambda i: (i, 0))],
        out_specs=[pl.BlockSpec((tm, d), lambda i: (i, 0)), pl.BlockSpec((d, npk), lambda i: (0, 0)),
                   pl.BlockSpec((1, d), lambda i: (0, 0))],
        out_shape=[jax.ShapeDtypeStruct((t, d), F32), jax.ShapeDtypeStruct((d, npk), F32),
                   jax.ShapeDtypeStruct((1, d), F32)],
        compiler_params=_params("arbitrary"))(x, g1, wp, *dsegs, dx_res)


def _out_proj_fwd(x0, mix, wo, g2, tm):
    t, d = x0.shape
    dq = wo.shape[1]
    widths = [m.shape[1] for m in mix]

    def body(x_ref, *rest):
        m_refs = rest[:len(mix)]
        w_ref, g_ref, x1_ref, h2_ref = rest[len(mix):]
        acc = x_ref[...]
        off = 0
        for m_ref, wd in zip(m_refs, widths):
            for k in range(wd // dq):
                acc = acc + jnp.dot(m_ref[:, k * dq:(k + 1) * dq].astype(w_ref.dtype), w_ref[off // dq + k],
                                    preferred_element_type=F32)
            off += wd
        x1_ref[...] = acc
        xn, _ = _rms_fwd(acc)
        h2_ref[...] = (xn * g_ref[...]).astype(h2_ref.dtype)

    return pl.pallas_call(
        body, name="out_proj_fwd", grid=(t // tm,),
        in_specs=[pl.BlockSpec((tm, d), lambda i: (i, 0))]
                 + [pl.BlockSpec((tm, wd), lambda i: (i, 0)) for wd in widths]
                 + [pl.BlockSpec((4, dq, d), lambda i: (0, 0, 0)), pl.BlockSpec((1, d), lambda i: (0, 0))],
        out_specs=[pl.BlockSpec((tm, d), lambda i: (i, 0)), pl.BlockSpec((tm, d), lambda i: (i, 0))],
        out_shape=[jax.ShapeDtypeStruct((t, d), F32), jax.ShapeDtypeStruct((t, d), MM_DTYPE)],
        compiler_params=_params("arbitrary"))(x0, *mix, wo, g2)


def _out_proj_bwd(dx2, dh2, x1, g2, mix, wo, tm):
    t, d = x1.shape
    dq = wo.shape[1]
    widths = [m.shape[1] for m in mix]
    nm = len(mix)

    def body(dx2_ref, dh2_ref, x1_ref, g_ref, *rest):
        m_refs = rest[:nm]
        w_ref = rest[nm]
        dx1_ref = rest[nm + 1]
        dm_refs = rest[nm + 2:nm + 2 + nm]
        dw_ref, dg_ref = rest[nm + 2 + nm:]
        i = pl.program_id(0)

        @pl.when(i == 0)
        def _():
            dw_ref[...] = jnp.zeros_like(dw_ref)
            dg_ref[...] = jnp.zeros_like(dg_ref)

        xn, r = _rms_fwd(x1_ref[...])
        dh2v = dh2_ref[...]
        dg_ref[...] += jnp.sum(dh2v * xn, axis=0, keepdims=True)
        dx1 = dx2_ref[...] + _rms_bwd(dh2v * g_ref[...], xn, r)
        dx1_ref[...] = dx1
        dx1c = dx1.astype(w_ref.dtype)
        off = 0
        for m_ref, dm_ref, wd in zip(m_refs, dm_refs, widths):
            for k in range(wd // dq):
                j = off // dq + k
                cols = slice(k * dq, (k + 1) * dq)
                dm_ref[:, cols] = lax.dot_general(dx1c, w_ref[j], (((1,), (1,)), ((), ())), preferred_element_type=F32)
                dw_ref[j] += lax.dot_general(m_ref[:, cols].astype(w_ref.dtype), dx1c, (((0,), (0,)), ((), ())),
                                             preferred_element_type=F32)
            off += wd

    tile = lambda wd: pl.BlockSpec((tm, wd), lambda i: (i, 0))
    return pl.pallas_call(
        body, name="out_proj_bwd", grid=(t // tm,),
        in_specs=[tile(d), tile(d), tile(d), pl.BlockSpec((1, d), lambda i: (0, 0))]
                 + [tile(wd) for wd in widths] + [pl.BlockSpec((4, dq, d), lambda i: (0, 0, 0))],
        out_specs=[tile(d)] + [tile(wd) for wd in widths]
                  + [pl.BlockSpec((4, dq, d), lambda i: (0, 0, 0)), pl.BlockSpec((1, d), lambda i: (0, 0))],
        out_shape=[jax.ShapeDtypeStruct((t, d), F32)] + [jax.ShapeDtypeStruct((t, wd), F32) for wd in widths]
                  + [jax.ShapeDtypeStruct((4, dq, d), F32), jax.ShapeDtypeStruct((1, d), F32)],
        compiler_params=_params("arbitrary"))(dx2, dh2, x1, g2, *mix, wo)


def _ffn_fwd(x1, h2, wg, wu, wd, tm):
    t, d = x1.shape
    fs = wg.shape[2]

    def body(x1_ref, h2_ref, wg_ref, wu_ref, wd_ref, x2_ref, gp_ref, up_ref):
        @pl.when(pl.program_id(1) == 0)
        def _():
            x2_ref[...] = x1_ref[...]

        h = h2_ref[...]
        gp = jnp.dot(h, wg_ref[...], preferred_element_type=F32)
        up = jnp.dot(h, wu_ref[...], preferred_element_type=F32)
        gp_ref[...] = gp
        up_ref[...] = up
        ff = gp * _sigmoid(gp) * up
        x2_ref[...] += jnp.dot(ff.astype(wd_ref.dtype), wd_ref[...], preferred_element_type=F32)

    return pl.pallas_call(
        body, name="ffn_fwd", grid=(t // tm, 4),
        in_specs=[pl.BlockSpec((tm, d), lambda i, j: (i, 0)), pl.BlockSpec((tm, d), lambda i, j: (i, 0)),
                  pl.BlockSpec((None, d, fs), lambda i, j: (j, 0, 0)),
                  pl.BlockSpec((None, d, fs), lambda i, j: (j, 0, 0)),
                  pl.BlockSpec((None, fs, d), lambda i, j: (j, 0, 0))],
        out_specs=[pl.BlockSpec((tm, d), lambda i, j: (i, 0)), pl.BlockSpec((None, tm, fs), lambda i, j: (j, i, 0)),
                   pl.BlockSpec((None, tm, fs), lambda i, j: (j, i, 0))],
        out_shape=[jax.ShapeDtypeStruct((t, d), F32), jax.ShapeDtypeStruct((4, t, fs), F32),
                   jax.ShapeDtypeStruct((4, t, fs), F32)],
        compiler_params=_params("arbitrary", "arbitrary"))(x1, h2, wg, wu, wd)


def _ffn_bwd(dx2, h2, gp, up, wg, wu, wd, tm):
    t, d = dx2.shape
    fs = wg.shape[2]

    def body(dx2_ref, h2_ref, gp_ref, up_ref, wg_ref, wu_ref, wd_ref, dh2_ref, dwg_ref, dwu_ref, dwd_ref):
        j, i = pl.program_id(0), pl.program_id(1)

        @pl.when(i == 0)
        def _():
            dwg_ref[...] = jnp.zeros_like(dwg_ref)
            dwu_ref[...] = jnp.zeros_like(dwu_ref)
            dwd_ref[...] = jnp.zeros_like(dwd_ref)

        cdt = wg_ref.dtype
        h = h2_ref[...]
        gpv, upv = gp_ref[...], up_ref[...]
        s = _sigmoid(gpv)
        silu = gpv * s
        dx2c = dx2_ref[...].astype(cdt)
        dff = lax.dot_general(dx2c, wd_ref[...], (((1,), (1,)), ((), ())), preferred_element_type=F32)
        dwd_ref[...] += lax.dot_general((silu * upv).astype(cdt), dx2c, (((0,), (0,)), ((), ())), preferred_element_type=F32)
        dup = (dff * silu).astype(cdt)
        dgp = (dff * upv * _dsilu(gpv, s)).astype(cdt)
        dwg_ref[...] += lax.dot_general(h, dgp, (((0,), (0,)), ((), ())), preferred_element_type=F32)
        dwu_ref[...] += lax.dot_general(h, dup, (((0,), (0,)), ((), ())), preferred_element_type=F32)
        dh = (lax.dot_general(dgp, wg_ref[...], (((1,), (1,)), ((), ())), preferred_element_type=F32)
              + lax.dot_general(dup, wu_ref[...], (((1,), (1,)), ((), ())), preferred_element_type=F32))
        rows = pl.ds(pl.multiple_of(i * tm, tm), tm)

        @pl.when(j == 0)
        def _():
            dh2_ref[rows, :] = dh

        @pl.when(j != 0)
        def _():
            dh2_ref[rows, :] += dh

    return pl.pallas_call(
        body, name="ffn_bwd", grid=(4, t // tm),
        in_specs=[pl.BlockSpec((tm, d), lambda j, i: (i, 0)), pl.BlockSpec((tm, d), lambda j, i: (i, 0)),
                  pl.BlockSpec((None, tm, fs), lambda j, i: (j, i, 0)), pl.BlockSpec((None, tm, fs), lambda j, i: (j, i, 0)),
                  pl.BlockSpec((None, d, fs), lambda j, i: (j, 0, 0)),
                  pl.BlockSpec((None, d, fs), lambda j, i: (j, 0, 0)),
                  pl.BlockSpec((None, fs, d), lambda j, i: (j, 0, 0))],
        out_specs=[pl.BlockSpec((t, d), lambda j, i: (0, 0)), pl.BlockSpec((None, d, fs), lambda j, i: (j, 0, 0)),
                   pl.BlockSpec((None, d, fs), lambda j, i: (j, 0, 0)), pl.BlockSpec((None, fs, d), lambda j, i: (j, 0, 0))],
        out_shape=[jax.ShapeDtypeStruct((t, d), F32), jax.ShapeDtypeStruct((4, d, fs), F32),
                   jax.ShapeDtypeStruct((4, d, fs), F32), jax.ShapeDtypeStruct((4, fs, d), F32)],
        compiler_params=_params("arbitrary", "arbitrary"))(dx2, h2, gp, up, wg, wu, wd)


def _ple_fwd(x2, p, wpg, wpp, tm):
    t, d = x2.shape
    q = p.shape[1]
    dq = d // 4

    def body(x_ref, p_ref, wg_ref, wp_ref, o_ref):
        xv = x_ref[...]
        xc = xv.astype(wg_ref.dtype)
        pc = p_ref[...].astype(wp_ref.dtype)
        pre = jnp.dot(xc[:, :dq], wg_ref[0], preferred_element_type=F32)
        for j in range(1, 4):
            pre = pre + jnp.dot(xc[:, j * dq:(j + 1) * dq], wg_ref[j], preferred_element_type=F32)
        gate = _sigmoid(pre)
        for j in range(4):
            cols = slice(j * dq, (j + 1) * dq)
            o_ref[:, cols] = xv[:, cols] + gate[:, cols] * jnp.dot(pc, wp_ref[j], preferred_element_type=F32)

    return pl.pallas_call(
        body, name="ple_fwd", grid=(t // tm,),
        in_specs=[pl.BlockSpec((tm, d), lambda i: (i, 0)), pl.BlockSpec((tm, q), lambda i: (i, 0)),
                  pl.BlockSpec((4, dq, d), lambda i: (0, 0, 0)),
                  pl.BlockSpec((4, q, dq), lambda i: (0, 0, 0))],
        out_specs=pl.BlockSpec((tm, d), lambda i: (i, 0)),
        out_shape=jax.ShapeDtypeStruct((t, d), F32),
        compiler_params=_params("arbitrary"))(x2, p, wpg, wpp)


def _ple_bwd(dx3, x2, p, wpg, wpp, tm):
    t, d = x2.shape
    q = p.shape[1]
    dq = d // 4

    def body(dx3_ref, x_ref, p_ref, wg_ref, wp_ref, dx2_ref, dwg_ref, dwp_ref):
        @pl.when(pl.program_id(0) == 0)
        def _():
            dwg_ref[...] = jnp.zeros_like(dwg_ref)
            dwp_ref[...] = jnp.zeros_like(dwp_ref)

        cdt = wg_ref.dtype
        xc = x_ref[...].astype(cdt)
        pc = p_ref[...].astype(cdt)
        pre = jnp.dot(xc[:, :dq], wg_ref[0], preferred_element_type=F32)
        for j in range(1, 4):
            pre = pre + jnp.dot(xc[:, j * dq:(j + 1) * dq], wg_ref[j], preferred_element_type=F32)
        gate = _sigmoid(pre)
        dx3v = dx3_ref[...]
        dpp = (dx3v * gate).astype(cdt)
        dgate = dx3v * gate * (1.0 - gate)
        dpre_parts = []
        for j in range(4):
            cols = slice(j * dq, (j + 1) * dq)
            pp_j = jnp.dot(pc, wp_ref[j], preferred_element_type=F32)
            dpre_parts.append((dgate[:, cols] * pp_j).astype(cdt))
            dwp_ref[j] += lax.dot_general(pc, dpp[:, cols], (((0,), (0,)), ((), ())), preferred_element_type=F32)
        dpre = jnp.concatenate(dpre_parts, axis=1)
        for j in range(4):
            cols = slice(j * dq, (j + 1) * dq)
            dwg_ref[j] += lax.dot_general(xc[:, cols], dpre, (((0,), (0,)), ((), ())), preferred_element_type=F32)
            dx2_ref[:, cols] = dx3v[:, cols] + lax.dot_general(dpre, wg_ref[j], (((1,), (1,)), ((), ())),
                                                               preferred_element_type=F32)

    return pl.pallas_call(
        body, name="ple_bwd", grid=(t // tm,),
        in_specs=[pl.BlockSpec((tm, d), lambda i: (i, 0)), pl.BlockSpec((tm, d), lambda i: (i, 0)),
                  pl.BlockSpec((tm, q), lambda i: (i, 0)), pl.BlockSpec((4, dq, d), lambda i: (0, 0, 0)),
                  pl.BlockSpec((4, q, dq), lambda i: (0, 0, 0))],
        out_specs=[pl.BlockSpec((tm, d), lambda i: (i, 0)), pl.BlockSpec((4, dq, d), lambda i: (0, 0, 0)),
                   pl.BlockSpec((4, q, dq), lambda i: (0, 0, 0))],
        out_shape=[jax.ShapeDtypeStruct((t, d), F32), jax.ShapeDtypeStruct((4, dq, d), F32),
                   jax.ShapeDtypeStruct((4, q, dq), F32)],
        compiler_params=_params("arbitrary"))(dx3, x2, p, wpg, wpp)


def _loss_head(x, target, fg, tm):
    t, d = x.shape

    def body(x_ref, t_ref, g_ref, dx_ref, loss_ref, dg_ref):
        @pl.when(pl.program_id(0) == 0)
        def _():
            loss_ref[...] = jnp.zeros_like(loss_ref)
            dg_ref[...] = jnp.zeros_like(dg_ref)

        xn, r = _rms_fwd(x_ref[...])
        g = g_ref[...]
        err = xn * g - t_ref[...]
        loss_ref[...] += 0.5 * jnp.sum(jnp.sum(err * err, axis=-1, keepdims=True) / d, axis=0, keepdims=True)
        dy = err / d
        dg_ref[...] += jnp.sum(dy * xn, axis=0, keepdims=True)
        dx_ref[...] = _rms_bwd(dy * g, xn, r)

    return pl.pallas_call(
        body, name="loss_head", grid=(t // tm,),
        in_specs=[pl.BlockSpec((tm, d), lambda i: (i, 0)), pl.BlockSpec((tm, d), lambda i: (i, 0)),
                  pl.BlockSpec((1, d), lambda i: (0, 0))],
        out_specs=[pl.BlockSpec((tm, d), lambda i: (i, 0)), pl.BlockSpec((1, 1), lambda i: (0, 0)),
                   pl.BlockSpec((1, d), lambda i: (0, 0))],
        out_shape=[jax.ShapeDtypeStruct((t, d), F32), jax.ShapeDtypeStruct((1, 1), F32),
                   jax.ShapeDtypeStruct((1, d), F32)],
        compiler_params=_params("arbitrary"))(x, target, fg)


def _qkv_conv_act(xv, w, j, heads):
    k = QKV_CONV_WIDTH
    y = w[k - 1:k] * xv
    for s in range(1, k):
        y = y + w[k - 1 - s:k - s] * _shift_down(xv, s)
    sg = _sigmoid(y)
    s_act = y * sg
    nrm = lax.rsqrt(jnp.sum(s_act * s_act, axis=-1, keepdims=True) + EPS)
    scale = jnp.where(j < heads, HEAD_DIM ** -0.5, 1.0).astype(F32)
    return y, sg, s_act, nrm, scale


def _qkv_conv_fwd(qkv_pre, conv_w, heads):
    t = qkv_pre.shape[0]
    nblk = 3 * heads

    def body(x_ref, w_ref, o_ref):
        j = pl.program_id(0)
        _, _, s_act, nrm, scale = _qkv_conv_act(x_ref[...], w_ref[...], j, heads)
        o_ref[...] = jnp.where(j < 2 * heads, s_act * (nrm * scale), s_act)

    return pl.pallas_call(
        body, name="qkv_conv_fwd", grid=(nblk,),
        in_specs=[pl.BlockSpec((t, LANES), lambda j: (0, j)), pl.BlockSpec((QKV_CONV_WIDTH, LANES), lambda j: (0, j))],
        out_specs=pl.BlockSpec((t, LANES), lambda j: (0, j)),
        out_shape=jax.ShapeDtypeStruct(qkv_pre.shape, F32),
        compiler_params=_params("arbitrary"))(qkv_pre, conv_w)


def _qkv_conv_bwd(qkv_pre, conv_w, dqkv, heads):
    t = qkv_pre.shape[0]
    nblk = 3 * heads
    k = QKV_CONV_WIDTH

    def body(x_ref, w_ref, dn_ref, dx_ref, dw_ref):
        j = pl.program_id(0)
        xv, w = x_ref[...], w_ref[...]
        y, sg, s_act, nrm, scale = _qkv_conv_act(xv, w, j, heads)
        dn = dn_ref[...]
        dsn = dn * scale
        ds_qk = nrm * dsn - s_act * (nrm * nrm * nrm) * jnp.sum(dsn * s_act, axis=-1, keepdims=True)
        ds = jnp.where(j < 2 * heads, ds_qk, dn)
        dy = ds * _dsilu(y, sg)
        dx = w[k - 1:k] * dy
        dw_ref[k - 1:k, :] = jnp.sum(dy * xv, axis=0, keepdims=True)
        for s in range(1, k):
            dx = dx + w[k - 1 - s:k - s] * _shift_up(dy, s)
            dw_ref[k - 1 - s:k - s, :] = jnp.sum(dy * _shift_down(xv, s), axis=0, keepdims=True)
        dx_ref[...] = dx

    return pl.pallas_call(
        body, name="qkv_conv_bwd", grid=(nblk,),
        in_specs=[pl.BlockSpec((t, LANES), lambda j: (0, j)), pl.BlockSpec((k, LANES), lambda j: (0, j)),
                  pl.BlockSpec((t, LANES), lambda j: (0, j))],
        out_specs=[pl.BlockSpec((t, LANES), lambda j: (0, j)), pl.BlockSpec((k, LANES), lambda j: (0, j))],
        out_shape=[jax.ShapeDtypeStruct(qkv_pre.shape, F32), jax.ShapeDtypeStruct(conv_w.shape, F32)],
        compiler_params=_params("arbitrary"))(qkv_pre, conv_w, dqkv)


def _pool_windows(shape, j, group_dim):
    lane = lax.broadcasted_iota(jnp.int32, shape, 1) + j * LANES
    grp = lane // group_dim
    win = jnp.left_shift(2, grp).astype(F32)
    cnt = jnp.minimum((_rows(shape) + 1).astype(F32), win)
    return grp, cnt


def _pool_select(grp, levels):
    out = levels[0]
    for gi in range(1, POOL_GROUPS):
        out = jnp.where(grp == gi, levels[gi], out)
    return out


def _pool_mean(hv, grp, cnt):
    acc, levels, width = hv, [], 1
    for _ in range(POOL_GROUPS):
        acc = acc + _shift_down(acc, width)
        width *= 2
        levels.append(acc)
    return _pool_select(grp, levels) / cnt - hv


def _pool_fwd(hp, wbd, scale, group_dim):
    t, dp = hp.shape

    def body(h_ref, w_ref, s_ref, o_ref):
        hv = h_ref[...]
        grp, cnt = _pool_windows(hv.shape, pl.program_id(0), group_dim)
        pooled = _pool_mean(hv, grp, cnt)
        o_ref[...] = _mm(pooled, w_ref[...]) * s_ref[...]

    return pl.pallas_call(
        body, name="pool_fwd", grid=(dp // LANES,),
        in_specs=[pl.BlockSpec((t, LANES), lambda j: (0, j)), pl.BlockSpec((LANES, LANES), lambda j: (j, j)),
                  pl.BlockSpec((1, LANES), lambda j: (0, j))],
        out_specs=pl.BlockSpec((t, LANES), lambda j: (0, j)),
        out_shape=jax.ShapeDtypeStruct(hp.shape, F32),
        compiler_params=_params("arbitrary"))(hp, wbd, scale)


def _pool_bwd(hp, wbd, scale, dob, group_dim):
    t, dp = hp.shape

    def body(h_ref, w_ref, s_ref, do_ref, dh_ref, dw_ref, ds_ref):
        hv = h_ref[...]
        grp, cnt = _pool_windows(hv.shape, pl.program_id(0), group_dim)
        pooled = _pool_mean(hv, grp, cnt)
        wv = w_ref[...]
        dov = do_ref[...]
        ds_ref[...] = jnp.sum(dov * _mm(pooled, wv), axis=0, keepdims=True)
        dys = dov * s_ref[...]
        dw_ref[0] = _mm_tn(pooled, dys)
        dpooled = _mm_nt(dys, wv)
        acc, levels, width = dpooled / cnt, [], 1
        for _ in range(POOL_GROUPS):
            acc = acc + _shift_up(acc, width)
            width *= 2
            levels.append(acc)
        dh_ref[...] = _pool_select(grp, levels) - dpooled

    nb = dp // LANES
    return pl.pallas_call(
        body, name="pool_bwd", grid=(nb,),
        in_specs=[pl.BlockSpec((t, LANES), lambda j: (0, j)), pl.BlockSpec((LANES, LANES), lambda j: (j, j)),
                  pl.BlockSpec((1, LANES), lambda j: (0, j)), pl.BlockSpec((t, LANES), lambda j: (0, j))],
        out_specs=[pl.BlockSpec((t, LANES), lambda j: (0, j)), pl.BlockSpec((1, LANES, LANES), lambda j: (j, 0, 0)),
                   pl.BlockSpec((1, LANES), lambda j: (0, j))],
        out_shape=[jax.ShapeDtypeStruct(hp.shape, F32), jax.ShapeDtypeStruct((nb, LANES, LANES), F32),
                   jax.ShapeDtypeStruct((1, dp), F32)],
        compiler_params=_params("arbitrary"))(hp, wbd, scale, dob)


def _sconv_fwd(cbcch, w):
    t, dc3 = cbcch.shape
    nb = dc3 // 3 // LANES
    k = SCONV_WIDTH

    def body(b_ref, c_ref, h_ref, w_ref, o_ref):
        m = c_ref[...] * h_ref[...]
        wv = w_ref[...]
        y = wv[k - 1:k] * m
        for s in range(1, k):
            y = y + wv[k - 1 - s:k - s] * _shift_down(m, s)
        o_ref[...] = b_ref[...] * y

    return pl.pallas_call(
        body, name="sconv_fwd", grid=(nb,),
        in_specs=[pl.BlockSpec((t, LANES), lambda j: (0, j)), pl.BlockSpec((t, LANES), lambda j: (0, nb + j)),
                  pl.BlockSpec((t, LANES), lambda j: (0, 2 * nb + j)), pl.BlockSpec((k, LANES), lambda j: (0, j))],
        out_specs=pl.BlockSpec((t, LANES), lambda j: (0, j)),
        out_shape=jax.ShapeDtypeStruct((t, dc3 // 3), F32),
        compiler_params=_params("arbitrary"))(cbcch, cbcch, cbcch, w)


def _sconv_bwd(cbcch, w, doc):
    t, dc3 = cbcch.shape
    nb = dc3 // 3 // LANES
    k = SCONV_WIDTH

    def body(b_ref, c_ref, h_ref, w_ref, do_ref, db_ref, dc_ref, dh_ref, dw_ref):
        cv, hv = c_ref[...], h_ref[...]
        m = cv * hv
        wv = w_ref[...]
        dov = do_ref[...]
        dy = dov * b_ref[...]
        y = wv[k - 1:k] * m
        dm = wv[k - 1:k] * dy
        dw_ref[k - 1:k, :] = jnp.sum(dy * m, axis=0, keepdims=True)
        for s in range(1, k):
            ms = _shift_down(m, s)
            y = y + wv[k - 1 - s:k - s] * ms
            dm = dm + wv[k - 1 - s:k - s] * _shift_up(dy, s)
            dw_ref[k - 1 - s:k - s, :] = jnp.sum(dy * ms, axis=0, keepdims=True)
        db_ref[...] = dov * y
        dc_ref[...] = dm * hv
        dh_ref[...] = dm * cv

    col = lambda o: pl.BlockSpec((t, LANES), lambda j: (0, o * nb + j))
    return pl.pallas_call(
        body, name="sconv_bwd", grid=(nb,),
        in_specs=[col(0), col(1), col(2), pl.BlockSpec((k, LANES), lambda j: (0, j)), col(0)],
        out_specs=[col(0), col(0), col(0), pl.BlockSpec((k, LANES), lambda j: (0, j))],
        out_shape=[jax.ShapeDtypeStruct((t, dc3 // 3), F32)] * 3 + [jax.ShapeDtypeStruct(w.shape, F32)],
        compiler_params=_params("arbitrary"))(cbcch, cbcch, cbcch, w, doc)


def _per_head(fn, a, b):
    return jnp.stack([fn(a[h], b[h]) for h in range(a.shape[0])])


def _bmm(a, b):
    return _per_head(_hmm, a, b)


def _bmm_nt(a, b):
    return _per_head(_hmm_nt, a, b)


def _bmm_tn(a, b):
    return _per_head(_hmm_tn, a, b)


def _inv_unit_lower(low):
    c = low.shape[-1]
    eye = (_rows((c, c)) == lax.broadcasted_iota(jnp.int32, (c, c), 1)).astype(F32)
    pw = -low
    inv = eye + pw
    span = 2
    while span < c:
        pw = _bmm(pw, pw)
        inv = inv + _bmm(inv, pw)
        span *= 2
    return inv


def _heads_of(ref, base, heads):
    return jnp.stack([ref[:, base + h * HEAD_DIM:base + (h + 1) * HEAD_DIM] for h in range(heads)])


def _chunk_common(q, k, v, a_col, b_col, alog, dtb, kept=None):
    hn, c, _ = q.shape
    beta = _sigmoid(b_col)
    xg = a_col + dtb
    softplus = jnp.maximum(xg, 0.0) + jnp.log(1.0 + jnp.exp(-jnp.abs(xg)))
    neg_ea = -jnp.exp(alog)
    g = neg_ea * softplus
    ri = _rows((c, c))
    ci = lax.broadcasted_iota(jnp.int32, (c, c), 1)
    incl, strict = ri >= ci, ri > ci
    inclf = jnp.broadcast_to(incl.astype(F32), (hn, c, c))
    gcb = _bmm(inclf, jnp.broadcast_to(g, (hn, c, HEAD_DIM)))
    gc_row = jnp.sum(jnp.where(ri <= ci, jnp.broadcast_to(g, (hn, c, c)), 0.0), axis=1, keepdims=True)
    dmat = jnp.where(incl, jnp.exp(jnp.where(incl, gcb[:, :, :1] - gc_row, 0.0)), 0.0)
    eg = jnp.exp(gcb)
    gl = gcb[:, c - 1:c, :]
    egl = jnp.exp(gl)
    edl = jnp.exp(gl - gcb)
    kb, vb = k * beta, v * beta
    kbe = kb * eg
    if kept is None:
        a0 = _bmm_nt(kb, k)
        tm = _inv_unit_lower(jnp.where(strict, a0 * dmat, 0.0))
        p0 = _bmm_nt(q, k)
        u, w = _bmm(tm, vb), _bmm(tm, kbe)
    else:
        (a0, tm, p0, w), u = kept, None
    return dict(beta=beta, xg=xg, neg_ea=neg_ea, g=g, incl=incl, strict=strict, inclf=inclf, dmat=dmat, eg=eg,
                egl=egl, edl=edl, kb=kb, vb=vb, a0=a0, tm=tm, kbe=kbe, u=u, w=w, p0=p0,
                attn=p0 * dmat, qe=q * eg, kd=k * edl)


def _chunk_step(cm, state):
    vn = cm["u"] - _bmm(cm["w"], state)
    o = _bmm(cm["qe"], state) + _bmm(cm["attn"], vn)
    new_state = state * cm["egl"][:, :, :1] + _bmm_tn(cm["kd"], vn)
    return vn, o, new_state


def _gated_norm(o, zv, og):
    xo, ro = _rms_fwd(o)
    sgz = _sigmoid(zv)
    return xo, ro, sgz, xo * og * (zv * sgz)


def _gate_columns(abv, gpv, heads):
    a_col = jnp.stack([abv[:, h:h + 1] for h in range(heads)])
    b_col = jnp.stack([abv[:, heads + h:heads + h + 1] for h in range(heads)])
    alog = jnp.stack([gpv[0:1, h:h + 1] for h in range(heads)])
    dtb = jnp.stack([gpv[1:2, h:h + 1] for h in range(heads)])
    return a_col, b_col, alog, dtb


def _delta_fwd(qkv, z, ab, gpar, heads):
    t = qkv.shape[0]
    da = heads * HEAD_DIM
    n = t // CHUNK

    def body(qkv_ref, z_ref, ab_ref, gp_ref, oa_ref, st_ref, kc_ref, kw_ref, s_ref):
        @pl.when(pl.program_id(0) == 0)
        def _():
            s_ref[...] = jnp.zeros_like(s_ref)

        gpv = gp_ref[...]
        cm = _chunk_common(_heads_of(qkv_ref, 0, heads), _heads_of(qkv_ref, da, heads), _heads_of(qkv_ref, 2 * da, heads),
                           *_gate_columns(ab_ref[...], gpv, heads))
        state = s_ref[...]
        st_ref[0] = state
        vn, o, new_state = _chunk_step(cm, state)
        s_ref[...] = new_state
        for slot, val in enumerate((cm["a0"], cm["tm"], cm["p0"])):
            kc_ref[0, slot] = val
        for slot, val in enumerate((cm["w"], vn, o)):
            kw_ref[0, slot] = val
        oa = _gated_norm(o, _heads_of(z_ref, 0, heads), gpv[2:3, :])[3]
        for h in range(heads):
            oa_ref[:, h * HEAD_DIM:(h + 1) * HEAD_DIM] = oa[h]

    return pl.pallas_call(
        body, name="delta_fwd", grid=(n,),
        in_specs=[pl.BlockSpec((CHUNK, 3 * da), lambda i: (i, 0)), pl.BlockSpec((CHUNK, da), lambda i: (i, 0)),
                  pl.BlockSpec((CHUNK, LANES), lambda i: (i, 0)), pl.BlockSpec((8, LANES), lambda i: (0, 0))],
        out_specs=[pl.BlockSpec((CHUNK, da), lambda i: (i, 0)),
                   pl.BlockSpec((1, heads, HEAD_DIM, HEAD_DIM), lambda i: (i, 0, 0, 0)),
                   pl.BlockSpec((1, 3, heads, CHUNK, CHUNK), lambda i: (i, 0, 0, 0, 0)),
                   pl.BlockSpec((1, 3, heads, CHUNK, HEAD_DIM), lambda i: (i, 0, 0, 0, 0))],
        out_shape=[jax.ShapeDtypeStruct((t, da), F32), jax.ShapeDtypeStruct((n, heads, HEAD_DIM, HEAD_DIM), F32),
                   jax.ShapeDtypeStruct((n, 3, heads, CHUNK, CHUNK), F32),
                   jax.ShapeDtypeStruct((n, 3, heads, CHUNK, HEAD_DIM), F32)],
        scratch_shapes=[pltpu.VMEM((heads, HEAD_DIM, HEAD_DIM), F32)],
        compiler_params=_params("arbitrary"))(qkv, z, ab, gpar)


def _delta_bwd(qkv, z, ab, gpar, states, kept_c, kept_w, doa, heads):
    t = qkv.shape[0]
    da = heads * HEAD_DIM
    n = t // CHUNK
    c = CHUNK

    def body(qkv_ref, z_ref, ab_ref, gp_ref, st_ref, kc_ref, kw_ref, doa_ref, dqkv_ref, dz_ref, dab_ref, dpar_ref, ds_ref):
        @pl.when(pl.program_id(0) == 0)
        def _():
            ds_ref[...] = jnp.zeros_like(ds_ref)
            dpar_ref[...] = jnp.zeros_like(dpar_ref)

        gpv = gp_ref[...]
        og = gpv[2:3, :]
        q, k, v = _heads_of(qkv_ref, 0, heads), _heads_of(qkv_ref, da, heads), _heads_of(qkv_ref, 2 * da, heads)
        cm = _chunk_common(q, k, v, *_gate_columns(ab_ref[...], gpv, heads),
                           kept=(kc_ref[0, 0], kc_ref[0, 1], kc_ref[0, 2], kw_ref[0, 0]))
        state = st_ref[0]
        dsp = ds_ref[...]
        vn, o = kw_ref[0, 1], kw_ref[0, 2]
        zv = _heads_of(z_ref, 0, heads)
        xo, ro, sgz, _ = _gated_norm(o, zv, og)
        doav = _heads_of(doa_ref, 0, heads)
        don = doav * (zv * sgz)
        dz = doav * (xo * og) * _dsilu(zv, sgz)
        d_og = jnp.sum(jnp.sum(don * xo, axis=1, keepdims=True), axis=0)
        do = _rms_bwd(don * og, xo, ro)
        tm, dmat, eg, edl, egl = cm["tm"], cm["dmat"], cm["eg"], cm["edl"], cm["egl"]
        dvn = _bmm_tn(cm["attn"], do) + _bmm(cm["kd"], dsp)
        dqe = _bmm_nt(do, state)
        ds_ref[...] = _bmm_tn(cm["qe"], do) + dsp * egl[:, :, :1] - _bmm_tn(cm["w"], dvn)
        dattn = _bmm_nt(do, vn)
        dkd = _bmm_nt(vn, dsp)
        dkd_kd = jnp.sum(dkd * cm["kd"], axis=-1, keepdims=True)
        dgl = (jnp.sum(jnp.sum(dsp * state, axis=-1, keepdims=True), axis=1, keepdims=True) * egl[:, :, :1]
               + jnp.sum(dkd_kd, axis=1, keepdims=True))
        dgc = jnp.sum(dqe * cm["qe"], axis=-1, keepdims=True) - dkd_kd
        dk = dkd * edl
        dq = dqe * eg
        dw = -_bmm_nt(dvn, state)
        dp0 = dattn * dmat
        dd = jnp.where(cm["incl"], dattn * cm["p0"], 0.0)
        dq = dq + _bmm(dp0, k)
        dk = dk + _bmm_tn(dp0, q)
        dtm = _bmm_nt(dvn, cm["vb"]) + _bmm_nt(dw, cm["kbe"])
        dvb = _bmm_tn(tm, dvn)
        dkbe = _bmm_tn(tm, dw)
        dkb = dkbe * eg
        dgc = dgc + jnp.sum(dkbe * cm["kbe"], axis=-1, keepdims=True)
        dlow = jnp.where(cm["strict"], -_bmm_tn(tm, _bmm_nt(dtm, tm)), 0.0)
        dd = dd + dlow * cm["a0"]
        da0 = dlow * dmat
        dkb = dkb + _bmm(da0, k)
        dk = dk + _bmm_tn(da0, cm["kb"])
        ddd = dd * dmat
        ones = jnp.ones((heads, c, HEAD_DIM), F32)
        dgc = dgc + jnp.sum(ddd, axis=-1, keepdims=True) - _bmm_tn(ddd, ones)[:, :, :1]
        dgc = dgc + jnp.where(_rows((c, 1)) == c - 1, dgl, 0.0)
        dg = _bmm_tn(cm["inclf"], jnp.broadcast_to(dgc, (heads, c, HEAD_DIM)))[:, :, :1]
        beta = cm["beta"]
        dk = dk + dkb * beta
        dbeta = jnp.sum(dkb * k, axis=-1, keepdims=True) + jnp.sum(dvb * v, axis=-1, keepdims=True)
        dv = dvb * beta
        db_col = dbeta * beta * (1.0 - beta)
        da_col = dg * cm["neg_ea"] * _sigmoid(cm["xg"])
        d_alog = jnp.sum(dg * cm["g"], axis=1, keepdims=True)
        d_dtb = jnp.sum(da_col, axis=1, keepdims=True)
        lane = lax.broadcasted_iota(jnp.int32, (c, LANES), 1)
        lane8 = lax.broadcasted_iota(jnp.int32, (8, LANES), 1)
        row8 = _rows((8, LANES))
        dab = jnp.zeros((c, LANES), F32)
        dpar = jnp.where(row8 == 2, d_og, 0.0)
        for h in range(heads):
            lo = h * HEAD_DIM
            dqkv_ref[:, lo:lo + HEAD_DIM] = dq[h]
            dqkv_ref[:, da + lo:da + lo + HEAD_DIM] = dk[h]
            dqkv_ref[:, 2 * da + lo:2 * da + lo + HEAD_DIM] = dv[h]
            dz_ref[:, lo:lo + HEAD_DIM] = dz[h]
            dab = dab + jnp.where(lane == h, da_col[h], 0.0) + jnp.where(lane == heads + h, db_col[h], 0.0)
            dpar = (dpar + jnp.where((row8 == 0) & (lane8 == h), d_alog[h], 0.0)
                    + jnp.where((row8 == 1) & (lane8 == h), d_dtb[h], 0.0))
        dab_ref[...] = dab
        dpar_ref[...] += dpar

    rev = lambda i: (n - 1 - i, 0)
    return pl.pallas_call(
        body, name="delta_bwd", grid=(n,),
        in_specs=[pl.BlockSpec((c, 3 * da), rev), pl.BlockSpec((c, da), rev), pl.BlockSpec((c, LANES), rev),
                  pl.BlockSpec((8, LANES), lambda i: (0, 0)),
                  pl.BlockSpec((1, heads, HEAD_DIM, HEAD_DIM), lambda i: (n - 1 - i, 0, 0, 0)),
                  pl.BlockSpec((1, 3, heads, c, c), lambda i: (n - 1 - i, 0, 0, 0, 0)),
                  pl.BlockSpec((1, 3, heads, c, HEAD_DIM), lambda i: (n - 1 - i, 0, 0, 0, 0)),
                  pl.BlockSpec((c, da), rev)],
        out_specs=[pl.BlockSpec((c, 3 * da), rev), pl.BlockSpec((c, da), rev), pl.BlockSpec((c, LANES), rev),
                   pl.BlockSpec((8, LANES), lambda i: (0, 0))],
        out_shape=[jax.ShapeDtypeStruct((t, 3 * da), F32), jax.ShapeDtypeStruct((t, da), F32),
                   jax.ShapeDtypeStruct((t, LANES), F32), jax.ShapeDtypeStruct((8, LANES), F32)],
        scratch_shapes=[pltpu.VMEM((heads, HEAD_DIM, HEAD_DIM), F32)],
        compiler_params=_params("arbitrary"))(qkv, z, ab, gpar, states, kept_c, kept_w, doa)


def _w_in_pieces(shard_cols, da, heads):
    a0, nab = 4 * da, 2 * heads
    d_in = 4 * shard_cols
    runs = [(0, a0, 0), (a0, a0 + nab, d_in - nab), (a0 + nab, d_in, a0)]
    pieces = []
    for j in range(4):
        lo, hi = j * shard_cols, (j + 1) * shard_cols
        for rlo, rhi, plo in runs:
            s, e = max(lo, rlo), min(hi, rhi)
            if s < e:
                pieces.append((j, s - lo, e - s, plo + (s - rlo)))
    return pieces, d_in - nab + LANES


def _w_in_pack(w4, da, heads):
    _, d, sc = w4.shape
    pieces, npk = _w_in_pieces(sc, da, heads)
    tr = _tile_rows(d, 256, SUBLANES_WIRE)

    def body(w_ref, o_ref):
        o_ref[:, npk - LANES:] = jnp.zeros((tr, LANES), o_ref.dtype)
        for j, lo, ln, dst in pieces:
            o_ref[:, dst:dst + ln] = w_ref[j, :, lo:lo + ln]

    return pl.pallas_call(
        body, name="w_in_pack", grid=(d // tr,),
        in_specs=[pl.BlockSpec((4, tr, sc), lambda i: (0, i, 0))],
        out_specs=pl.BlockSpec((tr, npk), lambda i: (i, 0)),
        out_shape=jax.ShapeDtypeStruct((d, npk), w4.dtype),
        compiler_params=_params("arbitrary"))(w4)


def _w_in_unpack(dwp, sc, da, heads):
    d, npk = dwp.shape
    pieces, _ = _w_in_pieces(sc, da, heads)
    tr = _tile_rows(d, 256)

    def body(g_ref, o_ref):
        for j, lo, ln, dst in pieces:
            o_ref[j, :, lo:lo + ln] = g_ref[:, dst:dst + ln]

    return pl.pallas_call(
        body, name="w_in_unpack", grid=(d // tr,),
        in_specs=[pl.BlockSpec((tr, npk), lambda i: (i, 0))],
        out_specs=pl.BlockSpec((4, tr, sc), lambda i: (0, i, 0)),
        out_shape=jax.ShapeDtypeStruct((4, d, sc), F32),
        compiler_params=_params("arbitrary"))(dwp)


def _block_diag(pool_w):
    g, gd, _ = pool_w.shape
    out = jnp.zeros((g * gd, g * gd), pool_w.dtype)
    for gi in range(g):
        out = lax.dynamic_update_slice(out, pool_w[gi], (gi * gd, gi * gd))
    return out


def _layer_dims(d):
    heads = (d // 2) // HEAD_DIM
    return heads, heads * HEAD_DIM, d // 4, d // 4


BIG = ("w_in", "w_gate", "w_up", "ple_proj", "w_out", "w_down", "ple_gate")


def _prepare_layer(small, li):
    d = small["norm1_g"].shape[1]
    heads, _, _, _ = _layer_dims(d)
    gpar = jnp.zeros((8, LANES), F32)
    gpar = gpar.at[0, :heads].set(small["a_log"][li]).at[1, :heads].set(small["dt_bias"][li]).at[2, :].set(small["onorm_g"][li])
    return dict(norm1_g=small["norm1_g"][li][None], conv_qkv=small["conv_qkv"][li], gpar=gpar, pool_bd=_block_diag(small["pool_w"][li]).astype(MM_DTYPE),
                pool_scale=small["pool_scale"][li][None], sconv_w=small["sconv_w"][li], norm2_g=small["norm2_g"][li][None])


def _layer_fwd(x0, p, gw, lw, tm, arrive):
    d = x0.shape[1]
    heads, da, dp, dc = _layer_dims(d)
    segs = (3 * da, da, dp, 3 * dc, LANES)
    lw["w_in_p"] = _w_in_pack(gw["w_in"], da, heads).astype(MM_DTYPE)
    qkv_pre, z, hp, cbcch, ab = _in_proj_fwd(x0, lw["norm1_g"], lw["w_in_p"], segs, tm)
    qkv = _qkv_conv_fwd(qkv_pre, lw["conv_qkv"], heads)
    oa, states, kept_c, kept_w = _delta_fwd(qkv, z, ab, lw["gpar"], heads)
    ob = _pool_fwd(hp, lw["pool_bd"], lw["pool_scale"], dp // POOL_GROUPS)
    oc = _sconv_fwd(cbcch, lw["sconv_w"])
    arrive("mixed", oa)
    x1, h2 = _out_proj_fwd(x0, (oa, ob, oc), gw["w_out"], lw["norm2_g"], tm)
    x2, gp, up = _ffn_fwd(x1, h2, gw["w_gate"], gw["w_up"], gw["w_down"], tm)
    arrive("ffn", x2)
    x3 = _ple_fwd(x2, p, gw["ple_gate"], gw["ple_proj"], tm)
    arrive("end", x3)
    saved = dict(x0=x0, qkv_pre=qkv_pre, z=z, hp=hp, cbcch=cbcch, ab=ab, qkv=qkv, states=states, kept_c=kept_c, kept_w=kept_w, oa=oa, ob=ob, oc=oc,
                 x1=x1, h2=h2, gp=gp, up=up, x2=x2)
    return x3, saved


def _layer_bwd(dx3, p, gw, lw, sv, tm):
    d = dx3.shape[1]
    heads, da, dp, dc = _layer_dims(d)
    segs = (3 * da, da, dp, dc, dc, dc, LANES)
    gd = dp // POOL_GROUPS
    dx2, d_ple_gate, d_ple_proj = _ple_bwd(dx3, sv["x2"], p, gw["ple_gate"], gw["ple_proj"], tm)
    dh2, d_w_gate, d_w_up, d_w_down = _ffn_bwd(dx2, sv["h2"], sv["gp"], sv["up"], gw["w_gate"], gw["w_up"], gw["w_down"],
                                               min(tm, 256))
    dx1, doa, dob, doc, d_w_out, d_norm2 = _out_proj_bwd(dx2, dh2, sv["x1"], lw["norm2_g"],
                                                         (sv["oa"], sv["ob"], sv["oc"]), gw["w_out"], tm)
    dcb, dcc, dch, d_sconv = _sconv_bwd(sv["cbcch"], lw["sconv_w"], doc)
    dhp, d_pool_bd, d_pool_scale = _pool_bwd(sv["hp"], lw["pool_bd"], lw["pool_scale"], dob, gd)
    dqkv, dz, dab, dpar = _delta_bwd(sv["qkv"], sv["z"], sv["ab"], lw["gpar"], sv["states"], sv["kept_c"], sv["kept_w"], doa,
                                      heads)
    dqkv_pre, d_conv_qkv = _qkv_conv_bwd(sv["qkv_pre"], lw["conv_qkv"], dqkv, heads)
    dsegs = (dqkv_pre, dz, dhp, dcb, dcc, dch, dab)
    dx0, d_w_in_p, d_norm1 = _in_proj_bwd(sv["x0"], lw["norm1_g"], lw["w_in_p"], dsegs, dx1, segs, tm)
    per = LANES // gd
    bd = d_pool_bd.reshape(dp // LANES, per, gd, per, gd)
    d_pool_w = jnp.stack([bd[gi // per, gi % per, :, gi % per, :] for gi in range(POOL_GROUPS)])
    big = dict(w_in=_w_in_unpack(d_w_in_p, gw["w_in"].shape[2], da, heads), w_gate=d_w_gate, w_up=d_w_up,
               ple_proj=d_ple_proj, w_out=d_w_out, w_down=d_w_down, ple_gate=d_ple_gate)
    small = dict(norm1_g=d_norm1[0], conv_qkv=d_conv_qkv, a_log=dpar[0, :heads], dt_bias=dpar[1, :heads], onorm_g=dpar[2],
                 pool_w=d_pool_w, pool_scale=d_pool_scale[0], sconv_w=d_sconv, norm2_g=d_norm2[0])
    return dx0, big, small


def _local_step(x, p, target, gw, small, send_grads=None, arrive=None):
    t, d = x.shape
    depth = p.shape[0]
    tm = 512 if t % 512 == 0 else 128
    layers = [_prepare_layer(small, li) for li in range(depth)]
    saved = []
    h = x
    for li in range(depth):
        h, sv = _layer_fwd(h, p[li], gw[li], layers[li], tm,
                           (lambda stage, after, li=li: arrive(li, stage, after)) if arrive else (lambda stage, after: None))
        saved.append(sv)
    dx, loss, d_final = _loss_head(h, target, small["final_g"][None], tm)
    big, sm = [None] * depth, [None] * depth
    token = None
    for li in reversed(range(depth)):
        p_li = p[li] if token is None else p[li] + token[0, 0]
        dx, big[li], sm[li] = _layer_bwd(dx, p_li, gw[li], layers[li], saved[li], tm)
        if send_grads is not None and li > 0:
            big[li], token = send_grads(li, big[li])
    small_grads = {n: jnp.stack([g[n] for g in sm]) for n in sm[0]}
    small_grads["final_g"] = d_final[0]
    return loss[0, 0], dx, big, small_grads


def _coords():
    return lax.axis_index("x"), lax.axis_index("y"), lax.axis_index("c")


def _other_chips(x, y):
    return [(1 - x, y), (x, 1 - y), (1 - x, 1 - y)]


def _place_shards(ws, me_idx):
    nt = len(ws)
    depth = ws[0].shape[0]

    def body(me_ref, *refs):
        for t, w_ref in enumerate(refs[:nt]):
            for li in range(depth):
                refs[nt + li * nt + t][...] = w_ref[li].astype(WIRE_DTYPE)

    outs = pl.pallas_call(
        body, name="place_shards",
        grid_spec=pltpu.PrefetchScalarGridSpec(
            num_scalar_prefetch=1, grid=(4,),
            in_specs=[pl.BlockSpec((depth, w.shape[1] // 4, w.shape[2]), lambda i, me_ref: (0, i, 0)) for w in ws],
            out_specs=[pl.BlockSpec((None, w.shape[1] // 4, w.shape[2]), lambda i, me_ref: (me_ref[0], i, 0))
                       for _ in range(depth) for w in ws]),
        out_shape=[jax.ShapeDtypeStruct((4,) + w.shape[1:], WIRE_DTYPE) for _ in range(depth) for w in ws],
        compiler_params=_params("arbitrary"))(me_idx, *ws)
    return [list(outs[li * nt:(li + 1) * nt]) for li in range(depth)]


def _half_block(ref, chip, pc):
    rh = ref.shape[1] // 2
    return ref.at[chip, pl.ds(pc * rh, rh)]


def _gather_copies(out_refs, send_sems, recv_sems, stage):
    nt = len(out_refs)
    x, y, c = _coords()
    pairs = []
    for j, (cx, cy) in enumerate(_other_chips(x, y)):
        for t in range(nt):
            sems = dict(send_sem=send_sems[j * nt + t], recv_sem=recv_sems[j * nt + t], device_id_type=MESH)
            if stage == 0:
                mine, theirs, to = _half_block(out_refs[t], 2 * x + y, c), _half_block(out_refs[t], 2 * cx + cy, c), (cx, cy, c)
            else:
                mine, theirs, to = (_half_block(out_refs[t], 2 * cx + cy, c), _half_block(out_refs[t], 2 * cx + cy, 1 - c),
                                    (x, y, 1 - c))
            pairs.append((pltpu.make_async_remote_copy(src_ref=mine, dst_ref=mine, device_id=to, **sems),
                          pltpu.make_async_remote_copy(src_ref=theirs, dst_ref=theirs, device_id=to, **sems)))
    return pairs


def _all_gather_chips(placed):
    nt = len(placed)

    def body(*refs):
        out_refs = refs[nt:2 * nt]
        send_sems, recv_sems = refs[2 * nt:]
        nc = 3 * nt
        first = _gather_copies(out_refs, [send_sems.at[k] for k in range(nc)], [recv_sems.at[k] for k in range(nc)], 0)
        passed = _gather_copies(out_refs, [send_sems.at[nc + k] for k in range(nc)], [recv_sems.at[nc + k] for k in range(nc)], 1)
        for start, _ in first:
            start.start()
        for (_, arrival), (forward, _) in zip(first, passed):
            arrival.wait_recv()
            forward.start()
        for _, arrival in passed:
            arrival.wait_recv()
        for start, _ in first + passed:
            start.wait_send()

    return pl.pallas_call(
        body, name="all_gather_chips", out_shape=[jax.ShapeDtypeStruct(a.shape, a.dtype) for a in placed],
        in_specs=[ANY] * nt, out_specs=[ANY] * nt, input_output_aliases={t: t for t in range(nt)},
        scratch_shapes=[pltpu.SemaphoreType.DMA((6 * nt,)), pltpu.SemaphoreType.DMA((6 * nt,))],
    )(*placed)


def _gather_call(name, arrs, wait_sems, after, stage):
    nt = len(arrs)
    nc = 3 * nt
    n_wait = len(wait_sems)
    n_new = 2 * nc if stage < 2 else 0
    arrs = [pltpu.with_memory_space_constraint(a, pltpu.HBM) for a in arrs]

    def body(*refs):
        a_refs = refs[:nt]
        waits = refs[nt:nt + n_wait]
        news = refs[nt + n_wait + 1:nt + n_wait + 1 + n_new]
        token = refs[-1]
        if stage > 0:
            for start, arrival in _gather_copies(a_refs, waits[:nc], waits[nc:], stage - 1):
                start.wait_send()
                arrival.wait_recv()
        if stage < 2:
            for start, _ in _gather_copies(a_refs, news[:nc], news[nc:], stage):
                start.start()
        token[...] = jnp.zeros_like(token)

    outs = pl.pallas_call(
        body, name=name,
        out_shape=(*[pltpu.SemaphoreType.DMA(())] * n_new, *[pltpu.HBM(a.shape, a.dtype) for a in arrs],
                   jax.ShapeDtypeStruct((8, LANES), F32)),
        in_specs=[HBM] * nt + [SEM] * n_wait + [ANY],
        out_specs=(*[SEM] * n_new, *[HBM] * nt, pl.BlockSpec(memory_space=pltpu.VMEM)),
        input_output_aliases={t: n_new + t for t in range(nt)},
        compiler_params=pltpu.CompilerParams(has_side_effects=pltpu.SideEffectType.DATAFLOW_SIDE_EFFECTING),
    )(*arrs, *wait_sems, after)
    return list(outs[:n_new]), list(outs[n_new:n_new + nt]), outs[-1]


def _sibling_swap_half(gs):
    nt = len(gs)

    def body(*refs):
        g_refs, out_refs = refs[:nt], refs[nt:2 * nt]
        send_sems, recv_sems = refs[2 * nt:]
        x, y, c = _coords()
        cps = []
        for t in range(nt):
            rh = g_refs[t].shape[1] // 2
            cps.append(pltpu.make_async_remote_copy(src_ref=g_refs[t].at[:, pl.ds((1 - c) * rh, rh)], dst_ref=out_refs[t],
                                                    send_sem=send_sems.at[t], recv_sem=recv_sems.at[t], device_id=(x, y, 1 - c),
                                                    device_id_type=MESH))
        for cp in cps:
            cp.start()
        for cp in cps:
            cp.wait()

    return pl.pallas_call(
        body, name="sibling_swap_half",
        out_shape=[jax.ShapeDtypeStruct((g.shape[0], g.shape[1] // 2, g.shape[2]), g.dtype) for g in gs],
        in_specs=[ANY] * nt, out_specs=[ANY] * nt,
        scratch_shapes=[pltpu.SemaphoreType.DMA((nt,)), pltpu.SemaphoreType.DMA((nt,))])(*gs)


def _add_my_halves(gs, others, c_idx):
    nt = len(gs)

    def body(c_ref, *refs):
        for g_ref, o_ref, out_ref in zip(refs[:nt], refs[nt:2 * nt], refs[2 * nt:]):
            out_ref[...] = (g_ref[...].astype(F32) + o_ref[...].astype(F32)).astype(out_ref.dtype)

    def quarter(g):
        return pl.BlockSpec((None, g.shape[1] // 4, g.shape[2]), lambda j, i, c_ref: (j, i, 0))

    return pl.pallas_call(
        body, name="add_my_halves",
        grid_spec=pltpu.PrefetchScalarGridSpec(
            num_scalar_prefetch=1, grid=(4, 2),
            in_specs=[pl.BlockSpec((None, g.shape[1] // 4, g.shape[2]), lambda j, i, c_ref: (j, 2 * c_ref[0] + i, 0)) for g in gs]
                     + [quarter(g) for g in gs],
            out_specs=[quarter(g) for g in gs]),
        out_shape=[jax.ShapeDtypeStruct((4, g.shape[1] // 2, g.shape[2]), WIRE_DTYPE) for g in gs],
        compiler_params=_params("arbitrary", "arbitrary"))(c_idx, *gs, *others)


def _exchange_chips(parts):
    nt = len(parts)

    def body(*refs):
        p_refs, out_refs = refs[:nt], refs[nt:2 * nt]
        send_sems, recv_sems = refs[2 * nt:]
        x, y, c = _coords()
        chips = _other_chips(x, y)

        def copy(j, t):
            cx, cy = chips[j]
            return pltpu.make_async_remote_copy(src_ref=p_refs[t].at[2 * cx + cy], dst_ref=out_refs[t].at[j],
                                                send_sem=send_sems.at[j, t], recv_sem=recv_sems.at[j, t], device_id=(cx, cy, c),
                                                device_id_type=MESH)

        sends = [copy(j, t) for j in range(3) for t in range(nt)]
        for cp in sends:
            cp.start()
        for cp in sends:
            cp.wait_recv()
        for cp in sends:
            cp.wait_send()

    return pl.pallas_call(
        body, name="exchange_chips", out_shape=[jax.ShapeDtypeStruct((3,) + p.shape[1:], p.dtype) for p in parts],
        in_specs=[ANY] * nt, out_specs=[ANY] * nt,
        scratch_shapes=[pltpu.SemaphoreType.DMA((3, nt)), pltpu.SemaphoreType.DMA((3, nt))])(*parts)


def _exchange_start(parts):
    nt = len(parts)
    nc = 3 * nt
    in_hbm = [pltpu.with_memory_space_constraint(p, pltpu.HBM) for p in parts]
    lands = [pltpu.with_memory_space_constraint(lax.empty((3,) + p.shape[1:], p.dtype), pltpu.HBM) for p in parts]

    def body(*refs):
        p_refs, l_refs = refs[:nt], refs[nt:2 * nt]
        send_sems, recv_sems, token = refs[2 * nt:2 * nt + nc], refs[2 * nt + nc:2 * nt + 2 * nc], refs[-1]
        x, y, c = _coords()
        for j, (cx, cy) in enumerate(_other_chips(x, y)):
            for t in range(nt):
                pltpu.make_async_remote_copy(src_ref=p_refs[t].at[2 * cx + cy], dst_ref=l_refs[t].at[j],
                                             send_sem=send_sems[j * nt + t], recv_sem=recv_sems[j * nt + t], device_id=(cx, cy, c),
                                             device_id_type=MESH).start()
        token[...] = jnp.zeros_like(token)

    outs = pl.pallas_call(
        body, name="exchange_start",
        out_shape=(*[pltpu.SemaphoreType.DMA(())] * (2 * nc), *[pltpu.HBM(a.shape, a.dtype) for a in in_hbm + lands],
                   jax.ShapeDtypeStruct((8, LANES), F32)),
        in_specs=[HBM] * (2 * nt), out_specs=(*[SEM] * (2 * nc), *[HBM] * (2 * nt), pl.BlockSpec(memory_space=pltpu.VMEM)),
        input_output_aliases={t: 2 * nc + t for t in range(2 * nt)},
        compiler_params=pltpu.CompilerParams(has_side_effects=pltpu.SideEffectType.DATAFLOW_SIDE_EFFECTING))(*in_hbm, *lands)
    return (list(outs[:nc]), list(outs[nc:2 * nc]), list(outs[2 * nc:2 * nc + nt]), list(outs[2 * nc + nt:2 * nc + 2 * nt]),
            outs[-1])


def _exchange_wait(send_sems, recv_sems, parts, lands, after):
    nt = len(parts)
    nc = 3 * nt

    def body(*refs):
        p_refs, l_refs = refs[:nt], refs[nt:2 * nt]
        send_ref, recv_ref = refs[2 * nt:2 * nt + nc], refs[2 * nt + nc:2 * nt + 2 * nc]
        x, y, c = _coords()
        for j, (cx, cy) in enumerate(_other_chips(x, y)):
            for t in range(nt):
                cp = pltpu.make_async_remote_copy(src_ref=p_refs[t].at[2 * cx + cy], dst_ref=l_refs[t].at[j],
                                                  send_sem=send_ref[j * nt + t], recv_sem=recv_ref[j * nt + t], device_id=(cx, cy, c),
                                                  device_id_type=MESH)
                cp.wait_send()
                cp.wait_recv()

    outs = pl.pallas_call(
        body, name="exchange_wait", out_shape=tuple(pltpu.HBM(a.shape, a.dtype) for a in parts + lands),
        in_specs=[HBM] * (2 * nt) + [SEM] * (2 * nc) + [ANY], out_specs=tuple([HBM] * (2 * nt)),
        input_output_aliases={t: t for t in range(2 * nt)},
        compiler_params=pltpu.CompilerParams(has_side_effects=pltpu.SideEffectType.DATAFLOW_SIDE_EFFECTING),
    )(*parts, *lands, *send_sems, *recv_sems, after)
    return list(outs[:nt]), list(outs[nt:])


def _sum_into(pairs, recvs, idx, li, depth, accs):
    nt = len(pairs)

    def body(idx_ref, *refs):
        for p_ref, r_ref, out_ref in zip(refs[:nt], refs[nt:2 * nt], refs[-nt:]):
            out_ref[...] = p_ref[...].astype(F32) + r_ref[0].astype(F32) + r_ref[1].astype(F32) + r_ref[2].astype(F32)

    in_specs = ([pl.BlockSpec((None, p.shape[1] // 2, p.shape[2]), lambda i, idx_ref: (idx_ref[0], i, 0)) for p in pairs]
                + [pl.BlockSpec((3, p.shape[1] // 2, p.shape[2]), lambda i, idx_ref: (0, i, 0)) for p in pairs])
    args = [idx, *pairs, *recvs]
    aliases = {}
    if accs[0] is not None:
        in_specs += [ANY] * nt
        args += list(accs)
        aliases = {1 + 2 * nt + t: t for t in range(nt)}
    return pl.pallas_call(
        body, name="sum_into",
        grid_spec=pltpu.PrefetchScalarGridSpec(
            num_scalar_prefetch=1, grid=(2,), in_specs=in_specs,
            out_specs=[pl.BlockSpec((None, p.shape[1] // 2, p.shape[2]), lambda i, idx_ref: (li, 2 * idx_ref[1] + i, 0))
                       for p in pairs]),
        out_shape=[jax.ShapeDtypeStruct((depth, 2 * p.shape[1], p.shape[2]), F32) for p in pairs],
        input_output_aliases=aliases, compiler_params=_params("arbitrary"))(*args)


def _sum_slots(parts):
    n, rows, cols = parts.shape
    tr = _tile_rows(rows, 512, SUBLANES_WIRE)

    def body(p_ref, out_ref):
        acc = p_ref[0].astype(F32)
        for s in range(1, n):
            acc = acc + p_ref[s].astype(F32)
        out_ref[...] = acc

    return pl.pallas_call(
        body, name="sum_slots", grid=(rows // tr,),
        in_specs=[pl.BlockSpec((n, tr, cols), lambda i: (0, i, 0))],
        out_specs=pl.BlockSpec((tr, cols), lambda i: (i, 0)),
        out_shape=jax.ShapeDtypeStruct((rows, cols), F32),
        compiler_params=_params("arbitrary"))(parts)


def _sibling_share(gs):
    nt = len(gs)
    depth = gs[0].shape[0]

    def body(*refs):
        out_refs = refs[nt:2 * nt]
        send_sems, recv_sems = refs[2 * nt:]
        x, y, c = _coords()
        sends, recvs = [], []
        for t in range(nt):
            rh = out_refs[t].shape[1] // 2
            for li in range(depth):
                mine = out_refs[t].at[li, pl.ds(c * rh, rh)]
                theirs = out_refs[t].at[li, pl.ds((1 - c) * rh, rh)]
                sems = dict(send_sem=send_sems.at[t, li], recv_sem=recv_sems.at[t, li], device_id=(x, y, 1 - c), device_id_type=MESH)
                sends.append(pltpu.make_async_remote_copy(src_ref=mine, dst_ref=mine, **sems))
                recvs.append(pltpu.make_async_remote_copy(src_ref=theirs, dst_ref=theirs, **sems))
        for cp in sends:
            cp.start()
        for cp in recvs:
            cp.wait_recv()
        for cp in sends:
            cp.wait_send()

    return pl.pallas_call(
        body, name="sibling_share", out_shape=[jax.ShapeDtypeStruct(g.shape, g.dtype) for g in gs],
        in_specs=[ANY] * nt, out_specs=[ANY] * nt, input_output_aliases={t: t for t in range(nt)},
        scratch_shapes=[pltpu.SemaphoreType.DMA((nt, depth)), pltpu.SemaphoreType.DMA((nt, depth))])(*gs)


def _all_gather_devices(buf):
    def body(b_ref, out_ref, send_sems, recv_sems, local_sem):
        x, y, c = _coords()
        me = 4 * x + 2 * y + c
        mine = pltpu.make_async_copy(b_ref, out_ref.at[me], local_sem)
        mine.start()
        peers = []
        for k in range(1, 8):
            fx, fy, fc = (k >> 2) & 1, (k >> 1) & 1, k & 1
            peers.append((x ^ fx, y ^ fy, c ^ fc))
        sends = [pltpu.make_async_remote_copy(src_ref=b_ref, dst_ref=out_ref.at[me], send_sem=send_sems.at[k],
                                              recv_sem=recv_sems.at[k], device_id=peer, device_id_type=MESH)
                 for k, peer in enumerate(peers)]
        for cp in sends:
            cp.start()
        for k, (px, py, pc) in enumerate(peers):
            pltpu.make_async_remote_copy(src_ref=b_ref, dst_ref=out_ref.at[4 * px + 2 * py + pc], send_sem=send_sems.at[k],
                                         recv_sem=recv_sems.at[k], device_id=(px, py, pc), device_id_type=MESH).wait_recv()
        for cp in sends:
            cp.wait_send()
        mine.wait()

    return pl.pallas_call(
        body, name="all_gather_devices", out_shape=jax.ShapeDtypeStruct((8,) + buf.shape, buf.dtype),
        in_specs=[ANY], out_specs=ANY,
        scratch_shapes=[pltpu.SemaphoreType.DMA((7,)), pltpu.SemaphoreType.DMA((7,)), pltpu.SemaphoreType.DMA(())])(buf)


def _pair_sums(big_grads, c_idx):
    gs = [big_grads[n] for n in BIG]
    return _add_my_halves(gs, _sibling_swap_half(gs), c_idx)


SMALL_SHARDED = ("conv_qkv", "sconv_w")
REPLICATED = ("norm1_g", "a_log", "dt_bias", "onorm_g", "pool_w", "pool_scale", "norm2_g", "final_g")
ALL_WEIGHTS = ("norm1_g", "w_in", "conv_qkv", "a_log", "dt_bias", "onorm_g", "pool_w", "pool_scale", "sconv_w", "w_out",
               "norm2_g", "w_gate", "w_up", "w_down", "ple_proj", "ple_gate", "final_g")


def _pad_rows(flat, row_multiple):
    m = flat.shape[0]
    r = -(-m // (LANES * row_multiple)) * row_multiple
    return jnp.pad(flat, (0, r * LANES - m)).reshape(r, LANES)


def _adamw(w, g, m, v):
    shape = w.shape
    cols = shape[-1]
    rows = w.size // cols
    tr = _tile_rows(rows, 512)
    c1 = 1.0 / (1.0 - ADAM_B1 ** ADAM_STEP)
    c2 = 1.0 / (1.0 - ADAM_B2 ** ADAM_STEP)

    def body(w_ref, g_ref, m_ref, v_ref, d_ref, nm_ref, nv_ref, go_ref):
        gv = g_ref[...]
        nm = ADAM_B1 * m_ref[...] + (1.0 - ADAM_B1) * gv
        nv = ADAM_B2 * v_ref[...] + (1.0 - ADAM_B2) * (gv * gv)
        nm_ref[...] = nm
        nv_ref[...] = nv
        go_ref[...] = gv
        d_ref[...] = -ADAM_LR * ((nm * c1) / (jnp.sqrt(nv * c2) + ADAM_EPS) + ADAM_WD * w_ref[...])

    spec = pl.BlockSpec((tr, cols), lambda i: (i, 0))
    outs = pl.pallas_call(
        body, name="adamw", grid=(rows // tr,), in_specs=[spec] * 4, out_specs=[spec] * 4,
        out_shape=[jax.ShapeDtypeStruct((rows, cols), F32)] * 4,
        compiler_params=_params("arbitrary"))(*[a.reshape(rows, cols) for a in (w, g, m, v)])
    return tuple(o.reshape(shape) for o in outs)


def kernel(x, p, norm1_g, w_in, conv_qkv, a_log, dt_bias, onorm_g, pool_w, pool_scale, sconv_w, w_out, norm2_g, w_gate, w_up, w_down, ple_proj, ple_gate, final_g, loss_target, m_norm1_g, m_w_in, m_conv_qkv, m_a_log, m_dt_bias, m_onorm_g, m_pool_w, m_pool_scale, m_sconv_w, m_w_out, m_norm2_g, m_w_gate, m_w_up, m_w_down, m_ple_proj, m_ple_gate, m_final_g, v_norm1_g, v_w_in, v_conv_qkv, v_a_log, v_dt_bias, v_onorm_g, v_pool_w, v_pool_scale, v_sconv_w, v_w_out, v_norm2_g, v_w_gate, v_w_up, v_w_down, v_ple_proj, v_ple_gate, v_final_g):
    weights = dict(zip(ALL_WEIGHTS, (norm1_g, w_in, conv_qkv, a_log, dt_bias, onorm_g, pool_w, pool_scale, sconv_w, w_out,
                                     norm2_g, w_gate, w_up, w_down, ple_proj, ple_gate, final_g)))
    mom_m = dict(zip(ALL_WEIGHTS, (m_norm1_g, m_w_in, m_conv_qkv, m_a_log, m_dt_bias, m_onorm_g, m_pool_w, m_pool_scale,
                                   m_sconv_w, m_w_out, m_norm2_g, m_w_gate, m_w_up, m_w_down, m_ple_proj, m_ple_gate, m_final_g)))
    mom_v = dict(zip(ALL_WEIGHTS, (v_norm1_g, v_w_in, v_conv_qkv, v_a_log, v_dt_bias, v_onorm_g, v_pool_w, v_pool_scale,
                                   v_sconv_w, v_w_out, v_norm2_g, v_w_gate, v_w_up, v_w_down, v_ple_proj, v_ple_gate, v_final_g)))
    c_idx = lax.axis_index("c").astype(jnp.int32).reshape(1)
    chip = (2 * lax.axis_index("x") + lax.axis_index("y")).astype(jnp.int32)
    me_idx = chip.reshape(1)
    idx = jnp.stack([chip, lax.axis_index("c").astype(jnp.int32)])
    depth = p.shape[0]

    placed = _place_shards([weights[n] for n in BIG], me_idx)
    gw = [dict() for _ in range(depth)]
    gw[0]["w_in"] = _all_gather_chips(placed[0][:1])[0]
    rest = BIG[1:]
    sems_a, arrs_a, token = _gather_call("gather_rest_start", placed[0][1:], [], gw[0]["w_in"], 0)
    upper = []
    for li in range(1, depth):
        sems_u, arrs_u, token = _gather_call("gather_upper_start", placed[li], [], token, 0)
        upper.append([li, sems_u, arrs_u])

    def arrive(li, stage, after):
        nonlocal sems_a, arrs_a
        if li == 0 and stage == "mixed":
            sems_a, arrs_a, _ = _gather_call("gather_rest_forward", arrs_a, sems_a, after, 1)
            _, arrs_a, _ = _gather_call("gather_rest_finish", arrs_a, sems_a, after, 2)
            gw[0].update(zip(rest, arrs_a))
        if li + 1 < depth and stage == "ffn":
            u = upper[li]
            u[1], u[2], _ = _gather_call("gather_upper_forward", u[2], u[1], after, 1)
        if li + 1 < depth and stage == "end":
            u = upper[li]
            _, u[2], _ = _gather_call("gather_upper_finish", u[2], u[1], after, 2)
            gw[u[0]].update(zip(BIG, u[2]))

    small = {n: weights[n] for n in REPLICATED}
    small["norm1_g"] = small["norm1_g"] + token[0, 0]
    sflat = _pad_rows(jnp.concatenate([weights[n].reshape(-1) for n in SMALL_SHARDED]), 8)
    sgath = _all_gather_devices(sflat)[0::2].reshape(4, -1)
    off = 0
    for n in SMALL_SHARDED:
        shp = weights[n].shape
        part = sgath[:, off:off + weights[n].size].reshape((4,) + shp)
        small[n] = jnp.moveaxis(part, 0, -2).reshape(shp[:-1] + (4 * shp[-1],))
        off += weights[n].size

    def send_grads(li, grads):
        started = _exchange_start(_pair_sums(grads, c_idx))
        return started[:4], started[4]

    loss_local, dx, big_grads, small_grads = _local_step(x[0], p[:, 0], loss_target[0], gw, small, send_grads, arrive)

    accs = [None] * len(BIG)
    for li in reversed(range(depth)):
        if li > 0:
            pairs, recvs = _exchange_wait(*big_grads[li], dx)
        else:
            pairs = _pair_sums(big_grads[li], c_idx)
            recvs = _exchange_chips(pairs)
        accs = _sum_into(pairs, recvs, idx, li, depth, accs)
    gshard = dict(zip(BIG, _sibling_share(accs)))

    rnames = REPLICATED + SMALL_SHARDED
    rflat = _pad_rows(jnp.concatenate([small_grads[n].reshape(-1) for n in rnames]), 8)
    rsum = _sum_slots(_all_gather_devices(rflat)).reshape(-1)
    off = 0
    for n in rnames:
        whole = rsum[off:off + small_grads[n].size].reshape(small_grads[n].shape)
        off += small_grads[n].size
        if n in SMALL_SHARDED:
            cols = weights[n].shape[-1]
            whole = lax.dynamic_slice_in_dim(whole, chip * cols, cols, axis=whole.ndim - 1)
        gshard[n] = whole

    loss = lax.psum(loss_local, ("x", "y", "c"))
    deltas, new_m, new_v, grad_out = {}, {}, {}, {}
    for n in ALL_WEIGHTS:
        deltas[n], new_m[n], new_v[n], grad_out[n] = _adamw(weights[n], gshard[n], mom_m[n], mom_v[n])
    return (loss, dx[None], *[grad_out[n] for n in ALL_WEIGHTS], *[deltas[n] for n in ALL_WEIGHTS],
            *[new_m[n] for n in ALL_WEIGHTS], *[new_v[n] for n in ALL_WEIGHTS])
```

```python
import jax
import jax.numpy as jnp
from jax import lax
from jax.experimental import pallas as pl
from jax.experimental.pallas import tpu as pltpu

F32 = jnp.float32
MM_DTYPE = jnp.bfloat16
WIRE_DTYPE = jnp.bfloat16
HI = lax.Precision.HIGHEST
EPS = 1e-6
HEAD_DIM = 128
CHUNK = 64
QKV_CONV_WIDTH = 4
SCONV_WIDTH = 3
POOL_GROUPS = 4
LANES = 128
SUBLANES_WIRE = 16
VMEM_LIMIT_BYTES = 56 * 1024 * 1024
ADAM_LR, ADAM_B1, ADAM_B2, ADAM_EPS, ADAM_WD, ADAM_STEP = 0.001, 0.9, 0.999, 1e-08, 0.01, 10
MESH = pl.DeviceIdType.MESH
ANY = pl.BlockSpec(memory_space=pl.ANY)
HBM = pl.BlockSpec(memory_space=pltpu.HBM)
SEM = pl.BlockSpec(memory_space=pltpu.SEMAPHORE)


def _params(*sem):
    return pltpu.CompilerParams(vmem_limit_bytes=VMEM_LIMIT_BYTES, dimension_semantics=sem if sem else None)


def _mm(a, b):
    return jnp.dot(a.astype(MM_DTYPE), b.astype(MM_DTYPE), preferred_element_type=F32)


def _mm_nt(a, b):
    return lax.dot_general(a.astype(MM_DTYPE), b.astype(MM_DTYPE), (((1,), (1,)), ((), ())), preferred_element_type=F32)


def _mm_tn(a, b):
    return lax.dot_general(a.astype(MM_DTYPE), b.astype(MM_DTYPE), (((0,), (0,)), ((), ())), preferred_element_type=F32)


def _hmm(a, b):
    return jnp.dot(a, b, preferred_element_type=F32, precision=HI)


def _hmm_nt(a, b):
    return lax.dot_general(a, b, (((1,), (1,)), ((), ())), preferred_element_type=F32, precision=HI)


def _hmm_tn(a, b):
    return lax.dot_general(a, b, (((0,), (0,)), ((), ())), preferred_element_type=F32, precision=HI)


def _sigmoid(x):
    return 1.0 / (1.0 + jnp.exp(-x))


def _dsilu(x, s):
    return s * (1.0 + x * (1.0 - s))


def _rows(shape):
    return lax.broadcasted_iota(jnp.int32, shape, 0)


def _shift_down(x, s):
    if s == 0:
        return x
    return jnp.where(_rows(x.shape) >= s, pltpu.roll(x, s, 0), 0.0)


def _shift_up(x, s):
    if s == 0:
        return x
    t = x.shape[0]
    return jnp.where(_rows(x.shape) < t - s, pltpu.roll(x, t - s, 0), 0.0)


def _rms_fwd(x):
    r = lax.rsqrt(jnp.mean(x * x, axis=-1, keepdims=True) + EPS)
    return x * r, r


def _rms_bwd(dxn, xn, r):
    return r * (dxn - xn * jnp.mean(dxn * xn, axis=-1, keepdims=True))


def _tile_rows(n, cap, mult=8):
    best = None
    for d in range(mult, min(n, cap) + 1, mult):
        if n % d == 0:
            best = d
    return best if best is not None else n


def _in_proj_fwd(x, g1, wp, segs, tm):
    t, d = x.shape
    npk = wp.shape[1]

    def body(x_ref, g_ref, w_ref, *o_refs):
        xn, _ = _rms_fwd(x_ref[...])
        h = (xn * g_ref[...]).astype(w_ref.dtype)
        off = 0
        for o_ref, wd in zip(o_refs, segs):
            o_ref[...] = jnp.dot(h, w_ref[:, off:off + wd], preferred_element_type=F32)
            off += wd

    return pl.pallas_call(
        body, name="in_proj_fwd", grid=(t // tm,),
        in_specs=[pl.BlockSpec((tm, d), lambda i: (i, 0)), pl.BlockSpec((1, d), lambda i: (0, 0)),
                  pl.BlockSpec((d, npk), lambda i: (0, 0))],
        out_specs=[pl.BlockSpec((tm, wd), lambda i: (i, 0)) for wd in segs],
        out_shape=[jax.ShapeDtypeStruct((t, wd), F32) for wd in segs],
        compiler_params=_params("arbitrary"))(x, g1, wp)


def _in_proj_bwd(x, g1, wp, dsegs, dx_res, segs, tm):
    t, d = x.shape
    npk = wp.shape[1]
    nseg = len(segs)

    def body(x_ref, g_ref, w_ref, *rest):
        ds_refs = rest[:nseg]
        dxr_ref, dx_ref, dw_ref, dg_ref = rest[nseg:]
        i = pl.program_id(0)

        @pl.when(i == 0)
        def _():
            dw_ref[...] = jnp.zeros_like(dw_ref)
            dg_ref[...] = jnp.zeros_like(dg_ref)

        xn, r = _rms_fwd(x_ref[...])
        g = g_ref[...]
        h = (xn * g).astype(w_ref.dtype)
        dh = jnp.zeros((tm, d), F32)
        off = 0
        for ds_ref, wd in zip(ds_refs, segs):
            dsv = ds_ref[...].astype(w_ref.dtype)
            dh = dh + lax.dot_general(dsv, w_ref[:, off:off + wd], (((1,), (1,)), ((), ())), preferred_element_type=F32)
            dw_ref[:, off:off + wd] += lax.dot_general(h, dsv, (((0,), (0,)), ((), ())), preferred_element_type=F32)
            off += wd
        dg_ref[...] += jnp.sum(dh * xn, axis=0, keepdims=True)
        dx_ref[...] = dxr_ref[...] + _rms_bwd(dh * g, xn, r)

    return pl.pallas_call(
        body, name="in_proj_bwd", grid=(t // tm,),
        in_specs=[pl.BlockSpec((tm, d), lambda i: (i, 0)), pl.BlockSpec((1, d), lambda i: (0, 0)),
                  pl.BlockSpec((d, npk), lambda i: (0, 0))]
                 + [pl.BlockSpec((tm, wd), lambda i: (i, 0)) for wd in segs]
                 + [pl.BlockSpec((tm, d), lambda i: (i, 0))],
        out_specs=[pl.BlockSpec((tm, d), lambda i: (i, 0)), pl.BlockSpec((d, npk), lambda i: (0, 0)),
                   pl.BlockSpec((1, d), lambda i: (0, 0))],
        out_shape=[jax.ShapeDtypeStruct((t, d), F32), jax.ShapeDtypeStruct((d, npk), F32),
                   jax.ShapeDtypeStruct((1, d), F32)],
        compiler_params=_params("arbitrary"))(x, g1, wp, *dsegs, dx_res)


def _out_proj_fwd(x0, mix, wo, g2, tm):
    t, d = x0.shape
    dq = wo.shape[1]
    widths = [m.shape[1] for m in mix]

    def body(x_ref, *rest):
        m_refs = rest[:len(mix)]
        w_ref, g_ref, x1_ref, h2_ref = rest[len(mix):]
        acc = x_ref[...]
        off = 0
        for m_ref, wd in zip(m_refs, widths):
            for k in range(wd // dq):
                acc = acc + jnp.dot(m_ref[:, k * dq:(k + 1) * dq].astype(w_ref.dtype), w_ref[off // dq + k],
                                    preferred_element_type=F32)
            off += wd
        x1_ref[...] = acc
        xn, _ = _rms_fwd(acc)
        h2_ref[...] = (xn * g_ref[...]).astype(h2_ref.dtype)

    return pl.pallas_call(
        body, name="out_proj_fwd", grid=(t // tm,),
        in_specs=[pl.BlockSpec((tm, d), lambda i: (i, 0))]
                 + [pl.BlockSpec((tm, wd), lambda i: (i, 0)) for wd in widths]
                 + [pl.BlockSpec((4, dq, d), lambda i: (0, 0, 0)), pl.BlockSpec((1, d), lambda i: (0, 0))],
        out_specs=[pl.BlockSpec((tm, d), lambda i: (i, 0)), pl.BlockSpec((tm, d), lambda i: (i, 0))],
        out_shape=[jax.ShapeDtypeStruct((t, d), F32), jax.ShapeDtypeStruct((t, d), MM_DTYPE)],
        compiler_params=_params("arbitrary"))(x0, *mix, wo, g2)


def _out_proj_bwd(dx2, dh2, x1, g2, mix, wo, tm):
    t, d = x1.shape
    dq = wo.shape[1]
    widths = [m.shape[1] for m in mix]
    nm = len(mix)

    def body(dx2_ref, dh2_ref, x1_ref, g_ref, *rest):
        m_refs = rest[:nm]
        w_ref = rest[nm]
        dx1_ref = rest[nm + 1]
        dm_refs = rest[nm + 2:nm + 2 + nm]
        dw_ref, dg_ref = rest[nm + 2 + nm:]
        i = pl.program_id(0)

        @pl.when(i == 0)
        def _():
            dw_ref[...] = jnp.zeros_like(dw_ref)
            dg_ref[...] = jnp.zeros_like(dg_ref)

        xn, r = _rms_fwd(x1_ref[...])
        dh2v = dh2_ref[...]
        dg_ref[...] += jnp.sum(dh2v * xn, axis=0, keepdims=True)
        dx1 = dx2_ref[...] + _rms_bwd(dh2v * g_ref[...], xn, r)
        dx1_ref[...] = dx1
        dx1c = dx1.astype(w_ref.dtype)
        off = 0
        for m_ref, dm_ref, wd in zip(m_refs, dm_refs, widths):
            for k in range(wd // dq):
                j = off // dq + k
                cols = slice(k * dq, (k + 1) * dq)
                dm_ref[:, cols] = lax.dot_general(dx1c, w_ref[j], (((1,), (1,)), ((), ())), preferred_element_type=F32)
                dw_ref[j] += lax.dot_general(m_ref[:, cols].astype(w_ref.dtype), dx1c, (((0,), (0,)), ((), ())),
                                             preferred_element_type=F32)
            off += wd

    tile = lambda wd: pl.BlockSpec((tm, wd), lambda i: (i, 0))
    return pl.pallas_call(
        body, name="out_proj_bwd", grid=(t // tm,),
        in_specs=[tile(d), tile(d), tile(d), pl.BlockSpec((1, d), lambda i: (0, 0))]
                 + [tile(wd) for wd in widths] + [pl.BlockSpec((4, dq, d), lambda i: (0, 0, 0))],
        out_specs=[tile(d)] + [tile(wd) for wd in widths]
                  + [pl.BlockSpec((4, dq, d), lambda i: (0, 0, 0)), pl.BlockSpec((1, d), lambda i: (0, 0))],
        out_shape=[jax.ShapeDtypeStruct((t, d), F32)] + [jax.ShapeDtypeStruct((t, wd), F32) for wd in widths]
                  + [jax.ShapeDtypeStruct((4, dq, d), F32), jax.ShapeDtypeStruct((1, d), F32)],
        compiler_params=_params("arbitrary"))(dx2, dh2, x1, g2, *mix, wo)


def _ffn_fwd(x1, h2, wg, wu, wd, tm):
    t, d = x1.shape
    fs = wg.shape[2]

    def body(x1_ref, h2_ref, wg_ref, wu_ref, wd_ref, x2_ref, gp_ref, up_ref):
        @pl.when(pl.program_id(1) == 0)
        def _():
            x2_ref[...] = x1_ref[...]

        h = h2_ref[...]
        gp = jnp.dot(h, wg_ref[...], preferred_element_type=F32)
        up = jnp.dot(h, wu_ref[...], preferred_element_type=F32)
        gp_ref[...] = gp
        up_ref[...] = up
        ff = gp * _sigmoid(gp) * up
        x2_ref[...] += jnp.dot(ff.astype(wd_ref.dtype), wd_ref[...], preferred_element_type=F32)

    return pl.pallas_call(
        body, name="ffn_fwd", grid=(t // tm, 4),
        in_specs=[pl.BlockSpec((tm, d), lambda i, j: (i, 0)), pl.BlockSpec((tm, d), lambda i, j: (i, 0)),
                  pl.BlockSpec((None, d, fs), lambda i, j: (j, 0, 0)),
                  pl.BlockSpec((None, d, fs), lambda i, j: (j, 0, 0)),
                  pl.BlockSpec((None, fs, d), lambda i, j: (j, 0, 0))],
        out_specs=[pl.BlockSpec((tm, d), lambda i, j: (i, 0)), pl.BlockSpec((None, tm, fs), lambda i, j: (j, i, 0)),
                   pl.BlockSpec((None, tm, fs), lambda i, j: (j, i, 0))],
        out_shape=[jax.ShapeDtypeStruct((t, d), F32), jax.ShapeDtypeStruct((4, t, fs), F32),
                   jax.ShapeDtypeStruct((4, t, fs), F32)],
        compiler_params=_params("arbitrary", "arbitrary"))(x1, h2, wg, wu, wd)


def _ffn_bwd(dx2, h2, gp, up, wg, wu, wd, tm):
    t, d = dx2.shape
    fs = wg.shape[2]

    def body(dx2_ref, h2_ref, gp_ref, up_ref, wg_ref, wu_ref, wd_ref, dh2_ref, dwg_ref, dwu_ref, dwd_ref):
        j, i = pl.program_id(0), pl.program_id(1)

        @pl.when(i == 0)
        def _():
            dwg_ref[...] = jnp.zeros_like(dwg_ref)
            dwu_ref[...] = jnp.zeros_like(dwu_ref)
            dwd_ref[...] = jnp.zeros_like(dwd_ref)

        cdt = wg_ref.dtype
        h = h2_ref[...]
        gpv, upv = gp_ref[...], up_ref[...]
        s = _sigmoid(gpv)
        silu = gpv * s
        dx2c = dx2_ref[...].astype(cdt)
        dff = lax.dot_general(dx2c, wd_ref[...], (((1,), (1,)), ((), ())), preferred_element_type=F32)
        dwd_ref[...] += lax.dot_general((silu * upv).astype(cdt), dx2c, (((0,), (0,)), ((), ())), preferred_element_type=F32)
        dup = (dff * silu).astype(cdt)
        dgp = (dff * upv * _dsilu(gpv, s)).astype(cdt)
        dwg_ref[...] += lax.dot_general(h, dgp, (((0,), (0,)), ((), ())), preferred_element_type=F32)
        dwu_ref[...] += lax.dot_general(h, dup, (((0,), (0,)), ((), ())), preferred_element_type=F32)
        dh = (lax.dot_general(dgp, wg_ref[...], (((1,), (1,)), ((), ())), preferred_element_type=F32)
              + lax.dot_general(dup, wu_ref[...], (((1,), (1,)), ((), ())), preferred_element_type=F32))
        rows = pl.ds(pl.multiple_of(i * tm, tm), tm)

        @pl.when(j == 0)
        def _():
            dh2_ref[rows, :] = dh

        @pl.when(j != 0)
        def _():
            dh2_ref[rows, :] += dh

    return pl.pallas_call(
        body, name="ffn_bwd", grid=(4, t // tm),
        in_specs=[pl.BlockSpec((tm, d), lambda j, i: (i, 0)), pl.BlockSpec((tm, d), lambda j, i: (i, 0)),
                  pl.BlockSpec((None, tm, fs), lambda j, i: (j, i, 0)), pl.BlockSpec((None, tm, fs), lambda j, i: (j, i, 0)),
                  pl.BlockSpec((None, d, fs), lambda j, i: (j, 0, 0)),
                  pl.BlockSpec((None, d, fs), lambda j, i: (j, 0, 0)),
                  pl.BlockSpec((None, fs, d), lambda j, i: (j, 0, 0))],
        out_specs=[pl.BlockSpec((t, d), lambda j, i: (0, 0)), pl.BlockSpec((None, d, fs), lambda j, i: (j, 0, 0)),
                   pl.BlockSpec((None, d, fs), lambda j, i: (j, 0, 0)), pl.BlockSpec((None, fs, d), lambda j, i: (j, 0, 0))],
        out_shape=[jax.ShapeDtypeStruct((t, d), F32), jax.ShapeDtypeStruct((4, d, fs), F32),
                   jax.ShapeDtypeStruct((4, d, fs), F32), jax.ShapeDtypeStruct((4, fs, d), F32)],
        compiler_params=_params("arbitrary", "arbitrary"))(dx2, h2, gp, up, wg, wu, wd)


def _ple_fwd(x2, p, wpg, wpp, tm):
    t, d = x2.shape
    q = p.shape[1]
    dq = d // 4

    def body(x_ref, p_ref, wg_ref, wp_ref, o_ref):
        xv = x_ref[...]
        xc = xv.astype(wg_ref.dtype)
        pc = p_ref[...].astype(wp_ref.dtype)
        pre = jnp.dot(xc[:, :dq], wg_ref[0], preferred_element_type=F32)
        for j in range(1, 4):
            pre = pre + jnp.dot(xc[:, j * dq:(j + 1) * dq], wg_ref[j], preferred_element_type=F32)
        gate = _sigmoid(pre)
        for j in range(4):
            cols = slice(j * dq, (j + 1) * dq)
            o_ref[:, cols] = xv[:, cols] + gate[:, cols] * jnp.dot(pc, wp_ref[j], preferred_element_type=F32)

    return pl.pallas_call(
        body, name="ple_fwd", grid=(t // tm,),
        in_specs=[pl.BlockSpec((tm, d), lambda i: (i, 0)), pl.BlockSpec((tm, q), lambda i: (i, 0)),
                  pl.BlockSpec((4, dq, d), lambda i: (0, 0, 0)),
                  pl.BlockSpec((4, q, dq), lambda i: (0, 0, 0))],
        out_specs=pl.BlockSpec((tm, d), lambda i: (i, 0)),
        out_shape=jax.ShapeDtypeStruct((t, d), F32),
        compiler_params=_params("arbitrary"))(x2, p, wpg, wpp)


def _ple_bwd(dx3, x2, p, wpg, wpp, tm):
    t, d = x2.shape
    q = p.shape[1]
    dq = d // 4

    def body(dx3_ref, x_ref, p_ref, wg_ref, wp_ref, dx2_ref, dwg_ref, dwp_ref):
        @pl.when(pl.program_id(0) == 0)
        def _():
            dwg_ref[...] = jnp.zeros_like(dwg_ref)
            dwp_ref[...] = jnp.zeros_like(dwp_ref)

        cdt = wg_ref.dtype
        xc = x_ref[...].astype(cdt)
        pc = p_ref[...].astype(cdt)
        pre = jnp.dot(xc[:, :dq], wg_ref[0], preferred_element_type=F32)
        for j in range(1, 4):
            pre = pre + jnp.dot(xc[:, j * dq:(j + 1) * dq], wg_ref[j], preferred_element_type=F32)
        gate = _sigmoid(pre)
        dx3v = dx3_ref[...]
        dpp = (dx3v * gate).astype(cdt)
        dgate = dx3v * gate * (1.0 - gate)
        dpre_parts = []
        for j in range(4):
            cols = slice(j * dq, (j + 1) * dq)
            pp_j = jnp.dot(pc, wp_ref[j], preferred_element_type=F32)
            dpre_parts.append((dgate[:, cols] * pp_j).astype(cdt))
            dwp_ref[j] += lax.dot_general(pc, dpp[:, cols], (((0,), (0,)), ((), ())), preferred_element_type=F32)
        dpre = jnp.concatenate(dpre_parts, axis=1)
        for j in range(4):
            cols = slice(j * dq, (j + 1) * dq)
            dwg_ref[j] += lax.dot_general(xc[:, cols], dpre, (((0,), (0,)), ((), ())), preferred_element_type=F32)
            dx2_ref[:, cols] = dx3v[:, cols] + lax.dot_general(dpre, wg_ref[j], (((1,), (1,)), ((), ())),
                                                               preferred_element_type=F32)

    return pl.pallas_call(
        body, name="ple_bwd", grid=(t // tm,),
        in_specs=[pl.BlockSpec((tm, d), lambda i: (i, 0)), pl.BlockSpec((tm, d), lambda i: (i, 0)),
                  pl.BlockSpec((tm, q), lambda i: (i, 0)), pl.BlockSpec((4, dq, d), lambda i: (0, 0, 0)),
                  pl.BlockSpec((4, q, dq), lambda i: (0, 0, 0))],
        out_specs=[pl.BlockSpec((tm, d), lambda i: (i, 0)), pl.BlockSpec((4, dq, d), lambda i: (0, 0, 0)),
                   pl.BlockSpec((4, q, dq), lambda i: (0, 0, 0))],
        out_shape=[jax.ShapeDtypeStruct((t, d), F32), jax.ShapeDtypeStruct((4, dq, d), F32),
                   jax.ShapeDtypeStruct((4, q, dq), F32)],
        compiler_params=_params("arbitrary"))(dx3, x2, p, wpg, wpp)


def _loss_head(x, target, fg, tm):
    t, d = x.shape

    def body(x_ref, t_ref, g_ref, dx_ref, loss_ref, dg_ref):
        @pl.when(pl.program_id(0) == 0)
        def _():
            loss_ref[...] = jnp.zeros_like(loss_ref)
            dg_ref[...] = jnp.zeros_like(dg_ref)

        xn, r = _rms_fwd(x_ref[...])
        g = g_ref[...]
        err = xn * g - t_ref[...]
        loss_ref[...] += 0.5 * jnp.sum(jnp.sum(err * err, axis=-1, keepdims=True) / d, axis=0, keepdims=True)
        dy = err / d
        dg_ref[...] += jnp.sum(dy * xn, axis=0, keepdims=True)
        dx_ref[...] = _rms_bwd(dy * g, xn, r)

    return pl.pallas_call(
        body, name="loss_head", grid=(t // tm,),
        in_specs=[pl.BlockSpec((tm, d), lambda i: (i, 0)), pl.BlockSpec((tm, d), lambda i: (i, 0)),
                  pl.BlockSpec((1, d), lambda i: (0, 0))],
        out_specs=[pl.BlockSpec((tm, d), lambda i: (i, 0)), pl.BlockSpec((1, 1), lambda i: (0, 0)),
                   pl.BlockSpec((1, d), lambda i: (0, 0))],
        out_shape=[jax.ShapeDtypeStruct((t, d), F32), jax.ShapeDtypeStruct((1, 1), F32),
                   jax.ShapeDtypeStruct((1, d), F32)],
        compiler_params=_params("arbitrary"))(x, target, fg)


def _qkv_conv_act(xv, w, j, heads):
    k = QKV_CONV_WIDTH
    y = w[k - 1:k] * xv
    for s in range(1, k):
        y = y + w[k - 1 - s:k - s] * _shift_down(xv, s)
    sg = _sigmoid(y)
    s_act = y * sg
    nrm = lax.rsqrt(jnp.sum(s_act * s_act, axis=-1, keepdims=True) + EPS)
    scale = jnp.where(j < heads, HEAD_DIM ** -0.5, 1.0).astype(F32)
    return y, sg, s_act, nrm, scale


def _qkv_conv_fwd(qkv_pre, conv_w, heads):
    t = qkv_pre.shape[0]
    nblk = 3 * heads

    def body(x_ref, w_ref, o_ref):
        j = pl.program_id(0)
        _, _, s_act, nrm, scale = _qkv_conv_act(x_ref[...], w_ref[...], j, heads)
        o_ref[...] = jnp.where(j < 2 * heads, s_act * (nrm * scale), s_act)

    return pl.pallas_call(
        body, name="qkv_conv_fwd", grid=(nblk,),
        in_specs=[pl.BlockSpec((t, LANES), lambda j: (0, j)), pl.BlockSpec((QKV_CONV_WIDTH, LANES), lambda j: (0, j))],
        out_specs=pl.BlockSpec((t, LANES), lambda j: (0, j)),
        out_shape=jax.ShapeDtypeStruct(qkv_pre.shape, F32),
        compiler_params=_params("arbitrary"))(qkv_pre, conv_w)


def _qkv_conv_bwd(qkv_pre, conv_w, dqkv, heads):
    t = qkv_pre.shape[0]
    nblk = 3 * heads
    k = QKV_CONV_WIDTH

    def body(x_ref, w_ref, dn_ref, dx_ref, dw_ref):
        j = pl.program_id(0)
        xv, w = x_ref[...], w_ref[...]
        y, sg, s_act, nrm, scale = _qkv_conv_act(xv, w, j, heads)
        dn = dn_ref[...]
        dsn = dn * scale
        ds_qk = nrm * dsn - s_act * (nrm * nrm * nrm) * jnp.sum(dsn * s_act, axis=-1, keepdims=True)
        ds = jnp.where(j < 2 * heads, ds_qk, dn)
        dy = ds * _dsilu(y, sg)
        dx = w[k - 1:k] * dy
        dw_ref[k - 1:k, :] = jnp.sum(dy * xv, axis=0, keepdims=True)
        for s in range(1, k):
            dx = dx + w[k - 1 - s:k - s] * _shift_up(dy, s)
            dw_ref[k - 1 - s:k - s, :] = jnp.sum(dy * _shift_down(xv, s), axis=0, keepdims=True)
        dx_ref[...] = dx

    return pl.pallas_call(
        body, name="qkv_conv_bwd", grid=(nblk,),
        in_specs=[pl.BlockSpec((t, LANES), lambda j: (0, j)), pl.BlockSpec((k, LANES), lambda j: (0, j)),
                  pl.BlockSpec((t, LANES), lambda j: (0, j))],
        out_specs=[pl.BlockSpec((t, LANES), lambda j: (0, j)), pl.BlockSpec((k, LANES), lambda j: (0, j))],
        out_shape=[jax.ShapeDtypeStruct(qkv_pre.shape, F32), jax.ShapeDtypeStruct(conv_w.shape, F32)],
        compiler_params=_params("arbitrary"))(qkv_pre, conv_w, dqkv)


def _pool_windows(shape, j, group_dim):
    lane = lax.broadcasted_iota(jnp.int32, shape, 1) + j * LANES
    grp = lane // group_dim
    win = jnp.left_shift(2, grp).astype(F32)
    cnt = jnp.minimum((_rows(shape) + 1).astype(F32), win)
    return grp, cnt


def _pool_select(grp, levels):
    out = levels[0]
    for gi in range(1, POOL_GROUPS):
        out = jnp.where(grp == gi, levels[gi], out)
    return out


def _pool_mean(hv, grp, cnt):
    acc, levels, width = hv, [], 1
    for _ in range(POOL_GROUPS):
        acc = acc + _shift_down(acc, width)
        width *= 2
        levels.append(acc)
    return _pool_select(grp, levels) / cnt - hv


def _pool_fwd(hp, wbd, scale, group_dim):
    t, dp = hp.shape

    def body(h_ref, w_ref, s_ref, o_ref):
        hv = h_ref[...]
        grp, cnt = _pool_windows(hv.shape, pl.program_id(0), group_dim)
        pooled = _pool_mean(hv, grp, cnt)
        o_ref[...] = _mm(pooled, w_ref[...]) * s_ref[...]

    return pl.pallas_call(
        body, name="pool_fwd", grid=(dp // LANES,),
        in_specs=[pl.BlockSpec((t, LANES), lambda j: (0, j)), pl.BlockSpec((LANES, LANES), lambda j: (j, j)),
                  pl.BlockSpec((1, LANES), lambda j: (0, j))],
        out_specs=pl.BlockSpec((t, LANES), lambda j: (0, j)),
        out_shape=jax.ShapeDtypeStruct(hp.shape, F32),
        compiler_params=_params("arbitrary"))(hp, wbd, scale)


def _pool_bwd(hp, wbd, scale, dob, group_dim):
    t, dp = hp.shape

    def body(h_ref, w_ref, s_ref, do_ref, dh_ref, dw_ref, ds_ref):
        hv = h_ref[...]
        grp, cnt = _pool_windows(hv.shape, pl.program_id(0), group_dim)
        pooled = _pool_mean(hv, grp, cnt)
        wv = w_ref[...]
        dov = do_ref[...]
        ds_ref[...] = jnp.sum(dov * _mm(pooled, wv), axis=0, keepdims=True)
        dys = dov * s_ref[...]
        dw_ref[0] = _mm_tn(pooled, dys)
        dpooled = _mm_nt(dys, wv)
        acc, levels, width = dpooled / cnt, [], 1
        for _ in range(POOL_GROUPS):
            acc = acc + _shift_up(acc, width)
            width *= 2
            levels.append(acc)
        dh_ref[...] = _pool_select(grp, levels) - dpooled

    nb = dp // LANES
    return pl.pallas_call(
        body, name="pool_bwd", grid=(nb,),
        in_specs=[pl.BlockSpec((t, LANES), lambda j: (0, j)), pl.BlockSpec((LANES, LANES), lambda j: (j, j)),
                  pl.BlockSpec((1, LANES), lambda j: (0, j)), pl.BlockSpec((t, LANES), lambda j: (0, j))],
        out_specs=[pl.BlockSpec((t, LANES), lambda j: (0, j)), pl.BlockSpec((1, LANES, LANES), lambda j: (j, 0, 0)),
                   pl.BlockSpec((1, LANES), lambda j: (0, j))],
        out_shape=[jax.ShapeDtypeStruct(hp.shape, F32), jax.ShapeDtypeStruct((nb, LANES, LANES), F32),
                   jax.ShapeDtypeStruct((1, dp), F32)],
        compiler_params=_params("arbitrary"))(hp, wbd, scale, dob)


def _sconv_fwd(cbcch, w):
    t, dc3 = cbcch.shape
    nb = dc3 // 3 // LANES
    k = SCONV_WIDTH

    def body(b_ref, c_ref, h_ref, w_ref, o_ref):
        m = c_ref[...] * h_ref[...]
        wv = w_ref[...]
        y = wv[k - 1:k] * m
        for s in range(1, k):
            y = y + wv[k - 1 - s:k - s] * _shift_down(m, s)
        o_ref[...] = b_ref[...] * y

    return pl.pallas_call(
        body, name="sconv_fwd", grid=(nb,),
        in_specs=[pl.BlockSpec((t, LANES), lambda j: (0, j)), pl.BlockSpec((t, LANES), lambda j: (0, nb + j)),
                  pl.BlockSpec((t, LANES), lambda j: (0, 2 * nb + j)), pl.BlockSpec((k, LANES), lambda j: (0, j))],
        out_specs=pl.BlockSpec((t, LANES), lambda j: (0, j)),
        out_shape=jax.ShapeDtypeStruct((t, dc3 // 3), F32),
        compiler_params=_params("arbitrary"))(cbcch, cbcch, cbcch, w)


def _sconv_bwd(cbcch, w, doc):
    t, dc3 = cbcch.shape
    nb = dc3 // 3 // LANES
    k = SCONV_WIDTH

    def body(b_ref, c_ref, h_ref, w_ref, do_ref, db_ref, dc_ref, dh_ref, dw_ref):
        cv, hv = c_ref[...], h_ref[...]
        m = cv * hv
        wv = w_ref[...]
        dov = do_ref[...]
        dy = dov * b_ref[...]
        y = wv[k - 1:k] * m
        dm = wv[k - 1:k] * dy
        dw_ref[k - 1:k, :] = jnp.sum(dy * m, axis=0, keepdims=True)
        for s in range(1, k):
            ms = _shift_down(m, s)
            y = y + wv[k - 1 - s:k - s] * ms
            dm = dm + wv[k - 1 - s:k - s] * _shift_up(dy, s)
            dw_ref[k - 1 - s:k - s, :] = jnp.sum(dy * ms, axis=0, keepdims=True)
        db_ref[...] = dov * y
        dc_ref[...] = dm * hv
        dh_ref[...] = dm * cv

    col = lambda o: pl.BlockSpec((t, LANES), lambda j: (0, o * nb + j))
    return pl.pallas_call(
        body, name="sconv_bwd", grid=(nb,),
        in_specs=[col(0), col(1), col(2), pl.BlockSpec((k, LANES), lambda j: (0, j)), col(0)],
        out_specs=[col(0), col(0), col(0), pl.BlockSpec((k, LANES), lambda j: (0, j))],
        out_shape=[jax.ShapeDtypeStruct((t, dc3 // 3), F32)] * 3 + [jax.ShapeDtypeStruct(w.shape, F32)],
        compiler_params=_params("arbitrary"))(cbcch, cbcch, cbcch, w, doc)


def _per_head(fn, a, b):
    return jnp.stack([fn(a[h], b[h]) for h in range(a.shape[0])])


def _bmm(a, b):
    return _per_head(_hmm, a, b)


def _bmm_nt(a, b):
    return _per_head(_hmm_nt, a, b)


def _bmm_tn(a, b):
    return _per_head(_hmm_tn, a, b)


def _inv_unit_lower(low):
    c = low.shape[-1]
    eye = (_rows((c, c)) == lax.broadcasted_iota(jnp.int32, (c, c), 1)).astype(F32)
    pw = -low
    inv = eye + pw
    span = 2
    while span < c:
        pw = _bmm(pw, pw)
        inv = inv + _bmm(inv, pw)
        span *= 2
    return inv


def _heads_of(ref, base, heads):
    return jnp.stack([ref[:, base + h * HEAD_DIM:base + (h + 1) * HEAD_DIM] for h in range(heads)])


def _chunk_common(q, k, v, a_col, b_col, alog, dtb, kept=None):
    hn, c, _ = q.shape
    beta = _sigmoid(b_col)
    xg = a_col + dtb
    softplus = jnp.maximum(xg, 0.0) + jnp.log(1.0 + jnp.exp(-jnp.abs(xg)))
    neg_ea = -jnp.exp(alog)
    g = neg_ea * softplus
    ri = _rows((c, c))
    ci = lax.broadcasted_iota(jnp.int32, (c, c), 1)
    incl, strict = ri >= ci, ri > ci
    inclf = jnp.broadcast_to(incl.astype(F32), (hn, c, c))
    gcb = _bmm(inclf, jnp.broadcast_to(g, (hn, c, HEAD_DIM)))
    gc_row = jnp.sum(jnp.where(ri <= ci, jnp.broadcast_to(g, (hn, c, c)), 0.0), axis=1, keepdims=True)
    dmat = jnp.where(incl, jnp.exp(jnp.where(incl, gcb[:, :, :1] - gc_row, 0.0)), 0.0)
    eg = jnp.exp(gcb)
    gl = gcb[:, c - 1:c, :]
    egl = jnp.exp(gl)
    edl = jnp.exp(gl - gcb)
    kb, vb = k * beta, v * beta
    kbe = kb * eg
    if kept is None:
        a0 = _bmm_nt(kb, k)
        tm = _inv_unit_lower(jnp.where(strict, a0 * dmat, 0.0))
        p0 = _bmm_nt(q, k)
        u, w = _bmm(tm, vb), _bmm(tm, kbe)
    else:
        (a0, tm, p0, w), u = kept, None
    return dict(beta=beta, xg=xg, neg_ea=neg_ea, g=g, incl=incl, strict=strict, inclf=inclf, dmat=dmat, eg=eg,
                egl=egl, edl=edl, kb=kb, vb=vb, a0=a0, tm=tm, kbe=kbe, u=u, w=w, p0=p0,
                attn=p0 * dmat, qe=q * eg, kd=k * edl)


def _chunk_step(cm, state):
    vn = cm["u"] - _bmm(cm["w"], state)
    o = _bmm(cm["qe"], state) + _bmm(cm["attn"], vn)
    new_state = state * cm["egl"][:, :, :1] + _bmm_tn(cm["kd"], vn)
    return vn, o, new_state


def _gated_norm(o, zv, og):
    xo, ro = _rms_fwd(o)
    sgz = _sigmoid(zv)
    return xo, ro, sgz, xo * og * (zv * sgz)


def _gate_columns(abv, gpv, heads):
    a_col = jnp.stack([abv[:, h:h + 1] for h in range(heads)])
    b_col = jnp.stack([abv[:, heads + h:heads + h + 1] for h in range(heads)])
    alog = jnp.stack([gpv[0:1, h:h + 1] for h in range(heads)])
    dtb = jnp.stack([gpv[1:2, h:h + 1] for h in range(heads)])
    return a_col, b_col, alog, dtb


def _delta_fwd(qkv, z, ab, gpar, heads):
    t = qkv.shape[0]
    da = heads * HEAD_DIM
    n = t // CHUNK

    def body(qkv_ref, z_ref, ab_ref, gp_ref, oa_ref, st_ref, kc_ref, kw_ref, s_ref):
        @pl.when(pl.program_id(0) == 0)
        def _():
            s_ref[...] = jnp.zeros_like(s_ref)

        gpv = gp_ref[...]
        cm = _chunk_common(_heads_of(qkv_ref, 0, heads), _heads_of(qkv_ref, da, heads), _heads_of(qkv_ref, 2 * da, heads),
                           *_gate_columns(ab_ref[...], gpv, heads))
        state = s_ref[...]
        st_ref[0] = state
        vn, o, new_state = _chunk_step(cm, state)
        s_ref[...] = new_state
        for slot, val in enumerate((cm["a0"], cm["tm"], cm["p0"])):
            kc_ref[0, slot] = val
        for slot, val in enumerate((cm["w"], vn, o)):
            kw_ref[0, slot] = val
        oa = _gated_norm(o, _heads_of(z_ref, 0, heads), gpv[2:3, :])[3]
        for h in range(heads):
            oa_ref[:, h * HEAD_DIM:(h + 1) * HEAD_DIM] = oa[h]

    return pl.pallas_call(
        body, name="delta_fwd", grid=(n,),
        in_specs=[pl.BlockSpec((CHUNK, 3 * da), lambda i: (i, 0)), pl.BlockSpec((CHUNK, da), lambda i: (i, 0)),
                  pl.BlockSpec((CHUNK, LANES), lambda i: (i, 0)), pl.BlockSpec((8, LANES), lambda i: (0, 0))],
        out_specs=[pl.BlockSpec((CHUNK, da), lambda i: (i, 0)),
                   pl.BlockSpec((1, heads, HEAD_DIM, HEAD_DIM), lambda i: (i, 0, 0, 0)),
                   pl.BlockSpec((1, 3, heads, CHUNK, CHUNK), lambda i: (i, 0, 0, 0, 0)),
                   pl.BlockSpec((1, 3, heads, CHUNK, HEAD_DIM), lambda i: (i, 0, 0, 0, 0))],
        out_shape=[jax.ShapeDtypeStruct((t, da), F32), jax.ShapeDtypeStruct((n, heads, HEAD_DIM, HEAD_DIM), F32),
                   jax.ShapeDtypeStruct((n, 3, heads, CHUNK, CHUNK), F32),
                   jax.ShapeDtypeStruct((n, 3, heads, CHUNK, HEAD_DIM), F32)],
        scratch_shapes=[pltpu.VMEM((heads, HEAD_DIM, HEAD_DIM), F32)],
        compiler_params=_params("arbitrary"))(qkv, z, ab, gpar)


def _delta_bwd(qkv, z, ab, gpar, states, kept_c, kept_w, doa, heads):
    t = qkv.shape[0]
    da = heads * HEAD_DIM
    n = t // CHUNK
    c = CHUNK

    def body(qkv_ref, z_ref, ab_ref, gp_ref, st_ref, kc_ref, kw_ref, doa_ref, dqkv_ref, dz_ref, dab_ref, dpar_ref, ds_ref):
        @pl.when(pl.program_id(0) == 0)
        def _():
            ds_ref[...] = jnp.zeros_like(ds_ref)
            dpar_ref[...] = jnp.zeros_like(dpar_ref)

        gpv = gp_ref[...]
        og = gpv[2:3, :]
        q, k, v = _heads_of(qkv_ref, 0, heads), _heads_of(qkv_ref, da, heads), _heads_of(qkv_ref, 2 * da, heads)
        cm = _chunk_common(q, k, v, *_gate_columns(ab_ref[...], gpv, heads),
                           kept=(kc_ref[0, 0], kc_ref[0, 1], kc_ref[0, 2], kw_ref[0, 0]))
        state = st_ref[0]
        dsp = ds_ref[...]
        vn, o = kw_ref[0, 1], kw_ref[0, 2]
        zv = _heads_of(z_ref, 0, heads)
        xo, ro, sgz, _ = _gated_norm(o, zv, og)
        doav = _heads_of(doa_ref, 0, heads)
        don = doav * (zv * sgz)
        dz = doav * (xo * og) * _dsilu(zv, sgz)
        d_og = jnp.sum(jnp.sum(don * xo, axis=1, keepdims=True), axis=0)
        do = _rms_bwd(don * og, xo, ro)
        tm, dmat, eg, edl, egl = cm["tm"], cm["dmat"], cm["eg"], cm["edl"], cm["egl"]
        dvn = _bmm_tn(cm["attn"], do) + _bmm(cm["kd"], dsp)
        dqe = _bmm_nt(do, state)
        ds_ref[...] = _bmm_tn(cm["qe"], do) + dsp * egl[:, :, :1] - _bmm_tn(cm["w"], dvn)
        dattn = _bmm_nt(do, vn)
        dkd = _bmm_nt(vn, dsp)
        dkd_kd = jnp.sum(dkd * cm["kd"], axis=-1, keepdims=True)
        dgl = (jnp.sum(jnp.sum(dsp * state, axis=-1, keepdims=True), axis=1, keepdims=True) * egl[:, :, :1]
               + jnp.sum(dkd_kd, axis=1, keepdims=True))
        dgc = jnp.sum(dqe * cm["qe"], axis=-1, keepdims=True) - dkd_kd
        dk = dkd * edl
        dq = dqe * eg
        dw = -_bmm_nt(dvn, state)
        dp0 = dattn * dmat
        dd = jnp.where(cm["incl"], dattn * cm["p0"], 0.0)
        dq = dq + _bmm(dp0, k)
        dk = dk + _bmm_tn(dp0, q)
        dtm = _bmm_nt(dvn, cm["vb"]) + _bmm_nt(dw, cm["kbe"])
        dvb = _bmm_tn(tm, dvn)
        dkbe = _bmm_tn(tm, dw)
        dkb = dkbe * eg
        dgc = dgc + jnp.sum(dkbe * cm["kbe"], axis=-1, keepdims=True)
        dlow = jnp.where(cm["strict"], -_bmm_tn(tm, _bmm_nt(dtm, tm)), 0.0)
        dd = dd + dlow * cm["a0"]
        da0 = dlow * dmat
        dkb = dkb + _bmm(da0, k)
        dk = dk + _bmm_tn(da0, cm["kb"])
        ddd = dd * dmat
        ones = jnp.ones((heads, c, HEAD_DIM), F32)
        dgc = dgc + jnp.sum(ddd, axis=-1, keepdims=True) - _bmm_tn(ddd, ones)[:, :, :1]
        dgc = dgc + jnp.where(_rows((c, 1)) == c - 1, dgl, 0.0)
        dg = _bmm_tn(cm["inclf"], jnp.broadcast_to(dgc, (heads, c, HEAD_DIM)))[:, :, :1]
        beta = cm["beta"]
        dk = dk + dkb * beta
        dbeta = jnp.sum(dkb * k, axis=-1, keepdims=True) + jnp.sum(dvb * v, axis=-1, keepdims=True)
        dv = dvb * beta
        db_col = dbeta * beta * (1.0 - beta)
        da_col = dg * cm["neg_ea"] * _sigmoid(cm["xg"])
        d_alog = jnp.sum(dg * cm["g"], axis=1, keepdims=True)
        d_dtb = jnp.sum(da_col, axis=1, keepdims=True)
        lane = lax.broadcasted_iota(jnp.int32, (c, LANES), 1)
        lane8 = lax.broadcasted_iota(jnp.int32, (8, LANES), 1)
        row8 = _rows((8, LANES))
        dab = jnp.zeros((c, LANES), F32)
        dpar = jnp.where(row8 == 2, d_og, 0.0)
        for h in range(heads):
            lo = h * HEAD_DIM
            dqkv_ref[:, lo:lo + HEAD_DIM] = dq[h]
            dqkv_ref[:, da + lo:da + lo + HEAD_DIM] = dk[h]
            dqkv_ref[:, 2 * da + lo:2 * da + lo + HEAD_DIM] = dv[h]
            dz_ref[:, lo:lo + HEAD_DIM] = dz[h]
            dab = dab + jnp.where(lane == h, da_col[h], 0.0) + jnp.where(lane == heads + h, db_col[h], 0.0)
            dpar = (dpar + jnp.where((row8 == 0) & (lane8 == h), d_alog[h], 0.0)
                    + jnp.where((row8 == 1) & (lane8 == h), d_dtb[h], 0.0))
        dab_ref[...] = dab
        dpar_ref[...] += dpar

    rev = lambda i: (n - 1 - i, 0)
    return pl.pallas_call(
        body, name="delta_bwd", grid=(n,),
        in_specs=[pl.BlockSpec((c, 3 * da), rev), pl.BlockSpec((c, da), rev), pl.BlockSpec((c, LANES), rev),
                  pl.BlockSpec((8, LANES), lambda i: (0, 0)),
                  pl.BlockSpec((1, heads, HEAD_DIM, HEAD_DIM), lambda i: (n - 1 - i, 0, 0, 0)),
                  pl.BlockSpec((1, 3, heads, c, c), lambda i: (n - 1 - i, 0, 0, 0, 0)),
                  pl.BlockSpec((1, 3, heads, c, HEAD_DIM), lambda i: (n - 1 - i, 0, 0, 0, 0)),
                  pl.BlockSpec((c, da), rev)],
        out_specs=[pl.BlockSpec((c, 3 * da), rev), pl.BlockSpec((c, da), rev), pl.BlockSpec((c, LANES), rev),
                   pl.BlockSpec((8, LANES), lambda i: (0, 0))],
        out_shape=[jax.ShapeDtypeStruct((t, 3 * da), F32), jax.ShapeDtypeStruct((t, da), F32),
                   jax.ShapeDtypeStruct((t, LANES), F32), jax.ShapeDtypeStruct((8, LANES), F32)],
        scratch_shapes=[pltpu.VMEM((heads, HEAD_DIM, HEAD_DIM), F32)],
        compiler_params=_params("arbitrary"))(qkv, z, ab, gpar, states, kept_c, kept_w, doa)


def _w_in_pieces(shard_cols, da, heads):
    a0, nab = 4 * da, 2 * heads
    d_in = 4 * shard_cols
    runs = [(0, a0, 0), (a0, a0 + nab, d_in - nab), (a0 + nab, d_in, a0)]
    pieces = []
    for j in range(4):
        lo, hi = j * shard_cols, (j + 1) * shard_cols
        for rlo, rhi, plo in runs:
            s, e = max(lo, rlo), min(hi, rhi)
            if s < e:
                pieces.append((j, s - lo, e - s, plo + (s - rlo)))
    return pieces, d_in - nab + LANES


def _w_in_pack(w4, da, heads):
    _, d, sc = w4.shape
    pieces, npk = _w_in_pieces(sc, da, heads)
    tr = _tile_rows(d, 256, SUBLANES_WIRE)

    def body(w_ref, o_ref):
        o_ref[:, npk - LANES:] = jnp.zeros((tr, LANES), o_ref.dtype)
        for j, lo, ln, dst in pieces:
            o_ref[:, dst:dst + ln] = w_ref[j, :, lo:lo + ln]

    return pl.pallas_call(
        body, name="w_in_pack", grid=(d // tr,),
        in_specs=[pl.BlockSpec((4, tr, sc), lambda i: (0, i, 0))],
        out_specs=pl.BlockSpec((tr, npk), lambda i: (i, 0)),
        out_shape=jax.ShapeDtypeStruct((d, npk), w4.dtype),
        compiler_params=_params("arbitrary"))(w4)


def _w_in_unpack(dwp, sc, da, heads):
    d, npk = dwp.shape
    pieces, _ = _w_in_pieces(sc, da, heads)
    tr = _tile_rows(d, 256)

    def body(g_ref, o_ref):
        for j, lo, ln, dst in pieces:
            o_ref[j, :, lo:lo + ln] = g_ref[:, dst:dst + ln]

    return pl.pallas_call(
        body, name="w_in_unpack", grid=(d // tr,),
        in_specs=[pl.BlockSpec((tr, npk), lambda i: (i, 0))],
        out_specs=pl.BlockSpec((4, tr, sc), lambda i: (0, i, 0)),
        out_shape=jax.ShapeDtypeStruct((4, d, sc), F32),
        compiler_params=_params("arbitrary"))(dwp)


def _block_diag(pool_w):
    g, gd, _ = pool_w.shape
    out = jnp.zeros((g * gd, g * gd), pool_w.dtype)
    for gi in range(g):
        out = lax.dynamic_update_slice(out, pool_w[gi], (gi * gd, gi * gd))
    return out


def _layer_dims(d):
    heads = (d // 2) // HEAD_DIM
    return heads, heads * HEAD_DIM, d // 4, d // 4


BIG = ("w_in", "w_gate", "w_up", "ple_proj", "w_out", "w_down", "ple_gate")


def _prepare_layer(small, li):
    d = small["norm1_g"].shape[1]
    heads, _, _, _ = _layer_dims(d)
    gpar = jnp.zeros((8, LANES), F32)
    gpar = gpar.at[0, :heads].set(small["a_log"][li]).at[1, :heads].set(small["dt_bias"][li]).at[2, :].set(small["onorm_g"][li])
    return dict(norm1_g=small["norm1_g"][li][None], conv_qkv=small["conv_qkv"][li], gpar=gpar, pool_bd=_block_diag(small["pool_w"][li]).astype(MM_DTYPE),
                pool_scale=small["pool_scale"][li][None], sconv_w=small["sconv_w"][li], norm2_g=small["norm2_g"][li][None])


def _layer_fwd(x0, p, gw, lw, tm, arrive):
    d = x0.shape[1]
    heads, da, dp, dc = _layer_dims(d)
    segs = (3 * da, da, dp, 3 * dc, LANES)
    lw["w_in_p"] = _w_in_pack(gw["w_in"], da, heads).astype(MM_DTYPE)
    qkv_pre, z, hp, cbcch, ab = _in_proj_fwd(x0, lw["norm1_g"], lw["w_in_p"], segs, tm)
    qkv = _qkv_conv_fwd(qkv_pre, lw["conv_qkv"], heads)
    oa, states, kept_c, kept_w = _delta_fwd(qkv, z, ab, lw["gpar"], heads)
    ob = _pool_fwd(hp, lw["pool_bd"], lw["pool_scale"], dp // POOL_GROUPS)
    oc = _sconv_fwd(cbcch, lw["sconv_w"])
    arrive("mixed", oa)
    x1, h2 = _out_proj_fwd(x0, (oa, ob, oc), gw["w_out"], lw["norm2_g"], tm)
    x2, gp, up = _ffn_fwd(x1, h2, gw["w_gate"], gw["w_up"], gw["w_down"], tm)
    arrive("ffn", x2)
    x3 = _ple_fwd(x2, p, gw["ple_gate"], gw["ple_proj"], tm)
    arrive("end", x3)
    saved = dict(x0=x0, qkv_pre=qkv_pre, z=z, hp=hp, cbcch=cbcch, ab=ab, qkv=qkv, states=states, kept_c=kept_c, kept_w=kept_w, oa=oa, ob=ob, oc=oc,
                 x1=x1, h2=h2, gp=gp, up=up, x2=x2)
    return x3, saved


def _layer_bwd(dx3, p, gw, lw, sv, tm, produced):
    def after_token(tok, arr):
        return arr if tok is None else arr + tok[0, 0]

    d = dx3.shape[1]
    heads, da, dp, dc = _layer_dims(d)
    segs = (3 * da, da, dp, dc, dc, dc, LANES)
    gd = dp // POOL_GROUPS
    dx2, d_ple_gate, d_ple_proj = _ple_bwd(dx3, sv["x2"], p, gw["ple_gate"], gw["ple_proj"], tm)
    dh2, d_w_gate, d_w_up, d_w_down = _ffn_bwd(dx2, sv["h2"], sv["gp"], sv["up"], gw["w_gate"], gw["w_up"], gw["w_down"],
                                               min(tm, 256))
    tok = produced("ffn", dict(w_gate=d_w_gate, w_up=d_w_up, ple_proj=d_ple_proj, w_down=d_w_down, ple_gate=d_ple_gate), dh2)
    dx1, doa, dob, doc, d_w_out, d_norm2 = _out_proj_bwd(dx2, dh2, sv["x1"], after_token(tok, lw["norm2_g"]),
                                                         (sv["oa"], sv["ob"], sv["oc"]), gw["w_out"], tm)
    dcb, dcc, dch, d_sconv = _sconv_bwd(sv["cbcch"], lw["sconv_w"], doc)
    dhp, d_pool_bd, d_pool_scale = _pool_bwd(sv["hp"], lw["pool_bd"], lw["pool_scale"], dob, gd)
    dqkv, dz, dab, dpar = _delta_bwd(sv["qkv"], sv["z"], sv["ab"], lw["gpar"], sv["states"], sv["kept_c"], sv["kept_w"], doa,
                                      heads)
    tok = produced("mixers", {}, dqkv)
    dqkv_pre, d_conv_qkv = _qkv_conv_bwd(sv["qkv_pre"], lw["conv_qkv"], dqkv, heads)
    dsegs = (dqkv_pre, dz, dhp, dcb, dcc, dch, dab)
    dx0, d_w_in_p, d_norm1 = _in_proj_bwd(sv["x0"], after_token(tok, lw["norm1_g"]), lw["w_in_p"], dsegs, dx1, segs, tm)
    per = LANES // gd
    bd = d_pool_bd.reshape(dp // LANES, per, gd, per, gd)
    d_pool_w = jnp.stack([bd[gi // per, gi % per, :, gi % per, :] for gi in range(POOL_GROUPS)])
    big = dict(w_in=_w_in_unpack(d_w_in_p, gw["w_in"].shape[2], da, heads), w_gate=d_w_gate, w_up=d_w_up,
               ple_proj=d_ple_proj, w_out=d_w_out, w_down=d_w_down, ple_gate=d_ple_gate)
    small = dict(norm1_g=d_norm1[0], conv_qkv=d_conv_qkv, a_log=dpar[0, :heads], dt_bias=dpar[1, :heads], onorm_g=dpar[2],
                 pool_w=d_pool_w, pool_scale=d_pool_scale[0], sconv_w=d_sconv, norm2_g=d_norm2[0])
    tok = produced("end", dict(w_in=big["w_in"], w_out=d_w_out), big["w_in"])
    return dx0, big, small, tok


def _local_step(x, p, target, gw, small, produced=None, arrive=None):
    t, d = x.shape
    depth = p.shape[0]
    tm = 512 if t % 512 == 0 else 128
    layers = [_prepare_layer(small, li) for li in range(depth)]
    saved = []
    h = x
    for li in range(depth):
        h, sv = _layer_fwd(h, p[li], gw[li], layers[li], tm,
                           (lambda stage, after, li=li: arrive(li, stage, after)) if arrive else (lambda stage, after: None))
        saved.append(sv)
    dx, loss, d_final = _loss_head(h, target, small["final_g"][None], tm)
    big, sm = [None] * depth, [None] * depth
    token = None
    for li in reversed(range(depth)):
        p_li = p[li] if token is None else p[li] + token[0, 0]
        dx, big[li], sm[li], token = _layer_bwd(
            dx, p_li, gw[li], layers[li], saved[li], tm,
            (lambda stage, grads, after, li=li: produced(li, stage, grads, after)) if produced else (lambda *a: None))
    small_grads = {n: jnp.stack([g[n] for g in sm]) for n in sm[0]}
    small_grads["final_g"] = d_final[0]
    return loss[0, 0], dx, big, small_grads


def _coords():
    return lax.axis_index("x"), lax.axis_index("y"), lax.axis_index("c")


def _other_chips(x, y):
    return [(1 - x, y), (x, 1 - y), (1 - x, 1 - y)]


def _place_shards(ws, me_idx):
    nt = len(ws)
    depth = ws[0].shape[0]

    def body(me_ref, *refs):
        for t, w_ref in enumerate(refs[:nt]):
            for li in range(depth):
                refs[nt + li * nt + t][...] = w_ref[li].astype(WIRE_DTYPE)

    outs = pl.pallas_call(
        body, name="place_shards",
        grid_spec=pltpu.PrefetchScalarGridSpec(
            num_scalar_prefetch=1, grid=(4,),
            in_specs=[pl.BlockSpec((depth, w.shape[1] // 4, w.shape[2]), lambda i, me_ref: (0, i, 0)) for w in ws],
            out_specs=[pl.BlockSpec((None, w.shape[1] // 4, w.shape[2]), lambda i, me_ref: (me_ref[0], i, 0))
                       for _ in range(depth) for w in ws]),
        out_shape=[jax.ShapeDtypeStruct((4,) + w.shape[1:], WIRE_DTYPE) for _ in range(depth) for w in ws],
        compiler_params=_params("arbitrary"))(me_idx, *ws)
    return [list(outs[li * nt:(li + 1) * nt]) for li in range(depth)]


def _half_block(ref, chip, pc):
    rh = ref.shape[1] // 2
    return ref.at[chip, pl.ds(pc * rh, rh)]


def _gather_copies(out_refs, send_sems, recv_sems, stage):
    nt = len(out_refs)
    x, y, c = _coords()
    pairs = []
    for j, (cx, cy) in enumerate(_other_chips(x, y)):
        for t in range(nt):
            sems = dict(send_sem=send_sems[j * nt + t], recv_sem=recv_sems[j * nt + t], device_id_type=MESH)
            if stage == 0:
                mine, theirs, to = _half_block(out_refs[t], 2 * x + y, c), _half_block(out_refs[t], 2 * cx + cy, c), (cx, cy, c)
            else:
                mine, theirs, to = (_half_block(out_refs[t], 2 * cx + cy, c), _half_block(out_refs[t], 2 * cx + cy, 1 - c),
                                    (x, y, 1 - c))
            pairs.append((pltpu.make_async_remote_copy(src_ref=mine, dst_ref=mine, device_id=to, **sems),
                          pltpu.make_async_remote_copy(src_ref=theirs, dst_ref=theirs, device_id=to, **sems)))
    return pairs


def _all_gather_chips(placed):
    nt = len(placed)

    def body(*refs):
        out_refs = refs[nt:2 * nt]
        send_sems, recv_sems = refs[2 * nt:]
        nc = 3 * nt
        first = _gather_copies(out_refs, [send_sems.at[k] for k in range(nc)], [recv_sems.at[k] for k in range(nc)], 0)
        passed = _gather_copies(out_refs, [send_sems.at[nc + k] for k in range(nc)], [recv_sems.at[nc + k] for k in range(nc)], 1)
        for start, _ in first:
            start.start()
        for (_, arrival), (forward, _) in zip(first, passed):
            arrival.wait_recv()
            forward.start()
        for _, arrival in passed:
            arrival.wait_recv()
        for start, _ in first + passed:
            start.wait_send()

    return pl.pallas_call(
        body, name="all_gather_chips", out_shape=[jax.ShapeDtypeStruct(a.shape, a.dtype) for a in placed],
        in_specs=[ANY] * nt, out_specs=[ANY] * nt, input_output_aliases={t: t for t in range(nt)},
        scratch_shapes=[pltpu.SemaphoreType.DMA((6 * nt,)), pltpu.SemaphoreType.DMA((6 * nt,))],
    )(*placed)


def _gather_call(name, arrs, wait_sems, after, stage):
    nt = len(arrs)
    nc = 3 * nt
    n_wait = len(wait_sems)
    n_new = 2 * nc if stage < 2 else 0
    arrs = [pltpu.with_memory_space_constraint(a, pltpu.HBM) for a in arrs]

    def body(*refs):
        a_refs = refs[:nt]
        waits = refs[nt:nt + n_wait]
        news = refs[nt + n_wait + 1:nt + n_wait + 1 + n_new]
        token = refs[-1]
        if stage > 0:
            for start, arrival in _gather_copies(a_refs, waits[:nc], waits[nc:], stage - 1):
                start.wait_send()
                arrival.wait_recv()
        if stage < 2:
            for start, _ in _gather_copies(a_refs, news[:nc], news[nc:], stage):
                start.start()
        token[...] = jnp.zeros_like(token)

    outs = pl.pallas_call(
        body, name=name,
        out_shape=(*[pltpu.SemaphoreType.DMA(())] * n_new, *[pltpu.HBM(a.shape, a.dtype) for a in arrs],
                   jax.ShapeDtypeStruct((8, LANES), F32)),
        in_specs=[HBM] * nt + [SEM] * n_wait + [ANY],
        out_specs=(*[SEM] * n_new, *[HBM] * nt, pl.BlockSpec(memory_space=pltpu.VMEM)),
        input_output_aliases={t: n_new + t for t in range(nt)},
        compiler_params=pltpu.CompilerParams(has_side_effects=pltpu.SideEffectType.DATAFLOW_SIDE_EFFECTING),
    )(*arrs, *wait_sems, after)
    return list(outs[:n_new]), list(outs[n_new:n_new + nt]), outs[-1]


def _sibling_swap_half(gs):
    nt = len(gs)

    def body(*refs):
        g_refs, out_refs = refs[:nt], refs[nt:2 * nt]
        send_sems, recv_sems = refs[2 * nt:]
        x, y, c = _coords()
        cps = []
        for t in range(nt):
            rh = g_refs[t].shape[1] // 2
            cps.append(pltpu.make_async_remote_copy(src_ref=g_refs[t].at[:, pl.ds((1 - c) * rh, rh)], dst_ref=out_refs[t],
                                                    send_sem=send_sems.at[t], recv_sem=recv_sems.at[t], device_id=(x, y, 1 - c),
                                                    device_id_type=MESH))
        for cp in cps:
            cp.start()
        for cp in cps:
            cp.wait()

    return pl.pallas_call(
        body, name="sibling_swap_half",
        out_shape=[jax.ShapeDtypeStruct((g.shape[0], g.shape[1] // 2, g.shape[2]), g.dtype) for g in gs],
        in_specs=[ANY] * nt, out_specs=[ANY] * nt,
        scratch_shapes=[pltpu.SemaphoreType.DMA((nt,)), pltpu.SemaphoreType.DMA((nt,))])(*gs)


def _add_my_halves(gs, others, c_idx):
    nt = len(gs)

    def body(c_ref, *refs):
        for g_ref, o_ref, out_ref in zip(refs[:nt], refs[nt:2 * nt], refs[2 * nt:]):
            out_ref[...] = (g_ref[...].astype(F32) + o_ref[...].astype(F32)).astype(out_ref.dtype)

    def quarter(g):
        return pl.BlockSpec((None, g.shape[1] // 4, g.shape[2]), lambda j, i, c_ref: (j, i, 0))

    return pl.pallas_call(
        body, name="add_my_halves",
        grid_spec=pltpu.PrefetchScalarGridSpec(
            num_scalar_prefetch=1, grid=(4, 2),
            in_specs=[pl.BlockSpec((None, g.shape[1] // 4, g.shape[2]), lambda j, i, c_ref: (j, 2 * c_ref[0] + i, 0)) for g in gs]
                     + [quarter(g) for g in gs],
            out_specs=[quarter(g) for g in gs]),
        out_shape=[jax.ShapeDtypeStruct((4, g.shape[1] // 2, g.shape[2]), WIRE_DTYPE) for g in gs],
        compiler_params=_params("arbitrary", "arbitrary"))(c_idx, *gs, *others)


def _exchange_chips(parts):
    nt = len(parts)

    def body(*refs):
        p_refs, out_refs = refs[:nt], refs[nt:2 * nt]
        send_sems, recv_sems = refs[2 * nt:]
        x, y, c = _coords()
        chips = _other_chips(x, y)

        def copy(j, t):
            cx, cy = chips[j]
            return pltpu.make_async_remote_copy(src_ref=p_refs[t].at[2 * cx + cy], dst_ref=out_refs[t].at[j],
                                                send_sem=send_sems.at[j, t], recv_sem=recv_sems.at[j, t], device_id=(cx, cy, c),
                                                device_id_type=MESH)

        sends = [copy(j, t) for j in range(3) for t in range(nt)]
        for cp in sends:
            cp.start()
        for cp in sends:
            cp.wait_recv()
        for cp in sends:
            cp.wait_send()

    return pl.pallas_call(
        body, name="exchange_chips", out_shape=[jax.ShapeDtypeStruct((3,) + p.shape[1:], p.dtype) for p in parts],
        in_specs=[ANY] * nt, out_specs=[ANY] * nt,
        scratch_shapes=[pltpu.SemaphoreType.DMA((3, nt)), pltpu.SemaphoreType.DMA((3, nt))])(*parts)


def _split_plan(kind, s_refs, l_refs):
    x, y, c = _coords()
    if kind == "swap":
        return [(s.at[:, pl.ds((1 - c) * (s.shape[1] // 2), s.shape[1] // 2)], l, (x, y, 1 - c)) for s, l in zip(s_refs, l_refs)]
    return [(s.at[2 * cx + cy], l.at[j], (cx, cy, c)) for j, (cx, cy) in enumerate(_other_chips(x, y))
            for s, l in zip(s_refs, l_refs)]


def _split_landing(kind, a):
    return (a.shape[0], a.shape[1] // 2, a.shape[2]) if kind == "swap" else (3,) + a.shape[1:]


def _copies_start(name, kind, srcs, after=None):
    ns = len(srcs)
    n = ns if kind == "swap" else 3 * ns
    srcs = [pltpu.with_memory_space_constraint(a, pltpu.HBM) for a in srcs]
    lands = [pltpu.with_memory_space_constraint(lax.empty(_split_landing(kind, a), a.dtype), pltpu.HBM) for a in srcs]
    extra = [] if after is None else [after]

    def body(*refs):
        first_sem = 2 * ns + len(extra)
        sems, token = refs[first_sem:first_sem + 2 * n], refs[-1]
        for k, (src, dst, dev) in enumerate(_split_plan(kind, refs[:ns], refs[ns:2 * ns])):
            pltpu.make_async_remote_copy(src_ref=src, dst_ref=dst, send_sem=sems[k], recv_sem=sems[n + k], device_id=dev,
                                         device_id_type=MESH).start()
        token[...] = jnp.zeros_like(token)

    outs = pl.pallas_call(
        body, name=name,
        out_shape=(*[pltpu.SemaphoreType.DMA(())] * (2 * n), *[pltpu.HBM(a.shape, a.dtype) for a in srcs + lands],
                   jax.ShapeDtypeStruct((8, LANES), F32)),
        in_specs=[HBM] * (2 * ns) + [ANY] * len(extra),
        out_specs=(*[SEM] * (2 * n), *[HBM] * (2 * ns), pl.BlockSpec(memory_space=pltpu.VMEM)),
        input_output_aliases={t: 2 * n + t for t in range(2 * ns)},
        compiler_params=pltpu.CompilerParams(has_side_effects=pltpu.SideEffectType.DATAFLOW_SIDE_EFFECTING),
    )(*srcs, *lands, *extra)
    return list(outs[:2 * n]), list(outs[2 * n:2 * n + ns]), list(outs[2 * n + ns:2 * n + 2 * ns]), outs[-1]


def _copies_wait(name, kind, sems, srcs, lands, after):
    ns = len(srcs)
    n = len(sems) // 2

    def body(*refs):
        sem_refs = refs[2 * ns:2 * ns + 2 * n]
        for k, (src, dst, dev) in enumerate(_split_plan(kind, refs[:ns], refs[ns:2 * ns])):
            cp = pltpu.make_async_remote_copy(src_ref=src, dst_ref=dst, send_sem=sem_refs[k], recv_sem=sem_refs[n + k],
                                              device_id=dev, device_id_type=MESH)
            cp.wait_send()
            cp.wait_recv()

    outs = pl.pallas_call(
        body, name=name, out_shape=tuple(pltpu.HBM(a.shape, a.dtype) for a in srcs + lands),
        in_specs=[HBM] * (2 * ns) + [SEM] * (2 * n) + [ANY], out_specs=tuple([HBM] * (2 * ns)),
        input_output_aliases={t: t for t in range(2 * ns)},
        compiler_params=pltpu.CompilerParams(has_side_effects=pltpu.SideEffectType.DATAFLOW_SIDE_EFFECTING),
    )(*srcs, *lands, *sems, after)
    return list(outs[:ns]), list(outs[ns:])


def _sum_into(pairs, recvs, idx, li, depth, accs):
    nt = len(pairs)

    def body(idx_ref, *refs):
        for p_ref, r_ref, out_ref in zip(refs[:nt], refs[nt:2 * nt], refs[-nt:]):
            out_ref[...] = p_ref[...].astype(F32) + r_ref[0].astype(F32) + r_ref[1].astype(F32) + r_ref[2].astype(F32)

    in_specs = ([pl.BlockSpec((None, p.shape[1] // 2, p.shape[2]), lambda i, idx_ref: (idx_ref[0], i, 0)) for p in pairs]
                + [pl.BlockSpec((3, p.shape[1] // 2, p.shape[2]), lambda i, idx_ref: (0, i, 0)) for p in pairs])
    args = [idx, *pairs, *recvs]
    aliases = {}
    if accs[0] is not None:
        in_specs += [ANY] * nt
        args += list(accs)
        aliases = {1 + 2 * nt + t: t for t in range(nt)}
    return pl.pallas_call(
        body, name="sum_into",
        grid_spec=pltpu.PrefetchScalarGridSpec(
            num_scalar_prefetch=1, grid=(2,), in_specs=in_specs,
            out_specs=[pl.BlockSpec((None, p.shape[1] // 2, p.shape[2]), lambda i, idx_ref: (li, 2 * idx_ref[1] + i, 0))
                       for p in pairs]),
        out_shape=[jax.ShapeDtypeStruct((depth, 2 * p.shape[1], p.shape[2]), F32) for p in pairs],
        input_output_aliases=aliases, compiler_params=_params("arbitrary"))(*args)


def _sum_slots(parts):
    n, rows, cols = parts.shape
    tr = _tile_rows(rows, 512, SUBLANES_WIRE)

    def body(p_ref, out_ref):
        acc = p_ref[0].astype(F32)
        for s in range(1, n):
            acc = acc + p_ref[s].astype(F32)
        out_ref[...] = acc

    return pl.pallas_call(
        body, name="sum_slots", grid=(rows // tr,),
        in_specs=[pl.BlockSpec((n, tr, cols), lambda i: (0, i, 0))],
        out_specs=pl.BlockSpec((tr, cols), lambda i: (i, 0)),
        out_shape=jax.ShapeDtypeStruct((rows, cols), F32),
        compiler_params=_params("arbitrary"))(parts)


def _sibling_share(gs):
    nt = len(gs)
    depth = gs[0].shape[0]

    def body(*refs):
        out_refs = refs[nt:2 * nt]
        send_sems, recv_sems = refs[2 * nt:]
        x, y, c = _coords()
        sends, recvs = [], []
        for t in range(nt):
            rh = out_refs[t].shape[1] // 2
            for li in range(depth):
                mine = out_refs[t].at[li, pl.ds(c * rh, rh)]
                theirs = out_refs[t].at[li, pl.ds((1 - c) * rh, rh)]
                sems = dict(send_sem=send_sems.at[t, li], recv_sem=recv_sems.at[t, li], device_id=(x, y, 1 - c), device_id_type=MESH)
                sends.append(pltpu.make_async_remote_copy(src_ref=mine, dst_ref=mine, **sems))
                recvs.append(pltpu.make_async_remote_copy(src_ref=theirs, dst_ref=theirs, **sems))
        for cp in sends:
            cp.start()
        for cp in recvs:
            cp.wait_recv()
        for cp in sends:
            cp.wait_send()

    return pl.pallas_call(
        body, name="sibling_share", out_shape=[jax.ShapeDtypeStruct(g.shape, g.dtype) for g in gs],
        in_specs=[ANY] * nt, out_specs=[ANY] * nt, input_output_aliases={t: t for t in range(nt)},
        scratch_shapes=[pltpu.SemaphoreType.DMA((nt, depth)), pltpu.SemaphoreType.DMA((nt, depth))])(*gs)


def _all_gather_devices(buf):
    def body(b_ref, out_ref, send_sems, recv_sems, local_sem):
        x, y, c = _coords()
        me = 4 * x + 2 * y + c
        mine = pltpu.make_async_copy(b_ref, out_ref.at[me], local_sem)
        mine.start()
        peers = []
        for k in range(1, 8):
            fx, fy, fc = (k >> 2) & 1, (k >> 1) & 1, k & 1
            peers.append((x ^ fx, y ^ fy, c ^ fc))
        sends = [pltpu.make_async_remote_copy(src_ref=b_ref, dst_ref=out_ref.at[me], send_sem=send_sems.at[k],
                                              recv_sem=recv_sems.at[k], device_id=peer, device_id_type=MESH)
                 for k, peer in enumerate(peers)]
        for cp in sends:
            cp.start()
        for k, (px, py, pc) in enumerate(peers):
            pltpu.make_async_remote_copy(src_ref=b_ref, dst_ref=out_ref.at[4 * px + 2 * py + pc], send_sem=send_sems.at[k],
                                         recv_sem=recv_sems.at[k], device_id=(px, py, pc), device_id_type=MESH).wait_recv()
        for cp in sends:
            cp.wait_send()
        mine.wait()

    return pl.pallas_call(
        body, name="all_gather_devices", out_shape=jax.ShapeDtypeStruct((8,) + buf.shape, buf.dtype),
        in_specs=[ANY], out_specs=ANY,
        scratch_shapes=[pltpu.SemaphoreType.DMA((7,)), pltpu.SemaphoreType.DMA((7,)), pltpu.SemaphoreType.DMA(())])(buf)


def _pair_sums(big_grads, c_idx):
    gs = [big_grads[n] for n in BIG]
    return _add_my_halves(gs, _sibling_swap_half(gs), c_idx)


SMALL_SHARDED = ("conv_qkv", "sconv_w")
REPLICATED = ("norm1_g", "a_log", "dt_bias", "onorm_g", "pool_w", "pool_scale", "norm2_g", "final_g")
ALL_WEIGHTS = ("norm1_g", "w_in", "conv_qkv", "a_log", "dt_bias", "onorm_g", "pool_w", "pool_scale", "sconv_w", "w_out",
               "norm2_g", "w_gate", "w_up", "w_down", "ple_proj", "ple_gate", "final_g")


def _pad_rows(flat, row_multiple):
    m = flat.shape[0]
    r = -(-m // (LANES * row_multiple)) * row_multiple
    return jnp.pad(flat, (0, r * LANES - m)).reshape(r, LANES)


def _adamw(w, g, m, v):
    shape = w.shape
    cols = shape[-1]
    rows = w.size // cols
    tr = _tile_rows(rows, 512)
    c1 = 1.0 / (1.0 - ADAM_B1 ** ADAM_STEP)
    c2 = 1.0 / (1.0 - ADAM_B2 ** ADAM_STEP)

    def body(w_ref, g_ref, m_ref, v_ref, d_ref, nm_ref, nv_ref, go_ref):
        gv = g_ref[...]
        nm = ADAM_B1 * m_ref[...] + (1.0 - ADAM_B1) * gv
        nv = ADAM_B2 * v_ref[...] + (1.0 - ADAM_B2) * (gv * gv)
        nm_ref[...] = nm
        nv_ref[...] = nv
        go_ref[...] = gv
        d_ref[...] = -ADAM_LR * ((nm * c1) / (jnp.sqrt(nv * c2) + ADAM_EPS) + ADAM_WD * w_ref[...])

    spec = pl.BlockSpec((tr, cols), lambda i: (i, 0))
    outs = pl.pallas_call(
        body, name="adamw", grid=(rows // tr,), in_specs=[spec] * 4, out_specs=[spec] * 4,
        out_shape=[jax.ShapeDtypeStruct((rows, cols), F32)] * 4,
        compiler_params=_params("arbitrary"))(*[a.reshape(rows, cols) for a in (w, g, m, v)])
    return tuple(o.reshape(shape) for o in outs)


def kernel(x, p, norm1_g, w_in, conv_qkv, a_log, dt_bias, onorm_g, pool_w, pool_scale, sconv_w, w_out, norm2_g, w_gate, w_up, w_down, ple_proj, ple_gate, final_g, loss_target, m_norm1_g, m_w_in, m_conv_qkv, m_a_log, m_dt_bias, m_onorm_g, m_pool_w, m_pool_scale, m_sconv_w, m_w_out, m_norm2_g, m_w_gate, m_w_up, m_w_down, m_ple_proj, m_ple_gate, m_final_g, v_norm1_g, v_w_in, v_conv_qkv, v_a_log, v_dt_bias, v_onorm_g, v_pool_w, v_pool_scale, v_sconv_w, v_w_out, v_norm2_g, v_w_gate, v_w_up, v_w_down, v_ple_proj, v_ple_gate, v_final_g):
    weights = dict(zip(ALL_WEIGHTS, (norm1_g, w_in, conv_qkv, a_log, dt_bias, onorm_g, pool_w, pool_scale, sconv_w, w_out,
                                     norm2_g, w_gate, w_up, w_down, ple_proj, ple_gate, final_g)))
    mom_m = dict(zip(ALL_WEIGHTS, (m_norm1_g, m_w_in, m_conv_qkv, m_a_log, m_dt_bias, m_onorm_g, m_pool_w, m_pool_scale,
                                   m_sconv_w, m_w_out, m_norm2_g, m_w_gate, m_w_up, m_w_down, m_ple_proj, m_ple_gate, m_final_g)))
    mom_v = dict(zip(ALL_WEIGHTS, (v_norm1_g, v_w_in, v_conv_qkv, v_a_log, v_dt_bias, v_onorm_g, v_pool_w, v_pool_scale,
                                   v_sconv_w, v_w_out, v_norm2_g, v_w_gate, v_w_up, v_w_down, v_ple_proj, v_ple_gate, v_final_g)))
    c_idx = lax.axis_index("c").astype(jnp.int32).reshape(1)
    chip = (2 * lax.axis_index("x") + lax.axis_index("y")).astype(jnp.int32)
    me_idx = chip.reshape(1)
    idx = jnp.stack([chip, lax.axis_index("c").astype(jnp.int32)])
    depth = p.shape[0]

    placed = _place_shards([weights[n] for n in BIG], me_idx)
    gw = [dict() for _ in range(depth)]
    gw[0]["w_in"] = _all_gather_chips(placed[0][:1])[0]
    rest = BIG[1:]
    sems_a, arrs_a, token = _gather_call("gather_rest_start", placed[0][1:], [], gw[0]["w_in"], 0)
    upper = []
    for li in range(1, depth):
        sems_u, arrs_u, token = _gather_call("gather_upper_start", placed[li], [], token, 0)
        upper.append([li, sems_u, arrs_u])

    def arrive(li, stage, after):
        nonlocal sems_a, arrs_a
        if li == 0 and stage == "mixed":
            sems_a, arrs_a, _ = _gather_call("gather_rest_forward", arrs_a, sems_a, after, 1)
            _, arrs_a, _ = _gather_call("gather_rest_finish", arrs_a, sems_a, after, 2)
            gw[0].update(zip(rest, arrs_a))
        if li + 1 < depth and stage == "ffn":
            u = upper[li]
            u[1], u[2], _ = _gather_call("gather_upper_forward", u[2], u[1], after, 1)
        if li + 1 < depth and stage == "end":
            u = upper[li]
            _, u[2], _ = _gather_call("gather_upper_finish", u[2], u[1], after, 2)
            gw[u[0]].update(zip(BIG, u[2]))

    small = {n: weights[n] for n in REPLICATED}
    small["norm1_g"] = small["norm1_g"] + token[0, 0]
    sflat = _pad_rows(jnp.concatenate([weights[n].reshape(-1) for n in SMALL_SHARDED]), 8)
    sgath = _all_gather_devices(sflat)[0::2].reshape(4, -1)
    off = 0
    for n in SMALL_SHARDED:
        shp = weights[n].shape
        part = sgath[:, off:off + weights[n].size].reshape((4,) + shp)
        small[n] = jnp.moveaxis(part, 0, -2).reshape(shp[:-1] + (4 * shp[-1],))
        off += weights[n].size

    pending = []

    def advance(g, after):
        if g["stage"] == 0:
            gs, others = _copies_wait("swap_wait_" + g["tag"], "swap", *g["handle"], after)
            g["handle"] = _copies_start("exchange_start_" + g["tag"], "exchange", _add_my_halves(gs, others, c_idx))
            g["stage"] = 1
            return g["handle"][3]
        return None

    def produced(li, stage, grads, after):
        token = None
        for g in pending:
            token = advance(g, after) if g["stage"] == 0 else token
        if grads:
            names = [n for n in BIG if n in grads]
            handle = _copies_start("swap_start_%d%s" % (li, stage), "swap", [grads[n] for n in names], token)
            pending.append(dict(li=li, names=names, tag="%d%s" % (li, stage), stage=0, handle=handle[:3]))
            token = handle[3]
        return token

    loss_local, dx, _, small_grads = _local_step(x[0], p[:, 0], loss_target[0], gw, small, produced, arrive)
    accs = {}
    for g in pending:
        advance(g, dx)
    for g in pending:
        pairs, recvs = _copies_wait("exchange_wait_" + g["tag"], "exchange", *g["handle"][:3], dx)
        have = [accs.get(n) for n in g["names"]]
        accs.update(zip(g["names"], _sum_into(pairs, recvs, idx, g["li"], depth, have)))
    gshard = dict(zip(BIG, _sibling_share([accs[n] for n in BIG])))


    rnames = REPLICATED + SMALL_SHARDED
    rflat = _pad_rows(jnp.concatenate([small_grads[n].reshape(-1) for n in rnames]), 8)
    rsum = _sum_slots(_all_gather_devices(rflat)).reshape(-1)
    off = 0
    for n in rnames:
        whole = rsum[off:off + small_grads[n].size].reshape(small_grads[n].shape)
        off += small_grads[n].size
        if n in SMALL_SHARDED:
            cols = weights[n].shape[-1]
            whole = lax.dynamic_slice_in_dim(whole, chip * cols, cols, axis=whole.ndim - 1)
        gshard[n] = whole

    loss = lax.psum(loss_local, ("x", "y", "c"))
    deltas, new_m, new_v, grad_out = {}, {}, {}, {}
    for n in ALL_WEIGHTS:
        deltas[n], new_m[n], new_v[n], grad_out[n] = _adamw(weights[n], gshard[n], mom_m[n], mom_v[n])
    return (loss, dx[None], *[grad_out[n] for n in ALL_WEIGHTS], *[deltas[n] for n in ALL_WEIGHTS],
            *[new_m[n] for n in ALL_WEIGHTS], *[new_v[n] for n in ALL_WEIGHTS])
```

```python
import jax
import jax.numpy as jnp
from jax import lax
from jax.experimental import pallas as pl
from jax.experimental.pallas import tpu as pltpu

F32 = jnp.float32
MM_DTYPE = jnp.bfloat16
WIRE_DTYPE = jnp.bfloat16
HI = lax.Precision.HIGHEST
EPS = 1e-6
HEAD_DIM = 128
CHUNK = 64
QKV_CONV_WIDTH = 4
SCONV_WIDTH = 3
POOL_GROUPS = 4
LANES = 128
SUBLANES_WIRE = 16
VMEM_LIMIT_BYTES = 56 * 1024 * 1024
ADAM_LR, ADAM_B1, ADAM_B2, ADAM_EPS, ADAM_WD, ADAM_STEP = 0.001, 0.9, 0.999, 1e-08, 0.01, 10
MESH = pl.DeviceIdType.MESH
ANY = pl.BlockSpec(memory_space=pl.ANY)
HBM = pl.BlockSpec(memory_space=pltpu.HBM)
SEM = pl.BlockSpec(memory_space=pltpu.SEMAPHORE)


def _params(*sem):
    return pltpu.CompilerParams(vmem_limit_bytes=VMEM_LIMIT_BYTES, dimension_semantics=sem if sem else None)


def _mm(a, b):
    return jnp.dot(a.astype(MM_DTYPE), b.astype(MM_DTYPE), preferred_element_type=F32)


def _mm_nt(a, b):
    return lax.dot_general(a.astype(MM_DTYPE), b.astype(MM_DTYPE), (((1,), (1,)), ((), ())), preferred_element_type=F32)


def _mm_tn(a, b):
    return lax.dot_general(a.astype(MM_DTYPE), b.astype(MM_DTYPE), (((0,), (0,)), ((), ())), preferred_element_type=F32)


def _hmm(a, b):
    return jnp.dot(a, b, preferred_element_type=F32, precision=HI)


def _hmm_nt(a, b):
    return lax.dot_general(a, b, (((1,), (1,)), ((), ())), preferred_element_type=F32, precision=HI)


def _hmm_tn(a, b):
    return lax.dot_general(a, b, (((0,), (0,)), ((), ())), preferred_element_type=F32, precision=HI)


def _sigmoid(x):
    return 1.0 / (1.0 + jnp.exp(-x))


def _dsilu(x, s):
    return s * (1.0 + x * (1.0 - s))


def _rows(shape):
    return lax.broadcasted_iota(jnp.int32, shape, 0)


def _shift_down(x, s):
    if s == 0:
        return x
    return jnp.where(_rows(x.shape) >= s, pltpu.roll(x, s, 0), 0.0)


def _shift_up(x, s):
    if s == 0:
        return x
    t = x.shape[0]
    return jnp.where(_rows(x.shape) < t - s, pltpu.roll(x, t - s, 0), 0.0)


def _rms_fwd(x):
    r = lax.rsqrt(jnp.mean(x * x, axis=-1, keepdims=True) + EPS)
    return x * r, r


def _rms_bwd(dxn, xn, r):
    return r * (dxn - xn * jnp.mean(dxn * xn, axis=-1, keepdims=True))


def _tile_rows(n, cap, mult=8):
    best = None
    for d in range(mult, min(n, cap) + 1, mult):
        if n % d == 0:
            best = d
    return best if best is not None else n


def _in_proj_fwd(x, g1, wp, segs, tm):
    t, d = x.shape
    npk = wp.shape[1]

    def body(x_ref, g_ref, w_ref, *o_refs):
        xn, _ = _rms_fwd(x_ref[...])
        h = (xn * g_ref[...]).astype(w_ref.dtype)
        off = 0
        for o_ref, wd in zip(o_refs, segs):
            o_ref[...] = jnp.dot(h, w_ref[:, off:off + wd], preferred_element_type=F32)
            off += wd

    return pl.pallas_call(
        body, name="in_proj_fwd", grid=(t // tm,),
        in_specs=[pl.BlockSpec((tm, d), lambda i: (i, 0)), pl.BlockSpec((1, d), lambda i: (0, 0)),
                  pl.BlockSpec((d, npk), lambda i: (0, 0))],
        out_specs=[pl.BlockSpec((tm, wd), lambda i: (i, 0)) for wd in segs],
        out_shape=[jax.ShapeDtypeStruct((t, wd), F32) for wd in segs],
        compiler_params=_params("arbitrary"))(x, g1, wp)


def _in_proj_bwd(x, g1, wp, dsegs, dx_res, segs, tm):
    t, d = x.shape
    npk = wp.shape[1]
    nseg = len(segs)

    def body(x_ref, g_ref, w_ref, *rest):
        ds_refs = rest[:nseg]
        dxr_ref, dx_ref, dw_ref, dg_ref = rest[nseg:]
        i = pl.program_id(0)

        @pl.when(i == 0)
        def _():
            dw_ref[...] = jnp.zeros_like(dw_ref)
            dg_ref[...] = jnp.zeros_like(dg_ref)

        xn, r = _rms_fwd(x_ref[...])
        g = g_ref[...]
        h = (xn * g).astype(w_ref.dtype)
        dh = jnp.zeros((tm, d), F32)
        off = 0
        for ds_ref, wd in zip(ds_refs, segs):
            dsv = ds_ref[...].astype(w_ref.dtype)
            dh = dh + lax.dot_general(dsv, w_ref[:, off:off + wd], (((1,), (1,)), ((), ())), preferred_element_type=F32)
            dw_ref[:, off:off + wd] += lax.dot_general(h, dsv, (((0,), (0,)), ((), ())), preferred_element_type=F32)
            off += wd
        dg_ref[...] += jnp.sum(dh * xn, axis=0, keepdims=True)
        dx_ref[...] = dxr_ref[...] + _rms_bwd(dh * g, xn, r)

    return pl.pallas_call(
        body, name="in_proj_bwd", grid=(t // tm,),
        in_specs=[pl.BlockSpec((tm, d), lambda i: (i, 0)), pl.BlockSpec((1, d), lambda i: (0, 0)),
                  pl.BlockSpec((d, npk), lambda i: (0, 0))]
                 + [pl.BlockSpec((tm, wd), lambda i: (i, 0)) for wd in segs]
                 + [pl.BlockSpec((tm, d), lambda i: (i, 0))],
        out_specs=[pl.BlockSpec((tm, d), lambda i: (i, 0)), pl.BlockSpec((d, npk), lambda i: (0, 0)),
                   pl.BlockSpec((1, d), lambda i: (0, 0))],
        out_shape=[jax.ShapeDtypeStruct((t, d), F32), jax.ShapeDtypeStruct((d, npk), F32),
                   jax.ShapeDtypeStruct((1, d), F32)],
        compiler_params=_params("arbitrary"))(x, g1, wp, *dsegs, dx_res)


def _out_proj_fwd(x0, mix, wo, g2, tm):
    t, d = x0.shape
    dq = wo.shape[1]
    widths = [m.shape[1] for m in mix]

    def body(x_ref, *rest):
        m_refs = rest[:len(mix)]
        w_ref, g_ref, x1_ref, h2_ref = rest[len(mix):]
        acc = x_ref[...]
        off = 0
        for m_ref, wd in zip(m_refs, widths):
            for k in range(wd // dq):
                acc = acc + jnp.dot(m_ref[:, k * dq:(k + 1) * dq].astype(w_ref.dtype), w_ref[off // dq + k],
                                    preferred_element_type=F32)
            off += wd
        x1_ref[...] = acc
        xn, _ = _rms_fwd(acc)
        h2_ref[...] = (xn * g_ref[...]).astype(h2_ref.dtype)

    return pl.pallas_call(
        body, name="out_proj_fwd", grid=(t // tm,),
        in_specs=[pl.BlockSpec((tm, d), lambda i: (i, 0))]
                 + [pl.BlockSpec((tm, wd), lambda i: (i, 0)) for wd in widths]
                 + [pl.BlockSpec((4, dq, d), lambda i: (0, 0, 0)), pl.BlockSpec((1, d), lambda i: (0, 0))],
        out_specs=[pl.BlockSpec((tm, d), lambda i: (i, 0)), pl.BlockSpec((tm, d), lambda i: (i, 0))],
        out_shape=[jax.ShapeDtypeStruct((t, d), F32), jax.ShapeDtypeStruct((t, d), MM_DTYPE)],
        compiler_params=_params("arbitrary"))(x0, *mix, wo, g2)


def _out_proj_bwd(dx2, dh2, x1, g2, mix, wo, tm):
    t, d = x1.shape
    dq = wo.shape[1]
    widths = [m.shape[1] for m in mix]
    nm = len(mix)

    def body(dx2_ref, dh2_ref, x1_ref, g_ref, *rest):
        m_refs = rest[:nm]
        w_ref = rest[nm]
        dx1_ref = rest[nm + 1]
        dm_refs = rest[nm + 2:nm + 2 + nm]
        dw_ref, dg_ref = rest[nm + 2 + nm:]
        i = pl.program_id(0)

        @pl.when(i == 0)
        def _():
            dw_ref[...] = jnp.zeros_like(dw_ref)
            dg_ref[...] = jnp.zeros_like(dg_ref)

        xn, r = _rms_fwd(x1_ref[...])
        dh2v = dh2_ref[...]
        dg_ref[...] += jnp.sum(dh2v * xn, axis=0, keepdims=True)
        dx1 = dx2_ref[...] + _rms_bwd(dh2v * g_ref[...], xn, r)
        dx1_ref[...] = dx1
        dx1c = dx1.astype(w_ref.dtype)
        off = 0
        for m_ref, dm_ref, wd in zip(m_refs, dm_refs, widths):
            for k in range(wd // dq):
                j = off // dq + k
                cols = slice(k * dq, (k + 1) * dq)
                dm_ref[:, cols] = lax.dot_general(dx1c, w_ref[j], (((1,), (1,)), ((), ())), preferred_element_type=F32)
                dw_ref[j] += lax.dot_general(m_ref[:, cols].astype(w_ref.dtype), dx1c, (((0,), (0,)), ((), ())),
                                             preferred_element_type=F32)
            off += wd

    tile = lambda wd: pl.BlockSpec((tm, wd), lambda i: (i, 0))
    return pl.pallas_call(
        body, name="out_proj_bwd", grid=(t // tm,),
        in_specs=[tile(d), tile(d), tile(d), pl.BlockSpec((1, d), lambda i: (0, 0))]
                 + [tile(wd) for wd in widths] + [pl.BlockSpec((4, dq, d), lambda i: (0, 0, 0))],
        out_specs=[tile(d)] + [tile(wd) for wd in widths]
                  + [pl.BlockSpec((4, dq, d), lambda i: (0, 0, 0)), pl.BlockSpec((1, d), lambda i: (0, 0))],
        out_shape=[jax.ShapeDtypeStruct((t, d), F32)] + [jax.ShapeDtypeStruct((t, wd), F32) for wd in widths]
                  + [jax.ShapeDtypeStruct((4, dq, d), F32), jax.ShapeDtypeStruct((1, d), F32)],
        compiler_params=_params("arbitrary"))(dx2, dh2, x1, g2, *mix, wo)


def _ffn_fwd(x1, h2, wg, wu, wd, tm):
    t, d = x1.shape
    fs = wg.shape[2]

    def body(x1_ref, h2_ref, wg_ref, wu_ref, wd_ref, x2_ref, gp_ref, up_ref):
        @pl.when(pl.program_id(1) == 0)
        def _():
            x2_ref[...] = x1_ref[...]

        h = h2_ref[...]
        gp = jnp.dot(h, wg_ref[...], preferred_element_type=F32)
        up = jnp.dot(h, wu_ref[...], preferred_element_type=F32)
        gp_ref[...] = gp
        up_ref[...] = up
        ff = gp * _sigmoid(gp) * up
        x2_ref[...] += jnp.dot(ff.astype(wd_ref.dtype), wd_ref[...], preferred_element_type=F32)

    return pl.pallas_call(
        body, name="ffn_fwd", grid=(t // tm, 4),
        in_specs=[pl.BlockSpec((tm, d), lambda i, j: (i, 0)), pl.BlockSpec((tm, d), lambda i, j: (i, 0)),
                  pl.BlockSpec((None, d, fs), lambda i, j: (j, 0, 0)),
                  pl.BlockSpec((None, d, fs), lambda i, j: (j, 0, 0)),
                  pl.BlockSpec((None, fs, d), lambda i, j: (j, 0, 0))],
        out_specs=[pl.BlockSpec((tm, d), lambda i, j: (i, 0)), pl.BlockSpec((None, tm, fs), lambda i, j: (j, i, 0)),
                   pl.BlockSpec((None, tm, fs), lambda i, j: (j, i, 0))],
        out_shape=[jax.ShapeDtypeStruct((t, d), F32), jax.ShapeDtypeStruct((4, t, fs), F32),
                   jax.ShapeDtypeStruct((4, t, fs), F32)],
        compiler_params=_params("arbitrary", "arbitrary"))(x1, h2, wg, wu, wd)


def _ffn_bwd(dx2, h2, gp, up, wg, wu, wd, tm):
    t, d = dx2.shape
    fs = wg.shape[2]

    def body(dx2_ref, h2_ref, gp_ref, up_ref, wg_ref, wu_ref, wd_ref, dh2_ref, dwg_ref, dwu_ref, dwd_ref):
        j, i = pl.program_id(0), pl.program_id(1)

        @pl.when(i == 0)
        def _():
            dwg_ref[...] = jnp.zeros_like(dwg_ref)
            dwu_ref[...] = jnp.zeros_like(dwu_ref)
            dwd_ref[...] = jnp.zeros_like(dwd_ref)

        cdt = wg_ref.dtype
        h = h2_ref[...]
        gpv, upv = gp_ref[...], up_ref[...]
        s = _sigmoid(gpv)
        silu = gpv * s
        dx2c = dx2_ref[...].astype(cdt)
        dff = lax.dot_general(dx2c, wd_ref[...], (((1,), (1,)), ((), ())), preferred_element_type=F32)
        dwd_ref[...] += lax.dot_general((silu * upv).astype(cdt), dx2c, (((0,), (0,)), ((), ())), preferred_element_type=F32)
        dup = (dff * silu).astype(cdt)
        dgp = (dff * upv * _dsilu(gpv, s)).astype(cdt)
        dwg_ref[...] += lax.dot_general(h, dgp, (((0,), (0,)), ((), ())), preferred_element_type=F32)
        dwu_ref[...] += lax.dot_general(h, dup, (((0,), (0,)), ((), ())), preferred_element_type=F32)
        dh = (lax.dot_general(dgp, wg_ref[...], (((1,), (1,)), ((), ())), preferred_element_type=F32)
              + lax.dot_general(dup, wu_ref[...], (((1,), (1,)), ((), ())), preferred_element_type=F32))
        rows = pl.ds(pl.multiple_of(i * tm, tm), tm)

        @pl.when(j == 0)
        def _():
            dh2_ref[rows, :] = dh

        @pl.when(j != 0)
        def _():
            dh2_ref[rows, :] += dh

    return pl.pallas_call(
        body, name="ffn_bwd", grid=(4, t // tm),
        in_specs=[pl.BlockSpec((tm, d), lambda j, i: (i, 0)), pl.BlockSpec((tm, d), lambda j, i: (i, 0)),
                  pl.BlockSpec((None, tm, fs), lambda j, i: (j, i, 0)), pl.BlockSpec((None, tm, fs), lambda j, i: (j, i, 0)),
                  pl.BlockSpec((None, d, fs), lambda j, i: (j, 0, 0)),
                  pl.BlockSpec((None, d, fs), lambda j, i: (j, 0, 0)),
                  pl.BlockSpec((None, fs, d), lambda j, i: (j, 0, 0))],
        out_specs=[pl.BlockSpec((t, d), lambda j, i: (0, 0)), pl.BlockSpec((None, d, fs), lambda j, i: (j, 0, 0)),
                   pl.BlockSpec((None, d, fs), lambda j, i: (j, 0, 0)), pl.BlockSpec((None, fs, d), lambda j, i: (j, 0, 0))],
        out_shape=[jax.ShapeDtypeStruct((t, d), F32), jax.ShapeDtypeStruct((4, d, fs), F32),
                   jax.ShapeDtypeStruct((4, d, fs), F32), jax.ShapeDtypeStruct((4, fs, d), F32)],
        compiler_params=_params("arbitrary", "arbitrary"))(dx2, h2, gp, up, wg, wu, wd)


def _ple_fwd(x2, p, wpg, wpp, tm):
    t, d = x2.shape
    q = p.shape[1]
    dq = d // 4

    def body(x_ref, p_ref, wg_ref, wp_ref, o_ref):
        xv = x_ref[...]
        xc = xv.astype(wg_ref.dtype)
        pc = p_ref[...].astype(wp_ref.dtype)
        pre = jnp.dot(xc[:, :dq], wg_ref[0], preferred_element_type=F32)
        for j in range(1, 4):
            pre = pre + jnp.dot(xc[:, j * dq:(j + 1) * dq], wg_ref[j], preferred_element_type=F32)
        gate = _sigmoid(pre)
        for j in range(4):
            cols = slice(j * dq, (j + 1) * dq)
            o_ref[:, cols] = xv[:, cols] + gate[:, cols] * jnp.dot(pc, wp_ref[j], preferred_element_type=F32)

    return pl.pallas_call(
        body, name="ple_fwd", grid=(t // tm,),
        in_specs=[pl.BlockSpec((tm, d), lambda i: (i, 0)), pl.BlockSpec((tm, q), lambda i: (i, 0)),
                  pl.BlockSpec((4, dq, d), lambda i: (0, 0, 0)),
                  pl.BlockSpec((4, q, dq), lambda i: (0, 0, 0))],
        out_specs=pl.BlockSpec((tm, d), lambda i: (i, 0)),
        out_shape=jax.ShapeDtypeStruct((t, d), F32),
        compiler_params=_params("arbitrary"))(x2, p, wpg, wpp)


def _ple_bwd(dx3, x2, p, wpg, wpp, tm):
    t, d = x2.shape
    q = p.shape[1]
    dq = d // 4

    def body(dx3_ref, x_ref, p_ref, wg_ref, wp_ref, dx2_ref, dwg_ref, dwp_ref):
        @pl.when(pl.program_id(0) == 0)
        def _():
            dwg_ref[...] = jnp.zeros_like(dwg_ref)
            dwp_ref[...] = jnp.zeros_like(dwp_ref)

        cdt = wg_ref.dtype
        xc = x_ref[...].astype(cdt)
        pc = p_ref[...].astype(cdt)
        pre = jnp.dot(xc[:, :dq], wg_ref[0], preferred_element_type=F32)
        for j in range(1, 4):
            pre = pre + jnp.dot(xc[:, j * dq:(j + 1) * dq], wg_ref[j], preferred_element_type=F32)
        gate = _sigmoid(pre)
        dx3v = dx3_ref[...]
        dpp = (dx3v * gate).astype(cdt)
        dgate = dx3v * gate * (1.0 - gate)
        dpre_parts = []
        for j in range(4):
            cols = slice(j * dq, (j + 1) * dq)
            pp_j = jnp.dot(pc, wp_ref[j], preferred_element_type=F32)
            dpre_parts.append((dgate[:, cols] * pp_j).astype(cdt))
            dwp_ref[j] += lax.dot_general(pc, dpp[:, cols], (((0,), (0,)), ((), ())), preferred_element_type=F32)
        dpre = jnp.concatenate(dpre_parts, axis=1)
        for j in range(4):
            cols = slice(j * dq, (j + 1) * dq)
            dwg_ref[j] += lax.dot_general(xc[:, cols], dpre, (((0,), (0,)), ((), ())), preferred_element_type=F32)
            dx2_ref[:, cols] = dx3v[:, cols] + lax.dot_general(dpre, wg_ref[j], (((1,), (1,)), ((), ())),
                                                               preferred_element_type=F32)

    return pl.pallas_call(
        body, name="ple_bwd", grid=(t // tm,),
        in_specs=[pl.BlockSpec((tm, d), lambda i: (i, 0)), pl.BlockSpec((tm, d), lambda i: (i, 0)),
                  pl.BlockSpec((tm, q), lambda i: (i, 0)), pl.BlockSpec((4, dq, d), lambda i: (0, 0, 0)),
                  pl.BlockSpec((4, q, dq), lambda i: (0, 0, 0))],
        out_specs=[pl.BlockSpec((tm, d), lambda i: (i, 0)), pl.BlockSpec((4, dq, d), lambda i: (0, 0, 0)),
                   pl.BlockSpec((4, q, dq), lambda i: (0, 0, 0))],
        out_shape=[jax.ShapeDtypeStruct((t, d), F32), jax.ShapeDtypeStruct((4, dq, d), F32),
                   jax.ShapeDtypeStruct((4, q, dq), F32)],
        compiler_params=_params("arbitrary"))(dx3, x2, p, wpg, wpp)


def _loss_head(x, target, fg, tm):
    t, d = x.shape

    def body(x_ref, t_ref, g_ref, dx_ref, loss_ref, dg_ref):
        @pl.when(pl.program_id(0) == 0)
        def _():
            loss_ref[...] = jnp.zeros_like(loss_ref)
            dg_ref[...] = jnp.zeros_like(dg_ref)

        xn, r = _rms_fwd(x_ref[...])
        g = g_ref[...]
        err = xn * g - t_ref[...]
        loss_ref[...] += 0.5 * jnp.sum(jnp.sum(err * err, axis=-1, keepdims=True) / d, axis=0, keepdims=True)
        dy = err / d
        dg_ref[...] += jnp.sum(dy * xn, axis=0, keepdims=True)
        dx_ref[...] = _rms_bwd(dy * g, xn, r)

    return pl.pallas_call(
        body, name="loss_head", grid=(t // tm,),
        in_specs=[pl.BlockSpec((tm, d), lambda i: (i, 0)), pl.BlockSpec((tm, d), lambda i: (i, 0)),
                  pl.BlockSpec((1, d), lambda i: (0, 0))],
        out_specs=[pl.BlockSpec((tm, d), lambda i: (i, 0)), pl.BlockSpec((1, 1), lambda i: (0, 0)),
                   pl.BlockSpec((1, d), lambda i: (0, 0))],
        out_shape=[jax.ShapeDtypeStruct((t, d), F32), jax.ShapeDtypeStruct((1, 1), F32),
                   jax.ShapeDtypeStruct((1, d), F32)],
        compiler_params=_params("arbitrary"))(x, target, fg)


def _qkv_conv_act(xv, w, j, heads):
    k = QKV_CONV_WIDTH
    y = w[k - 1:k] * xv
    for s in range(1, k):
        y = y + w[k - 1 - s:k - s] * _shift_down(xv, s)
    sg = _sigmoid(y)
    s_act = y * sg
    nrm = lax.rsqrt(jnp.sum(s_act * s_act, axis=-1, keepdims=True) + EPS)
    scale = jnp.where(j < heads, HEAD_DIM ** -0.5, 1.0).astype(F32)
    return y, sg, s_act, nrm, scale


def _qkv_conv_fwd(qkv_pre, conv_w, heads):
    t = qkv_pre.shape[0]
    nblk = 3 * heads

    def body(x_ref, w_ref, o_ref):
        j = pl.program_id(0)
        _, _, s_act, nrm, scale = _qkv_conv_act(x_ref[...], w_ref[...], j, heads)
        o_ref[...] = jnp.where(j < 2 * heads, s_act * (nrm * scale), s_act)

    return pl.pallas_call(
        body, name="qkv_conv_fwd", grid=(nblk,),
        in_specs=[pl.BlockSpec((t, LANES), lambda j: (0, j)), pl.BlockSpec((QKV_CONV_WIDTH, LANES), lambda j: (0, j))],
        out_specs=pl.BlockSpec((t, LANES), lambda j: (0, j)),
        out_shape=jax.ShapeDtypeStruct(qkv_pre.shape, F32),
        compiler_params=_params("arbitrary"))(qkv_pre, conv_w)


def _qkv_conv_bwd(qkv_pre, conv_w, dqkv, heads):
    t = qkv_pre.shape[0]
    nblk = 3 * heads
    k = QKV_CONV_WIDTH

    def body(x_ref, w_ref, dn_ref, dx_ref, dw_ref):
        j = pl.program_id(0)
        xv, w = x_ref[...], w_ref[...]
        y, sg, s_act, nrm, scale = _qkv_conv_act(xv, w, j, heads)
        dn = dn_ref[...]
        dsn = dn * scale
        ds_qk = nrm * dsn - s_act * (nrm * nrm * nrm) * jnp.sum(dsn * s_act, axis=-1, keepdims=True)
        ds = jnp.where(j < 2 * heads, ds_qk, dn)
        dy = ds * _dsilu(y, sg)
        dx = w[k - 1:k] * dy
        dw_ref[k - 1:k, :] = jnp.sum(dy * xv, axis=0, keepdims=True)
        for s in range(1, k):
            dx = dx + w[k - 1 - s:k - s] * _shift_up(dy, s)
            dw_ref[k - 1 - s:k - s, :] = jnp.sum(dy * _shift_down(xv, s), axis=0, keepdims=True)
        dx_ref[...] = dx

    return pl.pallas_call(
        body, name="qkv_conv_bwd", grid=(nblk,),
        in_specs=[pl.BlockSpec((t, LANES), lambda j: (0, j)), pl.BlockSpec((k, LANES), lambda j: (0, j)),
                  pl.BlockSpec((t, LANES), lambda j: (0, j))],
        out_specs=[pl.BlockSpec((t, LANES), lambda j: (0, j)), pl.BlockSpec((k, LANES), lambda j: (0, j))],
        out_shape=[jax.ShapeDtypeStruct(qkv_pre.shape, F32), jax.ShapeDtypeStruct(conv_w.shape, F32)],
        compiler_params=_params("arbitrary"))(qkv_pre, conv_w, dqkv)


def _pool_windows(shape, j, group_dim):
    lane = lax.broadcasted_iota(jnp.int32, shape, 1) + j * LANES
    grp = lane // group_dim
    win = jnp.left_shift(2, grp).astype(F32)
    cnt = jnp.minimum((_rows(shape) + 1).astype(F32), win)
    return grp, cnt


def _pool_select(grp, levels):
    out = levels[0]
    for gi in range(1, POOL_GROUPS):
        out = jnp.where(grp == gi, levels[gi], out)
    return out


def _pool_mean(hv, grp, cnt):
    acc, levels, width = hv, [], 1
    for _ in range(POOL_GROUPS):
        acc = acc + _shift_down(acc, width)
        width *= 2
        levels.append(acc)
    return _pool_select(grp, levels) / cnt - hv


def _pool_fwd(hp, wbd, scale, group_dim):
    t, dp = hp.shape

    def body(h_ref, w_ref, s_ref, o_ref):
        hv = h_ref[...]
        grp, cnt = _pool_windows(hv.shape, pl.program_id(0), group_dim)
        pooled = _pool_mean(hv, grp, cnt)
        o_ref[...] = _mm(pooled, w_ref[...]) * s_ref[...]

    return pl.pallas_call(
        body, name="pool_fwd", grid=(dp // LANES,),
        in_specs=[pl.BlockSpec((t, LANES), lambda j: (0, j)), pl.BlockSpec((LANES, LANES), lambda j: (j, j)),
                  pl.BlockSpec((1, LANES), lambda j: (0, j))],
        out_specs=pl.BlockSpec((t, LANES), lambda j: (0, j)),
        out_shape=jax.ShapeDtypeStruct(hp.shape, F32),
        compiler_params=_params("arbitrary"))(hp, wbd, scale)


def _pool_bwd(hp, wbd, scale, dob, group_dim):
    t, dp = hp.shape

    def body(h_ref, w_ref, s_ref, do_ref, dh_ref, dw_ref, ds_ref):
        hv = h_ref[...]
        grp, cnt = _pool_windows(hv.shape, pl.program_id(0), group_dim)
        pooled = _pool_mean(hv, grp, cnt)
        wv = w_ref[...]
        dov = do_ref[...]
        ds_ref[...] = jnp.sum(dov * _mm(pooled, wv), axis=0, keepdims=True)
        dys = dov * s_ref[...]
        dw_ref[0] = _mm_tn(pooled, dys)
        dpooled = _mm_nt(dys, wv)
        acc, levels, width = dpooled / cnt, [], 1
        for _ in range(POOL_GROUPS):
            acc = acc + _shift_up(acc, width)
            width *= 2
            levels.append(acc)
        dh_ref[...] = _pool_select(grp, levels) - dpooled

    nb = dp // LANES
    return pl.pallas_call(
        body, name="pool_bwd", grid=(nb,),
        in_specs=[pl.BlockSpec((t, LANES), lambda j: (0, j)), pl.BlockSpec((LANES, LANES), lambda j: (j, j)),
                  pl.BlockSpec((1, LANES), lambda j: (0, j)), pl.BlockSpec((t, LANES), lambda j: (0, j))],
        out_specs=[pl.BlockSpec((t, LANES), lambda j: (0, j)), pl.BlockSpec((1, LANES, LANES), lambda j: (j, 0, 0)),
                   pl.BlockSpec((1, LANES), lambda j: (0, j))],
        out_shape=[jax.ShapeDtypeStruct(hp.shape, F32), jax.ShapeDtypeStruct((nb, LANES, LANES), F32),
                   jax.ShapeDtypeStruct((1, dp), F32)],
        compiler_params=_params("arbitrary"))(hp, wbd, scale, dob)


def _sconv_fwd(cbcch, w):
    t, dc3 = cbcch.shape
    nb = dc3 // 3 // LANES
    k = SCONV_WIDTH

    def body(b_ref, c_ref, h_ref, w_ref, o_ref):
        m = c_ref[...] * h_ref[...]
        wv = w_ref[...]
        y = wv[k - 1:k] * m
        for s in range(1, k):
            y = y + wv[k - 1 - s:k - s] * _shift_down(m, s)
        o_ref[...] = b_ref[...] * y

    return pl.pallas_call(
        body, name="sconv_fwd", grid=(nb,),
        in_specs=[pl.BlockSpec((t, LANES), lambda j: (0, j)), pl.BlockSpec((t, LANES), lambda j: (0, nb + j)),
                  pl.BlockSpec((t, LANES), lambda j: (0, 2 * nb + j)), pl.BlockSpec((k, LANES), lambda j: (0, j))],
        out_specs=pl.BlockSpec((t, LANES), lambda j: (0, j)),
        out_shape=jax.ShapeDtypeStruct((t, dc3 // 3), F32),
        compiler_params=_params("arbitrary"))(cbcch, cbcch, cbcch, w)


def _sconv_bwd(cbcch, w, doc):
    t, dc3 = cbcch.shape
    nb = dc3 // 3 // LANES
    k = SCONV_WIDTH

    def body(b_ref, c_ref, h_ref, w_ref, do_ref, db_ref, dc_ref, dh_ref, dw_ref):
        cv, hv = c_ref[...], h_ref[...]
        m = cv * hv
        wv = w_ref[...]
        dov = do_ref[...]
        dy = dov * b_ref[...]
        y = wv[k - 1:k] * m
        dm = wv[k - 1:k] * dy
        dw_ref[k - 1:k, :] = jnp.sum(dy * m, axis=0, keepdims=True)
        for s in range(1, k):
            ms = _shift_down(m, s)
            y = y + wv[k - 1 - s:k - s] * ms
            dm = dm + wv[k - 1 - s:k - s] * _shift_up(dy, s)
            dw_ref[k - 1 - s:k - s, :] = jnp.sum(dy * ms, axis=0, keepdims=True)
        db_ref[...] = dov * y
        dc_ref[...] = dm * hv
        dh_ref[...] = dm * cv

    col = lambda o: pl.BlockSpec((t, LANES), lambda j: (0, o * nb + j))
    return pl.pallas_call(
        body, name="sconv_bwd", grid=(nb,),
        in_specs=[col(0), col(1), col(2), pl.BlockSpec((k, LANES), lambda j: (0, j)), col(0)],
        out_specs=[col(0), col(0), col(0), pl.BlockSpec((k, LANES), lambda j: (0, j))],
        out_shape=[jax.ShapeDtypeStruct((t, dc3 // 3), F32)] * 3 + [jax.ShapeDtypeStruct(w.shape, F32)],
        compiler_params=_params("arbitrary"))(cbcch, cbcch, cbcch, w, doc)


class _Split:
    def __init__(self, a):
        self.hi = a.astype(jnp.bfloat16)
        self.lo = (a - self.hi.astype(F32)).astype(jnp.bfloat16)


def _per_head(dims, a, b):
    a = a if isinstance(a, _Split) else _Split(a)
    b = b if isinstance(b, _Split) else _Split(b)

    def dot(x, y):
        return lax.dot_general(x, y, (dims, ((), ())), preferred_element_type=F32)

    return jnp.stack([dot(a.hi[h], b.hi[h]) + (dot(a.hi[h], b.lo[h]) + dot(a.lo[h], b.hi[h])) for h in range(a.hi.shape[0])])


def _bmm(a, b):
    return _per_head(((1,), (0,)), a, b)


def _bmm_nt(a, b):
    return _per_head(((1,), (1,)), a, b)


def _bmm_tn(a, b):
    return _per_head(((0,), (0,)), a, b)


def _inv_unit_lower(low):
    c = low.shape[-1]
    eye = (_rows((c, c)) == lax.broadcasted_iota(jnp.int32, (c, c), 1)).astype(F32)
    pw = -low
    inv = eye + pw
    span = 2
    while span < c:
        pws = _Split(pw)
        pw = _bmm(pws, pws)
        inv = inv + _bmm(inv, pw)
        span *= 2
    return inv


def _heads_of(ref, base, heads):
    return jnp.stack([ref[:, base + h * HEAD_DIM:base + (h + 1) * HEAD_DIM] for h in range(heads)])


def _chunk_common(q, k, v, a_col, b_col, alog, dtb, kept=None):
    hn, c, _ = q.shape
    beta = _sigmoid(b_col)
    xg = a_col + dtb
    softplus = jnp.maximum(xg, 0.0) + jnp.log(1.0 + jnp.exp(-jnp.abs(xg)))
    neg_ea = -jnp.exp(alog)
    g = neg_ea * softplus
    ri = _rows((c, c))
    ci = lax.broadcasted_iota(jnp.int32, (c, c), 1)
    incl, strict = ri >= ci, ri > ci
    inclf = jnp.broadcast_to(incl.astype(F32), (hn, c, c))
    gcb = _bmm(inclf, jnp.broadcast_to(g, (hn, c, HEAD_DIM)))
    gc_row = jnp.sum(jnp.where(ri <= ci, jnp.broadcast_to(g, (hn, c, c)), 0.0), axis=1, keepdims=True)
    dmat = jnp.where(incl, jnp.exp(jnp.where(incl, gcb[:, :, :1] - gc_row, 0.0)), 0.0)
    eg = jnp.exp(gcb)
    gl = gcb[:, c - 1:c, :]
    egl = jnp.exp(gl)
    edl = jnp.exp(gl - gcb)
    kb, vb = k * beta, v * beta
    kbe = kb * eg
    if kept is None:
        ks = _Split(k)
        a0 = _bmm_nt(kb, ks)
        tm = _inv_unit_lower(jnp.where(strict, a0 * dmat, 0.0))
        p0 = _bmm_nt(q, ks)
        tms = _Split(tm)
        u, w = _bmm(tms, vb), _bmm(tms, kbe)
    else:
        (a0, tm, p0, w), u = kept, None
    return dict(beta=beta, xg=xg, neg_ea=neg_ea, g=g, incl=incl, strict=strict, inclf=inclf, dmat=dmat, eg=eg,
                egl=egl, edl=edl, kb=kb, vb=vb, a0=a0, tm=tm, kbe=kbe, u=u, w=w, p0=p0,
                attn=p0 * dmat, qe=q * eg, kd=k * edl)


def _chunk_step(cm, state):
    ss = _Split(state)
    vn = cm["u"] - _bmm(cm["w"], ss)
    vns = _Split(vn)
    o = _bmm(cm["qe"], ss) + _bmm(cm["attn"], vns)
    new_state = state * cm["egl"][:, :, :1] + _bmm_tn(cm["kd"], vns)
    return vn, o, new_state


def _gated_norm(o, zv, og):
    xo, ro = _rms_fwd(o)
    sgz = _sigmoid(zv)
    return xo, ro, sgz, xo * og * (zv * sgz)


def _gate_columns(abv, gpv, heads):
    a_col = jnp.stack([abv[:, h:h + 1] for h in range(heads)])
    b_col = jnp.stack([abv[:, heads + h:heads + h + 1] for h in range(heads)])
    alog = jnp.stack([gpv[0:1, h:h + 1] for h in range(heads)])
    dtb = jnp.stack([gpv[1:2, h:h + 1] for h in range(heads)])
    return a_col, b_col, alog, dtb


def _delta_fwd(qkv, z, ab, gpar, heads):
    t = qkv.shape[0]
    da = heads * HEAD_DIM
    n = t // CHUNK

    def body(qkv_ref, z_ref, ab_ref, gp_ref, oa_ref, st_ref, kc_ref, kw_ref, s_ref):
        @pl.when(pl.program_id(0) == 0)
        def _():
            s_ref[...] = jnp.zeros_like(s_ref)

        gpv = gp_ref[...]
        cm = _chunk_common(_heads_of(qkv_ref, 0, heads), _heads_of(qkv_ref, da, heads), _heads_of(qkv_ref, 2 * da, heads),
                           *_gate_columns(ab_ref[...], gpv, heads))
        state = s_ref[...]
        st_ref[0] = state
        vn, o, new_state = _chunk_step(cm, state)
        s_ref[...] = new_state
        for slot, val in enumerate((cm["a0"], cm["tm"], cm["p0"])):
            kc_ref[0, slot] = val
        for slot, val in enumerate((cm["w"], vn, o)):
            kw_ref[0, slot] = val
        oa = _gated_norm(o, _heads_of(z_ref, 0, heads), gpv[2:3, :])[3]
        for h in range(heads):
            oa_ref[:, h * HEAD_DIM:(h + 1) * HEAD_DIM] = oa[h]

    return pl.pallas_call(
        body, name="delta_fwd", grid=(n,),
        in_specs=[pl.BlockSpec((CHUNK, 3 * da), lambda i: (i, 0)), pl.BlockSpec((CHUNK, da), lambda i: (i, 0)),
                  pl.BlockSpec((CHUNK, LANES), lambda i: (i, 0)), pl.BlockSpec((8, LANES), lambda i: (0, 0))],
        out_specs=[pl.BlockSpec((CHUNK, da), lambda i: (i, 0)),
                   pl.BlockSpec((1, heads, HEAD_DIM, HEAD_DIM), lambda i: (i, 0, 0, 0)),
                   pl.BlockSpec((1, 3, heads, CHUNK, CHUNK), lambda i: (i, 0, 0, 0, 0)),
                   pl.BlockSpec((1, 3, heads, CHUNK, HEAD_DIM), lambda i: (i, 0, 0, 0, 0))],
        out_shape=[jax.ShapeDtypeStruct((t, da), F32), jax.ShapeDtypeStruct((n, heads, HEAD_DIM, HEAD_DIM), F32),
                   jax.ShapeDtypeStruct((n, 3, heads, CHUNK, CHUNK), F32),
                   jax.ShapeDtypeStruct((n, 3, heads, CHUNK, HEAD_DIM), F32)],
        scratch_shapes=[pltpu.VMEM((heads, HEAD_DIM, HEAD_DIM), F32)],
        compiler_params=_params("arbitrary"))(qkv, z, ab, gpar)


def _delta_bwd(qkv, z, ab, gpar, states, kept_c, kept_w, doa, heads):
    t = qkv.shape[0]
    da = heads * HEAD_DIM
    n = t // CHUNK
    c = CHUNK

    def body(qkv_ref, z_ref, ab_ref, gp_ref, st_ref, kc_ref, kw_ref, doa_ref, dqkv_ref, dz_ref, dab_ref, dpar_ref, ds_ref):
        @pl.when(pl.program_id(0) == 0)
        def _():
            ds_ref[...] = jnp.zeros_like(ds_ref)
            dpar_ref[...] = jnp.zeros_like(dpar_ref)

        gpv = gp_ref[...]
        og = gpv[2:3, :]
        q, k, v = _heads_of(qkv_ref, 0, heads), _heads_of(qkv_ref, da, heads), _heads_of(qkv_ref, 2 * da, heads)
        cm = _chunk_common(q, k, v, *_gate_columns(ab_ref[...], gpv, heads),
                           kept=(kc_ref[0, 0], kc_ref[0, 1], kc_ref[0, 2], kw_ref[0, 0]))
        state = st_ref[0]
        dsp = ds_ref[...]
        vn, o = kw_ref[0, 1], kw_ref[0, 2]
        zv = _heads_of(z_ref, 0, heads)
        xo, ro, sgz, _ = _gated_norm(o, zv, og)
        doav = _heads_of(doa_ref, 0, heads)
        don = doav * (zv * sgz)
        dz = doav * (xo * og) * _dsilu(zv, sgz)
        d_og = jnp.sum(jnp.sum(don * xo, axis=1, keepdims=True), axis=0)
        do = _rms_bwd(don * og, xo, ro)
        tm, dmat, eg, edl, egl = cm["tm"], cm["dmat"], cm["eg"], cm["edl"], cm["egl"]
        dos, dsps, sts, tms, ks = _Split(do), _Split(dsp), _Split(state), _Split(tm), _Split(k)
        dvn = _bmm_tn(cm["attn"], dos) + _bmm(cm["kd"], dsps)
        dvns = _Split(dvn)
        dqe = _bmm_nt(dos, sts)
        ds_ref[...] = _bmm_tn(cm["qe"], dos) + dsp * egl[:, :, :1] - _bmm_tn(cm["w"], dvns)
        dattn = _bmm_nt(dos, vn)
        dkd = _bmm_nt(vn, dsps)
        dkd_kd = jnp.sum(dkd * cm["kd"], axis=-1, keepdims=True)
        dgl = (jnp.sum(jnp.sum(dsp * state, axis=-1, keepdims=True), axis=1, keepdims=True) * egl[:, :, :1]
               + jnp.sum(dkd_kd, axis=1, keepdims=True))
        dgc = jnp.sum(dqe * cm["qe"], axis=-1, keepdims=True) - dkd_kd
        dk = dkd * edl
        dq = dqe * eg
        dw = -_bmm_nt(dvns, sts)
        dws = _Split(dw)
        dp0 = dattn * dmat
        dd = jnp.where(cm["incl"], dattn * cm["p0"], 0.0)
        dp0s = _Split(dp0)
        dq = dq + _bmm(dp0s, ks)
        dk = dk + _bmm_tn(dp0s, q)
        dtm = _bmm_nt(dvns, cm["vb"]) + _bmm_nt(dws, cm["kbe"])
        dvb = _bmm_tn(tms, dvns)
        dkbe = _bmm_tn(tms, dws)
        dkb = dkbe * eg
        dgc = dgc + jnp.sum(dkbe * cm["kbe"], axis=-1, keepdims=True)
        dlow = jnp.where(cm["strict"], -_bmm_tn(tms, _bmm_nt(dtm, tms)), 0.0)
        dd = dd + dlow * cm["a0"]
        da0 = dlow * dmat
        da0s = _Split(da0)
        dkb = dkb + _bmm(da0s, ks)
        dk = dk + _bmm_tn(da0s, cm["kb"])
        ddd = dd * dmat
        ones = jnp.ones((heads, c, HEAD_DIM), F32)
        dgc = dgc + jnp.sum(ddd, axis=-1, keepdims=True) - _bmm_tn(ddd, ones)[:, :, :1]
        dgc = dgc + jnp.where(_rows((c, 1)) == c - 1, dgl, 0.0)
        dg = _bmm_tn(cm["inclf"], jnp.broadcast_to(dgc, (heads, c, HEAD_DIM)))[:, :, :1]
        beta = cm["beta"]
        dk = dk + dkb * beta
        dbeta = jnp.sum(dkb * k, axis=-1, keepdims=True) + jnp.sum(dvb * v, axis=-1, keepdims=True)
        dv = dvb * beta
        db_col = dbeta * beta * (1.0 - beta)
        da_col = dg * cm["neg_ea"] * _sigmoid(cm["xg"])
        d_alog = jnp.sum(dg * cm["g"], axis=1, keepdims=True)
        d_dtb = jnp.sum(da_col, axis=1, keepdims=True)
        lane = lax.broadcasted_iota(jnp.int32, (c, LANES), 1)
        lane8 = lax.broadcasted_iota(jnp.int32, (8, LANES), 1)
        row8 = _rows((8, LANES))
        dab = jnp.zeros((c, LANES), F32)
        dpar = jnp.where(row8 == 2, d_og, 0.0)
        for h in range(heads):
            lo = h * HEAD_DIM
            dqkv_ref[:, lo:lo + HEAD_DIM] = dq[h]
            dqkv_ref[:, da + lo:da + lo + HEAD_DIM] = dk[h]
            dqkv_ref[:, 2 * da + lo:2 * da + lo + HEAD_DIM] = dv[h]
            dz_ref[:, lo:lo + HEAD_DIM] = dz[h]
            dab = dab + jnp.where(lane == h, da_col[h], 0.0) + jnp.where(lane == heads + h, db_col[h], 0.0)
            dpar = (dpar + jnp.where((row8 == 0) & (lane8 == h), d_alog[h], 0.0)
                    + jnp.where((row8 == 1) & (lane8 == h), d_dtb[h], 0.0))
        dab_ref[...] = dab
        dpar_ref[...] += dpar

    rev = lambda i: (n - 1 - i, 0)
    return pl.pallas_call(
        body, name="delta_bwd", grid=(n,),
        in_specs=[pl.BlockSpec((c, 3 * da), rev), pl.BlockSpec((c, da), rev), pl.BlockSpec((c, LANES), rev),
                  pl.BlockSpec((8, LANES), lambda i: (0, 0)),
                  pl.BlockSpec((1, heads, HEAD_DIM, HEAD_DIM), lambda i: (n - 1 - i, 0, 0, 0)),
                  pl.BlockSpec((1, 3, heads, c, c), lambda i: (n - 1 - i, 0, 0, 0, 0)),
                  pl.BlockSpec((1, 3, heads, c, HEAD_DIM), lambda i: (n - 1 - i, 0, 0, 0, 0)),
                  pl.BlockSpec((c, da), rev)],
        out_specs=[pl.BlockSpec((c, 3 * da), rev), pl.BlockSpec((c, da), rev), pl.BlockSpec((c, LANES), rev),
                   pl.BlockSpec((8, LANES), lambda i: (0, 0))],
        out_shape=[jax.ShapeDtypeStruct((t, 3 * da), F32), jax.ShapeDtypeStruct((t, da), F32),
                   jax.ShapeDtypeStruct((t, LANES), F32), jax.ShapeDtypeStruct((8, LANES), F32)],
        scratch_shapes=[pltpu.VMEM((heads, HEAD_DIM, HEAD_DIM), F32)],
        compiler_params=_params("arbitrary"))(qkv, z, ab, gpar, states, kept_c, kept_w, doa)


def _w_in_pieces(shard_cols, da, heads):
    a0, nab = 4 * da, 2 * heads
    d_in = 4 * shard_cols
    runs = [(0, a0, 0), (a0, a0 + nab, d_in - nab), (a0 + nab, d_in, a0)]
    pieces = []
    for j in range(4):
        lo, hi = j * shard_cols, (j + 1) * shard_cols
        for rlo, rhi, plo in runs:
            s, e = max(lo, rlo), min(hi, rhi)
            if s < e:
                pieces.append((j, s - lo, e - s, plo + (s - rlo)))
    return pieces, d_in - nab + LANES


def _w_in_pack(w4, da, heads):
    _, d, sc = w4.shape
    pieces, npk = _w_in_pieces(sc, da, heads)
    tr = _tile_rows(d, 256, SUBLANES_WIRE)

    def body(w_ref, o_ref):
        o_ref[:, npk - LANES:] = jnp.zeros((tr, LANES), o_ref.dtype)
        for j, lo, ln, dst in pieces:
            o_ref[:, dst:dst + ln] = w_ref[j, :, lo:lo + ln]

    return pl.pallas_call(
        body, name="w_in_pack", grid=(d // tr,),
        in_specs=[pl.BlockSpec((4, tr, sc), lambda i: (0, i, 0))],
        out_specs=pl.BlockSpec((tr, npk), lambda i: (i, 0)),
        out_shape=jax.ShapeDtypeStruct((d, npk), w4.dtype),
        compiler_params=_params("arbitrary"))(w4)


def _w_in_unpack(dwp, sc, da, heads):
    d, npk = dwp.shape
    pieces, _ = _w_in_pieces(sc, da, heads)
    tr = _tile_rows(d, 256)

    def body(g_ref, o_ref):
        for j, lo, ln, dst in pieces:
            o_ref[j, :, lo:lo + ln] = g_ref[:, dst:dst + ln]

    return pl.pallas_call(
        body, name="w_in_unpack", grid=(d // tr,),
        in_specs=[pl.BlockSpec((tr, npk), lambda i: (i, 0))],
        out_specs=pl.BlockSpec((4, tr, sc), lambda i: (0, i, 0)),
        out_shape=jax.ShapeDtypeStruct((4, d, sc), F32),
        compiler_params=_params("arbitrary"))(dwp)


def _block_diag(pool_w):
    g, gd, _ = pool_w.shape
    out = jnp.zeros((g * gd, g * gd), pool_w.dtype)
    for gi in range(g):
        out = lax.dynamic_update_slice(out, pool_w[gi], (gi * gd, gi * gd))
    return out


def _layer_dims(d):
    heads = (d // 2) // HEAD_DIM
    return heads, heads * HEAD_DIM, d // 4, d // 4


BIG = ("w_in", "w_gate", "w_up", "ple_proj", "w_out", "w_down", "ple_gate")


def _prepare_layer(small, li):
    d = small["norm1_g"].shape[1]
    heads, _, _, _ = _layer_dims(d)
    gpar = jnp.zeros((8, LANES), F32)
    gpar = gpar.at[0, :heads].set(small["a_log"][li]).at[1, :heads].set(small["dt_bias"][li]).at[2, :].set(small["onorm_g"][li])
    return dict(norm1_g=small["norm1_g"][li][None], conv_qkv=small["conv_qkv"][li], gpar=gpar, pool_bd=_block_diag(small["pool_w"][li]).astype(MM_DTYPE),
                pool_scale=small["pool_scale"][li][None], sconv_w=small["sconv_w"][li], norm2_g=small["norm2_g"][li][None])


def _layer_fwd(x0, p, gw, lw, tm, arrive):
    d = x0.shape[1]
    heads, da, dp, dc = _layer_dims(d)
    segs = (3 * da, da, dp, 3 * dc, LANES)
    lw["w_in_p"] = _w_in_pack(gw["w_in"], da, heads).astype(MM_DTYPE)
    qkv_pre, z, hp, cbcch, ab = _in_proj_fwd(x0, lw["norm1_g"], lw["w_in_p"], segs, tm)
    qkv = _qkv_conv_fwd(qkv_pre, lw["conv_qkv"], heads)
    oa, states, kept_c, kept_w = _delta_fwd(qkv, z, ab, lw["gpar"], heads)
    ob = _pool_fwd(hp, lw["pool_bd"], lw["pool_scale"], dp // POOL_GROUPS)
    oc = _sconv_fwd(cbcch, lw["sconv_w"])
    arrive("mixed", oa)
    x1, h2 = _out_proj_fwd(x0, (oa, ob, oc), gw["w_out"], lw["norm2_g"], tm)
    x2, gp, up = _ffn_fwd(x1, h2, gw["w_gate"], gw["w_up"], gw["w_down"], tm)
    arrive("ffn", x2)
    x3 = _ple_fwd(x2, p, gw["ple_gate"], gw["ple_proj"], tm)
    arrive("end", x3)
    saved = dict(x0=x0, qkv_pre=qkv_pre, z=z, hp=hp, cbcch=cbcch, ab=ab, qkv=qkv, states=states, kept_c=kept_c, kept_w=kept_w, oa=oa, ob=ob, oc=oc,
                 x1=x1, h2=h2, gp=gp, up=up, x2=x2)
    return x3, saved


def _layer_bwd(dx3, p, gw, lw, sv, tm, produced):
    def after_token(tok, arr):
        return arr if tok is None else arr + tok[0, 0]

    d = dx3.shape[1]
    heads, da, dp, dc = _layer_dims(d)
    segs = (3 * da, da, dp, dc, dc, dc, LANES)
    gd = dp // POOL_GROUPS
    dx2, d_ple_gate, d_ple_proj = _ple_bwd(dx3, sv["x2"], p, gw["ple_gate"], gw["ple_proj"], tm)
    dh2, d_w_gate, d_w_up, d_w_down = _ffn_bwd(dx2, sv["h2"], sv["gp"], sv["up"], gw["w_gate"], gw["w_up"], gw["w_down"],
                                               min(tm, 256))
    tok = produced("ffn", dict(w_gate=d_w_gate, w_up=d_w_up, ple_proj=d_ple_proj, w_down=d_w_down, ple_gate=d_ple_gate), dh2)
    dx1, doa, dob, doc, d_w_out, d_norm2 = _out_proj_bwd(dx2, dh2, sv["x1"], after_token(tok, lw["norm2_g"]),
                                                         (sv["oa"], sv["ob"], sv["oc"]), gw["w_out"], tm)
    dcb, dcc, dch, d_sconv = _sconv_bwd(sv["cbcch"], lw["sconv_w"], doc)
    dhp, d_pool_bd, d_pool_scale = _pool_bwd(sv["hp"], lw["pool_bd"], lw["pool_scale"], dob, gd)
    dqkv, dz, dab, dpar = _delta_bwd(sv["qkv"], sv["z"], sv["ab"], lw["gpar"], sv["states"], sv["kept_c"], sv["kept_w"], doa,
                                      heads)
    tok = produced("mixers", {}, dqkv)
    dqkv_pre, d_conv_qkv = _qkv_conv_bwd(sv["qkv_pre"], lw["conv_qkv"], dqkv, heads)
    dsegs = (dqkv_pre, dz, dhp, dcb, dcc, dch, dab)
    dx0, d_w_in_p, d_norm1 = _in_proj_bwd(sv["x0"], after_token(tok, lw["norm1_g"]), lw["w_in_p"], dsegs, dx1, segs, tm)
    per = LANES // gd
    bd = d_pool_bd.reshape(dp // LANES, per, gd, per, gd)
    d_pool_w = jnp.stack([bd[gi // per, gi % per, :, gi % per, :] for gi in range(POOL_GROUPS)])
    big = dict(w_in=_w_in_unpack(d_w_in_p, gw["w_in"].shape[2], da, heads), w_gate=d_w_gate, w_up=d_w_up,
               ple_proj=d_ple_proj, w_out=d_w_out, w_down=d_w_down, ple_gate=d_ple_gate)
    small = dict(norm1_g=d_norm1[0], conv_qkv=d_conv_qkv, a_log=dpar[0, :heads], dt_bias=dpar[1, :heads], onorm_g=dpar[2],
                 pool_w=d_pool_w, pool_scale=d_pool_scale[0], sconv_w=d_sconv, norm2_g=d_norm2[0])
    tok = produced("end", dict(w_in=big["w_in"], w_out=d_w_out), big["w_in"])
    return dx0, big, small, tok


def _local_step(x, p, target, gw, small, produced=None, arrive=None):
    t, d = x.shape
    depth = p.shape[0]
    tm = 512 if t % 512 == 0 else 128
    layers = [_prepare_layer(small, li) for li in range(depth)]
    saved = []
    h = x
    for li in range(depth):
        h, sv = _layer_fwd(h, p[li], gw[li], layers[li], tm,
                           (lambda stage, after, li=li: arrive(li, stage, after)) if arrive else (lambda stage, after: None))
        saved.append(sv)
    dx, loss, d_final = _loss_head(h, target, small["final_g"][None], tm)
    big, sm = [None] * depth, [None] * depth
    token = None
    for li in reversed(range(depth)):
        p_li = p[li] if token is None else p[li] + token[0, 0]
        dx, big[li], sm[li], token = _layer_bwd(
            dx, p_li, gw[li], layers[li], saved[li], tm,
            (lambda stage, grads, after, li=li: produced(li, stage, grads, after)) if produced else (lambda *a: None))
    small_grads = {n: jnp.stack([g[n] for g in sm]) for n in sm[0]}
    small_grads["final_g"] = d_final[0]
    return loss[0, 0], dx, big, small_grads


def _coords():
    return lax.axis_index("x"), lax.axis_index("y"), lax.axis_index("c")


def _other_chips(x, y):
    return [(1 - x, y), (x, 1 - y), (1 - x, 1 - y)]


def _place_shards(ws, me_idx):
    nt = len(ws)
    depth = ws[0].shape[0]

    def body(me_ref, *refs):
        for t, w_ref in enumerate(refs[:nt]):
            for li in range(depth):
                refs[nt + li * nt + t][...] = w_ref[li].astype(WIRE_DTYPE)

    outs = pl.pallas_call(
        body, name="place_shards",
        grid_spec=pltpu.PrefetchScalarGridSpec(
            num_scalar_prefetch=1, grid=(4,),
            in_specs=[pl.BlockSpec((depth, w.shape[1] // 4, w.shape[2]), lambda i, me_ref: (0, i, 0)) for w in ws],
            out_specs=[pl.BlockSpec((None, w.shape[1] // 4, w.shape[2]), lambda i, me_ref: (me_ref[0], i, 0))
                       for _ in range(depth) for w in ws]),
        out_shape=[jax.ShapeDtypeStruct((4,) + w.shape[1:], WIRE_DTYPE) for _ in range(depth) for w in ws],
        compiler_params=_params("arbitrary"))(me_idx, *ws)
    return [list(outs[li * nt:(li + 1) * nt]) for li in range(depth)]


def _half_block(ref, chip, pc):
    rh = ref.shape[1] // 2
    return ref.at[chip, pl.ds(pc * rh, rh)]


def _gather_copies(out_refs, send_sems, recv_sems, stage):
    nt = len(out_refs)
    x, y, c = _coords()
    pairs = []
    for j, (cx, cy) in enumerate(_other_chips(x, y)):
        for t in range(nt):
            sems = dict(send_sem=send_sems[j * nt + t], recv_sem=recv_sems[j * nt + t], device_id_type=MESH)
            if stage == 0:
                mine, theirs, to = _half_block(out_refs[t], 2 * x + y, c), _half_block(out_refs[t], 2 * cx + cy, c), (cx, cy, c)
            else:
                mine, theirs, to = (_half_block(out_refs[t], 2 * cx + cy, c), _half_block(out_refs[t], 2 * cx + cy, 1 - c),
                                    (x, y, 1 - c))
            pairs.append((pltpu.make_async_remote_copy(src_ref=mine, dst_ref=mine, device_id=to, **sems),
                          pltpu.make_async_remote_copy(src_ref=theirs, dst_ref=theirs, device_id=to, **sems)))
    return pairs


def _all_gather_chips(placed):
    nt = len(placed)

    def body(*refs):
        out_refs = refs[nt:2 * nt]
        send_sems, recv_sems = refs[2 * nt:]
        nc = 3 * nt
        first = _gather_copies(out_refs, [send_sems.at[k] for k in range(nc)], [recv_sems.at[k] for k in range(nc)], 0)
        passed = _gather_copies(out_refs, [send_sems.at[nc + k] for k in range(nc)], [recv_sems.at[nc + k] for k in range(nc)], 1)
        for start, _ in first:
            start.start()
        for (_, arrival), (forward, _) in zip(first, passed):
            arrival.wait_recv()
            forward.start()
        for _, arrival in passed:
            arrival.wait_recv()
        for start, _ in first + passed:
            start.wait_send()

    return pl.pallas_call(
        body, name="all_gather_chips", out_shape=[jax.ShapeDtypeStruct(a.shape, a.dtype) for a in placed],
        in_specs=[ANY] * nt, out_specs=[ANY] * nt, input_output_aliases={t: t for t in range(nt)},
        scratch_shapes=[pltpu.SemaphoreType.DMA((6 * nt,)), pltpu.SemaphoreType.DMA((6 * nt,))],
    )(*placed)


def _gather_call(name, arrs, wait_sems, after, stage):
    nt = len(arrs)
    nc = 3 * nt
    n_wait = len(wait_sems)
    n_new = 2 * nc if stage < 2 else 0
    arrs = [pltpu.with_memory_space_constraint(a, pltpu.HBM) for a in arrs]

    def body(*refs):
        a_refs = refs[:nt]
        waits = refs[nt:nt + n_wait]
        news = refs[nt + n_wait + 1:nt + n_wait + 1 + n_new]
        token = refs[-1]
        if stage > 0:
            for start, arrival in _gather_copies(a_refs, waits[:nc], waits[nc:], stage - 1):
                start.wait_send()
                arrival.wait_recv()
        if stage < 2:
            for start, _ in _gather_copies(a_refs, news[:nc], news[nc:], stage):
                start.start()
        token[...] = jnp.zeros_like(token)

    outs = pl.pallas_call(
        body, name=name,
        out_shape=(*[pltpu.SemaphoreType.DMA(())] * n_new, *[pltpu.HBM(a.shape, a.dtype) for a in arrs],
                   jax.ShapeDtypeStruct((8, LANES), F32)),
        in_specs=[HBM] * nt + [SEM] * n_wait + [ANY],
        out_specs=(*[SEM] * n_new, *[HBM] * nt, pl.BlockSpec(memory_space=pltpu.VMEM)),
        input_output_aliases={t: n_new + t for t in range(nt)},
        compiler_params=pltpu.CompilerParams(has_side_effects=pltpu.SideEffectType.DATAFLOW_SIDE_EFFECTING),
    )(*arrs, *wait_sems, after)
    return list(outs[:n_new]), list(outs[n_new:n_new + nt]), outs[-1]


def _sibling_swap_half(gs):
    nt = len(gs)

    def body(*refs):
        g_refs, out_refs = refs[:nt], refs[nt:2 * nt]
        send_sems, recv_sems = refs[2 * nt:]
        x, y, c = _coords()
        cps = []
        for t in range(nt):
            rh = g_refs[t].shape[1] // 2
            cps.append(pltpu.make_async_remote_copy(src_ref=g_refs[t].at[:, pl.ds((1 - c) * rh, rh)], dst_ref=out_refs[t],
                                                    send_sem=send_sems.at[t], recv_sem=recv_sems.at[t], device_id=(x, y, 1 - c),
                                                    device_id_type=MESH))
        for cp in cps:
            cp.start()
        for cp in cps:
            cp.wait()

    return pl.pallas_call(
        body, name="sibling_swap_half",
        out_shape=[jax.ShapeDtypeStruct((g.shape[0], g.shape[1] // 2, g.shape[2]), g.dtype) for g in gs],
        in_specs=[ANY] * nt, out_specs=[ANY] * nt,
        scratch_shapes=[pltpu.SemaphoreType.DMA((nt,)), pltpu.SemaphoreType.DMA((nt,))])(*gs)


def _add_my_halves(gs, others, c_idx):
    nt = len(gs)

    def body(c_ref, *refs):
        for g_ref, o_ref, out_ref in zip(refs[:nt], refs[nt:2 * nt], refs[2 * nt:]):
            out_ref[...] = (g_ref[...].astype(F32) + o_ref[...].astype(F32)).astype(out_ref.dtype)

    def quarter(g):
        return pl.BlockSpec((None, g.shape[1] // 4, g.shape[2]), lambda j, i, c_ref: (j, i, 0))

    return pl.pallas_call(
        body, name="add_my_halves",
        grid_spec=pltpu.PrefetchScalarGridSpec(
            num_scalar_prefetch=1, grid=(4, 2),
            in_specs=[pl.BlockSpec((None, g.shape[1] // 4, g.shape[2]), lambda j, i, c_ref: (j, 2 * c_ref[0] + i, 0)) for g in gs]
                     + [quarter(g) for g in gs],
            out_specs=[quarter(g) for g in gs]),
        out_shape=[jax.ShapeDtypeStruct((4, g.shape[1] // 2, g.shape[2]), WIRE_DTYPE) for g in gs],
        compiler_params=_params("arbitrary", "arbitrary"))(c_idx, *gs, *others)


def _exchange_chips(parts):
    nt = len(parts)

    def body(*refs):
        p_refs, out_refs = refs[:nt], refs[nt:2 * nt]
        send_sems, recv_sems = refs[2 * nt:]
        x, y, c = _coords()
        chips = _other_chips(x, y)

        def copy(j, t):
            cx, cy = chips[j]
            return pltpu.make_async_remote_copy(src_ref=p_refs[t].at[2 * cx + cy], dst_ref=out_refs[t].at[j],
                                                send_sem=send_sems.at[j, t], recv_sem=recv_sems.at[j, t], device_id=(cx, cy, c),
                                                device_id_type=MESH)

        sends = [copy(j, t) for j in range(3) for t in range(nt)]
        for cp in sends:
            cp.start()
        for cp in sends:
            cp.wait_recv()
        for cp in sends:
            cp.wait_send()

    return pl.pallas_call(
        body, name="exchange_chips", out_shape=[jax.ShapeDtypeStruct((3,) + p.shape[1:], p.dtype) for p in parts],
        in_specs=[ANY] * nt, out_specs=[ANY] * nt,
        scratch_shapes=[pltpu.SemaphoreType.DMA((3, nt)), pltpu.SemaphoreType.DMA((3, nt))])(*parts)


def _split_plan(kind, s_refs, l_refs):
    x, y, c = _coords()
    if kind == "swap":
        return [(s.at[:, pl.ds((1 - c) * (s.shape[1] // 2), s.shape[1] // 2)], l, (x, y, 1 - c)) for s, l in zip(s_refs, l_refs)]
    return [(s.at[2 * cx + cy], l.at[j], (cx, cy, c)) for j, (cx, cy) in enumerate(_other_chips(x, y))
            for s, l in zip(s_refs, l_refs)]


def _split_landing(kind, a):
    return (a.shape[0], a.shape[1] // 2, a.shape[2]) if kind == "swap" else (3,) + a.shape[1:]


def _copies_start(name, kind, srcs, after=None):
    ns = len(srcs)
    n = ns if kind == "swap" else 3 * ns
    srcs = [pltpu.with_memory_space_constraint(a, pltpu.HBM) for a in srcs]
    lands = [pltpu.with_memory_space_constraint(lax.empty(_split_landing(kind, a), a.dtype), pltpu.HBM) for a in srcs]
    extra = [] if after is None else [after]

    def body(*refs):
        first_sem = 2 * ns + len(extra)
        sems, token = refs[first_sem:first_sem + 2 * n], refs[-1]
        for k, (src, dst, dev) in enumerate(_split_plan(kind, refs[:ns], refs[ns:2 * ns])):
            pltpu.make_async_remote_copy(src_ref=src, dst_ref=dst, send_sem=sems[k], recv_sem=sems[n + k], device_id=dev,
                                         device_id_type=MESH).start()
        token[...] = jnp.zeros_like(token)

    outs = pl.pallas_call(
        body, name=name,
        out_shape=(*[pltpu.SemaphoreType.DMA(())] * (2 * n), *[pltpu.HBM(a.shape, a.dtype) for a in srcs + lands],
                   jax.ShapeDtypeStruct((8, LANES), F32)),
        in_specs=[HBM] * (2 * ns) + [ANY] * len(extra),
        out_specs=(*[SEM] * (2 * n), *[HBM] * (2 * ns), pl.BlockSpec(memory_space=pltpu.VMEM)),
        input_output_aliases={t: 2 * n + t for t in range(2 * ns)},
        compiler_params=pltpu.CompilerParams(has_side_effects=pltpu.SideEffectType.DATAFLOW_SIDE_EFFECTING),
    )(*srcs, *lands, *extra)
    return list(outs[:2 * n]), list(outs[2 * n:2 * n + ns]), list(outs[2 * n + ns:2 * n + 2 * ns]), outs[-1]


def _copies_wait(name, kind, sems, srcs, lands, after):
    ns = len(srcs)
    n = len(sems) // 2

    def body(*refs):
        sem_refs = refs[2 * ns:2 * ns + 2 * n]
        for k, (src, dst, dev) in enumerate(_split_plan(kind, refs[:ns], refs[ns:2 * ns])):
            cp = pltpu.make_async_remote_copy(src_ref=src, dst_ref=dst, send_sem=sem_refs[k], recv_sem=sem_refs[n + k],
                                              device_id=dev, device_id_type=MESH)
            cp.wait_send()
            cp.wait_recv()

    outs = pl.pallas_call(
        body, name=name, out_shape=tuple(pltpu.HBM(a.shape, a.dtype) for a in srcs + lands),
        in_specs=[HBM] * (2 * ns) + [SEM] * (2 * n) + [ANY], out_specs=tuple([HBM] * (2 * ns)),
        input_output_aliases={t: t for t in range(2 * ns)},
        compiler_params=pltpu.CompilerParams(has_side_effects=pltpu.SideEffectType.DATAFLOW_SIDE_EFFECTING),
    )(*srcs, *lands, *sems, after)
    return list(outs[:ns]), list(outs[ns:])


def _sum_into(pairs, recvs, idx, li, depth, accs):
    nt = len(pairs)

    def body(idx_ref, *refs):
        for p_ref, r_ref, out_ref in zip(refs[:nt], refs[nt:2 * nt], refs[-nt:]):
            out_ref[...] = p_ref[...].astype(F32) + r_ref[0].astype(F32) + r_ref[1].astype(F32) + r_ref[2].astype(F32)

    in_specs = ([pl.BlockSpec((None, p.shape[1] // 2, p.shape[2]), lambda i, idx_ref: (idx_ref[0], i, 0)) for p in pairs]
                + [pl.BlockSpec((3, p.shape[1] // 2, p.shape[2]), lambda i, idx_ref: (0, i, 0)) for p in pairs])
    args = [idx, *pairs, *recvs]
    aliases = {}
    if accs[0] is not None:
        in_specs += [ANY] * nt
        args += list(accs)
        aliases = {1 + 2 * nt + t: t for t in range(nt)}
    return pl.pallas_call(
        body, name="sum_into",
        grid_spec=pltpu.PrefetchScalarGridSpec(
            num_scalar_prefetch=1, grid=(2,), in_specs=in_specs,
            out_specs=[pl.BlockSpec((None, p.shape[1] // 2, p.shape[2]), lambda i, idx_ref: (li, 2 * idx_ref[1] + i, 0))
                       for p in pairs]),
        out_shape=[jax.ShapeDtypeStruct((depth, 2 * p.shape[1], p.shape[2]), F32) for p in pairs],
        input_output_aliases=aliases, compiler_params=_params("arbitrary"))(*args)


def _sum_slots(parts):
    n, rows, cols = parts.shape
    tr = _tile_rows(rows, 512, SUBLANES_WIRE)

    def body(p_ref, out_ref):
        acc = p_ref[0].astype(F32)
        for s in range(1, n):
            acc = acc + p_ref[s].astype(F32)
        out_ref[...] = acc

    return pl.pallas_call(
        body, name="sum_slots", grid=(rows // tr,),
        in_specs=[pl.BlockSpec((n, tr, cols), lambda i: (0, i, 0))],
        out_specs=pl.BlockSpec((tr, cols), lambda i: (i, 0)),
        out_shape=jax.ShapeDtypeStruct((rows, cols), F32),
        compiler_params=_params("arbitrary"))(parts)


def _sibling_share(gs):
    nt = len(gs)
    depth = gs[0].shape[0]

    def body(*refs):
        out_refs = refs[nt:2 * nt]
        send_sems, recv_sems = refs[2 * nt:]
        x, y, c = _coords()
        sends, recvs = [], []
        for t in range(nt):
            rh = out_refs[t].shape[1] // 2
            for li in range(depth):
                mine = out_refs[t].at[li, pl.ds(c * rh, rh)]
                theirs = out_refs[t].at[li, pl.ds((1 - c) * rh, rh)]
                sems = dict(send_sem=send_sems.at[t, li], recv_sem=recv_sems.at[t, li], device_id=(x, y, 1 - c), device_id_type=MESH)
                sends.append(pltpu.make_async_remote_copy(src_ref=mine, dst_ref=mine, **sems))
                recvs.append(pltpu.make_async_remote_copy(src_ref=theirs, dst_ref=theirs, **sems))
        for cp in sends:
            cp.start()
        for cp in recvs:
            cp.wait_recv()
        for cp in sends:
            cp.wait_send()

    return pl.pallas_call(
        body, name="sibling_share", out_shape=[jax.ShapeDtypeStruct(g.shape, g.dtype) for g in gs],
        in_specs=[ANY] * nt, out_specs=[ANY] * nt, input_output_aliases={t: t for t in range(nt)},
        scratch_shapes=[pltpu.SemaphoreType.DMA((nt, depth)), pltpu.SemaphoreType.DMA((nt, depth))])(*gs)


def _all_gather_devices(buf):
    def body(b_ref, out_ref, send_sems, recv_sems, local_sem):
        x, y, c = _coords()
        me = 4 * x + 2 * y + c
        mine = pltpu.make_async_copy(b_ref, out_ref.at[me], local_sem)
        mine.start()
        peers = []
        for k in range(1, 8):
            fx, fy, fc = (k >> 2) & 1, (k >> 1) & 1, k & 1
            peers.append((x ^ fx, y ^ fy, c ^ fc))
        sends = [pltpu.make_async_remote_copy(src_ref=b_ref, dst_ref=out_ref.at[me], send_sem=send_sems.at[k],
                                              recv_sem=recv_sems.at[k], device_id=peer, device_id_type=MESH)
                 for k, peer in enumerate(peers)]
        for cp in sends:
            cp.start()
        for k, (px, py, pc) in enumerate(peers):
            pltpu.make_async_remote_copy(src_ref=b_ref, dst_ref=out_ref.at[4 * px + 2 * py + pc], send_sem=send_sems.at[k],
                                         recv_sem=recv_sems.at[k], device_id=(px, py, pc), device_id_type=MESH).wait_recv()
        for cp in sends:
            cp.wait_send()
        mine.wait()

    return pl.pallas_call(
        body, name="all_gather_devices", out_shape=jax.ShapeDtypeStruct((8,) + buf.shape, buf.dtype),
        in_specs=[ANY], out_specs=ANY,
        scratch_shapes=[pltpu.SemaphoreType.DMA((7,)), pltpu.SemaphoreType.DMA((7,)), pltpu.SemaphoreType.DMA(())])(buf)


def _pair_sums(big_grads, c_idx):
    gs = [big_grads[n] for n in BIG]
    return _add_my_halves(gs, _sibling_swap_half(gs), c_idx)


SMALL_SHARDED = ("conv_qkv", "sconv_w")
REPLICATED = ("norm1_g", "a_log", "dt_bias", "onorm_g", "pool_w", "pool_scale", "norm2_g", "final_g")
ALL_WEIGHTS = ("norm1_g", "w_in", "conv_qkv", "a_log", "dt_bias", "onorm_g", "pool_w", "pool_scale", "sconv_w", "w_out",
               "norm2_g", "w_gate", "w_up", "w_down", "ple_proj", "ple_gate", "final_g")


def _pad_rows(flat, row_multiple):
    m = flat.shape[0]
    r = -(-m // (LANES * row_multiple)) * row_multiple
    return jnp.pad(flat, (0, r * LANES - m)).reshape(r, LANES)


def _adamw(w, g, m, v):
    shape = w.shape
    cols = shape[-1]
    rows = w.size // cols
    tr = _tile_rows(rows, 512)
    c1 = 1.0 / (1.0 - ADAM_B1 ** ADAM_STEP)
    c2 = 1.0 / (1.0 - ADAM_B2 ** ADAM_STEP)

    def body(w_ref, g_ref, m_ref, v_ref, d_ref, nm_ref, nv_ref, go_ref):
        gv = g_ref[...]
        nm = ADAM_B1 * m_ref[...] + (1.0 - ADAM_B1) * gv
        nv = ADAM_B2 * v_ref[...] + (1.0 - ADAM_B2) * (gv * gv)
        nm_ref[...] = nm
        nv_ref[...] = nv
        go_ref[...] = gv
        d_ref[...] = -ADAM_LR * ((nm * c1) / (jnp.sqrt(nv * c2) + ADAM_EPS) + ADAM_WD * w_ref[...])

    spec = pl.BlockSpec((tr, cols), lambda i: (i, 0))
    outs = pl.pallas_call(
        body, name="adamw", grid=(rows // tr,), in_specs=[spec] * 4, out_specs=[spec] * 4,
        out_shape=[jax.ShapeDtypeStruct((rows, cols), F32)] * 4,
        compiler_params=_params("arbitrary"))(*[a.reshape(rows, cols) for a in (w, g, m, v)])
    return tuple(o.reshape(shape) for o in outs)


def kernel(x, p, norm1_g, w_in, conv_qkv, a_log, dt_bias, onorm_g, pool_w, pool_scale, sconv_w, w_out, norm2_g, w_gate, w_up, w_down, ple_proj, ple_gate, final_g, loss_target, m_norm1_g, m_w_in, m_conv_qkv, m_a_log, m_dt_bias, m_onorm_g, m_pool_w, m_pool_scale, m_sconv_w, m_w_out, m_norm2_g, m_w_gate, m_w_up, m_w_down, m_ple_proj, m_ple_gate, m_final_g, v_norm1_g, v_w_in, v_conv_qkv, v_a_log, v_dt_bias, v_onorm_g, v_pool_w, v_pool_scale, v_sconv_w, v_w_out, v_norm2_g, v_w_gate, v_w_up, v_w_down, v_ple_proj, v_ple_gate, v_final_g):
    weights = dict(zip(ALL_WEIGHTS, (norm1_g, w_in, conv_qkv, a_log, dt_bias, onorm_g, pool_w, pool_scale, sconv_w, w_out,
                                     norm2_g, w_gate, w_up, w_down, ple_proj, ple_gate, final_g)))
    mom_m = dict(zip(ALL_WEIGHTS, (m_norm1_g, m_w_in, m_conv_qkv, m_a_log, m_dt_bias, m_onorm_g, m_pool_w, m_pool_scale,
                                   m_sconv_w, m_w_out, m_norm2_g, m_w_gate, m_w_up, m_w_down, m_ple_proj, m_ple_gate, m_final_g)))
    mom_v = dict(zip(ALL_WEIGHTS, (v_norm1_g, v_w_in, v_conv_qkv, v_a_log, v_dt_bias, v_onorm_g, v_pool_w, v_pool_scale,
                                   v_sconv_w, v_w_out, v_norm2_g, v_w_gate, v_w_up, v_w_down, v_ple_proj, v_ple_gate, v_final_g)))
    c_idx = lax.axis_index("c").astype(jnp.int32).reshape(1)
    chip = (2 * lax.axis_index("x") + lax.axis_index("y")).astype(jnp.int32)
    me_idx = chip.reshape(1)
    idx = jnp.stack([chip, lax.axis_index("c").astype(jnp.int32)])
    depth = p.shape[0]

    placed = _place_shards([weights[n] for n in BIG], me_idx)
    gw = [dict() for _ in range(depth)]
    gw[0]["w_in"] = _all_gather_chips(placed[0][:1])[0]
    rest = BIG[1:]
    sems_a, arrs_a, token = _gather_call("gather_rest_start", placed[0][1:], [], gw[0]["w_in"], 0)
    upper = []
    for li in range(1, depth):
        sems_u, arrs_u, token = _gather_call("gather_upper_start", placed[li], [], token, 0)
        upper.append([li, sems_u, arrs_u])

    def arrive(li, stage, after):
        nonlocal sems_a, arrs_a
        if li == 0 and stage == "mixed":
            sems_a, arrs_a, _ = _gather_call("gather_rest_forward", arrs_a, sems_a, after, 1)
            _, arrs_a, _ = _gather_call("gather_rest_finish", arrs_a, sems_a, after, 2)
            gw[0].update(zip(rest, arrs_a))
        if li + 1 < depth and stage == "ffn":
            u = upper[li]
            u[1], u[2], _ = _gather_call("gather_upper_forward", u[2], u[1], after, 1)
        if li + 1 < depth and stage == "end":
            u = upper[li]
            _, u[2], _ = _gather_call("gather_upper_finish", u[2], u[1], after, 2)
            gw[u[0]].update(zip(BIG, u[2]))

    small = {n: weights[n] for n in REPLICATED}
    small["norm1_g"] = small["norm1_g"] + token[0, 0]
    sflat = _pad_rows(jnp.concatenate([weights[n].reshape(-1) for n in SMALL_SHARDED]), 8)
    sgath = _all_gather_devices(sflat)[0::2].reshape(4, -1)
    off = 0
    for n in SMALL_SHARDED:
        shp = weights[n].shape
        part = sgath[:, off:off + weights[n].size].reshape((4,) + shp)
        small[n] = jnp.moveaxis(part, 0, -2).reshape(shp[:-1] + (4 * shp[-1],))
        off += weights[n].size

    pending = []

    def advance(g, after):
        if g["stage"] == 0:
            gs, others = _copies_wait("swap_wait_" + g["tag"], "swap", *g["handle"], after)
            g["handle"] = _copies_start("exchange_start_" + g["tag"], "exchange", _add_my_halves(gs, others, c_idx))
            g["stage"] = 1
            return g["handle"][3]
        return None

    def produced(li, stage, grads, after):
        token = None
        for g in pending:
            token = advance(g, after) if g["stage"] == 0 else token
        if grads:
            names = [n for n in BIG if n in grads]
            handle = _copies_start("swap_start_%d%s" % (li, stage), "swap", [grads[n] for n in names], token)
            pending.append(dict(li=li, names=names, tag="%d%s" % (li, stage), stage=0, handle=handle[:3]))
            token = handle[3]
        return token

    loss_local, dx, _, small_grads = _local_step(x[0], p[:, 0], loss_target[0], gw, small, produced, arrive)
    accs = {}
    for g in pending:
        advance(g, dx)
    for g in pending:
        pairs, recvs = _copies_wait("exchange_wait_" + g["tag"], "exchange", *g["handle"][:3], dx)
        have = [accs.get(n) for n in g["names"]]
        accs.update(zip(g["names"], _sum_into(pairs, recvs, idx, g["li"], depth, have)))
    gshard = dict(zip(BIG, _sibling_share([accs[n] for n in BIG])))


    rnames = REPLICATED + SMALL_SHARDED
    rflat = _pad_rows(jnp.concatenate([small_grads[n].reshape(-1) for n in rnames]), 8)
    rsum = _sum_slots(_all_gather_devices(rflat)).reshape(-1)
    off = 0
    for n in rnames:
        whole = rsum[off:off + small_grads[n].size].reshape(small_grads[n].shape)
        off += small_grads[n].size
        if n in SMALL_SHARDED:
            cols = weights[n].shape[-1]
            whole = lax.dynamic_slice_in_dim(whole, chip * cols, cols, axis=whole.ndim - 1)
        gshard[n] = whole

    loss = lax.psum(loss_local, ("x", "y", "c"))
    deltas, new_m, new_v, grad_out = {}, {}, {}, {}
    for n in ALL_WEIGHTS:
        deltas[n], new_m[n], new_v[n], grad_out[n] = _adamw(weights[n], gshard[n], mom_m[n], mom_v[n])
    return (loss, dx[None], *[grad_out[n] for n in ALL_WEIGHTS], *[deltas[n] for n in ALL_WEIGHTS],
            *[new_m[n] for n in ALL_WEIGHTS], *[new_v[n] for n in ALL_WEIGHTS])
```

```python
import jax
import jax.numpy as jnp
from jax import lax
from jax.experimental import pallas as pl
from jax.experimental.pallas import tpu as pltpu

F32 = jnp.float32
MM_DTYPE = jnp.bfloat16
WIRE_DTYPE = jnp.bfloat16
HI = lax.Precision.HIGHEST
EPS = 1e-6
HEAD_DIM = 128
CHUNK = 64
QKV_CONV_WIDTH = 4
SCONV_WIDTH = 3
POOL_GROUPS = 4
LANES = 128
SUBLANES_WIRE = 16
VMEM_LIMIT_BYTES = 56 * 1024 * 1024
ADAM_LR, ADAM_B1, ADAM_B2, ADAM_EPS, ADAM_WD, ADAM_STEP = 0.001, 0.9, 0.999, 1e-08, 0.01, 10
MESH = pl.DeviceIdType.MESH
ANY = pl.BlockSpec(memory_space=pl.ANY)
HBM = pl.BlockSpec(memory_space=pltpu.HBM)
SEM = pl.BlockSpec(memory_space=pltpu.SEMAPHORE)


def _params(*sem):
    return pltpu.CompilerParams(vmem_limit_bytes=VMEM_LIMIT_BYTES, dimension_semantics=sem if sem else None)


def _mm(a, b):
    return jnp.dot(a.astype(MM_DTYPE), b.astype(MM_DTYPE), preferred_element_type=F32)


def _mm_nt(a, b):
    return lax.dot_general(a.astype(MM_DTYPE), b.astype(MM_DTYPE), (((1,), (1,)), ((), ())), preferred_element_type=F32)


def _mm_tn(a, b):
    return lax.dot_general(a.astype(MM_DTYPE), b.astype(MM_DTYPE), (((0,), (0,)), ((), ())), preferred_element_type=F32)


def _hmm(a, b):
    return jnp.dot(a, b, preferred_element_type=F32, precision=HI)


def _hmm_nt(a, b):
    return lax.dot_general(a, b, (((1,), (1,)), ((), ())), preferred_element_type=F32, precision=HI)


def _hmm_tn(a, b):
    return lax.dot_general(a, b, (((0,), (0,)), ((), ())), preferred_element_type=F32, precision=HI)


def _sigmoid(x):
    return 1.0 / (1.0 + jnp.exp(-x))


def _dsilu(x, s):
    return s * (1.0 + x * (1.0 - s))


def _rows(shape):
    return lax.broadcasted_iota(jnp.int32, shape, 0)


def _shift_down(x, s):
    if s == 0:
        return x
    return jnp.where(_rows(x.shape) >= s, pltpu.roll(x, s, 0), 0.0)


def _shift_up(x, s):
    if s == 0:
        return x
    t = x.shape[0]
    return jnp.where(_rows(x.shape) < t - s, pltpu.roll(x, t - s, 0), 0.0)


def _rms_fwd(x):
    r = lax.rsqrt(jnp.mean(x * x, axis=-1, keepdims=True) + EPS)
    return x * r, r


def _rms_bwd(dxn, xn, r):
    return r * (dxn - xn * jnp.mean(dxn * xn, axis=-1, keepdims=True))


def _tile_rows(n, cap, mult=8):
    best = None
    for d in range(mult, min(n, cap) + 1, mult):
        if n % d == 0:
            best = d
    return best if best is not None else n


def _in_proj_fwd(x, g1, wp, segs, tm):
    t, d = x.shape
    npk = wp.shape[1]

    def body(x_ref, g_ref, w_ref, *o_refs):
        xn, _ = _rms_fwd(x_ref[...])
        h = (xn * g_ref[...]).astype(w_ref.dtype)
        off = 0
        for o_ref, wd in zip(o_refs, segs):
            o_ref[...] = jnp.dot(h, w_ref[:, off:off + wd], preferred_element_type=F32)
            off += wd

    return pl.pallas_call(
        body, name="in_proj_fwd", grid=(t // tm,),
        in_specs=[pl.BlockSpec((tm, d), lambda i: (i, 0)), pl.BlockSpec((1, d), lambda i: (0, 0)),
                  pl.BlockSpec((d, npk), lambda i: (0, 0))],
        out_specs=[pl.BlockSpec((tm, wd), lambda i: (i, 0)) for wd in segs],
        out_shape=[jax.ShapeDtypeStruct((t, wd), F32) for wd in segs],
        compiler_params=_params("arbitrary"))(x, g1, wp)


def _in_proj_bwd(x, g1, wp, dsegs, dx_res, segs, tm):
    t, d = x.shape
    npk = wp.shape[1]
    nseg = len(segs)

    def body(x_ref, g_ref, w_ref, *rest):
        ds_refs = rest[:nseg]
        dxr_ref, dx_ref, dw_ref, dg_ref = rest[nseg:]
        i = pl.program_id(0)

        @pl.when(i == 0)
        def _():
            dw_ref[...] = jnp.zeros_like(dw_ref)
            dg_ref[...] = jnp.zeros_like(dg_ref)

        xn, r = _rms_fwd(x_ref[...])
        g = g_ref[...]
        h = (xn * g).astype(w_ref.dtype)
        dh = jnp.zeros((tm, d), F32)
        off = 0
        for ds_ref, wd in zip(ds_refs, segs):
            dsv = ds_ref[...].astype(w_ref.dtype)
            dh = dh + lax.dot_general(dsv, w_ref[:, off:off + wd], (((1,), (1,)), ((), ())), preferred_element_type=F32)
            dw_ref[:, off:off + wd] += lax.dot_general(h, dsv, (((0,), (0,)), ((), ())), preferred_element_type=F32)
            off += wd
        dg_ref[...] += jnp.sum(dh * xn, axis=0, keepdims=True)
        dx_ref[...] = dxr_ref[...] + _rms_bwd(dh * g, xn, r)

    return pl.pallas_call(
        body, name="in_proj_bwd", grid=(t // tm,),
        in_specs=[pl.BlockSpec((tm, d), lambda i: (i, 0)), pl.BlockSpec((1, d), lambda i: (0, 0)),
                  pl.BlockSpec((d, npk), lambda i: (0, 0))]
                 + [pl.BlockSpec((tm, wd), lambda i: (i, 0)) for wd in segs]
                 + [pl.BlockSpec((tm, d), lambda i: (i, 0))],
        out_specs=[pl.BlockSpec((tm, d), lambda i: (i, 0)), pl.BlockSpec((d, npk), lambda i: (0, 0)),
                   pl.BlockSpec((1, d), lambda i: (0, 0))],
        out_shape=[jax.ShapeDtypeStruct((t, d), F32), jax.ShapeDtypeStruct((d, npk), F32),
                   jax.ShapeDtypeStruct((1, d), F32)],
        compiler_params=_params("arbitrary"))(x, g1, wp, *dsegs, dx_res)


def _out_proj_fwd(x0, mix, wo, g2, tm):
    t, d = x0.shape
    dq = wo.shape[1]
    widths = [m.shape[1] for m in mix]

    def body(x_ref, *rest):
        m_refs = rest[:len(mix)]
        w_ref, g_ref, x1_ref, h2_ref = rest[len(mix):]
        acc = x_ref[...]
        off = 0
        for m_ref, wd in zip(m_refs, widths):
            for k in range(wd // dq):
                acc = acc + jnp.dot(m_ref[:, k * dq:(k + 1) * dq].astype(w_ref.dtype), w_ref[off // dq + k],
                                    preferred_element_type=F32)
            off += wd
        x1_ref[...] = acc
        xn, _ = _rms_fwd(acc)
        h2_ref[...] = (xn * g_ref[...]).astype(h2_ref.dtype)

    return pl.pallas_call(
        body, name="out_proj_fwd", grid=(t // tm,),
        in_specs=[pl.BlockSpec((tm, d), lambda i: (i, 0))]
                 + [pl.BlockSpec((tm, wd), lambda i: (i, 0)) for wd in widths]
                 + [pl.BlockSpec((4, dq, d), lambda i: (0, 0, 0)), pl.BlockSpec((1, d), lambda i: (0, 0))],
        out_specs=[pl.BlockSpec((tm, d), lambda i: (i, 0)), pl.BlockSpec((tm, d), lambda i: (i, 0))],
        out_shape=[jax.ShapeDtypeStruct((t, d), F32), jax.ShapeDtypeStruct((t, d), MM_DTYPE)],
        compiler_params=_params("arbitrary"))(x0, *mix, wo, g2)


def _out_proj_bwd(dx2, dh2, x1, g2, mix, wo, tm):
    t, d = x1.shape
    dq = wo.shape[1]
    widths = [m.shape[1] for m in mix]
    nm = len(mix)

    def body(dx2_ref, dh2_ref, x1_ref, g_ref, *rest):
        m_refs = rest[:nm]
        w_ref = rest[nm]
        dx1_ref = rest[nm + 1]
        dm_refs = rest[nm + 2:nm + 2 + nm]
        dw_ref, dg_ref = rest[nm + 2 + nm:]
        i = pl.program_id(0)

        @pl.when(i == 0)
        def _():
            dw_ref[...] = jnp.zeros_like(dw_ref)
            dg_ref[...] = jnp.zeros_like(dg_ref)

        xn, r = _rms_fwd(x1_ref[...])
        dh2v = dh2_ref[...]
        dg_ref[...] += jnp.sum(dh2v * xn, axis=0, keepdims=True)
        dx1 = dx2_ref[...] + _rms_bwd(dh2v * g_ref[...], xn, r)
        dx1_ref[...] = dx1
        dx1c = dx1.astype(w_ref.dtype)
        off = 0
        for m_ref, dm_ref, wd in zip(m_refs, dm_refs, widths):
            for k in range(wd // dq):
                j = off // dq + k
                cols = slice(k * dq, (k + 1) * dq)
                dm_ref[:, cols] = lax.dot_general(dx1c, w_ref[j], (((1,), (1,)), ((), ())), preferred_element_type=F32)
                dw_ref[j] += lax.dot_general(m_ref[:, cols].astype(w_ref.dtype), dx1c, (((0,), (0,)), ((), ())),
                                             preferred_element_type=F32)
            off += wd

    tile = lambda wd: pl.BlockSpec((tm, wd), lambda i: (i, 0))
    return pl.pallas_call(
        body, name="out_proj_bwd", grid=(t // tm,),
        in_specs=[tile(d), tile(d), tile(d), pl.BlockSpec((1, d), lambda i: (0, 0))]
                 + [tile(wd) for wd in widths] + [pl.BlockSpec((4, dq, d), lambda i: (0, 0, 0))],
        out_specs=[tile(d)] + [tile(wd) for wd in widths]
                  + [pl.BlockSpec((4, dq, d), lambda i: (0, 0, 0)), pl.BlockSpec((1, d), lambda i: (0, 0))],
        out_shape=[jax.ShapeDtypeStruct((t, d), F32)] + [jax.ShapeDtypeStruct((t, wd), F32) for wd in widths]
                  + [jax.ShapeDtypeStruct((4, dq, d), F32), jax.ShapeDtypeStruct((1, d), F32)],
        compiler_params=_params("arbitrary"))(dx2, dh2, x1, g2, *mix, wo)


def _ffn_fwd(x1, h2, wg, wu, wd, tm):
    t, d = x1.shape
    fs = wg.shape[2]

    def body(x1_ref, h2_ref, wg_ref, wu_ref, wd_ref, x2_ref, gp_ref, up_ref):
        @pl.when(pl.program_id(1) == 0)
        def _():
            x2_ref[...] = x1_ref[...]

        h = h2_ref[...]
        gp = jnp.dot(h, wg_ref[...], preferred_element_type=F32)
        up = jnp.dot(h, wu_ref[...], preferred_element_type=F32)
        gp_ref[...] = gp
        up_ref[...] = up
        ff = gp * _sigmoid(gp) * up
        x2_ref[...] += jnp.dot(ff.astype(wd_ref.dtype), wd_ref[...], preferred_element_type=F32)

    return pl.pallas_call(
        body, name="ffn_fwd", grid=(t // tm, 4),
        in_specs=[pl.BlockSpec((tm, d), lambda i, j: (i, 0)), pl.BlockSpec((tm, d), lambda i, j: (i, 0)),
                  pl.BlockSpec((None, d, fs), lambda i, j: (j, 0, 0)),
                  pl.BlockSpec((None, d, fs), lambda i, j: (j, 0, 0)),
                  pl.BlockSpec((None, fs, d), lambda i, j: (j, 0, 0))],
        out_specs=[pl.BlockSpec((tm, d), lambda i, j: (i, 0)), pl.BlockSpec((None, tm, fs), lambda i, j: (j, i, 0)),
                   pl.BlockSpec((None, tm, fs), lambda i, j: (j, i, 0))],
        out_shape=[jax.ShapeDtypeStruct((t, d), F32), jax.ShapeDtypeStruct((4, t, fs), F32),
                   jax.ShapeDtypeStruct((4, t, fs), F32)],
        compiler_params=_params("arbitrary", "arbitrary"))(x1, h2, wg, wu, wd)


def _ffn_bwd(dx2, h2, gp, up, wg, wu, wd, tm):
    t, d = dx2.shape
    fs = wg.shape[2]

    def body(dx2_ref, h2_ref, gp_ref, up_ref, wg_ref, wu_ref, wd_ref, dh2_ref, dwg_ref, dwu_ref, dwd_ref):
        j, i = pl.program_id(0), pl.program_id(1)

        @pl.when(i == 0)
        def _():
            dwg_ref[...] = jnp.zeros_like(dwg_ref)
            dwu_ref[...] = jnp.zeros_like(dwu_ref)
            dwd_ref[...] = jnp.zeros_like(dwd_ref)

        cdt = wg_ref.dtype
        h = h2_ref[...]
        gpv, upv = gp_ref[...], up_ref[...]
        s = _sigmoid(gpv)
        silu = gpv * s
        dx2c = dx2_ref[...].astype(cdt)
        dff = lax.dot_general(dx2c, wd_ref[...], (((1,), (1,)), ((), ())), preferred_element_type=F32)
        dwd_ref[...] += lax.dot_general((silu * upv).astype(cdt), dx2c, (((0,), (0,)), ((), ())), preferred_element_type=F32)
        dup = (dff * silu).astype(cdt)
        dgp = (dff * upv * _dsilu(gpv, s)).astype(cdt)
        dwg_ref[...] += lax.dot_general(h, dgp, (((0,), (0,)), ((), ())), preferred_element_type=F32)
        dwu_ref[...] += lax.dot_general(h, dup, (((0,), (0,)), ((), ())), preferred_element_type=F32)
        dh = (lax.dot_general(dgp, wg_ref[...], (((1,), (1,)), ((), ())), preferred_element_type=F32)
              + lax.dot_general(dup, wu_ref[...], (((1,), (1,)), ((), ())), preferred_element_type=F32))
        rows = pl.ds(pl.multiple_of(i * tm, tm), tm)

        @pl.when(j == 0)
        def _():
            dh2_ref[rows, :] = dh

        @pl.when(j != 0)
        def _():
            dh2_ref[rows, :] += dh

    return pl.pallas_call(
        body, name="ffn_bwd", grid=(4, t // tm),
        in_specs=[pl.BlockSpec((tm, d), lambda j, i: (i, 0)), pl.BlockSpec((tm, d), lambda j, i: (i, 0)),
                  pl.BlockSpec((None, tm, fs), lambda j, i: (j, i, 0)), pl.BlockSpec((None, tm, fs), lambda j, i: (j, i, 0)),
                  pl.BlockSpec((None, d, fs), lambda j, i: (j, 0, 0)),
                  pl.BlockSpec((None, d, fs), lambda j, i: (j, 0, 0)),
                  pl.BlockSpec((None, fs, d), lambda j, i: (j, 0, 0))],
        out_specs=[pl.BlockSpec((t, d), lambda j, i: (0, 0)), pl.BlockSpec((None, d, fs), lambda j, i: (j, 0, 0)),
                   pl.BlockSpec((None, d, fs), lambda j, i: (j, 0, 0)), pl.BlockSpec((None, fs, d), lambda j, i: (j, 0, 0))],
        out_shape=[jax.ShapeDtypeStruct((t, d), F32), jax.ShapeDtypeStruct((4, d, fs), F32),
                   jax.ShapeDtypeStruct((4, d, fs), F32), jax.ShapeDtypeStruct((4, fs, d), F32)],
        compiler_params=_params("arbitrary", "arbitrary"))(dx2, h2, gp, up, wg, wu, wd)


def _ple_fwd(x2, p, wpg, wpp, tm):
    t, d = x2.shape
    q = p.shape[1]
    dq = d // 4

    def body(x_ref, p_ref, wg_ref, wp_ref, o_ref):
        xv = x_ref[...]
        xc = xv.astype(wg_ref.dtype)
        pc = p_ref[...].astype(wp_ref.dtype)
        pre = jnp.dot(xc[:, :dq], wg_ref[0], preferred_element_type=F32)
        for j in range(1, 4):
            pre = pre + jnp.dot(xc[:, j * dq:(j + 1) * dq], wg_ref[j], preferred_element_type=F32)
        gate = _sigmoid(pre)
        for j in range(4):
            cols = slice(j * dq, (j + 1) * dq)
            o_ref[:, cols] = xv[:, cols] + gate[:, cols] * jnp.dot(pc, wp_ref[j], preferred_element_type=F32)

    return pl.pallas_call(
        body, name="ple_fwd", grid=(t // tm,),
        in_specs=[pl.BlockSpec((tm, d), lambda i: (i, 0)), pl.BlockSpec((tm, q), lambda i: (i, 0)),
                  pl.BlockSpec((4, dq, d), lambda i: (0, 0, 0)),
                  pl.BlockSpec((4, q, dq), lambda i: (0, 0, 0))],
        out_specs=pl.BlockSpec((tm, d), lambda i: (i, 0)),
        out_shape=jax.ShapeDtypeStruct((t, d), F32),
        compiler_params=_params("arbitrary"))(x2, p, wpg, wpp)


def _ple_bwd(dx3, x2, p, wpg, wpp, tm):
    t, d = x2.shape
    q = p.shape[1]
    dq = d // 4

    def body(dx3_ref, x_ref, p_ref, wg_ref, wp_ref, dx2_ref, dwg_ref, dwp_ref):
        @pl.when(pl.program_id(0) == 0)
        def _():
            dwg_ref[...] = jnp.zeros_like(dwg_ref)
            dwp_ref[...] = jnp.zeros_like(dwp_ref)

        cdt = wg_ref.dtype
        xc = x_ref[...].astype(cdt)
        pc = p_ref[...].astype(cdt)
        pre = jnp.dot(xc[:, :dq], wg_ref[0], preferred_element_type=F32)
        for j in range(1, 4):
            pre = pre + jnp.dot(xc[:, j * dq:(j + 1) * dq], wg_ref[j], preferred_element_type=F32)
        gate = _sigmoid(pre)
        dx3v = dx3_ref[...]
        dpp = (dx3v * gate).astype(cdt)
        dgate = dx3v * gate * (1.0 - gate)
        dpre_parts = []
        for j in range(4):
            cols = slice(j * dq, (j + 1) * dq)
            pp_j = jnp.dot(pc, wp_ref[j], preferred_element_type=F32)
            dpre_parts.append((dgate[:, cols] * pp_j).astype(cdt))
            dwp_ref[j] += lax.dot_general(pc, dpp[:, cols], (((0,), (0,)), ((), ())), preferred_element_type=F32)
        dpre = jnp.concatenate(dpre_parts, axis=1)
        for j in range(4):
            cols = slice(j * dq, (j + 1) * dq)
            dwg_ref[j] += lax.dot_general(xc[:, cols], dpre, (((0,), (0,)), ((), ())), preferred_element_type=F32)
            dx2_ref[:, cols] = dx3v[:, cols] + lax.dot_general(dpre, wg_ref[j], (((1,), (1,)), ((), ())),
                                                               preferred_element_type=F32)

    return pl.pallas_call(
        body, name="ple_bwd", grid=(t // tm,),
        in_specs=[pl.BlockSpec((tm, d), lambda i: (i, 0)), pl.BlockSpec((tm, d), lambda i: (i, 0)),
                  pl.BlockSpec((tm, q), lambda i: (i, 0)), pl.BlockSpec((4, dq, d), lambda i: (0, 0, 0)),
                  pl.BlockSpec((4, q, dq), lambda i: (0, 0, 0))],
        out_specs=[pl.BlockSpec((tm, d), lambda i: (i, 0)), pl.BlockSpec((4, dq, d), lambda i: (0, 0, 0)),
                   pl.BlockSpec((4, q, dq), lambda i: (0, 0, 0))],
        out_shape=[jax.ShapeDtypeStruct((t, d), F32), jax.ShapeDtypeStruct((4, dq, d), F32),
                   jax.ShapeDtypeStruct((4, q, dq), F32)],
        compiler_params=_params("arbitrary"))(dx3, x2, p, wpg, wpp)


def _loss_head(x, target, fg, tm):
    t, d = x.shape

    def body(x_ref, t_ref, g_ref, dx_ref, loss_ref, dg_ref):
        @pl.when(pl.program_id(0) == 0)
        def _():
            loss_ref[...] = jnp.zeros_like(loss_ref)
            dg_ref[...] = jnp.zeros_like(dg_ref)

        xn, r = _rms_fwd(x_ref[...])
        g = g_ref[...]
        err = xn * g - t_ref[...]
        loss_ref[...] += 0.5 * jnp.sum(jnp.sum(err * err, axis=-1, keepdims=True) / d, axis=0, keepdims=True)
        dy = err / d
        dg_ref[...] += jnp.sum(dy * xn, axis=0, keepdims=True)
        dx_ref[...] = _rms_bwd(dy * g, xn, r)

    return pl.pallas_call(
        body, name="loss_head", grid=(t // tm,),
        in_specs=[pl.BlockSpec((tm, d), lambda i: (i, 0)), pl.BlockSpec((tm, d), lambda i: (i, 0)),
                  pl.BlockSpec((1, d), lambda i: (0, 0))],
        out_specs=[pl.BlockSpec((tm, d), lambda i: (i, 0)), pl.BlockSpec((1, 1), lambda i: (0, 0)),
                   pl.BlockSpec((1, d), lambda i: (0, 0))],
        out_shape=[jax.ShapeDtypeStruct((t, d), F32), jax.ShapeDtypeStruct((1, 1), F32),
                   jax.ShapeDtypeStruct((1, d), F32)],
        compiler_params=_params("arbitrary"))(x, target, fg)


def _qkv_conv_act(xv, w, j, heads):
    k = QKV_CONV_WIDTH
    y = w[k - 1:k] * xv
    for s in range(1, k):
        y = y + w[k - 1 - s:k - s] * _shift_down(xv, s)
    sg = _sigmoid(y)
    s_act = y * sg
    nrm = lax.rsqrt(jnp.sum(s_act * s_act, axis=-1, keepdims=True) + EPS)
    scale = jnp.where(j < heads, HEAD_DIM ** -0.5, 1.0).astype(F32)
    return y, sg, s_act, nrm, scale


def _qkv_conv_fwd(qkv_pre, conv_w, heads):
    t = qkv_pre.shape[0]
    nblk = 3 * heads

    def body(x_ref, w_ref, o_ref):
        j = pl.program_id(0)
        _, _, s_act, nrm, scale = _qkv_conv_act(x_ref[...], w_ref[...], j, heads)
        o_ref[...] = jnp.where(j < 2 * heads, s_act * (nrm * scale), s_act)

    return pl.pallas_call(
        body, name="qkv_conv_fwd", grid=(nblk,),
        in_specs=[pl.BlockSpec((t, LANES), lambda j: (0, j)), pl.BlockSpec((QKV_CONV_WIDTH, LANES), lambda j: (0, j))],
        out_specs=pl.BlockSpec((t, LANES), lambda j: (0, j)),
        out_shape=jax.ShapeDtypeStruct(qkv_pre.shape, F32),
        compiler_params=_params("arbitrary"))(qkv_pre, conv_w)


def _qkv_conv_bwd(qkv_pre, conv_w, dqkv, heads):
    t = qkv_pre.shape[0]
    nblk = 3 * heads
    k = QKV_CONV_WIDTH

    def body(x_ref, w_ref, dn_ref, dx_ref, dw_ref):
        j = pl.program_id(0)
        xv, w = x_ref[...], w_ref[...]
        y, sg, s_act, nrm, scale = _qkv_conv_act(xv, w, j, heads)
        dn = dn_ref[...]
        dsn = dn * scale
        ds_qk = nrm * dsn - s_act * (nrm * nrm * nrm) * jnp.sum(dsn * s_act, axis=-1, keepdims=True)
        ds = jnp.where(j < 2 * heads, ds_qk, dn)
        dy = ds * _dsilu(y, sg)
        dx = w[k - 1:k] * dy
        dw_ref[k - 1:k, :] = jnp.sum(dy * xv, axis=0, keepdims=True)
        for s in range(1, k):
            dx = dx + w[k - 1 - s:k - s] * _shift_up(dy, s)
            dw_ref[k - 1 - s:k - s, :] = jnp.sum(dy * _shift_down(xv, s), axis=0, keepdims=True)
        dx_ref[...] = dx

    return pl.pallas_call(
        body, name="qkv_conv_bwd", grid=(nblk,),
        in_specs=[pl.BlockSpec((t, LANES), lambda j: (0, j)), pl.BlockSpec((k, LANES), lambda j: (0, j)),
                  pl.BlockSpec((t, LANES), lambda j: (0, j))],
        out_specs=[pl.BlockSpec((t, LANES), lambda j: (0, j)), pl.BlockSpec((k, LANES), lambda j: (0, j))],
        out_shape=[jax.ShapeDtypeStruct(qkv_pre.shape, F32), jax.ShapeDtypeStruct(conv_w.shape, F32)],
        compiler_params=_params("arbitrary"))(qkv_pre, conv_w, dqkv)


def _pool_windows(shape, j, group_dim):
    lane = lax.broadcasted_iota(jnp.int32, shape, 1) + j * LANES
    grp = lane // group_dim
    win = jnp.left_shift(2, grp).astype(F32)
    cnt = jnp.minimum((_rows(shape) + 1).astype(F32), win)
    return grp, cnt


def _pool_select(grp, levels):
    out = levels[0]
    for gi in range(1, POOL_GROUPS):
        out = jnp.where(grp == gi, levels[gi], out)
    return out


def _pool_mean(hv, grp, cnt):
    acc, levels, width = hv, [], 1
    for _ in range(POOL_GROUPS):
        acc = acc + _shift_down(acc, width)
        width *= 2
        levels.append(acc)
    return _pool_select(grp, levels) / cnt - hv


def _pool_fwd(hp, wbd, scale, group_dim):
    t, dp = hp.shape

    def body(h_ref, w_ref, s_ref, o_ref):
        hv = h_ref[...]
        grp, cnt = _pool_windows(hv.shape, pl.program_id(0), group_dim)
        pooled = _pool_mean(hv, grp, cnt)
        o_ref[...] = _mm(pooled, w_ref[...]) * s_ref[...]

    return pl.pallas_call(
        body, name="pool_fwd", grid=(dp // LANES,),
        in_specs=[pl.BlockSpec((t, LANES), lambda j: (0, j)), pl.BlockSpec((LANES, LANES), lambda j: (j, j)),
                  pl.BlockSpec((1, LANES), lambda j: (0, j))],
        out_specs=pl.BlockSpec((t, LANES), lambda j: (0, j)),
        out_shape=jax.ShapeDtypeStruct(hp.shape, F32),
        compiler_params=_params("arbitrary"))(hp, wbd, scale)


def _pool_bwd(hp, wbd, scale, dob, group_dim):
    t, dp = hp.shape

    def body(h_ref, w_ref, s_ref, do_ref, dh_ref, dw_ref, ds_ref):
        hv = h_ref[...]
        grp, cnt = _pool_windows(hv.shape, pl.program_id(0), group_dim)
        pooled = _pool_mean(hv, grp, cnt)
        wv = w_ref[...]
        dov = do_ref[...]
        ds_ref[...] = jnp.sum(dov * _mm(pooled, wv), axis=0, keepdims=True)
        dys = dov * s_ref[...]
        dw_ref[0] = _mm_tn(pooled, dys)
        dpooled = _mm_nt(dys, wv)
        acc, levels, width = dpooled / cnt, [], 1
        for _ in range(POOL_GROUPS):
            acc = acc + _shift_up(acc, width)
            width *= 2
            levels.append(acc)
        dh_ref[...] = _pool_select(grp, levels) - dpooled

    nb = dp // LANES
    return pl.pallas_call(
        body, name="pool_bwd", grid=(nb,),
        in_specs=[pl.BlockSpec((t, LANES), lambda j: (0, j)), pl.BlockSpec((LANES, LANES), lambda j: (j, j)),
                  pl.BlockSpec((1, LANES), lambda j: (0, j)), pl.BlockSpec((t, LANES), lambda j: (0, j))],
        out_specs=[pl.BlockSpec((t, LANES), lambda j: (0, j)), pl.BlockSpec((1, LANES, LANES), lambda j: (j, 0, 0)),
                   pl.BlockSpec((1, LANES), lambda j: (0, j))],
        out_shape=[jax.ShapeDtypeStruct(hp.shape, F32), jax.ShapeDtypeStruct((nb, LANES, LANES), F32),
                   jax.ShapeDtypeStruct((1, dp), F32)],
        compiler_params=_params("arbitrary"))(hp, wbd, scale, dob)


def _sconv_fwd(cbcch, w):
    t, dc3 = cbcch.shape
    nb = dc3 // 3 // LANES
    k = SCONV_WIDTH

    def body(b_ref, c_ref, h_ref, w_ref, o_ref):
        m = c_ref[...] * h_ref[...]
        wv = w_ref[...]
        y = wv[k - 1:k] * m
        for s in range(1, k):
            y = y + wv[k - 1 - s:k - s] * _shift_down(m, s)
        o_ref[...] = b_ref[...] * y

    return pl.pallas_call(
        body, name="sconv_fwd", grid=(nb,),
        in_specs=[pl.BlockSpec((t, LANES), lambda j: (0, j)), pl.BlockSpec((t, LANES), lambda j: (0, nb + j)),
                  pl.BlockSpec((t, LANES), lambda j: (0, 2 * nb + j)), pl.BlockSpec((k, LANES), lambda j: (0, j))],
        out_specs=pl.BlockSpec((t, LANES), lambda j: (0, j)),
        out_shape=jax.ShapeDtypeStruct((t, dc3 // 3), F32),
        compiler_params=_params("arbitrary"))(cbcch, cbcch, cbcch, w)


def _sconv_bwd(cbcch, w, doc):
    t, dc3 = cbcch.shape
    nb = dc3 // 3 // LANES
    k = SCONV_WIDTH

    def body(b_ref, c_ref, h_ref, w_ref, do_ref, db_ref, dc_ref, dh_ref, dw_ref):
        cv, hv = c_ref[...], h_ref[...]
        m = cv * hv
        wv = w_ref[...]
        dov = do_ref[...]
        dy = dov * b_ref[...]
        y = wv[k - 1:k] * m
        dm = wv[k - 1:k] * dy
        dw_ref[k - 1:k, :] = jnp.sum(dy * m, axis=0, keepdims=True)
        for s in range(1, k):
            ms = _shift_down(m, s)
            y = y + wv[k - 1 - s:k - s] * ms
            dm = dm + wv[k - 1 - s:k - s] * _shift_up(dy, s)
            dw_ref[k - 1 - s:k - s, :] = jnp.sum(dy * ms, axis=0, keepdims=True)
        db_ref[...] = dov * y
        dc_ref[...] = dm * hv
        dh_ref[...] = dm * cv

    col = lambda o: pl.BlockSpec((t, LANES), lambda j: (0, o * nb + j))
    return pl.pallas_call(
        body, name="sconv_bwd", grid=(nb,),
        in_specs=[col(0), col(1), col(2), pl.BlockSpec((k, LANES), lambda j: (0, j)), col(0)],
        out_specs=[col(0), col(0), col(0), pl.BlockSpec((k, LANES), lambda j: (0, j))],
        out_shape=[jax.ShapeDtypeStruct((t, dc3 // 3), F32)] * 3 + [jax.ShapeDtypeStruct(w.shape, F32)],
        compiler_params=_params("arbitrary"))(cbcch, cbcch, cbcch, w, doc)


class _Split:
    def __init__(self, a):
        self.hi = a.astype(jnp.bfloat16)
        self.lo = (a - self.hi.astype(F32)).astype(jnp.bfloat16)


def _per_head(dims, a, b):
    a = a if isinstance(a, _Split) else _Split(a)
    b = b if isinstance(b, _Split) else _Split(b)

    def dot(x, y):
        return lax.dot_general(x, y, (dims, ((), ())), preferred_element_type=F32)

    return jnp.stack([dot(a.hi[h], b.hi[h]) + (dot(a.hi[h], b.lo[h]) + dot(a.lo[h], b.hi[h])) for h in range(a.hi.shape[0])])


def _bmm(a, b):
    return _per_head(((1,), (0,)), a, b)


def _bmm_nt(a, b):
    return _per_head(((1,), (1,)), a, b)


def _bmm_tn(a, b):
    return _per_head(((0,), (0,)), a, b)


def _inv_unit_lower(low):
    c = low.shape[-1]
    eye = (_rows((c, c)) == lax.broadcasted_iota(jnp.int32, (c, c), 1)).astype(F32)
    pw = -low
    inv = eye + pw
    span = 2
    while span < c:
        pws = _Split(pw)
        pw = _bmm(pws, pws)
        inv = inv + _bmm(inv, pw)
        span *= 2
    return inv


def _heads_of(ref, base, heads):
    return jnp.stack([ref[:, base + h * HEAD_DIM:base + (h + 1) * HEAD_DIM] for h in range(heads)])


def _chunk_common(q, k, v, a_col, b_col, alog, dtb, kept=None):
    hn, c, _ = q.shape
    beta = _sigmoid(b_col)
    xg = a_col + dtb
    softplus = jnp.maximum(xg, 0.0) + jnp.log(1.0 + jnp.exp(-jnp.abs(xg)))
    neg_ea = -jnp.exp(alog)
    g = neg_ea * softplus
    ri = _rows((c, c))
    ci = lax.broadcasted_iota(jnp.int32, (c, c), 1)
    incl, strict = ri >= ci, ri > ci
    inclf = jnp.broadcast_to(incl.astype(F32), (hn, c, c))
    gcb = _bmm(inclf, jnp.broadcast_to(g, (hn, c, HEAD_DIM)))
    gc_row = jnp.sum(jnp.where(ri <= ci, jnp.broadcast_to(g, (hn, c, c)), 0.0), axis=1, keepdims=True)
    dmat = jnp.where(incl, jnp.exp(jnp.where(incl, gcb[:, :, :1] - gc_row, 0.0)), 0.0)
    eg = jnp.exp(gcb)
    gl = gcb[:, c - 1:c, :]
    egl = jnp.exp(gl)
    edl = jnp.exp(gl - gcb)
    kb, vb = k * beta, v * beta
    kbe = kb * eg
    if kept is None:
        ks = _Split(k)
        a0 = _bmm_nt(kb, ks)
        tm = _inv_unit_lower(jnp.where(strict, a0 * dmat, 0.0))
        p0 = _bmm_nt(q, ks)
        tms = _Split(tm)
        u, w = _bmm(tms, vb), _bmm(tms, kbe)
    else:
        (a0, tm, p0, w), u = kept, None
    return dict(beta=beta, xg=xg, neg_ea=neg_ea, g=g, incl=incl, strict=strict, inclf=inclf, dmat=dmat, eg=eg,
                egl=egl, edl=edl, kb=kb, vb=vb, a0=a0, tm=tm, kbe=kbe, u=u, w=w, p0=p0,
                attn=p0 * dmat, qe=q * eg, kd=k * edl)


def _chunk_step(cm, state):
    ss = _Split(state)
    vn = cm["u"] - _bmm(cm["w"], ss)
    vns = _Split(vn)
    o = _bmm(cm["qe"], ss) + _bmm(cm["attn"], vns)
    new_state = state * cm["egl"][:, :, :1] + _bmm_tn(cm["kd"], vns)
    return vn, o, new_state


def _gated_norm(o, zv, og):
    xo, ro = _rms_fwd(o)
    sgz = _sigmoid(zv)
    return xo, ro, sgz, xo * og * (zv * sgz)


def _gate_columns(abv, gpv, heads):
    a_col = jnp.stack([abv[:, h:h + 1] for h in range(heads)])
    b_col = jnp.stack([abv[:, heads + h:heads + h + 1] for h in range(heads)])
    alog = jnp.stack([gpv[0:1, h:h + 1] for h in range(heads)])
    dtb = jnp.stack([gpv[1:2, h:h + 1] for h in range(heads)])
    return a_col, b_col, alog, dtb


def _delta_fwd(qkv, z, ab, gpar, heads):
    t = qkv.shape[0]
    da = heads * HEAD_DIM
    n = t // CHUNK

    def body(qkv_ref, z_ref, ab_ref, gp_ref, oa_ref, st_ref, kc_ref, kw_ref, s_ref):
        @pl.when(pl.program_id(0) == 0)
        def _():
            s_ref[...] = jnp.zeros_like(s_ref)

        gpv = gp_ref[...]
        cm = _chunk_common(_heads_of(qkv_ref, 0, heads), _heads_of(qkv_ref, da, heads), _heads_of(qkv_ref, 2 * da, heads),
                           *_gate_columns(ab_ref[...], gpv, heads))
        state = s_ref[...]
        st_ref[0] = state
        vn, o, new_state = _chunk_step(cm, state)
        s_ref[...] = new_state
        for slot, val in enumerate((cm["a0"], cm["tm"], cm["p0"])):
            kc_ref[0, slot] = val
        for slot, val in enumerate((cm["w"], vn, o)):
            kw_ref[0, slot] = val
        oa = _gated_norm(o, _heads_of(z_ref, 0, heads), gpv[2:3, :])[3]
        for h in range(heads):
            oa_ref[:, h * HEAD_DIM:(h + 1) * HEAD_DIM] = oa[h]

    return pl.pallas_call(
        body, name="delta_fwd", grid=(n,),
        in_specs=[pl.BlockSpec((CHUNK, 3 * da), lambda i: (i, 0)), pl.BlockSpec((CHUNK, da), lambda i: (i, 0)),
                  pl.BlockSpec((CHUNK, LANES), lambda i: (i, 0)), pl.BlockSpec((8, LANES), lambda i: (0, 0))],
        out_specs=[pl.BlockSpec((CHUNK, da), lambda i: (i, 0)),
                   pl.BlockSpec((1, heads, HEAD_DIM, HEAD_DIM), lambda i: (i, 0, 0, 0)),
                   pl.BlockSpec((1, 3, heads, CHUNK, CHUNK), lambda i: (i, 0, 0, 0, 0)),
                   pl.BlockSpec((1, 3, heads, CHUNK, HEAD_DIM), lambda i: (i, 0, 0, 0, 0))],
        out_shape=[jax.ShapeDtypeStruct((t, da), F32), jax.ShapeDtypeStruct((n, heads, HEAD_DIM, HEAD_DIM), F32),
                   jax.ShapeDtypeStruct((n, 3, heads, CHUNK, CHUNK), F32),
                   jax.ShapeDtypeStruct((n, 3, heads, CHUNK, HEAD_DIM), F32)],
        scratch_shapes=[pltpu.VMEM((heads, HEAD_DIM, HEAD_DIM), F32)],
        compiler_params=_params("arbitrary"))(qkv, z, ab, gpar)


def _delta_bwd(qkv, z, ab, gpar, states, kept_c, kept_w, doa, heads):
    t = qkv.shape[0]
    da = heads * HEAD_DIM
    n = t // CHUNK
    c = CHUNK

    def body(qkv_ref, z_ref, ab_ref, gp_ref, st_ref, kc_ref, kw_ref, doa_ref, dqkv_ref, dz_ref, dab_ref, dpar_ref, ds_ref):
        @pl.when(pl.program_id(0) == 0)
        def _():
            ds_ref[...] = jnp.zeros_like(ds_ref)
            dpar_ref[...] = jnp.zeros_like(dpar_ref)

        gpv = gp_ref[...]
        og = gpv[2:3, :]
        q, k, v = _heads_of(qkv_ref, 0, heads), _heads_of(qkv_ref, da, heads), _heads_of(qkv_ref, 2 * da, heads)
        cm = _chunk_common(q, k, v, *_gate_columns(ab_ref[...], gpv, heads),
                           kept=(kc_ref[0, 0], kc_ref[0, 1], kc_ref[0, 2], kw_ref[0, 0]))
        state = st_ref[0]
        dsp = ds_ref[...]
        vn, o = kw_ref[0, 1], kw_ref[0, 2]
        zv = _heads_of(z_ref, 0, heads)
        xo, ro, sgz, _ = _gated_norm(o, zv, og)
        doav = _heads_of(doa_ref, 0, heads)
        don = doav * (zv * sgz)
        dz = doav * (xo * og) * _dsilu(zv, sgz)
        d_og = jnp.sum(jnp.sum(don * xo, axis=1, keepdims=True), axis=0)
        do = _rms_bwd(don * og, xo, ro)
        tm, dmat, eg, edl, egl = cm["tm"], cm["dmat"], cm["eg"], cm["edl"], cm["egl"]
        dos, dsps, sts, tms, ks = _Split(do), _Split(dsp), _Split(state), _Split(tm), _Split(k)
        dvn = _bmm_tn(cm["attn"], dos) + _bmm(cm["kd"], dsps)
        dvns = _Split(dvn)
        dqe = _bmm_nt(dos, sts)
        ds_ref[...] = _bmm_tn(cm["qe"], dos) + dsp * egl[:, :, :1] - _bmm_tn(cm["w"], dvns)
        dattn = _bmm_nt(dos, vn)
        dkd = _bmm_nt(vn, dsps)
        dkd_kd = jnp.sum(dkd * cm["kd"], axis=-1, keepdims=True)
        dgl = (jnp.sum(jnp.sum(dsp * state, axis=-1, keepdims=True), axis=1, keepdims=True) * egl[:, :, :1]
               + jnp.sum(dkd_kd, axis=1, keepdims=True))
        dgc = jnp.sum(dqe * cm["qe"], axis=-1, keepdims=True) - dkd_kd
        dk = dkd * edl
        dq = dqe * eg
        dw = -_bmm_nt(dvns, sts)
        dws = _Split(dw)
        dp0 = dattn * dmat
        dd = jnp.where(cm["incl"], dattn * cm["p0"], 0.0)
        dp0s = _Split(dp0)
        dq = dq + _bmm(dp0s, ks)
        dk = dk + _bmm_tn(dp0s, q)
        dtm = _bmm_nt(dvns, cm["vb"]) + _bmm_nt(dws, cm["kbe"])
        dvb = _bmm_tn(tms, dvns)
        dkbe = _bmm_tn(tms, dws)
        dkb = dkbe * eg
        dgc = dgc + jnp.sum(dkbe * cm["kbe"], axis=-1, keepdims=True)
        dlow = jnp.where(cm["strict"], -_bmm_tn(tms, _bmm_nt(dtm, tms)), 0.0)
        dd = dd + dlow * cm["a0"]
        da0 = dlow * dmat
        da0s = _Split(da0)
        dkb = dkb + _bmm(da0s, ks)
        dk = dk + _bmm_tn(da0s, cm["kb"])
        ddd = dd * dmat
        ones = jnp.ones((heads, c, HEAD_DIM), F32)
        dgc = dgc + jnp.sum(ddd, axis=-1, keepdims=True) - _bmm_tn(ddd, ones)[:, :, :1]
        dgc = dgc + jnp.where(_rows((c, 1)) == c - 1, dgl, 0.0)
        dg = _bmm_tn(cm["inclf"], jnp.broadcast_to(dgc, (heads, c, HEAD_DIM)))[:, :, :1]
        beta = cm["beta"]
        dk = dk + dkb * beta
        dbeta = jnp.sum(dkb * k, axis=-1, keepdims=True) + jnp.sum(dvb * v, axis=-1, keepdims=True)
        dv = dvb * beta
        db_col = dbeta * beta * (1.0 - beta)
        da_col = dg * cm["neg_ea"] * _sigmoid(cm["xg"])
        d_alog = jnp.sum(dg * cm["g"], axis=1, keepdims=True)
        d_dtb = jnp.sum(da_col, axis=1, keepdims=True)
        lane = lax.broadcasted_iota(jnp.int32, (c, LANES), 1)
        lane8 = lax.broadcasted_iota(jnp.int32, (8, LANES), 1)
        row8 = _rows((8, LANES))
        dab = jnp.zeros((c, LANES), F32)
        dpar = jnp.where(row8 == 2, d_og, 0.0)
        for h in range(heads):
            lo = h * HEAD_DIM
            dqkv_ref[:, lo:lo + HEAD_DIM] = dq[h]
            dqkv_ref[:, da + lo:da + lo + HEAD_DIM] = dk[h]
            dqkv_ref[:, 2 * da + lo:2 * da + lo + HEAD_DIM] = dv[h]
            dz_ref[:, lo:lo + HEAD_DIM] = dz[h]
            dab = dab + jnp.where(lane == h, da_col[h], 0.0) + jnp.where(lane == heads + h, db_col[h], 0.0)
            dpar = (dpar + jnp.where((row8 == 0) & (lane8 == h), d_alog[h], 0.0)
                    + jnp.where((row8 == 1) & (lane8 == h), d_dtb[h], 0.0))
        dab_ref[...] = dab
        dpar_ref[...] += dpar

    rev = lambda i: (n - 1 - i, 0)
    return pl.pallas_call(
        body, name="delta_bwd", grid=(n,),
        in_specs=[pl.BlockSpec((c, 3 * da), rev), pl.BlockSpec((c, da), rev), pl.BlockSpec((c, LANES), rev),
                  pl.BlockSpec((8, LANES), lambda i: (0, 0)),
                  pl.BlockSpec((1, heads, HEAD_DIM, HEAD_DIM), lambda i: (n - 1 - i, 0, 0, 0)),
                  pl.BlockSpec((1, 3, heads, c, c), lambda i: (n - 1 - i, 0, 0, 0, 0)),
                  pl.BlockSpec((1, 3, heads, c, HEAD_DIM), lambda i: (n - 1 - i, 0, 0, 0, 0)),
                  pl.BlockSpec((c, da), rev)],
        out_specs=[pl.BlockSpec((c, 3 * da), rev), pl.BlockSpec((c, da), rev), pl.BlockSpec((c, LANES), rev),
                   pl.BlockSpec((8, LANES), lambda i: (0, 0))],
        out_shape=[jax.ShapeDtypeStruct((t, 3 * da), F32), jax.ShapeDtypeStruct((t, da), F32),
                   jax.ShapeDtypeStruct((t, LANES), F32), jax.ShapeDtypeStruct((8, LANES), F32)],
        scratch_shapes=[pltpu.VMEM((heads, HEAD_DIM, HEAD_DIM), F32)],
        compiler_params=_params("arbitrary"))(qkv, z, ab, gpar, states, kept_c, kept_w, doa)


def _w_in_pieces(shard_cols, da, heads):
    a0, nab = 4 * da, 2 * heads
    d_in = 4 * shard_cols
    runs = [(0, a0, 0), (a0, a0 + nab, d_in - nab), (a0 + nab, d_in, a0)]
    pieces = []
    for j in range(4):
        lo, hi = j * shard_cols, (j + 1) * shard_cols
        for rlo, rhi, plo in runs:
            s, e = max(lo, rlo), min(hi, rhi)
            if s < e:
                pieces.append((j, s - lo, e - s, plo + (s - rlo)))
    return pieces, d_in - nab + LANES


def _w_in_pack(w4, da, heads):
    _, d, sc = w4.shape
    pieces, npk = _w_in_pieces(sc, da, heads)
    tr = _tile_rows(d, 256, SUBLANES_WIRE)

    def body(w_ref, o_ref):
        o_ref[:, npk - LANES:] = jnp.zeros((tr, LANES), o_ref.dtype)
        for j, lo, ln, dst in pieces:
            o_ref[:, dst:dst + ln] = w_ref[j, :, lo:lo + ln]

    return pl.pallas_call(
        body, name="w_in_pack", grid=(d // tr,),
        in_specs=[pl.BlockSpec((4, tr, sc), lambda i: (0, i, 0))],
        out_specs=pl.BlockSpec((tr, npk), lambda i: (i, 0)),
        out_shape=jax.ShapeDtypeStruct((d, npk), w4.dtype),
        compiler_params=_params("arbitrary"))(w4)


def _w_in_unpack(dwp, sc, da, heads):
    d, npk = dwp.shape
    pieces, _ = _w_in_pieces(sc, da, heads)
    tr = _tile_rows(d, 256)

    def body(g_ref, o_ref):
        for j, lo, ln, dst in pieces:
            o_ref[j, :, lo:lo + ln] = g_ref[:, dst:dst + ln]

    return pl.pallas_call(
        body, name="w_in_unpack", grid=(d // tr,),
        in_specs=[pl.BlockSpec((tr, npk), lambda i: (i, 0))],
        out_specs=pl.BlockSpec((4, tr, sc), lambda i: (0, i, 0)),
        out_shape=jax.ShapeDtypeStruct((4, d, sc), F32),
        compiler_params=_params("arbitrary"))(dwp)


def _block_diag(pool_w):
    g, gd, _ = pool_w.shape
    out = jnp.zeros((g * gd, g * gd), pool_w.dtype)
    for gi in range(g):
        out = lax.dynamic_update_slice(out, pool_w[gi], (gi * gd, gi * gd))
    return out


def _layer_dims(d):
    heads = (d // 2) // HEAD_DIM
    return heads, heads * HEAD_DIM, d // 4, d // 4


BIG = ("w_in", "w_gate", "w_up", "ple_proj", "w_out", "w_down", "ple_gate")


def _prepare_layer(small, li):
    d = small["norm1_g"].shape[1]
    heads, _, _, _ = _layer_dims(d)
    gpar = jnp.zeros((8, LANES), F32)
    gpar = gpar.at[0, :heads].set(small["a_log"][li]).at[1, :heads].set(small["dt_bias"][li]).at[2, :].set(small["onorm_g"][li])
    return dict(norm1_g=small["norm1_g"][li][None], conv_qkv=small["conv_qkv"][li], gpar=gpar, pool_bd=_block_diag(small["pool_w"][li]).astype(MM_DTYPE),
                pool_scale=small["pool_scale"][li][None], sconv_w=small["sconv_w"][li], norm2_g=small["norm2_g"][li][None])


def _layer_fwd(x0, p, gw, lw, tm, arrive):
    d = x0.shape[1]
    heads, da, dp, dc = _layer_dims(d)
    segs = (3 * da, da, dp, 3 * dc, LANES)
    lw["w_in_p"] = _w_in_pack(gw["w_in"], da, heads).astype(MM_DTYPE)
    qkv_pre, z, hp, cbcch, ab = _in_proj_fwd(x0, lw["norm1_g"], lw["w_in_p"], segs, tm)
    qkv = _qkv_conv_fwd(qkv_pre, lw["conv_qkv"], heads)
    oa, states, kept_c, kept_w = _delta_fwd(qkv, z, ab, lw["gpar"], heads)
    ob = _pool_fwd(hp, lw["pool_bd"], lw["pool_scale"], dp // POOL_GROUPS)
    oc = _sconv_fwd(cbcch, lw["sconv_w"])
    arrive("mixed", oa)
    x1, h2 = _out_proj_fwd(x0, (oa, ob, oc), gw["w_out"], lw["norm2_g"], tm)
    x2, gp, up = _ffn_fwd(x1, h2, gw["w_gate"], gw["w_up"], gw["w_down"], tm)
    arrive("ffn", x2)
    x3 = _ple_fwd(x2, p, gw["ple_gate"], gw["ple_proj"], tm)
    arrive("end", x3)
    saved = dict(x0=x0, qkv_pre=qkv_pre, z=z, hp=hp, cbcch=cbcch, ab=ab, qkv=qkv, states=states, kept_c=kept_c, kept_w=kept_w, oa=oa, ob=ob, oc=oc,
                 x1=x1, h2=h2, gp=gp, up=up, x2=x2)
    return x3, saved


def _layer_bwd(dx3, p, gw, lw, sv, tm, produced):
    def after_token(tok, arr):
        return arr if tok is None else arr + tok[0, 0]

    d = dx3.shape[1]
    heads, da, dp, dc = _layer_dims(d)
    segs = (3 * da, da, dp, dc, dc, dc, LANES)
    gd = dp // POOL_GROUPS
    dx2, d_ple_gate, d_ple_proj = _ple_bwd(dx3, sv["x2"], p, gw["ple_gate"], gw["ple_proj"], tm)
    dh2, d_w_gate, d_w_up, d_w_down = _ffn_bwd(dx2, sv["h2"], sv["gp"], sv["up"], gw["w_gate"], gw["w_up"], gw["w_down"],
                                               min(tm, 256))
    tok = produced("ffn", dict(w_gate=d_w_gate, w_up=d_w_up, ple_proj=d_ple_proj, w_down=d_w_down, ple_gate=d_ple_gate), dh2)
    dx1, doa, dob, doc, d_w_out, d_norm2 = _out_proj_bwd(dx2, dh2, sv["x1"], after_token(tok, lw["norm2_g"]),
                                                         (sv["oa"], sv["ob"], sv["oc"]), gw["w_out"], tm)
    dcb, dcc, dch, d_sconv = _sconv_bwd(sv["cbcch"], lw["sconv_w"], doc)
    dhp, d_pool_bd, d_pool_scale = _pool_bwd(sv["hp"], lw["pool_bd"], lw["pool_scale"], dob, gd)
    dqkv, dz, dab, dpar = _delta_bwd(sv["qkv"], sv["z"], sv["ab"], lw["gpar"], sv["states"], sv["kept_c"], sv["kept_w"], doa,
                                      heads)
    tok = produced("mixers", {}, dqkv)
    dqkv_pre, d_conv_qkv = _qkv_conv_bwd(sv["qkv_pre"], lw["conv_qkv"], dqkv, heads)
    dsegs = (dqkv_pre, dz, dhp, dcb, dcc, dch, dab)
    dx0, d_w_in_p, d_norm1 = _in_proj_bwd(sv["x0"], after_token(tok, lw["norm1_g"]), lw["w_in_p"], dsegs, dx1, segs, tm)
    per = LANES // gd
    bd = d_pool_bd.reshape(dp // LANES, per, gd, per, gd)
    d_pool_w = jnp.stack([bd[gi // per, gi % per, :, gi % per, :] for gi in range(POOL_GROUPS)])
    big = dict(w_in=_w_in_unpack(d_w_in_p, gw["w_in"].shape[2], da, heads), w_gate=d_w_gate, w_up=d_w_up,
               ple_proj=d_ple_proj, w_out=d_w_out, w_down=d_w_down, ple_gate=d_ple_gate)
    small = dict(norm1_g=d_norm1[0], conv_qkv=d_conv_qkv, a_log=dpar[0, :heads], dt_bias=dpar[1, :heads], onorm_g=dpar[2],
                 pool_w=d_pool_w, pool_scale=d_pool_scale[0], sconv_w=d_sconv, norm2_g=d_norm2[0])
    tok = produced("end", dict(w_in=big["w_in"], w_out=d_w_out), big["w_in"])
    return dx0, big, small, tok


def _local_step(x, p, target, gw, small, produced=None, arrive=None):
    t, d = x.shape
    depth = p.shape[0]
    tm = 512 if t % 512 == 0 else 128
    layers = [_prepare_layer(small, li) for li in range(depth)]
    saved = []
    h = x
    for li in range(depth):
        h, sv = _layer_fwd(h, p[li], gw[li], layers[li], tm,
                           (lambda stage, after, li=li: arrive(li, stage, after)) if arrive else (lambda stage, after: None))
        saved.append(sv)
    dx, loss, d_final = _loss_head(h, target, small["final_g"][None], tm)
    big, sm = [None] * depth, [None] * depth
    token = None
    for li in reversed(range(depth)):
        p_li = p[li] if token is None else p[li] + token[0, 0]
        dx, big[li], sm[li], token = _layer_bwd(
            dx, p_li, gw[li], layers[li], saved[li], tm,
            (lambda stage, grads, after, li=li: produced(li, stage, grads, after)) if produced else (lambda *a: None))
    small_grads = {n: jnp.stack([g[n] for g in sm]) for n in sm[0]}
    small_grads["final_g"] = d_final[0]
    return loss[0, 0], dx, big, small_grads


def _coords():
    return lax.axis_index("x"), lax.axis_index("y"), lax.axis_index("c")


def _other_chips(x, y):
    return [(1 - x, y), (x, 1 - y), (1 - x, 1 - y)]


def _place_shards(ws, me_idx):
    nt = len(ws)
    depth = ws[0].shape[0]

    def body(me_ref, *refs):
        for t, w_ref in enumerate(refs[:nt]):
            for li in range(depth):
                refs[nt + li * nt + t][...] = w_ref[li].astype(WIRE_DTYPE)

    outs = pl.pallas_call(
        body, name="place_shards",
        grid_spec=pltpu.PrefetchScalarGridSpec(
            num_scalar_prefetch=1, grid=(4,),
            in_specs=[pl.BlockSpec((depth, w.shape[1] // 4, w.shape[2]), lambda i, me_ref: (0, i, 0)) for w in ws],
            out_specs=[pl.BlockSpec((None, w.shape[1] // 4, w.shape[2]), lambda i, me_ref: (me_ref[0], i, 0))
                       for _ in range(depth) for w in ws]),
        out_shape=[jax.ShapeDtypeStruct((4,) + w.shape[1:], WIRE_DTYPE) for _ in range(depth) for w in ws],
        compiler_params=_params("arbitrary"))(me_idx, *ws)
    return [list(outs[li * nt:(li + 1) * nt]) for li in range(depth)]


def _half_block(ref, chip, pc):
    rh = ref.shape[1] // 2
    return ref.at[chip, pl.ds(pc * rh, rh)]


def _gather_copies(out_refs, send_sems, recv_sems, stage):
    nt = len(out_refs)
    x, y, c = _coords()
    pairs = []
    for j, (cx, cy) in enumerate(_other_chips(x, y)):
        for t in range(nt):
            sems = dict(send_sem=send_sems[j * nt + t], recv_sem=recv_sems[j * nt + t], device_id_type=MESH)
            if stage == 0:
                mine, theirs, to = _half_block(out_refs[t], 2 * x + y, c), _half_block(out_refs[t], 2 * cx + cy, c), (cx, cy, c)
            else:
                mine, theirs, to = (_half_block(out_refs[t], 2 * cx + cy, c), _half_block(out_refs[t], 2 * cx + cy, 1 - c),
                                    (x, y, 1 - c))
            pairs.append((pltpu.make_async_remote_copy(src_ref=mine, dst_ref=mine, device_id=to, **sems),
                          pltpu.make_async_remote_copy(src_ref=theirs, dst_ref=theirs, device_id=to, **sems)))
    return pairs


def _all_gather_chips(placed):
    nt = len(placed)

    def body(*refs):
        out_refs = refs[nt:2 * nt]
        send_sems, recv_sems = refs[2 * nt:]
        nc = 3 * nt
        first = _gather_copies(out_refs, [send_sems.at[k] for k in range(nc)], [recv_sems.at[k] for k in range(nc)], 0)
        passed = _gather_copies(out_refs, [send_sems.at[nc + k] for k in range(nc)], [recv_sems.at[nc + k] for k in range(nc)], 1)
        for start, _ in first:
            start.start()
        for (_, arrival), (forward, _) in zip(first, passed):
            arrival.wait_recv()
            forward.start()
        for _, arrival in passed:
            arrival.wait_recv()
        for start, _ in first + passed:
            start.wait_send()

    return pl.pallas_call(
        body, name="all_gather_chips", out_shape=[jax.ShapeDtypeStruct(a.shape, a.dtype) for a in placed],
        in_specs=[ANY] * nt, out_specs=[ANY] * nt, input_output_aliases={t: t for t in range(nt)},
        scratch_shapes=[pltpu.SemaphoreType.DMA((6 * nt,)), pltpu.SemaphoreType.DMA((6 * nt,))],
    )(*placed)


def _gather_call(name, arrs, wait_sems, after, stage):
    nt = len(arrs)
    nc = 3 * nt
    n_wait = len(wait_sems)
    n_new = 2 * nc if stage < 2 else 0
    arrs = [pltpu.with_memory_space_constraint(a, pltpu.HBM) for a in arrs]

    def body(*refs):
        a_refs = refs[:nt]
        waits = refs[nt:nt + n_wait]
        news = refs[nt + n_wait + 1:nt + n_wait + 1 + n_new]
        token = refs[-1]
        if stage > 0:
            for start, arrival in _gather_copies(a_refs, waits[:nc], waits[nc:], stage - 1):
                start.wait_send()
                arrival.wait_recv()
        if stage < 2:
            for start, _ in _gather_copies(a_refs, news[:nc], news[nc:], stage):
                start.start()
        token[...] = jnp.zeros_like(token)

    outs = pl.pallas_call(
        body, name=name,
        out_shape=(*[pltpu.SemaphoreType.DMA(())] * n_new, *[pltpu.HBM(a.shape, a.dtype) for a in arrs],
                   jax.ShapeDtypeStruct((8, LANES), F32)),
        in_specs=[HBM] * nt + [SEM] * n_wait + [ANY],
        out_specs=(*[SEM] * n_new, *[HBM] * nt, pl.BlockSpec(memory_space=pltpu.VMEM)),
        input_output_aliases={t: n_new + t for t in range(nt)},
        compiler_params=pltpu.CompilerParams(has_side_effects=pltpu.SideEffectType.DATAFLOW_SIDE_EFFECTING),
    )(*arrs, *wait_sems, after)
    return list(outs[:n_new]), list(outs[n_new:n_new + nt]), outs[-1]


def _sibling_swap_half(gs):
    nt = len(gs)

    def body(*refs):
        g_refs, out_refs = refs[:nt], refs[nt:2 * nt]
        send_sems, recv_sems = refs[2 * nt:]
        x, y, c = _coords()
        cps = []
        for t in range(nt):
            rh = g_refs[t].shape[1] // 2
            cps.append(pltpu.make_async_remote_copy(src_ref=g_refs[t].at[:, pl.ds((1 - c) * rh, rh)], dst_ref=out_refs[t],
                                                    send_sem=send_sems.at[t], recv_sem=recv_sems.at[t], device_id=(x, y, 1 - c),
                                                    device_id_type=MESH))
        for cp in cps:
            cp.start()
        for cp in cps:
            cp.wait()

    return pl.pallas_call(
        body, name="sibling_swap_half",
        out_shape=[jax.ShapeDtypeStruct((g.shape[0], g.shape[1] // 2, g.shape[2]), g.dtype) for g in gs],
        in_specs=[ANY] * nt, out_specs=[ANY] * nt,
        scratch_shapes=[pltpu.SemaphoreType.DMA((nt,)), pltpu.SemaphoreType.DMA((nt,))])(*gs)


def _add_my_halves(gs, others, c_idx):
    nt = len(gs)

    def body(c_ref, *refs):
        for g_ref, o_ref, out_ref in zip(refs[:nt], refs[nt:2 * nt], refs[2 * nt:]):
            out_ref[...] = (g_ref[...].astype(F32) + o_ref[...].astype(F32)).astype(out_ref.dtype)

    def quarter(g):
        return pl.BlockSpec((None, g.shape[1] // 4, g.shape[2]), lambda j, i, c_ref: (j, i, 0))

    return pl.pallas_call(
        body, name="add_my_halves",
        grid_spec=pltpu.PrefetchScalarGridSpec(
            num_scalar_prefetch=1, grid=(4, 2),
            in_specs=[pl.BlockSpec((None, g.shape[1] // 4, g.shape[2]), lambda j, i, c_ref: (j, 2 * c_ref[0] + i, 0)) for g in gs]
                     + [quarter(g) for g in gs],
            out_specs=[quarter(g) for g in gs]),
        out_shape=[jax.ShapeDtypeStruct((4, g.shape[1] // 2, g.shape[2]), WIRE_DTYPE) for g in gs],
        compiler_params=_params("arbitrary", "arbitrary"))(c_idx, *gs, *others)


def _exchange_chips(parts):
    nt = len(parts)

    def body(*refs):
        p_refs, out_refs = refs[:nt], refs[nt:2 * nt]
        send_sems, recv_sems = refs[2 * nt:]
        x, y, c = _coords()
        chips = _other_chips(x, y)

        def copy(j, t):
            cx, cy = chips[j]
            return pltpu.make_async_remote_copy(src_ref=p_refs[t].at[2 * cx + cy], dst_ref=out_refs[t].at[j],
                                                send_sem=send_sems.at[j, t], recv_sem=recv_sems.at[j, t], device_id=(cx, cy, c),
                                                device_id_type=MESH)

        sends = [copy(j, t) for j in range(3) for t in range(nt)]
        for cp in sends:
            cp.start()
        for cp in sends:
            cp.wait_recv()
        for cp in sends:
            cp.wait_send()

    return pl.pallas_call(
        body, name="exchange_chips", out_shape=[jax.ShapeDtypeStruct((3,) + p.shape[1:], p.dtype) for p in parts],
        in_specs=[ANY] * nt, out_specs=[ANY] * nt,
        scratch_shapes=[pltpu.SemaphoreType.DMA((3, nt)), pltpu.SemaphoreType.DMA((3, nt))])(*parts)


def _split_plan(kind, s_refs, l_refs):
    x, y, c = _coords()
    if kind == "swap":
        return [(s.at[:, pl.ds((1 - c) * (s.shape[1] // 2), s.shape[1] // 2)], l, (x, y, 1 - c)) for s, l in zip(s_refs, l_refs)]
    return [(s.at[2 * cx + cy], l.at[j], (cx, cy, c)) for j, (cx, cy) in enumerate(_other_chips(x, y))
            for s, l in zip(s_refs, l_refs)]


def _split_landing(kind, a):
    return (a.shape[0], a.shape[1] // 2, a.shape[2]) if kind == "swap" else (3,) + a.shape[1:]


def _copies_start(name, kind, srcs, after=None):
    ns = len(srcs)
    n = ns if kind == "swap" else 3 * ns
    srcs = [pltpu.with_memory_space_constraint(a, pltpu.HBM) for a in srcs]
    lands = [pltpu.with_memory_space_constraint(lax.empty(_split_landing(kind, a), a.dtype), pltpu.HBM) for a in srcs]
    extra = [] if after is None else [after]

    def body(*refs):
        first_sem = 2 * ns + len(extra)
        sems, token = refs[first_sem:first_sem + 2 * n], refs[-1]
        for k, (src, dst, dev) in enumerate(_split_plan(kind, refs[:ns], refs[ns:2 * ns])):
            pltpu.make_async_remote_copy(src_ref=src, dst_ref=dst, send_sem=sems[k], recv_sem=sems[n + k], device_id=dev,
                                         device_id_type=MESH).start()
        token[...] = jnp.zeros_like(token)

    outs = pl.pallas_call(
        body, name=name,
        out_shape=(*[pltpu.SemaphoreType.DMA(())] * (2 * n), *[pltpu.HBM(a.shape, a.dtype) for a in srcs + lands],
                   jax.ShapeDtypeStruct((8, LANES), F32)),
        in_specs=[HBM] * (2 * ns) + [ANY] * len(extra),
        out_specs=(*[SEM] * (2 * n), *[HBM] * (2 * ns), pl.BlockSpec(memory_space=pltpu.VMEM)),
        input_output_aliases={t: 2 * n + t for t in range(2 * ns)},
        compiler_params=pltpu.CompilerParams(has_side_effects=pltpu.SideEffectType.DATAFLOW_SIDE_EFFECTING),
    )(*srcs, *lands, *extra)
    return list(outs[:2 * n]), list(outs[2 * n:2 * n + ns]), list(outs[2 * n + ns:2 * n + 2 * ns]), outs[-1]


def _copies_wait(name, kind, sems, srcs, lands, after):
    ns = len(srcs)
    n = len(sems) // 2

    def body(*refs):
        sem_refs = refs[2 * ns:2 * ns + 2 * n]
        for k, (src, dst, dev) in enumerate(_split_plan(kind, refs[:ns], refs[ns:2 * ns])):
            cp = pltpu.make_async_remote_copy(src_ref=src, dst_ref=dst, send_sem=sem_refs[k], recv_sem=sem_refs[n + k],
                                              device_id=dev, device_id_type=MESH)
            cp.wait_send()
            cp.wait_recv()

    outs = pl.pallas_call(
        body, name=name, out_shape=tuple(pltpu.HBM(a.shape, a.dtype) for a in srcs + lands),
        in_specs=[HBM] * (2 * ns) + [SEM] * (2 * n) + [ANY], out_specs=tuple([HBM] * (2 * ns)),
        input_output_aliases={t: t for t in range(2 * ns)},
        compiler_params=pltpu.CompilerParams(has_side_effects=pltpu.SideEffectType.DATAFLOW_SIDE_EFFECTING),
    )(*srcs, *lands, *sems, after)
    return list(outs[:ns]), list(outs[ns:])


def _sum_into(pairs, recvs, idx):
    nt = len(pairs)

    def body(idx_ref, *refs):
        for p_ref, r_ref, out_ref in zip(refs[:nt], refs[nt:2 * nt], refs[2 * nt:]):
            out_ref[...] = p_ref[...].astype(F32) + r_ref[0].astype(F32) + r_ref[1].astype(F32) + r_ref[2].astype(F32)

    return pl.pallas_call(
        body, name="sum_into",
        grid_spec=pltpu.PrefetchScalarGridSpec(
            num_scalar_prefetch=1, grid=(2,),
            in_specs=[pl.BlockSpec((None, p.shape[1] // 2, p.shape[2]), lambda i, idx_ref: (idx_ref[0], i, 0)) for p in pairs]
                     + [pl.BlockSpec((3, p.shape[1] // 2, p.shape[2]), lambda i, idx_ref: (0, i, 0)) for p in pairs],
            out_specs=[pl.BlockSpec((p.shape[1] // 2, p.shape[2]), lambda i, idx_ref: (2 * idx_ref[1] + i, 0)) for p in pairs]),
        out_shape=[jax.ShapeDtypeStruct((2 * p.shape[1], p.shape[2]), F32) for p in pairs],
        compiler_params=_params("arbitrary"))(idx, *pairs, *recvs)


def _sum_slots(parts):
    n, rows, cols = parts.shape
    tr = _tile_rows(rows, 512, SUBLANES_WIRE)

    def body(p_ref, out_ref):
        acc = p_ref[0].astype(F32)
        for s in range(1, n):
            acc = acc + p_ref[s].astype(F32)
        out_ref[...] = acc

    return pl.pallas_call(
        body, name="sum_slots", grid=(rows // tr,),
        in_specs=[pl.BlockSpec((n, tr, cols), lambda i: (0, i, 0))],
        out_specs=pl.BlockSpec((tr, cols), lambda i: (i, 0)),
        out_shape=jax.ShapeDtypeStruct((rows, cols), F32),
        compiler_params=_params("arbitrary"))(parts)


def _sibling_share(gs):
    nt = len(gs)

    def body(*refs):
        out_refs = refs[nt:2 * nt]
        send_sems, recv_sems = refs[2 * nt:]
        x, y, c = _coords()
        sends, recvs = [], []
        for t in range(nt):
            rh = out_refs[t].shape[0] // 2
            mine, theirs = out_refs[t].at[pl.ds(c * rh, rh)], out_refs[t].at[pl.ds((1 - c) * rh, rh)]
            sems = dict(send_sem=send_sems.at[t], recv_sem=recv_sems.at[t], device_id=(x, y, 1 - c), device_id_type=MESH)
            sends.append(pltpu.make_async_remote_copy(src_ref=mine, dst_ref=mine, **sems))
            recvs.append(pltpu.make_async_remote_copy(src_ref=theirs, dst_ref=theirs, **sems))
        for cp in sends:
            cp.start()
        for cp in recvs:
            cp.wait_recv()
        for cp in sends:
            cp.wait_send()

    return pl.pallas_call(
        body, name="sibling_share", out_shape=[jax.ShapeDtypeStruct(g.shape, g.dtype) for g in gs],
        in_specs=[ANY] * nt, out_specs=[ANY] * nt, input_output_aliases={t: t for t in range(nt)},
        scratch_shapes=[pltpu.SemaphoreType.DMA((nt,)), pltpu.SemaphoreType.DMA((nt,))])(*gs)


def _all_gather_devices(buf):
    def body(b_ref, out_ref, send_sems, recv_sems, local_sem):
        x, y, c = _coords()
        me = 4 * x + 2 * y + c
        mine = pltpu.make_async_copy(b_ref, out_ref.at[me], local_sem)
        mine.start()
        peers = []
        for k in range(1, 8):
            fx, fy, fc = (k >> 2) & 1, (k >> 1) & 1, k & 1
            peers.append((x ^ fx, y ^ fy, c ^ fc))
        sends = [pltpu.make_async_remote_copy(src_ref=b_ref, dst_ref=out_ref.at[me], send_sem=send_sems.at[k],
                                              recv_sem=recv_sems.at[k], device_id=peer, device_id_type=MESH)
                 for k, peer in enumerate(peers)]
        for cp in sends:
            cp.start()
        for k, (px, py, pc) in enumerate(peers):
            pltpu.make_async_remote_copy(src_ref=b_ref, dst_ref=out_ref.at[4 * px + 2 * py + pc], send_sem=send_sems.at[k],
                                         recv_sem=recv_sems.at[k], device_id=(px, py, pc), device_id_type=MESH).wait_recv()
        for cp in sends:
            cp.wait_send()
        mine.wait()

    return pl.pallas_call(
        body, name="all_gather_devices", out_shape=jax.ShapeDtypeStruct((8,) + buf.shape, buf.dtype),
        in_specs=[ANY], out_specs=ANY,
        scratch_shapes=[pltpu.SemaphoreType.DMA((7,)), pltpu.SemaphoreType.DMA((7,)), pltpu.SemaphoreType.DMA(())])(buf)


def _pair_sums(big_grads, c_idx):
    gs = [big_grads[n] for n in BIG]
    return _add_my_halves(gs, _sibling_swap_half(gs), c_idx)


SMALL_SHARDED = ("conv_qkv", "sconv_w")
REPLICATED = ("norm1_g", "a_log", "dt_bias", "onorm_g", "pool_w", "pool_scale", "norm2_g", "final_g")
ALL_WEIGHTS = ("norm1_g", "w_in", "conv_qkv", "a_log", "dt_bias", "onorm_g", "pool_w", "pool_scale", "sconv_w", "w_out",
               "norm2_g", "w_gate", "w_up", "w_down", "ple_proj", "ple_gate", "final_g")


def _pad_rows(flat, row_multiple):
    m = flat.shape[0]
    r = -(-m // (LANES * row_multiple)) * row_multiple
    return jnp.pad(flat, (0, r * LANES - m)).reshape(r, LANES)


def _adamw_math(w, g, m, v):
    c1 = 1.0 / (1.0 - ADAM_B1 ** ADAM_STEP)
    c2 = 1.0 / (1.0 - ADAM_B2 ** ADAM_STEP)
    nm = ADAM_B1 * m + (1.0 - ADAM_B1) * g
    nv = ADAM_B2 * v + (1.0 - ADAM_B2) * (g * g)
    return -ADAM_LR * ((nm * c1) / (jnp.sqrt(nv * c2) + ADAM_EPS) + ADAM_WD * w), nm, nv


def _adamw(w, g, m, v):
    shape = w.shape
    cols = shape[-1]
    rows = w.size // cols
    tr = _tile_rows(rows, 512)

    def body(w_ref, g_ref, m_ref, v_ref, d_ref, nm_ref, nv_ref, go_ref):
        gv = g_ref[...]
        d_ref[...], nm_ref[...], nv_ref[...] = _adamw_math(w_ref[...], gv, m_ref[...], v_ref[...])
        go_ref[...] = gv

    spec = pl.BlockSpec((tr, cols), lambda i: (i, 0))
    outs = pl.pallas_call(
        body, name="adamw", grid=(rows // tr,), in_specs=[spec] * 4, out_specs=[spec] * 4,
        out_shape=[jax.ShapeDtypeStruct((rows, cols), F32)] * 4,
        compiler_params=_params("arbitrary"))(*[a.reshape(rows, cols) for a in (w, g, m, v)])
    return tuple(o.reshape(shape) for o in outs)


def _adamw_layer(w, g, m, v, li, outs):
    depth, rows, cols = w.shape
    tr = _tile_rows(rows, 512)

    def body(w_ref, g_ref, m_ref, v_ref, *rest):
        d_ref, nm_ref, nv_ref, go_ref = rest[-4:]
        gv = g_ref[...]
        d_ref[...], nm_ref[...], nv_ref[...] = _adamw_math(w_ref[...], gv, m_ref[...], v_ref[...])
        go_ref[...] = gv

    layer = pl.BlockSpec((None, tr, cols), lambda i: (li, i, 0))
    prev = [] if outs is None else list(outs)
    return pl.pallas_call(
        body, name="adamw_layer", grid=(rows // tr,),
        in_specs=[layer, pl.BlockSpec((tr, cols), lambda i: (i, 0)), layer, layer] + [ANY] * len(prev), out_specs=[layer] * 4,
        out_shape=[jax.ShapeDtypeStruct(w.shape, F32)] * 4, input_output_aliases={4 + k: k for k in range(len(prev))},
        compiler_params=_params("arbitrary"))(w, g, m, v, *prev)


def kernel(x, p, norm1_g, w_in, conv_qkv, a_log, dt_bias, onorm_g, pool_w, pool_scale, sconv_w, w_out, norm2_g, w_gate, w_up, w_down, ple_proj, ple_gate, final_g, loss_target, m_norm1_g, m_w_in, m_conv_qkv, m_a_log, m_dt_bias, m_onorm_g, m_pool_w, m_pool_scale, m_sconv_w, m_w_out, m_norm2_g, m_w_gate, m_w_up, m_w_down, m_ple_proj, m_ple_gate, m_final_g, v_norm1_g, v_w_in, v_conv_qkv, v_a_log, v_dt_bias, v_onorm_g, v_pool_w, v_pool_scale, v_sconv_w, v_w_out, v_norm2_g, v_w_gate, v_w_up, v_w_down, v_ple_proj, v_ple_gate, v_final_g):
    weights = dict(zip(ALL_WEIGHTS, (norm1_g, w_in, conv_qkv, a_log, dt_bias, onorm_g, pool_w, pool_scale, sconv_w, w_out,
                                     norm2_g, w_gate, w_up, w_down, ple_proj, ple_gate, final_g)))
    mom_m = dict(zip(ALL_WEIGHTS, (m_norm1_g, m_w_in, m_conv_qkv, m_a_log, m_dt_bias, m_onorm_g, m_pool_w, m_pool_scale,
                                   m_sconv_w, m_w_out, m_norm2_g, m_w_gate, m_w_up, m_w_down, m_ple_proj, m_ple_gate, m_final_g)))
    mom_v = dict(zip(ALL_WEIGHTS, (v_norm1_g, v_w_in, v_conv_qkv, v_a_log, v_dt_bias, v_onorm_g, v_pool_w, v_pool_scale,
                                   v_sconv_w, v_w_out, v_norm2_g, v_w_gate, v_w_up, v_w_down, v_ple_proj, v_ple_gate, v_final_g)))
    c_idx = lax.axis_index("c").astype(jnp.int32).reshape(1)
    chip = (2 * lax.axis_index("x") + lax.axis_index("y")).astype(jnp.int32)
    me_idx = chip.reshape(1)
    idx = jnp.stack([chip, lax.axis_index("c").astype(jnp.int32)])
    depth = p.shape[0]

    placed = _place_shards([weights[n] for n in BIG], me_idx)
    gw = [dict() for _ in range(depth)]
    gw[0]["w_in"] = _all_gather_chips(placed[0][:1])[0]
    early = ("w_in", "w_out")
    late = tuple(n for n in BIG if n not in early)
    groups = [dict(li=0, names=BIG[1:], forward=(0, "mixed"), finish=(0, "mixed"))]
    for li in range(1, depth):
        groups.append(dict(li=li, names=early, forward=(li - 1, "ffn"), finish=(li - 1, "end")))
        groups.append(dict(li=li, names=late, forward=(li, "mixed"), finish=(li, "mixed")))
    token = gw[0]["w_in"]
    for k, g in enumerate(groups):
        arrs = [placed[g["li"]][BIG.index(n)] for n in g["names"]]
        g["sems"], g["arrs"], token = _gather_call("gather_start_%d" % k, arrs, [], token, 0)

    def arrive(li, stage, after):
        for k, g in enumerate(groups):
            if g["forward"] == (li, stage):
                g["sems"], g["arrs"], _ = _gather_call("gather_forward_%d" % k, g["arrs"], g["sems"], after, 1)
            if g["finish"] == (li, stage):
                _, g["arrs"], _ = _gather_call("gather_finish_%d" % k, g["arrs"], g["sems"], after, 2)
                gw[g["li"]].update(zip(g["names"], g["arrs"]))

    small = {n: weights[n] for n in REPLICATED}
    small["norm1_g"] = small["norm1_g"] + token[0, 0]
    sflat = _pad_rows(jnp.concatenate([weights[n].reshape(-1) for n in SMALL_SHARDED]), 8)
    sgath = _all_gather_devices(sflat)[0::2].reshape(4, -1)
    off = 0
    for n in SMALL_SHARDED:
        shp = weights[n].shape
        part = sgath[:, off:off + weights[n].size].reshape((4,) + shp)
        small[n] = jnp.moveaxis(part, 0, -2).reshape(shp[:-1] + (4 * shp[-1],))
        off += weights[n].size

    pending = []
    last_token = [None]

    def advance(g, after):
        if g["stage"] == 0:
            gs, others = _copies_wait("swap_wait_" + g["tag"], "swap", *g["handle"], after)
            g["handle"] = _copies_start("exchange_start_" + g["tag"], "exchange", _add_my_halves(gs, others, c_idx))
            g["stage"] = 1
            return g["handle"][3]
        return None

    def produced(li, stage, grads, after):
        token = None
        for g in pending:
            token = advance(g, after) if g["stage"] == 0 else token
        if grads:
            names = [n for n in BIG if n in grads]
            handle = _copies_start("swap_start_%d%s" % (li, stage), "swap", [grads[n] for n in names], token)
            pending.append(dict(li=li, names=names, tag="%d%s" % (li, stage), stage=0, handle=handle[:3]))
            token = handle[3]
        last_token[0] = last_token[0] if token is None else token
        return token

    loss_local, dx, _, small_grads = _local_step(x[0], p[:, 0], loss_target[0], gw, small, produced, arrive)
    big_outs = {n: None for n in BIG}

    def finish(g, after):
        pairs, recvs = _copies_wait("exchange_wait_" + g["tag"], "exchange", *g["handle"][:3], after)
        for n, grad in zip(g["names"], _sibling_share(_sum_into(pairs, recvs, idx))):
            big_outs[n] = _adamw_layer(weights[n], grad, mom_m[n], mom_v[n], g["li"], big_outs[n])
        return big_outs[g["names"][-1]][0]

    done = finish(pending[0], last_token[0])
    started = advance(pending[-1], done)
    for g in pending[1:-1]:
        done = finish(g, started)
    finish(pending[-1], done)


    gshard = {}
    rnames = REPLICATED + SMALL_SHARDED
    rflat = _pad_rows(jnp.concatenate([small_grads[n].reshape(-1) for n in rnames]), 8)
    rsum = _sum_slots(_all_gather_devices(rflat)).reshape(-1)
    off = 0
    for n in rnames:
        whole = rsum[off:off + small_grads[n].size].reshape(small_grads[n].shape)
        off += small_grads[n].size
        if n in SMALL_SHARDED:
            cols = weights[n].shape[-1]
            whole = lax.dynamic_slice_in_dim(whole, chip * cols, cols, axis=whole.ndim - 1)
        gshard[n] = whole

    loss = lax.psum(loss_local, ("x", "y", "c"))
    deltas, new_m, new_v, grad_out = {}, {}, {}, {}
    for n in ALL_WEIGHTS:
        if n in BIG:
            deltas[n], new_m[n], new_v[n], grad_out[n] = big_outs[n]
        else:
            deltas[n], new_m[n], new_v[n], grad_out[n] = _adamw(weights[n], gshard[n], mom_m[n], mom_v[n])
    return (loss, dx[None], *[grad_out[n] for n in ALL_WEIGHTS], *[deltas[n] for n in ALL_WEIGHTS],
            *[new_m[n] for n in ALL_WEIGHTS], *[new_v[n] for n in ALL_WEIGHTS])
```

```python
import jax
import jax.numpy as jnp
from jax import lax
from jax.experimental import pallas as pl
from jax.experimental.pallas import tpu as pltpu

F32 = jnp.float32
MM_DTYPE = jnp.bfloat16
WIRE_DTYPE = jnp.bfloat16
HI = lax.Precision.HIGHEST
EPS = 1e-6
HEAD_DIM = 128
CHUNK = 64
QKV_CONV_WIDTH = 4
SCONV_WIDTH = 3
POOL_GROUPS = 4
LANES = 128
SUBLANES_WIRE = 16
VMEM_LIMIT_BYTES = 56 * 1024 * 1024
ADAM_LR, ADAM_B1, ADAM_B2, ADAM_EPS, ADAM_WD, ADAM_STEP = 0.001, 0.9, 0.999, 1e-08, 0.01, 10
MESH = pl.DeviceIdType.MESH
ANY = pl.BlockSpec(memory_space=pl.ANY)
HBM = pl.BlockSpec(memory_space=pltpu.HBM)
SEM = pl.BlockSpec(memory_space=pltpu.SEMAPHORE)


def _params(*sem):
    return pltpu.CompilerParams(vmem_limit_bytes=VMEM_LIMIT_BYTES, dimension_semantics=sem if sem else None)


def _mm(a, b):
    return jnp.dot(a.astype(MM_DTYPE), b.astype(MM_DTYPE), preferred_element_type=F32)


def _mm_nt(a, b):
    return lax.dot_general(a.astype(MM_DTYPE), b.astype(MM_DTYPE), (((1,), (1,)), ((), ())), preferred_element_type=F32)


def _mm_tn(a, b):
    return lax.dot_general(a.astype(MM_DTYPE), b.astype(MM_DTYPE), (((0,), (0,)), ((), ())), preferred_element_type=F32)


def _hmm(a, b):
    return jnp.dot(a, b, preferred_element_type=F32, precision=HI)


def _hmm_nt(a, b):
    return lax.dot_general(a, b, (((1,), (1,)), ((), ())), preferred_element_type=F32, precision=HI)


def _hmm_tn(a, b):
    return lax.dot_general(a, b, (((0,), (0,)), ((), ())), preferred_element_type=F32, precision=HI)


def _sigmoid(x):
    return 1.0 / (1.0 + jnp.exp(-x))


def _dsilu(x, s):
    return s * (1.0 + x * (1.0 - s))


def _rows(shape):
    return lax.broadcasted_iota(jnp.int32, shape, 0)


def _shift_down(x, s):
    if s == 0:
        return x
    return jnp.where(_rows(x.shape) >= s, pltpu.roll(x, s, 0), 0.0)


def _shift_up(x, s):
    if s == 0:
        return x
    t = x.shape[0]
    return jnp.where(_rows(x.shape) < t - s, pltpu.roll(x, t - s, 0), 0.0)


def _rms_fwd(x):
    r = lax.rsqrt(jnp.mean(x * x, axis=-1, keepdims=True) + EPS)
    return x * r, r


def _rms_bwd(dxn, xn, r):
    return r * (dxn - xn * jnp.mean(dxn * xn, axis=-1, keepdims=True))


def _tile_rows(n, cap, mult=8):
    best = None
    for d in range(mult, min(n, cap) + 1, mult):
        if n % d == 0:
            best = d
    return best if best is not None else n


def _in_proj_fwd(x, g1, wp, segs, tm):
    t, d = x.shape
    npk = wp.shape[1]

    def body(x_ref, g_ref, w_ref, *o_refs):
        xn, _ = _rms_fwd(x_ref[...])
        h = (xn * g_ref[...]).astype(w_ref.dtype)
        off = 0
        for o_ref, wd in zip(o_refs, segs):
            o_ref[...] = jnp.dot(h, w_ref[:, off:off + wd], preferred_element_type=F32)
            off += wd

    return pl.pallas_call(
        body, name="in_proj_fwd", grid=(t // tm,),
        in_specs=[pl.BlockSpec((tm, d), lambda i: (i, 0)), pl.BlockSpec((1, d), lambda i: (0, 0)),
                  pl.BlockSpec((d, npk), lambda i: (0, 0))],
        out_specs=[pl.BlockSpec((tm, wd), lambda i: (i, 0)) for wd in segs],
        out_shape=[jax.ShapeDtypeStruct((t, wd), F32) for wd in segs],
        compiler_params=_params("arbitrary"))(x, g1, wp)


def _in_proj_bwd(x, g1, wp, dsegs, dx_res, segs, tm):
    t, d = x.shape
    npk = wp.shape[1]
    nseg = len(segs)

    def body(x_ref, g_ref, w_ref, *rest):
        ds_refs = rest[:nseg]
        dxr_ref, dx_ref, dw_ref, dg_ref = rest[nseg:]
        i = pl.program_id(0)

        @pl.when(i == 0)
        def _():
            dw_ref[...] = jnp.zeros_like(dw_ref)
            dg_ref[...] = jnp.zeros_like(dg_ref)

        xn, r = _rms_fwd(x_ref[...])
        g = g_ref[...]
        h = (xn * g).astype(w_ref.dtype)
        dh = jnp.zeros((tm, d), F32)
        off = 0
        for ds_ref, wd in zip(ds_refs, segs):
            dsv = ds_ref[...].astype(w_ref.dtype)
            dh = dh + lax.dot_general(dsv, w_ref[:, off:off + wd], (((1,), (1,)), ((), ())), preferred_element_type=F32)
            dw_ref[:, off:off + wd] += lax.dot_general(h, dsv, (((0,), (0,)), ((), ())), preferred_element_type=F32)
            off += wd
        dg_ref[...] += jnp.sum(dh * xn, axis=0, keepdims=True)
        dx_ref[...] = dxr_ref[...] + _rms_bwd(dh * g, xn, r)

    return pl.pallas_call(
        body, name="in_proj_bwd", grid=(t // tm,),
        in_specs=[pl.BlockSpec((tm, d), lambda i: (i, 0)), pl.BlockSpec((1, d), lambda i: (0, 0)),
                  pl.BlockSpec((d, npk), lambda i: (0, 0))]
                 + [pl.BlockSpec((tm, wd), lambda i: (i, 0)) for wd in segs]
                 + [pl.BlockSpec((tm, d), lambda i: (i, 0))],
        out_specs=[pl.BlockSpec((tm, d), lambda i: (i, 0)), pl.BlockSpec((d, npk), lambda i: (0, 0)),
                   pl.BlockSpec((1, d), lambda i: (0, 0))],
        out_shape=[jax.ShapeDtypeStruct((t, d), F32), jax.ShapeDtypeStruct((d, npk), F32),
                   jax.ShapeDtypeStruct((1, d), F32)],
        compiler_params=_params("arbitrary"))(x, g1, wp, *dsegs, dx_res)


def _out_proj_fwd(x0, mix, wo, g2, tm):
    t, d = x0.shape
    dq = wo.shape[1]
    widths = [m.shape[1] for m in mix]

    def body(x_ref, *rest):
        m_refs = rest[:len(mix)]
        w_ref, g_ref, x1_ref, h2_ref = rest[len(mix):]
        acc = x_ref[...]
        off = 0
        for m_ref, wd in zip(m_refs, widths):
            for k in range(wd // dq):
                acc = acc + jnp.dot(m_ref[:, k * dq:(k + 1) * dq].astype(w_ref.dtype), w_ref[off // dq + k],
                                    preferred_element_type=F32)
            off += wd
        x1_ref[...] = acc
        xn, _ = _rms_fwd(acc)
        h2_ref[...] = (xn * g_ref[...]).astype(h2_ref.dtype)

    return pl.pallas_call(
        body, name="out_proj_fwd", grid=(t // tm,),
        in_specs=[pl.BlockSpec((tm, d), lambda i: (i, 0))]
                 + [pl.BlockSpec((tm, wd), lambda i: (i, 0)) for wd in widths]
                 + [pl.BlockSpec((4, dq, d), lambda i: (0, 0, 0)), pl.BlockSpec((1, d), lambda i: (0, 0))],
        out_specs=[pl.BlockSpec((tm, d), lambda i: (i, 0)), pl.BlockSpec((tm, d), lambda i: (i, 0))],
        out_shape=[jax.ShapeDtypeStruct((t, d), F32), jax.ShapeDtypeStruct((t, d), MM_DTYPE)],
        compiler_params=_params("arbitrary"))(x0, *mix, wo, g2)


def _out_proj_bwd(dx2, dh2, x1, g2, mix, wo, tm):
    t, d = x1.shape
    dq = wo.shape[1]
    widths = [m.shape[1] for m in mix]
    nm = len(mix)

    def body(dx2_ref, dh2_ref, x1_ref, g_ref, *rest):
        m_refs = rest[:nm]
        w_ref = rest[nm]
        dx1_ref = rest[nm + 1]
        dm_refs = rest[nm + 2:nm + 2 + nm]
        dw_ref, dg_ref = rest[nm + 2 + nm:]
        i = pl.program_id(0)

        @pl.when(i == 0)
        def _():
            dw_ref[...] = jnp.zeros_like(dw_ref)
            dg_ref[...] = jnp.zeros_like(dg_ref)

        xn, r = _rms_fwd(x1_ref[...])
        dh2v = dh2_ref[...]
        dg_ref[...] += jnp.sum(dh2v * xn, axis=0, keepdims=True)
        dx1 = dx2_ref[...] + _rms_bwd(dh2v * g_ref[...], xn, r)
        dx1_ref[...] = dx1
        dx1c = dx1.astype(w_ref.dtype)
        off = 0
        for m_ref, dm_ref, wd in zip(m_refs, dm_refs, widths):
            for k in range(wd // dq):
                j = off // dq + k
                cols = slice(k * dq, (k + 1) * dq)
                dm_ref[:, cols] = lax.dot_general(dx1c, w_ref[j], (((1,), (1,)), ((), ())), preferred_element_type=F32)
                dw_ref[j] += lax.dot_general(m_ref[:, cols].astype(w_ref.dtype), dx1c, (((0,), (0,)), ((), ())),
                                             preferred_element_type=F32)
            off += wd

    tile = lambda wd: pl.BlockSpec((tm, wd), lambda i: (i, 0))
    return pl.pallas_call(
        body, name="out_proj_bwd", grid=(t // tm,),
        in_specs=[tile(d), tile(d), tile(d), pl.BlockSpec((1, d), lambda i: (0, 0))]
                 + [tile(wd) for wd in widths] + [pl.BlockSpec((4, dq, d), lambda i: (0, 0, 0))],
        out_specs=[tile(d)] + [tile(wd) for wd in widths]
                  + [pl.BlockSpec((4, dq, d), lambda i: (0, 0, 0)), pl.BlockSpec((1, d), lambda i: (0, 0))],
        out_shape=[jax.ShapeDtypeStruct((t, d), F32)] + [jax.ShapeDtypeStruct((t, wd), F32) for wd in widths]
                  + [jax.ShapeDtypeStruct((4, dq, d), F32), jax.ShapeDtypeStruct((1, d), F32)],
        compiler_params=_params("arbitrary"))(dx2, dh2, x1, g2, *mix, wo)


def _ffn_fwd(x1, h2, wg, wu, wd, tm):
    t, d = x1.shape
    fs = wg.shape[2]

    def body(x1_ref, h2_ref, wg_ref, wu_ref, wd_ref, x2_ref, gp_ref, up_ref):
        @pl.when(pl.program_id(1) == 0)
        def _():
            x2_ref[...] = x1_ref[...]

        h = h2_ref[...]
        gp = jnp.dot(h, wg_ref[...], preferred_element_type=F32)
        up = jnp.dot(h, wu_ref[...], preferred_element_type=F32)
        gp_ref[...] = gp
        up_ref[...] = up
        ff = gp * _sigmoid(gp) * up
        x2_ref[...] += jnp.dot(ff.astype(wd_ref.dtype), wd_ref[...], preferred_element_type=F32)

    return pl.pallas_call(
        body, name="ffn_fwd", grid=(t // tm, 4),
        in_specs=[pl.BlockSpec((tm, d), lambda i, j: (i, 0)), pl.BlockSpec((tm, d), lambda i, j: (i, 0)),
                  pl.BlockSpec((None, d, fs), lambda i, j: (j, 0, 0)),
                  pl.BlockSpec((None, d, fs), lambda i, j: (j, 0, 0)),
                  pl.BlockSpec((None, fs, d), lambda i, j: (j, 0, 0))],
        out_specs=[pl.BlockSpec((tm, d), lambda i, j: (i, 0)), pl.BlockSpec((None, tm, fs), lambda i, j: (j, i, 0)),
                   pl.BlockSpec((None, tm, fs), lambda i, j: (j, i, 0))],
        out_shape=[jax.ShapeDtypeStruct((t, d), F32), jax.ShapeDtypeStruct((4, t, fs), F32),
                   jax.ShapeDtypeStruct((4, t, fs), F32)],
        compiler_params=_params("arbitrary", "arbitrary"))(x1, h2, wg, wu, wd)


def _ffn_bwd(dx2, h2, gp, up, wg, wu, wd, tm):
    t, d = dx2.shape
    fs = wg.shape[2]

    def body(dx2_ref, h2_ref, gp_ref, up_ref, wg_ref, wu_ref, wd_ref, dh2_ref, dwg_ref, dwu_ref, dwd_ref):
        j, i = pl.program_id(0), pl.program_id(1)

        @pl.when(i == 0)
        def _():
            dwg_ref[...] = jnp.zeros_like(dwg_ref)
            dwu_ref[...] = jnp.zeros_like(dwu_ref)
            dwd_ref[...] = jnp.zeros_like(dwd_ref)

        cdt = wg_ref.dtype
        h = h2_ref[...]
        gpv, upv = gp_ref[...], up_ref[...]
        s = _sigmoid(gpv)
        silu = gpv * s
        dx2c = dx2_ref[...].astype(cdt)
        dff = lax.dot_general(dx2c, wd_ref[...], (((1,), (1,)), ((), ())), preferred_element_type=F32)
        dwd_ref[...] += lax.dot_general((silu * upv).astype(cdt), dx2c, (((0,), (0,)), ((), ())), preferred_element_type=F32)
        dup = (dff * silu).astype(cdt)
        dgp = (dff * upv * _dsilu(gpv, s)).astype(cdt)
        dwg_ref[...] += lax.dot_general(h, dgp, (((0,), (0,)), ((), ())), preferred_element_type=F32)
        dwu_ref[...] += lax.dot_general(h, dup, (((0,), (0,)), ((), ())), preferred_element_type=F32)
        dh = (lax.dot_general(dgp, wg_ref[...], (((1,), (1,)), ((), ())), preferred_element_type=F32)
              + lax.dot_general(dup, wu_ref[...], (((1,), (1,)), ((), ())), preferred_element_type=F32))
        rows = pl.ds(pl.multiple_of(i * tm, tm), tm)

        @pl.when(j == 0)
        def _():
            dh2_ref[rows, :] = dh

        @pl.when(j != 0)
        def _():
            dh2_ref[rows, :] += dh

    return pl.pallas_call(
        body, name="ffn_bwd", grid=(4, t // tm),
        in_specs=[pl.BlockSpec((tm, d), lambda j, i: (i, 0)), pl.BlockSpec((tm, d), lambda j, i: (i, 0)),
                  pl.BlockSpec((None, tm, fs), lambda j, i: (j, i, 0)), pl.BlockSpec((None, tm, fs), lambda j, i: (j, i, 0)),
                  pl.BlockSpec((None, d, fs), lambda j, i: (j, 0, 0)),
                  pl.BlockSpec((None, d, fs), lambda j, i: (j, 0, 0)),
                  pl.BlockSpec((None, fs, d), lambda j, i: (j, 0, 0))],
        out_specs=[pl.BlockSpec((t, d), lambda j, i: (0, 0)), pl.BlockSpec((None, d, fs), lambda j, i: (j, 0, 0)),
                   pl.BlockSpec((None, d, fs), lambda j, i: (j, 0, 0)), pl.BlockSpec((None, fs, d), lambda j, i: (j, 0, 0))],
        out_shape=[jax.ShapeDtypeStruct((t, d), F32), jax.ShapeDtypeStruct((4, d, fs), F32),
                   jax.ShapeDtypeStruct((4, d, fs), F32), jax.ShapeDtypeStruct((4, fs, d), F32)],
        compiler_params=_params("arbitrary", "arbitrary"))(dx2, h2, gp, up, wg, wu, wd)


def _ple_fwd(x2, p, wpg, wpp, tm):
    t, d = x2.shape
    q = p.shape[1]
    dq = d // 4

    def body(x_ref, p_ref, wg_ref, wp_ref, o_ref):
        xv = x_ref[...]
        xc = xv.astype(wg_ref.dtype)
        pc = p_ref[...].astype(wp_ref.dtype)
        pre = jnp.dot(xc[:, :dq], wg_ref[0], preferred_element_type=F32)
        for j in range(1, 4):
            pre = pre + jnp.dot(xc[:, j * dq:(j + 1) * dq], wg_ref[j], preferred_element_type=F32)
        gate = _sigmoid(pre)
        for j in range(4):
            cols = slice(j * dq, (j + 1) * dq)
            o_ref[:, cols] = xv[:, cols] + gate[:, cols] * jnp.dot(pc, wp_ref[j], preferred_element_type=F32)

    return pl.pallas_call(
        body, name="ple_fwd", grid=(t // tm,),
        in_specs=[pl.BlockSpec((tm, d), lambda i: (i, 0)), pl.BlockSpec((tm, q), lambda i: (i, 0)),
                  pl.BlockSpec((4, dq, d), lambda i: (0, 0, 0)),
                  pl.BlockSpec((4, q, dq), lambda i: (0, 0, 0))],
        out_specs=pl.BlockSpec((tm, d), lambda i: (i, 0)),
        out_shape=jax.ShapeDtypeStruct((t, d), F32),
        compiler_params=_params("arbitrary"))(x2, p, wpg, wpp)


def _ple_bwd(dx3, x2, p, wpg, wpp, tm):
    t, d = x2.shape
    q = p.shape[1]
    dq = d // 4

    def body(dx3_ref, x_ref, p_ref, wg_ref, wp_ref, dx2_ref, dwg_ref, dwp_ref):
        @pl.when(pl.program_id(0) == 0)
        def _():
            dwg_ref[...] = jnp.zeros_like(dwg_ref)
            dwp_ref[...] = jnp.zeros_like(dwp_ref)

        cdt = wg_ref.dtype
        xc = x_ref[...].astype(cdt)
        pc = p_ref[...].astype(cdt)
        pre = jnp.dot(xc[:, :dq], wg_ref[0], preferred_element_type=F32)
        for j in range(1, 4):
            pre = pre + jnp.dot(xc[:, j * dq:(j + 1) * dq], wg_ref[j], preferred_element_type=F32)
        gate = _sigmoid(pre)
        dx3v = dx3_ref[...]
        dpp = (dx3v * gate).astype(cdt)
        dgate = dx3v * gate * (1.0 - gate)
        dpre_parts = []
        for j in range(4):
            cols = slice(j * dq, (j + 1) * dq)
            pp_j = jnp.dot(pc, wp_ref[j], preferred_element_type=F32)
            dpre_parts.append((dgate[:, cols] * pp_j).astype(cdt))
            dwp_ref[j] += lax.dot_general(pc, dpp[:, cols], (((0,), (0,)), ((), ())), preferred_element_type=F32)
        dpre = jnp.concatenate(dpre_parts, axis=1)
        for j in range(4):
            cols = slice(j * dq, (j + 1) * dq)
            dwg_ref[j] += lax.dot_general(xc[:, cols], dpre, (((0,), (0,)), ((), ())), preferred_element_type=F32)
            dx2_ref[:, cols] = dx3v[:, cols] + lax.dot_general(dpre, wg_ref[j], (((1,), (1,)), ((), ())),
                                                               preferred_element_type=F32)

    return pl.pallas_call(
        body, name="ple_bwd", grid=(t // tm,),
        in_specs=[pl.BlockSpec((tm, d), lambda i: (i, 0)), pl.BlockSpec((tm, d), lambda i: (i, 0)),
                  pl.BlockSpec((tm, q), lambda i: (i, 0)), pl.BlockSpec((4, dq, d), lambda i: (0, 0, 0)),
                  pl.BlockSpec((4, q, dq), lambda i: (0, 0, 0))],
        out_specs=[pl.BlockSpec((tm, d), lambda i: (i, 0)), pl.BlockSpec((4, dq, d), lambda i: (0, 0, 0)),
                   pl.BlockSpec((4, q, dq), lambda i: (0, 0, 0))],
        out_shape=[jax.ShapeDtypeStruct((t, d), F32), jax.ShapeDtypeStruct((4, dq, d), F32),
                   jax.ShapeDtypeStruct((4, q, dq), F32)],
        compiler_params=_params("arbitrary"))(dx3, x2, p, wpg, wpp)


def _loss_head(x, target, fg, tm):
    t, d = x.shape

    def body(x_ref, t_ref, g_ref, dx_ref, loss_ref, dg_ref):
        @pl.when(pl.program_id(0) == 0)
        def _():
            loss_ref[...] = jnp.zeros_like(loss_ref)
            dg_ref[...] = jnp.zeros_like(dg_ref)

        xn, r = _rms_fwd(x_ref[...])
        g = g_ref[...]
        err = xn * g - t_ref[...]
        loss_ref[...] += 0.5 * jnp.sum(jnp.sum(err * err, axis=-1, keepdims=True) / d, axis=0, keepdims=True)
        dy = err / d
        dg_ref[...] += jnp.sum(dy * xn, axis=0, keepdims=True)
        dx_ref[...] = _rms_bwd(dy * g, xn, r)

    return pl.pallas_call(
        body, name="loss_head", grid=(t // tm,),
        in_specs=[pl.BlockSpec((tm, d), lambda i: (i, 0)), pl.BlockSpec((tm, d), lambda i: (i, 0)),
                  pl.BlockSpec((1, d), lambda i: (0, 0))],
        out_specs=[pl.BlockSpec((tm, d), lambda i: (i, 0)), pl.BlockSpec((1, 1), lambda i: (0, 0)),
                   pl.BlockSpec((1, d), lambda i: (0, 0))],
        out_shape=[jax.ShapeDtypeStruct((t, d), F32), jax.ShapeDtypeStruct((1, 1), F32),
                   jax.ShapeDtypeStruct((1, d), F32)],
        compiler_params=_params("arbitrary"))(x, target, fg)


def _qkv_conv_act(xv, w, j, heads):
    k = QKV_CONV_WIDTH
    y = w[k - 1:k] * xv
    for s in range(1, k):
        y = y + w[k - 1 - s:k - s] * _shift_down(xv, s)
    sg = _sigmoid(y)
    s_act = y * sg
    nrm = lax.rsqrt(jnp.sum(s_act * s_act, axis=-1, keepdims=True) + EPS)
    scale = jnp.where(j < heads, HEAD_DIM ** -0.5, 1.0).astype(F32)
    return y, sg, s_act, nrm, scale


def _qkv_conv_fwd(qkv_pre, conv_w, heads):
    t = qkv_pre.shape[0]
    nblk = 3 * heads

    def body(x_ref, w_ref, o_ref):
        j = pl.program_id(0)
        _, _, s_act, nrm, scale = _qkv_conv_act(x_ref[...], w_ref[...], j, heads)
        o_ref[...] = jnp.where(j < 2 * heads, s_act * (nrm * scale), s_act)

    return pl.pallas_call(
        body, name="qkv_conv_fwd", grid=(nblk,),
        in_specs=[pl.BlockSpec((t, LANES), lambda j: (0, j)), pl.BlockSpec((QKV_CONV_WIDTH, LANES), lambda j: (0, j))],
        out_specs=pl.BlockSpec((t, LANES), lambda j: (0, j)),
        out_shape=jax.ShapeDtypeStruct(qkv_pre.shape, F32),
        compiler_params=_params("arbitrary"))(qkv_pre, conv_w)


def _qkv_conv_bwd(qkv_pre, conv_w, dqkv, heads):
    t = qkv_pre.shape[0]
    nblk = 3 * heads
    k = QKV_CONV_WIDTH

    def body(x_ref, w_ref, dn_ref, dx_ref, dw_ref):
        j = pl.program_id(0)
        xv, w = x_ref[...], w_ref[...]
        y, sg, s_act, nrm, scale = _qkv_conv_act(xv, w, j, heads)
        dn = dn_ref[...]
        dsn = dn * scale
        ds_qk = nrm * dsn - s_act * (nrm * nrm * nrm) * jnp.sum(dsn * s_act, axis=-1, keepdims=True)
        ds = jnp.where(j < 2 * heads, ds_qk, dn)
        dy = ds * _dsilu(y, sg)
        dx = w[k - 1:k] * dy
        dw_ref[k - 1:k, :] = jnp.sum(dy * xv, axis=0, keepdims=True)
        for s in range(1, k):
            dx = dx + w[k - 1 - s:k - s] * _shift_up(dy, s)
            dw_ref[k - 1 - s:k - s, :] = jnp.sum(dy * _shift_down(xv, s), axis=0, keepdims=True)
        dx_ref[...] = dx

    return pl.pallas_call(
        body, name="qkv_conv_bwd", grid=(nblk,),
        in_specs=[pl.BlockSpec((t, LANES), lambda j: (0, j)), pl.BlockSpec((k, LANES), lambda j: (0, j)),
                  pl.BlockSpec((t, LANES), lambda j: (0, j))],
        out_specs=[pl.BlockSpec((t, LANES), lambda j: (0, j)), pl.BlockSpec((k, LANES), lambda j: (0, j))],
        out_shape=[jax.ShapeDtypeStruct(qkv_pre.shape, F32), jax.ShapeDtypeStruct(conv_w.shape, F32)],
        compiler_params=_params("arbitrary"))(qkv_pre, conv_w, dqkv)


def _pool_windows(shape, j, group_dim):
    lane = lax.broadcasted_iota(jnp.int32, shape, 1) + j * LANES
    grp = lane // group_dim
    win = jnp.left_shift(2, grp).astype(F32)
    cnt = jnp.minimum((_rows(shape) + 1).astype(F32), win)
    return grp, cnt


def _pool_select(grp, levels):
    out = levels[0]
    for gi in range(1, POOL_GROUPS):
        out = jnp.where(grp == gi, levels[gi], out)
    return out


def _pool_mean(hv, grp, cnt):
    acc, levels, width = hv, [], 1
    for _ in range(POOL_GROUPS):
        acc = acc + _shift_down(acc, width)
        width *= 2
        levels.append(acc)
    return _pool_select(grp, levels) / cnt - hv


def _pool_fwd(hp, wbd, scale, group_dim):
    t, dp = hp.shape

    def body(h_ref, w_ref, s_ref, o_ref):
        hv = h_ref[...]
        grp, cnt = _pool_windows(hv.shape, pl.program_id(0), group_dim)
        pooled = _pool_mean(hv, grp, cnt)
        o_ref[...] = _mm(pooled, w_ref[...]) * s_ref[...]

    return pl.pallas_call(
        body, name="pool_fwd", grid=(dp // LANES,),
        in_specs=[pl.BlockSpec((t, LANES), lambda j: (0, j)), pl.BlockSpec((LANES, LANES), lambda j: (j, j)),
                  pl.BlockSpec((1, LANES), lambda j: (0, j))],
        out_specs=pl.BlockSpec((t, LANES), lambda j: (0, j)),
        out_shape=jax.ShapeDtypeStruct(hp.shape, F32),
        compiler_params=_params("arbitrary"))(hp, wbd, scale)


def _pool_bwd(hp, wbd, scale, dob, group_dim):
    t, dp = hp.shape

    def body(h_ref, w_ref, s_ref, do_ref, dh_ref, dw_ref, ds_ref):
        hv = h_ref[...]
        grp, cnt = _pool_windows(hv.shape, pl.program_id(0), group_dim)
        pooled = _pool_mean(hv, grp, cnt)
        wv = w_ref[...]
        dov = do_ref[...]
        ds_ref[...] = jnp.sum(dov * _mm(pooled, wv), axis=0, keepdims=True)
        dys = dov * s_ref[...]
        dw_ref[0] = _mm_tn(pooled, dys)
        dpooled = _mm_nt(dys, wv)
        acc, levels, width = dpooled / cnt, [], 1
        for _ in range(POOL_GROUPS):
            acc = acc + _shift_up(acc, width)
            width *= 2
            levels.append(acc)
        dh_ref[...] = _pool_select(grp, levels) - dpooled

    nb = dp // LANES
    return pl.pallas_call(
        body, name="pool_bwd", grid=(nb,),
        in_specs=[pl.BlockSpec((t, LANES), lambda j: (0, j)), pl.BlockSpec((LANES, LANES), lambda j: (j, j)),
                  pl.BlockSpec((1, LANES), lambda j: (0, j)), pl.BlockSpec((t, LANES), lambda j: (0, j))],
        out_specs=[pl.BlockSpec((t, LANES), lambda j: (0, j)), pl.BlockSpec((1, LANES, LANES), lambda j: (j, 0, 0)),
                   pl.BlockSpec((1, LANES), lambda j: (0, j))],
        out_shape=[jax.ShapeDtypeStruct(hp.shape, F32), jax.ShapeDtypeStruct((nb, LANES, LANES), F32),
                   jax.ShapeDtypeStruct((1, dp), F32)],
        compiler_params=_params("arbitrary"))(hp, wbd, scale, dob)


def _sconv_fwd(cbcch, w):
    t, dc3 = cbcch.shape
    nb = dc3 // 3 // LANES
    k = SCONV_WIDTH

    def body(b_ref, c_ref, h_ref, w_ref, o_ref):
        m = c_ref[...] * h_ref[...]
        wv = w_ref[...]
        y = wv[k - 1:k] * m
        for s in range(1, k):
            y = y + wv[k - 1 - s:k - s] * _shift_down(m, s)
        o_ref[...] = b_ref[...] * y

    return pl.pallas_call(
        body, name="sconv_fwd", grid=(nb,),
        in_specs=[pl.BlockSpec((t, LANES), lambda j: (0, j)), pl.BlockSpec((t, LANES), lambda j: (0, nb + j)),
                  pl.BlockSpec((t, LANES), lambda j: (0, 2 * nb + j)), pl.BlockSpec((k, LANES), lambda j: (0, j))],
        out_specs=pl.BlockSpec((t, LANES), lambda j: (0, j)),
        out_shape=jax.ShapeDtypeStruct((t, dc3 // 3), F32),
        compiler_params=_params("arbitrary"))(cbcch, cbcch, cbcch, w)


def _sconv_bwd(cbcch, w, doc):
    t, dc3 = cbcch.shape
    nb = dc3 // 3 // LANES
    k = SCONV_WIDTH

    def body(b_ref, c_ref, h_ref, w_ref, do_ref, db_ref, dc_ref, dh_ref, dw_ref):
        cv, hv = c_ref[...], h_ref[...]
        m = cv * hv
        wv = w_ref[...]
        dov = do_ref[...]
        dy = dov * b_ref[...]
        y = wv[k - 1:k] * m
        dm = wv[k - 1:k] * dy
        dw_ref[k - 1:k, :] = jnp.sum(dy * m, axis=0, keepdims=True)
        for s in range(1, k):
            ms = _shift_down(m, s)
            y = y + wv[k - 1 - s:k - s] * ms
            dm = dm + wv[k - 1 - s:k - s] * _shift_up(dy, s)
            dw_ref[k - 1 - s:k - s, :] = jnp.sum(dy * ms, axis=0, keepdims=True)
        db_ref[...] = dov * y
        dc_ref[...] = dm * hv
        dh_ref[...] = dm * cv

    col = lambda o: pl.BlockSpec((t, LANES), lambda j: (0, o * nb + j))
    return pl.pallas_call(
        body, name="sconv_bwd", grid=(nb,),
        in_specs=[col(0), col(1), col(2), pl.BlockSpec((k, LANES), lambda j: (0, j)), col(0)],
        out_specs=[col(0), col(0), col(0), pl.BlockSpec((k, LANES), lambda j: (0, j))],
        out_shape=[jax.ShapeDtypeStruct((t, dc3 // 3), F32)] * 3 + [jax.ShapeDtypeStruct(w.shape, F32)],
        compiler_params=_params("arbitrary"))(cbcch, cbcch, cbcch, w, doc)


class _Split:
    def __init__(self, a):
        self.hi = a.astype(jnp.bfloat16)
        self.lo = (a - self.hi.astype(F32)).astype(jnp.bfloat16)


def _per_head(dims, a, b):
    a = a if isinstance(a, _Split) else _Split(a)
    b = b if isinstance(b, _Split) else _Split(b)

    def dot(x, y):
        return lax.dot_general(x, y, (dims, ((), ())), preferred_element_type=F32)

    return jnp.stack([dot(a.hi[h], b.hi[h]) + (dot(a.hi[h], b.lo[h]) + dot(a.lo[h], b.hi[h])) for h in range(a.hi.shape[0])])


def _bmm(a, b):
    return _per_head(((1,), (0,)), a, b)


def _bmm_nt(a, b):
    return _per_head(((1,), (1,)), a, b)


def _bmm_tn(a, b):
    return _per_head(((0,), (0,)), a, b)


def _inv_unit_lower(low):
    c = low.shape[-1]
    eye = (_rows((c, c)) == lax.broadcasted_iota(jnp.int32, (c, c), 1)).astype(F32)
    pw = -low
    inv = eye + pw
    span = 2
    while span < c:
        pws = _Split(pw)
        pw = _bmm(pws, pws)
        inv = inv + _bmm(inv, pw)
        span *= 2
    return inv


def _heads_of(ref, base, heads):
    return jnp.stack([ref[:, base + h * HEAD_DIM:base + (h + 1) * HEAD_DIM] for h in range(heads)])


def _chunk_common(q, k, v, a_col, b_col, alog, dtb, kept=None):
    hn, c, _ = q.shape
    beta = _sigmoid(b_col)
    xg = a_col + dtb
    softplus = jnp.maximum(xg, 0.0) + jnp.log(1.0 + jnp.exp(-jnp.abs(xg)))
    neg_ea = -jnp.exp(alog)
    g = neg_ea * softplus
    ri = _rows((c, c))
    ci = lax.broadcasted_iota(jnp.int32, (c, c), 1)
    incl, strict = ri >= ci, ri > ci
    inclf = jnp.broadcast_to(incl.astype(F32), (hn, c, c))
    gcb = _bmm(inclf, jnp.broadcast_to(g, (hn, c, HEAD_DIM)))
    gc_row = jnp.sum(jnp.where(ri <= ci, jnp.broadcast_to(g, (hn, c, c)), 0.0), axis=1, keepdims=True)
    dmat = jnp.where(incl, jnp.exp(jnp.where(incl, gcb[:, :, :1] - gc_row, 0.0)), 0.0)
    eg = jnp.exp(gcb)
    gl = gcb[:, c - 1:c, :]
    egl = jnp.exp(gl)
    edl = jnp.exp(gl - gcb)
    kb, vb = k * beta, v * beta
    kbe = kb * eg
    if kept is None:
        ks = _Split(k)
        a0 = _bmm_nt(kb, ks)
        tm = _inv_unit_lower(jnp.where(strict, a0 * dmat, 0.0))
        p0 = _bmm_nt(q, ks)
        tms = _Split(tm)
        u, w = _bmm(tms, vb), _bmm(tms, kbe)
    else:
        (a0, tm, p0, w), u = kept, None
    return dict(beta=beta, xg=xg, neg_ea=neg_ea, g=g, incl=incl, strict=strict, inclf=inclf, dmat=dmat, eg=eg,
                egl=egl, edl=edl, kb=kb, vb=vb, a0=a0, tm=tm, kbe=kbe, u=u, w=w, p0=p0,
                attn=p0 * dmat, qe=q * eg, kd=k * edl)


def _chunk_step(cm, state):
    ss = _Split(state)
    vn = cm["u"] - _bmm(cm["w"], ss)
    vns = _Split(vn)
    o = _bmm(cm["qe"], ss) + _bmm(cm["attn"], vns)
    new_state = state * cm["egl"][:, :, :1] + _bmm_tn(cm["kd"], vns)
    return vn, o, new_state


def _gated_norm(o, zv, og):
    xo, ro = _rms_fwd(o)
    sgz = _sigmoid(zv)
    return xo, ro, sgz, xo * og * (zv * sgz)


def _gate_columns(abv, gpv, heads):
    a_col = jnp.stack([abv[:, h:h + 1] for h in range(heads)])
    b_col = jnp.stack([abv[:, heads + h:heads + h + 1] for h in range(heads)])
    alog = jnp.stack([gpv[0:1, h:h + 1] for h in range(heads)])
    dtb = jnp.stack([gpv[1:2, h:h + 1] for h in range(heads)])
    return a_col, b_col, alog, dtb


def _delta_fwd(qkv, z, ab, gpar, heads):
    t = qkv.shape[0]
    da = heads * HEAD_DIM
    n = t // CHUNK

    def body(qkv_ref, z_ref, ab_ref, gp_ref, oa_ref, st_ref, kc_ref, kw_ref, s_ref):
        @pl.when(pl.program_id(0) == 0)
        def _():
            s_ref[...] = jnp.zeros_like(s_ref)

        gpv = gp_ref[...]
        cm = _chunk_common(_heads_of(qkv_ref, 0, heads), _heads_of(qkv_ref, da, heads), _heads_of(qkv_ref, 2 * da, heads),
                           *_gate_columns(ab_ref[...], gpv, heads))
        state = s_ref[...]
        st_ref[0] = state
        vn, o, new_state = _chunk_step(cm, state)
        s_ref[...] = new_state
        for slot, val in enumerate((cm["a0"], cm["tm"], cm["p0"])):
            kc_ref[0, slot] = val
        for slot, val in enumerate((cm["w"], vn, o)):
            kw_ref[0, slot] = val
        oa = _gated_norm(o, _heads_of(z_ref, 0, heads), gpv[2:3, :])[3]
        for h in range(heads):
            oa_ref[:, h * HEAD_DIM:(h + 1) * HEAD_DIM] = oa[h]

    return pl.pallas_call(
        body, name="delta_fwd", grid=(n,),
        in_specs=[pl.BlockSpec((CHUNK, 3 * da), lambda i: (i, 0)), pl.BlockSpec((CHUNK, da), lambda i: (i, 0)),
                  pl.BlockSpec((CHUNK, LANES), lambda i: (i, 0)), pl.BlockSpec((8, LANES), lambda i: (0, 0))],
        out_specs=[pl.BlockSpec((CHUNK, da), lambda i: (i, 0)),
                   pl.BlockSpec((1, heads, HEAD_DIM, HEAD_DIM), lambda i: (i, 0, 0, 0)),
                   pl.BlockSpec((1, 3, heads, CHUNK, CHUNK), lambda i: (i, 0, 0, 0, 0)),
                   pl.BlockSpec((1, 3, heads, CHUNK, HEAD_DIM), lambda i: (i, 0, 0, 0, 0))],
        out_shape=[jax.ShapeDtypeStruct((t, da), F32), jax.ShapeDtypeStruct((n, heads, HEAD_DIM, HEAD_DIM), F32),
                   jax.ShapeDtypeStruct((n, 3, heads, CHUNK, CHUNK), F32),
                   jax.ShapeDtypeStruct((n, 3, heads, CHUNK, HEAD_DIM), F32)],
        scratch_shapes=[pltpu.VMEM((heads, HEAD_DIM, HEAD_DIM), F32)],
        compiler_params=_params("arbitrary"))(qkv, z, ab, gpar)


def _delta_bwd(qkv, z, ab, gpar, states, kept_c, kept_w, doa, heads):
    t = qkv.shape[0]
    da = heads * HEAD_DIM
    n = t // CHUNK
    c = CHUNK

    def body(qkv_ref, z_ref, ab_ref, gp_ref, st_ref, kc_ref, kw_ref, doa_ref, dqkv_ref, dz_ref, dab_ref, dpar_ref, ds_ref):
        @pl.when(pl.program_id(0) == 0)
        def _():
            ds_ref[...] = jnp.zeros_like(ds_ref)
            dpar_ref[...] = jnp.zeros_like(dpar_ref)

        gpv = gp_ref[...]
        og = gpv[2:3, :]
        q, k, v = _heads_of(qkv_ref, 0, heads), _heads_of(qkv_ref, da, heads), _heads_of(qkv_ref, 2 * da, heads)
        cm = _chunk_common(q, k, v, *_gate_columns(ab_ref[...], gpv, heads),
                           kept=(kc_ref[0, 0], kc_ref[0, 1], kc_ref[0, 2], kw_ref[0, 0]))
        state = st_ref[0]
        dsp = ds_ref[...]
        vn, o = kw_ref[0, 1], kw_ref[0, 2]
        zv = _heads_of(z_ref, 0, heads)
        xo, ro, sgz, _ = _gated_norm(o, zv, og)
        doav = _heads_of(doa_ref, 0, heads)
        don = doav * (zv * sgz)
        dz = doav * (xo * og) * _dsilu(zv, sgz)
        d_og = jnp.sum(jnp.sum(don * xo, axis=1, keepdims=True), axis=0)
        do = _rms_bwd(don * og, xo, ro)
        tm, dmat, eg, edl, egl = cm["tm"], cm["dmat"], cm["eg"], cm["edl"], cm["egl"]
        dos, dsps, sts, tms, ks = _Split(do), _Split(dsp), _Split(state), _Split(tm), _Split(k)
        dvn = _bmm_tn(cm["attn"], dos) + _bmm(cm["kd"], dsps)
        dvns = _Split(dvn)
        dqe = _bmm_nt(dos, sts)
        ds_ref[...] = _bmm_tn(cm["qe"], dos) + dsp * egl[:, :, :1] - _bmm_tn(cm["w"], dvns)
        dattn = _bmm_nt(dos, vn)
        dkd = _bmm_nt(vn, dsps)
        dkd_kd = jnp.sum(dkd * cm["kd"], axis=-1, keepdims=True)
        dgl = (jnp.sum(jnp.sum(dsp * state, axis=-1, keepdims=True), axis=1, keepdims=True) * egl[:, :, :1]
               + jnp.sum(dkd_kd, axis=1, keepdims=True))
        dgc = jnp.sum(dqe * cm["qe"], axis=-1, keepdims=True) - dkd_kd
        dk = dkd * edl
        dq = dqe * eg
        dw = -_bmm_nt(dvns, sts)
        dws = _Split(dw)
        dp0 = dattn * dmat
        dd = jnp.where(cm["incl"], dattn * cm["p0"], 0.0)
        dp0s = _Split(dp0)
        dq = dq + _bmm(dp0s, ks)
        dk = dk + _bmm_tn(dp0s, q)
        dtm = _bmm_nt(dvns, cm["vb"]) + _bmm_nt(dws, cm["kbe"])
        dvb = _bmm_tn(tms, dvns)
        dkbe = _bmm_tn(tms, dws)
        dkb = dkbe * eg
        dgc = dgc + jnp.sum(dkbe * cm["kbe"], axis=-1, keepdims=True)
        dlow = jnp.where(cm["strict"], -_bmm_tn(tms, _bmm_nt(dtm, tms)), 0.0)
        dd = dd + dlow * cm["a0"]
        da0 = dlow * dmat
        da0s = _Split(da0)
        dkb = dkb + _bmm(da0s, ks)
        dk = dk + _bmm_tn(da0s, cm["kb"])
        ddd = dd * dmat
        ones = jnp.ones((heads, c, HEAD_DIM), F32)
        dgc = dgc + jnp.sum(ddd, axis=-1, keepdims=True) - _bmm_tn(ddd, ones)[:, :, :1]
        dgc = dgc + jnp.where(_rows((c, 1)) == c - 1, dgl, 0.0)
        dg = _bmm_tn(cm["inclf"], jnp.broadcast_to(dgc, (heads, c, HEAD_DIM)))[:, :, :1]
        beta = cm["beta"]
        dk = dk + dkb * beta
        dbeta = jnp.sum(dkb * k, axis=-1, keepdims=True) + jnp.sum(dvb * v, axis=-1, keepdims=True)
        dv = dvb * beta
        db_col = dbeta * beta * (1.0 - beta)
        da_col = dg * cm["neg_ea"] * _sigmoid(cm["xg"])
        d_alog = jnp.sum(dg * cm["g"], axis=1, keepdims=True)
        d_dtb = jnp.sum(da_col, axis=1, keepdims=True)
        lane = lax.broadcasted_iota(jnp.int32, (c, LANES), 1)
        lane8 = lax.broadcasted_iota(jnp.int32, (8, LANES), 1)
        row8 = _rows((8, LANES))
        dab = jnp.zeros((c, LANES), F32)
        dpar = jnp.where(row8 == 2, d_og, 0.0)
        for h in range(heads):
            lo = h * HEAD_DIM
            dqkv_ref[:, lo:lo + HEAD_DIM] = dq[h]
            dqkv_ref[:, da + lo:da + lo + HEAD_DIM] = dk[h]
            dqkv_ref[:, 2 * da + lo:2 * da + lo + HEAD_DIM] = dv[h]
            dz_ref[:, lo:lo + HEAD_DIM] = dz[h]
            dab = dab + jnp.where(lane == h, da_col[h], 0.0) + jnp.where(lane == heads + h, db_col[h], 0.0)
            dpar = (dpar + jnp.where((row8 == 0) & (lane8 == h), d_alog[h], 0.0)
                    + jnp.where((row8 == 1) & (lane8 == h), d_dtb[h], 0.0))
        dab_ref[...] = dab
        dpar_ref[...] += dpar

    rev = lambda i: (n - 1 - i, 0)
    return pl.pallas_call(
        body, name="delta_bwd", grid=(n,),
        in_specs=[pl.BlockSpec((c, 3 * da), rev), pl.BlockSpec((c, da), rev), pl.BlockSpec((c, LANES), rev),
                  pl.BlockSpec((8, LANES), lambda i: (0, 0)),
                  pl.BlockSpec((1, heads, HEAD_DIM, HEAD_DIM), lambda i: (n - 1 - i, 0, 0, 0)),
                  pl.BlockSpec((1, 3, heads, c, c), lambda i: (n - 1 - i, 0, 0, 0, 0)),
                  pl.BlockSpec((1, 3, heads, c, HEAD_DIM), lambda i: (n - 1 - i, 0, 0, 0, 0)),
                  pl.BlockSpec((c, da), rev)],
        out_specs=[pl.BlockSpec((c, 3 * da), rev), pl.BlockSpec((c, da), rev), pl.BlockSpec((c, LANES), rev),
                   pl.BlockSpec((8, LANES), lambda i: (0, 0))],
        out_shape=[jax.ShapeDtypeStruct((t, 3 * da), F32), jax.ShapeDtypeStruct((t, da), F32),
                   jax.ShapeDtypeStruct((t, LANES), F32), jax.ShapeDtypeStruct((8, LANES), F32)],
        scratch_shapes=[pltpu.VMEM((heads, HEAD_DIM, HEAD_DIM), F32)],
        compiler_params=_params("arbitrary"))(qkv, z, ab, gpar, states, kept_c, kept_w, doa)


def _w_in_pieces(shard_cols, da, heads):
    a0, nab = 4 * da, 2 * heads
    d_in = 4 * shard_cols
    runs = [(0, a0, 0), (a0, a0 + nab, d_in - nab), (a0 + nab, d_in, a0)]
    pieces = []
    for j in range(4):
        lo, hi = j * shard_cols, (j + 1) * shard_cols
        for rlo, rhi, plo in runs:
            s, e = max(lo, rlo), min(hi, rhi)
            if s < e:
                pieces.append((j, s - lo, e - s, plo + (s - rlo)))
    return pieces, d_in - nab + LANES


def _w_in_pack(w4, da, heads):
    _, d, sc = w4.shape
    pieces, npk = _w_in_pieces(sc, da, heads)
    tr = _tile_rows(d, 256, SUBLANES_WIRE)

    def body(w_ref, o_ref):
        o_ref[:, npk - LANES:] = jnp.zeros((tr, LANES), o_ref.dtype)
        for j, lo, ln, dst in pieces:
            o_ref[:, dst:dst + ln] = w_ref[j, :, lo:lo + ln]

    return pl.pallas_call(
        body, name="w_in_pack", grid=(d // tr,),
        in_specs=[pl.BlockSpec((4, tr, sc), lambda i: (0, i, 0))],
        out_specs=pl.BlockSpec((tr, npk), lambda i: (i, 0)),
        out_shape=jax.ShapeDtypeStruct((d, npk), w4.dtype),
        compiler_params=_params("arbitrary"))(w4)


def _w_in_unpack(dwp, sc, da, heads):
    d, npk = dwp.shape
    pieces, _ = _w_in_pieces(sc, da, heads)
    tr = _tile_rows(d, 256)

    def body(g_ref, o_ref):
        for j, lo, ln, dst in pieces:
            o_ref[j, :, lo:lo + ln] = g_ref[:, dst:dst + ln]

    return pl.pallas_call(
        body, name="w_in_unpack", grid=(d // tr,),
        in_specs=[pl.BlockSpec((tr, npk), lambda i: (i, 0))],
        out_specs=pl.BlockSpec((4, tr, sc), lambda i: (0, i, 0)),
        out_shape=jax.ShapeDtypeStruct((4, d, sc), F32),
        compiler_params=_params("arbitrary"))(dwp)


def _block_diag(pool_w):
    g, gd, _ = pool_w.shape
    out = jnp.zeros((g * gd, g * gd), pool_w.dtype)
    for gi in range(g):
        out = lax.dynamic_update_slice(out, pool_w[gi], (gi * gd, gi * gd))
    return out


def _layer_dims(d):
    heads = (d // 2) // HEAD_DIM
    return heads, heads * HEAD_DIM, d // 4, d // 4


BIG = ("w_in", "w_gate", "w_up", "ple_proj", "w_out", "w_down", "ple_gate")


def _prepare_layer(small, li):
    d = small["norm1_g"].shape[1]
    heads, _, _, _ = _layer_dims(d)
    gpar = jnp.zeros((8, LANES), F32)
    gpar = gpar.at[0, :heads].set(small["a_log"][li]).at[1, :heads].set(small["dt_bias"][li]).at[2, :].set(small["onorm_g"][li])
    return dict(norm1_g=small["norm1_g"][li][None], conv_qkv=small["conv_qkv"][li], gpar=gpar, pool_bd=_block_diag(small["pool_w"][li]).astype(MM_DTYPE),
                pool_scale=small["pool_scale"][li][None], sconv_w=small["sconv_w"][li], norm2_g=small["norm2_g"][li][None])


def _layer_fwd(x0, p, gw, lw, tm, arrive):
    d = x0.shape[1]
    heads, da, dp, dc = _layer_dims(d)
    segs = (3 * da, da, dp, 3 * dc, LANES)
    lw["w_in_p"] = _w_in_pack(gw["w_in"], da, heads).astype(MM_DTYPE)
    qkv_pre, z, hp, cbcch, ab = _in_proj_fwd(x0, lw["norm1_g"], lw["w_in_p"], segs, tm)
    qkv = _qkv_conv_fwd(qkv_pre, lw["conv_qkv"], heads)
    oa, states, kept_c, kept_w = _delta_fwd(qkv, z, ab, lw["gpar"], heads)
    ob = _pool_fwd(hp, lw["pool_bd"], lw["pool_scale"], dp // POOL_GROUPS)
    oc = _sconv_fwd(cbcch, lw["sconv_w"])
    arrive("mixed", oa)
    x1, h2 = _out_proj_fwd(x0, (oa, ob, oc), gw["w_out"], lw["norm2_g"], tm)
    x2, gp, up = _ffn_fwd(x1, h2, gw["w_gate"], gw["w_up"], gw["w_down"], tm)
    arrive("ffn", x2)
    x3 = _ple_fwd(x2, p, gw["ple_gate"], gw["ple_proj"], tm)
    arrive("end", x3)
    saved = dict(x0=x0, qkv_pre=qkv_pre, z=z, hp=hp, cbcch=cbcch, ab=ab, qkv=qkv, states=states, kept_c=kept_c, kept_w=kept_w, oa=oa, ob=ob, oc=oc,
                 x1=x1, h2=h2, gp=gp, up=up, x2=x2)
    return x3, saved


def _layer_bwd(dx3, p, gw, lw, sv, tm, produced):
    def after_token(tok, arr):
        return arr if tok is None else arr + tok[0, 0]

    d = dx3.shape[1]
    heads, da, dp, dc = _layer_dims(d)
    segs = (3 * da, da, dp, dc, dc, dc, LANES)
    gd = dp // POOL_GROUPS
    dx2, d_ple_gate, d_ple_proj = _ple_bwd(dx3, sv["x2"], p, gw["ple_gate"], gw["ple_proj"], tm)
    dh2, d_w_gate, d_w_up, d_w_down = _ffn_bwd(dx2, sv["h2"], sv["gp"], sv["up"], gw["w_gate"], gw["w_up"], gw["w_down"],
                                               min(tm, 256))
    tok = produced("ffn", dict(w_gate=d_w_gate, w_up=d_w_up, ple_proj=d_ple_proj, w_down=d_w_down, ple_gate=d_ple_gate), dh2)
    dx1, doa, dob, doc, d_w_out, d_norm2 = _out_proj_bwd(dx2, dh2, sv["x1"], after_token(tok, lw["norm2_g"]),
                                                         (sv["oa"], sv["ob"], sv["oc"]), gw["w_out"], tm)
    dcb, dcc, dch, d_sconv = _sconv_bwd(sv["cbcch"], lw["sconv_w"], doc)
    dhp, d_pool_bd, d_pool_scale = _pool_bwd(sv["hp"], lw["pool_bd"], lw["pool_scale"], dob, gd)
    dqkv, dz, dab, dpar = _delta_bwd(sv["qkv"], sv["z"], sv["ab"], lw["gpar"], sv["states"], sv["kept_c"], sv["kept_w"], doa,
                                      heads)
    tok = produced("mixers", {}, dqkv)
    dqkv_pre, d_conv_qkv = _qkv_conv_bwd(sv["qkv_pre"], lw["conv_qkv"], dqkv, heads)
    dsegs = (dqkv_pre, dz, dhp, dcb, dcc, dch, dab)
    dx0, d_w_in_p, d_norm1 = _in_proj_bwd(sv["x0"], after_token(tok, lw["norm1_g"]), lw["w_in_p"], dsegs, dx1, segs, tm)
    per = LANES // gd
    bd = d_pool_bd.reshape(dp // LANES, per, gd, per, gd)
    d_pool_w = jnp.stack([bd[gi // per, gi % per, :, gi % per, :] for gi in range(POOL_GROUPS)])
    big = dict(w_in=_w_in_unpack(d_w_in_p, gw["w_in"].shape[2], da, heads), w_gate=d_w_gate, w_up=d_w_up,
               ple_proj=d_ple_proj, w_out=d_w_out, w_down=d_w_down, ple_gate=d_ple_gate)
    small = dict(norm1_g=d_norm1[0], conv_qkv=d_conv_qkv, a_log=dpar[0, :heads], dt_bias=dpar[1, :heads], onorm_g=dpar[2],
                 pool_w=d_pool_w, pool_scale=d_pool_scale[0], sconv_w=d_sconv, norm2_g=d_norm2[0])
    tok = produced("end", dict(w_in=big["w_in"], w_out=d_w_out), big["w_in"])
    return dx0, big, small, tok


def _local_step(x, p, target, gw, small, produced=None, arrive=None):
    t, d = x.shape
    depth = p.shape[0]
    tm = 512 if t % 512 == 0 else 128
    layers = [_prepare_layer(small, li) for li in range(depth)]
    saved = []
    h = x
    for li in range(depth):
        h, sv = _layer_fwd(h, p[li], gw[li], layers[li], tm,
                           (lambda stage, after, li=li: arrive(li, stage, after)) if arrive else (lambda stage, after: None))
        saved.append(sv)
    dx, loss, d_final = _loss_head(h, target, small["final_g"][None], tm)
    big, sm = [None] * depth, [None] * depth
    token = None
    for li in reversed(range(depth)):
        p_li = p[li] if token is None else p[li] + token[0, 0]
        dx, big[li], sm[li], token = _layer_bwd(
            dx, p_li, gw[li], layers[li], saved[li], tm,
            (lambda stage, grads, after, li=li: produced(li, stage, grads, after)) if produced else (lambda *a: None))
    small_grads = {n: jnp.stack([g[n] for g in sm]) for n in sm[0]}
    small_grads["final_g"] = d_final[0]
    return loss[0, 0], dx, big, small_grads


def _coords():
    return lax.axis_index("x"), lax.axis_index("y"), lax.axis_index("c")


def _other_chips(x, y):
    return [(1 - x, y), (x, 1 - y), (1 - x, 1 - y)]


def _place_shards(ws, me_idx):
    nt = len(ws)
    depth = ws[0].shape[0]

    def body(me_ref, *refs):
        for t, w_ref in enumerate(refs[:nt]):
            for li in range(depth):
                refs[nt + li * nt + t][...] = w_ref[li].astype(WIRE_DTYPE)

    outs = pl.pallas_call(
        body, name="place_shards",
        grid_spec=pltpu.PrefetchScalarGridSpec(
            num_scalar_prefetch=1, grid=(4,),
            in_specs=[pl.BlockSpec((depth, w.shape[1] // 4, w.shape[2]), lambda i, me_ref: (0, i, 0)) for w in ws],
            out_specs=[pl.BlockSpec((None, w.shape[1] // 4, w.shape[2]), lambda i, me_ref: (me_ref[0], i, 0))
                       for _ in range(depth) for w in ws]),
        out_shape=[jax.ShapeDtypeStruct((4,) + w.shape[1:], WIRE_DTYPE) for _ in range(depth) for w in ws],
        compiler_params=_params("arbitrary"))(me_idx, *ws)
    return [list(outs[li * nt:(li + 1) * nt]) for li in range(depth)]


def _half_block(ref, chip, pc):
    rh = ref.shape[1] // 2
    return ref.at[chip, pl.ds(pc * rh, rh)]


def _gather_copies(out_refs, send_sems, recv_sems, stage):
    nt = len(out_refs)
    x, y, c = _coords()
    pairs = []
    for j, (cx, cy) in enumerate(_other_chips(x, y)):
        for t in range(nt):
            sems = dict(send_sem=send_sems[j * nt + t], recv_sem=recv_sems[j * nt + t], device_id_type=MESH)
            if stage == 0:
                mine, theirs, to = _half_block(out_refs[t], 2 * x + y, c), _half_block(out_refs[t], 2 * cx + cy, c), (cx, cy, c)
            else:
                mine, theirs, to = (_half_block(out_refs[t], 2 * cx + cy, c), _half_block(out_refs[t], 2 * cx + cy, 1 - c),
                                    (x, y, 1 - c))
            pairs.append((pltpu.make_async_remote_copy(src_ref=mine, dst_ref=mine, device_id=to, **sems),
                          pltpu.make_async_remote_copy(src_ref=theirs, dst_ref=theirs, device_id=to, **sems)))
    return pairs


def _all_gather_chips(placed):
    nt = len(placed)

    def body(*refs):
        out_refs = refs[nt:2 * nt]
        send_sems, recv_sems = refs[2 * nt:]
        nc = 3 * nt
        first = _gather_copies(out_refs, [send_sems.at[k] for k in range(nc)], [recv_sems.at[k] for k in range(nc)], 0)
        passed = _gather_copies(out_refs, [send_sems.at[nc + k] for k in range(nc)], [recv_sems.at[nc + k] for k in range(nc)], 1)
        for start, _ in first:
            start.start()
        for (_, arrival), (forward, _) in zip(first, passed):
            arrival.wait_recv()
            forward.start()
        for _, arrival in passed:
            arrival.wait_recv()
        for start, _ in first + passed:
            start.wait_send()

    return pl.pallas_call(
        body, name="all_gather_chips", out_shape=[jax.ShapeDtypeStruct(a.shape, a.dtype) for a in placed],
        in_specs=[ANY] * nt, out_specs=[ANY] * nt, input_output_aliases={t: t for t in range(nt)},
        scratch_shapes=[pltpu.SemaphoreType.DMA((6 * nt,)), pltpu.SemaphoreType.DMA((6 * nt,))],
    )(*placed)


def _gather_call(name, arrs, wait_sems, after, stage):
    nt = len(arrs)
    nc = 3 * nt
    n_wait = len(wait_sems)
    n_new = 2 * nc if stage < 2 else 0
    arrs = [pltpu.with_memory_space_constraint(a, pltpu.HBM) for a in arrs]

    def body(*refs):
        a_refs = refs[:nt]
        waits = refs[nt:nt + n_wait]
        news = refs[nt + n_wait + 1:nt + n_wait + 1 + n_new]
        token = refs[-1]
        if stage > 0:
            for start, arrival in _gather_copies(a_refs, waits[:nc], waits[nc:], stage - 1):
                start.wait_send()
                arrival.wait_recv()
        if stage < 2:
            for start, _ in _gather_copies(a_refs, news[:nc], news[nc:], stage):
                start.start()
        token[...] = jnp.zeros_like(token)

    outs = pl.pallas_call(
        body, name=name,
        out_shape=(*[pltpu.SemaphoreType.DMA(())] * n_new, *[pltpu.HBM(a.shape, a.dtype) for a in arrs],
                   jax.ShapeDtypeStruct((8, LANES), F32)),
        in_specs=[HBM] * nt + [SEM] * n_wait + [ANY],
        out_specs=(*[SEM] * n_new, *[HBM] * nt, pl.BlockSpec(memory_space=pltpu.VMEM)),
        input_output_aliases={t: n_new + t for t in range(nt)},
        compiler_params=pltpu.CompilerParams(has_side_effects=pltpu.SideEffectType.DATAFLOW_SIDE_EFFECTING),
    )(*arrs, *wait_sems, after)
    return list(outs[:n_new]), list(outs[n_new:n_new + nt]), outs[-1]


def _sibling_swap_half(gs):
    nt = len(gs)

    def body(*refs):
        g_refs, out_refs = refs[:nt], refs[nt:2 * nt]
        send_sems, recv_sems = refs[2 * nt:]
        x, y, c = _coords()
        cps = []
        for t in range(nt):
            rh = g_refs[t].shape[1] // 2
            cps.append(pltpu.make_async_remote_copy(src_ref=g_refs[t].at[:, pl.ds((1 - c) * rh, rh)], dst_ref=out_refs[t],
                                                    send_sem=send_sems.at[t], recv_sem=recv_sems.at[t], device_id=(x, y, 1 - c),
                                                    device_id_type=MESH))
        for cp in cps:
            cp.start()
        for cp in cps:
            cp.wait()

    return pl.pallas_call(
        body, name="sibling_swap_half",
        out_shape=[jax.ShapeDtypeStruct((g.shape[0], g.shape[1] // 2, g.shape[2]), g.dtype) for g in gs],
        in_specs=[ANY] * nt, out_specs=[ANY] * nt,
        scratch_shapes=[pltpu.SemaphoreType.DMA((nt,)), pltpu.SemaphoreType.DMA((nt,))])(*gs)


def _add_my_halves(gs, others, c_idx):
    nt = len(gs)

    def body(c_ref, *refs):
        for g_ref, o_ref, out_ref in zip(refs[:nt], refs[nt:2 * nt], refs[2 * nt:]):
            out_ref[...] = (g_ref[...].astype(F32) + o_ref[...].astype(F32)).astype(out_ref.dtype)

    def quarter(g):
        return pl.BlockSpec((None, g.shape[1] // 4, g.shape[2]), lambda j, i, c_ref: (j, i, 0))

    return pl.pallas_call(
        body, name="add_my_halves",
        grid_spec=pltpu.PrefetchScalarGridSpec(
            num_scalar_prefetch=1, grid=(4, 2),
            in_specs=[pl.BlockSpec((None, g.shape[1] // 4, g.shape[2]), lambda j, i, c_ref: (j, 2 * c_ref[0] + i, 0)) for g in gs]
                     + [quarter(g) for g in gs],
            out_specs=[quarter(g) for g in gs]),
        out_shape=[jax.ShapeDtypeStruct((4, g.shape[1] // 2, g.shape[2]), WIRE_DTYPE) for g in gs],
        compiler_params=_params("arbitrary", "arbitrary"))(c_idx, *gs, *others)


def _exchange_chips(parts):
    nt = len(parts)

    def body(*refs):
        p_refs, out_refs = refs[:nt], refs[nt:2 * nt]
        send_sems, recv_sems = refs[2 * nt:]
        x, y, c = _coords()
        chips = _other_chips(x, y)

        def copy(j, t):
            cx, cy = chips[j]
            return pltpu.make_async_remote_copy(src_ref=p_refs[t].at[2 * cx + cy], dst_ref=out_refs[t].at[j],
                                                send_sem=send_sems.at[j, t], recv_sem=recv_sems.at[j, t], device_id=(cx, cy, c),
                                                device_id_type=MESH)

        sends = [copy(j, t) for j in range(3) for t in range(nt)]
        for cp in sends:
            cp.start()
        for cp in sends:
            cp.wait_recv()
        for cp in sends:
            cp.wait_send()

    return pl.pallas_call(
        body, name="exchange_chips", out_shape=[jax.ShapeDtypeStruct((3,) + p.shape[1:], p.dtype) for p in parts],
        in_specs=[ANY] * nt, out_specs=[ANY] * nt,
        scratch_shapes=[pltpu.SemaphoreType.DMA((3, nt)), pltpu.SemaphoreType.DMA((3, nt))])(*parts)


def _split_plan(kind, s_refs, l_refs):
    x, y, c = _coords()
    if kind == "swap":
        return [(s.at[:, pl.ds((1 - c) * (s.shape[1] // 2), s.shape[1] // 2)], l, (x, y, 1 - c)) for s, l in zip(s_refs, l_refs)]
    return [(s.at[2 * cx + cy], l.at[j], (cx, cy, c)) for j, (cx, cy) in enumerate(_other_chips(x, y))
            for s, l in zip(s_refs, l_refs)]


def _split_landing(kind, a):
    return (a.shape[0], a.shape[1] // 2, a.shape[2]) if kind == "swap" else (3,) + a.shape[1:]


def _copies_start(name, kind, srcs, after=None):
    ns = len(srcs)
    n = ns if kind == "swap" else 3 * ns
    srcs = [pltpu.with_memory_space_constraint(a, pltpu.HBM) for a in srcs]
    lands = [pltpu.with_memory_space_constraint(lax.empty(_split_landing(kind, a), a.dtype), pltpu.HBM) for a in srcs]
    extra = [] if after is None else [after]

    def body(*refs):
        first_sem = 2 * ns + len(extra)
        sems, token = refs[first_sem:first_sem + 2 * n], refs[-1]
        for k, (src, dst, dev) in enumerate(_split_plan(kind, refs[:ns], refs[ns:2 * ns])):
            pltpu.make_async_remote_copy(src_ref=src, dst_ref=dst, send_sem=sems[k], recv_sem=sems[n + k], device_id=dev,
                                         device_id_type=MESH).start()
        token[...] = jnp.zeros_like(token)

    outs = pl.pallas_call(
        body, name=name,
        out_shape=(*[pltpu.SemaphoreType.DMA(())] * (2 * n), *[pltpu.HBM(a.shape, a.dtype) for a in srcs + lands],
                   jax.ShapeDtypeStruct((8, LANES), F32)),
        in_specs=[HBM] * (2 * ns) + [ANY] * len(extra),
        out_specs=(*[SEM] * (2 * n), *[HBM] * (2 * ns), pl.BlockSpec(memory_space=pltpu.VMEM)),
        input_output_aliases={t: 2 * n + t for t in range(2 * ns)},
        compiler_params=pltpu.CompilerParams(has_side_effects=pltpu.SideEffectType.DATAFLOW_SIDE_EFFECTING),
    )(*srcs, *lands, *extra)
    return list(outs[:2 * n]), list(outs[2 * n:2 * n + ns]), list(outs[2 * n + ns:2 * n + 2 * ns]), outs[-1]


def _copies_wait(name, kind, sems, srcs, lands, after):
    ns = len(srcs)
    n = len(sems) // 2

    def body(*refs):
        sem_refs = refs[2 * ns:2 * ns + 2 * n]
        for k, (src, dst, dev) in enumerate(_split_plan(kind, refs[:ns], refs[ns:2 * ns])):
            cp = pltpu.make_async_remote_copy(src_ref=src, dst_ref=dst, send_sem=sem_refs[k], recv_sem=sem_refs[n + k],
                                              device_id=dev, device_id_type=MESH)
            cp.wait_send()
            cp.wait_recv()

    outs = pl.pallas_call(
        body, name=name, out_shape=tuple(pltpu.HBM(a.shape, a.dtype) for a in srcs + lands),
        in_specs=[HBM] * (2 * ns) + [SEM] * (2 * n) + [ANY], out_specs=tuple([HBM] * (2 * ns)),
        input_output_aliases={t: t for t in range(2 * ns)},
        compiler_params=pltpu.CompilerParams(has_side_effects=pltpu.SideEffectType.DATAFLOW_SIDE_EFFECTING),
    )(*srcs, *lands, *sems, after)
    return list(outs[:ns]), list(outs[ns:])


def _sum_into(pairs, recvs, idx, li, depth, accs):
    nt = len(pairs)

    def body(idx_ref, *refs):
        for p_ref, r_ref, out_ref in zip(refs[:nt], refs[nt:2 * nt], refs[-nt:]):
            out_ref[...] = p_ref[...].astype(F32) + r_ref[0].astype(F32) + r_ref[1].astype(F32) + r_ref[2].astype(F32)

    in_specs = ([pl.BlockSpec((None, p.shape[1] // 2, p.shape[2]), lambda i, idx_ref: (idx_ref[0], i, 0)) for p in pairs]
                + [pl.BlockSpec((3, p.shape[1] // 2, p.shape[2]), lambda i, idx_ref: (0, i, 0)) for p in pairs])
    args = [idx, *pairs, *recvs]
    aliases = {}
    if accs[0] is not None:
        in_specs += [ANY] * nt
        args += list(accs)
        aliases = {1 + 2 * nt + t: t for t in range(nt)}
    return pl.pallas_call(
        body, name="sum_into",
        grid_spec=pltpu.PrefetchScalarGridSpec(
            num_scalar_prefetch=1, grid=(2,), in_specs=in_specs,
            out_specs=[pl.BlockSpec((None, p.shape[1] // 2, p.shape[2]), lambda i, idx_ref: (li, 2 * idx_ref[1] + i, 0))
                       for p in pairs]),
        out_shape=[jax.ShapeDtypeStruct((depth, 2 * p.shape[1], p.shape[2]), F32) for p in pairs],
        input_output_aliases=aliases, compiler_params=_params("arbitrary"))(*args)


def _sum_slots(parts):
    n, rows, cols = parts.shape
    tr = _tile_rows(rows, 512, SUBLANES_WIRE)

    def body(p_ref, out_ref):
        acc = p_ref[0].astype(F32)
        for s in range(1, n):
            acc = acc + p_ref[s].astype(F32)
        out_ref[...] = acc

    return pl.pallas_call(
        body, name="sum_slots", grid=(rows // tr,),
        in_specs=[pl.BlockSpec((n, tr, cols), lambda i: (0, i, 0))],
        out_specs=pl.BlockSpec((tr, cols), lambda i: (i, 0)),
        out_shape=jax.ShapeDtypeStruct((rows, cols), F32),
        compiler_params=_params("arbitrary"))(parts)


def _sibling_share(gs, li):
    nt = len(gs)

    def body(*refs):
        out_refs = refs[nt:2 * nt]
        send_sems, recv_sems = refs[2 * nt:]
        x, y, c = _coords()
        sends, recvs = [], []
        for t in range(nt):
            rh = out_refs[t].shape[1] // 2
            mine, theirs = out_refs[t].at[li, pl.ds(c * rh, rh)], out_refs[t].at[li, pl.ds((1 - c) * rh, rh)]
            sems = dict(send_sem=send_sems.at[t], recv_sem=recv_sems.at[t], device_id=(x, y, 1 - c), device_id_type=MESH)
            sends.append(pltpu.make_async_remote_copy(src_ref=mine, dst_ref=mine, **sems))
            recvs.append(pltpu.make_async_remote_copy(src_ref=theirs, dst_ref=theirs, **sems))
        for cp in sends:
            cp.start()
        for cp in recvs:
            cp.wait_recv()
        for cp in sends:
            cp.wait_send()

    return pl.pallas_call(
        body, name="sibling_share", out_shape=[jax.ShapeDtypeStruct(g.shape, g.dtype) for g in gs],
        in_specs=[ANY] * nt, out_specs=[ANY] * nt, input_output_aliases={t: t for t in range(nt)},
        scratch_shapes=[pltpu.SemaphoreType.DMA((nt,)), pltpu.SemaphoreType.DMA((nt,))])(*gs)


def _all_gather_devices(buf, after=None):
    extra = [] if after is None else [after]

    def body(b_ref, *rest):
        out_ref, send_sems, recv_sems, local_sem = rest[len(extra):]
        x, y, c = _coords()
        me = 4 * x + 2 * y + c
        mine = pltpu.make_async_copy(b_ref, out_ref.at[me], local_sem)
        mine.start()
        peers = []
        for k in range(1, 8):
            fx, fy, fc = (k >> 2) & 1, (k >> 1) & 1, k & 1
            peers.append((x ^ fx, y ^ fy, c ^ fc))
        sends = [pltpu.make_async_remote_copy(src_ref=b_ref, dst_ref=out_ref.at[me], send_sem=send_sems.at[k],
                                              recv_sem=recv_sems.at[k], device_id=peer, device_id_type=MESH)
                 for k, peer in enumerate(peers)]
        for cp in sends:
            cp.start()
        for k, (px, py, pc) in enumerate(peers):
            pltpu.make_async_remote_copy(src_ref=b_ref, dst_ref=out_ref.at[4 * px + 2 * py + pc], send_sem=send_sems.at[k],
                                         recv_sem=recv_sems.at[k], device_id=(px, py, pc), device_id_type=MESH).wait_recv()
        for cp in sends:
            cp.wait_send()
        mine.wait()

    return pl.pallas_call(
        body, name="all_gather_devices", out_shape=jax.ShapeDtypeStruct((8,) + buf.shape, buf.dtype),
        in_specs=[ANY] * (1 + len(extra)), out_specs=ANY,
        scratch_shapes=[pltpu.SemaphoreType.DMA((7,)), pltpu.SemaphoreType.DMA((7,)), pltpu.SemaphoreType.DMA(())])(buf, *extra)


def _pair_sums(big_grads, c_idx):
    gs = [big_grads[n] for n in BIG]
    return _add_my_halves(gs, _sibling_swap_half(gs), c_idx)


SMALL_SHARDED = ("conv_qkv", "sconv_w")
REPLICATED = ("norm1_g", "a_log", "dt_bias", "onorm_g", "pool_w", "pool_scale", "norm2_g", "final_g")
ALL_WEIGHTS = ("norm1_g", "w_in", "conv_qkv", "a_log", "dt_bias", "onorm_g", "pool_w", "pool_scale", "sconv_w", "w_out",
               "norm2_g", "w_gate", "w_up", "w_down", "ple_proj", "ple_gate", "final_g")


def _pad_rows(flat, row_multiple):
    m = flat.shape[0]
    r = -(-m // (LANES * row_multiple)) * row_multiple
    return jnp.pad(flat, (0, r * LANES - m)).reshape(r, LANES)


def _adamw_math(w, g, m, v):
    c1 = 1.0 / (1.0 - ADAM_B1 ** ADAM_STEP)
    c2 = 1.0 / (1.0 - ADAM_B2 ** ADAM_STEP)
    nm = ADAM_B1 * m + (1.0 - ADAM_B1) * g
    nv = ADAM_B2 * v + (1.0 - ADAM_B2) * (g * g)
    return -ADAM_LR * ((nm * c1) / (jnp.sqrt(nv * c2) + ADAM_EPS) + ADAM_WD * w), nm, nv


def _adamw(w, g, m, v):
    shape = w.shape
    cols = shape[-1]
    rows = w.size // cols
    tr = _tile_rows(rows, 512)

    def body(w_ref, g_ref, m_ref, v_ref, d_ref, nm_ref, nv_ref, go_ref):
        gv = g_ref[...]
        d_ref[...], nm_ref[...], nv_ref[...] = _adamw_math(w_ref[...], gv, m_ref[...], v_ref[...])
        go_ref[...] = gv

    spec = pl.BlockSpec((tr, cols), lambda i: (i, 0))
    outs = pl.pallas_call(
        body, name="adamw", grid=(rows // tr,), in_specs=[spec] * 4, out_specs=[spec] * 4,
        out_shape=[jax.ShapeDtypeStruct((rows, cols), F32)] * 4,
        compiler_params=_params("arbitrary"))(*[a.reshape(rows, cols) for a in (w, g, m, v)])
    return tuple(o.reshape(shape) for o in outs)


def kernel(x, p, norm1_g, w_in, conv_qkv, a_log, dt_bias, onorm_g, pool_w, pool_scale, sconv_w, w_out, norm2_g, w_gate, w_up, w_down, ple_proj, ple_gate, final_g, loss_target, m_norm1_g, m_w_in, m_conv_qkv, m_a_log, m_dt_bias, m_onorm_g, m_pool_w, m_pool_scale, m_sconv_w, m_w_out, m_norm2_g, m_w_gate, m_w_up, m_w_down, m_ple_proj, m_ple_gate, m_final_g, v_norm1_g, v_w_in, v_conv_qkv, v_a_log, v_dt_bias, v_onorm_g, v_pool_w, v_pool_scale, v_sconv_w, v_w_out, v_norm2_g, v_w_gate, v_w_up, v_w_down, v_ple_proj, v_ple_gate, v_final_g):
    weights = dict(zip(ALL_WEIGHTS, (norm1_g, w_in, conv_qkv, a_log, dt_bias, onorm_g, pool_w, pool_scale, sconv_w, w_out,
                                     norm2_g, w_gate, w_up, w_down, ple_proj, ple_gate, final_g)))
    mom_m = dict(zip(ALL_WEIGHTS, (m_norm1_g, m_w_in, m_conv_qkv, m_a_log, m_dt_bias, m_onorm_g, m_pool_w, m_pool_scale,
                                   m_sconv_w, m_w_out, m_norm2_g, m_w_gate, m_w_up, m_w_down, m_ple_proj, m_ple_gate, m_final_g)))
    mom_v = dict(zip(ALL_WEIGHTS, (v_norm1_g, v_w_in, v_conv_qkv, v_a_log, v_dt_bias, v_onorm_g, v_pool_w, v_pool_scale,
                                   v_sconv_w, v_w_out, v_norm2_g, v_w_gate, v_w_up, v_w_down, v_ple_proj, v_ple_gate, v_final_g)))
    c_idx = lax.axis_index("c").astype(jnp.int32).reshape(1)
    chip = (2 * lax.axis_index("x") + lax.axis_index("y")).astype(jnp.int32)
    me_idx = chip.reshape(1)
    idx = jnp.stack([chip, lax.axis_index("c").astype(jnp.int32)])
    depth = p.shape[0]

    placed = _place_shards([weights[n] for n in BIG], me_idx)
    gw = [dict() for _ in range(depth)]
    gw[0]["w_in"] = _all_gather_chips(placed[0][:1])[0]
    early = ("w_in", "w_out")
    late = tuple(n for n in BIG if n not in early)
    groups = [dict(li=0, names=BIG[1:], forward=(0, "mixed"), finish=(0, "mixed"))]
    for li in range(1, depth):
        groups.append(dict(li=li, names=early, forward=(li - 1, "ffn"), finish=(li - 1, "end")))
        groups.append(dict(li=li, names=late, forward=(li, "mixed"), finish=(li, "mixed")))
    def arrive(li, stage, after):
        for k, g in enumerate(groups):
            if g["forward"] == (li, stage):
                g["sems"], g["arrs"], _ = _gather_call("gather_forward_%d" % k, g["arrs"], g["sems"], after, 1)
            if g["finish"] == (li, stage):
                _, g["arrs"], _ = _gather_call("gather_finish_%d" % k, g["arrs"], g["sems"], after, 2)
                gw[g["li"]].update(zip(g["names"], g["arrs"]))

    small = {n: weights[n] for n in REPLICATED}
    sflat = _pad_rows(jnp.concatenate([weights[n].reshape(-1) for n in SMALL_SHARDED]), 8)
    sgath8 = _all_gather_devices(sflat, gw[0]["w_in"])
    sgath = sgath8[0::2].reshape(4, -1)
    off = 0
    for n in SMALL_SHARDED:
        shp = weights[n].shape
        part = sgath[:, off:off + weights[n].size].reshape((4,) + shp)
        small[n] = jnp.moveaxis(part, 0, -2).reshape(shp[:-1] + (4 * shp[-1],))
        off += weights[n].size
    token = sgath8
    for k, g in enumerate(groups):
        arrs = [placed[g["li"]][BIG.index(n)] for n in g["names"]]
        g["sems"], g["arrs"], token = _gather_call("gather_start_%d" % k, arrs, [], token, 0)

    small["norm1_g"] = small["norm1_g"] + token[0, 0]

    pending = []
    last_token = [None]

    def advance(g, after):
        if g["stage"] == 0:
            gs, others = _copies_wait("swap_wait_" + g["tag"], "swap", *g["handle"], after)
            g["handle"] = _copies_start("exchange_start_" + g["tag"], "exchange", _add_my_halves(gs, others, c_idx))
            g["stage"] = 1
            return g["handle"][3]
        return None

    def produced(li, stage, grads, after):
        token = None
        for g in pending:
            token = advance(g, after) if g["stage"] == 0 else token
        if grads:
            names = [n for n in BIG if n in grads]
            handle = _copies_start("swap_start_%d%s" % (li, stage), "swap", [grads[n] for n in names], token)
            pending.append(dict(li=li, names=names, tag="%d%s" % (li, stage), stage=0, handle=handle[:3]))
            token = handle[3]
        last_token[0] = last_token[0] if token is None else token
        return token

    loss_local, dx, _, small_grads = _local_step(x[0], p[:, 0], loss_target[0], gw, small, produced, arrive)
    accs, big_outs = {}, {}

    def finish(g, after):
        pairs, recvs = _copies_wait("exchange_wait_" + g["tag"], "exchange", *g["handle"][:3], after)
        summed = _sum_into(pairs, recvs, idx, g["li"], depth, [accs.get(n) for n in g["names"]])
        accs.update(zip(g["names"], _sibling_share(summed, g["li"])))
        return accs[g["names"][-1]]

    def update(names):
        for n in names:
            big_outs[n] = _adamw(weights[n], accs[n], mom_m[n], mom_v[n])
        return big_outs[names[-1]][0]

    done = finish(pending[0], last_token[0])
    done = advance(pending[-1], done)
    for g in pending[1:-1]:
        done = finish(g, done)
    last = pending[-1]["names"]
    done = update([n for n in BIG if n not in last])
    finish(pending[-1], done)
    update(last)


    gshard = {}
    rnames = REPLICATED + SMALL_SHARDED
    rflat = _pad_rows(jnp.concatenate([small_grads[n].reshape(-1) for n in rnames]), 8)
    rsum = _sum_slots(_all_gather_devices(rflat)).reshape(-1)
    off = 0
    for n in rnames:
        whole = rsum[off:off + small_grads[n].size].reshape(small_grads[n].shape)
        off += small_grads[n].size
        if n in SMALL_SHARDED:
            cols = weights[n].shape[-1]
            whole = lax.dynamic_slice_in_dim(whole, chip * cols, cols, axis=whole.ndim - 1)
        gshard[n] = whole

    loss = lax.psum(loss_local, ("x", "y", "c"))
    deltas, new_m, new_v, grad_out = {}, {}, {}, {}
    for n in ALL_WEIGHTS:
        if n in BIG:
            deltas[n], new_m[n], new_v[n], grad_out[n] = big_outs[n]
        else:
            deltas[n], new_m[n], new_v[n], grad_out[n] = _adamw(weights[n], gshard[n], mom_m[n], mom_v[n])
    return (loss, dx[None], *[grad_out[n] for n in ALL_WEIGHTS], *[deltas[n] for n in ALL_WEIGHTS],
            *[new_m[n] for n in ALL_WEIGHTS], *[new_v[n] for n in ALL_WEIGHTS])
```

```python
import jax
import jax.numpy as jnp
from jax import lax
from jax.experimental import pallas as pl
from jax.experimental.pallas import tpu as pltpu

F32 = jnp.float32
MM_DTYPE = jnp.bfloat16
WIRE_DTYPE = jnp.bfloat16
HI = lax.Precision.HIGHEST
EPS = 1e-6
HEAD_DIM = 128
CHUNK = 64
QKV_CONV_WIDTH = 4
SCONV_WIDTH = 3
POOL_GROUPS = 4
LANES = 128
SUBLANES_WIRE = 16
VMEM_LIMIT_BYTES = 56 * 1024 * 1024
ADAM_LR, ADAM_B1, ADAM_B2, ADAM_EPS, ADAM_WD, ADAM_STEP = 0.001, 0.9, 0.999, 1e-08, 0.01, 10
MESH = pl.DeviceIdType.MESH
ANY = pl.BlockSpec(memory_space=pl.ANY)
HBM = pl.BlockSpec(memory_space=pltpu.HBM)
SEM = pl.BlockSpec(memory_space=pltpu.SEMAPHORE)


def _params(*sem):
    return pltpu.CompilerParams(vmem_limit_bytes=VMEM_LIMIT_BYTES, dimension_semantics=sem if sem else None)


def _mm(a, b):
    return jnp.dot(a.astype(MM_DTYPE), b.astype(MM_DTYPE), preferred_element_type=F32)


def _mm_nt(a, b):
    return lax.dot_general(a.astype(MM_DTYPE), b.astype(MM_DTYPE), (((1,), (1,)), ((), ())), preferred_element_type=F32)


def _mm_tn(a, b):
    return lax.dot_general(a.astype(MM_DTYPE), b.astype(MM_DTYPE), (((0,), (0,)), ((), ())), preferred_element_type=F32)


def _hmm(a, b):
    return jnp.dot(a, b, preferred_element_type=F32, precision=HI)


def _hmm_nt(a, b):
    return lax.dot_general(a, b, (((1,), (1,)), ((), ())), preferred_element_type=F32, precision=HI)


def _hmm_tn(a, b):
    return lax.dot_general(a, b, (((0,), (0,)), ((), ())), preferred_element_type=F32, precision=HI)


def _sigmoid(x):
    return 1.0 / (1.0 + jnp.exp(-x))


def _dsilu(x, s):
    return s * (1.0 + x * (1.0 - s))


def _rows(shape):
    return lax.broadcasted_iota(jnp.int32, shape, 0)


def _shift_down(x, s):
    if s == 0:
        return x
    return jnp.where(_rows(x.shape) >= s, pltpu.roll(x, s, 0), 0.0)


def _shift_up(x, s):
    if s == 0:
        return x
    t = x.shape[0]
    return jnp.where(_rows(x.shape) < t - s, pltpu.roll(x, t - s, 0), 0.0)


def _rms_fwd(x):
    r = lax.rsqrt(jnp.mean(x * x, axis=-1, keepdims=True) + EPS)
    return x * r, r


def _rms_bwd(dxn, xn, r):
    return r * (dxn - xn * jnp.mean(dxn * xn, axis=-1, keepdims=True))


def _tile_rows(n, cap, mult=8):
    best = None
    for d in range(mult, min(n, cap) + 1, mult):
        if n % d == 0:
            best = d
    return best if best is not None else n


def _in_proj_fwd(x, g1, wp, segs, tm):
    t, d = x.shape
    npk = wp.shape[1]

    def body(x_ref, g_ref, w_ref, *o_refs):
        xn, _ = _rms_fwd(x_ref[...])
        h = (xn * g_ref[...]).astype(w_ref.dtype)
        off = 0
        for o_ref, wd in zip(o_refs, segs):
            o_ref[...] = jnp.dot(h, w_ref[:, off:off + wd], preferred_element_type=F32)
            off += wd

    return pl.pallas_call(
        body, name="in_proj_fwd", grid=(t // tm,),
        in_specs=[pl.BlockSpec((tm, d), lambda i: (i, 0)), pl.BlockSpec((1, d), lambda i: (0, 0)),
                  pl.BlockSpec((d, npk), lambda i: (0, 0))],
        out_specs=[pl.BlockSpec((tm, wd), lambda i: (i, 0)) for wd in segs],
        out_shape=[jax.ShapeDtypeStruct((t, wd), F32) for wd in segs],
        compiler_params=_params("arbitrary"))(x, g1, wp)


def _in_proj_bwd(x, g1, wp, dsegs, dx_res, segs, tm):
    t, d = x.shape
    npk = wp.shape[1]
    nseg = len(segs)

    def body(x_ref, g_ref, w_ref, *rest):
        ds_refs = rest[:nseg]
        dxr_ref, dx_ref, dw_ref, dg_ref = rest[nseg:]
        i = pl.program_id(0)

        @pl.when(i == 0)
        def _():
            dw_ref[...] = jnp.zeros_like(dw_ref)
            dg_ref[...] = jnp.zeros_like(dg_ref)

        xn, r = _rms_fwd(x_ref[...])
        g = g_ref[...]
        h = (xn * g).astype(w_ref.dtype)
        dh = jnp.zeros((tm, d), F32)
        off = 0
        for ds_ref, wd in zip(ds_refs, segs):
            dsv = ds_ref[...].astype(w_ref.dtype)
            dh = dh + lax.dot_general(dsv, w_ref[:, off:off + wd], (((1,), (1,)), ((), ())), preferred_element_type=F32)
            dw_ref[:, off:off + wd] += lax.dot_general(h, dsv, (((0,), (0,)), ((), ())), preferred_element_type=F32)
            off += wd
        dg_ref[...] += jnp.sum(dh * xn, axis=0, keepdims=True)
        dx_ref[...] = dxr_ref[...] + _rms_bwd(dh * g, xn, r)

    return pl.pallas_call(
        body, name="in_proj_bwd", grid=(t // tm,),
        in_specs=[pl.BlockSpec((tm, d), lambda i: (i, 0)), pl.BlockSpec((1, d), lambda i: (0, 0)),
                  pl.BlockSpec((d, npk), lambda i: (0, 0))]
                 + [pl.BlockSpec((tm, wd), lambda i: (i, 0)) for wd in segs]
                 + [pl.BlockSpec((tm, d), lambda i: (i, 0))],
        out_specs=[pl.BlockSpec((tm, d), lambda i: (i, 0)), pl.BlockSpec((d, npk), lambda i: (0, 0)),
                   pl.BlockSpec((1, d), lambda i: (0, 0))],
        out_shape=[jax.ShapeDtypeStruct((t, d), F32), jax.ShapeDtypeStruct((d, npk), F32),
                   jax.ShapeDtypeStruct((1, d), F32)],
        compiler_params=_params("arbitrary"))(x, g1, wp, *dsegs, dx_res)


def _out_proj_fwd(x0, mix, wo, g2, tm):
    t, d = x0.shape
    dq = wo.shape[1]
    widths = [m.shape[1] for m in mix]

    def body(x_ref, *rest):
        m_refs = rest[:len(mix)]
        w_ref, g_ref, x1_ref, h2_ref = rest[len(mix):]
        acc = x_ref[...]
        off = 0
        for m_ref, wd in zip(m_refs, widths):
            for k in range(wd // dq):
                acc = acc + jnp.dot(m_ref[:, k * dq:(k + 1) * dq].astype(w_ref.dtype), w_ref[off // dq + k],
                                    preferred_element_type=F32)
            off += wd
        x1_ref[...] = acc
        xn, _ = _rms_fwd(acc)
        h2_ref[...] = (xn * g_ref[...]).astype(h2_ref.dtype)

    return pl.pallas_call(
        body, name="out_proj_fwd", grid=(t // tm,),
        in_specs=[pl.BlockSpec((tm, d), lambda i: (i, 0))]
                 + [pl.BlockSpec((tm, wd), lambda i: (i, 0)) for wd in widths]
                 + [pl.BlockSpec((4, dq, d), lambda i: (0, 0, 0)), pl.BlockSpec((1, d), lambda i: (0, 0))],
        out_specs=[pl.BlockSpec((tm, d), lambda i: (i, 0)), pl.BlockSpec((tm, d), lambda i: (i, 0))],
        out_shape=[jax.ShapeDtypeStruct((t, d), F32), jax.ShapeDtypeStruct((t, d), MM_DTYPE)],
        compiler_params=_params("arbitrary"))(x0, *mix, wo, g2)


def _out_proj_bwd(dx2, dh2, x1, g2, mix, wo, tm):
    t, d = x1.shape
    dq = wo.shape[1]
    widths = [m.shape[1] for m in mix]
    nm = len(mix)

    def body(dx2_ref, dh2_ref, x1_ref, g_ref, *rest):
        m_refs = rest[:nm]
        w_ref = rest[nm]
        dx1_ref = rest[nm + 1]
        dm_refs = rest[nm + 2:nm + 2 + nm]
        dw_ref, dg_ref = rest[nm + 2 + nm:]
        i = pl.program_id(0)

        @pl.when(i == 0)
        def _():
            dw_ref[...] = jnp.zeros_like(dw_ref)
            dg_ref[...] = jnp.zeros_like(dg_ref)

        xn, r = _rms_fwd(x1_ref[...])
        dh2v = dh2_ref[...]
        dg_ref[...] += jnp.sum(dh2v * xn, axis=0, keepdims=True)
        dx1 = dx2_ref[...] + _rms_bwd(dh2v * g_ref[...], xn, r)
        dx1_ref[...] = dx1
        dx1c = dx1.astype(w_ref.dtype)
        off = 0
        for m_ref, dm_ref, wd in zip(m_refs, dm_refs, widths):
            for k in range(wd // dq):
                j = off // dq + k
                cols = slice(k * dq, (k + 1) * dq)
                dm_ref[:, cols] = lax.dot_general(dx1c, w_ref[j], (((1,), (1,)), ((), ())), preferred_element_type=F32)
                dw_ref[j] += lax.dot_general(m_ref[:, cols].astype(w_ref.dtype), dx1c, (((0,), (0,)), ((), ())),
                                             preferred_element_type=F32)
            off += wd

    tile = lambda wd: pl.BlockSpec((tm, wd), lambda i: (i, 0))
    return pl.pallas_call(
        body, name="out_proj_bwd", grid=(t // tm,),
        in_specs=[tile(d), tile(d), tile(d), pl.BlockSpec((1, d), lambda i: (0, 0))]
                 + [tile(wd) for wd in widths] + [pl.BlockSpec((4, dq, d), lambda i: (0, 0, 0))],
        out_specs=[tile(d)] + [tile(wd) for wd in widths]
                  + [pl.BlockSpec((4, dq, d), lambda i: (0, 0, 0)), pl.BlockSpec((1, d), lambda i: (0, 0))],
        out_shape=[jax.ShapeDtypeStruct((t, d), F32)] + [jax.ShapeDtypeStruct((t, wd), F32) for wd in widths]
                  + [jax.ShapeDtypeStruct((4, dq, d), F32), jax.ShapeDtypeStruct((1, d), F32)],
        compiler_params=_params("arbitrary"))(dx2, dh2, x1, g2, *mix, wo)


def _ffn_fwd(x1, h2, wg, wu, wd, tm):
    t, d = x1.shape
    fs = wg.shape[1]

    def body(x1_ref, h2_ref, wg_ref, wu_ref, wd_ref, x2_ref, gp_ref, up_ref):
        @pl.when(pl.program_id(1) == 0)
        def _():
            x2_ref[...] = x1_ref[...]

        h = h2_ref[...]
        nt = (((1,), (1,)), ((), ()))
        gp = lax.dot_general(h, wg_ref[...], nt, preferred_element_type=F32)
        up = lax.dot_general(h, wu_ref[...], nt, preferred_element_type=F32)
        gp_ref[...] = gp
        up_ref[...] = up
        ff = gp * _sigmoid(gp) * up
        x2_ref[...] += jnp.dot(ff.astype(wd_ref.dtype), wd_ref[...], preferred_element_type=F32)

    return pl.pallas_call(
        body, name="ffn_fwd", grid=(t // tm, 4),
        in_specs=[pl.BlockSpec((tm, d), lambda i, j: (i, 0)), pl.BlockSpec((tm, d), lambda i, j: (i, 0)),
                  pl.BlockSpec((None, fs, d), lambda i, j: (j, 0, 0)),
                  pl.BlockSpec((None, fs, d), lambda i, j: (j, 0, 0)),
                  pl.BlockSpec((None, fs, d), lambda i, j: (j, 0, 0))],
        out_specs=[pl.BlockSpec((tm, d), lambda i, j: (i, 0)), pl.BlockSpec((None, tm, fs), lambda i, j: (j, i, 0)),
                   pl.BlockSpec((None, tm, fs), lambda i, j: (j, i, 0))],
        out_shape=[jax.ShapeDtypeStruct((t, d), F32), jax.ShapeDtypeStruct((4, t, fs), F32),
                   jax.ShapeDtypeStruct((4, t, fs), F32)],
        compiler_params=_params("arbitrary", "arbitrary"))(x1, h2, wg, wu, wd)


def _ffn_bwd(dx2, h2, gp, up, wg, wu, wd, tm):
    t, d = dx2.shape
    fs = wg.shape[1]

    def body(dx2_ref, h2_ref, gp_ref, up_ref, wg_ref, wu_ref, wd_ref, dh2_ref, dwg_ref, dwu_ref, dwd_ref):
        j, i = pl.program_id(0), pl.program_id(1)

        @pl.when(i == 0)
        def _():
            dwg_ref[...] = jnp.zeros_like(dwg_ref)
            dwu_ref[...] = jnp.zeros_like(dwu_ref)
            dwd_ref[...] = jnp.zeros_like(dwd_ref)

        cdt = wg_ref.dtype
        h = h2_ref[...]
        gpv, upv = gp_ref[...], up_ref[...]
        s = _sigmoid(gpv)
        silu = gpv * s
        dx2c = dx2_ref[...].astype(cdt)
        dff = lax.dot_general(dx2c, wd_ref[...], (((1,), (1,)), ((), ())), preferred_element_type=F32)
        dwd_ref[...] += lax.dot_general((silu * upv).astype(cdt), dx2c, (((0,), (0,)), ((), ())), preferred_element_type=F32)
        dup = (dff * silu).astype(cdt)
        dgp = (dff * upv * _dsilu(gpv, s)).astype(cdt)
        dwg_ref[...] += lax.dot_general(dgp, h, (((0,), (0,)), ((), ())), preferred_element_type=F32)
        dwu_ref[...] += lax.dot_general(dup, h, (((0,), (0,)), ((), ())), preferred_element_type=F32)
        dh = (jnp.dot(dgp, wg_ref[...], preferred_element_type=F32) + jnp.dot(dup, wu_ref[...], preferred_element_type=F32))
        rows = pl.ds(pl.multiple_of(i * tm, tm), tm)

        @pl.when(j == 0)
        def _():
            dh2_ref[rows, :] = dh

        @pl.when(j != 0)
        def _():
            dh2_ref[rows, :] += dh

    return pl.pallas_call(
        body, name="ffn_bwd", grid=(4, t // tm),
        in_specs=[pl.BlockSpec((tm, d), lambda j, i: (i, 0)), pl.BlockSpec((tm, d), lambda j, i: (i, 0)),
                  pl.BlockSpec((None, tm, fs), lambda j, i: (j, i, 0)), pl.BlockSpec((None, tm, fs), lambda j, i: (j, i, 0)),
                  pl.BlockSpec((None, fs, d), lambda j, i: (j, 0, 0)),
                  pl.BlockSpec((None, fs, d), lambda j, i: (j, 0, 0)),
                  pl.BlockSpec((None, fs, d), lambda j, i: (j, 0, 0))],
        out_specs=[pl.BlockSpec((t, d), lambda j, i: (0, 0)), pl.BlockSpec((None, fs, d), lambda j, i: (j, 0, 0)),
                   pl.BlockSpec((None, fs, d), lambda j, i: (j, 0, 0)), pl.BlockSpec((None, fs, d), lambda j, i: (j, 0, 0))],
        out_shape=[jax.ShapeDtypeStruct((t, d), F32)] + [jax.ShapeDtypeStruct((4, fs, d), F32)] * 3,
        compiler_params=_params("arbitrary", "arbitrary"))(dx2, h2, gp, up, wg, wu, wd)


def _ple_fwd(x2, p, wpg, wpp, tm):
    t, d = x2.shape
    q = p.shape[1]
    dq = d // 4

    def body(x_ref, p_ref, wg_ref, wp_ref, o_ref):
        xv = x_ref[...]
        xc = xv.astype(wg_ref.dtype)
        pc = p_ref[...].astype(wp_ref.dtype)
        pre = jnp.dot(xc[:, :dq], wg_ref[0], preferred_element_type=F32)
        for j in range(1, 4):
            pre = pre + jnp.dot(xc[:, j * dq:(j + 1) * dq], wg_ref[j], preferred_element_type=F32)
        gate = _sigmoid(pre)
        for j in range(4):
            cols = slice(j * dq, (j + 1) * dq)
            o_ref[:, cols] = xv[:, cols] + gate[:, cols] * jnp.dot(pc, wp_ref[j], preferred_element_type=F32)

    return pl.pallas_call(
        body, name="ple_fwd", grid=(t // tm,),
        in_specs=[pl.BlockSpec((tm, d), lambda i: (i, 0)), pl.BlockSpec((tm, q), lambda i: (i, 0)),
                  pl.BlockSpec((4, dq, d), lambda i: (0, 0, 0)),
                  pl.BlockSpec((4, q, dq), lambda i: (0, 0, 0))],
        out_specs=pl.BlockSpec((tm, d), lambda i: (i, 0)),
        out_shape=jax.ShapeDtypeStruct((t, d), F32),
        compiler_params=_params("arbitrary"))(x2, p, wpg, wpp)


def _ple_bwd(dx3, x2, p, wpg, wpp, tm):
    t, d = x2.shape
    q = p.shape[1]
    dq = d // 4

    def body(dx3_ref, x_ref, p_ref, wg_ref, wp_ref, dx2_ref, dwg_ref, dwp_ref):
        @pl.when(pl.program_id(0) == 0)
        def _():
            dwg_ref[...] = jnp.zeros_like(dwg_ref)
            dwp_ref[...] = jnp.zeros_like(dwp_ref)

        cdt = wg_ref.dtype
        xc = x_ref[...].astype(cdt)
        pc = p_ref[...].astype(cdt)
        pre = jnp.dot(xc[:, :dq], wg_ref[0], preferred_element_type=F32)
        for j in range(1, 4):
            pre = pre + jnp.dot(xc[:, j * dq:(j + 1) * dq], wg_ref[j], preferred_element_type=F32)
        gate = _sigmoid(pre)
        dx3v = dx3_ref[...]
        dpp = (dx3v * gate).astype(cdt)
        dgate = dx3v * gate * (1.0 - gate)
        dpre_parts = []
        for j in range(4):
            cols = slice(j * dq, (j + 1) * dq)
            pp_j = jnp.dot(pc, wp_ref[j], preferred_element_type=F32)
            dpre_parts.append((dgate[:, cols] * pp_j).astype(cdt))
            dwp_ref[j] += lax.dot_general(pc, dpp[:, cols], (((0,), (0,)), ((), ())), preferred_element_type=F32)
        dpre = jnp.concatenate(dpre_parts, axis=1)
        for j in range(4):
            cols = slice(j * dq, (j + 1) * dq)
            dwg_ref[j] += lax.dot_general(xc[:, cols], dpre, (((0,), (0,)), ((), ())), preferred_element_type=F32)
            dx2_ref[:, cols] = dx3v[:, cols] + lax.dot_general(dpre, wg_ref[j], (((1,), (1,)), ((), ())),
                                                               preferred_element_type=F32)

    return pl.pallas_call(
        body, name="ple_bwd", grid=(t // tm,),
        in_specs=[pl.BlockSpec((tm, d), lambda i: (i, 0)), pl.BlockSpec((tm, d), lambda i: (i, 0)),
                  pl.BlockSpec((tm, q), lambda i: (i, 0)), pl.BlockSpec((4, dq, d), lambda i: (0, 0, 0)),
                  pl.BlockSpec((4, q, dq), lambda i: (0, 0, 0))],
        out_specs=[pl.BlockSpec((tm, d), lambda i: (i, 0)), pl.BlockSpec((4, dq, d), lambda i: (0, 0, 0)),
                   pl.BlockSpec((4, q, dq), lambda i: (0, 0, 0))],
        out_shape=[jax.ShapeDtypeStruct((t, d), F32), jax.ShapeDtypeStruct((4, dq, d), F32),
                   jax.ShapeDtypeStruct((4, q, dq), F32)],
        compiler_params=_params("arbitrary"))(dx3, x2, p, wpg, wpp)


def _loss_head(x, target, fg, tm):
    t, d = x.shape

    def body(x_ref, t_ref, g_ref, dx_ref, loss_ref, dg_ref):
        @pl.when(pl.program_id(0) == 0)
        def _():
            loss_ref[...] = jnp.zeros_like(loss_ref)
            dg_ref[...] = jnp.zeros_like(dg_ref)

        xn, r = _rms_fwd(x_ref[...])
        g = g_ref[...]
        err = xn * g - t_ref[...]
        loss_ref[...] += 0.5 * jnp.sum(jnp.sum(err * err, axis=-1, keepdims=True) / d, axis=0, keepdims=True)
        dy = err / d
        dg_ref[...] += jnp.sum(dy * xn, axis=0, keepdims=True)
        dx_ref[...] = _rms_bwd(dy * g, xn, r)

    return pl.pallas_call(
        body, name="loss_head", grid=(t // tm,),
        in_specs=[pl.BlockSpec((tm, d), lambda i: (i, 0)), pl.BlockSpec((tm, d), lambda i: (i, 0)),
                  pl.BlockSpec((1, d), lambda i: (0, 0))],
        out_specs=[pl.BlockSpec((tm, d), lambda i: (i, 0)), pl.BlockSpec((1, 1), lambda i: (0, 0)),
                   pl.BlockSpec((1, d), lambda i: (0, 0))],
        out_shape=[jax.ShapeDtypeStruct((t, d), F32), jax.ShapeDtypeStruct((1, 1), F32),
                   jax.ShapeDtypeStruct((1, d), F32)],
        compiler_params=_params("arbitrary"))(x, target, fg)


def _qkv_conv_act(xv, w, j, heads):
    k = QKV_CONV_WIDTH
    y = w[k - 1:k] * xv
    for s in range(1, k):
        y = y + w[k - 1 - s:k - s] * _shift_down(xv, s)
    sg = _sigmoid(y)
    s_act = y * sg
    nrm = lax.rsqrt(jnp.sum(s_act * s_act, axis=-1, keepdims=True) + EPS)
    scale = jnp.where(j < heads, HEAD_DIM ** -0.5, 1.0).astype(F32)
    return y, sg, s_act, nrm, scale


def _qkv_conv_fwd(qkv_pre, conv_w, heads):
    t = qkv_pre.shape[0]
    nblk = 3 * heads

    def body(x_ref, w_ref, o_ref):
        j = pl.program_id(0)
        _, _, s_act, nrm, scale = _qkv_conv_act(x_ref[...], w_ref[...], j, heads)
        o_ref[...] = jnp.where(j < 2 * heads, s_act * (nrm * scale), s_act)

    return pl.pallas_call(
        body, name="qkv_conv_fwd", grid=(nblk,),
        in_specs=[pl.BlockSpec((t, LANES), lambda j: (0, j)), pl.BlockSpec((QKV_CONV_WIDTH, LANES), lambda j: (0, j))],
        out_specs=pl.BlockSpec((t, LANES), lambda j: (0, j)),
        out_shape=jax.ShapeDtypeStruct(qkv_pre.shape, F32),
        compiler_params=_params("arbitrary"))(qkv_pre, conv_w)


def _qkv_conv_bwd(qkv_pre, conv_w, dqkv, heads):
    t = qkv_pre.shape[0]
    nblk = 3 * heads
    k = QKV_CONV_WIDTH

    def body(x_ref, w_ref, dn_ref, dx_ref, dw_ref):
        j = pl.program_id(0)
        xv, w = x_ref[...], w_ref[...]
        y, sg, s_act, nrm, scale = _qkv_conv_act(xv, w, j, heads)
        dn = dn_ref[...]
        dsn = dn * scale
        ds_qk = nrm * dsn - s_act * (nrm * nrm * nrm) * jnp.sum(dsn * s_act, axis=-1, keepdims=True)
        ds = jnp.where(j < 2 * heads, ds_qk, dn)
        dy = ds * _dsilu(y, sg)
        dx = w[k - 1:k] * dy
        dw_ref[k - 1:k, :] = jnp.sum(dy * xv, axis=0, keepdims=True)
        for s in range(1, k):
            dx = dx + w[k - 1 - s:k - s] * _shift_up(dy, s)
            dw_ref[k - 1 - s:k - s, :] = jnp.sum(dy * _shift_down(xv, s), axis=0, keepdims=True)
        dx_ref[...] = dx

    return pl.pallas_call(
        body, name="qkv_conv_bwd", grid=(nblk,),
        in_specs=[pl.BlockSpec((t, LANES), lambda j: (0, j)), pl.BlockSpec((k, LANES), lambda j: (0, j)),
                  pl.BlockSpec((t, LANES), lambda j: (0, j))],
        out_specs=[pl.BlockSpec((t, LANES), lambda j: (0, j)), pl.BlockSpec((k, LANES), lambda j: (0, j))],
        out_shape=[jax.ShapeDtypeStruct(qkv_pre.shape, F32), jax.ShapeDtypeStruct(conv_w.shape, F32)],
        compiler_params=_params("arbitrary"))(qkv_pre, conv_w, dqkv)


def _pool_windows(shape, j, group_dim):
    lane = lax.broadcasted_iota(jnp.int32, shape, 1) + j * LANES
    grp = lane // group_dim
    win = jnp.left_shift(2, grp).astype(F32)
    cnt = jnp.minimum((_rows(shape) + 1).astype(F32), win)
    return grp, cnt


def _pool_select(grp, levels):
    out = levels[0]
    for gi in range(1, POOL_GROUPS):
        out = jnp.where(grp == gi, levels[gi], out)
    return out


def _pool_mean(hv, grp, cnt):
    acc, levels, width = hv, [], 1
    for _ in range(POOL_GROUPS):
        acc = acc + _shift_down(acc, width)
        width *= 2
        levels.append(acc)
    return _pool_select(grp, levels) / cnt - hv


def _pool_fwd(hp, wbd, scale, group_dim):
    t, dp = hp.shape

    def body(h_ref, w_ref, s_ref, o_ref):
        hv = h_ref[...]
        grp, cnt = _pool_windows(hv.shape, pl.program_id(0), group_dim)
        pooled = _pool_mean(hv, grp, cnt)
        o_ref[...] = _mm(pooled, w_ref[...]) * s_ref[...]

    return pl.pallas_call(
        body, name="pool_fwd", grid=(dp // LANES,),
        in_specs=[pl.BlockSpec((t, LANES), lambda j: (0, j)), pl.BlockSpec((LANES, LANES), lambda j: (j, j)),
                  pl.BlockSpec((1, LANES), lambda j: (0, j))],
        out_specs=pl.BlockSpec((t, LANES), lambda j: (0, j)),
        out_shape=jax.ShapeDtypeStruct(hp.shape, F32),
        compiler_params=_params("arbitrary"))(hp, wbd, scale)


def _pool_bwd(hp, wbd, scale, dob, group_dim):
    t, dp = hp.shape

    def body(h_ref, w_ref, s_ref, do_ref, dh_ref, dw_ref, ds_ref):
        hv = h_ref[...]
        grp, cnt = _pool_windows(hv.shape, pl.program_id(0), group_dim)
        pooled = _pool_mean(hv, grp, cnt)
        wv = w_ref[...]
        dov = do_ref[...]
        ds_ref[...] = jnp.sum(dov * _mm(pooled, wv), axis=0, keepdims=True)
        dys = dov * s_ref[...]
        dw_ref[0] = _mm_tn(pooled, dys)
        dpooled = _mm_nt(dys, wv)
        acc, levels, width = dpooled / cnt, [], 1
        for _ in range(POOL_GROUPS):
            acc = acc + _shift_up(acc, width)
            width *= 2
            levels.append(acc)
        dh_ref[...] = _pool_select(grp, levels) - dpooled

    nb = dp // LANES
    return pl.pallas_call(
        body, name="pool_bwd", grid=(nb,),
        in_specs=[pl.BlockSpec((t, LANES), lambda j: (0, j)), pl.BlockSpec((LANES, LANES), lambda j: (j, j)),
                  pl.BlockSpec((1, LANES), lambda j: (0, j)), pl.BlockSpec((t, LANES), lambda j: (0, j))],
        out_specs=[pl.BlockSpec((t, LANES), lambda j: (0, j)), pl.BlockSpec((1, LANES, LANES), lambda j: (j, 0, 0)),
                   pl.BlockSpec((1, LANES), lambda j: (0, j))],
        out_shape=[jax.ShapeDtypeStruct(hp.shape, F32), jax.ShapeDtypeStruct((nb, LANES, LANES), F32),
                   jax.ShapeDtypeStruct((1, dp), F32)],
        compiler_params=_params("arbitrary"))(hp, wbd, scale, dob)


def _sconv_fwd(cbcch, w):
    t, dc3 = cbcch.shape
    nb = dc3 // 3 // LANES
    k = SCONV_WIDTH

    def body(b_ref, c_ref, h_ref, w_ref, o_ref):
        m = c_ref[...] * h_ref[...]
        wv = w_ref[...]
        y = wv[k - 1:k] * m
        for s in range(1, k):
            y = y + wv[k - 1 - s:k - s] * _shift_down(m, s)
        o_ref[...] = b_ref[...] * y

    return pl.pallas_call(
        body, name="sconv_fwd", grid=(nb,),
        in_specs=[pl.BlockSpec((t, LANES), lambda j: (0, j)), pl.BlockSpec((t, LANES), lambda j: (0, nb + j)),
                  pl.BlockSpec((t, LANES), lambda j: (0, 2 * nb + j)), pl.BlockSpec((k, LANES), lambda j: (0, j))],
        out_specs=pl.BlockSpec((t, LANES), lambda j: (0, j)),
        out_shape=jax.ShapeDtypeStruct((t, dc3 // 3), F32),
        compiler_params=_params("arbitrary"))(cbcch, cbcch, cbcch, w)


def _sconv_bwd(cbcch, w, doc):
    t, dc3 = cbcch.shape
    nb = dc3 // 3 // LANES
    k = SCONV_WIDTH

    def body(b_ref, c_ref, h_ref, w_ref, do_ref, db_ref, dc_ref, dh_ref, dw_ref):
        cv, hv = c_ref[...], h_ref[...]
        m = cv * hv
        wv = w_ref[...]
        dov = do_ref[...]
        dy = dov * b_ref[...]
        y = wv[k - 1:k] * m
        dm = wv[k - 1:k] * dy
        dw_ref[k - 1:k, :] = jnp.sum(dy * m, axis=0, keepdims=True)
        for s in range(1, k):
            ms = _shift_down(m, s)
            y = y + wv[k - 1 - s:k - s] * ms
            dm = dm + wv[k - 1 - s:k - s] * _shift_up(dy, s)
            dw_ref[k - 1 - s:k - s, :] = jnp.sum(dy * ms, axis=0, keepdims=True)
        db_ref[...] = dov * y
        dc_ref[...] = dm * hv
        dh_ref[...] = dm * cv

    col = lambda o: pl.BlockSpec((t, LANES), lambda j: (0, o * nb + j))
    return pl.pallas_call(
        body, name="sconv_bwd", grid=(nb,),
        in_specs=[col(0), col(1), col(2), pl.BlockSpec((k, LANES), lambda j: (0, j)), col(0)],
        out_specs=[col(0), col(0), col(0), pl.BlockSpec((k, LANES), lambda j: (0, j))],
        out_shape=[jax.ShapeDtypeStruct((t, dc3 // 3), F32)] * 3 + [jax.ShapeDtypeStruct(w.shape, F32)],
        compiler_params=_params("arbitrary"))(cbcch, cbcch, cbcch, w, doc)


class _Split:
    def __init__(self, a):
        self.hi = a.astype(jnp.bfloat16)
        self.lo = (a - self.hi.astype(F32)).astype(jnp.bfloat16)


def _per_head(dims, a, b):
    a = a if isinstance(a, _Split) else _Split(a)
    b = b if isinstance(b, _Split) else _Split(b)

    def dot(x, y):
        return lax.dot_general(x, y, (dims, ((), ())), preferred_element_type=F32)

    return jnp.stack([dot(a.hi[h], b.hi[h]) + (dot(a.hi[h], b.lo[h]) + dot(a.lo[h], b.hi[h])) for h in range(a.hi.shape[0])])


def _bmm(a, b):
    return _per_head(((1,), (0,)), a, b)


def _bmm_nt(a, b):
    return _per_head(((1,), (1,)), a, b)


def _bmm_tn(a, b):
    return _per_head(((0,), (0,)), a, b)


def _inv_unit_lower(low):
    c = low.shape[-1]
    eye = (_rows((c, c)) == lax.broadcasted_iota(jnp.int32, (c, c), 1)).astype(F32)
    pw = -low
    inv = eye + pw
    span = 2
    while span < c:
        pws = _Split(pw)
        pw = _bmm(pws, pws)
        inv = inv + _bmm(inv, pw)
        span *= 2
    return inv


def _heads_of(ref, base, heads):
    return jnp.stack([ref[:, base + h * HEAD_DIM:base + (h + 1) * HEAD_DIM] for h in range(heads)])


def _chunk_common(q, k, v, a_col, b_col, alog, dtb, kept=None):
    hn, c, _ = q.shape
    beta = _sigmoid(b_col)
    xg = a_col + dtb
    softplus = jnp.maximum(xg, 0.0) + jnp.log(1.0 + jnp.exp(-jnp.abs(xg)))
    neg_ea = -jnp.exp(alog)
    g = neg_ea * softplus
    ri = _rows((c, c))
    ci = lax.broadcasted_iota(jnp.int32, (c, c), 1)
    incl, strict = ri >= ci, ri > ci
    inclf = jnp.broadcast_to(incl.astype(F32), (hn, c, c))
    gcb = _bmm(inclf, jnp.broadcast_to(g, (hn, c, HEAD_DIM)))
    gc_row = jnp.sum(jnp.where(ri <= ci, jnp.broadcast_to(g, (hn, c, c)), 0.0), axis=1, keepdims=True)
    dmat = jnp.where(incl, jnp.exp(jnp.where(incl, gcb[:, :, :1] - gc_row, 0.0)), 0.0)
    eg = jnp.exp(gcb)
    gl = gcb[:, c - 1:c, :]
    egl = jnp.exp(gl)
    edl = jnp.exp(gl - gcb)
    kb, vb = k * beta, v * beta
    kbe = kb * eg
    if kept is None:
        ks = _Split(k)
        a0 = _bmm_nt(kb, ks)
        tm = _inv_unit_lower(jnp.where(strict, a0 * dmat, 0.0))
        p0 = _bmm_nt(q, ks)
        tms = _Split(tm)
        u, w = _bmm(tms, vb), _bmm(tms, kbe)
    else:
        (a0, tm, p0, w), u = kept, None
    return dict(beta=beta, xg=xg, neg_ea=neg_ea, g=g, incl=incl, strict=strict, inclf=inclf, dmat=dmat, eg=eg,
                egl=egl, edl=edl, kb=kb, vb=vb, a0=a0, tm=tm, kbe=kbe, u=u, w=w, p0=p0,
                attn=p0 * dmat, qe=q * eg, kd=k * edl)


def _chunk_step(cm, state):
    ss = _Split(state)
    vn = cm["u"] - _bmm(cm["w"], ss)
    vns = _Split(vn)
    o = _bmm(cm["qe"], ss) + _bmm(cm["attn"], vns)
    new_state = state * cm["egl"][:, :, :1] + _bmm_tn(cm["kd"], vns)
    return vn, o, new_state


def _gated_norm(o, zv, og):
    xo, ro = _rms_fwd(o)
    sgz = _sigmoid(zv)
    return xo, ro, sgz, xo * og * (zv * sgz)


def _gate_columns(abv, gpv, heads):
    a_col = jnp.stack([abv[:, h:h + 1] for h in range(heads)])
    b_col = jnp.stack([abv[:, heads + h:heads + h + 1] for h in range(heads)])
    alog = jnp.stack([gpv[0:1, h:h + 1] for h in range(heads)])
    dtb = jnp.stack([gpv[1:2, h:h + 1] for h in range(heads)])
    return a_col, b_col, alog, dtb


def _delta_fwd(qkv, z, ab, gpar, heads):
    t = qkv.shape[0]
    da = heads * HEAD_DIM
    n = t // CHUNK

    def body(qkv_ref, z_ref, ab_ref, gp_ref, oa_ref, st_ref, kc_ref, kw_ref, s_ref):
        @pl.when(pl.program_id(0) == 0)
        def _():
            s_ref[...] = jnp.zeros_like(s_ref)

        gpv = gp_ref[...]
        cm = _chunk_common(_heads_of(qkv_ref, 0, heads), _heads_of(qkv_ref, da, heads), _heads_of(qkv_ref, 2 * da, heads),
                           *_gate_columns(ab_ref[...], gpv, heads))
        state = s_ref[...]
        st_ref[0] = state
        vn, o, new_state = _chunk_step(cm, state)
        s_ref[...] = new_state
        for slot, val in enumerate((cm["a0"], cm["tm"], cm["p0"])):
            kc_ref[0, slot] = val
        for slot, val in enumerate((cm["w"], vn, o)):
            kw_ref[0, slot] = val
        oa = _gated_norm(o, _heads_of(z_ref, 0, heads), gpv[2:3, :])[3]
        for h in range(heads):
            oa_ref[:, h * HEAD_DIM:(h + 1) * HEAD_DIM] = oa[h]

    return pl.pallas_call(
        body, name="delta_fwd", grid=(n,),
        in_specs=[pl.BlockSpec((CHUNK, 3 * da), lambda i: (i, 0)), pl.BlockSpec((CHUNK, da), lambda i: (i, 0)),
                  pl.BlockSpec((CHUNK, LANES), lambda i: (i, 0)), pl.BlockSpec((8, LANES), lambda i: (0, 0))],
        out_specs=[pl.BlockSpec((CHUNK, da), lambda i: (i, 0)),
                   pl.BlockSpec((1, heads, HEAD_DIM, HEAD_DIM), lambda i: (i, 0, 0, 0)),
                   pl.BlockSpec((1, 3, heads, CHUNK, CHUNK), lambda i: (i, 0, 0, 0, 0)),
                   pl.BlockSpec((1, 3, heads, CHUNK, HEAD_DIM), lambda i: (i, 0, 0, 0, 0))],
        out_shape=[jax.ShapeDtypeStruct((t, da), F32), jax.ShapeDtypeStruct((n, heads, HEAD_DIM, HEAD_DIM), F32),
                   jax.ShapeDtypeStruct((n, 3, heads, CHUNK, CHUNK), F32),
                   jax.ShapeDtypeStruct((n, 3, heads, CHUNK, HEAD_DIM), F32)],
        scratch_shapes=[pltpu.VMEM((heads, HEAD_DIM, HEAD_DIM), F32)],
        compiler_params=_params("arbitrary"))(qkv, z, ab, gpar)


def _delta_bwd(qkv, z, ab, gpar, states, kept_c, kept_w, doa, heads):
    t = qkv.shape[0]
    da = heads * HEAD_DIM
    n = t // CHUNK
    c = CHUNK

    def body(qkv_ref, z_ref, ab_ref, gp_ref, st_ref, kc_ref, kw_ref, doa_ref, dqkv_ref, dz_ref, dab_ref, dpar_ref, ds_ref):
        @pl.when(pl.program_id(0) == 0)
        def _():
            ds_ref[...] = jnp.zeros_like(ds_ref)
            dpar_ref[...] = jnp.zeros_like(dpar_ref)

        gpv = gp_ref[...]
        og = gpv[2:3, :]
        q, k, v = _heads_of(qkv_ref, 0, heads), _heads_of(qkv_ref, da, heads), _heads_of(qkv_ref, 2 * da, heads)
        cm = _chunk_common(q, k, v, *_gate_columns(ab_ref[...], gpv, heads),
                           kept=(kc_ref[0, 0], kc_ref[0, 1], kc_ref[0, 2], kw_ref[0, 0]))
        state = st_ref[0]
        dsp = ds_ref[...]
        vn, o = kw_ref[0, 1], kw_ref[0, 2]
        zv = _heads_of(z_ref, 0, heads)
        xo, ro, sgz, _ = _gated_norm(o, zv, og)
        doav = _heads_of(doa_ref, 0, heads)
        don = doav * (zv * sgz)
        dz = doav * (xo * og) * _dsilu(zv, sgz)
        d_og = jnp.sum(jnp.sum(don * xo, axis=1, keepdims=True), axis=0)
        do = _rms_bwd(don * og, xo, ro)
        tm, dmat, eg, edl, egl = cm["tm"], cm["dmat"], cm["eg"], cm["edl"], cm["egl"]
        dos, dsps, sts, tms, ks = _Split(do), _Split(dsp), _Split(state), _Split(tm), _Split(k)
        dvn = _bmm_tn(cm["attn"], dos) + _bmm(cm["kd"], dsps)
        dvns = _Split(dvn)
        dqe = _bmm_nt(dos, sts)
        ds_ref[...] = _bmm_tn(cm["qe"], dos) + dsp * egl[:, :, :1] - _bmm_tn(cm["w"], dvns)
        dattn = _bmm_nt(dos, vn)
        dkd = _bmm_nt(vn, dsps)
        dkd_kd = jnp.sum(dkd * cm["kd"], axis=-1, keepdims=True)
        dgl = (jnp.sum(jnp.sum(dsp * state, axis=-1, keepdims=True), axis=1, keepdims=True) * egl[:, :, :1]
               + jnp.sum(dkd_kd, axis=1, keepdims=True))
        dgc = jnp.sum(dqe * cm["qe"], axis=-1, keepdims=True) - dkd_kd
        dk = dkd * edl
        dq = dqe * eg
        dw = -_bmm_nt(dvns, sts)
        dws = _Split(dw)
        dp0 = dattn * dmat
        dd = jnp.where(cm["incl"], dattn * cm["p0"], 0.0)
        dp0s = _Split(dp0)
        dq = dq + _bmm(dp0s, ks)
        dk = dk + _bmm_tn(dp0s, q)
        dtm = _bmm_nt(dvns, cm["vb"]) + _bmm_nt(dws, cm["kbe"])
        dvb = _bmm_tn(tms, dvns)
        dkbe = _bmm_tn(tms, dws)
        dkb = dkbe * eg
        dgc = dgc + jnp.sum(dkbe * cm["kbe"], axis=-1, keepdims=True)
        dlow = jnp.where(cm["strict"], -_bmm_tn(tms, _bmm_nt(dtm, tms)), 0.0)
        dd = dd + dlow * cm["a0"]
        da0 = dlow * dmat
        da0s = _Split(da0)
        dkb = dkb + _bmm(da0s, ks)
        dk = dk + _bmm_tn(da0s, cm["kb"])
        ddd = dd * dmat
        ones = jnp.ones((heads, c, HEAD_DIM), F32)
        dgc = dgc + jnp.sum(ddd, axis=-1, keepdims=True) - _bmm_tn(ddd, ones)[:, :, :1]
        dgc = dgc + jnp.where(_rows((c, 1)) == c - 1, dgl, 0.0)
        dg = _bmm_tn(cm["inclf"], jnp.broadcast_to(dgc, (heads, c, HEAD_DIM)))[:, :, :1]
        beta = cm["beta"]
        dk = dk + dkb * beta
        dbeta = jnp.sum(dkb * k, axis=-1, keepdims=True) + jnp.sum(dvb * v, axis=-1, keepdims=True)
        dv = dvb * beta
        db_col = dbeta * beta * (1.0 - beta)
        da_col = dg * cm["neg_ea"] * _sigmoid(cm["xg"])
        d_alog = jnp.sum(dg * cm["g"], axis=1, keepdims=True)
        d_dtb = jnp.sum(da_col, axis=1, keepdims=True)
        lane = lax.broadcasted_iota(jnp.int32, (c, LANES), 1)
        lane8 = lax.broadcasted_iota(jnp.int32, (8, LANES), 1)
        row8 = _rows((8, LANES))
        dab = jnp.zeros((c, LANES), F32)
        dpar = jnp.where(row8 == 2, d_og, 0.0)
        for h in range(heads):
            lo = h * HEAD_DIM
            dqkv_ref[:, lo:lo + HEAD_DIM] = dq[h]
            dqkv_ref[:, da + lo:da + lo + HEAD_DIM] = dk[h]
            dqkv_ref[:, 2 * da + lo:2 * da + lo + HEAD_DIM] = dv[h]
            dz_ref[:, lo:lo + HEAD_DIM] = dz[h]
            dab = dab + jnp.where(lane == h, da_col[h], 0.0) + jnp.where(lane == heads + h, db_col[h], 0.0)
            dpar = (dpar + jnp.where((row8 == 0) & (lane8 == h), d_alog[h], 0.0)
                    + jnp.where((row8 == 1) & (lane8 == h), d_dtb[h], 0.0))
        dab_ref[...] = dab
        dpar_ref[...] += dpar

    rev = lambda i: (n - 1 - i, 0)
    return pl.pallas_call(
        body, name="delta_bwd", grid=(n,),
        in_specs=[pl.BlockSpec((c, 3 * da), rev), pl.BlockSpec((c, da), rev), pl.BlockSpec((c, LANES), rev),
                  pl.BlockSpec((8, LANES), lambda i: (0, 0)),
                  pl.BlockSpec((1, heads, HEAD_DIM, HEAD_DIM), lambda i: (n - 1 - i, 0, 0, 0)),
                  pl.BlockSpec((1, 3, heads, c, c), lambda i: (n - 1 - i, 0, 0, 0, 0)),
                  pl.BlockSpec((1, 3, heads, c, HEAD_DIM), lambda i: (n - 1 - i, 0, 0, 0, 0)),
                  pl.BlockSpec((c, da), rev)],
        out_specs=[pl.BlockSpec((c, 3 * da), rev), pl.BlockSpec((c, da), rev), pl.BlockSpec((c, LANES), rev),
                   pl.BlockSpec((8, LANES), lambda i: (0, 0))],
        out_shape=[jax.ShapeDtypeStruct((t, 3 * da), F32), jax.ShapeDtypeStruct((t, da), F32),
                   jax.ShapeDtypeStruct((t, LANES), F32), jax.ShapeDtypeStruct((8, LANES), F32)],
        scratch_shapes=[pltpu.VMEM((heads, HEAD_DIM, HEAD_DIM), F32)],
        compiler_params=_params("arbitrary"))(qkv, z, ab, gpar, states, kept_c, kept_w, doa)


def _w_in_pieces(shard_cols, da, heads):
    a0, nab = 4 * da, 2 * heads
    d_in = 4 * shard_cols
    runs = [(0, a0, 0), (a0, a0 + nab, d_in - nab), (a0 + nab, d_in, a0)]
    pieces = []
    for j in range(4):
        lo, hi = j * shard_cols, (j + 1) * shard_cols
        for rlo, rhi, plo in runs:
            s, e = max(lo, rlo), min(hi, rhi)
            if s < e:
                pieces.append((j, s - lo, e - s, plo + (s - rlo)))
    return pieces, d_in - nab + LANES


def _w_in_pack(w4, da, heads):
    _, d, sc = w4.shape
    pieces, npk = _w_in_pieces(sc, da, heads)
    tr = _tile_rows(d, 256, SUBLANES_WIRE)

    def body(w_ref, o_ref):
        o_ref[:, npk - LANES:] = jnp.zeros((tr, LANES), o_ref.dtype)
        for j, lo, ln, dst in pieces:
            o_ref[:, dst:dst + ln] = w_ref[j, :, lo:lo + ln]

    return pl.pallas_call(
        body, name="w_in_pack", grid=(d // tr,),
        in_specs=[pl.BlockSpec((4, tr, sc), lambda i: (0, i, 0))],
        out_specs=pl.BlockSpec((tr, npk), lambda i: (i, 0)),
        out_shape=jax.ShapeDtypeStruct((d, npk), w4.dtype),
        compiler_params=_params("arbitrary"))(w4)


def _w_in_unpack(dwp, sc, da, heads):
    d, npk = dwp.shape
    pieces, _ = _w_in_pieces(sc, da, heads)
    tr = _tile_rows(d, 256)

    def body(g_ref, o_ref):
        for j, lo, ln, dst in pieces:
            o_ref[j, :, lo:lo + ln] = g_ref[:, dst:dst + ln]

    return pl.pallas_call(
        body, name="w_in_unpack", grid=(d // tr,),
        in_specs=[pl.BlockSpec((tr, npk), lambda i: (i, 0))],
        out_specs=pl.BlockSpec((4, tr, sc), lambda i: (0, i, 0)),
        out_shape=jax.ShapeDtypeStruct((4, d, sc), F32),
        compiler_params=_params("arbitrary"))(dwp)


def _block_diag(pool_w):
    g, gd, _ = pool_w.shape
    out = jnp.zeros((g * gd, g * gd), pool_w.dtype)
    for gi in range(g):
        out = lax.dynamic_update_slice(out, pool_w[gi], (gi * gd, gi * gd))
    return out


def _layer_dims(d):
    heads = (d // 2) // HEAD_DIM
    return heads, heads * HEAD_DIM, d // 4, d // 4


BIG = ("w_in", "w_gate", "w_up", "ple_proj", "w_out", "w_down", "ple_gate")
TRANSPOSED = ("w_gate", "w_up")


def _prepare_layer(small, li):
    d = small["norm1_g"].shape[1]
    heads, _, _, _ = _layer_dims(d)
    gpar = jnp.zeros((8, LANES), F32)
    gpar = gpar.at[0, :heads].set(small["a_log"][li]).at[1, :heads].set(small["dt_bias"][li]).at[2, :].set(small["onorm_g"][li])
    return dict(norm1_g=small["norm1_g"][li][None], conv_qkv=small["conv_qkv"][li], gpar=gpar, pool_bd=_block_diag(small["pool_w"][li]).astype(MM_DTYPE),
                pool_scale=small["pool_scale"][li][None], sconv_w=small["sconv_w"][li], norm2_g=small["norm2_g"][li][None])


def _layer_fwd(x0, p, gw, lw, tm, arrive):
    d = x0.shape[1]
    heads, da, dp, dc = _layer_dims(d)
    segs = (3 * da, da, dp, 3 * dc, LANES)
    lw["w_in_p"] = _w_in_pack(gw["w_in"], da, heads).astype(MM_DTYPE)
    qkv_pre, z, hp, cbcch, ab = _in_proj_fwd(x0, lw["norm1_g"], lw["w_in_p"], segs, tm)
    qkv = _qkv_conv_fwd(qkv_pre, lw["conv_qkv"], heads)
    oa, states, kept_c, kept_w = _delta_fwd(qkv, z, ab, lw["gpar"], heads)
    ob = _pool_fwd(hp, lw["pool_bd"], lw["pool_scale"], dp // POOL_GROUPS)
    oc = _sconv_fwd(cbcch, lw["sconv_w"])
    arrive("mixed", oa)
    x1, h2 = _out_proj_fwd(x0, (oa, ob, oc), gw["w_out"], lw["norm2_g"], tm)
    x2, gp, up = _ffn_fwd(x1, h2, gw["w_gate"], gw["w_up"], gw["w_down"], tm)
    arrive("ffn", x2)
    x3 = _ple_fwd(x2, p, gw["ple_gate"], gw["ple_proj"], tm)
    arrive("end", x3)
    saved = dict(x0=x0, qkv_pre=qkv_pre, z=z, hp=hp, cbcch=cbcch, ab=ab, qkv=qkv, states=states, kept_c=kept_c, kept_w=kept_w, oa=oa, ob=ob, oc=oc,
                 x1=x1, h2=h2, gp=gp, up=up, x2=x2)
    return x3, saved


def _layer_bwd(dx3, p, gw, lw, sv, tm, produced):
    def after_token(tok, arr):
        return arr if tok is None else arr + tok[0, 0]

    d = dx3.shape[1]
    heads, da, dp, dc = _layer_dims(d)
    segs = (3 * da, da, dp, dc, dc, dc, LANES)
    gd = dp // POOL_GROUPS
    dx2, d_ple_gate, d_ple_proj = _ple_bwd(dx3, sv["x2"], p, gw["ple_gate"], gw["ple_proj"], tm)
    dh2, d_w_gate, d_w_up, d_w_down = _ffn_bwd(dx2, sv["h2"], sv["gp"], sv["up"], gw["w_gate"], gw["w_up"], gw["w_down"],
                                               min(tm, 256))
    tok = produced("ffn", dict(w_gate=d_w_gate, w_up=d_w_up, ple_proj=d_ple_proj, w_down=d_w_down, ple_gate=d_ple_gate), dh2)
    dx1, doa, dob, doc, d_w_out, d_norm2 = _out_proj_bwd(dx2, dh2, sv["x1"], after_token(tok, lw["norm2_g"]),
                                                         (sv["oa"], sv["ob"], sv["oc"]), gw["w_out"], tm)
    dcb, dcc, dch, d_sconv = _sconv_bwd(sv["cbcch"], lw["sconv_w"], doc)
    dhp, d_pool_bd, d_pool_scale = _pool_bwd(sv["hp"], lw["pool_bd"], lw["pool_scale"], dob, gd)
    dqkv, dz, dab, dpar = _delta_bwd(sv["qkv"], sv["z"], sv["ab"], lw["gpar"], sv["states"], sv["kept_c"], sv["kept_w"], doa,
                                      heads)
    tok = produced("mixers", {}, dqkv)
    dqkv_pre, d_conv_qkv = _qkv_conv_bwd(sv["qkv_pre"], lw["conv_qkv"], dqkv, heads)
    dsegs = (dqkv_pre, dz, dhp, dcb, dcc, dch, dab)
    dx0, d_w_in_p, d_norm1 = _in_proj_bwd(sv["x0"], after_token(tok, lw["norm1_g"]), lw["w_in_p"], dsegs, dx1, segs, tm)
    per = LANES // gd
    bd = d_pool_bd.reshape(dp // LANES, per, gd, per, gd)
    d_pool_w = jnp.stack([bd[gi // per, gi % per, :, gi % per, :] for gi in range(POOL_GROUPS)])
    big = dict(w_in=_w_in_unpack(d_w_in_p, gw["w_in"].shape[2], da, heads), w_gate=d_w_gate, w_up=d_w_up,
               ple_proj=d_ple_proj, w_out=d_w_out, w_down=d_w_down, ple_gate=d_ple_gate)
    small = dict(norm1_g=d_norm1[0], conv_qkv=d_conv_qkv, a_log=dpar[0, :heads], dt_bias=dpar[1, :heads], onorm_g=dpar[2],
                 pool_w=d_pool_w, pool_scale=d_pool_scale[0], sconv_w=d_sconv, norm2_g=d_norm2[0])
    tok = produced("end", dict(w_in=big["w_in"], w_out=d_w_out), big["w_in"])
    return dx0, big, small, tok


def _local_step(x, p, target, gw, small, produced=None, arrive=None):
    t, d = x.shape
    depth = p.shape[0]
    tm = 512 if t % 512 == 0 else 128
    layers = [_prepare_layer(small, li) for li in range(depth)]
    saved = []
    h = x
    for li in range(depth):
        h, sv = _layer_fwd(h, p[li], gw[li], layers[li], tm,
                           (lambda stage, after, li=li: arrive(li, stage, after)) if arrive else (lambda stage, after: None))
        saved.append(sv)
    dx, loss, d_final = _loss_head(h, target, small["final_g"][None], tm)
    big, sm = [None] * depth, [None] * depth
    token = None
    for li in reversed(range(depth)):
        p_li = p[li] if token is None else p[li] + token[0, 0]
        dx, big[li], sm[li], token = _layer_bwd(
            dx, p_li, gw[li], layers[li], saved[li], tm,
            (lambda stage, grads, after, li=li: produced(li, stage, grads, after)) if produced else (lambda *a: None))
    small_grads = {n: jnp.stack([g[n] for g in sm]) for n in sm[0]}
    small_grads["final_g"] = d_final[0]
    return loss[0, 0], dx, big, small_grads


def _coords():
    return lax.axis_index("x"), lax.axis_index("y"), lax.axis_index("c")


def _other_chips(x, y):
    return [(1 - x, y), (x, 1 - y), (1 - x, 1 - y)]


def _place_shards(ws, me_idx):
    nt = len(ws)
    depth = ws[0].shape[0]

    def body(me_ref, *refs):
        for t, w_ref in enumerate(refs[:nt]):
            for li in range(depth):
                refs[nt + li * nt + t][...] = w_ref[li].astype(WIRE_DTYPE)

    outs = pl.pallas_call(
        body, name="place_shards",
        grid_spec=pltpu.PrefetchScalarGridSpec(
            num_scalar_prefetch=1, grid=(4,),
            in_specs=[pl.BlockSpec((depth, w.shape[1] // 4, w.shape[2]), lambda i, me_ref: (0, i, 0)) for w in ws],
            out_specs=[pl.BlockSpec((None, w.shape[1] // 4, w.shape[2]), lambda i, me_ref: (me_ref[0], i, 0))
                       for _ in range(depth) for w in ws]),
        out_shape=[jax.ShapeDtypeStruct((4,) + w.shape[1:], WIRE_DTYPE) for _ in range(depth) for w in ws],
        compiler_params=_params("arbitrary"))(me_idx, *ws)
    return [list(outs[li * nt:(li + 1) * nt]) for li in range(depth)]


def _half_block(ref, chip, pc):
    rh = ref.shape[1] // 2
    return ref.at[chip, pl.ds(pc * rh, rh)]


def _gather_copies(out_refs, send_sems, recv_sems, stage):
    nt = len(out_refs)
    x, y, c = _coords()
    pairs = []
    for j, (cx, cy) in enumerate(_other_chips(x, y)):
        for t in range(nt):
            sems = dict(send_sem=send_sems[j * nt + t], recv_sem=recv_sems[j * nt + t], device_id_type=MESH)
            if stage == 0:
                mine, theirs, to = _half_block(out_refs[t], 2 * x + y, c), _half_block(out_refs[t], 2 * cx + cy, c), (cx, cy, c)
            else:
                mine, theirs, to = (_half_block(out_refs[t], 2 * cx + cy, c), _half_block(out_refs[t], 2 * cx + cy, 1 - c),
                                    (x, y, 1 - c))
            pairs.append((pltpu.make_async_remote_copy(src_ref=mine, dst_ref=mine, device_id=to, **sems),
                          pltpu.make_async_remote_copy(src_ref=theirs, dst_ref=theirs, device_id=to, **sems)))
    return pairs


def _all_gather_chips(placed):
    nt = len(placed)

    def body(*refs):
        out_refs = refs[nt:2 * nt]
        send_sems, recv_sems = refs[2 * nt:]
        nc = 3 * nt
        first = _gather_copies(out_refs, [send_sems.at[k] for k in range(nc)], [recv_sems.at[k] for k in range(nc)], 0)
        passed = _gather_copies(out_refs, [send_sems.at[nc + k] for k in range(nc)], [recv_sems.at[nc + k] for k in range(nc)], 1)
        for start, _ in first:
            start.start()
        for (_, arrival), (forward, _) in zip(first, passed):
            arrival.wait_recv()
            forward.start()
        for _, arrival in passed:
            arrival.wait_recv()
        for start, _ in first + passed:
            start.wait_send()

    return pl.pallas_call(
        body, name="all_gather_chips", out_shape=[jax.ShapeDtypeStruct(a.shape, a.dtype) for a in placed],
        in_specs=[ANY] * nt, out_specs=[ANY] * nt, input_output_aliases={t: t for t in range(nt)},
        scratch_shapes=[pltpu.SemaphoreType.DMA((6 * nt,)), pltpu.SemaphoreType.DMA((6 * nt,))],
    )(*placed)


def _gather_call(name, arrs, wait_sems, after, stage):
    nt = len(arrs)
    nc = 3 * nt
    n_wait = len(wait_sems)
    n_new = 2 * nc if stage < 2 else 0
    arrs = [pltpu.with_memory_space_constraint(a, pltpu.HBM) for a in arrs]

    def body(*refs):
        a_refs = refs[:nt]
        waits = refs[nt:nt + n_wait]
        news = refs[nt + n_wait + 1:nt + n_wait + 1 + n_new]
        token = refs[-1]
        if stage > 0:
            for start, arrival in _gather_copies(a_refs, waits[:nc], waits[nc:], stage - 1):
                start.wait_send()
                arrival.wait_recv()
        if stage < 2:
            for start, _ in _gather_copies(a_refs, news[:nc], news[nc:], stage):
                start.start()
        token[...] = jnp.zeros_like(token)

    outs = pl.pallas_call(
        body, name=name,
        out_shape=(*[pltpu.SemaphoreType.DMA(())] * n_new, *[pltpu.HBM(a.shape, a.dtype) for a in arrs],
                   jax.ShapeDtypeStruct((8, LANES), F32)),
        in_specs=[HBM] * nt + [SEM] * n_wait + [ANY],
        out_specs=(*[SEM] * n_new, *[HBM] * nt, pl.BlockSpec(memory_space=pltpu.VMEM)),
        input_output_aliases={t: n_new + t for t in range(nt)},
        compiler_params=pltpu.CompilerParams(has_side_effects=pltpu.SideEffectType.DATAFLOW_SIDE_EFFECTING),
    )(*arrs, *wait_sems, after)
    return list(outs[:n_new]), list(outs[n_new:n_new + nt]), outs[-1]


def _sibling_swap_half(gs):
    nt = len(gs)

    def body(*refs):
        g_refs, out_refs = refs[:nt], refs[nt:2 * nt]
        send_sems, recv_sems = refs[2 * nt:]
        x, y, c = _coords()
        cps = []
        for t in range(nt):
            rh = g_refs[t].shape[1] // 2
            cps.append(pltpu.make_async_remote_copy(src_ref=g_refs[t].at[:, pl.ds((1 - c) * rh, rh)], dst_ref=out_refs[t],
                                                    send_sem=send_sems.at[t], recv_sem=recv_sems.at[t], device_id=(x, y, 1 - c),
                                                    device_id_type=MESH))
        for cp in cps:
            cp.start()
        for cp in cps:
            cp.wait()

    return pl.pallas_call(
        body, name="sibling_swap_half",
        out_shape=[jax.ShapeDtypeStruct((g.shape[0], g.shape[1] // 2, g.shape[2]), g.dtype) for g in gs],
        in_specs=[ANY] * nt, out_specs=[ANY] * nt,
        scratch_shapes=[pltpu.SemaphoreType.DMA((nt,)), pltpu.SemaphoreType.DMA((nt,))])(*gs)


def _add_my_halves(gs, others, c_idx):
    nt = len(gs)

    def body(c_ref, *refs):
        for g_ref, o_ref, out_ref in zip(refs[:nt], refs[nt:2 * nt], refs[2 * nt:]):
            out_ref[...] = (g_ref[...].astype(F32) + o_ref[...].astype(F32)).astype(out_ref.dtype)

    def quarter(g):
        return pl.BlockSpec((None, g.shape[1] // 4, g.shape[2]), lambda j, i, c_ref: (j, i, 0))

    return pl.pallas_call(
        body, name="add_my_halves",
        grid_spec=pltpu.PrefetchScalarGridSpec(
            num_scalar_prefetch=1, grid=(4, 2),
            in_specs=[pl.BlockSpec((None, g.shape[1] // 4, g.shape[2]), lambda j, i, c_ref: (j, 2 * c_ref[0] + i, 0)) for g in gs]
                     + [quarter(g) for g in gs],
            out_specs=[quarter(g) for g in gs]),
        out_shape=[jax.ShapeDtypeStruct((4, g.shape[1] // 2, g.shape[2]), WIRE_DTYPE) for g in gs],
        compiler_params=_params("arbitrary", "arbitrary"))(c_idx, *gs, *others)


def _exchange_chips(parts):
    nt = len(parts)

    def body(*refs):
        p_refs, out_refs = refs[:nt], refs[nt:2 * nt]
        send_sems, recv_sems = refs[2 * nt:]
        x, y, c = _coords()
        chips = _other_chips(x, y)

        def copy(j, t):
            cx, cy = chips[j]
            return pltpu.make_async_remote_copy(src_ref=p_refs[t].at[2 * cx + cy], dst_ref=out_refs[t].at[j],
                                                send_sem=send_sems.at[j, t], recv_sem=recv_sems.at[j, t], device_id=(cx, cy, c),
                                                device_id_type=MESH)

        sends = [copy(j, t) for j in range(3) for t in range(nt)]
        for cp in sends:
            cp.start()
        for cp in sends:
            cp.wait_recv()
        for cp in sends:
            cp.wait_send()

    return pl.pallas_call(
        body, name="exchange_chips", out_shape=[jax.ShapeDtypeStruct((3,) + p.shape[1:], p.dtype) for p in parts],
        in_specs=[ANY] * nt, out_specs=[ANY] * nt,
        scratch_shapes=[pltpu.SemaphoreType.DMA((3, nt)), pltpu.SemaphoreType.DMA((3, nt))])(*parts)


def _split_plan(kind, s_refs, l_refs):
    x, y, c = _coords()
    if kind == "swap":
        return [(s.at[:, pl.ds((1 - c) * (s.shape[1] // 2), s.shape[1] // 2)], l, (x, y, 1 - c)) for s, l in zip(s_refs, l_refs)]
    return [(s.at[2 * cx + cy], l.at[j], (cx, cy, c)) for j, (cx, cy) in enumerate(_other_chips(x, y))
            for s, l in zip(s_refs, l_refs)]


def _split_landing(kind, a):
    return (a.shape[0], a.shape[1] // 2, a.shape[2]) if kind == "swap" else (3,) + a.shape[1:]


def _copies_start(name, kind, srcs, after=None):
    ns = len(srcs)
    n = ns if kind == "swap" else 3 * ns
    srcs = [pltpu.with_memory_space_constraint(a, pltpu.HBM) for a in srcs]
    lands = [pltpu.with_memory_space_constraint(lax.empty(_split_landing(kind, a), a.dtype), pltpu.HBM) for a in srcs]
    extra = [] if after is None else [after]

    def body(*refs):
        first_sem = 2 * ns + len(extra)
        sems, token = refs[first_sem:first_sem + 2 * n], refs[-1]
        for k, (src, dst, dev) in enumerate(_split_plan(kind, refs[:ns], refs[ns:2 * ns])):
            pltpu.make_async_remote_copy(src_ref=src, dst_ref=dst, send_sem=sems[k], recv_sem=sems[n + k], device_id=dev,
                                         device_id_type=MESH).start()
        token[...] = jnp.zeros_like(token)

    outs = pl.pallas_call(
        body, name=name,
        out_shape=(*[pltpu.SemaphoreType.DMA(())] * (2 * n), *[pltpu.HBM(a.shape, a.dtype) for a in srcs + lands],
                   jax.ShapeDtypeStruct((8, LANES), F32)),
        in_specs=[HBM] * (2 * ns) + [ANY] * len(extra),
        out_specs=(*[SEM] * (2 * n), *[HBM] * (2 * ns), pl.BlockSpec(memory_space=pltpu.VMEM)),
        input_output_aliases={t: 2 * n + t for t in range(2 * ns)},
        compiler_params=pltpu.CompilerParams(has_side_effects=pltpu.SideEffectType.DATAFLOW_SIDE_EFFECTING),
    )(*srcs, *lands, *extra)
    return list(outs[:2 * n]), list(outs[2 * n:2 * n + ns]), list(outs[2 * n + ns:2 * n + 2 * ns]), outs[-1]


def _copies_wait(name, kind, sems, srcs, lands, after):
    ns = len(srcs)
    n = len(sems) // 2

    def body(*refs):
        sem_refs = refs[2 * ns:2 * ns + 2 * n]
        for k, (src, dst, dev) in enumerate(_split_plan(kind, refs[:ns], refs[ns:2 * ns])):
            cp = pltpu.make_async_remote_copy(src_ref=src, dst_ref=dst, send_sem=sem_refs[k], recv_sem=sem_refs[n + k],
                                              device_id=dev, device_id_type=MESH)
            cp.wait_send()
            cp.wait_recv()

    outs = pl.pallas_call(
        body, name=name, out_shape=tuple(pltpu.HBM(a.shape, a.dtype) for a in srcs + lands),
        in_specs=[HBM] * (2 * ns) + [SEM] * (2 * n) + [ANY], out_specs=tuple([HBM] * (2 * ns)),
        input_output_aliases={t: t for t in range(2 * ns)},
        compiler_params=pltpu.CompilerParams(has_side_effects=pltpu.SideEffectType.DATAFLOW_SIDE_EFFECTING),
    )(*srcs, *lands, *sems, after)
    return list(outs[:ns]), list(outs[ns:])


def _sum_into(pairs, recvs, idx, li, depth, accs):
    nt = len(pairs)

    def body(idx_ref, *refs):
        for p_ref, r_ref, out_ref in zip(refs[:nt], refs[nt:2 * nt], refs[-nt:]):
            out_ref[...] = p_ref[...].astype(F32) + r_ref[0].astype(F32) + r_ref[1].astype(F32) + r_ref[2].astype(F32)

    in_specs = ([pl.BlockSpec((None, p.shape[1] // 2, p.shape[2]), lambda i, idx_ref: (idx_ref[0], i, 0)) for p in pairs]
                + [pl.BlockSpec((3, p.shape[1] // 2, p.shape[2]), lambda i, idx_ref: (0, i, 0)) for p in pairs])
    args = [idx, *pairs, *recvs]
    aliases = {}
    if accs[0] is not None:
        in_specs += [ANY] * nt
        args += list(accs)
        aliases = {1 + 2 * nt + t: t for t in range(nt)}
    return pl.pallas_call(
        body, name="sum_into",
        grid_spec=pltpu.PrefetchScalarGridSpec(
            num_scalar_prefetch=1, grid=(2,), in_specs=in_specs,
            out_specs=[pl.BlockSpec((None, p.shape[1] // 2, p.shape[2]), lambda i, idx_ref: (li, 2 * idx_ref[1] + i, 0))
                       for p in pairs]),
        out_shape=[jax.ShapeDtypeStruct((depth, 2 * p.shape[1], p.shape[2]), F32) for p in pairs],
        input_output_aliases=aliases, compiler_params=_params("arbitrary"))(*args)


def _sum_slots(parts):
    n, rows, cols = parts.shape
    tr = _tile_rows(rows, 512, SUBLANES_WIRE)

    def body(p_ref, out_ref):
        acc = p_ref[0].astype(F32)
        for s in range(1, n):
            acc = acc + p_ref[s].astype(F32)
        out_ref[...] = acc

    return pl.pallas_call(
        body, name="sum_slots", grid=(rows // tr,),
        in_specs=[pl.BlockSpec((n, tr, cols), lambda i: (0, i, 0))],
        out_specs=pl.BlockSpec((tr, cols), lambda i: (i, 0)),
        out_shape=jax.ShapeDtypeStruct((rows, cols), F32),
        compiler_params=_params("arbitrary"))(parts)


def _sibling_share(gs, li):
    nt = len(gs)

    def body(*refs):
        out_refs = refs[nt:2 * nt]
        send_sems, recv_sems = refs[2 * nt:]
        x, y, c = _coords()
        sends, recvs = [], []
        for t in range(nt):
            rh = out_refs[t].shape[1] // 2
            mine, theirs = out_refs[t].at[li, pl.ds(c * rh, rh)], out_refs[t].at[li, pl.ds((1 - c) * rh, rh)]
            sems = dict(send_sem=send_sems.at[t], recv_sem=recv_sems.at[t], device_id=(x, y, 1 - c), device_id_type=MESH)
            sends.append(pltpu.make_async_remote_copy(src_ref=mine, dst_ref=mine, **sems))
            recvs.append(pltpu.make_async_remote_copy(src_ref=theirs, dst_ref=theirs, **sems))
        for cp in sends:
            cp.start()
        for cp in recvs:
            cp.wait_recv()
        for cp in sends:
            cp.wait_send()

    return pl.pallas_call(
        body, name="sibling_share", out_shape=[jax.ShapeDtypeStruct(g.shape, g.dtype) for g in gs],
        in_specs=[ANY] * nt, out_specs=[ANY] * nt, input_output_aliases={t: t for t in range(nt)},
        scratch_shapes=[pltpu.SemaphoreType.DMA((nt,)), pltpu.SemaphoreType.DMA((nt,))])(*gs)


def _all_gather_devices(buf, after=None):
    extra = [] if after is None else [after]

    def body(b_ref, *rest):
        out_ref, send_sems, recv_sems, local_sem = rest[len(extra):]
        x, y, c = _coords()
        me = 4 * x + 2 * y + c
        mine = pltpu.make_async_copy(b_ref, out_ref.at[me], local_sem)
        mine.start()
        peers = []
        for k in range(1, 8):
            fx, fy, fc = (k >> 2) & 1, (k >> 1) & 1, k & 1
            peers.append((x ^ fx, y ^ fy, c ^ fc))
        sends = [pltpu.make_async_remote_copy(src_ref=b_ref, dst_ref=out_ref.at[me], send_sem=send_sems.at[k],
                                              recv_sem=recv_sems.at[k], device_id=peer, device_id_type=MESH)
                 for k, peer in enumerate(peers)]
        for cp in sends:
            cp.start()
        for k, (px, py, pc) in enumerate(peers):
            pltpu.make_async_remote_copy(src_ref=b_ref, dst_ref=out_ref.at[4 * px + 2 * py + pc], send_sem=send_sems.at[k],
                                         recv_sem=recv_sems.at[k], device_id=(px, py, pc), device_id_type=MESH).wait_recv()
        for cp in sends:
            cp.wait_send()
        mine.wait()

    return pl.pallas_call(
        body, name="all_gather_devices", out_shape=jax.ShapeDtypeStruct((8,) + buf.shape, buf.dtype),
        in_specs=[ANY] * (1 + len(extra)), out_specs=ANY,
        scratch_shapes=[pltpu.SemaphoreType.DMA((7,)), pltpu.SemaphoreType.DMA((7,)), pltpu.SemaphoreType.DMA(())])(buf, *extra)


def _pair_sums(big_grads, c_idx):
    gs = [big_grads[n] for n in BIG]
    return _add_my_halves(gs, _sibling_swap_half(gs), c_idx)


SMALL_SHARDED = ("conv_qkv", "sconv_w")
REPLICATED = ("norm1_g", "a_log", "dt_bias", "onorm_g", "pool_w", "pool_scale", "norm2_g", "final_g")
ALL_WEIGHTS = ("norm1_g", "w_in", "conv_qkv", "a_log", "dt_bias", "onorm_g", "pool_w", "pool_scale", "sconv_w", "w_out",
               "norm2_g", "w_gate", "w_up", "w_down", "ple_proj", "ple_gate", "final_g")


def _pad_rows(flat, row_multiple):
    m = flat.shape[0]
    r = -(-m // (LANES * row_multiple)) * row_multiple
    return jnp.pad(flat, (0, r * LANES - m)).reshape(r, LANES)


def _adamw_math(w, g, m, v):
    c1 = 1.0 / (1.0 - ADAM_B1 ** ADAM_STEP)
    c2 = 1.0 / (1.0 - ADAM_B2 ** ADAM_STEP)
    nm = ADAM_B1 * m + (1.0 - ADAM_B1) * g
    nv = ADAM_B2 * v + (1.0 - ADAM_B2) * (g * g)
    return -ADAM_LR * ((nm * c1) / (jnp.sqrt(nv * c2) + ADAM_EPS) + ADAM_WD * w), nm, nv


def _adamw(w, g, m, v):
    shape = w.shape
    cols = shape[-1]
    rows = w.size // cols
    tr = _tile_rows(rows, 512)

    def body(w_ref, g_ref, m_ref, v_ref, d_ref, nm_ref, nv_ref, go_ref):
        gv = g_ref[...]
        d_ref[...], nm_ref[...], nv_ref[...] = _adamw_math(w_ref[...], gv, m_ref[...], v_ref[...])
        go_ref[...] = gv

    spec = pl.BlockSpec((tr, cols), lambda i: (i, 0))
    outs = pl.pallas_call(
        body, name="adamw", grid=(rows // tr,), in_specs=[spec] * 4, out_specs=[spec] * 4,
        out_shape=[jax.ShapeDtypeStruct((rows, cols), F32)] * 4,
        compiler_params=_params("arbitrary"))(*[a.reshape(rows, cols) for a in (w, g, m, v)])
    return tuple(o.reshape(shape) for o in outs)


def kernel(x, p, norm1_g, w_in, conv_qkv, a_log, dt_bias, onorm_g, pool_w, pool_scale, sconv_w, w_out, norm2_g, w_gate, w_up, w_down, ple_proj, ple_gate, final_g, loss_target, m_norm1_g, m_w_in, m_conv_qkv, m_a_log, m_dt_bias, m_onorm_g, m_pool_w, m_pool_scale, m_sconv_w, m_w_out, m_norm2_g, m_w_gate, m_w_up, m_w_down, m_ple_proj, m_ple_gate, m_final_g, v_norm1_g, v_w_in, v_conv_qkv, v_a_log, v_dt_bias, v_onorm_g, v_pool_w, v_pool_scale, v_sconv_w, v_w_out, v_norm2_g, v_w_gate, v_w_up, v_w_down, v_ple_proj, v_ple_gate, v_final_g):
    weights = dict(zip(ALL_WEIGHTS, (norm1_g, w_in, conv_qkv, a_log, dt_bias, onorm_g, pool_w, pool_scale, sconv_w, w_out,
                                     norm2_g, w_gate, w_up, w_down, ple_proj, ple_gate, final_g)))
    mom_m = dict(zip(ALL_WEIGHTS, (m_norm1_g, m_w_in, m_conv_qkv, m_a_log, m_dt_bias, m_onorm_g, m_pool_w, m_pool_scale,
                                   m_sconv_w, m_w_out, m_norm2_g, m_w_gate, m_w_up, m_w_down, m_ple_proj, m_ple_gate, m_final_g)))
    mom_v = dict(zip(ALL_WEIGHTS, (v_norm1_g, v_w_in, v_conv_qkv, v_a_log, v_dt_bias, v_onorm_g, v_pool_w, v_pool_scale,
                                   v_sconv_w, v_w_out, v_norm2_g, v_w_gate, v_w_up, v_w_down, v_ple_proj, v_ple_gate, v_final_g)))
    for n in TRANSPOSED:
        weights[n], mom_m[n], mom_v[n] = (jnp.swapaxes(a[n], 1, 2) for a in (weights, mom_m, mom_v))
    c_idx = lax.axis_index("c").astype(jnp.int32).reshape(1)
    chip = (2 * lax.axis_index("x") + lax.axis_index("y")).astype(jnp.int32)
    me_idx = chip.reshape(1)
    idx = jnp.stack([chip, lax.axis_index("c").astype(jnp.int32)])
    depth = p.shape[0]

    placed = _place_shards([weights[n] for n in BIG], me_idx)
    gw = [dict() for _ in range(depth)]
    gw[0]["w_in"] = _all_gather_chips(placed[0][:1])[0]
    early = ("w_in", "w_out")
    late = tuple(n for n in BIG if n not in early)
    groups = [dict(li=0, names=BIG[1:], forward=(0, "mixed"), finish=(0, "mixed"))]
    for li in range(1, depth):
        groups.append(dict(li=li, names=early, forward=(li - 1, "ffn"), finish=(li - 1, "end")))
        groups.append(dict(li=li, names=late, forward=(li, "mixed"), finish=(li, "mixed")))
    def arrive(li, stage, after):
        for k, g in enumerate(groups):
            if g["forward"] == (li, stage):
                g["sems"], g["arrs"], _ = _gather_call("gather_forward_%d" % k, g["arrs"], g["sems"], after, 1)
            if g["finish"] == (li, stage):
                _, g["arrs"], _ = _gather_call("gather_finish_%d" % k, g["arrs"], g["sems"], after, 2)
                gw[g["li"]].update(zip(g["names"], g["arrs"]))

    small = {n: weights[n] for n in REPLICATED}
    sflat = _pad_rows(jnp.concatenate([weights[n].reshape(-1) for n in SMALL_SHARDED]), 8)
    sgath8 = _all_gather_devices(sflat, gw[0]["w_in"])
    sgath = sgath8[0::2].reshape(4, -1)
    off = 0
    for n in SMALL_SHARDED:
        shp = weights[n].shape
        part = sgath[:, off:off + weights[n].size].reshape((4,) + shp)
        small[n] = jnp.moveaxis(part, 0, -2).reshape(shp[:-1] + (4 * shp[-1],))
        off += weights[n].size
    token = sgath8
    for k, g in enumerate(groups):
        arrs = [placed[g["li"]][BIG.index(n)] for n in g["names"]]
        g["sems"], g["arrs"], token = _gather_call("gather_start_%d" % k, arrs, [], token, 0)

    small["norm1_g"] = small["norm1_g"] + token[0, 0]

    pending = []
    last_token = [None]

    def advance(g, after):
        if g["stage"] == 0:
            gs, others = _copies_wait("swap_wait_" + g["tag"], "swap", *g["handle"], after)
            g["handle"] = _copies_start("exchange_start_" + g["tag"], "exchange", _add_my_halves(gs, others, c_idx))
            g["stage"] = 1
            return g["handle"][3]
        return None

    def produced(li, stage, grads, after):
        token = None
        for g in pending:
            token = advance(g, after) if g["stage"] == 0 else token
        if grads:
            names = [n for n in BIG if n in grads]
            handle = _copies_start("swap_start_%d%s" % (li, stage), "swap", [grads[n] for n in names], token)
            pending.append(dict(li=li, names=names, tag="%d%s" % (li, stage), stage=0, handle=handle[:3]))
            token = handle[3]
        last_token[0] = last_token[0] if token is None else token
        return token

    loss_local, dx, _, small_grads = _local_step(x[0], p[:, 0], loss_target[0], gw, small, produced, arrive)
    accs, big_outs = {}, {}

    def finish(g, after):
        pairs, recvs = _copies_wait("exchange_wait_" + g["tag"], "exchange", *g["handle"][:3], after)
        summed = _sum_into(pairs, recvs, idx, g["li"], depth, [accs.get(n) for n in g["names"]])
        accs.update(zip(g["names"], _sibling_share(summed, g["li"])))
        return accs[g["names"][-1]]

    def update(names):
        for n in names:
            big_outs[n] = _adamw(weights[n], accs[n], mom_m[n], mom_v[n])
        return big_outs[names[-1]][0]

    done = finish(pending[0], last_token[0])
    done = advance(pending[-1], done)
    for g in pending[1:-1]:
        done = finish(g, done)
    last = pending[-1]["names"]
    done = update([n for n in BIG if n not in last])
    finish(pending[-1], done)
    update(last)


    gshard = {}
    rnames = REPLICATED + SMALL_SHARDED
    rflat = _pad_rows(jnp.concatenate([small_grads[n].reshape(-1) for n in rnames]), 8)
    rsum = _sum_slots(_all_gather_devices(rflat)).reshape(-1)
    off = 0
    for n in rnames:
        whole = rsum[off:off + small_grads[n].size].reshape(small_grads[n].shape)
        off += small_grads[n].size
        if n in SMALL_SHARDED:
            cols = weights[n].shape[-1]
            whole = lax.dynamic_slice_in_dim(whole, chip * cols, cols, axis=whole.ndim - 1)
        gshard[n] = whole

    loss = lax.psum(loss_local, ("x", "y", "c"))
    deltas, new_m, new_v, grad_out = {}, {}, {}, {}
    for n in ALL_WEIGHTS:
        if n in BIG:
            deltas[n], new_m[n], new_v[n], grad_out[n] = big_outs[n]
        else:
            deltas[n], new_m[n], new_v[n], grad_out[n] = _adamw(weights[n], gshard[n], mom_m[n], mom_v[n])
    for n in TRANSPOSED:
        deltas[n], new_m[n], new_v[n], grad_out[n] = (jnp.swapaxes(a[n], 1, 2) for a in (deltas, new_m, new_v, grad_out))
    return (loss, dx[None], *[grad_out[n] for n in ALL_WEIGHTS], *[deltas[n] for n in ALL_WEIGHTS],
            *[new_m[n] for n in ALL_WEIGHTS], *[new_v[n] for n in ALL_WEIGHTS])
```

```python
import jax
import jax.numpy as jnp
from jax import lax
from jax.experimental import pallas as pl
from jax.experimental.pallas import tpu as pltpu

F32 = jnp.float32
MM_DTYPE = jnp.bfloat16
WIRE_DTYPE = jnp.bfloat16
HI = lax.Precision.HIGHEST
EPS = 1e-6
HEAD_DIM = 128
CHUNK = 64
QKV_CONV_WIDTH = 4
SCONV_WIDTH = 3
POOL_GROUPS = 4
LANES = 128
SUBLANES_WIRE = 16
VMEM_LIMIT_BYTES = 56 * 1024 * 1024
ADAM_LR, ADAM_B1, ADAM_B2, ADAM_EPS, ADAM_WD, ADAM_STEP = 0.001, 0.9, 0.999, 1e-08, 0.01, 10
MESH = pl.DeviceIdType.MESH
ANY = pl.BlockSpec(memory_space=pl.ANY)
HBM = pl.BlockSpec(memory_space=pltpu.HBM)
SEM = pl.BlockSpec(memory_space=pltpu.SEMAPHORE)


def _params(*sem):
    return pltpu.CompilerParams(vmem_limit_bytes=VMEM_LIMIT_BYTES, dimension_semantics=sem if sem else None)


def _mm(a, b):
    return jnp.dot(a.astype(MM_DTYPE), b.astype(MM_DTYPE), preferred_element_type=F32)


def _mm_nt(a, b):
    return lax.dot_general(a.astype(MM_DTYPE), b.astype(MM_DTYPE), (((1,), (1,)), ((), ())), preferred_element_type=F32)


def _mm_tn(a, b):
    return lax.dot_general(a.astype(MM_DTYPE), b.astype(MM_DTYPE), (((0,), (0,)), ((), ())), preferred_element_type=F32)


def _hmm(a, b):
    return jnp.dot(a, b, preferred_element_type=F32, precision=HI)


def _hmm_nt(a, b):
    return lax.dot_general(a, b, (((1,), (1,)), ((), ())), preferred_element_type=F32, precision=HI)


def _hmm_tn(a, b):
    return lax.dot_general(a, b, (((0,), (0,)), ((), ())), preferred_element_type=F32, precision=HI)


def _sigmoid(x):
    return 1.0 / (1.0 + jnp.exp(-x))


def _dsilu(x, s):
    return s * (1.0 + x * (1.0 - s))


def _rows(shape):
    return lax.broadcasted_iota(jnp.int32, shape, 0)


def _shift_down(x, s):
    if s == 0:
        return x
    return jnp.where(_rows(x.shape) >= s, pltpu.roll(x, s, 0), 0.0)


def _shift_up(x, s):
    if s == 0:
        return x
    t = x.shape[0]
    return jnp.where(_rows(x.shape) < t - s, pltpu.roll(x, t - s, 0), 0.0)


def _rms_fwd(x):
    r = lax.rsqrt(jnp.mean(x * x, axis=-1, keepdims=True) + EPS)
    return x * r, r


def _rms_bwd(dxn, xn, r):
    return r * (dxn - xn * jnp.mean(dxn * xn, axis=-1, keepdims=True))


def _tile_rows(n, cap, mult=8):
    best = None
    for d in range(mult, min(n, cap) + 1, mult):
        if n % d == 0:
            best = d
    return best if best is not None else n


def _in_proj_fwd(x, g1, wp, segs, tm):
    t, d = x.shape
    npk = wp.shape[1]

    def body(x_ref, g_ref, w_ref, *o_refs):
        xn, _ = _rms_fwd(x_ref[...])
        h = (xn * g_ref[...]).astype(w_ref.dtype)
        off = 0
        for o_ref, wd in zip(o_refs, segs):
            o_ref[...] = jnp.dot(h, w_ref[:, off:off + wd], preferred_element_type=F32)
            off += wd

    return pl.pallas_call(
        body, name="in_proj_fwd", grid=(t // tm,),
        in_specs=[pl.BlockSpec((tm, d), lambda i: (i, 0)), pl.BlockSpec((1, d), lambda i: (0, 0)),
                  pl.BlockSpec((d, npk), lambda i: (0, 0))],
        out_specs=[pl.BlockSpec((tm, wd), lambda i: (i, 0)) for wd in segs],
        out_shape=[jax.ShapeDtypeStruct((t, wd), F32) for wd in segs],
        compiler_params=_params("arbitrary"))(x, g1, wp)


def _in_proj_bwd(x, g1, wp, dsegs, dx_res, segs, tm):
    t, d = x.shape
    npk = wp.shape[1]
    nseg = len(segs)

    def body(x_ref, g_ref, w_ref, *rest):
        ds_refs = rest[:nseg]
        dxr_ref, dx_ref, dw_ref, dg_ref = rest[nseg:]
        i = pl.program_id(0)

        @pl.when(i == 0)
        def _():
            dw_ref[...] = jnp.zeros_like(dw_ref)
            dg_ref[...] = jnp.zeros_like(dg_ref)

        xn, r = _rms_fwd(x_ref[...])
        g = g_ref[...]
        h = (xn * g).astype(w_ref.dtype)
        dh = jnp.zeros((tm, d), F32)
        off = 0
        for ds_ref, wd in zip(ds_refs, segs):
            dsv = ds_ref[...].astype(w_ref.dtype)
            dh = dh + lax.dot_general(dsv, w_ref[:, off:off + wd], (((1,), (1,)), ((), ())), preferred_element_type=F32)
            dw_ref[:, off:off + wd] += lax.dot_general(h, dsv, (((0,), (0,)), ((), ())), preferred_element_type=F32)
            off += wd
        dg_ref[...] += jnp.sum(dh * xn, axis=0, keepdims=True)
        dx_ref[...] = dxr_ref[...] + _rms_bwd(dh * g, xn, r)

    return pl.pallas_call(
        body, name="in_proj_bwd", grid=(t // tm,),
        in_specs=[pl.BlockSpec((tm, d), lambda i: (i, 0)), pl.BlockSpec((1, d), lambda i: (0, 0)),
                  pl.BlockSpec((d, npk), lambda i: (0, 0))]
                 + [pl.BlockSpec((tm, wd), lambda i: (i, 0)) for wd in segs]
                 + [pl.BlockSpec((tm, d), lambda i: (i, 0))],
        out_specs=[pl.BlockSpec((tm, d), lambda i: (i, 0)), pl.BlockSpec((d, npk), lambda i: (0, 0)),
                   pl.BlockSpec((1, d), lambda i: (0, 0))],
        out_shape=[jax.ShapeDtypeStruct((t, d), F32), jax.ShapeDtypeStruct((d, npk), F32),
                   jax.ShapeDtypeStruct((1, d), F32)],
        compiler_params=_params("arbitrary"))(x, g1, wp, *dsegs, dx_res)


def _out_proj_fwd(x0, mix, wo, g2, tm):
    t, d = x0.shape
    dq = wo.shape[1]
    widths = [m.shape[1] for m in mix]

    def body(x_ref, *rest):
        m_refs = rest[:len(mix)]
        w_ref, g_ref, x1_ref, h2_ref = rest[len(mix):]
        acc = x_ref[...]
        off = 0
        for m_ref, wd in zip(m_refs, widths):
            for k in range(wd // dq):
                acc = acc + jnp.dot(m_ref[:, k * dq:(k + 1) * dq].astype(w_ref.dtype), w_ref[off // dq + k],
                                    preferred_element_type=F32)
            off += wd
        x1_ref[...] = acc
        xn, _ = _rms_fwd(acc)
        h2_ref[...] = (xn * g_ref[...]).astype(h2_ref.dtype)

    return pl.pallas_call(
        body, name="out_proj_fwd", grid=(t // tm,),
        in_specs=[pl.BlockSpec((tm, d), lambda i: (i, 0))]
                 + [pl.BlockSpec((tm, wd), lambda i: (i, 0)) for wd in widths]
                 + [pl.BlockSpec((4, dq, d), lambda i: (0, 0, 0)), pl.BlockSpec((1, d), lambda i: (0, 0))],
        out_specs=[pl.BlockSpec((tm, d), lambda i: (i, 0)), pl.BlockSpec((tm, d), lambda i: (i, 0))],
        out_shape=[jax.ShapeDtypeStruct((t, d), F32), jax.ShapeDtypeStruct((t, d), MM_DTYPE)],
        compiler_params=_params("arbitrary"))(x0, *mix, wo, g2)


def _out_proj_bwd(dx2, dh2, x1, g2, mix, wo, tm):
    t, d = x1.shape
    dq = wo.shape[1]
    widths = [m.shape[1] for m in mix]
    nm = len(mix)

    def body(dx2_ref, dh2_ref, x1_ref, g_ref, *rest):
        m_refs = rest[:nm]
        w_ref = rest[nm]
        dx1_ref = rest[nm + 1]
        dm_refs = rest[nm + 2:nm + 2 + nm]
        dw_ref, dg_ref = rest[nm + 2 + nm:]
        i = pl.program_id(0)

        @pl.when(i == 0)
        def _():
            dw_ref[...] = jnp.zeros_like(dw_ref)
            dg_ref[...] = jnp.zeros_like(dg_ref)

        xn, r = _rms_fwd(x1_ref[...])
        dh2v = dh2_ref[...]
        dg_ref[...] += jnp.sum(dh2v * xn, axis=0, keepdims=True)
        dx1 = dx2_ref[...] + _rms_bwd(dh2v * g_ref[...], xn, r)
        dx1_ref[...] = dx1
        dx1c = dx1.astype(w_ref.dtype)
        off = 0
        for m_ref, dm_ref, wd in zip(m_refs, dm_refs, widths):
            for k in range(wd // dq):
                j = off // dq + k
                cols = slice(k * dq, (k + 1) * dq)
                dm_ref[:, cols] = lax.dot_general(dx1c, w_ref[j], (((1,), (1,)), ((), ())), preferred_element_type=F32)
                dw_ref[j] += lax.dot_general(m_ref[:, cols].astype(w_ref.dtype), dx1c, (((0,), (0,)), ((), ())),
                                             preferred_element_type=F32)
            off += wd

    tile = lambda wd: pl.BlockSpec((tm, wd), lambda i: (i, 0))
    return pl.pallas_call(
        body, name="out_proj_bwd", grid=(t // tm,),
        in_specs=[tile(d), tile(d), tile(d), pl.BlockSpec((1, d), lambda i: (0, 0))]
                 + [tile(wd) for wd in widths] + [pl.BlockSpec((4, dq, d), lambda i: (0, 0, 0))],
        out_specs=[tile(d)] + [tile(wd) for wd in widths]
                  + [pl.BlockSpec((4, dq, d), lambda i: (0, 0, 0)), pl.BlockSpec((1, d), lambda i: (0, 0))],
        out_shape=[jax.ShapeDtypeStruct((t, d), F32)] + [jax.ShapeDtypeStruct((t, wd), F32) for wd in widths]
                  + [jax.ShapeDtypeStruct((4, dq, d), F32), jax.ShapeDtypeStruct((1, d), F32)],
        compiler_params=_params("arbitrary"))(dx2, dh2, x1, g2, *mix, wo)


def _ffn_fwd(x1, h2, wg, wu, wd, tm):
    t, d = x1.shape
    fs = wg.shape[1]

    def body(x1_ref, h2_ref, wg_ref, wu_ref, wd_ref, x2_ref, gp_ref, up_ref):
        @pl.when(pl.program_id(1) == 0)
        def _():
            x2_ref[...] = x1_ref[...]

        h = h2_ref[...]
        nt = (((1,), (1,)), ((), ()))
        gp = lax.dot_general(h, wg_ref[...], nt, preferred_element_type=F32)
        up = lax.dot_general(h, wu_ref[...], nt, preferred_element_type=F32)
        gp_ref[...] = gp
        up_ref[...] = up
        ff = gp * _sigmoid(gp) * up
        x2_ref[...] += jnp.dot(ff.astype(wd_ref.dtype), wd_ref[...], preferred_element_type=F32)

    return pl.pallas_call(
        body, name="ffn_fwd", grid=(t // tm, 4),
        in_specs=[pl.BlockSpec((tm, d), lambda i, j: (i, 0)), pl.BlockSpec((tm, d), lambda i, j: (i, 0)),
                  pl.BlockSpec((None, fs, d), lambda i, j: (j, 0, 0)),
                  pl.BlockSpec((None, fs, d), lambda i, j: (j, 0, 0)),
                  pl.BlockSpec((None, fs, d), lambda i, j: (j, 0, 0))],
        out_specs=[pl.BlockSpec((tm, d), lambda i, j: (i, 0)), pl.BlockSpec((None, tm, fs), lambda i, j: (j, i, 0)),
                   pl.BlockSpec((None, tm, fs), lambda i, j: (j, i, 0))],
        out_shape=[jax.ShapeDtypeStruct((t, d), F32), jax.ShapeDtypeStruct((4, t, fs), F32),
                   jax.ShapeDtypeStruct((4, t, fs), F32)],
        compiler_params=_params("arbitrary", "arbitrary"))(x1, h2, wg, wu, wd)


def _ffn_bwd(dx2, h2, gp, up, wg, wu, wd, tm):
    t, d = dx2.shape
    fs = wg.shape[1]

    def body(dx2_ref, h2_ref, gp_ref, up_ref, wg_ref, wu_ref, wd_ref, dh2_ref, dwg_ref, dwu_ref, dwd_ref):
        j, i = pl.program_id(0), pl.program_id(1)

        @pl.when(i == 0)
        def _():
            dwg_ref[...] = jnp.zeros_like(dwg_ref)
            dwu_ref[...] = jnp.zeros_like(dwu_ref)
            dwd_ref[...] = jnp.zeros_like(dwd_ref)

        cdt = wg_ref.dtype
        h = h2_ref[...]
        gpv, upv = gp_ref[...], up_ref[...]
        s = _sigmoid(gpv)
        silu = gpv * s
        dx2c = dx2_ref[...].astype(cdt)
        dff = lax.dot_general(dx2c, wd_ref[...], (((1,), (1,)), ((), ())), preferred_element_type=F32)
        dwd_ref[...] += lax.dot_general((silu * upv).astype(cdt), dx2c, (((0,), (0,)), ((), ())), preferred_element_type=F32)
        dup = (dff * silu).astype(cdt)
        dgp = (dff * upv * _dsilu(gpv, s)).astype(cdt)
        dwg_ref[...] += lax.dot_general(dgp, h, (((0,), (0,)), ((), ())), preferred_element_type=F32)
        dwu_ref[...] += lax.dot_general(dup, h, (((0,), (0,)), ((), ())), preferred_element_type=F32)
        dh = (jnp.dot(dgp, wg_ref[...], preferred_element_type=F32) + jnp.dot(dup, wu_ref[...], preferred_element_type=F32))
        rows = pl.ds(pl.multiple_of(i * tm, tm), tm)

        @pl.when(j == 0)
        def _():
            dh2_ref[rows, :] = dh

        @pl.when(j != 0)
        def _():
            dh2_ref[rows, :] += dh

    return pl.pallas_call(
        body, name="ffn_bwd", grid=(4, t // tm),
        in_specs=[pl.BlockSpec((tm, d), lambda j, i: (i, 0)), pl.BlockSpec((tm, d), lambda j, i: (i, 0)),
                  pl.BlockSpec((None, tm, fs), lambda j, i: (j, i, 0)), pl.BlockSpec((None, tm, fs), lambda j, i: (j, i, 0)),
                  pl.BlockSpec((None, fs, d), lambda j, i: (j, 0, 0), pipeline_mode=pl.Buffered(1)),
                  pl.BlockSpec((None, fs, d), lambda j, i: (j, 0, 0), pipeline_mode=pl.Buffered(1)),
                  pl.BlockSpec((None, fs, d), lambda j, i: (j, 0, 0), pipeline_mode=pl.Buffered(1))],
        out_specs=[pl.BlockSpec((t, d), lambda j, i: (0, 0), pipeline_mode=pl.Buffered(1)),
                   pl.BlockSpec((None, fs, d), lambda j, i: (j, 0, 0)),
                   pl.BlockSpec((None, fs, d), lambda j, i: (j, 0, 0)), pl.BlockSpec((None, fs, d), lambda j, i: (j, 0, 0))],
        out_shape=[jax.ShapeDtypeStruct((t, d), F32)] + [jax.ShapeDtypeStruct((4, fs, d), F32)] * 3,
        compiler_params=_params("arbitrary", "arbitrary"))(dx2, h2, gp, up, wg, wu, wd)


def _ple_fwd(x2, p, wpg, wpp, tm):
    t, d = x2.shape
    q = p.shape[1]
    dq = d // 4

    def body(x_ref, p_ref, wg_ref, wp_ref, o_ref):
        xv = x_ref[...]
        xc = xv.astype(wg_ref.dtype)
        pc = p_ref[...].astype(wp_ref.dtype)
        pre = jnp.dot(xc[:, :dq], wg_ref[0], preferred_element_type=F32)
        for j in range(1, 4):
            pre = pre + jnp.dot(xc[:, j * dq:(j + 1) * dq], wg_ref[j], preferred_element_type=F32)
        gate = _sigmoid(pre)
        for j in range(4):
            cols = slice(j * dq, (j + 1) * dq)
            o_ref[:, cols] = xv[:, cols] + gate[:, cols] * jnp.dot(pc, wp_ref[j], preferred_element_type=F32)

    return pl.pallas_call(
        body, name="ple_fwd", grid=(t // tm,),
        in_specs=[pl.BlockSpec((tm, d), lambda i: (i, 0)), pl.BlockSpec((tm, q), lambda i: (i, 0)),
                  pl.BlockSpec((4, dq, d), lambda i: (0, 0, 0)),
                  pl.BlockSpec((4, q, dq), lambda i: (0, 0, 0))],
        out_specs=pl.BlockSpec((tm, d), lambda i: (i, 0)),
        out_shape=jax.ShapeDtypeStruct((t, d), F32),
        compiler_params=_params("arbitrary"))(x2, p, wpg, wpp)


def _ple_bwd(dx3, x2, p, wpg, wpp, tm):
    t, d = x2.shape
    q = p.shape[1]
    dq = d // 4

    def body(dx3_ref, x_ref, p_ref, wg_ref, wp_ref, dx2_ref, dwg_ref, dwp_ref):
        @pl.when(pl.program_id(0) == 0)
        def _():
            dwg_ref[...] = jnp.zeros_like(dwg_ref)
            dwp_ref[...] = jnp.zeros_like(dwp_ref)

        cdt = wg_ref.dtype
        xc = x_ref[...].astype(cdt)
        pc = p_ref[...].astype(cdt)
        pre = jnp.dot(xc[:, :dq], wg_ref[0], preferred_element_type=F32)
        for j in range(1, 4):
            pre = pre + jnp.dot(xc[:, j * dq:(j + 1) * dq], wg_ref[j], preferred_element_type=F32)
        gate = _sigmoid(pre)
        dx3v = dx3_ref[...]
        dpp = (dx3v * gate).astype(cdt)
        dgate = dx3v * gate * (1.0 - gate)
        dpre_parts = []
        for j in range(4):
            cols = slice(j * dq, (j + 1) * dq)
            pp_j = jnp.dot(pc, wp_ref[j], preferred_element_type=F32)
            dpre_parts.append((dgate[:, cols] * pp_j).astype(cdt))
            dwp_ref[j] += lax.dot_general(pc, dpp[:, cols], (((0,), (0,)), ((), ())), preferred_element_type=F32)
        dpre = jnp.concatenate(dpre_parts, axis=1)
        for j in range(4):
            cols = slice(j * dq, (j + 1) * dq)
            dwg_ref[j] += lax.dot_general(xc[:, cols], dpre, (((0,), (0,)), ((), ())), preferred_element_type=F32)
            dx2_ref[:, cols] = dx3v[:, cols] + lax.dot_general(dpre, wg_ref[j], (((1,), (1,)), ((), ())),
                                                               preferred_element_type=F32)

    return pl.pallas_call(
        body, name="ple_bwd", grid=(t // tm,),
        in_specs=[pl.BlockSpec((tm, d), lambda i: (i, 0)), pl.BlockSpec((tm, d), lambda i: (i, 0)),
                  pl.BlockSpec((tm, q), lambda i: (i, 0)), pl.BlockSpec((4, dq, d), lambda i: (0, 0, 0)),
                  pl.BlockSpec((4, q, dq), lambda i: (0, 0, 0))],
        out_specs=[pl.BlockSpec((tm, d), lambda i: (i, 0)), pl.BlockSpec((4, dq, d), lambda i: (0, 0, 0)),
                   pl.BlockSpec((4, q, dq), lambda i: (0, 0, 0))],
        out_shape=[jax.ShapeDtypeStruct((t, d), F32), jax.ShapeDtypeStruct((4, dq, d), F32),
                   jax.ShapeDtypeStruct((4, q, dq), F32)],
        compiler_params=_params("arbitrary"))(dx3, x2, p, wpg, wpp)


def _loss_head(x, target, fg, tm):
    t, d = x.shape

    def body(x_ref, t_ref, g_ref, dx_ref, loss_ref, dg_ref):
        @pl.when(pl.program_id(0) == 0)
        def _():
            loss_ref[...] = jnp.zeros_like(loss_ref)
            dg_ref[...] = jnp.zeros_like(dg_ref)

        xn, r = _rms_fwd(x_ref[...])
        g = g_ref[...]
        err = xn * g - t_ref[...]
        loss_ref[...] += 0.5 * jnp.sum(jnp.sum(err * err, axis=-1, keepdims=True) / d, axis=0, keepdims=True)
        dy = err / d
        dg_ref[...] += jnp.sum(dy * xn, axis=0, keepdims=True)
        dx_ref[...] = _rms_bwd(dy * g, xn, r)

    return pl.pallas_call(
        body, name="loss_head", grid=(t // tm,),
        in_specs=[pl.BlockSpec((tm, d), lambda i: (i, 0)), pl.BlockSpec((tm, d), lambda i: (i, 0)),
                  pl.BlockSpec((1, d), lambda i: (0, 0))],
        out_specs=[pl.BlockSpec((tm, d), lambda i: (i, 0)), pl.BlockSpec((1, 1), lambda i: (0, 0)),
                   pl.BlockSpec((1, d), lambda i: (0, 0))],
        out_shape=[jax.ShapeDtypeStruct((t, d), F32), jax.ShapeDtypeStruct((1, 1), F32),
                   jax.ShapeDtypeStruct((1, d), F32)],
        compiler_params=_params("arbitrary"))(x, target, fg)


def _qkv_conv_act(xv, w, j, heads):
    k = QKV_CONV_WIDTH
    y = w[k - 1:k] * xv
    for s in range(1, k):
        y = y + w[k - 1 - s:k - s] * _shift_down(xv, s)
    sg = _sigmoid(y)
    s_act = y * sg
    nrm = lax.rsqrt(jnp.sum(s_act * s_act, axis=-1, keepdims=True) + EPS)
    scale = jnp.where(j < heads, HEAD_DIM ** -0.5, 1.0).astype(F32)
    return y, sg, s_act, nrm, scale


def _qkv_conv_fwd(qkv_pre, conv_w, heads):
    t = qkv_pre.shape[0]
    nblk = 3 * heads

    def body(x_ref, w_ref, o_ref):
        j = pl.program_id(0)
        _, _, s_act, nrm, scale = _qkv_conv_act(x_ref[...], w_ref[...], j, heads)
        o_ref[...] = jnp.where(j < 2 * heads, s_act * (nrm * scale), s_act)

    return pl.pallas_call(
        body, name="qkv_conv_fwd", grid=(nblk,),
        in_specs=[pl.BlockSpec((t, LANES), lambda j: (0, j)), pl.BlockSpec((QKV_CONV_WIDTH, LANES), lambda j: (0, j))],
        out_specs=pl.BlockSpec((t, LANES), lambda j: (0, j)),
        out_shape=jax.ShapeDtypeStruct(qkv_pre.shape, F32),
        compiler_params=_params("arbitrary"))(qkv_pre, conv_w)


def _qkv_conv_bwd(qkv_pre, conv_w, dqkv, heads):
    t = qkv_pre.shape[0]
    nblk = 3 * heads
    k = QKV_CONV_WIDTH

    def body(x_ref, w_ref, dn_ref, dx_ref, dw_ref):
        j = pl.program_id(0)
        xv, w = x_ref[...], w_ref[...]
        y, sg, s_act, nrm, scale = _qkv_conv_act(xv, w, j, heads)
        dn = dn_ref[...]
        dsn = dn * scale
        ds_qk = nrm * dsn - s_act * (nrm * nrm * nrm) * jnp.sum(dsn * s_act, axis=-1, keepdims=True)
        ds = jnp.where(j < 2 * heads, ds_qk, dn)
        dy = ds * _dsilu(y, sg)
        dx = w[k - 1:k] * dy
        dw_ref[k - 1:k, :] = jnp.sum(dy * xv, axis=0, keepdims=True)
        for s in range(1, k):
            dx = dx + w[k - 1 - s:k - s] * _shift_up(dy, s)
            dw_ref[k - 1 - s:k - s, :] = jnp.sum(dy * _shift_down(xv, s), axis=0, keepdims=True)
        dx_ref[...] = dx

    return pl.pallas_call(
        body, name="qkv_conv_bwd", grid=(nblk,),
        in_specs=[pl.BlockSpec((t, LANES), lambda j: (0, j)), pl.BlockSpec((k, LANES), lambda j: (0, j)),
                  pl.BlockSpec((t, LANES), lambda j: (0, j))],
        out_specs=[pl.BlockSpec((t, LANES), lambda j: (0, j)), pl.BlockSpec((k, LANES), lambda j: (0, j))],
        out_shape=[jax.ShapeDtypeStruct(qkv_pre.shape, F32), jax.ShapeDtypeStruct(conv_w.shape, F32)],
        compiler_params=_params("arbitrary"))(qkv_pre, conv_w, dqkv)


def _pool_windows(shape, j, group_dim):
    lane = lax.broadcasted_iota(jnp.int32, shape, 1) + j * LANES
    grp = lane // group_dim
    win = jnp.left_shift(2, grp).astype(F32)
    cnt = jnp.minimum((_rows(shape) + 1).astype(F32), win)
    return grp, cnt


def _pool_select(grp, levels):
    out = levels[0]
    for gi in range(1, POOL_GROUPS):
        out = jnp.where(grp == gi, levels[gi], out)
    return out


def _pool_mean(hv, grp, cnt):
    acc, levels, width = hv, [], 1
    for _ in range(POOL_GROUPS):
        acc = acc + _shift_down(acc, width)
        width *= 2
        levels.append(acc)
    return _pool_select(grp, levels) / cnt - hv


def _pool_fwd(hp, wbd, scale, group_dim):
    t, dp = hp.shape

    def body(h_ref, w_ref, s_ref, o_ref):
        hv = h_ref[...]
        grp, cnt = _pool_windows(hv.shape, pl.program_id(0), group_dim)
        pooled = _pool_mean(hv, grp, cnt)
        o_ref[...] = _mm(pooled, w_ref[...]) * s_ref[...]

    return pl.pallas_call(
        body, name="pool_fwd", grid=(dp // LANES,),
        in_specs=[pl.BlockSpec((t, LANES), lambda j: (0, j)), pl.BlockSpec((LANES, LANES), lambda j: (j, j)),
                  pl.BlockSpec((1, LANES), lambda j: (0, j))],
        out_specs=pl.BlockSpec((t, LANES), lambda j: (0, j)),
        out_shape=jax.ShapeDtypeStruct(hp.shape, F32),
        compiler_params=_params("arbitrary"))(hp, wbd, scale)


def _pool_bwd(hp, wbd, scale, dob, group_dim):
    t, dp = hp.shape

    def body(h_ref, w_ref, s_ref, do_ref, dh_ref, dw_ref, ds_ref):
        hv = h_ref[...]
        grp, cnt = _pool_windows(hv.shape, pl.program_id(0), group_dim)
        pooled = _pool_mean(hv, grp, cnt)
        wv = w_ref[...]
        dov = do_ref[...]
        ds_ref[...] = jnp.sum(dov * _mm(pooled, wv), axis=0, keepdims=True)
        dys = dov * s_ref[...]
        dw_ref[0] = _mm_tn(pooled, dys)
        dpooled = _mm_nt(dys, wv)
        acc, levels, width = dpooled / cnt, [], 1
        for _ in range(POOL_GROUPS):
            acc = acc + _shift_up(acc, width)
            width *= 2
            levels.append(acc)
        dh_ref[...] = _pool_select(grp, levels) - dpooled

    nb = dp // LANES
    return pl.pallas_call(
        body, name="pool_bwd", grid=(nb,),
        in_specs=[pl.BlockSpec((t, LANES), lambda j: (0, j)), pl.BlockSpec((LANES, LANES), lambda j: (j, j)),
                  pl.BlockSpec((1, LANES), lambda j: (0, j)), pl.BlockSpec((t, LANES), lambda j: (0, j))],
        out_specs=[pl.BlockSpec((t, LANES), lambda j: (0, j)), pl.BlockSpec((1, LANES, LANES), lambda j: (j, 0, 0)),
                   pl.BlockSpec((1, LANES), lambda j: (0, j))],
        out_shape=[jax.ShapeDtypeStruct(hp.shape, F32), jax.ShapeDtypeStruct((nb, LANES, LANES), F32),
                   jax.ShapeDtypeStruct((1, dp), F32)],
        compiler_params=_params("arbitrary"))(hp, wbd, scale, dob)


def _sconv_fwd(cbcch, w):
    t, dc3 = cbcch.shape
    nb = dc3 // 3 // LANES
    k = SCONV_WIDTH

    def body(b_ref, c_ref, h_ref, w_ref, o_ref):
        m = c_ref[...] * h_ref[...]
        wv = w_ref[...]
        y = wv[k - 1:k] * m
        for s in range(1, k):
            y = y + wv[k - 1 - s:k - s] * _shift_down(m, s)
        o_ref[...] = b_ref[...] * y

    return pl.pallas_call(
        body, name="sconv_fwd", grid=(nb,),
        in_specs=[pl.BlockSpec((t, LANES), lambda j: (0, j)), pl.BlockSpec((t, LANES), lambda j: (0, nb + j)),
                  pl.BlockSpec((t, LANES), lambda j: (0, 2 * nb + j)), pl.BlockSpec((k, LANES), lambda j: (0, j))],
        out_specs=pl.BlockSpec((t, LANES), lambda j: (0, j)),
        out_shape=jax.ShapeDtypeStruct((t, dc3 // 3), F32),
        compiler_params=_params("arbitrary"))(cbcch, cbcch, cbcch, w)


def _sconv_bwd(cbcch, w, doc):
    t, dc3 = cbcch.shape
    nb = dc3 // 3 // LANES
    k = SCONV_WIDTH

    def body(b_ref, c_ref, h_ref, w_ref, do_ref, db_ref, dc_ref, dh_ref, dw_ref):
        cv, hv = c_ref[...], h_ref[...]
        m = cv * hv
        wv = w_ref[...]
        dov = do_ref[...]
        dy = dov * b_ref[...]
        y = wv[k - 1:k] * m
        dm = wv[k - 1:k] * dy
        dw_ref[k - 1:k, :] = jnp.sum(dy * m, axis=0, keepdims=True)
        for s in range(1, k):
            ms = _shift_down(m, s)
            y = y + wv[k - 1 - s:k - s] * ms
            dm = dm + wv[k - 1 - s:k - s] * _shift_up(dy, s)
            dw_ref[k - 1 - s:k - s, :] = jnp.sum(dy * ms, axis=0, keepdims=True)
        db_ref[...] = dov * y
        dc_ref[...] = dm * hv
        dh_ref[...] = dm * cv

    col = lambda o: pl.BlockSpec((t, LANES), lambda j: (0, o * nb + j))
    return pl.pallas_call(
        body, name="sconv_bwd", grid=(nb,),
        in_specs=[col(0), col(1), col(2), pl.BlockSpec((k, LANES), lambda j: (0, j)), col(0)],
        out_specs=[col(0), col(0), col(0), pl.BlockSpec((k, LANES), lambda j: (0, j))],
        out_shape=[jax.ShapeDtypeStruct((t, dc3 // 3), F32)] * 3 + [jax.ShapeDtypeStruct(w.shape, F32)],
        compiler_params=_params("arbitrary"))(cbcch, cbcch, cbcch, w, doc)


class _Split:
    def __init__(self, a):
        self.hi = a.astype(jnp.bfloat16)
        self.lo = (a - self.hi.astype(F32)).astype(jnp.bfloat16)


def _per_head(dims, a, b):
    a = a if isinstance(a, _Split) else _Split(a)
    b = b if isinstance(b, _Split) else _Split(b)

    def dot(x, y):
        return lax.dot_general(x, y, (dims, ((), ())), preferred_element_type=F32)

    return jnp.stack([dot(a.hi[h], b.hi[h]) + (dot(a.hi[h], b.lo[h]) + dot(a.lo[h], b.hi[h])) for h in range(a.hi.shape[0])])


def _bmm(a, b):
    return _per_head(((1,), (0,)), a, b)


def _bmm_nt(a, b):
    return _per_head(((1,), (1,)), a, b)


def _bmm_tn(a, b):
    return _per_head(((0,), (0,)), a, b)


def _inv_unit_lower(low):
    c = low.shape[-1]
    eye = (_rows((c, c)) == lax.broadcasted_iota(jnp.int32, (c, c), 1)).astype(F32)
    pw = -low
    inv = eye + pw
    span = 2
    while span < c:
        pws = _Split(pw)
        pw = _bmm(pws, pws)
        inv = inv + _bmm(inv, pw)
        span *= 2
    return inv


def _heads_of(ref, base, heads):
    return jnp.stack([ref[:, base + h * HEAD_DIM:base + (h + 1) * HEAD_DIM] for h in range(heads)])


def _chunk_common(q, k, v, a_col, b_col, alog, dtb, kept=None):
    hn, c, _ = q.shape
    beta = _sigmoid(b_col)
    xg = a_col + dtb
    softplus = jnp.maximum(xg, 0.0) + jnp.log(1.0 + jnp.exp(-jnp.abs(xg)))
    neg_ea = -jnp.exp(alog)
    g = neg_ea * softplus
    ri = _rows((c, c))
    ci = lax.broadcasted_iota(jnp.int32, (c, c), 1)
    incl, strict = ri >= ci, ri > ci
    inclf = jnp.broadcast_to(incl.astype(F32), (hn, c, c))
    gcb = _bmm(inclf, jnp.broadcast_to(g, (hn, c, HEAD_DIM)))
    gc_row = jnp.sum(jnp.where(ri <= ci, jnp.broadcast_to(g, (hn, c, c)), 0.0), axis=1, keepdims=True)
    dmat = jnp.where(incl, jnp.exp(jnp.where(incl, gcb[:, :, :1] - gc_row, 0.0)), 0.0)
    eg = jnp.exp(gcb)
    gl = gcb[:, c - 1:c, :]
    egl = jnp.exp(gl)
    edl = jnp.exp(gl - gcb)
    kb, vb = k * beta, v * beta
    kbe = kb * eg
    if kept is None:
        ks = _Split(k)
        a0 = _bmm_nt(kb, ks)
        tm = _inv_unit_lower(jnp.where(strict, a0 * dmat, 0.0))
        p0 = _bmm_nt(q, ks)
        tms = _Split(tm)
        u, w = _bmm(tms, vb), _bmm(tms, kbe)
    else:
        (a0, tm, p0, w), u = kept, None
    return dict(beta=beta, xg=xg, neg_ea=neg_ea, g=g, incl=incl, strict=strict, inclf=inclf, dmat=dmat, eg=eg,
                egl=egl, edl=edl, kb=kb, vb=vb, a0=a0, tm=tm, kbe=kbe, u=u, w=w, p0=p0,
                attn=p0 * dmat, qe=q * eg, kd=k * edl)


def _chunk_step(cm, state):
    ss = _Split(state)
    vn = cm["u"] - _bmm(cm["w"], ss)
    vns = _Split(vn)
    o = _bmm(cm["qe"], ss) + _bmm(cm["attn"], vns)
    new_state = state * cm["egl"][:, :, :1] + _bmm_tn(cm["kd"], vns)
    return vn, o, new_state


def _gated_norm(o, zv, og):
    xo, ro = _rms_fwd(o)
    sgz = _sigmoid(zv)
    return xo, ro, sgz, xo * og * (zv * sgz)


def _gate_columns(abv, gpv, heads):
    a_col = jnp.stack([abv[:, h:h + 1] for h in range(heads)])
    b_col = jnp.stack([abv[:, heads + h:heads + h + 1] for h in range(heads)])
    alog = jnp.stack([gpv[0:1, h:h + 1] for h in range(heads)])
    dtb = jnp.stack([gpv[1:2, h:h + 1] for h in range(heads)])
    return a_col, b_col, alog, dtb


def _delta_fwd(qkv, z, ab, gpar, heads):
    t = qkv.shape[0]
    da = heads * HEAD_DIM
    n = t // CHUNK

    def body(qkv_ref, z_ref, ab_ref, gp_ref, oa_ref, st_ref, kc_ref, kw_ref, s_ref):
        @pl.when(pl.program_id(0) == 0)
        def _():
            s_ref[...] = jnp.zeros_like(s_ref)

        gpv = gp_ref[...]
        cm = _chunk_common(_heads_of(qkv_ref, 0, heads), _heads_of(qkv_ref, da, heads), _heads_of(qkv_ref, 2 * da, heads),
                           *_gate_columns(ab_ref[...], gpv, heads))
        state = s_ref[...]
        st_ref[0] = state
        vn, o, new_state = _chunk_step(cm, state)
        s_ref[...] = new_state
        for slot, val in enumerate((cm["a0"], cm["tm"], cm["p0"])):
            kc_ref[0, slot] = val
        for slot, val in enumerate((cm["w"], vn, o)):
            kw_ref[0, slot] = val
        oa = _gated_norm(o, _heads_of(z_ref, 0, heads), gpv[2:3, :])[3]
        for h in range(heads):
            oa_ref[:, h * HEAD_DIM:(h + 1) * HEAD_DIM] = oa[h]

    return pl.pallas_call(
        body, name="delta_fwd", grid=(n,),
        in_specs=[pl.BlockSpec((CHUNK, 3 * da), lambda i: (i, 0)), pl.BlockSpec((CHUNK, da), lambda i: (i, 0)),
                  pl.BlockSpec((CHUNK, LANES), lambda i: (i, 0)), pl.BlockSpec((8, LANES), lambda i: (0, 0))],
        out_specs=[pl.BlockSpec((CHUNK, da), lambda i: (i, 0)),
                   pl.BlockSpec((1, heads, HEAD_DIM, HEAD_DIM), lambda i: (i, 0, 0, 0)),
                   pl.BlockSpec((1, 3, heads, CHUNK, CHUNK), lambda i: (i, 0, 0, 0, 0)),
                   pl.BlockSpec((1, 3, heads, CHUNK, HEAD_DIM), lambda i: (i, 0, 0, 0, 0))],
        out_shape=[jax.ShapeDtypeStruct((t, da), F32), jax.ShapeDtypeStruct((n, heads, HEAD_DIM, HEAD_DIM), F32),
                   jax.ShapeDtypeStruct((n, 3, heads, CHUNK, CHUNK), F32),
                   jax.ShapeDtypeStruct((n, 3, heads, CHUNK, HEAD_DIM), F32)],
        scratch_shapes=[pltpu.VMEM((heads, HEAD_DIM, HEAD_DIM), F32)],
        compiler_params=_params("arbitrary"))(qkv, z, ab, gpar)


def _delta_bwd(qkv, z, ab, gpar, states, kept_c, kept_w, doa, heads):
    t = qkv.shape[0]
    da = heads * HEAD_DIM
    n = t // CHUNK
    c = CHUNK

    def body(qkv_ref, z_ref, ab_ref, gp_ref, st_ref, kc_ref, kw_ref, doa_ref, dqkv_ref, dz_ref, dab_ref, dpar_ref, ds_ref):
        @pl.when(pl.program_id(0) == 0)
        def _():
            ds_ref[...] = jnp.zeros_like(ds_ref)
            dpar_ref[...] = jnp.zeros_like(dpar_ref)

        gpv = gp_ref[...]
        og = gpv[2:3, :]
        q, k, v = _heads_of(qkv_ref, 0, heads), _heads_of(qkv_ref, da, heads), _heads_of(qkv_ref, 2 * da, heads)
        cm = _chunk_common(q, k, v, *_gate_columns(ab_ref[...], gpv, heads),
                           kept=(kc_ref[0, 0], kc_ref[0, 1], kc_ref[0, 2], kw_ref[0, 0]))
        state = st_ref[0]
        dsp = ds_ref[...]
        vn, o = kw_ref[0, 1], kw_ref[0, 2]
        zv = _heads_of(z_ref, 0, heads)
        xo, ro, sgz, _ = _gated_norm(o, zv, og)
        doav = _heads_of(doa_ref, 0, heads)
        don = doav * (zv * sgz)
        dz = doav * (xo * og) * _dsilu(zv, sgz)
        d_og = jnp.sum(jnp.sum(don * xo, axis=1, keepdims=True), axis=0)
        do = _rms_bwd(don * og, xo, ro)
        tm, dmat, eg, edl, egl = cm["tm"], cm["dmat"], cm["eg"], cm["edl"], cm["egl"]
        dos, dsps, sts, tms, ks = _Split(do), _Split(dsp), _Split(state), _Split(tm), _Split(k)
        dvn = _bmm_tn(cm["attn"], dos) + _bmm(cm["kd"], dsps)
        dvns = _Split(dvn)
        dqe = _bmm_nt(dos, sts)
        ds_ref[...] = _bmm_tn(cm["qe"], dos) + dsp * egl[:, :, :1] - _bmm_tn(cm["w"], dvns)
        dattn = _bmm_nt(dos, vn)
        dkd = _bmm_nt(vn, dsps)
        dkd_kd = jnp.sum(dkd * cm["kd"], axis=-1, keepdims=True)
        dgl = (jnp.sum(jnp.sum(dsp * state, axis=-1, keepdims=True), axis=1, keepdims=True) * egl[:, :, :1]
               + jnp.sum(dkd_kd, axis=1, keepdims=True))
        dgc = jnp.sum(dqe * cm["qe"], axis=-1, keepdims=True) - dkd_kd
        dk = dkd * edl
        dq = dqe * eg
        dw = -_bmm_nt(dvns, sts)
        dws = _Split(dw)
        dp0 = dattn * dmat
        dd = jnp.where(cm["incl"], dattn * cm["p0"], 0.0)
        dp0s = _Split(dp0)
        dq = dq + _bmm(dp0s, ks)
        dk = dk + _bmm_tn(dp0s, q)
        dtm = _bmm_nt(dvns, cm["vb"]) + _bmm_nt(dws, cm["kbe"])
        dvb = _bmm_tn(tms, dvns)
        dkbe = _bmm_tn(tms, dws)
        dkb = dkbe * eg
        dgc = dgc + jnp.sum(dkbe * cm["kbe"], axis=-1, keepdims=True)
        dlow = jnp.where(cm["strict"], -_bmm_tn(tms, _bmm_nt(dtm, tms)), 0.0)
        dd = dd + dlow * cm["a0"]
        da0 = dlow * dmat
        da0s = _Split(da0)
        dkb = dkb + _bmm(da0s, ks)
        dk = dk + _bmm_tn(da0s, cm["kb"])
        ddd = dd * dmat
        ones = jnp.ones((heads, c, HEAD_DIM), F32)
        dgc = dgc + jnp.sum(ddd, axis=-1, keepdims=True) - _bmm_tn(ddd, ones)[:, :, :1]
        dgc = dgc + jnp.where(_rows((c, 1)) == c - 1, dgl, 0.0)
        dg = _bmm_tn(cm["inclf"], jnp.broadcast_to(dgc, (heads, c, HEAD_DIM)))[:, :, :1]
        beta = cm["beta"]
        dk = dk + dkb * beta
        dbeta = jnp.sum(dkb * k, axis=-1, keepdims=True) + jnp.sum(dvb * v, axis=-1, keepdims=True)
        dv = dvb * beta
        db_col = dbeta * beta * (1.0 - beta)
        da_col = dg * cm["neg_ea"] * _sigmoid(cm["xg"])
        d_alog = jnp.sum(dg * cm["g"], axis=1, keepdims=True)
        d_dtb = jnp.sum(da_col, axis=1, keepdims=True)
        lane = lax.broadcasted_iota(jnp.int32, (c, LANES), 1)
        lane8 = lax.broadcasted_iota(jnp.int32, (8, LANES), 1)
        row8 = _rows((8, LANES))
        dab = jnp.zeros((c, LANES), F32)
        dpar = jnp.where(row8 == 2, d_og, 0.0)
        for h in range(heads):
            lo = h * HEAD_DIM
            dqkv_ref[:, lo:lo + HEAD_DIM] = dq[h]
            dqkv_ref[:, da + lo:da + lo + HEAD_DIM] = dk[h]
            dqkv_ref[:, 2 * da + lo:2 * da + lo + HEAD_DIM] = dv[h]
            dz_ref[:, lo:lo + HEAD_DIM] = dz[h]
            dab = dab + jnp.where(lane == h, da_col[h], 0.0) + jnp.where(lane == heads + h, db_col[h], 0.0)
            dpar = (dpar + jnp.where((row8 == 0) & (lane8 == h), d_alog[h], 0.0)
                    + jnp.where((row8 == 1) & (lane8 == h), d_dtb[h], 0.0))
        dab_ref[...] = dab
        dpar_ref[...] += dpar

    rev = lambda i: (n - 1 - i, 0)
    return pl.pallas_call(
        body, name="delta_bwd", grid=(n,),
        in_specs=[pl.BlockSpec((c, 3 * da), rev), pl.BlockSpec((c, da), rev), pl.BlockSpec((c, LANES), rev),
                  pl.BlockSpec((8, LANES), lambda i: (0, 0)),
                  pl.BlockSpec((1, heads, HEAD_DIM, HEAD_DIM), lambda i: (n - 1 - i, 0, 0, 0)),
                  pl.BlockSpec((1, 3, heads, c, c), lambda i: (n - 1 - i, 0, 0, 0, 0)),
                  pl.BlockSpec((1, 3, heads, c, HEAD_DIM), lambda i: (n - 1 - i, 0, 0, 0, 0)),
                  pl.BlockSpec((c, da), rev)],
        out_specs=[pl.BlockSpec((c, 3 * da), rev), pl.BlockSpec((c, da), rev), pl.BlockSpec((c, LANES), rev),
                   pl.BlockSpec((8, LANES), lambda i: (0, 0))],
        out_shape=[jax.ShapeDtypeStruct((t, 3 * da), F32), jax.ShapeDtypeStruct((t, da), F32),
                   jax.ShapeDtypeStruct((t, LANES), F32), jax.ShapeDtypeStruct((8, LANES), F32)],
        scratch_shapes=[pltpu.VMEM((heads, HEAD_DIM, HEAD_DIM), F32)],
        compiler_params=_params("arbitrary"))(qkv, z, ab, gpar, states, kept_c, kept_w, doa)


def _w_in_pieces(shard_cols, da, heads):
    a0, nab = 4 * da, 2 * heads
    d_in = 4 * shard_cols
    runs = [(0, a0, 0), (a0, a0 + nab, d_in - nab), (a0 + nab, d_in, a0)]
    pieces = []
    for j in range(4):
        lo, hi = j * shard_cols, (j + 1) * shard_cols
        for rlo, rhi, plo in runs:
            s, e = max(lo, rlo), min(hi, rhi)
            if s < e:
                pieces.append((j, s - lo, e - s, plo + (s - rlo)))
    return pieces, d_in - nab + LANES


def _w_in_pack(w4, da, heads):
    _, d, sc = w4.shape
    pieces, npk = _w_in_pieces(sc, da, heads)
    tr = _tile_rows(d, 256, SUBLANES_WIRE)

    def body(w_ref, o_ref):
        o_ref[:, npk - LANES:] = jnp.zeros((tr, LANES), o_ref.dtype)
        for j, lo, ln, dst in pieces:
            o_ref[:, dst:dst + ln] = w_ref[j, :, lo:lo + ln]

    return pl.pallas_call(
        body, name="w_in_pack", grid=(d // tr,),
        in_specs=[pl.BlockSpec((4, tr, sc), lambda i: (0, i, 0))],
        out_specs=pl.BlockSpec((tr, npk), lambda i: (i, 0)),
        out_shape=jax.ShapeDtypeStruct((d, npk), w4.dtype),
        compiler_params=_params("arbitrary"))(w4)


def _w_in_unpack(dwp, sc, da, heads):
    d, npk = dwp.shape
    pieces, _ = _w_in_pieces(sc, da, heads)
    tr = _tile_rows(d, 256)

    def body(g_ref, o_ref):
        for j, lo, ln, dst in pieces:
            o_ref[j, :, lo:lo + ln] = g_ref[:, dst:dst + ln]

    return pl.pallas_call(
        body, name="w_in_unpack", grid=(d // tr,),
        in_specs=[pl.BlockSpec((tr, npk), lambda i: (i, 0))],
        out_specs=pl.BlockSpec((4, tr, sc), lambda i: (0, i, 0)),
        out_shape=jax.ShapeDtypeStruct((4, d, sc), F32),
        compiler_params=_params("arbitrary"))(dwp)


def _block_diag(pool_w):
    g, gd, _ = pool_w.shape
    out = jnp.zeros((g * gd, g * gd), pool_w.dtype)
    for gi in range(g):
        out = lax.dynamic_update_slice(out, pool_w[gi], (gi * gd, gi * gd))
    return out


def _layer_dims(d):
    heads = (d // 2) // HEAD_DIM
    return heads, heads * HEAD_DIM, d // 4, d // 4


BIG = ("w_in", "w_gate", "w_up", "ple_proj", "w_out", "w_down", "ple_gate")
TRANSPOSED = ("w_gate", "w_up")


def _prepare_layer(small, li):
    d = small["norm1_g"].shape[1]
    heads, _, _, _ = _layer_dims(d)
    gpar = jnp.zeros((8, LANES), F32)
    gpar = gpar.at[0, :heads].set(small["a_log"][li]).at[1, :heads].set(small["dt_bias"][li]).at[2, :].set(small["onorm_g"][li])
    return dict(norm1_g=small["norm1_g"][li][None], conv_qkv=small["conv_qkv"][li], gpar=gpar, pool_bd=_block_diag(small["pool_w"][li]).astype(MM_DTYPE),
                pool_scale=small["pool_scale"][li][None], sconv_w=small["sconv_w"][li], norm2_g=small["norm2_g"][li][None])


def _layer_fwd(x0, p, gw, lw, tm, arrive):
    d = x0.shape[1]
    heads, da, dp, dc = _layer_dims(d)
    segs = (3 * da, da, dp, 3 * dc, LANES)
    lw["w_in_p"] = _w_in_pack(gw["w_in"], da, heads).astype(MM_DTYPE)
    qkv_pre, z, hp, cbcch, ab = _in_proj_fwd(x0, lw["norm1_g"], lw["w_in_p"], segs, tm)
    qkv = _qkv_conv_fwd(qkv_pre, lw["conv_qkv"], heads)
    oa, states, kept_c, kept_w = _delta_fwd(qkv, z, ab, lw["gpar"], heads)
    ob = _pool_fwd(hp, lw["pool_bd"], lw["pool_scale"], dp // POOL_GROUPS)
    oc = _sconv_fwd(cbcch, lw["sconv_w"])
    arrive("mixed", oa)
    x1, h2 = _out_proj_fwd(x0, (oa, ob, oc), gw["w_out"], lw["norm2_g"], tm)
    x2, gp, up = _ffn_fwd(x1, h2, gw["w_gate"], gw["w_up"], gw["w_down"], tm)
    arrive("ffn", x2)
    x3 = _ple_fwd(x2, p, gw["ple_gate"], gw["ple_proj"], tm)
    arrive("end", x3)
    saved = dict(x0=x0, qkv_pre=qkv_pre, z=z, hp=hp, cbcch=cbcch, ab=ab, qkv=qkv, states=states, kept_c=kept_c, kept_w=kept_w, oa=oa, ob=ob, oc=oc,
                 x1=x1, h2=h2, gp=gp, up=up, x2=x2)
    return x3, saved


def _layer_bwd(dx3, p, gw, lw, sv, tm, produced):
    def after_token(tok, arr):
        return arr if tok is None else arr + tok[0, 0]

    d = dx3.shape[1]
    heads, da, dp, dc = _layer_dims(d)
    segs = (3 * da, da, dp, dc, dc, dc, LANES)
    gd = dp // POOL_GROUPS
    dx2, d_ple_gate, d_ple_proj = _ple_bwd(dx3, sv["x2"], p, gw["ple_gate"], gw["ple_proj"], tm)
    dh2, d_w_gate, d_w_up, d_w_down = _ffn_bwd(dx2, sv["h2"], sv["gp"], sv["up"], gw["w_gate"], gw["w_up"], gw["w_down"],
                                               tm)
    tok = produced("ffn", dict(w_gate=d_w_gate, w_up=d_w_up, ple_proj=d_ple_proj, w_down=d_w_down, ple_gate=d_ple_gate), dh2)
    dx1, doa, dob, doc, d_w_out, d_norm2 = _out_proj_bwd(dx2, dh2, sv["x1"], after_token(tok, lw["norm2_g"]),
                                                         (sv["oa"], sv["ob"], sv["oc"]), gw["w_out"], tm)
    dcb, dcc, dch, d_sconv = _sconv_bwd(sv["cbcch"], lw["sconv_w"], doc)
    dhp, d_pool_bd, d_pool_scale = _pool_bwd(sv["hp"], lw["pool_bd"], lw["pool_scale"], dob, gd)
    dqkv, dz, dab, dpar = _delta_bwd(sv["qkv"], sv["z"], sv["ab"], lw["gpar"], sv["states"], sv["kept_c"], sv["kept_w"], doa,
                                      heads)
    tok = produced("mixers", {}, dqkv)
    dqkv_pre, d_conv_qkv = _qkv_conv_bwd(sv["qkv_pre"], lw["conv_qkv"], dqkv, heads)
    dsegs = (dqkv_pre, dz, dhp, dcb, dcc, dch, dab)
    dx0, d_w_in_p, d_norm1 = _in_proj_bwd(sv["x0"], after_token(tok, lw["norm1_g"]), lw["w_in_p"], dsegs, dx1, segs, tm)
    per = LANES // gd
    bd = d_pool_bd.reshape(dp // LANES, per, gd, per, gd)
    d_pool_w = jnp.stack([bd[gi // per, gi % per, :, gi % per, :] for gi in range(POOL_GROUPS)])
    big = dict(w_in=_w_in_unpack(d_w_in_p, gw["w_in"].shape[2], da, heads), w_gate=d_w_gate, w_up=d_w_up,
               ple_proj=d_ple_proj, w_out=d_w_out, w_down=d_w_down, ple_gate=d_ple_gate)
    small = dict(norm1_g=d_norm1[0], conv_qkv=d_conv_qkv, a_log=dpar[0, :heads], dt_bias=dpar[1, :heads], onorm_g=dpar[2],
                 pool_w=d_pool_w, pool_scale=d_pool_scale[0], sconv_w=d_sconv, norm2_g=d_norm2[0])
    tok = produced("end", dict(w_in=big["w_in"], w_out=d_w_out), big["w_in"])
    return dx0, big, small, tok


def _local_step(x, p, target, gw, small, produced=None, arrive=None):
    t, d = x.shape
    depth = p.shape[0]
    tm = 512 if t % 512 == 0 else 128
    layers = [_prepare_layer(small, li) for li in range(depth)]
    saved = []
    h = x
    for li in range(depth):
        h, sv = _layer_fwd(h, p[li], gw[li], layers[li], tm,
                           (lambda stage, after, li=li: arrive(li, stage, after)) if arrive else (lambda stage, after: None))
        saved.append(sv)
    dx, loss, d_final = _loss_head(h, target, small["final_g"][None], tm)
    big, sm = [None] * depth, [None] * depth
    token = None
    for li in reversed(range(depth)):
        p_li = p[li] if token is None else p[li] + token[0, 0]
        dx, big[li], sm[li], token = _layer_bwd(
            dx, p_li, gw[li], layers[li], saved[li], tm,
            (lambda stage, grads, after, li=li: produced(li, stage, grads, after)) if produced else (lambda *a: None))
    small_grads = {n: jnp.stack([g[n] for g in sm]) for n in sm[0]}
    small_grads["final_g"] = d_final[0]
    return loss[0, 0], dx, big, small_grads


def _coords():
    return lax.axis_index("x"), lax.axis_index("y"), lax.axis_index("c")


def _other_chips(x, y):
    return [(1 - x, y), (x, 1 - y), (1 - x, 1 - y)]


def _place_shards(ws, me_idx):
    nt = len(ws)
    depth = ws[0].shape[0]

    def body(me_ref, *refs):
        for t, w_ref in enumerate(refs[:nt]):
            for li in range(depth):
                refs[nt + li * nt + t][...] = w_ref[li].astype(WIRE_DTYPE)

    outs = pl.pallas_call(
        body, name="place_shards",
        grid_spec=pltpu.PrefetchScalarGridSpec(
            num_scalar_prefetch=1, grid=(4,),
            in_specs=[pl.BlockSpec((depth, w.shape[1] // 4, w.shape[2]), lambda i, me_ref: (0, i, 0)) for w in ws],
            out_specs=[pl.BlockSpec((None, w.shape[1] // 4, w.shape[2]), lambda i, me_ref: (me_ref[0], i, 0))
                       for _ in range(depth) for w in ws]),
        out_shape=[jax.ShapeDtypeStruct((4,) + w.shape[1:], WIRE_DTYPE) for _ in range(depth) for w in ws],
        compiler_params=_params("arbitrary"))(me_idx, *ws)
    return [list(outs[li * nt:(li + 1) * nt]) for li in range(depth)]


def _half_block(ref, chip, pc):
    rh = ref.shape[1] // 2
    return ref.at[chip, pl.ds(pc * rh, rh)]


def _gather_copies(out_refs, send_sems, recv_sems, stage):
    nt = len(out_refs)
    x, y, c = _coords()
    pairs = []
    for j, (cx, cy) in enumerate(_other_chips(x, y)):
        for t in range(nt):
            sems = dict(send_sem=send_sems[j * nt + t], recv_sem=recv_sems[j * nt + t], device_id_type=MESH)
            if stage == 0:
                mine, theirs, to = _half_block(out_refs[t], 2 * x + y, c), _half_block(out_refs[t], 2 * cx + cy, c), (cx, cy, c)
            else:
                mine, theirs, to = (_half_block(out_refs[t], 2 * cx + cy, c), _half_block(out_refs[t], 2 * cx + cy, 1 - c),
                                    (x, y, 1 - c))
            pairs.append((pltpu.make_async_remote_copy(src_ref=mine, dst_ref=mine, device_id=to, **sems),
                          pltpu.make_async_remote_copy(src_ref=theirs, dst_ref=theirs, device_id=to, **sems)))
    return pairs


def _all_gather_chips(placed):
    nt = len(placed)

    def body(*refs):
        out_refs = refs[nt:2 * nt]
        send_sems, recv_sems = refs[2 * nt:]
        nc = 3 * nt
        first = _gather_copies(out_refs, [send_sems.at[k] for k in range(nc)], [recv_sems.at[k] for k in range(nc)], 0)
        passed = _gather_copies(out_refs, [send_sems.at[nc + k] for k in range(nc)], [recv_sems.at[nc + k] for k in range(nc)], 1)
        for start, _ in first:
            start.start()
        for (_, arrival), (forward, _) in zip(first, passed):
            arrival.wait_recv()
            forward.start()
        for _, arrival in passed:
            arrival.wait_recv()
        for start, _ in first + passed:
            start.wait_send()

    return pl.pallas_call(
        body, name="all_gather_chips", out_shape=[jax.ShapeDtypeStruct(a.shape, a.dtype) for a in placed],
        in_specs=[ANY] * nt, out_specs=[ANY] * nt, input_output_aliases={t: t for t in range(nt)},
        scratch_shapes=[pltpu.SemaphoreType.DMA((6 * nt,)), pltpu.SemaphoreType.DMA((6 * nt,))],
    )(*placed)


def _gather_call(name, arrs, wait_sems, after, stage):
    nt = len(arrs)
    nc = 3 * nt
    n_wait = len(wait_sems)
    n_new = 2 * nc if stage < 2 else 0
    arrs = [pltpu.with_memory_space_constraint(a, pltpu.HBM) for a in arrs]

    def body(*refs):
        a_refs = refs[:nt]
        waits = refs[nt:nt + n_wait]
        news = refs[nt + n_wait + 1:nt + n_wait + 1 + n_new]
        token = refs[-1]
        if stage > 0:
            for start, arrival in _gather_copies(a_refs, waits[:nc], waits[nc:], stage - 1):
                start.wait_send()
                arrival.wait_recv()
        if stage < 2:
            for start, _ in _gather_copies(a_refs, news[:nc], news[nc:], stage):
                start.start()
        token[...] = jnp.zeros_like(token)

    outs = pl.pallas_call(
        body, name=name,
        out_shape=(*[pltpu.SemaphoreType.DMA(())] * n_new, *[pltpu.HBM(a.shape, a.dtype) for a in arrs],
                   jax.ShapeDtypeStruct((8, LANES), F32)),
        in_specs=[HBM] * nt + [SEM] * n_wait + [ANY],
        out_specs=(*[SEM] * n_new, *[HBM] * nt, pl.BlockSpec(memory_space=pltpu.VMEM)),
        input_output_aliases={t: n_new + t for t in range(nt)},
        compiler_params=pltpu.CompilerParams(has_side_effects=pltpu.SideEffectType.DATAFLOW_SIDE_EFFECTING),
    )(*arrs, *wait_sems, after)
    return list(outs[:n_new]), list(outs[n_new:n_new + nt]), outs[-1]


def _sibling_swap_half(gs):
    nt = len(gs)

    def body(*refs):
        g_refs, out_refs = refs[:nt], refs[nt:2 * nt]
        send_sems, recv_sems = refs[2 * nt:]
        x, y, c = _coords()
        cps = []
        for t in range(nt):
            rh = g_refs[t].shape[1] // 2
            cps.append(pltpu.make_async_remote_copy(src_ref=g_refs[t].at[:, pl.ds((1 - c) * rh, rh)], dst_ref=out_refs[t],
                                                    send_sem=send_sems.at[t], recv_sem=recv_sems.at[t], device_id=(x, y, 1 - c),
                                                    device_id_type=MESH))
        for cp in cps:
            cp.start()
        for cp in cps:
            cp.wait()

    return pl.pallas_call(
        body, name="sibling_swap_half",
        out_shape=[jax.ShapeDtypeStruct((g.shape[0], g.shape[1] // 2, g.shape[2]), g.dtype) for g in gs],
        in_specs=[ANY] * nt, out_specs=[ANY] * nt,
        scratch_shapes=[pltpu.SemaphoreType.DMA((nt,)), pltpu.SemaphoreType.DMA((nt,))])(*gs)


def _add_my_halves(gs, others, c_idx):
    nt = len(gs)

    def body(c_ref, *refs):
        for g_ref, o_ref, out_ref in zip(refs[:nt], refs[nt:2 * nt], refs[2 * nt:]):
            out_ref[...] = (g_ref[...].astype(F32) + o_ref[...].astype(F32)).astype(out_ref.dtype)

    def quarter(g):
        return pl.BlockSpec((None, g.shape[1] // 4, g.shape[2]), lambda j, i, c_ref: (j, i, 0))

    return pl.pallas_call(
        body, name="add_my_halves",
        grid_spec=pltpu.PrefetchScalarGridSpec(
            num_scalar_prefetch=1, grid=(4, 2),
            in_specs=[pl.BlockSpec((None, g.shape[1] // 4, g.shape[2]), lambda j, i, c_ref: (j, 2 * c_ref[0] + i, 0)) for g in gs]
                     + [quarter(g) for g in gs],
            out_specs=[quarter(g) for g in gs]),
        out_shape=[jax.ShapeDtypeStruct((4, g.shape[1] // 2, g.shape[2]), WIRE_DTYPE) for g in gs],
        compiler_params=_params("arbitrary", "arbitrary"))(c_idx, *gs, *others)


def _exchange_chips(parts):
    nt = len(parts)

    def body(*refs):
        p_refs, out_refs = refs[:nt], refs[nt:2 * nt]
        send_sems, recv_sems = refs[2 * nt:]
        x, y, c = _coords()
        chips = _other_chips(x, y)

        def copy(j, t):
            cx, cy = chips[j]
            return pltpu.make_async_remote_copy(src_ref=p_refs[t].at[2 * cx + cy], dst_ref=out_refs[t].at[j],
                                                send_sem=send_sems.at[j, t], recv_sem=recv_sems.at[j, t], device_id=(cx, cy, c),
                                                device_id_type=MESH)

        sends = [copy(j, t) for j in range(3) for t in range(nt)]
        for cp in sends:
            cp.start()
        for cp in sends:
            cp.wait_recv()
        for cp in sends:
            cp.wait_send()

    return pl.pallas_call(
        body, name="exchange_chips", out_shape=[jax.ShapeDtypeStruct((3,) + p.shape[1:], p.dtype) for p in parts],
        in_specs=[ANY] * nt, out_specs=[ANY] * nt,
        scratch_shapes=[pltpu.SemaphoreType.DMA((3, nt)), pltpu.SemaphoreType.DMA((3, nt))])(*parts)


def _split_plan(kind, s_refs, l_refs):
    x, y, c = _coords()
    if kind == "swap":
        return [(s.at[:, pl.ds((1 - c) * (s.shape[1] // 2), s.shape[1] // 2)], l, (x, y, 1 - c)) for s, l in zip(s_refs, l_refs)]
    return [(s.at[2 * cx + cy], l.at[j], (cx, cy, c)) for j, (cx, cy) in enumerate(_other_chips(x, y))
            for s, l in zip(s_refs, l_refs)]


def _split_landing(kind, a):
    return (a.shape[0], a.shape[1] // 2, a.shape[2]) if kind == "swap" else (3,) + a.shape[1:]


def _copies_start(name, kind, srcs, after=None):
    ns = len(srcs)
    n = ns if kind == "swap" else 3 * ns
    srcs = [pltpu.with_memory_space_constraint(a, pltpu.HBM) for a in srcs]
    lands = [pltpu.with_memory_space_constraint(lax.empty(_split_landing(kind, a), a.dtype), pltpu.HBM) for a in srcs]
    extra = [] if after is None else [after]

    def body(*refs):
        first_sem = 2 * ns + len(extra)
        sems, token = refs[first_sem:first_sem + 2 * n], refs[-1]
        for k, (src, dst, dev) in enumerate(_split_plan(kind, refs[:ns], refs[ns:2 * ns])):
            pltpu.make_async_remote_copy(src_ref=src, dst_ref=dst, send_sem=sems[k], recv_sem=sems[n + k], device_id=dev,
                                         device_id_type=MESH).start()
        token[...] = jnp.zeros_like(token)

    outs = pl.pallas_call(
        body, name=name,
        out_shape=(*[pltpu.SemaphoreType.DMA(())] * (2 * n), *[pltpu.HBM(a.shape, a.dtype) for a in srcs + lands],
                   jax.ShapeDtypeStruct((8, LANES), F32)),
        in_specs=[HBM] * (2 * ns) + [ANY] * len(extra),
        out_specs=(*[SEM] * (2 * n), *[HBM] * (2 * ns), pl.BlockSpec(memory_space=pltpu.VMEM)),
        input_output_aliases={t: 2 * n + t for t in range(2 * ns)},
        compiler_params=pltpu.CompilerParams(has_side_effects=pltpu.SideEffectType.DATAFLOW_SIDE_EFFECTING),
    )(*srcs, *lands, *extra)
    return list(outs[:2 * n]), list(outs[2 * n:2 * n + ns]), list(outs[2 * n + ns:2 * n + 2 * ns]), outs[-1]


def _copies_wait(name, kind, sems, srcs, lands, after):
    ns = len(srcs)
    n = len(sems) // 2

    def body(*refs):
        sem_refs = refs[2 * ns:2 * ns + 2 * n]
        for k, (src, dst, dev) in enumerate(_split_plan(kind, refs[:ns], refs[ns:2 * ns])):
            cp = pltpu.make_async_remote_copy(src_ref=src, dst_ref=dst, send_sem=sem_refs[k], recv_sem=sem_refs[n + k],
                                              device_id=dev, device_id_type=MESH)
            cp.wait_send()
            cp.wait_recv()

    outs = pl.pallas_call(
        body, name=name, out_shape=tuple(pltpu.HBM(a.shape, a.dtype) for a in srcs + lands),
        in_specs=[HBM] * (2 * ns) + [SEM] * (2 * n) + [ANY], out_specs=tuple([HBM] * (2 * ns)),
        input_output_aliases={t: t for t in range(2 * ns)},
        compiler_params=pltpu.CompilerParams(has_side_effects=pltpu.SideEffectType.DATAFLOW_SIDE_EFFECTING),
    )(*srcs, *lands, *sems, after)
    return list(outs[:ns]), list(outs[ns:])


def _sum_into(pairs, recvs, idx, li, depth, accs):
    nt = len(pairs)

    def body(idx_ref, *refs):
        for p_ref, r_ref, out_ref in zip(refs[:nt], refs[nt:2 * nt], refs[-nt:]):
            out_ref[...] = p_ref[...].astype(F32) + r_ref[0].astype(F32) + r_ref[1].astype(F32) + r_ref[2].astype(F32)

    in_specs = ([pl.BlockSpec((None, p.shape[1] // 2, p.shape[2]), lambda i, idx_ref: (idx_ref[0], i, 0)) for p in pairs]
                + [pl.BlockSpec((3, p.shape[1] // 2, p.shape[2]), lambda i, idx_ref: (0, i, 0)) for p in pairs])
    args = [idx, *pairs, *recvs]
    aliases = {}
    if accs[0] is not None:
        in_specs += [ANY] * nt
        args += list(accs)
        aliases = {1 + 2 * nt + t: t for t in range(nt)}
    return pl.pallas_call(
        body, name="sum_into",
        grid_spec=pltpu.PrefetchScalarGridSpec(
            num_scalar_prefetch=1, grid=(2,), in_specs=in_specs,
            out_specs=[pl.BlockSpec((None, p.shape[1] // 2, p.shape[2]), lambda i, idx_ref: (li, 2 * idx_ref[1] + i, 0))
                       for p in pairs]),
        out_shape=[jax.ShapeDtypeStruct((depth, 2 * p.shape[1], p.shape[2]), F32) for p in pairs],
        input_output_aliases=aliases, compiler_params=_params("arbitrary"))(*args)


def _sum_slots(parts):
    n, rows, cols = parts.shape
    tr = _tile_rows(rows, 512, SUBLANES_WIRE)

    def body(p_ref, out_ref):
        acc = p_ref[0].astype(F32)
        for s in range(1, n):
            acc = acc + p_ref[s].astype(F32)
        out_ref[...] = acc

    return pl.pallas_call(
        body, name="sum_slots", grid=(rows // tr,),
        in_specs=[pl.BlockSpec((n, tr, cols), lambda i: (0, i, 0))],
        out_specs=pl.BlockSpec((tr, cols), lambda i: (i, 0)),
        out_shape=jax.ShapeDtypeStruct((rows, cols), F32),
        compiler_params=_params("arbitrary"))(parts)


def _sibling_share(gs, li):
    nt = len(gs)

    def body(*refs):
        out_refs = refs[nt:2 * nt]
        send_sems, recv_sems = refs[2 * nt:]
        x, y, c = _coords()
        sends, recvs = [], []
        for t in range(nt):
            rh = out_refs[t].shape[1] // 2
            mine, theirs = out_refs[t].at[li, pl.ds(c * rh, rh)], out_refs[t].at[li, pl.ds((1 - c) * rh, rh)]
            sems = dict(send_sem=send_sems.at[t], recv_sem=recv_sems.at[t], device_id=(x, y, 1 - c), device_id_type=MESH)
            sends.append(pltpu.make_async_remote_copy(src_ref=mine, dst_ref=mine, **sems))
            recvs.append(pltpu.make_async_remote_copy(src_ref=theirs, dst_ref=theirs, **sems))
        for cp in sends:
            cp.start()
        for cp in recvs:
            cp.wait_recv()
        for cp in sends:
            cp.wait_send()

    return pl.pallas_call(
        body, name="sibling_share", out_shape=[jax.ShapeDtypeStruct(g.shape, g.dtype) for g in gs],
        in_specs=[ANY] * nt, out_specs=[ANY] * nt, input_output_aliases={t: t for t in range(nt)},
        scratch_shapes=[pltpu.SemaphoreType.DMA((nt,)), pltpu.SemaphoreType.DMA((nt,))])(*gs)


def _all_gather_devices(buf, after=None):
    extra = [] if after is None else [after]

    def body(b_ref, *rest):
        out_ref, send_sems, recv_sems, local_sem = rest[len(extra):]
        x, y, c = _coords()
        me = 4 * x + 2 * y + c
        mine = pltpu.make_async_copy(b_ref, out_ref.at[me], local_sem)
        mine.start()
        peers = []
        for k in range(1, 8):
            fx, fy, fc = (k >> 2) & 1, (k >> 1) & 1, k & 1
            peers.append((x ^ fx, y ^ fy, c ^ fc))
        sends = [pltpu.make_async_remote_copy(src_ref=b_ref, dst_ref=out_ref.at[me], send_sem=send_sems.at[k],
                                              recv_sem=recv_sems.at[k], device_id=peer, device_id_type=MESH)
                 for k, peer in enumerate(peers)]
        for cp in sends:
            cp.start()
        for k, (px, py, pc) in enumerate(peers):
            pltpu.make_async_remote_copy(src_ref=b_ref, dst_ref=out_ref.at[4 * px + 2 * py + pc], send_sem=send_sems.at[k],
                                         recv_sem=recv_sems.at[k], device_id=(px, py, pc), device_id_type=MESH).wait_recv()
        for cp in sends:
            cp.wait_send()
        mine.wait()

    return pl.pallas_call(
        body, name="all_gather_devices", out_shape=jax.ShapeDtypeStruct((8,) + buf.shape, buf.dtype),
        in_specs=[ANY] * (1 + len(extra)), out_specs=ANY,
        scratch_shapes=[pltpu.SemaphoreType.DMA((7,)), pltpu.SemaphoreType.DMA((7,)), pltpu.SemaphoreType.DMA(())])(buf, *extra)


def _pair_sums(big_grads, c_idx):
    gs = [big_grads[n] for n in BIG]
    return _add_my_halves(gs, _sibling_swap_half(gs), c_idx)


SMALL_SHARDED = ("conv_qkv", "sconv_w")
REPLICATED = ("norm1_g", "a_log", "dt_bias", "onorm_g", "pool_w", "pool_scale", "norm2_g", "final_g")
ALL_WEIGHTS = ("norm1_g", "w_in", "conv_qkv", "a_log", "dt_bias", "onorm_g", "pool_w", "pool_scale", "sconv_w", "w_out",
               "norm2_g", "w_gate", "w_up", "w_down", "ple_proj", "ple_gate", "final_g")


def _pad_rows(flat, row_multiple):
    m = flat.shape[0]
    r = -(-m // (LANES * row_multiple)) * row_multiple
    return jnp.pad(flat, (0, r * LANES - m)).reshape(r, LANES)


def _adamw_math(w, g, m, v):
    c1 = 1.0 / (1.0 - ADAM_B1 ** ADAM_STEP)
    c2 = 1.0 / (1.0 - ADAM_B2 ** ADAM_STEP)
    nm = ADAM_B1 * m + (1.0 - ADAM_B1) * g
    nv = ADAM_B2 * v + (1.0 - ADAM_B2) * (g * g)
    return -ADAM_LR * ((nm * c1) / (jnp.sqrt(nv * c2) + ADAM_EPS) + ADAM_WD * w), nm, nv


def _adamw(w, g, m, v):
    shape = w.shape
    cols = shape[-1]
    rows = w.size // cols
    tr = _tile_rows(rows, 512)

    def body(w_ref, g_ref, m_ref, v_ref, d_ref, nm_ref, nv_ref, go_ref):
        gv = g_ref[...]
        d_ref[...], nm_ref[...], nv_ref[...] = _adamw_math(w_ref[...], gv, m_ref[...], v_ref[...])
        go_ref[...] = gv

    spec = pl.BlockSpec((tr, cols), lambda i: (i, 0))
    outs = pl.pallas_call(
        body, name="adamw", grid=(rows // tr,), in_specs=[spec] * 4, out_specs=[spec] * 4,
        out_shape=[jax.ShapeDtypeStruct((rows, cols), F32)] * 4,
        compiler_params=_params("arbitrary"))(*[a.reshape(rows, cols) for a in (w, g, m, v)])
    return tuple(o.reshape(shape) for o in outs)


def kernel(x, p, norm1_g, w_in, conv_qkv, a_log, dt_bias, onorm_g, pool_w, pool_scale, sconv_w, w_out, norm2_g, w_gate, w_up, w_down, ple_proj, ple_gate, final_g, loss_target, m_norm1_g, m_w_in, m_conv_qkv, m_a_log, m_dt_bias, m_onorm_g, m_pool_w, m_pool_scale, m_sconv_w, m_w_out, m_norm2_g, m_w_gate, m_w_up, m_w_down, m_ple_proj, m_ple_gate, m_final_g, v_norm1_g, v_w_in, v_conv_qkv, v_a_log, v_dt_bias, v_onorm_g, v_pool_w, v_pool_scale, v_sconv_w, v_w_out, v_norm2_g, v_w_gate, v_w_up, v_w_down, v_ple_proj, v_ple_gate, v_final_g):
    weights = dict(zip(ALL_WEIGHTS, (norm1_g, w_in, conv_qkv, a_log, dt_bias, onorm_g, pool_w, pool_scale, sconv_w, w_out,
                                     norm2_g, w_gate, w_up, w_down, ple_proj, ple_gate, final_g)))
    mom_m = dict(zip(ALL_WEIGHTS, (m_norm1_g, m_w_in, m_conv_qkv, m_a_log, m_dt_bias, m_onorm_g, m_pool_w, m_pool_scale,
                                   m_sconv_w, m_w_out, m_norm2_g, m_w_gate, m_w_up, m_w_down, m_ple_proj, m_ple_gate, m_final_g)))
    mom_v = dict(zip(ALL_WEIGHTS, (v_norm1_g, v_w_in, v_conv_qkv, v_a_log, v_dt_bias, v_onorm_g, v_pool_w, v_pool_scale,
                                   v_sconv_w, v_w_out, v_norm2_g, v_w_gate, v_w_up, v_w_down, v_ple_proj, v_ple_gate, v_final_g)))
    for n in TRANSPOSED:
        weights[n], mom_m[n], mom_v[n] = (jnp.swapaxes(a[n], 1, 2) for a in (weights, mom_m, mom_v))
    c_idx = lax.axis_index("c").astype(jnp.int32).reshape(1)
    chip = (2 * lax.axis_index("x") + lax.axis_index("y")).astype(jnp.int32)
    me_idx = chip.reshape(1)
    idx = jnp.stack([chip, lax.axis_index("c").astype(jnp.int32)])
    depth = p.shape[0]

    placed = _place_shards([weights[n] for n in BIG], me_idx)
    gw = [dict() for _ in range(depth)]
    gw[0]["w_in"] = _all_gather_chips(placed[0][:1])[0]
    early = ("w_in", "w_out")
    late = tuple(n for n in BIG if n not in early)
    groups = [dict(li=0, names=BIG[1:], forward=(0, "mixed"), finish=(0, "mixed"))]
    for li in range(1, depth):
        groups.append(dict(li=li, names=early, forward=(li - 1, "ffn"), finish=(li - 1, "end")))
        groups.append(dict(li=li, names=late, forward=(li, "mixed"), finish=(li, "mixed")))
    def arrive(li, stage, after):
        for k, g in enumerate(groups):
            if g["forward"] == (li, stage):
                g["sems"], g["arrs"], _ = _gather_call("gather_forward_%d" % k, g["arrs"], g["sems"], after, 1)
            if g["finish"] == (li, stage):
                _, g["arrs"], _ = _gather_call("gather_finish_%d" % k, g["arrs"], g["sems"], after, 2)
                gw[g["li"]].update(zip(g["names"], g["arrs"]))

    small = {n: weights[n] for n in REPLICATED}
    sflat = _pad_rows(jnp.concatenate([weights[n].reshape(-1) for n in SMALL_SHARDED]), 8)
    sgath8 = _all_gather_devices(sflat, gw[0]["w_in"])
    sgath = sgath8[0::2].reshape(4, -1)
    off = 0
    for n in SMALL_SHARDED:
        shp = weights[n].shape
        part = sgath[:, off:off + weights[n].size].reshape((4,) + shp)
        small[n] = jnp.moveaxis(part, 0, -2).reshape(shp[:-1] + (4 * shp[-1],))
        off += weights[n].size
    token = sgath8
    for k, g in enumerate(groups):
        arrs = [placed[g["li"]][BIG.index(n)] for n in g["names"]]
        g["sems"], g["arrs"], token = _gather_call("gather_start_%d" % k, arrs, [], token, 0)

    small["norm1_g"] = small["norm1_g"] + token[0, 0]

    pending = []
    last_token = [None]

    def advance(g, after):
        if g["stage"] == 0:
            gs, others = _copies_wait("swap_wait_" + g["tag"], "swap", *g["handle"], after)
            g["handle"] = _copies_start("exchange_start_" + g["tag"], "exchange", _add_my_halves(gs, others, c_idx))
            g["stage"] = 1
            return g["handle"][3]
        return None

    def produced(li, stage, grads, after):
        token = None
        for g in pending:
            token = advance(g, after) if g["stage"] == 0 else token
        if grads:
            names = [n for n in BIG if n in grads]
            handle = _copies_start("swap_start_%d%s" % (li, stage), "swap", [grads[n] for n in names], token)
            pending.append(dict(li=li, names=names, tag="%d%s" % (li, stage), stage=0, handle=handle[:3]))
            token = handle[3]
        last_token[0] = last_token[0] if token is None else token
        return token

    loss_local, dx, _, small_grads = _local_step(x[0], p[:, 0], loss_target[0], gw, small, produced, arrive)
    accs, big_outs = {}, {}

    def finish(g, after):
        pairs, recvs = _copies_wait("exchange_wait_" + g["tag"], "exchange", *g["handle"][:3], after)
        summed = _sum_into(pairs, recvs, idx, g["li"], depth, [accs.get(n) for n in g["names"]])
        accs.update(zip(g["names"], _sibling_share(summed, g["li"])))
        return accs[g["names"][-1]]

    def update(names):
        for n in names:
            big_outs[n] = _adamw(weights[n], accs[n], mom_m[n], mom_v[n])
        return big_outs[names[-1]][0]

    done = finish(pending[0], last_token[0])
    done = advance(pending[-1], done)
    for g in pending[1:-1]:
        done = finish(g, done)
    last = pending[-1]["names"]
    done = update([n for n in BIG if n not in last])
    finish(pending[-1], done)
    update(last)


    gshard = {}
    rnames = REPLICATED + SMALL_SHARDED
    rflat = _pad_rows(jnp.concatenate([small_grads[n].reshape(-1) for n in rnames]), 8)
    rsum = _sum_slots(_all_gather_devices(rflat)).reshape(-1)
    off = 0
    for n in rnames:
        whole = rsum[off:off + small_grads[n].size].reshape(small_grads[n].shape)
        off += small_grads[n].size
        if n in SMALL_SHARDED:
            cols = weights[n].shape[-1]
            whole = lax.dynamic_slice_in_dim(whole, chip * cols, cols, axis=whole.ndim - 1)
        gshard[n] = whole

    loss = lax.psum(loss_local, ("x", "y", "c"))
    deltas, new_m, new_v, grad_out = {}, {}, {}, {}
    for n in ALL_WEIGHTS:
        if n in BIG:
            deltas[n], new_m[n], new_v[n], grad_out[n] = big_outs[n]
        else:
            deltas[n], new_m[n], new_v[n], grad_out[n] = _adamw(weights[n], gshard[n], mom_m[n], mom_v[n])
    for n in TRANSPOSED:
        deltas[n], new_m[n], new_v[n], grad_out[n] = (jnp.swapaxes(a[n], 1, 2) for a in (deltas, new_m, new_v, grad_out))
    return (loss, dx[None], *[grad_out[n] for n in ALL_WEIGHTS], *[deltas[n] for n in ALL_WEIGHTS],
            *[new_m[n] for n in ALL_WEIGHTS], *[new_v[n] for n in ALL_WEIGHTS])
```

```python
import jax
import jax.numpy as jnp
from jax import lax
from jax.experimental import pallas as pl
from jax.experimental.pallas import tpu as pltpu

F32 = jnp.float32
MM_DTYPE = jnp.bfloat16
WIRE_DTYPE = jnp.bfloat16
HI = lax.Precision.HIGHEST
EPS = 1e-6
HEAD_DIM = 128
CHUNK = 64
QKV_CONV_WIDTH = 4
SCONV_WIDTH = 3
POOL_GROUPS = 4
LANES = 128
SUBLANES_WIRE = 16
VMEM_LIMIT_BYTES = 56 * 1024 * 1024
ADAM_LR, ADAM_B1, ADAM_B2, ADAM_EPS, ADAM_WD, ADAM_STEP = 0.001, 0.9, 0.999, 1e-08, 0.01, 10
MESH = pl.DeviceIdType.MESH
ANY = pl.BlockSpec(memory_space=pl.ANY)
HBM = pl.BlockSpec(memory_space=pltpu.HBM)
SEM = pl.BlockSpec(memory_space=pltpu.SEMAPHORE)


def _params(*sem):
    return pltpu.CompilerParams(vmem_limit_bytes=VMEM_LIMIT_BYTES, dimension_semantics=sem if sem else None)


def _mm(a, b):
    return jnp.dot(a.astype(MM_DTYPE), b.astype(MM_DTYPE), preferred_element_type=F32)


def _mm_nt(a, b):
    return lax.dot_general(a.astype(MM_DTYPE), b.astype(MM_DTYPE), (((1,), (1,)), ((), ())), preferred_element_type=F32)


def _mm_tn(a, b):
    return lax.dot_general(a.astype(MM_DTYPE), b.astype(MM_DTYPE), (((0,), (0,)), ((), ())), preferred_element_type=F32)


def _hmm(a, b):
    return jnp.dot(a, b, preferred_element_type=F32, precision=HI)


def _hmm_nt(a, b):
    return lax.dot_general(a, b, (((1,), (1,)), ((), ())), preferred_element_type=F32, precision=HI)


def _hmm_tn(a, b):
    return lax.dot_general(a, b, (((0,), (0,)), ((), ())), preferred_element_type=F32, precision=HI)


def _sigmoid(x):
    return 1.0 / (1.0 + jnp.exp(-x))


def _dsilu(x, s):
    return s * (1.0 + x * (1.0 - s))


def _rows(shape):
    return lax.broadcasted_iota(jnp.int32, shape, 0)


def _shift_down(x, s):
    if s == 0:
        return x
    return jnp.where(_rows(x.shape) >= s, pltpu.roll(x, s, 0), 0.0)


def _shift_up(x, s):
    if s == 0:
        return x
    t = x.shape[0]
    return jnp.where(_rows(x.shape) < t - s, pltpu.roll(x, t - s, 0), 0.0)


def _rms_fwd(x):
    r = lax.rsqrt(jnp.mean(x * x, axis=-1, keepdims=True) + EPS)
    return x * r, r


def _rms_bwd(dxn, xn, r):
    return r * (dxn - xn * jnp.mean(dxn * xn, axis=-1, keepdims=True))


def _tile_rows(n, cap, mult=8):
    best = None
    for d in range(mult, min(n, cap) + 1, mult):
        if n % d == 0:
            best = d
    return best if best is not None else n


def _in_proj_fwd(x, g1, wp, segs, tm):
    t, d = x.shape
    npk = wp.shape[1]

    def body(x_ref, g_ref, w_ref, *o_refs):
        xn, _ = _rms_fwd(x_ref[...])
        h = (xn * g_ref[...]).astype(w_ref.dtype)
        off = 0
        for o_ref, wd in zip(o_refs, segs):
            o_ref[...] = jnp.dot(h, w_ref[:, off:off + wd], preferred_element_type=F32)
            off += wd

    return pl.pallas_call(
        body, name="in_proj_fwd", grid=(t // tm,),
        in_specs=[pl.BlockSpec((tm, d), lambda i: (i, 0)), pl.BlockSpec((1, d), lambda i: (0, 0)),
                  pl.BlockSpec((d, npk), lambda i: (0, 0))],
        out_specs=[pl.BlockSpec((tm, wd), lambda i: (i, 0)) for wd in segs],
        out_shape=[jax.ShapeDtypeStruct((t, wd), F32) for wd in segs],
        compiler_params=_params("arbitrary"))(x, g1, wp)


def _in_proj_bwd(x, g1, wp, dsegs, dx_res, segs, tm):
    t, d = x.shape
    npk = wp.shape[1]
    nseg = len(segs)

    def body(x_ref, g_ref, w_ref, *rest):
        ds_refs = rest[:nseg]
        dxr_ref, dx_ref, dw_ref, dg_ref = rest[nseg:]
        i = pl.program_id(0)

        @pl.when(i == 0)
        def _():
            dw_ref[...] = jnp.zeros_like(dw_ref)
            dg_ref[...] = jnp.zeros_like(dg_ref)

        xn, r = _rms_fwd(x_ref[...])
        g = g_ref[...]
        h = (xn * g).astype(w_ref.dtype)
        dh = jnp.zeros((tm, d), F32)
        off = 0
        for ds_ref, wd in zip(ds_refs, segs):
            dsv = ds_ref[...].astype(w_ref.dtype)
            dh = dh + lax.dot_general(dsv, w_ref[:, off:off + wd], (((1,), (1,)), ((), ())), preferred_element_type=F32)
            dw_ref[:, off:off + wd] += lax.dot_general(h, dsv, (((0,), (0,)), ((), ())), preferred_element_type=F32)
            off += wd
        dg_ref[...] += jnp.sum(dh * xn, axis=0, keepdims=True)
        dx_ref[...] = dxr_ref[...] + _rms_bwd(dh * g, xn, r)

    return pl.pallas_call(
        body, name="in_proj_bwd", grid=(t // tm,),
        in_specs=[pl.BlockSpec((tm, d), lambda i: (i, 0)), pl.BlockSpec((1, d), lambda i: (0, 0)),
                  pl.BlockSpec((d, npk), lambda i: (0, 0))]
                 + [pl.BlockSpec((tm, wd), lambda i: (i, 0)) for wd in segs]
                 + [pl.BlockSpec((tm, d), lambda i: (i, 0))],
        out_specs=[pl.BlockSpec((tm, d), lambda i: (i, 0)), pl.BlockSpec((d, npk), lambda i: (0, 0)),
                   pl.BlockSpec((1, d), lambda i: (0, 0))],
        out_shape=[jax.ShapeDtypeStruct((t, d), F32), jax.ShapeDtypeStruct((d, npk), F32),
                   jax.ShapeDtypeStruct((1, d), F32)],
        compiler_params=_params("arbitrary"))(x, g1, wp, *dsegs, dx_res)


def _out_proj_fwd(x0, mix, wo, g2, tm):
    t, d = x0.shape
    dq = wo.shape[1]
    widths = [m.shape[1] for m in mix]

    def body(x_ref, *rest):
        m_refs = rest[:len(mix)]
        w_ref, g_ref, x1_ref, h2_ref = rest[len(mix):]
        acc = x_ref[...]
        off = 0
        for m_ref, wd in zip(m_refs, widths):
            for k in range(wd // dq):
                acc = acc + jnp.dot(m_ref[:, k * dq:(k + 1) * dq].astype(w_ref.dtype), w_ref[off // dq + k],
                                    preferred_element_type=F32)
            off += wd
        x1_ref[...] = acc
        xn, _ = _rms_fwd(acc)
        h2_ref[...] = (xn * g_ref[...]).astype(h2_ref.dtype)

    return pl.pallas_call(
        body, name="out_proj_fwd", grid=(t // tm,),
        in_specs=[pl.BlockSpec((tm, d), lambda i: (i, 0))]
                 + [pl.BlockSpec((tm, wd), lambda i: (i, 0)) for wd in widths]
                 + [pl.BlockSpec((4, dq, d), lambda i: (0, 0, 0)), pl.BlockSpec((1, d), lambda i: (0, 0))],
        out_specs=[pl.BlockSpec((tm, d), lambda i: (i, 0)), pl.BlockSpec((tm, d), lambda i: (i, 0))],
        out_shape=[jax.ShapeDtypeStruct((t, d), F32), jax.ShapeDtypeStruct((t, d), MM_DTYPE)],
        compiler_params=_params("arbitrary"))(x0, *mix, wo, g2)


def _out_proj_bwd(dx2, dh2, x1, g2, mix, wo, tm):
    t, d = x1.shape
    dq = wo.shape[1]
    widths = [m.shape[1] for m in mix]
    nm = len(mix)

    def body(dx2_ref, dh2_ref, x1_ref, g_ref, *rest):
        m_refs = rest[:nm]
        w_ref = rest[nm]
        dx1_ref = rest[nm + 1]
        dm_refs = rest[nm + 2:nm + 2 + nm]
        dw_ref, dg_ref = rest[nm + 2 + nm:]
        i = pl.program_id(0)

        @pl.when(i == 0)
        def _():
            dw_ref[...] = jnp.zeros_like(dw_ref)
            dg_ref[...] = jnp.zeros_like(dg_ref)

        xn, r = _rms_fwd(x1_ref[...])
        dh2v = dh2_ref[...]
        dg_ref[...] += jnp.sum(dh2v * xn, axis=0, keepdims=True)
        dx1 = dx2_ref[...] + _rms_bwd(dh2v * g_ref[...], xn, r)
        dx1_ref[...] = dx1
        dx1c = dx1.astype(w_ref.dtype)
        off = 0
        for m_ref, dm_ref, wd in zip(m_refs, dm_refs, widths):
            for k in range(wd // dq):
                j = off // dq + k
                cols = slice(k * dq, (k + 1) * dq)
                dm_ref[:, cols] = lax.dot_general(dx1c, w_ref[j], (((1,), (1,)), ((), ())), preferred_element_type=F32)
                dw_ref[j] += lax.dot_general(m_ref[:, cols].astype(w_ref.dtype), dx1c, (((0,), (0,)), ((), ())),
                                             preferred_element_type=F32)
            off += wd

    tile = lambda wd: pl.BlockSpec((tm, wd), lambda i: (i, 0))
    return pl.pallas_call(
        body, name="out_proj_bwd", grid=(t // tm,),
        in_specs=[tile(d), tile(d), tile(d), pl.BlockSpec((1, d), lambda i: (0, 0))]
                 + [tile(wd) for wd in widths] + [pl.BlockSpec((4, dq, d), lambda i: (0, 0, 0))],
        out_specs=[tile(d)] + [tile(wd) for wd in widths]
                  + [pl.BlockSpec((4, dq, d), lambda i: (0, 0, 0)), pl.BlockSpec((1, d), lambda i: (0, 0))],
        out_shape=[jax.ShapeDtypeStruct((t, d), F32)] + [jax.ShapeDtypeStruct((t, wd), F32) for wd in widths]
                  + [jax.ShapeDtypeStruct((4, dq, d), F32), jax.ShapeDtypeStruct((1, d), F32)],
        compiler_params=_params("arbitrary"))(dx2, dh2, x1, g2, *mix, wo)


def _ffn_fwd(x1, h2, wg, wu, wd, tm):
    t, d = x1.shape
    fs = wg.shape[1]

    def body(x1_ref, h2_ref, wg_ref, wu_ref, wd_ref, x2_ref, gp_ref, up_ref):
        @pl.when(pl.program_id(1) == 0)
        def _():
            x2_ref[...] = x1_ref[...]

        h = h2_ref[...]
        nt = (((1,), (1,)), ((), ()))
        gp = lax.dot_general(h, wg_ref[...], nt, preferred_element_type=F32)
        up = lax.dot_general(h, wu_ref[...], nt, preferred_element_type=F32)
        gp_ref[...] = gp
        up_ref[...] = up
        ff = gp * _sigmoid(gp) * up
        x2_ref[...] += jnp.dot(ff.astype(wd_ref.dtype), wd_ref[...], preferred_element_type=F32)

    return pl.pallas_call(
        body, name="ffn_fwd", grid=(t // tm, 4),
        in_specs=[pl.BlockSpec((tm, d), lambda i, j: (i, 0)), pl.BlockSpec((tm, d), lambda i, j: (i, 0)),
                  pl.BlockSpec((None, fs, d), lambda i, j: (j, 0, 0)),
                  pl.BlockSpec((None, fs, d), lambda i, j: (j, 0, 0)),
                  pl.BlockSpec((None, fs, d), lambda i, j: (j, 0, 0))],
        out_specs=[pl.BlockSpec((tm, d), lambda i, j: (i, 0)), pl.BlockSpec((None, tm, fs), lambda i, j: (j, i, 0)),
                   pl.BlockSpec((None, tm, fs), lambda i, j: (j, i, 0))],
        out_shape=[jax.ShapeDtypeStruct((t, d), F32), jax.ShapeDtypeStruct((4, t, fs), F32),
                   jax.ShapeDtypeStruct((4, t, fs), F32)],
        compiler_params=_params("arbitrary", "arbitrary"))(x1, h2, wg, wu, wd)


def _ffn_bwd(dx2, h2, gp, up, wg, wu, wd, tm):
    t, d = dx2.shape
    fs = wg.shape[1]

    def body(dx2_ref, h2_ref, gp_ref, up_ref, wg_ref, wu_ref, wd_ref, dh2_ref, dwg_ref, dwu_ref, dwd_ref):
        j, i = pl.program_id(0), pl.program_id(1)

        @pl.when(i == 0)
        def _():
            dwg_ref[...] = jnp.zeros_like(dwg_ref)
            dwu_ref[...] = jnp.zeros_like(dwu_ref)
            dwd_ref[...] = jnp.zeros_like(dwd_ref)

        cdt = wg_ref.dtype
        h = h2_ref[...]
        gpv, upv = gp_ref[...], up_ref[...]
        s = _sigmoid(gpv)
        silu = gpv * s
        dx2c = dx2_ref[...].astype(cdt)
        dff = lax.dot_general(dx2c, wd_ref[...], (((1,), (1,)), ((), ())), preferred_element_type=F32)
        dwd_ref[...] += lax.dot_general((silu * upv).astype(cdt), dx2c, (((0,), (0,)), ((), ())), preferred_element_type=F32)
        dup = (dff * silu).astype(cdt)
        dgp = (dff * upv * _dsilu(gpv, s)).astype(cdt)
        dwg_ref[...] += lax.dot_general(dgp, h, (((0,), (0,)), ((), ())), preferred_element_type=F32)
        dwu_ref[...] += lax.dot_general(dup, h, (((0,), (0,)), ((), ())), preferred_element_type=F32)
        dh = (jnp.dot(dgp, wg_ref[...], preferred_element_type=F32) + jnp.dot(dup, wu_ref[...], preferred_element_type=F32))
        rows = pl.ds(pl.multiple_of(i * tm, tm), tm)

        @pl.when(j == 0)
        def _():
            dh2_ref[rows, :] = dh

        @pl.when(j != 0)
        def _():
            dh2_ref[rows, :] += dh

    return pl.pallas_call(
        body, name="ffn_bwd", grid=(4, t // tm),
        in_specs=[pl.BlockSpec((tm, d), lambda j, i: (i, 0)), pl.BlockSpec((tm, d), lambda j, i: (i, 0)),
                  pl.BlockSpec((None, tm, fs), lambda j, i: (j, i, 0)), pl.BlockSpec((None, tm, fs), lambda j, i: (j, i, 0)),
                  pl.BlockSpec((None, fs, d), lambda j, i: (j, 0, 0)),
                  pl.BlockSpec((None, fs, d), lambda j, i: (j, 0, 0)),
                  pl.BlockSpec((None, fs, d), lambda j, i: (j, 0, 0))],
        out_specs=[pl.BlockSpec((t, d), lambda j, i: (0, 0)), pl.BlockSpec((None, fs, d), lambda j, i: (j, 0, 0)),
                   pl.BlockSpec((None, fs, d), lambda j, i: (j, 0, 0)), pl.BlockSpec((None, fs, d), lambda j, i: (j, 0, 0))],
        out_shape=[jax.ShapeDtypeStruct((t, d), F32)] + [jax.ShapeDtypeStruct((4, fs, d), F32)] * 3,
        compiler_params=_params("arbitrary", "arbitrary"))(dx2, h2, gp, up, wg, wu, wd)


def _ple_fwd(x2, p, wpg, wpp, tm):
    t, d = x2.shape
    q = p.shape[1]
    dq = d // 4

    def body(x_ref, p_ref, wg_ref, wp_ref, o_ref):
        xv = x_ref[...]
        xc = xv.astype(wg_ref.dtype)
        pc = p_ref[...].astype(wp_ref.dtype)
        pre = jnp.dot(xc[:, :dq], wg_ref[0], preferred_element_type=F32)
        for j in range(1, 4):
            pre = pre + jnp.dot(xc[:, j * dq:(j + 1) * dq], wg_ref[j], preferred_element_type=F32)
        gate = _sigmoid(pre)
        for j in range(4):
            cols = slice(j * dq, (j + 1) * dq)
            o_ref[:, cols] = xv[:, cols] + gate[:, cols] * jnp.dot(pc, wp_ref[j], preferred_element_type=F32)

    return pl.pallas_call(
        body, name="ple_fwd", grid=(t // tm,),
        in_specs=[pl.BlockSpec((tm, d), lambda i: (i, 0)), pl.BlockSpec((tm, q), lambda i: (i, 0)),
                  pl.BlockSpec((4, dq, d), lambda i: (0, 0, 0)),
                  pl.BlockSpec((4, q, dq), lambda i: (0, 0, 0))],
        out_specs=pl.BlockSpec((tm, d), lambda i: (i, 0)),
        out_shape=jax.ShapeDtypeStruct((t, d), F32),
        compiler_params=_params("arbitrary"))(x2, p, wpg, wpp)


def _ple_bwd(dx3, x2, p, wpg, wpp, tm):
    t, d = x2.shape
    q = p.shape[1]
    dq = d // 4

    def body(dx3_ref, x_ref, p_ref, wg_ref, wp_ref, dx2_ref, dwg_ref, dwp_ref):
        @pl.when(pl.program_id(0) == 0)
        def _():
            dwg_ref[...] = jnp.zeros_like(dwg_ref)
            dwp_ref[...] = jnp.zeros_like(dwp_ref)

        cdt = wg_ref.dtype
        xc = x_ref[...].astype(cdt)
        pc = p_ref[...].astype(cdt)
        pre = jnp.dot(xc[:, :dq], wg_ref[0], preferred_element_type=F32)
        for j in range(1, 4):
            pre = pre + jnp.dot(xc[:, j * dq:(j + 1) * dq], wg_ref[j], preferred_element_type=F32)
        gate = _sigmoid(pre)
        dx3v = dx3_ref[...]
        dpp = (dx3v * gate).astype(cdt)
        dgate = dx3v * gate * (1.0 - gate)
        dpre_parts = []
        for j in range(4):
            cols = slice(j * dq, (j + 1) * dq)
            pp_j = jnp.dot(pc, wp_ref[j], preferred_element_type=F32)
            dpre_parts.append((dgate[:, cols] * pp_j).astype(cdt))
            dwp_ref[j] += lax.dot_general(pc, dpp[:, cols], (((0,), (0,)), ((), ())), preferred_element_type=F32)
        dpre = jnp.concatenate(dpre_parts, axis=1)
        for j in range(4):
            cols = slice(j * dq, (j + 1) * dq)
            dwg_ref[j] += lax.dot_general(xc[:, cols], dpre, (((0,), (0,)), ((), ())), preferred_element_type=F32)
            dx2_ref[:, cols] = dx3v[:, cols] + lax.dot_general(dpre, wg_ref[j], (((1,), (1,)), ((), ())),
                                                               preferred_element_type=F32)

    return pl.pallas_call(
        body, name="ple_bwd", grid=(t // tm,),
        in_specs=[pl.BlockSpec((tm, d), lambda i: (i, 0)), pl.BlockSpec((tm, d), lambda i: (i, 0)),
                  pl.BlockSpec((tm, q), lambda i: (i, 0)), pl.BlockSpec((4, dq, d), lambda i: (0, 0, 0)),
                  pl.BlockSpec((4, q, dq), lambda i: (0, 0, 0))],
        out_specs=[pl.BlockSpec((tm, d), lambda i: (i, 0)), pl.BlockSpec((4, dq, d), lambda i: (0, 0, 0)),
                   pl.BlockSpec((4, q, dq), lambda i: (0, 0, 0))],
        out_shape=[jax.ShapeDtypeStruct((t, d), F32), jax.ShapeDtypeStruct((4, dq, d), F32),
                   jax.ShapeDtypeStruct((4, q, dq), F32)],
        compiler_params=_params("arbitrary"))(dx3, x2, p, wpg, wpp)


def _loss_head(x, target, fg, tm):
    t, d = x.shape

    def body(x_ref, t_ref, g_ref, dx_ref, loss_ref, dg_ref):
        @pl.when(pl.program_id(0) == 0)
        def _():
            loss_ref[...] = jnp.zeros_like(loss_ref)
            dg_ref[...] = jnp.zeros_like(dg_ref)

        xn, r = _rms_fwd(x_ref[...])
        g = g_ref[...]
        err = xn * g - t_ref[...]
        loss_ref[...] += 0.5 * jnp.sum(jnp.sum(err * err, axis=-1, keepdims=True) / d, axis=0, keepdims=True)
        dy = err / d
        dg_ref[...] += jnp.sum(dy * xn, axis=0, keepdims=True)
        dx_ref[...] = _rms_bwd(dy * g, xn, r)

    return pl.pallas_call(
        body, name="loss_head", grid=(t // tm,),
        in_specs=[pl.BlockSpec((tm, d), lambda i: (i, 0)), pl.BlockSpec((tm, d), lambda i: (i, 0)),
                  pl.BlockSpec((1, d), lambda i: (0, 0))],
        out_specs=[pl.BlockSpec((tm, d), lambda i: (i, 0)), pl.BlockSpec((1, 1), lambda i: (0, 0)),
                   pl.BlockSpec((1, d), lambda i: (0, 0))],
        out_shape=[jax.ShapeDtypeStruct((t, d), F32), jax.ShapeDtypeStruct((1, 1), F32),
                   jax.ShapeDtypeStruct((1, d), F32)],
        compiler_params=_params("arbitrary"))(x, target, fg)


def _qkv_conv_act(xv, w, j, heads):
    k = QKV_CONV_WIDTH
    y = w[k - 1:k] * xv
    for s in range(1, k):
        y = y + w[k - 1 - s:k - s] * _shift_down(xv, s)
    sg = _sigmoid(y)
    s_act = y * sg
    nrm = lax.rsqrt(jnp.sum(s_act * s_act, axis=-1, keepdims=True) + EPS)
    scale = jnp.where(j < heads, HEAD_DIM ** -0.5, 1.0).astype(F32)
    return y, sg, s_act, nrm, scale


def _qkv_conv_fwd(qkv_pre, conv_w, heads):
    t = qkv_pre.shape[0]
    nblk = 3 * heads

    def body(x_ref, w_ref, o_ref):
        j = pl.program_id(0)
        _, _, s_act, nrm, scale = _qkv_conv_act(x_ref[...], w_ref[...], j, heads)
        o_ref[...] = jnp.where(j < 2 * heads, s_act * (nrm * scale), s_act)

    return pl.pallas_call(
        body, name="qkv_conv_fwd", grid=(nblk,),
        in_specs=[pl.BlockSpec((t, LANES), lambda j: (0, j)), pl.BlockSpec((QKV_CONV_WIDTH, LANES), lambda j: (0, j))],
        out_specs=pl.BlockSpec((t, LANES), lambda j: (0, j)),
        out_shape=jax.ShapeDtypeStruct(qkv_pre.shape, F32),
        compiler_params=_params("arbitrary"))(qkv_pre, conv_w)


def _qkv_conv_bwd(qkv_pre, conv_w, dqkv, heads):
    t = qkv_pre.shape[0]
    nblk = 3 * heads
    k = QKV_CONV_WIDTH

    def body(x_ref, w_ref, dn_ref, dx_ref, dw_ref):
        j = pl.program_id(0)
        xv, w = x_ref[...], w_ref[...]
        y, sg, s_act, nrm, scale = _qkv_conv_act(xv, w, j, heads)
        dn = dn_ref[...]
        dsn = dn * scale
        ds_qk = nrm * dsn - s_act * (nrm * nrm * nrm) * jnp.sum(dsn * s_act, axis=-1, keepdims=True)
        ds = jnp.where(j < 2 * heads, ds_qk, dn)
        dy = ds * _dsilu(y, sg)
        dx = w[k - 1:k] * dy
        dw_ref[k - 1:k, :] = jnp.sum(dy * xv, axis=0, keepdims=True)
        for s in range(1, k):
            dx = dx + w[k - 1 - s:k - s] * _shift_up(dy, s)
            dw_ref[k - 1 - s:k - s, :] = jnp.sum(dy * _shift_down(xv, s), axis=0, keepdims=True)
        dx_ref[...] = dx

    return pl.pallas_call(
        body, name="qkv_conv_bwd", grid=(nblk,),
        in_specs=[pl.BlockSpec((t, LANES), lambda j: (0, j)), pl.BlockSpec((k, LANES), lambda j: (0, j)),
                  pl.BlockSpec((t, LANES), lambda j: (0, j))],
        out_specs=[pl.BlockSpec((t, LANES), lambda j: (0, j)), pl.BlockSpec((k, LANES), lambda j: (0, j))],
        out_shape=[jax.ShapeDtypeStruct(qkv_pre.shape, F32), jax.ShapeDtypeStruct(conv_w.shape, F32)],
        compiler_params=_params("arbitrary"))(qkv_pre, conv_w, dqkv)


def _pool_windows(shape, j, group_dim):
    lane = lax.broadcasted_iota(jnp.int32, shape, 1) + j * LANES
    grp = lane // group_dim
    win = jnp.left_shift(2, grp).astype(F32)
    cnt = jnp.minimum((_rows(shape) + 1).astype(F32), win)
    return grp, cnt


def _pool_select(grp, levels):
    out = levels[0]
    for gi in range(1, POOL_GROUPS):
        out = jnp.where(grp == gi, levels[gi], out)
    return out


def _pool_mean(hv, grp, cnt):
    acc, levels, width = hv, [], 1
    for _ in range(POOL_GROUPS):
        acc = acc + _shift_down(acc, width)
        width *= 2
        levels.append(acc)
    return _pool_select(grp, levels) / cnt - hv


def _pool_fwd(hp, wbd, scale, group_dim):
    t, dp = hp.shape

    def body(h_ref, w_ref, s_ref, o_ref):
        hv = h_ref[...]
        grp, cnt = _pool_windows(hv.shape, pl.program_id(0), group_dim)
        pooled = _pool_mean(hv, grp, cnt)
        o_ref[...] = _mm(pooled, w_ref[...]) * s_ref[...]

    return pl.pallas_call(
        body, name="pool_fwd", grid=(dp // LANES,),
        in_specs=[pl.BlockSpec((t, LANES), lambda j: (0, j)), pl.BlockSpec((LANES, LANES), lambda j: (j, j)),
                  pl.BlockSpec((1, LANES), lambda j: (0, j))],
        out_specs=pl.BlockSpec((t, LANES), lambda j: (0, j)),
        out_shape=jax.ShapeDtypeStruct(hp.shape, F32),
        compiler_params=_params("arbitrary"))(hp, wbd, scale)


def _pool_bwd(hp, wbd, scale, dob, group_dim):
    t, dp = hp.shape

    def body(h_ref, w_ref, s_ref, do_ref, dh_ref, dw_ref, ds_ref):
        hv = h_ref[...]
        grp, cnt = _pool_windows(hv.shape, pl.program_id(0), group_dim)
        pooled = _pool_mean(hv, grp, cnt)
        wv = w_ref[...]
        dov = do_ref[...]
        ds_ref[...] = jnp.sum(dov * _mm(pooled, wv), axis=0, keepdims=True)
        dys = dov * s_ref[...]
        dw_ref[0] = _mm_tn(pooled, dys)
        dpooled = _mm_nt(dys, wv)
        acc, levels, width = dpooled / cnt, [], 1
        for _ in range(POOL_GROUPS):
            acc = acc + _shift_up(acc, width)
            width *= 2
            levels.append(acc)
        dh_ref[...] = _pool_select(grp, levels) - dpooled

    nb = dp // LANES
    return pl.pallas_call(
        body, name="pool_bwd", grid=(nb,),
        in_specs=[pl.BlockSpec((t, LANES), lambda j: (0, j)), pl.BlockSpec((LANES, LANES), lambda j: (j, j)),
                  pl.BlockSpec((1, LANES), lambda j: (0, j)), pl.BlockSpec((t, LANES), lambda j: (0, j))],
        out_specs=[pl.BlockSpec((t, LANES), lambda j: (0, j)), pl.BlockSpec((1, LANES, LANES), lambda j: (j, 0, 0)),
                   pl.BlockSpec((1, LANES), lambda j: (0, j))],
        out_shape=[jax.ShapeDtypeStruct(hp.shape, F32), jax.ShapeDtypeStruct((nb, LANES, LANES), F32),
                   jax.ShapeDtypeStruct((1, dp), F32)],
        compiler_params=_params("arbitrary"))(hp, wbd, scale, dob)


def _sconv_fwd(cbcch, w):
    t, dc3 = cbcch.shape
    nb = dc3 // 3 // LANES
    k = SCONV_WIDTH

    def body(b_ref, c_ref, h_ref, w_ref, o_ref):
        m = c_ref[...] * h_ref[...]
        wv = w_ref[...]
        y = wv[k - 1:k] * m
        for s in range(1, k):
            y = y + wv[k - 1 - s:k - s] * _shift_down(m, s)
        o_ref[...] = b_ref[...] * y

    return pl.pallas_call(
        body, name="sconv_fwd", grid=(nb,),
        in_specs=[pl.BlockSpec((t, LANES), lambda j: (0, j)), pl.BlockSpec((t, LANES), lambda j: (0, nb + j)),
                  pl.BlockSpec((t, LANES), lambda j: (0, 2 * nb + j)), pl.BlockSpec((k, LANES), lambda j: (0, j))],
        out_specs=pl.BlockSpec((t, LANES), lambda j: (0, j)),
        out_shape=jax.ShapeDtypeStruct((t, dc3 // 3), F32),
        compiler_params=_params("arbitrary"))(cbcch, cbcch, cbcch, w)


def _sconv_bwd(cbcch, w, doc):
    t, dc3 = cbcch.shape
    nb = dc3 // 3 // LANES
    k = SCONV_WIDTH

    def body(b_ref, c_ref, h_ref, w_ref, do_ref, db_ref, dc_ref, dh_ref, dw_ref):
        cv, hv = c_ref[...], h_ref[...]
        m = cv * hv
        wv = w_ref[...]
        dov = do_ref[...]
        dy = dov * b_ref[...]
        y = wv[k - 1:k] * m
        dm = wv[k - 1:k] * dy
        dw_ref[k - 1:k, :] = jnp.sum(dy * m, axis=0, keepdims=True)
        for s in range(1, k):
            ms = _shift_down(m, s)
            y = y + wv[k - 1 - s:k - s] * ms
            dm = dm + wv[k - 1 - s:k - s] * _shift_up(dy, s)
            dw_ref[k - 1 - s:k - s, :] = jnp.sum(dy * ms, axis=0, keepdims=True)
        db_ref[...] = dov * y
        dc_ref[...] = dm * hv
        dh_ref[...] = dm * cv

    col = lambda o: pl.BlockSpec((t, LANES), lambda j: (0, o * nb + j))
    return pl.pallas_call(
        body, name="sconv_bwd", grid=(nb,),
        in_specs=[col(0), col(1), col(2), pl.BlockSpec((k, LANES), lambda j: (0, j)), col(0)],
        out_specs=[col(0), col(0), col(0), pl.BlockSpec((k, LANES), lambda j: (0, j))],
        out_shape=[jax.ShapeDtypeStruct((t, dc3 // 3), F32)] * 3 + [jax.ShapeDtypeStruct(w.shape, F32)],
        compiler_params=_params("arbitrary"))(cbcch, cbcch, cbcch, w, doc)


class _Split:
    def __init__(self, a):
        self.hi = a.astype(jnp.bfloat16)
        self.lo = (a - self.hi.astype(F32)).astype(jnp.bfloat16)


def _per_head(dims, a, b):
    a = a if isinstance(a, _Split) else _Split(a)
    b = b if isinstance(b, _Split) else _Split(b)

    def dot(x, y):
        return lax.dot_general(x, y, (dims, ((), ())), preferred_element_type=F32)

    return jnp.stack([dot(a.hi[h], b.hi[h]) + (dot(a.hi[h], b.lo[h]) + dot(a.lo[h], b.hi[h])) for h in range(a.hi.shape[0])])


def _bmm(a, b):
    return _per_head(((1,), (0,)), a, b)


def _bmm_nt(a, b):
    return _per_head(((1,), (1,)), a, b)


def _bmm_tn(a, b):
    return _per_head(((0,), (0,)), a, b)


def _inv_unit_lower(low):
    c = low.shape[-1]
    eye = (_rows((c, c)) == lax.broadcasted_iota(jnp.int32, (c, c), 1)).astype(F32)
    pw = -low
    inv = eye + pw
    span = 2
    while span < c:
        pws = _Split(pw)
        pw = _bmm(pws, pws)
        inv = inv + _bmm(inv, pw)
        span *= 2
    return inv


def _heads_of(ref, base, heads):
    return jnp.stack([ref[:, base + h * HEAD_DIM:base + (h + 1) * HEAD_DIM] for h in range(heads)])


def _chunk_common(q, k, v, a_col, b_col, alog, dtb, kept=None):
    hn, c, _ = q.shape
    beta = _sigmoid(b_col)
    xg = a_col + dtb
    softplus = jnp.maximum(xg, 0.0) + jnp.log(1.0 + jnp.exp(-jnp.abs(xg)))
    neg_ea = -jnp.exp(alog)
    g = neg_ea * softplus
    ri = _rows((c, c))
    ci = lax.broadcasted_iota(jnp.int32, (c, c), 1)
    incl, strict = ri >= ci, ri > ci
    inclf = jnp.broadcast_to(incl.astype(F32), (hn, c, c))
    gcb = _bmm(inclf, jnp.broadcast_to(g, (hn, c, HEAD_DIM)))
    gc_row = jnp.sum(jnp.where(ri <= ci, jnp.broadcast_to(g, (hn, c, c)), 0.0), axis=1, keepdims=True)
    dmat = jnp.where(incl, jnp.exp(jnp.where(incl, gcb[:, :, :1] - gc_row, 0.0)), 0.0)
    eg = jnp.exp(gcb)
    gl = gcb[:, c - 1:c, :]
    egl = jnp.exp(gl)
    edl = jnp.exp(gl - gcb)
    kb, vb = k * beta, v * beta
    kbe = kb * eg
    if kept is None:
        ks = _Split(k)
        a0 = _bmm_nt(kb, ks)
        tm = _inv_unit_lower(jnp.where(strict, a0 * dmat, 0.0))
        p0 = _bmm_nt(q, ks)
        tms = _Split(tm)
        u, w = _bmm(tms, vb), _bmm(tms, kbe)
    else:
        (a0, tm, p0, w), u = kept, None
    return dict(beta=beta, xg=xg, neg_ea=neg_ea, g=g, incl=incl, strict=strict, inclf=inclf, dmat=dmat, eg=eg,
                egl=egl, edl=edl, kb=kb, vb=vb, a0=a0, tm=tm, kbe=kbe, u=u, w=w, p0=p0,
                attn=p0 * dmat, qe=q * eg, kd=k * edl)


def _chunk_step(cm, state):
    ss = _Split(state)
    vn = cm["u"] - _bmm(cm["w"], ss)
    vns = _Split(vn)
    o = _bmm(cm["qe"], ss) + _bmm(cm["attn"], vns)
    new_state = state * cm["egl"][:, :, :1] + _bmm_tn(cm["kd"], vns)
    return vn, o, new_state


def _gated_norm(o, zv, og):
    xo, ro = _rms_fwd(o)
    sgz = _sigmoid(zv)
    return xo, ro, sgz, xo * og * (zv * sgz)


def _gate_columns(abv, gpv, heads):
    a_col = jnp.stack([abv[:, h:h + 1] for h in range(heads)])
    b_col = jnp.stack([abv[:, heads + h:heads + h + 1] for h in range(heads)])
    alog = jnp.stack([gpv[0:1, h:h + 1] for h in range(heads)])
    dtb = jnp.stack([gpv[1:2, h:h + 1] for h in range(heads)])
    return a_col, b_col, alog, dtb


def _delta_fwd(qkv, z, ab, gpar, heads):
    t = qkv.shape[0]
    da = heads * HEAD_DIM
    n = t // CHUNK

    def body(qkv_ref, z_ref, ab_ref, gp_ref, oa_ref, st_ref, kc_ref, kw_ref, s_ref):
        @pl.when(pl.program_id(0) == 0)
        def _():
            s_ref[...] = jnp.zeros_like(s_ref)

        gpv = gp_ref[...]
        cm = _chunk_common(_heads_of(qkv_ref, 0, heads), _heads_of(qkv_ref, da, heads), _heads_of(qkv_ref, 2 * da, heads),
                           *_gate_columns(ab_ref[...], gpv, heads))
        state = s_ref[...]
        st_ref[0] = state
        vn, o, new_state = _chunk_step(cm, state)
        s_ref[...] = new_state
        for slot, val in enumerate((cm["a0"], cm["tm"], cm["p0"])):
            kc_ref[0, slot] = val
        for slot, val in enumerate((cm["w"], vn, o)):
            kw_ref[0, slot] = val
        oa = _gated_norm(o, _heads_of(z_ref, 0, heads), gpv[2:3, :])[3]
        for h in range(heads):
            oa_ref[:, h * HEAD_DIM:(h + 1) * HEAD_DIM] = oa[h]

    return pl.pallas_call(
        body, name="delta_fwd", grid=(n,),
        in_specs=[pl.BlockSpec((CHUNK, 3 * da), lambda i: (i, 0)), pl.BlockSpec((CHUNK, da), lambda i: (i, 0)),
                  pl.BlockSpec((CHUNK, LANES), lambda i: (i, 0)), pl.BlockSpec((8, LANES), lambda i: (0, 0))],
        out_specs=[pl.BlockSpec((CHUNK, da), lambda i: (i, 0)),
                   pl.BlockSpec((1, heads, HEAD_DIM, HEAD_DIM), lambda i: (i, 0, 0, 0)),
                   pl.BlockSpec((1, 3, heads, CHUNK, CHUNK), lambda i: (i, 0, 0, 0, 0)),
                   pl.BlockSpec((1, 3, heads, CHUNK, HEAD_DIM), lambda i: (i, 0, 0, 0, 0))],
        out_shape=[jax.ShapeDtypeStruct((t, da), F32), jax.ShapeDtypeStruct((n, heads, HEAD_DIM, HEAD_DIM), F32),
                   jax.ShapeDtypeStruct((n, 3, heads, CHUNK, CHUNK), F32),
                   jax.ShapeDtypeStruct((n, 3, heads, CHUNK, HEAD_DIM), F32)],
        scratch_shapes=[pltpu.VMEM((heads, HEAD_DIM, HEAD_DIM), F32)],
        compiler_params=_params("arbitrary"))(qkv, z, ab, gpar)


def _delta_bwd(qkv, z, ab, gpar, states, kept_c, kept_w, doa, heads):
    t = qkv.shape[0]
    da = heads * HEAD_DIM
    n = t // CHUNK
    c = CHUNK

    def body(qkv_ref, z_ref, ab_ref, gp_ref, st_ref, kc_ref, kw_ref, doa_ref, dqkv_ref, dz_ref, dab_ref, dpar_ref, ds_ref):
        @pl.when(pl.program_id(0) == 0)
        def _():
            ds_ref[...] = jnp.zeros_like(ds_ref)
            dpar_ref[...] = jnp.zeros_like(dpar_ref)

        gpv = gp_ref[...]
        og = gpv[2:3, :]
        q, k, v = _heads_of(qkv_ref, 0, heads), _heads_of(qkv_ref, da, heads), _heads_of(qkv_ref, 2 * da, heads)
        cm = _chunk_common(q, k, v, *_gate_columns(ab_ref[...], gpv, heads),
                           kept=(kc_ref[0, 0], kc_ref[0, 1], kc_ref[0, 2], kw_ref[0, 0]))
        state = st_ref[0]
        dsp = ds_ref[...]
        vn, o = kw_ref[0, 1], kw_ref[0, 2]
        zv = _heads_of(z_ref, 0, heads)
        xo, ro, sgz, _ = _gated_norm(o, zv, og)
        doav = _heads_of(doa_ref, 0, heads)
        don = doav * (zv * sgz)
        dz = doav * (xo * og) * _dsilu(zv, sgz)
        d_og = jnp.sum(jnp.sum(don * xo, axis=1, keepdims=True), axis=0)
        do = _rms_bwd(don * og, xo, ro)
        tm, dmat, eg, edl, egl = cm["tm"], cm["dmat"], cm["eg"], cm["edl"], cm["egl"]
        dos, dsps, sts, tms, ks = _Split(do), _Split(dsp), _Split(state), _Split(tm), _Split(k)
        dvn = _bmm_tn(cm["attn"], dos) + _bmm(cm["kd"], dsps)
        dvns = _Split(dvn)
        dqe = _bmm_nt(dos, sts)
        ds_ref[...] = _bmm_tn(cm["qe"], dos) + dsp * egl[:, :, :1] - _bmm_tn(cm["w"], dvns)
        dattn = _bmm_nt(dos, vn)
        dkd = _bmm_nt(vn, dsps)
        dkd_kd = jnp.sum(dkd * cm["kd"], axis=-1, keepdims=True)
        dgl = (jnp.sum(jnp.sum(dsp * state, axis=-1, keepdims=True), axis=1, keepdims=True) * egl[:, :, :1]
               + jnp.sum(dkd_kd, axis=1, keepdims=True))
        dgc = jnp.sum(dqe * cm["qe"], axis=-1, keepdims=True) - dkd_kd
        dk = dkd * edl
        dq = dqe * eg
        dw = -_bmm_nt(dvns, sts)
        dws = _Split(dw)
        dp0 = dattn * dmat
        dd = jnp.where(cm["incl"], dattn * cm["p0"], 0.0)
        dp0s = _Split(dp0)
        dq = dq + _bmm(dp0s, ks)
        dk = dk + _bmm_tn(dp0s, q)
        dtm = _bmm_nt(dvns, cm["vb"]) + _bmm_nt(dws, cm["kbe"])
        dvb = _bmm_tn(tms, dvns)
        dkbe = _bmm_tn(tms, dws)
        dkb = dkbe * eg
        dgc = dgc + jnp.sum(dkbe * cm["kbe"], axis=-1, keepdims=True)
        dlow = jnp.where(cm["strict"], -_bmm_tn(tms, _bmm_nt(dtm, tms)), 0.0)
        dd = dd + dlow * cm["a0"]
        da0 = dlow * dmat
        da0s = _Split(da0)
        dkb = dkb + _bmm(da0s, ks)
        dk = dk + _bmm_tn(da0s, cm["kb"])
        ddd = dd * dmat
        ones = jnp.ones((heads, c, HEAD_DIM), F32)
        dgc = dgc + jnp.sum(ddd, axis=-1, keepdims=True) - _bmm_tn(ddd, ones)[:, :, :1]
        dgc = dgc + jnp.where(_rows((c, 1)) == c - 1, dgl, 0.0)
        dg = _bmm_tn(cm["inclf"], jnp.broadcast_to(dgc, (heads, c, HEAD_DIM)))[:, :, :1]
        beta = cm["beta"]
        dk = dk + dkb * beta
        dbeta = jnp.sum(dkb * k, axis=-1, keepdims=True) + jnp.sum(dvb * v, axis=-1, keepdims=True)
        dv = dvb * beta
        db_col = dbeta * beta * (1.0 - beta)
        da_col = dg * cm["neg_ea"] * _sigmoid(cm["xg"])
        d_alog = jnp.sum(dg * cm["g"], axis=1, keepdims=True)
        d_dtb = jnp.sum(da_col, axis=1, keepdims=True)
        lane = lax.broadcasted_iota(jnp.int32, (c, LANES), 1)
        lane8 = lax.broadcasted_iota(jnp.int32, (8, LANES), 1)
        row8 = _rows((8, LANES))
        dab = jnp.zeros((c, LANES), F32)
        dpar = jnp.where(row8 == 2, d_og, 0.0)
        for h in range(heads):
            lo = h * HEAD_DIM
            dqkv_ref[:, lo:lo + HEAD_DIM] = dq[h]
            dqkv_ref[:, da + lo:da + lo + HEAD_DIM] = dk[h]
            dqkv_ref[:, 2 * da + lo:2 * da + lo + HEAD_DIM] = dv[h]
            dz_ref[:, lo:lo + HEAD_DIM] = dz[h]
            dab = dab + jnp.where(lane == h, da_col[h], 0.0) + jnp.where(lane == heads + h, db_col[h], 0.0)
            dpar = (dpar + jnp.where((row8 == 0) & (lane8 == h), d_alog[h], 0.0)
                    + jnp.where((row8 == 1) & (lane8 == h), d_dtb[h], 0.0))
        dab_ref[...] = dab
        dpar_ref[...] += dpar

    rev = lambda i: (n - 1 - i, 0)
    return pl.pallas_call(
        body, name="delta_bwd", grid=(n,),
        in_specs=[pl.BlockSpec((c, 3 * da), rev), pl.BlockSpec((c, da), rev), pl.BlockSpec((c, LANES), rev),
                  pl.BlockSpec((8, LANES), lambda i: (0, 0)),
                  pl.BlockSpec((1, heads, HEAD_DIM, HEAD_DIM), lambda i: (n - 1 - i, 0, 0, 0)),
                  pl.BlockSpec((1, 3, heads, c, c), lambda i: (n - 1 - i, 0, 0, 0, 0)),
                  pl.BlockSpec((1, 3, heads, c, HEAD_DIM), lambda i: (n - 1 - i, 0, 0, 0, 0)),
                  pl.BlockSpec((c, da), rev)],
        out_specs=[pl.BlockSpec((c, 3 * da), rev), pl.BlockSpec((c, da), rev), pl.BlockSpec((c, LANES), rev),
                   pl.BlockSpec((8, LANES), lambda i: (0, 0))],
        out_shape=[jax.ShapeDtypeStruct((t, 3 * da), F32), jax.ShapeDtypeStruct((t, da), F32),
                   jax.ShapeDtypeStruct((t, LANES), F32), jax.ShapeDtypeStruct((8, LANES), F32)],
        scratch_shapes=[pltpu.VMEM((heads, HEAD_DIM, HEAD_DIM), F32)],
        compiler_params=_params("arbitrary"))(qkv, z, ab, gpar, states, kept_c, kept_w, doa)


def _w_in_pieces(shard_cols, da, heads):
    a0, nab = 4 * da, 2 * heads
    d_in = 4 * shard_cols
    runs = [(0, a0, 0), (a0, a0 + nab, d_in - nab), (a0 + nab, d_in, a0)]
    pieces = []
    for j in range(4):
        lo, hi = j * shard_cols, (j + 1) * shard_cols
        for rlo, rhi, plo in runs:
            s, e = max(lo, rlo), min(hi, rhi)
            if s < e:
                pieces.append((j, s - lo, e - s, plo + (s - rlo)))
    return pieces, d_in - nab + LANES


def _w_in_pack(w4, da, heads):
    _, d, sc = w4.shape
    pieces, npk = _w_in_pieces(sc, da, heads)
    tr = _tile_rows(d, 256, SUBLANES_WIRE)

    def body(w_ref, o_ref):
        o_ref[:, npk - LANES:] = jnp.zeros((tr, LANES), o_ref.dtype)
        for j, lo, ln, dst in pieces:
            o_ref[:, dst:dst + ln] = w_ref[j, :, lo:lo + ln]

    return pl.pallas_call(
        body, name="w_in_pack", grid=(d // tr,),
        in_specs=[pl.BlockSpec((4, tr, sc), lambda i: (0, i, 0))],
        out_specs=pl.BlockSpec((tr, npk), lambda i: (i, 0)),
        out_shape=jax.ShapeDtypeStruct((d, npk), w4.dtype),
        compiler_params=_params("arbitrary"))(w4)


def _w_in_unpack(dwp, sc, da, heads):
    d, npk = dwp.shape
    pieces, _ = _w_in_pieces(sc, da, heads)
    tr = _tile_rows(d, 256)

    def body(g_ref, o_ref):
        for j, lo, ln, dst in pieces:
            o_ref[j, :, lo:lo + ln] = g_ref[:, dst:dst + ln]

    return pl.pallas_call(
        body, name="w_in_unpack", grid=(d // tr,),
        in_specs=[pl.BlockSpec((tr, npk), lambda i: (i, 0))],
        out_specs=pl.BlockSpec((4, tr, sc), lambda i: (0, i, 0)),
        out_shape=jax.ShapeDtypeStruct((4, d, sc), F32),
        compiler_params=_params("arbitrary"))(dwp)


def _block_diag(pool_w):
    g, gd, _ = pool_w.shape
    out = jnp.zeros((g * gd, g * gd), pool_w.dtype)
    for gi in range(g):
        out = lax.dynamic_update_slice(out, pool_w[gi], (gi * gd, gi * gd))
    return out


def _layer_dims(d):
    heads = (d // 2) // HEAD_DIM
    return heads, heads * HEAD_DIM, d // 4, d // 4


BIG = ("w_in", "w_gate", "w_up", "ple_proj", "w_out", "w_down", "ple_gate")
TRANSPOSED = ("w_gate", "w_up")


def _prepare_layer(small, li):
    d = small["norm1_g"].shape[1]
    heads, _, _, _ = _layer_dims(d)
    gpar = jnp.zeros((8, LANES), F32)
    gpar = gpar.at[0, :heads].set(small["a_log"][li]).at[1, :heads].set(small["dt_bias"][li]).at[2, :].set(small["onorm_g"][li])
    return dict(norm1_g=small["norm1_g"][li][None], conv_qkv=small["conv_qkv"][li], gpar=gpar, pool_bd=_block_diag(small["pool_w"][li]).astype(MM_DTYPE),
                pool_scale=small["pool_scale"][li][None], sconv_w=small["sconv_w"][li], norm2_g=small["norm2_g"][li][None])


def _layer_fwd(x0, p, gw, lw, tm, arrive):
    d = x0.shape[1]
    heads, da, dp, dc = _layer_dims(d)
    segs = (3 * da, da, dp, 3 * dc, LANES)
    lw["w_in_p"] = _w_in_pack(gw["w_in"], da, heads).astype(MM_DTYPE)
    qkv_pre, z, hp, cbcch, ab = _in_proj_fwd(x0, lw["norm1_g"], lw["w_in_p"], segs, tm)
    qkv = _qkv_conv_fwd(qkv_pre, lw["conv_qkv"], heads)
    oa, states, kept_c, kept_w = _delta_fwd(qkv, z, ab, lw["gpar"], heads)
    ob = _pool_fwd(hp, lw["pool_bd"], lw["pool_scale"], dp // POOL_GROUPS)
    oc = _sconv_fwd(cbcch, lw["sconv_w"])
    arrive("mixed", oa)
    x1, h2 = _out_proj_fwd(x0, (oa, ob, oc), gw["w_out"], lw["norm2_g"], tm)
    x2, gp, up = _ffn_fwd(x1, h2, gw["w_gate"], gw["w_up"], gw["w_down"], tm)
    arrive("ffn", x2)
    x3 = _ple_fwd(x2, p, gw["ple_gate"], gw["ple_proj"], tm)
    arrive("end", x3)
    saved = dict(x0=x0, qkv_pre=qkv_pre, z=z, hp=hp, cbcch=cbcch, ab=ab, qkv=qkv, states=states, kept_c=kept_c, kept_w=kept_w, oa=oa, ob=ob, oc=oc,
                 x1=x1, h2=h2, gp=gp, up=up, x2=x2)
    return x3, saved


def _layer_bwd(dx3, p, gw, lw, sv, tm, produced):
    def after_token(tok, arr):
        return arr if tok is None else arr + tok[0, 0]

    d = dx3.shape[1]
    heads, da, dp, dc = _layer_dims(d)
    segs = (3 * da, da, dp, dc, dc, dc, LANES)
    gd = dp // POOL_GROUPS
    dx2, d_ple_gate, d_ple_proj = _ple_bwd(dx3, sv["x2"], p, gw["ple_gate"], gw["ple_proj"], tm)
    dh2, d_w_gate, d_w_up, d_w_down = _ffn_bwd(dx2, sv["h2"], sv["gp"], sv["up"], gw["w_gate"], gw["w_up"], gw["w_down"],
                                               min(tm, 256))
    tok = produced("ffn", dict(w_gate=d_w_gate, w_up=d_w_up, ple_proj=d_ple_proj, w_down=d_w_down, ple_gate=d_ple_gate), dh2)
    dx1, doa, dob, doc, d_w_out, d_norm2 = _out_proj_bwd(dx2, dh2, sv["x1"], after_token(tok, lw["norm2_g"]),
                                                         (sv["oa"], sv["ob"], sv["oc"]), gw["w_out"], tm)
    dcb, dcc, dch, d_sconv = _sconv_bwd(sv["cbcch"], lw["sconv_w"], doc)
    dhp, d_pool_bd, d_pool_scale = _pool_bwd(sv["hp"], lw["pool_bd"], lw["pool_scale"], dob, gd)
    dqkv, dz, dab, dpar = _delta_bwd(sv["qkv"], sv["z"], sv["ab"], lw["gpar"], sv["states"], sv["kept_c"], sv["kept_w"], doa,
                                      heads)
    tok = produced("mixers", {}, dqkv)
    dqkv_pre, d_conv_qkv = _qkv_conv_bwd(sv["qkv_pre"], lw["conv_qkv"], dqkv, heads)
    dsegs = (dqkv_pre, dz, dhp, dcb, dcc, dch, dab)
    dx0, d_w_in_p, d_norm1 = _in_proj_bwd(sv["x0"], after_token(tok, lw["norm1_g"]), lw["w_in_p"], dsegs, dx1, segs, tm)
    per = LANES // gd
    bd = d_pool_bd.reshape(dp // LANES, per, gd, per, gd)
    d_pool_w = jnp.stack([bd[gi // per, gi % per, :, gi % per, :] for gi in range(POOL_GROUPS)])
    big = dict(w_in=_w_in_unpack(d_w_in_p, gw["w_in"].shape[2], da, heads), w_gate=d_w_gate, w_up=d_w_up,
               ple_proj=d_ple_proj, w_out=d_w_out, w_down=d_w_down, ple_gate=d_ple_gate)
    small = dict(norm1_g=d_norm1[0], conv_qkv=d_conv_qkv, a_log=dpar[0, :heads], dt_bias=dpar[1, :heads], onorm_g=dpar[2],
                 pool_w=d_pool_w, pool_scale=d_pool_scale[0], sconv_w=d_sconv, norm2_g=d_norm2[0])
    tok = produced("end", dict(w_in=big["w_in"], w_out=d_w_out), big["w_in"])
    return dx0, big, small, tok


def _local_step(x, p, target, gw, small, produced=None, arrive=None):
    t, d = x.shape
    depth = p.shape[0]
    tm = 512 if t % 512 == 0 else 128
    layers = [_prepare_layer(small, li) for li in range(depth)]
    saved = []
    h = x
    for li in range(depth):
        h, sv = _layer_fwd(h, p[li], gw[li], layers[li], tm,
                           (lambda stage, after, li=li: arrive(li, stage, after)) if arrive else (lambda stage, after: None))
        saved.append(sv)
    dx, loss, d_final = _loss_head(h, target, small["final_g"][None], tm)
    big, sm = [None] * depth, [None] * depth
    token = None
    for li in reversed(range(depth)):
        p_li = p[li] if token is None else p[li] + token[0, 0]
        dx, big[li], sm[li], token = _layer_bwd(
            dx, p_li, gw[li], layers[li], saved[li], tm,
            (lambda stage, grads, after, li=li: produced(li, stage, grads, after)) if produced else (lambda *a: None))
    small_grads = {n: jnp.stack([g[n] for g in sm]) for n in sm[0]}
    small_grads["final_g"] = d_final[0]
    return loss[0, 0], dx, big, small_grads


def _coords():
    return lax.axis_index("x"), lax.axis_index("y"), lax.axis_index("c")


def _other_chips(x, y):
    return [(1 - x, y), (x, 1 - y), (1 - x, 1 - y)]


def _place_shards(ws, me_idx):
    nt = len(ws)
    depth = ws[0].shape[0]

    def body(me_ref, *refs):
        for t, w_ref in enumerate(refs[:nt]):
            for li in range(depth):
                refs[nt + li * nt + t][...] = w_ref[li].astype(WIRE_DTYPE)

    outs = pl.pallas_call(
        body, name="place_shards",
        grid_spec=pltpu.PrefetchScalarGridSpec(
            num_scalar_prefetch=1, grid=(4,),
            in_specs=[pl.BlockSpec((depth, w.shape[1] // 4, w.shape[2]), lambda i, me_ref: (0, i, 0)) for w in ws],
            out_specs=[pl.BlockSpec((None, w.shape[1] // 4, w.shape[2]), lambda i, me_ref: (me_ref[0], i, 0))
                       for _ in range(depth) for w in ws]),
        out_shape=[jax.ShapeDtypeStruct((4,) + w.shape[1:], WIRE_DTYPE) for _ in range(depth) for w in ws],
        compiler_params=_params("arbitrary"))(me_idx, *ws)
    return [list(outs[li * nt:(li + 1) * nt]) for li in range(depth)]


def _half_block(ref, chip, pc):
    rh = ref.shape[1] // 2
    return ref.at[chip, pl.ds(pc * rh, rh)]


def _gather_copies(out_refs, send_sems, recv_sems, stage):
    nt = len(out_refs)
    x, y, c = _coords()
    pairs = []
    for j, (cx, cy) in enumerate(_other_chips(x, y)):
        for t in range(nt):
            sems = dict(send_sem=send_sems[j * nt + t], recv_sem=recv_sems[j * nt + t], device_id_type=MESH)
            if stage == 0:
                mine, theirs, to = _half_block(out_refs[t], 2 * x + y, c), _half_block(out_refs[t], 2 * cx + cy, c), (cx, cy, c)
            else:
                mine, theirs, to = (_half_block(out_refs[t], 2 * cx + cy, c), _half_block(out_refs[t], 2 * cx + cy, 1 - c),
                                    (x, y, 1 - c))
            pairs.append((pltpu.make_async_remote_copy(src_ref=mine, dst_ref=mine, device_id=to, **sems),
                          pltpu.make_async_remote_copy(src_ref=theirs, dst_ref=theirs, device_id=to, **sems)))
    return pairs


def _all_gather_chips(placed):
    nt = len(placed)

    def body(*refs):
        out_refs = refs[nt:2 * nt]
        send_sems, recv_sems = refs[2 * nt:]
        nc = 3 * nt
        first = _gather_copies(out_refs, [send_sems.at[k] for k in range(nc)], [recv_sems.at[k] for k in range(nc)], 0)
        passed = _gather_copies(out_refs, [send_sems.at[nc + k] for k in range(nc)], [recv_sems.at[nc + k] for k in range(nc)], 1)
        for start, _ in first:
            start.start()
        for (_, arrival), (forward, _) in zip(first, passed):
            arrival.wait_recv()
            forward.start()
        for _, arrival in passed:
            arrival.wait_recv()
        for start, _ in first + passed:
            start.wait_send()

    return pl.pallas_call(
        body, name="all_gather_chips", out_shape=[jax.ShapeDtypeStruct(a.shape, a.dtype) for a in placed],
        in_specs=[ANY] * nt, out_specs=[ANY] * nt, input_output_aliases={t: t for t in range(nt)},
        scratch_shapes=[pltpu.SemaphoreType.DMA((6 * nt,)), pltpu.SemaphoreType.DMA((6 * nt,))],
    )(*placed)


def _gather_call(name, arrs, wait_sems, after, stage):
    nt = len(arrs)
    nc = 3 * nt
    n_wait = len(wait_sems)
    n_new = 2 * nc if stage < 2 else 0
    arrs = [pltpu.with_memory_space_constraint(a, pltpu.HBM) for a in arrs]

    def body(*refs):
        a_refs = refs[:nt]
        waits = refs[nt:nt + n_wait]
        news = refs[nt + n_wait + 1:nt + n_wait + 1 + n_new]
        token = refs[-1]
        if stage > 0:
            for start, arrival in _gather_copies(a_refs, waits[:nc], waits[nc:], stage - 1):
                start.wait_send()
                arrival.wait_recv()
        if stage < 2:
            for start, _ in _gather_copies(a_refs, news[:nc], news[nc:], stage):
                start.start()
        token[...] = jnp.zeros_like(token)

    outs = pl.pallas_call(
        body, name=name,
        out_shape=(*[pltpu.SemaphoreType.DMA(())] * n_new, *[pltpu.HBM(a.shape, a.dtype) for a in arrs],
                   jax.ShapeDtypeStruct((8, LANES), F32)),
        in_specs=[HBM] * nt + [SEM] * n_wait + [ANY],
        out_specs=(*[SEM] * n_new, *[HBM] * nt, pl.BlockSpec(memory_space=pltpu.VMEM)),
        input_output_aliases={t: n_new + t for t in range(nt)},
        compiler_params=pltpu.CompilerParams(has_side_effects=pltpu.SideEffectType.DATAFLOW_SIDE_EFFECTING),
    )(*arrs, *wait_sems, after)
    return list(outs[:n_new]), list(outs[n_new:n_new + nt]), outs[-1]


def _sibling_swap_half(gs):
    nt = len(gs)

    def body(*refs):
        g_refs, out_refs = refs[:nt], refs[nt:2 * nt]
        send_sems, recv_sems = refs[2 * nt:]
        x, y, c = _coords()
        cps = []
        for t in range(nt):
            rh = g_refs[t].shape[1] // 2
            cps.append(pltpu.make_async_remote_copy(src_ref=g_refs[t].at[:, pl.ds((1 - c) * rh, rh)], dst_ref=out_refs[t],
                                                    send_sem=send_sems.at[t], recv_sem=recv_sems.at[t], device_id=(x, y, 1 - c),
                                                    device_id_type=MESH))
        for cp in cps:
            cp.start()
        for cp in cps:
            cp.wait()

    return pl.pallas_call(
        body, name="sibling_swap_half",
        out_shape=[jax.ShapeDtypeStruct((g.shape[0], g.shape[1] // 2, g.shape[2]), g.dtype) for g in gs],
        in_specs=[ANY] * nt, out_specs=[ANY] * nt,
        scratch_shapes=[pltpu.SemaphoreType.DMA((nt,)), pltpu.SemaphoreType.DMA((nt,))])(*gs)


def _add_my_halves(gs, others, c_idx):
    nt = len(gs)

    def body(c_ref, *refs):
        for g_ref, o_ref, out_ref in zip(refs[:nt], refs[nt:2 * nt], refs[2 * nt:]):
            out_ref[...] = (g_ref[...].astype(F32) + o_ref[...].astype(F32)).astype(out_ref.dtype)

    def quarter(g):
        return pl.BlockSpec((None, g.shape[1] // 4, g.shape[2]), lambda j, i, c_ref: (j, i, 0))

    return pl.pallas_call(
        body, name="add_my_halves",
        grid_spec=pltpu.PrefetchScalarGridSpec(
            num_scalar_prefetch=1, grid=(4, 2),
            in_specs=[pl.BlockSpec((None, g.shape[1] // 4, g.shape[2]), lambda j, i, c_ref: (j, 2 * c_ref[0] + i, 0)) for g in gs]
                     + [quarter(g) for g in gs],
            out_specs=[quarter(g) for g in gs]),
        out_shape=[jax.ShapeDtypeStruct((4, g.shape[1] // 2, g.shape[2]), WIRE_DTYPE) for g in gs],
        compiler_params=_params("arbitrary", "arbitrary"))(c_idx, *gs, *others)


def _exchange_chips(parts):
    nt = len(parts)

    def body(*refs):
        p_refs, out_refs = refs[:nt], refs[nt:2 * nt]
        send_sems, recv_sems = refs[2 * nt:]
        x, y, c = _coords()
        chips = _other_chips(x, y)

        def copy(j, t):
            cx, cy = chips[j]
            return pltpu.make_async_remote_copy(src_ref=p_refs[t].at[2 * cx + cy], dst_ref=out_refs[t].at[j],
                                                send_sem=send_sems.at[j, t], recv_sem=recv_sems.at[j, t], device_id=(cx, cy, c),
                                                device_id_type=MESH)

        sends = [copy(j, t) for j in range(3) for t in range(nt)]
        for cp in sends:
            cp.start()
        for cp in sends:
            cp.wait_recv()
        for cp in sends:
            cp.wait_send()

    return pl.pallas_call(
        body, name="exchange_chips", out_shape=[jax.ShapeDtypeStruct((3,) + p.shape[1:], p.dtype) for p in parts],
        in_specs=[ANY] * nt, out_specs=[ANY] * nt,
        scratch_shapes=[pltpu.SemaphoreType.DMA((3, nt)), pltpu.SemaphoreType.DMA((3, nt))])(*parts)


def _split_plan(kind, s_refs, l_refs):
    x, y, c = _coords()
    if kind == "swap":
        return [(s.at[:, pl.ds((1 - c) * (s.shape[1] // 2), s.shape[1] // 2)], l, (x, y, 1 - c)) for s, l in zip(s_refs, l_refs)]
    return [(s.at[2 * cx + cy], l.at[j], (cx, cy, c)) for j, (cx, cy) in enumerate(_other_chips(x, y))
            for s, l in zip(s_refs, l_refs)]


def _split_landing(kind, a):
    return (a.shape[0], a.shape[1] // 2, a.shape[2]) if kind == "swap" else (3,) + a.shape[1:]


def _copies_start(name, kind, srcs, after=None):
    ns = len(srcs)
    n = ns if kind == "swap" else 3 * ns
    srcs = [pltpu.with_memory_space_constraint(a, pltpu.HBM) for a in srcs]
    lands = [pltpu.with_memory_space_constraint(lax.empty(_split_landing(kind, a), a.dtype), pltpu.HBM) for a in srcs]
    extra = [] if after is None else [after]

    def body(*refs):
        first_sem = 2 * ns + len(extra)
        sems, token = refs[first_sem:first_sem + 2 * n], refs[-1]
        for k, (src, dst, dev) in enumerate(_split_plan(kind, refs[:ns], refs[ns:2 * ns])):
            pltpu.make_async_remote_copy(src_ref=src, dst_ref=dst, send_sem=sems[k], recv_sem=sems[n + k], device_id=dev,
                                         device_id_type=MESH).start()
        token[...] = jnp.zeros_like(token)

    outs = pl.pallas_call(
        body, name=name,
        out_shape=(*[pltpu.SemaphoreType.DMA(())] * (2 * n), *[pltpu.HBM(a.shape, a.dtype) for a in srcs + lands],
                   jax.ShapeDtypeStruct((8, LANES), F32)),
        in_specs=[HBM] * (2 * ns) + [ANY] * len(extra),
        out_specs=(*[SEM] * (2 * n), *[HBM] * (2 * ns), pl.BlockSpec(memory_space=pltpu.VMEM)),
        input_output_aliases={t: 2 * n + t for t in range(2 * ns)},
        compiler_params=pltpu.CompilerParams(has_side_effects=pltpu.SideEffectType.DATAFLOW_SIDE_EFFECTING),
    )(*srcs, *lands, *extra)
    return list(outs[:2 * n]), list(outs[2 * n:2 * n + ns]), list(outs[2 * n + ns:2 * n + 2 * ns]), outs[-1]


def _copies_wait(name, kind, sems, srcs, lands, after):
    ns = len(srcs)
    n = len(sems) // 2

    def body(*refs):
        sem_refs = refs[2 * ns:2 * ns + 2 * n]
        for k, (src, dst, dev) in enumerate(_split_plan(kind, refs[:ns], refs[ns:2 * ns])):
            cp = pltpu.make_async_remote_copy(src_ref=src, dst_ref=dst, send_sem=sem_refs[k], recv_sem=sem_refs[n + k],
                                              device_id=dev, device_id_type=MESH)
            cp.wait_send()
            cp.wait_recv()

    outs = pl.pallas_call(
        body, name=name, out_shape=tuple(pltpu.HBM(a.shape, a.dtype) for a in srcs + lands),
        in_specs=[HBM] * (2 * ns) + [SEM] * (2 * n) + [ANY], out_specs=tuple([HBM] * (2 * ns)),
        input_output_aliases={t: t for t in range(2 * ns)},
        compiler_params=pltpu.CompilerParams(has_side_effects=pltpu.SideEffectType.DATAFLOW_SIDE_EFFECTING),
    )(*srcs, *lands, *sems, after)
    return list(outs[:ns]), list(outs[ns:])


def _sum_into(pairs, recvs, idx, li, depth, accs):
    nt = len(pairs)

    def body(idx_ref, *refs):
        for p_ref, r_ref, out_ref in zip(refs[:nt], refs[nt:2 * nt], refs[-nt:]):
            out_ref[...] = p_ref[...].astype(F32) + r_ref[0].astype(F32) + r_ref[1].astype(F32) + r_ref[2].astype(F32)

    in_specs = ([pl.BlockSpec((None, p.shape[1] // 2, p.shape[2]), lambda i, idx_ref: (idx_ref[0], i, 0)) for p in pairs]
                + [pl.BlockSpec((3, p.shape[1] // 2, p.shape[2]), lambda i, idx_ref: (0, i, 0)) for p in pairs])
    args = [idx, *pairs, *recvs]
    aliases = {}
    if accs[0] is not None:
        in_specs += [ANY] * nt
        args += list(accs)
        aliases = {1 + 2 * nt + t: t for t in range(nt)}
    return pl.pallas_call(
        body, name="sum_into",
        grid_spec=pltpu.PrefetchScalarGridSpec(
            num_scalar_prefetch=1, grid=(2,), in_specs=in_specs,
            out_specs=[pl.BlockSpec((None, p.shape[1] // 2, p.shape[2]), lambda i, idx_ref: (li, 2 * idx_ref[1] + i, 0))
                       for p in pairs]),
        out_shape=[jax.ShapeDtypeStruct((depth, 2 * p.shape[1], p.shape[2]), F32) for p in pairs],
        input_output_aliases=aliases, compiler_params=_params("arbitrary"))(*args)


def _sum_slots(parts):
    n, rows, cols = parts.shape
    tr = _tile_rows(rows, 512, SUBLANES_WIRE)

    def body(p_ref, out_ref):
        acc = p_ref[0].astype(F32)
        for s in range(1, n):
            acc = acc + p_ref[s].astype(F32)
        out_ref[...] = acc

    return pl.pallas_call(
        body, name="sum_slots", grid=(rows // tr,),
        in_specs=[pl.BlockSpec((n, tr, cols), lambda i: (0, i, 0))],
        out_specs=pl.BlockSpec((tr, cols), lambda i: (i, 0)),
        out_shape=jax.ShapeDtypeStruct((rows, cols), F32),
        compiler_params=_params("arbitrary"))(parts)


def _sibling_share(gs, li):
    nt = len(gs)

    def body(*refs):
        out_refs = refs[nt:2 * nt]
        send_sems, recv_sems = refs[2 * nt:]
        x, y, c = _coords()
        sends, recvs = [], []
        for t in range(nt):
            rh = out_refs[t].shape[1] // 2
            mine, theirs = out_refs[t].at[li, pl.ds(c * rh, rh)], out_refs[t].at[li, pl.ds((1 - c) * rh, rh)]
            sems = dict(send_sem=send_sems.at[t], recv_sem=recv_sems.at[t], device_id=(x, y, 1 - c), device_id_type=MESH)
            sends.append(pltpu.make_async_remote_copy(src_ref=mine, dst_ref=mine, **sems))
            recvs.append(pltpu.make_async_remote_copy(src_ref=theirs, dst_ref=theirs, **sems))
        for cp in sends:
            cp.start()
        for cp in recvs:
            cp.wait_recv()
        for cp in sends:
            cp.wait_send()

    return pl.pallas_call(
        body, name="sibling_share", out_shape=[jax.ShapeDtypeStruct(g.shape, g.dtype) for g in gs],
        in_specs=[ANY] * nt, out_specs=[ANY] * nt, input_output_aliases={t: t for t in range(nt)},
        scratch_shapes=[pltpu.SemaphoreType.DMA((nt,)), pltpu.SemaphoreType.DMA((nt,))])(*gs)


def _all_gather_devices(buf, after=None):
    extra = [] if after is None else [after]

    def body(b_ref, *rest):
        out_ref, send_sems, recv_sems, local_sem = rest[len(extra):]
        x, y, c = _coords()
        me = 4 * x + 2 * y + c
        mine = pltpu.make_async_copy(b_ref, out_ref.at[me], local_sem)
        mine.start()
        peers = []
        for k in range(1, 8):
            fx, fy, fc = (k >> 2) & 1, (k >> 1) & 1, k & 1
            peers.append((x ^ fx, y ^ fy, c ^ fc))
        sends = [pltpu.make_async_remote_copy(src_ref=b_ref, dst_ref=out_ref.at[me], send_sem=send_sems.at[k],
                                              recv_sem=recv_sems.at[k], device_id=peer, device_id_type=MESH)
                 for k, peer in enumerate(peers)]
        for cp in sends:
            cp.start()
        for k, (px, py, pc) in enumerate(peers):
            pltpu.make_async_remote_copy(src_ref=b_ref, dst_ref=out_ref.at[4 * px + 2 * py + pc], send_sem=send_sems.at[k],
                                         recv_sem=recv_sems.at[k], device_id=(px, py, pc), device_id_type=MESH).wait_recv()
        for cp in sends:
            cp.wait_send()
        mine.wait()

    return pl.pallas_call(
        body, name="all_gather_devices", out_shape=jax.ShapeDtypeStruct((8,) + buf.shape, buf.dtype),
        in_specs=[ANY] * (1 + len(extra)), out_specs=ANY,
        scratch_shapes=[pltpu.SemaphoreType.DMA((7,)), pltpu.SemaphoreType.DMA((7,)), pltpu.SemaphoreType.DMA(())])(buf, *extra)


def _pair_sums(big_grads, c_idx):
    gs = [big_grads[n] for n in BIG]
    return _add_my_halves(gs, _sibling_swap_half(gs), c_idx)


SMALL_SHARDED = ("conv_qkv", "sconv_w")
REPLICATED = ("norm1_g", "a_log", "dt_bias", "onorm_g", "pool_w", "pool_scale", "norm2_g", "final_g")
ALL_WEIGHTS = ("norm1_g", "w_in", "conv_qkv", "a_log", "dt_bias", "onorm_g", "pool_w", "pool_scale", "sconv_w", "w_out",
               "norm2_g", "w_gate", "w_up", "w_down", "ple_proj", "ple_gate", "final_g")


def _pad_rows(flat, row_multiple):
    m = flat.shape[0]
    r = -(-m // (LANES * row_multiple)) * row_multiple
    return jnp.pad(flat, (0, r * LANES - m)).reshape(r, LANES)


def _adamw_math(w, g, m, v):
    c1 = 1.0 / (1.0 - ADAM_B1 ** ADAM_STEP)
    c2 = 1.0 / (1.0 - ADAM_B2 ** ADAM_STEP)
    nm = ADAM_B1 * m + (1.0 - ADAM_B1) * g
    nv = ADAM_B2 * v + (1.0 - ADAM_B2) * (g * g)
    return -ADAM_LR * ((nm * c1) / (jnp.sqrt(nv * c2) + ADAM_EPS) + ADAM_WD * w), nm, nv


def _adamw(w, g, m, v):
    shape = w.shape
    cols = shape[-1]
    rows = w.size // cols
    tr = _tile_rows(rows, 512)

    def body(w_ref, g_ref, m_ref, v_ref, d_ref, nm_ref, nv_ref, go_ref):
        gv = g_ref[...]
        d_ref[...], nm_ref[...], nv_ref[...] = _adamw_math(w_ref[...], gv, m_ref[...], v_ref[...])
        go_ref[...] = gv

    spec = pl.BlockSpec((tr, cols), lambda i: (i, 0))
    outs = pl.pallas_call(
        body, name="adamw", grid=(rows // tr,), in_specs=[spec] * 4, out_specs=[spec] * 4,
        out_shape=[jax.ShapeDtypeStruct((rows, cols), F32)] * 4,
        compiler_params=_params("arbitrary"))(*[a.reshape(rows, cols) for a in (w, g, m, v)])
    return tuple(o.reshape(shape) for o in outs)


def kernel(x, p, norm1_g, w_in, conv_qkv, a_log, dt_bias, onorm_g, pool_w, pool_scale, sconv_w, w_out, norm2_g, w_gate, w_up, w_down, ple_proj, ple_gate, final_g, loss_target, m_norm1_g, m_w_in, m_conv_qkv, m_a_log, m_dt_bias, m_onorm_g, m_pool_w, m_pool_scale, m_sconv_w, m_w_out, m_norm2_g, m_w_gate, m_w_up, m_w_down, m_ple_proj, m_ple_gate, m_final_g, v_norm1_g, v_w_in, v_conv_qkv, v_a_log, v_dt_bias, v_onorm_g, v_pool_w, v_pool_scale, v_sconv_w, v_w_out, v_norm2_g, v_w_gate, v_w_up, v_w_down, v_ple_proj, v_ple_gate, v_final_g):
    weights = dict(zip(ALL_WEIGHTS, (norm1_g, w_in, conv_qkv, a_log, dt_bias, onorm_g, pool_w, pool_scale, sconv_w, w_out,
                                     norm2_g, w_gate, w_up, w_down, ple_proj, ple_gate, final_g)))
    mom_m = dict(zip(ALL_WEIGHTS, (m_norm1_g, m_w_in, m_conv_qkv, m_a_log, m_dt_bias, m_onorm_g, m_pool_w, m_pool_scale,
                                   m_sconv_w, m_w_out, m_norm2_g, m_w_gate, m_w_up, m_w_down, m_ple_proj, m_ple_gate, m_final_g)))
    mom_v = dict(zip(ALL_WEIGHTS, (v_norm1_g, v_w_in, v_conv_qkv, v_a_log, v_dt_bias, v_onorm_g, v_pool_w, v_pool_scale,
                                   v_sconv_w, v_w_out, v_norm2_g, v_w_gate, v_w_up, v_w_down, v_ple_proj, v_ple_gate, v_final_g)))
    for n in TRANSPOSED:
        weights[n], mom_m[n], mom_v[n] = (jnp.swapaxes(a[n], 1, 2) for a in (weights, mom_m, mom_v))
    c_idx = lax.axis_index("c").astype(jnp.int32).reshape(1)
    chip = (2 * lax.axis_index("x") + lax.axis_index("y")).astype(jnp.int32)
    me_idx = chip.reshape(1)
    idx = jnp.stack([chip, lax.axis_index("c").astype(jnp.int32)])
    depth = p.shape[0]

    small = {n: weights[n] for n in REPLICATED}
    sflat = _pad_rows(jnp.concatenate([weights[n].reshape(-1) for n in SMALL_SHARDED]), 8)
    sgath8 = _all_gather_devices(sflat)
    placed_in = _place_shards([weights["w_in"]], me_idx)
    sems, arrs, _ = _gather_call("gather_first_start", placed_in[0], [], sgath8, 0)
    placed_rest = _place_shards([weights[n] for n in BIG[1:]], me_idx)
    placed = [placed_in[li] + placed_rest[li] for li in range(depth)]
    sems, arrs, _ = _gather_call("gather_first_forward", arrs, sems, placed_rest[0][0], 1)
    _, arrs, token = _gather_call("gather_first_finish", arrs, sems, placed_rest[0][0], 2)
    gw = [dict() for _ in range(depth)]
    gw[0]["w_in"] = arrs[0]
    early = ("w_in", "w_out")
    late = tuple(n for n in BIG if n not in early)
    groups = [dict(li=0, names=BIG[1:], forward=(0, "mixed"), finish=(0, "mixed"))]
    for li in range(1, depth):
        groups.append(dict(li=li, names=early, forward=(li - 1, "ffn"), finish=(li - 1, "end")))
        groups.append(dict(li=li, names=late, forward=(li, "mixed"), finish=(li, "mixed")))
    def arrive(li, stage, after):
        for k, g in enumerate(groups):
            if g["forward"] == (li, stage):
                g["sems"], g["arrs"], _ = _gather_call("gather_forward_%d" % k, g["arrs"], g["sems"], after, 1)
            if g["finish"] == (li, stage):
                _, g["arrs"], _ = _gather_call("gather_finish_%d" % k, g["arrs"], g["sems"], after, 2)
                gw[g["li"]].update(zip(g["names"], g["arrs"]))

    sgath = sgath8[0::2].reshape(4, -1)
    off = 0
    for n in SMALL_SHARDED:
        shp = weights[n].shape
        part = sgath[:, off:off + weights[n].size].reshape((4,) + shp)
        small[n] = jnp.moveaxis(part, 0, -2).reshape(shp[:-1] + (4 * shp[-1],))
        off += weights[n].size
    for k, g in enumerate(groups):
        arrs = [placed[g["li"]][BIG.index(n)] for n in g["names"]]
        g["sems"], g["arrs"], token = _gather_call("gather_start_%d" % k, arrs, [], token, 0)

    small["norm1_g"] = small["norm1_g"] + token[0, 0]

    pending = []
    last_token = [None]

    def advance(g, after):
        if g["stage"] == 0:
            gs, others = _copies_wait("swap_wait_" + g["tag"], "swap", *g["handle"], after)
            g["handle"] = _copies_start("exchange_start_" + g["tag"], "exchange", _add_my_halves(gs, others, c_idx))
            g["stage"] = 1
            return g["handle"][3]
        return None

    held = {}

    def produced(li, stage, grads, after):
        token = None
        for g in pending:
            token = advance(g, after) if g["stage"] == 0 else token
        if li > 0 and stage != "end":
            held.update(grads)
            grads = {}
        elif li > 0:
            grads = {**held, **grads}
            held.clear()
        if grads:
            names = [n for n in BIG if n in grads]
            handle = _copies_start("swap_start_%d%s" % (li, stage), "swap", [grads[n] for n in names], token)
            pending.append(dict(li=li, names=names, tag="%d%s" % (li, stage), stage=0, handle=handle[:3]))
            token = handle[3]
        last_token[0] = last_token[0] if token is None else token
        return token

    loss_local, dx, _, small_grads = _local_step(x[0], p[:, 0], loss_target[0], gw, small, produced, arrive)
    accs, big_outs = {}, {}

    def finish(g, after):
        pairs, recvs = _copies_wait("exchange_wait_" + g["tag"], "exchange", *g["handle"][:3], after)
        summed = _sum_into(pairs, recvs, idx, g["li"], depth, [accs.get(n) for n in g["names"]])
        accs.update(zip(g["names"], _sibling_share(summed, g["li"])))
        return accs[g["names"][-1]]

    def update(names):
        for n in names:
            big_outs[n] = _adamw(weights[n], accs[n], mom_m[n], mom_v[n])
        return jnp.stack([big_outs[n][0].reshape(-1)[0] for n in names])

    done = finish(pending[0], last_token[0])
    done = advance(pending[-1], done)
    for g in pending[1:-1]:
        done = finish(g, done)
    last = pending[-1]["names"]
    done = update([n for n in BIG if n not in last])
    finish(pending[-1], done)
    update(last)


    gshard = {}
    rnames = REPLICATED + SMALL_SHARDED
    rflat = _pad_rows(jnp.concatenate([small_grads[n].reshape(-1) for n in rnames]), 8)
    rsum = _sum_slots(_all_gather_devices(rflat)).reshape(-1)
    off = 0
    for n in rnames:
        whole = rsum[off:off + small_grads[n].size].reshape(small_grads[n].shape)
        off += small_grads[n].size
        if n in SMALL_SHARDED:
            cols = weights[n].shape[-1]
            whole = lax.dynamic_slice_in_dim(whole, chip * cols, cols, axis=whole.ndim - 1)
        gshard[n] = whole

    loss = lax.psum(loss_local, ("x", "y", "c"))
    deltas, new_m, new_v, grad_out = {}, {}, {}, {}
    for n in ALL_WEIGHTS:
        if n in BIG:
            deltas[n], new_m[n], new_v[n], grad_out[n] = big_outs[n]
        else:
            deltas[n], new_m[n], new_v[n], grad_out[n] = _adamw(weights[n], gshard[n], mom_m[n], mom_v[n])
    for n in TRANSPOSED:
        deltas[n], new_m[n], new_v[n], grad_out[n] = (jnp.swapaxes(a[n], 1, 2) for a in (deltas, new_m, new_v, grad_out))
    return (loss, dx[None], *[grad_out[n] for n in ALL_WEIGHTS], *[deltas[n] for n in ALL_WEIGHTS],
            *[new_m[n] for n in ALL_WEIGHTS], *[new_v[n] for n in ALL_WEIGHTS])
```

```python
import jax
import jax.numpy as jnp
from jax import lax
from jax.experimental import pallas as pl
from jax.experimental.pallas import tpu as pltpu

F32 = jnp.float32
MM_DTYPE = jnp.bfloat16
WIRE_DTYPE = jnp.bfloat16
HI = lax.Precision.HIGHEST
EPS = 1e-6
HEAD_DIM = 128
CHUNK = 64
QKV_CONV_WIDTH = 4
SCONV_WIDTH = 3
POOL_GROUPS = 4
LANES = 128
SUBLANES_WIRE = 16
VMEM_LIMIT_BYTES = 56 * 1024 * 1024
ADAM_LR, ADAM_B1, ADAM_B2, ADAM_EPS, ADAM_WD, ADAM_STEP = 0.001, 0.9, 0.999, 1e-08, 0.01, 10
MESH = pl.DeviceIdType.MESH
ANY = pl.BlockSpec(memory_space=pl.ANY)
HBM = pl.BlockSpec(memory_space=pltpu.HBM)
SEM = pl.BlockSpec(memory_space=pltpu.SEMAPHORE)


def _params(*sem):
    return pltpu.CompilerParams(vmem_limit_bytes=VMEM_LIMIT_BYTES, dimension_semantics=sem if sem else None)


def _mm(a, b):
    return jnp.dot(a.astype(MM_DTYPE), b.astype(MM_DTYPE), preferred_element_type=F32)


def _mm_nt(a, b):
    return lax.dot_general(a.astype(MM_DTYPE), b.astype(MM_DTYPE), (((1,), (1,)), ((), ())), preferred_element_type=F32)


def _mm_tn(a, b):
    return lax.dot_general(a.astype(MM_DTYPE), b.astype(MM_DTYPE), (((0,), (0,)), ((), ())), preferred_element_type=F32)


def _hmm(a, b):
    return jnp.dot(a, b, preferred_element_type=F32, precision=HI)


def _hmm_nt(a, b):
    return lax.dot_general(a, b, (((1,), (1,)), ((), ())), preferred_element_type=F32, precision=HI)


def _hmm_tn(a, b):
    return lax.dot_general(a, b, (((0,), (0,)), ((), ())), preferred_element_type=F32, precision=HI)


def _sigmoid(x):
    return 1.0 / (1.0 + jnp.exp(-x))


def _dsilu(x, s):
    return s * (1.0 + x * (1.0 - s))


def _rows(shape):
    return lax.broadcasted_iota(jnp.int32, shape, 0)


def _shift_down(x, s):
    if s == 0:
        return x
    return jnp.where(_rows(x.shape) >= s, pltpu.roll(x, s, 0), 0.0)


def _shift_up(x, s):
    if s == 0:
        return x
    t = x.shape[0]
    return jnp.where(_rows(x.shape) < t - s, pltpu.roll(x, t - s, 0), 0.0)


def _rms_fwd(x):
    r = lax.rsqrt(jnp.mean(x * x, axis=-1, keepdims=True) + EPS)
    return x * r, r


def _rms_bwd(dxn, xn, r):
    return r * (dxn - xn * jnp.mean(dxn * xn, axis=-1, keepdims=True))


def _tile_rows(n, cap, mult=8):
    best = None
    for d in range(mult, min(n, cap) + 1, mult):
        if n % d == 0:
            best = d
    return best if best is not None else n


def _in_proj_fwd(x, g1, wp, segs, tm):
    t, d = x.shape
    npk = wp.shape[1]

    def body(x_ref, g_ref, w_ref, *o_refs):
        xn, _ = _rms_fwd(x_ref[...])
        h = (xn * g_ref[...]).astype(w_ref.dtype)
        off = 0
        for o_ref, wd in zip(o_refs, segs):
            o_ref[...] = jnp.dot(h, w_ref[:, off:off + wd], preferred_element_type=F32)
            off += wd

    return pl.pallas_call(
        body, name="in_proj_fwd", grid=(t // tm,),
        in_specs=[pl.BlockSpec((tm, d), lambda i: (i, 0)), pl.BlockSpec((1, d), lambda i: (0, 0)),
                  pl.BlockSpec((d, npk), lambda i: (0, 0))],
        out_specs=[pl.BlockSpec((tm, wd), lambda i: (i, 0)) for wd in segs],
        out_shape=[jax.ShapeDtypeStruct((t, wd), F32) for wd in segs],
        compiler_params=_params("arbitrary"))(x, g1, wp)


def _in_proj_bwd(x, g1, wp, dsegs, dx_res, segs, tm):
    t, d = x.shape
    npk = wp.shape[1]
    nseg = len(segs)

    def body(x_ref, g_ref, w_ref, *rest):
        ds_refs = rest[:nseg]
        dxr_ref, dx_ref, dw_ref, dg_ref = rest[nseg:]
        i = pl.program_id(0)

        @pl.when(i == 0)
        def _():
            dw_ref[...] = jnp.zeros_like(dw_ref)
            dg_ref[...] = jnp.zeros_like(dg_ref)

        xn, r = _rms_fwd(x_ref[...])
        g = g_ref[...]
        h = (xn * g).astype(w_ref.dtype)
        dh = jnp.zeros((tm, d), F32)
        off = 0
        for ds_ref, wd in zip(ds_refs, segs):
            dsv = ds_ref[...].astype(w_ref.dtype)
            dh = dh + lax.dot_general(dsv, w_ref[:, off:off + wd], (((1,), (1,)), ((), ())), preferred_element_type=F32)
            dw_ref[:, off:off + wd] += lax.dot_general(h, dsv, (((0,), (0,)), ((), ())), preferred_element_type=F32)
            off += wd
        dg_ref[...] += jnp.sum(dh * xn, axis=0, keepdims=True)
        dx_ref[...] = dxr_ref[...] + _rms_bwd(dh * g, xn, r)

    return pl.pallas_call(
        body, name="in_proj_bwd", grid=(t // tm,),
        in_specs=[pl.BlockSpec((tm, d), lambda i: (i, 0)), pl.BlockSpec((1, d), lambda i: (0, 0)),
                  pl.BlockSpec((d, npk), lambda i: (0, 0))]
                 + [pl.BlockSpec((tm, wd), lambda i: (i, 0)) for wd in segs]
                 + [pl.BlockSpec((tm, d), lambda i: (i, 0))],
        out_specs=[pl.BlockSpec((tm, d), lambda i: (i, 0)), pl.BlockSpec((d, npk), lambda i: (0, 0)),
                   pl.BlockSpec((1, d), lambda i: (0, 0))],
        out_shape=[jax.ShapeDtypeStruct((t, d), F32), jax.ShapeDtypeStruct((d, npk), F32),
                   jax.ShapeDtypeStruct((1, d), F32)],
        compiler_params=_params("arbitrary"))(x, g1, wp, *dsegs, dx_res)


def _out_proj_fwd(x0, mix, wo, g2, tm):
    t, d = x0.shape
    dq = wo.shape[1]
    widths = [m.shape[1] for m in mix]

    def body(x_ref, *rest):
        m_refs = rest[:len(mix)]
        w_ref, g_ref, x1_ref, h2_ref = rest[len(mix):]
        acc = x_ref[...]
        off = 0
        for m_ref, wd in zip(m_refs, widths):
            for k in range(wd // dq):
                acc = acc + jnp.dot(m_ref[:, k * dq:(k + 1) * dq].astype(w_ref.dtype), w_ref[off // dq + k],
                                    preferred_element_type=F32)
            off += wd
        x1_ref[...] = acc
        xn, _ = _rms_fwd(acc)
        h2_ref[...] = (xn * g_ref[...]).astype(h2_ref.dtype)

    return pl.pallas_call(
        body, name="out_proj_fwd", grid=(t // tm,),
        in_specs=[pl.BlockSpec((tm, d), lambda i: (i, 0))]
                 + [pl.BlockSpec((tm, wd), lambda i: (i, 0)) for wd in widths]
                 + [pl.BlockSpec((4, dq, d), lambda i: (0, 0, 0)), pl.BlockSpec((1, d), lambda i: (0, 0))],
        out_specs=[pl.BlockSpec((tm, d), lambda i: (i, 0)), pl.BlockSpec((tm, d), lambda i: (i, 0))],
        out_shape=[jax.ShapeDtypeStruct((t, d), F32), jax.ShapeDtypeStruct((t, d), MM_DTYPE)],
        compiler_params=_params("arbitrary"))(x0, *mix, wo, g2)


def _out_proj_bwd(dx2, dh2, x1, g2, mix, wo, tm):
    t, d = x1.shape
    dq = wo.shape[1]
    widths = [m.shape[1] for m in mix]
    nm = len(mix)

    def body(dx2_ref, dh2_ref, x1_ref, g_ref, *rest):
        m_refs = rest[:nm]
        w_ref = rest[nm]
        dx1_ref = rest[nm + 1]
        dm_refs = rest[nm + 2:nm + 2 + nm]
        dw_ref, dg_ref = rest[nm + 2 + nm:]
        i = pl.program_id(0)

        @pl.when(i == 0)
        def _():
            dw_ref[...] = jnp.zeros_like(dw_ref)
            dg_ref[...] = jnp.zeros_like(dg_ref)

        xn, r = _rms_fwd(x1_ref[...])
        dh2v = dh2_ref[...]
        dg_ref[...] += jnp.sum(dh2v * xn, axis=0, keepdims=True)
        dx1 = dx2_ref[...] + _rms_bwd(dh2v * g_ref[...], xn, r)
        dx1_ref[...] = dx1
        dx1c = dx1.astype(w_ref.dtype)
        off = 0
        for m_ref, dm_ref, wd in zip(m_refs, dm_refs, widths):
            for k in range(wd // dq):
                j = off // dq + k
                cols = slice(k * dq, (k + 1) * dq)
                dm_ref[:, cols] = lax.dot_general(dx1c, w_ref[j], (((1,), (1,)), ((), ())), preferred_element_type=F32)
                dw_ref[j] += lax.dot_general(m_ref[:, cols].astype(w_ref.dtype), dx1c, (((0,), (0,)), ((), ())),
                                             preferred_element_type=F32)
            off += wd

    tile = lambda wd: pl.BlockSpec((tm, wd), lambda i: (i, 0))
    return pl.pallas_call(
        body, name="out_proj_bwd", grid=(t // tm,),
        in_specs=[tile(d), tile(d), tile(d), pl.BlockSpec((1, d), lambda i: (0, 0))]
                 + [tile(wd) for wd in widths] + [pl.BlockSpec((4, dq, d), lambda i: (0, 0, 0))],
        out_specs=[tile(d)] + [tile(wd) for wd in widths]
                  + [pl.BlockSpec((4, dq, d), lambda i: (0, 0, 0)), pl.BlockSpec((1, d), lambda i: (0, 0))],
        out_shape=[jax.ShapeDtypeStruct((t, d), F32)] + [jax.ShapeDtypeStruct((t, wd), F32) for wd in widths]
                  + [jax.ShapeDtypeStruct((4, dq, d), F32), jax.ShapeDtypeStruct((1, d), F32)],
        compiler_params=_params("arbitrary"))(dx2, dh2, x1, g2, *mix, wo)


def _ffn_fwd(x1, h2, wg, wu, wd, tm):
    t, d = x1.shape
    fs = wg.shape[1]

    def body(x1_ref, h2_ref, wg_ref, wu_ref, wd_ref, x2_ref, gp_ref, up_ref):
        @pl.when(pl.program_id(1) == 0)
        def _():
            x2_ref[...] = x1_ref[...]

        h = h2_ref[...]
        nt = (((1,), (1,)), ((), ()))
        gp = lax.dot_general(h, wg_ref[...], nt, preferred_element_type=F32)
        up = lax.dot_general(h, wu_ref[...], nt, preferred_element_type=F32)
        gp_ref[...] = gp
        up_ref[...] = up
        ff = gp * _sigmoid(gp) * up
        x2_ref[...] += jnp.dot(ff.astype(wd_ref.dtype), wd_ref[...], preferred_element_type=F32)

    return pl.pallas_call(
        body, name="ffn_fwd", grid=(t // tm, 4),
        in_specs=[pl.BlockSpec((tm, d), lambda i, j: (i, 0)), pl.BlockSpec((tm, d), lambda i, j: (i, 0)),
                  pl.BlockSpec((None, fs, d), lambda i, j: (j, 0, 0)),
                  pl.BlockSpec((None, fs, d), lambda i, j: (j, 0, 0)),
                  pl.BlockSpec((None, fs, d), lambda i, j: (j, 0, 0))],
        out_specs=[pl.BlockSpec((tm, d), lambda i, j: (i, 0)), pl.BlockSpec((None, tm, fs), lambda i, j: (j, i, 0)),
                   pl.BlockSpec((None, tm, fs), lambda i, j: (j, i, 0))],
        out_shape=[jax.ShapeDtypeStruct((t, d), F32), jax.ShapeDtypeStruct((4, t, fs), F32),
                   jax.ShapeDtypeStruct((4, t, fs), F32)],
        compiler_params=_params("arbitrary", "arbitrary"))(x1, h2, wg, wu, wd)


def _ffn_bwd(dx2, h2, gp, up, wg, wu, wd, tm):
    t, d = dx2.shape
    fs = wg.shape[1]

    def body(dx2_ref, h2_ref, gp_ref, up_ref, wg_ref, wu_ref, wd_ref, dh2_ref, dwg_ref, dwu_ref, dwd_ref):
        j, i = pl.program_id(0), pl.program_id(1)

        @pl.when(i == 0)
        def _():
            dwg_ref[...] = jnp.zeros_like(dwg_ref)
            dwu_ref[...] = jnp.zeros_like(dwu_ref)
            dwd_ref[...] = jnp.zeros_like(dwd_ref)

        cdt = wg_ref.dtype
        h = h2_ref[...]
        gpv, upv = gp_ref[...], up_ref[...]
        s = _sigmoid(gpv)
        silu = gpv * s
        dx2c = dx2_ref[...].astype(cdt)
        dff = lax.dot_general(dx2c, wd_ref[...], (((1,), (1,)), ((), ())), preferred_element_type=F32)
        dwd_ref[...] += lax.dot_general((silu * upv).astype(cdt), dx2c, (((0,), (0,)), ((), ())), preferred_element_type=F32)
        dup = (dff * silu).astype(cdt)
        dgp = (dff * upv * _dsilu(gpv, s)).astype(cdt)
        dwg_ref[...] += lax.dot_general(dgp, h, (((0,), (0,)), ((), ())), preferred_element_type=F32)
        dwu_ref[...] += lax.dot_general(dup, h, (((0,), (0,)), ((), ())), preferred_element_type=F32)
        dh = (jnp.dot(dgp, wg_ref[...], preferred_element_type=F32) + jnp.dot(dup, wu_ref[...], preferred_element_type=F32))
        rows = pl.ds(pl.multiple_of(i * tm, tm), tm)

        @pl.when(j == 0)
        def _():
            dh2_ref[rows, :] = dh

        @pl.when(j != 0)
        def _():
            dh2_ref[rows, :] += dh

    return pl.pallas_call(
        body, name="ffn_bwd", grid=(4, t // tm),
        in_specs=[pl.BlockSpec((tm, d), lambda j, i: (i, 0)), pl.BlockSpec((tm, d), lambda j, i: (i, 0)),
                  pl.BlockSpec((None, tm, fs), lambda j, i: (j, i, 0)), pl.BlockSpec((None, tm, fs), lambda j, i: (j, i, 0)),
                  pl.BlockSpec((None, fs, d), lambda j, i: (j, 0, 0)),
                  pl.BlockSpec((None, fs, d), lambda j, i: (j, 0, 0)),
                  pl.BlockSpec((None, fs, d), lambda j, i: (j, 0, 0))],
        out_specs=[pl.BlockSpec((t, d), lambda j, i: (0, 0)), pl.BlockSpec((None, fs, d), lambda j, i: (j, 0, 0)),
                   pl.BlockSpec((None, fs, d), lambda j, i: (j, 0, 0)), pl.BlockSpec((None, fs, d), lambda j, i: (j, 0, 0))],
        out_shape=[jax.ShapeDtypeStruct((t, d), F32)] + [jax.ShapeDtypeStruct((4, fs, d), F32)] * 3,
        compiler_params=_params("arbitrary", "arbitrary"))(dx2, h2, gp, up, wg, wu, wd)


def _ple_fwd(x2, p, wpg, wpp, tm):
    t, d = x2.shape
    q = p.shape[1]
    dq = d // 4

    def body(x_ref, p_ref, wg_ref, wp_ref, o_ref):
        xv = x_ref[...]
        xc = xv.astype(wg_ref.dtype)
        pc = p_ref[...].astype(wp_ref.dtype)
        pre = jnp.dot(xc[:, :dq], wg_ref[0], preferred_element_type=F32)
        for j in range(1, 4):
            pre = pre + jnp.dot(xc[:, j * dq:(j + 1) * dq], wg_ref[j], preferred_element_type=F32)
        gate = _sigmoid(pre)
        for j in range(4):
            cols = slice(j * dq, (j + 1) * dq)
            o_ref[:, cols] = xv[:, cols] + gate[:, cols] * jnp.dot(pc, wp_ref[j], preferred_element_type=F32)

    return pl.pallas_call(
        body, name="ple_fwd", grid=(t // tm,),
        in_specs=[pl.BlockSpec((tm, d), lambda i: (i, 0)), pl.BlockSpec((tm, q), lambda i: (i, 0)),
                  pl.BlockSpec((4, dq, d), lambda i: (0, 0, 0)),
                  pl.BlockSpec((4, q, dq), lambda i: (0, 0, 0))],
        out_specs=pl.BlockSpec((tm, d), lambda i: (i, 0)),
        out_shape=jax.ShapeDtypeStruct((t, d), F32),
        compiler_params=_params("arbitrary"))(x2, p, wpg, wpp)


def _ple_bwd(dx3, x2, p, wpg, wpp, tm):
    t, d = x2.shape
    q = p.shape[1]
    dq = d // 4

    def body(dx3_ref, x_ref, p_ref, wg_ref, wp_ref, dx2_ref, dwg_ref, dwp_ref):
        @pl.when(pl.program_id(0) == 0)
        def _():
            dwg_ref[...] = jnp.zeros_like(dwg_ref)
            dwp_ref[...] = jnp.zeros_like(dwp_ref)

        cdt = wg_ref.dtype
        xc = x_ref[...].astype(cdt)
        pc = p_ref[...].astype(cdt)
        pre = jnp.dot(xc[:, :dq], wg_ref[0], preferred_element_type=F32)
        for j in range(1, 4):
            pre = pre + jnp.dot(xc[:, j * dq:(j + 1) * dq], wg_ref[j], preferred_element_type=F32)
        gate = _sigmoid(pre)
        dx3v = dx3_ref[...]
        dpp = (dx3v * gate).astype(cdt)
        dgate = dx3v * gate * (1.0 - gate)
        dpre_parts = []
        for j in range(4):
            cols = slice(j * dq, (j + 1) * dq)
            pp_j = jnp.dot(pc, wp_ref[j], preferred_element_type=F32)
            dpre_parts.append((dgate[:, cols] * pp_j).astype(cdt))
            dwp_ref[j] += lax.dot_general(pc, dpp[:, cols], (((0,), (0,)), ((), ())), preferred_element_type=F32)
        dpre = jnp.concatenate(dpre_parts, axis=1)
        for j in range(4):
            cols = slice(j * dq, (j + 1) * dq)
            dwg_ref[j] += lax.dot_general(xc[:, cols], dpre, (((0,), (0,)), ((), ())), preferred_element_type=F32)
            dx2_ref[:, cols] = dx3v[:, cols] + lax.dot_general(dpre, wg_ref[j], (((1,), (1,)), ((), ())),
                                                               preferred_element_type=F32)

    return pl.pallas_call(
        body, name="ple_bwd", grid=(t // tm,),
        in_specs=[pl.BlockSpec((tm, d), lambda i: (i, 0)), pl.BlockSpec((tm, d), lambda i: (i, 0)),
                  pl.BlockSpec((tm, q), lambda i: (i, 0)), pl.BlockSpec((4, dq, d), lambda i: (0, 0, 0)),
                  pl.BlockSpec((4, q, dq), lambda i: (0, 0, 0))],
        out_specs=[pl.BlockSpec((tm, d), lambda i: (i, 0)), pl.BlockSpec((4, dq, d), lambda i: (0, 0, 0)),
                   pl.BlockSpec((4, q, dq), lambda i: (0, 0, 0))],
        out_shape=[jax.ShapeDtypeStruct((t, d), F32), jax.ShapeDtypeStruct((4, dq, d), F32),
                   jax.ShapeDtypeStruct((4, q, dq), F32)],
        compiler_params=_params("arbitrary"))(dx3, x2, p, wpg, wpp)


def _loss_head(x, target, fg, tm):
    t, d = x.shape

    def body(x_ref, t_ref, g_ref, dx_ref, loss_ref, dg_ref):
        @pl.when(pl.program_id(0) == 0)
        def _():
            loss_ref[...] = jnp.zeros_like(loss_ref)
            dg_ref[...] = jnp.zeros_like(dg_ref)

        xn, r = _rms_fwd(x_ref[...])
        g = g_ref[...]
        err = xn * g - t_ref[...]
        loss_ref[...] += 0.5 * jnp.sum(jnp.sum(err * err, axis=-1, keepdims=True) / d, axis=0, keepdims=True)
        dy = err / d
        dg_ref[...] += jnp.sum(dy * xn, axis=0, keepdims=True)
        dx_ref[...] = _rms_bwd(dy * g, xn, r)

    return pl.pallas_call(
        body, name="loss_head", grid=(t // tm,),
        in_specs=[pl.BlockSpec((tm, d), lambda i: (i, 0)), pl.BlockSpec((tm, d), lambda i: (i, 0)),
                  pl.BlockSpec((1, d), lambda i: (0, 0))],
        out_specs=[pl.BlockSpec((tm, d), lambda i: (i, 0)), pl.BlockSpec((1, 1), lambda i: (0, 0)),
                   pl.BlockSpec((1, d), lambda i: (0, 0))],
        out_shape=[jax.ShapeDtypeStruct((t, d), F32), jax.ShapeDtypeStruct((1, 1), F32),
                   jax.ShapeDtypeStruct((1, d), F32)],
        compiler_params=_params("arbitrary"))(x, target, fg)


def _qkv_conv_act(xv, w, j, heads):
    k = QKV_CONV_WIDTH
    y = w[k - 1:k] * xv
    for s in range(1, k):
        y = y + w[k - 1 - s:k - s] * _shift_down(xv, s)
    sg = _sigmoid(y)
    s_act = y * sg
    nrm = lax.rsqrt(jnp.sum(s_act * s_act, axis=-1, keepdims=True) + EPS)
    scale = jnp.where(j < heads, HEAD_DIM ** -0.5, 1.0).astype(F32)
    return y, sg, s_act, nrm, scale


def _qkv_conv_fwd(qkv_pre, conv_w, heads):
    t = qkv_pre.shape[0]
    nblk = 3 * heads

    def body(x_ref, w_ref, o_ref):
        j = pl.program_id(0)
        _, _, s_act, nrm, scale = _qkv_conv_act(x_ref[...], w_ref[...], j, heads)
        o_ref[...] = jnp.where(j < 2 * heads, s_act * (nrm * scale), s_act)

    return pl.pallas_call(
        body, name="qkv_conv_fwd", grid=(nblk,),
        in_specs=[pl.BlockSpec((t, LANES), lambda j: (0, j)), pl.BlockSpec((QKV_CONV_WIDTH, LANES), lambda j: (0, j))],
        out_specs=pl.BlockSpec((t, LANES), lambda j: (0, j)),
        out_shape=jax.ShapeDtypeStruct(qkv_pre.shape, F32),
        compiler_params=_params("arbitrary"))(qkv_pre, conv_w)


def _qkv_conv_bwd(qkv_pre, conv_w, dqkv, heads):
    t = qkv_pre.shape[0]
    nblk = 3 * heads
    k = QKV_CONV_WIDTH

    def body(x_ref, w_ref, dn_ref, dx_ref, dw_ref):
        j = pl.program_id(0)
        xv, w = x_ref[...], w_ref[...]
        y, sg, s_act, nrm, scale = _qkv_conv_act(xv, w, j, heads)
        dn = dn_ref[...]
        dsn = dn * scale
        ds_qk = nrm * dsn - s_act * (nrm * nrm * nrm) * jnp.sum(dsn * s_act, axis=-1, keepdims=True)
        ds = jnp.where(j < 2 * heads, ds_qk, dn)
        dy = ds * _dsilu(y, sg)
        dx = w[k - 1:k] * dy
        dw_ref[k - 1:k, :] = jnp.sum(dy * xv, axis=0, keepdims=True)
        for s in range(1, k):
            dx = dx + w[k - 1 - s:k - s] * _shift_up(dy, s)
            dw_ref[k - 1 - s:k - s, :] = jnp.sum(dy * _shift_down(xv, s), axis=0, keepdims=True)
        dx_ref[...] = dx

    return pl.pallas_call(
        body, name="qkv_conv_bwd", grid=(nblk,),
        in_specs=[pl.BlockSpec((t, LANES), lambda j: (0, j)), pl.BlockSpec((k, LANES), lambda j: (0, j)),
                  pl.BlockSpec((t, LANES), lambda j: (0, j))],
        out_specs=[pl.BlockSpec((t, LANES), lambda j: (0, j)), pl.BlockSpec((k, LANES), lambda j: (0, j))],
        out_shape=[jax.ShapeDtypeStruct(qkv_pre.shape, F32), jax.ShapeDtypeStruct(conv_w.shape, F32)],
        compiler_params=_params("arbitrary"))(qkv_pre, conv_w, dqkv)


def _pool_windows(shape, j, group_dim):
    lane = lax.broadcasted_iota(jnp.int32, shape, 1) + j * LANES
    grp = lane // group_dim
    win = jnp.left_shift(2, grp).astype(F32)
    cnt = jnp.minimum((_rows(shape) + 1).astype(F32), win)
    return grp, cnt


def _pool_select(grp, levels):
    out = levels[0]
    for gi in range(1, POOL_GROUPS):
        out = jnp.where(grp == gi, levels[gi], out)
    return out


def _pool_mean(hv, grp, cnt):
    acc, levels, width = hv, [], 1
    for _ in range(POOL_GROUPS):
        acc = acc + _shift_down(acc, width)
        width *= 2
        levels.append(acc)
    return _pool_select(grp, levels) / cnt - hv


def _pool_fwd(hp, wbd, scale, group_dim):
    t, dp = hp.shape

    def body(h_ref, w_ref, s_ref, o_ref):
        hv = h_ref[...]
        grp, cnt = _pool_windows(hv.shape, pl.program_id(0), group_dim)
        pooled = _pool_mean(hv, grp, cnt)
        o_ref[...] = _mm(pooled, w_ref[...]) * s_ref[...]

    return pl.pallas_call(
        body, name="pool_fwd", grid=(dp // LANES,),
        in_specs=[pl.BlockSpec((t, LANES), lambda j: (0, j)), pl.BlockSpec((LANES, LANES), lambda j: (j, j)),
                  pl.BlockSpec((1, LANES), lambda j: (0, j))],
        out_specs=pl.BlockSpec((t, LANES), lambda j: (0, j)),
        out_shape=jax.ShapeDtypeStruct(hp.shape, F32),
        compiler_params=_params("arbitrary"))(hp, wbd, scale)


def _pool_bwd(hp, wbd, scale, dob, group_dim):
    t, dp = hp.shape

    def body(h_ref, w_ref, s_ref, do_ref, dh_ref, dw_ref, ds_ref):
        hv = h_ref[...]
        grp, cnt = _pool_windows(hv.shape, pl.program_id(0), group_dim)
        pooled = _pool_mean(hv, grp, cnt)
        wv = w_ref[...]
        dov = do_ref[...]
        ds_ref[...] = jnp.sum(dov * _mm(pooled, wv), axis=0, keepdims=True)
        dys = dov * s_ref[...]
        dw_ref[0] = _mm_tn(pooled, dys)
        dpooled = _mm_nt(dys, wv)
        acc, levels, width = dpooled / cnt, [], 1
        for _ in range(POOL_GROUPS):
            acc = acc + _shift_up(acc, width)
            width *= 2
            levels.append(acc)
        dh_ref[...] = _pool_select(grp, levels) - dpooled

    nb = dp // LANES
    return pl.pallas_call(
        body, name="pool_bwd", grid=(nb,),
        in_specs=[pl.BlockSpec((t, LANES), lambda j: (0, j)), pl.BlockSpec((LANES, LANES), lambda j: (j, j)),
                  pl.BlockSpec((1, LANES), lambda j: (0, j)), pl.BlockSpec((t, LANES), lambda j: (0, j))],
        out_specs=[pl.BlockSpec((t, LANES), lambda j: (0, j)), pl.BlockSpec((1, LANES, LANES), lambda j: (j, 0, 0)),
                   pl.BlockSpec((1, LANES), lambda j: (0, j))],
        out_shape=[jax.ShapeDtypeStruct(hp.shape, F32), jax.ShapeDtypeStruct((nb, LANES, LANES), F32),
                   jax.ShapeDtypeStruct((1, dp), F32)],
        compiler_params=_params("arbitrary"))(hp, wbd, scale, dob)


def _sconv_fwd(cbcch, w):
    t, dc3 = cbcch.shape
    nb = dc3 // 3 // LANES
    k = SCONV_WIDTH

    def body(b_ref, c_ref, h_ref, w_ref, o_ref):
        m = c_ref[...] * h_ref[...]
        wv = w_ref[...]
        y = wv[k - 1:k] * m
        for s in range(1, k):
            y = y + wv[k - 1 - s:k - s] * _shift_down(m, s)
        o_ref[...] = b_ref[...] * y

    return pl.pallas_call(
        body, name="sconv_fwd", grid=(nb,),
        in_specs=[pl.BlockSpec((t, LANES), lambda j: (0, j)), pl.BlockSpec((t, LANES), lambda j: (0, nb + j)),
                  pl.BlockSpec((t, LANES), lambda j: (0, 2 * nb + j)), pl.BlockSpec((k, LANES), lambda j: (0, j))],
        out_specs=pl.BlockSpec((t, LANES), lambda j: (0, j)),
        out_shape=jax.ShapeDtypeStruct((t, dc3 // 3), F32),
        compiler_params=_params("arbitrary"))(cbcch, cbcch, cbcch, w)


def _sconv_bwd(cbcch, w, doc):
    t, dc3 = cbcch.shape
    nb = dc3 // 3 // LANES
    k = SCONV_WIDTH

    def body(b_ref, c_ref, h_ref, w_ref, do_ref, db_ref, dc_ref, dh_ref, dw_ref):
        cv, hv = c_ref[...], h_ref[...]
        m = cv * hv
        wv = w_ref[...]
        dov = do_ref[...]
        dy = dov * b_ref[...]
        y = wv[k - 1:k] * m
        dm = wv[k - 1:k] * dy
        dw_ref[k - 1:k, :] = jnp.sum(dy * m, axis=0, keepdims=True)
        for s in range(1, k):
            ms = _shift_down(m, s)
            y = y + wv[k - 1 - s:k - s] * ms
            dm = dm + wv[k - 1 - s:k - s] * _shift_up(dy, s)
            dw_ref[k - 1 - s:k - s, :] = jnp.sum(dy * ms, axis=0, keepdims=True)
        db_ref[...] = dov * y
        dc_ref[...] = dm * hv
        dh_ref[...] = dm * cv

    col = lambda o: pl.BlockSpec((t, LANES), lambda j: (0, o * nb + j))
    return pl.pallas_call(
        body, name="sconv_bwd", grid=(nb,),
        in_specs=[col(0), col(1), col(2), pl.BlockSpec((k, LANES), lambda j: (0, j)), col(0)],
        out_specs=[col(0), col(0), col(0), pl.BlockSpec((k, LANES), lambda j: (0, j))],
        out_shape=[jax.ShapeDtypeStruct((t, dc3 // 3), F32)] * 3 + [jax.ShapeDtypeStruct(w.shape, F32)],
        compiler_params=_params("arbitrary"))(cbcch, cbcch, cbcch, w, doc)


class _Split:
    def __init__(self, a):
        self.hi = a.astype(jnp.bfloat16)
        self.lo = (a - self.hi.astype(F32)).astype(jnp.bfloat16)


def _per_head(dims, a, b):
    a = a if isinstance(a, _Split) else _Split(a)
    b = b if isinstance(b, _Split) else _Split(b)

    def dot(x, y):
        return lax.dot_general(x, y, (dims, ((), ())), preferred_element_type=F32)

    return jnp.stack([dot(a.hi[h], b.hi[h]) + (dot(a.hi[h], b.lo[h]) + dot(a.lo[h], b.hi[h])) for h in range(a.hi.shape[0])])


def _bmm(a, b):
    return _per_head(((1,), (0,)), a, b)


def _bmm_nt(a, b):
    return _per_head(((1,), (1,)), a, b)


def _bmm_tn(a, b):
    return _per_head(((0,), (0,)), a, b)


def _inv_unit_lower(low):
    c = low.shape[-1]
    eye = (_rows((c, c)) == lax.broadcasted_iota(jnp.int32, (c, c), 1)).astype(F32)
    pw = -low
    inv = eye + pw
    span = 2
    while span < c:
        pws = _Split(pw)
        pw = _bmm(pws, pws)
        inv = inv + _bmm(inv, pw)
        span *= 2
    return inv


def _heads_of(ref, base, heads):
    return jnp.stack([ref[:, base + h * HEAD_DIM:base + (h + 1) * HEAD_DIM] for h in range(heads)])


def _chunk_common(q, k, v, a_col, b_col, alog, dtb, kept=None):
    hn, c, _ = q.shape
    beta = _sigmoid(b_col)
    xg = a_col + dtb
    softplus = jnp.maximum(xg, 0.0) + jnp.log(1.0 + jnp.exp(-jnp.abs(xg)))
    neg_ea = -jnp.exp(alog)
    g = neg_ea * softplus
    ri = _rows((c, c))
    ci = lax.broadcasted_iota(jnp.int32, (c, c), 1)
    incl, strict = ri >= ci, ri > ci
    inclf = jnp.broadcast_to(incl.astype(F32), (hn, c, c))
    gcb = _bmm(inclf, jnp.broadcast_to(g, (hn, c, HEAD_DIM)))
    gc_row = jnp.sum(jnp.where(ri <= ci, jnp.broadcast_to(g, (hn, c, c)), 0.0), axis=1, keepdims=True)
    dmat = jnp.where(incl, jnp.exp(jnp.where(incl, gcb[:, :, :1] - gc_row, 0.0)), 0.0)
    eg = jnp.exp(gcb)
    gl = gcb[:, c - 1:c, :]
    egl = jnp.exp(gl)
    edl = jnp.exp(gl - gcb)
    kb, vb = k * beta, v * beta
    kbe = kb * eg
    if kept is None:
        ks = _Split(k)
        a0 = _bmm_nt(kb, ks)
        tm = _inv_unit_lower(jnp.where(strict, a0 * dmat, 0.0))
        p0 = _bmm_nt(q, ks)
        tms = _Split(tm)
        u, w = _bmm(tms, vb), _bmm(tms, kbe)
    else:
        (a0, tm, p0, w), u = kept, None
    return dict(beta=beta, xg=xg, neg_ea=neg_ea, g=g, incl=incl, strict=strict, inclf=inclf, dmat=dmat, eg=eg,
                egl=egl, edl=edl, kb=kb, vb=vb, a0=a0, tm=tm, kbe=kbe, u=u, w=w, p0=p0,
                attn=p0 * dmat, qe=q * eg, kd=k * edl)


def _chunk_step(cm, state):
    ss = _Split(state)
    vn = cm["u"] - _bmm(cm["w"], ss)
    vns = _Split(vn)
    o = _bmm(cm["qe"], ss) + _bmm(cm["attn"], vns)
    new_state = state * cm["egl"][:, :, :1] + _bmm_tn(cm["kd"], vns)
    return vn, o, new_state


def _gated_norm(o, zv, og):
    xo, ro = _rms_fwd(o)
    sgz = _sigmoid(zv)
    return xo, ro, sgz, xo * og * (zv * sgz)


def _gate_columns(abv, gpv, heads):
    a_col = jnp.stack([abv[:, h:h + 1] for h in range(heads)])
    b_col = jnp.stack([abv[:, heads + h:heads + h + 1] for h in range(heads)])
    alog = jnp.stack([gpv[0:1, h:h + 1] for h in range(heads)])
    dtb = jnp.stack([gpv[1:2, h:h + 1] for h in range(heads)])
    return a_col, b_col, alog, dtb


def _delta_fwd(qkv, z, ab, gpar, heads):
    t = qkv.shape[0]
    da = heads * HEAD_DIM
    n = t // CHUNK

    def body(qkv_ref, z_ref, ab_ref, gp_ref, oa_ref, st_ref, kc_ref, kw_ref, s_ref):
        @pl.when(pl.program_id(0) == 0)
        def _():
            s_ref[...] = jnp.zeros_like(s_ref)

        gpv = gp_ref[...]
        cm = _chunk_common(_heads_of(qkv_ref, 0, heads), _heads_of(qkv_ref, da, heads), _heads_of(qkv_ref, 2 * da, heads),
                           *_gate_columns(ab_ref[...], gpv, heads))
        state = s_ref[...]
        st_ref[0] = state
        vn, o, new_state = _chunk_step(cm, state)
        s_ref[...] = new_state
        for slot, val in enumerate((cm["a0"], cm["tm"], cm["p0"])):
            kc_ref[0, slot] = val
        for slot, val in enumerate((cm["w"], vn, o)):
            kw_ref[0, slot] = val
        oa = _gated_norm(o, _heads_of(z_ref, 0, heads), gpv[2:3, :])[3]
        for h in range(heads):
            oa_ref[:, h * HEAD_DIM:(h + 1) * HEAD_DIM] = oa[h]

    return pl.pallas_call(
        body, name="delta_fwd", grid=(n,),
        in_specs=[pl.BlockSpec((CHUNK, 3 * da), lambda i: (i, 0)), pl.BlockSpec((CHUNK, da), lambda i: (i, 0)),
                  pl.BlockSpec((CHUNK, LANES), lambda i: (i, 0)), pl.BlockSpec((8, LANES), lambda i: (0, 0))],
        out_specs=[pl.BlockSpec((CHUNK, da), lambda i: (i, 0)),
                   pl.BlockSpec((1, heads, HEAD_DIM, HEAD_DIM), lambda i: (i, 0, 0, 0)),
                   pl.BlockSpec((1, 3, heads, CHUNK, CHUNK), lambda i: (i, 0, 0, 0, 0)),
                   pl.BlockSpec((1, 3, heads, CHUNK, HEAD_DIM), lambda i: (i, 0, 0, 0, 0))],
        out_shape=[jax.ShapeDtypeStruct((t, da), F32), jax.ShapeDtypeStruct((n, heads, HEAD_DIM, HEAD_DIM), F32),
                   jax.ShapeDtypeStruct((n, 3, heads, CHUNK, CHUNK), F32),
                   jax.ShapeDtypeStruct((n, 3, heads, CHUNK, HEAD_DIM), F32)],
        scratch_shapes=[pltpu.VMEM((heads, HEAD_DIM, HEAD_DIM), F32)],
        compiler_params=_params("arbitrary"))(qkv, z, ab, gpar)


def _delta_bwd(qkv, z, ab, gpar, states, kept_c, kept_w, doa, heads):
    t = qkv.shape[0]
    da = heads * HEAD_DIM
    n = t // CHUNK
    c = CHUNK

    def body(qkv_ref, z_ref, ab_ref, gp_ref, st_ref, kc_ref, kw_ref, doa_ref, dqkv_ref, dz_ref, dab_ref, dpar_ref, ds_ref):
        @pl.when(pl.program_id(0) == 0)
        def _():
            ds_ref[...] = jnp.zeros_like(ds_ref)
            dpar_ref[...] = jnp.zeros_like(dpar_ref)

        gpv = gp_ref[...]
        og = gpv[2:3, :]
        q, k, v = _heads_of(qkv_ref, 0, heads), _heads_of(qkv_ref, da, heads), _heads_of(qkv_ref, 2 * da, heads)
        cm = _chunk_common(q, k, v, *_gate_columns(ab_ref[...], gpv, heads),
                           kept=(kc_ref[0, 0], kc_ref[0, 1], kc_ref[0, 2], kw_ref[0, 0]))
        state = st_ref[0]
        dsp = ds_ref[...]
        vn, o = kw_ref[0, 1], kw_ref[0, 2]
        zv = _heads_of(z_ref, 0, heads)
        xo, ro, sgz, _ = _gated_norm(o, zv, og)
        doav = _heads_of(doa_ref, 0, heads)
        don = doav * (zv * sgz)
        dz = doav * (xo * og) * _dsilu(zv, sgz)
        d_og = jnp.sum(jnp.sum(don * xo, axis=1, keepdims=True), axis=0)
        do = _rms_bwd(don * og, xo, ro)
        tm, dmat, eg, edl, egl = cm["tm"], cm["dmat"], cm["eg"], cm["edl"], cm["egl"]
        dos, dsps, sts, tms, ks = _Split(do), _Split(dsp), _Split(state), _Split(tm), _Split(k)
        dvn = _bmm_tn(cm["attn"], dos) + _bmm(cm["kd"], dsps)
        dvns = _Split(dvn)
        dqe = _bmm_nt(dos, sts)
        ds_ref[...] = _bmm_tn(cm["qe"], dos) + dsp * egl[:, :, :1] - _bmm_tn(cm["w"], dvns)
        dattn = _bmm_nt(dos, vn)
        dkd = _bmm_nt(vn, dsps)
        dkd_kd = jnp.sum(dkd * cm["kd"], axis=-1, keepdims=True)
        dgl = (jnp.sum(jnp.sum(dsp * state, axis=-1, keepdims=True), axis=1, keepdims=True) * egl[:, :, :1]
               + jnp.sum(dkd_kd, axis=1, keepdims=True))
        dgc = jnp.sum(dqe * cm["qe"], axis=-1, keepdims=True) - dkd_kd
        dk = dkd * edl
        dq = dqe * eg
        dw = -_bmm_nt(dvns, sts)
        dws = _Split(dw)
        dp0 = dattn * dmat
        dd = jnp.where(cm["incl"], dattn * cm["p0"], 0.0)
        dp0s = _Split(dp0)
        dq = dq + _bmm(dp0s, ks)
        dk = dk + _bmm_tn(dp0s, q)
        dtm = _bmm_nt(dvns, cm["vb"]) + _bmm_nt(dws, cm["kbe"])
        dvb = _bmm_tn(tms, dvns)
        dkbe = _bmm_tn(tms, dws)
        dkb = dkbe * eg
        dgc = dgc + jnp.sum(dkbe * cm["kbe"], axis=-1, keepdims=True)
        dlow = jnp.where(cm["strict"], -_bmm_tn(tms, _bmm_nt(dtm, tms)), 0.0)
        dd = dd + dlow * cm["a0"]
        da0 = dlow * dmat
        da0s = _Split(da0)
        dkb = dkb + _bmm(da0s, ks)
        dk = dk + _bmm_tn(da0s, cm["kb"])
        ddd = dd * dmat
        ones = jnp.ones((heads, c, HEAD_DIM), F32)
        dgc = dgc + jnp.sum(ddd, axis=-1, keepdims=True) - _bmm_tn(ddd, ones)[:, :, :1]
        dgc = dgc + jnp.where(_rows((c, 1)) == c - 1, dgl, 0.0)
        dg = _bmm_tn(cm["inclf"], jnp.broadcast_to(dgc, (heads, c, HEAD_DIM)))[:, :, :1]
        beta = cm["beta"]
        dk = dk + dkb * beta
        dbeta = jnp.sum(dkb * k, axis=-1, keepdims=True) + jnp.sum(dvb * v, axis=-1, keepdims=True)
        dv = dvb * beta
        db_col = dbeta * beta * (1.0 - beta)
        da_col = dg * cm["neg_ea"] * _sigmoid(cm["xg"])
        d_alog = jnp.sum(dg * cm["g"], axis=1, keepdims=True)
        d_dtb = jnp.sum(da_col, axis=1, keepdims=True)
        lane = lax.broadcasted_iota(jnp.int32, (c, LANES), 1)
        lane8 = lax.broadcasted_iota(jnp.int32, (8, LANES), 1)
        row8 = _rows((8, LANES))
        dab = jnp.zeros((c, LANES), F32)
        dpar = jnp.where(row8 == 2, d_og, 0.0)
        for h in range(heads):
            lo = h * HEAD_DIM
            dqkv_ref[:, lo:lo + HEAD_DIM] = dq[h]
            dqkv_ref[:, da + lo:da + lo + HEAD_DIM] = dk[h]
            dqkv_ref[:, 2 * da + lo:2 * da + lo + HEAD_DIM] = dv[h]
            dz_ref[:, lo:lo + HEAD_DIM] = dz[h]
            dab = dab + jnp.where(lane == h, da_col[h], 0.0) + jnp.where(lane == heads + h, db_col[h], 0.0)
            dpar = (dpar + jnp.where((row8 == 0) & (lane8 == h), d_alog[h], 0.0)
                    + jnp.where((row8 == 1) & (lane8 == h), d_dtb[h], 0.0))
        dab_ref[...] = dab
        dpar_ref[...] += dpar

    rev = lambda i: (n - 1 - i, 0)
    return pl.pallas_call(
        body, name="delta_bwd", grid=(n,),
        in_specs=[pl.BlockSpec((c, 3 * da), rev), pl.BlockSpec((c, da), rev), pl.BlockSpec((c, LANES), rev),
                  pl.BlockSpec((8, LANES), lambda i: (0, 0)),
                  pl.BlockSpec((1, heads, HEAD_DIM, HEAD_DIM), lambda i: (n - 1 - i, 0, 0, 0)),
                  pl.BlockSpec((1, 3, heads, c, c), lambda i: (n - 1 - i, 0, 0, 0, 0)),
                  pl.BlockSpec((1, 3, heads, c, HEAD_DIM), lambda i: (n - 1 - i, 0, 0, 0, 0)),
                  pl.BlockSpec((c, da), rev)],
        out_specs=[pl.BlockSpec((c, 3 * da), rev), pl.BlockSpec((c, da), rev), pl.BlockSpec((c, LANES), rev),
                   pl.BlockSpec((8, LANES), lambda i: (0, 0))],
        out_shape=[jax.ShapeDtypeStruct((t, 3 * da), F32), jax.ShapeDtypeStruct((t, da), F32),
                   jax.ShapeDtypeStruct((t, LANES), F32), jax.ShapeDtypeStruct((8, LANES), F32)],
        scratch_shapes=[pltpu.VMEM((heads, HEAD_DIM, HEAD_DIM), F32)],
        compiler_params=_params("arbitrary"))(qkv, z, ab, gpar, states, kept_c, kept_w, doa)


def _w_in_pieces(shard_cols, da, heads):
    a0, nab = 4 * da, 2 * heads
    d_in = 4 * shard_cols
    runs = [(0, a0, 0), (a0, a0 + nab, d_in - nab), (a0 + nab, d_in, a0)]
    pieces = []
    for j in range(4):
        lo, hi = j * shard_cols, (j + 1) * shard_cols
        for rlo, rhi, plo in runs:
            s, e = max(lo, rlo), min(hi, rhi)
            if s < e:
                pieces.append((j, s - lo, e - s, plo + (s - rlo)))
    return pieces, d_in - nab + LANES


def _w_in_pack(w4, da, heads):
    _, d, sc = w4.shape
    pieces, npk = _w_in_pieces(sc, da, heads)
    tr = _tile_rows(d, 256, SUBLANES_WIRE)

    def body(w_ref, o_ref):
        o_ref[:, npk - LANES:] = jnp.zeros((tr, LANES), o_ref.dtype)
        for j, lo, ln, dst in pieces:
            o_ref[:, dst:dst + ln] = w_ref[j, :, lo:lo + ln]

    return pl.pallas_call(
        body, name="w_in_pack", grid=(d // tr,),
        in_specs=[pl.BlockSpec((4, tr, sc), lambda i: (0, i, 0))],
        out_specs=pl.BlockSpec((tr, npk), lambda i: (i, 0)),
        out_shape=jax.ShapeDtypeStruct((d, npk), w4.dtype),
        compiler_params=_params("arbitrary"))(w4)


def _w_in_unpack(dwp, sc, da, heads):
    d, npk = dwp.shape
    pieces, _ = _w_in_pieces(sc, da, heads)
    tr = _tile_rows(d, 256)

    def body(g_ref, o_ref):
        for j, lo, ln, dst in pieces:
            o_ref[j, :, lo:lo + ln] = g_ref[:, dst:dst + ln]

    return pl.pallas_call(
        body, name="w_in_unpack", grid=(d // tr,),
        in_specs=[pl.BlockSpec((tr, npk), lambda i: (i, 0))],
        out_specs=pl.BlockSpec((4, tr, sc), lambda i: (0, i, 0)),
        out_shape=jax.ShapeDtypeStruct((4, d, sc), F32),
        compiler_params=_params("arbitrary"))(dwp)


def _block_diag(pool_w):
    g, gd, _ = pool_w.shape
    out = jnp.zeros((g * gd, g * gd), pool_w.dtype)
    for gi in range(g):
        out = lax.dynamic_update_slice(out, pool_w[gi], (gi * gd, gi * gd))
    return out


def _layer_dims(d):
    heads = (d // 2) // HEAD_DIM
    return heads, heads * HEAD_DIM, d // 4, d // 4


BIG = ("w_in", "w_gate", "w_up", "ple_proj", "w_out", "w_down", "ple_gate")
TRANSPOSED = ("w_gate", "w_up")


def _prepare_layer(small, li):
    d = small["norm1_g"].shape[1]
    heads, _, _, _ = _layer_dims(d)
    gpar = jnp.zeros((8, LANES), F32)
    gpar = gpar.at[0, :heads].set(small["a_log"][li]).at[1, :heads].set(small["dt_bias"][li]).at[2, :].set(small["onorm_g"][li])
    return dict(norm1_g=small["norm1_g"][li][None], conv_qkv=small["conv_qkv"][li], gpar=gpar, pool_bd=_block_diag(small["pool_w"][li]).astype(MM_DTYPE),
                pool_scale=small["pool_scale"][li][None], sconv_w=small["sconv_w"][li], norm2_g=small["norm2_g"][li][None])


def _layer_fwd(x0, p, gw, lw, tm, arrive):
    d = x0.shape[1]
    heads, da, dp, dc = _layer_dims(d)
    segs = (3 * da, da, dp, 3 * dc, LANES)
    lw["w_in_p"] = _w_in_pack(gw["w_in"], da, heads).astype(MM_DTYPE)
    qkv_pre, z, hp, cbcch, ab = _in_proj_fwd(x0, lw["norm1_g"], lw["w_in_p"], segs, tm)
    qkv = _qkv_conv_fwd(qkv_pre, lw["conv_qkv"], heads)
    oa, states, kept_c, kept_w = _delta_fwd(qkv, z, ab, lw["gpar"], heads)
    ob = _pool_fwd(hp, lw["pool_bd"], lw["pool_scale"], dp // POOL_GROUPS)
    oc = _sconv_fwd(cbcch, lw["sconv_w"])
    arrive("mixed", oa)
    x1, h2 = _out_proj_fwd(x0, (oa, ob, oc), gw["w_out"], lw["norm2_g"], tm)
    x2, gp, up = _ffn_fwd(x1, h2, gw["w_gate"], gw["w_up"], gw["w_down"], tm)
    arrive("ffn", x2)
    x3 = _ple_fwd(x2, p, gw["ple_gate"], gw["ple_proj"], tm)
    arrive("end", x3)
    saved = dict(x0=x0, qkv_pre=qkv_pre, z=z, hp=hp, cbcch=cbcch, ab=ab, qkv=qkv, states=states, kept_c=kept_c, kept_w=kept_w, oa=oa, ob=ob, oc=oc,
                 x1=x1, h2=h2, gp=gp, up=up, x2=x2)
    return x3, saved


def _layer_bwd(dx3, p, gw, lw, sv, tm, produced):
    def after_token(tok, arr):
        return arr if tok is None else arr + tok[0, 0]

    d = dx3.shape[1]
    heads, da, dp, dc = _layer_dims(d)
    segs = (3 * da, da, dp, dc, dc, dc, LANES)
    gd = dp // POOL_GROUPS
    dx2, d_ple_gate, d_ple_proj = _ple_bwd(dx3, sv["x2"], p, gw["ple_gate"], gw["ple_proj"], tm)
    dh2, d_w_gate, d_w_up, d_w_down = _ffn_bwd(dx2, sv["h2"], sv["gp"], sv["up"], gw["w_gate"], gw["w_up"], gw["w_down"],
                                               min(tm, 256))
    tok = produced("ffn", dict(w_gate=d_w_gate, w_up=d_w_up, ple_proj=d_ple_proj, w_down=d_w_down, ple_gate=d_ple_gate), dh2)
    dx1, doa, dob, doc, d_w_out, d_norm2 = _out_proj_bwd(dx2, dh2, sv["x1"], after_token(tok, lw["norm2_g"]),
                                                         (sv["oa"], sv["ob"], sv["oc"]), gw["w_out"], tm)
    dcb, dcc, dch, d_sconv = _sconv_bwd(sv["cbcch"], lw["sconv_w"], doc)
    dhp, d_pool_bd, d_pool_scale = _pool_bwd(sv["hp"], lw["pool_bd"], lw["pool_scale"], dob, gd)
    dqkv, dz, dab, dpar = _delta_bwd(sv["qkv"], sv["z"], sv["ab"], lw["gpar"], sv["states"], sv["kept_c"], sv["kept_w"], doa,
                                      heads)
    tok = produced("mixers", {}, dqkv)
    dqkv_pre, d_conv_qkv = _qkv_conv_bwd(sv["qkv_pre"], lw["conv_qkv"], dqkv, heads)
    dsegs = (dqkv_pre, dz, dhp, dcb, dcc, dch, dab)
    dx0, d_w_in_p, d_norm1 = _in_proj_bwd(sv["x0"], after_token(tok, lw["norm1_g"]), lw["w_in_p"], dsegs, dx1, segs, tm)
    per = LANES // gd
    bd = d_pool_bd.reshape(dp // LANES, per, gd, per, gd)
    d_pool_w = jnp.stack([bd[gi // per, gi % per, :, gi % per, :] for gi in range(POOL_GROUPS)])
    big = dict(w_in=_w_in_unpack(d_w_in_p, gw["w_in"].shape[2], da, heads), w_gate=d_w_gate, w_up=d_w_up,
               ple_proj=d_ple_proj, w_out=d_w_out, w_down=d_w_down, ple_gate=d_ple_gate)
    small = dict(norm1_g=d_norm1[0], conv_qkv=d_conv_qkv, a_log=dpar[0, :heads], dt_bias=dpar[1, :heads], onorm_g=dpar[2],
                 pool_w=d_pool_w, pool_scale=d_pool_scale[0], sconv_w=d_sconv, norm2_g=d_norm2[0])
    tok = produced("end", dict(w_in=big["w_in"], w_out=d_w_out), big["w_in"])
    return dx0, big, small, tok


def _local_step(x, p, target, gw, small, produced=None, arrive=None):
    t, d = x.shape
    depth = p.shape[0]
    tm = 512 if t % 512 == 0 else 128
    layers = [_prepare_layer(small, li) for li in range(depth)]
    saved = []
    h = x
    for li in range(depth):
        h, sv = _layer_fwd(h, p[li], gw[li], layers[li], tm,
                           (lambda stage, after, li=li: arrive(li, stage, after)) if arrive else (lambda stage, after: None))
        saved.append(sv)
    dx, loss, d_final = _loss_head(h, target, small["final_g"][None], tm)
    big, sm = [None] * depth, [None] * depth
    token = None
    for li in reversed(range(depth)):
        p_li = p[li] if token is None else p[li] + token[0, 0]
        dx, big[li], sm[li], token = _layer_bwd(
            dx, p_li, gw[li], layers[li], saved[li], tm,
            (lambda stage, grads, after, li=li: produced(li, stage, grads, after)) if produced else (lambda *a: None))
    small_grads = {n: jnp.stack([g[n] for g in sm]) for n in sm[0]}
    small_grads["final_g"] = d_final[0]
    return loss[0, 0], dx, big, small_grads


def _coords():
    return lax.axis_index("x"), lax.axis_index("y"), lax.axis_index("c")


def _other_chips(x, y):
    return [(1 - x, y), (x, 1 - y), (1 - x, 1 - y)]


def _place_shards(ws, me_idx):
    nt = len(ws)
    depth = ws[0].shape[0]

    def body(me_ref, *refs):
        for t, w_ref in enumerate(refs[:nt]):
            for li in range(depth):
                refs[nt + li * nt + t][...] = w_ref[li].astype(WIRE_DTYPE)

    outs = pl.pallas_call(
        body, name="place_shards",
        grid_spec=pltpu.PrefetchScalarGridSpec(
            num_scalar_prefetch=1, grid=(4,),
            in_specs=[pl.BlockSpec((depth, w.shape[1] // 4, w.shape[2]), lambda i, me_ref: (0, i, 0)) for w in ws],
            out_specs=[pl.BlockSpec((None, w.shape[1] // 4, w.shape[2]), lambda i, me_ref: (me_ref[0], i, 0))
                       for _ in range(depth) for w in ws]),
        out_shape=[jax.ShapeDtypeStruct((4,) + w.shape[1:], WIRE_DTYPE) for _ in range(depth) for w in ws],
        compiler_params=_params("arbitrary"))(me_idx, *ws)
    return [list(outs[li * nt:(li + 1) * nt]) for li in range(depth)]


def _half_block(ref, chip, pc):
    rh = ref.shape[1] // 2
    return ref.at[chip, pl.ds(pc * rh, rh)]


def _gather_copies(out_refs, send_sems, recv_sems, stage):
    nt = len(out_refs)
    x, y, c = _coords()
    pairs = []
    for j, (cx, cy) in enumerate(_other_chips(x, y)):
        for t in range(nt):
            sems = dict(send_sem=send_sems[j * nt + t], recv_sem=recv_sems[j * nt + t], device_id_type=MESH)
            if stage == 0:
                mine, theirs, to = _half_block(out_refs[t], 2 * x + y, c), _half_block(out_refs[t], 2 * cx + cy, c), (cx, cy, c)
            else:
                mine, theirs, to = (_half_block(out_refs[t], 2 * cx + cy, c), _half_block(out_refs[t], 2 * cx + cy, 1 - c),
                                    (x, y, 1 - c))
            pairs.append((pltpu.make_async_remote_copy(src_ref=mine, dst_ref=mine, device_id=to, **sems),
                          pltpu.make_async_remote_copy(src_ref=theirs, dst_ref=theirs, device_id=to, **sems)))
    return pairs


def _all_gather_chips(placed):
    nt = len(placed)

    def body(*refs):
        out_refs = refs[nt:2 * nt]
        send_sems, recv_sems = refs[2 * nt:]
        nc = 3 * nt
        first = _gather_copies(out_refs, [send_sems.at[k] for k in range(nc)], [recv_sems.at[k] for k in range(nc)], 0)
        passed = _gather_copies(out_refs, [send_sems.at[nc + k] for k in range(nc)], [recv_sems.at[nc + k] for k in range(nc)], 1)
        for start, _ in first:
            start.start()
        for (_, arrival), (forward, _) in zip(first, passed):
            arrival.wait_recv()
            forward.start()
        for _, arrival in passed:
            arrival.wait_recv()
        for start, _ in first + passed:
            start.wait_send()

    return pl.pallas_call(
        body, name="all_gather_chips", out_shape=[jax.ShapeDtypeStruct(a.shape, a.dtype) for a in placed],
        in_specs=[ANY] * nt, out_specs=[ANY] * nt, input_output_aliases={t: t for t in range(nt)},
        scratch_shapes=[pltpu.SemaphoreType.DMA((6 * nt,)), pltpu.SemaphoreType.DMA((6 * nt,))],
    )(*placed)


def _gather_call(name, arrs, wait_sems, after, stage):
    nt = len(arrs)
    nc = 3 * nt
    n_wait = len(wait_sems)
    n_new = 2 * nc if stage < 2 else 0
    arrs = [pltpu.with_memory_space_constraint(a, pltpu.HBM) for a in arrs]

    def body(*refs):
        a_refs = refs[:nt]
        waits = refs[nt:nt + n_wait]
        news = refs[nt + n_wait + 1:nt + n_wait + 1 + n_new]
        token = refs[-1]
        if stage > 0:
            for start, arrival in _gather_copies(a_refs, waits[:nc], waits[nc:], stage - 1):
                start.wait_send()
                arrival.wait_recv()
        if stage < 2:
            for start, _ in _gather_copies(a_refs, news[:nc], news[nc:], stage):
                start.start()
        token[...] = jnp.zeros_like(token)

    outs = pl.pallas_call(
        body, name=name,
        out_shape=(*[pltpu.SemaphoreType.DMA(())] * n_new, *[pltpu.HBM(a.shape, a.dtype) for a in arrs],
                   jax.ShapeDtypeStruct((8, LANES), F32)),
        in_specs=[HBM] * nt + [SEM] * n_wait + [ANY],
        out_specs=(*[SEM] * n_new, *[HBM] * nt, pl.BlockSpec(memory_space=pltpu.VMEM)),
        input_output_aliases={t: n_new + t for t in range(nt)},
        compiler_params=pltpu.CompilerParams(has_side_effects=pltpu.SideEffectType.DATAFLOW_SIDE_EFFECTING),
    )(*arrs, *wait_sems, after)
    return list(outs[:n_new]), list(outs[n_new:n_new + nt]), outs[-1]


def _sibling_swap_half(gs):
    nt = len(gs)

    def body(*refs):
        g_refs, out_refs = refs[:nt], refs[nt:2 * nt]
        send_sems, recv_sems = refs[2 * nt:]
        x, y, c = _coords()
        cps = []
        for t in range(nt):
            rh = g_refs[t].shape[1] // 2
            cps.append(pltpu.make_async_remote_copy(src_ref=g_refs[t].at[:, pl.ds((1 - c) * rh, rh)], dst_ref=out_refs[t],
                                                    send_sem=send_sems.at[t], recv_sem=recv_sems.at[t], device_id=(x, y, 1 - c),
                                                    device_id_type=MESH))
        for cp in cps:
            cp.start()
        for cp in cps:
            cp.wait()

    return pl.pallas_call(
        body, name="sibling_swap_half",
        out_shape=[jax.ShapeDtypeStruct((g.shape[0], g.shape[1] // 2, g.shape[2]), g.dtype) for g in gs],
        in_specs=[ANY] * nt, out_specs=[ANY] * nt,
        scratch_shapes=[pltpu.SemaphoreType.DMA((nt,)), pltpu.SemaphoreType.DMA((nt,))])(*gs)


def _add_my_halves(gs, others, c_idx):
    nt = len(gs)

    def body(c_ref, *refs):
        for g_ref, o_ref, out_ref in zip(refs[:nt], refs[nt:2 * nt], refs[2 * nt:]):
            out_ref[...] = (g_ref[...].astype(F32) + o_ref[...].astype(F32)).astype(out_ref.dtype)

    def quarter(g):
        return pl.BlockSpec((None, g.shape[1] // 4, g.shape[2]), lambda j, i, c_ref: (j, i, 0))

    return pl.pallas_call(
        body, name="add_my_halves",
        grid_spec=pltpu.PrefetchScalarGridSpec(
            num_scalar_prefetch=1, grid=(4, 2),
            in_specs=[pl.BlockSpec((None, g.shape[1] // 4, g.shape[2]), lambda j, i, c_ref: (j, 2 * c_ref[0] + i, 0)) for g in gs]
                     + [quarter(g) for g in gs],
            out_specs=[quarter(g) for g in gs]),
        out_shape=[jax.ShapeDtypeStruct((4, g.shape[1] // 2, g.shape[2]), WIRE_DTYPE) for g in gs],
        compiler_params=_params("arbitrary", "arbitrary"))(c_idx, *gs, *others)


def _exchange_chips(parts):
    nt = len(parts)

    def body(*refs):
        p_refs, out_refs = refs[:nt], refs[nt:2 * nt]
        send_sems, recv_sems = refs[2 * nt:]
        x, y, c = _coords()
        chips = _other_chips(x, y)

        def copy(j, t):
            cx, cy = chips[j]
            return pltpu.make_async_remote_copy(src_ref=p_refs[t].at[2 * cx + cy], dst_ref=out_refs[t].at[j],
                                                send_sem=send_sems.at[j, t], recv_sem=recv_sems.at[j, t], device_id=(cx, cy, c),
                                                device_id_type=MESH)

        sends = [copy(j, t) for j in range(3) for t in range(nt)]
        for cp in sends:
            cp.start()
        for cp in sends:
            cp.wait_recv()
        for cp in sends:
            cp.wait_send()

    return pl.pallas_call(
        body, name="exchange_chips", out_shape=[jax.ShapeDtypeStruct((3,) + p.shape[1:], p.dtype) for p in parts],
        in_specs=[ANY] * nt, out_specs=[ANY] * nt,
        scratch_shapes=[pltpu.SemaphoreType.DMA((3, nt)), pltpu.SemaphoreType.DMA((3, nt))])(*parts)


def _split_plan(kind, s_refs, l_refs):
    x, y, c = _coords()
    if kind == "devices":
        peers = [(x ^ ((k >> 2) & 1), y ^ ((k >> 1) & 1), c ^ (k & 1)) for k in range(1, 8)]
        return [(s, l.at[4 * x + 2 * y + c], peer) for peer in peers for s, l in zip(s_refs, l_refs)]
    if kind == "swap":
        return [(s.at[:, pl.ds((1 - c) * (s.shape[1] // 2), s.shape[1] // 2)], l, (x, y, 1 - c)) for s, l in zip(s_refs, l_refs)]
    return [(s.at[2 * cx + cy], l.at[j], (cx, cy, c)) for j, (cx, cy) in enumerate(_other_chips(x, y))
            for s, l in zip(s_refs, l_refs)]


def _split_landing(kind, a):
    if kind == "devices":
        return (8,) + a.shape
    return (a.shape[0], a.shape[1] // 2, a.shape[2]) if kind == "swap" else (3,) + a.shape[1:]


def _copies_start(name, kind, srcs, after=None):
    ns = len(srcs)
    n = {"swap": 1, "exchange": 3, "devices": 7}[kind] * ns
    srcs = [pltpu.with_memory_space_constraint(a, pltpu.HBM) for a in srcs]
    fresh = jnp.zeros if kind == "devices" else lax.empty
    lands = [pltpu.with_memory_space_constraint(fresh(_split_landing(kind, a), a.dtype), pltpu.HBM) for a in srcs]
    extra = [] if after is None else [after]

    def body(*refs):
        first_sem = 2 * ns + len(extra)
        sems, token = refs[first_sem:first_sem + 2 * n], refs[-1]
        for k, (src, dst, dev) in enumerate(_split_plan(kind, refs[:ns], refs[ns:2 * ns])):
            pltpu.make_async_remote_copy(src_ref=src, dst_ref=dst, send_sem=sems[k], recv_sem=sems[n + k], device_id=dev,
                                         device_id_type=MESH).start()
        token[...] = jnp.zeros_like(token)

    outs = pl.pallas_call(
        body, name=name,
        out_shape=(*[pltpu.SemaphoreType.DMA(())] * (2 * n), *[pltpu.HBM(a.shape, a.dtype) for a in srcs + lands],
                   jax.ShapeDtypeStruct((8, LANES), F32)),
        in_specs=[HBM] * (2 * ns) + [ANY] * len(extra),
        out_specs=(*[SEM] * (2 * n), *[HBM] * (2 * ns), pl.BlockSpec(memory_space=pltpu.VMEM)),
        input_output_aliases={t: 2 * n + t for t in range(2 * ns)},
        compiler_params=pltpu.CompilerParams(has_side_effects=pltpu.SideEffectType.DATAFLOW_SIDE_EFFECTING),
    )(*srcs, *lands, *extra)
    return list(outs[:2 * n]), list(outs[2 * n:2 * n + ns]), list(outs[2 * n + ns:2 * n + 2 * ns]), outs[-1]


def _copies_wait(name, kind, sems, srcs, lands, after):
    ns = len(srcs)
    n = len(sems) // 2

    def body(*refs):
        sem_refs = refs[2 * ns:2 * ns + 2 * n]
        for k, (src, dst, dev) in enumerate(_split_plan(kind, refs[:ns], refs[ns:2 * ns])):
            cp = pltpu.make_async_remote_copy(src_ref=src, dst_ref=dst, send_sem=sem_refs[k], recv_sem=sem_refs[n + k],
                                              device_id=dev, device_id_type=MESH)
            cp.wait_send()
            cp.wait_recv()

    outs = pl.pallas_call(
        body, name=name, out_shape=tuple(pltpu.HBM(a.shape, a.dtype) for a in srcs + lands),
        in_specs=[HBM] * (2 * ns) + [SEM] * (2 * n) + [ANY], out_specs=tuple([HBM] * (2 * ns)),
        input_output_aliases={t: t for t in range(2 * ns)},
        compiler_params=pltpu.CompilerParams(has_side_effects=pltpu.SideEffectType.DATAFLOW_SIDE_EFFECTING),
    )(*srcs, *lands, *sems, after)
    return list(outs[:ns]), list(outs[ns:])


def _sum_into(pairs, recvs, idx, li, depth, accs):
    nt = len(pairs)

    def body(idx_ref, *refs):
        for p_ref, r_ref, out_ref in zip(refs[:nt], refs[nt:2 * nt], refs[-nt:]):
            out_ref[...] = p_ref[...].astype(F32) + r_ref[0].astype(F32) + r_ref[1].astype(F32) + r_ref[2].astype(F32)

    in_specs = ([pl.BlockSpec((None, p.shape[1] // 2, p.shape[2]), lambda i, idx_ref: (idx_ref[0], i, 0)) for p in pairs]
                + [pl.BlockSpec((3, p.shape[1] // 2, p.shape[2]), lambda i, idx_ref: (0, i, 0)) for p in pairs])
    args = [idx, *pairs, *recvs]
    aliases = {}
    if accs[0] is not None:
        in_specs += [ANY] * nt
        args += list(accs)
        aliases = {1 + 2 * nt + t: t for t in range(nt)}
    return pl.pallas_call(
        body, name="sum_into",
        grid_spec=pltpu.PrefetchScalarGridSpec(
            num_scalar_prefetch=1, grid=(2,), in_specs=in_specs,
            out_specs=[pl.BlockSpec((None, p.shape[1] // 2, p.shape[2]), lambda i, idx_ref: (li, 2 * idx_ref[1] + i, 0))
                       for p in pairs]),
        out_shape=[jax.ShapeDtypeStruct((depth, 2 * p.shape[1], p.shape[2]), F32) for p in pairs],
        input_output_aliases=aliases, compiler_params=_params("arbitrary"))(*args)


def _sum_devices(own, land, me_dev):
    rows = own.shape[0]
    tr = _tile_rows(rows, 512)

    def body(me_ref, o_ref, l_ref, out_ref):
        acc = jnp.where(me_ref[0] == 0, o_ref[...], l_ref[0])
        for s in range(1, 8):
            acc = acc + jnp.where(me_ref[0] == s, o_ref[...], l_ref[s])
        out_ref[...] = acc

    return pl.pallas_call(
        body, name="sum_devices",
        grid_spec=pltpu.PrefetchScalarGridSpec(
            num_scalar_prefetch=1, grid=(rows // tr,),
            in_specs=[pl.BlockSpec((tr, LANES), lambda i, me_ref: (i, 0)), pl.BlockSpec((8, tr, LANES), lambda i, me_ref: (0, i, 0))],
            out_specs=pl.BlockSpec((tr, LANES), lambda i, me_ref: (i, 0))),
        out_shape=jax.ShapeDtypeStruct((rows, LANES), F32), compiler_params=_params("arbitrary"))(me_dev, own, land)


def _sum_slots(parts):
    n, rows, cols = parts.shape
    tr = _tile_rows(rows, 512, SUBLANES_WIRE)

    def body(p_ref, out_ref):
        acc = p_ref[0].astype(F32)
        for s in range(1, n):
            acc = acc + p_ref[s].astype(F32)
        out_ref[...] = acc

    return pl.pallas_call(
        body, name="sum_slots", grid=(rows // tr,),
        in_specs=[pl.BlockSpec((n, tr, cols), lambda i: (0, i, 0))],
        out_specs=pl.BlockSpec((tr, cols), lambda i: (i, 0)),
        out_shape=jax.ShapeDtypeStruct((rows, cols), F32),
        compiler_params=_params("arbitrary"))(parts)


def _sibling_share(gs, li):
    nt = len(gs)

    def body(*refs):
        out_refs = refs[nt:2 * nt]
        send_sems, recv_sems = refs[2 * nt:]
        x, y, c = _coords()
        sends, recvs = [], []
        for t in range(nt):
            rh = out_refs[t].shape[1] // 2
            mine, theirs = out_refs[t].at[li, pl.ds(c * rh, rh)], out_refs[t].at[li, pl.ds((1 - c) * rh, rh)]
            sems = dict(send_sem=send_sems.at[t], recv_sem=recv_sems.at[t], device_id=(x, y, 1 - c), device_id_type=MESH)
            sends.append(pltpu.make_async_remote_copy(src_ref=mine, dst_ref=mine, **sems))
            recvs.append(pltpu.make_async_remote_copy(src_ref=theirs, dst_ref=theirs, **sems))
        for cp in sends:
            cp.start()
        for cp in recvs:
            cp.wait_recv()
        for cp in sends:
            cp.wait_send()

    return pl.pallas_call(
        body, name="sibling_share", out_shape=[jax.ShapeDtypeStruct(g.shape, g.dtype) for g in gs],
        in_specs=[ANY] * nt, out_specs=[ANY] * nt, input_output_aliases={t: t for t in range(nt)},
        scratch_shapes=[pltpu.SemaphoreType.DMA((nt,)), pltpu.SemaphoreType.DMA((nt,))])(*gs)


def _all_gather_devices(buf, after=None):
    extra = [] if after is None else [after]

    def body(b_ref, *rest):
        out_ref, send_sems, recv_sems, local_sem = rest[len(extra):]
        x, y, c = _coords()
        me = 4 * x + 2 * y + c
        mine = pltpu.make_async_copy(b_ref, out_ref.at[me], local_sem)
        mine.start()
        peers = []
        for k in range(1, 8):
            fx, fy, fc = (k >> 2) & 1, (k >> 1) & 1, k & 1
            peers.append((x ^ fx, y ^ fy, c ^ fc))
        sends = [pltpu.make_async_remote_copy(src_ref=b_ref, dst_ref=out_ref.at[me], send_sem=send_sems.at[k],
                                              recv_sem=recv_sems.at[k], device_id=peer, device_id_type=MESH)
                 for k, peer in enumerate(peers)]
        for cp in sends:
            cp.start()
        for k, (px, py, pc) in enumerate(peers):
            pltpu.make_async_remote_copy(src_ref=b_ref, dst_ref=out_ref.at[4 * px + 2 * py + pc], send_sem=send_sems.at[k],
                                         recv_sem=recv_sems.at[k], device_id=(px, py, pc), device_id_type=MESH).wait_recv()
        for cp in sends:
            cp.wait_send()
        mine.wait()

    return pl.pallas_call(
        body, name="all_gather_devices", out_shape=jax.ShapeDtypeStruct((8,) + buf.shape, buf.dtype),
        in_specs=[ANY] * (1 + len(extra)), out_specs=ANY,
        scratch_shapes=[pltpu.SemaphoreType.DMA((7,)), pltpu.SemaphoreType.DMA((7,)), pltpu.SemaphoreType.DMA(())])(buf, *extra)


def _pair_sums(big_grads, c_idx):
    gs = [big_grads[n] for n in BIG]
    return _add_my_halves(gs, _sibling_swap_half(gs), c_idx)


SMALL_SHARDED = ("conv_qkv", "sconv_w")
REPLICATED = ("norm1_g", "a_log", "dt_bias", "onorm_g", "pool_w", "pool_scale", "norm2_g", "final_g")
ALL_WEIGHTS = ("norm1_g", "w_in", "conv_qkv", "a_log", "dt_bias", "onorm_g", "pool_w", "pool_scale", "sconv_w", "w_out",
               "norm2_g", "w_gate", "w_up", "w_down", "ple_proj", "ple_gate", "final_g")


def _pad_rows(flat, row_multiple):
    m = flat.shape[0]
    r = -(-m // (LANES * row_multiple)) * row_multiple
    return jnp.pad(flat, (0, r * LANES - m)).reshape(r, LANES)


def _adamw_math(w, g, m, v):
    c1 = 1.0 / (1.0 - ADAM_B1 ** ADAM_STEP)
    c2 = 1.0 / (1.0 - ADAM_B2 ** ADAM_STEP)
    nm = ADAM_B1 * m + (1.0 - ADAM_B1) * g
    nv = ADAM_B2 * v + (1.0 - ADAM_B2) * (g * g)
    return -ADAM_LR * ((nm * c1) / (jnp.sqrt(nv * c2) + ADAM_EPS) + ADAM_WD * w), nm, nv


def _adamw(w, g, m, v):
    shape = w.shape
    cols = shape[-1]
    rows = w.size // cols
    tr = _tile_rows(rows, 512)

    def body(w_ref, g_ref, m_ref, v_ref, d_ref, nm_ref, nv_ref, go_ref):
        gv = g_ref[...]
        d_ref[...], nm_ref[...], nv_ref[...] = _adamw_math(w_ref[...], gv, m_ref[...], v_ref[...])
        go_ref[...] = gv

    spec = pl.BlockSpec((tr, cols), lambda i: (i, 0))
    outs = pl.pallas_call(
        body, name="adamw", grid=(rows // tr,), in_specs=[spec] * 4, out_specs=[spec] * 4,
        out_shape=[jax.ShapeDtypeStruct((rows, cols), F32)] * 4,
        compiler_params=_params("arbitrary"))(*[a.reshape(rows, cols) for a in (w, g, m, v)])
    return tuple(o.reshape(shape) for o in outs)


def kernel(x, p, norm1_g, w_in, conv_qkv, a_log, dt_bias, onorm_g, pool_w, pool_scale, sconv_w, w_out, norm2_g, w_gate, w_up, w_down, ple_proj, ple_gate, final_g, loss_target, m_norm1_g, m_w_in, m_conv_qkv, m_a_log, m_dt_bias, m_onorm_g, m_pool_w, m_pool_scale, m_sconv_w, m_w_out, m_norm2_g, m_w_gate, m_w_up, m_w_down, m_ple_proj, m_ple_gate, m_final_g, v_norm1_g, v_w_in, v_conv_qkv, v_a_log, v_dt_bias, v_onorm_g, v_pool_w, v_pool_scale, v_sconv_w, v_w_out, v_norm2_g, v_w_gate, v_w_up, v_w_down, v_ple_proj, v_ple_gate, v_final_g):
    weights = dict(zip(ALL_WEIGHTS, (norm1_g, w_in, conv_qkv, a_log, dt_bias, onorm_g, pool_w, pool_scale, sconv_w, w_out,
                                     norm2_g, w_gate, w_up, w_down, ple_proj, ple_gate, final_g)))
    mom_m = dict(zip(ALL_WEIGHTS, (m_norm1_g, m_w_in, m_conv_qkv, m_a_log, m_dt_bias, m_onorm_g, m_pool_w, m_pool_scale,
                                   m_sconv_w, m_w_out, m_norm2_g, m_w_gate, m_w_up, m_w_down, m_ple_proj, m_ple_gate, m_final_g)))
    mom_v = dict(zip(ALL_WEIGHTS, (v_norm1_g, v_w_in, v_conv_qkv, v_a_log, v_dt_bias, v_onorm_g, v_pool_w, v_pool_scale,
                                   v_sconv_w, v_w_out, v_norm2_g, v_w_gate, v_w_up, v_w_down, v_ple_proj, v_ple_gate, v_final_g)))
    for n in TRANSPOSED:
        weights[n], mom_m[n], mom_v[n] = (jnp.swapaxes(a[n], 1, 2) for a in (weights, mom_m, mom_v))
    c_idx = lax.axis_index("c").astype(jnp.int32).reshape(1)
    chip = (2 * lax.axis_index("x") + lax.axis_index("y")).astype(jnp.int32)
    me_idx = chip.reshape(1)
    idx = jnp.stack([chip, lax.axis_index("c").astype(jnp.int32)])
    depth = p.shape[0]

    small = {n: weights[n] for n in REPLICATED}
    sflat = _pad_rows(jnp.concatenate([weights[n].reshape(-1) for n in SMALL_SHARDED]), 8)
    sgath8 = _all_gather_devices(sflat)
    placed_in = _place_shards([weights["w_in"]], me_idx)
    sems, arrs, _ = _gather_call("gather_first_start", placed_in[0], [], sgath8, 0)
    placed_rest = _place_shards([weights[n] for n in BIG[1:]], me_idx)
    placed = [placed_in[li] + placed_rest[li] for li in range(depth)]
    sems, arrs, _ = _gather_call("gather_first_forward", arrs, sems, placed_rest[0][0], 1)
    _, arrs, token = _gather_call("gather_first_finish", arrs, sems, placed_rest[0][0], 2)
    gw = [dict() for _ in range(depth)]
    gw[0]["w_in"] = arrs[0]
    early = ("w_in", "w_out")
    late = tuple(n for n in BIG if n not in early)
    groups = [dict(li=0, names=BIG[1:], forward=(0, "mixed"), finish=(0, "mixed"))]
    for li in range(1, depth):
        groups.append(dict(li=li, names=early, forward=(li - 1, "ffn"), finish=(li - 1, "end")))
        groups.append(dict(li=li, names=late, forward=(li, "mixed"), finish=(li, "mixed")))
    def arrive(li, stage, after):
        for k, g in enumerate(groups):
            if g["forward"] == (li, stage):
                g["sems"], g["arrs"], _ = _gather_call("gather_forward_%d" % k, g["arrs"], g["sems"], after, 1)
            if g["finish"] == (li, stage):
                _, g["arrs"], _ = _gather_call("gather_finish_%d" % k, g["arrs"], g["sems"], after, 2)
                gw[g["li"]].update(zip(g["names"], g["arrs"]))

    sgath = sgath8[0::2].reshape(4, -1)
    off = 0
    for n in SMALL_SHARDED:
        shp = weights[n].shape
        part = sgath[:, off:off + weights[n].size].reshape((4,) + shp)
        small[n] = jnp.moveaxis(part, 0, -2).reshape(shp[:-1] + (4 * shp[-1],))
        off += weights[n].size
    for k, g in enumerate(groups):
        arrs = [placed[g["li"]][BIG.index(n)] for n in g["names"]]
        g["sems"], g["arrs"], token = _gather_call("gather_start_%d" % k, arrs, [], token, 0)

    small["norm1_g"] = small["norm1_g"] + token[0, 0]

    pending = []
    last_token = [None]

    def advance(g, after):
        if g["stage"] == 0:
            gs, others = _copies_wait("swap_wait_" + g["tag"], "swap", *g["handle"], after)
            g["handle"] = _copies_start("exchange_start_" + g["tag"], "exchange", _add_my_halves(gs, others, c_idx))
            g["stage"] = 1
            return g["handle"][3]
        return None

    held = {}

    def produced(li, stage, grads, after):
        token = None
        for g in pending:
            token = advance(g, after) if g["stage"] == 0 else token
        if li > 0 and stage != "end":
            held.update(grads)
            grads = {}
        elif li > 0:
            grads = {**held, **grads}
            held.clear()
        if grads:
            names = [n for n in BIG if n in grads]
            handle = _copies_start("swap_start_%d%s" % (li, stage), "swap", [grads[n] for n in names], token)
            pending.append(dict(li=li, names=names, tag="%d%s" % (li, stage), stage=0, handle=handle[:3]))
            token = handle[3]
        last_token[0] = last_token[0] if token is None else token
        return token

    loss_local, dx, _, small_grads = _local_step(x[0], p[:, 0], loss_target[0], gw, small, produced, arrive)
    rnames = REPLICATED + SMALL_SHARDED
    rflat = _pad_rows(jnp.concatenate([small_grads[n].reshape(-1) for n in rnames] + [loss_local.reshape(1)]), 8)
    small_handle = _copies_start("small_start", "devices", [rflat], last_token[0])
    accs, big_outs = {}, {}

    def finish(g, after):
        pairs, recvs = _copies_wait("exchange_wait_" + g["tag"], "exchange", *g["handle"][:3], after)
        summed = _sum_into(pairs, recvs, idx, g["li"], depth, [accs.get(n) for n in g["names"]])
        accs.update(zip(g["names"], _sibling_share(summed, g["li"])))
        return accs[g["names"][-1]]

    def update(names):
        for n in names:
            big_outs[n] = _adamw(weights[n], accs[n], mom_m[n], mom_v[n])
        return jnp.stack([big_outs[n][0].reshape(-1)[0] for n in names])

    done = finish(pending[0], small_handle[3])
    done = advance(pending[-1], done)
    for g in pending[1:-1]:
        done = finish(g, done)
    last = pending[-1]["names"]
    done = update([n for n in BIG if n not in last])
    finish(pending[-1], done)
    done = update(last)


    gshard = {}
    (own,), (land,) = _copies_wait("small_wait", "devices", *small_handle[:3], done)
    me_dev = (2 * chip + lax.axis_index("c").astype(jnp.int32)).reshape(1)
    rsum = _sum_devices(own, land, me_dev).reshape(-1)
    off = 0
    for n in rnames:
        whole = rsum[off:off + small_grads[n].size].reshape(small_grads[n].shape)
        off += small_grads[n].size
        if n in SMALL_SHARDED:
            cols = weights[n].shape[-1]
            whole = lax.dynamic_slice_in_dim(whole, chip * cols, cols, axis=whole.ndim - 1)
        gshard[n] = whole

    loss = rsum[off]

    deltas, new_m, new_v, grad_out = {}, {}, {}, {}
    for n in ALL_WEIGHTS:
        if n in BIG:
            deltas[n], new_m[n], new_v[n], grad_out[n] = big_outs[n]
        else:
            deltas[n], new_m[n], new_v[n], grad_out[n] = _adamw(weights[n], gshard[n], mom_m[n], mom_v[n])
    for n in TRANSPOSED:
        deltas[n], new_m[n], new_v[n], grad_out[n] = (jnp.swapaxes(a[n], 1, 2) for a in (deltas, new_m, new_v, grad_out))
    return (loss, dx[None], *[grad_out[n] for n in ALL_WEIGHTS], *[deltas[n] for n in ALL_WEIGHTS],
            *[new_m[n] for n in ALL_WEIGHTS], *[new_v[n] for n in ALL_WEIGHTS])
```

```python
import jax
import jax.numpy as jnp
from jax import lax
from jax.experimental import pallas as pl
from jax.experimental.pallas import tpu as pltpu

F32 = jnp.float32
MM_DTYPE = jnp.bfloat16
WIRE_DTYPE = jnp.bfloat16
EPS = 1e-6
HEAD_DIM = 128
CHUNK = 64
QKV_CONV_WIDTH = 4
SCONV_WIDTH = 3
POOL_GROUPS = 4
LANES = 128
SUBLANES_WIRE = 16
VMEM_LIMIT_BYTES = 56 * 1024 * 1024
ADAM_LR, ADAM_B1, ADAM_B2, ADAM_EPS, ADAM_WD, ADAM_STEP = 0.001, 0.9, 0.999, 1e-08, 0.01, 10
MESH = pl.DeviceIdType.MESH
ANY = pl.BlockSpec(memory_space=pl.ANY)
HBM = pl.BlockSpec(memory_space=pltpu.HBM)
SEM = pl.BlockSpec(memory_space=pltpu.SEMAPHORE)


def _params(*sem):
    return pltpu.CompilerParams(vmem_limit_bytes=VMEM_LIMIT_BYTES, dimension_semantics=sem if sem else None)


def _mm(a, b):
    return jnp.dot(a.astype(MM_DTYPE), b.astype(MM_DTYPE), preferred_element_type=F32)


def _mm_nt(a, b):
    return lax.dot_general(a.astype(MM_DTYPE), b.astype(MM_DTYPE), (((1,), (1,)), ((), ())), preferred_element_type=F32)


def _mm_tn(a, b):
    return lax.dot_general(a.astype(MM_DTYPE), b.astype(MM_DTYPE), (((0,), (0,)), ((), ())), preferred_element_type=F32)


def _sigmoid(x):
    return 1.0 / (1.0 + jnp.exp(-x))


def _dsilu(x, s):
    return s * (1.0 + x * (1.0 - s))


def _rows(shape):
    return lax.broadcasted_iota(jnp.int32, shape, 0)


def _shift_down(x, s):
    if s == 0:
        return x
    return jnp.where(_rows(x.shape) >= s, pltpu.roll(x, s, 0), 0.0)


def _shift_up(x, s):
    if s == 0:
        return x
    t = x.shape[0]
    return jnp.where(_rows(x.shape) < t - s, pltpu.roll(x, t - s, 0), 0.0)


def _rms_fwd(x):
    r = lax.rsqrt(jnp.mean(x * x, axis=-1, keepdims=True) + EPS)
    return x * r, r


def _rms_bwd(dxn, xn, r):
    return r * (dxn - xn * jnp.mean(dxn * xn, axis=-1, keepdims=True))


def _tile_rows(n, cap, mult=8):
    best = None
    for d in range(mult, min(n, cap) + 1, mult):
        if n % d == 0:
            best = d
    return best if best is not None else n


def _in_proj_fwd(x, g1, wp, segs, tm):
    t, d = x.shape
    npk = wp.shape[1]

    def body(x_ref, g_ref, w_ref, *o_refs):
        xn, _ = _rms_fwd(x_ref[...])
        h = (xn * g_ref[...]).astype(w_ref.dtype)
        off = 0
        for o_ref, wd in zip(o_refs, segs):
            o_ref[...] = jnp.dot(h, w_ref[:, off:off + wd], preferred_element_type=F32)
            off += wd

    return pl.pallas_call(
        body, name="in_proj_fwd", grid=(t // tm,),
        in_specs=[pl.BlockSpec((tm, d), lambda i: (i, 0)), pl.BlockSpec((1, d), lambda i: (0, 0)),
                  pl.BlockSpec((d, npk), lambda i: (0, 0))],
        out_specs=[pl.BlockSpec((tm, wd), lambda i: (i, 0)) for wd in segs],
        out_shape=[jax.ShapeDtypeStruct((t, wd), F32) for wd in segs],
        compiler_params=_params("arbitrary"))(x, g1, wp)


def _in_proj_bwd(x, g1, wp, dsegs, dx_res, segs, tm):
    t, d = x.shape
    npk = wp.shape[1]
    nseg = len(segs)

    def body(x_ref, g_ref, w_ref, *rest):
        ds_refs = rest[:nseg]
        dxr_ref, dx_ref, dw_ref, dg_ref = rest[nseg:]
        i = pl.program_id(0)

        @pl.when(i == 0)
        def _():
            dw_ref[...] = jnp.zeros_like(dw_ref)
            dg_ref[...] = jnp.zeros_like(dg_ref)

        xn, r = _rms_fwd(x_ref[...])
        g = g_ref[...]
        h = (xn * g).astype(w_ref.dtype)
        dh = jnp.zeros((tm, d), F32)
        off = 0
        for ds_ref, wd in zip(ds_refs, segs):
            dsv = ds_ref[...].astype(w_ref.dtype)
            dh = dh + lax.dot_general(dsv, w_ref[:, off:off + wd], (((1,), (1,)), ((), ())), preferred_element_type=F32)
            dw_ref[:, off:off + wd] += lax.dot_general(h, dsv, (((0,), (0,)), ((), ())), preferred_element_type=F32)
            off += wd
        dg_ref[...] += jnp.sum(dh * xn, axis=0, keepdims=True)
        dx_ref[...] = dxr_ref[...] + _rms_bwd(dh * g, xn, r)

    return pl.pallas_call(
        body, name="in_proj_bwd", grid=(t // tm,),
        in_specs=[pl.BlockSpec((tm, d), lambda i: (i, 0)), pl.BlockSpec((1, d), lambda i: (0, 0)),
                  pl.BlockSpec((d, npk), lambda i: (0, 0))]
                 + [pl.BlockSpec((tm, wd), lambda i: (i, 0)) for wd in segs]
                 + [pl.BlockSpec((tm, d), lambda i: (i, 0))],
        out_specs=[pl.BlockSpec((tm, d), lambda i: (i, 0)), pl.BlockSpec((d, npk), lambda i: (0, 0)),
                   pl.BlockSpec((1, d), lambda i: (0, 0))],
        out_shape=[jax.ShapeDtypeStruct((t, d), F32), jax.ShapeDtypeStruct((d, npk), F32),
                   jax.ShapeDtypeStruct((1, d), F32)],
        compiler_params=_params("arbitrary"))(x, g1, wp, *dsegs, dx_res)


def _out_proj_fwd(x0, mix, wo, g2, tm):
    t, d = x0.shape
    dq = wo.shape[1]
    widths = [m.shape[1] for m in mix]

    def body(x_ref, *rest):
        m_refs = rest[:len(mix)]
        w_ref, g_ref, x1_ref, h2_ref = rest[len(mix):]
        acc = x_ref[...]
        off = 0
        for m_ref, wd in zip(m_refs, widths):
            for k in range(wd // dq):
                acc = acc + jnp.dot(m_ref[:, k * dq:(k + 1) * dq].astype(w_ref.dtype), w_ref[off // dq + k],
                                    preferred_element_type=F32)
            off += wd
        x1_ref[...] = acc
        xn, _ = _rms_fwd(acc)
        h2_ref[...] = (xn * g_ref[...]).astype(h2_ref.dtype)

    return pl.pallas_call(
        body, name="out_proj_fwd", grid=(t // tm,),
        in_specs=[pl.BlockSpec((tm, d), lambda i: (i, 0))]
                 + [pl.BlockSpec((tm, wd), lambda i: (i, 0)) for wd in widths]
                 + [pl.BlockSpec((4, dq, d), lambda i: (0, 0, 0)), pl.BlockSpec((1, d), lambda i: (0, 0))],
        out_specs=[pl.BlockSpec((tm, d), lambda i: (i, 0)), pl.BlockSpec((tm, d), lambda i: (i, 0))],
        out_shape=[jax.ShapeDtypeStruct((t, d), F32), jax.ShapeDtypeStruct((t, d), MM_DTYPE)],
        compiler_params=_params("arbitrary"))(x0, *mix, wo, g2)


def _out_proj_bwd(dx2, dh2, x1, g2, mix, wo, tm):
    t, d = x1.shape
    dq = wo.shape[1]
    widths = [m.shape[1] for m in mix]
    nm = len(mix)

    def body(dx2_ref, dh2_ref, x1_ref, g_ref, *rest):
        m_refs = rest[:nm]
        w_ref = rest[nm]
        dx1_ref = rest[nm + 1]
        dm_refs = rest[nm + 2:nm + 2 + nm]
        dw_ref, dg_ref = rest[nm + 2 + nm:]
        i = pl.program_id(0)

        @pl.when(i == 0)
        def _():
            dw_ref[...] = jnp.zeros_like(dw_ref)
            dg_ref[...] = jnp.zeros_like(dg_ref)

        xn, r = _rms_fwd(x1_ref[...])
        dh2v = dh2_ref[...]
        dg_ref[...] += jnp.sum(dh2v * xn, axis=0, keepdims=True)
        dx1 = dx2_ref[...] + _rms_bwd(dh2v * g_ref[...], xn, r)
        dx1_ref[...] = dx1
        dx1c = dx1.astype(w_ref.dtype)
        off = 0
        for m_ref, dm_ref, wd in zip(m_refs, dm_refs, widths):
            for k in range(wd // dq):
                j = off // dq + k
                cols = slice(k * dq, (k + 1) * dq)
                dm_ref[:, cols] = lax.dot_general(dx1c, w_ref[j], (((1,), (1,)), ((), ())), preferred_element_type=F32)
                dw_ref[j] += lax.dot_general(m_ref[:, cols].astype(w_ref.dtype), dx1c, (((0,), (0,)), ((), ())),
                                             preferred_element_type=F32)
            off += wd

    tile = lambda wd: pl.BlockSpec((tm, wd), lambda i: (i, 0))
    return pl.pallas_call(
        body, name="out_proj_bwd", grid=(t // tm,),
        in_specs=[tile(d), tile(d), tile(d), pl.BlockSpec((1, d), lambda i: (0, 0))]
                 + [tile(wd) for wd in widths] + [pl.BlockSpec((4, dq, d), lambda i: (0, 0, 0))],
        out_specs=[tile(d)] + [tile(wd) for wd in widths]
                  + [pl.BlockSpec((4, dq, d), lambda i: (0, 0, 0)), pl.BlockSpec((1, d), lambda i: (0, 0))],
        out_shape=[jax.ShapeDtypeStruct((t, d), F32)] + [jax.ShapeDtypeStruct((t, wd), F32) for wd in widths]
                  + [jax.ShapeDtypeStruct((4, dq, d), F32), jax.ShapeDtypeStruct((1, d), F32)],
        compiler_params=_params("arbitrary"))(dx2, dh2, x1, g2, *mix, wo)


def _ffn_fwd(x1, h2, wg, wu, wd, tm):
    t, d = x1.shape
    fs = wg.shape[1]

    def body(x1_ref, h2_ref, wg_ref, wu_ref, wd_ref, x2_ref, gp_ref, up_ref):
        j, i = pl.program_id(0), pl.program_id(1)
        h = h2_ref[...]
        nt = (((1,), (1,)), ((), ()))
        gp = lax.dot_general(h, wg_ref[...], nt, preferred_element_type=F32)
        up = lax.dot_general(h, wu_ref[...], nt, preferred_element_type=F32)
        gp_ref[...] = gp
        up_ref[...] = up
        ff = gp * _sigmoid(gp) * up
        y = jnp.dot(ff.astype(wd_ref.dtype), wd_ref[...], preferred_element_type=F32)
        rows = pl.ds(pl.multiple_of(i * tm, tm), tm)

        @pl.when(j == 0)
        def _():
            x2_ref[rows, :] = x1_ref[...] + y

        @pl.when(j != 0)
        def _():
            x2_ref[rows, :] += y

    return pl.pallas_call(
        body, name="ffn_fwd", grid=(4, t // tm),
        in_specs=[pl.BlockSpec((tm, d), lambda j, i: (jnp.where(j == 0, i, 0), 0)),
                  pl.BlockSpec((tm, d), lambda j, i: (i, 0)),
                  pl.BlockSpec((None, fs, d), lambda j, i: (j, 0, 0)),
                  pl.BlockSpec((None, fs, d), lambda j, i: (j, 0, 0)),
                  pl.BlockSpec((None, fs, d), lambda j, i: (j, 0, 0))],
        out_specs=[pl.BlockSpec((t, d), lambda j, i: (0, 0)), pl.BlockSpec((None, tm, fs), lambda j, i: (j, i, 0)),
                   pl.BlockSpec((None, tm, fs), lambda j, i: (j, i, 0))],
        out_shape=[jax.ShapeDtypeStruct((t, d), F32), jax.ShapeDtypeStruct((4, t, fs), F32),
                   jax.ShapeDtypeStruct((4, t, fs), F32)],
        compiler_params=_params("arbitrary", "arbitrary"))(x1, h2, wg, wu, wd)


def _ffn_bwd(dx2, h2, gp, up, wg, wu, wd, tm):
    t, d = dx2.shape
    fs = wg.shape[1]

    def body(dx2_ref, h2_ref, gp_ref, up_ref, wg_ref, wu_ref, wd_ref, dh2_ref, dwg_ref, dwu_ref, dwd_ref):
        j, i = pl.program_id(0), pl.program_id(1)

        @pl.when(i == 0)
        def _():
            dwg_ref[...] = jnp.zeros_like(dwg_ref)
            dwu_ref[...] = jnp.zeros_like(dwu_ref)
            dwd_ref[...] = jnp.zeros_like(dwd_ref)

        cdt = wg_ref.dtype
        h = h2_ref[...]
        gpv, upv = gp_ref[...], up_ref[...]
        s = _sigmoid(gpv)
        silu = gpv * s
        dx2c = dx2_ref[...].astype(cdt)
        dff = lax.dot_general(dx2c, wd_ref[...], (((1,), (1,)), ((), ())), preferred_element_type=F32)
        dwd_ref[...] += lax.dot_general((silu * upv).astype(cdt), dx2c, (((0,), (0,)), ((), ())), preferred_element_type=F32)
        dup = (dff * silu).astype(cdt)
        dgp = (dff * upv * _dsilu(gpv, s)).astype(cdt)
        dwg_ref[...] += lax.dot_general(dgp, h, (((0,), (0,)), ((), ())), preferred_element_type=F32)
        dwu_ref[...] += lax.dot_general(dup, h, (((0,), (0,)), ((), ())), preferred_element_type=F32)
        dh = (jnp.dot(dgp, wg_ref[...], preferred_element_type=F32) + jnp.dot(dup, wu_ref[...], preferred_element_type=F32))
        rows = pl.ds(pl.multiple_of(i * tm, tm), tm)

        @pl.when(j == 0)
        def _():
            dh2_ref[rows, :] = dh

        @pl.when(j != 0)
        def _():
            dh2_ref[rows, :] += dh

    return pl.pallas_call(
        body, name="ffn_bwd", grid=(4, t // tm),
        in_specs=[pl.BlockSpec((tm, d), lambda j, i: (i, 0)), pl.BlockSpec((tm, d), lambda j, i: (i, 0)),
                  pl.BlockSpec((None, tm, fs), lambda j, i: (j, i, 0)), pl.BlockSpec((None, tm, fs), lambda j, i: (j, i, 0)),
                  pl.BlockSpec((None, fs, d), lambda j, i: (j, 0, 0)),
                  pl.BlockSpec((None, fs, d), lambda j, i: (j, 0, 0)),
                  pl.BlockSpec((None, fs, d), lambda j, i: (j, 0, 0))],
        out_specs=[pl.BlockSpec((t, d), lambda j, i: (0, 0)), pl.BlockSpec((None, fs, d), lambda j, i: (j, 0, 0)),
                   pl.BlockSpec((None, fs, d), lambda j, i: (j, 0, 0)), pl.BlockSpec((None, fs, d), lambda j, i: (j, 0, 0))],
        out_shape=[jax.ShapeDtypeStruct((t, d), F32)] + [jax.ShapeDtypeStruct((4, fs, d), F32)] * 3,
        compiler_params=_params("arbitrary", "arbitrary"))(dx2, h2, gp, up, wg, wu, wd)


def _ple_fwd(x2, p, wpg, wpp, tm):
    t, d = x2.shape
    q = p.shape[1]
    dq = d // 4

    def body(x_ref, p_ref, wg_ref, wp_ref, o_ref):
        xv = x_ref[...]
        xc = xv.astype(wg_ref.dtype)
        pc = p_ref[...].astype(wp_ref.dtype)
        pre = jnp.dot(xc[:, :dq], wg_ref[0], preferred_element_type=F32)
        for j in range(1, 4):
            pre = pre + jnp.dot(xc[:, j * dq:(j + 1) * dq], wg_ref[j], preferred_element_type=F32)
        gate = _sigmoid(pre)
        for j in range(4):
            cols = slice(j * dq, (j + 1) * dq)
            o_ref[:, cols] = xv[:, cols] + gate[:, cols] * jnp.dot(pc, wp_ref[j], preferred_element_type=F32)

    return pl.pallas_call(
        body, name="ple_fwd", grid=(t // tm,),
        in_specs=[pl.BlockSpec((tm, d), lambda i: (i, 0)), pl.BlockSpec((tm, q), lambda i: (i, 0)),
                  pl.BlockSpec((4, dq, d), lambda i: (0, 0, 0)),
                  pl.BlockSpec((4, q, dq), lambda i: (0, 0, 0))],
        out_specs=pl.BlockSpec((tm, d), lambda i: (i, 0)),
        out_shape=jax.ShapeDtypeStruct((t, d), F32),
        compiler_params=_params("arbitrary"))(x2, p, wpg, wpp)


def _ple_bwd(dx3, x2, p, wpg, wpp, tm):
    t, d = x2.shape
    q = p.shape[1]
    dq = d // 4

    def body(dx3_ref, x_ref, p_ref, wg_ref, wp_ref, dx2_ref, dwg_ref, dwp_ref):
        @pl.when(pl.program_id(0) == 0)
        def _():
            dwg_ref[...] = jnp.zeros_like(dwg_ref)
            dwp_ref[...] = jnp.zeros_like(dwp_ref)

        cdt = wg_ref.dtype
        xc = x_ref[...].astype(cdt)
        pc = p_ref[...].astype(cdt)
        pre = jnp.dot(xc[:, :dq], wg_ref[0], preferred_element_type=F32)
        for j in range(1, 4):
            pre = pre + jnp.dot(xc[:, j * dq:(j + 1) * dq], wg_ref[j], preferred_element_type=F32)
        gate = _sigmoid(pre)
        dx3v = dx3_ref[...]
        dpp = (dx3v * gate).astype(cdt)
        dgate = dx3v * gate * (1.0 - gate)
        dpre_parts = []
        for j in range(4):
            cols = slice(j * dq, (j + 1) * dq)
            pp_j = jnp.dot(pc, wp_ref[j], preferred_element_type=F32)
            dpre_parts.append((dgate[:, cols] * pp_j).astype(cdt))
            dwp_ref[j] += lax.dot_general(pc, dpp[:, cols], (((0,), (0,)), ((), ())), preferred_element_type=F32)
        dpre = jnp.concatenate(dpre_parts, axis=1)
        for j in range(4):
            cols = slice(j * dq, (j + 1) * dq)
            dwg_ref[j] += lax.dot_general(xc[:, cols], dpre, (((0,), (0,)), ((), ())), preferred_element_type=F32)
            dx2_ref[:, cols] = dx3v[:, cols] + lax.dot_general(dpre, wg_ref[j], (((1,), (1,)), ((), ())),
                                                               preferred_element_type=F32)

    return pl.pallas_call(
        body, name="ple_bwd", grid=(t // tm,),
        in_specs=[pl.BlockSpec((tm, d), lambda i: (i, 0)), pl.BlockSpec((tm, d), lambda i: (i, 0)),
                  pl.BlockSpec((tm, q), lambda i: (i, 0)), pl.BlockSpec((4, dq, d), lambda i: (0, 0, 0)),
                  pl.BlockSpec((4, q, dq), lambda i: (0, 0, 0))],
        out_specs=[pl.BlockSpec((tm, d), lambda i: (i, 0)), pl.BlockSpec((4, dq, d), lambda i: (0, 0, 0)),
                   pl.BlockSpec((4, q, dq), lambda i: (0, 0, 0))],
        out_shape=[jax.ShapeDtypeStruct((t, d), F32), jax.ShapeDtypeStruct((4, dq, d), F32),
                   jax.ShapeDtypeStruct((4, q, dq), F32)],
        compiler_params=_params("arbitrary"))(dx3, x2, p, wpg, wpp)


def _loss_head(x, target, fg, tm):
    t, d = x.shape

    def body(x_ref, t_ref, g_ref, dx_ref, loss_ref, dg_ref):
        @pl.when(pl.program_id(0) == 0)
        def _():
            loss_ref[...] = jnp.zeros_like(loss_ref)
            dg_ref[...] = jnp.zeros_like(dg_ref)

        xn, r = _rms_fwd(x_ref[...])
        g = g_ref[...]
        err = xn * g - t_ref[...]
        loss_ref[...] += 0.5 * jnp.sum(jnp.sum(err * err, axis=-1, keepdims=True) / d, axis=0, keepdims=True)
        dy = err / d
        dg_ref[...] += jnp.sum(dy * xn, axis=0, keepdims=True)
        dx_ref[...] = _rms_bwd(dy * g, xn, r)

    return pl.pallas_call(
        body, name="loss_head", grid=(t // tm,),
        in_specs=[pl.BlockSpec((tm, d), lambda i: (i, 0)), pl.BlockSpec((tm, d), lambda i: (i, 0)),
                  pl.BlockSpec((1, d), lambda i: (0, 0))],
        out_specs=[pl.BlockSpec((tm, d), lambda i: (i, 0)), pl.BlockSpec((1, 1), lambda i: (0, 0)),
                   pl.BlockSpec((1, d), lambda i: (0, 0))],
        out_shape=[jax.ShapeDtypeStruct((t, d), F32), jax.ShapeDtypeStruct((1, 1), F32),
                   jax.ShapeDtypeStruct((1, d), F32)],
        compiler_params=_params("arbitrary"))(x, target, fg)


def _qkv_conv_act(xv, w, j, heads):
    k = QKV_CONV_WIDTH
    y = w[k - 1:k] * xv
    for s in range(1, k):
        y = y + w[k - 1 - s:k - s] * _shift_down(xv, s)
    sg = _sigmoid(y)
    s_act = y * sg
    nrm = lax.rsqrt(jnp.sum(s_act * s_act, axis=-1, keepdims=True) + EPS)
    scale = jnp.where(j < heads, HEAD_DIM ** -0.5, 1.0).astype(F32)
    return y, sg, s_act, nrm, scale


def _qkv_conv_fwd(qkv_pre, conv_w, heads):
    t = qkv_pre.shape[0]
    nblk = 3 * heads

    def body(x_ref, w_ref, o_ref):
        j = pl.program_id(0)
        _, _, s_act, nrm, scale = _qkv_conv_act(x_ref[...], w_ref[...], j, heads)
        o_ref[...] = jnp.where(j < 2 * heads, s_act * (nrm * scale), s_act)

    return pl.pallas_call(
        body, name="qkv_conv_fwd", grid=(nblk,),
        in_specs=[pl.BlockSpec((t, LANES), lambda j: (0, j)), pl.BlockSpec((QKV_CONV_WIDTH, LANES), lambda j: (0, j))],
        out_specs=pl.BlockSpec((t, LANES), lambda j: (0, j)),
        out_shape=jax.ShapeDtypeStruct(qkv_pre.shape, F32),
        compiler_params=_params("arbitrary"))(qkv_pre, conv_w)


def _qkv_conv_bwd(qkv_pre, conv_w, dqkv, heads):
    t = qkv_pre.shape[0]
    nblk = 3 * heads
    k = QKV_CONV_WIDTH

    def body(x_ref, w_ref, dn_ref, dx_ref, dw_ref):
        j = pl.program_id(0)
        xv, w = x_ref[...], w_ref[...]
        y, sg, s_act, nrm, scale = _qkv_conv_act(xv, w, j, heads)
        dn = dn_ref[...]
        dsn = dn * scale
        ds_qk = nrm * dsn - s_act * (nrm * nrm * nrm) * jnp.sum(dsn * s_act, axis=-1, keepdims=True)
        ds = jnp.where(j < 2 * heads, ds_qk, dn)
        dy = ds * _dsilu(y, sg)
        dx = w[k - 1:k] * dy
        dw_ref[k - 1:k, :] = jnp.sum(dy * xv, axis=0, keepdims=True)
        for s in range(1, k):
            dx = dx + w[k - 1 - s:k - s] * _shift_up(dy, s)
            dw_ref[k - 1 - s:k - s, :] = jnp.sum(dy * _shift_down(xv, s), axis=0, keepdims=True)
        dx_ref[...] = dx

    return pl.pallas_call(
        body, name="qkv_conv_bwd", grid=(nblk,),
        in_specs=[pl.BlockSpec((t, LANES), lambda j: (0, j)), pl.BlockSpec((k, LANES), lambda j: (0, j)),
                  pl.BlockSpec((t, LANES), lambda j: (0, j))],
        out_specs=[pl.BlockSpec((t, LANES), lambda j: (0, j)), pl.BlockSpec((k, LANES), lambda j: (0, j))],
        out_shape=[jax.ShapeDtypeStruct(qkv_pre.shape, F32), jax.ShapeDtypeStruct(conv_w.shape, F32)],
        compiler_params=_params("arbitrary"))(qkv_pre, conv_w, dqkv)


def _pool_windows(shape, j, group_dim):
    lane = lax.broadcasted_iota(jnp.int32, shape, 1) + j * LANES
    grp = lane // group_dim
    win = jnp.left_shift(2, grp).astype(F32)
    cnt = jnp.minimum((_rows(shape) + 1).astype(F32), win)
    return grp, cnt


def _pool_select(grp, levels):
    out = levels[0]
    for gi in range(1, POOL_GROUPS):
        out = jnp.where(grp == gi, levels[gi], out)
    return out


def _pool_mean(hv, grp, cnt):
    acc, levels, width = hv, [], 1
    for _ in range(POOL_GROUPS):
        acc = acc + _shift_down(acc, width)
        width *= 2
        levels.append(acc)
    return _pool_select(grp, levels) / cnt - hv


def _pool_fwd(hp, wbd, scale, group_dim):
    t, dp = hp.shape

    def body(h_ref, w_ref, s_ref, o_ref):
        hv = h_ref[...]
        grp, cnt = _pool_windows(hv.shape, pl.program_id(0), group_dim)
        pooled = _pool_mean(hv, grp, cnt)
        o_ref[...] = _mm(pooled, w_ref[...]) * s_ref[...]

    return pl.pallas_call(
        body, name="pool_fwd", grid=(dp // LANES,),
        in_specs=[pl.BlockSpec((t, LANES), lambda j: (0, j)), pl.BlockSpec((LANES, LANES), lambda j: (j, j)),
                  pl.BlockSpec((1, LANES), lambda j: (0, j))],
        out_specs=pl.BlockSpec((t, LANES), lambda j: (0, j)),
        out_shape=jax.ShapeDtypeStruct(hp.shape, F32),
        compiler_params=_params("arbitrary"))(hp, wbd, scale)


def _pool_bwd(hp, wbd, scale, dob, group_dim):
    t, dp = hp.shape

    def body(h_ref, w_ref, s_ref, do_ref, dh_ref, dw_ref, ds_ref):
        hv = h_ref[...]
        grp, cnt = _pool_windows(hv.shape, pl.program_id(0), group_dim)
        pooled = _pool_mean(hv, grp, cnt)
        wv = w_ref[...]
        dov = do_ref[...]
        ds_ref[...] = jnp.sum(dov * _mm(pooled, wv), axis=0, keepdims=True)
        dys = dov * s_ref[...]
        dw_ref[0] = _mm_tn(pooled, dys)
        dpooled = _mm_nt(dys, wv)
        acc, levels, width = dpooled / cnt, [], 1
        for _ in range(POOL_GROUPS):
            acc = acc + _shift_up(acc, width)
            width *= 2
            levels.append(acc)
        dh_ref[...] = _pool_select(grp, levels) - dpooled

    nb = dp // LANES
    return pl.pallas_call(
        body, name="pool_bwd", grid=(nb,),
        in_specs=[pl.BlockSpec((t, LANES), lambda j: (0, j)), pl.BlockSpec((LANES, LANES), lambda j: (j, j)),
                  pl.BlockSpec((1, LANES), lambda j: (0, j)), pl.BlockSpec((t, LANES), lambda j: (0, j))],
        out_specs=[pl.BlockSpec((t, LANES), lambda j: (0, j)), pl.BlockSpec((1, LANES, LANES), lambda j: (j, 0, 0)),
                   pl.BlockSpec((1, LANES), lambda j: (0, j))],
        out_shape=[jax.ShapeDtypeStruct(hp.shape, F32), jax.ShapeDtypeStruct((nb, LANES, LANES), F32),
                   jax.ShapeDtypeStruct((1, dp), F32)],
        compiler_params=_params("arbitrary"))(hp, wbd, scale, dob)


def _sconv_fwd(cbcch, w):
    t, dc3 = cbcch.shape
    nb = dc3 // 3 // LANES
    k = SCONV_WIDTH

    def body(b_ref, c_ref, h_ref, w_ref, o_ref):
        m = c_ref[...] * h_ref[...]
        wv = w_ref[...]
        y = wv[k - 1:k] * m
        for s in range(1, k):
            y = y + wv[k - 1 - s:k - s] * _shift_down(m, s)
        o_ref[...] = b_ref[...] * y

    return pl.pallas_call(
        body, name="sconv_fwd", grid=(nb,),
        in_specs=[pl.BlockSpec((t, LANES), lambda j: (0, j)), pl.BlockSpec((t, LANES), lambda j: (0, nb + j)),
                  pl.BlockSpec((t, LANES), lambda j: (0, 2 * nb + j)), pl.BlockSpec((k, LANES), lambda j: (0, j))],
        out_specs=pl.BlockSpec((t, LANES), lambda j: (0, j)),
        out_shape=jax.ShapeDtypeStruct((t, dc3 // 3), F32),
        compiler_params=_params("arbitrary"))(cbcch, cbcch, cbcch, w)


def _sconv_bwd(cbcch, w, doc):
    t, dc3 = cbcch.shape
    nb = dc3 // 3 // LANES
    k = SCONV_WIDTH

    def body(b_ref, c_ref, h_ref, w_ref, do_ref, db_ref, dc_ref, dh_ref, dw_ref):
        cv, hv = c_ref[...], h_ref[...]
        m = cv * hv
        wv = w_ref[...]
        dov = do_ref[...]
        dy = dov * b_ref[...]
        y = wv[k - 1:k] * m
        dm = wv[k - 1:k] * dy
        dw_ref[k - 1:k, :] = jnp.sum(dy * m, axis=0, keepdims=True)
        for s in range(1, k):
            ms = _shift_down(m, s)
            y = y + wv[k - 1 - s:k - s] * ms
            dm = dm + wv[k - 1 - s:k - s] * _shift_up(dy, s)
            dw_ref[k - 1 - s:k - s, :] = jnp.sum(dy * ms, axis=0, keepdims=True)
        db_ref[...] = dov * y
        dc_ref[...] = dm * hv
        dh_ref[...] = dm * cv

    col = lambda o: pl.BlockSpec((t, LANES), lambda j: (0, o * nb + j))
    return pl.pallas_call(
        body, name="sconv_bwd", grid=(nb,),
        in_specs=[col(0), col(1), col(2), pl.BlockSpec((k, LANES), lambda j: (0, j)), col(0)],
        out_specs=[col(0), col(0), col(0), pl.BlockSpec((k, LANES), lambda j: (0, j))],
        out_shape=[jax.ShapeDtypeStruct((t, dc3 // 3), F32)] * 3 + [jax.ShapeDtypeStruct(w.shape, F32)],
        compiler_params=_params("arbitrary"))(cbcch, cbcch, cbcch, w, doc)


class _Split:
    def __init__(self, a):
        self.hi = a.astype(jnp.bfloat16)
        self.lo = (a - self.hi.astype(F32)).astype(jnp.bfloat16)


def _per_head(dims, a, b):
    a = a if isinstance(a, _Split) else _Split(a)
    b = b if isinstance(b, _Split) else _Split(b)

    def dot(x, y):
        return lax.dot_general(x, y, (dims, ((), ())), preferred_element_type=F32)

    return jnp.stack([dot(a.hi[h], b.hi[h]) + (dot(a.hi[h], b.lo[h]) + dot(a.lo[h], b.hi[h])) for h in range(a.hi.shape[0])])


def _bmm(a, b):
    return _per_head(((1,), (0,)), a, b)


def _bmm_nt(a, b):
    return _per_head(((1,), (1,)), a, b)


def _bmm_tn(a, b):
    return _per_head(((0,), (0,)), a, b)


def _inv_unit_lower(low):
    c = low.shape[-1]
    eye = (_rows((c, c)) == lax.broadcasted_iota(jnp.int32, (c, c), 1)).astype(F32)
    pw = -low
    inv = eye + pw
    span = 2
    while span < c:
        pws = _Split(pw)
        pw = _bmm(pws, pws)
        inv = inv + _bmm(inv, pw)
        span *= 2
    return inv


def _heads_of(ref, base, heads):
    return jnp.stack([ref[:, base + h * HEAD_DIM:base + (h + 1) * HEAD_DIM] for h in range(heads)])


def _chunk_common(q, k, v, a_col, b_col, alog, dtb, kept=None):
    hn, c, _ = q.shape
    beta = _sigmoid(b_col)
    xg = a_col + dtb
    softplus = jnp.maximum(xg, 0.0) + jnp.log(1.0 + jnp.exp(-jnp.abs(xg)))
    neg_ea = -jnp.exp(alog)
    g = neg_ea * softplus
    ri = _rows((c, c))
    ci = lax.broadcasted_iota(jnp.int32, (c, c), 1)
    incl, strict = ri >= ci, ri > ci
    inclf = jnp.broadcast_to(incl.astype(F32), (hn, c, c))
    gcb = _bmm(inclf, jnp.broadcast_to(g, (hn, c, HEAD_DIM)))
    gc_row = jnp.sum(jnp.where(ri <= ci, jnp.broadcast_to(g, (hn, c, c)), 0.0), axis=1, keepdims=True)
    dmat = jnp.where(incl, jnp.exp(jnp.where(incl, gcb[:, :, :1] - gc_row, 0.0)), 0.0)
    eg = jnp.exp(gcb)
    gl = gcb[:, c - 1:c, :]
    egl = jnp.exp(gl)
    edl = jnp.exp(gl - gcb)
    kb, vb = k * beta, v * beta
    kbe = kb * eg
    if kept is None:
        ks = _Split(k)
        a0 = _bmm_nt(kb, ks)
        tm = _inv_unit_lower(jnp.where(strict, a0 * dmat, 0.0))
        p0 = _bmm_nt(q, ks)
        tms = _Split(tm)
        u, w = _bmm(tms, vb), _bmm(tms, kbe)
    else:
        (a0, tm, p0, w), u = kept, None
    return dict(beta=beta, xg=xg, neg_ea=neg_ea, g=g, incl=incl, strict=strict, inclf=inclf, dmat=dmat, eg=eg,
                egl=egl, edl=edl, kb=kb, vb=vb, a0=a0, tm=tm, kbe=kbe, u=u, w=w, p0=p0,
                attn=p0 * dmat, qe=q * eg, kd=k * edl)


def _chunk_step(cm, state):
    ss = _Split(state)
    vn = cm["u"] - _bmm(cm["w"], ss)
    vns = _Split(vn)
    o = _bmm(cm["qe"], ss) + _bmm(cm["attn"], vns)
    new_state = state * cm["egl"][:, :, :1] + _bmm_tn(cm["kd"], vns)
    return vn, o, new_state


def _gated_norm(o, zv, og):
    xo, ro = _rms_fwd(o)
    sgz = _sigmoid(zv)
    return xo, ro, sgz, xo * og * (zv * sgz)


def _gate_columns(abv, gpv, heads):
    a_col = jnp.stack([abv[:, h:h + 1] for h in range(heads)])
    b_col = jnp.stack([abv[:, heads + h:heads + h + 1] for h in range(heads)])
    alog = jnp.stack([gpv[0:1, h:h + 1] for h in range(heads)])
    dtb = jnp.stack([gpv[1:2, h:h + 1] for h in range(heads)])
    return a_col, b_col, alog, dtb


def _delta_fwd(qkv, z, ab, gpar, heads):
    t = qkv.shape[0]
    da = heads * HEAD_DIM
    n = t // CHUNK

    def body(qkv_ref, z_ref, ab_ref, gp_ref, oa_ref, st_ref, kc_ref, kw_ref, s_ref):
        @pl.when(pl.program_id(0) == 0)
        def _():
            s_ref[...] = jnp.zeros_like(s_ref)

        gpv = gp_ref[...]
        cm = _chunk_common(_heads_of(qkv_ref, 0, heads), _heads_of(qkv_ref, da, heads), _heads_of(qkv_ref, 2 * da, heads),
                           *_gate_columns(ab_ref[...], gpv, heads))
        state = s_ref[...]
        st_ref[0] = state
        vn, o, new_state = _chunk_step(cm, state)
        s_ref[...] = new_state
        for slot, val in enumerate((cm["a0"], cm["tm"], cm["p0"])):
            kc_ref[0, slot] = val
        for slot, val in enumerate((cm["w"], vn, o)):
            kw_ref[0, slot] = val
        oa = _gated_norm(o, _heads_of(z_ref, 0, heads), gpv[2:3, :])[3]
        for h in range(heads):
            oa_ref[:, h * HEAD_DIM:(h + 1) * HEAD_DIM] = oa[h]

    return pl.pallas_call(
        body, name="delta_fwd", grid=(n,),
        in_specs=[pl.BlockSpec((CHUNK, 3 * da), lambda i: (i, 0)), pl.BlockSpec((CHUNK, da), lambda i: (i, 0)),
                  pl.BlockSpec((CHUNK, LANES), lambda i: (i, 0)), pl.BlockSpec((8, LANES), lambda i: (0, 0))],
        out_specs=[pl.BlockSpec((CHUNK, da), lambda i: (i, 0)),
                   pl.BlockSpec((1, heads, HEAD_DIM, HEAD_DIM), lambda i: (i, 0, 0, 0)),
                   pl.BlockSpec((1, 3, heads, CHUNK, CHUNK), lambda i: (i, 0, 0, 0, 0)),
                   pl.BlockSpec((1, 3, heads, CHUNK, HEAD_DIM), lambda i: (i, 0, 0, 0, 0))],
        out_shape=[jax.ShapeDtypeStruct((t, da), F32), jax.ShapeDtypeStruct((n, heads, HEAD_DIM, HEAD_DIM), F32),
                   jax.ShapeDtypeStruct((n, 3, heads, CHUNK, CHUNK), F32),
                   jax.ShapeDtypeStruct((n, 3, heads, CHUNK, HEAD_DIM), F32)],
        scratch_shapes=[pltpu.VMEM((heads, HEAD_DIM, HEAD_DIM), F32)],
        compiler_params=_params("arbitrary"))(qkv, z, ab, gpar)


def _delta_bwd(qkv, z, ab, gpar, states, kept_c, kept_w, doa, heads):
    t = qkv.shape[0]
    da = heads * HEAD_DIM
    n = t // CHUNK
    c = CHUNK

    def body(qkv_ref, z_ref, ab_ref, gp_ref, st_ref, kc_ref, kw_ref, doa_ref, dqkv_ref, dz_ref, dab_ref, dpar_ref, ds_ref):
        @pl.when(pl.program_id(0) == 0)
        def _():
            ds_ref[...] = jnp.zeros_like(ds_ref)
            dpar_ref[...] = jnp.zeros_like(dpar_ref)

        gpv = gp_ref[...]
        og = gpv[2:3, :]
        q, k, v = _heads_of(qkv_ref, 0, heads), _heads_of(qkv_ref, da, heads), _heads_of(qkv_ref, 2 * da, heads)
        cm = _chunk_common(q, k, v, *_gate_columns(ab_ref[...], gpv, heads),
                           kept=(kc_ref[0, 0], kc_ref[0, 1], kc_ref[0, 2], kw_ref[0, 0]))
        state = st_ref[0]
        dsp = ds_ref[...]
        vn, o = kw_ref[0, 1], kw_ref[0, 2]
        zv = _heads_of(z_ref, 0, heads)
        xo, ro, sgz, _ = _gated_norm(o, zv, og)
        doav = _heads_of(doa_ref, 0, heads)
        don = doav * (zv * sgz)
        dz = doav * (xo * og) * _dsilu(zv, sgz)
        d_og = jnp.sum(jnp.sum(don * xo, axis=1, keepdims=True), axis=0)
        do = _rms_bwd(don * og, xo, ro)
        tm, dmat, eg, edl, egl = cm["tm"], cm["dmat"], cm["eg"], cm["edl"], cm["egl"]
        dos, dsps, sts, tms, ks = _Split(do), _Split(dsp), _Split(state), _Split(tm), _Split(k)
        dvn = _bmm_tn(cm["attn"], dos) + _bmm(cm["kd"], dsps)
        dvns = _Split(dvn)
        dqe = _bmm_nt(dos, sts)
        ds_ref[...] = _bmm_tn(cm["qe"], dos) + dsp * egl[:, :, :1] - _bmm_tn(cm["w"], dvns)
        dattn = _bmm_nt(dos, vn)
        dkd = _bmm_nt(vn, dsps)
        dkd_kd = jnp.sum(dkd * cm["kd"], axis=-1, keepdims=True)
        dgl = (jnp.sum(jnp.sum(dsp * state, axis=-1, keepdims=True), axis=1, keepdims=True) * egl[:, :, :1]
               + jnp.sum(dkd_kd, axis=1, keepdims=True))
        dgc = jnp.sum(dqe * cm["qe"], axis=-1, keepdims=True) - dkd_kd
        dk = dkd * edl
        dq = dqe * eg
        dw = -_bmm_nt(dvns, sts)
        dws = _Split(dw)
        dp0 = dattn * dmat
        dd = jnp.where(cm["incl"], dattn * cm["p0"], 0.0)
        dp0s = _Split(dp0)
        dq = dq + _bmm(dp0s, ks)
        dk = dk + _bmm_tn(dp0s, q)
        dtm = _bmm_nt(dvns, cm["vb"]) + _bmm_nt(dws, cm["kbe"])
        dvb = _bmm_tn(tms, dvns)
        dkbe = _bmm_tn(tms, dws)
        dkb = dkbe * eg
        dgc = dgc + jnp.sum(dkbe * cm["kbe"], axis=-1, keepdims=True)
        dlow = jnp.where(cm["strict"], -_bmm_tn(tms, _bmm_nt(dtm, tms)), 0.0)
        dd = dd + dlow * cm["a0"]
        da0 = dlow * dmat
        da0s = _Split(da0)
        dkb = dkb + _bmm(da0s, ks)
        dk = dk + _bmm_tn(da0s, cm["kb"])
        ddd = dd * dmat
        ones = jnp.ones((heads, c, HEAD_DIM), F32)
        dgc = dgc + jnp.sum(ddd, axis=-1, keepdims=True) - _bmm_tn(ddd, ones)[:, :, :1]
        dgc = dgc + jnp.where(_rows((c, 1)) == c - 1, dgl, 0.0)
        dg = _bmm_tn(cm["inclf"], jnp.broadcast_to(dgc, (heads, c, HEAD_DIM)))[:, :, :1]
        beta = cm["beta"]
        dk = dk + dkb * beta
        dbeta = jnp.sum(dkb * k, axis=-1, keepdims=True) + jnp.sum(dvb * v, axis=-1, keepdims=True)
        dv = dvb * beta
        db_col = dbeta * beta * (1.0 - beta)
        da_col = dg * cm["neg_ea"] * _sigmoid(cm["xg"])
        d_alog = jnp.sum(dg * cm["g"], axis=1, keepdims=True)
        d_dtb = jnp.sum(da_col, axis=1, keepdims=True)
        lane = lax.broadcasted_iota(jnp.int32, (c, LANES), 1)
        lane8 = lax.broadcasted_iota(jnp.int32, (8, LANES), 1)
        row8 = _rows((8, LANES))
        dab = jnp.zeros((c, LANES), F32)
        dpar = jnp.where(row8 == 2, d_og, 0.0)
        for h in range(heads):
            lo = h * HEAD_DIM
            dqkv_ref[:, lo:lo + HEAD_DIM] = dq[h]
            dqkv_ref[:, da + lo:da + lo + HEAD_DIM] = dk[h]
            dqkv_ref[:, 2 * da + lo:2 * da + lo + HEAD_DIM] = dv[h]
            dz_ref[:, lo:lo + HEAD_DIM] = dz[h]
            dab = dab + jnp.where(lane == h, da_col[h], 0.0) + jnp.where(lane == heads + h, db_col[h], 0.0)
            dpar = (dpar + jnp.where((row8 == 0) & (lane8 == h), d_alog[h], 0.0)
                    + jnp.where((row8 == 1) & (lane8 == h), d_dtb[h], 0.0))
        dab_ref[...] = dab
        dpar_ref[...] += dpar

    rev = lambda i: (n - 1 - i, 0)
    return pl.pallas_call(
        body, name="delta_bwd", grid=(n,),
        in_specs=[pl.BlockSpec((c, 3 * da), rev), pl.BlockSpec((c, da), rev), pl.BlockSpec((c, LANES), rev),
                  pl.BlockSpec((8, LANES), lambda i: (0, 0)),
                  pl.BlockSpec((1, heads, HEAD_DIM, HEAD_DIM), lambda i: (n - 1 - i, 0, 0, 0)),
                  pl.BlockSpec((1, 3, heads, c, c), lambda i: (n - 1 - i, 0, 0, 0, 0)),
                  pl.BlockSpec((1, 3, heads, c, HEAD_DIM), lambda i: (n - 1 - i, 0, 0, 0, 0)),
                  pl.BlockSpec((c, da), rev)],
        out_specs=[pl.BlockSpec((c, 3 * da), rev), pl.BlockSpec((c, da), rev), pl.BlockSpec((c, LANES), rev),
                   pl.BlockSpec((8, LANES), lambda i: (0, 0))],
        out_shape=[jax.ShapeDtypeStruct((t, 3 * da), F32), jax.ShapeDtypeStruct((t, da), F32),
                   jax.ShapeDtypeStruct((t, LANES), F32), jax.ShapeDtypeStruct((8, LANES), F32)],
        scratch_shapes=[pltpu.VMEM((heads, HEAD_DIM, HEAD_DIM), F32)],
        compiler_params=_params("arbitrary"))(qkv, z, ab, gpar, states, kept_c, kept_w, doa)


def _w_in_pieces(shard_cols, da, heads):
    a0, nab = 4 * da, 2 * heads
    d_in = 4 * shard_cols
    runs = [(0, a0, 0), (a0, a0 + nab, d_in - nab), (a0 + nab, d_in, a0)]
    pieces = []
    for j in range(4):
        lo, hi = j * shard_cols, (j + 1) * shard_cols
        for rlo, rhi, plo in runs:
            s, e = max(lo, rlo), min(hi, rhi)
            if s < e:
                pieces.append((j, s - lo, e - s, plo + (s - rlo)))
    return pieces, d_in - nab + LANES


def _w_in_pack(w4, da, heads):
    _, d, sc = w4.shape
    pieces, npk = _w_in_pieces(sc, da, heads)
    tr = _tile_rows(d, 256, SUBLANES_WIRE)

    def body(w_ref, o_ref):
        o_ref[:, npk - LANES:] = jnp.zeros((tr, LANES), o_ref.dtype)
        for j, lo, ln, dst in pieces:
            o_ref[:, dst:dst + ln] = w_ref[j, :, lo:lo + ln]

    return pl.pallas_call(
        body, name="w_in_pack", grid=(d // tr,),
        in_specs=[pl.BlockSpec((4, tr, sc), lambda i: (0, i, 0))],
        out_specs=pl.BlockSpec((tr, npk), lambda i: (i, 0)),
        out_shape=jax.ShapeDtypeStruct((d, npk), w4.dtype),
        compiler_params=_params("arbitrary"))(w4)


def _w_in_unpack(dwp, sc, da, heads):
    d, npk = dwp.shape
    pieces, _ = _w_in_pieces(sc, da, heads)
    tr = _tile_rows(d, 256)

    def body(g_ref, o_ref):
        for j, lo, ln, dst in pieces:
            o_ref[j, :, lo:lo + ln] = g_ref[:, dst:dst + ln]

    return pl.pallas_call(
        body, name="w_in_unpack", grid=(d // tr,),
        in_specs=[pl.BlockSpec((tr, npk), lambda i: (i, 0))],
        out_specs=pl.BlockSpec((4, tr, sc), lambda i: (0, i, 0)),
        out_shape=jax.ShapeDtypeStruct((4, d, sc), F32),
        compiler_params=_params("arbitrary"))(dwp)


def _block_diag(pool_w):
    g, gd, _ = pool_w.shape
    out = jnp.zeros((g * gd, g * gd), pool_w.dtype)
    for gi in range(g):
        out = lax.dynamic_update_slice(out, pool_w[gi], (gi * gd, gi * gd))
    return out


def _layer_dims(d):
    heads = (d // 2) // HEAD_DIM
    return heads, heads * HEAD_DIM, d // 4, d // 4


BIG = ("w_in", "w_gate", "w_up", "ple_proj", "w_out", "w_down", "ple_gate")
TRANSPOSED = ("w_gate", "w_up")


def _prepare_layer(small, li):
    d = small["norm1_g"].shape[1]
    heads, _, _, _ = _layer_dims(d)
    gpar = jnp.zeros((8, LANES), F32)
    gpar = gpar.at[0, :heads].set(small["a_log"][li]).at[1, :heads].set(small["dt_bias"][li]).at[2, :].set(small["onorm_g"][li])
    return dict(norm1_g=small["norm1_g"][li][None], conv_qkv=small["conv_qkv"][li], gpar=gpar, pool_bd=_block_diag(small["pool_w"][li]).astype(MM_DTYPE),
                pool_scale=small["pool_scale"][li][None], sconv_w=small["sconv_w"][li], norm2_g=small["norm2_g"][li][None])


def _layer_fwd(x0, p, gw, lw, tm, arrive):
    d = x0.shape[1]
    heads, da, dp, dc = _layer_dims(d)
    segs = (3 * da, da, dp, 3 * dc, LANES)
    lw["w_in_p"] = _w_in_pack(gw["w_in"], da, heads).astype(MM_DTYPE)
    qkv_pre, z, hp, cbcch, ab = _in_proj_fwd(x0, lw["norm1_g"], lw["w_in_p"], segs, tm)
    qkv = _qkv_conv_fwd(qkv_pre, lw["conv_qkv"], heads)
    oa, states, kept_c, kept_w = _delta_fwd(qkv, z, ab, lw["gpar"], heads)
    ob = _pool_fwd(hp, lw["pool_bd"], lw["pool_scale"], dp // POOL_GROUPS)
    oc = _sconv_fwd(cbcch, lw["sconv_w"])
    arrive("mixed", oa)
    x1, h2 = _out_proj_fwd(x0, (oa, ob, oc), gw["w_out"], lw["norm2_g"], tm)
    x2, gp, up = _ffn_fwd(x1, h2, gw["w_gate"], gw["w_up"], gw["w_down"], tm)
    arrive("ffn", x2)
    x3 = _ple_fwd(x2, p, gw["ple_gate"], gw["ple_proj"], tm)
    arrive("end", x3)
    saved = dict(x0=x0, qkv_pre=qkv_pre, z=z, hp=hp, cbcch=cbcch, ab=ab, qkv=qkv, states=states, kept_c=kept_c, kept_w=kept_w, oa=oa, ob=ob, oc=oc,
                 x1=x1, h2=h2, gp=gp, up=up, x2=x2)
    return x3, saved


def _layer_bwd(dx3, p, gw, lw, sv, tm, produced):
    def after_token(tok, arr):
        return arr if tok is None else arr + tok[0, 0]

    d = dx3.shape[1]
    heads, da, dp, dc = _layer_dims(d)
    segs = (3 * da, da, dp, dc, dc, dc, LANES)
    gd = dp // POOL_GROUPS
    dx2, d_ple_gate, d_ple_proj = _ple_bwd(dx3, sv["x2"], p, gw["ple_gate"], gw["ple_proj"], tm)
    dh2, d_w_gate, d_w_up, d_w_down = _ffn_bwd(dx2, sv["h2"], sv["gp"], sv["up"], gw["w_gate"], gw["w_up"], gw["w_down"],
                                               min(tm, 256))
    tok = produced("ffn", dict(w_gate=d_w_gate, w_up=d_w_up, ple_proj=d_ple_proj, w_down=d_w_down, ple_gate=d_ple_gate), dh2)
    dx1, doa, dob, doc, d_w_out, d_norm2 = _out_proj_bwd(dx2, dh2, sv["x1"], after_token(tok, lw["norm2_g"]),
                                                         (sv["oa"], sv["ob"], sv["oc"]), gw["w_out"], tm)
    dcb, dcc, dch, d_sconv = _sconv_bwd(sv["cbcch"], lw["sconv_w"], doc)
    dhp, d_pool_bd, d_pool_scale = _pool_bwd(sv["hp"], lw["pool_bd"], lw["pool_scale"], dob, gd)
    dqkv, dz, dab, dpar = _delta_bwd(sv["qkv"], sv["z"], sv["ab"], lw["gpar"], sv["states"], sv["kept_c"], sv["kept_w"], doa,
                                      heads)
    tok = produced("mixers", {}, dqkv)
    dqkv_pre, d_conv_qkv = _qkv_conv_bwd(sv["qkv_pre"], lw["conv_qkv"], dqkv, heads)
    dsegs = (dqkv_pre, dz, dhp, dcb, dcc, dch, dab)
    dx0, d_w_in_p, d_norm1 = _in_proj_bwd(sv["x0"], after_token(tok, lw["norm1_g"]), lw["w_in_p"], dsegs, dx1, segs, tm)
    per = LANES // gd
    bd = d_pool_bd.reshape(dp // LANES, per, gd, per, gd)
    d_pool_w = jnp.stack([bd[gi // per, gi % per, :, gi % per, :] for gi in range(POOL_GROUPS)])
    big = dict(w_in=_w_in_unpack(d_w_in_p, gw["w_in"].shape[2], da, heads), w_gate=d_w_gate, w_up=d_w_up,
               ple_proj=d_ple_proj, w_out=d_w_out, w_down=d_w_down, ple_gate=d_ple_gate)
    small = dict(norm1_g=d_norm1[0], conv_qkv=d_conv_qkv, a_log=dpar[0, :heads], dt_bias=dpar[1, :heads], onorm_g=dpar[2],
                 pool_w=d_pool_w, pool_scale=d_pool_scale[0], sconv_w=d_sconv, norm2_g=d_norm2[0])
    tok = produced("end", dict(w_in=big["w_in"], w_out=d_w_out), big["w_in"])
    return dx0, big, small, tok


def _local_step(x, p, target, gw, small, produced=None, arrive=None):
    t, d = x.shape
    depth = p.shape[0]
    tm = 512 if t % 512 == 0 else 128
    layers = [_prepare_layer(small, li) for li in range(depth)]
    saved = []
    h = x
    for li in range(depth):
        h, sv = _layer_fwd(h, p[li], gw[li], layers[li], tm,
                           (lambda stage, after, li=li: arrive(li, stage, after)) if arrive else (lambda stage, after: None))
        saved.append(sv)
    dx, loss, d_final = _loss_head(h, target, small["final_g"][None], tm)
    big, sm = [None] * depth, [None] * depth
    token = None
    for li in reversed(range(depth)):
        p_li = p[li] if token is None else p[li] + token[0, 0]
        dx, big[li], sm[li], token = _layer_bwd(
            dx, p_li, gw[li], layers[li], saved[li], tm,
            (lambda stage, grads, after, li=li: produced(li, stage, grads, after)) if produced else (lambda *a: None))
    small_grads = {n: jnp.stack([g[n] for g in sm]) for n in sm[0]}
    small_grads["final_g"] = d_final[0]
    return loss[0, 0], dx, big, small_grads


def _coords():
    return lax.axis_index("x"), lax.axis_index("y"), lax.axis_index("c")


def _other_chips(x, y):
    return [(1 - x, y), (x, 1 - y), (1 - x, 1 - y)]


def _place_shards(ws, me_idx):
    nt = len(ws)
    depth = ws[0].shape[0]

    def body(me_ref, *refs):
        for t, w_ref in enumerate(refs[:nt]):
            for li in range(depth):
                refs[nt + li * nt + t][...] = w_ref[li].astype(WIRE_DTYPE)

    outs = pl.pallas_call(
        body, name="place_shards",
        grid_spec=pltpu.PrefetchScalarGridSpec(
            num_scalar_prefetch=1, grid=(4,),
            in_specs=[pl.BlockSpec((depth, w.shape[1] // 4, w.shape[2]), lambda i, me_ref: (0, i, 0)) for w in ws],
            out_specs=[pl.BlockSpec((None, w.shape[1] // 4, w.shape[2]), lambda i, me_ref: (me_ref[0], i, 0))
                       for _ in range(depth) for w in ws]),
        out_shape=[jax.ShapeDtypeStruct((4,) + w.shape[1:], WIRE_DTYPE) for _ in range(depth) for w in ws],
        compiler_params=_params("arbitrary"))(me_idx, *ws)
    return [list(outs[li * nt:(li + 1) * nt]) for li in range(depth)]


def _half_block(ref, chip, pc):
    rh = ref.shape[1] // 2
    return ref.at[chip, pl.ds(pc * rh, rh)]


def _gather_copies(out_refs, send_sems, recv_sems, stage):
    nt = len(out_refs)
    x, y, c = _coords()
    pairs = []
    for j, (cx, cy) in enumerate(_other_chips(x, y)):
        for t in range(nt):
            sems = dict(send_sem=send_sems[j * nt + t], recv_sem=recv_sems[j * nt + t], device_id_type=MESH)
            if stage == 0:
                mine, theirs, to = _half_block(out_refs[t], 2 * x + y, c), _half_block(out_refs[t], 2 * cx + cy, c), (cx, cy, c)
            else:
                mine, theirs, to = (_half_block(out_refs[t], 2 * cx + cy, c), _half_block(out_refs[t], 2 * cx + cy, 1 - c),
                                    (x, y, 1 - c))
            pairs.append((pltpu.make_async_remote_copy(src_ref=mine, dst_ref=mine, device_id=to, **sems),
                          pltpu.make_async_remote_copy(src_ref=theirs, dst_ref=theirs, device_id=to, **sems)))
    return pairs


def _gather_call(name, arrs, wait_sems, after, stage):
    nt = len(arrs)
    nc = 3 * nt
    n_wait = len(wait_sems)
    n_new = 2 * nc if stage < 2 else 0
    arrs = [pltpu.with_memory_space_constraint(a, pltpu.HBM) for a in arrs]

    def body(*refs):
        a_refs = refs[:nt]
        waits = refs[nt:nt + n_wait]
        news = refs[nt + n_wait + 1:nt + n_wait + 1 + n_new]
        token = refs[-1]
        if stage > 0:
            for start, arrival in _gather_copies(a_refs, waits[:nc], waits[nc:], stage - 1):
                start.wait_send()
                arrival.wait_recv()
        if stage < 2:
            for start, _ in _gather_copies(a_refs, news[:nc], news[nc:], stage):
                start.start()
        token[...] = jnp.zeros_like(token)

    outs = pl.pallas_call(
        body, name=name,
        out_shape=(*[pltpu.SemaphoreType.DMA(())] * n_new, *[pltpu.HBM(a.shape, a.dtype) for a in arrs],
                   jax.ShapeDtypeStruct((8, LANES), F32)),
        in_specs=[HBM] * nt + [SEM] * n_wait + [ANY],
        out_specs=(*[SEM] * n_new, *[HBM] * nt, pl.BlockSpec(memory_space=pltpu.VMEM)),
        input_output_aliases={t: n_new + t for t in range(nt)},
        compiler_params=pltpu.CompilerParams(has_side_effects=pltpu.SideEffectType.DATAFLOW_SIDE_EFFECTING),
    )(*arrs, *wait_sems, after)
    return list(outs[:n_new]), list(outs[n_new:n_new + nt]), outs[-1]


def _add_my_halves(gs, others, c_idx):
    nt = len(gs)

    def body(c_ref, *refs):
        for g_ref, o_ref, out_ref in zip(refs[:nt], refs[nt:2 * nt], refs[2 * nt:]):
            out_ref[...] = (g_ref[...].astype(F32) + o_ref[...].astype(F32)).astype(out_ref.dtype)

    def quarter(g):
        return pl.BlockSpec((None, g.shape[1] // 4, g.shape[2]), lambda j, i, c_ref: (j, i, 0))

    return pl.pallas_call(
        body, name="add_my_halves",
        grid_spec=pltpu.PrefetchScalarGridSpec(
            num_scalar_prefetch=1, grid=(4, 2),
            in_specs=[pl.BlockSpec((None, g.shape[1] // 4, g.shape[2]), lambda j, i, c_ref: (j, 2 * c_ref[0] + i, 0)) for g in gs]
                     + [quarter(g) for g in gs],
            out_specs=[quarter(g) for g in gs]),
        out_shape=[jax.ShapeDtypeStruct((4, g.shape[1] // 2, g.shape[2]), WIRE_DTYPE) for g in gs],
        compiler_params=_params("arbitrary", "arbitrary"))(c_idx, *gs, *others)


def _split_plan(kind, s_refs, l_refs):
    x, y, c = _coords()
    if kind == "devices":
        peers = [(x ^ ((k >> 2) & 1), y ^ ((k >> 1) & 1), c ^ (k & 1)) for k in range(1, 8)]
        return [(s, l.at[4 * x + 2 * y + c], peer) for peer in peers for s, l in zip(s_refs, l_refs)]
    if kind == "swap":
        return [(s.at[:, pl.ds((1 - c) * (s.shape[1] // 2), s.shape[1] // 2)], l, (x, y, 1 - c)) for s, l in zip(s_refs, l_refs)]
    return [(s.at[2 * cx + cy], l.at[j], (cx, cy, c)) for j, (cx, cy) in enumerate(_other_chips(x, y))
            for s, l in zip(s_refs, l_refs)]


def _split_landing(kind, a):
    if kind == "devices":
        return (8,) + a.shape
    return (a.shape[0], a.shape[1] // 2, a.shape[2]) if kind == "swap" else (3,) + a.shape[1:]


def _copies_start(name, kind, srcs, after=None):
    ns = len(srcs)
    n = {"swap": 1, "exchange": 3, "devices": 7}[kind] * ns
    srcs = [pltpu.with_memory_space_constraint(a, pltpu.HBM) for a in srcs]
    fresh = jnp.zeros if kind == "devices" else lax.empty
    lands = [pltpu.with_memory_space_constraint(fresh(_split_landing(kind, a), a.dtype), pltpu.HBM) for a in srcs]
    extra = [] if after is None else [after]

    def body(*refs):
        first_sem = 2 * ns + len(extra)
        sems, token = refs[first_sem:first_sem + 2 * n], refs[-1]
        for k, (src, dst, dev) in enumerate(_split_plan(kind, refs[:ns], refs[ns:2 * ns])):
            pltpu.make_async_remote_copy(src_ref=src, dst_ref=dst, send_sem=sems[k], recv_sem=sems[n + k], device_id=dev,
                                         device_id_type=MESH).start()
        token[...] = jnp.zeros_like(token)

    outs = pl.pallas_call(
        body, name=name,
        out_shape=(*[pltpu.SemaphoreType.DMA(())] * (2 * n), *[pltpu.HBM(a.shape, a.dtype) for a in srcs + lands],
                   jax.ShapeDtypeStruct((8, LANES), F32)),
        in_specs=[HBM] * (2 * ns) + [ANY] * len(extra),
        out_specs=(*[SEM] * (2 * n), *[HBM] * (2 * ns), pl.BlockSpec(memory_space=pltpu.VMEM)),
        input_output_aliases={t: 2 * n + t for t in range(2 * ns)},
        compiler_params=pltpu.CompilerParams(has_side_effects=pltpu.SideEffectType.DATAFLOW_SIDE_EFFECTING),
    )(*srcs, *lands, *extra)
    return list(outs[:2 * n]), list(outs[2 * n:2 * n + ns]), list(outs[2 * n + ns:2 * n + 2 * ns]), outs[-1]


def _copies_wait(name, kind, sems, srcs, lands, after):
    ns = len(srcs)
    n = len(sems) // 2

    def body(*refs):
        sem_refs = refs[2 * ns:2 * ns + 2 * n]
        for k, (src, dst, dev) in enumerate(_split_plan(kind, refs[:ns], refs[ns:2 * ns])):
            cp = pltpu.make_async_remote_copy(src_ref=src, dst_ref=dst, send_sem=sem_refs[k], recv_sem=sem_refs[n + k],
                                              device_id=dev, device_id_type=MESH)
            cp.wait_send()
            cp.wait_recv()

    outs = pl.pallas_call(
        body, name=name, out_shape=tuple(pltpu.HBM(a.shape, a.dtype) for a in srcs + lands),
        in_specs=[HBM] * (2 * ns) + [SEM] * (2 * n) + [ANY], out_specs=tuple([HBM] * (2 * ns)),
        input_output_aliases={t: t for t in range(2 * ns)},
        compiler_params=pltpu.CompilerParams(has_side_effects=pltpu.SideEffectType.DATAFLOW_SIDE_EFFECTING),
    )(*srcs, *lands, *sems, after)
    return list(outs[:ns]), list(outs[ns:])


def _sum_into(pairs, recvs, idx, li, depth, accs):
    nt = len(pairs)

    def body(idx_ref, *refs):
        for p_ref, r_ref, out_ref in zip(refs[:nt], refs[nt:2 * nt], refs[-nt:]):
            out_ref[...] = p_ref[...].astype(F32) + r_ref[0].astype(F32) + r_ref[1].astype(F32) + r_ref[2].astype(F32)

    in_specs = ([pl.BlockSpec((None, p.shape[1] // 2, p.shape[2]), lambda i, idx_ref: (idx_ref[0], i, 0)) for p in pairs]
                + [pl.BlockSpec((3, p.shape[1] // 2, p.shape[2]), lambda i, idx_ref: (0, i, 0)) for p in pairs])
    args = [idx, *pairs, *recvs]
    aliases = {}
    if accs[0] is not None:
        in_specs += [ANY] * nt
        args += list(accs)
        aliases = {1 + 2 * nt + t: t for t in range(nt)}
    return pl.pallas_call(
        body, name="sum_into",
        grid_spec=pltpu.PrefetchScalarGridSpec(
            num_scalar_prefetch=1, grid=(2,), in_specs=in_specs,
            out_specs=[pl.BlockSpec((None, p.shape[1] // 2, p.shape[2]), lambda i, idx_ref: (li, 2 * idx_ref[1] + i, 0))
                       for p in pairs]),
        out_shape=[jax.ShapeDtypeStruct((depth, 2 * p.shape[1], p.shape[2]), F32) for p in pairs],
        input_output_aliases=aliases, compiler_params=_params("arbitrary"))(*args)


def _sum_devices(own, land, me_dev):
    rows = own.shape[0]
    tr = _tile_rows(rows, 512)

    def body(me_ref, o_ref, l_ref, out_ref):
        acc = jnp.where(me_ref[0] == 0, o_ref[...], l_ref[0])
        for s in range(1, 8):
            acc = acc + jnp.where(me_ref[0] == s, o_ref[...], l_ref[s])
        out_ref[...] = acc

    return pl.pallas_call(
        body, name="sum_devices",
        grid_spec=pltpu.PrefetchScalarGridSpec(
            num_scalar_prefetch=1, grid=(rows // tr,),
            in_specs=[pl.BlockSpec((tr, LANES), lambda i, me_ref: (i, 0)), pl.BlockSpec((8, tr, LANES), lambda i, me_ref: (0, i, 0))],
            out_specs=pl.BlockSpec((tr, LANES), lambda i, me_ref: (i, 0))),
        out_shape=jax.ShapeDtypeStruct((rows, LANES), F32), compiler_params=_params("arbitrary"))(me_dev, own, land)


def _sibling_share(gs, li):
    nt = len(gs)

    def body(*refs):
        out_refs = refs[nt:2 * nt]
        send_sems, recv_sems = refs[2 * nt:]
        x, y, c = _coords()
        sends, recvs = [], []
        for t in range(nt):
            rh = out_refs[t].shape[1] // 2
            mine, theirs = out_refs[t].at[li, pl.ds(c * rh, rh)], out_refs[t].at[li, pl.ds((1 - c) * rh, rh)]
            sems = dict(send_sem=send_sems.at[t], recv_sem=recv_sems.at[t], device_id=(x, y, 1 - c), device_id_type=MESH)
            sends.append(pltpu.make_async_remote_copy(src_ref=mine, dst_ref=mine, **sems))
            recvs.append(pltpu.make_async_remote_copy(src_ref=theirs, dst_ref=theirs, **sems))
        for cp in sends:
            cp.start()
        for cp in recvs:
            cp.wait_recv()
        for cp in sends:
            cp.wait_send()

    return pl.pallas_call(
        body, name="sibling_share", out_shape=[jax.ShapeDtypeStruct(g.shape, g.dtype) for g in gs],
        in_specs=[ANY] * nt, out_specs=[ANY] * nt, input_output_aliases={t: t for t in range(nt)},
        scratch_shapes=[pltpu.SemaphoreType.DMA((nt,)), pltpu.SemaphoreType.DMA((nt,))])(*gs)


def _all_gather_devices(buf, after=None):
    extra = [] if after is None else [after]

    def body(b_ref, *rest):
        out_ref, send_sems, recv_sems, local_sem = rest[len(extra):]
        x, y, c = _coords()
        me = 4 * x + 2 * y + c
        mine = pltpu.make_async_copy(b_ref, out_ref.at[me], local_sem)
        mine.start()
        peers = []
        for k in range(1, 8):
            fx, fy, fc = (k >> 2) & 1, (k >> 1) & 1, k & 1
            peers.append((x ^ fx, y ^ fy, c ^ fc))
        sends = [pltpu.make_async_remote_copy(src_ref=b_ref, dst_ref=out_ref.at[me], send_sem=send_sems.at[k],
                                              recv_sem=recv_sems.at[k], device_id=peer, device_id_type=MESH)
                 for k, peer in enumerate(peers)]
        for cp in sends:
            cp.start()
        for k, (px, py, pc) in enumerate(peers):
            pltpu.make_async_remote_copy(src_ref=b_ref, dst_ref=out_ref.at[4 * px + 2 * py + pc], send_sem=send_sems.at[k],
                                         recv_sem=recv_sems.at[k], device_id=(px, py, pc), device_id_type=MESH).wait_recv()
        for cp in sends:
            cp.wait_send()
        mine.wait()

    return pl.pallas_call(
        body, name="all_gather_devices", out_shape=jax.ShapeDtypeStruct((8,) + buf.shape, buf.dtype),
        in_specs=[ANY] * (1 + len(extra)), out_specs=ANY,
        scratch_shapes=[pltpu.SemaphoreType.DMA((7,)), pltpu.SemaphoreType.DMA((7,)), pltpu.SemaphoreType.DMA(())])(buf, *extra)


SMALL_SHARDED = ("conv_qkv", "sconv_w")
REPLICATED = ("norm1_g", "a_log", "dt_bias", "onorm_g", "pool_w", "pool_scale", "norm2_g", "final_g")
ALL_WEIGHTS = ("norm1_g", "w_in", "conv_qkv", "a_log", "dt_bias", "onorm_g", "pool_w", "pool_scale", "sconv_w", "w_out",
               "norm2_g", "w_gate", "w_up", "w_down", "ple_proj", "ple_gate", "final_g")


def _pad_rows(flat, row_multiple):
    m = flat.shape[0]
    r = -(-m // (LANES * row_multiple)) * row_multiple
    return jnp.pad(flat, (0, r * LANES - m)).reshape(r, LANES)


def _adamw_math(w, g, m, v):
    c1 = 1.0 / (1.0 - ADAM_B1 ** ADAM_STEP)
    c2 = 1.0 / (1.0 - ADAM_B2 ** ADAM_STEP)
    nm = ADAM_B1 * m + (1.0 - ADAM_B1) * g
    nv = ADAM_B2 * v + (1.0 - ADAM_B2) * (g * g)
    return -ADAM_LR * ((nm * c1) / (jnp.sqrt(nv * c2) + ADAM_EPS) + ADAM_WD * w), nm, nv


def _adamw(w, g, m, v):
    shape = w.shape
    cols = shape[-1]
    rows = w.size // cols
    tr = _tile_rows(rows, 512)

    def body(w_ref, g_ref, m_ref, v_ref, d_ref, nm_ref, nv_ref, go_ref):
        gv = g_ref[...]
        d_ref[...], nm_ref[...], nv_ref[...] = _adamw_math(w_ref[...], gv, m_ref[...], v_ref[...])
        go_ref[...] = gv

    spec = pl.BlockSpec((tr, cols), lambda i: (i, 0))
    outs = pl.pallas_call(
        body, name="adamw", grid=(rows // tr,), in_specs=[spec] * 4, out_specs=[spec] * 4,
        out_shape=[jax.ShapeDtypeStruct((rows, cols), F32)] * 4,
        compiler_params=_params("arbitrary"))(*[a.reshape(rows, cols) for a in (w, g, m, v)])
    return tuple(o.reshape(shape) for o in outs)


def kernel(x, p, norm1_g, w_in, conv_qkv, a_log, dt_bias, onorm_g, pool_w, pool_scale, sconv_w, w_out, norm2_g, w_gate, w_up, w_down, ple_proj, ple_gate, final_g, loss_target, m_norm1_g, m_w_in, m_conv_qkv, m_a_log, m_dt_bias, m_onorm_g, m_pool_w, m_pool_scale, m_sconv_w, m_w_out, m_norm2_g, m_w_gate, m_w_up, m_w_down, m_ple_proj, m_ple_gate, m_final_g, v_norm1_g, v_w_in, v_conv_qkv, v_a_log, v_dt_bias, v_onorm_g, v_pool_w, v_pool_scale, v_sconv_w, v_w_out, v_norm2_g, v_w_gate, v_w_up, v_w_down, v_ple_proj, v_ple_gate, v_final_g):
    weights = dict(zip(ALL_WEIGHTS, (norm1_g, w_in, conv_qkv, a_log, dt_bias, onorm_g, pool_w, pool_scale, sconv_w, w_out,
                                     norm2_g, w_gate, w_up, w_down, ple_proj, ple_gate, final_g)))
    mom_m = dict(zip(ALL_WEIGHTS, (m_norm1_g, m_w_in, m_conv_qkv, m_a_log, m_dt_bias, m_onorm_g, m_pool_w, m_pool_scale,
                                   m_sconv_w, m_w_out, m_norm2_g, m_w_gate, m_w_up, m_w_down, m_ple_proj, m_ple_gate, m_final_g)))
    mom_v = dict(zip(ALL_WEIGHTS, (v_norm1_g, v_w_in, v_conv_qkv, v_a_log, v_dt_bias, v_onorm_g, v_pool_w, v_pool_scale,
                                   v_sconv_w, v_w_out, v_norm2_g, v_w_gate, v_w_up, v_w_down, v_ple_proj, v_ple_gate, v_final_g)))
    for n in TRANSPOSED:
        weights[n], mom_m[n], mom_v[n] = (jnp.swapaxes(a[n], 1, 2) for a in (weights, mom_m, mom_v))
    c_idx = lax.axis_index("c").astype(jnp.int32).reshape(1)
    chip = (2 * lax.axis_index("x") + lax.axis_index("y")).astype(jnp.int32)
    me_idx = chip.reshape(1)
    idx = jnp.stack([chip, lax.axis_index("c").astype(jnp.int32)])
    depth = p.shape[0]

    small = {n: weights[n] for n in REPLICATED}
    sflat = _pad_rows(jnp.concatenate([weights[n].reshape(-1) for n in SMALL_SHARDED]), 8)
    sgath8 = _all_gather_devices(sflat)
    placed_in = _place_shards([weights["w_in"]], me_idx)
    sems, arrs, _ = _gather_call("gather_first_start", placed_in[0], [], sgath8, 0)
    placed_rest = _place_shards([weights[n] for n in BIG[1:]], me_idx)
    placed = [placed_in[li] + placed_rest[li] for li in range(depth)]
    sems, arrs, _ = _gather_call("gather_first_forward", arrs, sems, placed_rest[0][0], 1)
    _, arrs, token = _gather_call("gather_first_finish", arrs, sems, placed_rest[0][0], 2)
    gw = [dict() for _ in range(depth)]
    gw[0]["w_in"] = arrs[0]
    early = ("w_in", "w_out")
    late = tuple(n for n in BIG if n not in early)
    groups = [dict(li=0, names=BIG[1:], forward=(0, "mixed"), finish=(0, "mixed"))]
    for li in range(1, depth):
        groups.append(dict(li=li, names=early, forward=(li - 1, "ffn"), finish=(li - 1, "end")))
        groups.append(dict(li=li, names=late, forward=(li, "mixed"), finish=(li, "mixed")))
    def arrive(li, stage, after):
        for k, g in enumerate(groups):
            if g["forward"] == (li, stage):
                g["sems"], g["arrs"], _ = _gather_call("gather_forward_%d" % k, g["arrs"], g["sems"], after, 1)
            if g["finish"] == (li, stage):
                _, g["arrs"], _ = _gather_call("gather_finish_%d" % k, g["arrs"], g["sems"], after, 2)
                gw[g["li"]].update(zip(g["names"], g["arrs"]))

    sgath = sgath8[0::2].reshape(4, -1)
    off = 0
    for n in SMALL_SHARDED:
        shp = weights[n].shape
        part = sgath[:, off:off + weights[n].size].reshape((4,) + shp)
        small[n] = jnp.moveaxis(part, 0, -2).reshape(shp[:-1] + (4 * shp[-1],))
        off += weights[n].size
    for k, g in enumerate(groups):
        arrs = [placed[g["li"]][BIG.index(n)] for n in g["names"]]
        g["sems"], g["arrs"], token = _gather_call("gather_start_%d" % k, arrs, [], token, 0)

    small["norm1_g"] = small["norm1_g"] + token[0, 0]

    pending = []
    last_token = [None]

    def advance(g, after):
        if g["stage"] == 0:
            gs, others = _copies_wait("swap_wait_" + g["tag"], "swap", *g["handle"], after)
            g["handle"] = _copies_start("exchange_start_" + g["tag"], "exchange", _add_my_halves(gs, others, c_idx))
            g["stage"] = 1
            return g["handle"][3]
        return None

    held = {}

    def produced(li, stage, grads, after):
        token = None
        for g in pending:
            token = advance(g, after) if g["stage"] == 0 else token
        if li > 0 and stage != "end":
            held.update(grads)
            grads = {}
        elif li > 0:
            grads = {**held, **grads}
            held.clear()
        if grads:
            names = [n for n in BIG if n in grads]
            handle = _copies_start("swap_start_%d%s" % (li, stage), "swap", [grads[n] for n in names], token)
            pending.append(dict(li=li, names=names, tag="%d%s" % (li, stage), stage=0, handle=handle[:3]))
            token = handle[3]
        last_token[0] = last_token[0] if token is None else token
        return token

    loss_local, dx, _, small_grads = _local_step(x[0], p[:, 0], loss_target[0], gw, small, produced, arrive)
    rnames = REPLICATED + SMALL_SHARDED
    rflat = _pad_rows(jnp.concatenate([small_grads[n].reshape(-1) for n in rnames] + [loss_local.reshape(1)]), 8)
    small_handle = _copies_start("small_start", "devices", [rflat], last_token[0])
    accs, big_outs = {}, {}

    def finish(g, after):
        pairs, recvs = _copies_wait("exchange_wait_" + g["tag"], "exchange", *g["handle"][:3], after)
        summed = _sum_into(pairs, recvs, idx, g["li"], depth, [accs.get(n) for n in g["names"]])
        accs.update(zip(g["names"], _sibling_share(summed, g["li"])))
        return accs[g["names"][-1]]

    def update(names):
        for n in names:
            big_outs[n] = _adamw(weights[n], accs[n], mom_m[n], mom_v[n])
        return jnp.stack([big_outs[n][0].reshape(-1)[0] for n in names])

    done = finish(pending[0], small_handle[3])
    done = advance(pending[-1], done)
    for g in pending[1:-1]:
        done = finish(g, done)
    last = pending[-1]["names"]
    done = update([n for n in BIG if n not in last])
    finish(pending[-1], done)
    done = update(last)


    gshard = {}
    (own,), (land,) = _copies_wait("small_wait", "devices", *small_handle[:3], done)
    me_dev = (2 * chip + lax.axis_index("c").astype(jnp.int32)).reshape(1)
    rsum = _sum_devices(own, land, me_dev).reshape(-1)
    off = 0
    for n in rnames:
        whole = rsum[off:off + small_grads[n].size].reshape(small_grads[n].shape)
        off += small_grads[n].size
        if n in SMALL_SHARDED:
            cols = weights[n].shape[-1]
            whole = lax.dynamic_slice_in_dim(whole, chip * cols, cols, axis=whole.ndim - 1)
        gshard[n] = whole

    loss = rsum[off]

    deltas, new_m, new_v, grad_out = {}, {}, {}, {}
    for n in ALL_WEIGHTS:
        if n in BIG:
            deltas[n], new_m[n], new_v[n], grad_out[n] = big_outs[n]
        else:
            deltas[n], new_m[n], new_v[n], grad_out[n] = _adamw(weights[n], gshard[n], mom_m[n], mom_v[n])
    for n in TRANSPOSED:
        deltas[n], new_m[n], new_v[n], grad_out[n] = (jnp.swapaxes(a[n], 1, 2) for a in (deltas, new_m, new_v, grad_out))
    return (loss, dx[None], *[grad_out[n] for n in ALL_WEIGHTS], *[deltas[n] for n in ALL_WEIGHTS],
            *[new_m[n] for n in ALL_WEIGHTS], *[new_v[n] for n in ALL_WEIGHTS])
```

```python
import jax
import jax.numpy as jnp
from jax import lax
from jax.experimental import pallas as pl
from jax.experimental.pallas import tpu as pltpu

F32 = jnp.float32
MM_DTYPE = jnp.bfloat16
WIRE_DTYPE = jnp.bfloat16
EPS = 1e-6
HEAD_DIM = 128
CHUNK = 64
QKV_CONV_WIDTH = 4
SCONV_WIDTH = 3
POOL_GROUPS = 4
LANES = 128
SUBLANES_WIRE = 16
VMEM_LIMIT_BYTES = 56 * 1024 * 1024
ADAM_LR, ADAM_B1, ADAM_B2, ADAM_EPS, ADAM_WD, ADAM_STEP = 0.001, 0.9, 0.999, 1e-08, 0.01, 10
MESH = pl.DeviceIdType.MESH
ANY = pl.BlockSpec(memory_space=pl.ANY)
HBM = pl.BlockSpec(memory_space=pltpu.HBM)
SEM = pl.BlockSpec(memory_space=pltpu.SEMAPHORE)


def _params(*sem):
    return pltpu.CompilerParams(vmem_limit_bytes=VMEM_LIMIT_BYTES, dimension_semantics=sem if sem else None)


def _mm(a, b):
    return jnp.dot(a.astype(MM_DTYPE), b.astype(MM_DTYPE), preferred_element_type=F32)


def _mm_nt(a, b):
    return lax.dot_general(a.astype(MM_DTYPE), b.astype(MM_DTYPE), (((1,), (1,)), ((), ())), preferred_element_type=F32)


def _mm_tn(a, b):
    return lax.dot_general(a.astype(MM_DTYPE), b.astype(MM_DTYPE), (((0,), (0,)), ((), ())), preferred_element_type=F32)


def _sigmoid(x):
    return 1.0 / (1.0 + jnp.exp(-x))


def _dsilu(x, s):
    return s * (1.0 + x * (1.0 - s))


def _rows(shape):
    return lax.broadcasted_iota(jnp.int32, shape, 0)


def _shift_down(x, s):
    if s == 0:
        return x
    return jnp.where(_rows(x.shape) >= s, pltpu.roll(x, s, 0), 0.0)


def _shift_up(x, s):
    if s == 0:
        return x
    t = x.shape[0]
    return jnp.where(_rows(x.shape) < t - s, pltpu.roll(x, t - s, 0), 0.0)


def _rms_fwd(x):
    r = lax.rsqrt(jnp.mean(x * x, axis=-1, keepdims=True) + EPS)
    return x * r, r


def _rms_bwd(dxn, xn, r):
    return r * (dxn - xn * jnp.mean(dxn * xn, axis=-1, keepdims=True))


def _tile_rows(n, cap, mult=8):
    best = None
    for d in range(mult, min(n, cap) + 1, mult):
        if n % d == 0:
            best = d
    return best if best is not None else n


def _in_proj_fwd(x, g1, wp, segs, tm):
    t, d = x.shape
    npk = wp.shape[1]

    def body(x_ref, g_ref, w_ref, *o_refs):
        xn, _ = _rms_fwd(x_ref[...])
        h = (xn * g_ref[...]).astype(w_ref.dtype)
        off = 0
        for o_ref, wd in zip(o_refs, segs):
            o_ref[...] = jnp.dot(h, w_ref[:, off:off + wd], preferred_element_type=F32)
            off += wd

    return pl.pallas_call(
        body, name="in_proj_fwd", grid=(t // tm,),
        in_specs=[pl.BlockSpec((tm, d), lambda i: (i, 0)), pl.BlockSpec((1, d), lambda i: (0, 0)),
                  pl.BlockSpec((d, npk), lambda i: (0, 0))],
        out_specs=[pl.BlockSpec((tm, wd), lambda i: (i, 0)) for wd in segs],
        out_shape=[jax.ShapeDtypeStruct((t, wd), F32) for wd in segs],
        compiler_params=_params("arbitrary"))(x, g1, wp)


def _in_proj_bwd(x, g1, wp, dsegs, dx_res, segs, tm):
    t, d = x.shape
    npk = wp.shape[1]
    nseg = len(segs)

    def body(x_ref, g_ref, w_ref, *rest):
        ds_refs = rest[:nseg]
        dxr_ref, dx_ref, dw_ref, dg_ref = rest[nseg:]
        i = pl.program_id(0)

        @pl.when(i == 0)
        def _():
            dw_ref[...] = jnp.zeros_like(dw_ref)
            dg_ref[...] = jnp.zeros_like(dg_ref)

        xn, r = _rms_fwd(x_ref[...])
        g = g_ref[...]
        h = (xn * g).astype(w_ref.dtype)
        dh = jnp.zeros((tm, d), F32)
        off = 0
        for ds_ref, wd in zip(ds_refs, segs):
            dsv = ds_ref[...].astype(w_ref.dtype)
            dh = dh + lax.dot_general(dsv, w_ref[:, off:off + wd], (((1,), (1,)), ((), ())), preferred_element_type=F32)
            dw_ref[:, off:off + wd] += lax.dot_general(h, dsv, (((0,), (0,)), ((), ())), preferred_element_type=F32)
            off += wd
        dg_ref[...] += jnp.sum(dh * xn, axis=0, keepdims=True)
        dx_ref[...] = dxr_ref[...] + _rms_bwd(dh * g, xn, r)

    return pl.pallas_call(
        body, name="in_proj_bwd", grid=(t // tm,),
        in_specs=[pl.BlockSpec((tm, d), lambda i: (i, 0)), pl.BlockSpec((1, d), lambda i: (0, 0)),
                  pl.BlockSpec((d, npk), lambda i: (0, 0))]
                 + [pl.BlockSpec((tm, wd), lambda i: (i, 0)) for wd in segs]
                 + [pl.BlockSpec((tm, d), lambda i: (i, 0))],
        out_specs=[pl.BlockSpec((tm, d), lambda i: (i, 0)), pl.BlockSpec((d, npk), lambda i: (0, 0)),
                   pl.BlockSpec((1, d), lambda i: (0, 0))],
        out_shape=[jax.ShapeDtypeStruct((t, d), F32), jax.ShapeDtypeStruct((d, npk), F32),
                   jax.ShapeDtypeStruct((1, d), F32)],
        compiler_params=_params("arbitrary"))(x, g1, wp, *dsegs, dx_res)


def _out_proj_fwd(x0, mix, wo, g2, tm):
    t, d = x0.shape
    dq = wo.shape[1]
    widths = [m.shape[1] for m in mix]

    def body(x_ref, *rest):
        m_refs = rest[:len(mix)]
        w_ref, g_ref, x1_ref, h2_ref = rest[len(mix):]
        acc = x_ref[...]
        off = 0
        for m_ref, wd in zip(m_refs, widths):
            for k in range(wd // dq):
                acc = acc + jnp.dot(m_ref[:, k * dq:(k + 1) * dq].astype(w_ref.dtype), w_ref[off // dq + k],
                                    preferred_element_type=F32)
            off += wd
        x1_ref[...] = acc
        xn, _ = _rms_fwd(acc)
        h2_ref[...] = (xn * g_ref[...]).astype(h2_ref.dtype)

    return pl.pallas_call(
        body, name="out_proj_fwd", grid=(t // tm,),
        in_specs=[pl.BlockSpec((tm, d), lambda i: (i, 0))]
                 + [pl.BlockSpec((tm, wd), lambda i: (i, 0)) for wd in widths]
                 + [pl.BlockSpec((4, dq, d), lambda i: (0, 0, 0)), pl.BlockSpec((1, d), lambda i: (0, 0))],
        out_specs=[pl.BlockSpec((tm, d), lambda i: (i, 0)), pl.BlockSpec((tm, d), lambda i: (i, 0))],
        out_shape=[jax.ShapeDtypeStruct((t, d), F32), jax.ShapeDtypeStruct((t, d), MM_DTYPE)],
        compiler_params=_params("arbitrary"))(x0, *mix, wo, g2)


def _out_proj_bwd(dx2, dh2, x1, g2, mix, wo, tm):
    t, d = x1.shape
    dq = wo.shape[1]
    widths = [m.shape[1] for m in mix]
    nm = len(mix)

    def body(dx2_ref, dh2_ref, x1_ref, g_ref, *rest):
        m_refs = rest[:nm]
        w_ref = rest[nm]
        dx1_ref = rest[nm + 1]
        dm_refs = rest[nm + 2:nm + 2 + nm]
        dw_ref, dg_ref = rest[nm + 2 + nm:]
        i = pl.program_id(0)

        @pl.when(i == 0)
        def _():
            dw_ref[...] = jnp.zeros_like(dw_ref)
            dg_ref[...] = jnp.zeros_like(dg_ref)

        xn, r = _rms_fwd(x1_ref[...])
        dh2v = dh2_ref[...]
        dg_ref[...] += jnp.sum(dh2v * xn, axis=0, keepdims=True)
        dx1 = dx2_ref[...] + _rms_bwd(dh2v * g_ref[...], xn, r)
        dx1_ref[...] = dx1
        dx1c = dx1.astype(w_ref.dtype)
        off = 0
        for m_ref, dm_ref, wd in zip(m_refs, dm_refs, widths):
            for k in range(wd // dq):
                j = off // dq + k
                cols = slice(k * dq, (k + 1) * dq)
                dm_ref[:, cols] = lax.dot_general(dx1c, w_ref[j], (((1,), (1,)), ((), ())), preferred_element_type=F32)
                dw_ref[j] += lax.dot_general(m_ref[:, cols].astype(w_ref.dtype), dx1c, (((0,), (0,)), ((), ())),
                                             preferred_element_type=F32)
            off += wd

    tile = lambda wd: pl.BlockSpec((tm, wd), lambda i: (i, 0))
    return pl.pallas_call(
        body, name="out_proj_bwd", grid=(t // tm,),
        in_specs=[tile(d), tile(d), tile(d), pl.BlockSpec((1, d), lambda i: (0, 0))]
                 + [tile(wd) for wd in widths] + [pl.BlockSpec((4, dq, d), lambda i: (0, 0, 0))],
        out_specs=[tile(d)] + [tile(wd) for wd in widths]
                  + [pl.BlockSpec((4, dq, d), lambda i: (0, 0, 0)), pl.BlockSpec((1, d), lambda i: (0, 0))],
        out_shape=[jax.ShapeDtypeStruct((t, d), F32)] + [jax.ShapeDtypeStruct((t, wd), F32) for wd in widths]
                  + [jax.ShapeDtypeStruct((4, dq, d), F32), jax.ShapeDtypeStruct((1, d), F32)],
        compiler_params=_params("arbitrary"))(dx2, dh2, x1, g2, *mix, wo)


def _ffn_fwd(x1, h2, wg, wu, wd, tm):
    t, d = x1.shape
    fs = wg.shape[1]

    def body(x1_ref, h2_ref, wg_ref, wu_ref, wd_ref, x2_ref, gp_ref, up_ref):
        @pl.when(pl.program_id(1) == 0)
        def _():
            x2_ref[...] = x1_ref[...]

        h = h2_ref[...]
        nt = (((1,), (1,)), ((), ()))
        gp = lax.dot_general(h, wg_ref[...], nt, preferred_element_type=F32)
        up = lax.dot_general(h, wu_ref[...], nt, preferred_element_type=F32)
        gp_ref[...] = gp
        up_ref[...] = up
        ff = gp * _sigmoid(gp) * up
        x2_ref[...] += jnp.dot(ff.astype(wd_ref.dtype), wd_ref[...], preferred_element_type=F32)

    return pl.pallas_call(
        body, name="ffn_fwd", grid=(t // tm, 4),
        in_specs=[pl.BlockSpec((tm, d), lambda i, j: (i, 0)), pl.BlockSpec((tm, d), lambda i, j: (i, 0)),
                  pl.BlockSpec((None, fs, d), lambda i, j: (j, 0, 0)),
                  pl.BlockSpec((None, fs, d), lambda i, j: (j, 0, 0)),
                  pl.BlockSpec((None, fs, d), lambda i, j: (j, 0, 0))],
        out_specs=[pl.BlockSpec((tm, d), lambda i, j: (i, 0)), pl.BlockSpec((None, tm, fs), lambda i, j: (j, i, 0)),
                   pl.BlockSpec((None, tm, fs), lambda i, j: (j, i, 0))],
        out_shape=[jax.ShapeDtypeStruct((t, d), F32), jax.ShapeDtypeStruct((4, t, fs), F32),
                   jax.ShapeDtypeStruct((4, t, fs), F32)],
        compiler_params=_params("arbitrary", "arbitrary"))(x1, h2, wg, wu, wd)


def _ffn_bwd(dx2, h2, gp, up, wg, wu, wd, tm):
    t, d = dx2.shape
    fs = wg.shape[1]

    def body(dx2_ref, h2_ref, gp_ref, up_ref, wg_ref, wu_ref, wd_ref, dh2_ref, dwg_ref, dwu_ref, dwd_ref):
        j, i = pl.program_id(0), pl.program_id(1)

        @pl.when(i == 0)
        def _():
            dwg_ref[...] = jnp.zeros_like(dwg_ref)
            dwu_ref[...] = jnp.zeros_like(dwu_ref)
            dwd_ref[...] = jnp.zeros_like(dwd_ref)

        cdt = wg_ref.dtype
        h = h2_ref[...]
        gpv, upv = gp_ref[...], up_ref[...]
        s = _sigmoid(gpv)
        silu = gpv * s
        dx2c = dx2_ref[...].astype(cdt)
        dff = lax.dot_general(dx2c, wd_ref[...], (((1,), (1,)), ((), ())), preferred_element_type=F32)
        dwd_ref[...] += lax.dot_general((silu * upv).astype(cdt), dx2c, (((0,), (0,)), ((), ())), preferred_element_type=F32)
        dup = (dff * silu).astype(cdt)
        dgp = (dff * upv * _dsilu(gpv, s)).astype(cdt)
        dwg_ref[...] += lax.dot_general(dgp, h, (((0,), (0,)), ((), ())), preferred_element_type=F32)
        dwu_ref[...] += lax.dot_general(dup, h, (((0,), (0,)), ((), ())), preferred_element_type=F32)
        dh = (jnp.dot(dgp, wg_ref[...], preferred_element_type=F32) + jnp.dot(dup, wu_ref[...], preferred_element_type=F32))
        rows = pl.ds(pl.multiple_of(i * tm, tm), tm)

        @pl.when(j == 0)
        def _():
            dh2_ref[rows, :] = dh

        @pl.when(j != 0)
        def _():
            dh2_ref[rows, :] += dh

    return pl.pallas_call(
        body, name="ffn_bwd", grid=(4, t // tm),
        in_specs=[pl.BlockSpec((tm, d), lambda j, i: (i, 0)), pl.BlockSpec((tm, d), lambda j, i: (i, 0)),
                  pl.BlockSpec((None, tm, fs), lambda j, i: (j, i, 0)), pl.BlockSpec((None, tm, fs), lambda j, i: (j, i, 0)),
                  pl.BlockSpec((None, fs, d), lambda j, i: (j, 0, 0)),
                  pl.BlockSpec((None, fs, d), lambda j, i: (j, 0, 0)),
                  pl.BlockSpec((None, fs, d), lambda j, i: (j, 0, 0))],
        out_specs=[pl.BlockSpec((t, d), lambda j, i: (0, 0)), pl.BlockSpec((None, fs, d), lambda j, i: (j, 0, 0)),
                   pl.BlockSpec((None, fs, d), lambda j, i: (j, 0, 0)), pl.BlockSpec((None, fs, d), lambda j, i: (j, 0, 0))],
        out_shape=[jax.ShapeDtypeStruct((t, d), F32)] + [jax.ShapeDtypeStruct((4, fs, d), F32)] * 3,
        compiler_params=_params("arbitrary", "arbitrary"))(dx2, h2, gp, up, wg, wu, wd)


def _ple_fwd(x2, p, wpg, wpp, tm):
    t, d = x2.shape
    q = p.shape[1]
    dq = d // 4

    def body(x_ref, p_ref, wg_ref, wp_ref, o_ref):
        xv = x_ref[...]
        xc = xv.astype(wg_ref.dtype)
        pc = p_ref[...].astype(wp_ref.dtype)
        pre = jnp.dot(xc[:, :dq], wg_ref[0], preferred_element_type=F32)
        for j in range(1, 4):
            pre = pre + jnp.dot(xc[:, j * dq:(j + 1) * dq], wg_ref[j], preferred_element_type=F32)
        gate = _sigmoid(pre)
        for j in range(4):
            cols = slice(j * dq, (j + 1) * dq)
            o_ref[:, cols] = xv[:, cols] + gate[:, cols] * jnp.dot(pc, wp_ref[j], preferred_element_type=F32)

    return pl.pallas_call(
        body, name="ple_fwd", grid=(t // tm,),
        in_specs=[pl.BlockSpec((tm, d), lambda i: (i, 0)), pl.BlockSpec((tm, q), lambda i: (i, 0)),
                  pl.BlockSpec((4, dq, d), lambda i: (0, 0, 0)),
                  pl.BlockSpec((4, q, dq), lambda i: (0, 0, 0))],
        out_specs=pl.BlockSpec((tm, d), lambda i: (i, 0)),
        out_shape=jax.ShapeDtypeStruct((t, d), F32),
        compiler_params=_params("arbitrary"))(x2, p, wpg, wpp)


def _ple_bwd(dx3, x2, p, wpg, wpp, tm):
    t, d = x2.shape
    q = p.shape[1]
    dq = d // 4

    def body(dx3_ref, x_ref, p_ref, wg_ref, wp_ref, dx2_ref, dwg_ref, dwp_ref):
        @pl.when(pl.program_id(0) == 0)
        def _():
            dwg_ref[...] = jnp.zeros_like(dwg_ref)
            dwp_ref[...] = jnp.zeros_like(dwp_ref)

        cdt = wg_ref.dtype
        xc = x_ref[...].astype(cdt)
        pc = p_ref[...].astype(cdt)
        pre = jnp.dot(xc[:, :dq], wg_ref[0], preferred_element_type=F32)
        for j in range(1, 4):
            pre = pre + jnp.dot(xc[:, j * dq:(j + 1) * dq], wg_ref[j], preferred_element_type=F32)
        gate = _sigmoid(pre)
        dx3v = dx3_ref[...]
        dpp = (dx3v * gate).astype(cdt)
        dgate = dx3v * gate * (1.0 - gate)
        dpre_parts = []
        for j in range(4):
            cols = slice(j * dq, (j + 1) * dq)
            pp_j = jnp.dot(pc, wp_ref[j], preferred_element_type=F32)
            dpre_parts.append((dgate[:, cols] * pp_j).astype(cdt))
            dwp_ref[j] += lax.dot_general(pc, dpp[:, cols], (((0,), (0,)), ((), ())), preferred_element_type=F32)
        dpre = jnp.concatenate(dpre_parts, axis=1)
        for j in range(4):
            cols = slice(j * dq, (j + 1) * dq)
            dwg_ref[j] += lax.dot_general(xc[:, cols], dpre, (((0,), (0,)), ((), ())), preferred_element_type=F32)
            dx2_ref[:, cols] = dx3v[:, cols] + lax.dot_general(dpre, wg_ref[j], (((1,), (1,)), ((), ())),
                                                               preferred_element_type=F32)

    return pl.pallas_call(
        body, name="ple_bwd", grid=(t // tm,),
        in_specs=[pl.BlockSpec((tm, d), lambda i: (i, 0)), pl.BlockSpec((tm, d), lambda i: (i, 0)),
                  pl.BlockSpec((tm, q), lambda i: (i, 0)), pl.BlockSpec((4, dq, d), lambda i: (0, 0, 0)),
                  pl.BlockSpec((4, q, dq), lambda i: (0, 0, 0))],
        out_specs=[pl.BlockSpec((tm, d), lambda i: (i, 0)), pl.BlockSpec((4, dq, d), lambda i: (0, 0, 0)),
                   pl.BlockSpec((4, q, dq), lambda i: (0, 0, 0))],
        out_shape=[jax.ShapeDtypeStruct((t, d), F32), jax.ShapeDtypeStruct((4, dq, d), F32),
                   jax.ShapeDtypeStruct((4, q, dq), F32)],
        compiler_params=_params("arbitrary"))(dx3, x2, p, wpg, wpp)


def _loss_head(x, target, fg, tm):
    t, d = x.shape

    def body(x_ref, t_ref, g_ref, dx_ref, loss_ref, dg_ref):
        @pl.when(pl.program_id(0) == 0)
        def _():
            loss_ref[...] = jnp.zeros_like(loss_ref)
            dg_ref[...] = jnp.zeros_like(dg_ref)

        xn, r = _rms_fwd(x_ref[...])
        g = g_ref[...]
        err = xn * g - t_ref[...]
        loss_ref[...] += 0.5 * jnp.sum(jnp.sum(err * err, axis=-1, keepdims=True) / d, axis=0, keepdims=True)
        dy = err / d
        dg_ref[...] += jnp.sum(dy * xn, axis=0, keepdims=True)
        dx_ref[...] = _rms_bwd(dy * g, xn, r)

    return pl.pallas_call(
        body, name="loss_head", grid=(t // tm,),
        in_specs=[pl.BlockSpec((tm, d), lambda i: (i, 0)), pl.BlockSpec((tm, d), lambda i: (i, 0)),
                  pl.BlockSpec((1, d), lambda i: (0, 0))],
        out_specs=[pl.BlockSpec((tm, d), lambda i: (i, 0)), pl.BlockSpec((1, 1), lambda i: (0, 0)),
                   pl.BlockSpec((1, d), lambda i: (0, 0))],
        out_shape=[jax.ShapeDtypeStruct((t, d), F32), jax.ShapeDtypeStruct((1, 1), F32),
                   jax.ShapeDtypeStruct((1, d), F32)],
        compiler_params=_params("arbitrary"))(x, target, fg)


def _qkv_conv_act(xv, w, j, heads):
    k = QKV_CONV_WIDTH
    y = w[k - 1:k] * xv
    for s in range(1, k):
        y = y + w[k - 1 - s:k - s] * _shift_down(xv, s)
    sg = _sigmoid(y)
    s_act = y * sg
    nrm = lax.rsqrt(jnp.sum(s_act * s_act, axis=-1, keepdims=True) + EPS)
    scale = jnp.where(j < heads, HEAD_DIM ** -0.5, 1.0).astype(F32)
    return y, sg, s_act, nrm, scale


def _qkv_conv_fwd(qkv_pre, conv_w, heads):
    t = qkv_pre.shape[0]
    nblk = 3 * heads

    def body(x_ref, w_ref, o_ref):
        j = pl.program_id(0)
        _, _, s_act, nrm, scale = _qkv_conv_act(x_ref[...], w_ref[...], j, heads)
        o_ref[...] = jnp.where(j < 2 * heads, s_act * (nrm * scale), s_act)

    return pl.pallas_call(
        body, name="qkv_conv_fwd", grid=(nblk,),
        in_specs=[pl.BlockSpec((t, LANES), lambda j: (0, j)), pl.BlockSpec((QKV_CONV_WIDTH, LANES), lambda j: (0, j))],
        out_specs=pl.BlockSpec((t, LANES), lambda j: (0, j)),
        out_shape=jax.ShapeDtypeStruct(qkv_pre.shape, F32),
        compiler_params=_params("arbitrary"))(qkv_pre, conv_w)


def _qkv_conv_bwd(qkv_pre, conv_w, dqkv, heads):
    t = qkv_pre.shape[0]
    nblk = 3 * heads
    k = QKV_CONV_WIDTH

    def body(x_ref, w_ref, dn_ref, dx_ref, dw_ref):
        j = pl.program_id(0)
        xv, w = x_ref[...], w_ref[...]
        y, sg, s_act, nrm, scale = _qkv_conv_act(xv, w, j, heads)
        dn = dn_ref[...]
        dsn = dn * scale
        ds_qk = nrm * dsn - s_act * (nrm * nrm * nrm) * jnp.sum(dsn * s_act, axis=-1, keepdims=True)
        ds = jnp.where(j < 2 * heads, ds_qk, dn)
        dy = ds * _dsilu(y, sg)
        dx = w[k - 1:k] * dy
        dw_ref[k - 1:k, :] = jnp.sum(dy * xv, axis=0, keepdims=True)
        for s in range(1, k):
            dx = dx + w[k - 1 - s:k - s] * _shift_up(dy, s)
            dw_ref[k - 1 - s:k - s, :] = jnp.sum(dy * _shift_down(xv, s), axis=0, keepdims=True)
        dx_ref[...] = dx

    return pl.pallas_call(
        body, name="qkv_conv_bwd", grid=(nblk,),
        in_specs=[pl.BlockSpec((t, LANES), lambda j: (0, j)), pl.BlockSpec((k, LANES), lambda j: (0, j)),
                  pl.BlockSpec((t, LANES), lambda j: (0, j))],
        out_specs=[pl.BlockSpec((t, LANES), lambda j: (0, j)), pl.BlockSpec((k, LANES), lambda j: (0, j))],
        out_shape=[jax.ShapeDtypeStruct(qkv_pre.shape, F32), jax.ShapeDtypeStruct(conv_w.shape, F32)],
        compiler_params=_params("arbitrary"))(qkv_pre, conv_w, dqkv)


def _pool_windows(shape, j, group_dim):
    lane = lax.broadcasted_iota(jnp.int32, shape, 1) + j * LANES
    grp = lane // group_dim
    win = jnp.left_shift(2, grp).astype(F32)
    cnt = jnp.minimum((_rows(shape) + 1).astype(F32), win)
    return grp, cnt


def _pool_select(grp, levels):
    out = levels[0]
    for gi in range(1, POOL_GROUPS):
        out = jnp.where(grp == gi, levels[gi], out)
    return out


def _pool_mean(hv, grp, cnt):
    acc, levels, width = hv, [], 1
    for _ in range(POOL_GROUPS):
        acc = acc + _shift_down(acc, width)
        width *= 2
        levels.append(acc)
    return _pool_select(grp, levels) / cnt - hv


def _pool_fwd(hp, wbd, scale, group_dim):
    t, dp = hp.shape

    def body(h_ref, w_ref, s_ref, o_ref):
        hv = h_ref[...]
        grp, cnt = _pool_windows(hv.shape, pl.program_id(0), group_dim)
        pooled = _pool_mean(hv, grp, cnt)
        o_ref[...] = _mm(pooled, w_ref[...]) * s_ref[...]

    return pl.pallas_call(
        body, name="pool_fwd", grid=(dp // LANES,),
        in_specs=[pl.BlockSpec((t, LANES), lambda j: (0, j)), pl.BlockSpec((LANES, LANES), lambda j: (j, j)),
                  pl.BlockSpec((1, LANES), lambda j: (0, j))],
        out_specs=pl.BlockSpec((t, LANES), lambda j: (0, j)),
        out_shape=jax.ShapeDtypeStruct(hp.shape, F32),
        compiler_params=_params("arbitrary"))(hp, wbd, scale)


def _pool_bwd(hp, wbd, scale, dob, group_dim):
    t, dp = hp.shape

    def body(h_ref, w_ref, s_ref, do_ref, dh_ref, dw_ref, ds_ref):
        hv = h_ref[...]
        grp, cnt = _pool_windows(hv.shape, pl.program_id(0), group_dim)
        pooled = _pool_mean(hv, grp, cnt)
        wv = w_ref[...]
        dov = do_ref[...]
        ds_ref[...] = jnp.sum(dov * _mm(pooled, wv), axis=0, keepdims=True)
        dys = dov * s_ref[...]
        dw_ref[0] = _mm_tn(pooled, dys)
        dpooled = _mm_nt(dys, wv)
        acc, levels, width = dpooled / cnt, [], 1
        for _ in range(POOL_GROUPS):
            acc = acc + _shift_up(acc, width)
            width *= 2
            levels.append(acc)
        dh_ref[...] = _pool_select(grp, levels) - dpooled

    nb = dp // LANES
    return pl.pallas_call(
        body, name="pool_bwd", grid=(nb,),
        in_specs=[pl.BlockSpec((t, LANES), lambda j: (0, j)), pl.BlockSpec((LANES, LANES), lambda j: (j, j)),
                  pl.BlockSpec((1, LANES), lambda j: (0, j)), pl.BlockSpec((t, LANES), lambda j: (0, j))],
        out_specs=[pl.BlockSpec((t, LANES), lambda j: (0, j)), pl.BlockSpec((1, LANES, LANES), lambda j: (j, 0, 0)),
                   pl.BlockSpec((1, LANES), lambda j: (0, j))],
        out_shape=[jax.ShapeDtypeStruct(hp.shape, F32), jax.ShapeDtypeStruct((nb, LANES, LANES), F32),
                   jax.ShapeDtypeStruct((1, dp), F32)],
        compiler_params=_params("arbitrary"))(hp, wbd, scale, dob)


def _sconv_fwd(cbcch, w):
    t, dc3 = cbcch.shape
    nb = dc3 // 3 // LANES
    k = SCONV_WIDTH

    def body(b_ref, c_ref, h_ref, w_ref, o_ref):
        m = c_ref[...] * h_ref[...]
        wv = w_ref[...]
        y = wv[k - 1:k] * m
        for s in range(1, k):
            y = y + wv[k - 1 - s:k - s] * _shift_down(m, s)
        o_ref[...] = b_ref[...] * y

    return pl.pallas_call(
        body, name="sconv_fwd", grid=(nb,),
        in_specs=[pl.BlockSpec((t, LANES), lambda j: (0, j)), pl.BlockSpec((t, LANES), lambda j: (0, nb + j)),
                  pl.BlockSpec((t, LANES), lambda j: (0, 2 * nb + j)), pl.BlockSpec((k, LANES), lambda j: (0, j))],
        out_specs=pl.BlockSpec((t, LANES), lambda j: (0, j)),
        out_shape=jax.ShapeDtypeStruct((t, dc3 // 3), F32),
        compiler_params=_params("arbitrary"))(cbcch, cbcch, cbcch, w)


def _sconv_bwd(cbcch, w, doc):
    t, dc3 = cbcch.shape
    nb = dc3 // 3 // LANES
    k = SCONV_WIDTH

    def body(b_ref, c_ref, h_ref, w_ref, do_ref, db_ref, dc_ref, dh_ref, dw_ref):
        cv, hv = c_ref[...], h_ref[...]
        m = cv * hv
        wv = w_ref[...]
        dov = do_ref[...]
        dy = dov * b_ref[...]
        y = wv[k - 1:k] * m
        dm = wv[k - 1:k] * dy
        dw_ref[k - 1:k, :] = jnp.sum(dy * m, axis=0, keepdims=True)
        for s in range(1, k):
            ms = _shift_down(m, s)
            y = y + wv[k - 1 - s:k - s] * ms
            dm = dm + wv[k - 1 - s:k - s] * _shift_up(dy, s)
            dw_ref[k - 1 - s:k - s, :] = jnp.sum(dy * ms, axis=0, keepdims=True)
        db_ref[...] = dov * y
        dc_ref[...] = dm * hv
        dh_ref[...] = dm * cv

    col = lambda o: pl.BlockSpec((t, LANES), lambda j: (0, o * nb + j))
    return pl.pallas_call(
        body, name="sconv_bwd", grid=(nb,),
        in_specs=[col(0), col(1), col(2), pl.BlockSpec((k, LANES), lambda j: (0, j)), col(0)],
        out_specs=[col(0), col(0), col(0), pl.BlockSpec((k, LANES), lambda j: (0, j))],
        out_shape=[jax.ShapeDtypeStruct((t, dc3 // 3), F32)] * 3 + [jax.ShapeDtypeStruct(w.shape, F32)],
        compiler_params=_params("arbitrary"))(cbcch, cbcch, cbcch, w, doc)


class _Split:
    def __init__(self, a, exact=False):
        self.hi = a.astype(jnp.bfloat16)
        self.lo = None if exact else (a - self.hi.astype(F32)).astype(jnp.bfloat16)


def _per_head(dims, a, b):
    a = a if isinstance(a, _Split) else _Split(a)
    b = b if isinstance(b, _Split) else _Split(b)

    def dot(x, y):
        return lax.dot_general(x, y, (dims, ((), ())), preferred_element_type=F32)

    def head(h):
        out = dot(a.hi[h], b.hi[h])
        for x, y in ((a.hi, b.lo), (a.lo, b.hi)):
            out = out if x is None or y is None else out + dot(x[h], y[h])
        return out

    return jnp.stack([head(h) for h in range(a.hi.shape[0])])


def _bmm(a, b):
    return _per_head(((1,), (0,)), a, b)


def _bmm_nt(a, b):
    return _per_head(((1,), (1,)), a, b)


def _bmm_tn(a, b):
    return _per_head(((0,), (0,)), a, b)


def _inv_unit_lower(low):
    c = low.shape[-1]
    eye = (_rows((c, c)) == lax.broadcasted_iota(jnp.int32, (c, c), 1)).astype(F32)
    pw = -low
    inv = eye + pw
    span = 2
    while span < c:
        pws = _Split(pw)
        pw = _bmm(pws, pws)
        inv = inv + _bmm(inv, pw)
        span *= 2
    return inv


def _heads_of(ref, base, heads):
    return jnp.stack([ref[:, base + h * HEAD_DIM:base + (h + 1) * HEAD_DIM] for h in range(heads)])


def _chunk_common(q, k, v, a_col, b_col, alog, dtb, kept=None):
    hn, c, _ = q.shape
    beta = _sigmoid(b_col)
    xg = a_col + dtb
    softplus = jnp.maximum(xg, 0.0) + jnp.log(1.0 + jnp.exp(-jnp.abs(xg)))
    neg_ea = -jnp.exp(alog)
    g = neg_ea * softplus
    ri = _rows((c, c))
    ci = lax.broadcasted_iota(jnp.int32, (c, c), 1)
    incl, strict = ri >= ci, ri > ci
    inclf = _Split(jnp.broadcast_to(incl.astype(F32), (hn, c, c)), exact=True)
    gcb = _bmm(inclf, jnp.broadcast_to(g, (hn, c, HEAD_DIM)))
    gc_row = jnp.sum(jnp.where(ri <= ci, jnp.broadcast_to(g, (hn, c, c)), 0.0), axis=1, keepdims=True)
    dmat = jnp.where(incl, jnp.exp(jnp.where(incl, gcb[:, :, :1] - gc_row, 0.0)), 0.0)
    eg = jnp.exp(gcb)
    gl = gcb[:, c - 1:c, :]
    egl = jnp.exp(gl)
    edl = jnp.exp(gl - gcb)
    kb, vb = k * beta, v * beta
    kbe = kb * eg
    if kept is None:
        ks = _Split(k)
        a0 = _bmm_nt(kb, ks)
        tm = _inv_unit_lower(jnp.where(strict, a0 * dmat, 0.0))
        p0 = _bmm_nt(q, ks)
        tms = _Split(tm)
        u, w = _bmm(tms, vb), _bmm(tms, kbe)
    else:
        (a0, tm, p0, w), u = kept, None
    return dict(beta=beta, xg=xg, neg_ea=neg_ea, g=g, incl=incl, strict=strict, inclf=inclf, dmat=dmat, eg=eg,
                egl=egl, edl=edl, kb=kb, vb=vb, a0=a0, tm=tm, kbe=kbe, u=u, w=w, p0=p0,
                attn=p0 * dmat, qe=q * eg, kd=k * edl)


def _chunk_step(cm, state):
    ss = _Split(state)
    vn = cm["u"] - _bmm(cm["w"], ss)
    vns = _Split(vn)
    o = _bmm(cm["qe"], ss) + _bmm(cm["attn"], vns)
    new_state = state * cm["egl"][:, :, :1] + _bmm_tn(cm["kd"], vns)
    return vn, o, new_state


def _gated_norm(o, zv, og):
    xo, ro = _rms_fwd(o)
    sgz = _sigmoid(zv)
    return xo, ro, sgz, xo * og * (zv * sgz)


def _gate_columns(abv, gpv, heads):
    a_col = jnp.stack([abv[:, h:h + 1] for h in range(heads)])
    b_col = jnp.stack([abv[:, heads + h:heads + h + 1] for h in range(heads)])
    alog = jnp.stack([gpv[0:1, h:h + 1] for h in range(heads)])
    dtb = jnp.stack([gpv[1:2, h:h + 1] for h in range(heads)])
    return a_col, b_col, alog, dtb


def _delta_fwd(qkv, z, ab, gpar, heads):
    t = qkv.shape[0]
    da = heads * HEAD_DIM
    n = t // CHUNK

    def body(qkv_ref, z_ref, ab_ref, gp_ref, oa_ref, st_ref, kc_ref, kw_ref, s_ref):
        @pl.when(pl.program_id(0) == 0)
        def _():
            s_ref[...] = jnp.zeros_like(s_ref)

        gpv = gp_ref[...]
        cm = _chunk_common(_heads_of(qkv_ref, 0, heads), _heads_of(qkv_ref, da, heads), _heads_of(qkv_ref, 2 * da, heads),
                           *_gate_columns(ab_ref[...], gpv, heads))
        state = s_ref[...]
        st_ref[0] = state
        vn, o, new_state = _chunk_step(cm, state)
        s_ref[...] = new_state
        for slot, val in enumerate((cm["a0"], cm["tm"], cm["p0"])):
            kc_ref[0, slot] = val
        for slot, val in enumerate((cm["w"], vn, o)):
            kw_ref[0, slot] = val
        oa = _gated_norm(o, _heads_of(z_ref, 0, heads), gpv[2:3, :])[3]
        for h in range(heads):
            oa_ref[:, h * HEAD_DIM:(h + 1) * HEAD_DIM] = oa[h]

    return pl.pallas_call(
        body, name="delta_fwd", grid=(n,),
        in_specs=[pl.BlockSpec((CHUNK, 3 * da), lambda i: (i, 0)), pl.BlockSpec((CHUNK, da), lambda i: (i, 0)),
                  pl.BlockSpec((CHUNK, LANES), lambda i: (i, 0)), pl.BlockSpec((8, LANES), lambda i: (0, 0))],
        out_specs=[pl.BlockSpec((CHUNK, da), lambda i: (i, 0)),
                   pl.BlockSpec((1, heads, HEAD_DIM, HEAD_DIM), lambda i: (i, 0, 0, 0)),
                   pl.BlockSpec((1, 3, heads, CHUNK, CHUNK), lambda i: (i, 0, 0, 0, 0)),
                   pl.BlockSpec((1, 3, heads, CHUNK, HEAD_DIM), lambda i: (i, 0, 0, 0, 0))],
        out_shape=[jax.ShapeDtypeStruct((t, da), F32), jax.ShapeDtypeStruct((n, heads, HEAD_DIM, HEAD_DIM), F32),
                   jax.ShapeDtypeStruct((n, 3, heads, CHUNK, CHUNK), F32),
                   jax.ShapeDtypeStruct((n, 3, heads, CHUNK, HEAD_DIM), F32)],
        scratch_shapes=[pltpu.VMEM((heads, HEAD_DIM, HEAD_DIM), F32)],
        compiler_params=_params("arbitrary"))(qkv, z, ab, gpar)


def _delta_bwd(qkv, z, ab, gpar, states, kept_c, kept_w, doa, heads):
    t = qkv.shape[0]
    da = heads * HEAD_DIM
    n = t // CHUNK
    c = CHUNK

    def body(qkv_ref, z_ref, ab_ref, gp_ref, st_ref, kc_ref, kw_ref, doa_ref, dqkv_ref, dz_ref, dab_ref, dpar_ref, ds_ref):
        @pl.when(pl.program_id(0) == 0)
        def _():
            ds_ref[...] = jnp.zeros_like(ds_ref)
            dpar_ref[...] = jnp.zeros_like(dpar_ref)

        gpv = gp_ref[...]
        og = gpv[2:3, :]
        q, k, v = _heads_of(qkv_ref, 0, heads), _heads_of(qkv_ref, da, heads), _heads_of(qkv_ref, 2 * da, heads)
        cm = _chunk_common(q, k, v, *_gate_columns(ab_ref[...], gpv, heads),
                           kept=(kc_ref[0, 0], kc_ref[0, 1], kc_ref[0, 2], kw_ref[0, 0]))
        state = st_ref[0]
        dsp = ds_ref[...]
        vn, o = kw_ref[0, 1], kw_ref[0, 2]
        zv = _heads_of(z_ref, 0, heads)
        xo, ro, sgz, _ = _gated_norm(o, zv, og)
        doav = _heads_of(doa_ref, 0, heads)
        don = doav * (zv * sgz)
        dz = doav * (xo * og) * _dsilu(zv, sgz)
        d_og = jnp.sum(jnp.sum(don * xo, axis=1, keepdims=True), axis=0)
        do = _rms_bwd(don * og, xo, ro)
        tm, dmat, eg, edl, egl = cm["tm"], cm["dmat"], cm["eg"], cm["edl"], cm["egl"]
        dos, dsps, sts, tms, ks = _Split(do), _Split(dsp), _Split(state), _Split(tm), _Split(k)
        dvn = _bmm_tn(cm["attn"], dos) + _bmm(cm["kd"], dsps)
        dvns = _Split(dvn)
        dqe = _bmm_nt(dos, sts)
        ds_ref[...] = _bmm_tn(cm["qe"], dos) + dsp * egl[:, :, :1] - _bmm_tn(cm["w"], dvns)
        dattn = _bmm_nt(dos, vn)
        dkd = _bmm_nt(vn, dsps)
        dkd_kd = jnp.sum(dkd * cm["kd"], axis=-1, keepdims=True)
        dgl = (jnp.sum(jnp.sum(dsp * state, axis=-1, keepdims=True), axis=1, keepdims=True) * egl[:, :, :1]
               + jnp.sum(dkd_kd, axis=1, keepdims=True))
        dgc = jnp.sum(dqe * cm["qe"], axis=-1, keepdims=True) - dkd_kd
        dk = dkd * edl
        dq = dqe * eg
        dw = -_bmm_nt(dvns, sts)
        dws = _Split(dw)
        dp0 = dattn * dmat
        dd = jnp.where(cm["incl"], dattn * cm["p0"], 0.0)
        dp0s = _Split(dp0)
        dq = dq + _bmm(dp0s, ks)
        dk = dk + _bmm_tn(dp0s, q)
        dtm = _bmm_nt(dvns, cm["vb"]) + _bmm_nt(dws, cm["kbe"])
        dvb = _bmm_tn(tms, dvns)
        dkbe = _bmm_tn(tms, dws)
        dkb = dkbe * eg
        dgc = dgc + jnp.sum(dkbe * cm["kbe"], axis=-1, keepdims=True)
        dlow = jnp.where(cm["strict"], -_bmm_tn(tms, _bmm_nt(dtm, tms)), 0.0)
        dd = dd + dlow * cm["a0"]
        da0 = dlow * dmat
        da0s = _Split(da0)
        dkb = dkb + _bmm(da0s, ks)
        dk = dk + _bmm_tn(da0s, cm["kb"])
        ddd = dd * dmat
        ones = _Split(jnp.ones((heads, c, HEAD_DIM), F32), exact=True)
        dgc = dgc + jnp.sum(ddd, axis=-1, keepdims=True) - _bmm_tn(ddd, ones)[:, :, :1]
        dgc = dgc + jnp.where(_rows((c, 1)) == c - 1, dgl, 0.0)
        dg = _bmm_tn(cm["inclf"], jnp.broadcast_to(dgc, (heads, c, HEAD_DIM)))[:, :, :1]
        beta = cm["beta"]
        dk = dk + dkb * beta
        dbeta = jnp.sum(dkb * k, axis=-1, keepdims=True) + jnp.sum(dvb * v, axis=-1, keepdims=True)
        dv = dvb * beta
        db_col = dbeta * beta * (1.0 - beta)
        da_col = dg * cm["neg_ea"] * _sigmoid(cm["xg"])
        d_alog = jnp.sum(dg * cm["g"], axis=1, keepdims=True)
        d_dtb = jnp.sum(da_col, axis=1, keepdims=True)
        lane = lax.broadcasted_iota(jnp.int32, (c, LANES), 1)
        lane8 = lax.broadcasted_iota(jnp.int32, (8, LANES), 1)
        row8 = _rows((8, LANES))
        dab = jnp.zeros((c, LANES), F32)
        dpar = jnp.where(row8 == 2, d_og, 0.0)
        for h in range(heads):
            lo = h * HEAD_DIM
            dqkv_ref[:, lo:lo + HEAD_DIM] = dq[h]
            dqkv_ref[:, da + lo:da + lo + HEAD_DIM] = dk[h]
            dqkv_ref[:, 2 * da + lo:2 * da + lo + HEAD_DIM] = dv[h]
            dz_ref[:, lo:lo + HEAD_DIM] = dz[h]
            dab = dab + jnp.where(lane == h, da_col[h], 0.0) + jnp.where(lane == heads + h, db_col[h], 0.0)
            dpar = (dpar + jnp.where((row8 == 0) & (lane8 == h), d_alog[h], 0.0)
                    + jnp.where((row8 == 1) & (lane8 == h), d_dtb[h], 0.0))
        dab_ref[...] = dab
        dpar_ref[...] += dpar

    rev = lambda i: (n - 1 - i, 0)
    return pl.pallas_call(
        body, name="delta_bwd", grid=(n,),
        in_specs=[pl.BlockSpec((c, 3 * da), rev), pl.BlockSpec((c, da), rev), pl.BlockSpec((c, LANES), rev),
                  pl.BlockSpec((8, LANES), lambda i: (0, 0)),
                  pl.BlockSpec((1, heads, HEAD_DIM, HEAD_DIM), lambda i: (n - 1 - i, 0, 0, 0)),
                  pl.BlockSpec((1, 3, heads, c, c), lambda i: (n - 1 - i, 0, 0, 0, 0)),
                  pl.BlockSpec((1, 3, heads, c, HEAD_DIM), lambda i: (n - 1 - i, 0, 0, 0, 0)),
                  pl.BlockSpec((c, da), rev)],
        out_specs=[pl.BlockSpec((c, 3 * da), rev), pl.BlockSpec((c, da), rev), pl.BlockSpec((c, LANES), rev),
                   pl.BlockSpec((8, LANES), lambda i: (0, 0))],
        out_shape=[jax.ShapeDtypeStruct((t, 3 * da), F32), jax.ShapeDtypeStruct((t, da), F32),
                   jax.ShapeDtypeStruct((t, LANES), F32), jax.ShapeDtypeStruct((8, LANES), F32)],
        scratch_shapes=[pltpu.VMEM((heads, HEAD_DIM, HEAD_DIM), F32)],
        compiler_params=_params("arbitrary"))(qkv, z, ab, gpar, states, kept_c, kept_w, doa)


def _w_in_pieces(shard_cols, da, heads):
    a0, nab = 4 * da, 2 * heads
    d_in = 4 * shard_cols
    runs = [(0, a0, 0), (a0, a0 + nab, d_in - nab), (a0 + nab, d_in, a0)]
    pieces = []
    for j in range(4):
        lo, hi = j * shard_cols, (j + 1) * shard_cols
        for rlo, rhi, plo in runs:
            s, e = max(lo, rlo), min(hi, rhi)
            if s < e:
                pieces.append((j, s - lo, e - s, plo + (s - rlo)))
    return pieces, d_in - nab + LANES


def _w_in_pack(w4, da, heads):
    _, d, sc = w4.shape
    pieces, npk = _w_in_pieces(sc, da, heads)
    tr = _tile_rows(d, 256, SUBLANES_WIRE)

    def body(w_ref, o_ref):
        o_ref[:, npk - LANES:] = jnp.zeros((tr, LANES), o_ref.dtype)
        for j, lo, ln, dst in pieces:
            o_ref[:, dst:dst + ln] = w_ref[j, :, lo:lo + ln]

    return pl.pallas_call(
        body, name="w_in_pack", grid=(d // tr,),
        in_specs=[pl.BlockSpec((4, tr, sc), lambda i: (0, i, 0))],
        out_specs=pl.BlockSpec((tr, npk), lambda i: (i, 0)),
        out_shape=jax.ShapeDtypeStruct((d, npk), w4.dtype),
        compiler_params=_params("arbitrary"))(w4)


def _w_in_unpack(dwp, sc, da, heads):
    d, npk = dwp.shape
    pieces, _ = _w_in_pieces(sc, da, heads)
    tr = _tile_rows(d, 256)

    def body(g_ref, o_ref):
        for j, lo, ln, dst in pieces:
            o_ref[j, :, lo:lo + ln] = g_ref[:, dst:dst + ln]

    return pl.pallas_call(
        body, name="w_in_unpack", grid=(d // tr,),
        in_specs=[pl.BlockSpec((tr, npk), lambda i: (i, 0))],
        out_specs=pl.BlockSpec((4, tr, sc), lambda i: (0, i, 0)),
        out_shape=jax.ShapeDtypeStruct((4, d, sc), F32),
        compiler_params=_params("arbitrary"))(dwp)


def _block_diag(pool_w):
    g, gd, _ = pool_w.shape
    out = jnp.zeros((g * gd, g * gd), pool_w.dtype)
    for gi in range(g):
        out = lax.dynamic_update_slice(out, pool_w[gi], (gi * gd, gi * gd))
    return out


def _layer_dims(d):
    heads = (d // 2) // HEAD_DIM
    return heads, heads * HEAD_DIM, d // 4, d // 4


BIG = ("w_in", "w_gate", "w_up", "ple_proj", "w_out", "w_down", "ple_gate")
TRANSPOSED = ("w_gate", "w_up")


def _prepare_layer(small, li):
    d = small["norm1_g"].shape[1]
    heads, _, _, _ = _layer_dims(d)
    gpar = jnp.zeros((8, LANES), F32)
    gpar = gpar.at[0, :heads].set(small["a_log"][li]).at[1, :heads].set(small["dt_bias"][li]).at[2, :].set(small["onorm_g"][li])
    return dict(norm1_g=small["norm1_g"][li][None], conv_qkv=small["conv_qkv"][li], gpar=gpar, pool_bd=_block_diag(small["pool_w"][li]).astype(MM_DTYPE),
                pool_scale=small["pool_scale"][li][None], sconv_w=small["sconv_w"][li], norm2_g=small["norm2_g"][li][None])


def _layer_fwd(x0, p, gw, lw, tm, arrive):
    d = x0.shape[1]
    heads, da, dp, dc = _layer_dims(d)
    segs = (3 * da, da, dp, 3 * dc, LANES)
    lw["w_in_p"] = _w_in_pack(gw["w_in"], da, heads).astype(MM_DTYPE)
    qkv_pre, z, hp, cbcch, ab = _in_proj_fwd(x0, lw["norm1_g"], lw["w_in_p"], segs, tm)
    qkv = _qkv_conv_fwd(qkv_pre, lw["conv_qkv"], heads)
    oa, states, kept_c, kept_w = _delta_fwd(qkv, z, ab, lw["gpar"], heads)
    ob = _pool_fwd(hp, lw["pool_bd"], lw["pool_scale"], dp // POOL_GROUPS)
    oc = _sconv_fwd(cbcch, lw["sconv_w"])
    arrive("mixed", oa)
    x1, h2 = _out_proj_fwd(x0, (oa, ob, oc), gw["w_out"], lw["norm2_g"], tm)
    x2, gp, up = _ffn_fwd(x1, h2, gw["w_gate"], gw["w_up"], gw["w_down"], tm)
    arrive("ffn", x2)
    x3 = _ple_fwd(x2, p, gw["ple_gate"], gw["ple_proj"], tm)
    arrive("end", x3)
    saved = dict(x0=x0, qkv_pre=qkv_pre, z=z, hp=hp, cbcch=cbcch, ab=ab, qkv=qkv, states=states, kept_c=kept_c, kept_w=kept_w, oa=oa, ob=ob, oc=oc,
                 x1=x1, h2=h2, gp=gp, up=up, x2=x2)
    return x3, saved


def _layer_bwd(dx3, p, gw, lw, sv, tm, produced):
    def after_token(tok, arr):
        return arr if tok is None else arr + tok[0, 0]

    d = dx3.shape[1]
    heads, da, dp, dc = _layer_dims(d)
    segs = (3 * da, da, dp, dc, dc, dc, LANES)
    gd = dp // POOL_GROUPS
    dx2, d_ple_gate, d_ple_proj = _ple_bwd(dx3, sv["x2"], p, gw["ple_gate"], gw["ple_proj"], tm)
    dh2, d_w_gate, d_w_up, d_w_down = _ffn_bwd(dx2, sv["h2"], sv["gp"], sv["up"], gw["w_gate"], gw["w_up"], gw["w_down"],
                                               min(tm, 256))
    tok = produced("ffn", dict(w_gate=d_w_gate, w_up=d_w_up, ple_proj=d_ple_proj, w_down=d_w_down, ple_gate=d_ple_gate), dh2)
    dx1, doa, dob, doc, d_w_out, d_norm2 = _out_proj_bwd(dx2, dh2, sv["x1"], after_token(tok, lw["norm2_g"]),
                                                         (sv["oa"], sv["ob"], sv["oc"]), gw["w_out"], tm)
    dcb, dcc, dch, d_sconv = _sconv_bwd(sv["cbcch"], lw["sconv_w"], doc)
    dhp, d_pool_bd, d_pool_scale = _pool_bwd(sv["hp"], lw["pool_bd"], lw["pool_scale"], dob, gd)
    dqkv, dz, dab, dpar = _delta_bwd(sv["qkv"], sv["z"], sv["ab"], lw["gpar"], sv["states"], sv["kept_c"], sv["kept_w"], doa,
                                      heads)
    tok = produced("mixers", {}, dqkv)
    dqkv_pre, d_conv_qkv = _qkv_conv_bwd(sv["qkv_pre"], lw["conv_qkv"], dqkv, heads)
    dsegs = (dqkv_pre, dz, dhp, dcb, dcc, dch, dab)
    dx0, d_w_in_p, d_norm1 = _in_proj_bwd(sv["x0"], after_token(tok, lw["norm1_g"]), lw["w_in_p"], dsegs, dx1, segs, tm)
    per = LANES // gd
    bd = d_pool_bd.reshape(dp // LANES, per, gd, per, gd)
    d_pool_w = jnp.stack([bd[gi // per, gi % per, :, gi % per, :] for gi in range(POOL_GROUPS)])
    big = dict(w_in=_w_in_unpack(d_w_in_p, gw["w_in"].shape[2], da, heads), w_gate=d_w_gate, w_up=d_w_up,
               ple_proj=d_ple_proj, w_out=d_w_out, w_down=d_w_down, ple_gate=d_ple_gate)
    small = dict(norm1_g=d_norm1[0], conv_qkv=d_conv_qkv, a_log=dpar[0, :heads], dt_bias=dpar[1, :heads], onorm_g=dpar[2],
                 pool_w=d_pool_w, pool_scale=d_pool_scale[0], sconv_w=d_sconv, norm2_g=d_norm2[0])
    tok = produced("end", dict(w_in=big["w_in"], w_out=d_w_out), big["w_in"])
    return dx0, big, small, tok


def _local_step(x, p, target, gw, small, produced=None, arrive=None):
    t, d = x.shape
    depth = p.shape[0]
    tm = 512 if t % 512 == 0 else 128
    layers = [_prepare_layer(small, li) for li in range(depth)]
    saved = []
    h = x
    for li in range(depth):
        h, sv = _layer_fwd(h, p[li], gw[li], layers[li], tm,
                           (lambda stage, after, li=li: arrive(li, stage, after)) if arrive else (lambda stage, after: None))
        saved.append(sv)
    dx, loss, d_final = _loss_head(h, target, small["final_g"][None], tm)
    big, sm = [None] * depth, [None] * depth
    token = None
    for li in reversed(range(depth)):
        p_li = p[li] if token is None else p[li] + token[0, 0]
        dx, big[li], sm[li], token = _layer_bwd(
            dx, p_li, gw[li], layers[li], saved[li], tm,
            (lambda stage, grads, after, li=li: produced(li, stage, grads, after)) if produced else (lambda *a: None))
    small_grads = {n: jnp.stack([g[n] for g in sm]) for n in sm[0]}
    small_grads["final_g"] = d_final[0]
    return loss[0, 0], dx, big, small_grads


def _coords():
    return lax.axis_index("x"), lax.axis_index("y"), lax.axis_index("c")


def _other_chips(x, y):
    return [(1 - x, y), (x, 1 - y), (1 - x, 1 - y)]


def _place_shards(ws, me_idx):
    nt = len(ws)
    depth = ws[0].shape[0]

    def body(me_ref, *refs):
        for t, w_ref in enumerate(refs[:nt]):
            for li in range(depth):
                refs[nt + li * nt + t][...] = w_ref[li].astype(WIRE_DTYPE)

    outs = pl.pallas_call(
        body, name="place_shards",
        grid_spec=pltpu.PrefetchScalarGridSpec(
            num_scalar_prefetch=1, grid=(4,),
            in_specs=[pl.BlockSpec((depth, w.shape[1] // 4, w.shape[2]), lambda i, me_ref: (0, i, 0)) for w in ws],
            out_specs=[pl.BlockSpec((None, w.shape[1] // 4, w.shape[2]), lambda i, me_ref: (me_ref[0], i, 0))
                       for _ in range(depth) for w in ws]),
        out_shape=[jax.ShapeDtypeStruct((4,) + w.shape[1:], WIRE_DTYPE) for _ in range(depth) for w in ws],
        compiler_params=_params("arbitrary"))(me_idx, *ws)
    return [list(outs[li * nt:(li + 1) * nt]) for li in range(depth)]


def _half_block(ref, chip, pc):
    rh = ref.shape[1] // 2
    return ref.at[chip, pl.ds(pc * rh, rh)]


def _gather_copies(out_refs, send_sems, recv_sems, stage):
    nt = len(out_refs)
    x, y, c = _coords()
    pairs = []
    for j, (cx, cy) in enumerate(_other_chips(x, y)):
        for t in range(nt):
            sems = dict(send_sem=send_sems[j * nt + t], recv_sem=recv_sems[j * nt + t], device_id_type=MESH)
            if stage == 0:
                mine, theirs, to = _half_block(out_refs[t], 2 * x + y, c), _half_block(out_refs[t], 2 * cx + cy, c), (cx, cy, c)
            else:
                mine, theirs, to = (_half_block(out_refs[t], 2 * cx + cy, c), _half_block(out_refs[t], 2 * cx + cy, 1 - c),
                                    (x, y, 1 - c))
            pairs.append((pltpu.make_async_remote_copy(src_ref=mine, dst_ref=mine, device_id=to, **sems),
                          pltpu.make_async_remote_copy(src_ref=theirs, dst_ref=theirs, device_id=to, **sems)))
    return pairs


def _gather_call(name, arrs, wait_sems, after, stage):
    nt = len(arrs)
    nc = 3 * nt
    n_wait = len(wait_sems)
    n_new = 2 * nc if stage < 2 else 0
    arrs = [pltpu.with_memory_space_constraint(a, pltpu.HBM) for a in arrs]

    def body(*refs):
        a_refs = refs[:nt]
        waits = refs[nt:nt + n_wait]
        news = refs[nt + n_wait + 1:nt + n_wait + 1 + n_new]
        token = refs[-1]
        if stage > 0:
            for start, arrival in _gather_copies(a_refs, waits[:nc], waits[nc:], stage - 1):
                start.wait_send()
                arrival.wait_recv()
        if stage < 2:
            for start, _ in _gather_copies(a_refs, news[:nc], news[nc:], stage):
                start.start()
        token[...] = jnp.zeros_like(token)

    outs = pl.pallas_call(
        body, name=name,
        out_shape=(*[pltpu.SemaphoreType.DMA(())] * n_new, *[pltpu.HBM(a.shape, a.dtype) for a in arrs],
                   jax.ShapeDtypeStruct((8, LANES), F32)),
        in_specs=[HBM] * nt + [SEM] * n_wait + [ANY],
        out_specs=(*[SEM] * n_new, *[HBM] * nt, pl.BlockSpec(memory_space=pltpu.VMEM)),
        input_output_aliases={t: n_new + t for t in range(nt)},
        compiler_params=pltpu.CompilerParams(has_side_effects=pltpu.SideEffectType.DATAFLOW_SIDE_EFFECTING),
    )(*arrs, *wait_sems, after)
    return list(outs[:n_new]), list(outs[n_new:n_new + nt]), outs[-1]


def _add_my_halves(gs, others, c_idx):
    nt = len(gs)

    def body(c_ref, *refs):
        for g_ref, o_ref, out_ref in zip(refs[:nt], refs[nt:2 * nt], refs[2 * nt:]):
            out_ref[...] = (g_ref[...].astype(F32) + o_ref[...].astype(F32)).astype(out_ref.dtype)

    def half(g):
        return pl.BlockSpec((None, g.shape[1] // 2, g.shape[2]), lambda j, c_ref: (j, 0, 0))

    return pl.pallas_call(
        body, name="add_my_halves",
        grid_spec=pltpu.PrefetchScalarGridSpec(
            num_scalar_prefetch=1, grid=(4,),
            in_specs=[pl.BlockSpec((None, g.shape[1] // 2, g.shape[2]), lambda j, c_ref: (j, c_ref[0], 0)) for g in gs]
                     + [half(g) for g in gs],
            out_specs=[half(g) for g in gs]),
        out_shape=[jax.ShapeDtypeStruct((4, g.shape[1] // 2, g.shape[2]), WIRE_DTYPE) for g in gs],
        compiler_params=_params("arbitrary"))(c_idx, *gs, *others)


def _split_plan(kind, s_refs, l_refs):
    x, y, c = _coords()
    if kind == "devices":
        peers = [(x ^ ((k >> 2) & 1), y ^ ((k >> 1) & 1), c ^ (k & 1)) for k in range(1, 8)]
        return [(s, l.at[4 * x + 2 * y + c], peer) for peer in peers for s, l in zip(s_refs, l_refs)]
    if kind == "swap":
        return [(s.at[:, pl.ds((1 - c) * (s.shape[1] // 2), s.shape[1] // 2)], l, (x, y, 1 - c)) for s, l in zip(s_refs, l_refs)]
    return [(s.at[2 * cx + cy], l.at[j], (cx, cy, c)) for j, (cx, cy) in enumerate(_other_chips(x, y))
            for s, l in zip(s_refs, l_refs)]


def _split_landing(kind, a):
    if kind == "devices":
        return (8,) + a.shape
    return (a.shape[0], a.shape[1] // 2, a.shape[2]) if kind == "swap" else (3,) + a.shape[1:]


def _copies_start(name, kind, srcs, after=None):
    ns = len(srcs)
    n = {"swap": 1, "exchange": 3, "devices": 7}[kind] * ns
    srcs = [pltpu.with_memory_space_constraint(a, pltpu.HBM) for a in srcs]
    fresh = jnp.zeros if kind == "devices" else lax.empty
    lands = [pltpu.with_memory_space_constraint(fresh(_split_landing(kind, a), a.dtype), pltpu.HBM) for a in srcs]
    extra = [] if after is None else [after]

    def body(*refs):
        first_sem = 2 * ns + len(extra)
        sems, token = refs[first_sem:first_sem + 2 * n], refs[-1]
        for k, (src, dst, dev) in enumerate(_split_plan(kind, refs[:ns], refs[ns:2 * ns])):
            pltpu.make_async_remote_copy(src_ref=src, dst_ref=dst, send_sem=sems[k], recv_sem=sems[n + k], device_id=dev,
                                         device_id_type=MESH).start()
        token[...] = jnp.zeros_like(token)

    outs = pl.pallas_call(
        body, name=name,
        out_shape=(*[pltpu.SemaphoreType.DMA(())] * (2 * n), *[pltpu.HBM(a.shape, a.dtype) for a in srcs + lands],
                   jax.ShapeDtypeStruct((8, LANES), F32)),
        in_specs=[HBM] * (2 * ns) + [ANY] * len(extra),
        out_specs=(*[SEM] * (2 * n), *[HBM] * (2 * ns), pl.BlockSpec(memory_space=pltpu.VMEM)),
        input_output_aliases={t: 2 * n + t for t in range(2 * ns)},
        compiler_params=pltpu.CompilerParams(has_side_effects=pltpu.SideEffectType.DATAFLOW_SIDE_EFFECTING),
    )(*srcs, *lands, *extra)
    return list(outs[:2 * n]), list(outs[2 * n:2 * n + ns]), list(outs[2 * n + ns:2 * n + 2 * ns]), outs[-1]


def _copies_wait(name, kind, sems, srcs, lands, after):
    ns = len(srcs)
    n = len(sems) // 2

    def body(*refs):
        sem_refs = refs[2 * ns:2 * ns + 2 * n]
        for k, (src, dst, dev) in enumerate(_split_plan(kind, refs[:ns], refs[ns:2 * ns])):
            cp = pltpu.make_async_remote_copy(src_ref=src, dst_ref=dst, send_sem=sem_refs[k], recv_sem=sem_refs[n + k],
                                              device_id=dev, device_id_type=MESH)
            cp.wait_send()
            cp.wait_recv()

    outs = pl.pallas_call(
        body, name=name, out_shape=tuple(pltpu.HBM(a.shape, a.dtype) for a in srcs + lands),
        in_specs=[HBM] * (2 * ns) + [SEM] * (2 * n) + [ANY], out_specs=tuple([HBM] * (2 * ns)),
        input_output_aliases={t: t for t in range(2 * ns)},
        compiler_params=pltpu.CompilerParams(has_side_effects=pltpu.SideEffectType.DATAFLOW_SIDE_EFFECTING),
    )(*srcs, *lands, *sems, after)
    return list(outs[:ns]), list(outs[ns:])


def _sum_into(pairs, recvs, idx, li, depth, accs):
    nt = len(pairs)

    def body(idx_ref, *refs):
        for p_ref, r_ref, out_ref in zip(refs[:nt], refs[nt:2 * nt], refs[-nt:]):
            out_ref[...] = p_ref[...].astype(F32) + r_ref[0].astype(F32) + r_ref[1].astype(F32) + r_ref[2].astype(F32)

    in_specs = ([pl.BlockSpec((None, p.shape[1] // 2, p.shape[2]), lambda i, idx_ref: (idx_ref[0], i, 0)) for p in pairs]
                + [pl.BlockSpec((3, p.shape[1] // 2, p.shape[2]), lambda i, idx_ref: (0, i, 0)) for p in pairs])
    args = [idx, *pairs, *recvs]
    aliases = {}
    if accs[0] is not None:
        in_specs += [ANY] * nt
        args += list(accs)
        aliases = {1 + 2 * nt + t: t for t in range(nt)}
    return pl.pallas_call(
        body, name="sum_into",
        grid_spec=pltpu.PrefetchScalarGridSpec(
            num_scalar_prefetch=1, grid=(2,), in_specs=in_specs,
            out_specs=[pl.BlockSpec((None, p.shape[1] // 2, p.shape[2]), lambda i, idx_ref: (li, 2 * idx_ref[1] + i, 0))
                       for p in pairs]),
        out_shape=[jax.ShapeDtypeStruct((depth, 2 * p.shape[1], p.shape[2]), F32) for p in pairs],
        input_output_aliases=aliases, compiler_params=_params("arbitrary"))(*args)


def _sum_devices(own, land, me_dev):
    rows = own.shape[0]
    tr = _tile_rows(rows, 512)

    def body(me_ref, o_ref, l_ref, out_ref):
        acc = jnp.where(me_ref[0] == 0, o_ref[...], l_ref[0])
        for s in range(1, 8):
            acc = acc + jnp.where(me_ref[0] == s, o_ref[...], l_ref[s])
        out_ref[...] = acc

    return pl.pallas_call(
        body, name="sum_devices",
        grid_spec=pltpu.PrefetchScalarGridSpec(
            num_scalar_prefetch=1, grid=(rows // tr,),
            in_specs=[pl.BlockSpec((tr, LANES), lambda i, me_ref: (i, 0)), pl.BlockSpec((8, tr, LANES), lambda i, me_ref: (0, i, 0))],
            out_specs=pl.BlockSpec((tr, LANES), lambda i, me_ref: (i, 0))),
        out_shape=jax.ShapeDtypeStruct((rows, LANES), F32), compiler_params=_params("arbitrary"))(me_dev, own, land)


def _sibling_share(gs, li):
    nt = len(gs)

    def body(*refs):
        out_refs = refs[nt:2 * nt]
        send_sems, recv_sems = refs[2 * nt:]
        x, y, c = _coords()
        sends, recvs = [], []
        for t in range(nt):
            rh = out_refs[t].shape[1] // 2
            mine, theirs = out_refs[t].at[li, pl.ds(c * rh, rh)], out_refs[t].at[li, pl.ds((1 - c) * rh, rh)]
            sems = dict(send_sem=send_sems.at[t], recv_sem=recv_sems.at[t], device_id=(x, y, 1 - c), device_id_type=MESH)
            sends.append(pltpu.make_async_remote_copy(src_ref=mine, dst_ref=mine, **sems))
            recvs.append(pltpu.make_async_remote_copy(src_ref=theirs, dst_ref=theirs, **sems))
        for cp in sends:
            cp.start()
        for cp in recvs:
            cp.wait_recv()
        for cp in sends:
            cp.wait_send()

    return pl.pallas_call(
        body, name="sibling_share", out_shape=[jax.ShapeDtypeStruct(g.shape, g.dtype) for g in gs],
        in_specs=[ANY] * nt, out_specs=[ANY] * nt, input_output_aliases={t: t for t in range(nt)},
        scratch_shapes=[pltpu.SemaphoreType.DMA((nt,)), pltpu.SemaphoreType.DMA((nt,))])(*gs)


def _all_gather_devices(buf, after=None):
    extra = [] if after is None else [after]

    def body(b_ref, *rest):
        out_ref, send_sems, recv_sems, local_sem = rest[len(extra):]
        x, y, c = _coords()
        me = 4 * x + 2 * y + c
        mine = pltpu.make_async_copy(b_ref, out_ref.at[me], local_sem)
        mine.start()
        peers = []
        for k in range(1, 8):
            fx, fy, fc = (k >> 2) & 1, (k >> 1) & 1, k & 1
            peers.append((x ^ fx, y ^ fy, c ^ fc))
        sends = [pltpu.make_async_remote_copy(src_ref=b_ref, dst_ref=out_ref.at[me], send_sem=send_sems.at[k],
                                              recv_sem=recv_sems.at[k], device_id=peer, device_id_type=MESH)
                 for k, peer in enumerate(peers)]
        for cp in sends:
            cp.start()
        for k, (px, py, pc) in enumerate(peers):
            pltpu.make_async_remote_copy(src_ref=b_ref, dst_ref=out_ref.at[4 * px + 2 * py + pc], send_sem=send_sems.at[k],
                                         recv_sem=recv_sems.at[k], device_id=(px, py, pc), device_id_type=MESH).wait_recv()
        for cp in sends:
            cp.wait_send()
        mine.wait()

    return pl.pallas_call(
        body, name="all_gather_devices", out_shape=jax.ShapeDtypeStruct((8,) + buf.shape, buf.dtype),
        in_specs=[ANY] * (1 + len(extra)), out_specs=ANY,
        scratch_shapes=[pltpu.SemaphoreType.DMA((7,)), pltpu.SemaphoreType.DMA((7,)), pltpu.SemaphoreType.DMA(())])(buf, *extra)


SMALL_SHARDED = ("conv_qkv", "sconv_w")
REPLICATED = ("norm1_g", "a_log", "dt_bias", "onorm_g", "pool_w", "pool_scale", "norm2_g", "final_g")
ALL_WEIGHTS = ("norm1_g", "w_in", "conv_qkv", "a_log", "dt_bias", "onorm_g", "pool_w", "pool_scale", "sconv_w", "w_out",
               "norm2_g", "w_gate", "w_up", "w_down", "ple_proj", "ple_gate", "final_g")


def _pad_rows(flat, row_multiple):
    m = flat.shape[0]
    r = -(-m // (LANES * row_multiple)) * row_multiple
    return jnp.pad(flat, (0, r * LANES - m)).reshape(r, LANES)


def _adamw_math(w, g, m, v):
    c1 = 1.0 / (1.0 - ADAM_B1 ** ADAM_STEP)
    c2 = 1.0 / (1.0 - ADAM_B2 ** ADAM_STEP)
    nm = ADAM_B1 * m + (1.0 - ADAM_B1) * g
    nv = ADAM_B2 * v + (1.0 - ADAM_B2) * (g * g)
    return -ADAM_LR * ((nm * c1) / (jnp.sqrt(nv * c2) + ADAM_EPS) + ADAM_WD * w), nm, nv


def _adamw(w, g, m, v):
    shape = w.shape
    cols = shape[-1]
    rows = w.size // cols
    tr = _tile_rows(rows, 512)

    def body(w_ref, g_ref, m_ref, v_ref, d_ref, nm_ref, nv_ref, go_ref):
        gv = g_ref[...]
        d_ref[...], nm_ref[...], nv_ref[...] = _adamw_math(w_ref[...], gv, m_ref[...], v_ref[...])
        go_ref[...] = gv

    spec = pl.BlockSpec((tr, cols), lambda i: (i, 0))
    outs = pl.pallas_call(
        body, name="adamw", grid=(rows // tr,), in_specs=[spec] * 4, out_specs=[spec] * 4,
        out_shape=[jax.ShapeDtypeStruct((rows, cols), F32)] * 4,
        compiler_params=_params("arbitrary"))(*[a.reshape(rows, cols) for a in (w, g, m, v)])
    return tuple(o.reshape(shape) for o in outs)


def kernel(x, p, norm1_g, w_in, conv_qkv, a_log, dt_bias, onorm_g, pool_w, pool_scale, sconv_w, w_out, norm2_g, w_gate, w_up, w_down, ple_proj, ple_gate, final_g, loss_target, m_norm1_g, m_w_in, m_conv_qkv, m_a_log, m_dt_bias, m_onorm_g, m_pool_w, m_pool_scale, m_sconv_w, m_w_out, m_norm2_g, m_w_gate, m_w_up, m_w_down, m_ple_proj, m_ple_gate, m_final_g, v_norm1_g, v_w_in, v_conv_qkv, v_a_log, v_dt_bias, v_onorm_g, v_pool_w, v_pool_scale, v_sconv_w, v_w_out, v_norm2_g, v_w_gate, v_w_up, v_w_down, v_ple_proj, v_ple_gate, v_final_g):
    weights = dict(zip(ALL_WEIGHTS, (norm1_g, w_in, conv_qkv, a_log, dt_bias, onorm_g, pool_w, pool_scale, sconv_w, w_out,
                                     norm2_g, w_gate, w_up, w_down, ple_proj, ple_gate, final_g)))
    mom_m = dict(zip(ALL_WEIGHTS, (m_norm1_g, m_w_in, m_conv_qkv, m_a_log, m_dt_bias, m_onorm_g, m_pool_w, m_pool_scale,
                                   m_sconv_w, m_w_out, m_norm2_g, m_w_gate, m_w_up, m_w_down, m_ple_proj, m_ple_gate, m_final_g)))
    mom_v = dict(zip(ALL_WEIGHTS, (v_norm1_g, v_w_in, v_conv_qkv, v_a_log, v_dt_bias, v_onorm_g, v_pool_w, v_pool_scale,
                                   v_sconv_w, v_w_out, v_norm2_g, v_w_gate, v_w_up, v_w_down, v_ple_proj, v_ple_gate, v_final_g)))
    for n in TRANSPOSED:
        weights[n], mom_m[n], mom_v[n] = (jnp.swapaxes(a[n], 1, 2) for a in (weights, mom_m, mom_v))
    c_idx = lax.axis_index("c").astype(jnp.int32).reshape(1)
    chip = (2 * lax.axis_index("x") + lax.axis_index("y")).astype(jnp.int32)
    me_idx = chip.reshape(1)
    idx = jnp.stack([chip, lax.axis_index("c").astype(jnp.int32)])
    depth = p.shape[0]

    small = {n: weights[n] for n in REPLICATED}
    sflat = _pad_rows(jnp.concatenate([weights[n].reshape(-1) for n in SMALL_SHARDED]), 8)
    sgath8 = _all_gather_devices(sflat)
    placed_in = _place_shards([weights["w_in"]], me_idx)
    sems, arrs, _ = _gather_call("gather_first_start", placed_in[0], [], sgath8, 0)
    placed_rest = _place_shards([weights[n] for n in BIG[1:]], me_idx)
    placed = [placed_in[li] + placed_rest[li] for li in range(depth)]
    sems, arrs, _ = _gather_call("gather_first_forward", arrs, sems, placed_rest[0][0], 1)
    _, arrs, token = _gather_call("gather_first_finish", arrs, sems, placed_rest[0][0], 2)
    gw = [dict() for _ in range(depth)]
    gw[0]["w_in"] = arrs[0]
    early = ("w_in", "w_out")
    late = tuple(n for n in BIG if n not in early)
    groups = [dict(li=0, names=BIG[1:], forward=(0, "mixed"), finish=(0, "mixed"))]
    for li in range(1, depth):
        groups.append(dict(li=li, names=early, forward=(li - 1, "ffn"), finish=(li - 1, "end")))
        groups.append(dict(li=li, names=late, forward=(li, "mixed"), finish=(li, "mixed")))
    def arrive(li, stage, after):
        for k, g in enumerate(groups):
            if g["forward"] == (li, stage):
                g["sems"], g["arrs"], _ = _gather_call("gather_forward_%d" % k, g["arrs"], g["sems"], after, 1)
            if g["finish"] == (li, stage):
                _, g["arrs"], _ = _gather_call("gather_finish_%d" % k, g["arrs"], g["sems"], after, 2)
                gw[g["li"]].update(zip(g["names"], g["arrs"]))

    sgath = sgath8[0::2].reshape(4, -1)
    off = 0
    for n in SMALL_SHARDED:
        shp = weights[n].shape
        part = sgath[:, off:off + weights[n].size].reshape((4,) + shp)
        small[n] = jnp.moveaxis(part, 0, -2).reshape(shp[:-1] + (4 * shp[-1],))
        off += weights[n].size
    for k, g in enumerate(groups):
        arrs = [placed[g["li"]][BIG.index(n)] for n in g["names"]]
        g["sems"], g["arrs"], token = _gather_call("gather_start_%d" % k, arrs, [], token, 0)

    small["norm1_g"] = small["norm1_g"] + token[0, 0]

    pending = []
    last_token = [None]

    def advance(g, after):
        if g["stage"] == 0:
            gs, others = _copies_wait("swap_wait_" + g["tag"], "swap", *g["handle"], after)
            g["handle"] = _copies_start("exchange_start_" + g["tag"], "exchange", _add_my_halves(gs, others, c_idx))
            g["stage"] = 1
            return g["handle"][3]
        return None

    held = {}

    def produced(li, stage, grads, after):
        token = None
        for g in pending:
            token = advance(g, after) if g["stage"] == 0 else token
        if li > 0 and stage != "end":
            held.update(grads)
            grads = {}
        elif li > 0:
            grads = {**held, **grads}
            held.clear()
        if grads:
            names = [n for n in BIG if n in grads]
            handle = _copies_start("swap_start_%d%s" % (li, stage), "swap", [grads[n] for n in names], token)
            pending.append(dict(li=li, names=names, tag="%d%s" % (li, stage), stage=0, handle=handle[:3]))
            token = handle[3]
        last_token[0] = last_token[0] if token is None else token
        return token

    loss_local, dx, _, small_grads = _local_step(x[0], p[:, 0], loss_target[0], gw, small, produced, arrive)
    rnames = REPLICATED + SMALL_SHARDED
    rflat = _pad_rows(jnp.concatenate([small_grads[n].reshape(-1) for n in rnames] + [loss_local.reshape(1)]), 8)
    small_handle = _copies_start("small_start", "devices", [rflat], last_token[0])
    accs, big_outs = {}, {}

    def finish(g, after):
        pairs, recvs = _copies_wait("exchange_wait_" + g["tag"], "exchange", *g["handle"][:3], after)
        summed = _sum_into(pairs, recvs, idx, g["li"], depth, [accs.get(n) for n in g["names"]])
        accs.update(zip(g["names"], _sibling_share(summed, g["li"])))
        return accs[g["names"][-1]]

    def update(names):
        for n in names:
            big_outs[n] = _adamw(weights[n], accs[n], mom_m[n], mom_v[n])
        return jnp.stack([big_outs[n][0].reshape(-1)[0] for n in names])

    done = finish(pending[0], small_handle[3])
    done = advance(pending[-1], done)
    for g in pending[1:-1]:
        done = finish(g, done)
    last = pending[-1]["names"]
    done = update([n for n in BIG if n not in last])
    finish(pending[-1], done)
    done = update(last)


    gshard = {}
    (own,), (land,) = _copies_wait("small_wait", "devices", *small_handle[:3], done)
    me_dev = (2 * chip + lax.axis_index("c").astype(jnp.int32)).reshape(1)
    rsum = _sum_devices(own, land, me_dev).reshape(-1)
    off = 0
    for n in rnames:
        whole = rsum[off:off + small_grads[n].size].reshape(small_grads[n].shape)
        off += small_grads[n].size
        if n in SMALL_SHARDED:
            cols = weights[n].shape[-1]
            whole = lax.dynamic_slice_in_dim(whole, chip * cols, cols, axis=whole.ndim - 1)
        gshard[n] = whole

    loss = rsum[off]

    deltas, new_m, new_v, grad_out = {}, {}, {}, {}
    for n in ALL_WEIGHTS:
        if n in BIG:
            deltas[n], new_m[n], new_v[n], grad_out[n] = big_outs[n]
        else:
            deltas[n], new_m[n], new_v[n], grad_out[n] = _adamw(weights[n], gshard[n], mom_m[n], mom_v[n])
    for n in TRANSPOSED:
        deltas[n], new_m[n], new_v[n], grad_out[n] = (jnp.swapaxes(a[n], 1, 2) for a in (deltas, new_m, new_v, grad_out))
    return (loss, dx[None], *[grad_out[n] for n in ALL_WEIGHTS], *[deltas[n] for n in ALL_WEIGHTS],
            *[new_m[n] for n in ALL_WEIGHTS], *[new_v[n] for n in ALL_WEIGHTS])
```

```python
import jax
import jax.numpy as jnp
from jax import lax
from jax.experimental import pallas as pl
from jax.experimental.pallas import tpu as pltpu

F32 = jnp.float32
MM_DTYPE = jnp.bfloat16
WIRE_DTYPE = jnp.bfloat16
EPS = 1e-6
HEAD_DIM = 128
CHUNK = 64
QKV_CONV_WIDTH = 4
SCONV_WIDTH = 3
POOL_GROUPS = 4
LANES = 128
SUBLANES_WIRE = 16
VMEM_LIMIT_BYTES = 56 * 1024 * 1024
ADAM_LR, ADAM_B1, ADAM_B2, ADAM_EPS, ADAM_WD, ADAM_STEP = 0.001, 0.9, 0.999, 1e-08, 0.01, 10
MESH = pl.DeviceIdType.MESH
ANY = pl.BlockSpec(memory_space=pl.ANY)
HBM = pl.BlockSpec(memory_space=pltpu.HBM)
SEM = pl.BlockSpec(memory_space=pltpu.SEMAPHORE)


def _params(*sem):
    return pltpu.CompilerParams(vmem_limit_bytes=VMEM_LIMIT_BYTES, dimension_semantics=sem if sem else None)


def _mm(a, b):
    return jnp.dot(a.astype(MM_DTYPE), b.astype(MM_DTYPE), preferred_element_type=F32)


def _mm_nt(a, b):
    return lax.dot_general(a.astype(MM_DTYPE), b.astype(MM_DTYPE), (((1,), (1,)), ((), ())), preferred_element_type=F32)


def _mm_tn(a, b):
    return lax.dot_general(a.astype(MM_DTYPE), b.astype(MM_DTYPE), (((0,), (0,)), ((), ())), preferred_element_type=F32)


def _sigmoid(x):
    return 1.0 / (1.0 + jnp.exp(-x))


def _dsilu(x, s):
    return s * (1.0 + x * (1.0 - s))


def _rows(shape):
    return lax.broadcasted_iota(jnp.int32, shape, 0)


def _shift_down(x, s):
    if s == 0:
        return x
    return jnp.where(_rows(x.shape) >= s, pltpu.roll(x, s, 0), 0.0)


def _shift_up(x, s):
    if s == 0:
        return x
    t = x.shape[0]
    return jnp.where(_rows(x.shape) < t - s, pltpu.roll(x, t - s, 0), 0.0)


def _rms_fwd(x):
    r = lax.rsqrt(jnp.mean(x * x, axis=-1, keepdims=True) + EPS)
    return x * r, r


def _rms_bwd(dxn, xn, r):
    return r * (dxn - xn * jnp.mean(dxn * xn, axis=-1, keepdims=True))


def _tile_rows(n, cap, mult=8):
    best = None
    for d in range(mult, min(n, cap) + 1, mult):
        if n % d == 0:
            best = d
    return best if best is not None else n


def _in_proj_fwd(x, g1, wp, segs, tm):
    t, d = x.shape
    npk = wp.shape[1]

    def body(x_ref, g_ref, w_ref, *o_refs):
        xn, _ = _rms_fwd(x_ref[...])
        h = (xn * g_ref[...]).astype(w_ref.dtype)
        off = 0
        for o_ref, wd in zip(o_refs, segs):
            o_ref[...] = jnp.dot(h, w_ref[:, off:off + wd], preferred_element_type=F32)
            off += wd

    return pl.pallas_call(
        body, name="in_proj_fwd", grid=(t // tm,),
        in_specs=[pl.BlockSpec((tm, d), lambda i: (i, 0)), pl.BlockSpec((1, d), lambda i: (0, 0)),
                  pl.BlockSpec((d, npk), lambda i: (0, 0))],
        out_specs=[pl.BlockSpec((tm, wd), lambda i: (i, 0)) for wd in segs],
        out_shape=[jax.ShapeDtypeStruct((t, wd), F32) for wd in segs],
        compiler_params=_params("arbitrary"))(x, g1, wp)


def _in_proj_bwd(x, g1, wp, dsegs, dx_res, segs, tm):
    t, d = x.shape
    npk = wp.shape[1]
    nseg = len(segs)

    def body(x_ref, g_ref, w_ref, *rest):
        ds_refs = rest[:nseg]
        dxr_ref, dx_ref, dw_ref, dg_ref = rest[nseg:]
        i = pl.program_id(0)

        @pl.when(i == 0)
        def _():
            dw_ref[...] = jnp.zeros_like(dw_ref)
            dg_ref[...] = jnp.zeros_like(dg_ref)

        xn, r = _rms_fwd(x_ref[...])
        g = g_ref[...]
        h = (xn * g).astype(w_ref.dtype)
        dcat = jnp.concatenate([ds_ref[...].astype(w_ref.dtype) for ds_ref in ds_refs], axis=1)
        dh = lax.dot_general(dcat, w_ref[...], (((1,), (1,)), ((), ())), preferred_element_type=F32)
        dw_ref[...] += lax.dot_general(h, dcat, (((0,), (0,)), ((), ())), preferred_element_type=F32)
        dg_ref[...] += jnp.sum(dh * xn, axis=0, keepdims=True)
        dx_ref[...] = dxr_ref[...] + _rms_bwd(dh * g, xn, r)

    return pl.pallas_call(
        body, name="in_proj_bwd", grid=(t // tm,),
        in_specs=[pl.BlockSpec((tm, d), lambda i: (i, 0)), pl.BlockSpec((1, d), lambda i: (0, 0)),
                  pl.BlockSpec((d, npk), lambda i: (0, 0))]
                 + [pl.BlockSpec((tm, wd), lambda i: (i, 0)) for wd in segs]
                 + [pl.BlockSpec((tm, d), lambda i: (i, 0))],
        out_specs=[pl.BlockSpec((tm, d), lambda i: (i, 0)), pl.BlockSpec((d, npk), lambda i: (0, 0)),
                   pl.BlockSpec((1, d), lambda i: (0, 0))],
        out_shape=[jax.ShapeDtypeStruct((t, d), F32), jax.ShapeDtypeStruct((d, npk), F32),
                   jax.ShapeDtypeStruct((1, d), F32)],
        compiler_params=_params("arbitrary"))(x, g1, wp, *dsegs, dx_res)


def _out_proj_fwd(x0, mix, wo, g2, tm):
    t, d = x0.shape
    dq = wo.shape[1]
    widths = [m.shape[1] for m in mix]

    def body(x_ref, *rest):
        m_refs = rest[:len(mix)]
        w_ref, g_ref, x1_ref, h2_ref = rest[len(mix):]
        acc = x_ref[...]
        off = 0
        for m_ref, wd in zip(m_refs, widths):
            for k in range(wd // dq):
                acc = acc + jnp.dot(m_ref[:, k * dq:(k + 1) * dq].astype(w_ref.dtype), w_ref[off // dq + k],
                                    preferred_element_type=F32)
            off += wd
        x1_ref[...] = acc
        xn, _ = _rms_fwd(acc)
        h2_ref[...] = (xn * g_ref[...]).astype(h2_ref.dtype)

    return pl.pallas_call(
        body, name="out_proj_fwd", grid=(t // tm,),
        in_specs=[pl.BlockSpec((tm, d), lambda i: (i, 0))]
                 + [pl.BlockSpec((tm, wd), lambda i: (i, 0)) for wd in widths]
                 + [pl.BlockSpec((4, dq, d), lambda i: (0, 0, 0)), pl.BlockSpec((1, d), lambda i: (0, 0))],
        out_specs=[pl.BlockSpec((tm, d), lambda i: (i, 0)), pl.BlockSpec((tm, d), lambda i: (i, 0))],
        out_shape=[jax.ShapeDtypeStruct((t, d), F32), jax.ShapeDtypeStruct((t, d), MM_DTYPE)],
        compiler_params=_params("arbitrary"))(x0, *mix, wo, g2)


def _out_proj_bwd(dx2, dh2, x1, g2, mix, wo, tm):
    t, d = x1.shape
    dq = wo.shape[1]
    widths = [m.shape[1] for m in mix]
    nm = len(mix)

    def body(dx2_ref, dh2_ref, x1_ref, g_ref, *rest):
        m_refs = rest[:nm]
        w_ref = rest[nm]
        dx1_ref = rest[nm + 1]
        dm_refs = rest[nm + 2:nm + 2 + nm]
        dw_ref, dg_ref = rest[nm + 2 + nm:]
        i = pl.program_id(0)

        @pl.when(i == 0)
        def _():
            dw_ref[...] = jnp.zeros_like(dw_ref)
            dg_ref[...] = jnp.zeros_like(dg_ref)

        xn, r = _rms_fwd(x1_ref[...])
        dh2v = dh2_ref[...]
        dg_ref[...] += jnp.sum(dh2v * xn, axis=0, keepdims=True)
        dx1 = dx2_ref[...] + _rms_bwd(dh2v * g_ref[...], xn, r)
        dx1_ref[...] = dx1
        dx1c = dx1.astype(w_ref.dtype)
        off = 0
        for m_ref, dm_ref, wd in zip(m_refs, dm_refs, widths):
            for k in range(wd // dq):
                j = off // dq + k
                cols = slice(k * dq, (k + 1) * dq)
                dm_ref[:, cols] = lax.dot_general(dx1c, w_ref[j], (((1,), (1,)), ((), ())), preferred_element_type=F32)
                dw_ref[j] += lax.dot_general(m_ref[:, cols].astype(w_ref.dtype), dx1c, (((0,), (0,)), ((), ())),
                                             preferred_element_type=F32)
            off += wd

    tile = lambda wd: pl.BlockSpec((tm, wd), lambda i: (i, 0))
    return pl.pallas_call(
        body, name="out_proj_bwd", grid=(t // tm,),
        in_specs=[tile(d), tile(d), tile(d), pl.BlockSpec((1, d), lambda i: (0, 0))]
                 + [tile(wd) for wd in widths] + [pl.BlockSpec((4, dq, d), lambda i: (0, 0, 0))],
        out_specs=[tile(d)] + [tile(wd) for wd in widths]
                  + [pl.BlockSpec((4, dq, d), lambda i: (0, 0, 0)), pl.BlockSpec((1, d), lambda i: (0, 0))],
        out_shape=[jax.ShapeDtypeStruct((t, d), F32)] + [jax.ShapeDtypeStruct((t, wd), F32) for wd in widths]
                  + [jax.ShapeDtypeStruct((4, dq, d), F32), jax.ShapeDtypeStruct((1, d), F32)],
        compiler_params=_params("arbitrary"))(dx2, dh2, x1, g2, *mix, wo)


def _ffn_fwd(x1, h2, wg, wu, wd, tm):
    t, d = x1.shape
    fs = wg.shape[1]

    def body(x1_ref, h2_ref, wg_ref, wu_ref, wd_ref, x2_ref, gp_ref, up_ref):
        @pl.when(pl.program_id(1) == 0)
        def _():
            x2_ref[...] = x1_ref[...]

        h = h2_ref[...]
        nt = (((1,), (1,)), ((), ()))
        gp = lax.dot_general(h, wg_ref[...], nt, preferred_element_type=F32)
        up = lax.dot_general(h, wu_ref[...], nt, preferred_element_type=F32)
        gp_ref[...] = gp
        up_ref[...] = up
        ff = gp * _sigmoid(gp) * up
        x2_ref[...] += jnp.dot(ff.astype(wd_ref.dtype), wd_ref[...], preferred_element_type=F32)

    return pl.pallas_call(
        body, name="ffn_fwd", grid=(t // tm, 4),
        in_specs=[pl.BlockSpec((tm, d), lambda i, j: (i, 0)), pl.BlockSpec((tm, d), lambda i, j: (i, 0)),
                  pl.BlockSpec((None, fs, d), lambda i, j: (j, 0, 0)),
                  pl.BlockSpec((None, fs, d), lambda i, j: (j, 0, 0)),
                  pl.BlockSpec((None, fs, d), lambda i, j: (j, 0, 0))],
        out_specs=[pl.BlockSpec((tm, d), lambda i, j: (i, 0)), pl.BlockSpec((None, tm, fs), lambda i, j: (j, i, 0)),
                   pl.BlockSpec((None, tm, fs), lambda i, j: (j, i, 0))],
        out_shape=[jax.ShapeDtypeStruct((t, d), F32), jax.ShapeDtypeStruct((4, t, fs), F32),
                   jax.ShapeDtypeStruct((4, t, fs), F32)],
        compiler_params=_params("arbitrary", "arbitrary"))(x1, h2, wg, wu, wd)


def _ffn_bwd(dx2, h2, gp, up, wg, wu, wd, tm):
    t, d = dx2.shape
    fs = wg.shape[1]

    def body(dx2_ref, h2_ref, gp_ref, up_ref, wg_ref, wu_ref, wd_ref, dh2_ref, dwg_ref, dwu_ref, dwd_ref):
        j, i = pl.program_id(0), pl.program_id(1)

        @pl.when(i == 0)
        def _():
            dwg_ref[...] = jnp.zeros_like(dwg_ref)
            dwu_ref[...] = jnp.zeros_like(dwu_ref)
            dwd_ref[...] = jnp.zeros_like(dwd_ref)

        cdt = wg_ref.dtype
        h = h2_ref[...]
        gpv, upv = gp_ref[...], up_ref[...]
        s = _sigmoid(gpv)
        silu = gpv * s
        dx2c = dx2_ref[...].astype(cdt)
        dff = lax.dot_general(dx2c, wd_ref[...], (((1,), (1,)), ((), ())), preferred_element_type=F32)
        dwd_ref[...] += lax.dot_general((silu * upv).astype(cdt), dx2c, (((0,), (0,)), ((), ())), preferred_element_type=F32)
        dup = (dff * silu).astype(cdt)
        dgp = (dff * upv * _dsilu(gpv, s)).astype(cdt)
        dwg_ref[...] += lax.dot_general(dgp, h, (((0,), (0,)), ((), ())), preferred_element_type=F32)
        dwu_ref[...] += lax.dot_general(dup, h, (((0,), (0,)), ((), ())), preferred_element_type=F32)
        dh = (jnp.dot(dgp, wg_ref[...], preferred_element_type=F32) + jnp.dot(dup, wu_ref[...], preferred_element_type=F32))
        rows = pl.ds(pl.multiple_of(i * tm, tm), tm)

        @pl.when(j == 0)
        def _():
            dh2_ref[rows, :] = dh

        @pl.when(j != 0)
        def _():
            dh2_ref[rows, :] += dh

    return pl.pallas_call(
        body, name="ffn_bwd", grid=(4, t // tm),
        in_specs=[pl.BlockSpec((tm, d), lambda j, i: (i, 0)), pl.BlockSpec((tm, d), lambda j, i: (i, 0)),
                  pl.BlockSpec((None, tm, fs), lambda j, i: (j, i, 0)), pl.BlockSpec((None, tm, fs), lambda j, i: (j, i, 0)),
                  pl.BlockSpec((None, fs, d), lambda j, i: (j, 0, 0)),
                  pl.BlockSpec((None, fs, d), lambda j, i: (j, 0, 0)),
                  pl.BlockSpec((None, fs, d), lambda j, i: (j, 0, 0))],
        out_specs=[pl.BlockSpec((t, d), lambda j, i: (0, 0)), pl.BlockSpec((None, fs, d), lambda j, i: (j, 0, 0)),
                   pl.BlockSpec((None, fs, d), lambda j, i: (j, 0, 0)), pl.BlockSpec((None, fs, d), lambda j, i: (j, 0, 0))],
        out_shape=[jax.ShapeDtypeStruct((t, d), F32)] + [jax.ShapeDtypeStruct((4, fs, d), F32)] * 3,
        compiler_params=_params("arbitrary", "arbitrary"))(dx2, h2, gp, up, wg, wu, wd)


def _ple_fwd(x2, p, wpg, wpp, tm):
    t, d = x2.shape
    q = p.shape[1]
    dq = d // 4

    def body(x_ref, p_ref, wg_ref, wp_ref, o_ref):
        xv = x_ref[...]
        xc = xv.astype(wg_ref.dtype)
        pc = p_ref[...].astype(wp_ref.dtype)
        pre = jnp.dot(xc[:, :dq], wg_ref[0], preferred_element_type=F32)
        for j in range(1, 4):
            pre = pre + jnp.dot(xc[:, j * dq:(j + 1) * dq], wg_ref[j], preferred_element_type=F32)
        gate = _sigmoid(pre)
        for j in range(4):
            cols = slice(j * dq, (j + 1) * dq)
            o_ref[:, cols] = xv[:, cols] + gate[:, cols] * jnp.dot(pc, wp_ref[j], preferred_element_type=F32)

    return pl.pallas_call(
        body, name="ple_fwd", grid=(t // tm,),
        in_specs=[pl.BlockSpec((tm, d), lambda i: (i, 0)), pl.BlockSpec((tm, q), lambda i: (i, 0)),
                  pl.BlockSpec((4, dq, d), lambda i: (0, 0, 0)),
                  pl.BlockSpec((4, q, dq), lambda i: (0, 0, 0))],
        out_specs=pl.BlockSpec((tm, d), lambda i: (i, 0)),
        out_shape=jax.ShapeDtypeStruct((t, d), F32),
        compiler_params=_params("arbitrary"))(x2, p, wpg, wpp)


def _ple_bwd(dx3, x2, p, wpg, wpp, tm):
    t, d = x2.shape
    q = p.shape[1]
    dq = d // 4

    def body(dx3_ref, x_ref, p_ref, wg_ref, wp_ref, dx2_ref, dwg_ref, dwp_ref):
        @pl.when(pl.program_id(0) == 0)
        def _():
            dwg_ref[...] = jnp.zeros_like(dwg_ref)
            dwp_ref[...] = jnp.zeros_like(dwp_ref)

        cdt = wg_ref.dtype
        xc = x_ref[...].astype(cdt)
        pc = p_ref[...].astype(cdt)
        pre = jnp.dot(xc[:, :dq], wg_ref[0], preferred_element_type=F32)
        for j in range(1, 4):
            pre = pre + jnp.dot(xc[:, j * dq:(j + 1) * dq], wg_ref[j], preferred_element_type=F32)
        gate = _sigmoid(pre)
        dx3v = dx3_ref[...]
        dpp = (dx3v * gate).astype(cdt)
        dgate = dx3v * gate * (1.0 - gate)
        dpre_parts = []
        for j in range(4):
            cols = slice(j * dq, (j + 1) * dq)
            pp_j = jnp.dot(pc, wp_ref[j], preferred_element_type=F32)
            dpre_parts.append((dgate[:, cols] * pp_j).astype(cdt))
            dwp_ref[j] += lax.dot_general(pc, dpp[:, cols], (((0,), (0,)), ((), ())), preferred_element_type=F32)
        dpre = jnp.concatenate(dpre_parts, axis=1)
        for j in range(4):
            cols = slice(j * dq, (j + 1) * dq)
            dwg_ref[j] += lax.dot_general(xc[:, cols], dpre, (((0,), (0,)), ((), ())), preferred_element_type=F32)
            dx2_ref[:, cols] = dx3v[:, cols] + lax.dot_general(dpre, wg_ref[j], (((1,), (1,)), ((), ())),
                                                               preferred_element_type=F32)

    return pl.pallas_call(
        body, name="ple_bwd", grid=(t // tm,),
        in_specs=[pl.BlockSpec((tm, d), lambda i: (i, 0)), pl.BlockSpec((tm, d), lambda i: (i, 0)),
                  pl.BlockSpec((tm, q), lambda i: (i, 0)), pl.BlockSpec((4, dq, d), lambda i: (0, 0, 0)),
                  pl.BlockSpec((4, q, dq), lambda i: (0, 0, 0))],
        out_specs=[pl.BlockSpec((tm, d), lambda i: (i, 0)), pl.BlockSpec((4, dq, d), lambda i: (0, 0, 0)),
                   pl.BlockSpec((4, q, dq), lambda i: (0, 0, 0))],
        out_shape=[jax.ShapeDtypeStruct((t, d), F32), jax.ShapeDtypeStruct((4, dq, d), F32),
                   jax.ShapeDtypeStruct((4, q, dq), F32)],
        compiler_params=_params("arbitrary"))(dx3, x2, p, wpg, wpp)


def _loss_head(x, target, fg, tm):
    t, d = x.shape

    def body(x_ref, t_ref, g_ref, dx_ref, loss_ref, dg_ref):
        @pl.when(pl.program_id(0) == 0)
        def _():
            loss_ref[...] = jnp.zeros_like(loss_ref)
            dg_ref[...] = jnp.zeros_like(dg_ref)

        xn, r = _rms_fwd(x_ref[...])
        g = g_ref[...]
        err = xn * g - t_ref[...]
        loss_ref[...] += 0.5 * jnp.sum(jnp.sum(err * err, axis=-1, keepdims=True) / d, axis=0, keepdims=True)
        dy = err / d
        dg_ref[...] += jnp.sum(dy * xn, axis=0, keepdims=True)
        dx_ref[...] = _rms_bwd(dy * g, xn, r)

    return pl.pallas_call(
        body, name="loss_head", grid=(t // tm,),
        in_specs=[pl.BlockSpec((tm, d), lambda i: (i, 0)), pl.BlockSpec((tm, d), lambda i: (i, 0)),
                  pl.BlockSpec((1, d), lambda i: (0, 0))],
        out_specs=[pl.BlockSpec((tm, d), lambda i: (i, 0)), pl.BlockSpec((1, 1), lambda i: (0, 0)),
                   pl.BlockSpec((1, d), lambda i: (0, 0))],
        out_shape=[jax.ShapeDtypeStruct((t, d), F32), jax.ShapeDtypeStruct((1, 1), F32),
                   jax.ShapeDtypeStruct((1, d), F32)],
        compiler_params=_params("arbitrary"))(x, target, fg)


def _qkv_conv_act(xv, w, j, heads):
    k = QKV_CONV_WIDTH
    y = w[k - 1:k] * xv
    for s in range(1, k):
        y = y + w[k - 1 - s:k - s] * _shift_down(xv, s)
    sg = _sigmoid(y)
    s_act = y * sg
    nrm = lax.rsqrt(jnp.sum(s_act * s_act, axis=-1, keepdims=True) + EPS)
    scale = jnp.where(j < heads, HEAD_DIM ** -0.5, 1.0).astype(F32)
    return y, sg, s_act, nrm, scale


def _qkv_conv_fwd(qkv_pre, conv_w, heads):
    t = qkv_pre.shape[0]
    nblk = 3 * heads

    def body(x_ref, w_ref, o_ref):
        j = pl.program_id(0)
        _, _, s_act, nrm, scale = _qkv_conv_act(x_ref[...], w_ref[...], j, heads)
        o_ref[...] = jnp.where(j < 2 * heads, s_act * (nrm * scale), s_act)

    return pl.pallas_call(
        body, name="qkv_conv_fwd", grid=(nblk,),
        in_specs=[pl.BlockSpec((t, LANES), lambda j: (0, j)), pl.BlockSpec((QKV_CONV_WIDTH, LANES), lambda j: (0, j))],
        out_specs=pl.BlockSpec((t, LANES), lambda j: (0, j)),
        out_shape=jax.ShapeDtypeStruct(qkv_pre.shape, F32),
        compiler_params=_params("arbitrary"))(qkv_pre, conv_w)


def _qkv_conv_bwd(qkv_pre, conv_w, dqkv, heads):
    t = qkv_pre.shape[0]
    nblk = 3 * heads
    k = QKV_CONV_WIDTH

    def body(x_ref, w_ref, dn_ref, dx_ref, dw_ref):
        j = pl.program_id(0)
        xv, w = x_ref[...], w_ref[...]
        y, sg, s_act, nrm, scale = _qkv_conv_act(xv, w, j, heads)
        dn = dn_ref[...]
        dsn = dn * scale
        ds_qk = nrm * dsn - s_act * (nrm * nrm * nrm) * jnp.sum(dsn * s_act, axis=-1, keepdims=True)
        ds = jnp.where(j < 2 * heads, ds_qk, dn)
        dy = ds * _dsilu(y, sg)
        dx = w[k - 1:k] * dy
        dw_ref[k - 1:k, :] = jnp.sum(dy * xv, axis=0, keepdims=True)
        for s in range(1, k):
            dx = dx + w[k - 1 - s:k - s] * _shift_up(dy, s)
            dw_ref[k - 1 - s:k - s, :] = jnp.sum(dy * _shift_down(xv, s), axis=0, keepdims=True)
        dx_ref[...] = dx

    return pl.pallas_call(
        body, name="qkv_conv_bwd", grid=(nblk,),
        in_specs=[pl.BlockSpec((t, LANES), lambda j: (0, j)), pl.BlockSpec((k, LANES), lambda j: (0, j)),
                  pl.BlockSpec((t, LANES), lambda j: (0, j))],
        out_specs=[pl.BlockSpec((t, LANES), lambda j: (0, j)), pl.BlockSpec((k, LANES), lambda j: (0, j))],
        out_shape=[jax.ShapeDtypeStruct(qkv_pre.shape, F32), jax.ShapeDtypeStruct(conv_w.shape, F32)],
        compiler_params=_params("arbitrary"))(qkv_pre, conv_w, dqkv)


def _pool_windows(shape, j, group_dim):
    lane = lax.broadcasted_iota(jnp.int32, shape, 1) + j * LANES
    grp = lane // group_dim
    win = jnp.left_shift(2, grp).astype(F32)
    cnt = jnp.minimum((_rows(shape) + 1).astype(F32), win)
    return grp, cnt


def _pool_select(grp, levels):
    out = levels[0]
    for gi in range(1, POOL_GROUPS):
        out = jnp.where(grp == gi, levels[gi], out)
    return out


def _pool_mean(hv, grp, cnt):
    acc, levels, width = hv, [], 1
    for _ in range(POOL_GROUPS):
        acc = acc + _shift_down(acc, width)
        width *= 2
        levels.append(acc)
    return _pool_select(grp, levels) / cnt - hv


def _pool_fwd(hp, wbd, scale, group_dim):
    t, dp = hp.shape

    def body(h_ref, w_ref, s_ref, o_ref):
        hv = h_ref[...]
        grp, cnt = _pool_windows(hv.shape, pl.program_id(0), group_dim)
        pooled = _pool_mean(hv, grp, cnt)
        o_ref[...] = _mm(pooled, w_ref[...]) * s_ref[...]

    return pl.pallas_call(
        body, name="pool_fwd", grid=(dp // LANES,),
        in_specs=[pl.BlockSpec((t, LANES), lambda j: (0, j)), pl.BlockSpec((LANES, LANES), lambda j: (j, j)),
                  pl.BlockSpec((1, LANES), lambda j: (0, j))],
        out_specs=pl.BlockSpec((t, LANES), lambda j: (0, j)),
        out_shape=jax.ShapeDtypeStruct(hp.shape, F32),
        compiler_params=_params("arbitrary"))(hp, wbd, scale)


def _pool_bwd(hp, wbd, scale, dob, group_dim):
    t, dp = hp.shape

    def body(h_ref, w_ref, s_ref, do_ref, dh_ref, dw_ref, ds_ref):
        hv = h_ref[...]
        grp, cnt = _pool_windows(hv.shape, pl.program_id(0), group_dim)
        pooled = _pool_mean(hv, grp, cnt)
        wv = w_ref[...]
        dov = do_ref[...]
        ds_ref[...] = jnp.sum(dov * _mm(pooled, wv), axis=0, keepdims=True)
        dys = dov * s_ref[...]
        dw_ref[0] = _mm_tn(pooled, dys)
        dpooled = _mm_nt(dys, wv)
        acc, levels, width = dpooled / cnt, [], 1
        for _ in range(POOL_GROUPS):
            acc = acc + _shift_up(acc, width)
            width *= 2
            levels.append(acc)
        dh_ref[...] = _pool_select(grp, levels) - dpooled

    nb = dp // LANES
    return pl.pallas_call(
        body, name="pool_bwd", grid=(nb,),
        in_specs=[pl.BlockSpec((t, LANES), lambda j: (0, j)), pl.BlockSpec((LANES, LANES), lambda j: (j, j)),
                  pl.BlockSpec((1, LANES), lambda j: (0, j)), pl.BlockSpec((t, LANES), lambda j: (0, j))],
        out_specs=[pl.BlockSpec((t, LANES), lambda j: (0, j)), pl.BlockSpec((1, LANES, LANES), lambda j: (j, 0, 0)),
                   pl.BlockSpec((1, LANES), lambda j: (0, j))],
        out_shape=[jax.ShapeDtypeStruct(hp.shape, F32), jax.ShapeDtypeStruct((nb, LANES, LANES), F32),
                   jax.ShapeDtypeStruct((1, dp), F32)],
        compiler_params=_params("arbitrary"))(hp, wbd, scale, dob)


def _sconv_fwd(cbcch, w):
    t, dc3 = cbcch.shape
    nb = dc3 // 3 // LANES
    k = SCONV_WIDTH

    def body(b_ref, c_ref, h_ref, w_ref, o_ref):
        m = c_ref[...] * h_ref[...]
        wv = w_ref[...]
        y = wv[k - 1:k] * m
        for s in range(1, k):
            y = y + wv[k - 1 - s:k - s] * _shift_down(m, s)
        o_ref[...] = b_ref[...] * y

    return pl.pallas_call(
        body, name="sconv_fwd", grid=(nb,),
        in_specs=[pl.BlockSpec((t, LANES), lambda j: (0, j)), pl.BlockSpec((t, LANES), lambda j: (0, nb + j)),
                  pl.BlockSpec((t, LANES), lambda j: (0, 2 * nb + j)), pl.BlockSpec((k, LANES), lambda j: (0, j))],
        out_specs=pl.BlockSpec((t, LANES), lambda j: (0, j)),
        out_shape=jax.ShapeDtypeStruct((t, dc3 // 3), F32),
        compiler_params=_params("arbitrary"))(cbcch, cbcch, cbcch, w)


def _sconv_bwd(cbcch, w, doc):
    t, dc3 = cbcch.shape
    nb = dc3 // 3 // LANES
    k = SCONV_WIDTH

    def body(b_ref, c_ref, h_ref, w_ref, do_ref, db_ref, dc_ref, dh_ref, dw_ref):
        cv, hv = c_ref[...], h_ref[...]
        m = cv * hv
        wv = w_ref[...]
        dov = do_ref[...]
        dy = dov * b_ref[...]
        y = wv[k - 1:k] * m
        dm = wv[k - 1:k] * dy
        dw_ref[k - 1:k, :] = jnp.sum(dy * m, axis=0, keepdims=True)
        for s in range(1, k):
            ms = _shift_down(m, s)
            y = y + wv[k - 1 - s:k - s] * ms
            dm = dm + wv[k - 1 - s:k - s] * _shift_up(dy, s)
            dw_ref[k - 1 - s:k - s, :] = jnp.sum(dy * ms, axis=0, keepdims=True)
        db_ref[...] = dov * y
        dc_ref[...] = dm * hv
        dh_ref[...] = dm * cv

    col = lambda o: pl.BlockSpec((t, LANES), lambda j: (0, o * nb + j))
    return pl.pallas_call(
        body, name="sconv_bwd", grid=(nb,),
        in_specs=[col(0), col(1), col(2), pl.BlockSpec((k, LANES), lambda j: (0, j)), col(0)],
        out_specs=[col(0), col(0), col(0), pl.BlockSpec((k, LANES), lambda j: (0, j))],
        out_shape=[jax.ShapeDtypeStruct((t, dc3 // 3), F32)] * 3 + [jax.ShapeDtypeStruct(w.shape, F32)],
        compiler_params=_params("arbitrary"))(cbcch, cbcch, cbcch, w, doc)


class _Split:
    def __init__(self, a, exact=False):
        self.hi = a.astype(jnp.bfloat16)
        self.lo = None if exact else (a - self.hi.astype(F32)).astype(jnp.bfloat16)


def _per_head(dims, a, b):
    a = a if isinstance(a, _Split) else _Split(a)
    b = b if isinstance(b, _Split) else _Split(b)

    def dot(x, y):
        return lax.dot_general(x, y, (dims, ((), ())), preferred_element_type=F32)

    def head(h):
        out = dot(a.hi[h], b.hi[h])
        for x, y in ((a.hi, b.lo), (a.lo, b.hi)):
            out = out if x is None or y is None else out + dot(x[h], y[h])
        return out

    return jnp.stack([head(h) for h in range(a.hi.shape[0])])


def _bmm(a, b):
    return _per_head(((1,), (0,)), a, b)


def _bmm_nt(a, b):
    return _per_head(((1,), (1,)), a, b)


def _bmm_tn(a, b):
    return _per_head(((0,), (0,)), a, b)


def _inv_unit_lower(low):
    c = low.shape[-1]
    eye = (_rows((c, c)) == lax.broadcasted_iota(jnp.int32, (c, c), 1)).astype(F32)
    pw = -low
    inv = eye + pw
    span = 2
    while span < c:
        pws = _Split(pw)
        pw = _bmm(pws, pws)
        inv = inv + _bmm(inv, pw)
        span *= 2
    return inv


def _heads_of(ref, base, heads):
    return jnp.stack([ref[:, base + h * HEAD_DIM:base + (h + 1) * HEAD_DIM] for h in range(heads)])


def _chunk_common(q, k, v, a_col, b_col, alog, dtb, kept=None):
    hn, c, _ = q.shape
    beta = _sigmoid(b_col)
    xg = a_col + dtb
    softplus = jnp.maximum(xg, 0.0) + jnp.log(1.0 + jnp.exp(-jnp.abs(xg)))
    neg_ea = -jnp.exp(alog)
    g = neg_ea * softplus
    ri = _rows((c, c))
    ci = lax.broadcasted_iota(jnp.int32, (c, c), 1)
    incl, strict = ri >= ci, ri > ci
    inclf = _Split(jnp.broadcast_to(incl.astype(F32), (hn, c, c)), exact=True)
    gcb = _bmm(inclf, jnp.broadcast_to(g, (hn, c, HEAD_DIM)))
    gc_row = jnp.sum(jnp.where(ri <= ci, jnp.broadcast_to(g, (hn, c, c)), 0.0), axis=1, keepdims=True)
    dmat = jnp.where(incl, jnp.exp(jnp.where(incl, gcb[:, :, :1] - gc_row, 0.0)), 0.0)
    eg = jnp.exp(gcb)
    gl = gcb[:, c - 1:c, :]
    egl = jnp.exp(gl)
    edl = jnp.exp(gl - gcb)
    kb, vb = k * beta, v * beta
    kbe = kb * eg
    if kept is None:
        ks = _Split(k)
        a0 = _bmm_nt(kb, ks)
        tm = _inv_unit_lower(jnp.where(strict, a0 * dmat, 0.0))
        p0 = _bmm_nt(q, ks)
        tms = _Split(tm)
        u, w = _bmm(tms, vb), _bmm(tms, kbe)
    else:
        (a0, tm, p0, w), u = kept, None
    return dict(beta=beta, xg=xg, neg_ea=neg_ea, g=g, incl=incl, strict=strict, inclf=inclf, dmat=dmat, eg=eg,
                egl=egl, edl=edl, kb=kb, vb=vb, a0=a0, tm=tm, kbe=kbe, u=u, w=w, p0=p0,
                attn=p0 * dmat, qe=q * eg, kd=k * edl)


def _chunk_step(cm, state):
    ss = _Split(state)
    vn = cm["u"] - _bmm(cm["w"], ss)
    vns = _Split(vn)
    o = _bmm(cm["qe"], ss) + _bmm(cm["attn"], vns)
    new_state = state * cm["egl"][:, :, :1] + _bmm_tn(cm["kd"], vns)
    return vn, o, new_state


def _gated_norm(o, zv, og):
    xo, ro = _rms_fwd(o)
    sgz = _sigmoid(zv)
    return xo, ro, sgz, xo * og * (zv * sgz)


def _gate_columns(abv, gpv, heads):
    a_col = jnp.stack([abv[:, h:h + 1] for h in range(heads)])
    b_col = jnp.stack([abv[:, heads + h:heads + h + 1] for h in range(heads)])
    alog = jnp.stack([gpv[0:1, h:h + 1] for h in range(heads)])
    dtb = jnp.stack([gpv[1:2, h:h + 1] for h in range(heads)])
    return a_col, b_col, alog, dtb


def _delta_fwd(qkv, z, ab, gpar, heads):
    t = qkv.shape[0]
    da = heads * HEAD_DIM
    n = t // CHUNK

    def body(qkv_ref, z_ref, ab_ref, gp_ref, oa_ref, st_ref, kc_ref, kw_ref, s_ref):
        @pl.when(pl.program_id(0) == 0)
        def _():
            s_ref[...] = jnp.zeros_like(s_ref)

        gpv = gp_ref[...]
        cm = _chunk_common(_heads_of(qkv_ref, 0, heads), _heads_of(qkv_ref, da, heads), _heads_of(qkv_ref, 2 * da, heads),
                           *_gate_columns(ab_ref[...], gpv, heads))
        state = s_ref[...]
        st_ref[0] = state
        vn, o, new_state = _chunk_step(cm, state)
        s_ref[...] = new_state
        for slot, val in enumerate((cm["a0"], cm["tm"], cm["p0"])):
            kc_ref[0, slot] = val
        for slot, val in enumerate((cm["w"], vn, o)):
            kw_ref[0, slot] = val
        oa = _gated_norm(o, _heads_of(z_ref, 0, heads), gpv[2:3, :])[3]
        for h in range(heads):
            oa_ref[:, h * HEAD_DIM:(h + 1) * HEAD_DIM] = oa[h]

    return pl.pallas_call(
        body, name="delta_fwd", grid=(n,),
        in_specs=[pl.BlockSpec((CHUNK, 3 * da), lambda i: (i, 0)), pl.BlockSpec((CHUNK, da), lambda i: (i, 0)),
                  pl.BlockSpec((CHUNK, LANES), lambda i: (i, 0)), pl.BlockSpec((8, LANES), lambda i: (0, 0))],
        out_specs=[pl.BlockSpec((CHUNK, da), lambda i: (i, 0)),
                   pl.BlockSpec((1, heads, HEAD_DIM, HEAD_DIM), lambda i: (i, 0, 0, 0)),
                   pl.BlockSpec((1, 3, heads, CHUNK, CHUNK), lambda i: (i, 0, 0, 0, 0)),
                   pl.BlockSpec((1, 3, heads, CHUNK, HEAD_DIM), lambda i: (i, 0, 0, 0, 0))],
        out_shape=[jax.ShapeDtypeStruct((t, da), F32), jax.ShapeDtypeStruct((n, heads, HEAD_DIM, HEAD_DIM), F32),
                   jax.ShapeDtypeStruct((n, 3, heads, CHUNK, CHUNK), F32),
                   jax.ShapeDtypeStruct((n, 3, heads, CHUNK, HEAD_DIM), F32)],
        scratch_shapes=[pltpu.VMEM((heads, HEAD_DIM, HEAD_DIM), F32)],
        compiler_params=_params("arbitrary"))(qkv, z, ab, gpar)


def _delta_bwd(qkv, z, ab, gpar, states, kept_c, kept_w, doa, heads):
    t = qkv.shape[0]
    da = heads * HEAD_DIM
    n = t // CHUNK
    c = CHUNK

    def body(qkv_ref, z_ref, ab_ref, gp_ref, st_ref, kc_ref, kw_ref, doa_ref, dqkv_ref, dz_ref, dab_ref, dpar_ref, ds_ref):
        @pl.when(pl.program_id(0) == 0)
        def _():
            ds_ref[...] = jnp.zeros_like(ds_ref)
            dpar_ref[...] = jnp.zeros_like(dpar_ref)

        gpv = gp_ref[...]
        og = gpv[2:3, :]
        q, k, v = _heads_of(qkv_ref, 0, heads), _heads_of(qkv_ref, da, heads), _heads_of(qkv_ref, 2 * da, heads)
        cm = _chunk_common(q, k, v, *_gate_columns(ab_ref[...], gpv, heads),
                           kept=(kc_ref[0, 0], kc_ref[0, 1], kc_ref[0, 2], kw_ref[0, 0]))
        state = st_ref[0]
        dsp = ds_ref[...]
        vn, o = kw_ref[0, 1], kw_ref[0, 2]
        zv = _heads_of(z_ref, 0, heads)
        xo, ro, sgz, _ = _gated_norm(o, zv, og)
        doav = _heads_of(doa_ref, 0, heads)
        don = doav * (zv * sgz)
        dz = doav * (xo * og) * _dsilu(zv, sgz)
        d_og = jnp.sum(jnp.sum(don * xo, axis=1, keepdims=True), axis=0)
        do = _rms_bwd(don * og, xo, ro)
        tm, dmat, eg, edl, egl = cm["tm"], cm["dmat"], cm["eg"], cm["edl"], cm["egl"]
        dos, dsps, sts, tms, ks = _Split(do), _Split(dsp), _Split(state), _Split(tm), _Split(k)
        dvn = _bmm_tn(cm["attn"], dos) + _bmm(cm["kd"], dsps)
        dvns = _Split(dvn)
        dqe = _bmm_nt(dos, sts)
        ds_ref[...] = _bmm_tn(cm["qe"], dos) + dsp * egl[:, :, :1] - _bmm_tn(cm["w"], dvns)
        dattn = _bmm_nt(dos, vn)
        dkd = _bmm_nt(vn, dsps)
        dkd_kd = jnp.sum(dkd * cm["kd"], axis=-1, keepdims=True)
        dgl = (jnp.sum(jnp.sum(dsp * state, axis=-1, keepdims=True), axis=1, keepdims=True) * egl[:, :, :1]
               + jnp.sum(dkd_kd, axis=1, keepdims=True))
        dgc = jnp.sum(dqe * cm["qe"], axis=-1, keepdims=True) - dkd_kd
        dk = dkd * edl
        dq = dqe * eg
        dw = -_bmm_nt(dvns, sts)
        dws = _Split(dw)
        dp0 = dattn * dmat
        dd = jnp.where(cm["incl"], dattn * cm["p0"], 0.0)
        dp0s = _Split(dp0)
        dq = dq + _bmm(dp0s, ks)
        dk = dk + _bmm_tn(dp0s, q)
        dtm = _bmm_nt(dvns, cm["vb"]) + _bmm_nt(dws, cm["kbe"])
        dvb = _bmm_tn(tms, dvns)
        dkbe = _bmm_tn(tms, dws)
        dkb = dkbe * eg
        dgc = dgc + jnp.sum(dkbe * cm["kbe"], axis=-1, keepdims=True)
        dlow = jnp.where(cm["strict"], -_bmm_tn(tms, _bmm_nt(dtm, tms)), 0.0)
        dd = dd + dlow * cm["a0"]
        da0 = dlow * dmat
        da0s = _Split(da0)
        dkb = dkb + _bmm(da0s, ks)
        dk = dk + _bmm_tn(da0s, cm["kb"])
        ddd = dd * dmat
        ones = _Split(jnp.ones((heads, c, HEAD_DIM), F32), exact=True)
        dgc = dgc + jnp.sum(ddd, axis=-1, keepdims=True) - _bmm_tn(ddd, ones)[:, :, :1]
        dgc = dgc + jnp.where(_rows((c, 1)) == c - 1, dgl, 0.0)
        dg = _bmm_tn(cm["inclf"], jnp.broadcast_to(dgc, (heads, c, HEAD_DIM)))[:, :, :1]
        beta = cm["beta"]
        dk = dk + dkb * beta
        dbeta = jnp.sum(dkb * k, axis=-1, keepdims=True) + jnp.sum(dvb * v, axis=-1, keepdims=True)
        dv = dvb * beta
        db_col = dbeta * beta * (1.0 - beta)
        da_col = dg * cm["neg_ea"] * _sigmoid(cm["xg"])
        d_alog = jnp.sum(dg * cm["g"], axis=1, keepdims=True)
        d_dtb = jnp.sum(da_col, axis=1, keepdims=True)
        lane = lax.broadcasted_iota(jnp.int32, (c, LANES), 1)
        lane8 = lax.broadcasted_iota(jnp.int32, (8, LANES), 1)
        row8 = _rows((8, LANES))
        dab = jnp.zeros((c, LANES), F32)
        dpar = jnp.where(row8 == 2, d_og, 0.0)
        for h in range(heads):
            lo = h * HEAD_DIM
            dqkv_ref[:, lo:lo + HEAD_DIM] = dq[h]
            dqkv_ref[:, da + lo:da + lo + HEAD_DIM] = dk[h]
            dqkv_ref[:, 2 * da + lo:2 * da + lo + HEAD_DIM] = dv[h]
            dz_ref[:, lo:lo + HEAD_DIM] = dz[h]
            dab = dab + jnp.where(lane == h, da_col[h], 0.0) + jnp.where(lane == heads + h, db_col[h], 0.0)
            dpar = (dpar + jnp.where((row8 == 0) & (lane8 == h), d_alog[h], 0.0)
                    + jnp.where((row8 == 1) & (lane8 == h), d_dtb[h], 0.0))
        dab_ref[...] = dab
        dpar_ref[...] += dpar

    rev = lambda i: (n - 1 - i, 0)
    return pl.pallas_call(
        body, name="delta_bwd", grid=(n,),
        in_specs=[pl.BlockSpec((c, 3 * da), rev), pl.BlockSpec((c, da), rev), pl.BlockSpec((c, LANES), rev),
                  pl.BlockSpec((8, LANES), lambda i: (0, 0)),
                  pl.BlockSpec((1, heads, HEAD_DIM, HEAD_DIM), lambda i: (n - 1 - i, 0, 0, 0)),
                  pl.BlockSpec((1, 3, heads, c, c), lambda i: (n - 1 - i, 0, 0, 0, 0)),
                  pl.BlockSpec((1, 3, heads, c, HEAD_DIM), lambda i: (n - 1 - i, 0, 0, 0, 0)),
                  pl.BlockSpec((c, da), rev)],
        out_specs=[pl.BlockSpec((c, 3 * da), rev), pl.BlockSpec((c, da), rev), pl.BlockSpec((c, LANES), rev),
                   pl.BlockSpec((8, LANES), lambda i: (0, 0))],
        out_shape=[jax.ShapeDtypeStruct((t, 3 * da), F32), jax.ShapeDtypeStruct((t, da), F32),
                   jax.ShapeDtypeStruct((t, LANES), F32), jax.ShapeDtypeStruct((8, LANES), F32)],
        scratch_shapes=[pltpu.VMEM((heads, HEAD_DIM, HEAD_DIM), F32)],
        compiler_params=_params("arbitrary"))(qkv, z, ab, gpar, states, kept_c, kept_w, doa)


def _w_in_pieces(shard_cols, da, heads):
    a0, nab = 4 * da, 2 * heads
    d_in = 4 * shard_cols
    runs = [(0, a0, 0), (a0, a0 + nab, d_in - nab), (a0 + nab, d_in, a0)]
    pieces = []
    for j in range(4):
        lo, hi = j * shard_cols, (j + 1) * shard_cols
        for rlo, rhi, plo in runs:
            s, e = max(lo, rlo), min(hi, rhi)
            if s < e:
                pieces.append((j, s - lo, e - s, plo + (s - rlo)))
    return pieces, d_in - nab + LANES


def _w_in_pack(w4, da, heads):
    _, d, sc = w4.shape
    pieces, npk = _w_in_pieces(sc, da, heads)
    tr = _tile_rows(d, 256, SUBLANES_WIRE)

    def body(w_ref, o_ref):
        o_ref[:, npk - LANES:] = jnp.zeros((tr, LANES), o_ref.dtype)
        for j, lo, ln, dst in pieces:
            o_ref[:, dst:dst + ln] = w_ref[j, :, lo:lo + ln]

    return pl.pallas_call(
        body, name="w_in_pack", grid=(d // tr,),
        in_specs=[pl.BlockSpec((4, tr, sc), lambda i: (0, i, 0))],
        out_specs=pl.BlockSpec((tr, npk), lambda i: (i, 0)),
        out_shape=jax.ShapeDtypeStruct((d, npk), w4.dtype),
        compiler_params=_params("arbitrary"))(w4)


def _w_in_unpack(dwp, sc, da, heads):
    d, npk = dwp.shape
    pieces, _ = _w_in_pieces(sc, da, heads)
    tr = _tile_rows(d, 256)

    def body(g_ref, o_ref):
        for j, lo, ln, dst in pieces:
            o_ref[j, :, lo:lo + ln] = g_ref[:, dst:dst + ln]

    return pl.pallas_call(
        body, name="w_in_unpack", grid=(d // tr,),
        in_specs=[pl.BlockSpec((tr, npk), lambda i: (i, 0))],
        out_specs=pl.BlockSpec((4, tr, sc), lambda i: (0, i, 0)),
        out_shape=jax.ShapeDtypeStruct((4, d, sc), F32),
        compiler_params=_params("arbitrary"))(dwp)


def _block_diag(pool_w):
    g, gd, _ = pool_w.shape
    out = jnp.zeros((g * gd, g * gd), pool_w.dtype)
    for gi in range(g):
        out = lax.dynamic_update_slice(out, pool_w[gi], (gi * gd, gi * gd))
    return out


def _layer_dims(d):
    heads = (d // 2) // HEAD_DIM
    return heads, heads * HEAD_DIM, d // 4, d // 4


BIG = ("w_in", "w_gate", "w_up", "ple_proj", "w_out", "w_down", "ple_gate")
TRANSPOSED = ("w_gate", "w_up")


def _prepare_layer(small, li):
    d = small["norm1_g"].shape[1]
    heads, _, _, _ = _layer_dims(d)
    gpar = jnp.zeros((8, LANES), F32)
    gpar = gpar.at[0, :heads].set(small["a_log"][li]).at[1, :heads].set(small["dt_bias"][li]).at[2, :].set(small["onorm_g"][li])
    return dict(norm1_g=small["norm1_g"][li][None], conv_qkv=small["conv_qkv"][li], gpar=gpar, pool_bd=_block_diag(small["pool_w"][li]).astype(MM_DTYPE),
                pool_scale=small["pool_scale"][li][None], sconv_w=small["sconv_w"][li], norm2_g=small["norm2_g"][li][None])


def _layer_fwd(x0, p, gw, lw, tm, arrive):
    d = x0.shape[1]
    heads, da, dp, dc = _layer_dims(d)
    segs = (3 * da, da, dp, 3 * dc, LANES)
    lw["w_in_p"] = _w_in_pack(gw["w_in"], da, heads).astype(MM_DTYPE)
    qkv_pre, z, hp, cbcch, ab = _in_proj_fwd(x0, lw["norm1_g"], lw["w_in_p"], segs, tm)
    qkv = _qkv_conv_fwd(qkv_pre, lw["conv_qkv"], heads)
    oa, states, kept_c, kept_w = _delta_fwd(qkv, z, ab, lw["gpar"], heads)
    ob = _pool_fwd(hp, lw["pool_bd"], lw["pool_scale"], dp // POOL_GROUPS)
    oc = _sconv_fwd(cbcch, lw["sconv_w"])
    arrive("mixed", oa)
    x1, h2 = _out_proj_fwd(x0, (oa, ob, oc), gw["w_out"], lw["norm2_g"], tm)
    x2, gp, up = _ffn_fwd(x1, h2, gw["w_gate"], gw["w_up"], gw["w_down"], tm)
    arrive("ffn", x2)
    x3 = _ple_fwd(x2, p, gw["ple_gate"], gw["ple_proj"], tm)
    arrive("end", x3)
    saved = dict(x0=x0, qkv_pre=qkv_pre, z=z, hp=hp, cbcch=cbcch, ab=ab, qkv=qkv, states=states, kept_c=kept_c, kept_w=kept_w, oa=oa, ob=ob, oc=oc,
                 x1=x1, h2=h2, gp=gp, up=up, x2=x2)
    return x3, saved


def _layer_bwd(dx3, p, gw, lw, sv, tm, produced):
    def after_token(tok, arr):
        return arr if tok is None else arr + tok[0, 0]

    d = dx3.shape[1]
    heads, da, dp, dc = _layer_dims(d)
    segs = (3 * da, da, dp, dc, dc, dc, LANES)
    gd = dp // POOL_GROUPS
    dx2, d_ple_gate, d_ple_proj = _ple_bwd(dx3, sv["x2"], p, gw["ple_gate"], gw["ple_proj"], tm)
    dh2, d_w_gate, d_w_up, d_w_down = _ffn_bwd(dx2, sv["h2"], sv["gp"], sv["up"], gw["w_gate"], gw["w_up"], gw["w_down"],
                                               min(tm, 256))
    tok = produced("ffn", dict(w_gate=d_w_gate, w_up=d_w_up, ple_proj=d_ple_proj, w_down=d_w_down, ple_gate=d_ple_gate), dh2)
    dx1, doa, dob, doc, d_w_out, d_norm2 = _out_proj_bwd(dx2, dh2, sv["x1"], after_token(tok, lw["norm2_g"]),
                                                         (sv["oa"], sv["ob"], sv["oc"]), gw["w_out"], tm)
    dcb, dcc, dch, d_sconv = _sconv_bwd(sv["cbcch"], lw["sconv_w"], doc)
    dhp, d_pool_bd, d_pool_scale = _pool_bwd(sv["hp"], lw["pool_bd"], lw["pool_scale"], dob, gd)
    dqkv, dz, dab, dpar = _delta_bwd(sv["qkv"], sv["z"], sv["ab"], lw["gpar"], sv["states"], sv["kept_c"], sv["kept_w"], doa,
                                      heads)
    tok = produced("mixers", {}, dqkv)
    dqkv_pre, d_conv_qkv = _qkv_conv_bwd(sv["qkv_pre"], lw["conv_qkv"], dqkv, heads)
    dsegs = (dqkv_pre, dz, dhp, dcb, dcc, dch, dab)
    dx0, d_w_in_p, d_norm1 = _in_proj_bwd(sv["x0"], after_token(tok, lw["norm1_g"]), lw["w_in_p"], dsegs, dx1, segs, tm)
    per = LANES // gd
    bd = d_pool_bd.reshape(dp // LANES, per, gd, per, gd)
    d_pool_w = jnp.stack([bd[gi // per, gi % per, :, gi % per, :] for gi in range(POOL_GROUPS)])
    big = dict(w_in=_w_in_unpack(d_w_in_p, gw["w_in"].shape[2], da, heads), w_gate=d_w_gate, w_up=d_w_up,
               ple_proj=d_ple_proj, w_out=d_w_out, w_down=d_w_down, ple_gate=d_ple_gate)
    small = dict(norm1_g=d_norm1[0], conv_qkv=d_conv_qkv, a_log=dpar[0, :heads], dt_bias=dpar[1, :heads], onorm_g=dpar[2],
                 pool_w=d_pool_w, pool_scale=d_pool_scale[0], sconv_w=d_sconv, norm2_g=d_norm2[0])
    tok = produced("end", dict(w_in=big["w_in"], w_out=d_w_out), big["w_in"])
    return dx0, big, small, tok


def _local_step(x, p, target, gw, small, produced=None, arrive=None):
    t, d = x.shape
    depth = p.shape[0]
    tm = 512 if t % 512 == 0 else 128
    layers = [_prepare_layer(small, li) for li in range(depth)]
    saved = []
    h = x
    for li in range(depth):
        h, sv = _layer_fwd(h, p[li], gw[li], layers[li], tm,
                           (lambda stage, after, li=li: arrive(li, stage, after)) if arrive else (lambda stage, after: None))
        saved.append(sv)
    dx, loss, d_final = _loss_head(h, target, small["final_g"][None], tm)
    big, sm = [None] * depth, [None] * depth
    token = None
    for li in reversed(range(depth)):
        p_li = p[li] if token is None else p[li] + token[0, 0]
        dx, big[li], sm[li], token = _layer_bwd(
            dx, p_li, gw[li], layers[li], saved[li], tm,
            (lambda stage, grads, after, li=li: produced(li, stage, grads, after)) if produced else (lambda *a: None))
    small_grads = {n: jnp.stack([g[n] for g in sm]) for n in sm[0]}
    small_grads["final_g"] = d_final[0]
    return loss[0, 0], dx, big, small_grads


def _coords():
    return lax.axis_index("x"), lax.axis_index("y"), lax.axis_index("c")


def _other_chips(x, y):
    return [(1 - x, y), (x, 1 - y), (1 - x, 1 - y)]


def _place_shards(ws, me_idx):
    nt = len(ws)
    depth = ws[0].shape[0]

    def body(me_ref, *refs):
        for t, w_ref in enumerate(refs[:nt]):
            for li in range(depth):
                refs[nt + li * nt + t][...] = w_ref[li].astype(WIRE_DTYPE)

    outs = pl.pallas_call(
        body, name="place_shards",
        grid_spec=pltpu.PrefetchScalarGridSpec(
            num_scalar_prefetch=1, grid=(4,),
            in_specs=[pl.BlockSpec((depth, w.shape[1] // 4, w.shape[2]), lambda i, me_ref: (0, i, 0)) for w in ws],
            out_specs=[pl.BlockSpec((None, w.shape[1] // 4, w.shape[2]), lambda i, me_ref: (me_ref[0], i, 0))
                       for _ in range(depth) for w in ws]),
        out_shape=[jax.ShapeDtypeStruct((4,) + w.shape[1:], WIRE_DTYPE) for _ in range(depth) for w in ws],
        compiler_params=_params("arbitrary"))(me_idx, *ws)
    return [list(outs[li * nt:(li + 1) * nt]) for li in range(depth)]


def _half_block(ref, chip, pc):
    rh = ref.shape[1] // 2
    return ref.at[chip, pl.ds(pc * rh, rh)]


def _gather_copies(out_refs, send_sems, recv_sems, stage):
    nt = len(out_refs)
    x, y, c = _coords()
    pairs = []
    for j, (cx, cy) in enumerate(_other_chips(x, y)):
        for t in range(nt):
            sems = dict(send_sem=send_sems[j * nt + t], recv_sem=recv_sems[j * nt + t], device_id_type=MESH)
            if stage == 0:
                mine, theirs, to = _half_block(out_refs[t], 2 * x + y, c), _half_block(out_refs[t], 2 * cx + cy, c), (cx, cy, c)
            else:
                mine, theirs, to = (_half_block(out_refs[t], 2 * cx + cy, c), _half_block(out_refs[t], 2 * cx + cy, 1 - c),
                                    (x, y, 1 - c))
            pairs.append((pltpu.make_async_remote_copy(src_ref=mine, dst_ref=mine, device_id=to, **sems),
                          pltpu.make_async_remote_copy(src_ref=theirs, dst_ref=theirs, device_id=to, **sems)))
    return pairs


def _gather_call(name, arrs, wait_sems, after, stage):
    nt = len(arrs)
    nc = 3 * nt
    n_wait = len(wait_sems)
    n_new = 2 * nc if stage < 2 else 0
    arrs = [pltpu.with_memory_space_constraint(a, pltpu.HBM) for a in arrs]

    def body(*refs):
        a_refs = refs[:nt]
        waits = refs[nt:nt + n_wait]
        news = refs[nt + n_wait + 1:nt + n_wait + 1 + n_new]
        token = refs[-1]
        if stage > 0:
            for start, arrival in _gather_copies(a_refs, waits[:nc], waits[nc:], stage - 1):
                start.wait_send()
                arrival.wait_recv()
        if stage < 2:
            for start, _ in _gather_copies(a_refs, news[:nc], news[nc:], stage):
                start.start()
        token[...] = jnp.zeros_like(token)

    outs = pl.pallas_call(
        body, name=name,
        out_shape=(*[pltpu.SemaphoreType.DMA(())] * n_new, *[pltpu.HBM(a.shape, a.dtype) for a in arrs],
                   jax.ShapeDtypeStruct((8, LANES), F32)),
        in_specs=[HBM] * nt + [SEM] * n_wait + [ANY],
        out_specs=(*[SEM] * n_new, *[HBM] * nt, pl.BlockSpec(memory_space=pltpu.VMEM)),
        input_output_aliases={t: n_new + t for t in range(nt)},
        compiler_params=pltpu.CompilerParams(has_side_effects=pltpu.SideEffectType.DATAFLOW_SIDE_EFFECTING),
    )(*arrs, *wait_sems, after)
    return list(outs[:n_new]), list(outs[n_new:n_new + nt]), outs[-1]


def _add_my_halves(gs, others, c_idx):
    nt = len(gs)

    def body(c_ref, *refs):
        for g_ref, o_ref, out_ref in zip(refs[:nt], refs[nt:2 * nt], refs[2 * nt:]):
            out_ref[...] = (g_ref[...].astype(F32) + o_ref[...].astype(F32)).astype(out_ref.dtype)

    def half(g):
        return pl.BlockSpec((None, g.shape[1] // 2, g.shape[2]), lambda j, c_ref: (j, 0, 0))

    return pl.pallas_call(
        body, name="add_my_halves",
        grid_spec=pltpu.PrefetchScalarGridSpec(
            num_scalar_prefetch=1, grid=(4,),
            in_specs=[pl.BlockSpec((None, g.shape[1] // 2, g.shape[2]), lambda j, c_ref: (j, c_ref[0], 0)) for g in gs]
                     + [half(g) for g in gs],
            out_specs=[half(g) for g in gs]),
        out_shape=[jax.ShapeDtypeStruct((4, g.shape[1] // 2, g.shape[2]), WIRE_DTYPE) for g in gs],
        compiler_params=_params("arbitrary"))(c_idx, *gs, *others)


def _split_plan(kind, s_refs, l_refs):
    x, y, c = _coords()
    if kind == "devices":
        peers = [(x ^ ((k >> 2) & 1), y ^ ((k >> 1) & 1), c ^ (k & 1)) for k in range(1, 8)]
        return [(s, l.at[4 * x + 2 * y + c], peer) for peer in peers for s, l in zip(s_refs, l_refs)]
    if kind == "swap":
        return [(s.at[:, pl.ds((1 - c) * (s.shape[1] // 2), s.shape[1] // 2)], l, (x, y, 1 - c)) for s, l in zip(s_refs, l_refs)]
    return [(s.at[2 * cx + cy], l.at[j], (cx, cy, c)) for j, (cx, cy) in enumerate(_other_chips(x, y))
            for s, l in zip(s_refs, l_refs)]


def _split_landing(kind, a):
    if kind == "devices":
        return (8,) + a.shape
    return (a.shape[0], a.shape[1] // 2, a.shape[2]) if kind == "swap" else (3,) + a.shape[1:]


def _copies_start(name, kind, srcs, after=None):
    ns = len(srcs)
    n = {"swap": 1, "exchange": 3, "devices": 7}[kind] * ns
    srcs = [pltpu.with_memory_space_constraint(a, pltpu.HBM) for a in srcs]
    fresh = jnp.zeros if kind == "devices" else lax.empty
    lands = [pltpu.with_memory_space_constraint(fresh(_split_landing(kind, a), a.dtype), pltpu.HBM) for a in srcs]
    extra = [] if after is None else [after]

    def body(*refs):
        first_sem = 2 * ns + len(extra)
        sems, token = refs[first_sem:first_sem + 2 * n], refs[-1]
        for k, (src, dst, dev) in enumerate(_split_plan(kind, refs[:ns], refs[ns:2 * ns])):
            pltpu.make_async_remote_copy(src_ref=src, dst_ref=dst, send_sem=sems[k], recv_sem=sems[n + k], device_id=dev,
                                         device_id_type=MESH).start()
        token[...] = jnp.zeros_like(token)

    outs = pl.pallas_call(
        body, name=name,
        out_shape=(*[pltpu.SemaphoreType.DMA(())] * (2 * n), *[pltpu.HBM(a.shape, a.dtype) for a in srcs + lands],
                   jax.ShapeDtypeStruct((8, LANES), F32)),
        in_specs=[HBM] * (2 * ns) + [ANY] * len(extra),
        out_specs=(*[SEM] * (2 * n), *[HBM] * (2 * ns), pl.BlockSpec(memory_space=pltpu.VMEM)),
        input_output_aliases={t: 2 * n + t for t in range(2 * ns)},
        compiler_params=pltpu.CompilerParams(has_side_effects=pltpu.SideEffectType.DATAFLOW_SIDE_EFFECTING),
    )(*srcs, *lands, *extra)
    return list(outs[:2 * n]), list(outs[2 * n:2 * n + ns]), list(outs[2 * n + ns:2 * n + 2 * ns]), outs[-1]


def _copies_wait(name, kind, sems, srcs, lands, after):
    ns = len(srcs)
    n = len(sems) // 2

    def body(*refs):
        sem_refs = refs[2 * ns:2 * ns + 2 * n]
        for k, (src, dst, dev) in enumerate(_split_plan(kind, refs[:ns], refs[ns:2 * ns])):
            cp = pltpu.make_async_remote_copy(src_ref=src, dst_ref=dst, send_sem=sem_refs[k], recv_sem=sem_refs[n + k],
                                              device_id=dev, device_id_type=MESH)
            cp.wait_send()
            cp.wait_recv()

    outs = pl.pallas_call(
        body, name=name, out_shape=tuple(pltpu.HBM(a.shape, a.dtype) for a in srcs + lands),
        in_specs=[HBM] * (2 * ns) + [SEM] * (2 * n) + [ANY], out_specs=tuple([HBM] * (2 * ns)),
        input_output_aliases={t: t for t in range(2 * ns)},
        compiler_params=pltpu.CompilerParams(has_side_effects=pltpu.SideEffectType.DATAFLOW_SIDE_EFFECTING),
    )(*srcs, *lands, *sems, after)
    return list(outs[:ns]), list(outs[ns:])


def _sum_into(pairs, recvs, idx, li, depth, accs):
    nt = len(pairs)

    def body(idx_ref, *refs):
        for p_ref, r_ref, out_ref in zip(refs[:nt], refs[nt:2 * nt], refs[-nt:]):
            out_ref[...] = p_ref[...].astype(F32) + r_ref[0].astype(F32) + r_ref[1].astype(F32) + r_ref[2].astype(F32)

    in_specs = ([pl.BlockSpec((None, p.shape[1] // 2, p.shape[2]), lambda i, idx_ref: (idx_ref[0], i, 0)) for p in pairs]
                + [pl.BlockSpec((3, p.shape[1] // 2, p.shape[2]), lambda i, idx_ref: (0, i, 0)) for p in pairs])
    args = [idx, *pairs, *recvs]
    aliases = {}
    if accs[0] is not None:
        in_specs += [ANY] * nt
        args += list(accs)
        aliases = {1 + 2 * nt + t: t for t in range(nt)}
    return pl.pallas_call(
        body, name="sum_into",
        grid_spec=pltpu.PrefetchScalarGridSpec(
            num_scalar_prefetch=1, grid=(2,), in_specs=in_specs,
            out_specs=[pl.BlockSpec((None, p.shape[1] // 2, p.shape[2]), lambda i, idx_ref: (li, 2 * idx_ref[1] + i, 0))
                       for p in pairs]),
        out_shape=[jax.ShapeDtypeStruct((depth, 2 * p.shape[1], p.shape[2]), F32) for p in pairs],
        input_output_aliases=aliases, compiler_params=_params("arbitrary"))(*args)


def _sum_devices(own, land, me_dev):
    rows = own.shape[0]
    tr = _tile_rows(rows, 512)

    def body(me_ref, o_ref, l_ref, out_ref):
        acc = jnp.where(me_ref[0] == 0, o_ref[...], l_ref[0])
        for s in range(1, 8):
            acc = acc + jnp.where(me_ref[0] == s, o_ref[...], l_ref[s])
        out_ref[...] = acc

    return pl.pallas_call(
        body, name="sum_devices",
        grid_spec=pltpu.PrefetchScalarGridSpec(
            num_scalar_prefetch=1, grid=(rows // tr,),
            in_specs=[pl.BlockSpec((tr, LANES), lambda i, me_ref: (i, 0)), pl.BlockSpec((8, tr, LANES), lambda i, me_ref: (0, i, 0))],
            out_specs=pl.BlockSpec((tr, LANES), lambda i, me_ref: (i, 0))),
        out_shape=jax.ShapeDtypeStruct((rows, LANES), F32), compiler_params=_params("arbitrary"))(me_dev, own, land)


def _sibling_share(gs, li):
    nt = len(gs)

    def body(*refs):
        out_refs = refs[nt:2 * nt]
        send_sems, recv_sems = refs[2 * nt:]
        x, y, c = _coords()
        sends, recvs = [], []
        for t in range(nt):
            rh = out_refs[t].shape[1] // 2
            mine, theirs = out_refs[t].at[li, pl.ds(c * rh, rh)], out_refs[t].at[li, pl.ds((1 - c) * rh, rh)]
            sems = dict(send_sem=send_sems.at[t], recv_sem=recv_sems.at[t], device_id=(x, y, 1 - c), device_id_type=MESH)
            sends.append(pltpu.make_async_remote_copy(src_ref=mine, dst_ref=mine, **sems))
            recvs.append(pltpu.make_async_remote_copy(src_ref=theirs, dst_ref=theirs, **sems))
        for cp in sends:
            cp.start()
        for cp in recvs:
            cp.wait_recv()
        for cp in sends:
            cp.wait_send()

    return pl.pallas_call(
        body, name="sibling_share", out_shape=[jax.ShapeDtypeStruct(g.shape, g.dtype) for g in gs],
        in_specs=[ANY] * nt, out_specs=[ANY] * nt, input_output_aliases={t: t for t in range(nt)},
        scratch_shapes=[pltpu.SemaphoreType.DMA((nt,)), pltpu.SemaphoreType.DMA((nt,))])(*gs)


def _all_gather_devices(buf, after=None):
    extra = [] if after is None else [after]

    def body(b_ref, *rest):
        out_ref, send_sems, recv_sems, local_sem = rest[len(extra):]
        x, y, c = _coords()
        me = 4 * x + 2 * y + c
        mine = pltpu.make_async_copy(b_ref, out_ref.at[me], local_sem)
        mine.start()
        peers = []
        for k in range(1, 8):
            fx, fy, fc = (k >> 2) & 1, (k >> 1) & 1, k & 1
            peers.append((x ^ fx, y ^ fy, c ^ fc))
        sends = [pltpu.make_async_remote_copy(src_ref=b_ref, dst_ref=out_ref.at[me], send_sem=send_sems.at[k],
                                              recv_sem=recv_sems.at[k], device_id=peer, device_id_type=MESH)
                 for k, peer in enumerate(peers)]
        for cp in sends:
            cp.start()
        for k, (px, py, pc) in enumerate(peers):
            pltpu.make_async_remote_copy(src_ref=b_ref, dst_ref=out_ref.at[4 * px + 2 * py + pc], send_sem=send_sems.at[k],
                                         recv_sem=recv_sems.at[k], device_id=(px, py, pc), device_id_type=MESH).wait_recv()
        for cp in sends:
            cp.wait_send()
        mine.wait()

    return pl.pallas_call(
        body, name="all_gather_devices", out_shape=jax.ShapeDtypeStruct((8,) + buf.shape, buf.dtype),
        in_specs=[ANY] * (1 + len(extra)), out_specs=ANY,
        scratch_shapes=[pltpu.SemaphoreType.DMA((7,)), pltpu.SemaphoreType.DMA((7,)), pltpu.SemaphoreType.DMA(())])(buf, *extra)


SMALL_SHARDED = ("conv_qkv", "sconv_w")
REPLICATED = ("norm1_g", "a_log", "dt_bias", "onorm_g", "pool_w", "pool_scale", "norm2_g", "final_g")
ALL_WEIGHTS = ("norm1_g", "w_in", "conv_qkv", "a_log", "dt_bias", "onorm_g", "pool_w", "pool_scale", "sconv_w", "w_out",
               "norm2_g", "w_gate", "w_up", "w_down", "ple_proj", "ple_gate", "final_g")


def _pad_rows(flat, row_multiple):
    m = flat.shape[0]
    r = -(-m // (LANES * row_multiple)) * row_multiple
    return jnp.pad(flat, (0, r * LANES - m)).reshape(r, LANES)


def _adamw_math(w, g, m, v):
    c1 = 1.0 / (1.0 - ADAM_B1 ** ADAM_STEP)
    c2 = 1.0 / (1.0 - ADAM_B2 ** ADAM_STEP)
    nm = ADAM_B1 * m + (1.0 - ADAM_B1) * g
    nv = ADAM_B2 * v + (1.0 - ADAM_B2) * (g * g)
    return -ADAM_LR * ((nm * c1) / (jnp.sqrt(nv * c2) + ADAM_EPS) + ADAM_WD * w), nm, nv


def _adamw(w, g, m, v):
    shape = w.shape
    cols = shape[-1]
    rows = w.size // cols
    tr = _tile_rows(rows, 512)

    def body(w_ref, g_ref, m_ref, v_ref, d_ref, nm_ref, nv_ref, go_ref):
        gv = g_ref[...]
        d_ref[...], nm_ref[...], nv_ref[...] = _adamw_math(w_ref[...], gv, m_ref[...], v_ref[...])
        go_ref[...] = gv

    spec = pl.BlockSpec((tr, cols), lambda i: (i, 0))
    outs = pl.pallas_call(
        body, name="adamw", grid=(rows // tr,), in_specs=[spec] * 4, out_specs=[spec] * 4,
        out_shape=[jax.ShapeDtypeStruct((rows, cols), F32)] * 4,
        compiler_params=_params("arbitrary"))(*[a.reshape(rows, cols) for a in (w, g, m, v)])
    return tuple(o.reshape(shape) for o in outs)


def kernel(x, p, norm1_g, w_in, conv_qkv, a_log, dt_bias, onorm_g, pool_w, pool_scale, sconv_w, w_out, norm2_g, w_gate, w_up, w_down, ple_proj, ple_gate, final_g, loss_target, m_norm1_g, m_w_in, m_conv_qkv, m_a_log, m_dt_bias, m_onorm_g, m_pool_w, m_pool_scale, m_sconv_w, m_w_out, m_norm2_g, m_w_gate, m_w_up, m_w_down, m_ple_proj, m_ple_gate, m_final_g, v_norm1_g, v_w_in, v_conv_qkv, v_a_log, v_dt_bias, v_onorm_g, v_pool_w, v_pool_scale, v_sconv_w, v_w_out, v_norm2_g, v_w_gate, v_w_up, v_w_down, v_ple_proj, v_ple_gate, v_final_g):
    weights = dict(zip(ALL_WEIGHTS, (norm1_g, w_in, conv_qkv, a_log, dt_bias, onorm_g, pool_w, pool_scale, sconv_w, w_out,
                                     norm2_g, w_gate, w_up, w_down, ple_proj, ple_gate, final_g)))
    mom_m = dict(zip(ALL_WEIGHTS, (m_norm1_g, m_w_in, m_conv_qkv, m_a_log, m_dt_bias, m_onorm_g, m_pool_w, m_pool_scale,
                                   m_sconv_w, m_w_out, m_norm2_g, m_w_gate, m_w_up, m_w_down, m_ple_proj, m_ple_gate, m_final_g)))
    mom_v = dict(zip(ALL_WEIGHTS, (v_norm1_g, v_w_in, v_conv_qkv, v_a_log, v_dt_bias, v_onorm_g, v_pool_w, v_pool_scale,
                                   v_sconv_w, v_w_out, v_norm2_g, v_w_gate, v_w_up, v_w_down, v_ple_proj, v_ple_gate, v_final_g)))
    for n in TRANSPOSED:
        weights[n], mom_m[n], mom_v[n] = (jnp.swapaxes(a[n], 1, 2) for a in (weights, mom_m, mom_v))
    c_idx = lax.axis_index("c").astype(jnp.int32).reshape(1)
    chip = (2 * lax.axis_index("x") + lax.axis_index("y")).astype(jnp.int32)
    me_idx = chip.reshape(1)
    idx = jnp.stack([chip, lax.axis_index("c").astype(jnp.int32)])
    depth = p.shape[0]

    small = {n: weights[n] for n in REPLICATED}
    sflat = _pad_rows(jnp.concatenate([weights[n].reshape(-1) for n in SMALL_SHARDED]), 8)
    sgath8 = _all_gather_devices(sflat)
    placed_in = _place_shards([weights["w_in"]], me_idx)
    sems, arrs, _ = _gather_call("gather_first_start", placed_in[0], [], sgath8, 0)
    placed_rest = _place_shards([weights[n] for n in BIG[1:]], me_idx)
    placed = [placed_in[li] + placed_rest[li] for li in range(depth)]
    sems, arrs, _ = _gather_call("gather_first_forward", arrs, sems, placed_rest[0][0], 1)
    _, arrs, token = _gather_call("gather_first_finish", arrs, sems, placed_rest[0][0], 2)
    gw = [dict() for _ in range(depth)]
    gw[0]["w_in"] = arrs[0]
    early = ("w_in", "w_out")
    late = tuple(n for n in BIG if n not in early)
    groups = [dict(li=0, names=BIG[1:], forward=(0, "mixed"), finish=(0, "mixed"))]
    for li in range(1, depth):
        groups.append(dict(li=li, names=early, forward=(li - 1, "ffn"), finish=(li - 1, "end")))
        groups.append(dict(li=li, names=late, forward=(li, "mixed"), finish=(li, "mixed")))
    def arrive(li, stage, after):
        for k, g in enumerate(groups):
            if g["forward"] == (li, stage):
                g["sems"], g["arrs"], _ = _gather_call("gather_forward_%d" % k, g["arrs"], g["sems"], after, 1)
            if g["finish"] == (li, stage):
                _, g["arrs"], _ = _gather_call("gather_finish_%d" % k, g["arrs"], g["sems"], after, 2)
                gw[g["li"]].update(zip(g["names"], g["arrs"]))

    sgath = sgath8[0::2].reshape(4, -1)
    off = 0
    for n in SMALL_SHARDED:
        shp = weights[n].shape
        part = sgath[:, off:off + weights[n].size].reshape((4,) + shp)
        small[n] = jnp.moveaxis(part, 0, -2).reshape(shp[:-1] + (4 * shp[-1],))
        off += weights[n].size
    for k, g in enumerate(groups):
        arrs = [placed[g["li"]][BIG.index(n)] for n in g["names"]]
        g["sems"], g["arrs"], token = _gather_call("gather_start_%d" % k, arrs, [], token, 0)

    small["norm1_g"] = small["norm1_g"] + token[0, 0]

    pending = []
    last_token = [None]

    def advance(g, after):
        if g["stage"] == 0:
            gs, others = _copies_wait("swap_wait_" + g["tag"], "swap", *g["handle"], after)
            g["handle"] = _copies_start("exchange_start_" + g["tag"], "exchange", _add_my_halves(gs, others, c_idx))
            g["stage"] = 1
            return g["handle"][3]
        return None

    held = {}

    def produced(li, stage, grads, after):
        token = None
        for g in pending:
            token = advance(g, after) if g["stage"] == 0 else token
        if li > 0 and stage != "end":
            held.update(grads)
            grads = {}
        elif li > 0:
            grads = {**held, **grads}
            held.clear()
        if grads:
            names = [n for n in BIG if n in grads]
            handle = _copies_start("swap_start_%d%s" % (li, stage), "swap", [grads[n] for n in names], token)
            pending.append(dict(li=li, names=names, tag="%d%s" % (li, stage), stage=0, handle=handle[:3]))
            token = handle[3]
        last_token[0] = last_token[0] if token is None else token
        return token

    loss_local, dx, _, small_grads = _local_step(x[0], p[:, 0], loss_target[0], gw, small, produced, arrive)
    rnames = REPLICATED + SMALL_SHARDED
    rflat = _pad_rows(jnp.concatenate([small_grads[n].reshape(-1) for n in rnames] + [loss_local.reshape(1)]), 8)
    small_handle = _copies_start("small_start", "devices", [rflat], last_token[0])
    accs, big_outs = {}, {}

    def finish(g, after):
        pairs, recvs = _copies_wait("exchange_wait_" + g["tag"], "exchange", *g["handle"][:3], after)
        summed = _sum_into(pairs, recvs, idx, g["li"], depth, [accs.get(n) for n in g["names"]])
        accs.update(zip(g["names"], _sibling_share(summed, g["li"])))
        return accs[g["names"][-1]]

    def update(names):
        for n in names:
            big_outs[n] = _adamw(weights[n], accs[n], mom_m[n], mom_v[n])
        return jnp.stack([big_outs[n][0].reshape(-1)[0] for n in names])

    done = finish(pending[0], small_handle[3])
    done = advance(pending[-1], done)
    for g in pending[1:-1]:
        done = finish(g, done)
    last = pending[-1]["names"]
    done = update([n for n in BIG if n not in last])
    finish(pending[-1], done)
    done = update(last)


    gshard = {}
    (own,), (land,) = _copies_wait("small_wait", "devices", *small_handle[:3], done)
    me_dev = (2 * chip + lax.axis_index("c").astype(jnp.int32)).reshape(1)
    rsum = _sum_devices(own, land, me_dev).reshape(-1)
    off = 0
    for n in rnames:
        whole = rsum[off:off + small_grads[n].size].reshape(small_grads[n].shape)
        off += small_grads[n].size
        if n in SMALL_SHARDED:
            cols = weights[n].shape[-1]
            whole = lax.dynamic_slice_in_dim(whole, chip * cols, cols, axis=whole.ndim - 1)
        gshard[n] = whole

    loss = rsum[off]

    deltas, new_m, new_v, grad_out = {}, {}, {}, {}
    for n in ALL_WEIGHTS:
        if n in BIG:
            deltas[n], new_m[n], new_v[n], grad_out[n] = big_outs[n]
        else:
            deltas[n], new_m[n], new_v[n], grad_out[n] = _adamw(weights[n], gshard[n], mom_m[n], mom_v[n])
    for n in TRANSPOSED:
        deltas[n], new_m[n], new_v[n], grad_out[n] = (jnp.swapaxes(a[n], 1, 2) for a in (deltas, new_m, new_v, grad_out))
    return (loss, dx[None], *[grad_out[n] for n in ALL_WEIGHTS], *[deltas[n] for n in ALL_WEIGHTS],
            *[new_m[n] for n in ALL_WEIGHTS], *[new_v[n] for n in ALL_WEIGHTS])
```

```python
import jax
import jax.numpy as jnp
from jax import lax
from jax.experimental import pallas as pl
from jax.experimental.pallas import tpu as pltpu

F32 = jnp.float32
MM_DTYPE = jnp.bfloat16
WIRE_DTYPE = jnp.bfloat16
EPS = 1e-6
HEAD_DIM = 128
CHUNK = 64
QKV_CONV_WIDTH = 4
SCONV_WIDTH = 3
POOL_GROUPS = 4
LANES = 128
SUBLANES_WIRE = 16
VMEM_LIMIT_BYTES = 56 * 1024 * 1024
ADAM_LR, ADAM_B1, ADAM_B2, ADAM_EPS, ADAM_WD, ADAM_STEP = 0.001, 0.9, 0.999, 1e-08, 0.01, 10
MESH = pl.DeviceIdType.MESH
ANY = pl.BlockSpec(memory_space=pl.ANY)
HBM = pl.BlockSpec(memory_space=pltpu.HBM)
SEM = pl.BlockSpec(memory_space=pltpu.SEMAPHORE)


def _params(*sem):
    return pltpu.CompilerParams(vmem_limit_bytes=VMEM_LIMIT_BYTES, dimension_semantics=sem if sem else None)


def _mm(a, b):
    return jnp.dot(a.astype(MM_DTYPE), b.astype(MM_DTYPE), preferred_element_type=F32)


def _mm_nt(a, b):
    return lax.dot_general(a.astype(MM_DTYPE), b.astype(MM_DTYPE), (((1,), (1,)), ((), ())), preferred_element_type=F32)


def _mm_tn(a, b):
    return lax.dot_general(a.astype(MM_DTYPE), b.astype(MM_DTYPE), (((0,), (0,)), ((), ())), preferred_element_type=F32)


def _sigmoid(x):
    return 1.0 / (1.0 + jnp.exp(-x))


def _dsilu(x, s):
    return s * (1.0 + x * (1.0 - s))


def _rows(shape):
    return lax.broadcasted_iota(jnp.int32, shape, 0)


def _shift_down(x, s):
    if s == 0:
        return x
    return jnp.where(_rows(x.shape) >= s, pltpu.roll(x, s, 0), 0.0)


def _shift_up(x, s):
    if s == 0:
        return x
    t = x.shape[0]
    return jnp.where(_rows(x.shape) < t - s, pltpu.roll(x, t - s, 0), 0.0)


def _rms_fwd(x):
    r = lax.rsqrt(jnp.mean(x * x, axis=-1, keepdims=True) + EPS)
    return x * r, r


def _rms_bwd(dxn, xn, r):
    return r * (dxn - xn * jnp.mean(dxn * xn, axis=-1, keepdims=True))


def _tile_rows(n, cap, mult=8):
    best = None
    for d in range(mult, min(n, cap) + 1, mult):
        if n % d == 0:
            best = d
    return best if best is not None else n


def _in_proj_fwd(x, g1, wp, segs, tm):
    t, d = x.shape
    npk = wp.shape[1]

    def body(x_ref, g_ref, w_ref, *o_refs):
        xn, _ = _rms_fwd(x_ref[...])
        h = (xn * g_ref[...]).astype(w_ref.dtype)
        off = 0
        for o_ref, wd in zip(o_refs, segs):
            o_ref[...] = jnp.dot(h, w_ref[:, off:off + wd], preferred_element_type=F32)
            off += wd

    return pl.pallas_call(
        body, name="in_proj_fwd", grid=(t // tm,),
        in_specs=[pl.BlockSpec((tm, d), lambda i: (i, 0)), pl.BlockSpec((1, d), lambda i: (0, 0)),
                  pl.BlockSpec((d, npk), lambda i: (0, 0))],
        out_specs=[pl.BlockSpec((tm, wd), lambda i: (i, 0)) for wd in segs],
        out_shape=[jax.ShapeDtypeStruct((t, wd), F32) for wd in segs],
        compiler_params=_params("arbitrary"))(x, g1, wp)


def _in_proj_bwd(x, g1, wp, dsegs, dx_res, segs, tm):
    t, d = x.shape
    npk = wp.shape[1]
    nseg = len(segs)

    def body(x_ref, g_ref, w_ref, *rest):
        ds_refs = rest[:nseg]
        dxr_ref, dx_ref, dw_ref, dg_ref = rest[nseg:]
        i = pl.program_id(0)

        @pl.when(i == 0)
        def _():
            dw_ref[...] = jnp.zeros_like(dw_ref)
            dg_ref[...] = jnp.zeros_like(dg_ref)

        xn, r = _rms_fwd(x_ref[...])
        g = g_ref[...]
        h = (xn * g).astype(w_ref.dtype)
        dcat = jnp.concatenate([ds_ref[...].astype(w_ref.dtype) for ds_ref in ds_refs], axis=1)
        dh = lax.dot_general(dcat, w_ref[...], (((1,), (1,)), ((), ())), preferred_element_type=F32)
        dw_ref[...] += lax.dot_general(h, dcat, (((0,), (0,)), ((), ())), preferred_element_type=F32)
        dg_ref[...] += jnp.sum(dh * xn, axis=0, keepdims=True)
        dx_ref[...] = dxr_ref[...] + _rms_bwd(dh * g, xn, r)

    return pl.pallas_call(
        body, name="in_proj_bwd", grid=(t // tm,),
        in_specs=[pl.BlockSpec((tm, d), lambda i: (i, 0)), pl.BlockSpec((1, d), lambda i: (0, 0)),
                  pl.BlockSpec((d, npk), lambda i: (0, 0))]
                 + [pl.BlockSpec((tm, wd), lambda i: (i, 0)) for wd in segs]
                 + [pl.BlockSpec((tm, d), lambda i: (i, 0))],
        out_specs=[pl.BlockSpec((tm, d), lambda i: (i, 0)), pl.BlockSpec((d, npk), lambda i: (0, 0)),
                   pl.BlockSpec((1, d), lambda i: (0, 0))],
        out_shape=[jax.ShapeDtypeStruct((t, d), F32), jax.ShapeDtypeStruct((d, npk), F32),
                   jax.ShapeDtypeStruct((1, d), F32)],
        compiler_params=_params("arbitrary"))(x, g1, wp, *dsegs, dx_res)


def _out_proj_fwd(x0, mix, wo, g2, tm):
    t, d = x0.shape
    dq = wo.shape[1]
    widths = [m.shape[1] for m in mix]

    def body(x_ref, *rest):
        m_refs = rest[:len(mix)]
        w_ref, g_ref, x1_ref, h2_ref = rest[len(mix):]
        mixed = jnp.concatenate([m_ref[...].astype(w_ref.dtype) for m_ref in m_refs], axis=1)
        acc = x_ref[...] + jnp.dot(mixed, w_ref[...].reshape(4 * dq, d), preferred_element_type=F32)
        x1_ref[...] = acc
        xn, _ = _rms_fwd(acc)
        h2_ref[...] = (xn * g_ref[...]).astype(h2_ref.dtype)

    return pl.pallas_call(
        body, name="out_proj_fwd", grid=(t // tm,),
        in_specs=[pl.BlockSpec((tm, d), lambda i: (i, 0))]
                 + [pl.BlockSpec((tm, wd), lambda i: (i, 0)) for wd in widths]
                 + [pl.BlockSpec((4, dq, d), lambda i: (0, 0, 0)), pl.BlockSpec((1, d), lambda i: (0, 0))],
        out_specs=[pl.BlockSpec((tm, d), lambda i: (i, 0)), pl.BlockSpec((tm, d), lambda i: (i, 0))],
        out_shape=[jax.ShapeDtypeStruct((t, d), F32), jax.ShapeDtypeStruct((t, d), MM_DTYPE)],
        compiler_params=_params("arbitrary"))(x0, *mix, wo, g2)


def _out_proj_bwd(dx2, dh2, x1, g2, mix, wo, tm):
    t, d = x1.shape
    dq = wo.shape[1]
    widths = [m.shape[1] for m in mix]
    nm = len(mix)

    def body(dx2_ref, dh2_ref, x1_ref, g_ref, *rest):
        m_refs = rest[:nm]
        w_ref = rest[nm]
        dx1_ref = rest[nm + 1]
        dm_refs = rest[nm + 2:nm + 2 + nm]
        dw_ref, dg_ref = rest[nm + 2 + nm:]
        i = pl.program_id(0)

        @pl.when(i == 0)
        def _():
            dw_ref[...] = jnp.zeros_like(dw_ref)
            dg_ref[...] = jnp.zeros_like(dg_ref)

        xn, r = _rms_fwd(x1_ref[...])
        dh2v = dh2_ref[...]
        dg_ref[...] += jnp.sum(dh2v * xn, axis=0, keepdims=True)
        dx1 = dx2_ref[...] + _rms_bwd(dh2v * g_ref[...], xn, r)
        dx1_ref[...] = dx1
        dx1c = dx1.astype(w_ref.dtype)
        mixed = jnp.concatenate([m_ref[...].astype(w_ref.dtype) for m_ref in m_refs], axis=1)
        dmixed = lax.dot_general(dx1c, w_ref[...].reshape(4 * dq, d), (((1,), (1,)), ((), ())), preferred_element_type=F32)
        dw_ref[...] += lax.dot_general(mixed, dx1c, (((0,), (0,)), ((), ())), preferred_element_type=F32).reshape(4, dq, d)
        off = 0
        for dm_ref, wd in zip(dm_refs, widths):
            dm_ref[...] = dmixed[:, off:off + wd]
            off += wd

    tile = lambda wd: pl.BlockSpec((tm, wd), lambda i: (i, 0))
    return pl.pallas_call(
        body, name="out_proj_bwd", grid=(t // tm,),
        in_specs=[tile(d), tile(d), tile(d), pl.BlockSpec((1, d), lambda i: (0, 0))]
                 + [tile(wd) for wd in widths] + [pl.BlockSpec((4, dq, d), lambda i: (0, 0, 0))],
        out_specs=[tile(d)] + [tile(wd) for wd in widths]
                  + [pl.BlockSpec((4, dq, d), lambda i: (0, 0, 0)), pl.BlockSpec((1, d), lambda i: (0, 0))],
        out_shape=[jax.ShapeDtypeStruct((t, d), F32)] + [jax.ShapeDtypeStruct((t, wd), F32) for wd in widths]
                  + [jax.ShapeDtypeStruct((4, dq, d), F32), jax.ShapeDtypeStruct((1, d), F32)],
        compiler_params=_params("arbitrary"))(dx2, dh2, x1, g2, *mix, wo)


def _ffn_fwd(x1, h2, wg, wu, wd, tm):
    t, d = x1.shape
    fs = wg.shape[1]

    def body(x1_ref, h2_ref, wg_ref, wu_ref, wd_ref, x2_ref, gp_ref, up_ref):
        @pl.when(pl.program_id(1) == 0)
        def _():
            x2_ref[...] = x1_ref[...]

        h = h2_ref[...]
        nt = (((1,), (1,)), ((), ()))
        gp = lax.dot_general(h, wg_ref[...], nt, preferred_element_type=F32)
        up = lax.dot_general(h, wu_ref[...], nt, preferred_element_type=F32)
        gp_ref[...] = gp
        up_ref[...] = up
        ff = gp * _sigmoid(gp) * up
        x2_ref[...] += jnp.dot(ff.astype(wd_ref.dtype), wd_ref[...], preferred_element_type=F32)

    return pl.pallas_call(
        body, name="ffn_fwd", grid=(t // tm, 4),
        in_specs=[pl.BlockSpec((tm, d), lambda i, j: (i, 0)), pl.BlockSpec((tm, d), lambda i, j: (i, 0)),
                  pl.BlockSpec((None, fs, d), lambda i, j: (j, 0, 0)),
                  pl.BlockSpec((None, fs, d), lambda i, j: (j, 0, 0)),
                  pl.BlockSpec((None, fs, d), lambda i, j: (j, 0, 0))],
        out_specs=[pl.BlockSpec((tm, d), lambda i, j: (i, 0)), pl.BlockSpec((None, tm, fs), lambda i, j: (j, i, 0)),
                   pl.BlockSpec((None, tm, fs), lambda i, j: (j, i, 0))],
        out_shape=[jax.ShapeDtypeStruct((t, d), F32), jax.ShapeDtypeStruct((4, t, fs), F32),
                   jax.ShapeDtypeStruct((4, t, fs), F32)],
        compiler_params=_params("arbitrary", "arbitrary"))(x1, h2, wg, wu, wd)


def _ffn_bwd(dx2, h2, gp, up, wg, wu, wd, tm):
    t, d = dx2.shape
    fs = wg.shape[1]

    def body(dx2_ref, h2_ref, gp_ref, up_ref, wg_ref, wu_ref, wd_ref, dh2_ref, dwg_ref, dwu_ref, dwd_ref):
        j, i = pl.program_id(0), pl.program_id(1)

        @pl.when(i == 0)
        def _():
            dwg_ref[...] = jnp.zeros_like(dwg_ref)
            dwu_ref[...] = jnp.zeros_like(dwu_ref)
            dwd_ref[...] = jnp.zeros_like(dwd_ref)

        cdt = wg_ref.dtype
        h = h2_ref[...]
        gpv, upv = gp_ref[...], up_ref[...]
        s = _sigmoid(gpv)
        silu = gpv * s
        dx2c = dx2_ref[...].astype(cdt)
        dff = lax.dot_general(dx2c, wd_ref[...], (((1,), (1,)), ((), ())), preferred_element_type=F32)
        dwd_ref[...] += lax.dot_general((silu * upv).astype(cdt), dx2c, (((0,), (0,)), ((), ())), preferred_element_type=F32)
        dup = (dff * silu).astype(cdt)
        dgp = (dff * upv * _dsilu(gpv, s)).astype(cdt)
        dwg_ref[...] += lax.dot_general(dgp, h, (((0,), (0,)), ((), ())), preferred_element_type=F32)
        dwu_ref[...] += lax.dot_general(dup, h, (((0,), (0,)), ((), ())), preferred_element_type=F32)
        dh = (jnp.dot(dgp, wg_ref[...], preferred_element_type=F32) + jnp.dot(dup, wu_ref[...], preferred_element_type=F32))
        rows = pl.ds(pl.multiple_of(i * tm, tm), tm)

        @pl.when(j == 0)
        def _():
            dh2_ref[rows, :] = dh

        @pl.when(j != 0)
        def _():
            dh2_ref[rows, :] += dh

    return pl.pallas_call(
        body, name="ffn_bwd", grid=(4, t // tm),
        in_specs=[pl.BlockSpec((tm, d), lambda j, i: (i, 0)), pl.BlockSpec((tm, d), lambda j, i: (i, 0)),
                  pl.BlockSpec((None, tm, fs), lambda j, i: (j, i, 0)), pl.BlockSpec((None, tm, fs), lambda j, i: (j, i, 0)),
                  pl.BlockSpec((None, fs, d), lambda j, i: (j, 0, 0)),
                  pl.BlockSpec((None, fs, d), lambda j, i: (j, 0, 0)),
                  pl.BlockSpec((None, fs, d), lambda j, i: (j, 0, 0))],
        out_specs=[pl.BlockSpec((t, d), lambda j, i: (0, 0)), pl.BlockSpec((None, fs, d), lambda j, i: (j, 0, 0)),
                   pl.BlockSpec((None, fs, d), lambda j, i: (j, 0, 0)), pl.BlockSpec((None, fs, d), lambda j, i: (j, 0, 0))],
        out_shape=[jax.ShapeDtypeStruct((t, d), F32)] + [jax.ShapeDtypeStruct((4, fs, d), F32)] * 3,
        compiler_params=_params("arbitrary", "arbitrary"))(dx2, h2, gp, up, wg, wu, wd)


def _ple_fwd(x2, p, wpg, wpp, tm):
    t, d = x2.shape
    q = p.shape[1]
    dq = d // 4

    def body(x_ref, p_ref, wg_ref, wp_ref, o_ref):
        xv = x_ref[...]
        xc = xv.astype(wg_ref.dtype)
        pc = p_ref[...].astype(wp_ref.dtype)
        gate = _sigmoid(jnp.dot(xc, wg_ref[...].reshape(d, d), preferred_element_type=F32))
        for j in range(4):
            cols = slice(j * dq, (j + 1) * dq)
            o_ref[:, cols] = xv[:, cols] + gate[:, cols] * jnp.dot(pc, wp_ref[j], preferred_element_type=F32)

    return pl.pallas_call(
        body, name="ple_fwd", grid=(t // tm,),
        in_specs=[pl.BlockSpec((tm, d), lambda i: (i, 0)), pl.BlockSpec((tm, q), lambda i: (i, 0)),
                  pl.BlockSpec((4, dq, d), lambda i: (0, 0, 0)),
                  pl.BlockSpec((4, q, dq), lambda i: (0, 0, 0))],
        out_specs=pl.BlockSpec((tm, d), lambda i: (i, 0)),
        out_shape=jax.ShapeDtypeStruct((t, d), F32),
        compiler_params=_params("arbitrary"))(x2, p, wpg, wpp)


def _ple_bwd(dx3, x2, p, wpg, wpp, tm):
    t, d = x2.shape
    q = p.shape[1]
    dq = d // 4

    def body(dx3_ref, x_ref, p_ref, wg_ref, wp_ref, dx2_ref, dwg_ref, dwp_ref):
        @pl.when(pl.program_id(0) == 0)
        def _():
            dwg_ref[...] = jnp.zeros_like(dwg_ref)
            dwp_ref[...] = jnp.zeros_like(dwp_ref)

        cdt = wg_ref.dtype
        xc = x_ref[...].astype(cdt)
        pc = p_ref[...].astype(cdt)
        gate = _sigmoid(jnp.dot(xc, wg_ref[...].reshape(d, d), preferred_element_type=F32))
        dx3v = dx3_ref[...]
        dpp = (dx3v * gate).astype(cdt)
        dgate = dx3v * gate * (1.0 - gate)
        dpre_parts = []
        for j in range(4):
            cols = slice(j * dq, (j + 1) * dq)
            pp_j = jnp.dot(pc, wp_ref[j], preferred_element_type=F32)
            dpre_parts.append((dgate[:, cols] * pp_j).astype(cdt))
            dwp_ref[j] += lax.dot_general(pc, dpp[:, cols], (((0,), (0,)), ((), ())), preferred_element_type=F32)
        dpre = jnp.concatenate(dpre_parts, axis=1)
        dwg_ref[...] += lax.dot_general(xc, dpre, (((0,), (0,)), ((), ())), preferred_element_type=F32).reshape(4, dq, d)
        dx2_ref[...] = dx3v + lax.dot_general(dpre, wg_ref[...].reshape(d, d), (((1,), (1,)), ((), ())),
                                              preferred_element_type=F32)

    return pl.pallas_call(
        body, name="ple_bwd", grid=(t // tm,),
        in_specs=[pl.BlockSpec((tm, d), lambda i: (i, 0)), pl.BlockSpec((tm, d), lambda i: (i, 0)),
                  pl.BlockSpec((tm, q), lambda i: (i, 0)), pl.BlockSpec((4, dq, d), lambda i: (0, 0, 0)),
                  pl.BlockSpec((4, q, dq), lambda i: (0, 0, 0))],
        out_specs=[pl.BlockSpec((tm, d), lambda i: (i, 0)), pl.BlockSpec((4, dq, d), lambda i: (0, 0, 0)),
                   pl.BlockSpec((4, q, dq), lambda i: (0, 0, 0))],
        out_shape=[jax.ShapeDtypeStruct((t, d), F32), jax.ShapeDtypeStruct((4, dq, d), F32),
                   jax.ShapeDtypeStruct((4, q, dq), F32)],
        compiler_params=_params("arbitrary"))(dx3, x2, p, wpg, wpp)


def _loss_head(x, target, fg, tm):
    t, d = x.shape

    def body(x_ref, t_ref, g_ref, dx_ref, loss_ref, dg_ref):
        @pl.when(pl.program_id(0) == 0)
        def _():
            loss_ref[...] = jnp.zeros_like(loss_ref)
            dg_ref[...] = jnp.zeros_like(dg_ref)

        xn, r = _rms_fwd(x_ref[...])
        g = g_ref[...]
        err = xn * g - t_ref[...]
        loss_ref[...] += 0.5 * jnp.sum(jnp.sum(err * err, axis=-1, keepdims=True) / d, axis=0, keepdims=True)
        dy = err / d
        dg_ref[...] += jnp.sum(dy * xn, axis=0, keepdims=True)
        dx_ref[...] = _rms_bwd(dy * g, xn, r)

    return pl.pallas_call(
        body, name="loss_head", grid=(t // tm,),
        in_specs=[pl.BlockSpec((tm, d), lambda i: (i, 0)), pl.BlockSpec((tm, d), lambda i: (i, 0)),
                  pl.BlockSpec((1, d), lambda i: (0, 0))],
        out_specs=[pl.BlockSpec((tm, d), lambda i: (i, 0)), pl.BlockSpec((1, 1), lambda i: (0, 0)),
                   pl.BlockSpec((1, d), lambda i: (0, 0))],
        out_shape=[jax.ShapeDtypeStruct((t, d), F32), jax.ShapeDtypeStruct((1, 1), F32),
                   jax.ShapeDtypeStruct((1, d), F32)],
        compiler_params=_params("arbitrary"))(x, target, fg)


def _qkv_conv_act(xv, w, j, heads):
    k = QKV_CONV_WIDTH
    y = w[k - 1:k] * xv
    for s in range(1, k):
        y = y + w[k - 1 - s:k - s] * _shift_down(xv, s)
    sg = _sigmoid(y)
    s_act = y * sg
    nrm = lax.rsqrt(jnp.sum(s_act * s_act, axis=-1, keepdims=True) + EPS)
    scale = jnp.where(j < heads, HEAD_DIM ** -0.5, 1.0).astype(F32)
    return y, sg, s_act, nrm, scale


def _qkv_conv_fwd(qkv_pre, conv_w, heads):
    t = qkv_pre.shape[0]
    nblk = 3 * heads

    def body(x_ref, w_ref, o_ref):
        j = pl.program_id(0)
        _, _, s_act, nrm, scale = _qkv_conv_act(x_ref[...], w_ref[...], j, heads)
        o_ref[...] = jnp.where(j < 2 * heads, s_act * (nrm * scale), s_act)

    return pl.pallas_call(
        body, name="qkv_conv_fwd", grid=(nblk,),
        in_specs=[pl.BlockSpec((t, LANES), lambda j: (0, j)), pl.BlockSpec((QKV_CONV_WIDTH, LANES), lambda j: (0, j))],
        out_specs=pl.BlockSpec((t, LANES), lambda j: (0, j)),
        out_shape=jax.ShapeDtypeStruct(qkv_pre.shape, F32),
        compiler_params=_params("arbitrary"))(qkv_pre, conv_w)


def _qkv_conv_bwd(qkv_pre, conv_w, dqkv, heads):
    t = qkv_pre.shape[0]
    nblk = 3 * heads
    k = QKV_CONV_WIDTH

    def body(x_ref, w_ref, dn_ref, dx_ref, dw_ref):
        j = pl.program_id(0)
        xv, w = x_ref[...], w_ref[...]
        y, sg, s_act, nrm, scale = _qkv_conv_act(xv, w, j, heads)
        dn = dn_ref[...]
        dsn = dn * scale
        ds_qk = nrm * dsn - s_act * (nrm * nrm * nrm) * jnp.sum(dsn * s_act, axis=-1, keepdims=True)
        ds = jnp.where(j < 2 * heads, ds_qk, dn)
        dy = ds * _dsilu(y, sg)
        dx = w[k - 1:k] * dy
        dw_ref[k - 1:k, :] = jnp.sum(dy * xv, axis=0, keepdims=True)
        for s in range(1, k):
            dx = dx + w[k - 1 - s:k - s] * _shift_up(dy, s)
            dw_ref[k - 1 - s:k - s, :] = jnp.sum(dy * _shift_down(xv, s), axis=0, keepdims=True)
        dx_ref[...] = dx

    return pl.pallas_call(
        body, name="qkv_conv_bwd", grid=(nblk,),
        in_specs=[pl.BlockSpec((t, LANES), lambda j: (0, j)), pl.BlockSpec((k, LANES), lambda j: (0, j)),
                  pl.BlockSpec((t, LANES), lambda j: (0, j))],
        out_specs=[pl.BlockSpec((t, LANES), lambda j: (0, j)), pl.BlockSpec((k, LANES), lambda j: (0, j))],
        out_shape=[jax.ShapeDtypeStruct(qkv_pre.shape, F32), jax.ShapeDtypeStruct(conv_w.shape, F32)],
        compiler_params=_params("arbitrary"))(qkv_pre, conv_w, dqkv)


def _pool_windows(shape, j, group_dim):
    lane = lax.broadcasted_iota(jnp.int32, shape, 1) + j * LANES
    grp = lane // group_dim
    win = jnp.left_shift(2, grp).astype(F32)
    cnt = jnp.minimum((_rows(shape) + 1).astype(F32), win)
    return grp, cnt


def _pool_select(grp, levels):
    out = levels[0]
    for gi in range(1, POOL_GROUPS):
        out = jnp.where(grp == gi, levels[gi], out)
    return out


def _pool_mean(hv, grp, cnt):
    acc, levels, width = hv, [], 1
    for _ in range(POOL_GROUPS):
        acc = acc + _shift_down(acc, width)
        width *= 2
        levels.append(acc)
    return _pool_select(grp, levels) / cnt - hv


def _pool_fwd(hp, wbd, scale, group_dim):
    t, dp = hp.shape

    def body(h_ref, w_ref, s_ref, o_ref):
        hv = h_ref[...]
        grp, cnt = _pool_windows(hv.shape, pl.program_id(0), group_dim)
        pooled = _pool_mean(hv, grp, cnt)
        o_ref[...] = _mm(pooled, w_ref[...]) * s_ref[...]

    return pl.pallas_call(
        body, name="pool_fwd", grid=(dp // LANES,),
        in_specs=[pl.BlockSpec((t, LANES), lambda j: (0, j)), pl.BlockSpec((LANES, LANES), lambda j: (j, j)),
                  pl.BlockSpec((1, LANES), lambda j: (0, j))],
        out_specs=pl.BlockSpec((t, LANES), lambda j: (0, j)),
        out_shape=jax.ShapeDtypeStruct(hp.shape, F32),
        compiler_params=_params("arbitrary"))(hp, wbd, scale)


def _pool_bwd(hp, wbd, scale, dob, group_dim):
    t, dp = hp.shape

    def body(h_ref, w_ref, s_ref, do_ref, dh_ref, dw_ref, ds_ref):
        hv = h_ref[...]
        grp, cnt = _pool_windows(hv.shape, pl.program_id(0), group_dim)
        pooled = _pool_mean(hv, grp, cnt)
        wv = w_ref[...]
        dov = do_ref[...]
        ds_ref[...] = jnp.sum(dov * _mm(pooled, wv), axis=0, keepdims=True)
        dys = dov * s_ref[...]
        dw_ref[0] = _mm_tn(pooled, dys)
        dpooled = _mm_nt(dys, wv)
        acc, levels, width = dpooled / cnt, [], 1
        for _ in range(POOL_GROUPS):
            acc = acc + _shift_up(acc, width)
            width *= 2
            levels.append(acc)
        dh_ref[...] = _pool_select(grp, levels) - dpooled

    nb = dp // LANES
    return pl.pallas_call(
        body, name="pool_bwd", grid=(nb,),
        in_specs=[pl.BlockSpec((t, LANES), lambda j: (0, j)), pl.BlockSpec((LANES, LANES), lambda j: (j, j)),
                  pl.BlockSpec((1, LANES), lambda j: (0, j)), pl.BlockSpec((t, LANES), lambda j: (0, j))],
        out_specs=[pl.BlockSpec((t, LANES), lambda j: (0, j)), pl.BlockSpec((1, LANES, LANES), lambda j: (j, 0, 0)),
                   pl.BlockSpec((1, LANES), lambda j: (0, j))],
        out_shape=[jax.ShapeDtypeStruct(hp.shape, F32), jax.ShapeDtypeStruct((nb, LANES, LANES), F32),
                   jax.ShapeDtypeStruct((1, dp), F32)],
        compiler_params=_params("arbitrary"))(hp, wbd, scale, dob)


def _sconv_fwd(cbcch, w):
    t, dc3 = cbcch.shape
    nb = dc3 // 3 // LANES
    k = SCONV_WIDTH

    def body(b_ref, c_ref, h_ref, w_ref, o_ref):
        m = c_ref[...] * h_ref[...]
        wv = w_ref[...]
        y = wv[k - 1:k] * m
        for s in range(1, k):
            y = y + wv[k - 1 - s:k - s] * _shift_down(m, s)
        o_ref[...] = b_ref[...] * y

    return pl.pallas_call(
        body, name="sconv_fwd", grid=(nb,),
        in_specs=[pl.BlockSpec((t, LANES), lambda j: (0, j)), pl.BlockSpec((t, LANES), lambda j: (0, nb + j)),
                  pl.BlockSpec((t, LANES), lambda j: (0, 2 * nb + j)), pl.BlockSpec((k, LANES), lambda j: (0, j))],
        out_specs=pl.BlockSpec((t, LANES), lambda j: (0, j)),
        out_shape=jax.ShapeDtypeStruct((t, dc3 // 3), F32),
        compiler_params=_params("arbitrary"))(cbcch, cbcch, cbcch, w)


def _sconv_bwd(cbcch, w, doc):
    t, dc3 = cbcch.shape
    nb = dc3 // 3 // LANES
    k = SCONV_WIDTH

    def body(b_ref, c_ref, h_ref, w_ref, do_ref, db_ref, dc_ref, dh_ref, dw_ref):
        cv, hv = c_ref[...], h_ref[...]
        m = cv * hv
        wv = w_ref[...]
        dov = do_ref[...]
        dy = dov * b_ref[...]
        y = wv[k - 1:k] * m
        dm = wv[k - 1:k] * dy
        dw_ref[k - 1:k, :] = jnp.sum(dy * m, axis=0, keepdims=True)
        for s in range(1, k):
            ms = _shift_down(m, s)
            y = y + wv[k - 1 - s:k - s] * ms
            dm = dm + wv[k - 1 - s:k - s] * _shift_up(dy, s)
            dw_ref[k - 1 - s:k - s, :] = jnp.sum(dy * ms, axis=0, keepdims=True)
        db_ref[...] = dov * y
        dc_ref[...] = dm * hv
        dh_ref[...] = dm * cv

    col = lambda o: pl.BlockSpec((t, LANES), lambda j: (0, o * nb + j))
    return pl.pallas_call(
        body, name="sconv_bwd", grid=(nb,),
        in_specs=[col(0), col(1), col(2), pl.BlockSpec((k, LANES), lambda j: (0, j)), col(0)],
        out_specs=[col(0), col(0), col(0), pl.BlockSpec((k, LANES), lambda j: (0, j))],
        out_shape=[jax.ShapeDtypeStruct((t, dc3 // 3), F32)] * 3 + [jax.ShapeDtypeStruct(w.shape, F32)],
        compiler_params=_params("arbitrary"))(cbcch, cbcch, cbcch, w, doc)


class _Split:
    def __init__(self, a, exact=False):
        self.hi = a.astype(jnp.bfloat16)
        self.lo = None if exact else (a - self.hi.astype(F32)).astype(jnp.bfloat16)


def _per_head(dims, a, b):
    a = a if isinstance(a, _Split) else _Split(a)
    b = b if isinstance(b, _Split) else _Split(b)

    def dot(x, y):
        return lax.dot_general(x, y, (dims, ((), ())), preferred_element_type=F32)

    def head(h):
        out = dot(a.hi[h], b.hi[h])
        for x, y in ((a.hi, b.lo), (a.lo, b.hi)):
            out = out if x is None or y is None else out + dot(x[h], y[h])
        return out

    return jnp.stack([head(h) for h in range(a.hi.shape[0])])


def _bmm(a, b):
    return _per_head(((1,), (0,)), a, b)


def _bmm_nt(a, b):
    return _per_head(((1,), (1,)), a, b)


def _bmm_tn(a, b):
    return _per_head(((0,), (0,)), a, b)


def _inv_unit_lower(low):
    c = low.shape[-1]
    eye = (_rows((c, c)) == lax.broadcasted_iota(jnp.int32, (c, c), 1)).astype(F32)
    pw = -low
    inv = eye + pw
    span = 2
    while span < c:
        pws = _Split(pw)
        pw = _bmm(pws, pws)
        inv = inv + _bmm(inv, pw)
        span *= 2
    return inv


def _heads_of(ref, base, heads):
    return jnp.stack([ref[:, base + h * HEAD_DIM:base + (h + 1) * HEAD_DIM] for h in range(heads)])


def _chunk_common(q, k, v, a_col, b_col, alog, dtb, kept=None):
    hn, c, _ = q.shape
    beta = _sigmoid(b_col)
    xg = a_col + dtb
    softplus = jnp.maximum(xg, 0.0) + jnp.log(1.0 + jnp.exp(-jnp.abs(xg)))
    neg_ea = -jnp.exp(alog)
    g = neg_ea * softplus
    ri = _rows((c, c))
    ci = lax.broadcasted_iota(jnp.int32, (c, c), 1)
    incl, strict = ri >= ci, ri > ci
    inclf = _Split(jnp.broadcast_to(incl.astype(F32), (hn, c, c)), exact=True)
    gcb = _bmm(inclf, jnp.broadcast_to(g, (hn, c, HEAD_DIM)))
    gc_row = jnp.sum(jnp.where(ri <= ci, jnp.broadcast_to(g, (hn, c, c)), 0.0), axis=1, keepdims=True)
    dmat = jnp.where(incl, jnp.exp(jnp.where(incl, gcb[:, :, :1] - gc_row, 0.0)), 0.0)
    eg = jnp.exp(gcb)
    gl = gcb[:, c - 1:c, :]
    egl = jnp.exp(gl)
    edl = jnp.exp(gl - gcb)
    kb, vb = k * beta, v * beta
    kbe = kb * eg
    if kept is None:
        ks = _Split(k)
        a0 = _bmm_nt(kb, ks)
        tm = _inv_unit_lower(jnp.where(strict, a0 * dmat, 0.0))
        p0 = _bmm_nt(q, ks)
        tms = _Split(tm)
        u, w = _bmm(tms, vb), _bmm(tms, kbe)
    else:
        (a0, tm, p0, w), u = kept, None
    return dict(beta=beta, xg=xg, neg_ea=neg_ea, g=g, incl=incl, strict=strict, inclf=inclf, dmat=dmat, eg=eg,
                egl=egl, edl=edl, kb=kb, vb=vb, a0=a0, tm=tm, kbe=kbe, u=u, w=w, p0=p0,
                attn=p0 * dmat, qe=q * eg, kd=k * edl)


def _chunk_step(cm, state):
    ss = _Split(state)
    vn = cm["u"] - _bmm(cm["w"], ss)
    vns = _Split(vn)
    o = _bmm(cm["qe"], ss) + _bmm(cm["attn"], vns)
    new_state = state * cm["egl"][:, :, :1] + _bmm_tn(cm["kd"], vns)
    return vn, o, new_state


def _gated_norm(o, zv, og):
    xo, ro = _rms_fwd(o)
    sgz = _sigmoid(zv)
    return xo, ro, sgz, xo * og * (zv * sgz)


def _gate_columns(abv, gpv, heads):
    a_col = jnp.stack([abv[:, h:h + 1] for h in range(heads)])
    b_col = jnp.stack([abv[:, heads + h:heads + h + 1] for h in range(heads)])
    alog = jnp.stack([gpv[0:1, h:h + 1] for h in range(heads)])
    dtb = jnp.stack([gpv[1:2, h:h + 1] for h in range(heads)])
    return a_col, b_col, alog, dtb


def _delta_fwd(qkv, z, ab, gpar, heads):
    t = qkv.shape[0]
    da = heads * HEAD_DIM
    n = t // CHUNK

    def body(qkv_ref, z_ref, ab_ref, gp_ref, oa_ref, st_ref, kc_ref, kw_ref, s_ref):
        @pl.when(pl.program_id(0) == 0)
        def _():
            s_ref[...] = jnp.zeros_like(s_ref)

        gpv = gp_ref[...]
        cm = _chunk_common(_heads_of(qkv_ref, 0, heads), _heads_of(qkv_ref, da, heads), _heads_of(qkv_ref, 2 * da, heads),
                           *_gate_columns(ab_ref[...], gpv, heads))
        state = s_ref[...]
        st_ref[0] = state
        vn, o, new_state = _chunk_step(cm, state)
        s_ref[...] = new_state
        for slot, val in enumerate((cm["a0"], cm["tm"], cm["p0"])):
            kc_ref[0, slot] = val
        for slot, val in enumerate((cm["w"], vn, o)):
            kw_ref[0, slot] = val
        oa = _gated_norm(o, _heads_of(z_ref, 0, heads), gpv[2:3, :])[3]
        for h in range(heads):
            oa_ref[:, h * HEAD_DIM:(h + 1) * HEAD_DIM] = oa[h]

    return pl.pallas_call(
        body, name="delta_fwd", grid=(n,),
        in_specs=[pl.BlockSpec((CHUNK, 3 * da), lambda i: (i, 0)), pl.BlockSpec((CHUNK, da), lambda i: (i, 0)),
                  pl.BlockSpec((CHUNK, LANES), lambda i: (i, 0)), pl.BlockSpec((8, LANES), lambda i: (0, 0))],
        out_specs=[pl.BlockSpec((CHUNK, da), lambda i: (i, 0)),
                   pl.BlockSpec((1, heads, HEAD_DIM, HEAD_DIM), lambda i: (i, 0, 0, 0)),
                   pl.BlockSpec((1, 3, heads, CHUNK, CHUNK), lambda i: (i, 0, 0, 0, 0)),
                   pl.BlockSpec((1, 3, heads, CHUNK, HEAD_DIM), lambda i: (i, 0, 0, 0, 0))],
        out_shape=[jax.ShapeDtypeStruct((t, da), F32), jax.ShapeDtypeStruct((n, heads, HEAD_DIM, HEAD_DIM), F32),
                   jax.ShapeDtypeStruct((n, 3, heads, CHUNK, CHUNK), F32),
                   jax.ShapeDtypeStruct((n, 3, heads, CHUNK, HEAD_DIM), F32)],
        scratch_shapes=[pltpu.VMEM((heads, HEAD_DIM, HEAD_DIM), F32)],
        compiler_params=_params("arbitrary"))(qkv, z, ab, gpar)


def _delta_bwd(qkv, z, ab, gpar, states, kept_c, kept_w, doa, heads):
    t = qkv.shape[0]
    da = heads * HEAD_DIM
    n = t // CHUNK
    c = CHUNK

    def body(qkv_ref, z_ref, ab_ref, gp_ref, st_ref, kc_ref, kw_ref, doa_ref, dqkv_ref, dz_ref, dab_ref, dpar_ref, ds_ref):
        @pl.when(pl.program_id(0) == 0)
        def _():
            ds_ref[...] = jnp.zeros_like(ds_ref)
            dpar_ref[...] = jnp.zeros_like(dpar_ref)

        gpv = gp_ref[...]
        og = gpv[2:3, :]
        q, k, v = _heads_of(qkv_ref, 0, heads), _heads_of(qkv_ref, da, heads), _heads_of(qkv_ref, 2 * da, heads)
        cm = _chunk_common(q, k, v, *_gate_columns(ab_ref[...], gpv, heads),
                           kept=(kc_ref[0, 0], kc_ref[0, 1], kc_ref[0, 2], kw_ref[0, 0]))
        state = st_ref[0]
        dsp = ds_ref[...]
        vn, o = kw_ref[0, 1], kw_ref[0, 2]
        zv = _heads_of(z_ref, 0, heads)
        xo, ro, sgz, _ = _gated_norm(o, zv, og)
        doav = _heads_of(doa_ref, 0, heads)
        don = doav * (zv * sgz)
        dz = doav * (xo * og) * _dsilu(zv, sgz)
        d_og = jnp.sum(jnp.sum(don * xo, axis=1, keepdims=True), axis=0)
        do = _rms_bwd(don * og, xo, ro)
        tm, dmat, eg, edl, egl = cm["tm"], cm["dmat"], cm["eg"], cm["edl"], cm["egl"]
        dos, dsps, sts, tms, ks = _Split(do), _Split(dsp), _Split(state), _Split(tm), _Split(k)
        dvn = _bmm_tn(cm["attn"], dos) + _bmm(cm["kd"], dsps)
        dvns = _Split(dvn)
        dqe = _bmm_nt(dos, sts)
        ds_ref[...] = _bmm_tn(cm["qe"], dos) + dsp * egl[:, :, :1] - _bmm_tn(cm["w"], dvns)
        dattn = _bmm_nt(dos, vn)
        dkd = _bmm_nt(vn, dsps)
        dkd_kd = jnp.sum(dkd * cm["kd"], axis=-1, keepdims=True)
        dgl = (jnp.sum(jnp.sum(dsp * state, axis=-1, keepdims=True), axis=1, keepdims=True) * egl[:, :, :1]
               + jnp.sum(dkd_kd, axis=1, keepdims=True))
        dgc = jnp.sum(dqe * cm["qe"], axis=-1, keepdims=True) - dkd_kd
        dk = dkd * edl
        dq = dqe * eg
        dw = -_bmm_nt(dvns, sts)
        dws = _Split(dw)
        dp0 = dattn * dmat
        dd = jnp.where(cm["incl"], dattn * cm["p0"], 0.0)
        dp0s = _Split(dp0)
        dq = dq + _bmm(dp0s, ks)
        dk = dk + _bmm_tn(dp0s, q)
        dtm = _bmm_nt(dvns, cm["vb"]) + _bmm_nt(dws, cm["kbe"])
        dvb = _bmm_tn(tms, dvns)
        dkbe = _bmm_tn(tms, dws)
        dkb = dkbe * eg
        dgc = dgc + jnp.sum(dkbe * cm["kbe"], axis=-1, keepdims=True)
        dlow = jnp.where(cm["strict"], -_bmm_tn(tms, _bmm_nt(dtm, tms)), 0.0)
        dd = dd + dlow * cm["a0"]
        da0 = dlow * dmat
        da0s = _Split(da0)
        dkb = dkb + _bmm(da0s, ks)
        dk = dk + _bmm_tn(da0s, cm["kb"])
        ddd = dd * dmat
        ones = _Split(jnp.ones((heads, c, HEAD_DIM), F32), exact=True)
        dgc = dgc + jnp.sum(ddd, axis=-1, keepdims=True) - _bmm_tn(ddd, ones)[:, :, :1]
        dgc = dgc + jnp.where(_rows((c, 1)) == c - 1, dgl, 0.0)
        dg = _bmm_tn(cm["inclf"], jnp.broadcast_to(dgc, (heads, c, HEAD_DIM)))[:, :, :1]
        beta = cm["beta"]
        dk = dk + dkb * beta
        dbeta = jnp.sum(dkb * k, axis=-1, keepdims=True) + jnp.sum(dvb * v, axis=-1, keepdims=True)
        dv = dvb * beta
        db_col = dbeta * beta * (1.0 - beta)
        da_col = dg * cm["neg_ea"] * _sigmoid(cm["xg"])
        d_alog = jnp.sum(dg * cm["g"], axis=1, keepdims=True)
        d_dtb = jnp.sum(da_col, axis=1, keepdims=True)
        lane = lax.broadcasted_iota(jnp.int32, (c, LANES), 1)
        lane8 = lax.broadcasted_iota(jnp.int32, (8, LANES), 1)
        row8 = _rows((8, LANES))
        dab = jnp.zeros((c, LANES), F32)
        dpar = jnp.where(row8 == 2, d_og, 0.0)
        for h in range(heads):
            lo = h * HEAD_DIM
            dqkv_ref[:, lo:lo + HEAD_DIM] = dq[h]
            dqkv_ref[:, da + lo:da + lo + HEAD_DIM] = dk[h]
            dqkv_ref[:, 2 * da + lo:2 * da + lo + HEAD_DIM] = dv[h]
            dz_ref[:, lo:lo + HEAD_DIM] = dz[h]
            dab = dab + jnp.where(lane == h, da_col[h], 0.0) + jnp.where(lane == heads + h, db_col[h], 0.0)
            dpar = (dpar + jnp.where((row8 == 0) & (lane8 == h), d_alog[h], 0.0)
                    + jnp.where((row8 == 1) & (lane8 == h), d_dtb[h], 0.0))
        dab_ref[...] = dab
        dpar_ref[...] += dpar

    rev = lambda i: (n - 1 - i, 0)
    return pl.pallas_call(
        body, name="delta_bwd", grid=(n,),
        in_specs=[pl.BlockSpec((c, 3 * da), rev), pl.BlockSpec((c, da), rev), pl.BlockSpec((c, LANES), rev),
                  pl.BlockSpec((8, LANES), lambda i: (0, 0)),
                  pl.BlockSpec((1, heads, HEAD_DIM, HEAD_DIM), lambda i: (n - 1 - i, 0, 0, 0)),
                  pl.BlockSpec((1, 3, heads, c, c), lambda i: (n - 1 - i, 0, 0, 0, 0)),
                  pl.BlockSpec((1, 3, heads, c, HEAD_DIM), lambda i: (n - 1 - i, 0, 0, 0, 0)),
                  pl.BlockSpec((c, da), rev)],
        out_specs=[pl.BlockSpec((c, 3 * da), rev), pl.BlockSpec((c, da), rev), pl.BlockSpec((c, LANES), rev),
                   pl.BlockSpec((8, LANES), lambda i: (0, 0))],
        out_shape=[jax.ShapeDtypeStruct((t, 3 * da), F32), jax.ShapeDtypeStruct((t, da), F32),
                   jax.ShapeDtypeStruct((t, LANES), F32), jax.ShapeDtypeStruct((8, LANES), F32)],
        scratch_shapes=[pltpu.VMEM((heads, HEAD_DIM, HEAD_DIM), F32)],
        compiler_params=_params("arbitrary"))(qkv, z, ab, gpar, states, kept_c, kept_w, doa)


def _w_in_pieces(shard_cols, da, heads):
    a0, nab = 4 * da, 2 * heads
    d_in = 4 * shard_cols
    runs = [(0, a0, 0), (a0, a0 + nab, d_in - nab), (a0 + nab, d_in, a0)]
    pieces = []
    for j in range(4):
        lo, hi = j * shard_cols, (j + 1) * shard_cols
        for rlo, rhi, plo in runs:
            s, e = max(lo, rlo), min(hi, rhi)
            if s < e:
                pieces.append((j, s - lo, e - s, plo + (s - rlo)))
    return pieces, d_in - nab + LANES


def _w_in_pack(w4, da, heads):
    _, d, sc = w4.shape
    pieces, npk = _w_in_pieces(sc, da, heads)
    tr = _tile_rows(d, 256, SUBLANES_WIRE)

    def body(w_ref, o_ref):
        o_ref[:, npk - LANES:] = jnp.zeros((tr, LANES), o_ref.dtype)
        for j, lo, ln, dst in pieces:
            o_ref[:, dst:dst + ln] = w_ref[j, :, lo:lo + ln]

    return pl.pallas_call(
        body, name="w_in_pack", grid=(d // tr,),
        in_specs=[pl.BlockSpec((4, tr, sc), lambda i: (0, i, 0))],
        out_specs=pl.BlockSpec((tr, npk), lambda i: (i, 0)),
        out_shape=jax.ShapeDtypeStruct((d, npk), w4.dtype),
        compiler_params=_params("arbitrary"))(w4)


def _w_in_unpack(dwp, sc, da, heads):
    d, npk = dwp.shape
    pieces, _ = _w_in_pieces(sc, da, heads)
    tr = _tile_rows(d, 256)

    def body(g_ref, o_ref):
        for j, lo, ln, dst in pieces:
            o_ref[j, :, lo:lo + ln] = g_ref[:, dst:dst + ln]

    return pl.pallas_call(
        body, name="w_in_unpack", grid=(d // tr,),
        in_specs=[pl.BlockSpec((tr, npk), lambda i: (i, 0))],
        out_specs=pl.BlockSpec((4, tr, sc), lambda i: (0, i, 0)),
        out_shape=jax.ShapeDtypeStruct((4, d, sc), F32),
        compiler_params=_params("arbitrary"))(dwp)


def _block_diag(pool_w):
    g, gd, _ = pool_w.shape
    out = jnp.zeros((g * gd, g * gd), pool_w.dtype)
    for gi in range(g):
        out = lax.dynamic_update_slice(out, pool_w[gi], (gi * gd, gi * gd))
    return out


def _layer_dims(d):
    heads = (d // 2) // HEAD_DIM
    return heads, heads * HEAD_DIM, d // 4, d // 4


BIG = ("w_in", "w_gate", "w_up", "ple_proj", "w_out", "w_down", "ple_gate")
TRANSPOSED = ("w_gate", "w_up")


def _prepare_layer(small, li):
    d = small["norm1_g"].shape[1]
    heads, _, _, _ = _layer_dims(d)
    gpar = jnp.zeros((8, LANES), F32)
    gpar = gpar.at[0, :heads].set(small["a_log"][li]).at[1, :heads].set(small["dt_bias"][li]).at[2, :].set(small["onorm_g"][li])
    return dict(norm1_g=small["norm1_g"][li][None], conv_qkv=small["conv_qkv"][li], gpar=gpar, pool_bd=_block_diag(small["pool_w"][li]).astype(MM_DTYPE),
                pool_scale=small["pool_scale"][li][None], sconv_w=small["sconv_w"][li], norm2_g=small["norm2_g"][li][None])


def _layer_fwd(x0, p, gw, lw, tm, arrive):
    d = x0.shape[1]
    heads, da, dp, dc = _layer_dims(d)
    segs = (3 * da, da, dp, 3 * dc, LANES)
    lw["w_in_p"] = _w_in_pack(gw["w_in"], da, heads).astype(MM_DTYPE)
    qkv_pre, z, hp, cbcch, ab = _in_proj_fwd(x0, lw["norm1_g"], lw["w_in_p"], segs, tm)
    qkv = _qkv_conv_fwd(qkv_pre, lw["conv_qkv"], heads)
    oa, states, kept_c, kept_w = _delta_fwd(qkv, z, ab, lw["gpar"], heads)
    ob = _pool_fwd(hp, lw["pool_bd"], lw["pool_scale"], dp // POOL_GROUPS)
    oc = _sconv_fwd(cbcch, lw["sconv_w"])
    arrive("mixed", oa)
    x1, h2 = _out_proj_fwd(x0, (oa, ob, oc), gw["w_out"], lw["norm2_g"], tm)
    x2, gp, up = _ffn_fwd(x1, h2, gw["w_gate"], gw["w_up"], gw["w_down"], tm)
    arrive("ffn", x2)
    x3 = _ple_fwd(x2, p, gw["ple_gate"], gw["ple_proj"], tm)
    arrive("end", x3)
    saved = dict(x0=x0, qkv_pre=qkv_pre, z=z, hp=hp, cbcch=cbcch, ab=ab, qkv=qkv, states=states, kept_c=kept_c, kept_w=kept_w, oa=oa, ob=ob, oc=oc,
                 x1=x1, h2=h2, gp=gp, up=up, x2=x2)
    return x3, saved


def _layer_bwd(dx3, p, gw, lw, sv, tm, produced):
    def after_token(tok, arr):
        return arr if tok is None else arr + tok[0, 0]

    d = dx3.shape[1]
    heads, da, dp, dc = _layer_dims(d)
    segs = (3 * da, da, dp, dc, dc, dc, LANES)
    gd = dp // POOL_GROUPS
    dx2, d_ple_gate, d_ple_proj = _ple_bwd(dx3, sv["x2"], p, gw["ple_gate"], gw["ple_proj"], tm)
    dh2, d_w_gate, d_w_up, d_w_down = _ffn_bwd(dx2, sv["h2"], sv["gp"], sv["up"], gw["w_gate"], gw["w_up"], gw["w_down"],
                                               min(tm, 256))
    tok = produced("ffn", dict(w_gate=d_w_gate, w_up=d_w_up, ple_proj=d_ple_proj, w_down=d_w_down, ple_gate=d_ple_gate), dh2)
    dx1, doa, dob, doc, d_w_out, d_norm2 = _out_proj_bwd(dx2, dh2, sv["x1"], after_token(tok, lw["norm2_g"]),
                                                         (sv["oa"], sv["ob"], sv["oc"]), gw["w_out"], tm)
    dcb, dcc, dch, d_sconv = _sconv_bwd(sv["cbcch"], lw["sconv_w"], doc)
    dhp, d_pool_bd, d_pool_scale = _pool_bwd(sv["hp"], lw["pool_bd"], lw["pool_scale"], dob, gd)
    dqkv, dz, dab, dpar = _delta_bwd(sv["qkv"], sv["z"], sv["ab"], lw["gpar"], sv["states"], sv["kept_c"], sv["kept_w"], doa,
                                      heads)
    tok = produced("mixers", {}, dqkv)
    dqkv_pre, d_conv_qkv = _qkv_conv_bwd(sv["qkv_pre"], lw["conv_qkv"], dqkv, heads)
    dsegs = (dqkv_pre, dz, dhp, dcb, dcc, dch, dab)
    dx0, d_w_in_p, d_norm1 = _in_proj_bwd(sv["x0"], after_token(tok, lw["norm1_g"]), lw["w_in_p"], dsegs, dx1, segs, tm)
    per = LANES // gd
    bd = d_pool_bd.reshape(dp // LANES, per, gd, per, gd)
    d_pool_w = jnp.stack([bd[gi // per, gi % per, :, gi % per, :] for gi in range(POOL_GROUPS)])
    big = dict(w_in=_w_in_unpack(d_w_in_p, gw["w_in"].shape[2], da, heads), w_gate=d_w_gate, w_up=d_w_up,
               ple_proj=d_ple_proj, w_out=d_w_out, w_down=d_w_down, ple_gate=d_ple_gate)
    small = dict(norm1_g=d_norm1[0], conv_qkv=d_conv_qkv, a_log=dpar[0, :heads], dt_bias=dpar[1, :heads], onorm_g=dpar[2],
                 pool_w=d_pool_w, pool_scale=d_pool_scale[0], sconv_w=d_sconv, norm2_g=d_norm2[0])
    tok = produced("end", dict(w_in=big["w_in"], w_out=d_w_out), big["w_in"])
    return dx0, big, small, tok


def _local_step(x, p, target, gw, small, produced=None, arrive=None):
    t, d = x.shape
    depth = p.shape[0]
    tm = 512 if t % 512 == 0 else 128
    layers = [_prepare_layer(small, li) for li in range(depth)]
    saved = []
    h = x
    for li in range(depth):
        h, sv = _layer_fwd(h, p[li], gw[li], layers[li], tm,
                           (lambda stage, after, li=li: arrive(li, stage, after)) if arrive else (lambda stage, after: None))
        saved.append(sv)
    dx, loss, d_final = _loss_head(h, target, small["final_g"][None], tm)
    big, sm = [None] * depth, [None] * depth
    token = None
    for li in reversed(range(depth)):
        p_li = p[li] if token is None else p[li] + token[0, 0]
        dx, big[li], sm[li], token = _layer_bwd(
            dx, p_li, gw[li], layers[li], saved[li], tm,
            (lambda stage, grads, after, li=li: produced(li, stage, grads, after)) if produced else (lambda *a: None))
    small_grads = {n: jnp.stack([g[n] for g in sm]) for n in sm[0]}
    small_grads["final_g"] = d_final[0]
    return loss[0, 0], dx, big, small_grads


def _coords():
    return lax.axis_index("x"), lax.axis_index("y"), lax.axis_index("c")


def _other_chips(x, y):
    return [(1 - x, y), (x, 1 - y), (1 - x, 1 - y)]


def _place_shards(ws, me_idx):
    nt = len(ws)
    depth = ws[0].shape[0]

    def body(me_ref, *refs):
        for t, w_ref in enumerate(refs[:nt]):
            for li in range(depth):
                refs[nt + li * nt + t][...] = w_ref[li].astype(WIRE_DTYPE)

    outs = pl.pallas_call(
        body, name="place_shards",
        grid_spec=pltpu.PrefetchScalarGridSpec(
            num_scalar_prefetch=1, grid=(4,),
            in_specs=[pl.BlockSpec((depth, w.shape[1] // 4, w.shape[2]), lambda i, me_ref: (0, i, 0)) for w in ws],
            out_specs=[pl.BlockSpec((None, w.shape[1] // 4, w.shape[2]), lambda i, me_ref: (me_ref[0], i, 0))
                       for _ in range(depth) for w in ws]),
        out_shape=[jax.ShapeDtypeStruct((4,) + w.shape[1:], WIRE_DTYPE) for _ in range(depth) for w in ws],
        compiler_params=_params("arbitrary"))(me_idx, *ws)
    return [list(outs[li * nt:(li + 1) * nt]) for li in range(depth)]


def _half_block(ref, chip, pc):
    rh = ref.shape[1] // 2
    return ref.at[chip, pl.ds(pc * rh, rh)]


def _gather_copies(out_refs, send_sems, recv_sems, stage):
    nt = len(out_refs)
    x, y, c = _coords()
    pairs = []
    for j, (cx, cy) in enumerate(_other_chips(x, y)):
        for t in range(nt):
            sems = dict(send_sem=send_sems[j * nt + t], recv_sem=recv_sems[j * nt + t], device_id_type=MESH)
            if stage == 0:
                mine, theirs, to = _half_block(out_refs[t], 2 * x + y, c), _half_block(out_refs[t], 2 * cx + cy, c), (cx, cy, c)
            else:
                mine, theirs, to = (_half_block(out_refs[t], 2 * cx + cy, c), _half_block(out_refs[t], 2 * cx + cy, 1 - c),
                                    (x, y, 1 - c))
            pairs.append((pltpu.make_async_remote_copy(src_ref=mine, dst_ref=mine, device_id=to, **sems),
                          pltpu.make_async_remote_copy(src_ref=theirs, dst_ref=theirs, device_id=to, **sems)))
    return pairs


def _gather_call(name, arrs, wait_sems, after, stage):
    nt = len(arrs)
    nc = 3 * nt
    n_wait = len(wait_sems)
    n_new = 2 * nc if stage < 2 else 0
    arrs = [pltpu.with_memory_space_constraint(a, pltpu.HBM) for a in arrs]

    def body(*refs):
        a_refs = refs[:nt]
        waits = refs[nt:nt + n_wait]
        news = refs[nt + n_wait + 1:nt + n_wait + 1 + n_new]
        token = refs[-1]
        if stage > 0:
            for start, arrival in _gather_copies(a_refs, waits[:nc], waits[nc:], stage - 1):
                start.wait_send()
                arrival.wait_recv()
        if stage < 2:
            for start, _ in _gather_copies(a_refs, news[:nc], news[nc:], stage):
                start.start()
        token[...] = jnp.zeros_like(token)

    outs = pl.pallas_call(
        body, name=name,
        out_shape=(*[pltpu.SemaphoreType.DMA(())] * n_new, *[pltpu.HBM(a.shape, a.dtype) for a in arrs],
                   jax.ShapeDtypeStruct((8, LANES), F32)),
        in_specs=[HBM] * nt + [SEM] * n_wait + [ANY],
        out_specs=(*[SEM] * n_new, *[HBM] * nt, pl.BlockSpec(memory_space=pltpu.VMEM)),
        input_output_aliases={t: n_new + t for t in range(nt)},
        compiler_params=pltpu.CompilerParams(has_side_effects=pltpu.SideEffectType.DATAFLOW_SIDE_EFFECTING),
    )(*arrs, *wait_sems, after)
    return list(outs[:n_new]), list(outs[n_new:n_new + nt]), outs[-1]


def _add_my_halves(gs, others, c_idx):
    nt = len(gs)

    def body(c_ref, *refs):
        for g_ref, o_ref, out_ref in zip(refs[:nt], refs[nt:2 * nt], refs[2 * nt:]):
            out_ref[...] = (g_ref[...].astype(F32) + o_ref[...].astype(F32)).astype(out_ref.dtype)

    def half(g):
        return pl.BlockSpec((None, g.shape[1] // 2, g.shape[2]), lambda j, c_ref: (j, 0, 0))

    return pl.pallas_call(
        body, name="add_my_halves",
        grid_spec=pltpu.PrefetchScalarGridSpec(
            num_scalar_prefetch=1, grid=(4,),
            in_specs=[pl.BlockSpec((None, g.shape[1] // 2, g.shape[2]), lambda j, c_ref: (j, c_ref[0], 0)) for g in gs]
                     + [half(g) for g in gs],
            out_specs=[half(g) for g in gs]),
        out_shape=[jax.ShapeDtypeStruct((4, g.shape[1] // 2, g.shape[2]), WIRE_DTYPE) for g in gs],
        compiler_params=_params("arbitrary"))(c_idx, *gs, *others)


def _split_plan(kind, s_refs, l_refs):
    x, y, c = _coords()
    if kind == "devices":
        peers = [(x ^ ((k >> 2) & 1), y ^ ((k >> 1) & 1), c ^ (k & 1)) for k in range(1, 8)]
        return [(s, l.at[4 * x + 2 * y + c], peer) for peer in peers for s, l in zip(s_refs, l_refs)]
    if kind == "swap":
        return [(s.at[:, pl.ds((1 - c) * (s.shape[1] // 2), s.shape[1] // 2)], l, (x, y, 1 - c)) for s, l in zip(s_refs, l_refs)]
    return [(s.at[2 * cx + cy], l.at[j], (cx, cy, c)) for j, (cx, cy) in enumerate(_other_chips(x, y))
            for s, l in zip(s_refs, l_refs)]


def _split_landing(kind, a):
    if kind == "devices":
        return (8,) + a.shape
    return (a.shape[0], a.shape[1] // 2, a.shape[2]) if kind == "swap" else (3,) + a.shape[1:]


def _copies_start(name, kind, srcs, after=None):
    ns = len(srcs)
    n = {"swap": 1, "exchange": 3, "devices": 7}[kind] * ns
    srcs = [pltpu.with_memory_space_constraint(a, pltpu.HBM) for a in srcs]
    fresh = jnp.zeros if kind == "devices" else lax.empty
    lands = [pltpu.with_memory_space_constraint(fresh(_split_landing(kind, a), a.dtype), pltpu.HBM) for a in srcs]
    extra = [] if after is None else [after]

    def body(*refs):
        first_sem = 2 * ns + len(extra)
        sems, token = refs[first_sem:first_sem + 2 * n], refs[-1]
        for k, (src, dst, dev) in enumerate(_split_plan(kind, refs[:ns], refs[ns:2 * ns])):
            pltpu.make_async_remote_copy(src_ref=src, dst_ref=dst, send_sem=sems[k], recv_sem=sems[n + k], device_id=dev,
                                         device_id_type=MESH).start()
        token[...] = jnp.zeros_like(token)

    outs = pl.pallas_call(
        body, name=name,
        out_shape=(*[pltpu.SemaphoreType.DMA(())] * (2 * n), *[pltpu.HBM(a.shape, a.dtype) for a in srcs + lands],
                   jax.ShapeDtypeStruct((8, LANES), F32)),
        in_specs=[HBM] * (2 * ns) + [ANY] * len(extra),
        out_specs=(*[SEM] * (2 * n), *[HBM] * (2 * ns), pl.BlockSpec(memory_space=pltpu.VMEM)),
        input_output_aliases={t: 2 * n + t for t in range(2 * ns)},
        compiler_params=pltpu.CompilerParams(has_side_effects=pltpu.SideEffectType.DATAFLOW_SIDE_EFFECTING),
    )(*srcs, *lands, *extra)
    return list(outs[:2 * n]), list(outs[2 * n:2 * n + ns]), list(outs[2 * n + ns:2 * n + 2 * ns]), outs[-1]


def _copies_wait(name, kind, sems, srcs, lands, after):
    ns = len(srcs)
    n = len(sems) // 2

    def body(*refs):
        sem_refs = refs[2 * ns:2 * ns + 2 * n]
        for k, (src, dst, dev) in enumerate(_split_plan(kind, refs[:ns], refs[ns:2 * ns])):
            cp = pltpu.make_async_remote_copy(src_ref=src, dst_ref=dst, send_sem=sem_refs[k], recv_sem=sem_refs[n + k],
                                              device_id=dev, device_id_type=MESH)
            cp.wait_send()
            cp.wait_recv()

    outs = pl.pallas_call(
        body, name=name, out_shape=tuple(pltpu.HBM(a.shape, a.dtype) for a in srcs + lands),
        in_specs=[HBM] * (2 * ns) + [SEM] * (2 * n) + [ANY], out_specs=tuple([HBM] * (2 * ns)),
        input_output_aliases={t: t for t in range(2 * ns)},
        compiler_params=pltpu.CompilerParams(has_side_effects=pltpu.SideEffectType.DATAFLOW_SIDE_EFFECTING),
    )(*srcs, *lands, *sems, after)
    return list(outs[:ns]), list(outs[ns:])


def _sum_into(pairs, recvs, idx, li, depth, accs):
    nt = len(pairs)

    def body(idx_ref, *refs):
        for p_ref, r_ref, out_ref in zip(refs[:nt], refs[nt:2 * nt], refs[-nt:]):
            out_ref[...] = p_ref[...].astype(F32) + r_ref[0].astype(F32) + r_ref[1].astype(F32) + r_ref[2].astype(F32)

    in_specs = ([pl.BlockSpec((None, p.shape[1] // 2, p.shape[2]), lambda i, idx_ref: (idx_ref[0], i, 0)) for p in pairs]
                + [pl.BlockSpec((3, p.shape[1] // 2, p.shape[2]), lambda i, idx_ref: (0, i, 0)) for p in pairs])
    args = [idx, *pairs, *recvs]
    aliases = {}
    if accs[0] is not None:
        in_specs += [ANY] * nt
        args += list(accs)
        aliases = {1 + 2 * nt + t: t for t in range(nt)}
    return pl.pallas_call(
        body, name="sum_into",
        grid_spec=pltpu.PrefetchScalarGridSpec(
            num_scalar_prefetch=1, grid=(2,), in_specs=in_specs,
            out_specs=[pl.BlockSpec((None, p.shape[1] // 2, p.shape[2]), lambda i, idx_ref: (li, 2 * idx_ref[1] + i, 0))
                       for p in pairs]),
        out_shape=[jax.ShapeDtypeStruct((depth, 2 * p.shape[1], p.shape[2]), F32) for p in pairs],
        input_output_aliases=aliases, compiler_params=_params("arbitrary"))(*args)


def _sum_devices(own, land, me_dev):
    rows = own.shape[0]
    tr = _tile_rows(rows, 512)

    def body(me_ref, o_ref, l_ref, out_ref):
        acc = jnp.where(me_ref[0] == 0, o_ref[...], l_ref[0])
        for s in range(1, 8):
            acc = acc + jnp.where(me_ref[0] == s, o_ref[...], l_ref[s])
        out_ref[...] = acc

    return pl.pallas_call(
        body, name="sum_devices",
        grid_spec=pltpu.PrefetchScalarGridSpec(
            num_scalar_prefetch=1, grid=(rows // tr,),
            in_specs=[pl.BlockSpec((tr, LANES), lambda i, me_ref: (i, 0)), pl.BlockSpec((8, tr, LANES), lambda i, me_ref: (0, i, 0))],
            out_specs=pl.BlockSpec((tr, LANES), lambda i, me_ref: (i, 0))),
        out_shape=jax.ShapeDtypeStruct((rows, LANES), F32), compiler_params=_params("arbitrary"))(me_dev, own, land)


def _sibling_share(gs, li):
    nt = len(gs)

    def body(*refs):
        out_refs = refs[nt:2 * nt]
        send_sems, recv_sems = refs[2 * nt:]
        x, y, c = _coords()
        sends, recvs = [], []
        for t in range(nt):
            rh = out_refs[t].shape[1] // 2
            mine, theirs = out_refs[t].at[li, pl.ds(c * rh, rh)], out_refs[t].at[li, pl.ds((1 - c) * rh, rh)]
            sems = dict(send_sem=send_sems.at[t], recv_sem=recv_sems.at[t], device_id=(x, y, 1 - c), device_id_type=MESH)
            sends.append(pltpu.make_async_remote_copy(src_ref=mine, dst_ref=mine, **sems))
            recvs.append(pltpu.make_async_remote_copy(src_ref=theirs, dst_ref=theirs, **sems))
        for cp in sends:
            cp.start()
        for cp in recvs:
            cp.wait_recv()
        for cp in sends:
            cp.wait_send()

    return pl.pallas_call(
        body, name="sibling_share", out_shape=[jax.ShapeDtypeStruct(g.shape, g.dtype) for g in gs],
        in_specs=[ANY] * nt, out_specs=[ANY] * nt, input_output_aliases={t: t for t in range(nt)},
        scratch_shapes=[pltpu.SemaphoreType.DMA((nt,)), pltpu.SemaphoreType.DMA((nt,))])(*gs)


def _all_gather_devices(buf, after=None):
    extra = [] if after is None else [after]

    def body(b_ref, *rest):
        out_ref, send_sems, recv_sems, local_sem = rest[len(extra):]
        x, y, c = _coords()
        me = 4 * x + 2 * y + c
        mine = pltpu.make_async_copy(b_ref, out_ref.at[me], local_sem)
        mine.start()
        peers = []
        for k in range(1, 8):
            fx, fy, fc = (k >> 2) & 1, (k >> 1) & 1, k & 1
            peers.append((x ^ fx, y ^ fy, c ^ fc))
        sends = [pltpu.make_async_remote_copy(src_ref=b_ref, dst_ref=out_ref.at[me], send_sem=send_sems.at[k],
                                              recv_sem=recv_sems.at[k], device_id=peer, device_id_type=MESH)
                 for k, peer in enumerate(peers)]
        for cp in sends:
            cp.start()
        for k, (px, py, pc) in enumerate(peers):
            pltpu.make_async_remote_copy(src_ref=b_ref, dst_ref=out_ref.at[4 * px + 2 * py + pc], send_sem=send_sems.at[k],
                                         recv_sem=recv_sems.at[k], device_id=(px, py, pc), device_id_type=MESH).wait_recv()
        for cp in sends:
            cp.wait_send()
        mine.wait()

    return pl.pallas_call(
        body, name="all_gather_devices", out_shape=jax.ShapeDtypeStruct((8,) + buf.shape, buf.dtype),
        in_specs=[ANY] * (1 + len(extra)), out_specs=ANY,
        scratch_shapes=[pltpu.SemaphoreType.DMA((7,)), pltpu.SemaphoreType.DMA((7,)), pltpu.SemaphoreType.DMA(())])(buf, *extra)


SMALL_SHARDED = ("conv_qkv", "sconv_w")
REPLICATED = ("norm1_g", "a_log", "dt_bias", "onorm_g", "pool_w", "pool_scale", "norm2_g", "final_g")
ALL_WEIGHTS = ("norm1_g", "w_in", "conv_qkv", "a_log", "dt_bias", "onorm_g", "pool_w", "pool_scale", "sconv_w", "w_out",
               "norm2_g", "w_gate", "w_up", "w_down", "ple_proj", "ple_gate", "final_g")


def _pad_rows(flat, row_multiple):
    m = flat.shape[0]
    r = -(-m // (LANES * row_multiple)) * row_multiple
    return jnp.pad(flat, (0, r * LANES - m)).reshape(r, LANES)


def _adamw_math(w, g, m, v):
    c1 = 1.0 / (1.0 - ADAM_B1 ** ADAM_STEP)
    c2 = 1.0 / (1.0 - ADAM_B2 ** ADAM_STEP)
    nm = ADAM_B1 * m + (1.0 - ADAM_B1) * g
    nv = ADAM_B2 * v + (1.0 - ADAM_B2) * (g * g)
    return -ADAM_LR * ((nm * c1) / (jnp.sqrt(nv * c2) + ADAM_EPS) + ADAM_WD * w), nm, nv


def _adamw(w, g, m, v):
    shape = w.shape
    cols = shape[-1]
    rows = w.size // cols
    tr = _tile_rows(rows, 512)

    def body(w_ref, g_ref, m_ref, v_ref, d_ref, nm_ref, nv_ref, go_ref):
        gv = g_ref[...]
        d_ref[...], nm_ref[...], nv_ref[...] = _adamw_math(w_ref[...], gv, m_ref[...], v_ref[...])
        go_ref[...] = gv

    spec = pl.BlockSpec((tr, cols), lambda i: (i, 0))
    outs = pl.pallas_call(
        body, name="adamw", grid=(rows // tr,), in_specs=[spec] * 4, out_specs=[spec] * 4,
        out_shape=[jax.ShapeDtypeStruct((rows, cols), F32)] * 4,
        compiler_params=_params("arbitrary"))(*[a.reshape(rows, cols) for a in (w, g, m, v)])
    return tuple(o.reshape(shape) for o in outs)


def kernel(x, p, norm1_g, w_in, conv_qkv, a_log, dt_bias, onorm_g, pool_w, pool_scale, sconv_w, w_out, norm2_g, w_gate, w_up, w_down, ple_proj, ple_gate, final_g, loss_target, m_norm1_g, m_w_in, m_conv_qkv, m_a_log, m_dt_bias, m_onorm_g, m_pool_w, m_pool_scale, m_sconv_w, m_w_out, m_norm2_g, m_w_gate, m_w_up, m_w_down, m_ple_proj, m_ple_gate, m_final_g, v_norm1_g, v_w_in, v_conv_qkv, v_a_log, v_dt_bias, v_onorm_g, v_pool_w, v_pool_scale, v_sconv_w, v_w_out, v_norm2_g, v_w_gate, v_w_up, v_w_down, v_ple_proj, v_ple_gate, v_final_g):
    weights = dict(zip(ALL_WEIGHTS, (norm1_g, w_in, conv_qkv, a_log, dt_bias, onorm_g, pool_w, pool_scale, sconv_w, w_out,
                                     norm2_g, w_gate, w_up, w_down, ple_proj, ple_gate, final_g)))
    mom_m = dict(zip(ALL_WEIGHTS, (m_norm1_g, m_w_in, m_conv_qkv, m_a_log, m_dt_bias, m_onorm_g, m_pool_w, m_pool_scale,
                                   m_sconv_w, m_w_out, m_norm2_g, m_w_gate, m_w_up, m_w_down, m_ple_proj, m_ple_gate, m_final_g)))
    mom_v = dict(zip(ALL_WEIGHTS, (v_norm1_g, v_w_in, v_conv_qkv, v_a_log, v_dt_bias, v_onorm_g, v_pool_w, v_pool_scale,
                                   v_sconv_w, v_w_out, v_norm2_g, v_w_gate, v_w_up, v_w_down, v_ple_proj, v_ple_gate, v_final_g)))
    for n in TRANSPOSED:
        weights[n], mom_m[n], mom_v[n] = (jnp.swapaxes(a[n], 1, 2) for a in (weights, mom_m, mom_v))
    c_idx = lax.axis_index("c").astype(jnp.int32).reshape(1)
    chip = (2 * lax.axis_index("x") + lax.axis_index("y")).astype(jnp.int32)
    me_idx = chip.reshape(1)
    idx = jnp.stack([chip, lax.axis_index("c").astype(jnp.int32)])
    depth = p.shape[0]

    small = {n: weights[n] for n in REPLICATED}
    sflat = _pad_rows(jnp.concatenate([weights[n].reshape(-1) for n in SMALL_SHARDED]), 8)
    sgath8 = _all_gather_devices(sflat)
    placed_in = _place_shards([weights["w_in"]], me_idx)
    sems, arrs, _ = _gather_call("gather_first_start", placed_in[0], [], sgath8, 0)
    placed_rest = _place_shards([weights[n] for n in BIG[1:]], me_idx)
    placed = [placed_in[li] + placed_rest[li] for li in range(depth)]
    sems, arrs, _ = _gather_call("gather_first_forward", arrs, sems, placed_rest[0][0], 1)
    _, arrs, token = _gather_call("gather_first_finish", arrs, sems, placed_rest[0][0], 2)
    gw = [dict() for _ in range(depth)]
    gw[0]["w_in"] = arrs[0]
    early = ("w_in", "w_out")
    late = tuple(n for n in BIG if n not in early)
    groups = [dict(li=0, names=BIG[1:], forward=(0, "mixed"), finish=(0, "mixed"))]
    for li in range(1, depth):
        groups.append(dict(li=li, names=early, forward=(li - 1, "ffn"), finish=(li - 1, "end")))
        groups.append(dict(li=li, names=late, forward=(li, "mixed"), finish=(li, "mixed")))
    def arrive(li, stage, after):
        for k, g in enumerate(groups):
            if g["forward"] == (li, stage):
                g["sems"], g["arrs"], _ = _gather_call("gather_forward_%d" % k, g["arrs"], g["sems"], after, 1)
            if g["finish"] == (li, stage):
                _, g["arrs"], _ = _gather_call("gather_finish_%d" % k, g["arrs"], g["sems"], after, 2)
                gw[g["li"]].update(zip(g["names"], g["arrs"]))

    sgath = sgath8[0::2].reshape(4, -1)
    off = 0
    for n in SMALL_SHARDED:
        shp = weights[n].shape
        part = sgath[:, off:off + weights[n].size].reshape((4,) + shp)
        small[n] = jnp.moveaxis(part, 0, -2).reshape(shp[:-1] + (4 * shp[-1],))
        off += weights[n].size
    for k, g in enumerate(groups):
        arrs = [placed[g["li"]][BIG.index(n)] for n in g["names"]]
        g["sems"], g["arrs"], token = _gather_call("gather_start_%d" % k, arrs, [], token, 0)

    small["norm1_g"] = small["norm1_g"] + token[0, 0]

    pending = []
    last_token = [None]

    def advance(g, after):
        if g["stage"] == 0:
            gs, others = _copies_wait("swap_wait_" + g["tag"], "swap", *g["handle"], after)
            g["handle"] = _copies_start("exchange_start_" + g["tag"], "exchange", _add_my_halves(gs, others, c_idx))
            g["stage"] = 1
            return g["handle"][3]
        return None

    held = {}

    def produced(li, stage, grads, after):
        token = None
        for g in pending:
            token = advance(g, after) if g["stage"] == 0 else token
        if li > 0 and stage != "end":
            held.update(grads)
            grads = {}
        elif li > 0:
            grads = {**held, **grads}
            held.clear()
        if grads:
            names = [n for n in BIG if n in grads]
            handle = _copies_start("swap_start_%d%s" % (li, stage), "swap", [grads[n] for n in names], token)
            pending.append(dict(li=li, names=names, tag="%d%s" % (li, stage), stage=0, handle=handle[:3]))
            token = handle[3]
        last_token[0] = last_token[0] if token is None else token
        return token

    loss_local, dx, _, small_grads = _local_step(x[0], p[:, 0], loss_target[0], gw, small, produced, arrive)
    rnames = REPLICATED + SMALL_SHARDED
    rflat = _pad_rows(jnp.concatenate([small_grads[n].reshape(-1) for n in rnames] + [loss_local.reshape(1)]), 8)
    small_handle = _copies_start("small_start", "devices", [rflat], last_token[0])
    accs, big_outs = {}, {}

    def finish(g, after):
        pairs, recvs = _copies_wait("exchange_wait_" + g["tag"], "exchange", *g["handle"][:3], after)
        summed = _sum_into(pairs, recvs, idx, g["li"], depth, [accs.get(n) for n in g["names"]])
        accs.update(zip(g["names"], _sibling_share(summed, g["li"])))
        return accs[g["names"][-1]]

    def update(names):
        for n in names:
            big_outs[n] = _adamw(weights[n], accs[n], mom_m[n], mom_v[n])
        return jnp.stack([big_outs[n][0].reshape(-1)[0] for n in names])

    done = finish(pending[0], small_handle[3])
    done = advance(pending[-1], done)
    for g in pending[1:-1]:
        done = finish(g, done)
    last = pending[-1]["names"]
    done = update([n for n in BIG if n not in last])
    finish(pending[-1], done)
    done = update(last)


    gshard = {}
    (own,), (land,) = _copies_wait("small_wait", "devices", *small_handle[:3], done)
    me_dev = (2 * chip + lax.axis_index("c").astype(jnp.int32)).reshape(1)
    rsum = _sum_devices(own, land, me_dev).reshape(-1)
    off = 0
    for n in rnames:
        whole = rsum[off:off + small_grads[n].size].reshape(small_grads[n].shape)
        off += small_grads[n].size
        if n in SMALL_SHARDED:
            cols = weights[n].shape[-1]
            whole = lax.dynamic_slice_in_dim(whole, chip * cols, cols, axis=whole.ndim - 1)
        gshard[n] = whole

    loss = rsum[off]

    deltas, new_m, new_v, grad_out = {}, {}, {}, {}
    for n in ALL_WEIGHTS:
        if n in BIG:
            deltas[n], new_m[n], new_v[n], grad_out[n] = big_outs[n]
        else:
            deltas[n], new_m[n], new_v[n], grad_out[n] = _adamw(weights[n], gshard[n], mom_m[n], mom_v[n])
    for n in TRANSPOSED:
        deltas[n], new_m[n], new_v[n], grad_out[n] = (jnp.swapaxes(a[n], 1, 2) for a in (deltas, new_m, new_v, grad_out))
    return (loss, dx[None], *[grad_out[n] for n in ALL_WEIGHTS], *[deltas[n] for n in ALL_WEIGHTS],
            *[new_m[n] for n in ALL_WEIGHTS], *[new_v[n] for n in ALL_WEIGHTS])
```

```python
import jax
import jax.numpy as jnp
from jax import lax
from jax.experimental import pallas as pl
from jax.experimental.pallas import tpu as pltpu

F32 = jnp.float32
MM_DTYPE = jnp.bfloat16
WIRE_DTYPE = jnp.bfloat16
EPS = 1e-6
HEAD_DIM = 128
CHUNK = 64
QKV_CONV_WIDTH = 4
SCONV_WIDTH = 3
POOL_GROUPS = 4
LANES = 128
SUBLANES_WIRE = 16
VMEM_LIMIT_BYTES = 56 * 1024 * 1024
ADAM_LR, ADAM_B1, ADAM_B2, ADAM_EPS, ADAM_WD, ADAM_STEP = 0.001, 0.9, 0.999, 1e-08, 0.01, 10
MESH = pl.DeviceIdType.MESH
ANY = pl.BlockSpec(memory_space=pl.ANY)
HBM = pl.BlockSpec(memory_space=pltpu.HBM)
SEM = pl.BlockSpec(memory_space=pltpu.SEMAPHORE)


def _params(*sem):
    return pltpu.CompilerParams(vmem_limit_bytes=VMEM_LIMIT_BYTES, dimension_semantics=sem if sem else None)


def _mm(a, b):
    return jnp.dot(a.astype(MM_DTYPE), b.astype(MM_DTYPE), preferred_element_type=F32)


def _mm_nt(a, b):
    return lax.dot_general(a.astype(MM_DTYPE), b.astype(MM_DTYPE), (((1,), (1,)), ((), ())), preferred_element_type=F32)


def _mm_tn(a, b):
    return lax.dot_general(a.astype(MM_DTYPE), b.astype(MM_DTYPE), (((0,), (0,)), ((), ())), preferred_element_type=F32)


def _sigmoid(x):
    return 1.0 / (1.0 + jnp.exp(-x))


def _dsilu(x, s):
    return s * (1.0 + x * (1.0 - s))


def _rows(shape):
    return lax.broadcasted_iota(jnp.int32, shape, 0)


def _shift_down(x, s):
    if s == 0:
        return x
    return jnp.where(_rows(x.shape) >= s, pltpu.roll(x, s, 0), 0.0)


def _shift_up(x, s):
    if s == 0:
        return x
    t = x.shape[0]
    return jnp.where(_rows(x.shape) < t - s, pltpu.roll(x, t - s, 0), 0.0)


def _rms_fwd(x):
    r = lax.rsqrt(jnp.mean(x * x, axis=-1, keepdims=True) + EPS)
    return x * r, r


def _rms_bwd(dxn, xn, r):
    return r * (dxn - xn * jnp.mean(dxn * xn, axis=-1, keepdims=True))


def _tile_rows(n, cap, mult=8):
    best = None
    for d in range(mult, min(n, cap) + 1, mult):
        if n % d == 0:
            best = d
    return best if best is not None else n


def _in_proj_fwd(x, g1, wp, segs, tm):
    t, d = x.shape
    npk = wp.shape[1]

    def body(x_ref, g_ref, w_ref, *o_refs):
        xn, _ = _rms_fwd(x_ref[...])
        h = (xn * g_ref[...]).astype(w_ref.dtype)
        off = 0
        for o_ref, wd in zip(o_refs, segs):
            o_ref[...] = jnp.dot(h, w_ref[:, off:off + wd], preferred_element_type=F32)
            off += wd

    return pl.pallas_call(
        body, name="in_proj_fwd", grid=(t // tm,),
        in_specs=[pl.BlockSpec((tm, d), lambda i: (i, 0)), pl.BlockSpec((1, d), lambda i: (0, 0)),
                  pl.BlockSpec((d, npk), lambda i: (0, 0))],
        out_specs=[pl.BlockSpec((tm, wd), lambda i: (i, 0)) for wd in segs],
        out_shape=[jax.ShapeDtypeStruct((t, wd), F32) for wd in segs],
        compiler_params=_params("arbitrary"))(x, g1, wp)


def _in_proj_bwd(x, g1, wp, dsegs, dx_res, segs, tm):
    t, d = x.shape
    npk = wp.shape[1]
    nseg = len(segs)

    def body(x_ref, g_ref, w_ref, *rest):
        ds_refs = rest[:nseg]
        dxr_ref, dx_ref, dw_ref, dg_ref = rest[nseg:]
        i = pl.program_id(0)

        @pl.when(i == 0)
        def _():
            dw_ref[...] = jnp.zeros_like(dw_ref)
            dg_ref[...] = jnp.zeros_like(dg_ref)

        xn, r = _rms_fwd(x_ref[...])
        g = g_ref[...]
        h = (xn * g).astype(w_ref.dtype)
        dcat = jnp.concatenate([ds_ref[...].astype(w_ref.dtype) for ds_ref in ds_refs], axis=1)
        dh = lax.dot_general(dcat, w_ref[...], (((1,), (1,)), ((), ())), preferred_element_type=F32)
        dw_ref[...] += lax.dot_general(h, dcat, (((0,), (0,)), ((), ())), preferred_element_type=F32)
        dg_ref[...] += jnp.sum(dh * xn, axis=0, keepdims=True)
        dx_ref[...] = dxr_ref[...] + _rms_bwd(dh * g, xn, r)

    return pl.pallas_call(
        body, name="in_proj_bwd", grid=(t // tm,),
        in_specs=[pl.BlockSpec((tm, d), lambda i: (i, 0)), pl.BlockSpec((1, d), lambda i: (0, 0)),
                  pl.BlockSpec((d, npk), lambda i: (0, 0))]
                 + [pl.BlockSpec((tm, wd), lambda i: (i, 0)) for wd in segs]
                 + [pl.BlockSpec((tm, d), lambda i: (i, 0))],
        out_specs=[pl.BlockSpec((tm, d), lambda i: (i, 0)), pl.BlockSpec((d, npk), lambda i: (0, 0)),
                   pl.BlockSpec((1, d), lambda i: (0, 0))],
        out_shape=[jax.ShapeDtypeStruct((t, d), F32), jax.ShapeDtypeStruct((d, npk), F32),
                   jax.ShapeDtypeStruct((1, d), F32)],
        compiler_params=_params("arbitrary"))(x, g1, wp, *dsegs, dx_res)


def _out_proj_fwd(x0, mix, wo, g2, tm):
    t, d = x0.shape
    dq = wo.shape[1]
    widths = [m.shape[1] for m in mix]

    def body(x_ref, *rest):
        m_refs = rest[:len(mix)]
        w_ref, g_ref, x1_ref, h2_ref = rest[len(mix):]
        acc = x_ref[...]
        off = 0
        for m_ref, wd in zip(m_refs, widths):
            for k in range(wd // dq):
                acc = acc + jnp.dot(m_ref[:, k * dq:(k + 1) * dq].astype(w_ref.dtype), w_ref[off // dq + k],
                                    preferred_element_type=F32)
            off += wd
        x1_ref[...] = acc
        xn, _ = _rms_fwd(acc)
        h2_ref[...] = (xn * g_ref[...]).astype(h2_ref.dtype)

    return pl.pallas_call(
        body, name="out_proj_fwd", grid=(t // tm,),
        in_specs=[pl.BlockSpec((tm, d), lambda i: (i, 0))]
                 + [pl.BlockSpec((tm, wd), lambda i: (i, 0)) for wd in widths]
                 + [pl.BlockSpec((4, dq, d), lambda i: (0, 0, 0)), pl.BlockSpec((1, d), lambda i: (0, 0))],
        out_specs=[pl.BlockSpec((tm, d), lambda i: (i, 0)), pl.BlockSpec((tm, d), lambda i: (i, 0))],
        out_shape=[jax.ShapeDtypeStruct((t, d), F32), jax.ShapeDtypeStruct((t, d), MM_DTYPE)],
        compiler_params=_params("arbitrary"))(x0, *mix, wo, g2)


def _out_proj_bwd(dx2, dh2, x1, g2, mix, wo, tm):
    t, d = x1.shape
    dq = wo.shape[1]
    widths = [m.shape[1] for m in mix]
    nm = len(mix)

    def body(dx2_ref, dh2_ref, x1_ref, g_ref, *rest):
        m_refs = rest[:nm]
        w_ref = rest[nm]
        dx1_ref = rest[nm + 1]
        dm_refs = rest[nm + 2:nm + 2 + nm]
        dw_ref, dg_ref = rest[nm + 2 + nm:]
        i = pl.program_id(0)

        @pl.when(i == 0)
        def _():
            dw_ref[...] = jnp.zeros_like(dw_ref)
            dg_ref[...] = jnp.zeros_like(dg_ref)

        xn, r = _rms_fwd(x1_ref[...])
        dh2v = dh2_ref[...]
        dg_ref[...] += jnp.sum(dh2v * xn, axis=0, keepdims=True)
        dx1 = dx2_ref[...] + _rms_bwd(dh2v * g_ref[...], xn, r)
        dx1_ref[...] = dx1
        dx1c = dx1.astype(w_ref.dtype)
        off = 0
        for m_ref, dm_ref, wd in zip(m_refs, dm_refs, widths):
            for k in range(wd // dq):
                j = off // dq + k
                cols = slice(k * dq, (k + 1) * dq)
                dm_ref[:, cols] = lax.dot_general(dx1c, w_ref[j], (((1,), (1,)), ((), ())), preferred_element_type=F32)
                dw_ref[j] += lax.dot_general(m_ref[:, cols].astype(w_ref.dtype), dx1c, (((0,), (0,)), ((), ())),
                                             preferred_element_type=F32)
            off += wd

    tile = lambda wd: pl.BlockSpec((tm, wd), lambda i: (i, 0))
    return pl.pallas_call(
        body, name="out_proj_bwd", grid=(t // tm,),
        in_specs=[tile(d), tile(d), tile(d), pl.BlockSpec((1, d), lambda i: (0, 0))]
                 + [tile(wd) for wd in widths] + [pl.BlockSpec((4, dq, d), lambda i: (0, 0, 0))],
        out_specs=[tile(d)] + [tile(wd) for wd in widths]
                  + [pl.BlockSpec((4, dq, d), lambda i: (0, 0, 0)), pl.BlockSpec((1, d), lambda i: (0, 0))],
        out_shape=[jax.ShapeDtypeStruct((t, d), F32)] + [jax.ShapeDtypeStruct((t, wd), F32) for wd in widths]
                  + [jax.ShapeDtypeStruct((4, dq, d), F32), jax.ShapeDtypeStruct((1, d), F32)],
        compiler_params=_params("arbitrary"))(dx2, dh2, x1, g2, *mix, wo)


def _ffn_fwd(x1, h2, wg, wu, wd, tm):
    t, d = x1.shape
    fs = wg.shape[1]

    def body(x1_ref, h2_ref, wg_ref, wu_ref, wd_ref, x2_ref, gp_ref, up_ref):
        @pl.when(pl.program_id(1) == 0)
        def _():
            x2_ref[...] = x1_ref[...]

        h = h2_ref[...]
        nt = (((1,), (1,)), ((), ()))
        gp = lax.dot_general(h, wg_ref[...], nt, preferred_element_type=F32)
        up = lax.dot_general(h, wu_ref[...], nt, preferred_element_type=F32)
        gp_ref[...] = gp
        up_ref[...] = up
        ff = gp * _sigmoid(gp) * up
        x2_ref[...] += jnp.dot(ff.astype(wd_ref.dtype), wd_ref[...], preferred_element_type=F32)

    return pl.pallas_call(
        body, name="ffn_fwd", grid=(t // tm, 4),
        in_specs=[pl.BlockSpec((tm, d), lambda i, j: (i, 0)), pl.BlockSpec((tm, d), lambda i, j: (i, 0)),
                  pl.BlockSpec((None, fs, d), lambda i, j: (j, 0, 0)),
                  pl.BlockSpec((None, fs, d), lambda i, j: (j, 0, 0)),
                  pl.BlockSpec((None, fs, d), lambda i, j: (j, 0, 0))],
        out_specs=[pl.BlockSpec((tm, d), lambda i, j: (i, 0)), pl.BlockSpec((None, tm, fs), lambda i, j: (j, i, 0)),
                   pl.BlockSpec((None, tm, fs), lambda i, j: (j, i, 0))],
        out_shape=[jax.ShapeDtypeStruct((t, d), F32), jax.ShapeDtypeStruct((4, t, fs), F32),
                   jax.ShapeDtypeStruct((4, t, fs), F32)],
        compiler_params=_params("arbitrary", "arbitrary"))(x1, h2, wg, wu, wd)


def _ffn_bwd(dx2, h2, gp, up, wg, wu, wd, tm):
    t, d = dx2.shape
    fs = wg.shape[1]

    def body(dx2_ref, h2_ref, gp_ref, up_ref, wg_ref, wu_ref, wd_ref, dh2_ref, dwg_ref, dwu_ref, dwd_ref):
        j, i = pl.program_id(0), pl.program_id(1)

        @pl.when(i == 0)
        def _():
            dwg_ref[...] = jnp.zeros_like(dwg_ref)
            dwu_ref[...] = jnp.zeros_like(dwu_ref)
            dwd_ref[...] = jnp.zeros_like(dwd_ref)

        cdt = wg_ref.dtype
        h = h2_ref[...]
        gpv, upv = gp_ref[...], up_ref[...]
        s = _sigmoid(gpv)
        silu = gpv * s
        dx2c = dx2_ref[...].astype(cdt)
        dff = lax.dot_general(dx2c, wd_ref[...], (((1,), (1,)), ((), ())), preferred_element_type=F32)
        dwd_ref[...] += lax.dot_general((silu * upv).astype(cdt), dx2c, (((0,), (0,)), ((), ())), preferred_element_type=F32)
        dup = (dff * silu).astype(cdt)
        dgp = (dff * upv * _dsilu(gpv, s)).astype(cdt)
        dwg_ref[...] += lax.dot_general(dgp, h, (((0,), (0,)), ((), ())), preferred_element_type=F32)
        dwu_ref[...] += lax.dot_general(dup, h, (((0,), (0,)), ((), ())), preferred_element_type=F32)
        dh = (jnp.dot(dgp, wg_ref[...], preferred_element_type=F32) + jnp.dot(dup, wu_ref[...], preferred_element_type=F32))
        rows = pl.ds(pl.multiple_of(i * tm, tm), tm)

        @pl.when(j == 0)
        def _():
            dh2_ref[rows, :] = dh

        @pl.when(j != 0)
        def _():
            dh2_ref[rows, :] += dh

    return pl.pallas_call(
        body, name="ffn_bwd", grid=(4, t // tm),
        in_specs=[pl.BlockSpec((tm, d), lambda j, i: (i, 0)), pl.BlockSpec((tm, d), lambda j, i: (i, 0)),
                  pl.BlockSpec((None, tm, fs), lambda j, i: (j, i, 0)), pl.BlockSpec((None, tm, fs), lambda j, i: (j, i, 0)),
                  pl.BlockSpec((None, fs, d), lambda j, i: (j, 0, 0)),
                  pl.BlockSpec((None, fs, d), lambda j, i: (j, 0, 0)),
                  pl.BlockSpec((None, fs, d), lambda j, i: (j, 0, 0))],
        out_specs=[pl.BlockSpec((t, d), lambda j, i: (0, 0)), pl.BlockSpec((None, fs, d), lambda j, i: (j, 0, 0)),
                   pl.BlockSpec((None, fs, d), lambda j, i: (j, 0, 0)), pl.BlockSpec((None, fs, d), lambda j, i: (j, 0, 0))],
        out_shape=[jax.ShapeDtypeStruct((t, d), F32)] + [jax.ShapeDtypeStruct((4, fs, d), F32)] * 3,
        compiler_params=_params("arbitrary", "arbitrary"))(dx2, h2, gp, up, wg, wu, wd)


def _ple_fwd(x2, p, wpg, wpp, tm):
    t, d = x2.shape
    q = p.shape[1]
    dq = d // 4

    def body(x_ref, p_ref, wg_ref, wp_ref, o_ref):
        xv = x_ref[...]
        xc = xv.astype(wg_ref.dtype)
        pc = p_ref[...].astype(wp_ref.dtype)
        pre = jnp.dot(xc[:, :dq], wg_ref[0], preferred_element_type=F32)
        for j in range(1, 4):
            pre = pre + jnp.dot(xc[:, j * dq:(j + 1) * dq], wg_ref[j], preferred_element_type=F32)
        gate = _sigmoid(pre)
        for j in range(4):
            cols = slice(j * dq, (j + 1) * dq)
            o_ref[:, cols] = xv[:, cols] + gate[:, cols] * jnp.dot(pc, wp_ref[j], preferred_element_type=F32)

    return pl.pallas_call(
        body, name="ple_fwd", grid=(t // tm,),
        in_specs=[pl.BlockSpec((tm, d), lambda i: (i, 0)), pl.BlockSpec((tm, q), lambda i: (i, 0)),
                  pl.BlockSpec((4, dq, d), lambda i: (0, 0, 0)),
                  pl.BlockSpec((4, q, dq), lambda i: (0, 0, 0))],
        out_specs=pl.BlockSpec((tm, d), lambda i: (i, 0)),
        out_shape=jax.ShapeDtypeStruct((t, d), F32),
        compiler_params=_params("arbitrary"))(x2, p, wpg, wpp)


def _ple_bwd(dx3, x2, p, wpg, wpp, tm):
    t, d = x2.shape
    q = p.shape[1]
    dq = d // 4

    def body(dx3_ref, x_ref, p_ref, wg_ref, wp_ref, dx2_ref, dwg_ref, dwp_ref):
        @pl.when(pl.program_id(0) == 0)
        def _():
            dwg_ref[...] = jnp.zeros_like(dwg_ref)
            dwp_ref[...] = jnp.zeros_like(dwp_ref)

        cdt = wg_ref.dtype
        xc = x_ref[...].astype(cdt)
        pc = p_ref[...].astype(cdt)
        pre = jnp.dot(xc[:, :dq], wg_ref[0], preferred_element_type=F32)
        for j in range(1, 4):
            pre = pre + jnp.dot(xc[:, j * dq:(j + 1) * dq], wg_ref[j], preferred_element_type=F32)
        gate = _sigmoid(pre)
        dx3v = dx3_ref[...]
        dpp = (dx3v * gate).astype(cdt)
        dgate = dx3v * gate * (1.0 - gate)
        dpre_parts = []
        for j in range(4):
            cols = slice(j * dq, (j + 1) * dq)
            pp_j = jnp.dot(pc, wp_ref[j], preferred_element_type=F32)
            dpre_parts.append((dgate[:, cols] * pp_j).astype(cdt))
            dwp_ref[j] += lax.dot_general(pc, dpp[:, cols], (((0,), (0,)), ((), ())), preferred_element_type=F32)
        dpre = jnp.concatenate(dpre_parts, axis=1)
        for j in range(4):
            cols = slice(j * dq, (j + 1) * dq)
            dwg_ref[j] += lax.dot_general(xc[:, cols], dpre, (((0,), (0,)), ((), ())), preferred_element_type=F32)
            dx2_ref[:, cols] = dx3v[:, cols] + lax.dot_general(dpre, wg_ref[j], (((1,), (1,)), ((), ())),
                                                               preferred_element_type=F32)

    return pl.pallas_call(
        body, name="ple_bwd", grid=(t // tm,),
        in_specs=[pl.BlockSpec((tm, d), lambda i: (i, 0)), pl.BlockSpec((tm, d), lambda i: (i, 0)),
                  pl.BlockSpec((tm, q), lambda i: (i, 0)), pl.BlockSpec((4, dq, d), lambda i: (0, 0, 0)),
                  pl.BlockSpec((4, q, dq), lambda i: (0, 0, 0))],
        out_specs=[pl.BlockSpec((tm, d), lambda i: (i, 0)), pl.BlockSpec((4, dq, d), lambda i: (0, 0, 0)),
                   pl.BlockSpec((4, q, dq), lambda i: (0, 0, 0))],
        out_shape=[jax.ShapeDtypeStruct((t, d), F32), jax.ShapeDtypeStruct((4, dq, d), F32),
                   jax.ShapeDtypeStruct((4, q, dq), F32)],
        compiler_params=_params("arbitrary"))(dx3, x2, p, wpg, wpp)


def _loss_head(x, target, fg, tm):
    t, d = x.shape

    def body(x_ref, t_ref, g_ref, dx_ref, loss_ref, dg_ref):
        @pl.when(pl.program_id(0) == 0)
        def _():
            loss_ref[...] = jnp.zeros_like(loss_ref)
            dg_ref[...] = jnp.zeros_like(dg_ref)

        xn, r = _rms_fwd(x_ref[...])
        g = g_ref[...]
        err = xn * g - t_ref[...]
        loss_ref[...] += 0.5 * jnp.sum(jnp.sum(err * err, axis=-1, keepdims=True) / d, axis=0, keepdims=True)
        dy = err / d
        dg_ref[...] += jnp.sum(dy * xn, axis=0, keepdims=True)
        dx_ref[...] = _rms_bwd(dy * g, xn, r)

    return pl.pallas_call(
        body, name="loss_head", grid=(t // tm,),
        in_specs=[pl.BlockSpec((tm, d), lambda i: (i, 0)), pl.BlockSpec((tm, d), lambda i: (i, 0)),
                  pl.BlockSpec((1, d), lambda i: (0, 0))],
        out_specs=[pl.BlockSpec((tm, d), lambda i: (i, 0)), pl.BlockSpec((1, 1), lambda i: (0, 0)),
                   pl.BlockSpec((1, d), lambda i: (0, 0))],
        out_shape=[jax.ShapeDtypeStruct((t, d), F32), jax.ShapeDtypeStruct((1, 1), F32),
                   jax.ShapeDtypeStruct((1, d), F32)],
        compiler_params=_params("arbitrary"))(x, target, fg)


def _qkv_conv_act(xv, w, j, heads):
    k = QKV_CONV_WIDTH
    y = w[k - 1:k] * xv
    for s in range(1, k):
        y = y + w[k - 1 - s:k - s] * _shift_down(xv, s)
    sg = _sigmoid(y)
    s_act = y * sg
    nrm = lax.rsqrt(jnp.sum(s_act * s_act, axis=-1, keepdims=True) + EPS)
    scale = jnp.where(j < heads, HEAD_DIM ** -0.5, 1.0).astype(F32)
    return y, sg, s_act, nrm, scale


def _qkv_conv_fwd(qkv_pre, conv_w, heads):
    t = qkv_pre.shape[0]
    nblk = 3 * heads

    def body(x_ref, w_ref, o_ref):
        j = pl.program_id(0)
        _, _, s_act, nrm, scale = _qkv_conv_act(x_ref[...], w_ref[...], j, heads)
        o_ref[...] = jnp.where(j < 2 * heads, s_act * (nrm * scale), s_act)

    return pl.pallas_call(
        body, name="qkv_conv_fwd", grid=(nblk,),
        in_specs=[pl.BlockSpec((t, LANES), lambda j: (0, j)), pl.BlockSpec((QKV_CONV_WIDTH, LANES), lambda j: (0, j))],
        out_specs=pl.BlockSpec((t, LANES), lambda j: (0, j)),
        out_shape=jax.ShapeDtypeStruct(qkv_pre.shape, F32),
        compiler_params=_params("arbitrary"))(qkv_pre, conv_w)


def _qkv_conv_bwd(qkv_pre, conv_w, dqkv, heads):
    t = qkv_pre.shape[0]
    nblk = 3 * heads
    k = QKV_CONV_WIDTH

    def body(x_ref, w_ref, dn_ref, dx_ref, dw_ref):
        j = pl.program_id(0)
        xv, w = x_ref[...], w_ref[...]
        y, sg, s_act, nrm, scale = _qkv_conv_act(xv, w, j, heads)
        dn = dn_ref[...]
        dsn = dn * scale
        ds_qk = nrm * dsn - s_act * (nrm * nrm * nrm) * jnp.sum(dsn * s_act, axis=-1, keepdims=True)
        ds = jnp.where(j < 2 * heads, ds_qk, dn)
        dy = ds * _dsilu(y, sg)
        dx = w[k - 1:k] * dy
        dw_ref[k - 1:k, :] = jnp.sum(dy * xv, axis=0, keepdims=True)
        for s in range(1, k):
            dx = dx + w[k - 1 - s:k - s] * _shift_up(dy, s)
            dw_ref[k - 1 - s:k - s, :] = jnp.sum(dy * _shift_down(xv, s), axis=0, keepdims=True)
        dx_ref[...] = dx

    return pl.pallas_call(
        body, name="qkv_conv_bwd", grid=(nblk,),
        in_specs=[pl.BlockSpec((t, LANES), lambda j: (0, j)), pl.BlockSpec((k, LANES), lambda j: (0, j)),
                  pl.BlockSpec((t, LANES), lambda j: (0, j))],
        out_specs=[pl.BlockSpec((t, LANES), lambda j: (0, j)), pl.BlockSpec((k, LANES), lambda j: (0, j))],
        out_shape=[jax.ShapeDtypeStruct(qkv_pre.shape, F32), jax.ShapeDtypeStruct(conv_w.shape, F32)],
        compiler_params=_params("arbitrary"))(qkv_pre, conv_w, dqkv)


def _pool_windows(shape, j, group_dim):
    lane = lax.broadcasted_iota(jnp.int32, shape, 1) + j * LANES
    grp = lane // group_dim
    win = jnp.left_shift(2, grp).astype(F32)
    cnt = jnp.minimum((_rows(shape) + 1).astype(F32), win)
    return grp, cnt


def _pool_select(grp, levels):
    out = levels[0]
    for gi in range(1, POOL_GROUPS):
        out = jnp.where(grp == gi, levels[gi], out)
    return out


def _pool_mean(hv, grp, cnt):
    acc, levels, width = hv, [], 1
    for _ in range(POOL_GROUPS):
        acc = acc + _shift_down(acc, width)
        width *= 2
        levels.append(acc)
    return _pool_select(grp, levels) / cnt - hv


def _pool_fwd(hp, wbd, scale, group_dim):
    t, dp = hp.shape

    def body(h_ref, w_ref, s_ref, o_ref):
        hv = h_ref[...]
        grp, cnt = _pool_windows(hv.shape, pl.program_id(0), group_dim)
        pooled = _pool_mean(hv, grp, cnt)
        o_ref[...] = _mm(pooled, w_ref[...]) * s_ref[...]

    return pl.pallas_call(
        body, name="pool_fwd", grid=(dp // LANES,),
        in_specs=[pl.BlockSpec((t, LANES), lambda j: (0, j)), pl.BlockSpec((LANES, LANES), lambda j: (j, j)),
                  pl.BlockSpec((1, LANES), lambda j: (0, j))],
        out_specs=pl.BlockSpec((t, LANES), lambda j: (0, j)),
        out_shape=jax.ShapeDtypeStruct(hp.shape, F32),
        compiler_params=_params("arbitrary"))(hp, wbd, scale)


def _pool_bwd(hp, wbd, scale, dob, group_dim):
    t, dp = hp.shape

    def body(h_ref, w_ref, s_ref, do_ref, dh_ref, dw_ref, ds_ref):
        hv = h_ref[...]
        grp, cnt = _pool_windows(hv.shape, pl.program_id(0), group_dim)
        pooled = _pool_mean(hv, grp, cnt)
        wv = w_ref[...]
        dov = do_ref[...]
        ds_ref[...] = jnp.sum(dov * _mm(pooled, wv), axis=0, keepdims=True)
        dys = dov * s_ref[...]
        dw_ref[0] = _mm_tn(pooled, dys)
        dpooled = _mm_nt(dys, wv)
        acc, levels, width = dpooled / cnt, [], 1
        for _ in range(POOL_GROUPS):
            acc = acc + _shift_up(acc, width)
            width *= 2
            levels.append(acc)
        dh_ref[...] = _pool_select(grp, levels) - dpooled

    nb = dp // LANES
    return pl.pallas_call(
        body, name="pool_bwd", grid=(nb,),
        in_specs=[pl.BlockSpec((t, LANES), lambda j: (0, j)), pl.BlockSpec((LANES, LANES), lambda j: (j, j)),
                  pl.BlockSpec((1, LANES), lambda j: (0, j)), pl.BlockSpec((t, LANES), lambda j: (0, j))],
        out_specs=[pl.BlockSpec((t, LANES), lambda j: (0, j)), pl.BlockSpec((1, LANES, LANES), lambda j: (j, 0, 0)),
                   pl.BlockSpec((1, LANES), lambda j: (0, j))],
        out_shape=[jax.ShapeDtypeStruct(hp.shape, F32), jax.ShapeDtypeStruct((nb, LANES, LANES), F32),
                   jax.ShapeDtypeStruct((1, dp), F32)],
        compiler_params=_params("arbitrary"))(hp, wbd, scale, dob)


def _sconv_fwd(cbcch, w):
    t, dc3 = cbcch.shape
    nb = dc3 // 3 // LANES
    k = SCONV_WIDTH

    def body(b_ref, c_ref, h_ref, w_ref, o_ref):
        m = c_ref[...] * h_ref[...]
        wv = w_ref[...]
        y = wv[k - 1:k] * m
        for s in range(1, k):
            y = y + wv[k - 1 - s:k - s] * _shift_down(m, s)
        o_ref[...] = b_ref[...] * y

    return pl.pallas_call(
        body, name="sconv_fwd", grid=(nb,),
        in_specs=[pl.BlockSpec((t, LANES), lambda j: (0, j)), pl.BlockSpec((t, LANES), lambda j: (0, nb + j)),
                  pl.BlockSpec((t, LANES), lambda j: (0, 2 * nb + j)), pl.BlockSpec((k, LANES), lambda j: (0, j))],
        out_specs=pl.BlockSpec((t, LANES), lambda j: (0, j)),
        out_shape=jax.ShapeDtypeStruct((t, dc3 // 3), F32),
        compiler_params=_params("arbitrary"))(cbcch, cbcch, cbcch, w)


def _sconv_bwd(cbcch, w, doc):
    t, dc3 = cbcch.shape
    nb = dc3 // 3 // LANES
    k = SCONV_WIDTH

    def body(b_ref, c_ref, h_ref, w_ref, do_ref, db_ref, dc_ref, dh_ref, dw_ref):
        cv, hv = c_ref[...], h_ref[...]
        m = cv * hv
        wv = w_ref[...]
        dov = do_ref[...]
        dy = dov * b_ref[...]
        y = wv[k - 1:k] * m
        dm = wv[k - 1:k] * dy
        dw_ref[k - 1:k, :] = jnp.sum(dy * m, axis=0, keepdims=True)
        for s in range(1, k):
            ms = _shift_down(m, s)
            y = y + wv[k - 1 - s:k - s] * ms
            dm = dm + wv[k - 1 - s:k - s] * _shift_up(dy, s)
            dw_ref[k - 1 - s:k - s, :] = jnp.sum(dy * ms, axis=0, keepdims=True)
        db_ref[...] = dov * y
        dc_ref[...] = dm * hv
        dh_ref[...] = dm * cv

    col = lambda o: pl.BlockSpec((t, LANES), lambda j: (0, o * nb + j))
    return pl.pallas_call(
        body, name="sconv_bwd", grid=(nb,),
        in_specs=[col(0), col(1), col(2), pl.BlockSpec((k, LANES), lambda j: (0, j)), col(0)],
        out_specs=[col(0), col(0), col(0), pl.BlockSpec((k, LANES), lambda j: (0, j))],
        out_shape=[jax.ShapeDtypeStruct((t, dc3 // 3), F32)] * 3 + [jax.ShapeDtypeStruct(w.shape, F32)],
        compiler_params=_params("arbitrary"))(cbcch, cbcch, cbcch, w, doc)


class _Split:
    def __init__(self, a, exact=False):
        self.hi = a.astype(jnp.bfloat16)
        self.lo = None if exact else (a - self.hi.astype(F32)).astype(jnp.bfloat16)


def _per_head(dims, a, b):
    a = a if isinstance(a, _Split) else _Split(a)
    b = b if isinstance(b, _Split) else _Split(b)

    def dot(x, y):
        return lax.dot_general(x, y, (dims, ((), ())), preferred_element_type=F32)

    def head(h):
        out = dot(a.hi[h], b.hi[h])
        for x, y in ((a.hi, b.lo), (a.lo, b.hi)):
            out = out if x is None or y is None else out + dot(x[h], y[h])
        return out

    return jnp.stack([head(h) for h in range(a.hi.shape[0])])


def _bmm(a, b):
    return _per_head(((1,), (0,)), a, b)


def _bmm_nt(a, b):
    return _per_head(((1,), (1,)), a, b)


def _bmm_tn(a, b):
    return _per_head(((0,), (0,)), a, b)


def _inv_unit_lower(low):
    c = low.shape[-1]
    eye = (_rows((c, c)) == lax.broadcasted_iota(jnp.int32, (c, c), 1)).astype(F32)
    pw = -low
    inv = eye + pw
    span = 2
    while span < c:
        pws = _Split(pw)
        pw = _bmm(pws, pws)
        inv = inv + _bmm(inv, pw)
        span *= 2
    return inv


def _heads_of(ref, base, heads):
    return jnp.stack([ref[:, base + h * HEAD_DIM:base + (h + 1) * HEAD_DIM] for h in range(heads)])


def _chunk_common(q, k, v, a_col, b_col, alog, dtb, kept=None):
    hn, c, _ = q.shape
    beta = _sigmoid(b_col)
    xg = a_col + dtb
    softplus = jnp.maximum(xg, 0.0) + jnp.log(1.0 + jnp.exp(-jnp.abs(xg)))
    neg_ea = -jnp.exp(alog)
    g = neg_ea * softplus
    ri = _rows((c, c))
    ci = lax.broadcasted_iota(jnp.int32, (c, c), 1)
    incl, strict = ri >= ci, ri > ci
    inclf = _Split(jnp.broadcast_to(incl.astype(F32), (hn, c, c)), exact=True)
    gcb = _bmm(inclf, jnp.broadcast_to(g, (hn, c, HEAD_DIM)))
    gc_row = jnp.sum(jnp.where(ri <= ci, jnp.broadcast_to(g, (hn, c, c)), 0.0), axis=1, keepdims=True)
    dmat = jnp.where(incl, jnp.exp(jnp.where(incl, gcb[:, :, :1] - gc_row, 0.0)), 0.0)
    eg = jnp.exp(gcb)
    gl = gcb[:, c - 1:c, :]
    egl = jnp.exp(gl)
    edl = jnp.exp(gl - gcb)
    kb, vb = k * beta, v * beta
    kbe = kb * eg
    if kept is None:
        ks = _Split(k)
        a0 = _bmm_nt(kb, ks)
        tm = _inv_unit_lower(jnp.where(strict, a0 * dmat, 0.0))
        p0 = _bmm_nt(q, ks)
        tms = _Split(tm)
        u, w = _bmm(tms, vb), _bmm(tms, kbe)
    else:
        (a0, tm, p0, w), u = kept, None
    return dict(beta=beta, xg=xg, neg_ea=neg_ea, g=g, incl=incl, strict=strict, inclf=inclf, dmat=dmat, eg=eg,
                egl=egl, edl=edl, kb=kb, vb=vb, a0=a0, tm=tm, kbe=kbe, u=u, w=w, p0=p0,
                attn=p0 * dmat, qe=q * eg, kd=k * edl)


def _chunk_step(cm, state):
    ss = _Split(state)
    vn = cm["u"] - _bmm(cm["w"], ss)
    vns = _Split(vn)
    o = _bmm(cm["qe"], ss) + _bmm(cm["attn"], vns)
    new_state = state * cm["egl"][:, :, :1] + _bmm_tn(cm["kd"], vns)
    return vn, o, new_state


def _gated_norm(o, zv, og):
    xo, ro = _rms_fwd(o)
    sgz = _sigmoid(zv)
    return xo, ro, sgz, xo * og * (zv * sgz)


def _gate_columns(abv, gpv, heads):
    a_col = jnp.stack([abv[:, h:h + 1] for h in range(heads)])
    b_col = jnp.stack([abv[:, heads + h:heads + h + 1] for h in range(heads)])
    alog = jnp.stack([gpv[0:1, h:h + 1] for h in range(heads)])
    dtb = jnp.stack([gpv[1:2, h:h + 1] for h in range(heads)])
    return a_col, b_col, alog, dtb


def _delta_fwd(qkv, z, ab, gpar, heads):
    t = qkv.shape[0]
    da = heads * HEAD_DIM
    n = t // CHUNK

    def body(qkv_ref, z_ref, ab_ref, gp_ref, oa_ref, st_ref, kc_ref, kw_ref, s_ref):
        @pl.when(pl.program_id(0) == 0)
        def _():
            s_ref[...] = jnp.zeros_like(s_ref)

        gpv = gp_ref[...]
        cm = _chunk_common(_heads_of(qkv_ref, 0, heads), _heads_of(qkv_ref, da, heads), _heads_of(qkv_ref, 2 * da, heads),
                           *_gate_columns(ab_ref[...], gpv, heads))
        state = s_ref[...]
        st_ref[0] = state
        vn, o, new_state = _chunk_step(cm, state)
        s_ref[...] = new_state
        for slot, val in enumerate((cm["a0"], cm["tm"], cm["p0"])):
            kc_ref[0, slot] = val
        for slot, val in enumerate((cm["w"], vn, o)):
            kw_ref[0, slot] = val
        oa = _gated_norm(o, _heads_of(z_ref, 0, heads), gpv[2:3, :])[3]
        for h in range(heads):
            oa_ref[:, h * HEAD_DIM:(h + 1) * HEAD_DIM] = oa[h]

    return pl.pallas_call(
        body, name="delta_fwd", grid=(n,),
        in_specs=[pl.BlockSpec((CHUNK, 3 * da), lambda i: (i, 0)), pl.BlockSpec((CHUNK, da), lambda i: (i, 0)),
                  pl.BlockSpec((CHUNK, LANES), lambda i: (i, 0)), pl.BlockSpec((8, LANES), lambda i: (0, 0))],
        out_specs=[pl.BlockSpec((CHUNK, da), lambda i: (i, 0)),
                   pl.BlockSpec((1, heads, HEAD_DIM, HEAD_DIM), lambda i: (i, 0, 0, 0)),
                   pl.BlockSpec((1, 3, heads, CHUNK, CHUNK), lambda i: (i, 0, 0, 0, 0)),
                   pl.BlockSpec((1, 3, heads, CHUNK, HEAD_DIM), lambda i: (i, 0, 0, 0, 0))],
        out_shape=[jax.ShapeDtypeStruct((t, da), F32), jax.ShapeDtypeStruct((n, heads, HEAD_DIM, HEAD_DIM), F32),
                   jax.ShapeDtypeStruct((n, 3, heads, CHUNK, CHUNK), F32),
                   jax.ShapeDtypeStruct((n, 3, heads, CHUNK, HEAD_DIM), F32)],
        scratch_shapes=[pltpu.VMEM((heads, HEAD_DIM, HEAD_DIM), F32)],
        compiler_params=_params("arbitrary"))(qkv, z, ab, gpar)


def _delta_bwd(qkv, z, ab, gpar, states, kept_c, kept_w, doa, heads):
    t = qkv.shape[0]
    da = heads * HEAD_DIM
    n = t // CHUNK
    c = CHUNK

    def body(qkv_ref, z_ref, ab_ref, gp_ref, st_ref, kc_ref, kw_ref, doa_ref, dqkv_ref, dz_ref, dab_ref, dpar_ref, ds_ref):
        @pl.when(pl.program_id(0) == 0)
        def _():
            ds_ref[...] = jnp.zeros_like(ds_ref)
            dpar_ref[...] = jnp.zeros_like(dpar_ref)

        gpv = gp_ref[...]
        og = gpv[2:3, :]
        q, k, v = _heads_of(qkv_ref, 0, heads), _heads_of(qkv_ref, da, heads), _heads_of(qkv_ref, 2 * da, heads)
        cm = _chunk_common(q, k, v, *_gate_columns(ab_ref[...], gpv, heads),
                           kept=(kc_ref[0, 0], kc_ref[0, 1], kc_ref[0, 2], kw_ref[0, 0]))
        state = st_ref[0]
        dsp = ds_ref[...]
        vn, o = kw_ref[0, 1], kw_ref[0, 2]
        zv = _heads_of(z_ref, 0, heads)
        xo, ro, sgz, _ = _gated_norm(o, zv, og)
        doav = _heads_of(doa_ref, 0, heads)
        don = doav * (zv * sgz)
        dz = doav * (xo * og) * _dsilu(zv, sgz)
        d_og = jnp.sum(jnp.sum(don * xo, axis=1, keepdims=True), axis=0)
        do = _rms_bwd(don * og, xo, ro)
        tm, dmat, eg, edl, egl = cm["tm"], cm["dmat"], cm["eg"], cm["edl"], cm["egl"]
        dos, dsps, sts, tms, ks = _Split(do), _Split(dsp), _Split(state), _Split(tm), _Split(k)
        dvn = _bmm_tn(cm["attn"], dos) + _bmm(cm["kd"], dsps)
        dvns = _Split(dvn)
        dqe = _bmm_nt(dos, sts)
        ds_ref[...] = _bmm_tn(cm["qe"], dos) + dsp * egl[:, :, :1] - _bmm_tn(cm["w"], dvns)
        dattn = _bmm_nt(dos, vn)
        dkd = _bmm_nt(vn, dsps)
        dkd_kd = jnp.sum(dkd * cm["kd"], axis=-1, keepdims=True)
        dgl = (jnp.sum(jnp.sum(dsp * state, axis=-1, keepdims=True), axis=1, keepdims=True) * egl[:, :, :1]
               + jnp.sum(dkd_kd, axis=1, keepdims=True))
        dgc = jnp.sum(dqe * cm["qe"], axis=-1, keepdims=True) - dkd_kd
        dk = dkd * edl
        dq = dqe * eg
        dw = -_bmm_nt(dvns, sts)
        dws = _Split(dw)
        dp0 = dattn * dmat
        dd = jnp.where(cm["incl"], dattn * cm["p0"], 0.0)
        dp0s = _Split(dp0)
        dq = dq + _bmm(dp0s, ks)
        dk = dk + _bmm_tn(dp0s, q)
        dtm = _bmm_nt(dvns, cm["vb"]) + _bmm_nt(dws, cm["kbe"])
        dvb = _bmm_tn(tms, dvns)
        dkbe = _bmm_tn(tms, dws)
        dkb = dkbe * eg
        dgc = dgc + jnp.sum(dkbe * cm["kbe"], axis=-1, keepdims=True)
        dlow = jnp.where(cm["strict"], -_bmm_tn(tms, _bmm_nt(dtm, tms)), 0.0)
        dd = dd + dlow * cm["a0"]
        da0 = dlow * dmat
        da0s = _Split(da0)
        dkb = dkb + _bmm(da0s, ks)
        dk = dk + _bmm_tn(da0s, cm["kb"])
        ddd = dd * dmat
        ones = _Split(jnp.ones((heads, c, HEAD_DIM), F32), exact=True)
        dgc = dgc + jnp.sum(ddd, axis=-1, keepdims=True) - _bmm_tn(ddd, ones)[:, :, :1]
        dgc = dgc + jnp.where(_rows((c, 1)) == c - 1, dgl, 0.0)
        dg = _bmm_tn(cm["inclf"], jnp.broadcast_to(dgc, (heads, c, HEAD_DIM)))[:, :, :1]
        beta = cm["beta"]
        dk = dk + dkb * beta
        dbeta = jnp.sum(dkb * k, axis=-1, keepdims=True) + jnp.sum(dvb * v, axis=-1, keepdims=True)
        dv = dvb * beta
        db_col = dbeta * beta * (1.0 - beta)
        da_col = dg * cm["neg_ea"] * _sigmoid(cm["xg"])
        d_alog = jnp.sum(dg * cm["g"], axis=1, keepdims=True)
        d_dtb = jnp.sum(da_col, axis=1, keepdims=True)
        lane = lax.broadcasted_iota(jnp.int32, (c, LANES), 1)
        lane8 = lax.broadcasted_iota(jnp.int32, (8, LANES), 1)
        row8 = _rows((8, LANES))
        dab = jnp.zeros((c, LANES), F32)
        dpar = jnp.where(row8 == 2, d_og, 0.0)
        for h in range(heads):
            lo = h * HEAD_DIM
            dqkv_ref[:, lo:lo + HEAD_DIM] = dq[h]
            dqkv_ref[:, da + lo:da + lo + HEAD_DIM] = dk[h]
            dqkv_ref[:, 2 * da + lo:2 * da + lo + HEAD_DIM] = dv[h]
            dz_ref[:, lo:lo + HEAD_DIM] = dz[h]
            dab = dab + jnp.where(lane == h, da_col[h], 0.0) + jnp.where(lane == heads + h, db_col[h], 0.0)
            dpar = (dpar + jnp.where((row8 == 0) & (lane8 == h), d_alog[h], 0.0)
                    + jnp.where((row8 == 1) & (lane8 == h), d_dtb[h], 0.0))
        dab_ref[...] = dab
        dpar_ref[...] += dpar

    rev = lambda i: (n - 1 - i, 0)
    return pl.pallas_call(
        body, name="delta_bwd", grid=(n,),
        in_specs=[pl.BlockSpec((c, 3 * da), rev), pl.BlockSpec((c, da), rev), pl.BlockSpec((c, LANES), rev),
                  pl.BlockSpec((8, LANES), lambda i: (0, 0)),
                  pl.BlockSpec((1, heads, HEAD_DIM, HEAD_DIM), lambda i: (n - 1 - i, 0, 0, 0)),
                  pl.BlockSpec((1, 3, heads, c, c), lambda i: (n - 1 - i, 0, 0, 0, 0)),
                  pl.BlockSpec((1, 3, heads, c, HEAD_DIM), lambda i: (n - 1 - i, 0, 0, 0, 0)),
                  pl.BlockSpec((c, da), rev)],
        out_specs=[pl.BlockSpec((c, 3 * da), rev), pl.BlockSpec((c, da), rev), pl.BlockSpec((c, LANES), rev),
                   pl.BlockSpec((8, LANES), lambda i: (0, 0))],
        out_shape=[jax.ShapeDtypeStruct((t, 3 * da), F32), jax.ShapeDtypeStruct((t, da), F32),
                   jax.ShapeDtypeStruct((t, LANES), F32), jax.ShapeDtypeStruct((8, LANES), F32)],
        scratch_shapes=[pltpu.VMEM((heads, HEAD_DIM, HEAD_DIM), F32)],
        compiler_params=_params("arbitrary"))(qkv, z, ab, gpar, states, kept_c, kept_w, doa)


def _w_in_pieces(shard_cols, da, heads):
    a0, nab = 4 * da, 2 * heads
    d_in = 4 * shard_cols
    runs = [(0, a0, 0), (a0, a0 + nab, d_in - nab), (a0 + nab, d_in, a0)]
    pieces = []
    for j in range(4):
        lo, hi = j * shard_cols, (j + 1) * shard_cols
        for rlo, rhi, plo in runs:
            s, e = max(lo, rlo), min(hi, rhi)
            if s < e:
                pieces.append((j, s - lo, e - s, plo + (s - rlo)))
    return pieces, d_in - nab + LANES


def _w_in_pack(w4, da, heads):
    _, d, sc = w4.shape
    pieces, npk = _w_in_pieces(sc, da, heads)
    tr = _tile_rows(d, 256, SUBLANES_WIRE)

    def body(w_ref, o_ref):
        o_ref[:, npk - LANES:] = jnp.zeros((tr, LANES), o_ref.dtype)
        for j, lo, ln, dst in pieces:
            o_ref[:, dst:dst + ln] = w_ref[j, :, lo:lo + ln]

    return pl.pallas_call(
        body, name="w_in_pack", grid=(d // tr,),
        in_specs=[pl.BlockSpec((4, tr, sc), lambda i: (0, i, 0))],
        out_specs=pl.BlockSpec((tr, npk), lambda i: (i, 0)),
        out_shape=jax.ShapeDtypeStruct((d, npk), w4.dtype),
        compiler_params=_params("arbitrary"))(w4)


def _w_in_unpack(dwp, sc, da, heads):
    d, npk = dwp.shape
    pieces, _ = _w_in_pieces(sc, da, heads)
    tr = _tile_rows(d, 256)

    def body(g_ref, o_ref):
        for j, lo, ln, dst in pieces:
            o_ref[j, :, lo:lo + ln] = g_ref[:, dst:dst + ln]

    return pl.pallas_call(
        body, name="w_in_unpack", grid=(d // tr,),
        in_specs=[pl.BlockSpec((tr, npk), lambda i: (i, 0))],
        out_specs=pl.BlockSpec((4, tr, sc), lambda i: (0, i, 0)),
        out_shape=jax.ShapeDtypeStruct((4, d, sc), F32),
        compiler_params=_params("arbitrary"))(dwp)


def _block_diag(pool_w):
    g, gd, _ = pool_w.shape
    out = jnp.zeros((g * gd, g * gd), pool_w.dtype)
    for gi in range(g):
        out = lax.dynamic_update_slice(out, pool_w[gi], (gi * gd, gi * gd))
    return out


def _layer_dims(d):
    heads = (d // 2) // HEAD_DIM
    return heads, heads * HEAD_DIM, d // 4, d // 4


BIG = ("w_in", "w_gate", "w_up", "ple_proj", "w_out", "w_down", "ple_gate")
TRANSPOSED = ("w_gate", "w_up")


def _prepare_layer(small, li):
    d = small["norm1_g"].shape[1]
    heads, _, _, _ = _layer_dims(d)
    gpar = jnp.zeros((8, LANES), F32)
    gpar = gpar.at[0, :heads].set(small["a_log"][li]).at[1, :heads].set(small["dt_bias"][li]).at[2, :].set(small["onorm_g"][li])
    return dict(norm1_g=small["norm1_g"][li][None], conv_qkv=small["conv_qkv"][li], gpar=gpar, pool_bd=_block_diag(small["pool_w"][li]).astype(MM_DTYPE),
                pool_scale=small["pool_scale"][li][None], sconv_w=small["sconv_w"][li], norm2_g=small["norm2_g"][li][None])


def _layer_fwd(x0, p, gw, lw, tm, arrive):
    d = x0.shape[1]
    heads, da, dp, dc = _layer_dims(d)
    segs = (3 * da, da, dp, 3 * dc, LANES)
    lw["w_in_p"] = _w_in_pack(gw["w_in"], da, heads).astype(MM_DTYPE)
    qkv_pre, z, hp, cbcch, ab = _in_proj_fwd(x0, lw["norm1_g"], lw["w_in_p"], segs, tm)
    qkv = _qkv_conv_fwd(qkv_pre, lw["conv_qkv"], heads)
    oa, states, kept_c, kept_w = _delta_fwd(qkv, z, ab, lw["gpar"], heads)
    ob = _pool_fwd(hp, lw["pool_bd"], lw["pool_scale"], dp // POOL_GROUPS)
    oc = _sconv_fwd(cbcch, lw["sconv_w"])
    arrive("mixed", oa)
    x1, h2 = _out_proj_fwd(x0, (oa, ob, oc), gw["w_out"], lw["norm2_g"], tm)
    x2, gp, up = _ffn_fwd(x1, h2, gw["w_gate"], gw["w_up"], gw["w_down"], tm)
    arrive("ffn", x2)
    x3 = _ple_fwd(x2, p, gw["ple_gate"], gw["ple_proj"], tm)
    arrive("end", x3)
    saved = dict(x0=x0, qkv_pre=qkv_pre, z=z, hp=hp, cbcch=cbcch, ab=ab, qkv=qkv, states=states, kept_c=kept_c, kept_w=kept_w, oa=oa, ob=ob, oc=oc,
                 x1=x1, h2=h2, gp=gp, up=up, x2=x2)
    return x3, saved


def _layer_bwd(dx3, p, gw, lw, sv, tm, produced):
    def after_token(tok, arr):
        return arr if tok is None else arr + tok[0, 0]

    d = dx3.shape[1]
    heads, da, dp, dc = _layer_dims(d)
    segs = (3 * da, da, dp, dc, dc, dc, LANES)
    gd = dp // POOL_GROUPS
    dx2, d_ple_gate, d_ple_proj = _ple_bwd(dx3, sv["x2"], p, gw["ple_gate"], gw["ple_proj"], tm)
    dh2, d_w_gate, d_w_up, d_w_down = _ffn_bwd(dx2, sv["h2"], sv["gp"], sv["up"], gw["w_gate"], gw["w_up"], gw["w_down"],
                                               min(tm, 256))
    tok = produced("ffn", dict(w_gate=d_w_gate, w_up=d_w_up, ple_proj=d_ple_proj, w_down=d_w_down, ple_gate=d_ple_gate), dh2)
    dx1, doa, dob, doc, d_w_out, d_norm2 = _out_proj_bwd(dx2, dh2, sv["x1"], after_token(tok, lw["norm2_g"]),
                                                         (sv["oa"], sv["ob"], sv["oc"]), gw["w_out"], tm)
    dcb, dcc, dch, d_sconv = _sconv_bwd(sv["cbcch"], lw["sconv_w"], doc)
    dhp, d_pool_bd, d_pool_scale = _pool_bwd(sv["hp"], lw["pool_bd"], lw["pool_scale"], dob, gd)
    dqkv, dz, dab, dpar = _delta_bwd(sv["qkv"], sv["z"], sv["ab"], lw["gpar"], sv["states"], sv["kept_c"], sv["kept_w"], doa,
                                      heads)
    tok = produced("mixers", {}, dqkv)
    dqkv_pre, d_conv_qkv = _qkv_conv_bwd(sv["qkv_pre"], lw["conv_qkv"], dqkv, heads)
    dsegs = (dqkv_pre, dz, dhp, dcb, dcc, dch, dab)
    dx0, d_w_in_p, d_norm1 = _in_proj_bwd(sv["x0"], after_token(tok, lw["norm1_g"]), lw["w_in_p"], dsegs, dx1, segs, tm)
    per = LANES // gd
    bd = d_pool_bd.reshape(dp // LANES, per, gd, per, gd)
    d_pool_w = jnp.stack([bd[gi // per, gi % per, :, gi % per, :] for gi in range(POOL_GROUPS)])
    big = dict(w_in=_w_in_unpack(d_w_in_p, gw["w_in"].shape[2], da, heads), w_gate=d_w_gate, w_up=d_w_up,
               ple_proj=d_ple_proj, w_out=d_w_out, w_down=d_w_down, ple_gate=d_ple_gate)
    small = dict(norm1_g=d_norm1[0], conv_qkv=d_conv_qkv, a_log=dpar[0, :heads], dt_bias=dpar[1, :heads], onorm_g=dpar[2],
                 pool_w=d_pool_w, pool_scale=d_pool_scale[0], sconv_w=d_sconv, norm2_g=d_norm2[0])
    tok = produced("end", dict(w_in=big["w_in"], w_out=d_w_out), big["w_in"])
    return dx0, big, small, tok


def _local_step(x, p, target, gw, small, produced=None, arrive=None):
    t, d = x.shape
    depth = p.shape[0]
    tm = 512 if t % 512 == 0 else 128
    layers = [_prepare_layer(small, li) for li in range(depth)]
    saved = []
    h = x
    for li in range(depth):
        h, sv = _layer_fwd(h, p[li], gw[li], layers[li], tm,
                           (lambda stage, after, li=li: arrive(li, stage, after)) if arrive else (lambda stage, after: None))
        saved.append(sv)
    dx, loss, d_final = _loss_head(h, target, small["final_g"][None], tm)
    big, sm = [None] * depth, [None] * depth
    token = None
    for li in reversed(range(depth)):
        p_li = p[li] if token is None else p[li] + token[0, 0]
        dx, big[li], sm[li], token = _layer_bwd(
            dx, p_li, gw[li], layers[li], saved[li], tm,
            (lambda stage, grads, after, li=li: produced(li, stage, grads, after)) if produced else (lambda *a: None))
    small_grads = {n: jnp.stack([g[n] for g in sm]) for n in sm[0]}
    small_grads["final_g"] = d_final[0]
    return loss[0, 0], dx, big, small_grads


def _coords():
    return lax.axis_index("x"), lax.axis_index("y"), lax.axis_index("c")


def _other_chips(x, y):
    return [(1 - x, y), (x, 1 - y), (1 - x, 1 - y)]


def _place_shards(ws, me_idx):
    nt = len(ws)
    depth = ws[0].shape[0]

    def body(me_ref, *refs):
        for t, w_ref in enumerate(refs[:nt]):
            for li in range(depth):
                refs[nt + li * nt + t][...] = w_ref[li].astype(WIRE_DTYPE)

    outs = pl.pallas_call(
        body, name="place_shards",
        grid_spec=pltpu.PrefetchScalarGridSpec(
            num_scalar_prefetch=1, grid=(4,),
            in_specs=[pl.BlockSpec((depth, w.shape[1] // 4, w.shape[2]), lambda i, me_ref: (0, i, 0)) for w in ws],
            out_specs=[pl.BlockSpec((None, w.shape[1] // 4, w.shape[2]), lambda i, me_ref: (me_ref[0], i, 0))
                       for _ in range(depth) for w in ws]),
        out_shape=[jax.ShapeDtypeStruct((4,) + w.shape[1:], WIRE_DTYPE) for _ in range(depth) for w in ws],
        compiler_params=_params("arbitrary"))(me_idx, *ws)
    return [list(outs[li * nt:(li + 1) * nt]) for li in range(depth)]


def _half_block(ref, chip, pc):
    rh = ref.shape[1] // 2
    return ref.at[chip, pl.ds(pc * rh, rh)]


def _gather_copies(out_refs, send_sems, recv_sems, stage):
    nt = len(out_refs)
    x, y, c = _coords()
    pairs = []
    for j, (cx, cy) in enumerate(_other_chips(x, y)):
        for t in range(nt):
            sems = dict(send_sem=send_sems[j * nt + t], recv_sem=recv_sems[j * nt + t], device_id_type=MESH)
            if stage == 0:
                mine, theirs, to = _half_block(out_refs[t], 2 * x + y, c), _half_block(out_refs[t], 2 * cx + cy, c), (cx, cy, c)
            else:
                mine, theirs, to = (_half_block(out_refs[t], 2 * cx + cy, c), _half_block(out_refs[t], 2 * cx + cy, 1 - c),
                                    (x, y, 1 - c))
            pairs.append((pltpu.make_async_remote_copy(src_ref=mine, dst_ref=mine, device_id=to, **sems),
                          pltpu.make_async_remote_copy(src_ref=theirs, dst_ref=theirs, device_id=to, **sems)))
    return pairs


def _gather_call(name, arrs, wait_sems, after, stage):
    nt = len(arrs)
    nc = 3 * nt
    n_wait = len(wait_sems)
    n_new = 2 * nc if stage < 2 else 0
    arrs = [pltpu.with_memory_space_constraint(a, pltpu.HBM) for a in arrs]

    def body(*refs):
        a_refs = refs[:nt]
        waits = refs[nt:nt + n_wait]
        news = refs[nt + n_wait + 1:nt + n_wait + 1 + n_new]
        token = refs[-1]
        if stage > 0:
            for start, arrival in _gather_copies(a_refs, waits[:nc], waits[nc:], stage - 1):
                start.wait_send()
                arrival.wait_recv()
        if stage < 2:
            for start, _ in _gather_copies(a_refs, news[:nc], news[nc:], stage):
                start.start()
        token[...] = jnp.zeros_like(token)

    outs = pl.pallas_call(
        body, name=name,
        out_shape=(*[pltpu.SemaphoreType.DMA(())] * n_new, *[pltpu.HBM(a.shape, a.dtype) for a in arrs],
                   jax.ShapeDtypeStruct((8, LANES), F32)),
        in_specs=[HBM] * nt + [SEM] * n_wait + [ANY],
        out_specs=(*[SEM] * n_new, *[HBM] * nt, pl.BlockSpec(memory_space=pltpu.VMEM)),
        input_output_aliases={t: n_new + t for t in range(nt)},
        compiler_params=pltpu.CompilerParams(has_side_effects=pltpu.SideEffectType.DATAFLOW_SIDE_EFFECTING),
    )(*arrs, *wait_sems, after)
    return list(outs[:n_new]), list(outs[n_new:n_new + nt]), outs[-1]


def _add_my_halves(gs, others, c_idx):
    nt = len(gs)

    def body(c_ref, *refs):
        for g_ref, o_ref, out_ref in zip(refs[:nt], refs[nt:2 * nt], refs[2 * nt:]):
            out_ref[...] = (g_ref[...].astype(F32) + o_ref[...].astype(F32)).astype(out_ref.dtype)

    def half(g):
        return pl.BlockSpec((None, g.shape[1] // 2, g.shape[2]), lambda j, c_ref: (j, 0, 0))

    return pl.pallas_call(
        body, name="add_my_halves",
        grid_spec=pltpu.PrefetchScalarGridSpec(
            num_scalar_prefetch=1, grid=(4,),
            in_specs=[pl.BlockSpec((None, g.shape[1] // 2, g.shape[2]), lambda j, c_ref: (j, c_ref[0], 0)) for g in gs]
                     + [half(g) for g in gs],
            out_specs=[half(g) for g in gs]),
        out_shape=[jax.ShapeDtypeStruct((4, g.shape[1] // 2, g.shape[2]), WIRE_DTYPE) for g in gs],
        compiler_params=_params("arbitrary"))(c_idx, *gs, *others)


def _split_plan(kind, s_refs, l_refs):
    x, y, c = _coords()
    if kind == "devices":
        peers = [(x ^ ((k >> 2) & 1), y ^ ((k >> 1) & 1), c ^ (k & 1)) for k in range(1, 8)]
        return [(s, l.at[4 * x + 2 * y + c], peer) for peer in peers for s, l in zip(s_refs, l_refs)]
    if kind == "swap":
        return [(s.at[:, pl.ds((1 - c) * (s.shape[1] // 2), s.shape[1] // 2)], l, (x, y, 1 - c)) for s, l in zip(s_refs, l_refs)]
    return [(s.at[2 * cx + cy], l.at[j], (cx, cy, c)) for j, (cx, cy) in enumerate(_other_chips(x, y))
            for s, l in zip(s_refs, l_refs)]


def _split_landing(kind, a):
    if kind == "devices":
        return (8,) + a.shape
    return (a.shape[0], a.shape[1] // 2, a.shape[2]) if kind == "swap" else (3,) + a.shape[1:]


def _copies_start(name, kind, srcs, after=None):
    ns = len(srcs)
    n = {"swap": 1, "exchange": 3, "devices": 7}[kind] * ns
    srcs = [pltpu.with_memory_space_constraint(a, pltpu.HBM) for a in srcs]
    fresh = jnp.zeros if kind == "devices" else lax.empty
    lands = [pltpu.with_memory_space_constraint(fresh(_split_landing(kind, a), a.dtype), pltpu.HBM) for a in srcs]
    extra = [] if after is None else [after]

    def body(*refs):
        first_sem = 2 * ns + len(extra)
        sems, token = refs[first_sem:first_sem + 2 * n], refs[-1]
        for k, (src, dst, dev) in enumerate(_split_plan(kind, refs[:ns], refs[ns:2 * ns])):
            pltpu.make_async_remote_copy(src_ref=src, dst_ref=dst, send_sem=sems[k], recv_sem=sems[n + k], device_id=dev,
                                         device_id_type=MESH).start()
        token[...] = jnp.zeros_like(token)

    outs = pl.pallas_call(
        body, name=name,
        out_shape=(*[pltpu.SemaphoreType.DMA(())] * (2 * n), *[pltpu.HBM(a.shape, a.dtype) for a in srcs + lands],
                   jax.ShapeDtypeStruct((8, LANES), F32)),
        in_specs=[HBM] * (2 * ns) + [ANY] * len(extra),
        out_specs=(*[SEM] * (2 * n), *[HBM] * (2 * ns), pl.BlockSpec(memory_space=pltpu.VMEM)),
        input_output_aliases={t: 2 * n + t for t in range(2 * ns)},
        compiler_params=pltpu.CompilerParams(has_side_effects=pltpu.SideEffectType.DATAFLOW_SIDE_EFFECTING),
    )(*srcs, *lands, *extra)
    return list(outs[:2 * n]), list(outs[2 * n:2 * n + ns]), list(outs[2 * n + ns:2 * n + 2 * ns]), outs[-1]


def _copies_wait(name, kind, sems, srcs, lands, after):
    ns = len(srcs)
    n = len(sems) // 2

    def body(*refs):
        sem_refs = refs[2 * ns:2 * ns + 2 * n]
        for k, (src, dst, dev) in enumerate(_split_plan(kind, refs[:ns], refs[ns:2 * ns])):
            cp = pltpu.make_async_remote_copy(src_ref=src, dst_ref=dst, send_sem=sem_refs[k], recv_sem=sem_refs[n + k],
                                              device_id=dev, device_id_type=MESH)
            cp.wait_send()
            cp.wait_recv()

    outs = pl.pallas_call(
        body, name=name, out_shape=tuple(pltpu.HBM(a.shape, a.dtype) for a in srcs + lands),
        in_specs=[HBM] * (2 * ns) + [SEM] * (2 * n) + [ANY], out_specs=tuple([HBM] * (2 * ns)),
        input_output_aliases={t: t for t in range(2 * ns)},
        compiler_params=pltpu.CompilerParams(has_side_effects=pltpu.SideEffectType.DATAFLOW_SIDE_EFFECTING),
    )(*srcs, *lands, *sems, after)
    return list(outs[:ns]), list(outs[ns:])


def _sum_into(pairs, recvs, idx, li, depth, accs):
    nt = len(pairs)

    def body(idx_ref, *refs):
        for p_ref, r_ref, out_ref in zip(refs[:nt], refs[nt:2 * nt], refs[-nt:]):
            out_ref[...] = p_ref[...].astype(F32) + r_ref[0].astype(F32) + r_ref[1].astype(F32) + r_ref[2].astype(F32)

    in_specs = ([pl.BlockSpec((None, p.shape[1] // 2, p.shape[2]), lambda i, idx_ref: (idx_ref[0], i, 0)) for p in pairs]
                + [pl.BlockSpec((3, p.shape[1] // 2, p.shape[2]), lambda i, idx_ref: (0, i, 0)) for p in pairs])
    args = [idx, *pairs, *recvs]
    aliases = {}
    if accs[0] is not None:
        in_specs += [ANY] * nt
        args += list(accs)
        aliases = {1 + 2 * nt + t: t for t in range(nt)}
    return pl.pallas_call(
        body, name="sum_into",
        grid_spec=pltpu.PrefetchScalarGridSpec(
            num_scalar_prefetch=1, grid=(2,), in_specs=in_specs,
            out_specs=[pl.BlockSpec((None, p.shape[1] // 2, p.shape[2]), lambda i, idx_ref: (li, 2 * idx_ref[1] + i, 0))
                       for p in pairs]),
        out_shape=[jax.ShapeDtypeStruct((depth, 2 * p.shape[1], p.shape[2]), F32) for p in pairs],
        input_output_aliases=aliases, compiler_params=_params("arbitrary"))(*args)


def _sum_devices(own, land, me_dev):
    rows = own.shape[0]
    tr = _tile_rows(rows, 512)

    def body(me_ref, o_ref, l_ref, out_ref):
        acc = jnp.where(me_ref[0] == 0, o_ref[...], l_ref[0])
        for s in range(1, 8):
            acc = acc + jnp.where(me_ref[0] == s, o_ref[...], l_ref[s])
        out_ref[...] = acc

    return pl.pallas_call(
        body, name="sum_devices",
        grid_spec=pltpu.PrefetchScalarGridSpec(
            num_scalar_prefetch=1, grid=(rows // tr,),
            in_specs=[pl.BlockSpec((tr, LANES), lambda i, me_ref: (i, 0)), pl.BlockSpec((8, tr, LANES), lambda i, me_ref: (0, i, 0))],
            out_specs=pl.BlockSpec((tr, LANES), lambda i, me_ref: (i, 0))),
        out_shape=jax.ShapeDtypeStruct((rows, LANES), F32), compiler_params=_params("arbitrary"))(me_dev, own, land)


def _sibling_share(gs, li):
    nt = len(gs)

    def body(*refs):
        out_refs = refs[nt:2 * nt]
        send_sems, recv_sems = refs[2 * nt:]
        x, y, c = _coords()
        sends, recvs = [], []
        for t in range(nt):
            rh = out_refs[t].shape[1] // 2
            mine, theirs = out_refs[t].at[li, pl.ds(c * rh, rh)], out_refs[t].at[li, pl.ds((1 - c) * rh, rh)]
            sems = dict(send_sem=send_sems.at[t], recv_sem=recv_sems.at[t], device_id=(x, y, 1 - c), device_id_type=MESH)
            sends.append(pltpu.make_async_remote_copy(src_ref=mine, dst_ref=mine, **sems))
            recvs.append(pltpu.make_async_remote_copy(src_ref=theirs, dst_ref=theirs, **sems))
        for cp in sends:
            cp.start()
        for cp in recvs:
            cp.wait_recv()
        for cp in sends:
            cp.wait_send()

    return pl.pallas_call(
        body, name="sibling_share", out_shape=[jax.ShapeDtypeStruct(g.shape, g.dtype) for g in gs],
        in_specs=[ANY] * nt, out_specs=[ANY] * nt, input_output_aliases={t: t for t in range(nt)},
        scratch_shapes=[pltpu.SemaphoreType.DMA((nt,)), pltpu.SemaphoreType.DMA((nt,))])(*gs)


def _all_gather_devices(buf, after=None):
    extra = [] if after is None else [after]

    def body(b_ref, *rest):
        out_ref, send_sems, recv_sems, local_sem = rest[len(extra):]
        x, y, c = _coords()
        me = 4 * x + 2 * y + c
        mine = pltpu.make_async_copy(b_ref, out_ref.at[me], local_sem)
        mine.start()
        peers = []
        for k in range(1, 8):
            fx, fy, fc = (k >> 2) & 1, (k >> 1) & 1, k & 1
            peers.append((x ^ fx, y ^ fy, c ^ fc))
        sends = [pltpu.make_async_remote_copy(src_ref=b_ref, dst_ref=out_ref.at[me], send_sem=send_sems.at[k],
                                              recv_sem=recv_sems.at[k], device_id=peer, device_id_type=MESH)
                 for k, peer in enumerate(peers)]
        for cp in sends:
            cp.start()
        for k, (px, py, pc) in enumerate(peers):
            pltpu.make_async_remote_copy(src_ref=b_ref, dst_ref=out_ref.at[4 * px + 2 * py + pc], send_sem=send_sems.at[k],
                                         recv_sem=recv_sems.at[k], device_id=(px, py, pc), device_id_type=MESH).wait_recv()
        for cp in sends:
            cp.wait_send()
        mine.wait()

    return pl.pallas_call(
        body, name="all_gather_devices", out_shape=jax.ShapeDtypeStruct((8,) + buf.shape, buf.dtype),
        in_specs=[ANY] * (1 + len(extra)), out_specs=ANY,
        scratch_shapes=[pltpu.SemaphoreType.DMA((7,)), pltpu.SemaphoreType.DMA((7,)), pltpu.SemaphoreType.DMA(())])(buf, *extra)


SMALL_SHARDED = ("conv_qkv", "sconv_w")
REPLICATED = ("norm1_g", "a_log", "dt_bias", "onorm_g", "pool_w", "pool_scale", "norm2_g", "final_g")
ALL_WEIGHTS = ("norm1_g", "w_in", "conv_qkv", "a_log", "dt_bias", "onorm_g", "pool_w", "pool_scale", "sconv_w", "w_out",
               "norm2_g", "w_gate", "w_up", "w_down", "ple_proj", "ple_gate", "final_g")


def _pad_rows(flat, row_multiple):
    m = flat.shape[0]
    r = -(-m // (LANES * row_multiple)) * row_multiple
    return jnp.pad(flat, (0, r * LANES - m)).reshape(r, LANES)


def _adamw_math(w, g, m, v):
    c1 = 1.0 / (1.0 - ADAM_B1 ** ADAM_STEP)
    c2 = 1.0 / (1.0 - ADAM_B2 ** ADAM_STEP)
    nm = ADAM_B1 * m + (1.0 - ADAM_B1) * g
    nv = ADAM_B2 * v + (1.0 - ADAM_B2) * (g * g)
    return -ADAM_LR * ((nm * c1) / (jnp.sqrt(nv * c2) + ADAM_EPS) + ADAM_WD * w), nm, nv


def _adamw(w, g, m, v):
    shape = w.shape
    cols = shape[-1]
    rows = w.size // cols
    tr = _tile_rows(rows, 512)

    def body(w_ref, g_ref, m_ref, v_ref, d_ref, nm_ref, nv_ref, go_ref):
        gv = g_ref[...]
        d_ref[...], nm_ref[...], nv_ref[...] = _adamw_math(w_ref[...], gv, m_ref[...], v_ref[...])
        go_ref[...] = gv

    spec = pl.BlockSpec((tr, cols), lambda i: (i, 0))
    outs = pl.pallas_call(
        body, name="adamw", grid=(rows // tr,), in_specs=[spec] * 4, out_specs=[spec] * 4,
        out_shape=[jax.ShapeDtypeStruct((rows, cols), F32)] * 4,
        compiler_params=_params("arbitrary"))(*[a.reshape(rows, cols) for a in (w, g, m, v)])
    return tuple(o.reshape(shape) for o in outs)


def _adamw_together(ws, gs, ms, vs):
    k = len(ws)
    flat = [(w.size // w.shape[-1], w.shape[-1]) for w in ws]

    def body(*refs):
        for i in range(k):
            w_ref, g_ref, m_ref, v_ref = refs[4 * i:4 * i + 4]
            d_ref, nm_ref, nv_ref, go_ref = refs[4 * (k + i):4 * (k + i) + 4]
            gv = g_ref[...]
            d_ref[...], nm_ref[...], nv_ref[...] = _adamw_math(w_ref[...], gv, m_ref[...], v_ref[...])
            go_ref[...] = gv

    specs = [pl.BlockSpec(rc, lambda i: (0, 0)) for rc in flat for _ in range(4)]
    outs = pl.pallas_call(
        body, name="adamw_together", grid=(1,), in_specs=specs, out_specs=specs,
        out_shape=[jax.ShapeDtypeStruct(rc, F32) for rc in flat for _ in range(4)],
        compiler_params=_params("arbitrary"))(*[a.reshape(rc) for rc, four in zip(flat, zip(ws, gs, ms, vs)) for a in four])
    return [tuple(o.reshape(w.shape) for o in outs[4 * i:4 * i + 4]) for i, w in enumerate(ws)]


def kernel(x, p, norm1_g, w_in, conv_qkv, a_log, dt_bias, onorm_g, pool_w, pool_scale, sconv_w, w_out, norm2_g, w_gate, w_up, w_down, ple_proj, ple_gate, final_g, loss_target, m_norm1_g, m_w_in, m_conv_qkv, m_a_log, m_dt_bias, m_onorm_g, m_pool_w, m_pool_scale, m_sconv_w, m_w_out, m_norm2_g, m_w_gate, m_w_up, m_w_down, m_ple_proj, m_ple_gate, m_final_g, v_norm1_g, v_w_in, v_conv_qkv, v_a_log, v_dt_bias, v_onorm_g, v_pool_w, v_pool_scale, v_sconv_w, v_w_out, v_norm2_g, v_w_gate, v_w_up, v_w_down, v_ple_proj, v_ple_gate, v_final_g):
    weights = dict(zip(ALL_WEIGHTS, (norm1_g, w_in, conv_qkv, a_log, dt_bias, onorm_g, pool_w, pool_scale, sconv_w, w_out,
                                     norm2_g, w_gate, w_up, w_down, ple_proj, ple_gate, final_g)))
    mom_m = dict(zip(ALL_WEIGHTS, (m_norm1_g, m_w_in, m_conv_qkv, m_a_log, m_dt_bias, m_onorm_g, m_pool_w, m_pool_scale,
                                   m_sconv_w, m_w_out, m_norm2_g, m_w_gate, m_w_up, m_w_down, m_ple_proj, m_ple_gate, m_final_g)))
    mom_v = dict(zip(ALL_WEIGHTS, (v_norm1_g, v_w_in, v_conv_qkv, v_a_log, v_dt_bias, v_onorm_g, v_pool_w, v_pool_scale,
                                   v_sconv_w, v_w_out, v_norm2_g, v_w_gate, v_w_up, v_w_down, v_ple_proj, v_ple_gate, v_final_g)))
    for n in TRANSPOSED:
        weights[n], mom_m[n], mom_v[n] = (jnp.swapaxes(a[n], 1, 2) for a in (weights, mom_m, mom_v))
    c_idx = lax.axis_index("c").astype(jnp.int32).reshape(1)
    chip = (2 * lax.axis_index("x") + lax.axis_index("y")).astype(jnp.int32)
    me_idx = chip.reshape(1)
    idx = jnp.stack([chip, lax.axis_index("c").astype(jnp.int32)])
    depth = p.shape[0]

    small = {n: weights[n] for n in REPLICATED}
    sflat = _pad_rows(jnp.concatenate([weights[n].reshape(-1) for n in SMALL_SHARDED]), 8)
    sgath8 = _all_gather_devices(sflat)
    placed_in = _place_shards([weights["w_in"]], me_idx)
    sems, arrs, _ = _gather_call("gather_first_start", placed_in[0], [], sgath8, 0)
    placed_rest = _place_shards([weights[n] for n in BIG[1:]], me_idx)
    placed = [placed_in[li] + placed_rest[li] for li in range(depth)]
    sems, arrs, _ = _gather_call("gather_first_forward", arrs, sems, placed_rest[0][0], 1)
    _, arrs, token = _gather_call("gather_first_finish", arrs, sems, placed_rest[0][0], 2)
    gw = [dict() for _ in range(depth)]
    gw[0]["w_in"] = arrs[0]
    early = ("w_in", "w_out")
    late = tuple(n for n in BIG if n not in early)
    groups = [dict(li=0, names=BIG[1:], forward=(0, "mixed"), finish=(0, "mixed"))]
    for li in range(1, depth):
        groups.append(dict(li=li, names=early, forward=(li - 1, "ffn"), finish=(li - 1, "end")))
        groups.append(dict(li=li, names=late, forward=(li, "mixed"), finish=(li, "mixed")))
    def arrive(li, stage, after):
        for k, g in enumerate(groups):
            if g["forward"] == (li, stage):
                g["sems"], g["arrs"], _ = _gather_call("gather_forward_%d" % k, g["arrs"], g["sems"], after, 1)
            if g["finish"] == (li, stage):
                _, g["arrs"], _ = _gather_call("gather_finish_%d" % k, g["arrs"], g["sems"], after, 2)
                gw[g["li"]].update(zip(g["names"], g["arrs"]))

    sgath = sgath8[0::2].reshape(4, -1)
    off = 0
    for n in SMALL_SHARDED:
        shp = weights[n].shape
        part = sgath[:, off:off + weights[n].size].reshape((4,) + shp)
        small[n] = jnp.moveaxis(part, 0, -2).reshape(shp[:-1] + (4 * shp[-1],))
        off += weights[n].size
    for k, g in enumerate(groups):
        arrs = [placed[g["li"]][BIG.index(n)] for n in g["names"]]
        g["sems"], g["arrs"], token = _gather_call("gather_start_%d" % k, arrs, [], token, 0)

    small["norm1_g"] = small["norm1_g"] + token[0, 0]

    pending = []
    last_token = [None]

    def advance(g, after):
        if g["stage"] == 0:
            gs, others = _copies_wait("swap_wait_" + g["tag"], "swap", *g["handle"], after)
            g["handle"] = _copies_start("exchange_start_" + g["tag"], "exchange", _add_my_halves(gs, others, c_idx))
            g["stage"] = 1
            return g["handle"][3]
        return None

    held = {}

    def produced(li, stage, grads, after):
        token = None
        for g in pending:
            token = advance(g, after) if g["stage"] == 0 else token
        if li > 0 and stage != "end":
            held.update(grads)
            grads = {}
        elif li > 0:
            grads = {**held, **grads}
            held.clear()
        if grads:
            names = [n for n in BIG if n in grads]
            handle = _copies_start("swap_start_%d%s" % (li, stage), "swap", [grads[n] for n in names], token)
            pending.append(dict(li=li, names=names, tag="%d%s" % (li, stage), stage=0, handle=handle[:3]))
            token = handle[3]
        last_token[0] = last_token[0] if token is None else token
        return token

    loss_local, dx, _, small_grads = _local_step(x[0], p[:, 0], loss_target[0], gw, small, produced, arrive)
    rnames = REPLICATED + SMALL_SHARDED
    rflat = _pad_rows(jnp.concatenate([small_grads[n].reshape(-1) for n in rnames] + [loss_local.reshape(1)]), 8)
    small_handle = _copies_start("small_start", "devices", [rflat], last_token[0])
    accs, big_outs = {}, {}

    def finish(g, after):
        pairs, recvs = _copies_wait("exchange_wait_" + g["tag"], "exchange", *g["handle"][:3], after)
        summed = _sum_into(pairs, recvs, idx, g["li"], depth, [accs.get(n) for n in g["names"]])
        accs.update(zip(g["names"], _sibling_share(summed, g["li"])))
        return accs[g["names"][-1]]

    def update(names):
        for n in names:
            big_outs[n] = _adamw(weights[n], accs[n], mom_m[n], mom_v[n])
        return jnp.stack([big_outs[n][0].reshape(-1)[0] for n in names])

    done = finish(pending[0], small_handle[3])
    done = advance(pending[-1], done)
    for g in pending[1:-1]:
        done = finish(g, done)
    last = pending[-1]["names"]
    done = update([n for n in BIG if n not in last])
    finish(pending[-1], done)
    done = update(last)


    gshard = {}
    (own,), (land,) = _copies_wait("small_wait", "devices", *small_handle[:3], done)
    me_dev = (2 * chip + lax.axis_index("c").astype(jnp.int32)).reshape(1)
    rsum = _sum_devices(own, land, me_dev).reshape(-1)
    off = 0
    for n in rnames:
        whole = rsum[off:off + small_grads[n].size].reshape(small_grads[n].shape)
        off += small_grads[n].size
        if n in SMALL_SHARDED:
            cols = weights[n].shape[-1]
            whole = lax.dynamic_slice_in_dim(whole, chip * cols, cols, axis=whole.ndim - 1)
        gshard[n] = whole

    loss = rsum[off]

    deltas, new_m, new_v, grad_out = {}, {}, {}, {}
    others = [n for n in ALL_WEIGHTS if n not in BIG]
    big_outs.update(zip(others, _adamw_together(*[[a[n] for n in others] for a in (weights, gshard, mom_m, mom_v)])))
    for n in ALL_WEIGHTS:
        deltas[n], new_m[n], new_v[n], grad_out[n] = big_outs[n]
    for n in TRANSPOSED:
        deltas[n], new_m[n], new_v[n], grad_out[n] = (jnp.swapaxes(a[n], 1, 2) for a in (deltas, new_m, new_v, grad_out))
    return (loss, dx[None], *[grad_out[n] for n in ALL_WEIGHTS], *[deltas[n] for n in ALL_WEIGHTS],
            *[new_m[n] for n in ALL_WEIGHTS], *[new_v[n] for n in ALL_WEIGHTS])
```

```python
import jax
import jax.numpy as jnp
from jax import lax
from jax.experimental import pallas as pl
from jax.experimental.pallas import tpu as pltpu

F32 = jnp.float32
MM_DTYPE = jnp.bfloat16
WIRE_DTYPE = jnp.bfloat16
EPS = 1e-6
HEAD_DIM = 128
CHUNK = 64
QKV_CONV_WIDTH = 4
SCONV_WIDTH = 3
POOL_GROUPS = 4
LANES = 128
SUBLANES_WIRE = 16
VMEM_LIMIT_BYTES = 56 * 1024 * 1024
ADAM_LR, ADAM_B1, ADAM_B2, ADAM_EPS, ADAM_WD, ADAM_STEP = 0.001, 0.9, 0.999, 1e-08, 0.01, 10
MESH = pl.DeviceIdType.MESH
ANY = pl.BlockSpec(memory_space=pl.ANY)
HBM = pl.BlockSpec(memory_space=pltpu.HBM)
SEM = pl.BlockSpec(memory_space=pltpu.SEMAPHORE)


def _params(*sem):
    return pltpu.CompilerParams(vmem_limit_bytes=VMEM_LIMIT_BYTES, dimension_semantics=sem if sem else None)


def _mm(a, b):
    return jnp.dot(a.astype(MM_DTYPE), b.astype(MM_DTYPE), preferred_element_type=F32)


def _mm_nt(a, b):
    return lax.dot_general(a.astype(MM_DTYPE), b.astype(MM_DTYPE), (((1,), (1,)), ((), ())), preferred_element_type=F32)


def _mm_tn(a, b):
    return lax.dot_general(a.astype(MM_DTYPE), b.astype(MM_DTYPE), (((0,), (0,)), ((), ())), preferred_element_type=F32)


def _sigmoid(x):
    return 1.0 / (1.0 + jnp.exp(-x))


def _dsilu(x, s):
    return s * (1.0 + x * (1.0 - s))


def _rows(shape):
    return lax.broadcasted_iota(jnp.int32, shape, 0)


def _shift_down(x, s):
    if s == 0:
        return x
    return jnp.where(_rows(x.shape) >= s, pltpu.roll(x, s, 0), 0.0)


def _shift_up(x, s):
    if s == 0:
        return x
    t = x.shape[0]
    return jnp.where(_rows(x.shape) < t - s, pltpu.roll(x, t - s, 0), 0.0)


def _rms_fwd(x):
    r = lax.rsqrt(jnp.mean(x * x, axis=-1, keepdims=True) + EPS)
    return x * r, r


def _rms_bwd(dxn, xn, r):
    return r * (dxn - xn * jnp.mean(dxn * xn, axis=-1, keepdims=True))


def _tile_rows(n, cap, mult=8):
    best = None
    for d in range(mult, min(n, cap) + 1, mult):
        if n % d == 0:
            best = d
    return best if best is not None else n


def _in_proj_fwd(x, g1, wp, segs, tm):
    t, d = x.shape
    npk = wp.shape[1]

    def body(x_ref, g_ref, w_ref, *o_refs):
        xn, _ = _rms_fwd(x_ref[...])
        h = (xn * g_ref[...]).astype(w_ref.dtype)
        off = 0
        for o_ref, wd in zip(o_refs, segs):
            o_ref[...] = jnp.dot(h, w_ref[:, off:off + wd], preferred_element_type=F32)
            off += wd

    return pl.pallas_call(
        body, name="in_proj_fwd", grid=(t // tm,),
        in_specs=[pl.BlockSpec((tm, d), lambda i: (i, 0)), pl.BlockSpec((1, d), lambda i: (0, 0)),
                  pl.BlockSpec((d, npk), lambda i: (0, 0))],
        out_specs=[pl.BlockSpec((tm, wd), lambda i: (i, 0)) for wd in segs],
        out_shape=[jax.ShapeDtypeStruct((t, wd), F32) for wd in segs],
        compiler_params=_params("arbitrary"))(x, g1, wp)


def _in_proj_bwd(x, g1, wp, dsegs, dx_res, segs, tm):
    t, d = x.shape
    npk = wp.shape[1]
    nseg = len(segs)

    def body(x_ref, g_ref, w_ref, *rest):
        ds_refs = rest[:nseg]
        dxr_ref, dx_ref, dw_ref, dg_ref = rest[nseg:]
        i = pl.program_id(0)

        @pl.when(i == 0)
        def _():
            dw_ref[...] = jnp.zeros_like(dw_ref)
            dg_ref[...] = jnp.zeros_like(dg_ref)

        xn, r = _rms_fwd(x_ref[...])
        g = g_ref[...]
        h = (xn * g).astype(w_ref.dtype)
        dcat = jnp.concatenate([ds_ref[...].astype(w_ref.dtype) for ds_ref in ds_refs], axis=1)
        dh = lax.dot_general(dcat, w_ref[...], (((1,), (1,)), ((), ())), preferred_element_type=F32)
        dw_ref[...] += lax.dot_general(h, dcat, (((0,), (0,)), ((), ())), preferred_element_type=F32)
        dg_ref[...] += jnp.sum(dh * xn, axis=0, keepdims=True)
        dx_ref[...] = dxr_ref[...] + _rms_bwd(dh * g, xn, r)

    return pl.pallas_call(
        body, name="in_proj_bwd", grid=(t // tm,),
        in_specs=[pl.BlockSpec((tm, d), lambda i: (i, 0)), pl.BlockSpec((1, d), lambda i: (0, 0)),
                  pl.BlockSpec((d, npk), lambda i: (0, 0))]
                 + [pl.BlockSpec((tm, wd), lambda i: (i, 0)) for wd in segs]
                 + [pl.BlockSpec((tm, d), lambda i: (i, 0))],
        out_specs=[pl.BlockSpec((tm, d), lambda i: (i, 0)), pl.BlockSpec((d, npk), lambda i: (0, 0)),
                   pl.BlockSpec((1, d), lambda i: (0, 0))],
        out_shape=[jax.ShapeDtypeStruct((t, d), F32), jax.ShapeDtypeStruct((d, npk), F32),
                   jax.ShapeDtypeStruct((1, d), F32)],
        compiler_params=_params("arbitrary"))(x, g1, wp, *dsegs, dx_res)


def _out_proj_fwd(x0, mix, wo, g2, tm):
    t, d = x0.shape
    dq = wo.shape[1]
    widths = [m.shape[1] for m in mix]

    def body(x_ref, *rest):
        m_refs = rest[:len(mix)]
        w_ref, g_ref, x1_ref, h2_ref = rest[len(mix):]
        acc = x_ref[...]
        off = 0
        for m_ref, wd in zip(m_refs, widths):
            for k in range(wd // dq):
                acc = acc + jnp.dot(m_ref[:, k * dq:(k + 1) * dq].astype(w_ref.dtype), w_ref[off // dq + k],
                                    preferred_element_type=F32)
            off += wd
        x1_ref[...] = acc
        xn, _ = _rms_fwd(acc)
        h2_ref[...] = (xn * g_ref[...]).astype(h2_ref.dtype)

    return pl.pallas_call(
        body, name="out_proj_fwd", grid=(t // tm,),
        in_specs=[pl.BlockSpec((tm, d), lambda i: (i, 0))]
                 + [pl.BlockSpec((tm, wd), lambda i: (i, 0)) for wd in widths]
                 + [pl.BlockSpec((4, dq, d), lambda i: (0, 0, 0)), pl.BlockSpec((1, d), lambda i: (0, 0))],
        out_specs=[pl.BlockSpec((tm, d), lambda i: (i, 0)), pl.BlockSpec((tm, d), lambda i: (i, 0))],
        out_shape=[jax.ShapeDtypeStruct((t, d), F32), jax.ShapeDtypeStruct((t, d), MM_DTYPE)],
        compiler_params=_params("arbitrary"))(x0, *mix, wo, g2)


def _out_proj_bwd(dx2, dh2, x1, g2, mix, wo, tm):
    t, d = x1.shape
    dq = wo.shape[1]
    widths = [m.shape[1] for m in mix]
    nm = len(mix)

    def body(dx2_ref, dh2_ref, x1_ref, g_ref, *rest):
        m_refs = rest[:nm]
        w_ref = rest[nm]
        dx1_ref = rest[nm + 1]
        dm_refs = rest[nm + 2:nm + 2 + nm]
        dw_ref, dg_ref = rest[nm + 2 + nm:]
        i = pl.program_id(0)

        @pl.when(i == 0)
        def _():
            dw_ref[...] = jnp.zeros_like(dw_ref)
            dg_ref[...] = jnp.zeros_like(dg_ref)

        xn, r = _rms_fwd(x1_ref[...])
        dh2v = dh2_ref[...]
        dg_ref[...] += jnp.sum(dh2v * xn, axis=0, keepdims=True)
        dx1 = dx2_ref[...] + _rms_bwd(dh2v * g_ref[...], xn, r)
        dx1_ref[...] = dx1
        dx1c = dx1.astype(w_ref.dtype)
        off = 0
        for m_ref, dm_ref, wd in zip(m_refs, dm_refs, widths):
            for k in range(wd // dq):
                j = off // dq + k
                cols = slice(k * dq, (k + 1) * dq)
                dm_ref[:, cols] = lax.dot_general(dx1c, w_ref[j], (((1,), (1,)), ((), ())), preferred_element_type=F32)
                dw_ref[j] += lax.dot_general(m_ref[:, cols].astype(w_ref.dtype), dx1c, (((0,), (0,)), ((), ())),
                                             preferred_element_type=F32)
            off += wd

    tile = lambda wd: pl.BlockSpec((tm, wd), lambda i: (i, 0))
    return pl.pallas_call(
        body, name="out_proj_bwd", grid=(t // tm,),
        in_specs=[tile(d), tile(d), tile(d), pl.BlockSpec((1, d), lambda i: (0, 0))]
                 + [tile(wd) for wd in widths] + [pl.BlockSpec((4, dq, d), lambda i: (0, 0, 0))],
        out_specs=[tile(d)] + [tile(wd) for wd in widths]
                  + [pl.BlockSpec((4, dq, d), lambda i: (0, 0, 0)), pl.BlockSpec((1, d), lambda i: (0, 0))],
        out_shape=[jax.ShapeDtypeStruct((t, d), F32)] + [jax.ShapeDtypeStruct((t, wd), F32) for wd in widths]
                  + [jax.ShapeDtypeStruct((4, dq, d), F32), jax.ShapeDtypeStruct((1, d), F32)],
        compiler_params=_params("arbitrary"))(dx2, dh2, x1, g2, *mix, wo)


def _ffn_fwd(x1, h2, wg, wu, wd, tm):
    t, d = x1.shape
    fs = wg.shape[1]

    def body(x1_ref, h2_ref, wg_ref, wu_ref, wd_ref, x2_ref, gp_ref, up_ref):
        @pl.when(pl.program_id(1) == 0)
        def _():
            x2_ref[...] = x1_ref[...]

        h = h2_ref[...]
        nt = (((1,), (1,)), ((), ()))
        gp = lax.dot_general(h, wg_ref[...], nt, preferred_element_type=F32)
        up = lax.dot_general(h, wu_ref[...], nt, preferred_element_type=F32)
        gp_ref[...] = gp
        up_ref[...] = up
        ff = gp * _sigmoid(gp) * up
        x2_ref[...] += jnp.dot(ff.astype(wd_ref.dtype), wd_ref[...], preferred_element_type=F32)

    return pl.pallas_call(
        body, name="ffn_fwd", grid=(t // tm, 4),
        in_specs=[pl.BlockSpec((tm, d), lambda i, j: (i, 0)), pl.BlockSpec((tm, d), lambda i, j: (i, 0)),
                  pl.BlockSpec((None, fs, d), lambda i, j: (j, 0, 0)),
                  pl.BlockSpec((None, fs, d), lambda i, j: (j, 0, 0)),
                  pl.BlockSpec((None, fs, d), lambda i, j: (j, 0, 0))],
        out_specs=[pl.BlockSpec((tm, d), lambda i, j: (i, 0)), pl.BlockSpec((None, tm, fs), lambda i, j: (j, i, 0)),
                   pl.BlockSpec((None, tm, fs), lambda i, j: (j, i, 0))],
        out_shape=[jax.ShapeDtypeStruct((t, d), F32), jax.ShapeDtypeStruct((4, t, fs), F32),
                   jax.ShapeDtypeStruct((4, t, fs), F32)],
        compiler_params=_params("arbitrary", "arbitrary"))(x1, h2, wg, wu, wd)


def _ffn_bwd(dx2, h2, gp, up, wg, wu, wd, tm):
    t, d = dx2.shape
    fs = wg.shape[1]

    def body(dx2_ref, h2_ref, gp_ref, up_ref, wg_ref, wu_ref, wd_ref, dh2_ref, dwg_ref, dwu_ref, dwd_ref):
        j, i = pl.program_id(0), pl.program_id(1)

        @pl.when(i == 0)
        def _():
            dwg_ref[...] = jnp.zeros_like(dwg_ref)
            dwu_ref[...] = jnp.zeros_like(dwu_ref)
            dwd_ref[...] = jnp.zeros_like(dwd_ref)

        cdt = wg_ref.dtype
        h = h2_ref[...]
        gpv, upv = gp_ref[...], up_ref[...]
        s = _sigmoid(gpv)
        silu = gpv * s
        dx2c = dx2_ref[...].astype(cdt)
        dff = lax.dot_general(dx2c, wd_ref[...], (((1,), (1,)), ((), ())), preferred_element_type=F32)
        dwd_ref[...] += lax.dot_general((silu * upv).astype(cdt), dx2c, (((0,), (0,)), ((), ())), preferred_element_type=F32)
        dup = (dff * silu).astype(cdt)
        dgp = (dff * upv * _dsilu(gpv, s)).astype(cdt)
        dwg_ref[...] += lax.dot_general(dgp, h, (((0,), (0,)), ((), ())), preferred_element_type=F32)
        dwu_ref[...] += lax.dot_general(dup, h, (((0,), (0,)), ((), ())), preferred_element_type=F32)
        dh = (jnp.dot(dgp, wg_ref[...], preferred_element_type=F32) + jnp.dot(dup, wu_ref[...], preferred_element_type=F32))
        rows = pl.ds(pl.multiple_of(i * tm, tm), tm)

        @pl.when(j == 0)
        def _():
            dh2_ref[rows, :] = dh

        @pl.when(j != 0)
        def _():
            dh2_ref[rows, :] += dh

    return pl.pallas_call(
        body, name="ffn_bwd", grid=(4, t // tm),
        in_specs=[pl.BlockSpec((tm, d), lambda j, i: (i, 0)), pl.BlockSpec((tm, d), lambda j, i: (i, 0)),
                  pl.BlockSpec((None, tm, fs), lambda j, i: (j, i, 0)), pl.BlockSpec((None, tm, fs), lambda j, i: (j, i, 0)),
                  pl.BlockSpec((None, fs, d), lambda j, i: (j, 0, 0)),
                  pl.BlockSpec((None, fs, d), lambda j, i: (j, 0, 0)),
                  pl.BlockSpec((None, fs, d), lambda j, i: (j, 0, 0))],
        out_specs=[pl.BlockSpec((t, d), lambda j, i: (0, 0)), pl.BlockSpec((None, fs, d), lambda j, i: (j, 0, 0)),
                   pl.BlockSpec((None, fs, d), lambda j, i: (j, 0, 0)), pl.BlockSpec((None, fs, d), lambda j, i: (j, 0, 0))],
        out_shape=[jax.ShapeDtypeStruct((t, d), F32)] + [jax.ShapeDtypeStruct((4, fs, d), F32)] * 3,
        compiler_params=_params("arbitrary", "arbitrary"))(dx2, h2, gp, up, wg, wu, wd)


def _ple_fwd(x2, p, wpg, wpp, tm):
    t, d = x2.shape
    q = p.shape[1]
    dq = d // 4

    def body(x_ref, p_ref, wg_ref, wp_ref, o_ref):
        xv = x_ref[...]
        xc = xv.astype(wg_ref.dtype)
        pc = p_ref[...].astype(wp_ref.dtype)
        pre = jnp.dot(xc[:, :dq], wg_ref[0], preferred_element_type=F32)
        for j in range(1, 4):
            pre = pre + jnp.dot(xc[:, j * dq:(j + 1) * dq], wg_ref[j], preferred_element_type=F32)
        gate = _sigmoid(pre)
        for j in range(4):
            cols = slice(j * dq, (j + 1) * dq)
            o_ref[:, cols] = xv[:, cols] + gate[:, cols] * jnp.dot(pc, wp_ref[j], preferred_element_type=F32)

    return pl.pallas_call(
        body, name="ple_fwd", grid=(t // tm,),
        in_specs=[pl.BlockSpec((tm, d), lambda i: (i, 0)), pl.BlockSpec((tm, q), lambda i: (i, 0)),
                  pl.BlockSpec((4, dq, d), lambda i: (0, 0, 0)),
                  pl.BlockSpec((4, q, dq), lambda i: (0, 0, 0))],
        out_specs=pl.BlockSpec((tm, d), lambda i: (i, 0)),
        out_shape=jax.ShapeDtypeStruct((t, d), F32),
        compiler_params=_params("arbitrary"))(x2, p, wpg, wpp)


def _ple_bwd(dx3, x2, p, wpg, wpp, tm):
    t, d = x2.shape
    q = p.shape[1]
    dq = d // 4

    def body(dx3_ref, x_ref, p_ref, wg_ref, wp_ref, dx2_ref, dwg_ref, dwp_ref):
        @pl.when(pl.program_id(0) == 0)
        def _():
            dwg_ref[...] = jnp.zeros_like(dwg_ref)
            dwp_ref[...] = jnp.zeros_like(dwp_ref)

        cdt = wg_ref.dtype
        xc = x_ref[...].astype(cdt)
        pc = p_ref[...].astype(cdt)
        pre = jnp.dot(xc[:, :dq], wg_ref[0], preferred_element_type=F32)
        for j in range(1, 4):
            pre = pre + jnp.dot(xc[:, j * dq:(j + 1) * dq], wg_ref[j], preferred_element_type=F32)
        gate = _sigmoid(pre)
        dx3v = dx3_ref[...]
        dpp = (dx3v * gate).astype(cdt)
        dgate = dx3v * gate * (1.0 - gate)
        dpre_parts = []
        for j in range(4):
            cols = slice(j * dq, (j + 1) * dq)
            pp_j = jnp.dot(pc, wp_ref[j], preferred_element_type=F32)
            dpre_parts.append((dgate[:, cols] * pp_j).astype(cdt))
            dwp_ref[j] += lax.dot_general(pc, dpp[:, cols], (((0,), (0,)), ((), ())), preferred_element_type=F32)
        dpre = jnp.concatenate(dpre_parts, axis=1)
        for j in range(4):
            cols = slice(j * dq, (j + 1) * dq)
            dwg_ref[j] += lax.dot_general(xc[:, cols], dpre, (((0,), (0,)), ((), ())), preferred_element_type=F32)
            dx2_ref[:, cols] = dx3v[:, cols] + lax.dot_general(dpre, wg_ref[j], (((1,), (1,)), ((), ())),
                                                               preferred_element_type=F32)

    return pl.pallas_call(
        body, name="ple_bwd", grid=(t // tm,),
        in_specs=[pl.BlockSpec((tm, d), lambda i: (i, 0)), pl.BlockSpec((tm, d), lambda i: (i, 0)),
                  pl.BlockSpec((tm, q), lambda i: (i, 0)), pl.BlockSpec((4, dq, d), lambda i: (0, 0, 0)),
                  pl.BlockSpec((4, q, dq), lambda i: (0, 0, 0))],
        out_specs=[pl.BlockSpec((tm, d), lambda i: (i, 0)), pl.BlockSpec((4, dq, d), lambda i: (0, 0, 0)),
                   pl.BlockSpec((4, q, dq), lambda i: (0, 0, 0))],
        out_shape=[jax.ShapeDtypeStruct((t, d), F32), jax.ShapeDtypeStruct((4, dq, d), F32),
                   jax.ShapeDtypeStruct((4, q, dq), F32)],
        compiler_params=_params("arbitrary"))(dx3, x2, p, wpg, wpp)


def _loss_head(x, target, fg, tm):
    t, d = x.shape

    def body(x_ref, t_ref, g_ref, dx_ref, loss_ref, dg_ref):
        @pl.when(pl.program_id(0) == 0)
        def _():
            loss_ref[...] = jnp.zeros_like(loss_ref)
            dg_ref[...] = jnp.zeros_like(dg_ref)

        xn, r = _rms_fwd(x_ref[...])
        g = g_ref[...]
        err = xn * g - t_ref[...]
        loss_ref[...] += 0.5 * jnp.sum(jnp.sum(err * err, axis=-1, keepdims=True) / d, axis=0, keepdims=True)
        dy = err / d
        dg_ref[...] += jnp.sum(dy * xn, axis=0, keepdims=True)
        dx_ref[...] = _rms_bwd(dy * g, xn, r)

    return pl.pallas_call(
        body, name="loss_head", grid=(t // tm,),
        in_specs=[pl.BlockSpec((tm, d), lambda i: (i, 0)), pl.BlockSpec((tm, d), lambda i: (i, 0)),
                  pl.BlockSpec((1, d), lambda i: (0, 0))],
        out_specs=[pl.BlockSpec((tm, d), lambda i: (i, 0)), pl.BlockSpec((1, 1), lambda i: (0, 0)),
                   pl.BlockSpec((1, d), lambda i: (0, 0))],
        out_shape=[jax.ShapeDtypeStruct((t, d), F32), jax.ShapeDtypeStruct((1, 1), F32),
                   jax.ShapeDtypeStruct((1, d), F32)],
        compiler_params=_params("arbitrary"))(x, target, fg)


def _qkv_conv_act(xv, w, j, heads):
    k = QKV_CONV_WIDTH
    y = w[k - 1:k] * xv
    for s in range(1, k):
        y = y + w[k - 1 - s:k - s] * _shift_down(xv, s)
    sg = _sigmoid(y)
    s_act = y * sg
    nrm = lax.rsqrt(jnp.sum(s_act * s_act, axis=-1, keepdims=True) + EPS)
    scale = jnp.where(j < heads, HEAD_DIM ** -0.5, 1.0).astype(F32)
    return y, sg, s_act, nrm, scale


def _qkv_conv_fwd(qkv_pre, conv_w, heads):
    t = qkv_pre.shape[0]
    nblk = 3 * heads

    def body(x_ref, w_ref, o_ref):
        j = pl.program_id(0)
        _, _, s_act, nrm, scale = _qkv_conv_act(x_ref[...], w_ref[...], j, heads)
        o_ref[...] = jnp.where(j < 2 * heads, s_act * (nrm * scale), s_act)

    return pl.pallas_call(
        body, name="qkv_conv_fwd", grid=(nblk,),
        in_specs=[pl.BlockSpec((t, LANES), lambda j: (0, j)), pl.BlockSpec((QKV_CONV_WIDTH, LANES), lambda j: (0, j))],
        out_specs=pl.BlockSpec((t, LANES), lambda j: (0, j)),
        out_shape=jax.ShapeDtypeStruct(qkv_pre.shape, F32),
        compiler_params=_params("arbitrary"))(qkv_pre, conv_w)


def _qkv_conv_bwd(qkv_pre, conv_w, dqkv, heads):
    t = qkv_pre.shape[0]
    nblk = 3 * heads
    k = QKV_CONV_WIDTH

    def body(x_ref, w_ref, dn_ref, dx_ref, dw_ref):
        j = pl.program_id(0)
        xv, w = x_ref[...], w_ref[...]
        y, sg, s_act, nrm, scale = _qkv_conv_act(xv, w, j, heads)
        dn = dn_ref[...]
        dsn = dn * scale
        ds_qk = nrm * dsn - s_act * (nrm * nrm * nrm) * jnp.sum(dsn * s_act, axis=-1, keepdims=True)
        ds = jnp.where(j < 2 * heads, ds_qk, dn)
        dy = ds * _dsilu(y, sg)
        dx = w[k - 1:k] * dy
        dw_ref[k - 1:k, :] = jnp.sum(dy * xv, axis=0, keepdims=True)
        for s in range(1, k):
            dx = dx + w[k - 1 - s:k - s] * _shift_up(dy, s)
            dw_ref[k - 1 - s:k - s, :] = jnp.sum(dy * _shift_down(xv, s), axis=0, keepdims=True)
        dx_ref[...] = dx

    return pl.pallas_call(
        body, name="qkv_conv_bwd", grid=(nblk,),
        in_specs=[pl.BlockSpec((t, LANES), lambda j: (0, j)), pl.BlockSpec((k, LANES), lambda j: (0, j)),
                  pl.BlockSpec((t, LANES), lambda j: (0, j))],
        out_specs=[pl.BlockSpec((t, LANES), lambda j: (0, j)), pl.BlockSpec((k, LANES), lambda j: (0, j))],
        out_shape=[jax.ShapeDtypeStruct(qkv_pre.shape, F32), jax.ShapeDtypeStruct(conv_w.shape, F32)],
        compiler_params=_params("arbitrary"))(qkv_pre, conv_w, dqkv)


def _pool_windows(shape, j, group_dim):
    lane = lax.broadcasted_iota(jnp.int32, shape, 1) + j * LANES
    grp = lane // group_dim
    win = jnp.left_shift(2, grp).astype(F32)
    cnt = jnp.minimum((_rows(shape) + 1).astype(F32), win)
    return grp, cnt


def _pool_select(grp, levels):
    out = levels[0]
    for gi in range(1, POOL_GROUPS):
        out = jnp.where(grp == gi, levels[gi], out)
    return out


def _pool_mean(hv, grp, cnt):
    acc, levels, width = hv, [], 1
    for _ in range(POOL_GROUPS):
        acc = acc + _shift_down(acc, width)
        width *= 2
        levels.append(acc)
    return _pool_select(grp, levels) / cnt - hv


def _pool_fwd(hp, wbd, scale, group_dim):
    t, dp = hp.shape

    def body(h_ref, w_ref, s_ref, o_ref):
        hv = h_ref[...]
        grp, cnt = _pool_windows(hv.shape, pl.program_id(0), group_dim)
        pooled = _pool_mean(hv, grp, cnt)
        o_ref[...] = _mm(pooled, w_ref[...]) * s_ref[...]

    return pl.pallas_call(
        body, name="pool_fwd", grid=(dp // LANES,),
        in_specs=[pl.BlockSpec((t, LANES), lambda j: (0, j)), pl.BlockSpec((LANES, LANES), lambda j: (j, j)),
                  pl.BlockSpec((1, LANES), lambda j: (0, j))],
        out_specs=pl.BlockSpec((t, LANES), lambda j: (0, j)),
        out_shape=jax.ShapeDtypeStruct(hp.shape, F32),
        compiler_params=_params("arbitrary"))(hp, wbd, scale)


def _pool_bwd(hp, wbd, scale, dob, group_dim):
    t, dp = hp.shape

    def body(h_ref, w_ref, s_ref, do_ref, dh_ref, dw_ref, ds_ref):
        hv = h_ref[...]
        grp, cnt = _pool_windows(hv.shape, pl.program_id(0), group_dim)
        pooled = _pool_mean(hv, grp, cnt)
        wv = w_ref[...]
        dov = do_ref[...]
        ds_ref[...] = jnp.sum(dov * _mm(pooled, wv), axis=0, keepdims=True)
        dys = dov * s_ref[...]
        dw_ref[0] = _mm_tn(pooled, dys)
        dpooled = _mm_nt(dys, wv)
        acc, levels, width = dpooled / cnt, [], 1
        for _ in range(POOL_GROUPS):
            acc = acc + _shift_up(acc, width)
            width *= 2
            levels.append(acc)
        dh_ref[...] = _pool_select(grp, levels) - dpooled

    nb = dp // LANES
    return pl.pallas_call(
        body, name="pool_bwd", grid=(nb,),
        in_specs=[pl.BlockSpec((t, LANES), lambda j: (0, j)), pl.BlockSpec((LANES, LANES), lambda j: (j, j)),
                  pl.BlockSpec((1, LANES), lambda j: (0, j)), pl.BlockSpec((t, LANES), lambda j: (0, j))],
        out_specs=[pl.BlockSpec((t, LANES), lambda j: (0, j)), pl.BlockSpec((1, LANES, LANES), lambda j: (j, 0, 0)),
                   pl.BlockSpec((1, LANES), lambda j: (0, j))],
        out_shape=[jax.ShapeDtypeStruct(hp.shape, F32), jax.ShapeDtypeStruct((nb, LANES, LANES), F32),
                   jax.ShapeDtypeStruct((1, dp), F32)],
        compiler_params=_params("arbitrary"))(hp, wbd, scale, dob)


def _sconv_fwd(cbcch, w):
    t, dc3 = cbcch.shape
    nb = dc3 // 3 // LANES
    k = SCONV_WIDTH

    def body(b_ref, c_ref, h_ref, w_ref, o_ref):
        m = c_ref[...] * h_ref[...]
        wv = w_ref[...]
        y = wv[k - 1:k] * m
        for s in range(1, k):
            y = y + wv[k - 1 - s:k - s] * _shift_down(m, s)
        o_ref[...] = b_ref[...] * y

    return pl.pallas_call(
        body, name="sconv_fwd", grid=(nb,),
        in_specs=[pl.BlockSpec((t, LANES), lambda j: (0, j)), pl.BlockSpec((t, LANES), lambda j: (0, nb + j)),
                  pl.BlockSpec((t, LANES), lambda j: (0, 2 * nb + j)), pl.BlockSpec((k, LANES), lambda j: (0, j))],
        out_specs=pl.BlockSpec((t, LANES), lambda j: (0, j)),
        out_shape=jax.ShapeDtypeStruct((t, dc3 // 3), F32),
        compiler_params=_params("arbitrary"))(cbcch, cbcch, cbcch, w)


def _sconv_bwd(cbcch, w, doc):
    t, dc3 = cbcch.shape
    nb = dc3 // 3 // LANES
    k = SCONV_WIDTH

    def body(b_ref, c_ref, h_ref, w_ref, do_ref, db_ref, dc_ref, dh_ref, dw_ref):
        cv, hv = c_ref[...], h_ref[...]
        m = cv * hv
        wv = w_ref[...]
        dov = do_ref[...]
        dy = dov * b_ref[...]
        y = wv[k - 1:k] * m
        dm = wv[k - 1:k] * dy
        dw_ref[k - 1:k, :] = jnp.sum(dy * m, axis=0, keepdims=True)
        for s in range(1, k):
            ms = _shift_down(m, s)
            y = y + wv[k - 1 - s:k - s] * ms
            dm = dm + wv[k - 1 - s:k - s] * _shift_up(dy, s)
            dw_ref[k - 1 - s:k - s, :] = jnp.sum(dy * ms, axis=0, keepdims=True)
        db_ref[...] = dov * y
        dc_ref[...] = dm * hv
        dh_ref[...] = dm * cv

    col = lambda o: pl.BlockSpec((t, LANES), lambda j: (0, o * nb + j))
    return pl.pallas_call(
        body, name="sconv_bwd", grid=(nb,),
        in_specs=[col(0), col(1), col(2), pl.BlockSpec((k, LANES), lambda j: (0, j)), col(0)],
        out_specs=[col(0), col(0), col(0), pl.BlockSpec((k, LANES), lambda j: (0, j))],
        out_shape=[jax.ShapeDtypeStruct((t, dc3 // 3), F32)] * 3 + [jax.ShapeDtypeStruct(w.shape, F32)],
        compiler_params=_params("arbitrary"))(cbcch, cbcch, cbcch, w, doc)


class _Split:
    def __init__(self, a, exact=False):
        self.hi = a.astype(jnp.bfloat16)
        self.lo = None if exact else (a - self.hi.astype(F32)).astype(jnp.bfloat16)


def _per_head(dims, a, b):
    a = a if isinstance(a, _Split) else _Split(a)
    b = b if isinstance(b, _Split) else _Split(b)

    def dot(x, y):
        return lax.dot_general(x, y, (dims, ((), ())), preferred_element_type=F32)

    passes = [(x, y) for x, y in ((a.hi, b.hi), (a.hi, b.lo), (a.lo, b.hi)) if x is not None and y is not None]

    def head(h):
        return dot(jnp.concatenate([x[h] for x, _ in passes], axis=dims[0][0]),
                   jnp.concatenate([y[h] for _, y in passes], axis=dims[1][0]))

    return jnp.stack([head(h) for h in range(a.hi.shape[0])])


def _bmm(a, b):
    return _per_head(((1,), (0,)), a, b)


def _bmm_nt(a, b):
    return _per_head(((1,), (1,)), a, b)


def _bmm_tn(a, b):
    return _per_head(((0,), (0,)), a, b)


def _inv_unit_lower(low):
    c = low.shape[-1]
    eye = (_rows((c, c)) == lax.broadcasted_iota(jnp.int32, (c, c), 1)).astype(F32)
    pw = -low
    inv = eye + pw
    span = 2
    while span < c:
        pws = _Split(pw)
        pw = _bmm(pws, pws)
        inv = inv + _bmm(inv, pw)
        span *= 2
    return inv


def _heads_of(ref, base, heads):
    return jnp.stack([ref[:, base + h * HEAD_DIM:base + (h + 1) * HEAD_DIM] for h in range(heads)])


def _chunk_common(q, k, v, a_col, b_col, alog, dtb, kept=None):
    hn, c, _ = q.shape
    beta = _sigmoid(b_col)
    xg = a_col + dtb
    softplus = jnp.maximum(xg, 0.0) + jnp.log(1.0 + jnp.exp(-jnp.abs(xg)))
    neg_ea = -jnp.exp(alog)
    g = neg_ea * softplus
    ri = _rows((c, c))
    ci = lax.broadcasted_iota(jnp.int32, (c, c), 1)
    incl, strict = ri >= ci, ri > ci
    inclf = _Split(jnp.broadcast_to(incl.astype(F32), (hn, c, c)), exact=True)
    gcb = _bmm(inclf, jnp.broadcast_to(g, (hn, c, HEAD_DIM)))
    gc_row = jnp.sum(jnp.where(ri <= ci, jnp.broadcast_to(g, (hn, c, c)), 0.0), axis=1, keepdims=True)
    dmat = jnp.where(incl, jnp.exp(jnp.where(incl, gcb[:, :, :1] - gc_row, 0.0)), 0.0)
    eg = jnp.exp(gcb)
    gl = gcb[:, c - 1:c, :]
    egl = jnp.exp(gl)
    edl = jnp.exp(gl - gcb)
    kb, vb = k * beta, v * beta
    kbe = kb * eg
    if kept is None:
        ks = _Split(k)
        a0 = _bmm_nt(kb, ks)
        tm = _inv_unit_lower(jnp.where(strict, a0 * dmat, 0.0))
        p0 = _bmm_nt(q, ks)
        tms = _Split(tm)
        u, w = _bmm(tms, vb), _bmm(tms, kbe)
    else:
        (a0, tm, p0, w), u = kept, None
    return dict(beta=beta, xg=xg, neg_ea=neg_ea, g=g, incl=incl, strict=strict, inclf=inclf, dmat=dmat, eg=eg,
                egl=egl, edl=edl, kb=kb, vb=vb, a0=a0, tm=tm, kbe=kbe, u=u, w=w, p0=p0,
                attn=p0 * dmat, qe=q * eg, kd=k * edl)


def _chunk_step(cm, state):
    ss = _Split(state)
    vn = cm["u"] - _bmm(cm["w"], ss)
    vns = _Split(vn)
    o = _bmm(cm["qe"], ss) + _bmm(cm["attn"], vns)
    new_state = state * cm["egl"][:, :, :1] + _bmm_tn(cm["kd"], vns)
    return vn, o, new_state


def _gated_norm(o, zv, og):
    xo, ro = _rms_fwd(o)
    sgz = _sigmoid(zv)
    return xo, ro, sgz, xo * og * (zv * sgz)


def _gate_columns(abv, gpv, heads):
    a_col = jnp.stack([abv[:, h:h + 1] for h in range(heads)])
    b_col = jnp.stack([abv[:, heads + h:heads + h + 1] for h in range(heads)])
    alog = jnp.stack([gpv[0:1, h:h + 1] for h in range(heads)])
    dtb = jnp.stack([gpv[1:2, h:h + 1] for h in range(heads)])
    return a_col, b_col, alog, dtb


def _delta_fwd(qkv, z, ab, gpar, heads):
    t = qkv.shape[0]
    da = heads * HEAD_DIM
    n = t // CHUNK

    def body(qkv_ref, z_ref, ab_ref, gp_ref, oa_ref, st_ref, kc_ref, kw_ref, s_ref):
        @pl.when(pl.program_id(0) == 0)
        def _():
            s_ref[...] = jnp.zeros_like(s_ref)

        gpv = gp_ref[...]
        cm = _chunk_common(_heads_of(qkv_ref, 0, heads), _heads_of(qkv_ref, da, heads), _heads_of(qkv_ref, 2 * da, heads),
                           *_gate_columns(ab_ref[...], gpv, heads))
        state = s_ref[...]
        st_ref[0] = state
        vn, o, new_state = _chunk_step(cm, state)
        s_ref[...] = new_state
        for slot, val in enumerate((cm["a0"], cm["tm"], cm["p0"])):
            kc_ref[0, slot] = val
        for slot, val in enumerate((cm["w"], vn, o)):
            kw_ref[0, slot] = val
        oa = _gated_norm(o, _heads_of(z_ref, 0, heads), gpv[2:3, :])[3]
        for h in range(heads):
            oa_ref[:, h * HEAD_DIM:(h + 1) * HEAD_DIM] = oa[h]

    return pl.pallas_call(
        body, name="delta_fwd", grid=(n,),
        in_specs=[pl.BlockSpec((CHUNK, 3 * da), lambda i: (i, 0)), pl.BlockSpec((CHUNK, da), lambda i: (i, 0)),
                  pl.BlockSpec((CHUNK, LANES), lambda i: (i, 0)), pl.BlockSpec((8, LANES), lambda i: (0, 0))],
        out_specs=[pl.BlockSpec((CHUNK, da), lambda i: (i, 0)),
                   pl.BlockSpec((1, heads, HEAD_DIM, HEAD_DIM), lambda i: (i, 0, 0, 0)),
                   pl.BlockSpec((1, 3, heads, CHUNK, CHUNK), lambda i: (i, 0, 0, 0, 0)),
                   pl.BlockSpec((1, 3, heads, CHUNK, HEAD_DIM), lambda i: (i, 0, 0, 0, 0))],
        out_shape=[jax.ShapeDtypeStruct((t, da), F32), jax.ShapeDtypeStruct((n, heads, HEAD_DIM, HEAD_DIM), F32),
                   jax.ShapeDtypeStruct((n, 3, heads, CHUNK, CHUNK), F32),
                   jax.ShapeDtypeStruct((n, 3, heads, CHUNK, HEAD_DIM), F32)],
        scratch_shapes=[pltpu.VMEM((heads, HEAD_DIM, HEAD_DIM), F32)],
        compiler_params=_params("arbitrary"))(qkv, z, ab, gpar)


def _delta_bwd(qkv, z, ab, gpar, states, kept_c, kept_w, doa, heads):
    t = qkv.shape[0]
    da = heads * HEAD_DIM
    n = t // CHUNK
    c = CHUNK

    def body(qkv_ref, z_ref, ab_ref, gp_ref, st_ref, kc_ref, kw_ref, doa_ref, dqkv_ref, dz_ref, dab_ref, dpar_ref, ds_ref):
        @pl.when(pl.program_id(0) == 0)
        def _():
            ds_ref[...] = jnp.zeros_like(ds_ref)
            dpar_ref[...] = jnp.zeros_like(dpar_ref)

        gpv = gp_ref[...]
        og = gpv[2:3, :]
        q, k, v = _heads_of(qkv_ref, 0, heads), _heads_of(qkv_ref, da, heads), _heads_of(qkv_ref, 2 * da, heads)
        cm = _chunk_common(q, k, v, *_gate_columns(ab_ref[...], gpv, heads),
                           kept=(kc_ref[0, 0], kc_ref[0, 1], kc_ref[0, 2], kw_ref[0, 0]))
        state = st_ref[0]
        dsp = ds_ref[...]
        vn, o = kw_ref[0, 1], kw_ref[0, 2]
        zv = _heads_of(z_ref, 0, heads)
        xo, ro, sgz, _ = _gated_norm(o, zv, og)
        doav = _heads_of(doa_ref, 0, heads)
        don = doav * (zv * sgz)
        dz = doav * (xo * og) * _dsilu(zv, sgz)
        d_og = jnp.sum(jnp.sum(don * xo, axis=1, keepdims=True), axis=0)
        do = _rms_bwd(don * og, xo, ro)
        tm, dmat, eg, edl, egl = cm["tm"], cm["dmat"], cm["eg"], cm["edl"], cm["egl"]
        dos, dsps, sts, tms, ks = _Split(do), _Split(dsp), _Split(state), _Split(tm), _Split(k)
        dvn = _bmm_tn(cm["attn"], dos) + _bmm(cm["kd"], dsps)
        dvns = _Split(dvn)
        dqe = _bmm_nt(dos, sts)
        ds_ref[...] = _bmm_tn(cm["qe"], dos) + dsp * egl[:, :, :1] - _bmm_tn(cm["w"], dvns)
        dattn = _bmm_nt(dos, vn)
        dkd = _bmm_nt(vn, dsps)
        dkd_kd = jnp.sum(dkd * cm["kd"], axis=-1, keepdims=True)
        dgl = (jnp.sum(jnp.sum(dsp * state, axis=-1, keepdims=True), axis=1, keepdims=True) * egl[:, :, :1]
               + jnp.sum(dkd_kd, axis=1, keepdims=True))
        dgc = jnp.sum(dqe * cm["qe"], axis=-1, keepdims=True) - dkd_kd
        dk = dkd * edl
        dq = dqe * eg
        dw = -_bmm_nt(dvns, sts)
        dws = _Split(dw)
        dp0 = dattn * dmat
        dd = jnp.where(cm["incl"], dattn * cm["p0"], 0.0)
        dp0s = _Split(dp0)
        dq = dq + _bmm(dp0s, ks)
        dk = dk + _bmm_tn(dp0s, q)
        dtm = _bmm_nt(dvns, cm["vb"]) + _bmm_nt(dws, cm["kbe"])
        dvb = _bmm_tn(tms, dvns)
        dkbe = _bmm_tn(tms, dws)
        dkb = dkbe * eg
        dgc = dgc + jnp.sum(dkbe * cm["kbe"], axis=-1, keepdims=True)
        dlow = jnp.where(cm["strict"], -_bmm_tn(tms, _bmm_nt(dtm, tms)), 0.0)
        dd = dd + dlow * cm["a0"]
        da0 = dlow * dmat
        da0s = _Split(da0)
        dkb = dkb + _bmm(da0s, ks)
        dk = dk + _bmm_tn(da0s, cm["kb"])
        ddd = dd * dmat
        ones = _Split(jnp.ones((heads, c, HEAD_DIM), F32), exact=True)
        dgc = dgc + jnp.sum(ddd, axis=-1, keepdims=True) - _bmm_tn(ddd, ones)[:, :, :1]
        dgc = dgc + jnp.where(_rows((c, 1)) == c - 1, dgl, 0.0)
        dg = _bmm_tn(cm["inclf"], jnp.broadcast_to(dgc, (heads, c, HEAD_DIM)))[:, :, :1]
        beta = cm["beta"]
        dk = dk + dkb * beta
        dbeta = jnp.sum(dkb * k, axis=-1, keepdims=True) + jnp.sum(dvb * v, axis=-1, keepdims=True)
        dv = dvb * beta
        db_col = dbeta * beta * (1.0 - beta)
        da_col = dg * cm["neg_ea"] * _sigmoid(cm["xg"])
        d_alog = jnp.sum(dg * cm["g"], axis=1, keepdims=True)
        d_dtb = jnp.sum(da_col, axis=1, keepdims=True)
        lane = lax.broadcasted_iota(jnp.int32, (c, LANES), 1)
        lane8 = lax.broadcasted_iota(jnp.int32, (8, LANES), 1)
        row8 = _rows((8, LANES))
        dab = jnp.zeros((c, LANES), F32)
        dpar = jnp.where(row8 == 2, d_og, 0.0)
        for h in range(heads):
            lo = h * HEAD_DIM
            dqkv_ref[:, lo:lo + HEAD_DIM] = dq[h]
            dqkv_ref[:, da + lo:da + lo + HEAD_DIM] = dk[h]
            dqkv_ref[:, 2 * da + lo:2 * da + lo + HEAD_DIM] = dv[h]
            dz_ref[:, lo:lo + HEAD_DIM] = dz[h]
            dab = dab + jnp.where(lane == h, da_col[h], 0.0) + jnp.where(lane == heads + h, db_col[h], 0.0)
            dpar = (dpar + jnp.where((row8 == 0) & (lane8 == h), d_alog[h], 0.0)
                    + jnp.where((row8 == 1) & (lane8 == h), d_dtb[h], 0.0))
        dab_ref[...] = dab
        dpar_ref[...] += dpar

    rev = lambda i: (n - 1 - i, 0)
    return pl.pallas_call(
        body, name="delta_bwd", grid=(n,),
        in_specs=[pl.BlockSpec((c, 3 * da), rev), pl.BlockSpec((c, da), rev), pl.BlockSpec((c, LANES), rev),
                  pl.BlockSpec((8, LANES), lambda i: (0, 0)),
                  pl.BlockSpec((1, heads, HEAD_DIM, HEAD_DIM), lambda i: (n - 1 - i, 0, 0, 0)),
                  pl.BlockSpec((1, 3, heads, c, c), lambda i: (n - 1 - i, 0, 0, 0, 0)),
                  pl.BlockSpec((1, 3, heads, c, HEAD_DIM), lambda i: (n - 1 - i, 0, 0, 0, 0)),
                  pl.BlockSpec((c, da), rev)],
        out_specs=[pl.BlockSpec((c, 3 * da), rev), pl.BlockSpec((c, da), rev), pl.BlockSpec((c, LANES), rev),
                   pl.BlockSpec((8, LANES), lambda i: (0, 0))],
        out_shape=[jax.ShapeDtypeStruct((t, 3 * da), F32), jax.ShapeDtypeStruct((t, da), F32),
                   jax.ShapeDtypeStruct((t, LANES), F32), jax.ShapeDtypeStruct((8, LANES), F32)],
        scratch_shapes=[pltpu.VMEM((heads, HEAD_DIM, HEAD_DIM), F32)],
        compiler_params=_params("arbitrary"))(qkv, z, ab, gpar, states, kept_c, kept_w, doa)


def _w_in_pieces(shard_cols, da, heads):
    a0, nab = 4 * da, 2 * heads
    d_in = 4 * shard_cols
    runs = [(0, a0, 0), (a0, a0 + nab, d_in - nab), (a0 + nab, d_in, a0)]
    pieces = []
    for j in range(4):
        lo, hi = j * shard_cols, (j + 1) * shard_cols
        for rlo, rhi, plo in runs:
            s, e = max(lo, rlo), min(hi, rhi)
            if s < e:
                pieces.append((j, s - lo, e - s, plo + (s - rlo)))
    return pieces, d_in - nab + LANES


def _w_in_pack(w4, da, heads):
    _, d, sc = w4.shape
    pieces, npk = _w_in_pieces(sc, da, heads)
    tr = _tile_rows(d, 256, SUBLANES_WIRE)

    def body(w_ref, o_ref):
        o_ref[:, npk - LANES:] = jnp.zeros((tr, LANES), o_ref.dtype)
        for j, lo, ln, dst in pieces:
            o_ref[:, dst:dst + ln] = w_ref[j, :, lo:lo + ln]

    return pl.pallas_call(
        body, name="w_in_pack", grid=(d // tr,),
        in_specs=[pl.BlockSpec((4, tr, sc), lambda i: (0, i, 0))],
        out_specs=pl.BlockSpec((tr, npk), lambda i: (i, 0)),
        out_shape=jax.ShapeDtypeStruct((d, npk), w4.dtype),
        compiler_params=_params("arbitrary"))(w4)


def _w_in_unpack(dwp, sc, da, heads):
    d, npk = dwp.shape
    pieces, _ = _w_in_pieces(sc, da, heads)
    tr = _tile_rows(d, 256)

    def body(g_ref, o_ref):
        for j, lo, ln, dst in pieces:
            o_ref[j, :, lo:lo + ln] = g_ref[:, dst:dst + ln]

    return pl.pallas_call(
        body, name="w_in_unpack", grid=(d // tr,),
        in_specs=[pl.BlockSpec((tr, npk), lambda i: (i, 0))],
        out_specs=pl.BlockSpec((4, tr, sc), lambda i: (0, i, 0)),
        out_shape=jax.ShapeDtypeStruct((4, d, sc), F32),
        compiler_params=_params("arbitrary"))(dwp)


def _block_diag(pool_w):
    g, gd, _ = pool_w.shape
    out = jnp.zeros((g * gd, g * gd), pool_w.dtype)
    for gi in range(g):
        out = lax.dynamic_update_slice(out, pool_w[gi], (gi * gd, gi * gd))
    return out


def _layer_dims(d):
    heads = (d // 2) // HEAD_DIM
    return heads, heads * HEAD_DIM, d // 4, d // 4


BIG = ("w_in", "w_gate", "w_up", "ple_proj", "w_out", "w_down", "ple_gate")
TRANSPOSED = ("w_gate", "w_up")


def _prepare_layer(small, li):
    d = small["norm1_g"].shape[1]
    heads, _, _, _ = _layer_dims(d)
    gpar = jnp.zeros((8, LANES), F32)
    gpar = gpar.at[0, :heads].set(small["a_log"][li]).at[1, :heads].set(small["dt_bias"][li]).at[2, :].set(small["onorm_g"][li])
    return dict(norm1_g=small["norm1_g"][li][None], conv_qkv=small["conv_qkv"][li], gpar=gpar, pool_bd=_block_diag(small["pool_w"][li]).astype(MM_DTYPE),
                pool_scale=small["pool_scale"][li][None], sconv_w=small["sconv_w"][li], norm2_g=small["norm2_g"][li][None])


def _layer_fwd(x0, p, gw, lw, tm, arrive):
    d = x0.shape[1]
    heads, da, dp, dc = _layer_dims(d)
    segs = (3 * da, da, dp, 3 * dc, LANES)
    lw["w_in_p"] = _w_in_pack(gw["w_in"], da, heads).astype(MM_DTYPE)
    qkv_pre, z, hp, cbcch, ab = _in_proj_fwd(x0, lw["norm1_g"], lw["w_in_p"], segs, tm)
    qkv = _qkv_conv_fwd(qkv_pre, lw["conv_qkv"], heads)
    oa, states, kept_c, kept_w = _delta_fwd(qkv, z, ab, lw["gpar"], heads)
    ob = _pool_fwd(hp, lw["pool_bd"], lw["pool_scale"], dp // POOL_GROUPS)
    oc = _sconv_fwd(cbcch, lw["sconv_w"])
    arrive("mixed", oa)
    x1, h2 = _out_proj_fwd(x0, (oa, ob, oc), gw["w_out"], lw["norm2_g"], tm)
    x2, gp, up = _ffn_fwd(x1, h2, gw["w_gate"], gw["w_up"], gw["w_down"], tm)
    arrive("ffn", x2)
    x3 = _ple_fwd(x2, p, gw["ple_gate"], gw["ple_proj"], tm)
    arrive("end", x3)
    saved = dict(x0=x0, qkv_pre=qkv_pre, z=z, hp=hp, cbcch=cbcch, ab=ab, qkv=qkv, states=states, kept_c=kept_c, kept_w=kept_w, oa=oa, ob=ob, oc=oc,
                 x1=x1, h2=h2, gp=gp, up=up, x2=x2)
    return x3, saved


def _layer_bwd(dx3, p, gw, lw, sv, tm, produced):
    def after_token(tok, arr):
        return arr if tok is None else arr + tok[0, 0]

    d = dx3.shape[1]
    heads, da, dp, dc = _layer_dims(d)
    segs = (3 * da, da, dp, dc, dc, dc, LANES)
    gd = dp // POOL_GROUPS
    dx2, d_ple_gate, d_ple_proj = _ple_bwd(dx3, sv["x2"], p, gw["ple_gate"], gw["ple_proj"], tm)
    dh2, d_w_gate, d_w_up, d_w_down = _ffn_bwd(dx2, sv["h2"], sv["gp"], sv["up"], gw["w_gate"], gw["w_up"], gw["w_down"],
                                               min(tm, 256))
    tok = produced("ffn", dict(w_gate=d_w_gate, w_up=d_w_up, ple_proj=d_ple_proj, w_down=d_w_down, ple_gate=d_ple_gate), dh2)
    dx1, doa, dob, doc, d_w_out, d_norm2 = _out_proj_bwd(dx2, dh2, sv["x1"], after_token(tok, lw["norm2_g"]),
                                                         (sv["oa"], sv["ob"], sv["oc"]), gw["w_out"], tm)
    dcb, dcc, dch, d_sconv = _sconv_bwd(sv["cbcch"], lw["sconv_w"], doc)
    dhp, d_pool_bd, d_pool_scale = _pool_bwd(sv["hp"], lw["pool_bd"], lw["pool_scale"], dob, gd)
    dqkv, dz, dab, dpar = _delta_bwd(sv["qkv"], sv["z"], sv["ab"], lw["gpar"], sv["states"], sv["kept_c"], sv["kept_w"], doa,
                                      heads)
    tok = produced("mixers", {}, dqkv)
    dqkv_pre, d_conv_qkv = _qkv_conv_bwd(sv["qkv_pre"], lw["conv_qkv"], dqkv, heads)
    dsegs = (dqkv_pre, dz, dhp, dcb, dcc, dch, dab)
    dx0, d_w_in_p, d_norm1 = _in_proj_bwd(sv["x0"], after_token(tok, lw["norm1_g"]), lw["w_in_p"], dsegs, dx1, segs, tm)
    per = LANES // gd
    bd = d_pool_bd.reshape(dp // LANES, per, gd, per, gd)
    d_pool_w = jnp.stack([bd[gi // per, gi % per, :, gi % per, :] for gi in range(POOL_GROUPS)])
    big = dict(w_in=_w_in_unpack(d_w_in_p, gw["w_in"].shape[2], da, heads), w_gate=d_w_gate, w_up=d_w_up,
               ple_proj=d_ple_proj, w_out=d_w_out, w_down=d_w_down, ple_gate=d_ple_gate)
    small = dict(norm1_g=d_norm1[0], conv_qkv=d_conv_qkv, a_log=dpar[0, :heads], dt_bias=dpar[1, :heads], onorm_g=dpar[2],
                 pool_w=d_pool_w, pool_scale=d_pool_scale[0], sconv_w=d_sconv, norm2_g=d_norm2[0])
    tok = produced("end", dict(w_in=big["w_in"], w_out=d_w_out), big["w_in"])
    return dx0, big, small, tok


def _local_step(x, p, target, gw, small, produced=None, arrive=None):
    t, d = x.shape
    depth = p.shape[0]
    tm = 512 if t % 512 == 0 else 128
    layers = [_prepare_layer(small, li) for li in range(depth)]
    saved = []
    h = x
    for li in range(depth):
        h, sv = _layer_fwd(h, p[li], gw[li], layers[li], tm,
                           (lambda stage, after, li=li: arrive(li, stage, after)) if arrive else (lambda stage, after: None))
        saved.append(sv)
    dx, loss, d_final = _loss_head(h, target, small["final_g"][None], tm)
    big, sm = [None] * depth, [None] * depth
    token = None
    for li in reversed(range(depth)):
        p_li = p[li] if token is None else p[li] + token[0, 0]
        dx, big[li], sm[li], token = _layer_bwd(
            dx, p_li, gw[li], layers[li], saved[li], tm,
            (lambda stage, grads, after, li=li: produced(li, stage, grads, after)) if produced else (lambda *a: None))
    small_grads = {n: jnp.stack([g[n] for g in sm]) for n in sm[0]}
    small_grads["final_g"] = d_final[0]
    return loss[0, 0], dx, big, small_grads


def _coords():
    return lax.axis_index("x"), lax.axis_index("y"), lax.axis_index("c")


def _other_chips(x, y):
    return [(1 - x, y), (x, 1 - y), (1 - x, 1 - y)]


def _place_shards(ws, me_idx):
    nt = len(ws)
    depth = ws[0].shape[0]

    def body(me_ref, *refs):
        for t, w_ref in enumerate(refs[:nt]):
            for li in range(depth):
                refs[nt + li * nt + t][...] = w_ref[li].astype(WIRE_DTYPE)

    outs = pl.pallas_call(
        body, name="place_shards",
        grid_spec=pltpu.PrefetchScalarGridSpec(
            num_scalar_prefetch=1, grid=(4,),
            in_specs=[pl.BlockSpec((depth, w.shape[1] // 4, w.shape[2]), lambda i, me_ref: (0, i, 0)) for w in ws],
            out_specs=[pl.BlockSpec((None, w.shape[1] // 4, w.shape[2]), lambda i, me_ref: (me_ref[0], i, 0))
                       for _ in range(depth) for w in ws]),
        out_shape=[jax.ShapeDtypeStruct((4,) + w.shape[1:], WIRE_DTYPE) for _ in range(depth) for w in ws],
        compiler_params=_params("arbitrary"))(me_idx, *ws)
    return [list(outs[li * nt:(li + 1) * nt]) for li in range(depth)]


def _half_block(ref, chip, pc):
    rh = ref.shape[1] // 2
    return ref.at[chip, pl.ds(pc * rh, rh)]


def _gather_copies(out_refs, send_sems, recv_sems, stage):
    nt = len(out_refs)
    x, y, c = _coords()
    pairs = []
    for j, (cx, cy) in enumerate(_other_chips(x, y)):
        for t in range(nt):
            sems = dict(send_sem=send_sems[j * nt + t], recv_sem=recv_sems[j * nt + t], device_id_type=MESH)
            if stage == 0:
                mine, theirs, to = _half_block(out_refs[t], 2 * x + y, c), _half_block(out_refs[t], 2 * cx + cy, c), (cx, cy, c)
            else:
                mine, theirs, to = (_half_block(out_refs[t], 2 * cx + cy, c), _half_block(out_refs[t], 2 * cx + cy, 1 - c),
                                    (x, y, 1 - c))
            pairs.append((pltpu.make_async_remote_copy(src_ref=mine, dst_ref=mine, device_id=to, **sems),
                          pltpu.make_async_remote_copy(src_ref=theirs, dst_ref=theirs, device_id=to, **sems)))
    return pairs


def _gather_call(name, arrs, wait_sems, after, stage):
    nt = len(arrs)
    nc = 3 * nt
    n_wait = len(wait_sems)
    n_new = 2 * nc if stage < 2 else 0
    arrs = [pltpu.with_memory_space_constraint(a, pltpu.HBM) for a in arrs]

    def body(*refs):
        a_refs = refs[:nt]
        waits = refs[nt:nt + n_wait]
        news = refs[nt + n_wait + 1:nt + n_wait + 1 + n_new]
        token = refs[-1]
        if stage > 0:
            for start, arrival in _gather_copies(a_refs, waits[:nc], waits[nc:], stage - 1):
                start.wait_send()
                arrival.wait_recv()
        if stage < 2:
            for start, _ in _gather_copies(a_refs, news[:nc], news[nc:], stage):
                start.start()
        token[...] = jnp.zeros_like(token)

    outs = pl.pallas_call(
        body, name=name,
        out_shape=(*[pltpu.SemaphoreType.DMA(())] * n_new, *[pltpu.HBM(a.shape, a.dtype) for a in arrs],
                   jax.ShapeDtypeStruct((8, LANES), F32)),
        in_specs=[HBM] * nt + [SEM] * n_wait + [ANY],
        out_specs=(*[SEM] * n_new, *[HBM] * nt, pl.BlockSpec(memory_space=pltpu.VMEM)),
        input_output_aliases={t: n_new + t for t in range(nt)},
        compiler_params=pltpu.CompilerParams(has_side_effects=pltpu.SideEffectType.DATAFLOW_SIDE_EFFECTING),
    )(*arrs, *wait_sems, after)
    return list(outs[:n_new]), list(outs[n_new:n_new + nt]), outs[-1]


def _add_my_halves(gs, others, c_idx):
    nt = len(gs)

    def body(c_ref, *refs):
        for g_ref, o_ref, out_ref in zip(refs[:nt], refs[nt:2 * nt], refs[2 * nt:]):
            out_ref[...] = (g_ref[...].astype(F32) + o_ref[...].astype(F32)).astype(out_ref.dtype)

    def half(g):
        return pl.BlockSpec((None, g.shape[1] // 2, g.shape[2]), lambda j, c_ref: (j, 0, 0))

    return pl.pallas_call(
        body, name="add_my_halves",
        grid_spec=pltpu.PrefetchScalarGridSpec(
            num_scalar_prefetch=1, grid=(4,),
            in_specs=[pl.BlockSpec((None, g.shape[1] // 2, g.shape[2]), lambda j, c_ref: (j, c_ref[0], 0)) for g in gs]
                     + [half(g) for g in gs],
            out_specs=[half(g) for g in gs]),
        out_shape=[jax.ShapeDtypeStruct((4, g.shape[1] // 2, g.shape[2]), WIRE_DTYPE) for g in gs],
        compiler_params=_params("arbitrary"))(c_idx, *gs, *others)


def _split_plan(kind, s_refs, l_refs):
    x, y, c = _coords()
    if kind == "devices":
        peers = [(x ^ ((k >> 2) & 1), y ^ ((k >> 1) & 1), c ^ (k & 1)) for k in range(1, 8)]
        return [(s, l.at[4 * x + 2 * y + c], peer) for peer in peers for s, l in zip(s_refs, l_refs)]
    if kind == "swap":
        return [(s.at[:, pl.ds((1 - c) * (s.shape[1] // 2), s.shape[1] // 2)], l, (x, y, 1 - c)) for s, l in zip(s_refs, l_refs)]
    return [(s.at[2 * cx + cy], l.at[j], (cx, cy, c)) for j, (cx, cy) in enumerate(_other_chips(x, y))
            for s, l in zip(s_refs, l_refs)]


def _split_landing(kind, a):
    if kind == "devices":
        return (8,) + a.shape
    return (a.shape[0], a.shape[1] // 2, a.shape[2]) if kind == "swap" else (3,) + a.shape[1:]


def _copies_start(name, kind, srcs, after=None):
    ns = len(srcs)
    n = {"swap": 1, "exchange": 3, "devices": 7}[kind] * ns
    srcs = [pltpu.with_memory_space_constraint(a, pltpu.HBM) for a in srcs]
    fresh = jnp.zeros if kind == "devices" else lax.empty
    lands = [pltpu.with_memory_space_constraint(fresh(_split_landing(kind, a), a.dtype), pltpu.HBM) for a in srcs]
    extra = [] if after is None else [after]

    def body(*refs):
        first_sem = 2 * ns + len(extra)
        sems, token = refs[first_sem:first_sem + 2 * n], refs[-1]
        for k, (src, dst, dev) in enumerate(_split_plan(kind, refs[:ns], refs[ns:2 * ns])):
            pltpu.make_async_remote_copy(src_ref=src, dst_ref=dst, send_sem=sems[k], recv_sem=sems[n + k], device_id=dev,
                                         device_id_type=MESH).start()
        token[...] = jnp.zeros_like(token)

    outs = pl.pallas_call(
        body, name=name,
        out_shape=(*[pltpu.SemaphoreType.DMA(())] * (2 * n), *[pltpu.HBM(a.shape, a.dtype) for a in srcs + lands],
                   jax.ShapeDtypeStruct((8, LANES), F32)),
        in_specs=[HBM] * (2 * ns) + [ANY] * len(extra),
        out_specs=(*[SEM] * (2 * n), *[HBM] * (2 * ns), pl.BlockSpec(memory_space=pltpu.VMEM)),
        input_output_aliases={t: 2 * n + t for t in range(2 * ns)},
        compiler_params=pltpu.CompilerParams(has_side_effects=pltpu.SideEffectType.DATAFLOW_SIDE_EFFECTING),
    )(*srcs, *lands, *extra)
    return list(outs[:2 * n]), list(outs[2 * n:2 * n + ns]), list(outs[2 * n + ns:2 * n + 2 * ns]), outs[-1]


def _copies_wait(name, kind, sems, srcs, lands, after):
    ns = len(srcs)
    n = len(sems) // 2

    def body(*refs):
        sem_refs = refs[2 * ns:2 * ns + 2 * n]
        for k, (src, dst, dev) in enumerate(_split_plan(kind, refs[:ns], refs[ns:2 * ns])):
            cp = pltpu.make_async_remote_copy(src_ref=src, dst_ref=dst, send_sem=sem_refs[k], recv_sem=sem_refs[n + k],
                                              device_id=dev, device_id_type=MESH)
            cp.wait_send()
            cp.wait_recv()

    outs = pl.pallas_call(
        body, name=name, out_shape=tuple(pltpu.HBM(a.shape, a.dtype) for a in srcs + lands),
        in_specs=[HBM] * (2 * ns) + [SEM] * (2 * n) + [ANY], out_specs=tuple([HBM] * (2 * ns)),
        input_output_aliases={t: t for t in range(2 * ns)},
        compiler_params=pltpu.CompilerParams(has_side_effects=pltpu.SideEffectType.DATAFLOW_SIDE_EFFECTING),
    )(*srcs, *lands, *sems, after)
    return list(outs[:ns]), list(outs[ns:])


def _sum_into(pairs, recvs, idx, li, depth, accs):
    nt = len(pairs)

    def body(idx_ref, *refs):
        for p_ref, r_ref, out_ref in zip(refs[:nt], refs[nt:2 * nt], refs[-nt:]):
            out_ref[...] = p_ref[...].astype(F32) + r_ref[0].astype(F32) + r_ref[1].astype(F32) + r_ref[2].astype(F32)

    in_specs = ([pl.BlockSpec((None, p.shape[1] // 2, p.shape[2]), lambda i, idx_ref: (idx_ref[0], i, 0)) for p in pairs]
                + [pl.BlockSpec((3, p.shape[1] // 2, p.shape[2]), lambda i, idx_ref: (0, i, 0)) for p in pairs])
    args = [idx, *pairs, *recvs]
    aliases = {}
    if accs[0] is not None:
        in_specs += [ANY] * nt
        args += list(accs)
        aliases = {1 + 2 * nt + t: t for t in range(nt)}
    return pl.pallas_call(
        body, name="sum_into",
        grid_spec=pltpu.PrefetchScalarGridSpec(
            num_scalar_prefetch=1, grid=(2,), in_specs=in_specs,
            out_specs=[pl.BlockSpec((None, p.shape[1] // 2, p.shape[2]), lambda i, idx_ref: (li, 2 * idx_ref[1] + i, 0))
                       for p in pairs]),
        out_shape=[jax.ShapeDtypeStruct((depth, 2 * p.shape[1], p.shape[2]), F32) for p in pairs],
        input_output_aliases=aliases, compiler_params=_params("arbitrary"))(*args)


def _sum_devices(own, land, me_dev):
    rows = own.shape[0]
    tr = _tile_rows(rows, 512)

    def body(me_ref, o_ref, l_ref, out_ref):
        acc = jnp.where(me_ref[0] == 0, o_ref[...], l_ref[0])
        for s in range(1, 8):
            acc = acc + jnp.where(me_ref[0] == s, o_ref[...], l_ref[s])
        out_ref[...] = acc

    return pl.pallas_call(
        body, name="sum_devices",
        grid_spec=pltpu.PrefetchScalarGridSpec(
            num_scalar_prefetch=1, grid=(rows // tr,),
            in_specs=[pl.BlockSpec((tr, LANES), lambda i, me_ref: (i, 0)), pl.BlockSpec((8, tr, LANES), lambda i, me_ref: (0, i, 0))],
            out_specs=pl.BlockSpec((tr, LANES), lambda i, me_ref: (i, 0))),
        out_shape=jax.ShapeDtypeStruct((rows, LANES), F32), compiler_params=_params("arbitrary"))(me_dev, own, land)


def _sibling_share(gs, li):
    nt = len(gs)

    def body(*refs):
        out_refs = refs[nt:2 * nt]
        send_sems, recv_sems = refs[2 * nt:]
        x, y, c = _coords()
        sends, recvs = [], []
        for t in range(nt):
            rh = out_refs[t].shape[1] // 2
            mine, theirs = out_refs[t].at[li, pl.ds(c * rh, rh)], out_refs[t].at[li, pl.ds((1 - c) * rh, rh)]
            sems = dict(send_sem=send_sems.at[t], recv_sem=recv_sems.at[t], device_id=(x, y, 1 - c), device_id_type=MESH)
            sends.append(pltpu.make_async_remote_copy(src_ref=mine, dst_ref=mine, **sems))
            recvs.append(pltpu.make_async_remote_copy(src_ref=theirs, dst_ref=theirs, **sems))
        for cp in sends:
            cp.start()
        for cp in recvs:
            cp.wait_recv()
        for cp in sends:
            cp.wait_send()

    return pl.pallas_call(
        body, name="sibling_share", out_shape=[jax.ShapeDtypeStruct(g.shape, g.dtype) for g in gs],
        in_specs=[ANY] * nt, out_specs=[ANY] * nt, input_output_aliases={t: t for t in range(nt)},
        scratch_shapes=[pltpu.SemaphoreType.DMA((nt,)), pltpu.SemaphoreType.DMA((nt,))])(*gs)


def _all_gather_devices(buf, after=None):
    extra = [] if after is None else [after]

    def body(b_ref, *rest):
        out_ref, send_sems, recv_sems, local_sem = rest[len(extra):]
        x, y, c = _coords()
        me = 4 * x + 2 * y + c
        mine = pltpu.make_async_copy(b_ref, out_ref.at[me], local_sem)
        mine.start()
        peers = []
        for k in range(1, 8):
            fx, fy, fc = (k >> 2) & 1, (k >> 1) & 1, k & 1
            peers.append((x ^ fx, y ^ fy, c ^ fc))
        sends = [pltpu.make_async_remote_copy(src_ref=b_ref, dst_ref=out_ref.at[me], send_sem=send_sems.at[k],
                                              recv_sem=recv_sems.at[k], device_id=peer, device_id_type=MESH)
                 for k, peer in enumerate(peers)]
        for cp in sends:
            cp.start()
        for k, (px, py, pc) in enumerate(peers):
            pltpu.make_async_remote_copy(src_ref=b_ref, dst_ref=out_ref.at[4 * px + 2 * py + pc], send_sem=send_sems.at[k],
                                         recv_sem=recv_sems.at[k], device_id=(px, py, pc), device_id_type=MESH).wait_recv()
        for cp in sends:
            cp.wait_send()
        mine.wait()

    return pl.pallas_call(
        body, name="all_gather_devices", out_shape=jax.ShapeDtypeStruct((8,) + buf.shape, buf.dtype),
        in_specs=[ANY] * (1 + len(extra)), out_specs=ANY,
        scratch_shapes=[pltpu.SemaphoreType.DMA((7,)), pltpu.SemaphoreType.DMA((7,)), pltpu.SemaphoreType.DMA(())])(buf, *extra)


SMALL_SHARDED = ("conv_qkv", "sconv_w")
REPLICATED = ("norm1_g", "a_log", "dt_bias", "onorm_g", "pool_w", "pool_scale", "norm2_g", "final_g")
ALL_WEIGHTS = ("norm1_g", "w_in", "conv_qkv", "a_log", "dt_bias", "onorm_g", "pool_w", "pool_scale", "sconv_w", "w_out",
               "norm2_g", "w_gate", "w_up", "w_down", "ple_proj", "ple_gate", "final_g")


def _pad_rows(flat, row_multiple):
    m = flat.shape[0]
    r = -(-m // (LANES * row_multiple)) * row_multiple
    return jnp.pad(flat, (0, r * LANES - m)).reshape(r, LANES)


def _adamw_math(w, g, m, v):
    c1 = 1.0 / (1.0 - ADAM_B1 ** ADAM_STEP)
    c2 = 1.0 / (1.0 - ADAM_B2 ** ADAM_STEP)
    nm = ADAM_B1 * m + (1.0 - ADAM_B1) * g
    nv = ADAM_B2 * v + (1.0 - ADAM_B2) * (g * g)
    return -ADAM_LR * ((nm * c1) / (jnp.sqrt(nv * c2) + ADAM_EPS) + ADAM_WD * w), nm, nv


def _adamw(w, g, m, v):
    shape = w.shape
    cols = shape[-1]
    rows = w.size // cols
    tr = _tile_rows(rows, 512)

    def body(w_ref, g_ref, m_ref, v_ref, d_ref, nm_ref, nv_ref, go_ref):
        gv = g_ref[...]
        d_ref[...], nm_ref[...], nv_ref[...] = _adamw_math(w_ref[...], gv, m_ref[...], v_ref[...])
        go_ref[...] = gv

    spec = pl.BlockSpec((tr, cols), lambda i: (i, 0))
    outs = pl.pallas_call(
        body, name="adamw", grid=(rows // tr,), in_specs=[spec] * 4, out_specs=[spec] * 4,
        out_shape=[jax.ShapeDtypeStruct((rows, cols), F32)] * 4,
        compiler_params=_params("arbitrary"))(*[a.reshape(rows, cols) for a in (w, g, m, v)])
    return tuple(o.reshape(shape) for o in outs)


def _adamw_together(ws, gs, ms, vs):
    k = len(ws)
    flat = [(w.size // w.shape[-1], w.shape[-1]) for w in ws]

    def body(*refs):
        for i in range(k):
            w_ref, g_ref, m_ref, v_ref = refs[4 * i:4 * i + 4]
            d_ref, nm_ref, nv_ref, go_ref = refs[4 * (k + i):4 * (k + i) + 4]
            gv = g_ref[...]
            d_ref[...], nm_ref[...], nv_ref[...] = _adamw_math(w_ref[...], gv, m_ref[...], v_ref[...])
            go_ref[...] = gv

    specs = [pl.BlockSpec(rc, lambda i: (0, 0)) for rc in flat for _ in range(4)]
    outs = pl.pallas_call(
        body, name="adamw_together", grid=(1,), in_specs=specs, out_specs=specs,
        out_shape=[jax.ShapeDtypeStruct(rc, F32) for rc in flat for _ in range(4)],
        compiler_params=_params("arbitrary"))(*[a.reshape(rc) for rc, four in zip(flat, zip(ws, gs, ms, vs)) for a in four])
    return [tuple(o.reshape(w.shape) for o in outs[4 * i:4 * i + 4]) for i, w in enumerate(ws)]


def kernel(x, p, norm1_g, w_in, conv_qkv, a_log, dt_bias, onorm_g, pool_w, pool_scale, sconv_w, w_out, norm2_g, w_gate, w_up, w_down, ple_proj, ple_gate, final_g, loss_target, m_norm1_g, m_w_in, m_conv_qkv, m_a_log, m_dt_bias, m_onorm_g, m_pool_w, m_pool_scale, m_sconv_w, m_w_out, m_norm2_g, m_w_gate, m_w_up, m_w_down, m_ple_proj, m_ple_gate, m_final_g, v_norm1_g, v_w_in, v_conv_qkv, v_a_log, v_dt_bias, v_onorm_g, v_pool_w, v_pool_scale, v_sconv_w, v_w_out, v_norm2_g, v_w_gate, v_w_up, v_w_down, v_ple_proj, v_ple_gate, v_final_g):
    weights = dict(zip(ALL_WEIGHTS, (norm1_g, w_in, conv_qkv, a_log, dt_bias, onorm_g, pool_w, pool_scale, sconv_w, w_out,
                                     norm2_g, w_gate, w_up, w_down, ple_proj, ple_gate, final_g)))
    mom_m = dict(zip(ALL_WEIGHTS, (m_norm1_g, m_w_in, m_conv_qkv, m_a_log, m_dt_bias, m_onorm_g, m_pool_w, m_pool_scale,
                                   m_sconv_w, m_w_out, m_norm2_g, m_w_gate, m_w_up, m_w_down, m_ple_proj, m_ple_gate, m_final_g)))
    mom_v = dict(zip(ALL_WEIGHTS, (v_norm1_g, v_w_in, v_conv_qkv, v_a_log, v_dt_bias, v_onorm_g, v_pool_w, v_pool_scale,
                                   v_sconv_w, v_w_out, v_norm2_g, v_w_gate, v_w_up, v_w_down, v_ple_proj, v_ple_gate, v_final_g)))
    for n in TRANSPOSED:
        weights[n], mom_m[n], mom_v[n] = (jnp.swapaxes(a[n], 1, 2) for a in (weights, mom_m, mom_v))
    c_idx = lax.axis_index("c").astype(jnp.int32).reshape(1)
    chip = (2 * lax.axis_index("x") + lax.axis_index("y")).astype(jnp.int32)
    me_idx = chip.reshape(1)
    idx = jnp.stack([chip, lax.axis_index("c").astype(jnp.int32)])
    depth = p.shape[0]

    small = {n: weights[n] for n in REPLICATED}
    sflat = _pad_rows(jnp.concatenate([weights[n].reshape(-1) for n in SMALL_SHARDED]), 8)
    sgath8 = _all_gather_devices(sflat)
    placed_in = _place_shards([weights["w_in"]], me_idx)
    sems, arrs, _ = _gather_call("gather_first_start", placed_in[0], [], sgath8, 0)
    placed_rest = _place_shards([weights[n] for n in BIG[1:]], me_idx)
    placed = [placed_in[li] + placed_rest[li] for li in range(depth)]
    sems, arrs, _ = _gather_call("gather_first_forward", arrs, sems, placed_rest[0][0], 1)
    _, arrs, token = _gather_call("gather_first_finish", arrs, sems, placed_rest[0][0], 2)
    gw = [dict() for _ in range(depth)]
    gw[0]["w_in"] = arrs[0]
    early = ("w_in", "w_out")
    late = tuple(n for n in BIG if n not in early)
    groups = [dict(li=0, names=BIG[1:], forward=(0, "mixed"), finish=(0, "mixed"))]
    for li in range(1, depth):
        groups.append(dict(li=li, names=early, forward=(li - 1, "ffn"), finish=(li - 1, "end")))
        groups.append(dict(li=li, names=late, forward=(li, "mixed"), finish=(li, "mixed")))
    def arrive(li, stage, after):
        for k, g in enumerate(groups):
            if g["forward"] == (li, stage):
                g["sems"], g["arrs"], _ = _gather_call("gather_forward_%d" % k, g["arrs"], g["sems"], after, 1)
            if g["finish"] == (li, stage):
                _, g["arrs"], _ = _gather_call("gather_finish_%d" % k, g["arrs"], g["sems"], after, 2)
                gw[g["li"]].update(zip(g["names"], g["arrs"]))

    sgath = sgath8[0::2].reshape(4, -1)
    off = 0
    for n in SMALL_SHARDED:
        shp = weights[n].shape
        part = sgath[:, off:off + weights[n].size].reshape((4,) + shp)
        small[n] = jnp.moveaxis(part, 0, -2).reshape(shp[:-1] + (4 * shp[-1],))
        off += weights[n].size
    for k, g in enumerate(groups):
        arrs = [placed[g["li"]][BIG.index(n)] for n in g["names"]]
        g["sems"], g["arrs"], token = _gather_call("gather_start_%d" % k, arrs, [], token, 0)

    small["norm1_g"] = small["norm1_g"] + token[0, 0]

    pending = []
    last_token = [None]

    def advance(g, after):
        if g["stage"] == 0:
            gs, others = _copies_wait("swap_wait_" + g["tag"], "swap", *g["handle"], after)
            g["handle"] = _copies_start("exchange_start_" + g["tag"], "exchange", _add_my_halves(gs, others, c_idx))
            g["stage"] = 1
            return g["handle"][3]
        return None

    held = {}

    def produced(li, stage, grads, after):
        token = None
        for g in pending:
            token = advance(g, after) if g["stage"] == 0 else token
        if li > 0 and stage != "end":
            held.update(grads)
            grads = {}
        elif li > 0:
            grads = {**held, **grads}
            held.clear()
        if grads:
            names = [n for n in BIG if n in grads]
            handle = _copies_start("swap_start_%d%s" % (li, stage), "swap", [grads[n] for n in names], token)
            pending.append(dict(li=li, names=names, tag="%d%s" % (li, stage), stage=0, handle=handle[:3]))
            token = handle[3]
        last_token[0] = last_token[0] if token is None else token
        return token

    loss_local, dx, _, small_grads = _local_step(x[0], p[:, 0], loss_target[0], gw, small, produced, arrive)
    rnames = REPLICATED + SMALL_SHARDED
    rflat = _pad_rows(jnp.concatenate([small_grads[n].reshape(-1) for n in rnames] + [loss_local.reshape(1)]), 8)
    small_handle = _copies_start("small_start", "devices", [rflat], last_token[0])
    accs, big_outs = {}, {}

    def finish(g, after):
        pairs, recvs = _copies_wait("exchange_wait_" + g["tag"], "exchange", *g["handle"][:3], after)
        summed = _sum_into(pairs, recvs, idx, g["li"], depth, [accs.get(n) for n in g["names"]])
        accs.update(zip(g["names"], _sibling_share(summed, g["li"])))
        return accs[g["names"][-1]]

    def update(names):
        for n in names:
            big_outs[n] = _adamw(weights[n], accs[n], mom_m[n], mom_v[n])
        return jnp.stack([big_outs[n][0].reshape(-1)[0] for n in names])

    done = finish(pending[0], small_handle[3])
    done = advance(pending[-1], done)
    for g in pending[1:-1]:
        done = finish(g, done)
    last = pending[-1]["names"]
    done = update([n for n in BIG if n not in last])
    finish(pending[-1], done)
    done = update(last)


    gshard = {}
    (own,), (land,) = _copies_wait("small_wait", "devices", *small_handle[:3], done)
    me_dev = (2 * chip + lax.axis_index("c").astype(jnp.int32)).reshape(1)
    rsum = _sum_devices(own, land, me_dev).reshape(-1)
    off = 0
    for n in rnames:
        whole = rsum[off:off + small_grads[n].size].reshape(small_grads[n].shape)
        off += small_grads[n].size
        if n in SMALL_SHARDED:
            cols = weights[n].shape[-1]
            whole = lax.dynamic_slice_in_dim(whole, chip * cols, cols, axis=whole.ndim - 1)
        gshard[n] = whole

    loss = rsum[off]

    deltas, new_m, new_v, grad_out = {}, {}, {}, {}
    others = [n for n in ALL_WEIGHTS if n not in BIG]
    big_outs.update(zip(others, _adamw_together(*[[a[n] for n in others] for a in (weights, gshard, mom_m, mom_v)])))
    for n in ALL_WEIGHTS:
        deltas[n], new_m[n], new_v[n], grad_out[n] = big_outs[n]
    for n in TRANSPOSED:
        deltas[n], new_m[n], new_v[n], grad_out[n] = (jnp.swapaxes(a[n], 1, 2) for a in (deltas, new_m, new_v, grad_out))
    return (loss, dx[None], *[grad_out[n] for n in ALL_WEIGHTS], *[deltas[n] for n in ALL_WEIGHTS],
            *[new_m[n] for n in ALL_WEIGHTS], *[new_v[n] for n in ALL_WEIGHTS])
```

```python
import jax
import jax.numpy as jnp
from jax import lax
from jax.experimental import pallas as pl
from jax.experimental.pallas import tpu as pltpu

F32 = jnp.float32
MM_DTYPE = jnp.bfloat16
WIRE_DTYPE = jnp.bfloat16
EPS = 1e-6
HEAD_DIM = 128
CHUNK = 64
QKV_CONV_WIDTH = 4
SCONV_WIDTH = 3
POOL_GROUPS = 4
LANES = 128
SUBLANES_WIRE = 16
VMEM_LIMIT_BYTES = 56 * 1024 * 1024
ADAM_LR, ADAM_B1, ADAM_B2, ADAM_EPS, ADAM_WD, ADAM_STEP = 0.001, 0.9, 0.999, 1e-08, 0.01, 10
MESH = pl.DeviceIdType.MESH
ANY = pl.BlockSpec(memory_space=pl.ANY)
HBM = pl.BlockSpec(memory_space=pltpu.HBM)
SEM = pl.BlockSpec(memory_space=pltpu.SEMAPHORE)


def _params(*sem):
    return pltpu.CompilerParams(vmem_limit_bytes=VMEM_LIMIT_BYTES, dimension_semantics=sem if sem else None)


def _mm(a, b):
    return jnp.dot(a.astype(MM_DTYPE), b.astype(MM_DTYPE), preferred_element_type=F32)


def _mm_nt(a, b):
    return lax.dot_general(a.astype(MM_DTYPE), b.astype(MM_DTYPE), (((1,), (1,)), ((), ())), preferred_element_type=F32)


def _mm_tn(a, b):
    return lax.dot_general(a.astype(MM_DTYPE), b.astype(MM_DTYPE), (((0,), (0,)), ((), ())), preferred_element_type=F32)


def _sigmoid(x):
    return 1.0 / (1.0 + jnp.exp(-x))


def _dsilu(x, s):
    return s * (1.0 + x * (1.0 - s))


def _rows(shape):
    return lax.broadcasted_iota(jnp.int32, shape, 0)


def _shift_down(x, s):
    if s == 0:
        return x
    return jnp.where(_rows(x.shape) >= s, pltpu.roll(x, s, 0), 0.0)


def _shift_up(x, s):
    if s == 0:
        return x
    t = x.shape[0]
    return jnp.where(_rows(x.shape) < t - s, pltpu.roll(x, t - s, 0), 0.0)


def _rms_fwd(x):
    r = lax.rsqrt(jnp.mean(x * x, axis=-1, keepdims=True) + EPS)
    return x * r, r


def _rms_bwd(dxn, xn, r):
    return r * (dxn - xn * jnp.mean(dxn * xn, axis=-1, keepdims=True))


def _tile_rows(n, cap, mult=8):
    best = None
    for d in range(mult, min(n, cap) + 1, mult):
        if n % d == 0:
            best = d
    return best if best is not None else n


def _in_proj_fwd(x, g1, wp, segs, tm):
    t, d = x.shape
    npk = wp.shape[1]

    def body(x_ref, g_ref, w_ref, *o_refs):
        xn, _ = _rms_fwd(x_ref[...])
        h = (xn * g_ref[...]).astype(w_ref.dtype)
        off = 0
        for o_ref, wd in zip(o_refs, segs):
            o_ref[...] = jnp.dot(h, w_ref[:, off:off + wd], preferred_element_type=F32)
            off += wd

    return pl.pallas_call(
        body, name="in_proj_fwd", grid=(t // tm,),
        in_specs=[pl.BlockSpec((tm, d), lambda i: (i, 0)), pl.BlockSpec((1, d), lambda i: (0, 0)),
                  pl.BlockSpec((d, npk), lambda i: (0, 0))],
        out_specs=[pl.BlockSpec((tm, wd), lambda i: (i, 0)) for wd in segs],
        out_shape=[jax.ShapeDtypeStruct((t, wd), F32) for wd in segs],
        compiler_params=_params("arbitrary"))(x, g1, wp)


def _in_proj_bwd(x, g1, wp, dsegs, dx_res, segs, tm):
    t, d = x.shape
    npk = wp.shape[1]
    nseg = len(segs)

    def body(x_ref, g_ref, w_ref, *rest):
        ds_refs = rest[:nseg]
        dxr_ref, dx_ref, dw_ref, dg_ref = rest[nseg:]
        i = pl.program_id(0)

        @pl.when(i == 0)
        def _():
            dw_ref[...] = jnp.zeros_like(dw_ref)
            dg_ref[...] = jnp.zeros_like(dg_ref)

        xn, r = _rms_fwd(x_ref[...])
        g = g_ref[...]
        h = (xn * g).astype(w_ref.dtype)
        dcat = jnp.concatenate([ds_ref[...].astype(w_ref.dtype) for ds_ref in ds_refs], axis=1)
        dh = lax.dot_general(dcat, w_ref[...], (((1,), (1,)), ((), ())), preferred_element_type=F32)
        dw_ref[...] += lax.dot_general(h, dcat, (((0,), (0,)), ((), ())), preferred_element_type=F32)
        dg_ref[...] += jnp.sum(dh * xn, axis=0, keepdims=True)
        dx_ref[...] = dxr_ref[...] + _rms_bwd(dh * g, xn, r)

    return pl.pallas_call(
        body, name="in_proj_bwd", grid=(t // tm,),
        in_specs=[pl.BlockSpec((tm, d), lambda i: (i, 0)), pl.BlockSpec((1, d), lambda i: (0, 0)),
                  pl.BlockSpec((d, npk), lambda i: (0, 0))]
                 + [pl.BlockSpec((tm, wd), lambda i: (i, 0)) for wd in segs]
                 + [pl.BlockSpec((tm, d), lambda i: (i, 0))],
        out_specs=[pl.BlockSpec((tm, d), lambda i: (i, 0)), pl.BlockSpec((d, npk), lambda i: (0, 0)),
                   pl.BlockSpec((1, d), lambda i: (0, 0))],
        out_shape=[jax.ShapeDtypeStruct((t, d), F32), jax.ShapeDtypeStruct((d, npk), F32),
                   jax.ShapeDtypeStruct((1, d), F32)],
        compiler_params=_params("arbitrary"))(x, g1, wp, *dsegs, dx_res)


def _out_proj_fwd(x0, mix, wo, g2, tm):
    t, d = x0.shape
    dq = wo.shape[1]
    widths = [m.shape[1] for m in mix]

    def body(x_ref, *rest):
        m_refs = rest[:len(mix)]
        w_ref, g_ref, x1_ref, h2_ref = rest[len(mix):]
        acc = x_ref[...]
        off = 0
        for m_ref, wd in zip(m_refs, widths):
            for k in range(wd // dq):
                acc = acc + jnp.dot(m_ref[:, k * dq:(k + 1) * dq].astype(w_ref.dtype), w_ref[off // dq + k],
                                    preferred_element_type=F32)
            off += wd
        x1_ref[...] = acc
        xn, _ = _rms_fwd(acc)
        h2_ref[...] = (xn * g_ref[...]).astype(h2_ref.dtype)

    return pl.pallas_call(
        body, name="out_proj_fwd", grid=(t // tm,),
        in_specs=[pl.BlockSpec((tm, d), lambda i: (i, 0))]
                 + [pl.BlockSpec((tm, wd), lambda i: (i, 0)) for wd in widths]
                 + [pl.BlockSpec((4, dq, d), lambda i: (0, 0, 0)), pl.BlockSpec((1, d), lambda i: (0, 0))],
        out_specs=[pl.BlockSpec((tm, d), lambda i: (i, 0)), pl.BlockSpec((tm, d), lambda i: (i, 0))],
        out_shape=[jax.ShapeDtypeStruct((t, d), F32), jax.ShapeDtypeStruct((t, d), MM_DTYPE)],
        compiler_params=_params("arbitrary"))(x0, *mix, wo, g2)


def _out_proj_bwd(dx2, dh2, x1, g2, mix, wo, tm):
    t, d = x1.shape
    dq = wo.shape[1]
    widths = [m.shape[1] for m in mix]
    nm = len(mix)

    def body(dx2_ref, dh2_ref, x1_ref, g_ref, *rest):
        m_refs = rest[:nm]
        w_ref = rest[nm]
        dx1_ref = rest[nm + 1]
        dm_refs = rest[nm + 2:nm + 2 + nm]
        dw_ref, dg_ref = rest[nm + 2 + nm:]
        i = pl.program_id(0)

        @pl.when(i == 0)
        def _():
            dw_ref[...] = jnp.zeros_like(dw_ref)
            dg_ref[...] = jnp.zeros_like(dg_ref)

        xn, r = _rms_fwd(x1_ref[...])
        dh2v = dh2_ref[...]
        dg_ref[...] += jnp.sum(dh2v * xn, axis=0, keepdims=True)
        dx1 = dx2_ref[...] + _rms_bwd(dh2v * g_ref[...], xn, r)
        dx1_ref[...] = dx1
        dx1c = dx1.astype(w_ref.dtype)
        off = 0
        for m_ref, dm_ref, wd in zip(m_refs, dm_refs, widths):
            for k in range(wd // dq):
                j = off // dq + k
                cols = slice(k * dq, (k + 1) * dq)
                dm_ref[:, cols] = lax.dot_general(dx1c, w_ref[j], (((1,), (1,)), ((), ())), preferred_element_type=F32)
                dw_ref[j] += lax.dot_general(m_ref[:, cols].astype(w_ref.dtype), dx1c, (((0,), (0,)), ((), ())),
                                             preferred_element_type=F32)
            off += wd

    tile = lambda wd: pl.BlockSpec((tm, wd), lambda i: (i, 0))
    return pl.pallas_call(
        body, name="out_proj_bwd", grid=(t // tm,),
        in_specs=[tile(d), tile(d), tile(d), pl.BlockSpec((1, d), lambda i: (0, 0))]
                 + [tile(wd) for wd in widths] + [pl.BlockSpec((4, dq, d), lambda i: (0, 0, 0))],
        out_specs=[tile(d)] + [tile(wd) for wd in widths]
                  + [pl.BlockSpec((4, dq, d), lambda i: (0, 0, 0)), pl.BlockSpec((1, d), lambda i: (0, 0))],
        out_shape=[jax.ShapeDtypeStruct((t, d), F32)] + [jax.ShapeDtypeStruct((t, wd), F32) for wd in widths]
                  + [jax.ShapeDtypeStruct((4, dq, d), F32), jax.ShapeDtypeStruct((1, d), F32)],
        compiler_params=_params("arbitrary"))(dx2, dh2, x1, g2, *mix, wo)


def _ffn_fwd(x1, h2, wg, wu, wd, tm):
    t, d = x1.shape
    fs = wg.shape[1]

    def body(x1_ref, h2_ref, wg_ref, wu_ref, wd_ref, x2_ref, gp_ref, up_ref):
        @pl.when(pl.program_id(1) == 0)
        def _():
            x2_ref[...] = x1_ref[...]

        h = h2_ref[...]
        nt = (((1,), (1,)), ((), ()))
        gp = lax.dot_general(h, wg_ref[...], nt, preferred_element_type=F32)
        up = lax.dot_general(h, wu_ref[...], nt, preferred_element_type=F32)
        gp_ref[...] = gp
        up_ref[...] = up
        ff = gp * _sigmoid(gp) * up
        x2_ref[...] += jnp.dot(ff.astype(wd_ref.dtype), wd_ref[...], preferred_element_type=F32)

    return pl.pallas_call(
        body, name="ffn_fwd", grid=(t // tm, 4),
        in_specs=[pl.BlockSpec((tm, d), lambda i, j: (i, 0)), pl.BlockSpec((tm, d), lambda i, j: (i, 0)),
                  pl.BlockSpec((None, fs, d), lambda i, j: (j, 0, 0)),
                  pl.BlockSpec((None, fs, d), lambda i, j: (j, 0, 0)),
                  pl.BlockSpec((None, fs, d), lambda i, j: (j, 0, 0))],
        out_specs=[pl.BlockSpec((tm, d), lambda i, j: (i, 0)), pl.BlockSpec((None, tm, fs), lambda i, j: (j, i, 0)),
                   pl.BlockSpec((None, tm, fs), lambda i, j: (j, i, 0))],
        out_shape=[jax.ShapeDtypeStruct((t, d), F32), jax.ShapeDtypeStruct((4, t, fs), F32),
                   jax.ShapeDtypeStruct((4, t, fs), F32)],
        compiler_params=_params("arbitrary", "arbitrary"))(x1, h2, wg, wu, wd)


def _ffn_bwd(dx2, h2, gp, up, wg, wu, wd, tm):
    t, d = dx2.shape
    fs = wg.shape[1]

    def body(dx2_ref, h2_ref, gp_ref, up_ref, wg_ref, wu_ref, wd_ref, dh2_ref, dwg_ref, dwu_ref, dwd_ref):
        j, i = pl.program_id(0), pl.program_id(1)

        @pl.when(i == 0)
        def _():
            dwg_ref[...] = jnp.zeros_like(dwg_ref)
            dwu_ref[...] = jnp.zeros_like(dwu_ref)
            dwd_ref[...] = jnp.zeros_like(dwd_ref)

        cdt = wg_ref.dtype
        h = h2_ref[...]
        gpv, upv = gp_ref[...], up_ref[...]
        s = _sigmoid(gpv)
        silu = gpv * s
        dx2c = dx2_ref[...].astype(cdt)
        dff = lax.dot_general(dx2c, wd_ref[...], (((1,), (1,)), ((), ())), preferred_element_type=F32)
        dwd_ref[...] += lax.dot_general((silu * upv).astype(cdt), dx2c, (((0,), (0,)), ((), ())), preferred_element_type=F32)
        dup = (dff * silu).astype(cdt)
        dgp = (dff * upv * _dsilu(gpv, s)).astype(cdt)
        dwg_ref[...] += lax.dot_general(dgp, h, (((0,), (0,)), ((), ())), preferred_element_type=F32)
        dwu_ref[...] += lax.dot_general(dup, h, (((0,), (0,)), ((), ())), preferred_element_type=F32)
        dh = (jnp.dot(dgp, wg_ref[...], preferred_element_type=F32) + jnp.dot(dup, wu_ref[...], preferred_element_type=F32))
        rows = pl.ds(pl.multiple_of(i * tm, tm), tm)

        @pl.when(j == 0)
        def _():
            dh2_ref[rows, :] = dh

        @pl.when(j != 0)
        def _():
            dh2_ref[rows, :] += dh

    return pl.pallas_call(
        body, name="ffn_bwd", grid=(4, t // tm),
        in_specs=[pl.BlockSpec((tm, d), lambda j, i: (i, 0)), pl.BlockSpec((tm, d), lambda j, i: (i, 0)),
                  pl.BlockSpec((None, tm, fs), lambda j, i: (j, i, 0)), pl.BlockSpec((None, tm, fs), lambda j, i: (j, i, 0)),
                  pl.BlockSpec((None, fs, d), lambda j, i: (j, 0, 0)),
                  pl.BlockSpec((None, fs, d), lambda j, i: (j, 0, 0)),
                  pl.BlockSpec((None, fs, d), lambda j, i: (j, 0, 0))],
        out_specs=[pl.BlockSpec((t, d), lambda j, i: (0, 0)), pl.BlockSpec((None, fs, d), lambda j, i: (j, 0, 0)),
                   pl.BlockSpec((None, fs, d), lambda j, i: (j, 0, 0)), pl.BlockSpec((None, fs, d), lambda j, i: (j, 0, 0))],
        out_shape=[jax.ShapeDtypeStruct((t, d), F32)] + [jax.ShapeDtypeStruct((4, fs, d), F32)] * 3,
        compiler_params=_params("arbitrary", "arbitrary"))(dx2, h2, gp, up, wg, wu, wd)


def _ple_fwd(x2, p, wpg, wpp, tm):
    t, d = x2.shape
    q = p.shape[1]
    dq = d // 4

    def body(x_ref, p_ref, wg_ref, wp_ref, o_ref):
        xv = x_ref[...]
        xc = xv.astype(wg_ref.dtype)
        pc = p_ref[...].astype(wp_ref.dtype)
        pre = jnp.dot(xc[:, :dq], wg_ref[0], preferred_element_type=F32)
        for j in range(1, 4):
            pre = pre + jnp.dot(xc[:, j * dq:(j + 1) * dq], wg_ref[j], preferred_element_type=F32)
        gate = _sigmoid(pre)
        for j in range(4):
            cols = slice(j * dq, (j + 1) * dq)
            o_ref[:, cols] = xv[:, cols] + gate[:, cols] * jnp.dot(pc, wp_ref[j], preferred_element_type=F32)

    return pl.pallas_call(
        body, name="ple_fwd", grid=(t // tm,),
        in_specs=[pl.BlockSpec((tm, d), lambda i: (i, 0)), pl.BlockSpec((tm, q), lambda i: (i, 0)),
                  pl.BlockSpec((4, dq, d), lambda i: (0, 0, 0)),
                  pl.BlockSpec((4, q, dq), lambda i: (0, 0, 0))],
        out_specs=pl.BlockSpec((tm, d), lambda i: (i, 0)),
        out_shape=jax.ShapeDtypeStruct((t, d), F32),
        compiler_params=_params("arbitrary"))(x2, p, wpg, wpp)


def _ple_bwd(dx3, x2, p, wpg, wpp, tm):
    t, d = x2.shape
    q = p.shape[1]
    dq = d // 4

    def body(dx3_ref, x_ref, p_ref, wg_ref, wp_ref, dx2_ref, dwg_ref, dwp_ref):
        @pl.when(pl.program_id(0) == 0)
        def _():
            dwg_ref[...] = jnp.zeros_like(dwg_ref)
            dwp_ref[...] = jnp.zeros_like(dwp_ref)

        cdt = wg_ref.dtype
        xc = x_ref[...].astype(cdt)
        pc = p_ref[...].astype(cdt)
        pre = jnp.dot(xc[:, :dq], wg_ref[0], preferred_element_type=F32)
        for j in range(1, 4):
            pre = pre + jnp.dot(xc[:, j * dq:(j + 1) * dq], wg_ref[j], preferred_element_type=F32)
        gate = _sigmoid(pre)
        dx3v = dx3_ref[...]
        dpp = (dx3v * gate).astype(cdt)
        dgate = dx3v * gate * (1.0 - gate)
        dpre_parts = []
        for j in range(4):
            cols = slice(j * dq, (j + 1) * dq)
            pp_j = jnp.dot(pc, wp_ref[j], preferred_element_type=F32)
            dpre_parts.append((dgate[:, cols] * pp_j).astype(cdt))
            dwp_ref[j] += lax.dot_general(pc, dpp[:, cols], (((0,), (0,)), ((), ())), preferred_element_type=F32)
        dpre = jnp.concatenate(dpre_parts, axis=1)
        for j in range(4):
            cols = slice(j * dq, (j + 1) * dq)
            dwg_ref[j] += lax.dot_general(xc[:, cols], dpre, (((0,), (0,)), ((), ())), preferred_element_type=F32)
            dx2_ref[:, cols] = dx3v[:, cols] + lax.dot_general(dpre, wg_ref[j], (((1,), (1,)), ((), ())),
                                                               preferred_element_type=F32)

    return pl.pallas_call(
        body, name="ple_bwd", grid=(t // tm,),
        in_specs=[pl.BlockSpec((tm, d), lambda i: (i, 0)), pl.BlockSpec((tm, d), lambda i: (i, 0)),
                  pl.BlockSpec((tm, q), lambda i: (i, 0)), pl.BlockSpec((4, dq, d), lambda i: (0, 0, 0)),
                  pl.BlockSpec((4, q, dq), lambda i: (0, 0, 0))],
        out_specs=[pl.BlockSpec((tm, d), lambda i: (i, 0)), pl.BlockSpec((4, dq, d), lambda i: (0, 0, 0)),
                   pl.BlockSpec((4, q, dq), lambda i: (0, 0, 0))],
        out_shape=[jax.ShapeDtypeStruct((t, d), F32), jax.ShapeDtypeStruct((4, dq, d), F32),
                   jax.ShapeDtypeStruct((4, q, dq), F32)],
        compiler_params=_params("arbitrary"))(dx3, x2, p, wpg, wpp)


def _loss_head(x, target, fg, tm):
    t, d = x.shape

    def body(x_ref, t_ref, g_ref, dx_ref, loss_ref, dg_ref):
        @pl.when(pl.program_id(0) == 0)
        def _():
            loss_ref[...] = jnp.zeros_like(loss_ref)
            dg_ref[...] = jnp.zeros_like(dg_ref)

        xn, r = _rms_fwd(x_ref[...])
        g = g_ref[...]
        err = xn * g - t_ref[...]
        loss_ref[...] += 0.5 * jnp.sum(jnp.sum(err * err, axis=-1, keepdims=True) / d, axis=0, keepdims=True)
        dy = err / d
        dg_ref[...] += jnp.sum(dy * xn, axis=0, keepdims=True)
        dx_ref[...] = _rms_bwd(dy * g, xn, r)

    return pl.pallas_call(
        body, name="loss_head", grid=(t // tm,),
        in_specs=[pl.BlockSpec((tm, d), lambda i: (i, 0)), pl.BlockSpec((tm, d), lambda i: (i, 0)),
                  pl.BlockSpec((1, d), lambda i: (0, 0))],
        out_specs=[pl.BlockSpec((tm, d), lambda i: (i, 0)), pl.BlockSpec((1, 1), lambda i: (0, 0)),
                   pl.BlockSpec((1, d), lambda i: (0, 0))],
        out_shape=[jax.ShapeDtypeStruct((t, d), F32), jax.ShapeDtypeStruct((1, 1), F32),
                   jax.ShapeDtypeStruct((1, d), F32)],
        compiler_params=_params("arbitrary"))(x, target, fg)


def _qkv_conv_act(xv, w, j, heads):
    k = QKV_CONV_WIDTH
    y = w[k - 1:k] * xv
    for s in range(1, k):
        y = y + w[k - 1 - s:k - s] * _shift_down(xv, s)
    sg = _sigmoid(y)
    s_act = y * sg
    nrm = lax.rsqrt(jnp.sum(s_act * s_act, axis=-1, keepdims=True) + EPS)
    scale = jnp.where(j < heads, HEAD_DIM ** -0.5, 1.0).astype(F32)
    return y, sg, s_act, nrm, scale


def _qkv_conv_fwd(qkv_pre, conv_w, heads):
    t = qkv_pre.shape[0]
    nblk = 3 * heads

    def body(x_ref, w_ref, o_ref):
        j = pl.program_id(0)
        _, _, s_act, nrm, scale = _qkv_conv_act(x_ref[...], w_ref[...], j, heads)
        o_ref[...] = jnp.where(j < 2 * heads, s_act * (nrm * scale), s_act)

    return pl.pallas_call(
        body, name="qkv_conv_fwd", grid=(nblk,),
        in_specs=[pl.BlockSpec((t, LANES), lambda j: (0, j)), pl.BlockSpec((QKV_CONV_WIDTH, LANES), lambda j: (0, j))],
        out_specs=pl.BlockSpec((t, LANES), lambda j: (0, j)),
        out_shape=jax.ShapeDtypeStruct(qkv_pre.shape, F32),
        compiler_params=_params("arbitrary"))(qkv_pre, conv_w)


def _qkv_conv_bwd(qkv_pre, conv_w, dqkv, heads):
    t = qkv_pre.shape[0]
    nblk = 3 * heads
    k = QKV_CONV_WIDTH

    def body(x_ref, w_ref, dn_ref, dx_ref, dw_ref):
        j = pl.program_id(0)
        xv, w = x_ref[...], w_ref[...]
        y, sg, s_act, nrm, scale = _qkv_conv_act(xv, w, j, heads)
        dn = dn_ref[...]
        dsn = dn * scale
        ds_qk = nrm * dsn - s_act * (nrm * nrm * nrm) * jnp.sum(dsn * s_act, axis=-1, keepdims=True)
        ds = jnp.where(j < 2 * heads, ds_qk, dn)
        dy = ds * _dsilu(y, sg)
        dx = w[k - 1:k] * dy
        dw_ref[k - 1:k, :] = jnp.sum(dy * xv, axis=0, keepdims=True)
        for s in range(1, k):
            dx = dx + w[k - 1 - s:k - s] * _shift_up(dy, s)
            dw_ref[k - 1 - s:k - s, :] = jnp.sum(dy * _shift_down(xv, s), axis=0, keepdims=True)
        dx_ref[...] = dx

    return pl.pallas_call(
        body, name="qkv_conv_bwd", grid=(nblk,),
        in_specs=[pl.BlockSpec((t, LANES), lambda j: (0, j)), pl.BlockSpec((k, LANES), lambda j: (0, j)),
                  pl.BlockSpec((t, LANES), lambda j: (0, j))],
        out_specs=[pl.BlockSpec((t, LANES), lambda j: (0, j)), pl.BlockSpec((k, LANES), lambda j: (0, j))],
        out_shape=[jax.ShapeDtypeStruct(qkv_pre.shape, F32), jax.ShapeDtypeStruct(conv_w.shape, F32)],
        compiler_params=_params("arbitrary"))(qkv_pre, conv_w, dqkv)


def _pool_windows(shape, j, group_dim):
    lane = lax.broadcasted_iota(jnp.int32, shape, 1) + j * LANES
    grp = lane // group_dim
    win = jnp.left_shift(2, grp).astype(F32)
    cnt = jnp.minimum((_rows(shape) + 1).astype(F32), win)
    return grp, cnt


def _pool_select(grp, levels):
    out = levels[0]
    for gi in range(1, POOL_GROUPS):
        out = jnp.where(grp == gi, levels[gi], out)
    return out


def _pool_mean(hv, grp, cnt):
    acc, levels, width = hv, [], 1
    for _ in range(POOL_GROUPS):
        acc = acc + _shift_down(acc, width)
        width *= 2
        levels.append(acc)
    return _pool_select(grp, levels) / cnt - hv


def _pool_fwd(hp, wbd, scale, group_dim):
    t, dp = hp.shape

    def body(h_ref, w_ref, s_ref, o_ref):
        hv = h_ref[...]
        grp, cnt = _pool_windows(hv.shape, pl.program_id(0), group_dim)
        pooled = _pool_mean(hv, grp, cnt)
        o_ref[...] = _mm(pooled, w_ref[...]) * s_ref[...]

    return pl.pallas_call(
        body, name="pool_fwd", grid=(dp // LANES,),
        in_specs=[pl.BlockSpec((t, LANES), lambda j: (0, j)), pl.BlockSpec((LANES, LANES), lambda j: (j, j)),
                  pl.BlockSpec((1, LANES), lambda j: (0, j))],
        out_specs=pl.BlockSpec((t, LANES), lambda j: (0, j)),
        out_shape=jax.ShapeDtypeStruct(hp.shape, F32),
        compiler_params=_params("arbitrary"))(hp, wbd, scale)


def _pool_bwd(hp, wbd, scale, dob, group_dim):
    t, dp = hp.shape

    def body(h_ref, w_ref, s_ref, do_ref, dh_ref, dw_ref, ds_ref):
        hv = h_ref[...]
        grp, cnt = _pool_windows(hv.shape, pl.program_id(0), group_dim)
        pooled = _pool_mean(hv, grp, cnt)
        wv = w_ref[...]
        dov = do_ref[...]
        ds_ref[...] = jnp.sum(dov * _mm(pooled, wv), axis=0, keepdims=True)
        dys = dov * s_ref[...]
        dw_ref[0] = _mm_tn(pooled, dys)
        dpooled = _mm_nt(dys, wv)
        acc, levels, width = dpooled / cnt, [], 1
        for _ in range(POOL_GROUPS):
            acc = acc + _shift_up(acc, width)
            width *= 2
            levels.append(acc)
        dh_ref[...] = _pool_select(grp, levels) - dpooled

    nb = dp // LANES
    return pl.pallas_call(
        body, name="pool_bwd", grid=(nb,),
        in_specs=[pl.BlockSpec((t, LANES), lambda j: (0, j)), pl.BlockSpec((LANES, LANES), lambda j: (j, j)),
                  pl.BlockSpec((1, LANES), lambda j: (0, j)), pl.BlockSpec((t, LANES), lambda j: (0, j))],
        out_specs=[pl.BlockSpec((t, LANES), lambda j: (0, j)), pl.BlockSpec((1, LANES, LANES), lambda j: (j, 0, 0)),
                   pl.BlockSpec((1, LANES), lambda j: (0, j))],
        out_shape=[jax.ShapeDtypeStruct(hp.shape, F32), jax.ShapeDtypeStruct((nb, LANES, LANES), F32),
                   jax.ShapeDtypeStruct((1, dp), F32)],
        compiler_params=_params("arbitrary"))(hp, wbd, scale, dob)


def _sconv_fwd(cbcch, w):
    t, dc3 = cbcch.shape
    nb = dc3 // 3 // LANES
    k = SCONV_WIDTH

    def body(b_ref, c_ref, h_ref, w_ref, o_ref):
        m = c_ref[...] * h_ref[...]
        wv = w_ref[...]
        y = wv[k - 1:k] * m
        for s in range(1, k):
            y = y + wv[k - 1 - s:k - s] * _shift_down(m, s)
        o_ref[...] = b_ref[...] * y

    return pl.pallas_call(
        body, name="sconv_fwd", grid=(nb,),
        in_specs=[pl.BlockSpec((t, LANES), lambda j: (0, j)), pl.BlockSpec((t, LANES), lambda j: (0, nb + j)),
                  pl.BlockSpec((t, LANES), lambda j: (0, 2 * nb + j)), pl.BlockSpec((k, LANES), lambda j: (0, j))],
        out_specs=pl.BlockSpec((t, LANES), lambda j: (0, j)),
        out_shape=jax.ShapeDtypeStruct((t, dc3 // 3), F32),
        compiler_params=_params("arbitrary"))(cbcch, cbcch, cbcch, w)


def _sconv_bwd(cbcch, w, doc):
    t, dc3 = cbcch.shape
    nb = dc3 // 3 // LANES
    k = SCONV_WIDTH

    def body(b_ref, c_ref, h_ref, w_ref, do_ref, db_ref, dc_ref, dh_ref, dw_ref):
        cv, hv = c_ref[...], h_ref[...]
        m = cv * hv
        wv = w_ref[...]
        dov = do_ref[...]
        dy = dov * b_ref[...]
        y = wv[k - 1:k] * m
        dm = wv[k - 1:k] * dy
        dw_ref[k - 1:k, :] = jnp.sum(dy * m, axis=0, keepdims=True)
        for s in range(1, k):
            ms = _shift_down(m, s)
            y = y + wv[k - 1 - s:k - s] * ms
            dm = dm + wv[k - 1 - s:k - s] * _shift_up(dy, s)
            dw_ref[k - 1 - s:k - s, :] = jnp.sum(dy * ms, axis=0, keepdims=True)
        db_ref[...] = dov * y
        dc_ref[...] = dm * hv
        dh_ref[...] = dm * cv

    col = lambda o: pl.BlockSpec((t, LANES), lambda j: (0, o * nb + j))
    return pl.pallas_call(
        body, name="sconv_bwd", grid=(nb,),
        in_specs=[col(0), col(1), col(2), pl.BlockSpec((k, LANES), lambda j: (0, j)), col(0)],
        out_specs=[col(0), col(0), col(0), pl.BlockSpec((k, LANES), lambda j: (0, j))],
        out_shape=[jax.ShapeDtypeStruct((t, dc3 // 3), F32)] * 3 + [jax.ShapeDtypeStruct(w.shape, F32)],
        compiler_params=_params("arbitrary"))(cbcch, cbcch, cbcch, w, doc)


class _Split:
    def __init__(self, a, exact=False):
        self.hi = a.astype(jnp.bfloat16)
        self.lo = None if exact else (a - self.hi.astype(F32)).astype(jnp.bfloat16)


def _per_head(dims, a, b):
    a = a if isinstance(a, _Split) else _Split(a)
    b = b if isinstance(b, _Split) else _Split(b)

    def dot(x, y):
        return lax.dot_general(x, y, (dims, ((), ())), preferred_element_type=F32)

    passes = [(x, y) for x, y in ((a.hi, b.hi), (a.hi, b.lo), (a.lo, b.hi)) if x is not None and y is not None]

    depth = a.hi.shape[1 + dims[0][0]]
    lane_joined = dims[0][0] == 1 or dims[1][0] == 1

    def head(h):
        if lane_joined and depth % LANES:
            return sum(dot(x[h], y[h]) for x, y in passes)
        return dot(jnp.concatenate([x[h] for x, _ in passes], axis=dims[0][0]),
                   jnp.concatenate([y[h] for _, y in passes], axis=dims[1][0]))

    return jnp.stack([head(h) for h in range(a.hi.shape[0])])


def _bmm(a, b):
    return _per_head(((1,), (0,)), a, b)


def _bmm_nt(a, b):
    return _per_head(((1,), (1,)), a, b)


def _bmm_tn(a, b):
    return _per_head(((0,), (0,)), a, b)


def _inv_unit_lower(low):
    c = low.shape[-1]
    eye = (_rows((c, c)) == lax.broadcasted_iota(jnp.int32, (c, c), 1)).astype(F32)
    pw = -low
    inv = eye + pw
    span = 2
    while span < c:
        pws = _Split(pw)
        pw = _bmm(pws, pws)
        inv = inv + _bmm(inv, pw)
        span *= 2
    return inv


def _heads_of(ref, base, heads):
    return jnp.stack([ref[:, base + h * HEAD_DIM:base + (h + 1) * HEAD_DIM] for h in range(heads)])


def _chunk_common(q, k, v, a_col, b_col, alog, dtb, kept=None):
    hn, c, _ = q.shape
    beta = _sigmoid(b_col)
    xg = a_col + dtb
    softplus = jnp.maximum(xg, 0.0) + jnp.log(1.0 + jnp.exp(-jnp.abs(xg)))
    neg_ea = -jnp.exp(alog)
    g = neg_ea * softplus
    ri = _rows((c, c))
    ci = lax.broadcasted_iota(jnp.int32, (c, c), 1)
    incl, strict = ri >= ci, ri > ci
    inclf = _Split(jnp.broadcast_to(incl.astype(F32), (hn, c, c)), exact=True)
    gcb = _bmm(inclf, jnp.broadcast_to(g, (hn, c, HEAD_DIM)))
    gc_row = jnp.sum(jnp.where(ri <= ci, jnp.broadcast_to(g, (hn, c, c)), 0.0), axis=1, keepdims=True)
    dmat = jnp.where(incl, jnp.exp(jnp.where(incl, gcb[:, :, :1] - gc_row, 0.0)), 0.0)
    eg = jnp.exp(gcb)
    gl = gcb[:, c - 1:c, :]
    egl = jnp.exp(gl)
    edl = jnp.exp(gl - gcb)
    kb, vb = k * beta, v * beta
    kbe = kb * eg
    if kept is None:
        ks = _Split(k)
        a0 = _bmm_nt(kb, ks)
        tm = _inv_unit_lower(jnp.where(strict, a0 * dmat, 0.0))
        p0 = _bmm_nt(q, ks)
        tms = _Split(tm)
        u, w = _bmm(tms, vb), _bmm(tms, kbe)
    else:
        (a0, tm, p0, w), u = kept, None
    return dict(beta=beta, xg=xg, neg_ea=neg_ea, g=g, incl=incl, strict=strict, inclf=inclf, dmat=dmat, eg=eg,
                egl=egl, edl=edl, kb=kb, vb=vb, a0=a0, tm=tm, kbe=kbe, u=u, w=w, p0=p0,
                attn=p0 * dmat, qe=q * eg, kd=k * edl)


def _chunk_step(cm, state):
    ss = _Split(state)
    vn = cm["u"] - _bmm(cm["w"], ss)
    vns = _Split(vn)
    o = _bmm(cm["qe"], ss) + _bmm(cm["attn"], vns)
    new_state = state * cm["egl"][:, :, :1] + _bmm_tn(cm["kd"], vns)
    return vn, o, new_state


def _gated_norm(o, zv, og):
    xo, ro = _rms_fwd(o)
    sgz = _sigmoid(zv)
    return xo, ro, sgz, xo * og * (zv * sgz)


def _gate_columns(abv, gpv, heads):
    a_col = jnp.stack([abv[:, h:h + 1] for h in range(heads)])
    b_col = jnp.stack([abv[:, heads + h:heads + h + 1] for h in range(heads)])
    alog = jnp.stack([gpv[0:1, h:h + 1] for h in range(heads)])
    dtb = jnp.stack([gpv[1:2, h:h + 1] for h in range(heads)])
    return a_col, b_col, alog, dtb


def _delta_fwd(qkv, z, ab, gpar, heads):
    t = qkv.shape[0]
    da = heads * HEAD_DIM
    n = t // CHUNK

    def body(qkv_ref, z_ref, ab_ref, gp_ref, oa_ref, st_ref, kc_ref, kw_ref, s_ref):
        @pl.when(pl.program_id(0) == 0)
        def _():
            s_ref[...] = jnp.zeros_like(s_ref)

        gpv = gp_ref[...]
        cm = _chunk_common(_heads_of(qkv_ref, 0, heads), _heads_of(qkv_ref, da, heads), _heads_of(qkv_ref, 2 * da, heads),
                           *_gate_columns(ab_ref[...], gpv, heads))
        state = s_ref[...]
        st_ref[0] = state
        vn, o, new_state = _chunk_step(cm, state)
        s_ref[...] = new_state
        for slot, val in enumerate((cm["a0"], cm["tm"], cm["p0"])):
            kc_ref[0, slot] = val
        for slot, val in enumerate((cm["w"], vn, o)):
            kw_ref[0, slot] = val
        oa = _gated_norm(o, _heads_of(z_ref, 0, heads), gpv[2:3, :])[3]
        for h in range(heads):
            oa_ref[:, h * HEAD_DIM:(h + 1) * HEAD_DIM] = oa[h]

    return pl.pallas_call(
        body, name="delta_fwd", grid=(n,),
        in_specs=[pl.BlockSpec((CHUNK, 3 * da), lambda i: (i, 0)), pl.BlockSpec((CHUNK, da), lambda i: (i, 0)),
                  pl.BlockSpec((CHUNK, LANES), lambda i: (i, 0)), pl.BlockSpec((8, LANES), lambda i: (0, 0))],
        out_specs=[pl.BlockSpec((CHUNK, da), lambda i: (i, 0)),
                   pl.BlockSpec((1, heads, HEAD_DIM, HEAD_DIM), lambda i: (i, 0, 0, 0)),
                   pl.BlockSpec((1, 3, heads, CHUNK, CHUNK), lambda i: (i, 0, 0, 0, 0)),
                   pl.BlockSpec((1, 3, heads, CHUNK, HEAD_DIM), lambda i: (i, 0, 0, 0, 0))],
        out_shape=[jax.ShapeDtypeStruct((t, da), F32), jax.ShapeDtypeStruct((n, heads, HEAD_DIM, HEAD_DIM), F32),
                   jax.ShapeDtypeStruct((n, 3, heads, CHUNK, CHUNK), F32),
                   jax.ShapeDtypeStruct((n, 3, heads, CHUNK, HEAD_DIM), F32)],
        scratch_shapes=[pltpu.VMEM((heads, HEAD_DIM, HEAD_DIM), F32)],
        compiler_params=_params("arbitrary"))(qkv, z, ab, gpar)


def _delta_bwd(qkv, z, ab, gpar, states, kept_c, kept_w, doa, heads):
    t = qkv.shape[0]
    da = heads * HEAD_DIM
    n = t // CHUNK
    c = CHUNK

    def body(qkv_ref, z_ref, ab_ref, gp_ref, st_ref, kc_ref, kw_ref, doa_ref, dqkv_ref, dz_ref, dab_ref, dpar_ref, ds_ref):
        @pl.when(pl.program_id(0) == 0)
        def _():
            ds_ref[...] = jnp.zeros_like(ds_ref)
            dpar_ref[...] = jnp.zeros_like(dpar_ref)

        gpv = gp_ref[...]
        og = gpv[2:3, :]
        q, k, v = _heads_of(qkv_ref, 0, heads), _heads_of(qkv_ref, da, heads), _heads_of(qkv_ref, 2 * da, heads)
        cm = _chunk_common(q, k, v, *_gate_columns(ab_ref[...], gpv, heads),
                           kept=(kc_ref[0, 0], kc_ref[0, 1], kc_ref[0, 2], kw_ref[0, 0]))
        state = st_ref[0]
        dsp = ds_ref[...]
        vn, o = kw_ref[0, 1], kw_ref[0, 2]
        zv = _heads_of(z_ref, 0, heads)
        xo, ro, sgz, _ = _gated_norm(o, zv, og)
        doav = _heads_of(doa_ref, 0, heads)
        don = doav * (zv * sgz)
        dz = doav * (xo * og) * _dsilu(zv, sgz)
        d_og = jnp.sum(jnp.sum(don * xo, axis=1, keepdims=True), axis=0)
        do = _rms_bwd(don * og, xo, ro)
        tm, dmat, eg, edl, egl = cm["tm"], cm["dmat"], cm["eg"], cm["edl"], cm["egl"]
        dos, dsps, sts, tms, ks = _Split(do), _Split(dsp), _Split(state), _Split(tm), _Split(k)
        dvn = _bmm_tn(cm["attn"], dos) + _bmm(cm["kd"], dsps)
        dvns = _Split(dvn)
        dqe = _bmm_nt(dos, sts)
        ds_ref[...] = _bmm_tn(cm["qe"], dos) + dsp * egl[:, :, :1] - _bmm_tn(cm["w"], dvns)
        dattn = _bmm_nt(dos, vn)
        dkd = _bmm_nt(vn, dsps)
        dkd_kd = jnp.sum(dkd * cm["kd"], axis=-1, keepdims=True)
        dgl = (jnp.sum(jnp.sum(dsp * state, axis=-1, keepdims=True), axis=1, keepdims=True) * egl[:, :, :1]
               + jnp.sum(dkd_kd, axis=1, keepdims=True))
        dgc = jnp.sum(dqe * cm["qe"], axis=-1, keepdims=True) - dkd_kd
        dk = dkd * edl
        dq = dqe * eg
        dw = -_bmm_nt(dvns, sts)
        dws = _Split(dw)
        dp0 = dattn * dmat
        dd = jnp.where(cm["incl"], dattn * cm["p0"], 0.0)
        dp0s = _Split(dp0)
        dq = dq + _bmm(dp0s, ks)
        dk = dk + _bmm_tn(dp0s, q)
        dtm = _bmm_nt(dvns, cm["vb"]) + _bmm_nt(dws, cm["kbe"])
        dvb = _bmm_tn(tms, dvns)
        dkbe = _bmm_tn(tms, dws)
        dkb = dkbe * eg
        dgc = dgc + jnp.sum(dkbe * cm["kbe"], axis=-1, keepdims=True)
        dlow = jnp.where(cm["strict"], -_bmm_tn(tms, _bmm_nt(dtm, tms)), 0.0)
        dd = dd + dlow * cm["a0"]
        da0 = dlow * dmat
        da0s = _Split(da0)
        dkb = dkb + _bmm(da0s, ks)
        dk = dk + _bmm_tn(da0s, cm["kb"])
        ddd = dd * dmat
        ones = _Split(jnp.ones((heads, c, HEAD_DIM), F32), exact=True)
        dgc = dgc + jnp.sum(ddd, axis=-1, keepdims=True) - _bmm_tn(ddd, ones)[:, :, :1]
        dgc = dgc + jnp.where(_rows((c, 1)) == c - 1, dgl, 0.0)
        dg = _bmm_tn(cm["inclf"], jnp.broadcast_to(dgc, (heads, c, HEAD_DIM)))[:, :, :1]
        beta = cm["beta"]
        dk = dk + dkb * beta
        dbeta = jnp.sum(dkb * k, axis=-1, keepdims=True) + jnp.sum(dvb * v, axis=-1, keepdims=True)
        dv = dvb * beta
        db_col = dbeta * beta * (1.0 - beta)
        da_col = dg * cm["neg_ea"] * _sigmoid(cm["xg"])
        d_alog = jnp.sum(dg * cm["g"], axis=1, keepdims=True)
        d_dtb = jnp.sum(da_col, axis=1, keepdims=True)
        lane = lax.broadcasted_iota(jnp.int32, (c, LANES), 1)
        lane8 = lax.broadcasted_iota(jnp.int32, (8, LANES), 1)
        row8 = _rows((8, LANES))
        dab = jnp.zeros((c, LANES), F32)
        dpar = jnp.where(row8 == 2, d_og, 0.0)
        for h in range(heads):
            lo = h * HEAD_DIM
            dqkv_ref[:, lo:lo + HEAD_DIM] = dq[h]
            dqkv_ref[:, da + lo:da + lo + HEAD_DIM] = dk[h]
            dqkv_ref[:, 2 * da + lo:2 * da + lo + HEAD_DIM] = dv[h]
            dz_ref[:, lo:lo + HEAD_DIM] = dz[h]
            dab = dab + jnp.where(lane == h, da_col[h], 0.0) + jnp.where(lane == heads + h, db_col[h], 0.0)
            dpar = (dpar + jnp.where((row8 == 0) & (lane8 == h), d_alog[h], 0.0)
                    + jnp.where((row8 == 1) & (lane8 == h), d_dtb[h], 0.0))
        dab_ref[...] = dab
        dpar_ref[...] += dpar

    rev = lambda i: (n - 1 - i, 0)
    return pl.pallas_call(
        body, name="delta_bwd", grid=(n,),
        in_specs=[pl.BlockSpec((c, 3 * da), rev), pl.BlockSpec((c, da), rev), pl.BlockSpec((c, LANES), rev),
                  pl.BlockSpec((8, LANES), lambda i: (0, 0)),
                  pl.BlockSpec((1, heads, HEAD_DIM, HEAD_DIM), lambda i: (n - 1 - i, 0, 0, 0)),
                  pl.BlockSpec((1, 3, heads, c, c), lambda i: (n - 1 - i, 0, 0, 0, 0)),
                  pl.BlockSpec((1, 3, heads, c, HEAD_DIM), lambda i: (n - 1 - i, 0, 0, 0, 0)),
                  pl.BlockSpec((c, da), rev)],
        out_specs=[pl.BlockSpec((c, 3 * da), rev), pl.BlockSpec((c, da), rev), pl.BlockSpec((c, LANES), rev),
                   pl.BlockSpec((8, LANES), lambda i: (0, 0))],
        out_shape=[jax.ShapeDtypeStruct((t, 3 * da), F32), jax.ShapeDtypeStruct((t, da), F32),
                   jax.ShapeDtypeStruct((t, LANES), F32), jax.ShapeDtypeStruct((8, LANES), F32)],
        scratch_shapes=[pltpu.VMEM((heads, HEAD_DIM, HEAD_DIM), F32)],
        compiler_params=_params("arbitrary"))(qkv, z, ab, gpar, states, kept_c, kept_w, doa)


def _w_in_pieces(shard_cols, da, heads):
    a0, nab = 4 * da, 2 * heads
    d_in = 4 * shard_cols
    runs = [(0, a0, 0), (a0, a0 + nab, d_in - nab), (a0 + nab, d_in, a0)]
    pieces = []
    for j in range(4):
        lo, hi = j * shard_cols, (j + 1) * shard_cols
        for rlo, rhi, plo in runs:
            s, e = max(lo, rlo), min(hi, rhi)
            if s < e:
                pieces.append((j, s - lo, e - s, plo + (s - rlo)))
    return pieces, d_in - nab + LANES


def _w_in_pack(w4, da, heads):
    _, d, sc = w4.shape
    pieces, npk = _w_in_pieces(sc, da, heads)
    tr = _tile_rows(d, 256, SUBLANES_WIRE)

    def body(w_ref, o_ref):
        o_ref[:, npk - LANES:] = jnp.zeros((tr, LANES), o_ref.dtype)
        for j, lo, ln, dst in pieces:
            o_ref[:, dst:dst + ln] = w_ref[j, :, lo:lo + ln]

    return pl.pallas_call(
        body, name="w_in_pack", grid=(d // tr,),
        in_specs=[pl.BlockSpec((4, tr, sc), lambda i: (0, i, 0))],
        out_specs=pl.BlockSpec((tr, npk), lambda i: (i, 0)),
        out_shape=jax.ShapeDtypeStruct((d, npk), w4.dtype),
        compiler_params=_params("arbitrary"))(w4)


def _w_in_unpack(dwp, sc, da, heads):
    d, npk = dwp.shape
    pieces, _ = _w_in_pieces(sc, da, heads)
    tr = _tile_rows(d, 256)

    def body(g_ref, o_ref):
        for j, lo, ln, dst in pieces:
            o_ref[j, :, lo:lo + ln] = g_ref[:, dst:dst + ln]

    return pl.pallas_call(
        body, name="w_in_unpack", grid=(d // tr,),
        in_specs=[pl.BlockSpec((tr, npk), lambda i: (i, 0))],
        out_specs=pl.BlockSpec((4, tr, sc), lambda i: (0, i, 0)),
        out_shape=jax.ShapeDtypeStruct((4, d, sc), F32),
        compiler_params=_params("arbitrary"))(dwp)


def _block_diag(pool_w):
    g, gd, _ = pool_w.shape
    out = jnp.zeros((g * gd, g * gd), pool_w.dtype)
    for gi in range(g):
        out = lax.dynamic_update_slice(out, pool_w[gi], (gi * gd, gi * gd))
    return out


def _layer_dims(d):
    heads = (d // 2) // HEAD_DIM
    return heads, heads * HEAD_DIM, d // 4, d // 4


BIG = ("w_in", "w_gate", "w_up", "ple_proj", "w_out", "w_down", "ple_gate")
TRANSPOSED = ("w_gate", "w_up")


def _prepare_layer(small, li):
    d = small["norm1_g"].shape[1]
    heads, _, _, _ = _layer_dims(d)
    gpar = jnp.zeros((8, LANES), F32)
    gpar = gpar.at[0, :heads].set(small["a_log"][li]).at[1, :heads].set(small["dt_bias"][li]).at[2, :].set(small["onorm_g"][li])
    return dict(norm1_g=small["norm1_g"][li][None], conv_qkv=small["conv_qkv"][li], gpar=gpar, pool_bd=_block_diag(small["pool_w"][li]).astype(MM_DTYPE),
                pool_scale=small["pool_scale"][li][None], sconv_w=small["sconv_w"][li], norm2_g=small["norm2_g"][li][None])


def _layer_fwd(x0, p, gw, lw, tm, arrive):
    d = x0.shape[1]
    heads, da, dp, dc = _layer_dims(d)
    segs = (3 * da, da, dp, 3 * dc, LANES)
    lw["w_in_p"] = _w_in_pack(gw["w_in"], da, heads).astype(MM_DTYPE)
    qkv_pre, z, hp, cbcch, ab = _in_proj_fwd(x0, lw["norm1_g"], lw["w_in_p"], segs, tm)
    qkv = _qkv_conv_fwd(qkv_pre, lw["conv_qkv"], heads)
    oa, states, kept_c, kept_w = _delta_fwd(qkv, z, ab, lw["gpar"], heads)
    ob = _pool_fwd(hp, lw["pool_bd"], lw["pool_scale"], dp // POOL_GROUPS)
    oc = _sconv_fwd(cbcch, lw["sconv_w"])
    arrive("mixed", oa)
    x1, h2 = _out_proj_fwd(x0, (oa, ob, oc), gw["w_out"], lw["norm2_g"], tm)
    x2, gp, up = _ffn_fwd(x1, h2, gw["w_gate"], gw["w_up"], gw["w_down"], tm)
    arrive("ffn", x2)
    x3 = _ple_fwd(x2, p, gw["ple_gate"], gw["ple_proj"], tm)
    arrive("end", x3)
    saved = dict(x0=x0, qkv_pre=qkv_pre, z=z, hp=hp, cbcch=cbcch, ab=ab, qkv=qkv, states=states, kept_c=kept_c, kept_w=kept_w, oa=oa, ob=ob, oc=oc,
                 x1=x1, h2=h2, gp=gp, up=up, x2=x2)
    return x3, saved


def _layer_bwd(dx3, p, gw, lw, sv, tm, produced):
    def after_token(tok, arr):
        return arr if tok is None else arr + tok[0, 0]

    d = dx3.shape[1]
    heads, da, dp, dc = _layer_dims(d)
    segs = (3 * da, da, dp, dc, dc, dc, LANES)
    gd = dp // POOL_GROUPS
    dx2, d_ple_gate, d_ple_proj = _ple_bwd(dx3, sv["x2"], p, gw["ple_gate"], gw["ple_proj"], tm)
    dh2, d_w_gate, d_w_up, d_w_down = _ffn_bwd(dx2, sv["h2"], sv["gp"], sv["up"], gw["w_gate"], gw["w_up"], gw["w_down"],
                                               min(tm, 256))
    tok = produced("ffn", dict(w_gate=d_w_gate, w_up=d_w_up, ple_proj=d_ple_proj, w_down=d_w_down, ple_gate=d_ple_gate), dh2)
    dx1, doa, dob, doc, d_w_out, d_norm2 = _out_proj_bwd(dx2, dh2, sv["x1"], after_token(tok, lw["norm2_g"]),
                                                         (sv["oa"], sv["ob"], sv["oc"]), gw["w_out"], tm)
    dcb, dcc, dch, d_sconv = _sconv_bwd(sv["cbcch"], lw["sconv_w"], doc)
    dhp, d_pool_bd, d_pool_scale = _pool_bwd(sv["hp"], lw["pool_bd"], lw["pool_scale"], dob, gd)
    dqkv, dz, dab, dpar = _delta_bwd(sv["qkv"], sv["z"], sv["ab"], lw["gpar"], sv["states"], sv["kept_c"], sv["kept_w"], doa,
                                      heads)
    tok = produced("mixers", {}, dqkv)
    dqkv_pre, d_conv_qkv = _qkv_conv_bwd(sv["qkv_pre"], lw["conv_qkv"], dqkv, heads)
    dsegs = (dqkv_pre, dz, dhp, dcb, dcc, dch, dab)
    dx0, d_w_in_p, d_norm1 = _in_proj_bwd(sv["x0"], after_token(tok, lw["norm1_g"]), lw["w_in_p"], dsegs, dx1, segs, tm)
    per = LANES // gd
    bd = d_pool_bd.reshape(dp // LANES, per, gd, per, gd)
    d_pool_w = jnp.stack([bd[gi // per, gi % per, :, gi % per, :] for gi in range(POOL_GROUPS)])
    big = dict(w_in=_w_in_unpack(d_w_in_p, gw["w_in"].shape[2], da, heads), w_gate=d_w_gate, w_up=d_w_up,
               ple_proj=d_ple_proj, w_out=d_w_out, w_down=d_w_down, ple_gate=d_ple_gate)
    small = dict(norm1_g=d_norm1[0], conv_qkv=d_conv_qkv, a_log=dpar[0, :heads], dt_bias=dpar[1, :heads], onorm_g=dpar[2],
                 pool_w=d_pool_w, pool_scale=d_pool_scale[0], sconv_w=d_sconv, norm2_g=d_norm2[0])
    tok = produced("end", dict(w_in=big["w_in"], w_out=d_w_out), big["w_in"])
    return dx0, big, small, tok


def _local_step(x, p, target, gw, small, produced=None, arrive=None):
    t, d = x.shape
    depth = p.shape[0]
    tm = 512 if t % 512 == 0 else 128
    layers = [_prepare_layer(small, li) for li in range(depth)]
    saved = []
    h = x
    for li in range(depth):
        h, sv = _layer_fwd(h, p[li], gw[li], layers[li], tm,
                           (lambda stage, after, li=li: arrive(li, stage, after)) if arrive else (lambda stage, after: None))
        saved.append(sv)
    dx, loss, d_final = _loss_head(h, target, small["final_g"][None], tm)
    big, sm = [None] * depth, [None] * depth
    token = None
    for li in reversed(range(depth)):
        p_li = p[li] if token is None else p[li] + token[0, 0]
        dx, big[li], sm[li], token = _layer_bwd(
            dx, p_li, gw[li], layers[li], saved[li], tm,
            (lambda stage, grads, after, li=li: produced(li, stage, grads, after)) if produced else (lambda *a: None))
    small_grads = {n: jnp.stack([g[n] for g in sm]) for n in sm[0]}
    small_grads["final_g"] = d_final[0]
    return loss[0, 0], dx, big, small_grads


def _coords():
    return lax.axis_index("x"), lax.axis_index("y"), lax.axis_index("c")


def _other_chips(x, y):
    return [(1 - x, y), (x, 1 - y), (1 - x, 1 - y)]


def _place_shards(ws, me_idx):
    nt = len(ws)
    depth = ws[0].shape[0]

    def body(me_ref, *refs):
        for t, w_ref in enumerate(refs[:nt]):
            for li in range(depth):
                refs[nt + li * nt + t][...] = w_ref[li].astype(WIRE_DTYPE)

    outs = pl.pallas_call(
        body, name="place_shards",
        grid_spec=pltpu.PrefetchScalarGridSpec(
            num_scalar_prefetch=1, grid=(4,),
            in_specs=[pl.BlockSpec((depth, w.shape[1] // 4, w.shape[2]), lambda i, me_ref: (0, i, 0)) for w in ws],
            out_specs=[pl.BlockSpec((None, w.shape[1] // 4, w.shape[2]), lambda i, me_ref: (me_ref[0], i, 0))
                       for _ in range(depth) for w in ws]),
        out_shape=[jax.ShapeDtypeStruct((4,) + w.shape[1:], WIRE_DTYPE) for _ in range(depth) for w in ws],
        compiler_params=_params("arbitrary"))(me_idx, *ws)
    return [list(outs[li * nt:(li + 1) * nt]) for li in range(depth)]


def _half_block(ref, chip, pc):
    rh = ref.shape[1] // 2
    return ref.at[chip, pl.ds(pc * rh, rh)]


def _gather_copies(out_refs, send_sems, recv_sems, stage):
    nt = len(out_refs)
    x, y, c = _coords()
    pairs = []
    for j, (cx, cy) in enumerate(_other_chips(x, y)):
        for t in range(nt):
            sems = dict(send_sem=send_sems[j * nt + t], recv_sem=recv_sems[j * nt + t], device_id_type=MESH)
            if stage == 0:
                mine, theirs, to = _half_block(out_refs[t], 2 * x + y, c), _half_block(out_refs[t], 2 * cx + cy, c), (cx, cy, c)
            else:
                mine, theirs, to = (_half_block(out_refs[t], 2 * cx + cy, c), _half_block(out_refs[t], 2 * cx + cy, 1 - c),
                                    (x, y, 1 - c))
            pairs.append((pltpu.make_async_remote_copy(src_ref=mine, dst_ref=mine, device_id=to, **sems),
                          pltpu.make_async_remote_copy(src_ref=theirs, dst_ref=theirs, device_id=to, **sems)))
    return pairs


def _gather_call(name, arrs, wait_sems, after, stage):
    nt = len(arrs)
    nc = 3 * nt
    n_wait = len(wait_sems)
    n_new = 2 * nc if stage < 2 else 0
    arrs = [pltpu.with_memory_space_constraint(a, pltpu.HBM) for a in arrs]

    def body(*refs):
        a_refs = refs[:nt]
        waits = refs[nt:nt + n_wait]
        news = refs[nt + n_wait + 1:nt + n_wait + 1 + n_new]
        token = refs[-1]
        if stage > 0:
            for start, arrival in _gather_copies(a_refs, waits[:nc], waits[nc:], stage - 1):
                start.wait_send()
                arrival.wait_recv()
        if stage < 2:
            for start, _ in _gather_copies(a_refs, news[:nc], news[nc:], stage):
                start.start()
        token[...] = jnp.zeros_like(token)

    outs = pl.pallas_call(
        body, name=name,
        out_shape=(*[pltpu.SemaphoreType.DMA(())] * n_new, *[pltpu.HBM(a.shape, a.dtype) for a in arrs],
                   jax.ShapeDtypeStruct((8, LANES), F32)),
        in_specs=[HBM] * nt + [SEM] * n_wait + [ANY],
        out_specs=(*[SEM] * n_new, *[HBM] * nt, pl.BlockSpec(memory_space=pltpu.VMEM)),
        input_output_aliases={t: n_new + t for t in range(nt)},
        compiler_params=pltpu.CompilerParams(has_side_effects=pltpu.SideEffectType.DATAFLOW_SIDE_EFFECTING),
    )(*arrs, *wait_sems, after)
    return list(outs[:n_new]), list(outs[n_new:n_new + nt]), outs[-1]


def _add_my_halves(gs, others, c_idx):
    nt = len(gs)

    def body(c_ref, *refs):
        for g_ref, o_ref, out_ref in zip(refs[:nt], refs[nt:2 * nt], refs[2 * nt:]):
            out_ref[...] = (g_ref[...].astype(F32) + o_ref[...].astype(F32)).astype(out_ref.dtype)

    def half(g):
        return pl.BlockSpec((None, g.shape[1] // 2, g.shape[2]), lambda j, c_ref: (j, 0, 0))

    return pl.pallas_call(
        body, name="add_my_halves",
        grid_spec=pltpu.PrefetchScalarGridSpec(
            num_scalar_prefetch=1, grid=(4,),
            in_specs=[pl.BlockSpec((None, g.shape[1] // 2, g.shape[2]), lambda j, c_ref: (j, c_ref[0], 0)) for g in gs]
                     + [half(g) for g in gs],
            out_specs=[half(g) for g in gs]),
        out_shape=[jax.ShapeDtypeStruct((4, g.shape[1] // 2, g.shape[2]), WIRE_DTYPE) for g in gs],
        compiler_params=_params("arbitrary"))(c_idx, *gs, *others)


def _split_plan(kind, s_refs, l_refs):
    x, y, c = _coords()
    if kind == "devices":
        peers = [(x ^ ((k >> 2) & 1), y ^ ((k >> 1) & 1), c ^ (k & 1)) for k in range(1, 8)]
        return [(s, l.at[4 * x + 2 * y + c], peer) for peer in peers for s, l in zip(s_refs, l_refs)]
    if kind == "swap":
        return [(s.at[:, pl.ds((1 - c) * (s.shape[1] // 2), s.shape[1] // 2)], l, (x, y, 1 - c)) for s, l in zip(s_refs, l_refs)]
    return [(s.at[2 * cx + cy], l.at[j], (cx, cy, c)) for j, (cx, cy) in enumerate(_other_chips(x, y))
            for s, l in zip(s_refs, l_refs)]


def _split_landing(kind, a):
    if kind == "devices":
        return (8,) + a.shape
    return (a.shape[0], a.shape[1] // 2, a.shape[2]) if kind == "swap" else (3,) + a.shape[1:]


def _copies_start(name, kind, srcs, after=None):
    ns = len(srcs)
    n = {"swap": 1, "exchange": 3, "devices": 7}[kind] * ns
    srcs = [pltpu.with_memory_space_constraint(a, pltpu.HBM) for a in srcs]
    fresh = jnp.zeros if kind == "devices" else lax.empty
    lands = [pltpu.with_memory_space_constraint(fresh(_split_landing(kind, a), a.dtype), pltpu.HBM) for a in srcs]
    extra = [] if after is None else [after]

    def body(*refs):
        first_sem = 2 * ns + len(extra)
        sems, token = refs[first_sem:first_sem + 2 * n], refs[-1]
        for k, (src, dst, dev) in enumerate(_split_plan(kind, refs[:ns], refs[ns:2 * ns])):
            pltpu.make_async_remote_copy(src_ref=src, dst_ref=dst, send_sem=sems[k], recv_sem=sems[n + k], device_id=dev,
                                         device_id_type=MESH).start()
        token[...] = jnp.zeros_like(token)

    outs = pl.pallas_call(
        body, name=name,
        out_shape=(*[pltpu.SemaphoreType.DMA(())] * (2 * n), *[pltpu.HBM(a.shape, a.dtype) for a in srcs + lands],
                   jax.ShapeDtypeStruct((8, LANES), F32)),
        in_specs=[HBM] * (2 * ns) + [ANY] * len(extra),
        out_specs=(*[SEM] * (2 * n), *[HBM] * (2 * ns), pl.BlockSpec(memory_space=pltpu.VMEM)),
        input_output_aliases={t: 2 * n + t for t in range(2 * ns)},
        compiler_params=pltpu.CompilerParams(has_side_effects=pltpu.SideEffectType.DATAFLOW_SIDE_EFFECTING),
    )(*srcs, *lands, *extra)
    return list(outs[:2 * n]), list(outs[2 * n:2 * n + ns]), list(outs[2 * n + ns:2 * n + 2 * ns]), outs[-1]


def _copies_wait(name, kind, sems, srcs, lands, after):
    ns = len(srcs)
    n = len(sems) // 2

    def body(*refs):
        sem_refs = refs[2 * ns:2 * ns + 2 * n]
        for k, (src, dst, dev) in enumerate(_split_plan(kind, refs[:ns], refs[ns:2 * ns])):
            cp = pltpu.make_async_remote_copy(src_ref=src, dst_ref=dst, send_sem=sem_refs[k], recv_sem=sem_refs[n + k],
                                              device_id=dev, device_id_type=MESH)
            cp.wait_send()
            cp.wait_recv()

    outs = pl.pallas_call(
        body, name=name, out_shape=tuple(pltpu.HBM(a.shape, a.dtype) for a in srcs + lands),
        in_specs=[HBM] * (2 * ns) + [SEM] * (2 * n) + [ANY], out_specs=tuple([HBM] * (2 * ns)),
        input_output_aliases={t: t for t in range(2 * ns)},
        compiler_params=pltpu.CompilerParams(has_side_effects=pltpu.SideEffectType.DATAFLOW_SIDE_EFFECTING),
    )(*srcs, *lands, *sems, after)
    return list(outs[:ns]), list(outs[ns:])


def _sum_into(pairs, recvs, idx, li, depth, accs):
    nt = len(pairs)

    def body(idx_ref, *refs):
        for p_ref, r_ref, out_ref in zip(refs[:nt], refs[nt:2 * nt], refs[-nt:]):
            out_ref[...] = p_ref[...].astype(F32) + r_ref[0].astype(F32) + r_ref[1].astype(F32) + r_ref[2].astype(F32)

    in_specs = ([pl.BlockSpec((None, p.shape[1] // 2, p.shape[2]), lambda i, idx_ref: (idx_ref[0], i, 0)) for p in pairs]
                + [pl.BlockSpec((3, p.shape[1] // 2, p.shape[2]), lambda i, idx_ref: (0, i, 0)) for p in pairs])
    args = [idx, *pairs, *recvs]
    aliases = {}
    if accs[0] is not None:
        in_specs += [ANY] * nt
        args += list(accs)
        aliases = {1 + 2 * nt + t: t for t in range(nt)}
    return pl.pallas_call(
        body, name="sum_into",
        grid_spec=pltpu.PrefetchScalarGridSpec(
            num_scalar_prefetch=1, grid=(2,), in_specs=in_specs,
            out_specs=[pl.BlockSpec((None, p.shape[1] // 2, p.shape[2]), lambda i, idx_ref: (li, 2 * idx_ref[1] + i, 0))
                       for p in pairs]),
        out_shape=[jax.ShapeDtypeStruct((depth, 2 * p.shape[1], p.shape[2]), F32) for p in pairs],
        input_output_aliases=aliases, compiler_params=_params("arbitrary"))(*args)


def _sum_devices(own, land, me_dev):
    rows = own.shape[0]
    tr = _tile_rows(rows, 512)

    def body(me_ref, o_ref, l_ref, out_ref):
        acc = jnp.where(me_ref[0] == 0, o_ref[...], l_ref[0])
        for s in range(1, 8):
            acc = acc + jnp.where(me_ref[0] == s, o_ref[...], l_ref[s])
        out_ref[...] = acc

    return pl.pallas_call(
        body, name="sum_devices",
        grid_spec=pltpu.PrefetchScalarGridSpec(
            num_scalar_prefetch=1, grid=(rows // tr,),
            in_specs=[pl.BlockSpec((tr, LANES), lambda i, me_ref: (i, 0)), pl.BlockSpec((8, tr, LANES), lambda i, me_ref: (0, i, 0))],
            out_specs=pl.BlockSpec((tr, LANES), lambda i, me_ref: (i, 0))),
        out_shape=jax.ShapeDtypeStruct((rows, LANES), F32), compiler_params=_params("arbitrary"))(me_dev, own, land)


def _sibling_share(gs, li):
    nt = len(gs)

    def body(*refs):
        out_refs = refs[nt:2 * nt]
        send_sems, recv_sems = refs[2 * nt:]
        x, y, c = _coords()
        sends, recvs = [], []
        for t in range(nt):
            rh = out_refs[t].shape[1] // 2
            mine, theirs = out_refs[t].at[li, pl.ds(c * rh, rh)], out_refs[t].at[li, pl.ds((1 - c) * rh, rh)]
            sems = dict(send_sem=send_sems.at[t], recv_sem=recv_sems.at[t], device_id=(x, y, 1 - c), device_id_type=MESH)
            sends.append(pltpu.make_async_remote_copy(src_ref=mine, dst_ref=mine, **sems))
            recvs.append(pltpu.make_async_remote_copy(src_ref=theirs, dst_ref=theirs, **sems))
        for cp in sends:
            cp.start()
        for cp in recvs:
            cp.wait_recv()
        for cp in sends:
            cp.wait_send()

    return pl.pallas_call(
        body, name="sibling_share", out_shape=[jax.ShapeDtypeStruct(g.shape, g.dtype) for g in gs],
        in_specs=[ANY] * nt, out_specs=[ANY] * nt, input_output_aliases={t: t for t in range(nt)},
        scratch_shapes=[pltpu.SemaphoreType.DMA((nt,)), pltpu.SemaphoreType.DMA((nt,))])(*gs)


def _all_gather_devices(buf, after=None):
    extra = [] if after is None else [after]

    def body(b_ref, *rest):
        out_ref, send_sems, recv_sems, local_sem = rest[len(extra):]
        x, y, c = _coords()
        me = 4 * x + 2 * y + c
        mine = pltpu.make_async_copy(b_ref, out_ref.at[me], local_sem)
        mine.start()
        peers = []
        for k in range(1, 8):
            fx, fy, fc = (k >> 2) & 1, (k >> 1) & 1, k & 1
            peers.append((x ^ fx, y ^ fy, c ^ fc))
        sends = [pltpu.make_async_remote_copy(src_ref=b_ref, dst_ref=out_ref.at[me], send_sem=send_sems.at[k],
                                              recv_sem=recv_sems.at[k], device_id=peer, device_id_type=MESH)
                 for k, peer in enumerate(peers)]
        for cp in sends:
            cp.start()
        for k, (px, py, pc) in enumerate(peers):
            pltpu.make_async_remote_copy(src_ref=b_ref, dst_ref=out_ref.at[4 * px + 2 * py + pc], send_sem=send_sems.at[k],
                                         recv_sem=recv_sems.at[k], device_id=(px, py, pc), device_id_type=MESH).wait_recv()
        for cp in sends:
            cp.wait_send()
        mine.wait()

    return pl.pallas_call(
        body, name="all_gather_devices", out_shape=jax.ShapeDtypeStruct((8,) + buf.shape, buf.dtype),
        in_specs=[ANY] * (1 + len(extra)), out_specs=ANY,
        scratch_shapes=[pltpu.SemaphoreType.DMA((7,)), pltpu.SemaphoreType.DMA((7,)), pltpu.SemaphoreType.DMA(())])(buf, *extra)


SMALL_SHARDED = ("conv_qkv", "sconv_w")
REPLICATED = ("norm1_g", "a_log", "dt_bias", "onorm_g", "pool_w", "pool_scale", "norm2_g", "final_g")
ALL_WEIGHTS = ("norm1_g", "w_in", "conv_qkv", "a_log", "dt_bias", "onorm_g", "pool_w", "pool_scale", "sconv_w", "w_out",
               "norm2_g", "w_gate", "w_up", "w_down", "ple_proj", "ple_gate", "final_g")


def _pad_rows(flat, row_multiple):
    m = flat.shape[0]
    r = -(-m // (LANES * row_multiple)) * row_multiple
    return jnp.pad(flat, (0, r * LANES - m)).reshape(r, LANES)


def _adamw_math(w, g, m, v):
    c1 = 1.0 / (1.0 - ADAM_B1 ** ADAM_STEP)
    c2 = 1.0 / (1.0 - ADAM_B2 ** ADAM_STEP)
    nm = ADAM_B1 * m + (1.0 - ADAM_B1) * g
    nv = ADAM_B2 * v + (1.0 - ADAM_B2) * (g * g)
    return -ADAM_LR * ((nm * c1) / (jnp.sqrt(nv * c2) + ADAM_EPS) + ADAM_WD * w), nm, nv


def _adamw(w, g, m, v):
    shape = w.shape
    cols = shape[-1]
    rows = w.size // cols
    tr = _tile_rows(rows, 512)

    def body(w_ref, g_ref, m_ref, v_ref, d_ref, nm_ref, nv_ref, go_ref):
        gv = g_ref[...]
        d_ref[...], nm_ref[...], nv_ref[...] = _adamw_math(w_ref[...], gv, m_ref[...], v_ref[...])
        go_ref[...] = gv

    spec = pl.BlockSpec((tr, cols), lambda i: (i, 0))
    outs = pl.pallas_call(
        body, name="adamw", grid=(rows // tr,), in_specs=[spec] * 4, out_specs=[spec] * 4,
        out_shape=[jax.ShapeDtypeStruct((rows, cols), F32)] * 4,
        compiler_params=_params("arbitrary"))(*[a.reshape(rows, cols) for a in (w, g, m, v)])
    return tuple(o.reshape(shape) for o in outs)


def _adamw_together(ws, gs, ms, vs):
    k = len(ws)
    flat = [(w.size // w.shape[-1], w.shape[-1]) for w in ws]

    def body(*refs):
        for i in range(k):
            w_ref, g_ref, m_ref, v_ref = refs[4 * i:4 * i + 4]
            d_ref, nm_ref, nv_ref, go_ref = refs[4 * (k + i):4 * (k + i) + 4]
            gv = g_ref[...]
            d_ref[...], nm_ref[...], nv_ref[...] = _adamw_math(w_ref[...], gv, m_ref[...], v_ref[...])
            go_ref[...] = gv

    specs = [pl.BlockSpec(rc, lambda i: (0, 0)) for rc in flat for _ in range(4)]
    outs = pl.pallas_call(
        body, name="adamw_together", grid=(1,), in_specs=specs, out_specs=specs,
        out_shape=[jax.ShapeDtypeStruct(rc, F32) for rc in flat for _ in range(4)],
        compiler_params=_params("arbitrary"))(*[a.reshape(rc) for rc, four in zip(flat, zip(ws, gs, ms, vs)) for a in four])
    return [tuple(o.reshape(w.shape) for o in outs[4 * i:4 * i + 4]) for i, w in enumerate(ws)]


def kernel(x, p, norm1_g, w_in, conv_qkv, a_log, dt_bias, onorm_g, pool_w, pool_scale, sconv_w, w_out, norm2_g, w_gate, w_up, w_down, ple_proj, ple_gate, final_g, loss_target, m_norm1_g, m_w_in, m_conv_qkv, m_a_log, m_dt_bias, m_onorm_g, m_pool_w, m_pool_scale, m_sconv_w, m_w_out, m_norm2_g, m_w_gate, m_w_up, m_w_down, m_ple_proj, m_ple_gate, m_final_g, v_norm1_g, v_w_in, v_conv_qkv, v_a_log, v_dt_bias, v_onorm_g, v_pool_w, v_pool_scale, v_sconv_w, v_w_out, v_norm2_g, v_w_gate, v_w_up, v_w_down, v_ple_proj, v_ple_gate, v_final_g):
    weights = dict(zip(ALL_WEIGHTS, (norm1_g, w_in, conv_qkv, a_log, dt_bias, onorm_g, pool_w, pool_scale, sconv_w, w_out,
                                     norm2_g, w_gate, w_up, w_down, ple_proj, ple_gate, final_g)))
    mom_m = dict(zip(ALL_WEIGHTS, (m_norm1_g, m_w_in, m_conv_qkv, m_a_log, m_dt_bias, m_onorm_g, m_pool_w, m_pool_scale,
                                   m_sconv_w, m_w_out, m_norm2_g, m_w_gate, m_w_up, m_w_down, m_ple_proj, m_ple_gate, m_final_g)))
    mom_v = dict(zip(ALL_WEIGHTS, (v_norm1_g, v_w_in, v_conv_qkv, v_a_log, v_dt_bias, v_onorm_g, v_pool_w, v_pool_scale,
                                   v_sconv_w, v_w_out, v_norm2_g, v_w_gate, v_w_up, v_w_down, v_ple_proj, v_ple_gate, v_final_g)))
    for n in TRANSPOSED:
        weights[n], mom_m[n], mom_v[n] = (jnp.swapaxes(a[n], 1, 2) for a in (weights, mom_m, mom_v))
    c_idx = lax.axis_index("c").astype(jnp.int32).reshape(1)
    chip = (2 * lax.axis_index("x") + lax.axis_index("y")).astype(jnp.int32)
    me_idx = chip.reshape(1)
    idx = jnp.stack([chip, lax.axis_index("c").astype(jnp.int32)])
    depth = p.shape[0]

    small = {n: weights[n] for n in REPLICATED}
    sflat = _pad_rows(jnp.concatenate([weights[n].reshape(-1) for n in SMALL_SHARDED]), 8)
    sgath8 = _all_gather_devices(sflat)
    placed_in = _place_shards([weights["w_in"]], me_idx)
    sems, arrs, _ = _gather_call("gather_first_start", placed_in[0], [], sgath8, 0)
    placed_rest = _place_shards([weights[n] for n in BIG[1:]], me_idx)
    placed = [placed_in[li] + placed_rest[li] for li in range(depth)]
    sems, arrs, _ = _gather_call("gather_first_forward", arrs, sems, placed_rest[0][0], 1)
    _, arrs, token = _gather_call("gather_first_finish", arrs, sems, placed_rest[0][0], 2)
    gw = [dict() for _ in range(depth)]
    gw[0]["w_in"] = arrs[0]
    early = ("w_in", "w_out")
    late = tuple(n for n in BIG if n not in early)
    groups = [dict(li=0, names=BIG[1:], forward=(0, "mixed"), finish=(0, "mixed"))]
    for li in range(1, depth):
        groups.append(dict(li=li, names=early, forward=(li - 1, "ffn"), finish=(li - 1, "end")))
        groups.append(dict(li=li, names=late, forward=(li, "mixed"), finish=(li, "mixed")))
    def arrive(li, stage, after):
        for k, g in enumerate(groups):
            if g["forward"] == (li, stage):
                g["sems"], g["arrs"], _ = _gather_call("gather_forward_%d" % k, g["arrs"], g["sems"], after, 1)
            if g["finish"] == (li, stage):
                _, g["arrs"], _ = _gather_call("gather_finish_%d" % k, g["arrs"], g["sems"], after, 2)
                gw[g["li"]].update(zip(g["names"], g["arrs"]))

    sgath = sgath8[0::2].reshape(4, -1)
    off = 0
    for n in SMALL_SHARDED:
        shp = weights[n].shape
        part = sgath[:, off:off + weights[n].size].reshape((4,) + shp)
        small[n] = jnp.moveaxis(part, 0, -2).reshape(shp[:-1] + (4 * shp[-1],))
        off += weights[n].size
    for k, g in enumerate(groups):
        arrs = [placed[g["li"]][BIG.index(n)] for n in g["names"]]
        g["sems"], g["arrs"], token = _gather_call("gather_start_%d" % k, arrs, [], token, 0)

    small["norm1_g"] = small["norm1_g"] + token[0, 0]

    pending = []
    last_token = [None]

    def advance(g, after):
        if g["stage"] == 0:
            gs, others = _copies_wait("swap_wait_" + g["tag"], "swap", *g["handle"], after)
            g["handle"] = _copies_start("exchange_start_" + g["tag"], "exchange", _add_my_halves(gs, others, c_idx))
            g["stage"] = 1
            return g["handle"][3]
        return None

    held = {}

    def produced(li, stage, grads, after):
        token = None
        for g in pending:
            token = advance(g, after) if g["stage"] == 0 else token
        if li > 0 and stage != "end":
            held.update(grads)
            grads = {}
        elif li > 0:
            grads = {**held, **grads}
            held.clear()
        if grads:
            names = [n for n in BIG if n in grads]
            handle = _copies_start("swap_start_%d%s" % (li, stage), "swap", [grads[n] for n in names], token)
            pending.append(dict(li=li, names=names, tag="%d%s" % (li, stage), stage=0, handle=handle[:3]))
            token = handle[3]
        last_token[0] = last_token[0] if token is None else token
        return token

    loss_local, dx, _, small_grads = _local_step(x[0], p[:, 0], loss_target[0], gw, small, produced, arrive)
    rnames = REPLICATED + SMALL_SHARDED
    rflat = _pad_rows(jnp.concatenate([small_grads[n].reshape(-1) for n in rnames] + [loss_local.reshape(1)]), 8)
    small_handle = _copies_start("small_start", "devices", [rflat], last_token[0])
    accs, big_outs = {}, {}

    def finish(g, after):
        pairs, recvs = _copies_wait("exchange_wait_" + g["tag"], "exchange", *g["handle"][:3], after)
        summed = _sum_into(pairs, recvs, idx, g["li"], depth, [accs.get(n) for n in g["names"]])
        accs.update(zip(g["names"], _sibling_share(summed, g["li"])))
        return accs[g["names"][-1]]

    def update(names):
        for n in names:
            big_outs[n] = _adamw(weights[n], accs[n], mom_m[n], mom_v[n])
        return jnp.stack([big_outs[n][0].reshape(-1)[0] for n in names])

    done = finish(pending[0], small_handle[3])
    done = advance(pending[-1], done)
    for g in pending[1:-1]:
        done = finish(g, done)
    last = pending[-1]["names"]
    done = update([n for n in BIG if n not in last])
    finish(pending[-1], done)
    done = update(last)


    gshard = {}
    (own,), (land,) = _copies_wait("small_wait", "devices", *small_handle[:3], done)
    me_dev = (2 * chip + lax.axis_index("c").astype(jnp.int32)).reshape(1)
    rsum = _sum_devices(own, land, me_dev).reshape(-1)
    off = 0
    for n in rnames:
        whole = rsum[off:off + small_grads[n].size].reshape(small_grads[n].shape)
        off += small_grads[n].size
        if n in SMALL_SHARDED:
            cols = weights[n].shape[-1]
            whole = lax.dynamic_slice_in_dim(whole, chip * cols, cols, axis=whole.ndim - 1)
        gshard[n] = whole

    loss = rsum[off]

    deltas, new_m, new_v, grad_out = {}, {}, {}, {}
    others = [n for n in ALL_WEIGHTS if n not in BIG]
    big_outs.update(zip(others, _adamw_together(*[[a[n] for n in others] for a in (weights, gshard, mom_m, mom_v)])))
    for n in ALL_WEIGHTS:
        deltas[n], new_m[n], new_v[n], grad_out[n] = big_outs[n]
    for n in TRANSPOSED:
        deltas[n], new_m[n], new_v[n], grad_out[n] = (jnp.swapaxes(a[n], 1, 2) for a in (deltas, new_m, new_v, grad_out))
    return (loss, dx[None], *[grad_out[n] for n in ALL_WEIGHTS], *[deltas[n] for n in ALL_WEIGHTS],
            *[new_m[n] for n in ALL_WEIGHTS], *[new_v[n] for n in ALL_WEIGHTS])
```
